```python
import math
import jax, jax.numpy as jnp
from jax import lax
import numpy as np

D_MODEL = 1024
BATCH = 8
SEQ = 2048
DEPTH = 2

N_A_LAYERS = DEPTH // 2
N_B_LAYERS = DEPTH - N_A_LAYERS
N_META = 16

SSM_EXPAND = 2
D_INNER = SSM_EXPAND * D_MODEL
SSM_HEAD_DIM = 64
SSM_HEADS = D_INNER // SSM_HEAD_DIM
SSM_GROUPS = 4
SSM_HEADS_PER_GROUP = SSM_HEADS // SSM_GROUPS
D_STATE = 128
SSM_CONV = 4
CHUNK = 128
D_BC = SSM_GROUPS * D_STATE
D_XBC = D_INNER + 2 * D_BC
D_IN_PROJ = D_INNER + D_XBC + SSM_HEADS

ATTN_HEAD_DIM = 64
N_Q_HEADS = D_MODEL // ATTN_HEAD_DIM
N_KV_HEADS = 4
Q_PER_KV = N_Q_HEADS // N_KV_HEADS
D_ATTN = N_Q_HEADS * ATTN_HEAD_DIM
D_KV = N_KV_HEADS * ATTN_HEAD_DIM
WINDOW = 128
BLOCK = 128

D_FF = 2816
FFN_CONV = 3

RMS_EPS = 1e-6
NEG_INF = -1e30

kernel_name = "yoco_mamba2_swa_sink_hybrid"


def rms_norm(x, w):
    xf = x.astype(jnp.float32)
    y = xf * lax.rsqrt(jnp.mean(xf * xf, axis=-1, keepdims=True) + RMS_EPS)
    return (y * w.astype(jnp.float32)).astype(x.dtype)


def causal_dwconv(x, w, b):
    k, c = w.shape
    y = lax.conv_general_dilated(
        x, w[:, None, :].astype(x.dtype), window_strides=(1,), padding=[(k - 1, 0)],
        dimension_numbers=("NWC", "WIO", "NWC"), feature_group_count=c)
    return y + b.astype(x.dtype)


def ssd_chunked(xdt, a, b_in, c_in):
    bsz, t = xdt.shape[:2]
    nc = t // CHUNK
    xdt = xdt.reshape(bsz, nc, CHUNK, *xdt.shape[2:])
    a = a.reshape(bsz, nc, CHUNK, *a.shape[2:])
    bm = b_in.reshape(bsz, nc, CHUNK, *b_in.shape[2:])
    cm = c_in.reshape(bsz, nc, CHUNK, *c_in.shape[2:])
    a_cs = jnp.cumsum(a, axis=2)
    seg = a_cs[:, :, :, None] - a_cs[:, :, None, :]
    causal = jnp.tril(jnp.ones((CHUNK, CHUNK), bool))[None, None, :, :, None, None]
    decay_ls = jnp.where(causal, jnp.exp(jnp.where(causal, seg, 0.0)), 0.0)
    cb = jnp.einsum("bclgn,bcsgn->bclsg", cm, bm)
    y_diag = jnp.einsum("bclsg,bclsgk,bcsgkp->bclgkp", cb, decay_ls, xdt)
    decay_to_end = jnp.exp(a_cs[:, :, -1:] - a_cs)
    chunk_states = jnp.einsum("bclgn,bclgk,bclgkp->bcgkpn", bm, decay_to_end, xdt)
    chunk_decay = jnp.exp(a_cs[:, :, -1])

    def step(state, inp):
        st, dec = inp
        return state * dec[..., None, None] + st, state

    init = jnp.zeros_like(chunk_states[:, 0])
    _, prev = lax.scan(step, init, (jnp.moveaxis(chunk_states, 1, 0), jnp.moveaxis(chunk_decay, 1, 0)))
    prev = jnp.moveaxis(prev, 0, 1)
    y_off = jnp.einsum("bclgn,bcgkpn,bclgk->bclgkp", cm, prev, jnp.exp(a_cs))
    return (y_diag + y_off).reshape(bsz, t, *y_diag.shape[3:])


def mamba2_mixer(hn, w_in, conv_w, conv_b, dt_bias, a_log, d_skip, gate_norm, w_out):
    bsz, seq_len, _ = hn.shape
    zxbcdt = hn @ w_in
    z = zxbcdt[..., :D_INNER]
    xbc = zxbcdt[..., D_INNER:D_INNER + D_XBC]
    dt = zxbcdt[..., D_INNER + D_XBC:]
    xbc = jax.nn.silu(causal_dwconv(xbc, conv_w, conv_b))
    xs = xbc[..., :D_INNER]
    bm = xbc[..., D_INNER:D_INNER + D_BC]
    cm = xbc[..., D_INNER + D_BC:]
    dt = jax.nn.softplus(dt.astype(jnp.float32) + dt_bias.astype(jnp.float32))
    a = -jnp.exp(a_log.astype(jnp.float32)).reshape(SSM_GROUPS, SSM_HEADS_PER_GROUP)
    xh = xs.astype(jnp.float32).reshape(bsz, seq_len, SSM_GROUPS, SSM_HEADS_PER_GROUP, SSM_HEAD_DIM)
    dtg = dt.reshape(bsz, seq_len, SSM_GROUPS, SSM_HEADS_PER_GROUP)
    bg = bm.astype(jnp.float32).reshape(bsz, seq_len, SSM_GROUPS, D_STATE)
    cg = cm.astype(jnp.float32).reshape(bsz, seq_len, SSM_GROUPS, D_STATE)
    pad = CHUNK - N_META

    def padt(t):
        return jnp.pad(t, ((0, 0), (pad, 0)) + ((0, 0),) * (t.ndim - 2))

    y = ssd_chunked(padt(xh * dtg[..., None]), padt(dtg * a), padt(bg), padt(cg))[:, pad:]
    y = y + d_skip.astype(jnp.float32).reshape(SSM_GROUPS, SSM_HEADS_PER_GROUP)[:, :, None] * xh
    y = y.reshape(bsz, seq_len, D_INNER).astype(hn.dtype)
    y = rms_norm(y * jax.nn.silu(z), gate_norm)
    return y @ w_out


def swa_sink_attention(hn, w_q, k, v, sinks, w_o):
    bsz, seq_len, _ = hn.shape
    s_real = seq_len - N_META
    nb = s_real // BLOCK
    scale = 1.0 / math.sqrt(ATTN_HEAD_DIM)
    q = (hn @ w_q).reshape(bsz, seq_len, N_KV_HEADS, Q_PER_KV, ATTN_HEAD_DIM) * scale
    qm, qr = q[:, :N_META], q[:, N_META:]
    km, kr = k[:, :N_META], k[:, N_META:]
    vm, vr = v[:, :N_META], v[:, N_META:]
    sink = sinks.astype(jnp.float32).reshape(N_KV_HEADS, Q_PER_KV)

    sm = jnp.einsum("bqkgd,bskd->bkgqs", qm, km).astype(jnp.float32)
    sm = jnp.where(jnp.tril(jnp.ones((N_META, N_META), bool)), sm, NEG_INF)
    sm = jnp.concatenate([sm, jnp.broadcast_to(sink[None, :, :, None, None], sm.shape[:-1] + (1,))], -1)
    pm = jax.nn.softmax(sm, axis=-1)[..., :N_META].astype(v.dtype)
    om = jnp.einsum("bkgqs,bskd->bqkgd", pm, vm).reshape(bsz, N_META, D_ATTN)

    qb = qr.reshape(bsz, nb, BLOCK, N_KV_HEADS, Q_PER_KV, ATTN_HEAD_DIM)

    def band(t):
        tb = t.reshape(bsz, nb, BLOCK, N_KV_HEADS, ATTN_HEAD_DIM)
        prev = jnp.pad(tb, ((0, 0), (1, 0), (0, 0), (0, 0), (0, 0)))[:, :-1]
        return jnp.concatenate([prev, tb], axis=2)

    kband, vband = band(kr), band(vr)
    qi = jnp.arange(BLOCK)[:, None]
    si = jnp.arange(2 * BLOCK)[None, :]
    in_window = (si > qi + BLOCK - WINDOW) & (si <= qi + BLOCK)
    valid = (jnp.arange(nb)[:, None, None] > 0) | (si >= BLOCK)[None]
    mask = in_window[None] & valid
    s_meta = jnp.einsum("bnqkgd,bmkd->bnkgqm", qb, km).astype(jnp.float32)
    s_band = jnp.einsum("bnqkgd,bnskd->bnkgqs", qb, kband).astype(jnp.float32)
    s_band = jnp.where(mask[None, :, None, None], s_band, NEG_INF)
    s_sink = jnp.broadcast_to(sink[None, None, :, :, None, None], s_meta.shape[:-1] + (1,))
    p = jax.nn.softmax(jnp.concatenate([s_meta, s_band, s_sink], -1), axis=-1).astype(v.dtype)
    ob = (jnp.einsum("bnkgqm,bmkd->bnqkgd", p[..., :N_META], vm)
          + jnp.einsum("bnkgqs,bnskd->bnqkgd", p[..., N_META:N_META + 2 * BLOCK], vband))
    ob = ob.reshape(bsz, s_real, D_ATTN)
    return jnp.concatenate([om, ob], axis=1) @ w_o


def conv_ffn(hn, w_up, conv_w, conv_b, w_down):
    u = causal_dwconv(hn @ w_up, conv_w, conv_b)
    gate, val = u[..., :D_FF], u[..., D_FF:]
    return (jax.nn.silu(gate) * val) @ w_down


def _fwd_setup_inputs(seed: int = 0) -> dict:
    key = jax.random.key(seed)
    ks = jax.random.split(key, 32)
    f32 = jnp.float32

    def nrm(k, shape, scale):
        return jax.random.normal(k, shape, f32) * scale

    def gain(k, shape):
        return 1.0 + 0.1 * jax.random.normal(k, shape, f32)

    na, nbl = N_A_LAYERS, N_B_LAYERS
    dt0 = jnp.exp(jax.random.uniform(ks[5], (na, SSM_HEADS), f32) * (math.log(0.1) - math.log(0.001)) + math.log(0.001))
    return {
        "x": jax.random.normal(ks[0], (BATCH, SEQ, D_MODEL), f32),
        "meta_tokens": nrm(ks[1], (N_META, D_MODEL), 1.0),
        "a_norm_pre": gain(ks[2], (na, D_MODEL)),
        "a_w_in": nrm(ks[3], (na, D_MODEL, D_IN_PROJ), D_MODEL ** -0.5),
        "a_conv_w": nrm(ks[4], (na, SSM_CONV, D_XBC), SSM_CONV ** -0.5),
        "a_conv_b": nrm(ks[6], (na, D_XBC), 0.02),
        "a_dt_bias": dt0 + jnp.log(-jnp.expm1(-dt0)),
        "a_a_log": jnp.log(jax.random.uniform(ks[7], (na, SSM_HEADS), f32, 1.0, 16.0)),
        "a_d_skip": gain(ks[8], (na, SSM_HEADS)),
        "a_gate_norm": gain(ks[9], (na, D_INNER)),
        "a_w_out": nrm(ks[10], (na, D_INNER, D_MODEL), D_INNER ** -0.5),
        "a_norm_post": gain(ks[11], (na, D_MODEL)),
        "kv_norm": gain(ks[12], (D_MODEL,)),
        "w_kv": nrm(ks[13], (D_MODEL, 2 * D_KV), D_MODEL ** -0.5),
        "b_norm_pre": gain(ks[14], (nbl, D_MODEL)),
        "b_w_q": nrm(ks[15], (nbl, D_MODEL, D_ATTN), D_MODEL ** -0.5),
        "b_sinks": nrm(ks[16], (nbl, N_Q_HEADS), 0.5),
        "b_w_o": nrm(ks[17], (nbl, D_ATTN, D_MODEL), D_ATTN ** -0.5),
        "b_norm_post": gain(ks[18], (nbl, D_MODEL)),
        "f_norm_pre": gain(ks[19], (DEPTH, D_MODEL)),
        "f_w_up": nrm(ks[20], (DEPTH, D_MODEL, 2 * D_FF), D_MODEL ** -0.5),
        "f_conv_w": nrm(ks[21], (DEPTH, FFN_CONV, 2 * D_FF), FFN_CONV ** -0.5),
        "f_conv_b": nrm(ks[22], (DEPTH, 2 * D_FF), 0.02),
        "f_w_down": nrm(ks[23], (DEPTH, D_FF, D_MODEL), D_FF ** -0.5),
        "f_norm_post": gain(ks[24], (DEPTH, D_MODEL)),
    }


def _fwd_reference(x, meta_tokens, a_norm_pre, a_w_in, a_conv_w, a_conv_b, a_dt_bias, a_a_log, a_d_skip,
              a_gate_norm, a_w_out, a_norm_post, kv_norm, w_kv, b_norm_pre, b_w_q, b_sinks, b_w_o,
              b_norm_post, f_norm_pre, f_w_up, f_conv_w, f_conv_b, f_w_down, f_norm_post):
    bsz = x.shape[0]
    h = jnp.concatenate([jnp.broadcast_to(meta_tokens[None].astype(x.dtype), (bsz, N_META, D_MODEL)), x], axis=1)
    seq_len = h.shape[1]
    k_shared = None
    v_shared = None
    for i in range(DEPTH):
        if i < N_A_LAYERS:
            j = i
            mix = mamba2_mixer(rms_norm(h, a_norm_pre[j]), a_w_in[j], a_conv_w[j], a_conv_b[j], a_dt_bias[j],
                               a_a_log[j], a_d_skip[j], a_gate_norm[j], a_w_out[j])
            h = h + rms_norm(mix, a_norm_post[j])
        else:
            j = i - N_A_LAYERS
            if j == 0:
                kv = (rms_norm(h, kv_norm) @ w_kv).reshape(bsz, seq_len, 2, N_KV_HEADS, ATTN_HEAD_DIM)
                k_shared, v_shared = kv[:, :, 0], kv[:, :, 1]
            mix = swa_sink_attention(rms_norm(h, b_norm_pre[j]), b_w_q[j], k_shared, v_shared, b_sinks[j], b_w_o[j])
            h = h + rms_norm(mix, b_norm_post[j])
        ffn = conv_ffn(rms_norm(h, f_norm_pre[i]), f_w_up[i], f_conv_w[i], f_conv_b[i], f_w_down[i])
        h = h + rms_norm(ffn, f_norm_post[i])
    return h[:, N_META:]


import jax as _jax
import jax.numpy as _jnp

TWIN_FORMAT = 'train_step'
FWD_PARAMS = ['x', 'meta_tokens', 'a_norm_pre', 'a_w_in', 'a_conv_w', 'a_conv_b', 'a_dt_bias', 'a_a_log', 'a_d_skip', 'a_gate_norm', 'a_w_out', 'a_norm_post', 'kv_norm', 'w_kv', 'b_norm_pre', 'b_w_q', 'b_sinks', 'b_w_o', 'b_norm_post', 'f_norm_pre', 'f_w_up', 'f_conv_w', 'f_conv_b', 'f_w_down', 'f_norm_post']
TWIN_WEIGHTS = ['meta_tokens', 'a_norm_pre', 'a_w_in', 'a_conv_w', 'a_conv_b', 'a_dt_bias', 'a_a_log', 'a_d_skip', 'a_gate_norm', 'a_w_out', 'a_norm_post', 'kv_norm', 'w_kv', 'b_norm_pre', 'b_w_q', 'b_sinks', 'b_w_o', 'b_norm_post', 'f_norm_pre', 'f_w_up', 'f_conv_w', 'f_conv_b', 'f_w_down', 'f_norm_post']
TWIN_DIFF_INPUT = 'x'
TWIN_INPUTS = ['x', 'meta_tokens', 'a_norm_pre', 'a_w_in', 'a_conv_w', 'a_conv_b', 'a_dt_bias', 'a_a_log', 'a_d_skip', 'a_gate_norm', 'a_w_out', 'a_norm_post', 'kv_norm', 'w_kv', 'b_norm_pre', 'b_w_q', 'b_sinks', 'b_w_o', 'b_norm_post', 'f_norm_pre', 'f_w_up', 'f_conv_w', 'f_conv_b', 'f_w_down', 'f_norm_post', 'loss_target', 'm_meta_tokens', 'm_a_norm_pre', 'm_a_w_in', 'm_a_conv_w', 'm_a_conv_b', 'm_a_dt_bias', 'm_a_a_log', 'm_a_d_skip', 'm_a_gate_norm', 'm_a_w_out', 'm_a_norm_post', 'm_kv_norm', 'm_w_kv', 'm_b_norm_pre', 'm_b_w_q', 'm_b_sinks', 'm_b_w_o', 'm_b_norm_post', 'm_f_norm_pre', 'm_f_w_up', 'm_f_conv_w', 'm_f_conv_b', 'm_f_w_down', 'm_f_norm_post', 'v_meta_tokens', 'v_a_norm_pre', 'v_a_w_in', 'v_a_conv_w', 'v_a_conv_b', 'v_a_dt_bias', 'v_a_a_log', 'v_a_d_skip', 'v_a_gate_norm', 'v_a_w_out', 'v_a_norm_post', 'v_kv_norm', 'v_w_kv', 'v_b_norm_pre', 'v_b_w_q', 'v_b_sinks', 'v_b_w_o', 'v_b_norm_post', 'v_f_norm_pre', 'v_f_w_up', 'v_f_conv_w', 'v_f_conv_b', 'v_f_w_down', 'v_f_norm_post']
TWIN_OUTPUTS = ['loss', 'grad_x', 'grad_meta_tokens', 'grad_a_norm_pre', 'grad_a_w_in', 'grad_a_conv_w', 'grad_a_conv_b', 'grad_a_dt_bias', 'grad_a_a_log', 'grad_a_d_skip', 'grad_a_gate_norm', 'grad_a_w_out', 'grad_a_norm_post', 'grad_kv_norm', 'grad_w_kv', 'grad_b_norm_pre', 'grad_b_w_q', 'grad_b_sinks', 'grad_b_w_o', 'grad_b_norm_post', 'grad_f_norm_pre', 'grad_f_w_up', 'grad_f_conv_w', 'grad_f_conv_b', 'grad_f_w_down', 'grad_f_norm_post', 'delta_meta_tokens', 'delta_a_norm_pre', 'delta_a_w_in', 'delta_a_conv_w', 'delta_a_conv_b', 'delta_a_dt_bias', 'delta_a_a_log', 'delta_a_d_skip', 'delta_a_gate_norm', 'delta_a_w_out', 'delta_a_norm_post', 'delta_kv_norm', 'delta_w_kv', 'delta_b_norm_pre', 'delta_b_w_q', 'delta_b_sinks', 'delta_b_w_o', 'delta_b_norm_post', 'delta_f_norm_pre', 'delta_f_w_up', 'delta_f_conv_w', 'delta_f_conv_b', 'delta_f_w_down', 'delta_f_norm_post', 'new_m_meta_tokens', 'new_m_a_norm_pre', 'new_m_a_w_in', 'new_m_a_conv_w', 'new_m_a_conv_b', 'new_m_a_dt_bias', 'new_m_a_a_log', 'new_m_a_d_skip', 'new_m_a_gate_norm', 'new_m_a_w_out', 'new_m_a_norm_post', 'new_m_kv_norm', 'new_m_w_kv', 'new_m_b_norm_pre', 'new_m_b_w_q', 'new_m_b_sinks', 'new_m_b_w_o', 'new_m_b_norm_post', 'new_m_f_norm_pre', 'new_m_f_w_up', 'new_m_f_conv_w', 'new_m_f_conv_b', 'new_m_f_w_down', 'new_m_f_norm_post', 'new_v_meta_tokens', 'new_v_a_norm_pre', 'new_v_a_w_in', 'new_v_a_conv_w', 'new_v_a_conv_b', 'new_v_a_dt_bias', 'new_v_a_a_log', 'new_v_a_d_skip', 'new_v_a_gate_norm', 'new_v_a_w_out', 'new_v_a_norm_post', 'new_v_kv_norm', 'new_v_w_kv', 'new_v_b_norm_pre', 'new_v_b_w_q', 'new_v_b_sinks', 'new_v_b_w_o', 'new_v_b_norm_post', 'new_v_f_norm_pre', 'new_v_f_w_up', 'new_v_f_conv_w', 'new_v_f_conv_b', 'new_v_f_w_down', 'new_v_f_norm_post']
TWIN_LEAF_KINDS = {'loss': 'loss', 'grad_x': 'grad_x', 'grad_meta_tokens': 'grad_w', 'grad_a_norm_pre': 'grad_w', 'grad_a_w_in': 'grad_w', 'grad_a_conv_w': 'grad_w', 'grad_a_conv_b': 'grad_w', 'grad_a_dt_bias': 'grad_w', 'grad_a_a_log': 'grad_w', 'grad_a_d_skip': 'grad_w', 'grad_a_gate_norm': 'grad_w', 'grad_a_w_out': 'grad_w', 'grad_a_norm_post': 'grad_w', 'grad_kv_norm': 'grad_w', 'grad_w_kv': 'grad_w', 'grad_b_norm_pre': 'grad_w', 'grad_b_w_q': 'grad_w', 'grad_b_sinks': 'grad_w', 'grad_b_w_o': 'grad_w', 'grad_b_norm_post': 'grad_w', 'grad_f_norm_pre': 'grad_w', 'grad_f_w_up': 'grad_w', 'grad_f_conv_w': 'grad_w', 'grad_f_conv_b': 'grad_w', 'grad_f_w_down': 'grad_w', 'grad_f_norm_post': 'grad_w', 'delta_meta_tokens': 'delta_w', 'delta_a_norm_pre': 'delta_w', 'delta_a_w_in': 'delta_w', 'delta_a_conv_w': 'delta_w', 'delta_a_conv_b': 'delta_w', 'delta_a_dt_bias': 'delta_w', 'delta_a_a_log': 'delta_w', 'delta_a_d_skip': 'delta_w', 'delta_a_gate_norm': 'delta_w', 'delta_a_w_out': 'delta_w', 'delta_a_norm_post': 'delta_w', 'delta_kv_norm': 'delta_w', 'delta_w_kv': 'delta_w', 'delta_b_norm_pre': 'delta_w', 'delta_b_w_q': 'delta_w', 'delta_b_sinks': 'delta_w', 'delta_b_w_o': 'delta_w', 'delta_b_norm_post': 'delta_w', 'delta_f_norm_pre': 'delta_w', 'delta_f_w_up': 'delta_w', 'delta_f_conv_w': 'delta_w', 'delta_f_conv_b': 'delta_w', 'delta_f_w_down': 'delta_w', 'delta_f_norm_post': 'delta_w', 'new_m_meta_tokens': 'new_m', 'new_m_a_norm_pre': 'new_m', 'new_m_a_w_in': 'new_m', 'new_m_a_conv_w': 'new_m', 'new_m_a_conv_b': 'new_m', 'new_m_a_dt_bias': 'new_m', 'new_m_a_a_log': 'new_m', 'new_m_a_d_skip': 'new_m', 'new_m_a_gate_norm': 'new_m', 'new_m_a_w_out': 'new_m', 'new_m_a_norm_post': 'new_m', 'new_m_kv_norm': 'new_m', 'new_m_w_kv': 'new_m', 'new_m_b_norm_pre': 'new_m', 'new_m_b_w_q': 'new_m', 'new_m_b_sinks': 'new_m', 'new_m_b_w_o': 'new_m', 'new_m_b_norm_post': 'new_m', 'new_m_f_norm_pre': 'new_m', 'new_m_f_w_up': 'new_m', 'new_m_f_conv_w': 'new_m', 'new_m_f_conv_b': 'new_m', 'new_m_f_w_down': 'new_m', 'new_m_f_norm_post': 'new_m', 'new_v_meta_tokens': 'new_v', 'new_v_a_norm_pre': 'new_v', 'new_v_a_w_in': 'new_v', 'new_v_a_conv_w': 'new_v', 'new_v_a_conv_b': 'new_v', 'new_v_a_dt_bias': 'new_v', 'new_v_a_a_log': 'new_v', 'new_v_a_d_skip': 'new_v', 'new_v_a_gate_norm': 'new_v', 'new_v_a_w_out': 'new_v', 'new_v_a_norm_post': 'new_v', 'new_v_kv_norm': 'new_v', 'new_v_w_kv': 'new_v', 'new_v_b_norm_pre': 'new_v', 'new_v_b_w_q': 'new_v', 'new_v_b_sinks': 'new_v', 'new_v_b_w_o': 'new_v', 'new_v_b_norm_post': 'new_v', 'new_v_f_norm_pre': 'new_v', 'new_v_f_w_up': 'new_v', 'new_v_f_conv_w': 'new_v', 'new_v_f_conv_b': 'new_v', 'new_v_f_w_down': 'new_v', 'new_v_f_norm_post': 'new_v'}


def _forward(args):
    return _fwd_reference(*[args[k] for k in FWD_PARAMS])


def _output_shape():
    out = _jax.eval_shape(lambda: _forward(_fwd_setup_inputs(0)))
    return out.shape, out.dtype

N_MICROBATCH = 1
ADAM_LR = 0.001
ADAM_B1 = 0.9
ADAM_B2 = 0.999
ADAM_EPS = 1e-08
ADAM_WD = 0.01
ADAM_STEP = 10
PER_EXAMPLE_BATCH_AXIS = {'x': 0, 'loss_target': 0}
SHARED_INPUTS = []
_WEIGHT_DTYPES = {'meta_tokens': _jnp.float32, 'a_norm_pre': _jnp.float32, 'a_w_in': _jnp.float32, 'a_conv_w': _jnp.float32, 'a_conv_b': _jnp.float32, 'a_dt_bias': _jnp.float32, 'a_a_log': _jnp.float32, 'a_d_skip': _jnp.float32, 'a_gate_norm': _jnp.float32, 'a_w_out': _jnp.float32, 'a_norm_post': _jnp.float32, 'kv_norm': _jnp.float32, 'w_kv': _jnp.float32, 'b_norm_pre': _jnp.float32, 'b_w_q': _jnp.float32, 'b_sinks': _jnp.float32, 'b_w_o': _jnp.float32, 'b_norm_post': _jnp.float32, 'f_norm_pre': _jnp.float32, 'f_w_up': _jnp.float32, 'f_conv_w': _jnp.float32, 'f_conv_b': _jnp.float32, 'f_w_down': _jnp.float32, 'f_norm_post': _jnp.float32}
MOMENT_SCALE = {'meta_tokens': 3.162190e-01, 'a_norm_pre': 1.577058e+00, 'a_w_in': 6.585749e-01, 'a_conv_w': 1.757830e+00, 'a_conv_b': 5.672012e+00, 'a_dt_bias': 2.602869e+00, 'a_a_log': 1.185802e+01, 'a_d_skip': 1.256957e+01, 'a_gate_norm': 3.203659e+00, 'a_w_out': 4.476441e+00, 'a_norm_post': 1.668435e+01, 'kv_norm': 4.260168e+00, 'w_kv': 5.425676e+00, 'b_norm_pre': 3.452882e-01, 'b_w_q': 3.521054e-01, 'b_sinks': 1.046079e-01, 'b_w_o': 3.652384e+00, 'b_norm_post': 1.847857e+01, 'f_norm_pre': 1.815588e+00, 'f_w_up': 7.848276e-01, 'f_conv_w': 8.742690e-01, 'f_conv_b': 3.317816e+00, 'f_w_down': 1.552676e+00, 'f_norm_post': 1.626364e+01}


def _to_microbatches(a, axis):
    t = _jnp.moveaxis(a, axis, 0)
    t = t.reshape((N_MICROBATCH, t.shape[0] // N_MICROBATCH) + t.shape[1:])
    return _jnp.moveaxis(t, 1, axis + 1)


def setup_inputs(seed: int = 0) -> dict:
    inp = _fwd_setup_inputs(seed)
    key = _jax.random.fold_in(_jax.random.key(seed), 7919)
    shape, _ = _output_shape()
    out = dict(inp)
    out["loss_target"] = _jax.random.normal(_jax.random.fold_in(key, 0), shape, _jnp.float32)
    for i, name in enumerate(TWIN_WEIGHTS):
        w = inp[name].astype(_jnp.float32)
        if MOMENT_SCALE is None:
            s = _jnp.sqrt(_jnp.mean(_jnp.square(w)) + 1e-30)
        else:
            s = MOMENT_SCALE[name]
        km, kv = _jax.random.split(_jax.random.fold_in(key, i + 1))
        out[name] = w
        out["m_" + name] = s * _jax.random.normal(km, w.shape, _jnp.float32)
        out["v_" + name] = (s * s) * _jax.random.uniform(kv, w.shape, _jnp.float32, 0.5, 1.5)
    if N_MICROBATCH > 1:
        for name, axis in PER_EXAMPLE_BATCH_AXIS.items():
            out[name] = _to_microbatches(out[name], axis)
    return {'x': out['x'], 'meta_tokens': out['meta_tokens'], 'a_norm_pre': out['a_norm_pre'], 'a_w_in': out['a_w_in'], 'a_conv_w': out['a_conv_w'], 'a_conv_b': out['a_conv_b'], 'a_dt_bias': out['a_dt_bias'], 'a_a_log': out['a_a_log'], 'a_d_skip': out['a_d_skip'], 'a_gate_norm': out['a_gate_norm'], 'a_w_out': out['a_w_out'], 'a_norm_post': out['a_norm_post'], 'kv_norm': out['kv_norm'], 'w_kv': out['w_kv'], 'b_norm_pre': out['b_norm_pre'], 'b_w_q': out['b_w_q'], 'b_sinks': out['b_sinks'], 'b_w_o': out['b_w_o'], 'b_norm_post': out['b_norm_post'], 'f_norm_pre': out['f_norm_pre'], 'f_w_up': out['f_w_up'], 'f_conv_w': out['f_conv_w'], 'f_conv_b': out['f_conv_b'], 'f_w_down': out['f_w_down'], 'f_norm_post': out['f_norm_post'], 'loss_target': out['loss_target'], 'm_meta_tokens': out['m_meta_tokens'], 'm_a_norm_pre': out['m_a_norm_pre'], 'm_a_w_in': out['m_a_w_in'], 'm_a_conv_w': out['m_a_conv_w'], 'm_a_conv_b': out['m_a_conv_b'], 'm_a_dt_bias': out['m_a_dt_bias'], 'm_a_a_log': out['m_a_a_log'], 'm_a_d_skip': out['m_a_d_skip'], 'm_a_gate_norm': out['m_a_gate_norm'], 'm_a_w_out': out['m_a_w_out'], 'm_a_norm_post': out['m_a_norm_post'], 'm_kv_norm': out['m_kv_norm'], 'm_w_kv': out['m_w_kv'], 'm_b_norm_pre': out['m_b_norm_pre'], 'm_b_w_q': out['m_b_w_q'], 'm_b_sinks': out['m_b_sinks'], 'm_b_w_o': out['m_b_w_o'], 'm_b_norm_post': out['m_b_norm_post'], 'm_f_norm_pre': out['m_f_norm_pre'], 'm_f_w_up': out['m_f_w_up'], 'm_f_conv_w': out['m_f_conv_w'], 'm_f_conv_b': out['m_f_conv_b'], 'm_f_w_down': out['m_f_w_down'], 'm_f_norm_post': out['m_f_norm_post'], 'v_meta_tokens': out['v_meta_tokens'], 'v_a_norm_pre': out['v_a_norm_pre'], 'v_a_w_in': out['v_a_w_in'], 'v_a_conv_w': out['v_a_conv_w'], 'v_a_conv_b': out['v_a_conv_b'], 'v_a_dt_bias': out['v_a_dt_bias'], 'v_a_a_log': out['v_a_a_log'], 'v_a_d_skip': out['v_a_d_skip'], 'v_a_gate_norm': out['v_a_gate_norm'], 'v_a_w_out': out['v_a_w_out'], 'v_a_norm_post': out['v_a_norm_post'], 'v_kv_norm': out['v_kv_norm'], 'v_w_kv': out['v_w_kv'], 'v_b_norm_pre': out['v_b_norm_pre'], 'v_b_w_q': out['v_b_w_q'], 'v_b_sinks': out['v_b_sinks'], 'v_b_w_o': out['v_b_w_o'], 'v_b_norm_post': out['v_b_norm_post'], 'v_f_norm_pre': out['v_f_norm_pre'], 'v_f_w_up': out['v_f_w_up'], 'v_f_conv_w': out['v_f_conv_w'], 'v_f_conv_b': out['v_f_conv_b'], 'v_f_w_down': out['v_f_w_down'], 'v_f_norm_post': out['v_f_norm_post']}


def _loss(weights, diff, rest, loss_target):
    with _jax.named_scope("forward"):
        args = {**rest, TWIN_DIFF_INPUT: diff, **{k: w.astype(_WEIGHT_DTYPES[k]) for k, w in weights.items()}}
        y = _forward(args)
    with _jax.named_scope("loss_head"):
        err = _jnp.square(y.astype(_jnp.float32) - loss_target)
        return 0.5 * _jnp.sum(_jnp.mean(err, axis=-1)) if err.ndim else 0.5 * err


def _adamw(w, g, m, v):
    m = ADAM_B1 * m + (1.0 - ADAM_B1) * g
    v = ADAM_B2 * v + (1.0 - ADAM_B2) * _jnp.square(g)
    m_hat = m / (1.0 - ADAM_B1 ** ADAM_STEP)
    v_hat = v / (1.0 - ADAM_B2 ** ADAM_STEP)
    delta = -ADAM_LR * (m_hat / (_jnp.sqrt(v_hat) + ADAM_EPS) + ADAM_WD * w)
    return delta, m, v


def reference(x, meta_tokens, a_norm_pre, a_w_in, a_conv_w, a_conv_b, a_dt_bias, a_a_log, a_d_skip, a_gate_norm, a_w_out, a_norm_post, kv_norm, w_kv, b_norm_pre, b_w_q, b_sinks, b_w_o, b_norm_post, f_norm_pre, f_w_up, f_conv_w, f_conv_b, f_w_down, f_norm_post, loss_target, m_meta_tokens, m_a_norm_pre, m_a_w_in, m_a_conv_w, m_a_conv_b, m_a_dt_bias, m_a_a_log, m_a_d_skip, m_a_gate_norm, m_a_w_out, m_a_norm_post, m_kv_norm, m_w_kv, m_b_norm_pre, m_b_w_q, m_b_sinks, m_b_w_o, m_b_norm_post, m_f_norm_pre, m_f_w_up, m_f_conv_w, m_f_conv_b, m_f_w_down, m_f_norm_post, v_meta_tokens, v_a_norm_pre, v_a_w_in, v_a_conv_w, v_a_conv_b, v_a_dt_bias, v_a_a_log, v_a_d_skip, v_a_gate_norm, v_a_w_out, v_a_norm_post, v_kv_norm, v_w_kv, v_b_norm_pre, v_b_w_q, v_b_sinks, v_b_w_o, v_b_norm_post, v_f_norm_pre, v_f_w_up, v_f_conv_w, v_f_conv_b, v_f_w_down, v_f_norm_post):
    given = dict(x=x, meta_tokens=meta_tokens, a_norm_pre=a_norm_pre, a_w_in=a_w_in, a_conv_w=a_conv_w, a_conv_b=a_conv_b, a_dt_bias=a_dt_bias, a_a_log=a_a_log, a_d_skip=a_d_skip, a_gate_norm=a_gate_norm, a_w_out=a_w_out, a_norm_post=a_norm_post, kv_norm=kv_norm, w_kv=w_kv, b_norm_pre=b_norm_pre, b_w_q=b_w_q, b_sinks=b_sinks, b_w_o=b_w_o, b_norm_post=b_norm_post, f_norm_pre=f_norm_pre, f_w_up=f_w_up, f_conv_w=f_conv_w, f_conv_b=f_conv_b, f_w_down=f_w_down, f_norm_post=f_norm_post, loss_target=loss_target, m_meta_tokens=m_meta_tokens, m_a_norm_pre=m_a_norm_pre, m_a_w_in=m_a_w_in, m_a_conv_w=m_a_conv_w, m_a_conv_b=m_a_conv_b, m_a_dt_bias=m_a_dt_bias, m_a_a_log=m_a_a_log, m_a_d_skip=m_a_d_skip, m_a_gate_norm=m_a_gate_norm, m_a_w_out=m_a_w_out, m_a_norm_post=m_a_norm_post, m_kv_norm=m_kv_norm, m_w_kv=m_w_kv, m_b_norm_pre=m_b_norm_pre, m_b_w_q=m_b_w_q, m_b_sinks=m_b_sinks, m_b_w_o=m_b_w_o, m_b_norm_post=m_b_norm_post, m_f_norm_pre=m_f_norm_pre, m_f_w_up=m_f_w_up, m_f_conv_w=m_f_conv_w, m_f_conv_b=m_f_conv_b, m_f_w_down=m_f_w_down, m_f_norm_post=m_f_norm_post, v_meta_tokens=v_meta_tokens, v_a_norm_pre=v_a_norm_pre, v_a_w_in=v_a_w_in, v_a_conv_w=v_a_conv_w, v_a_conv_b=v_a_conv_b, v_a_dt_bias=v_a_dt_bias, v_a_a_log=v_a_a_log, v_a_d_skip=v_a_d_skip, v_a_gate_norm=v_a_gate_norm, v_a_w_out=v_a_w_out, v_a_norm_post=v_a_norm_post, v_kv_norm=v_kv_norm, v_w_kv=v_w_kv, v_b_norm_pre=v_b_norm_pre, v_b_w_q=v_b_w_q, v_b_sinks=v_b_sinks, v_b_w_o=v_b_w_o, v_b_norm_post=v_b_norm_post, v_f_norm_pre=v_f_norm_pre, v_f_w_up=v_f_w_up, v_f_conv_w=v_f_conv_w, v_f_conv_b=v_f_conv_b, v_f_w_down=v_f_w_down, v_f_norm_post=v_f_norm_post)
    weights = {n: given[n] for n in TWIN_WEIGHTS}
    shared = {n: given[n] for n in SHARED_INPUTS}
    per_example = {n: given[n] for n in ['x']}
    grad_fn = _jax.value_and_grad(_loss, argnums=(0, 1))

    def one_microbatch(ex, loss_target):
        ex = dict(ex)
        diff = ex.pop(TWIN_DIFF_INPUT)
        return grad_fn(weights, diff, {**shared, **ex}, loss_target)

    if N_MICROBATCH == 1:
        loss, (grad_w, grad_x) = one_microbatch(per_example, given["loss_target"])
    else:
        def body(carry, xs):
            loss_sum, grad_sum = carry
            l_k, (gw_k, gx_k) = one_microbatch(xs[0], xs[1])
            with _jax.named_scope("update"):
                return (loss_sum + l_k, _jax.tree.map(_jnp.add, grad_sum, gw_k)), gx_k

        init = (_jnp.zeros((), _jnp.float32), _jax.tree.map(_jnp.zeros_like, weights))
        (loss, grad_w), grad_x = _jax.lax.scan(body, init, (per_example, given["loss_target"]))
    with _jax.named_scope("update"):
        delta_w, new_m, new_v = {}, {}, {}
        for n in TWIN_WEIGHTS:
            delta_w[n], new_m[n], new_v[n] = _adamw(weights[n], grad_w[n], given["m_" + n], given["v_" + n])
    return (loss, grad_x, *[grad_w[n] for n in TWIN_WEIGHTS], *[delta_w[n] for n in TWIN_WEIGHTS],
            *[new_m[n] for n in TWIN_WEIGHTS], *[new_v[n] for n in TWIN_WEIGHTS])
```

```python
import functools
import math

import jax
import jax.numpy as jnp
from jax import lax
from jax.experimental import pallas as pl
from jax.experimental.pallas import tpu as pltpu

F32, BF16 = jnp.float32, jnp.bfloat16
MESH = pl.DeviceIdType.MESH
HIGHEST = lax.Precision.HIGHEST

D_MODEL = 1024
N_META = 16
CHUNK = 128
PAD_ROWS = CHUNK - N_META
D_INNER = 2048
D_STATE = 128
N_GROUPS = 4
HEADS_PER_GROUP = 8
SSM_HEADS = 32
HEAD_DIM = 64
D_BC = N_GROUPS * D_STATE
D_XBC = D_INNER + 2 * D_BC
D_MAIN = D_INNER + D_XBC
D_IN_PROJ = D_MAIN + SSM_HEADS
GROUP_W = HEADS_PER_GROUP * HEAD_DIM
SSM_CONV = 4
D_FF = 2816
FFN_CONV = 3
N_Q_HEADS = 16
N_KV_HEADS = 4
D_KV = 256
ATTN_SCALE = 1.0 / math.sqrt(HEAD_DIM)
RMS_EPS = 1e-6
NEG_INF = -1e30
LANES = 128
VMEM_LIMIT = 48 * 1024 * 1024

ADAM_LR, ADAM_B1, ADAM_B2, ADAM_EPS, ADAM_WD, ADAM_STEP = 0.001, 0.9, 0.999, 1e-08, 0.01, 10

N_CHIPS = 4
N_DEV = 8
FLAT_W = 1024
FLAT_ROWS = 6912
HALF_ROWS = FLAT_ROWS // 2


def _cparams(sem=None):
    return pltpu.CompilerParams(dimension_semantics=sem, vmem_limit_bytes=VMEM_LIMIT)


def _tile(n, cands=(512, 256, 128)):
    for t in cands:
        if n % t == 0:
            return t
    return n


def _row_tile(rows, width):
    for t in (544, 272):
        if rows % t == 0 and t * width * 4 <= (3 << 20):
            return t
    return 128


def _rows_mask(i, tm):
    rows = i * tm + lax.broadcasted_iota(jnp.int32, (tm, 1), 0)
    return rows >= PAD_ROWS


def _dot(a, b):
    return jnp.dot(a, b, preferred_element_type=F32)


def _dot_nt(a, b):
    return lax.dot_general(a, b, (((1,), (1,)), ((), ())), preferred_element_type=F32)


def _dot_tn(a, b):
    return lax.dot_general(a, b, (((0,), (0,)), ((), ())), preferred_element_type=F32)


def _dot_hi(a, b):
    return jnp.dot(a, b, preferred_element_type=F32, precision=HIGHEST)


def _sigmoid(x):
    return 1.0 / (1.0 + jnp.exp(-x))


def _mm(name, a, b, mode, out_dtype=F32, acc=None):
    resident_bytes = 8 << 20
    if mode == "nn":
        m, k = a.shape
        n = b.shape[1]
        tm = m
        while tm * k * 2 > resident_bytes and tm % 32 == 0:
            tm //= 2
        tn = _tile(n)
        grid = (m // tm, n // tn)
        in_specs = [pl.BlockSpec((tm, k), lambda i, j: (i, 0)), pl.BlockSpec((k, tn), lambda i, j: (0, j))]
        out_shape, out_block = (m, n), (tm, tn)
    elif mode == "nt":
        m, n = a.shape
        k = b.shape[0]
        tm = m
        while tm * n * 2 > resident_bytes and tm % 32 == 0:
            tm //= 2
        tk = _tile(k)
        grid = (m // tm, k // tk)
        in_specs = [pl.BlockSpec((tm, n), lambda i, j: (i, 0)), pl.BlockSpec((tk, n), lambda i, j: (j, 0))]
        out_shape, out_block = (m, k), (tm, tk)
    else:
        m, k = a.shape
        n = b.shape[1]
        tk, tn = _tile(k), _tile(n)
        grid = (k // tk, n // tn)
        in_specs = [pl.BlockSpec((m, tk), lambda i, j: (0, i)), pl.BlockSpec((m, tn), lambda i, j: (0, j))]
        out_shape, out_block = (k, n), (tk, tn)
    out_spec = pl.BlockSpec(out_block, lambda i, j: (i, j))
    has_acc = acc is not None

    def body(*refs):
        a_ref, b_ref = refs[0], refs[1]
        o_ref = refs[-1]
        av, bv = a_ref[...], b_ref[...]
        if mode == "nn":
            r = _dot(av, bv)
        elif mode == "nt":
            r = _dot_nt(av, bv)
        else:
            r = _dot_tn(av, bv)
        if has_acc:
            r = r + refs[2][...]
        o_ref[...] = r.astype(o_ref.dtype)

    operands = [a, b]
    if has_acc:
        in_specs = in_specs + [out_spec]
        operands.append(acc)
    return pl.pallas_call(
        body, name=name, out_shape=jax.ShapeDtypeStruct(out_shape, out_dtype), grid=grid,
        in_specs=in_specs, out_specs=out_spec, compiler_params=_cparams(("parallel", "parallel")),
    )(*operands)


def _rms_fwd(name, h, w):
    rows, width = h.shape
    tm = _row_tile(rows, width)

    def body(h_ref, w_ref, o_ref):
        x = h_ref[...]
        r = lax.rsqrt(jnp.mean(x * x, axis=-1, keepdims=True) + RMS_EPS)
        o_ref[...] = (x * r * w_ref[...]).astype(BF16)

    return pl.pallas_call(
        body, name=name, out_shape=jax.ShapeDtypeStruct((rows, width), BF16), grid=(rows // tm,),
        in_specs=[pl.BlockSpec((tm, width), lambda i: (i, 0)), pl.BlockSpec((1, width), lambda i: (0, 0))],
        out_specs=pl.BlockSpec((tm, width), lambda i: (i, 0)), compiler_params=_cparams(("parallel",)),
    )(h, w)


def _resid_norm_fwd(name, h, pre, w):
    rows, width = h.shape
    tm = _row_tile(rows, width)

    def body(h_ref, p_ref, w_ref, o_ref):
        p = p_ref[...]
        r = lax.rsqrt(jnp.mean(p * p, axis=-1, keepdims=True) + RMS_EPS)
        o_ref[...] = h_ref[...] + jnp.where(_rows_mask(pl.program_id(0), tm), p * r * w_ref[...], 0.0)

    row_spec = pl.BlockSpec((tm, width), lambda i: (i, 0))
    return pl.pallas_call(
        body, name=name, out_shape=jax.ShapeDtypeStruct((rows, width), F32), grid=(rows // tm,),
        in_specs=[row_spec, row_spec, pl.BlockSpec((1, width), lambda i: (0, 0))],
        out_specs=row_spec, compiler_params=_cparams(("parallel",)),
    )(h, pre, w)


def _resid_norm_bwd(name, dh, pre, w):
    rows, width = dh.shape
    tm = _row_tile(rows, width)

    def body(dh_ref, p_ref, w_ref, dp_ref, dw_ref):
        i = pl.program_id(0)
        dy = jnp.where(_rows_mask(i, tm), dh_ref[...], 0.0)
        p = p_ref[...]
        r = lax.rsqrt(jnp.mean(p * p, axis=-1, keepdims=True) + RMS_EPS)
        xhat = p * r
        dxhat = dy * w_ref[...]
        dp = r * (dxhat - xhat * jnp.mean(dxhat * xhat, axis=-1, keepdims=True))
        dp_ref[...] = dp.astype(BF16)

        @pl.when(i == 0)
        def _():
            dw_ref[...] = jnp.zeros_like(dw_ref)

        dw_ref[...] += jnp.sum(dy * xhat, axis=0, keepdims=True)

    row_spec = pl.BlockSpec((tm, width), lambda i: (i, 0))
    vec_spec = pl.BlockSpec((1, width), lambda i: (0, 0))
    return pl.pallas_call(
        body, name=name,
        out_shape=(jax.ShapeDtypeStruct((rows, width), BF16), jax.ShapeDtypeStruct((1, width), F32)),
        grid=(rows // tm,), in_specs=[row_spec, row_spec, vec_spec], out_specs=(row_spec, vec_spec),
        compiler_params=_cparams(("arbitrary",)),
    )(dh, pre, w)


def _norm_bwd_add(name, dh, dhn, h, w):
    rows, width = dh.shape
    tm = _row_tile(rows, width)

    def body(dh_ref, dhn_ref, h_ref, w_ref, o_ref, dw_ref):
        i = pl.program_id(0)
        x = h_ref[...]
        dy = dhn_ref[...]
        r = lax.rsqrt(jnp.mean(x * x, axis=-1, keepdims=True) + RMS_EPS)
        xhat = x * r
        dxhat = dy * w_ref[...]
        dx = r * (dxhat - xhat * jnp.mean(dxhat * xhat, axis=-1, keepdims=True))
        o_ref[...] = dh_ref[...] + jnp.where(_rows_mask(i, tm), dx, 0.0)

        @pl.when(i == 0)
        def _():
            dw_ref[...] = jnp.zeros_like(dw_ref)

        dw_ref[...] += jnp.sum(dy * xhat, axis=0, keepdims=True)

    row_spec = pl.BlockSpec((tm, width), lambda i: (i, 0))
    vec_spec = pl.BlockSpec((1, width), lambda i: (0, 0))
    return pl.pallas_call(
        body, name=name,
        out_shape=(jax.ShapeDtypeStruct((rows, width), F32), jax.ShapeDtypeStruct((1, width), F32)),
        grid=(rows // tm,), in_specs=[row_spec, row_spec, row_spec, vec_spec], out_specs=(row_spec, vec_spec),
        compiler_params=_cparams(("arbitrary",)),
    )(dh, dhn, h, w)


def _shift_down(x, s, rows):
    return pltpu.roll(x, s, 0) if s else x


def _shift_up(x, s, rows):
    return pltpu.roll(x, rows - s, 0) if s else x


def _conv4_fwd(name, zx, cw, cb):
    rows = zx.shape[0]
    off = D_INNER // LANES

    def body(x_ref, w_ref, b_ref, o_ref):
        x = x_ref[...]
        acc = b_ref[...] + w_ref[pl.ds(SSM_CONV - 1, 1), :] * x
        for s in range(1, SSM_CONV):
            acc = acc + w_ref[pl.ds(SSM_CONV - 1 - s, 1), :] * _shift_down(x, s, rows)
        valid = lax.broadcasted_iota(jnp.int32, (rows, 1), 0) >= PAD_ROWS
        o_ref[...] = jnp.where(valid, acc * _sigmoid(acc), 0.0)

    return pl.pallas_call(
        body, name=name, out_shape=jax.ShapeDtypeStruct((rows, D_XBC), F32), grid=(D_XBC // LANES,),
        in_specs=[pl.BlockSpec((rows, LANES), lambda j: (0, j + off)),
                  pl.BlockSpec((SSM_CONV, LANES), lambda j: (0, j)),
                  pl.BlockSpec((1, LANES), lambda j: (0, j))],
        out_specs=pl.BlockSpec((rows, LANES), lambda j: (0, j)), compiler_params=_cparams(("parallel",)),
    )(zx, cw, cb)


def _conv4_bwd(name, zx, dout, cw, cb, col0):
    rows, width = dout.shape
    zoff = (D_INNER + col0) // LANES
    woff = col0 // LANES

    def body(x_ref, d_ref, w_ref, b_ref, dx_ref, dw_ref, db_ref):
        x = x_ref[...]
        shifted = [_shift_down(x, s, rows) for s in range(SSM_CONV)]
        acc = b_ref[...]
        for s in range(SSM_CONV):
            acc = acc + w_ref[pl.ds(SSM_CONV - 1 - s, 1), :] * shifted[s]
        sig = _sigmoid(acc)
        valid = lax.broadcasted_iota(jnp.int32, (rows, 1), 0) >= PAD_ROWS
        dpre = jnp.where(valid, d_ref[...] * sig * (1.0 + acc * (1.0 - sig)), 0.0)
        dx = w_ref[pl.ds(SSM_CONV - 1, 1), :] * dpre
        for s in range(1, SSM_CONV):
            dx = dx + w_ref[pl.ds(SSM_CONV - 1 - s, 1), :] * _shift_up(dpre, s, rows)
        dx_ref[...] = dx.astype(BF16)
        for s in range(SSM_CONV):
            dw_ref[pl.ds(SSM_CONV - 1 - s, 1), :] = jnp.sum(dpre * shifted[s], axis=0, keepdims=True)
        db_ref[...] = jnp.sum(dpre, axis=0, keepdims=True)

    return pl.pallas_call(
        body, name=name,
        out_shape=(jax.ShapeDtypeStruct((rows, width), BF16), jax.ShapeDtypeStruct((SSM_CONV, width), F32),
                   jax.ShapeDtypeStruct((1, width), F32)),
        grid=(width // LANES,),
        in_specs=[pl.BlockSpec((rows, LANES), lambda j: (0, j + zoff)),
                  pl.BlockSpec((rows, LANES), lambda j: (0, j)),
                  pl.BlockSpec((SSM_CONV, LANES), lambda j: (0, j + woff)),
                  pl.BlockSpec((1, LANES), lambda j: (0, j + woff))],
        out_specs=(pl.BlockSpec((rows, LANES), lambda j: (0, j)),
                   pl.BlockSpec((SSM_CONV, LANES), lambda j: (0, j)),
                   pl.BlockSpec((1, LANES), lambda j: (0, j))),
        compiler_params=_cparams(("parallel",)),
    )(zx, dout, cw, cb)


def _ffn_conv_fwd(name, up, cw, cb):
    rows = up.shape[0]
    nt = D_FF // LANES

    def body(g_ref, v_ref, wg_ref, wv_ref, bg_ref, bv_ref, o_ref):
        g, v = g_ref[...], v_ref[...]
        ug, uv = bg_ref[...], bv_ref[...]
        for s in range(FFN_CONV):
            ug = ug + wg_ref[pl.ds(FFN_CONV - 1 - s, 1), :] * _shift_down(g, s, rows)
            uv = uv + wv_ref[pl.ds(FFN_CONV - 1 - s, 1), :] * _shift_down(v, s, rows)
        valid = lax.broadcasted_iota(jnp.int32, (rows, 1), 0) >= PAD_ROWS
        o_ref[...] = jnp.where(valid, ug * _sigmoid(ug) * uv, 0.0).astype(BF16)

    col = lambda shift: pl.BlockSpec((rows, LANES), lambda j: (0, j + shift))
    wsp = lambda shift: pl.BlockSpec((FFN_CONV, LANES), lambda j: (0, j + shift))
    bsp = lambda shift: pl.BlockSpec((1, LANES), lambda j: (0, j + shift))
    return pl.pallas_call(
        body, name=name, out_shape=jax.ShapeDtypeStruct((rows, D_FF), BF16), grid=(nt,),
        in_specs=[col(0), col(nt), wsp(0), wsp(nt), bsp(0), bsp(nt)],
        out_specs=pl.BlockSpec((rows, LANES), lambda j: (0, j)), compiler_params=_cparams(("parallel",)),
    )(up, up, cw, cw, cb, cb)


def _ffn_conv_bwd(name, up, dact, cw, cb):
    rows = up.shape[0]
    nt = D_FF // LANES

    def body(g_ref, v_ref, d_ref, wg_ref, wv_ref, bg_ref, bv_ref, dx_ref, dw_ref, db_ref):
        half = pl.program_id(0)
        g, v = g_ref[...], v_ref[...]
        gs = [_shift_down(g, s, rows) for s in range(FFN_CONV)]
        vs = [_shift_down(v, s, rows) for s in range(FFN_CONV)]
        ug, uv = bg_ref[...], bv_ref[...]
        for s in range(FFN_CONV):
            ug = ug + wg_ref[pl.ds(FFN_CONV - 1 - s, 1), :] * gs[s]
            uv = uv + wv_ref[pl.ds(FFN_CONV - 1 - s, 1), :] * vs[s]
        sig = _sigmoid(ug)
        valid = lax.broadcasted_iota(jnp.int32, (rows, 1), 0) >= PAD_ROWS
        d = jnp.where(valid, d_ref[...], 0.0)
        dug = d * uv * sig * (1.0 + ug * (1.0 - sig))
        duv = d * ug * sig
        is_gate = (half + jnp.zeros((rows, 1), jnp.int32)) == 0
        dpre = jnp.where(is_gate, dug, duv)
        is_gate_row = (half + jnp.zeros((1, LANES), jnp.int32)) == 0
        dx = jnp.zeros_like(dpre)
        for s in range(FFN_CONV):
            w_s = jnp.where(is_gate_row, wg_ref[pl.ds(FFN_CONV - 1 - s, 1), :], wv_ref[pl.ds(FFN_CONV - 1 - s, 1), :])
            dx = dx + w_s * _shift_up(dpre, s, rows)
            src = jnp.where(is_gate, gs[s], vs[s])
            dw_ref[pl.ds(FFN_CONV - 1 - s, 1), :] = jnp.sum(dpre * src, axis=0, keepdims=True)
        dx_ref[...] = dx.astype(BF16)
        db_ref[...] = jnp.sum(dpre, axis=0, keepdims=True)

    col = lambda shift: pl.BlockSpec((rows, LANES), lambda r, j: (0, j + shift))
    wsp = lambda shift: pl.BlockSpec((FFN_CONV, LANES), lambda r, j: (0, j + shift))
    bsp = lambda shift: pl.BlockSpec((1, LANES), lambda r, j: (0, j + shift))
    return pl.pallas_call(
        body, name=name,
        out_shape=(jax.ShapeDtypeStruct((rows, 2 * D_FF), BF16), jax.ShapeDtypeStruct((FFN_CONV, 2 * D_FF), F32),
                   jax.ShapeDtypeStruct((1, 2 * D_FF), F32)),
        grid=(2, nt),
        in_specs=[col(0), col(nt), col(0), wsp(0), wsp(nt), bsp(0), bsp(nt)],
        out_specs=(pl.BlockSpec((rows, LANES), lambda r, j: (0, r * nt + j)),
                   pl.BlockSpec((FFN_CONV, LANES), lambda r, j: (0, r * nt + j)),
                   pl.BlockSpec((1, LANES), lambda r, j: (0, r * nt + j))),
        compiler_params=_cparams(("parallel", "parallel")),
    )(up, up, dact, cw, cw, cb, cb)


def _dt_fwd(name, dtr, bias):
    rows = dtr.shape[0]
    tm = _row_tile(rows, LANES)

    def body(d_ref, b_ref, o_ref):
        v = d_ref[...] + b_ref[...]
        sp = jnp.maximum(v, 0.0) + jnp.log1p(jnp.exp(-jnp.abs(v)))
        lane = lax.broadcasted_iota(jnp.int32, (tm, LANES), 1)
        ok = _rows_mask(pl.program_id(0), tm) & (lane < SSM_HEADS)
        o_ref[...] = jnp.where(ok, sp, 0.0)

    return pl.pallas_call(
        body, name=name, out_shape=jax.ShapeDtypeStruct((rows, LANES), F32), grid=(rows // tm,),
        in_specs=[pl.BlockSpec((tm, LANES), lambda i: (i, 0)), pl.BlockSpec((1, LANES), lambda i: (0, 0))],
        out_specs=pl.BlockSpec((tm, LANES), lambda i: (i, 0)), compiler_params=_cparams(("parallel",)),
    )(dtr, bias)


def _dt_bwd(name, ddt, dtr, bias):
    rows = dtr.shape[0]
    tm = _row_tile(rows, LANES)

    def body(g_ref, d_ref, b_ref, o_ref, db_ref):
        i = pl.program_id(0)
        lane = lax.broadcasted_iota(jnp.int32, (tm, LANES), 1)
        ok = _rows_mask(i, tm) & (lane < SSM_HEADS)
        dv = jnp.where(ok, g_ref[...] * _sigmoid(d_ref[...] + b_ref[...]), 0.0)
        o_ref[...] = dv.astype(BF16)

        @pl.when(i == 0)
        def _():
            db_ref[...] = jnp.zeros_like(db_ref)

        db_ref[...] += jnp.sum(dv, axis=0, keepdims=True)

    row_spec = pl.BlockSpec((tm, LANES), lambda i: (i, 0))
    vec_spec = pl.BlockSpec((1, LANES), lambda i: (0, 0))
    return pl.pallas_call(
        body, name=name,
        out_shape=(jax.ShapeDtypeStruct((rows, LANES), BF16), jax.ShapeDtypeStruct((1, LANES), F32)),
        grid=(rows // tm,), in_specs=[row_spec, row_spec, vec_spec], out_specs=(row_spec, vec_spec),
        compiler_params=_cparams(("arbitrary",)),
    )(ddt, dtr, bias)


def _gate_fwd(name, y, zx, w):
    rows = y.shape[0]
    tm = _row_tile(rows, D_INNER)

    def body(y_ref, z_ref, w_ref, o_ref):
        z = z_ref[...]
        g = y_ref[...] * (z * _sigmoid(z))
        r = lax.rsqrt(jnp.mean(g * g, axis=-1, keepdims=True) + RMS_EPS)
        o_ref[...] = (g * r * w_ref[...]).astype(BF16)

    row_spec = pl.BlockSpec((tm, D_INNER), lambda i: (i, 0))
    return pl.pallas_call(
        body, name=name, out_shape=jax.ShapeDtypeStruct((rows, D_INNER), BF16), grid=(rows // tm,),
        in_specs=[row_spec, row_spec, pl.BlockSpec((1, D_INNER), lambda i: (0, 0))],
        out_specs=row_spec, compiler_params=_cparams(("parallel",)),
    )(y, zx, w)


def _gate_bwd(name, dyn, y, zx, w):
    rows = y.shape[0]
    tm = _row_tile(rows, D_INNER)

    def body(d_ref, y_ref, z_ref, w_ref, dy_ref, dz_ref, dw_ref):
        i = pl.program_id(0)
        z, yv = z_ref[...], y_ref[...]
        sig = _sigmoid(z)
        sz = z * sig
        g = yv * sz
        r = lax.rsqrt(jnp.mean(g * g, axis=-1, keepdims=True) + RMS_EPS)
        ghat = g * r
        dn = d_ref[...]
        dghat = dn * w_ref[...]
        dg = r * (dghat - ghat * jnp.mean(dghat * ghat, axis=-1, keepdims=True))
        dy_ref[...] = dg * sz
        dz_ref[...] = (dg * yv * sig * (1.0 + z * (1.0 - sig))).astype(BF16)

        @pl.when(i == 0)
        def _():
            dw_ref[...] = jnp.zeros_like(dw_ref)

        dw_ref[...] += jnp.sum(dn * ghat, axis=0, keepdims=True)

    row_spec = pl.BlockSpec((tm, D_INNER), lambda i: (i, 0))
    vec_spec = pl.BlockSpec((1, D_INNER), lambda i: (0, 0))
    return pl.pallas_call(
        body, name=name,
        out_shape=(jax.ShapeDtypeStruct((rows, D_INNER), F32), jax.ShapeDtypeStruct((rows, D_INNER), BF16),
                   jax.ShapeDtypeStruct((1, D_INNER), F32)),
        grid=(rows // tm,), in_specs=[row_spec, row_spec, row_spec, vec_spec],
        out_specs=(row_spec, row_spec, vec_spec), compiler_params=_cparams(("arbitrary",)),
    )(dyn, y, zx, w)


def _ssd_consts():
    r = lax.broadcasted_iota(jnp.int32, (CHUNK, CHUNK), 0)
    c = lax.broadcasted_iota(jnp.int32, (CHUNK, CHUNK), 1)
    k512 = lax.broadcasted_iota(jnp.int32, (LANES, GROUP_W), 0)
    j512 = lax.broadcasted_iota(jnp.int32, (LANES, GROUP_W), 1)
    expand = (jnp.right_shift(j512, 6) == k512).astype(F32)
    k1024 = lax.broadcasted_iota(jnp.int32, (LANES, HEADS_PER_GROUP * LANES), 0)
    j1024 = lax.broadcasted_iota(jnp.int32, (LANES, HEADS_PER_GROUP * LANES), 1)
    bcast = (jnp.right_shift(j1024, 7) == k1024).astype(F32)
    return r >= c, r <= c, expand, bcast


def _reduce_heads_matrix():
    j = lax.broadcasted_iota(jnp.int32, (GROUP_W, LANES), 0)
    k = lax.broadcasted_iota(jnp.int32, (GROUP_W, LANES), 1)
    return (jnp.right_shift(j, 6) == k).astype(F32)


def _ssd_common(x_ref, b_ref, c_ref, dt_ref, a128_ref, aexp_ref):
    causal, causal_t, expand, bcast = _ssd_consts()
    x = x_ref[...]
    dt = dt_ref[...]
    acs_col = _dot_hi(causal.astype(F32), dt) * a128_ref[...]
    dt_exp = _dot_hi(dt, expand)
    acs_exp = _dot_hi(acs_col, expand)
    acs_bc = _dot_hi(acs_col, bcast)
    tot_exp = jnp.sum(dt_exp * aexp_ref[...], axis=0, keepdims=True)
    xdt = x * dt_exp
    e_exp = jnp.exp(acs_exp)
    f_exp = jnp.exp(tot_exp - acs_exp)
    bm, cm = b_ref[...], c_ref[...]
    return (causal, causal_t), expand, x, dt, acs_col, dt_exp, acs_bc, tot_exp, xdt, e_exp, f_exp, bm, cm


def _head_decay(acs_bc, h, causal):
    a_l = acs_bc[:, h * LANES:(h + 1) * LANES]
    seg = a_l - a_l.T
    dm = jnp.where(causal[0], jnp.exp(jnp.minimum(seg, 0.0)), 0.0)
    dmt = jnp.where(causal[1], jnp.exp(jnp.minimum(-seg, 0.0)), 0.0)
    return dm, dmt


def _ssd_fwd(name, xbc, dt4, a128, aexp, dskexp):
    rows = xbc.shape[0]
    nc = rows // CHUNK
    bcol = D_INNER // LANES

    def body(x_ref, b_ref, c_ref, dt_ref, a128_ref, aexp_ref, dsk_ref, y_ref, st_ref, s_scr):
        @pl.when(pl.program_id(1) == 0)
        def _():
            s_scr[...] = jnp.zeros_like(s_scr)

        causal, _, x, _, _, _, acs_bc, tot_exp, xdt, e_exp, f_exp, bm, cm = _ssd_common(
            x_ref, b_ref, c_ref, dt_ref, a128_ref, aexp_ref)
        state = s_scr[...]
        st_ref[...] = state
        cb16, bb16 = cm.astype(BF16), bm.astype(BF16)
        cb = _dot_nt(cb16, bb16)
        base = e_exp * _dot(cb16, state.astype(BF16)) + dsk_ref[...] * x
        lane = lax.broadcasted_iota(jnp.int32, (CHUNK, LANES), 1)
        for p in range(HEADS_PER_GROUP // 2):
            sl = slice(p * LANES, (p + 1) * LANES)
            xp = xdt[:, sl].astype(BF16)
            yd = []
            for e in range(2):
                dm, _ = _head_decay(acs_bc, 2 * p + e, causal)
                yd.append(_dot((cb * dm).astype(BF16), xp))
            y_ref[:, sl] = jnp.where(lane < HEAD_DIM, yd[0], yd[1]) + base[:, sl]
        s_scr[...] = jnp.exp(tot_exp) * state + _dot(bm.T.astype(BF16), (f_exp * xdt).astype(BF16))

    vec = lambda w: pl.BlockSpec((None, 1, w), lambda g, c: (g, 0, 0))
    return pl.pallas_call(
        body, name=name,
        out_shape=(jax.ShapeDtypeStruct((rows, D_INNER), F32),
                   jax.ShapeDtypeStruct((N_GROUPS, nc, D_STATE, GROUP_W), F32)),
        grid=(N_GROUPS, nc),
        in_specs=[pl.BlockSpec((CHUNK, GROUP_W), lambda g, c: (c, g)),
                  pl.BlockSpec((CHUNK, LANES), lambda g, c: (c, bcol + g)),
                  pl.BlockSpec((CHUNK, LANES), lambda g, c: (c, bcol + N_GROUPS + g)),
                  pl.BlockSpec((None, CHUNK, LANES), lambda g, c: (g, c, 0)),
                  vec(LANES), vec(GROUP_W), vec(GROUP_W)],
        out_specs=(pl.BlockSpec((CHUNK, GROUP_W), lambda g, c: (c, g)),
                   pl.BlockSpec((None, None, D_STATE, GROUP_W), lambda g, c: (g, c, 0, 0))),
        scratch_shapes=[pltpu.VMEM((D_STATE, GROUP_W), F32)],
        compiler_params=_cparams(("parallel", "arbitrary")),
    )(xbc, xbc, xbc, dt4, a128, aexp, dskexp)


def _ssd_bwd(name, xbc, dt4, a128, aexp, dskexp, dy, states):
    rows = xbc.shape[0]
    nc = rows // CHUNK
    bcol = D_INNER // LANES
    last = nc - 1

    def body(x_ref, b_ref, c_ref, dt_ref, a128_ref, aexp_ref, dsk_ref, dy_ref, st_ref,
             dx_ref, db_ref, dc_ref, ddt_ref, dalog_ref, ddsk_ref, ds_scr):
        first = pl.program_id(1) == 0

        @pl.when(first)
        def _():
            ds_scr[...] = jnp.zeros_like(ds_scr)
            dalog_ref[...] = jnp.zeros_like(dalog_ref)
            ddsk_ref[...] = jnp.zeros_like(ddsk_ref)

        causal, _, x, dt, _, dt_exp, acs_bc, tot_exp, xdt, e_exp, f_exp, bm, cm = _ssd_common(
            x_ref, b_ref, c_ref, dt_ref, a128_ref, aexp_ref)
        reduce_heads = _reduce_heads_matrix()
        state, dstate = st_ref[...], ds_scr[...]
        dyv = dy_ref[...]
        cb16, bb16 = cm.astype(BF16), bm.astype(BF16)
        s16, ds16 = state.astype(BF16), dstate.astype(BF16)
        cb = _dot_nt(cb16, bb16)
        cbt = _dot_nt(bb16, cb16)
        cs = _dot(cb16, s16)
        bds = _dot(bb16, ds16)
        edy = e_exp * dyv
        fx = f_exp * xdt
        dxdt_base = f_exp * bds
        dc_acc = _dot_nt(edy.astype(BF16), s16)
        db_acc = _dot_nt(fx.astype(BF16), ds16)
        ds_scr[...] = jnp.exp(tot_exp) * dstate + _dot(cm.T.astype(BF16), edy.astype(BF16))
        q = fx * bds
        qh = _dot_hi(q, reduce_heads)
        dacs = _dot_hi(edy * cs, reduce_heads) - qh
        dtot = jnp.sum(qh + _dot_hi(jnp.exp(tot_exp) * dstate * state, reduce_heads), axis=0, keepdims=True)
        ddsk_ref[...] += jnp.sum(_dot_hi(dyv * x, reduce_heads), axis=0, keepdims=True)
        lane = lax.broadcasted_iota(jnp.int32, (CHUNK, LANES), 1)
        dcb = jnp.zeros((CHUNK, CHUNK), F32)
        dcbt = jnp.zeros((CHUNK, CHUNK), F32)
        ddt_x = jnp.zeros((CHUNK, LANES), F32)
        for p in range(HEADS_PER_GROUP // 2):
            sl = slice(p * LANES, (p + 1) * LANES)
            xp, dyp = xdt[:, sl], dyv[:, sl]
            xp16, dyp16 = xp.astype(BF16), dyp.astype(BF16)
            dxh = []
            for e in range(2):
                h = 2 * p + e
                mine = (lane < HEAD_DIM) if e == 0 else (lane >= HEAD_DIM)
                dm, dmt = _head_decay(acs_bc, h, causal)
                m, mt = cb * dm, cbt * dmt
                xh16 = jnp.where(mine, xp, 0.0).astype(BF16)
                dyh16 = jnp.where(mine, dyp, 0.0).astype(BF16)
                d_m = _dot_nt(dyh16, xp16)
                d_mt = _dot_nt(xh16, dyp16)
                dacs_h = (jnp.sum(d_m * m, axis=-1, keepdims=True)
                          - jnp.sum(d_mt * mt, axis=-1, keepdims=True))
                dacs = dacs + jnp.where(lane == h, dacs_h, 0.0)
                dcb = dcb + d_m * dm
                dcbt = dcbt + d_mt * dmt
                dxh.append(_dot(mt.astype(BF16), dyp16))
            dxdt = jnp.where(lane < HEAD_DIM, dxh[0], dxh[1]) + dxdt_base[:, sl]
            dx_ref[:, sl] = dxdt * dt_exp[:, sl] + dsk_ref[:, sl] * dyp
            prod = dxdt * x[:, sl]
            for e in range(2):
                mine = (lane < HEAD_DIM) if e == 0 else (lane >= HEAD_DIM)
                col = jnp.sum(jnp.where(mine, prod, 0.0), axis=-1, keepdims=True)
                ddt_x = ddt_x + jnp.where(lane == 2 * p + e, col, 0.0)
        dc_ref[...] = dc_acc + _dot(dcb.astype(BF16), bb16)
        db_ref[...] = db_acc + _dot(dcbt.astype(BF16), cb16)
        row = lax.broadcasted_iota(jnp.int32, (CHUNK, LANES), 0)
        dacs = dacs + jnp.where(row == CHUNK - 1, dtot, 0.0)
        da = _dot_hi(causal[1].astype(F32), dacs)
        ddt_ref[...] = da * a128_ref[...] + ddt_x
        dalog_ref[...] += jnp.sum(da * dt, axis=0, keepdims=True) * a128_ref[...]

    vec = lambda w: pl.BlockSpec((None, 1, w), lambda g, c: (g, 0, 0))
    return pl.pallas_call(
        body, name=name,
        out_shape=(jax.ShapeDtypeStruct((rows, D_INNER), F32), jax.ShapeDtypeStruct((rows, D_BC), F32),
                   jax.ShapeDtypeStruct((rows, D_BC), F32), jax.ShapeDtypeStruct((N_GROUPS, rows, LANES), F32),
                   jax.ShapeDtypeStruct((N_GROUPS, 1, LANES), F32), jax.ShapeDtypeStruct((N_GROUPS, 1, LANES), F32)),
        grid=(N_GROUPS, nc),
        in_specs=[pl.BlockSpec((CHUNK, GROUP_W), lambda g, c: (last - c, g)),
                  pl.BlockSpec((CHUNK, LANES), lambda g, c: (last - c, bcol + g)),
                  pl.BlockSpec((CHUNK, LANES), lambda g, c: (last - c, bcol + N_GROUPS + g)),
                  pl.BlockSpec((None, CHUNK, LANES), lambda g, c: (g, last - c, 0)),
                  vec(LANES), vec(GROUP_W), vec(GROUP_W),
                  pl.BlockSpec((CHUNK, GROUP_W), lambda g, c: (last - c, g)),
                  pl.BlockSpec((None, None, D_STATE, GROUP_W), lambda g, c: (g, last - c, 0, 0))],
        out_specs=(pl.BlockSpec((CHUNK, GROUP_W), lambda g, c: (last - c, g)),
                   pl.BlockSpec((CHUNK, LANES), lambda g, c: (last - c, g)),
                   pl.BlockSpec((CHUNK, LANES), lambda g, c: (last - c, g)),
                   pl.BlockSpec((None, CHUNK, LANES), lambda g, c: (g, last - c, 0)),
                   vec(LANES), vec(LANES)),
        scratch_shapes=[pltpu.VMEM((D_STATE, GROUP_W), F32)],
        compiler_params=_cparams(("parallel", "arbitrary")),
    )(xbc, xbc, xbc, dt4, a128, aexp, dskexp, dy, states)


def _attn_visible(b):
    row = lax.broadcasted_iota(jnp.int32, (CHUNK, 3 * CHUNK), 0)
    col = lax.broadcasted_iota(jnp.int32, (CHUNK, 3 * CHUNK), 1)
    bb = b + jnp.zeros_like(col)
    meta = (col < CHUNK) & (bb >= 1) & (col >= PAD_ROWS)
    prev = (col >= CHUNK) & (col < 2 * CHUNK) & (bb >= 2) & ((col - CHUNK) > row)
    cur = (col >= 2 * CHUNK) & ((col - 2 * CHUNK) <= row) & ((bb >= 1) | ((col - 2 * CHUNK) >= PAD_ROWS))
    return meta | prev | cur


def _attn_probs(qm16, kc16, visible, sink):
    s = jnp.where(visible, _dot_nt(qm16, kc16), NEG_INF)
    m = jnp.maximum(jnp.max(s, axis=-1, keepdims=True), sink)
    pe = jnp.exp(s - m)
    pe_sink = jnp.exp(sink - m)
    inv = 1.0 / (jnp.sum(pe, axis=-1, keepdims=True) + pe_sink)
    return pe * inv, pe_sink * inv


def _attn_specs():
    blk = lambda f: pl.BlockSpec((CHUNK, 2 * D_KV), f)
    kv3 = [blk(lambda b: (0, 0)), blk(lambda b: (jnp.maximum(b - 1, 0), 0)), blk(lambda b: (b, 0))]
    return kv3


def _attn_fwd(name, q, k2, v2, sinks):
    rows = q.shape[0]

    def body(q_ref, k0, kp, kc, v0, vp, vc, sink_ref, o_ref):
        visible = _attn_visible(pl.program_id(0))
        lane = lax.broadcasted_iota(jnp.int32, (CHUNK, LANES), 1)
        for kvh in range(N_KV_HEADS):
            ksl = slice(kvh * LANES, (kvh + 1) * LANES)
            kcat = jnp.concatenate([k0[:, ksl], kp[:, ksl], kc[:, ksl]], axis=0).astype(BF16)
            vcat = jnp.concatenate([v0[:, ksl], vp[:, ksl], vc[:, ksl]], axis=0).astype(BF16)
            for pp in range(2):
                pair = kvh * 2 + pp
                qsl = slice(pair * LANES, (pair + 1) * LANES)
                qp = q_ref[:, qsl] * ATTN_SCALE
                outs = []
                for e in range(2):
                    mine = (lane < HEAD_DIM) if e == 0 else (lane >= HEAD_DIM)
                    qm16 = jnp.where(mine, qp, 0.0).astype(BF16)
                    pn, _ = _attn_probs(qm16, kcat, visible, sink_ref[2 * pair + e])
                    outs.append(_dot(pn.astype(BF16), vcat))
                o_ref[:, qsl] = jnp.where(lane < HEAD_DIM, outs[0], outs[1]).astype(BF16)

    return pl.pallas_call(
        body, name=name, out_shape=jax.ShapeDtypeStruct((rows, D_MODEL), BF16), grid=(rows // CHUNK,),
        in_specs=[pl.BlockSpec((CHUNK, D_MODEL), lambda b: (b, 0))] + _attn_specs() + _attn_specs()
        + [pl.BlockSpec(memory_space=pltpu.SMEM)],
        out_specs=pl.BlockSpec((CHUNK, D_MODEL), lambda b: (b, 0)), compiler_params=_cparams(("parallel",)),
    )(q, k2, k2, k2, v2, v2, v2, sinks)


def _attn_bwd(name, q, k2, v2, sinks, do):
    rows = q.shape[0]

    def body(q_ref, k0, kp, kc, v0, vp, vc, sink_ref, do_ref,
             dq_ref, dkc_ref, dkp_ref, dvc_ref, dvp_ref, dkm_ref, dvm_ref, dsink_ref):
        @pl.when(pl.program_id(0) == 0)
        def _():
            dkm_ref[...] = jnp.zeros_like(dkm_ref)
            dvm_ref[...] = jnp.zeros_like(dvm_ref)
            dsink_ref[...] = jnp.zeros_like(dsink_ref)

        visible = _attn_visible(pl.program_id(0))
        lane = lax.broadcasted_iota(jnp.int32, (CHUNK, LANES), 1)
        lane1 = lax.broadcasted_iota(jnp.int32, (1, LANES), 1)
        dsink = jnp.zeros((1, LANES), F32)
        for kvh in range(N_KV_HEADS):
            ksl = slice(kvh * LANES, (kvh + 1) * LANES)
            kcat = jnp.concatenate([k0[:, ksl], kp[:, ksl], kc[:, ksl]], axis=0).astype(BF16)
            vcat = jnp.concatenate([v0[:, ksl], vp[:, ksl], vc[:, ksl]], axis=0).astype(BF16)
            dk_acc = jnp.zeros((3 * CHUNK, LANES), F32)
            dv_acc = jnp.zeros((3 * CHUNK, LANES), F32)
            for pp in range(2):
                pair = kvh * 2 + pp
                qsl = slice(pair * LANES, (pair + 1) * LANES)
                qp = q_ref[:, qsl] * ATTN_SCALE
                dop = do_ref[:, qsl]
                dqs = []
                for e in range(2):
                    head = 2 * pair + e
                    mine = (lane < HEAD_DIM) if e == 0 else (lane >= HEAD_DIM)
                    qm16 = jnp.where(mine, qp, 0.0).astype(BF16)
                    dom16 = jnp.where(mine, dop, 0.0).astype(BF16)
                    pn, psink = _attn_probs(qm16, kcat, visible, sink_ref[head])
                    dp = _dot_nt(dom16, vcat)
                    delta = jnp.sum(pn * dp, axis=-1, keepdims=True)
                    ds = pn * (dp - delta)
                    dsink = dsink + jnp.where(lane1 == head, -jnp.sum(psink * delta, axis=0, keepdims=True), 0.0)
                    dqs.append(_dot(ds.astype(BF16), kcat))
                    dk_acc = dk_acc + _dot(ds.T.astype(BF16), qm16)
                    dv_acc = dv_acc + _dot(pn.T.astype(BF16), dom16)
                dq_ref[:, qsl] = (jnp.where(lane < HEAD_DIM, dqs[0], dqs[1]) * ATTN_SCALE).astype(BF16)
            dkm_ref[:, ksl] += dk_acc[0:CHUNK]
            dvm_ref[:, ksl] += dv_acc[0:CHUNK]
            dkp_ref[:, ksl] = dk_acc[CHUNK:2 * CHUNK]
            dvp_ref[:, ksl] = dv_acc[CHUNK:2 * CHUNK]
            dkc_ref[:, ksl] = dk_acc[2 * CHUNK:3 * CHUNK]
            dvc_ref[:, ksl] = dv_acc[2 * CHUNK:3 * CHUNK]
        dsink_ref[...] += dsink

    qspec = pl.BlockSpec((CHUNK, D_MODEL), lambda b: (b, 0))
    kvspec = pl.BlockSpec((CHUNK, 2 * D_KV), lambda b: (b, 0))
    fixed = pl.BlockSpec((CHUNK, 2 * D_KV), lambda b: (0, 0))
    kv_shape = jax.ShapeDtypeStruct((rows, 2 * D_KV), F32)
    meta_shape = jax.ShapeDtypeStruct((CHUNK, 2 * D_KV), F32)
    return pl.pallas_call(
        body, name=name,
        out_shape=(jax.ShapeDtypeStruct((rows, D_MODEL), BF16), kv_shape, kv_shape, kv_shape, kv_shape,
                   meta_shape, meta_shape, jax.ShapeDtypeStruct((1, LANES), F32)),
        grid=(rows // CHUNK,),
        in_specs=[qspec] + _attn_specs() + _attn_specs() + [pl.BlockSpec(memory_space=pltpu.SMEM), qspec],
        out_specs=(qspec, kvspec, kvspec, kvspec, kvspec, fixed, fixed, pl.BlockSpec((1, LANES), lambda b: (0, 0))),
        compiler_params=_cparams(("arbitrary",)),
    )(q, k2, k2, k2, v2, v2, v2, sinks, do)


def _kv_grad_combine(name, d_cur, d_prev, d_meta):
    rows = d_cur.shape[0]
    nb = rows // CHUNK

    def body(c_ref, p_ref, m_ref, o_ref):
        j = pl.program_id(0)
        jj = j + jnp.zeros((CHUNK, 1), jnp.int32)
        o_ref[...] = (c_ref[...] + jnp.where(jj < nb - 1, p_ref[...], 0.0) + jnp.where(jj == 0, m_ref[...], 0.0))

    blk = lambda f: pl.BlockSpec((CHUNK, 2 * D_KV), f)
    return pl.pallas_call(
        body, name=name, out_shape=jax.ShapeDtypeStruct((rows, 2 * D_KV), F32), grid=(nb,),
        in_specs=[blk(lambda j: (j, 0)), blk(lambda j: (jnp.minimum(j + 1, nb - 1), 0)), blk(lambda j: (0, 0))],
        out_specs=blk(lambda j: (j, 0)), compiler_params=_cparams(("parallel",)),
    )(d_cur, d_prev, d_meta)


def _loss_head(name, h, target):
    rows = h.shape[0]

    def body(h_ref, t_ref, dh_ref, loss_ref):
        i = pl.program_id(0)
        real = (i + jnp.zeros((CHUNK, 1), jnp.int32)) >= 1
        diff = jnp.where(real, h_ref[...] - t_ref[...], 0.0)
        dh_ref[...] = diff * (1.0 / D_MODEL)

        @pl.when(i == 0)
        def _():
            loss_ref[...] = jnp.zeros_like(loss_ref)

        loss_ref[...] += jnp.sum(diff * diff) * (0.5 / D_MODEL)

    blk = pl.BlockSpec((CHUNK, D_MODEL), lambda i: (i, 0))
    return pl.pallas_call(
        body, name=name,
        out_shape=(jax.ShapeDtypeStruct((rows, D_MODEL), F32), jax.ShapeDtypeStruct((1, LANES), F32)),
        grid=(rows // CHUNK,),
        in_specs=[blk, pl.BlockSpec((CHUNK, D_MODEL), lambda i: (jnp.maximum(i - 1, 0), 0))],
        out_specs=(blk, pl.BlockSpec((1, LANES), lambda i: (0, 0))), compiler_params=_cparams(("arbitrary",)),
    )(h, target)


def _adamw(name, w, g, m, v):
    rows, width = w.shape
    tr = rows
    for cand in range(8, rows + 1, 8):
        if rows % cand == 0 and cand * width * 4 <= (1 << 20):
            tr = cand

    def body(w_ref, g_ref, m_ref, v_ref, d_ref, mo_ref, vo_ref):
        gv = g_ref[...]
        mn = ADAM_B1 * m_ref[...] + (1.0 - ADAM_B1) * gv
        vn = ADAM_B2 * v_ref[...] + (1.0 - ADAM_B2) * (gv * gv)
        m_hat = mn / (1.0 - ADAM_B1 ** ADAM_STEP)
        v_hat = vn / (1.0 - ADAM_B2 ** ADAM_STEP)
        d_ref[...] = -ADAM_LR * (m_hat / (jnp.sqrt(v_hat) + ADAM_EPS) + ADAM_WD * w_ref[...])
        mo_ref[...] = mn
        vo_ref[...] = vn

    blk = pl.BlockSpec((tr, width), lambda i: (i, 0))
    shp = jax.ShapeDtypeStruct((rows, width), F32)
    return pl.pallas_call(
        body, name=name, out_shape=(shp, shp, shp), grid=(rows // tr,), in_specs=[blk] * 4, out_specs=(blk,) * 3,
        compiler_params=_cparams(("parallel",)),
    )(w, g, m, v)


def _ffn_fwd(tag, h, p, i):
    hn = _rms_fwd(f"ffn{tag}_norm", h, p["f_norm_pre"][i:i + 1])
    up = _mm(f"ffn{tag}_up", hn, p["f_w_up"][i], "nn")
    act = _ffn_conv_fwd(f"ffn{tag}_conv", up, p["f_conv_w"][i], p["f_conv_b"][i:i + 1])
    pre = _mm(f"ffn{tag}_down", act, p["f_w_down"][i], "nn")
    h_new = _resid_norm_fwd(f"ffn{tag}_resid", h, pre, p["f_norm_post"][i:i + 1])
    return h_new, (h, hn, up, act, pre)


def _ffn_bwd(tag, dh, saved, p, i):
    h, hn, up, act, pre = saved
    dpre, g_post = _resid_norm_bwd(f"ffn{tag}_resid_bwd", dh, pre, p["f_norm_post"][i:i + 1])
    dact = _mm(f"ffn{tag}_down_dx", dpre, p["f_w_down"][i], "nt")
    g_down = _mm(f"ffn{tag}_down_dw", act, dpre, "tn")
    dup, g_cw, g_cb = _ffn_conv_bwd(f"ffn{tag}_conv_bwd", up, dact, p["f_conv_w"][i], p["f_conv_b"][i:i + 1])
    dhn = _mm(f"ffn{tag}_up_dx", dup, p["f_w_up"][i], "nt")
    g_up = _mm(f"ffn{tag}_up_dw", hn, dup, "tn")
    dh_new, g_pre = _norm_bwd_add(f"ffn{tag}_norm_bwd", dh, dhn, h, p["f_norm_pre"][i:i + 1])
    return dh_new, dict(f_norm_post=g_post, f_w_down=g_down, f_conv_w=g_cw, f_conv_b=g_cb, f_w_up=g_up, f_norm_pre=g_pre)


def _lanes_pad(a, width=LANES):
    return jnp.pad(a, [(0, 0)] * (a.ndim - 1) + [(0, width - a.shape[-1])])


def _dup_heads(a):
    rows = a.shape[0]
    a = a.reshape(rows, N_KV_HEADS, 1, HEAD_DIM)
    return jnp.broadcast_to(a, (rows, N_KV_HEADS, 2, HEAD_DIM)).reshape(rows, 2 * D_KV)


def _undup_heads(a):
    rows = a.shape[0]
    return a.reshape(rows, N_KV_HEADS, 2, HEAD_DIM).sum(axis=2).reshape(rows, D_KV)


def _local_step(x2, target, p):
    seq = x2.shape[0]
    rows = seq + CHUNK
    g = {}

    h0 = jnp.concatenate([jnp.zeros((PAD_ROWS, D_MODEL), F32), p["meta_tokens"], x2], axis=0)

    w_main = p["a_w_in"][:, :D_MAIN]
    w_dt = _lanes_pad(p["a_w_in"][:, D_MAIN:])
    dt_bias = _lanes_pad(p["a_dt_bias"])
    a_neg = -jnp.exp(p["a_a_log"].reshape(N_GROUPS, HEADS_PER_GROUP))
    a128 = _lanes_pad(a_neg.reshape(N_GROUPS, 1, HEADS_PER_GROUP))
    aexp = jnp.repeat(a_neg, HEAD_DIM, axis=1).reshape(N_GROUPS, 1, GROUP_W)
    dskexp = jnp.repeat(p["a_d_skip"].reshape(N_GROUPS, HEADS_PER_GROUP), HEAD_DIM, axis=1).reshape(N_GROUPS, 1, GROUP_W)

    hn0 = _rms_fwd("a_norm", h0, p["a_norm_pre"])
    zx = _mm("a_in_main", hn0, w_main, "nn")
    dtr = _mm("a_in_dt", hn0, w_dt, "nn")
    xbc = _conv4_fwd("a_conv", zx, p["a_conv_w"], p["a_conv_b"])
    dt = _dt_fwd("a_dt", dtr, dt_bias)
    dt4 = _lanes_pad(dt[:, :SSM_HEADS].reshape(rows, N_GROUPS, HEADS_PER_GROUP).transpose(1, 0, 2))
    y, states = _ssd_fwd("a_ssd", xbc, dt4, a128, aexp, dskexp)
    yn = _gate_fwd("a_gate", y, zx, p["a_gate_norm"])
    mix = _mm("a_out", yn, p["a_w_out"], "nn")
    h1 = _resid_norm_fwd("a_resid", h0, mix, p["a_norm_post"])

    h2, ffn0 = _ffn_fwd("0", h1, p, 0)

    hkv = _rms_fwd("kv_norm", h2, p["kv_norm"])
    kv = _mm("kv_proj", hkv, p["w_kv"], "nn")
    k2, v2 = _dup_heads(kv[:, :D_KV]), _dup_heads(kv[:, D_KV:])
    hn2 = _rms_fwd("b_norm", h2, p["b_norm_pre"])
    q = _mm("b_q", hn2, p["b_w_q"], "nn")
    sinks = p["b_sinks"].reshape(N_Q_HEADS)
    o = _attn_fwd("b_attn", q, k2, v2, sinks)
    attn = _mm("b_o", o, p["b_w_o"], "nn")
    h3 = _resid_norm_fwd("b_resid", h2, attn, p["b_norm_post"])

    h4, ffn1 = _ffn_fwd("1", h3, p, 1)

    dh, loss_vec = _loss_head("loss", h4, target)
    loss = loss_vec[0, 0]

    dh, g1 = _ffn_bwd("1", dh, ffn1, p, 1)

    dpre, g["b_norm_post"] = _resid_norm_bwd("b_resid_bwd", dh, attn, p["b_norm_post"])
    do = _mm("b_o_dx", dpre, p["b_w_o"], "nt")
    g["b_w_o"] = _mm("b_o_dw", o, dpre, "tn")
    dq, dkc, dkp, dvc, dvp, dkm, dvm, dsink = _attn_bwd("b_attn_bwd", q, k2, v2, sinks, do)
    g["b_sinks"] = dsink[:, :N_Q_HEADS]
    dhn2 = _mm("b_q_dx", dq, p["b_w_q"], "nt")
    g["b_w_q"] = _mm("b_q_dw", hn2, dq, "tn")
    dh, g["b_norm_pre"] = _norm_bwd_add("b_norm_bwd", dh, dhn2, h2, p["b_norm_pre"])
    dk2 = _kv_grad_combine("k_grad", dkc, dkp, dkm)
    dv2 = _kv_grad_combine("v_grad", dvc, dvp, dvm)
    dkv = jnp.concatenate([_undup_heads(dk2), _undup_heads(dv2)], axis=1).astype(BF16)
    dhkv = _mm("kv_proj_dx", dkv, p["w_kv"], "nt")
    g["w_kv"] = _mm("kv_proj_dw", hkv, dkv, "tn")
    dh, g["kv_norm"] = _norm_bwd_add("kv_norm_bwd", dh, dhkv, h2, p["kv_norm"])

    dh, g0 = _ffn_bwd("0", dh, ffn0, p, 0)
    for name in g0:
        g[name] = jnp.concatenate([g0[name], g1[name]], axis=0) if g0[name].shape[0] == 1 else jnp.stack([g0[name], g1[name]])

    dpre, g["a_norm_post"] = _resid_norm_bwd("a_resid_bwd", dh, mix, p["a_norm_post"])
    dyn = _mm("a_out_dx", dpre, p["a_w_out"], "nt")
    g["a_w_out"] = _mm("a_out_dw", yn, dpre, "tn")
    dy, dz, g["a_gate_norm"] = _gate_bwd("a_gate_bwd", dyn, y, zx, p["a_gate_norm"])
    dxs, dbm, dcm, ddt4, dalog, ddsk = _ssd_bwd("a_ssd_bwd", xbc, dt4, a128, aexp, dskexp, dy, states)
    g["a_a_log"] = dalog[:, 0, :HEADS_PER_GROUP].reshape(1, SSM_HEADS)
    g["a_d_skip"] = ddsk[:, 0, :HEADS_PER_GROUP].reshape(1, SSM_HEADS)
    ddt = _lanes_pad(ddt4[:, :, :HEADS_PER_GROUP].transpose(1, 0, 2).reshape(rows, SSM_HEADS))
    ddtr, dbias = _dt_bwd("a_dt_bwd", ddt, dtr, dt_bias)
    g["a_dt_bias"] = dbias[:, :SSM_HEADS]
    dxp, gw_x, gb_x = _conv4_bwd("a_conv_bwd_x", zx, dxs, p["a_conv_w"], p["a_conv_b"], 0)
    dbp, gw_b, gb_b = _conv4_bwd("a_conv_bwd_b", zx, dbm, p["a_conv_w"], p["a_conv_b"], D_INNER)
    dcp, gw_c, gb_c = _conv4_bwd("a_conv_bwd_c", zx, dcm, p["a_conv_w"], p["a_conv_b"], D_INNER + D_BC)
    g["a_conv_w"] = jnp.concatenate([gw_x, gw_b, gw_c], axis=1)
    g["a_conv_b"] = jnp.concatenate([gb_x, gb_b, gb_c], axis=1)
    dzx = jnp.concatenate([dz, dxp, dbp, dcp], axis=1)
    g_main = _mm("a_in_main_dw", hn0, dzx, "tn")
    g_dt = _mm("a_in_dt_dw", hn0, ddtr, "tn")
    g["a_w_in"] = jnp.concatenate([g_main, g_dt[:, :SSM_HEADS]], axis=1)
    dhn0 = _mm("a_in_main_dx", dzx, w_main, "nt")
    dhn0 = _mm("a_in_dt_dx", ddtr, w_dt, "nt", acc=dhn0)
    dh, g["a_norm_pre"] = _norm_bwd_add("a_norm_bwd", dh, dhn0, h0, p["a_norm_pre"])

    g["meta_tokens"] = dh[PAD_ROWS:CHUNK]
    return loss, dh[CHUNK:], g


def _place():
    return lax.axis_index("x"), lax.axis_index("y"), lax.axis_index("c")


def _other_chips(x, y):
    return [(1 - x, y), (x, 1 - y), (1 - x, 1 - y)]


ANY = pl.BlockSpec(memory_space=pl.ANY)
VMEM_SPEC = pl.BlockSpec(memory_space=pltpu.VMEM)


def _allgather_small(name, shard):
    rows = shard.shape[0]

    def body(s_ref, o_ref, send_sems, recv_sems):
        x, y, c = _place()
        me = 2 * x + y
        o_ref[me] = s_ref[...]
        chips = _other_chips(x, y)
        sends = [pltpu.make_async_remote_copy(s_ref, o_ref.at[me], send_sems.at[j], recv_sems.at[j],
                                              device_id=(cx, cy, c), device_id_type=MESH)
                 for j, (cx, cy) in enumerate(chips)]
        for cp in sends:
            cp.start()
        for j, (cx, cy) in enumerate(chips):
            pltpu.make_async_remote_copy(s_ref, o_ref.at[2 * cx + cy], send_sems.at[j], recv_sems.at[j],
                                         device_id=(cx, cy, c), device_id_type=MESH).wait_recv()
        for cp in sends:
            cp.wait_send()

    return pl.pallas_call(
        body, name=name, out_shape=jax.ShapeDtypeStruct((N_CHIPS, rows, LANES), F32),
        in_specs=[VMEM_SPEC], out_specs=VMEM_SPEC,
        scratch_shapes=[pltpu.SemaphoreType.DMA((3,)), pltpu.SemaphoreType.DMA((3,))],
        compiler_params=pltpu.CompilerParams(vmem_limit_bytes=VMEM_LIMIT),
    )(shard)


def _allgather_big(name, shard):
    def body(s_ref, o_ref, send_sems, recv_sems, local_sem):
        x, y, c = _place()
        me = 2 * x + y
        chips = _other_chips(x, y)
        half = pl.ds(c * HALF_ROWS, HALF_ROWS)
        other_half = pl.ds((1 - c) * HALF_ROWS, HALF_ROWS)
        sibling = (x, y, 1 - c)

        def ici(j, cx, cy, idx):
            return pltpu.make_async_remote_copy(s_ref.at[half], o_ref.at[idx, half], send_sems.at[j], recv_sems.at[j],
                                                device_id=(cx, cy, c), device_id_type=MESH)

        def d2d(j, idx, rows_):
            return pltpu.make_async_remote_copy(o_ref.at[idx, rows_], o_ref.at[idx, rows_], send_sems.at[3 + j],
                                                recv_sems.at[3 + j], device_id=sibling, device_id_type=MESH)

        mine = pltpu.make_async_copy(s_ref, o_ref.at[me], local_sem)
        mine.start()
        sends = [ici(j, cx, cy, me) for j, (cx, cy) in enumerate(chips)]
        for cp in sends:
            cp.start()
        passed = []
        for j, (cx, cy) in enumerate(chips):
            ici(j, cx, cy, 2 * cx + cy).wait_recv()
            fwd = d2d(j, 2 * cx + cy, half)
            fwd.start()
            passed.append(fwd)
        for j, (cx, cy) in enumerate(chips):
            d2d(j, 2 * cx + cy, other_half).wait_recv()
        for cp in sends + passed:
            cp.wait_send()
        mine.wait()

    return pl.pallas_call(
        body, name=name, out_shape=jax.ShapeDtypeStruct((N_CHIPS,) + shard.shape, shard.dtype),
        in_specs=[ANY], out_specs=ANY,
        scratch_shapes=[pltpu.SemaphoreType.DMA((6,)), pltpu.SemaphoreType.DMA((6,)), pltpu.SemaphoreType.DMA],
        compiler_params=pltpu.CompilerParams(vmem_limit_bytes=VMEM_LIMIT),
    )(shard)


def _allreduce_small(name, vec):
    rows = vec.shape[0]

    def body(v_ref, o_ref, buf, send_sems, recv_sems):
        x, y, c = _place()
        me = 4 * x + 2 * y + c
        buf[me] = v_ref[...]

        def peer(k):
            kx, ky, kc = (k >> 2) & 1, (k >> 1) & 1, k & 1
            return (1 - x if kx else x, 1 - y if ky else y, 1 - c if kc else c)

        sends = []
        for k in range(1, N_DEV):
            cp = pltpu.make_async_remote_copy(v_ref, buf.at[me], send_sems.at[k - 1], recv_sems.at[k - 1],
                                              device_id=peer(k), device_id_type=MESH)
            cp.start()
            sends.append(cp)
        for k in range(1, N_DEV):
            px, py, pc = peer(k)
            pltpu.make_async_remote_copy(v_ref, buf.at[4 * px + 2 * py + pc], send_sems.at[k - 1], recv_sems.at[k - 1],
                                         device_id=(px, py, pc), device_id_type=MESH).wait_recv()
        for cp in sends:
            cp.wait_send()
        acc = buf[0]
        for d in range(1, N_DEV):
            acc = acc + buf[d]
        o_ref[...] = acc

    return pl.pallas_call(
        body, name=name, out_shape=jax.ShapeDtypeStruct((rows, LANES), F32),
        in_specs=[VMEM_SPEC], out_specs=VMEM_SPEC,
        scratch_shapes=[pltpu.VMEM((N_DEV, rows, LANES), F32), pltpu.SemaphoreType.DMA((N_DEV - 1,)),
                        pltpu.SemaphoreType.DMA((N_DEV - 1,))],
        compiler_params=pltpu.CompilerParams(vmem_limit_bytes=VMEM_LIMIT),
    )(vec)


def _rs_pair_exchange(name, grads):
    def body(g_ref, p_ref, send_sem, recv_sem):
        x, y, c = _place()
        other_half = pl.ds((1 - c) * HALF_ROWS, HALF_ROWS)
        cp = pltpu.make_async_remote_copy(g_ref.at[:, other_half], p_ref, send_sem, recv_sem,
                                          device_id=(x, y, 1 - c), device_id_type=MESH)
        cp.start()
        cp.wait()

    return pl.pallas_call(
        body, name=name, out_shape=jax.ShapeDtypeStruct((N_CHIPS, HALF_ROWS, FLAT_W), grads.dtype),
        in_specs=[ANY], out_specs=ANY,
        scratch_shapes=[pltpu.SemaphoreType.DMA, pltpu.SemaphoreType.DMA],
        compiler_params=pltpu.CompilerParams(vmem_limit_bytes=VMEM_LIMIT),
    )(grads)


def _rs_pair_add(name, core, grads, partner):
    tr = HALF_ROWS // 3
    nb = HALF_ROWS // tr

    def body(c_ref, g_ref, p_ref, o_ref):
        o_ref[...] = (g_ref[...].astype(F32) + p_ref[...].astype(F32)).astype(BF16)

    return pl.pallas_call(
        body, name=name, out_shape=jax.ShapeDtypeStruct((N_CHIPS, HALF_ROWS, FLAT_W), BF16),
        grid_spec=pltpu.PrefetchScalarGridSpec(
            num_scalar_prefetch=1, grid=(N_CHIPS, nb),
            in_specs=[pl.BlockSpec((None, tr, FLAT_W), lambda s, i, c_ref: (s, c_ref[0] * nb + i, 0)),
                      pl.BlockSpec((None, tr, FLAT_W), lambda s, i, c_ref: (s, i, 0))],
            out_specs=pl.BlockSpec((None, tr, FLAT_W), lambda s, i, c_ref: (s, i, 0))),
        compiler_params=_cparams(("parallel", "parallel")),
    )(core, grads, partner)


def _rs_chip_exchange(name, partial):
    def body(q_ref, r_ref, send_sems, recv_sems, local_sem):
        x, y, c = _place()
        me = 2 * x + y
        chips = _other_chips(x, y)
        mine = pltpu.make_async_copy(q_ref.at[me], r_ref.at[me], local_sem)
        mine.start()
        sends = [pltpu.make_async_remote_copy(q_ref.at[2 * cx + cy], r_ref.at[me], send_sems.at[j], recv_sems.at[j],
                                              device_id=(cx, cy, c), device_id_type=MESH)
                 for j, (cx, cy) in enumerate(chips)]
        for cp in sends:
            cp.start()
        for j, (cx, cy) in enumerate(chips):
            pltpu.make_async_remote_copy(q_ref.at[me], r_ref.at[2 * cx + cy], send_sems.at[j], recv_sems.at[j],
                                         device_id=(cx, cy, c), device_id_type=MESH).wait_recv()
        for cp in sends:
            cp.wait_send()
        mine.wait()

    return pl.pallas_call(
        body, name=name, out_shape=jax.ShapeDtypeStruct(partial.shape, partial.dtype),
        in_specs=[ANY], out_specs=ANY,
        scratch_shapes=[pltpu.SemaphoreType.DMA((3,)), pltpu.SemaphoreType.DMA((3,)), pltpu.SemaphoreType.DMA],
        compiler_params=pltpu.CompilerParams(vmem_limit_bytes=VMEM_LIMIT),
    )(partial)


def _rs_chip_add(name, parts):
    tr = HALF_ROWS // 9

    def body(r_ref, o_ref):
        acc = r_ref[0].astype(F32)
        for s in range(1, N_CHIPS):
            acc = acc + r_ref[s].astype(F32)
        o_ref[...] = acc

    return pl.pallas_call(
        body, name=name, out_shape=jax.ShapeDtypeStruct((HALF_ROWS, FLAT_W), F32), grid=(HALF_ROWS // tr,),
        in_specs=[pl.BlockSpec((N_CHIPS, tr, FLAT_W), lambda i: (0, i, 0))],
        out_specs=pl.BlockSpec((tr, FLAT_W), lambda i: (i, 0)), compiler_params=_cparams(("parallel",)),
    )(parts)


def _rs_pair_gather(name, half_sum):
    def body(h_ref, f_ref, send_sem, recv_sem, local_sem):
        x, y, c = _place()
        half = pl.ds(c * HALF_ROWS, HALF_ROWS)
        other_half = pl.ds((1 - c) * HALF_ROWS, HALF_ROWS)
        mine = pltpu.make_async_copy(h_ref, f_ref.at[half], local_sem)
        mine.start()
        cp = pltpu.make_async_remote_copy(h_ref, f_ref.at[half], send_sem, recv_sem,
                                          device_id=(x, y, 1 - c), device_id_type=MESH)
        cp.start()
        pltpu.make_async_remote_copy(h_ref, f_ref.at[other_half], send_sem, recv_sem,
                                     device_id=(x, y, 1 - c), device_id_type=MESH).wait_recv()
        cp.wait_send()
        mine.wait()

    return pl.pallas_call(
        body, name=name, out_shape=jax.ShapeDtypeStruct((FLAT_ROWS, FLAT_W), F32),
        in_specs=[ANY], out_specs=ANY,
        scratch_shapes=[pltpu.SemaphoreType.DMA, pltpu.SemaphoreType.DMA, pltpu.SemaphoreType.DMA],
        compiler_params=pltpu.CompilerParams(vmem_limit_bytes=VMEM_LIMIT),
    )(half_sum)


WEIGHTS = ["meta_tokens", "a_norm_pre", "a_w_in", "a_conv_w", "a_conv_b", "a_dt_bias", "a_a_log", "a_d_skip",
           "a_gate_norm", "a_w_out", "a_norm_post", "kv_norm", "w_kv", "b_norm_pre", "b_w_q", "b_sinks", "b_w_o",
           "b_norm_post", "f_norm_pre", "f_w_up", "f_conv_w", "f_conv_b", "f_w_down", "f_norm_post"]
FULL_SHAPE = {
    "meta_tokens": (16, 1024), "a_norm_pre": (1, 1024), "a_w_in": (1, 1024, 5152), "a_conv_w": (1, 4, 3072),
    "a_conv_b": (1, 3072), "a_dt_bias": (1, 32), "a_a_log": (1, 32), "a_d_skip": (1, 32), "a_gate_norm": (1, 2048),
    "a_w_out": (1, 2048, 1024), "a_norm_post": (1, 1024), "kv_norm": (1024,), "w_kv": (1024, 512),
    "b_norm_pre": (1, 1024), "b_w_q": (1, 1024, 1024), "b_sinks": (1, 16), "b_w_o": (1, 1024, 1024),
    "b_norm_post": (1, 1024), "f_norm_pre": (2, 1024), "f_w_up": (2, 1024, 5632), "f_conv_w": (2, 3, 5632),
    "f_conv_b": (2, 5632), "f_w_down": (2, 2816, 1024), "f_norm_post": (2, 1024),
}
SHARD_AXIS = {
    "meta_tokens": 1, "a_norm_pre": 1, "a_w_in": 2, "a_conv_w": 2, "a_conv_b": 1, "a_dt_bias": None, "a_a_log": None,
    "a_d_skip": None, "a_gate_norm": 1, "a_w_out": 1, "a_norm_post": 1, "kv_norm": None, "w_kv": 0, "b_norm_pre": None,
    "b_w_q": 1, "b_sinks": None, "b_w_o": 1, "b_norm_post": None, "f_norm_pre": None, "f_w_up": 2, "f_conv_w": 2,
    "f_conv_b": None, "f_w_down": 1, "f_norm_post": None,
}
BIG = ["a_w_in", "a_w_out", "w_kv", "b_w_q", "b_w_o", "f_w_up", "f_w_down"]
SMALL = [n for n in WEIGHTS if n not in BIG]
SMALL_SHARDED = [n for n in SMALL if SHARD_AXIS[n] is not None]


def _shard_shape(name):
    shape = list(FULL_SHAPE[name])
    if SHARD_AXIS[name] is not None:
        shape[SHARD_AXIS[name]] //= N_CHIPS
    return tuple(shape)


def _numel(shape):
    return int(math.prod(shape))


def _pack(arrays, rows, width, dtype):
    flat = jnp.concatenate([a.reshape(-1).astype(dtype) for a in arrays])
    return jnp.pad(flat, (0, rows * width - flat.shape[0])).reshape(rows, width)


def _unpack(flat, names, shape_of):
    out, off = {}, 0
    for n in names:
        size = _numel(shape_of(n))
        out[n] = flat[..., off:off + size].reshape(flat.shape[:-1] + tuple(shape_of(n)))
        off += size
    return out


def _split_chips(name, full):
    ax = SHARD_AXIS[name]
    shape = full.shape
    cut = shape[:ax] + (N_CHIPS, shape[ax] // N_CHIPS) + shape[ax + 1:]
    return jnp.moveaxis(full.reshape(cut), ax, 0)


def _join_chips(name, stacked):
    ax = SHARD_AXIS[name]
    moved = jnp.moveaxis(stacked, 0, ax)
    shape = moved.shape
    return moved.reshape(shape[:ax] + (shape[ax] * shape[ax + 1],) + shape[ax + 2:])


def _rows_for(n_elems, width, mult):
    rows = -(-n_elems // width)
    return -(-rows // mult) * mult


def _as2d(a):
    return a.reshape(-1, a.shape[-1])


def kernel(x, meta_tokens, a_norm_pre, a_w_in, a_conv_w, a_conv_b, a_dt_bias, a_a_log, a_d_skip, a_gate_norm, a_w_out, a_norm_post, kv_norm, w_kv, b_norm_pre, b_w_q, b_sinks, b_w_o, b_norm_post, f_norm_pre, f_w_up, f_conv_w, f_conv_b, f_w_down, f_norm_post, loss_target, m_meta_tokens, m_a_norm_pre, m_a_w_in, m_a_conv_w, m_a_conv_b, m_a_dt_bias, m_a_a_log, m_a_d_skip, m_a_gate_norm, m_a_w_out, m_a_norm_post, m_kv_norm, m_w_kv, m_b_norm_pre, m_b_w_q, m_b_sinks, m_b_w_o, m_b_norm_post, m_f_norm_pre, m_f_w_up, m_f_conv_w, m_f_conv_b, m_f_w_down, m_f_norm_post, v_meta_tokens, v_a_norm_pre, v_a_w_in, v_a_conv_w, v_a_conv_b, v_a_dt_bias, v_a_a_log, v_a_d_skip, v_a_gate_norm, v_a_w_out, v_a_norm_post, v_kv_norm, v_w_kv, v_b_norm_pre, v_b_w_q, v_b_sinks, v_b_w_o, v_b_norm_post, v_f_norm_pre, v_f_w_up, v_f_conv_w, v_f_conv_b, v_f_w_down, v_f_norm_post):
    given = dict(locals())
    w = {n: given[n] for n in WEIGHTS}
    mom = {n: given["m_" + n] for n in WEIGHTS}
    var = {n: given["v_" + n] for n in WEIGHTS}
    chip = 2 * lax.axis_index("x") + lax.axis_index("y")
    core = lax.axis_index("c")

    small_rows = _rows_for(sum(_numel(_shard_shape(n)) for n in SMALL_SHARDED), LANES, 8)
    small_all = _allgather_small("gather_small", _pack([w[n] for n in SMALL_SHARDED], small_rows, LANES, F32))
    small_parts = _unpack(small_all.reshape(N_CHIPS, -1), SMALL_SHARDED, _shard_shape)
    big_all = _allgather_big("gather_big", _pack([w[n] for n in BIG], FLAT_ROWS, FLAT_W, BF16))
    big_parts = _unpack(big_all.reshape(N_CHIPS, -1), BIG, _shard_shape)
    full = {}
    for n in WEIGHTS:
        if n in BIG:
            full[n] = _join_chips(n, big_parts[n])
        elif n in SMALL_SHARDED:
            full[n] = _join_chips(n, small_parts[n])
        else:
            full[n] = w[n]
    p = dict(full)
    for n in ("a_w_in", "a_conv_w", "a_w_out", "b_w_q", "b_w_o"):
        p[n] = full[n][0]
    p["kv_norm"] = full["kv_norm"].reshape(1, D_MODEL)

    loss_local, grad_x, g = _local_step(x[0], loss_target[0], p)
    loss = lax.psum(loss_local, ("x", "y", "c"))
    g_full = {n: g[n].reshape(FULL_SHAPE[n]) for n in WEIGHTS}

    small_total = sum(_numel(FULL_SHAPE[n]) for n in SMALL)
    red_rows = _rows_for(small_total, LANES, 8)
    small_sum = _allreduce_small("reduce_small", _pack([g_full[n] for n in SMALL], red_rows, LANES, F32))
    small_red = _unpack(small_sum.reshape(-1), SMALL, lambda n: FULL_SHAPE[n])
    grads = {}
    for n in SMALL:
        if SHARD_AXIS[n] is None:
            grads[n] = small_red[n]
        else:
            grads[n] = lax.dynamic_index_in_dim(_split_chips(n, small_red[n]), chip, 0, keepdims=False)

    by_chip = jnp.concatenate([_split_chips(n, g_full[n]).reshape(N_CHIPS, -1) for n in BIG], axis=1)
    by_chip = jnp.pad(by_chip, ((0, 0), (0, FLAT_ROWS * FLAT_W - by_chip.shape[1])))
    by_chip = by_chip.astype(BF16).reshape(N_CHIPS, FLAT_ROWS, FLAT_W)
    partner = _rs_pair_exchange("reduce_pair_send", by_chip)
    pair_sum = _rs_pair_add("reduce_pair_add", core.reshape(1).astype(jnp.int32), by_chip, partner)
    from_chips = _rs_chip_exchange("reduce_chip_send", pair_sum)
    half_sum = _rs_chip_add("reduce_chip_add", from_chips)
    shard_sum = _rs_pair_gather("reduce_pair_gather", half_sum)
    grads.update(_unpack(shard_sum.reshape(-1), BIG, _shard_shape))

    delta, new_m, new_v = {}, {}, {}
    for n in BIG:
        shape = _shard_shape(n)
        d, m2, v2 = _adamw("adamw_" + n, _as2d(w[n]), _as2d(grads[n]), _as2d(mom[n]), _as2d(var[n]))
        delta[n], new_m[n], new_v[n] = d.reshape(shape), m2.reshape(shape), v2.reshape(shape)
    local_small = sum(_numel(_shard_shape(n)) for n in SMALL)
    opt_rows = _rows_for(local_small, LANES, 8)
    packed = [_pack([src[n] for n in SMALL], opt_rows, LANES, F32) for src in (w, grads, mom, var)]
    outs = _adamw("adamw_small", *packed)
    for dst, flat in zip((delta, new_m, new_v), outs):
        dst.update(_unpack(flat.reshape(-1), SMALL, _shard_shape))

    return (loss, grad_x[None], *[grads[n].reshape(_shard_shape(n)) for n in WEIGHTS],
            *[delta[n] for n in WEIGHTS], *[new_m[n] for n in WEIGHTS], *[new_v[n] for n in WEIGHTS])
```

```python
import functools
import math

import jax
import jax.numpy as jnp
from jax import lax
from jax.experimental import pallas as pl
from jax.experimental.pallas import tpu as pltpu

F32, BF16 = jnp.float32, jnp.bfloat16
MESH = pl.DeviceIdType.MESH
HIGHEST = lax.Precision.HIGHEST

D_MODEL = 1024
N_META = 16
CHUNK = 128
PAD_ROWS = CHUNK - N_META
D_INNER = 2048
D_STATE = 128
N_GROUPS = 4
HEADS_PER_GROUP = 8
SSM_HEADS = 32
HEAD_DIM = 64
D_BC = N_GROUPS * D_STATE
D_XBC = D_INNER + 2 * D_BC
D_MAIN = D_INNER + D_XBC
D_IN_PROJ = D_MAIN + SSM_HEADS
GROUP_W = HEADS_PER_GROUP * HEAD_DIM
SSM_CONV = 4
D_FF = 2816
FFN_CONV = 3
N_Q_HEADS = 16
N_KV_HEADS = 4
D_KV = 256
ATTN_SCALE = 1.0 / math.sqrt(HEAD_DIM)
RMS_EPS = 1e-6
NEG_INF = -1e30
LANES = 128
VMEM_LIMIT = 48 * 1024 * 1024

ADAM_LR, ADAM_B1, ADAM_B2, ADAM_EPS, ADAM_WD, ADAM_STEP = 0.001, 0.9, 0.999, 1e-08, 0.01, 10

N_CHIPS = 4
N_DEV = 8


def _cparams(sem=None):
    return pltpu.CompilerParams(dimension_semantics=sem, vmem_limit_bytes=VMEM_LIMIT)


def _tile(n, cands=(512, 256, 128)):
    for t in cands:
        if n % t == 0:
            return t
    return n


def _row_tile(rows, width):
    for t in (544, 272):
        if rows % t == 0 and t * width * 4 <= (3 << 20):
            return t
    return 128


def _rows_mask(i, tm):
    rows = i * tm + lax.broadcasted_iota(jnp.int32, (tm, 1), 0)
    return rows >= PAD_ROWS


def _dot(a, b):
    return jnp.dot(a, b, preferred_element_type=F32)


def _dot_nt(a, b):
    return lax.dot_general(a, b, (((1,), (1,)), ((), ())), preferred_element_type=F32)


def _dot_tn(a, b):
    return lax.dot_general(a, b, (((0,), (0,)), ((), ())), preferred_element_type=F32)


def _dot_hi(a, b):
    return jnp.dot(a, b, preferred_element_type=F32, precision=HIGHEST)


def _sigmoid(x):
    return 1.0 / (1.0 + jnp.exp(-x))


def _mm(name, a, b, mode, out_dtype=F32, acc=None):
    resident_bytes = 8 << 20
    if mode == "nn":
        m, k = a.shape
        n = b.shape[1]
        tm = m
        while tm * k * 2 > resident_bytes and tm % 32 == 0:
            tm //= 2
        tn = _tile(n)
        grid = (m // tm, n // tn)
        in_specs = [pl.BlockSpec((tm, k), lambda i, j: (i, 0)), pl.BlockSpec((k, tn), lambda i, j: (0, j))]
        out_shape, out_block = (m, n), (tm, tn)
    elif mode == "nt":
        m, n = a.shape
        k = b.shape[0]
        tm = m
        while tm * n * 2 > resident_bytes and tm % 32 == 0:
            tm //= 2
        tk = _tile(k)
        grid = (m // tm, k // tk)
        in_specs = [pl.BlockSpec((tm, n), lambda i, j: (i, 0)), pl.BlockSpec((tk, n), lambda i, j: (j, 0))]
        out_shape, out_block = (m, k), (tm, tk)
    else:
        m, k = a.shape
        n = b.shape[1]
        tk, tn = _tile(k), _tile(n)
        grid = (k // tk, n // tn)
        in_specs = [pl.BlockSpec((m, tk), lambda i, j: (0, i)), pl.BlockSpec((m, tn), lambda i, j: (0, j))]
        out_shape, out_block = (k, n), (tk, tn)
    out_spec = pl.BlockSpec(out_block, lambda i, j: (i, j))
    has_acc = acc is not None

    def body(*refs):
        a_ref, b_ref = refs[0], refs[1]
        o_ref = refs[-1]
        av, bv = a_ref[...], b_ref[...]
        if mode == "nn":
            r = _dot(av, bv)
        elif mode == "nt":
            r = _dot_nt(av, bv)
        else:
            r = _dot_tn(av, bv)
        if has_acc:
            r = r + refs[2][...]
        o_ref[...] = r.astype(o_ref.dtype)

    operands = [a, b]
    if has_acc:
        in_specs = in_specs + [out_spec]
        operands.append(acc)
    return pl.pallas_call(
        body, name=name, out_shape=jax.ShapeDtypeStruct(out_shape, out_dtype), grid=grid,
        in_specs=in_specs, out_specs=out_spec, compiler_params=_cparams(("parallel", "parallel")),
    )(*operands)


def _rms_fwd(name, h, w):
    rows, width = h.shape
    tm = _row_tile(rows, width)

    def body(h_ref, w_ref, o_ref):
        x = h_ref[...]
        r = lax.rsqrt(jnp.mean(x * x, axis=-1, keepdims=True) + RMS_EPS)
        o_ref[...] = (x * r * w_ref[...]).astype(BF16)

    return pl.pallas_call(
        body, name=name, out_shape=jax.ShapeDtypeStruct((rows, width), BF16), grid=(rows // tm,),
        in_specs=[pl.BlockSpec((tm, width), lambda i: (i, 0)), pl.BlockSpec((1, width), lambda i: (0, 0))],
        out_specs=pl.BlockSpec((tm, width), lambda i: (i, 0)), compiler_params=_cparams(("parallel",)),
    )(h, w)


def _resid_norm_fwd(name, h, pre, w):
    rows, width = h.shape
    tm = _row_tile(rows, width)

    def body(h_ref, p_ref, w_ref, o_ref):
        p = p_ref[...]
        r = lax.rsqrt(jnp.mean(p * p, axis=-1, keepdims=True) + RMS_EPS)
        o_ref[...] = h_ref[...] + jnp.where(_rows_mask(pl.program_id(0), tm), p * r * w_ref[...], 0.0)

    row_spec = pl.BlockSpec((tm, width), lambda i: (i, 0))
    return pl.pallas_call(
        body, name=name, out_shape=jax.ShapeDtypeStruct((rows, width), F32), grid=(rows // tm,),
        in_specs=[row_spec, row_spec, pl.BlockSpec((1, width), lambda i: (0, 0))],
        out_specs=row_spec, compiler_params=_cparams(("parallel",)),
    )(h, pre, w)


def _resid_norm_bwd(name, dh, pre, w):
    rows, width = dh.shape
    tm = _row_tile(rows, width)

    def body(dh_ref, p_ref, w_ref, dp_ref, dw_ref):
        i = pl.program_id(0)
        dy = jnp.where(_rows_mask(i, tm), dh_ref[...], 0.0)
        p = p_ref[...]
        r = lax.rsqrt(jnp.mean(p * p, axis=-1, keepdims=True) + RMS_EPS)
        xhat = p * r
        dxhat = dy * w_ref[...]
        dp = r * (dxhat - xhat * jnp.mean(dxhat * xhat, axis=-1, keepdims=True))
        dp_ref[...] = dp.astype(BF16)

        @pl.when(i == 0)
        def _():
            dw_ref[...] = jnp.zeros_like(dw_ref)

        dw_ref[...] += jnp.sum(dy * xhat, axis=0, keepdims=True)

    row_spec = pl.BlockSpec((tm, width), lambda i: (i, 0))
    vec_spec = pl.BlockSpec((1, width), lambda i: (0, 0))
    return pl.pallas_call(
        body, name=name,
        out_shape=(jax.ShapeDtypeStruct((rows, width), BF16), jax.ShapeDtypeStruct((1, width), F32)),
        grid=(rows // tm,), in_specs=[row_spec, row_spec, vec_spec], out_specs=(row_spec, vec_spec),
        compiler_params=_cparams(("arbitrary",)),
    )(dh, pre, w)


def _norm_bwd_add(name, dh, dhn, h, w):
    rows, width = dh.shape
    tm = _row_tile(rows, width)

    def body(dh_ref, dhn_ref, h_ref, w_ref, o_ref, dw_ref):
        i = pl.program_id(0)
        x = h_ref[...]
        dy = dhn_ref[...]
        r = lax.rsqrt(jnp.mean(x * x, axis=-1, keepdims=True) + RMS_EPS)
        xhat = x * r
        dxhat = dy * w_ref[...]
        dx = r * (dxhat - xhat * jnp.mean(dxhat * xhat, axis=-1, keepdims=True))
        o_ref[...] = dh_ref[...] + jnp.where(_rows_mask(i, tm), dx, 0.0)

        @pl.when(i == 0)
        def _():
            dw_ref[...] = jnp.zeros_like(dw_ref)

        dw_ref[...] += jnp.sum(dy * xhat, axis=0, keepdims=True)

    row_spec = pl.BlockSpec((tm, width), lambda i: (i, 0))
    vec_spec = pl.BlockSpec((1, width), lambda i: (0, 0))
    return pl.pallas_call(
        body, name=name,
        out_shape=(jax.ShapeDtypeStruct((rows, width), F32), jax.ShapeDtypeStruct((1, width), F32)),
        grid=(rows // tm,), in_specs=[row_spec, row_spec, row_spec, vec_spec], out_specs=(row_spec, vec_spec),
        compiler_params=_cparams(("arbitrary",)),
    )(dh, dhn, h, w)


def _shift_down(x, s, rows):
    return pltpu.roll(x, s, 0) if s else x


def _shift_up(x, s, rows):
    return pltpu.roll(x, rows - s, 0) if s else x


def _conv4_fwd(name, zx, cw, cb):
    rows = zx.shape[0]
    off = D_INNER // LANES

    def body(x_ref, w_ref, b_ref, o_ref):
        x = x_ref[...]
        acc = b_ref[...] + w_ref[pl.ds(SSM_CONV - 1, 1), :] * x
        for s in range(1, SSM_CONV):
            acc = acc + w_ref[pl.ds(SSM_CONV - 1 - s, 1), :] * _shift_down(x, s, rows)
        valid = lax.broadcasted_iota(jnp.int32, (rows, 1), 0) >= PAD_ROWS
        o_ref[...] = jnp.where(valid, acc * _sigmoid(acc), 0.0)

    return pl.pallas_call(
        body, name=name, out_shape=jax.ShapeDtypeStruct((rows, D_XBC), F32), grid=(D_XBC // LANES,),
        in_specs=[pl.BlockSpec((rows, LANES), lambda j: (0, j + off)),
                  pl.BlockSpec((SSM_CONV, LANES), lambda j: (0, j)),
                  pl.BlockSpec((1, LANES), lambda j: (0, j))],
        out_specs=pl.BlockSpec((rows, LANES), lambda j: (0, j)), compiler_params=_cparams(("parallel",)),
    )(zx, cw, cb)


def _conv4_bwd(name, zx, dout, cw, cb, col0):
    rows, width = dout.shape
    zoff = (D_INNER + col0) // LANES
    woff = col0 // LANES

    def body(x_ref, d_ref, w_ref, b_ref, dx_ref, dw_ref, db_ref):
        x = x_ref[...]
        shifted = [_shift_down(x, s, rows) for s in range(SSM_CONV)]
        acc = b_ref[...]
        for s in range(SSM_CONV):
            acc = acc + w_ref[pl.ds(SSM_CONV - 1 - s, 1), :] * shifted[s]
        sig = _sigmoid(acc)
        valid = lax.broadcasted_iota(jnp.int32, (rows, 1), 0) >= PAD_ROWS
        dpre = jnp.where(valid, d_ref[...] * sig * (1.0 + acc * (1.0 - sig)), 0.0)
        dx = w_ref[pl.ds(SSM_CONV - 1, 1), :] * dpre
        for s in range(1, SSM_CONV):
            dx = dx + w_ref[pl.ds(SSM_CONV - 1 - s, 1), :] * _shift_up(dpre, s, rows)
        dx_ref[...] = dx.astype(BF16)
        for s in range(SSM_CONV):
            dw_ref[pl.ds(SSM_CONV - 1 - s, 1), :] = jnp.sum(dpre * shifted[s], axis=0, keepdims=True)
        db_ref[...] = jnp.sum(dpre, axis=0, keepdims=True)

    return pl.pallas_call(
        body, name=name,
        out_shape=(jax.ShapeDtypeStruct((rows, width), BF16), jax.ShapeDtypeStruct((SSM_CONV, width), F32),
                   jax.ShapeDtypeStruct((1, width), F32)),
        grid=(width // LANES,),
        in_specs=[pl.BlockSpec((rows, LANES), lambda j: (0, j + zoff)),
                  pl.BlockSpec((rows, LANES), lambda j: (0, j)),
                  pl.BlockSpec((SSM_CONV, LANES), lambda j: (0, j + woff)),
                  pl.BlockSpec((1, LANES), lambda j: (0, j + woff))],
        out_specs=(pl.BlockSpec((rows, LANES), lambda j: (0, j)),
                   pl.BlockSpec((SSM_CONV, LANES), lambda j: (0, j)),
                   pl.BlockSpec((1, LANES), lambda j: (0, j))),
        compiler_params=_cparams(("parallel",)),
    )(zx, dout, cw, cb)


def _ffn_conv_fwd(name, up, cw, cb):
    rows = up.shape[0]
    nt = D_FF // LANES

    def body(g_ref, v_ref, wg_ref, wv_ref, bg_ref, bv_ref, o_ref):
        g, v = g_ref[...], v_ref[...]
        ug, uv = bg_ref[...], bv_ref[...]
        for s in range(FFN_CONV):
            ug = ug + wg_ref[pl.ds(FFN_CONV - 1 - s, 1), :] * _shift_down(g, s, rows)
            uv = uv + wv_ref[pl.ds(FFN_CONV - 1 - s, 1), :] * _shift_down(v, s, rows)
        valid = lax.broadcasted_iota(jnp.int32, (rows, 1), 0) >= PAD_ROWS
        o_ref[...] = jnp.where(valid, ug * _sigmoid(ug) * uv, 0.0).astype(BF16)

    col = lambda shift: pl.BlockSpec((rows, LANES), lambda j: (0, j + shift))
    wsp = lambda shift: pl.BlockSpec((FFN_CONV, LANES), lambda j: (0, j + shift))
    bsp = lambda shift: pl.BlockSpec((1, LANES), lambda j: (0, j + shift))
    return pl.pallas_call(
        body, name=name, out_shape=jax.ShapeDtypeStruct((rows, D_FF), BF16), grid=(nt,),
        in_specs=[col(0), col(nt), wsp(0), wsp(nt), bsp(0), bsp(nt)],
        out_specs=pl.BlockSpec((rows, LANES), lambda j: (0, j)), compiler_params=_cparams(("parallel",)),
    )(up, up, cw, cw, cb, cb)


def _ffn_conv_bwd(name, up, dact, cw, cb):
    rows = up.shape[0]
    nt = D_FF // LANES

    def body(g_ref, v_ref, d_ref, wg_ref, wv_ref, bg_ref, bv_ref, dx_ref, dw_ref, db_ref):
        half = pl.program_id(0)
        g, v = g_ref[...], v_ref[...]
        gs = [_shift_down(g, s, rows) for s in range(FFN_CONV)]
        vs = [_shift_down(v, s, rows) for s in range(FFN_CONV)]
        ug, uv = bg_ref[...], bv_ref[...]
        for s in range(FFN_CONV):
            ug = ug + wg_ref[pl.ds(FFN_CONV - 1 - s, 1), :] * gs[s]
            uv = uv + wv_ref[pl.ds(FFN_CONV - 1 - s, 1), :] * vs[s]
        sig = _sigmoid(ug)
        valid = lax.broadcasted_iota(jnp.int32, (rows, 1), 0) >= PAD_ROWS
        d = jnp.where(valid, d_ref[...], 0.0)
        dug = d * uv * sig * (1.0 + ug * (1.0 - sig))
        duv = d * ug * sig
        is_gate = (half + jnp.zeros((rows, 1), jnp.int32)) == 0
        dpre = jnp.where(is_gate, dug, duv)
        is_gate_row = (half + jnp.zeros((1, LANES), jnp.int32)) == 0
        dx = jnp.zeros_like(dpre)
        for s in range(FFN_CONV):
            w_s = jnp.where(is_gate_row, wg_ref[pl.ds(FFN_CONV - 1 - s, 1), :], wv_ref[pl.ds(FFN_CONV - 1 - s, 1), :])
            dx = dx + w_s * _shift_up(dpre, s, rows)
            src = jnp.where(is_gate, gs[s], vs[s])
            dw_ref[pl.ds(FFN_CONV - 1 - s, 1), :] = jnp.sum(dpre * src, axis=0, keepdims=True)
        dx_ref[...] = dx.astype(BF16)
        db_ref[...] = jnp.sum(dpre, axis=0, keepdims=True)

    col = lambda shift: pl.BlockSpec((rows, LANES), lambda r, j: (0, j + shift))
    wsp = lambda shift: pl.BlockSpec((FFN_CONV, LANES), lambda r, j: (0, j + shift))
    bsp = lambda shift: pl.BlockSpec((1, LANES), lambda r, j: (0, j + shift))
    return pl.pallas_call(
        body, name=name,
        out_shape=(jax.ShapeDtypeStruct((rows, 2 * D_FF), BF16), jax.ShapeDtypeStruct((FFN_CONV, 2 * D_FF), F32),
                   jax.ShapeDtypeStruct((1, 2 * D_FF), F32)),
        grid=(2, nt),
        in_specs=[col(0), col(nt), col(0), wsp(0), wsp(nt), bsp(0), bsp(nt)],
        out_specs=(pl.BlockSpec((rows, LANES), lambda r, j: (0, r * nt + j)),
                   pl.BlockSpec((FFN_CONV, LANES), lambda r, j: (0, r * nt + j)),
                   pl.BlockSpec((1, LANES), lambda r, j: (0, r * nt + j))),
        compiler_params=_cparams(("parallel", "parallel")),
    )(up, up, dact, cw, cw, cb, cb)


def _dt_fwd(name, dtr, bias):
    rows = dtr.shape[0]
    tm = _row_tile(rows, LANES)

    def body(d_ref, b_ref, o_ref):
        v = d_ref[...] + b_ref[...]
        sp = jnp.maximum(v, 0.0) + jnp.log1p(jnp.exp(-jnp.abs(v)))
        lane = lax.broadcasted_iota(jnp.int32, (tm, LANES), 1)
        ok = _rows_mask(pl.program_id(0), tm) & (lane < SSM_HEADS)
        o_ref[...] = jnp.where(ok, sp, 0.0)

    return pl.pallas_call(
        body, name=name, out_shape=jax.ShapeDtypeStruct((rows, LANES), F32), grid=(rows // tm,),
        in_specs=[pl.BlockSpec((tm, LANES), lambda i: (i, 0)), pl.BlockSpec((1, LANES), lambda i: (0, 0))],
        out_specs=pl.BlockSpec((tm, LANES), lambda i: (i, 0)), compiler_params=_cparams(("parallel",)),
    )(dtr, bias)


def _dt_bwd(name, ddt, dtr, bias):
    rows = dtr.shape[0]
    tm = _row_tile(rows, LANES)

    def body(g_ref, d_ref, b_ref, o_ref, db_ref):
        i = pl.program_id(0)
        lane = lax.broadcasted_iota(jnp.int32, (tm, LANES), 1)
        ok = _rows_mask(i, tm) & (lane < SSM_HEADS)
        dv = jnp.where(ok, g_ref[...] * _sigmoid(d_ref[...] + b_ref[...]), 0.0)
        o_ref[...] = dv.astype(BF16)

        @pl.when(i == 0)
        def _():
            db_ref[...] = jnp.zeros_like(db_ref)

        db_ref[...] += jnp.sum(dv, axis=0, keepdims=True)

    row_spec = pl.BlockSpec((tm, LANES), lambda i: (i, 0))
    vec_spec = pl.BlockSpec((1, LANES), lambda i: (0, 0))
    return pl.pallas_call(
        body, name=name,
        out_shape=(jax.ShapeDtypeStruct((rows, LANES), BF16), jax.ShapeDtypeStruct((1, LANES), F32)),
        grid=(rows // tm,), in_specs=[row_spec, row_spec, vec_spec], out_specs=(row_spec, vec_spec),
        compiler_params=_cparams(("arbitrary",)),
    )(ddt, dtr, bias)


def _gate_fwd(name, y, zx, w):
    rows = y.shape[0]
    tm = _row_tile(rows, D_INNER)

    def body(y_ref, z_ref, w_ref, o_ref):
        z = z_ref[...]
        g = y_ref[...] * (z * _sigmoid(z))
        r = lax.rsqrt(jnp.mean(g * g, axis=-1, keepdims=True) + RMS_EPS)
        o_ref[...] = (g * r * w_ref[...]).astype(BF16)

    row_spec = pl.BlockSpec((tm, D_INNER), lambda i: (i, 0))
    return pl.pallas_call(
        body, name=name, out_shape=jax.ShapeDtypeStruct((rows, D_INNER), BF16), grid=(rows // tm,),
        in_specs=[row_spec, row_spec, pl.BlockSpec((1, D_INNER), lambda i: (0, 0))],
        out_specs=row_spec, compiler_params=_cparams(("parallel",)),
    )(y, zx, w)


def _gate_bwd(name, dyn, y, zx, w):
    rows = y.shape[0]
    tm = _row_tile(rows, D_INNER)

    def body(d_ref, y_ref, z_ref, w_ref, dy_ref, dz_ref, dw_ref):
        i = pl.program_id(0)
        z, yv = z_ref[...], y_ref[...]
        sig = _sigmoid(z)
        sz = z * sig
        g = yv * sz
        r = lax.rsqrt(jnp.mean(g * g, axis=-1, keepdims=True) + RMS_EPS)
        ghat = g * r
        dn = d_ref[...]
        dghat = dn * w_ref[...]
        dg = r * (dghat - ghat * jnp.mean(dghat * ghat, axis=-1, keepdims=True))
        dy_ref[...] = dg * sz
        dz_ref[...] = (dg * yv * sig * (1.0 + z * (1.0 - sig))).astype(BF16)

        @pl.when(i == 0)
        def _():
            dw_ref[...] = jnp.zeros_like(dw_ref)

        dw_ref[...] += jnp.sum(dn * ghat, axis=0, keepdims=True)

    row_spec = pl.BlockSpec((tm, D_INNER), lambda i: (i, 0))
    vec_spec = pl.BlockSpec((1, D_INNER), lambda i: (0, 0))
    return pl.pallas_call(
        body, name=name,
        out_shape=(jax.ShapeDtypeStruct((rows, D_INNER), F32), jax.ShapeDtypeStruct((rows, D_INNER), BF16),
                   jax.ShapeDtypeStruct((1, D_INNER), F32)),
        grid=(rows // tm,), in_specs=[row_spec, row_spec, row_spec, vec_spec],
        out_specs=(row_spec, row_spec, vec_spec), compiler_params=_cparams(("arbitrary",)),
    )(dyn, y, zx, w)


def _ssd_consts():
    r = lax.broadcasted_iota(jnp.int32, (CHUNK, CHUNK), 0)
    c = lax.broadcasted_iota(jnp.int32, (CHUNK, CHUNK), 1)
    k512 = lax.broadcasted_iota(jnp.int32, (LANES, GROUP_W), 0)
    j512 = lax.broadcasted_iota(jnp.int32, (LANES, GROUP_W), 1)
    expand = (jnp.right_shift(j512, 6) == k512).astype(F32)
    k1024 = lax.broadcasted_iota(jnp.int32, (LANES, HEADS_PER_GROUP * LANES), 0)
    j1024 = lax.broadcasted_iota(jnp.int32, (LANES, HEADS_PER_GROUP * LANES), 1)
    bcast = (jnp.right_shift(j1024, 7) == k1024).astype(F32)
    return r >= c, r <= c, expand, bcast


def _reduce_heads_matrix():
    j = lax.broadcasted_iota(jnp.int32, (GROUP_W, LANES), 0)
    k = lax.broadcasted_iota(jnp.int32, (GROUP_W, LANES), 1)
    return (jnp.right_shift(j, 6) == k).astype(F32)


def _ssd_common(x_ref, b_ref, c_ref, dt_ref, a128_ref, aexp_ref):
    causal, causal_t, expand, bcast = _ssd_consts()
    x = x_ref[...]
    dt = dt_ref[...]
    acs_col = _dot_hi(causal.astype(F32), dt) * a128_ref[...]
    dt_exp = _dot_hi(dt, expand)
    acs_exp = _dot_hi(acs_col, expand)
    acs_bc = _dot_hi(acs_col, bcast)
    tot_exp = jnp.sum(dt_exp * aexp_ref[...], axis=0, keepdims=True)
    xdt = x * dt_exp
    e_exp = jnp.exp(acs_exp)
    f_exp = jnp.exp(tot_exp - acs_exp)
    bm, cm = b_ref[...], c_ref[...]
    return (causal, causal_t), expand, x, dt, acs_col, dt_exp, acs_bc, tot_exp, xdt, e_exp, f_exp, bm, cm


def _head_decay(acs_bc, h, causal):
    a_l = acs_bc[:, h * LANES:(h + 1) * LANES]
    seg = a_l - a_l.T
    dm = jnp.where(causal[0], jnp.exp(jnp.minimum(seg, 0.0)), 0.0)
    dmt = jnp.where(causal[1], jnp.exp(jnp.minimum(-seg, 0.0)), 0.0)
    return dm, dmt


def _ssd_fwd(name, xbc, dt4, a128, aexp, dskexp):
    rows = xbc.shape[0]
    nc = rows // CHUNK
    bcol = D_INNER // LANES

    def body(x_ref, b_ref, c_ref, dt_ref, a128_ref, aexp_ref, dsk_ref, y_ref, st_ref, s_scr):
        @pl.when(pl.program_id(1) == 0)
        def _():
            s_scr[...] = jnp.zeros_like(s_scr)

        causal, _, x, _, _, _, acs_bc, tot_exp, xdt, e_exp, f_exp, bm, cm = _ssd_common(
            x_ref, b_ref, c_ref, dt_ref, a128_ref, aexp_ref)
        state = s_scr[...]
        st_ref[...] = state
        cb16, bb16 = cm.astype(BF16), bm.astype(BF16)
        cb = _dot_nt(cb16, bb16)
        base = e_exp * _dot(cb16, state.astype(BF16)) + dsk_ref[...] * x
        lane = lax.broadcasted_iota(jnp.int32, (CHUNK, LANES), 1)
        for p in range(HEADS_PER_GROUP // 2):
            sl = slice(p * LANES, (p + 1) * LANES)
            xp = xdt[:, sl].astype(BF16)
            yd = []
            for e in range(2):
                dm, _ = _head_decay(acs_bc, 2 * p + e, causal)
                yd.append(_dot((cb * dm).astype(BF16), xp))
            y_ref[:, sl] = jnp.where(lane < HEAD_DIM, yd[0], yd[1]) + base[:, sl]
        s_scr[...] = jnp.exp(tot_exp) * state + _dot(bm.T.astype(BF16), (f_exp * xdt).astype(BF16))

    vec = lambda w: pl.BlockSpec((None, 1, w), lambda g, c: (g, 0, 0))
    return pl.pallas_call(
        body, name=name,
        out_shape=(jax.ShapeDtypeStruct((rows, D_INNER), F32),
                   jax.ShapeDtypeStruct((N_GROUPS, nc, D_STATE, GROUP_W), F32)),
        grid=(N_GROUPS, nc),
        in_specs=[pl.BlockSpec((CHUNK, GROUP_W), lambda g, c: (c, g)),
                  pl.BlockSpec((CHUNK, LANES), lambda g, c: (c, bcol + g)),
                  pl.BlockSpec((CHUNK, LANES), lambda g, c: (c, bcol + N_GROUPS + g)),
                  pl.BlockSpec((None, CHUNK, LANES), lambda g, c: (g, c, 0)),
                  vec(LANES), vec(GROUP_W), vec(GROUP_W)],
        out_specs=(pl.BlockSpec((CHUNK, GROUP_W), lambda g, c: (c, g)),
                   pl.BlockSpec((None, None, D_STATE, GROUP_W), lambda g, c: (g, c, 0, 0))),
        scratch_shapes=[pltpu.VMEM((D_STATE, GROUP_W), F32)],
        compiler_params=_cparams(("parallel", "arbitrary")),
    )(xbc, xbc, xbc, dt4, a128, aexp, dskexp)


def _ssd_bwd(name, xbc, dt4, a128, aexp, dskexp, dy, states):
    rows = xbc.shape[0]
    nc = rows // CHUNK
    bcol = D_INNER // LANES
    last = nc - 1

    def body(x_ref, b_ref, c_ref, dt_ref, a128_ref, aexp_ref, dsk_ref, dy_ref, st_ref,
             dx_ref, db_ref, dc_ref, ddt_ref, dalog_ref, ddsk_ref, ds_scr):
        first = pl.program_id(1) == 0

        @pl.when(first)
        def _():
            ds_scr[...] = jnp.zeros_like(ds_scr)
            dalog_ref[...] = jnp.zeros_like(dalog_ref)
            ddsk_ref[...] = jnp.zeros_like(ddsk_ref)

        causal, _, x, dt, _, dt_exp, acs_bc, tot_exp, xdt, e_exp, f_exp, bm, cm = _ssd_common(
            x_ref, b_ref, c_ref, dt_ref, a128_ref, aexp_ref)
        reduce_heads = _reduce_heads_matrix()
        state, dstate = st_ref[...], ds_scr[...]
        dyv = dy_ref[...]
        cb16, bb16 = cm.astype(BF16), bm.astype(BF16)
        s16, ds16 = state.astype(BF16), dstate.astype(BF16)
        cb = _dot_nt(cb16, bb16)
        cbt = _dot_nt(bb16, cb16)
        cs = _dot(cb16, s16)
        bds = _dot(bb16, ds16)
        edy = e_exp * dyv
        fx = f_exp * xdt
        dxdt_base = f_exp * bds
        dc_acc = _dot_nt(edy.astype(BF16), s16)
        db_acc = _dot_nt(fx.astype(BF16), ds16)
        ds_scr[...] = jnp.exp(tot_exp) * dstate + _dot(cm.T.astype(BF16), edy.astype(BF16))
        q = fx * bds
        qh = _dot_hi(q, reduce_heads)
        dacs = _dot_hi(edy * cs, reduce_heads) - qh
        dtot = jnp.sum(qh + _dot_hi(jnp.exp(tot_exp) * dstate * state, reduce_heads), axis=0, keepdims=True)
        ddsk_ref[...] += jnp.sum(_dot_hi(dyv * x, reduce_heads), axis=0, keepdims=True)
        lane = lax.broadcasted_iota(jnp.int32, (CHUNK, LANES), 1)
        dcb = jnp.zeros((CHUNK, CHUNK), F32)
        dcbt = jnp.zeros((CHUNK, CHUNK), F32)
        ddt_x = jnp.zeros((CHUNK, LANES), F32)
        for p in range(HEADS_PER_GROUP // 2):
            sl = slice(p * LANES, (p + 1) * LANES)
            xp, dyp = xdt[:, sl], dyv[:, sl]
            xp16, dyp16 = xp.astype(BF16), dyp.astype(BF16)
            dxh = []
            for e in range(2):
                h = 2 * p + e
                mine = (lane < HEAD_DIM) if e == 0 else (lane >= HEAD_DIM)
                dm, dmt = _head_decay(acs_bc, h, causal)
                m, mt = cb * dm, cbt * dmt
                xh16 = jnp.where(mine, xp, 0.0).astype(BF16)
                dyh16 = jnp.where(mine, dyp, 0.0).astype(BF16)
                d_m = _dot_nt(dyh16, xp16)
                d_mt = _dot_nt(xh16, dyp16)
                dacs_h = (jnp.sum(d_m * m, axis=-1, keepdims=True)
                          - jnp.sum(d_mt * mt, axis=-1, keepdims=True))
                dacs = dacs + jnp.where(lane == h, dacs_h, 0.0)
                dcb = dcb + d_m * dm
                dcbt = dcbt + d_mt * dmt
                dxh.append(_dot(mt.astype(BF16), dyp16))
            dxdt = jnp.where(lane < HEAD_DIM, dxh[0], dxh[1]) + dxdt_base[:, sl]
            dx_ref[:, sl] = dxdt * dt_exp[:, sl] + dsk_ref[:, sl] * dyp
            prod = dxdt * x[:, sl]
            for e in range(2):
                mine = (lane < HEAD_DIM) if e == 0 else (lane >= HEAD_DIM)
                col = jnp.sum(jnp.where(mine, prod, 0.0), axis=-1, keepdims=True)
                ddt_x = ddt_x + jnp.where(lane == 2 * p + e, col, 0.0)
        dc_ref[...] = dc_acc + _dot(dcb.astype(BF16), bb16)
        db_ref[...] = db_acc + _dot(dcbt.astype(BF16), cb16)
        row = lax.broadcasted_iota(jnp.int32, (CHUNK, LANES), 0)
        dacs = dacs + jnp.where(row == CHUNK - 1, dtot, 0.0)
        da = _dot_hi(causal[1].astype(F32), dacs)
        ddt_ref[...] = da * a128_ref[...] + ddt_x
        dalog_ref[...] += jnp.sum(da * dt, axis=0, keepdims=True) * a128_ref[...]

    vec = lambda w: pl.BlockSpec((None, 1, w), lambda g, c: (g, 0, 0))
    return pl.pallas_call(
        body, name=name,
        out_shape=(jax.ShapeDtypeStruct((rows, D_INNER), F32), jax.ShapeDtypeStruct((rows, D_BC), F32),
                   jax.ShapeDtypeStruct((rows, D_BC), F32), jax.ShapeDtypeStruct((N_GROUPS, rows, LANES), F32),
                   jax.ShapeDtypeStruct((N_GROUPS, 1, LANES), F32), jax.ShapeDtypeStruct((N_GROUPS, 1, LANES), F32)),
        grid=(N_GROUPS, nc),
        in_specs=[pl.BlockSpec((CHUNK, GROUP_W), lambda g, c: (last - c, g)),
                  pl.BlockSpec((CHUNK, LANES), lambda g, c: (last - c, bcol + g)),
                  pl.BlockSpec((CHUNK, LANES), lambda g, c: (last - c, bcol + N_GROUPS + g)),
                  pl.BlockSpec((None, CHUNK, LANES), lambda g, c: (g, last - c, 0)),
                  vec(LANES), vec(GROUP_W), vec(GROUP_W),
                  pl.BlockSpec((CHUNK, GROUP_W), lambda g, c: (last - c, g)),
                  pl.BlockSpec((None, None, D_STATE, GROUP_W), lambda g, c: (g, last - c, 0, 0))],
        out_specs=(pl.BlockSpec((CHUNK, GROUP_W), lambda g, c: (last - c, g)),
                   pl.BlockSpec((CHUNK, LANES), lambda g, c: (last - c, g)),
                   pl.BlockSpec((CHUNK, LANES), lambda g, c: (last - c, g)),
                   pl.BlockSpec((None, CHUNK, LANES), lambda g, c: (g, last - c, 0)),
                   vec(LANES), vec(LANES)),
        scratch_shapes=[pltpu.VMEM((D_STATE, GROUP_W), F32)],
        compiler_params=_cparams(("parallel", "arbitrary")),
    )(xbc, xbc, xbc, dt4, a128, aexp, dskexp, dy, states)


def _attn_visible(b):
    row = lax.broadcasted_iota(jnp.int32, (CHUNK, 3 * CHUNK), 0)
    col = lax.broadcasted_iota(jnp.int32, (CHUNK, 3 * CHUNK), 1)
    bb = b + jnp.zeros_like(col)
    meta = (col < CHUNK) & (bb >= 1) & (col >= PAD_ROWS)
    prev = (col >= CHUNK) & (col < 2 * CHUNK) & (bb >= 2) & ((col - CHUNK) > row)
    cur = (col >= 2 * CHUNK) & ((col - 2 * CHUNK) <= row) & ((bb >= 1) | ((col - 2 * CHUNK) >= PAD_ROWS))
    return meta | prev | cur


def _attn_probs(qm16, kc16, visible, sink):
    s = jnp.where(visible, _dot_nt(qm16, kc16), NEG_INF)
    m = jnp.maximum(jnp.max(s, axis=-1, keepdims=True), sink)
    pe = jnp.exp(s - m)
    pe_sink = jnp.exp(sink - m)
    inv = 1.0 / (jnp.sum(pe, axis=-1, keepdims=True) + pe_sink)
    return pe * inv, pe_sink * inv


def _attn_specs():
    blk = lambda f: pl.BlockSpec((CHUNK, 2 * D_KV), f)
    kv3 = [blk(lambda b: (0, 0)), blk(lambda b: (jnp.maximum(b - 1, 0), 0)), blk(lambda b: (b, 0))]
    return kv3


def _attn_fwd(name, q, k2, v2, sinks):
    rows = q.shape[0]

    def body(q_ref, k0, kp, kc, v0, vp, vc, sink_ref, o_ref):
        visible = _attn_visible(pl.program_id(0))
        lane = lax.broadcasted_iota(jnp.int32, (CHUNK, LANES), 1)
        for kvh in range(N_KV_HEADS):
            ksl = slice(kvh * LANES, (kvh + 1) * LANES)
            kcat = jnp.concatenate([k0[:, ksl], kp[:, ksl], kc[:, ksl]], axis=0).astype(BF16)
            vcat = jnp.concatenate([v0[:, ksl], vp[:, ksl], vc[:, ksl]], axis=0).astype(BF16)
            for pp in range(2):
                pair = kvh * 2 + pp
                qsl = slice(pair * LANES, (pair + 1) * LANES)
                qp = q_ref[:, qsl] * ATTN_SCALE
                outs = []
                for e in range(2):
                    mine = (lane < HEAD_DIM) if e == 0 else (lane >= HEAD_DIM)
                    qm16 = jnp.where(mine, qp, 0.0).astype(BF16)
                    pn, _ = _attn_probs(qm16, kcat, visible, sink_ref[2 * pair + e])
                    outs.append(_dot(pn.astype(BF16), vcat))
                o_ref[:, qsl] = jnp.where(lane < HEAD_DIM, outs[0], outs[1]).astype(BF16)

    return pl.pallas_call(
        body, name=name, out_shape=jax.ShapeDtypeStruct((rows, D_MODEL), BF16), grid=(rows // CHUNK,),
        in_specs=[pl.BlockSpec((CHUNK, D_MODEL), lambda b: (b, 0))] + _attn_specs() + _attn_specs()
        + [pl.BlockSpec(memory_space=pltpu.SMEM)],
        out_specs=pl.BlockSpec((CHUNK, D_MODEL), lambda b: (b, 0)), compiler_params=_cparams(("parallel",)),
    )(q, k2, k2, k2, v2, v2, v2, sinks)


def _attn_bwd(name, q, k2, v2, sinks, do):
    rows = q.shape[0]

    def body(q_ref, k0, kp, kc, v0, vp, vc, sink_ref, do_ref,
             dq_ref, dkc_ref, dkp_ref, dvc_ref, dvp_ref, dkm_ref, dvm_ref, dsink_ref):
        @pl.when(pl.program_id(0) == 0)
        def _():
            dkm_ref[...] = jnp.zeros_like(dkm_ref)
            dvm_ref[...] = jnp.zeros_like(dvm_ref)
            dsink_ref[...] = jnp.zeros_like(dsink_ref)

        visible = _attn_visible(pl.program_id(0))
        lane = lax.broadcasted_iota(jnp.int32, (CHUNK, LANES), 1)
        lane1 = lax.broadcasted_iota(jnp.int32, (1, LANES), 1)
        dsink = jnp.zeros((1, LANES), F32)
        for kvh in range(N_KV_HEADS):
            ksl = slice(kvh * LANES, (kvh + 1) * LANES)
            kcat = jnp.concatenate([k0[:, ksl], kp[:, ksl], kc[:, ksl]], axis=0).astype(BF16)
            vcat = jnp.concatenate([v0[:, ksl], vp[:, ksl], vc[:, ksl]], axis=0).astype(BF16)
            dk_acc = jnp.zeros((3 * CHUNK, LANES), F32)
            dv_acc = jnp.zeros((3 * CHUNK, LANES), F32)
            for pp in range(2):
                pair = kvh * 2 + pp
                qsl = slice(pair * LANES, (pair + 1) * LANES)
                qp = q_ref[:, qsl] * ATTN_SCALE
                dop = do_ref[:, qsl]
                dqs = []
                for e in range(2):
                    head = 2 * pair + e
                    mine = (lane < HEAD_DIM) if e == 0 else (lane >= HEAD_DIM)
                    qm16 = jnp.where(mine, qp, 0.0).astype(BF16)
                    dom16 = jnp.where(mine, dop, 0.0).astype(BF16)
                    pn, psink = _attn_probs(qm16, kcat, visible, sink_ref[head])
                    dp = _dot_nt(dom16, vcat)
                    delta = jnp.sum(pn * dp, axis=-1, keepdims=True)
                    ds = pn * (dp - delta)
                    dsink = dsink + jnp.where(lane1 == head, -jnp.sum(psink * delta, axis=0, keepdims=True), 0.0)
                    dqs.append(_dot(ds.astype(BF16), kcat))
                    dk_acc = dk_acc + _dot(ds.T.astype(BF16), qm16)
                    dv_acc = dv_acc + _dot(pn.T.astype(BF16), dom16)
                dq_ref[:, qsl] = (jnp.where(lane < HEAD_DIM, dqs[0], dqs[1]) * ATTN_SCALE).astype(BF16)
            dkm_ref[:, ksl] += dk_acc[0:CHUNK]
            dvm_ref[:, ksl] += dv_acc[0:CHUNK]
            dkp_ref[:, ksl] = dk_acc[CHUNK:2 * CHUNK]
            dvp_ref[:, ksl] = dv_acc[CHUNK:2 * CHUNK]
            dkc_ref[:, ksl] = dk_acc[2 * CHUNK:3 * CHUNK]
            dvc_ref[:, ksl] = dv_acc[2 * CHUNK:3 * CHUNK]
        dsink_ref[...] += dsink

    qspec = pl.BlockSpec((CHUNK, D_MODEL), lambda b: (b, 0))
    kvspec = pl.BlockSpec((CHUNK, 2 * D_KV), lambda b: (b, 0))
    fixed = pl.BlockSpec((CHUNK, 2 * D_KV), lambda b: (0, 0))
    kv_shape = jax.ShapeDtypeStruct((rows, 2 * D_KV), F32)
    meta_shape = jax.ShapeDtypeStruct((CHUNK, 2 * D_KV), F32)
    return pl.pallas_call(
        body, name=name,
        out_shape=(jax.ShapeDtypeStruct((rows, D_MODEL), BF16), kv_shape, kv_shape, kv_shape, kv_shape,
                   meta_shape, meta_shape, jax.ShapeDtypeStruct((1, LANES), F32)),
        grid=(rows // CHUNK,),
        in_specs=[qspec] + _attn_specs() + _attn_specs() + [pl.BlockSpec(memory_space=pltpu.SMEM), qspec],
        out_specs=(qspec, kvspec, kvspec, kvspec, kvspec, fixed, fixed, pl.BlockSpec((1, LANES), lambda b: (0, 0))),
        compiler_params=_cparams(("arbitrary",)),
    )(q, k2, k2, k2, v2, v2, v2, sinks, do)


def _kv_grad_combine(name, d_cur, d_prev, d_meta):
    rows = d_cur.shape[0]
    nb = rows // CHUNK

    def body(c_ref, p_ref, m_ref, o_ref):
        j = pl.program_id(0)
        jj = j + jnp.zeros((CHUNK, 1), jnp.int32)
        o_ref[...] = (c_ref[...] + jnp.where(jj < nb - 1, p_ref[...], 0.0) + jnp.where(jj == 0, m_ref[...], 0.0))

    blk = lambda f: pl.BlockSpec((CHUNK, 2 * D_KV), f)
    return pl.pallas_call(
        body, name=name, out_shape=jax.ShapeDtypeStruct((rows, 2 * D_KV), F32), grid=(nb,),
        in_specs=[blk(lambda j: (j, 0)), blk(lambda j: (jnp.minimum(j + 1, nb - 1), 0)), blk(lambda j: (0, 0))],
        out_specs=blk(lambda j: (j, 0)), compiler_params=_cparams(("parallel",)),
    )(d_cur, d_prev, d_meta)


def _loss_head(name, h, target):
    rows = h.shape[0]

    def body(h_ref, t_ref, dh_ref, loss_ref):
        i = pl.program_id(0)
        real = (i + jnp.zeros((CHUNK, 1), jnp.int32)) >= 1
        diff = jnp.where(real, h_ref[...] - t_ref[...], 0.0)
        dh_ref[...] = diff * (1.0 / D_MODEL)

        @pl.when(i == 0)
        def _():
            loss_ref[...] = jnp.zeros_like(loss_ref)

        loss_ref[...] += jnp.sum(diff * diff) * (0.5 / D_MODEL)

    blk = pl.BlockSpec((CHUNK, D_MODEL), lambda i: (i, 0))
    return pl.pallas_call(
        body, name=name,
        out_shape=(jax.ShapeDtypeStruct((rows, D_MODEL), F32), jax.ShapeDtypeStruct((1, LANES), F32)),
        grid=(rows // CHUNK,),
        in_specs=[blk, pl.BlockSpec((CHUNK, D_MODEL), lambda i: (jnp.maximum(i - 1, 0), 0))],
        out_specs=(blk, pl.BlockSpec((1, LANES), lambda i: (0, 0))), compiler_params=_cparams(("arbitrary",)),
    )(h, target)


def _adamw(name, w, g, m, v):
    rows, width = w.shape
    tr = rows
    for cand in range(8, rows + 1, 8):
        if rows % cand == 0 and cand * width * 4 <= (1 << 20):
            tr = cand

    def body(w_ref, g_ref, m_ref, v_ref, d_ref, mo_ref, vo_ref):
        gv = g_ref[...]
        mn = ADAM_B1 * m_ref[...] + (1.0 - ADAM_B1) * gv
        vn = ADAM_B2 * v_ref[...] + (1.0 - ADAM_B2) * (gv * gv)
        m_hat = mn / (1.0 - ADAM_B1 ** ADAM_STEP)
        v_hat = vn / (1.0 - ADAM_B2 ** ADAM_STEP)
        d_ref[...] = -ADAM_LR * (m_hat / (jnp.sqrt(v_hat) + ADAM_EPS) + ADAM_WD * w_ref[...])
        mo_ref[...] = mn
        vo_ref[...] = vn

    blk = pl.BlockSpec((tr, width), lambda i: (i, 0))
    shp = jax.ShapeDtypeStruct((rows, width), F32)
    return pl.pallas_call(
        body, name=name, out_shape=(shp, shp, shp), grid=(rows // tr,), in_specs=[blk] * 4, out_specs=(blk,) * 3,
        compiler_params=_cparams(("parallel",)),
    )(w, g, m, v)


def _ffn_fwd(tag, h, p, i):
    hn = _rms_fwd(f"ffn{tag}_norm", h, p["f_norm_pre"][i:i + 1])
    up = _mm(f"ffn{tag}_up", hn, p["f_w_up"][i], "nn")
    act = _ffn_conv_fwd(f"ffn{tag}_conv", up, p["f_conv_w"][i], p["f_conv_b"][i:i + 1])
    pre = _mm(f"ffn{tag}_down", act, p["f_w_down"][i], "nn")
    h_new = _resid_norm_fwd(f"ffn{tag}_resid", h, pre, p["f_norm_post"][i:i + 1])
    return h_new, (h, hn, up, act, pre)


def _ffn_bwd(tag, dh, saved, p, i):
    h, hn, up, act, pre = saved
    dpre, g_post = _resid_norm_bwd(f"ffn{tag}_resid_bwd", dh, pre, p["f_norm_post"][i:i + 1])
    dact = _mm(f"ffn{tag}_down_dx", dpre, p["f_w_down"][i], "nt")
    g_down = _mm(f"ffn{tag}_down_dw", act, dpre, "tn")
    dup, g_cw, g_cb = _ffn_conv_bwd(f"ffn{tag}_conv_bwd", up, dact, p["f_conv_w"][i], p["f_conv_b"][i:i + 1])
    dhn = _mm(f"ffn{tag}_up_dx", dup, p["f_w_up"][i], "nt")
    g_up = _mm(f"ffn{tag}_up_dw", hn, dup, "tn")
    dh_new, g_pre = _norm_bwd_add(f"ffn{tag}_norm_bwd", dh, dhn, h, p["f_norm_pre"][i:i + 1])
    return dh_new, dict(f_norm_post=g_post, f_w_down=g_down, f_conv_w=g_cw, f_conv_b=g_cb, f_w_up=g_up, f_norm_pre=g_pre)


def _lanes_pad(a, width=LANES):
    return jnp.pad(a, [(0, 0)] * (a.ndim - 1) + [(0, width - a.shape[-1])])


def _dup_heads(a):
    rows = a.shape[0]
    a = a.reshape(rows, N_KV_HEADS, 1, HEAD_DIM)
    return jnp.broadcast_to(a, (rows, N_KV_HEADS, 2, HEAD_DIM)).reshape(rows, 2 * D_KV)


def _undup_heads(a):
    rows = a.shape[0]
    return a.reshape(rows, N_KV_HEADS, 2, HEAD_DIM).sum(axis=2).reshape(rows, D_KV)


def _local_step(x2, target, p):
    seq = x2.shape[0]
    rows = seq + CHUNK
    g = {}

    h0 = jnp.concatenate([jnp.zeros((PAD_ROWS, D_MODEL), F32), p["meta_tokens"], x2], axis=0)

    w_main = p["a_w_in"][:, :D_MAIN]
    w_dt = _lanes_pad(p["a_w_in"][:, D_MAIN:])
    dt_bias = _lanes_pad(p["a_dt_bias"])
    a_neg = -jnp.exp(p["a_a_log"].reshape(N_GROUPS, HEADS_PER_GROUP))
    a128 = _lanes_pad(a_neg.reshape(N_GROUPS, 1, HEADS_PER_GROUP))
    aexp = jnp.repeat(a_neg, HEAD_DIM, axis=1).reshape(N_GROUPS, 1, GROUP_W)
    dskexp = jnp.repeat(p["a_d_skip"].reshape(N_GROUPS, HEADS_PER_GROUP), HEAD_DIM, axis=1).reshape(N_GROUPS, 1, GROUP_W)

    hn0 = _rms_fwd("a_norm", h0, p["a_norm_pre"])
    zx = _mm("a_in_main", hn0, w_main, "nn")
    dtr = _mm("a_in_dt", hn0, w_dt, "nn")
    xbc = _conv4_fwd("a_conv", zx, p["a_conv_w"], p["a_conv_b"])
    dt = _dt_fwd("a_dt", dtr, dt_bias)
    dt4 = _lanes_pad(dt[:, :SSM_HEADS].reshape(rows, N_GROUPS, HEADS_PER_GROUP).transpose(1, 0, 2))
    y, states = _ssd_fwd("a_ssd", xbc, dt4, a128, aexp, dskexp)
    yn = _gate_fwd("a_gate", y, zx, p["a_gate_norm"])
    mix = _mm("a_out", yn, p["a_w_out"], "nn")
    h1 = _resid_norm_fwd("a_resid", h0, mix, p["a_norm_post"])

    h2, ffn0 = _ffn_fwd("0", h1, p, 0)

    hkv = _rms_fwd("kv_norm", h2, p["kv_norm"])
    kv = _mm("kv_proj", hkv, p["w_kv"], "nn")
    k2, v2 = _dup_heads(kv[:, :D_KV]), _dup_heads(kv[:, D_KV:])
    hn2 = _rms_fwd("b_norm", h2, p["b_norm_pre"])
    q = _mm("b_q", hn2, p["b_w_q"], "nn")
    sinks = p["b_sinks"].reshape(N_Q_HEADS)
    o = _attn_fwd("b_attn", q, k2, v2, sinks)
    attn = _mm("b_o", o, p["b_w_o"], "nn")
    h3 = _resid_norm_fwd("b_resid", h2, attn, p["b_norm_post"])

    h4, ffn1 = _ffn_fwd("1", h3, p, 1)

    dh, loss_vec = _loss_head("loss", h4, target)
    loss = loss_vec[0, 0]

    dh, g1 = _ffn_bwd("1", dh, ffn1, p, 1)

    dpre, g["b_norm_post"] = _resid_norm_bwd("b_resid_bwd", dh, attn, p["b_norm_post"])
    do = _mm("b_o_dx", dpre, p["b_w_o"], "nt")
    g["b_w_o"] = _mm("b_o_dw", o, dpre, "tn")
    dq, dkc, dkp, dvc, dvp, dkm, dvm, dsink = _attn_bwd("b_attn_bwd", q, k2, v2, sinks, do)
    g["b_sinks"] = dsink[:, :N_Q_HEADS]
    dhn2 = _mm("b_q_dx", dq, p["b_w_q"], "nt")
    g["b_w_q"] = _mm("b_q_dw", hn2, dq, "tn")
    dh, g["b_norm_pre"] = _norm_bwd_add("b_norm_bwd", dh, dhn2, h2, p["b_norm_pre"])
    dk2 = _kv_grad_combine("k_grad", dkc, dkp, dkm)
    dv2 = _kv_grad_combine("v_grad", dvc, dvp, dvm)
    dkv = jnp.concatenate([_undup_heads(dk2), _undup_heads(dv2)], axis=1).astype(BF16)
    dhkv = _mm("kv_proj_dx", dkv, p["w_kv"], "nt")
    g["w_kv"] = _mm("kv_proj_dw", hkv, dkv, "tn")
    dh, g["kv_norm"] = _norm_bwd_add("kv_norm_bwd", dh, dhkv, h2, p["kv_norm"])

    dh, g0 = _ffn_bwd("0", dh, ffn0, p, 0)
    for name in g0:
        if name in ("f_w_up", "f_w_down"):
            g[name] = [g0[name], g1[name]]
        elif g0[name].shape[0] == 1:
            g[name] = jnp.concatenate([g0[name], g1[name]], axis=0)
        else:
            g[name] = jnp.stack([g0[name], g1[name]])

    dpre, g["a_norm_post"] = _resid_norm_bwd("a_resid_bwd", dh, mix, p["a_norm_post"])
    dyn = _mm("a_out_dx", dpre, p["a_w_out"], "nt")
    g["a_w_out"] = _mm("a_out_dw", yn, dpre, "tn")
    dy, dz, g["a_gate_norm"] = _gate_bwd("a_gate_bwd", dyn, y, zx, p["a_gate_norm"])
    dxs, dbm, dcm, ddt4, dalog, ddsk = _ssd_bwd("a_ssd_bwd", xbc, dt4, a128, aexp, dskexp, dy, states)
    g["a_a_log"] = dalog[:, 0, :HEADS_PER_GROUP].reshape(1, SSM_HEADS)
    g["a_d_skip"] = ddsk[:, 0, :HEADS_PER_GROUP].reshape(1, SSM_HEADS)
    ddt = _lanes_pad(ddt4[:, :, :HEADS_PER_GROUP].transpose(1, 0, 2).reshape(rows, SSM_HEADS))
    ddtr, dbias = _dt_bwd("a_dt_bwd", ddt, dtr, dt_bias)
    g["a_dt_bias"] = dbias[:, :SSM_HEADS]
    dxp, gw_x, gb_x = _conv4_bwd("a_conv_bwd_x", zx, dxs, p["a_conv_w"], p["a_conv_b"], 0)
    dbp, gw_b, gb_b = _conv4_bwd("a_conv_bwd_b", zx, dbm, p["a_conv_w"], p["a_conv_b"], D_INNER)
    dcp, gw_c, gb_c = _conv4_bwd("a_conv_bwd_c", zx, dcm, p["a_conv_w"], p["a_conv_b"], D_INNER + D_BC)
    g["a_conv_w"] = jnp.concatenate([gw_x, gw_b, gw_c], axis=1)
    g["a_conv_b"] = jnp.concatenate([gb_x, gb_b, gb_c], axis=1)
    dzx = jnp.concatenate([dz, dxp, dbp, dcp], axis=1)
    g_main = _mm("a_in_main_dw", hn0, dzx, "tn")
    g_dt = _mm("a_in_dt_dw", hn0, ddtr, "tn")
    g["a_w_in"] = jnp.concatenate([g_main, g_dt[:, :SSM_HEADS]], axis=1)
    dhn0 = _mm("a_in_main_dx", dzx, w_main, "nt")
    dhn0 = _mm("a_in_dt_dx", ddtr, w_dt, "nt", acc=dhn0)
    dh, g["a_norm_pre"] = _norm_bwd_add("a_norm_bwd", dh, dhn0, h0, p["a_norm_pre"])

    g["meta_tokens"] = dh[PAD_ROWS:CHUNK]
    return loss, dh[CHUNK:], g


def _place():
    return lax.axis_index("x"), lax.axis_index("y"), lax.axis_index("c")


def _other_chips(x, y):
    return [(1 - x, y), (x, 1 - y), (1 - x, 1 - y)]


ANY = pl.BlockSpec(memory_space=pl.ANY)
VMEM_SPEC = pl.BlockSpec(memory_space=pltpu.VMEM)


def _allgather_small(name, shard):
    rows = shard.shape[0]

    def body(s_ref, o_ref, send_sems, recv_sems):
        x, y, c = _place()
        me = 2 * x + y
        o_ref[me] = s_ref[...]
        chips = _other_chips(x, y)
        sends = [pltpu.make_async_remote_copy(s_ref, o_ref.at[me], send_sems.at[j], recv_sems.at[j],
                                              device_id=(cx, cy, c), device_id_type=MESH)
                 for j, (cx, cy) in enumerate(chips)]
        for cp in sends:
            cp.start()
        for j, (cx, cy) in enumerate(chips):
            pltpu.make_async_remote_copy(s_ref, o_ref.at[2 * cx + cy], send_sems.at[j], recv_sems.at[j],
                                         device_id=(cx, cy, c), device_id_type=MESH).wait_recv()
        for cp in sends:
            cp.wait_send()

    return pl.pallas_call(
        body, name=name, out_shape=jax.ShapeDtypeStruct((N_CHIPS, rows, LANES), F32),
        in_specs=[VMEM_SPEC], out_specs=VMEM_SPEC,
        scratch_shapes=[pltpu.SemaphoreType.DMA((3,)), pltpu.SemaphoreType.DMA((3,))],
        compiler_params=pltpu.CompilerParams(vmem_limit_bytes=VMEM_LIMIT),
    )(shard)


def _row_block(rows, width, itemsize, align, budget=2 << 20):
    best = None
    for cand in range(align, rows + 1, align):
        if rows % cand == 0 and cand * width * itemsize <= budget:
            best = cand
    assert best is not None, (rows, width)
    return best


def _cast_into_slot(name, chip, w2d):
    rows, width = w2d.shape
    tr = _row_block(rows, width, 4, 16)

    def body(chip_ref, w_ref, o_ref):
        o_ref[...] = w_ref[...].astype(BF16)

    return pl.pallas_call(
        body, name=name, out_shape=jax.ShapeDtypeStruct((N_CHIPS, rows, width), BF16),
        grid_spec=pltpu.PrefetchScalarGridSpec(
            num_scalar_prefetch=1, grid=(rows // tr,),
            in_specs=[pl.BlockSpec((tr, width), lambda i, chip_ref: (i, 0))],
            out_specs=pl.BlockSpec((None, tr, width), lambda i, chip_ref: (chip_ref[0], i, 0))),
        compiler_params=_cparams(("parallel",)),
    )(chip, w2d)


def _allgather_big(name, bufs):
    n = len(bufs)

    def body(*refs):
        outs = refs[n:2 * n]
        send_sems, recv_sems = refs[2 * n], refs[2 * n + 1]
        x, y, c = _place()
        me = 2 * x + y
        chips = _other_chips(x, y)
        sibling = (x, y, 1 - c)

        def rows_of(o, which):
            hr = o.shape[1] // 2
            return pl.ds(which * hr, hr)

        def ici(k, j, cx, cy, idx):
            o = outs[k]
            part = o.at[idx, rows_of(o, c)]
            return pltpu.make_async_remote_copy(part, part, send_sems.at[6 * k + j], recv_sems.at[6 * k + j],
                                                device_id=(cx, cy, c), device_id_type=MESH)

        def d2d(k, j, idx, which):
            o = outs[k]
            part = o.at[idx, rows_of(o, which)]
            return pltpu.make_async_remote_copy(part, part, send_sems.at[6 * k + 3 + j], recv_sems.at[6 * k + 3 + j],
                                                device_id=sibling, device_id_type=MESH)

        started = []
        for k in range(n):
            for j, (cx, cy) in enumerate(chips):
                cp = ici(k, j, cx, cy, me)
                cp.start()
                started.append(cp)
        for k in range(n):
            for j, (cx, cy) in enumerate(chips):
                ici(k, j, cx, cy, 2 * cx + cy).wait_recv()
                fwd = d2d(k, j, 2 * cx + cy, c)
                fwd.start()
                started.append(fwd)
        for k in range(n):
            for j, (cx, cy) in enumerate(chips):
                d2d(k, j, 2 * cx + cy, 1 - c).wait_recv()
        for cp in started:
            cp.wait_send()

    return pl.pallas_call(
        body, name=name, out_shape=[jax.ShapeDtypeStruct(b.shape, b.dtype) for b in bufs],
        in_specs=[ANY] * n, out_specs=[ANY] * n, input_output_aliases={k: k for k in range(n)},
        scratch_shapes=[pltpu.SemaphoreType.DMA((6 * n,)), pltpu.SemaphoreType.DMA((6 * n,))],
        compiler_params=pltpu.CompilerParams(vmem_limit_bytes=VMEM_LIMIT),
    )(*bufs)


def _allreduce_small(name, vec):
    rows = vec.shape[0]

    def body(v_ref, o_ref, buf, send_sems, recv_sems):
        x, y, c = _place()
        me = 4 * x + 2 * y + c
        buf[me] = v_ref[...]

        def peer(k):
            kx, ky, kc = (k >> 2) & 1, (k >> 1) & 1, k & 1
            return (1 - x if kx else x, 1 - y if ky else y, 1 - c if kc else c)

        sends = []
        for k in range(1, N_DEV):
            cp = pltpu.make_async_remote_copy(v_ref, buf.at[me], send_sems.at[k - 1], recv_sems.at[k - 1],
                                              device_id=peer(k), device_id_type=MESH)
            cp.start()
            sends.append(cp)
        for k in range(1, N_DEV):
            px, py, pc = peer(k)
            pltpu.make_async_remote_copy(v_ref, buf.at[4 * px + 2 * py + pc], send_sems.at[k - 1], recv_sems.at[k - 1],
                                         device_id=(px, py, pc), device_id_type=MESH).wait_recv()
        for cp in sends:
            cp.wait_send()
        acc = buf[0]
        for d in range(1, N_DEV):
            acc = acc + buf[d]
        o_ref[...] = acc

    return pl.pallas_call(
        body, name=name, out_shape=jax.ShapeDtypeStruct((rows, LANES), F32),
        in_specs=[VMEM_SPEC], out_specs=VMEM_SPEC,
        scratch_shapes=[pltpu.VMEM((N_DEV, rows, LANES), F32), pltpu.SemaphoreType.DMA((N_DEV - 1,)),
                        pltpu.SemaphoreType.DMA((N_DEV - 1,))],
        compiler_params=pltpu.CompilerParams(vmem_limit_bytes=VMEM_LIMIT),
    )(vec)


def _rs_pair_exchange(name, grads):
    n = len(grads)

    def body(*refs):
        ins, outs = refs[:n], refs[n:2 * n]
        send_sems, recv_sems = refs[2 * n], refs[2 * n + 1]
        x, y, c = _place()
        copies = []
        for k in range(n):
            hr = ins[k].shape[1] // 2
            cp = pltpu.make_async_remote_copy(ins[k].at[:, pl.ds((1 - c) * hr, hr)], outs[k], send_sems.at[k],
                                              recv_sems.at[k], device_id=(x, y, 1 - c), device_id_type=MESH)
            cp.start()
            copies.append(cp)
        for cp in copies:
            cp.wait()

    return pl.pallas_call(
        body, name=name,
        out_shape=[jax.ShapeDtypeStruct((N_CHIPS, g.shape[1] // 2, g.shape[2]), g.dtype) for g in grads],
        in_specs=[ANY] * n, out_specs=[ANY] * n,
        scratch_shapes=[pltpu.SemaphoreType.DMA((n,)), pltpu.SemaphoreType.DMA((n,))],
        compiler_params=pltpu.CompilerParams(vmem_limit_bytes=VMEM_LIMIT),
    )(*grads)


def _rs_pair_add(name, place, grads, partner):
    _, half_rows, width = partner.shape
    tr = _row_block(half_rows, width, 2, 16)
    nb = half_rows // tr

    def body(place_ref, g_ref, p_ref, o_ref):
        o_ref[...] = (g_ref[...].astype(F32) + p_ref[...].astype(F32)).astype(BF16)

    return pl.pallas_call(
        body, name=name, out_shape=jax.ShapeDtypeStruct(partner.shape, BF16),
        grid_spec=pltpu.PrefetchScalarGridSpec(
            num_scalar_prefetch=1, grid=(N_CHIPS, nb),
            in_specs=[pl.BlockSpec((None, tr, width), lambda s, i, pr: (s, pr[1] * nb + i, 0)),
                      pl.BlockSpec((None, tr, width), lambda s, i, pr: (s, i, 0))],
            out_specs=pl.BlockSpec((None, tr, width), lambda s, i, pr: (s, i, 0))),
        compiler_params=_cparams(("parallel", "parallel")),
    )(place, grads, partner)


def _rs_chip_exchange(name, partials):
    n = len(partials)

    def body(*refs):
        ins, outs = refs[:n], refs[n:2 * n]
        send_sems, recv_sems = refs[2 * n], refs[2 * n + 1]
        x, y, c = _place()
        chips = _other_chips(x, y)
        sends = []
        for k in range(n):
            for j, (cx, cy) in enumerate(chips):
                cp = pltpu.make_async_remote_copy(ins[k].at[2 * cx + cy], outs[k].at[j], send_sems.at[3 * k + j],
                                                  recv_sems.at[3 * k + j], device_id=(cx, cy, c), device_id_type=MESH)
                cp.start()
                sends.append(cp)
        for cp in sends:
            cp.wait()

    return pl.pallas_call(
        body, name=name,
        out_shape=[jax.ShapeDtypeStruct((3,) + q.shape[1:], q.dtype) for q in partials],
        in_specs=[ANY] * n, out_specs=[ANY] * n,
        scratch_shapes=[pltpu.SemaphoreType.DMA((3 * n,)), pltpu.SemaphoreType.DMA((3 * n,))],
        compiler_params=pltpu.CompilerParams(vmem_limit_bytes=VMEM_LIMIT),
    )(*partials)


def _rs_chip_add(name, place, mine, others):
    _, half_rows, width = mine.shape
    tr = _row_block(half_rows, width, 4, 16, budget=1 << 20)
    nb = half_rows // tr

    def body(place_ref, q_ref, r_ref, o_ref):
        acc = q_ref[...].astype(F32)
        for j in range(3):
            acc = acc + r_ref[j].astype(F32)
        o_ref[...] = acc

    return pl.pallas_call(
        body, name=name, out_shape=jax.ShapeDtypeStruct((2 * half_rows, width), F32),
        grid_spec=pltpu.PrefetchScalarGridSpec(
            num_scalar_prefetch=1, grid=(nb,),
            in_specs=[pl.BlockSpec((None, tr, width), lambda i, pr: (pr[0], i, 0)),
                      pl.BlockSpec((3, tr, width), lambda i, pr: (0, i, 0))],
            out_specs=pl.BlockSpec((tr, width), lambda i, pr: (pr[1] * nb + i, 0))),
        compiler_params=_cparams(("parallel",)),
    )(place, mine, others)


def _rs_pair_gather(name, shards):
    n = len(shards)

    def body(*refs):
        outs = refs[n:2 * n]
        send_sems, recv_sems = refs[2 * n], refs[2 * n + 1]
        x, y, c = _place()
        copies = []
        for k in range(n):
            hr = outs[k].shape[0] // 2
            part = outs[k].at[pl.ds(c * hr, hr)]
            cp = pltpu.make_async_remote_copy(part, part, send_sems.at[k], recv_sems.at[k],
                                              device_id=(x, y, 1 - c), device_id_type=MESH)
            cp.start()
            copies.append(cp)
        for k in range(n):
            hr = outs[k].shape[0] // 2
            theirs = outs[k].at[pl.ds((1 - c) * hr, hr)]
            pltpu.make_async_remote_copy(theirs, theirs, send_sems.at[k], recv_sems.at[k],
                                         device_id=(x, y, 1 - c), device_id_type=MESH).wait_recv()
        for cp in copies:
            cp.wait_send()

    return pl.pallas_call(
        body, name=name, out_shape=[jax.ShapeDtypeStruct(s.shape, s.dtype) for s in shards],
        in_specs=[ANY] * n, out_specs=[ANY] * n, input_output_aliases={k: k for k in range(n)},
        scratch_shapes=[pltpu.SemaphoreType.DMA((n,)), pltpu.SemaphoreType.DMA((n,))],
        compiler_params=pltpu.CompilerParams(vmem_limit_bytes=VMEM_LIMIT),
    )(*shards)


WEIGHTS = ["meta_tokens", "a_norm_pre", "a_w_in", "a_conv_w", "a_conv_b", "a_dt_bias", "a_a_log", "a_d_skip",
           "a_gate_norm", "a_w_out", "a_norm_post", "kv_norm", "w_kv", "b_norm_pre", "b_w_q", "b_sinks", "b_w_o",
           "b_norm_post", "f_norm_pre", "f_w_up", "f_conv_w", "f_conv_b", "f_w_down", "f_norm_post"]
FULL_SHAPE = {
    "meta_tokens": (16, 1024), "a_norm_pre": (1, 1024), "a_w_in": (1, 1024, 5152), "a_conv_w": (1, 4, 3072),
    "a_conv_b": (1, 3072), "a_dt_bias": (1, 32), "a_a_log": (1, 32), "a_d_skip": (1, 32), "a_gate_norm": (1, 2048),
    "a_w_out": (1, 2048, 1024), "a_norm_post": (1, 1024), "kv_norm": (1024,), "w_kv": (1024, 512),
    "b_norm_pre": (1, 1024), "b_w_q": (1, 1024, 1024), "b_sinks": (1, 16), "b_w_o": (1, 1024, 1024),
    "b_norm_post": (1, 1024), "f_norm_pre": (2, 1024), "f_w_up": (2, 1024, 5632), "f_conv_w": (2, 3, 5632),
    "f_conv_b": (2, 5632), "f_w_down": (2, 2816, 1024), "f_norm_post": (2, 1024),
}
SHARD_AXIS = {
    "meta_tokens": 1, "a_norm_pre": 1, "a_w_in": 2, "a_conv_w": 2, "a_conv_b": 1, "a_dt_bias": None, "a_a_log": None,
    "a_d_skip": None, "a_gate_norm": 1, "a_w_out": 1, "a_norm_post": 1, "kv_norm": None, "w_kv": 0, "b_norm_pre": None,
    "b_w_q": 1, "b_sinks": None, "b_w_o": 1, "b_norm_post": None, "f_norm_pre": None, "f_w_up": 2, "f_conv_w": 2,
    "f_conv_b": None, "f_w_down": 1, "f_norm_post": None,
}
BIG = ["a_w_in", "a_w_out", "w_kv", "b_w_q", "b_w_o", "f_w_up", "f_w_down"]
SMALL = [n for n in WEIGHTS if n not in BIG]
SMALL_SHARDED = [n for n in SMALL if SHARD_AXIS[n] is not None]


def _shard_shape(name):
    shape = list(FULL_SHAPE[name])
    if SHARD_AXIS[name] is not None:
        shape[SHARD_AXIS[name]] //= N_CHIPS
    return tuple(shape)


def _numel(shape):
    return int(math.prod(shape))


SUBLANES = 8


def _packed_rows(shape):
    rows = -(-_numel(shape) // LANES)
    return -(-rows // SUBLANES) * SUBLANES


def _pack(arrays):
    parts = []
    for a in arrays:
        size, rows = _numel(a.shape), _packed_rows(a.shape)
        if size % LANES == 0:
            part = jnp.pad(a.reshape(size // LANES, LANES), ((0, rows - size // LANES), (0, 0)))
        else:
            part = jnp.pad(a.reshape(-1), (0, rows * LANES - size)).reshape(rows, LANES)
        parts.append(part)
    return jnp.concatenate(parts, axis=0)


def _unpack(packed, names, shape_of):
    out, off = {}, 0
    lead = packed.shape[:-2]
    for n in names:
        shape = tuple(shape_of(n))
        size, rows = _numel(shape), _packed_rows(shape)
        part = packed[..., off:off + rows, :]
        if size % LANES == 0:
            out[n] = part[..., :size // LANES, :].reshape(lead + shape)
        else:
            out[n] = part.reshape(lead + (rows * LANES,))[..., :size].reshape(lead + shape)
        off += rows
    return out


def _split_chips(name, full):
    ax = SHARD_AXIS[name]
    shape = full.shape
    cut = shape[:ax] + (N_CHIPS, shape[ax] // N_CHIPS) + shape[ax + 1:]
    return jnp.moveaxis(full.reshape(cut), ax, 0)


def _join_chips(name, stacked):
    ax = SHARD_AXIS[name]
    moved = jnp.moveaxis(stacked, 0, ax)
    shape = moved.shape
    return moved.reshape(shape[:ax] + (shape[ax] * shape[ax + 1],) + shape[ax + 2:])


def _as2d(a):
    return a.reshape(-1, a.shape[-1])


def _full_from_gathered(name, buf):
    stacked = buf.reshape((N_CHIPS,) + _shard_shape(name))
    if name in ("f_w_up", "f_w_down"):
        return [_join_chips(name, stacked[:, i:i + 1])[0] for i in range(2)]
    joined = _join_chips(name, stacked)
    return joined[0] if joined.ndim == 3 else joined


def _gathered_from_full(name, g):
    if name in ("f_w_up", "f_w_down"):
        parts = [_split_chips(name, gi[None]) for gi in g]
        stacked = jnp.concatenate(parts, axis=1)
    else:
        stacked = _split_chips(name, g.reshape(FULL_SHAPE[name]))
    return stacked.reshape((N_CHIPS,) + _shard2d(name)).astype(BF16)


def _shard2d(name):
    shape = _shard_shape(name)
    return (_numel(shape[:-1]), shape[-1])


def kernel(x, meta_tokens, a_norm_pre, a_w_in, a_conv_w, a_conv_b, a_dt_bias, a_a_log, a_d_skip, a_gate_norm, a_w_out, a_norm_post, kv_norm, w_kv, b_norm_pre, b_w_q, b_sinks, b_w_o, b_norm_post, f_norm_pre, f_w_up, f_conv_w, f_conv_b, f_w_down, f_norm_post, loss_target, m_meta_tokens, m_a_norm_pre, m_a_w_in, m_a_conv_w, m_a_conv_b, m_a_dt_bias, m_a_a_log, m_a_d_skip, m_a_gate_norm, m_a_w_out, m_a_norm_post, m_kv_norm, m_w_kv, m_b_norm_pre, m_b_w_q, m_b_sinks, m_b_w_o, m_b_norm_post, m_f_norm_pre, m_f_w_up, m_f_conv_w, m_f_conv_b, m_f_w_down, m_f_norm_post, v_meta_tokens, v_a_norm_pre, v_a_w_in, v_a_conv_w, v_a_conv_b, v_a_dt_bias, v_a_a_log, v_a_d_skip, v_a_gate_norm, v_a_w_out, v_a_norm_post, v_kv_norm, v_w_kv, v_b_norm_pre, v_b_w_q, v_b_sinks, v_b_w_o, v_b_norm_post, v_f_norm_pre, v_f_w_up, v_f_conv_w, v_f_conv_b, v_f_w_down, v_f_norm_post):
    given = dict(locals())
    w = {n: given[n] for n in WEIGHTS}
    mom = {n: given["m_" + n] for n in WEIGHTS}
    var = {n: given["v_" + n] for n in WEIGHTS}
    chip = 2 * lax.axis_index("x") + lax.axis_index("y")
    core = lax.axis_index("c")
    place = jnp.stack([chip, core]).astype(jnp.int32)

    small_all = _allgather_small("gather_small", _pack([w[n] for n in SMALL_SHARDED]))
    small_parts = _unpack(small_all, SMALL_SHARDED, _shard_shape)
    slots = [_cast_into_slot("cast_" + n, place, _as2d(w[n])) for n in BIG]
    gathered = _allgather_big("gather_big", slots)
    p = {}
    for n in WEIGHTS:
        if n in BIG:
            p[n] = _full_from_gathered(n, gathered[BIG.index(n)])
        elif n in SMALL_SHARDED:
            p[n] = _join_chips(n, small_parts[n])
        else:
            p[n] = w[n]
    p["a_conv_w"] = p["a_conv_w"][0]
    p["kv_norm"] = p["kv_norm"].reshape(1, D_MODEL)

    loss_local, grad_x, g = _local_step(x[0], loss_target[0], p)
    loss = lax.psum(loss_local, ("x", "y", "c"))

    small_sum = _allreduce_small("reduce_small", _pack([g[n].reshape(FULL_SHAPE[n]) for n in SMALL]))
    small_red = _unpack(small_sum, SMALL, lambda n: FULL_SHAPE[n])
    grads = {}
    for n in SMALL:
        if SHARD_AXIS[n] is None:
            grads[n] = small_red[n]
        else:
            grads[n] = lax.dynamic_index_in_dim(_split_chips(n, small_red[n]), chip, 0, keepdims=False)

    by_chip = [_gathered_from_full(n, g[n]) for n in BIG]
    partner = _rs_pair_exchange("reduce_pair_send", by_chip)
    pair_sum = [_rs_pair_add("reduce_pair_add_" + n, place, gk, pk) for n, gk, pk in zip(BIG, by_chip, partner)]
    from_chips = _rs_chip_exchange("reduce_chip_send", pair_sum)
    half_sum = [_rs_chip_add("reduce_chip_add_" + n, place, qk, rk) for n, qk, rk in zip(BIG, pair_sum, from_chips)]
    shard_sum = _rs_pair_gather("reduce_pair_gather", half_sum)
    for n, s in zip(BIG, shard_sum):
        grads[n] = s.reshape(_shard_shape(n))

    delta, new_m, new_v = {}, {}, {}
    for n in BIG:
        shape = _shard_shape(n)
        d, m2, v2 = _adamw("adamw_" + n, _as2d(w[n]), _as2d(grads[n]), _as2d(mom[n]), _as2d(var[n]))
        delta[n], new_m[n], new_v[n] = d.reshape(shape), m2.reshape(shape), v2.reshape(shape)
    packed = [_pack([src[n].reshape(_shard_shape(n)) for n in SMALL]) for src in (w, grads, mom, var)]
    outs = _adamw("adamw_small", *packed)
    for dst, flat in zip((delta, new_m, new_v), outs):
        dst.update(_unpack(flat, SMALL, _shard_shape))

    return (loss, grad_x[None], *[grads[n].reshape(_shard_shape(n)) for n in WEIGHTS],
            *[delta[n] for n in WEIGHTS], *[new_m[n] for n in WEIGHTS], *[new_v[n] for n in WEIGHTS])
```

```python
import functools
import math

import jax
import jax.numpy as jnp
from jax import lax
from jax.experimental import pallas as pl
from jax.experimental.pallas import tpu as pltpu

F32, BF16 = jnp.float32, jnp.bfloat16
MESH = pl.DeviceIdType.MESH

D_MODEL = 1024
N_META = 16
CHUNK = 128
PAD_ROWS = CHUNK - N_META
D_INNER = 2048
D_STATE = 128
N_GROUPS = 4
HEADS_PER_GROUP = 8
SSM_HEADS = 32
HEAD_DIM = 64
D_BC = N_GROUPS * D_STATE
D_XBC = D_INNER + 2 * D_BC
D_MAIN = D_INNER + D_XBC
D_IN_PROJ = D_MAIN + SSM_HEADS
GROUP_W = HEADS_PER_GROUP * HEAD_DIM
SSM_CONV = 4
D_FF = 2816
FFN_CONV = 3
N_Q_HEADS = 16
N_KV_HEADS = 4
D_KV = 256
ATTN_SCALE = 1.0 / math.sqrt(HEAD_DIM)
RMS_EPS = 1e-6
NEG_INF = -1e30
LANES = 128
VMEM_LIMIT = 48 * 1024 * 1024

ADAM_LR, ADAM_B1, ADAM_B2, ADAM_EPS, ADAM_WD, ADAM_STEP = 0.001, 0.9, 0.999, 1e-08, 0.01, 10

N_CHIPS = 4
N_DEV = 8


def _cparams(sem=None):
    return pltpu.CompilerParams(dimension_semantics=sem, vmem_limit_bytes=VMEM_LIMIT)


def _tile(n, cands=(512, 256, 128)):
    for t in cands:
        if n % t == 0:
            return t
    return n


def _row_tile(rows, width):
    for t in (544, 272):
        if rows % t == 0 and t * width * 4 <= (3 << 20):
            return t
    return 128


def _rows_mask(i, tm):
    rows = i * tm + lax.broadcasted_iota(jnp.int32, (tm, 1), 0)
    return rows >= PAD_ROWS


def _dot(a, b):
    return jnp.dot(a, b, preferred_element_type=F32)


def _dot_nt(a, b):
    return lax.dot_general(a, b, (((1,), (1,)), ((), ())), preferred_element_type=F32)


def _dot_tn(a, b):
    return lax.dot_general(a, b, (((0,), (0,)), ((), ())), preferred_element_type=F32)


def _sigmoid(x):
    return 1.0 / (1.0 + jnp.exp(-x))


def _mm(name, a, b, mode, out_dtype=F32, acc=None, b_colblock=0):
    resident_bytes = 8 << 20
    if mode == "nn":
        m, k = a.shape
        n = b.shape[1]
        tm = m
        while tm * k * 2 > resident_bytes and tm % 32 == 0:
            tm //= 2
        tn = _tile(n)
        grid = (m // tm, n // tn)
        in_specs = [pl.BlockSpec((tm, k), lambda i, j: (i, 0)), pl.BlockSpec((k, tn), lambda i, j: (0, j))]
        out_shape, out_block = (m, n), (tm, tn)
    elif mode == "nt":
        m, n = a.shape
        k = b.shape[0]
        tm = m
        while tm * n * 2 > resident_bytes and tm % 32 == 0:
            tm //= 2
        tk = _tile(k)
        grid = (m // tm, k // tk)
        in_specs = [pl.BlockSpec((tm, n), lambda i, j: (i, 0)), pl.BlockSpec((tk, n), lambda i, j: (j, b_colblock))]
        out_shape, out_block = (m, k), (tm, tk)
    else:
        m, k = a.shape
        n = b.shape[1]
        tk, tn = _tile(k), _tile(n)
        grid = (k // tk, n // tn)
        in_specs = [pl.BlockSpec((m, tk), lambda i, j: (0, i)), pl.BlockSpec((m, tn), lambda i, j: (0, j))]
        out_shape, out_block = (k, n), (tk, tn)
    out_spec = pl.BlockSpec(out_block, lambda i, j: (i, j))
    has_acc = acc is not None

    def body(*refs):
        a_ref, b_ref = refs[0], refs[1]
        o_ref = refs[-1]
        av, bv = a_ref[...], b_ref[...]
        if mode == "nn":
            r = _dot(av, bv)
        elif mode == "nt":
            r = _dot_nt(av, bv)
        else:
            r = _dot_tn(av, bv)
        if has_acc:
            r = r + refs[2][...]
        o_ref[...] = r.astype(o_ref.dtype)

    operands = [a, b]
    if has_acc:
        in_specs = in_specs + [out_spec]
        operands.append(acc)
    return pl.pallas_call(
        body, name=name, out_shape=jax.ShapeDtypeStruct(out_shape, out_dtype), grid=grid,
        in_specs=in_specs, out_specs=out_spec, compiler_params=_cparams(("parallel", "parallel")),
    )(*operands)


def _rms_fwd(name, h, w):
    rows, width = h.shape
    tm = _row_tile(rows, width)

    def body(h_ref, w_ref, o_ref):
        x = h_ref[...]
        r = lax.rsqrt(jnp.mean(x * x, axis=-1, keepdims=True) + RMS_EPS)
        o_ref[...] = (x * r * w_ref[...]).astype(BF16)

    return pl.pallas_call(
        body, name=name, out_shape=jax.ShapeDtypeStruct((rows, width), BF16), grid=(rows // tm,),
        in_specs=[pl.BlockSpec((tm, width), lambda i: (i, 0)), pl.BlockSpec((1, width), lambda i: (0, 0))],
        out_specs=pl.BlockSpec((tm, width), lambda i: (i, 0)), compiler_params=_cparams(("parallel",)),
    )(h, w)


def _resid_norm_fwd(name, h, pre, w):
    rows, width = h.shape
    tm = _row_tile(rows, width)

    def body(h_ref, p_ref, w_ref, o_ref):
        p = p_ref[...]
        r = lax.rsqrt(jnp.mean(p * p, axis=-1, keepdims=True) + RMS_EPS)
        o_ref[...] = h_ref[...] + jnp.where(_rows_mask(pl.program_id(0), tm), p * r * w_ref[...], 0.0)

    row_spec = pl.BlockSpec((tm, width), lambda i: (i, 0))
    return pl.pallas_call(
        body, name=name, out_shape=jax.ShapeDtypeStruct((rows, width), F32), grid=(rows // tm,),
        in_specs=[row_spec, row_spec, pl.BlockSpec((1, width), lambda i: (0, 0))],
        out_specs=row_spec, compiler_params=_cparams(("parallel",)),
    )(h, pre, w)


def _resid_norm_bwd(name, dh, pre, w):
    rows, width = dh.shape
    tm = _row_tile(rows, width)

    def body(dh_ref, p_ref, w_ref, dp_ref, dw_ref):
        i = pl.program_id(0)
        dy = jnp.where(_rows_mask(i, tm), dh_ref[...], 0.0)
        p = p_ref[...]
        r = lax.rsqrt(jnp.mean(p * p, axis=-1, keepdims=True) + RMS_EPS)
        xhat = p * r
        dxhat = dy * w_ref[...]
        dp = r * (dxhat - xhat * jnp.mean(dxhat * xhat, axis=-1, keepdims=True))
        dp_ref[...] = dp.astype(BF16)

        @pl.when(i == 0)
        def _():
            dw_ref[...] = jnp.zeros_like(dw_ref)

        dw_ref[...] += jnp.sum(dy * xhat, axis=0, keepdims=True)

    row_spec = pl.BlockSpec((tm, width), lambda i: (i, 0))
    vec_spec = pl.BlockSpec((1, width), lambda i: (0, 0))
    return pl.pallas_call(
        body, name=name,
        out_shape=(jax.ShapeDtypeStruct((rows, width), BF16), jax.ShapeDtypeStruct((1, width), F32)),
        grid=(rows // tm,), in_specs=[row_spec, row_spec, vec_spec], out_specs=(row_spec, vec_spec),
        compiler_params=_cparams(("arbitrary",)),
    )(dh, pre, w)


def _norm_bwd_add(name, dh, dhn, h, w):
    rows, width = dh.shape
    tm = _row_tile(rows, width)

    def body(dh_ref, dhn_ref, h_ref, w_ref, o_ref, dw_ref):
        i = pl.program_id(0)
        x = h_ref[...]
        dy = dhn_ref[...]
        r = lax.rsqrt(jnp.mean(x * x, axis=-1, keepdims=True) + RMS_EPS)
        xhat = x * r
        dxhat = dy * w_ref[...]
        dx = r * (dxhat - xhat * jnp.mean(dxhat * xhat, axis=-1, keepdims=True))
        o_ref[...] = dh_ref[...] + jnp.where(_rows_mask(i, tm), dx, 0.0)

        @pl.when(i == 0)
        def _():
            dw_ref[...] = jnp.zeros_like(dw_ref)

        dw_ref[...] += jnp.sum(dy * xhat, axis=0, keepdims=True)

    row_spec = pl.BlockSpec((tm, width), lambda i: (i, 0))
    vec_spec = pl.BlockSpec((1, width), lambda i: (0, 0))
    return pl.pallas_call(
        body, name=name,
        out_shape=(jax.ShapeDtypeStruct((rows, width), F32), jax.ShapeDtypeStruct((1, width), F32)),
        grid=(rows // tm,), in_specs=[row_spec, row_spec, row_spec, vec_spec], out_specs=(row_spec, vec_spec),
        compiler_params=_cparams(("arbitrary",)),
    )(dh, dhn, h, w)


def _shift_down(x, s, rows):
    return pltpu.roll(x, s, 0) if s else x


def _shift_up(x, s, rows):
    return pltpu.roll(x, rows - s, 0) if s else x


def _conv4_fwd(name, zx, cw, cb):
    rows = zx.shape[0]
    off = D_INNER // LANES

    def body(x_ref, w_ref, b_ref, o_ref):
        x = x_ref[...]
        acc = b_ref[...] + w_ref[pl.ds(SSM_CONV - 1, 1), :] * x
        for s in range(1, SSM_CONV):
            acc = acc + w_ref[pl.ds(SSM_CONV - 1 - s, 1), :] * _shift_down(x, s, rows)
        valid = lax.broadcasted_iota(jnp.int32, (rows, 1), 0) >= PAD_ROWS
        o_ref[...] = jnp.where(valid, acc * _sigmoid(acc), 0.0)

    return pl.pallas_call(
        body, name=name, out_shape=jax.ShapeDtypeStruct((rows, D_XBC), F32), grid=(D_XBC // LANES,),
        in_specs=[pl.BlockSpec((rows, LANES), lambda j: (0, j + off)),
                  pl.BlockSpec((SSM_CONV, LANES), lambda j: (0, j)),
                  pl.BlockSpec((1, LANES), lambda j: (0, j))],
        out_specs=pl.BlockSpec((rows, LANES), lambda j: (0, j)), compiler_params=_cparams(("parallel",)),
    )(zx, cw, cb)


def _conv4_bwd(name, zx, dout, cw, cb, col0):
    rows, width = dout.shape
    zoff = (D_INNER + col0) // LANES
    woff = col0 // LANES

    def body(x_ref, d_ref, w_ref, b_ref, dx_ref, dw_ref, db_ref):
        x = x_ref[...]
        shifted = [_shift_down(x, s, rows) for s in range(SSM_CONV)]
        acc = b_ref[...]
        for s in range(SSM_CONV):
            acc = acc + w_ref[pl.ds(SSM_CONV - 1 - s, 1), :] * shifted[s]
        sig = _sigmoid(acc)
        valid = lax.broadcasted_iota(jnp.int32, (rows, 1), 0) >= PAD_ROWS
        dpre = jnp.where(valid, d_ref[...] * sig * (1.0 + acc * (1.0 - sig)), 0.0)
        dx = w_ref[pl.ds(SSM_CONV - 1, 1), :] * dpre
        for s in range(1, SSM_CONV):
            dx = dx + w_ref[pl.ds(SSM_CONV - 1 - s, 1), :] * _shift_up(dpre, s, rows)
        dx_ref[...] = dx.astype(BF16)
        for s in range(SSM_CONV):
            dw_ref[pl.ds(SSM_CONV - 1 - s, 1), :] = jnp.sum(dpre * shifted[s], axis=0, keepdims=True)
        db_ref[...] = jnp.sum(dpre, axis=0, keepdims=True)

    return pl.pallas_call(
        body, name=name,
        out_shape=(jax.ShapeDtypeStruct((rows, width), BF16), jax.ShapeDtypeStruct((SSM_CONV, width), F32),
                   jax.ShapeDtypeStruct((1, width), F32)),
        grid=(width // LANES,),
        in_specs=[pl.BlockSpec((rows, LANES), lambda j: (0, j + zoff)),
                  pl.BlockSpec((rows, LANES), lambda j: (0, j)),
                  pl.BlockSpec((SSM_CONV, LANES), lambda j: (0, j + woff)),
                  pl.BlockSpec((1, LANES), lambda j: (0, j + woff))],
        out_specs=(pl.BlockSpec((rows, LANES), lambda j: (0, j)),
                   pl.BlockSpec((SSM_CONV, LANES), lambda j: (0, j)),
                   pl.BlockSpec((1, LANES), lambda j: (0, j))),
        compiler_params=_cparams(("parallel",)),
    )(zx, dout, cw, cb)


def _ffn_conv_fwd(name, up, cw, cb):
    rows = up.shape[0]
    nt = D_FF // LANES

    def body(g_ref, v_ref, wg_ref, wv_ref, bg_ref, bv_ref, o_ref):
        g, v = g_ref[...], v_ref[...]
        ug, uv = bg_ref[...], bv_ref[...]
        for s in range(FFN_CONV):
            ug = ug + wg_ref[pl.ds(FFN_CONV - 1 - s, 1), :] * _shift_down(g, s, rows)
            uv = uv + wv_ref[pl.ds(FFN_CONV - 1 - s, 1), :] * _shift_down(v, s, rows)
        valid = lax.broadcasted_iota(jnp.int32, (rows, 1), 0) >= PAD_ROWS
        o_ref[...] = jnp.where(valid, ug * _sigmoid(ug) * uv, 0.0).astype(BF16)

    col = lambda shift: pl.BlockSpec((rows, LANES), lambda j: (0, j + shift))
    wsp = lambda shift: pl.BlockSpec((FFN_CONV, LANES), lambda j: (0, j + shift))
    bsp = lambda shift: pl.BlockSpec((1, LANES), lambda j: (0, j + shift))
    return pl.pallas_call(
        body, name=name, out_shape=jax.ShapeDtypeStruct((rows, D_FF), BF16), grid=(nt,),
        in_specs=[col(0), col(nt), wsp(0), wsp(nt), bsp(0), bsp(nt)],
        out_specs=pl.BlockSpec((rows, LANES), lambda j: (0, j)), compiler_params=_cparams(("parallel",)),
    )(up, up, cw, cw, cb, cb)


def _ffn_conv_bwd(name, up, dact, cw, cb):
    rows = up.shape[0]
    nt = D_FF // LANES

    def body(g_ref, v_ref, d_ref, wg_ref, wv_ref, bg_ref, bv_ref, dxg_ref, dxv_ref, dwg_ref, dwv_ref, dbg_ref, dbv_ref):
        g, v = g_ref[...], v_ref[...]
        gs = [_shift_down(g, s, rows) for s in range(FFN_CONV)]
        vs = [_shift_down(v, s, rows) for s in range(FFN_CONV)]
        ug, uv = bg_ref[...], bv_ref[...]
        for s in range(FFN_CONV):
            ug = ug + wg_ref[pl.ds(FFN_CONV - 1 - s, 1), :] * gs[s]
            uv = uv + wv_ref[pl.ds(FFN_CONV - 1 - s, 1), :] * vs[s]
        sig = _sigmoid(ug)
        valid = lax.broadcasted_iota(jnp.int32, (rows, 1), 0) >= PAD_ROWS
        d = jnp.where(valid, d_ref[...], 0.0)
        dsig = d * sig
        for dpre, src, w_ref, dx_ref, dw_ref, db_ref in (
                (dsig * uv * (1.0 + ug * (1.0 - sig)), gs, wg_ref, dxg_ref, dwg_ref, dbg_ref),
                (dsig * ug, vs, wv_ref, dxv_ref, dwv_ref, dbv_ref)):
            dx = w_ref[pl.ds(FFN_CONV - 1, 1), :] * dpre
            for s in range(1, FFN_CONV):
                dx = dx + w_ref[pl.ds(FFN_CONV - 1 - s, 1), :] * _shift_up(dpre, s, rows)
            dx_ref[...] = dx.astype(BF16)
            for s in range(FFN_CONV):
                dw_ref[pl.ds(FFN_CONV - 1 - s, 1), :] = jnp.sum(dpre * src[s], axis=0, keepdims=True)
            db_ref[...] = jnp.sum(dpre, axis=0, keepdims=True)

    col = lambda shift: pl.BlockSpec((rows, LANES), lambda j: (0, j + shift))
    wsp = lambda shift: pl.BlockSpec((FFN_CONV, LANES), lambda j: (0, j + shift))
    bsp = lambda shift: pl.BlockSpec((1, LANES), lambda j: (0, j + shift))
    dx_shape = jax.ShapeDtypeStruct((rows, D_FF), BF16)
    dw_shape = jax.ShapeDtypeStruct((FFN_CONV, D_FF), F32)
    db_shape = jax.ShapeDtypeStruct((1, D_FF), F32)
    return pl.pallas_call(
        body, name=name, out_shape=(dx_shape, dx_shape, dw_shape, dw_shape, db_shape, db_shape), grid=(nt,),
        in_specs=[col(0), col(nt), col(0), wsp(0), wsp(nt), bsp(0), bsp(nt)],
        out_specs=(col(0), col(0), wsp(0), wsp(0), bsp(0), bsp(0)),
        compiler_params=_cparams(("parallel",)),
    )(up, up, dact, cw, cw, cb, cb)


def _dt_fwd(name, dtr, bias):
    rows = dtr.shape[0]
    tm = _row_tile(rows, LANES)

    def body(d_ref, b_ref, o_ref):
        v = d_ref[...] + b_ref[...]
        sp = jnp.maximum(v, 0.0) + jnp.log1p(jnp.exp(-jnp.abs(v)))
        lane = lax.broadcasted_iota(jnp.int32, (tm, LANES), 1)
        ok = _rows_mask(pl.program_id(0), tm) & (lane < SSM_HEADS)
        o_ref[...] = jnp.where(ok, sp, 0.0)

    return pl.pallas_call(
        body, name=name, out_shape=jax.ShapeDtypeStruct((rows, LANES), F32), grid=(rows // tm,),
        in_specs=[pl.BlockSpec((tm, LANES), lambda i: (i, 0)), pl.BlockSpec((1, LANES), lambda i: (0, 0))],
        out_specs=pl.BlockSpec((tm, LANES), lambda i: (i, 0)), compiler_params=_cparams(("parallel",)),
    )(dtr, bias)


def _dt_bwd(name, ddt, dtr, bias):
    rows = dtr.shape[0]
    tm = _row_tile(rows, LANES)

    def body(g_ref, d_ref, b_ref, o_ref, db_ref):
        i = pl.program_id(0)
        lane = lax.broadcasted_iota(jnp.int32, (tm, LANES), 1)
        ok = _rows_mask(i, tm) & (lane < SSM_HEADS)
        dv = jnp.where(ok, g_ref[...] * _sigmoid(d_ref[...] + b_ref[...]), 0.0)
        o_ref[...] = dv.astype(BF16)

        @pl.when(i == 0)
        def _():
            db_ref[...] = jnp.zeros_like(db_ref)

        db_ref[...] += jnp.sum(dv, axis=0, keepdims=True)

    row_spec = pl.BlockSpec((tm, LANES), lambda i: (i, 0))
    vec_spec = pl.BlockSpec((1, LANES), lambda i: (0, 0))
    return pl.pallas_call(
        body, name=name,
        out_shape=(jax.ShapeDtypeStruct((rows, LANES), BF16), jax.ShapeDtypeStruct((1, LANES), F32)),
        grid=(rows // tm,), in_specs=[row_spec, row_spec, vec_spec], out_specs=(row_spec, vec_spec),
        compiler_params=_cparams(("arbitrary",)),
    )(ddt, dtr, bias)


def _gate_fwd(name, y, zx, w):
    rows = y.shape[0]
    tm = _row_tile(rows, D_INNER)

    def body(y_ref, z_ref, w_ref, o_ref):
        z = z_ref[...]
        g = y_ref[...] * (z * _sigmoid(z))
        r = lax.rsqrt(jnp.mean(g * g, axis=-1, keepdims=True) + RMS_EPS)
        o_ref[...] = (g * r * w_ref[...]).astype(BF16)

    row_spec = pl.BlockSpec((tm, D_INNER), lambda i: (i, 0))
    return pl.pallas_call(
        body, name=name, out_shape=jax.ShapeDtypeStruct((rows, D_INNER), BF16), grid=(rows // tm,),
        in_specs=[row_spec, row_spec, pl.BlockSpec((1, D_INNER), lambda i: (0, 0))],
        out_specs=row_spec, compiler_params=_cparams(("parallel",)),
    )(y, zx, w)


def _gate_bwd(name, dyn, y, zx, w):
    rows = y.shape[0]
    tm = _row_tile(rows, D_INNER)

    def body(d_ref, y_ref, z_ref, w_ref, dy_ref, dz_ref, dw_ref):
        i = pl.program_id(0)
        z, yv = z_ref[...], y_ref[...]
        sig = _sigmoid(z)
        sz = z * sig
        g = yv * sz
        r = lax.rsqrt(jnp.mean(g * g, axis=-1, keepdims=True) + RMS_EPS)
        ghat = g * r
        dn = d_ref[...]
        dghat = dn * w_ref[...]
        dg = r * (dghat - ghat * jnp.mean(dghat * ghat, axis=-1, keepdims=True))
        dy_ref[...] = dg * sz
        dz_ref[...] = (dg * yv * sig * (1.0 + z * (1.0 - sig))).astype(BF16)

        @pl.when(i == 0)
        def _():
            dw_ref[...] = jnp.zeros_like(dw_ref)

        dw_ref[...] += jnp.sum(dn * ghat, axis=0, keepdims=True)

    row_spec = pl.BlockSpec((tm, D_INNER), lambda i: (i, 0))
    vec_spec = pl.BlockSpec((1, D_INNER), lambda i: (0, 0))
    return pl.pallas_call(
        body, name=name,
        out_shape=(jax.ShapeDtypeStruct((rows, D_INNER), F32), jax.ShapeDtypeStruct((rows, D_INNER), BF16),
                   jax.ShapeDtypeStruct((1, D_INNER), F32)),
        grid=(rows // tm,), in_specs=[row_spec, row_spec, row_spec, vec_spec],
        out_specs=(row_spec, row_spec, vec_spec), compiler_params=_cparams(("arbitrary",)),
    )(dyn, y, zx, w)


def _split3(x):
    hi = x.astype(BF16)
    r1 = x - hi.astype(F32)
    mid = r1.astype(BF16)
    lo = (r1 - mid.astype(F32)).astype(BF16)
    return hi, mid, lo


def _dot3_data_lhs(x, sel):
    sel16 = sel.astype(F32).astype(BF16)
    hi, mid, lo = _split3(x)
    return _dot(hi, sel16) + _dot(mid, sel16) + _dot(lo, sel16)


def _dot3_data_rhs(sel, x):
    sel16 = sel.astype(F32).astype(BF16)
    hi, mid, lo = _split3(x)
    return _dot(sel16, hi) + _dot(sel16, mid) + _dot(sel16, lo)


def _causal_masks():
    r = lax.broadcasted_iota(jnp.int32, (CHUNK, CHUNK), 0)
    c = lax.broadcasted_iota(jnp.int32, (CHUNK, CHUNK), 1)
    return r >= c, r <= c


def _expand_heads_matrix():
    k = lax.broadcasted_iota(jnp.int32, (LANES, GROUP_W), 0)
    j = lax.broadcasted_iota(jnp.int32, (LANES, GROUP_W), 1)
    return jnp.right_shift(j, 6) == k


def _reduce_heads_matrix():
    j = lax.broadcasted_iota(jnp.int32, (GROUP_W, LANES), 0)
    k = lax.broadcasted_iota(jnp.int32, (GROUP_W, LANES), 1)
    return jnp.right_shift(j, 6) == k


def _reduce_pair_matrix(p):
    j = lax.broadcasted_iota(jnp.int32, (LANES, LANES), 0)
    k = lax.broadcasted_iota(jnp.int32, (LANES, LANES), 1)
    return (2 * p + jnp.right_shift(j, 6)) == k


def _ssd_prep(name, dt4, a128):
    rows = dt4.shape[1]
    nc = rows // CHUNK

    def body(dt_ref, a_ref, dte_ref, acs_ref):
        causal, _ = _causal_masks()
        expand = _expand_heads_matrix()
        dt = dt_ref[...]
        acs = _dot3_data_rhs(causal, dt) * a_ref[...]
        dte_ref[...] = _dot3_data_lhs(dt, expand)
        acs_ref[...] = _dot3_data_lhs(acs, expand)

    blk = pl.BlockSpec((CHUNK, GROUP_W), lambda g, c: (c, g))
    shp = jax.ShapeDtypeStruct((rows, D_INNER), F32)
    return pl.pallas_call(
        body, name=name, out_shape=(shp, shp), grid=(N_GROUPS, nc),
        in_specs=[pl.BlockSpec((None, CHUNK, LANES), lambda g, c: (g, c, 0)),
                  pl.BlockSpec((None, 1, LANES), lambda g, c: (g, 0, 0))],
        out_specs=(blk, blk), compiler_params=_cparams(("parallel", "parallel")),
    )(dt4, a128)


def _ssd_common(x_ref, b_ref, c_ref, dte_ref, acs_ref):
    x = x_ref[...]
    dt_exp = dte_ref[...]
    acs_exp = acs_ref[...]
    tot_exp = acs_ref[pl.ds(CHUNK - 1, 1), :]
    xdt = x * dt_exp
    e_exp = jnp.exp(acs_exp)
    f_exp = jnp.exp(tot_exp - acs_exp)
    return _causal_masks(), x, dt_exp, acs_exp, tot_exp, xdt, e_exp, f_exp, b_ref[...], c_ref[...]


def _pair_decay(acs_pair, e, causal):
    lane = lax.broadcasted_iota(jnp.int32, (CHUNK, LANES), 1)
    mine = (lane < HEAD_DIM) if e == 0 else (lane >= HEAD_DIM)
    a_l = jnp.where(mine, acs_pair, pltpu.roll(acs_pair, HEAD_DIM, 1))
    seg = a_l - a_l.T
    dm = jnp.where(causal[0], jnp.exp(jnp.minimum(seg, 0.0)), 0.0)
    dmt = jnp.where(causal[1], jnp.exp(jnp.minimum(-seg, 0.0)), 0.0)
    return dm, dmt


def _ssd_fwd(name, xbc, dt_exp, acs_exp, dskexp):
    rows = xbc.shape[0]
    nc = rows // CHUNK
    bcol = D_INNER // LANES

    def body(x_ref, b_ref, c_ref, dte_ref, acs_ref, dsk_ref, y_ref, st_ref, s_scr):
        @pl.when(pl.program_id(1) == 0)
        def _():
            s_scr[...] = jnp.zeros_like(s_scr)

        causal, x, _, acs_exp_v, tot_exp, xdt, e_exp, f_exp, bm, cm = _ssd_common(x_ref, b_ref, c_ref, dte_ref, acs_ref)
        state = s_scr[...]
        st_ref[...] = state
        cb16, bb16 = cm.astype(BF16), bm.astype(BF16)
        cb = _dot_nt(cb16, bb16)
        base = e_exp * _dot(cb16, state.astype(BF16)) + dsk_ref[...] * x
        lane = lax.broadcasted_iota(jnp.int32, (CHUNK, LANES), 1)
        for p in range(HEADS_PER_GROUP // 2):
            sl = slice(p * LANES, (p + 1) * LANES)
            xp = xdt[:, sl].astype(BF16)
            yd = []
            for e in range(2):
                dm, _ = _pair_decay(acs_exp_v[:, sl], e, causal)
                yd.append(_dot((cb * dm).astype(BF16), xp))
            y_ref[:, sl] = jnp.where(lane < HEAD_DIM, yd[0], yd[1]) + base[:, sl]
        s_scr[...] = jnp.exp(tot_exp) * state + _dot(bm.T.astype(BF16), (f_exp * xdt).astype(BF16))

    blk = pl.BlockSpec((CHUNK, GROUP_W), lambda g, c: (c, g))
    return pl.pallas_call(
        body, name=name,
        out_shape=(jax.ShapeDtypeStruct((rows, D_INNER), F32),
                   jax.ShapeDtypeStruct((N_GROUPS, nc, D_STATE, GROUP_W), F32)),
        grid=(N_GROUPS, nc),
        in_specs=[blk,
                  pl.BlockSpec((CHUNK, LANES), lambda g, c: (c, bcol + g)),
                  pl.BlockSpec((CHUNK, LANES), lambda g, c: (c, bcol + N_GROUPS + g)),
                  blk, blk, pl.BlockSpec((None, 1, GROUP_W), lambda g, c: (g, 0, 0))],
        out_specs=(blk, pl.BlockSpec((None, None, D_STATE, GROUP_W), lambda g, c: (g, c, 0, 0))),
        scratch_shapes=[pltpu.VMEM((D_STATE, GROUP_W), F32)],
        compiler_params=_cparams(("parallel", "arbitrary")),
    )(xbc, xbc, xbc, dt_exp, acs_exp, dskexp)


def _ssd_bwd(name, xbc, dt_exp, acs_exp, dt4, a128, dskexp, dy, states):
    rows = xbc.shape[0]
    nc = rows // CHUNK
    bcol = D_INNER // LANES
    last = nc - 1

    def body(x_ref, b_ref, c_ref, dte_ref, acs_ref, dt_ref, a128_ref, dsk_ref, dy_ref, st_ref,
             dx_ref, db_ref, dc_ref, ddt_ref, dalog_ref, ddsk_ref, ds_scr):
        first = pl.program_id(1) == 0

        @pl.when(first)
        def _():
            ds_scr[...] = jnp.zeros_like(ds_scr)
            dalog_ref[...] = jnp.zeros_like(dalog_ref)
            ddsk_ref[...] = jnp.zeros_like(ddsk_ref)

        causal, x, dt_exp, acs_exp_v, tot_exp, xdt, e_exp, f_exp, bm, cm = _ssd_common(
            x_ref, b_ref, c_ref, dte_ref, acs_ref)
        dt = dt_ref[...]
        reduce_heads = _reduce_heads_matrix()
        state, dstate = st_ref[...], ds_scr[...]
        dyv = dy_ref[...]
        cb16, bb16 = cm.astype(BF16), bm.astype(BF16)
        s16, ds16 = state.astype(BF16), dstate.astype(BF16)
        cb = _dot_nt(cb16, bb16)
        cbt = _dot_nt(bb16, cb16)
        cs = _dot(cb16, s16)
        bds = _dot(bb16, ds16)
        edy = e_exp * dyv
        fx = f_exp * xdt
        dxdt_base = f_exp * bds
        dc_acc = _dot_nt(edy.astype(BF16), s16)
        db_acc = _dot_nt(fx.astype(BF16), ds16)
        ds_scr[...] = jnp.exp(tot_exp) * dstate + _dot(cm.T.astype(BF16), edy.astype(BF16))
        q = fx * bds
        dacs = _dot3_data_lhs(edy * cs - q, reduce_heads)
        dtot = jnp.sum(_dot3_data_lhs(q + jnp.exp(tot_exp) * dstate * state, reduce_heads), axis=0, keepdims=True)
        ddsk_ref[...] += jnp.sum(_dot3_data_lhs(dyv * x, reduce_heads), axis=0, keepdims=True)
        lane = lax.broadcasted_iota(jnp.int32, (CHUNK, LANES), 1)
        dcb = jnp.zeros((CHUNK, CHUNK), F32)
        dcbt = jnp.zeros((CHUNK, CHUNK), F32)
        ddt_x = jnp.zeros((CHUNK, LANES), F32)
        for p in range(HEADS_PER_GROUP // 2):
            sl = slice(p * LANES, (p + 1) * LANES)
            xp, dyp = xdt[:, sl], dyv[:, sl]
            xp16, dyp16 = xp.astype(BF16), dyp.astype(BF16)
            dxh = []
            for e in range(2):
                h = 2 * p + e
                mine = (lane < HEAD_DIM) if e == 0 else (lane >= HEAD_DIM)
                dm, dmt = _pair_decay(acs_exp_v[:, sl], e, causal)
                m, mt = cb * dm, cbt * dmt
                xh16 = jnp.where(mine, xp, 0.0).astype(BF16)
                dyh16 = jnp.where(mine, dyp, 0.0).astype(BF16)
                d_m = _dot_nt(dyh16, xp16)
                d_mt = _dot_nt(xh16, dyp16)
                dacs_h = (jnp.sum(d_m * m, axis=-1, keepdims=True)
                          - jnp.sum(d_mt * mt, axis=-1, keepdims=True))
                dacs = dacs + jnp.where(lane == h, dacs_h, 0.0)
                dcb = dcb + d_m * dm
                dcbt = dcbt + d_mt * dmt
                dxh.append(_dot(mt.astype(BF16), dyp16))
            dxdt = jnp.where(lane < HEAD_DIM, dxh[0], dxh[1]) + dxdt_base[:, sl]
            dx_ref[:, sl] = dxdt * dt_exp[:, sl] + dsk_ref[:, sl] * dyp
            ddt_x = ddt_x + _dot3_data_lhs(dxdt * x[:, sl], _reduce_pair_matrix(p))
        dc_ref[...] = dc_acc + _dot(dcb.astype(BF16), bb16)
        db_ref[...] = db_acc + _dot(dcbt.astype(BF16), cb16)
        row = lax.broadcasted_iota(jnp.int32, (CHUNK, LANES), 0)
        dacs = dacs + jnp.where(row == CHUNK - 1, dtot, 0.0)
        da = _dot3_data_rhs(causal[1], dacs)
        ddt_ref[...] = da * a128_ref[...] + ddt_x
        dalog_ref[...] += jnp.sum(da * dt, axis=0, keepdims=True) * a128_ref[...]

    vec = lambda w: pl.BlockSpec((None, 1, w), lambda g, c: (g, 0, 0))
    blk = pl.BlockSpec((CHUNK, GROUP_W), lambda g, c: (last - c, g))
    return pl.pallas_call(
        body, name=name,
        out_shape=(jax.ShapeDtypeStruct((rows, D_INNER), F32), jax.ShapeDtypeStruct((rows, D_BC), F32),
                   jax.ShapeDtypeStruct((rows, D_BC), F32), jax.ShapeDtypeStruct((N_GROUPS, rows, LANES), F32),
                   jax.ShapeDtypeStruct((N_GROUPS, 1, LANES), F32), jax.ShapeDtypeStruct((N_GROUPS, 1, LANES), F32)),
        grid=(N_GROUPS, nc),
        in_specs=[blk,
                  pl.BlockSpec((CHUNK, LANES), lambda g, c: (last - c, bcol + g)),
                  pl.BlockSpec((CHUNK, LANES), lambda g, c: (last - c, bcol + N_GROUPS + g)),
                  blk, blk,
                  pl.BlockSpec((None, CHUNK, LANES), lambda g, c: (g, last - c, 0)),
                  vec(LANES), vec(GROUP_W), blk,
                  pl.BlockSpec((None, None, D_STATE, GROUP_W), lambda g, c: (g, last - c, 0, 0))],
        out_specs=(blk,
                   pl.BlockSpec((CHUNK, LANES), lambda g, c: (last - c, g)),
                   pl.BlockSpec((CHUNK, LANES), lambda g, c: (last - c, g)),
                   pl.BlockSpec((None, CHUNK, LANES), lambda g, c: (g, last - c, 0)),
                   vec(LANES), vec(LANES)),
        scratch_shapes=[pltpu.VMEM((D_STATE, GROUP_W), F32)],
        compiler_params=_cparams(("parallel", "arbitrary")),
    )(xbc, xbc, xbc, dt_exp, acs_exp, dt4, a128, dskexp, dy, states)


def _attn_visible(b, heads=1):
    row = jnp.bitwise_and(lax.broadcasted_iota(jnp.int32, (heads * CHUNK, 3 * CHUNK), 0), CHUNK - 1)
    col = lax.broadcasted_iota(jnp.int32, (heads * CHUNK, 3 * CHUNK), 1)
    bb = b + jnp.zeros_like(col)
    meta = (col < CHUNK) & (bb >= 1) & (col >= PAD_ROWS)
    prev = (col >= CHUNK) & (col < 2 * CHUNK) & (bb >= 2) & ((col - CHUNK) > row)
    cur = (col >= 2 * CHUNK) & ((col - 2 * CHUNK) <= row) & ((bb >= 1) | ((col - 2 * CHUNK) >= PAD_ROWS))
    return meta | prev | cur


def _attn_visible4(b):
    return _attn_visible(b, 4)


def _stack_heads(q_ref, sink_ref, kvh, scale):
    lane = lax.broadcasted_iota(jnp.int32, (CHUNK, LANES), 1)
    parts, sinks = [], []
    for pp in range(2):
        pair = kvh * 2 + pp
        qp = q_ref[:, pair * LANES:(pair + 1) * LANES] * scale
        for e in range(2):
            mine = (lane < HEAD_DIM) if e == 0 else (lane >= HEAD_DIM)
            parts.append(jnp.where(mine, qp, 0.0).astype(BF16))
            sinks.append(jnp.full((CHUNK, 1), sink_ref[2 * pair + e], F32))
    return jnp.concatenate(parts, axis=0), jnp.concatenate(sinks, axis=0)


def _attn_probs(qm16, kc16, visible, sink):
    s = jnp.where(visible, _dot_nt(qm16, kc16), NEG_INF)
    m = jnp.maximum(jnp.max(s, axis=-1, keepdims=True), sink)
    pe = jnp.exp(s - m)
    pe_sink = jnp.exp(sink - m)
    inv = 1.0 / (jnp.sum(pe, axis=-1, keepdims=True) + pe_sink)
    return pe * inv, pe_sink * inv


def _attn_specs():
    blk = lambda f: pl.BlockSpec((CHUNK, 2 * D_KV), f)
    kv3 = [blk(lambda b: (0, 0)), blk(lambda b: (jnp.maximum(b - 1, 0), 0)), blk(lambda b: (b, 0))]
    return kv3


def _attn_fwd(name, q, k2, v2, sinks):
    rows = q.shape[0]

    def body(q_ref, k0, kp, kc, v0, vp, vc, sink_ref, o_ref):
        visible = _attn_visible4(pl.program_id(0))
        lane = lax.broadcasted_iota(jnp.int32, (CHUNK, LANES), 1)
        for kvh in range(N_KV_HEADS):
            ksl = slice(kvh * LANES, (kvh + 1) * LANES)
            kcat = jnp.concatenate([k0[:, ksl], kp[:, ksl], kc[:, ksl]], axis=0).astype(BF16)
            vcat = jnp.concatenate([v0[:, ksl], vp[:, ksl], vc[:, ksl]], axis=0).astype(BF16)
            q4, sink4 = _stack_heads(q_ref, sink_ref, kvh, ATTN_SCALE)
            pn, _ = _attn_probs(q4, kcat, visible, sink4)
            o4 = _dot(pn.astype(BF16), vcat)
            for pp in range(2):
                qsl = slice((kvh * 2 + pp) * LANES, (kvh * 2 + pp + 1) * LANES)
                o_ref[:, qsl] = jnp.where(lane < HEAD_DIM, o4[(2 * pp) * CHUNK:(2 * pp + 1) * CHUNK],
                                          o4[(2 * pp + 1) * CHUNK:(2 * pp + 2) * CHUNK]).astype(BF16)

    return pl.pallas_call(
        body, name=name, out_shape=jax.ShapeDtypeStruct((rows, D_MODEL), BF16), grid=(rows // CHUNK,),
        in_specs=[pl.BlockSpec((CHUNK, D_MODEL), lambda b: (b, 0))] + _attn_specs() + _attn_specs()
        + [pl.BlockSpec(memory_space=pltpu.SMEM)],
        out_specs=pl.BlockSpec((CHUNK, D_MODEL), lambda b: (b, 0)), compiler_params=_cparams(("parallel",)),
    )(q, k2, k2, k2, v2, v2, v2, sinks)


def _attn_bwd(name, q, k2, v2, sinks, do):
    rows = q.shape[0]

    def body(q_ref, k0, kp, kc, v0, vp, vc, sink_ref, do_ref,
             dq_ref, dkc_ref, dkp_ref, dvc_ref, dvp_ref, dkm_ref, dvm_ref, dsink_ref):
        @pl.when(pl.program_id(0) == 0)
        def _():
            dkm_ref[...] = jnp.zeros_like(dkm_ref)
            dvm_ref[...] = jnp.zeros_like(dvm_ref)
            dsink_ref[...] = jnp.zeros_like(dsink_ref)

        visible = _attn_visible4(pl.program_id(0))
        lane = lax.broadcasted_iota(jnp.int32, (CHUNK, LANES), 1)
        lane1 = lax.broadcasted_iota(jnp.int32, (1, LANES), 1)
        dsink = jnp.zeros((1, LANES), F32)
        for kvh in range(N_KV_HEADS):
            ksl = slice(kvh * LANES, (kvh + 1) * LANES)
            kcat = jnp.concatenate([k0[:, ksl], kp[:, ksl], kc[:, ksl]], axis=0).astype(BF16)
            vcat = jnp.concatenate([v0[:, ksl], vp[:, ksl], vc[:, ksl]], axis=0).astype(BF16)
            q4, sink4 = _stack_heads(q_ref, sink_ref, kvh, ATTN_SCALE)
            do4, _ = _stack_heads(do_ref, sink_ref, kvh, 1.0)
            pn, psink = _attn_probs(q4, kcat, visible, sink4)
            dp = _dot_nt(do4, vcat)
            delta = jnp.sum(pn * dp, axis=-1, keepdims=True)
            ds = pn * (dp - delta)
            sink_terms = psink * delta
            dq4 = _dot(ds.astype(BF16), kcat)
            dk_acc = _dot(ds.T.astype(BF16), q4)
            dv_acc = _dot(pn.T.astype(BF16), do4)
            for j in range(4):
                part = jnp.sum(sink_terms[j * CHUNK:(j + 1) * CHUNK], axis=0, keepdims=True)
                dsink = dsink - jnp.where(lane1 == kvh * 4 + j, part, 0.0)
            for pp in range(2):
                qsl = slice((kvh * 2 + pp) * LANES, (kvh * 2 + pp + 1) * LANES)
                dq_pair = jnp.where(lane < HEAD_DIM, dq4[(2 * pp) * CHUNK:(2 * pp + 1) * CHUNK],
                                    dq4[(2 * pp + 1) * CHUNK:(2 * pp + 2) * CHUNK])
                dq_ref[:, qsl] = (dq_pair * ATTN_SCALE).astype(BF16)
            dkm_ref[:, ksl] += dk_acc[0:CHUNK]
            dvm_ref[:, ksl] += dv_acc[0:CHUNK]
            dkp_ref[:, ksl] = dk_acc[CHUNK:2 * CHUNK]
            dvp_ref[:, ksl] = dv_acc[CHUNK:2 * CHUNK]
            dkc_ref[:, ksl] = dk_acc[2 * CHUNK:3 * CHUNK]
            dvc_ref[:, ksl] = dv_acc[2 * CHUNK:3 * CHUNK]
        dsink_ref[...] += dsink

    qspec = pl.BlockSpec((CHUNK, D_MODEL), lambda b: (b, 0))
    kvspec = pl.BlockSpec((CHUNK, 2 * D_KV), lambda b: (b, 0))
    fixed = pl.BlockSpec((CHUNK, 2 * D_KV), lambda b: (0, 0))
    kv_shape = jax.ShapeDtypeStruct((rows, 2 * D_KV), F32)
    meta_shape = jax.ShapeDtypeStruct((CHUNK, 2 * D_KV), F32)
    return pl.pallas_call(
        body, name=name,
        out_shape=(jax.ShapeDtypeStruct((rows, D_MODEL), BF16), kv_shape, kv_shape, kv_shape, kv_shape,
                   meta_shape, meta_shape, jax.ShapeDtypeStruct((1, LANES), F32)),
        grid=(rows // CHUNK,),
        in_specs=[qspec] + _attn_specs() + _attn_specs() + [pl.BlockSpec(memory_space=pltpu.SMEM), qspec],
        out_specs=(qspec, kvspec, kvspec, kvspec, kvspec, fixed, fixed, pl.BlockSpec((1, LANES), lambda b: (0, 0))),
        compiler_params=_cparams(("arbitrary",)),
    )(q, k2, k2, k2, v2, v2, v2, sinks, do)


def _kv_grad_combine(name, d_cur, d_prev, d_meta):
    rows = d_cur.shape[0]
    nb = rows // CHUNK

    def body(c_ref, p_ref, m_ref, o_ref):
        j = pl.program_id(0)
        jj = j + jnp.zeros((CHUNK, 1), jnp.int32)
        o_ref[...] = (c_ref[...] + jnp.where(jj < nb - 1, p_ref[...], 0.0) + jnp.where(jj == 0, m_ref[...], 0.0))

    blk = lambda f: pl.BlockSpec((CHUNK, 2 * D_KV), f)
    return pl.pallas_call(
        body, name=name, out_shape=jax.ShapeDtypeStruct((rows, 2 * D_KV), F32), grid=(nb,),
        in_specs=[blk(lambda j: (j, 0)), blk(lambda j: (jnp.minimum(j + 1, nb - 1), 0)), blk(lambda j: (0, 0))],
        out_specs=blk(lambda j: (j, 0)), compiler_params=_cparams(("parallel",)),
    )(d_cur, d_prev, d_meta)


def _loss_head(name, h, target):
    rows = h.shape[0]

    def body(h_ref, t_ref, dh_ref, loss_ref):
        i = pl.program_id(0)
        real = (i + jnp.zeros((CHUNK, 1), jnp.int32)) >= 1
        diff = jnp.where(real, h_ref[...] - t_ref[...], 0.0)
        dh_ref[...] = diff * (1.0 / D_MODEL)

        @pl.when(i == 0)
        def _():
            loss_ref[...] = jnp.zeros_like(loss_ref)

        loss_ref[...] += jnp.sum(diff * diff) * (0.5 / D_MODEL)

    blk = pl.BlockSpec((CHUNK, D_MODEL), lambda i: (i, 0))
    return pl.pallas_call(
        body, name=name,
        out_shape=(jax.ShapeDtypeStruct((rows, D_MODEL), F32), jax.ShapeDtypeStruct((1, LANES), F32)),
        grid=(rows // CHUNK,),
        in_specs=[blk, pl.BlockSpec((CHUNK, D_MODEL), lambda i: (jnp.maximum(i - 1, 0), 0))],
        out_specs=(blk, pl.BlockSpec((1, LANES), lambda i: (0, 0))), compiler_params=_cparams(("arbitrary",)),
    )(h, target)


def _adamw(name, w, g, m, v):
    rows, width = w.shape
    tr = rows
    for cand in range(8, rows + 1, 8):
        if rows % cand == 0 and cand * width * 4 <= (1 << 20):
            tr = cand

    def body(w_ref, g_ref, m_ref, v_ref, d_ref, mo_ref, vo_ref):
        gv = g_ref[...]
        mn = ADAM_B1 * m_ref[...] + (1.0 - ADAM_B1) * gv
        vn = ADAM_B2 * v_ref[...] + (1.0 - ADAM_B2) * (gv * gv)
        m_hat = mn / (1.0 - ADAM_B1 ** ADAM_STEP)
        v_hat = vn / (1.0 - ADAM_B2 ** ADAM_STEP)
        d_ref[...] = -ADAM_LR * (m_hat / (jnp.sqrt(v_hat) + ADAM_EPS) + ADAM_WD * w_ref[...])
        mo_ref[...] = mn
        vo_ref[...] = vn

    blk = pl.BlockSpec((tr, width), lambda i: (i, 0))
    shp = jax.ShapeDtypeStruct((rows, width), F32)
    return pl.pallas_call(
        body, name=name, out_shape=(shp, shp, shp), grid=(rows // tr,), in_specs=[blk] * 4, out_specs=(blk,) * 3,
        compiler_params=_cparams(("parallel",)),
    )(w, g, m, v)


def _ffn_fwd(tag, h, p, i):
    hn = _rms_fwd(f"ffn{tag}_norm", h, p["f_norm_pre"][i:i + 1])
    up = _mm(f"ffn{tag}_up", hn, p["f_w_up"][i], "nn")
    act = _ffn_conv_fwd(f"ffn{tag}_conv", up, p["f_conv_w"][i], p["f_conv_b"][i:i + 1])
    pre = _mm(f"ffn{tag}_down", act, p["f_w_down"][i], "nn")
    h_new = _resid_norm_fwd(f"ffn{tag}_resid", h, pre, p["f_norm_post"][i:i + 1])
    return h_new, (h, hn, up, act, pre)


def _ffn_bwd(tag, dh, saved, p, i):
    h, hn, up, act, pre = saved
    dpre, g_post = _resid_norm_bwd(f"ffn{tag}_resid_bwd", dh, pre, p["f_norm_post"][i:i + 1])
    dact = _mm(f"ffn{tag}_down_dx", dpre, p["f_w_down"][i], "nt")
    g_down = _mm(f"ffn{tag}_down_dw", act, dpre, "tn")
    dug, duv, gwg, gwv, gbg, gbv = _ffn_conv_bwd(f"ffn{tag}_conv_bwd", up, dact, p["f_conv_w"][i], p["f_conv_b"][i:i + 1])
    g_cw, g_cb = jnp.concatenate([gwg, gwv], axis=1), jnp.concatenate([gbg, gbv], axis=1)
    dhn = _mm(f"ffn{tag}_up_dx_gate", dug, p["f_w_up"][i], "nt", b_colblock=0)
    dhn = _mm(f"ffn{tag}_up_dx_val", duv, p["f_w_up"][i], "nt", b_colblock=1, acc=dhn)
    g_up = jnp.concatenate([_mm(f"ffn{tag}_up_dw_gate", hn, dug, "tn"), _mm(f"ffn{tag}_up_dw_val", hn, duv, "tn")], axis=1)
    dh_new, g_pre = _norm_bwd_add(f"ffn{tag}_norm_bwd", dh, dhn, h, p["f_norm_pre"][i:i + 1])
    return dh_new, dict(f_norm_post=g_post, f_w_down=g_down, f_conv_w=g_cw, f_conv_b=g_cb, f_w_up=g_up, f_norm_pre=g_pre)


def _lanes_pad(a, width=LANES):
    return jnp.pad(a, [(0, 0)] * (a.ndim - 1) + [(0, width - a.shape[-1])])


def _dup_heads(a):
    rows = a.shape[0]
    a = a.reshape(rows, N_KV_HEADS, 1, HEAD_DIM)
    return jnp.broadcast_to(a, (rows, N_KV_HEADS, 2, HEAD_DIM)).reshape(rows, 2 * D_KV)


def _undup_heads(a):
    rows = a.shape[0]
    return a.reshape(rows, N_KV_HEADS, 2, HEAD_DIM).sum(axis=2).reshape(rows, D_KV)


def _local_step(x2, target, p):
    seq = x2.shape[0]
    rows = seq + CHUNK
    g = {}

    h0 = jnp.concatenate([jnp.zeros((PAD_ROWS, D_MODEL), F32), p["meta_tokens"], x2], axis=0)

    w_main = p["a_w_in"][:, :D_MAIN]
    w_dt = _lanes_pad(p["a_w_in"][:, D_MAIN:])
    dt_bias = _lanes_pad(p["a_dt_bias"])
    a_neg = -jnp.exp(p["a_a_log"].reshape(N_GROUPS, HEADS_PER_GROUP))
    a128 = _lanes_pad(a_neg.reshape(N_GROUPS, 1, HEADS_PER_GROUP))
    dskexp = jnp.repeat(p["a_d_skip"].reshape(N_GROUPS, HEADS_PER_GROUP), HEAD_DIM, axis=1).reshape(N_GROUPS, 1, GROUP_W)

    hn0 = _rms_fwd("a_norm", h0, p["a_norm_pre"])
    zx = _mm("a_in_main", hn0, w_main, "nn")
    dtr = _mm("a_in_dt", hn0, w_dt, "nn")
    xbc = _conv4_fwd("a_conv", zx, p["a_conv_w"], p["a_conv_b"])
    dt = _dt_fwd("a_dt", dtr, dt_bias)
    dt4 = _lanes_pad(dt[:, :SSM_HEADS].reshape(rows, N_GROUPS, HEADS_PER_GROUP).transpose(1, 0, 2))
    dt_exp, acs_exp = _ssd_prep("a_ssd_prep", dt4, a128)
    y, states = _ssd_fwd("a_ssd", xbc, dt_exp, acs_exp, dskexp)
    yn = _gate_fwd("a_gate", y, zx, p["a_gate_norm"])
    mix = _mm("a_out", yn, p["a_w_out"], "nn")
    h1 = _resid_norm_fwd("a_resid", h0, mix, p["a_norm_post"])

    h2, ffn0 = _ffn_fwd("0", h1, p, 0)

    hkv = _rms_fwd("kv_norm", h2, p["kv_norm"])
    kv = _mm("kv_proj", hkv, p["w_kv"], "nn")
    k2, v2 = _dup_heads(kv[:, :D_KV]), _dup_heads(kv[:, D_KV:])
    hn2 = _rms_fwd("b_norm", h2, p["b_norm_pre"])
    q = _mm("b_q", hn2, p["b_w_q"], "nn")
    sinks = p["b_sinks"].reshape(N_Q_HEADS)
    o = _attn_fwd("b_attn", q, k2, v2, sinks)
    attn = _mm("b_o", o, p["b_w_o"], "nn")
    h3 = _resid_norm_fwd("b_resid", h2, attn, p["b_norm_post"])

    h4, ffn1 = _ffn_fwd("1", h3, p, 1)

    dh, loss_vec = _loss_head("loss", h4, target)
    loss = loss_vec[0, 0]

    dh, g1 = _ffn_bwd("1", dh, ffn1, p, 1)

    dpre, g["b_norm_post"] = _resid_norm_bwd("b_resid_bwd", dh, attn, p["b_norm_post"])
    do = _mm("b_o_dx", dpre, p["b_w_o"], "nt")
    g["b_w_o"] = _mm("b_o_dw", o, dpre, "tn")
    dq, dkc, dkp, dvc, dvp, dkm, dvm, dsink = _attn_bwd("b_attn_bwd", q, k2, v2, sinks, do)
    g["b_sinks"] = dsink[:, :N_Q_HEADS]
    dhn2 = _mm("b_q_dx", dq, p["b_w_q"], "nt")
    g["b_w_q"] = _mm("b_q_dw", hn2, dq, "tn")
    dh, g["b_norm_pre"] = _norm_bwd_add("b_norm_bwd", dh, dhn2, h2, p["b_norm_pre"])
    dk2 = _kv_grad_combine("k_grad", dkc, dkp, dkm)
    dv2 = _kv_grad_combine("v_grad", dvc, dvp, dvm)
    dkv = jnp.concatenate([_undup_heads(dk2), _undup_heads(dv2)], axis=1).astype(BF16)
    dhkv = _mm("kv_proj_dx", dkv, p["w_kv"], "nt")
    g["w_kv"] = _mm("kv_proj_dw", hkv, dkv, "tn")
    dh, g["kv_norm"] = _norm_bwd_add("kv_norm_bwd", dh, dhkv, h2, p["kv_norm"])

    dh, g0 = _ffn_bwd("0", dh, ffn0, p, 0)
    for name in g0:
        if name in ("f_w_up", "f_w_down"):
            g[name] = [g0[name], g1[name]]
        elif g0[name].shape[0] == 1:
            g[name] = jnp.concatenate([g0[name], g1[name]], axis=0)
        else:
            g[name] = jnp.stack([g0[name], g1[name]])

    dpre, g["a_norm_post"] = _resid_norm_bwd("a_resid_bwd", dh, mix, p["a_norm_post"])
    dyn = _mm("a_out_dx", dpre, p["a_w_out"], "nt")
    g["a_w_out"] = _mm("a_out_dw", yn, dpre, "tn")
    dy, dz, g["a_gate_norm"] = _gate_bwd("a_gate_bwd", dyn, y, zx, p["a_gate_norm"])
    dxs, dbm, dcm, ddt4, dalog, ddsk = _ssd_bwd("a_ssd_bwd", xbc, dt_exp, acs_exp, dt4, a128, dskexp, dy, states)
    g["a_a_log"] = dalog[:, 0, :HEADS_PER_GROUP].reshape(1, SSM_HEADS)
    g["a_d_skip"] = ddsk[:, 0, :HEADS_PER_GROUP].reshape(1, SSM_HEADS)
    ddt = _lanes_pad(ddt4[:, :, :HEADS_PER_GROUP].transpose(1, 0, 2).reshape(rows, SSM_HEADS))
    ddtr, dbias = _dt_bwd("a_dt_bwd", ddt, dtr, dt_bias)
    g["a_dt_bias"] = dbias[:, :SSM_HEADS]
    dxp, gw_x, gb_x = _conv4_bwd("a_conv_bwd_x", zx, dxs, p["a_conv_w"], p["a_conv_b"], 0)
    dbp, gw_b, gb_b = _conv4_bwd("a_conv_bwd_b", zx, dbm, p["a_conv_w"], p["a_conv_b"], D_INNER)
    dcp, gw_c, gb_c = _conv4_bwd("a_conv_bwd_c", zx, dcm, p["a_conv_w"], p["a_conv_b"], D_INNER + D_BC)
    g["a_conv_w"] = jnp.concatenate([gw_x, gw_b, gw_c], axis=1)
    g["a_conv_b"] = jnp.concatenate([gb_x, gb_b, gb_c], axis=1)
    dzx = jnp.concatenate([dz, dxp, dbp, dcp], axis=1)
    g_main = _mm("a_in_main_dw", hn0, dzx, "tn")
    g_dt = _mm("a_in_dt_dw", hn0, ddtr, "tn")
    g["a_w_in"] = jnp.concatenate([g_main, g_dt[:, :SSM_HEADS]], axis=1)
    dhn0 = _mm("a_in_main_dx", dzx, w_main, "nt")
    dhn0 = _mm("a_in_dt_dx", ddtr, w_dt, "nt", acc=dhn0)
    dh, g["a_norm_pre"] = _norm_bwd_add("a_norm_bwd", dh, dhn0, h0, p["a_norm_pre"])

    g["meta_tokens"] = dh[PAD_ROWS:CHUNK]
    return loss, dh[CHUNK:], g


def _place():
    return lax.axis_index("x"), lax.axis_index("y"), lax.axis_index("c")


def _other_chips(x, y):
    return [(1 - x, y), (x, 1 - y), (1 - x, 1 - y)]


ANY = pl.BlockSpec(memory_space=pl.ANY)
VMEM_SPEC = pl.BlockSpec(memory_space=pltpu.VMEM)


def _allgather_small(name, shard):
    rows = shard.shape[0]

    def body(s_ref, o_ref, send_sems, recv_sems):
        x, y, c = _place()
        me = 2 * x + y
        o_ref[me] = s_ref[...]
        chips = _other_chips(x, y)
        sends = [pltpu.make_async_remote_copy(s_ref, o_ref.at[me], send_sems.at[j], recv_sems.at[j],
                                              device_id=(cx, cy, c), device_id_type=MESH)
                 for j, (cx, cy) in enumerate(chips)]
        for cp in sends:
            cp.start()
        for j, (cx, cy) in enumerate(chips):
            pltpu.make_async_remote_copy(s_ref, o_ref.at[2 * cx + cy], send_sems.at[j], recv_sems.at[j],
                                         device_id=(cx, cy, c), device_id_type=MESH).wait_recv()
        for cp in sends:
            cp.wait_send()

    return pl.pallas_call(
        body, name=name, out_shape=jax.ShapeDtypeStruct((N_CHIPS, rows, LANES), F32),
        in_specs=[VMEM_SPEC], out_specs=VMEM_SPEC,
        scratch_shapes=[pltpu.SemaphoreType.DMA((3,)), pltpu.SemaphoreType.DMA((3,))],
        compiler_params=pltpu.CompilerParams(vmem_limit_bytes=VMEM_LIMIT),
    )(shard)


def _row_block(rows, width, itemsize, align, budget=2 << 20):
    best = None
    for cand in range(align, rows + 1, align):
        if rows % cand == 0 and cand * width * itemsize <= budget:
            best = cand
    assert best is not None, (rows, width)
    return best


def _cast_into_slot(name, chip, w2d):
    rows, width = w2d.shape
    tr = _row_block(rows, width, 4, 16)

    def body(chip_ref, w_ref, o_ref):
        o_ref[...] = w_ref[...].astype(BF16)

    return pl.pallas_call(
        body, name=name, out_shape=jax.ShapeDtypeStruct((N_CHIPS, rows, width), BF16),
        grid_spec=pltpu.PrefetchScalarGridSpec(
            num_scalar_prefetch=1, grid=(rows // tr,),
            in_specs=[pl.BlockSpec((tr, width), lambda i, chip_ref: (i, 0))],
            out_specs=pl.BlockSpec((None, tr, width), lambda i, chip_ref: (chip_ref[0], i, 0))),
        compiler_params=_cparams(("parallel",)),
    )(chip, w2d)


def _allgather_big(name, bufs):
    n = len(bufs)

    def body(*refs):
        outs = refs[n:2 * n]
        send_sems, recv_sems = refs[2 * n], refs[2 * n + 1]
        x, y, c = _place()
        me = 2 * x + y
        chips = _other_chips(x, y)
        sibling = (x, y, 1 - c)

        def rows_of(o, which):
            hr = o.shape[1] // 2
            return pl.ds(which * hr, hr)

        def ici(k, j, cx, cy, idx):
            o = outs[k]
            part = o.at[idx, rows_of(o, c)]
            return pltpu.make_async_remote_copy(part, part, send_sems.at[6 * k + j], recv_sems.at[6 * k + j],
                                                device_id=(cx, cy, c), device_id_type=MESH)

        def d2d(k, j, idx, which):
            o = outs[k]
            part = o.at[idx, rows_of(o, which)]
            return pltpu.make_async_remote_copy(part, part, send_sems.at[6 * k + 3 + j], recv_sems.at[6 * k + 3 + j],
                                                device_id=sibling, device_id_type=MESH)

        started = []
        for k in range(n):
            for j, (cx, cy) in enumerate(chips):
                cp = ici(k, j, cx, cy, me)
                cp.start()
                started.append(cp)
        for k in range(n):
            for j, (cx, cy) in enumerate(chips):
                ici(k, j, cx, cy, 2 * cx + cy).wait_recv()
                fwd = d2d(k, j, 2 * cx + cy, c)
                fwd.start()
                started.append(fwd)
        for k in range(n):
            for j, (cx, cy) in enumerate(chips):
                d2d(k, j, 2 * cx + cy, 1 - c).wait_recv()
        for cp in started:
            cp.wait_send()

    return pl.pallas_call(
        body, name=name, out_shape=[jax.ShapeDtypeStruct(b.shape, b.dtype) for b in bufs],
        in_specs=[ANY] * n, out_specs=[ANY] * n, input_output_aliases={k: k for k in range(n)},
        scratch_shapes=[pltpu.SemaphoreType.DMA((6 * n,)), pltpu.SemaphoreType.DMA((6 * n,))],
        compiler_params=pltpu.CompilerParams(vmem_limit_bytes=VMEM_LIMIT),
    )(*bufs)


def _allreduce_small(name, vec):
    rows = vec.shape[0]

    def body(v_ref, o_ref, buf, send_sems, recv_sems):
        x, y, c = _place()
        me = 4 * x + 2 * y + c
        buf[me] = v_ref[...]

        def peer(k):
            kx, ky, kc = (k >> 2) & 1, (k >> 1) & 1, k & 1
            return (1 - x if kx else x, 1 - y if ky else y, 1 - c if kc else c)

        sends = []
        for k in range(1, N_DEV):
            cp = pltpu.make_async_remote_copy(v_ref, buf.at[me], send_sems.at[k - 1], recv_sems.at[k - 1],
                                              device_id=peer(k), device_id_type=MESH)
            cp.start()
            sends.append(cp)
        for k in range(1, N_DEV):
            px, py, pc = peer(k)
            pltpu.make_async_remote_copy(v_ref, buf.at[4 * px + 2 * py + pc], send_sems.at[k - 1], recv_sems.at[k - 1],
                                         device_id=(px, py, pc), device_id_type=MESH).wait_recv()
        for cp in sends:
            cp.wait_send()
        acc = buf[0]
        for d in range(1, N_DEV):
            acc = acc + buf[d]
        o_ref[...] = acc

    return pl.pallas_call(
        body, name=name, out_shape=jax.ShapeDtypeStruct((rows, LANES), F32),
        in_specs=[VMEM_SPEC], out_specs=VMEM_SPEC,
        scratch_shapes=[pltpu.VMEM((N_DEV, rows, LANES), F32), pltpu.SemaphoreType.DMA((N_DEV - 1,)),
                        pltpu.SemaphoreType.DMA((N_DEV - 1,))],
        compiler_params=pltpu.CompilerParams(vmem_limit_bytes=VMEM_LIMIT),
    )(vec)


def _rs_pair_exchange(name, grads):
    n = len(grads)

    def body(*refs):
        ins, outs = refs[:n], refs[n:2 * n]
        send_sems, recv_sems = refs[2 * n], refs[2 * n + 1]
        x, y, c = _place()
        copies = []
        for k in range(n):
            hr = ins[k].shape[1] // 2
            cp = pltpu.make_async_remote_copy(ins[k].at[:, pl.ds((1 - c) * hr, hr)], outs[k], send_sems.at[k],
                                              recv_sems.at[k], device_id=(x, y, 1 - c), device_id_type=MESH)
            cp.start()
            copies.append(cp)
        for cp in copies:
            cp.wait()

    return pl.pallas_call(
        body, name=name,
        out_shape=[jax.ShapeDtypeStruct((N_CHIPS, g.shape[1] // 2, g.shape[2]), g.dtype) for g in grads],
        in_specs=[ANY] * n, out_specs=[ANY] * n,
        scratch_shapes=[pltpu.SemaphoreType.DMA((n,)), pltpu.SemaphoreType.DMA((n,))],
        compiler_params=pltpu.CompilerParams(vmem_limit_bytes=VMEM_LIMIT),
    )(*grads)


def _rs_pair_add(name, place, grads, partner):
    _, half_rows, width = partner.shape
    tr = _row_block(half_rows, width, 2, 16)
    nb = half_rows // tr

    def body(place_ref, g_ref, p_ref, o_ref):
        o_ref[...] = (g_ref[...].astype(F32) + p_ref[...].astype(F32)).astype(BF16)

    return pl.pallas_call(
        body, name=name, out_shape=jax.ShapeDtypeStruct(partner.shape, BF16),
        grid_spec=pltpu.PrefetchScalarGridSpec(
            num_scalar_prefetch=1, grid=(N_CHIPS, nb),
            in_specs=[pl.BlockSpec((None, tr, width), lambda s, i, pr: (s, pr[1] * nb + i, 0)),
                      pl.BlockSpec((None, tr, width), lambda s, i, pr: (s, i, 0))],
            out_specs=pl.BlockSpec((None, tr, width), lambda s, i, pr: (s, i, 0))),
        compiler_params=_cparams(("parallel", "parallel")),
    )(place, grads, partner)


def _rs_chip_exchange(name, partials):
    n = len(partials)

    def body(*refs):
        ins, outs = refs[:n], refs[n:2 * n]
        send_sems, recv_sems = refs[2 * n], refs[2 * n + 1]
        x, y, c = _place()
        chips = _other_chips(x, y)
        sends = []
        for k in range(n):
            for j, (cx, cy) in enumerate(chips):
                cp = pltpu.make_async_remote_copy(ins[k].at[2 * cx + cy], outs[k].at[j], send_sems.at[3 * k + j],
                                                  recv_sems.at[3 * k + j], device_id=(cx, cy, c), device_id_type=MESH)
                cp.start()
                sends.append(cp)
        for cp in sends:
            cp.wait()

    return pl.pallas_call(
        body, name=name,
        out_shape=[jax.ShapeDtypeStruct((3,) + q.shape[1:], q.dtype) for q in partials],
        in_specs=[ANY] * n, out_specs=[ANY] * n,
        scratch_shapes=[pltpu.SemaphoreType.DMA((3 * n,)), pltpu.SemaphoreType.DMA((3 * n,))],
        compiler_params=pltpu.CompilerParams(vmem_limit_bytes=VMEM_LIMIT),
    )(*partials)


def _rs_chip_add(name, place, mine, others):
    _, half_rows, width = mine.shape
    tr = _row_block(half_rows, width, 4, 16, budget=1 << 20)
    nb = half_rows // tr

    def body(place_ref, q_ref, r_ref, o_ref):
        acc = q_ref[...].astype(F32)
        for j in range(3):
            acc = acc + r_ref[j].astype(F32)
        o_ref[...] = acc

    return pl.pallas_call(
        body, name=name, out_shape=jax.ShapeDtypeStruct((2 * half_rows, width), F32),
        grid_spec=pltpu.PrefetchScalarGridSpec(
            num_scalar_prefetch=1, grid=(nb,),
            in_specs=[pl.BlockSpec((None, tr, width), lambda i, pr: (pr[0], i, 0)),
                      pl.BlockSpec((3, tr, width), lambda i, pr: (0, i, 0))],
            out_specs=pl.BlockSpec((tr, width), lambda i, pr: (pr[1] * nb + i, 0))),
        compiler_params=_cparams(("parallel",)),
    )(place, mine, others)


def _rs_pair_gather(name, shards):
    n = len(shards)

    def body(*refs):
        outs = refs[n:2 * n]
        send_sems, recv_sems = refs[2 * n], refs[2 * n + 1]
        x, y, c = _place()
        copies = []
        for k in range(n):
            hr = outs[k].shape[0] // 2
            part = outs[k].at[pl.ds(c * hr, hr)]
            cp = pltpu.make_async_remote_copy(part, part, send_sems.at[k], recv_sems.at[k],
                                              device_id=(x, y, 1 - c), device_id_type=MESH)
            cp.start()
            copies.append(cp)
        for k in range(n):
            hr = outs[k].shape[0] // 2
            theirs = outs[k].at[pl.ds((1 - c) * hr, hr)]
            pltpu.make_async_remote_copy(theirs, theirs, send_sems.at[k], recv_sems.at[k],
                                         device_id=(x, y, 1 - c), device_id_type=MESH).wait_recv()
        for cp in copies:
            cp.wait_send()

    return pl.pallas_call(
        body, name=name, out_shape=[jax.ShapeDtypeStruct(s.shape, s.dtype) for s in shards],
        in_specs=[ANY] * n, out_specs=[ANY] * n, input_output_aliases={k: k for k in range(n)},
        scratch_shapes=[pltpu.SemaphoreType.DMA((n,)), pltpu.SemaphoreType.DMA((n,))],
        compiler_params=pltpu.CompilerParams(vmem_limit_bytes=VMEM_LIMIT),
    )(*shards)


WEIGHTS = ["meta_tokens", "a_norm_pre", "a_w_in", "a_conv_w", "a_conv_b", "a_dt_bias", "a_a_log", "a_d_skip",
           "a_gate_norm", "a_w_out", "a_norm_post", "kv_norm", "w_kv", "b_norm_pre", "b_w_q", "b_sinks", "b_w_o",
           "b_norm_post", "f_norm_pre", "f_w_up", "f_conv_w", "f_conv_b", "f_w_down", "f_norm_post"]
FULL_SHAPE = {
    "meta_tokens": (16, 1024), "a_norm_pre": (1, 1024), "a_w_in": (1, 1024, 5152), "a_conv_w": (1, 4, 3072),
    "a_conv_b": (1, 3072), "a_dt_bias": (1, 32), "a_a_log": (1, 32), "a_d_skip": (1, 32), "a_gate_norm": (1, 2048),
    "a_w_out": (1, 2048, 1024), "a_norm_post": (1, 1024), "kv_norm": (1024,), "w_kv": (1024, 512),
    "b_norm_pre": (1, 1024), "b_w_q": (1, 1024, 1024), "b_sinks": (1, 16), "b_w_o": (1, 1024, 1024),
    "b_norm_post": (1, 1024), "f_norm_pre": (2, 1024), "f_w_up": (2, 1024, 5632), "f_conv_w": (2, 3, 5632),
    "f_conv_b": (2, 5632), "f_w_down": (2, 2816, 1024), "f_norm_post": (2, 1024),
}
SHARD_AXIS = {
    "meta_tokens": 1, "a_norm_pre": 1, "a_w_in": 2, "a_conv_w": 2, "a_conv_b": 1, "a_dt_bias": None, "a_a_log": None,
    "a_d_skip": None, "a_gate_norm": 1, "a_w_out": 1, "a_norm_post": 1, "kv_norm": None, "w_kv": 0, "b_norm_pre": None,
    "b_w_q": 1, "b_sinks": None, "b_w_o": 1, "b_norm_post": None, "f_norm_pre": None, "f_w_up": 2, "f_conv_w": 2,
    "f_conv_b": None, "f_w_down": 1, "f_norm_post": None,
}
BIG = ["a_w_in", "a_w_out", "w_kv", "b_w_q", "b_w_o", "f_w_up", "f_w_down"]
SMALL = [n for n in WEIGHTS if n not in BIG]
SMALL_SHARDED = [n for n in SMALL if SHARD_AXIS[n] is not None]


def _shard_shape(name):
    shape = list(FULL_SHAPE[name])
    if SHARD_AXIS[name] is not None:
        shape[SHARD_AXIS[name]] //= N_CHIPS
    return tuple(shape)


def _numel(shape):
    return int(math.prod(shape))


SUBLANES = 8


def _packed_rows(shape):
    rows = -(-_numel(shape) // LANES)
    return -(-rows // SUBLANES) * SUBLANES


def _pack(arrays):
    parts = []
    for a in arrays:
        size, rows = _numel(a.shape), _packed_rows(a.shape)
        if size % LANES == 0:
            part = jnp.pad(a.reshape(size // LANES, LANES), ((0, rows - size // LANES), (0, 0)))
        else:
            part = jnp.pad(a.reshape(-1), (0, rows * LANES - size)).reshape(rows, LANES)
        parts.append(part)
    return jnp.concatenate(parts, axis=0)


def _unpack(packed, names, shape_of):
    out, off = {}, 0
    lead = packed.shape[:-2]
    for n in names:
        shape = tuple(shape_of(n))
        size, rows = _numel(shape), _packed_rows(shape)
        part = packed[..., off:off + rows, :]
        if size % LANES == 0:
            out[n] = part[..., :size // LANES, :].reshape(lead + shape)
        else:
            out[n] = part.reshape(lead + (rows * LANES,))[..., :size].reshape(lead + shape)
        off += rows
    return out


def _split_chips(name, full):
    ax = SHARD_AXIS[name]
    shape = full.shape
    cut = shape[:ax] + (N_CHIPS, shape[ax] // N_CHIPS) + shape[ax + 1:]
    return jnp.moveaxis(full.reshape(cut), ax, 0)


def _join_chips(name, stacked):
    ax = SHARD_AXIS[name]
    moved = jnp.moveaxis(stacked, 0, ax)
    shape = moved.shape
    return moved.reshape(shape[:ax] + (shape[ax] * shape[ax + 1],) + shape[ax + 2:])


def _as2d(a):
    return a.reshape(-1, a.shape[-1])


def _full_from_gathered(name, buf):
    stacked = buf.reshape((N_CHIPS,) + _shard_shape(name))
    if name in ("f_w_up", "f_w_down"):
        return [_join_chips(name, stacked[:, i:i + 1])[0] for i in range(2)]
    joined = _join_chips(name, stacked)
    return joined[0] if joined.ndim == 3 else joined


def _gathered_from_full(name, g):
    if name in ("f_w_up", "f_w_down"):
        parts = [_split_chips(name, gi[None]) for gi in g]
        stacked = jnp.concatenate(parts, axis=1)
    else:
        stacked = _split_chips(name, g.reshape(FULL_SHAPE[name]))
    return stacked.reshape((N_CHIPS,) + _shard2d(name)).astype(BF16)


def _shard2d(name):
    shape = _shard_shape(name)
    return (_numel(shape[:-1]), shape[-1])


def kernel(x, meta_tokens, a_norm_pre, a_w_in, a_conv_w, a_conv_b, a_dt_bias, a_a_log, a_d_skip, a_gate_norm, a_w_out, a_norm_post, kv_norm, w_kv, b_norm_pre, b_w_q, b_sinks, b_w_o, b_norm_post, f_norm_pre, f_w_up, f_conv_w, f_conv_b, f_w_down, f_norm_post, loss_target, m_meta_tokens, m_a_norm_pre, m_a_w_in, m_a_conv_w, m_a_conv_b, m_a_dt_bias, m_a_a_log, m_a_d_skip, m_a_gate_norm, m_a_w_out, m_a_norm_post, m_kv_norm, m_w_kv, m_b_norm_pre, m_b_w_q, m_b_sinks, m_b_w_o, m_b_norm_post, m_f_norm_pre, m_f_w_up, m_f_conv_w, m_f_conv_b, m_f_w_down, m_f_norm_post, v_meta_tokens, v_a_norm_pre, v_a_w_in, v_a_conv_w, v_a_conv_b, v_a_dt_bias, v_a_a_log, v_a_d_skip, v_a_gate_norm, v_a_w_out, v_a_norm_post, v_kv_norm, v_w_kv, v_b_norm_pre, v_b_w_q, v_b_sinks, v_b_w_o, v_b_norm_post, v_f_norm_pre, v_f_w_up, v_f_conv_w, v_f_conv_b, v_f_w_down, v_f_norm_post):
    given = dict(locals())
    w = {n: given[n] for n in WEIGHTS}
    mom = {n: given["m_" + n] for n in WEIGHTS}
    var = {n: given["v_" + n] for n in WEIGHTS}
    chip = 2 * lax.axis_index("x") + lax.axis_index("y")
    core = lax.axis_index("c")
    place = jnp.stack([chip, core]).astype(jnp.int32)

    small_all = _allgather_small("gather_small", _pack([w[n] for n in SMALL_SHARDED]))
    small_parts = _unpack(small_all, SMALL_SHARDED, _shard_shape)
    slots = [_cast_into_slot("cast_" + n, place, _as2d(w[n])) for n in BIG]
    gathered = _allgather_big("gather_big", slots)
    p = {}
    for n in WEIGHTS:
        if n in BIG:
            p[n] = _full_from_gathered(n, gathered[BIG.index(n)])
        elif n in SMALL_SHARDED:
            p[n] = _join_chips(n, small_parts[n])
        else:
            p[n] = w[n]
    p["a_conv_w"] = p["a_conv_w"][0]
    p["kv_norm"] = p["kv_norm"].reshape(1, D_MODEL)

    loss_local, grad_x, g = _local_step(x[0], loss_target[0], p)
    loss = lax.psum(loss_local, ("x", "y", "c"))

    small_sum = _allreduce_small("reduce_small", _pack([g[n].reshape(FULL_SHAPE[n]) for n in SMALL]))
    small_red = _unpack(small_sum, SMALL, lambda n: FULL_SHAPE[n])
    grads = {}
    for n in SMALL:
        if SHARD_AXIS[n] is None:
            grads[n] = small_red[n]
        else:
            grads[n] = lax.dynamic_index_in_dim(_split_chips(n, small_red[n]), chip, 0, keepdims=False)

    by_chip = [_gathered_from_full(n, g[n]) for n in BIG]
    partner = _rs_pair_exchange("reduce_pair_send", by_chip)
    pair_sum = [_rs_pair_add("reduce_pair_add_" + n, place, gk, pk) for n, gk, pk in zip(BIG, by_chip, partner)]
    from_chips = _rs_chip_exchange("reduce_chip_send", pair_sum)
    half_sum = [_rs_chip_add("reduce_chip_add_" + n, place, qk, rk) for n, qk, rk in zip(BIG, pair_sum, from_chips)]
    shard_sum = _rs_pair_gather("reduce_pair_gather", half_sum)
    for n, s in zip(BIG, shard_sum):
        grads[n] = s.reshape(_shard_shape(n))

    delta, new_m, new_v = {}, {}, {}
    for n in BIG:
        shape = _shard_shape(n)
        d, m2, v2 = _adamw("adamw_" + n, _as2d(w[n]), _as2d(grads[n]), _as2d(mom[n]), _as2d(var[n]))
        delta[n], new_m[n], new_v[n] = d.reshape(shape), m2.reshape(shape), v2.reshape(shape)
    packed = [_pack([src[n].reshape(_shard_shape(n)) for n in SMALL]) for src in (w, grads, mom, var)]
    outs = _adamw("adamw_small", *packed)
    for dst, flat in zip((delta, new_m, new_v), outs):
        dst.update(_unpack(flat, SMALL, _shard_shape))

    return (loss, grad_x[None], *[grads[n].reshape(_shard_shape(n)) for n in WEIGHTS],
            *[delta[n] for n in WEIGHTS], *[new_m[n] for n in WEIGHTS], *[new_v[n] for n in WEIGHTS])
```

```python
import functools
import math

import jax
import jax.numpy as jnp
from jax import lax
from jax.experimental import pallas as pl
from jax.experimental.pallas import tpu as pltpu

F32, BF16 = jnp.float32, jnp.bfloat16
MESH = pl.DeviceIdType.MESH

D_MODEL = 1024
N_META = 16
CHUNK = 128
PAD_ROWS = CHUNK - N_META
D_INNER = 2048
D_STATE = 128
N_GROUPS = 4
HEADS_PER_GROUP = 8
SSM_HEADS = 32
HEAD_DIM = 64
D_BC = N_GROUPS * D_STATE
D_XBC = D_INNER + 2 * D_BC
D_MAIN = D_INNER + D_XBC
D_IN_PROJ = D_MAIN + SSM_HEADS
GROUP_W = HEADS_PER_GROUP * HEAD_DIM
SSM_CONV = 4
D_FF = 2816
FFN_CONV = 3
N_Q_HEADS = 16
N_KV_HEADS = 4
D_KV = 256
ATTN_SCALE = 1.0 / math.sqrt(HEAD_DIM)
RMS_EPS = 1e-6
NEG_INF = -1e30
LANES = 128
VMEM_LIMIT = 48 * 1024 * 1024

ADAM_LR, ADAM_B1, ADAM_B2, ADAM_EPS, ADAM_WD, ADAM_STEP = 0.001, 0.9, 0.999, 1e-08, 0.01, 10

N_CHIPS = 4
N_DEV = 8


def _cparams(sem=None):
    return pltpu.CompilerParams(dimension_semantics=sem, vmem_limit_bytes=VMEM_LIMIT)


def _tile(n, cands=(512, 256, 128)):
    for t in cands:
        if n % t == 0:
            return t
    return n


def _row_tile(rows, width):
    for t in (544, 272):
        if rows % t == 0 and t * width * 4 <= (3 << 20):
            return t
    return 128


def _rows_mask(i, tm):
    rows = i * tm + lax.broadcasted_iota(jnp.int32, (tm, 1), 0)
    return rows >= PAD_ROWS


def _dot(a, b):
    return jnp.dot(a, b, preferred_element_type=F32)


def _dot_nt(a, b):
    return lax.dot_general(a, b, (((1,), (1,)), ((), ())), preferred_element_type=F32)


def _dot_tn(a, b):
    return lax.dot_general(a, b, (((0,), (0,)), ((), ())), preferred_element_type=F32)


def _sigmoid(x):
    return 1.0 / (1.0 + jnp.exp(-x))


def _mm(name, a, b, mode, out_dtype=F32, acc=None, b_colblock=0, n_cols=None):
    resident_bytes = 8 << 20
    if mode == "nn":
        m, k = a.shape
        n = n_cols or b.shape[1]
        tm = m
        while tm * k * 2 > resident_bytes and tm % 32 == 0:
            tm //= 2
        tn = _tile(n)
        grid = (m // tm, n // tn)
        in_specs = [pl.BlockSpec((tm, k), lambda i, j: (i, 0)), pl.BlockSpec((k, tn), lambda i, j: (0, j))]
        out_shape, out_block = (m, n), (tm, tn)
    elif mode == "nt":
        m, n = a.shape
        k = b.shape[0]
        tm = m
        while tm * n * 2 > resident_bytes and tm % 32 == 0:
            tm //= 2
        tk = _tile(k)
        grid = (m // tm, k // tk)
        in_specs = [pl.BlockSpec((tm, n), lambda i, j: (i, 0)), pl.BlockSpec((tk, n), lambda i, j: (j, b_colblock))]
        out_shape, out_block = (m, k), (tm, tk)
    else:
        m, k = a.shape
        n = b.shape[1]
        tk, tn = _tile(k), _tile(n)
        grid = (k // tk, n // tn)
        in_specs = [pl.BlockSpec((m, tk), lambda i, j: (0, i)), pl.BlockSpec((m, tn), lambda i, j: (0, j))]
        out_shape, out_block = (k, n), (tk, tn)
    out_spec = pl.BlockSpec(out_block, lambda i, j: (i, j))
    has_acc = acc is not None

    def body(*refs):
        a_ref, b_ref = refs[0], refs[1]
        o_ref = refs[-1]
        av, bv = a_ref[...], b_ref[...]
        if mode == "nn":
            r = _dot(av, bv)
        elif mode == "nt":
            r = _dot_nt(av, bv)
        else:
            r = _dot_tn(av, bv)
        if has_acc:
            r = r + refs[2][...]
        o_ref[...] = r.astype(o_ref.dtype)

    operands = [a, b]
    if has_acc:
        in_specs = in_specs + [out_spec]
        operands.append(acc)
    return pl.pallas_call(
        body, name=name, out_shape=jax.ShapeDtypeStruct(out_shape, out_dtype), grid=grid,
        in_specs=in_specs, out_specs=out_spec, compiler_params=_cparams(("parallel", "parallel")),
    )(*operands)


def _fit_rows(m, row_bytes, budget=8 << 20):
    tm = m
    while tm * row_bytes > budget and tm % 32 == 0:
        tm //= 2
    return tm


def _mm_nn_bychip(name, a, bc):
    m, k = a.shape
    n = bc.shape[2]
    tm = min(_fit_rows(m, k * 2), _fit_rows(m, n * 4))

    def body(a_ref, b_ref, o_ref):
        o_ref[...] = _dot(a_ref[...], b_ref[...])

    return pl.pallas_call(
        body, name=name, out_shape=jax.ShapeDtypeStruct((m, N_CHIPS * n), F32), grid=(m // tm, N_CHIPS),
        in_specs=[pl.BlockSpec((tm, k), lambda i, c: (i, 0)), pl.BlockSpec((None, k, n), lambda i, c: (c, 0, 0))],
        out_specs=pl.BlockSpec((tm, n), lambda i, c: (i, c)), compiler_params=_cparams(("parallel", "parallel")),
    )(a, bc)


def _mm_nt_bychip(name, a, bc, chip0, acc=None):
    m = a.shape[0]
    _, k, n = bc.shape
    nch = a.shape[1] // n
    tm, tk = _fit_rows(m, n * 2), _tile(k)
    has_acc = acc is not None

    def body(*refs):
        a_ref, b_ref, o_ref = refs[0], refs[1], refs[-1]

        @pl.when(pl.program_id(2) == 0)
        def _():
            o_ref[...] = refs[2][...] if has_acc else jnp.zeros_like(o_ref)

        o_ref[...] += _dot_nt(a_ref[...], b_ref[...])

    out_spec = pl.BlockSpec((tm, tk), lambda i, j, c: (i, j))
    in_specs = [pl.BlockSpec((tm, n), lambda i, j, c: (i, c)),
                pl.BlockSpec((None, tk, n), lambda i, j, c: (chip0 + c, j, 0))]
    operands = [a, bc]
    if has_acc:
        in_specs.append(out_spec)
        operands.append(acc)
    return pl.pallas_call(
        body, name=name, out_shape=jax.ShapeDtypeStruct((m, k), F32), grid=(m // tm, k // tk, nch),
        in_specs=in_specs, out_specs=out_spec, compiler_params=_cparams(("parallel", "parallel", "arbitrary")),
    )(*operands)


def _mm_tn_bychip(name, a, dy, n, chip0, into=None):
    m, k = a.shape
    nch = dy.shape[1] // n
    tk = _tile(k)

    def body(*refs):
        a_ref, d_ref, o_ref = refs[0], refs[1], refs[-1]
        o_ref[...] = _dot_tn(a_ref[...], d_ref[...]).astype(BF16)

    in_specs = [pl.BlockSpec((m, tk), lambda i, c: (0, i)), pl.BlockSpec((m, n), lambda i, c: (0, c))]
    operands = [a, dy]
    aliases = {}
    if into is not None:
        in_specs.append(pl.BlockSpec(memory_space=pl.ANY))
        operands.append(into)
        aliases = {2: 0}
    return pl.pallas_call(
        body, name=name, out_shape=jax.ShapeDtypeStruct((N_CHIPS, k, n), BF16), grid=(k // tk, nch),
        in_specs=in_specs, out_specs=pl.BlockSpec((None, tk, n), lambda i, c: (chip0 + c, i, 0)),
        input_output_aliases=aliases, compiler_params=_cparams(("parallel", "parallel")),
    )(*operands)


def _rms_fwd(name, h, w):
    rows, width = h.shape
    tm = _row_tile(rows, width)

    def body(h_ref, w_ref, o_ref):
        x = h_ref[...]
        r = lax.rsqrt(jnp.mean(x * x, axis=-1, keepdims=True) + RMS_EPS)
        o_ref[...] = (x * r * w_ref[...]).astype(BF16)

    return pl.pallas_call(
        body, name=name, out_shape=jax.ShapeDtypeStruct((rows, width), BF16), grid=(rows // tm,),
        in_specs=[pl.BlockSpec((tm, width), lambda i: (i, 0)), pl.BlockSpec((1, width), lambda i: (0, 0))],
        out_specs=pl.BlockSpec((tm, width), lambda i: (i, 0)), compiler_params=_cparams(("parallel",)),
    )(h, w)


def _resid_norm_fwd(name, h, pre, w):
    rows, width = h.shape
    tm = _row_tile(rows, width)

    def body(h_ref, p_ref, w_ref, o_ref):
        p = p_ref[...]
        r = lax.rsqrt(jnp.mean(p * p, axis=-1, keepdims=True) + RMS_EPS)
        o_ref[...] = h_ref[...] + jnp.where(_rows_mask(pl.program_id(0), tm), p * r * w_ref[...], 0.0)

    row_spec = pl.BlockSpec((tm, width), lambda i: (i, 0))
    return pl.pallas_call(
        body, name=name, out_shape=jax.ShapeDtypeStruct((rows, width), F32), grid=(rows // tm,),
        in_specs=[row_spec, row_spec, pl.BlockSpec((1, width), lambda i: (0, 0))],
        out_specs=row_spec, compiler_params=_cparams(("parallel",)),
    )(h, pre, w)


def _resid_norm_bwd(name, dh, pre, w):
    rows, width = dh.shape
    tm = _row_tile(rows, width)

    def body(dh_ref, p_ref, w_ref, dp_ref, dw_ref):
        i = pl.program_id(0)
        dy = jnp.where(_rows_mask(i, tm), dh_ref[...], 0.0)
        p = p_ref[...]
        r = lax.rsqrt(jnp.mean(p * p, axis=-1, keepdims=True) + RMS_EPS)
        xhat = p * r
        dxhat = dy * w_ref[...]
        dp = r * (dxhat - xhat * jnp.mean(dxhat * xhat, axis=-1, keepdims=True))
        dp_ref[...] = dp.astype(BF16)

        @pl.when(i == 0)
        def _():
            dw_ref[...] = jnp.zeros_like(dw_ref)

        dw_ref[...] += jnp.sum(dy * xhat, axis=0, keepdims=True)

    row_spec = pl.BlockSpec((tm, width), lambda i: (i, 0))
    vec_spec = pl.BlockSpec((1, width), lambda i: (0, 0))
    return pl.pallas_call(
        body, name=name,
        out_shape=(jax.ShapeDtypeStruct((rows, width), BF16), jax.ShapeDtypeStruct((1, width), F32)),
        grid=(rows // tm,), in_specs=[row_spec, row_spec, vec_spec], out_specs=(row_spec, vec_spec),
        compiler_params=_cparams(("arbitrary",)),
    )(dh, pre, w)


def _norm_bwd_add(name, dh, dhn, h, w):
    rows, width = dh.shape
    tm = _row_tile(rows, width)

    def body(dh_ref, dhn_ref, h_ref, w_ref, o_ref, dw_ref):
        i = pl.program_id(0)
        x = h_ref[...]
        dy = dhn_ref[...]
        r = lax.rsqrt(jnp.mean(x * x, axis=-1, keepdims=True) + RMS_EPS)
        xhat = x * r
        dxhat = dy * w_ref[...]
        dx = r * (dxhat - xhat * jnp.mean(dxhat * xhat, axis=-1, keepdims=True))
        o_ref[...] = dh_ref[...] + jnp.where(_rows_mask(i, tm), dx, 0.0)

        @pl.when(i == 0)
        def _():
            dw_ref[...] = jnp.zeros_like(dw_ref)

        dw_ref[...] += jnp.sum(dy * xhat, axis=0, keepdims=True)

    row_spec = pl.BlockSpec((tm, width), lambda i: (i, 0))
    vec_spec = pl.BlockSpec((1, width), lambda i: (0, 0))
    return pl.pallas_call(
        body, name=name,
        out_shape=(jax.ShapeDtypeStruct((rows, width), F32), jax.ShapeDtypeStruct((1, width), F32)),
        grid=(rows // tm,), in_specs=[row_spec, row_spec, row_spec, vec_spec], out_specs=(row_spec, vec_spec),
        compiler_params=_cparams(("arbitrary",)),
    )(dh, dhn, h, w)


def _shift_down(x, s, rows):
    return pltpu.roll(x, s, 0) if s else x


def _shift_up(x, s, rows):
    return pltpu.roll(x, rows - s, 0) if s else x


def _conv4_fwd(name, zx, cw, cb):
    rows = zx.shape[0]
    off = D_INNER // LANES

    def body(x_ref, w_ref, b_ref, o_ref):
        x = x_ref[...]
        acc = b_ref[...] + w_ref[pl.ds(SSM_CONV - 1, 1), :] * x
        for s in range(1, SSM_CONV):
            acc = acc + w_ref[pl.ds(SSM_CONV - 1 - s, 1), :] * _shift_down(x, s, rows)
        valid = lax.broadcasted_iota(jnp.int32, (rows, 1), 0) >= PAD_ROWS
        o_ref[...] = jnp.where(valid, acc * _sigmoid(acc), 0.0)

    return pl.pallas_call(
        body, name=name, out_shape=jax.ShapeDtypeStruct((rows, D_XBC), F32), grid=(D_XBC // LANES,),
        in_specs=[pl.BlockSpec((rows, LANES), lambda j: (0, j + off)),
                  pl.BlockSpec((SSM_CONV, LANES), lambda j: (0, j)),
                  pl.BlockSpec((1, LANES), lambda j: (0, j))],
        out_specs=pl.BlockSpec((rows, LANES), lambda j: (0, j)), compiler_params=_cparams(("parallel",)),
    )(zx, cw, cb)


def _conv4_bwd(name, zx, dout, cw, cb, col0):
    rows, width = dout.shape
    zoff = (D_INNER + col0) // LANES
    woff = col0 // LANES

    def body(x_ref, d_ref, w_ref, b_ref, dx_ref, dw_ref, db_ref):
        x = x_ref[...]
        shifted = [_shift_down(x, s, rows) for s in range(SSM_CONV)]
        acc = b_ref[...]
        for s in range(SSM_CONV):
            acc = acc + w_ref[pl.ds(SSM_CONV - 1 - s, 1), :] * shifted[s]
        sig = _sigmoid(acc)
        valid = lax.broadcasted_iota(jnp.int32, (rows, 1), 0) >= PAD_ROWS
        dpre = jnp.where(valid, d_ref[...] * sig * (1.0 + acc * (1.0 - sig)), 0.0)
        dx = w_ref[pl.ds(SSM_CONV - 1, 1), :] * dpre
        for s in range(1, SSM_CONV):
            dx = dx + w_ref[pl.ds(SSM_CONV - 1 - s, 1), :] * _shift_up(dpre, s, rows)
        dx_ref[...] = dx.astype(BF16)
        for s in range(SSM_CONV):
            dw_ref[pl.ds(SSM_CONV - 1 - s, 1), :] = jnp.sum(dpre * shifted[s], axis=0, keepdims=True)
        db_ref[...] = jnp.sum(dpre, axis=0, keepdims=True)

    return pl.pallas_call(
        body, name=name,
        out_shape=(jax.ShapeDtypeStruct((rows, width), BF16), jax.ShapeDtypeStruct((SSM_CONV, width), F32),
                   jax.ShapeDtypeStruct((1, width), F32)),
        grid=(width // LANES,),
        in_specs=[pl.BlockSpec((rows, LANES), lambda j: (0, j + zoff)),
                  pl.BlockSpec((rows, LANES), lambda j: (0, j)),
                  pl.BlockSpec((SSM_CONV, LANES), lambda j: (0, j + woff)),
                  pl.BlockSpec((1, LANES), lambda j: (0, j + woff))],
        out_specs=(pl.BlockSpec((rows, LANES), lambda j: (0, j)),
                   pl.BlockSpec((SSM_CONV, LANES), lambda j: (0, j)),
                   pl.BlockSpec((1, LANES), lambda j: (0, j))),
        compiler_params=_cparams(("parallel",)),
    )(zx, dout, cw, cb)


def _ffn_conv_fwd(name, up, cw, cb):
    rows = up.shape[0]
    nt = D_FF // LANES

    def body(g_ref, v_ref, wg_ref, wv_ref, bg_ref, bv_ref, o_ref):
        g, v = g_ref[...], v_ref[...]
        ug, uv = bg_ref[...], bv_ref[...]
        for s in range(FFN_CONV):
            ug = ug + wg_ref[pl.ds(FFN_CONV - 1 - s, 1), :] * _shift_down(g, s, rows)
            uv = uv + wv_ref[pl.ds(FFN_CONV - 1 - s, 1), :] * _shift_down(v, s, rows)
        valid = lax.broadcasted_iota(jnp.int32, (rows, 1), 0) >= PAD_ROWS
        o_ref[...] = jnp.where(valid, ug * _sigmoid(ug) * uv, 0.0).astype(BF16)

    col = lambda shift: pl.BlockSpec((rows, LANES), lambda j: (0, j + shift))
    wsp = lambda shift: pl.BlockSpec((FFN_CONV, LANES), lambda j: (0, j + shift))
    bsp = lambda shift: pl.BlockSpec((1, LANES), lambda j: (0, j + shift))
    return pl.pallas_call(
        body, name=name, out_shape=jax.ShapeDtypeStruct((rows, D_FF), BF16), grid=(nt,),
        in_specs=[col(0), col(nt), wsp(0), wsp(nt), bsp(0), bsp(nt)],
        out_specs=pl.BlockSpec((rows, LANES), lambda j: (0, j)), compiler_params=_cparams(("parallel",)),
    )(up, up, cw, cw, cb, cb)


def _ffn_conv_bwd(name, up, dact, cw, cb):
    rows = up.shape[0]
    nt = D_FF // LANES

    def body(g_ref, v_ref, d_ref, wg_ref, wv_ref, bg_ref, bv_ref, dxg_ref, dxv_ref, dwg_ref, dwv_ref, dbg_ref, dbv_ref):
        g, v = g_ref[...], v_ref[...]
        gs = [_shift_down(g, s, rows) for s in range(FFN_CONV)]
        vs = [_shift_down(v, s, rows) for s in range(FFN_CONV)]
        ug, uv = bg_ref[...], bv_ref[...]
        for s in range(FFN_CONV):
            ug = ug + wg_ref[pl.ds(FFN_CONV - 1 - s, 1), :] * gs[s]
            uv = uv + wv_ref[pl.ds(FFN_CONV - 1 - s, 1), :] * vs[s]
        sig = _sigmoid(ug)
        valid = lax.broadcasted_iota(jnp.int32, (rows, 1), 0) >= PAD_ROWS
        d = jnp.where(valid, d_ref[...], 0.0)
        dsig = d * sig
        for dpre, src, w_ref, dx_ref, dw_ref, db_ref in (
                (dsig * uv * (1.0 + ug * (1.0 - sig)), gs, wg_ref, dxg_ref, dwg_ref, dbg_ref),
                (dsig * ug, vs, wv_ref, dxv_ref, dwv_ref, dbv_ref)):
            dx = w_ref[pl.ds(FFN_CONV - 1, 1), :] * dpre
            for s in range(1, FFN_CONV):
                dx = dx + w_ref[pl.ds(FFN_CONV - 1 - s, 1), :] * _shift_up(dpre, s, rows)
            dx_ref[...] = dx.astype(BF16)
            for s in range(FFN_CONV):
                dw_ref[pl.ds(FFN_CONV - 1 - s, 1), :] = jnp.sum(dpre * src[s], axis=0, keepdims=True)
            db_ref[...] = jnp.sum(dpre, axis=0, keepdims=True)

    col = lambda shift: pl.BlockSpec((rows, LANES), lambda j: (0, j + shift))
    wsp = lambda shift: pl.BlockSpec((FFN_CONV, LANES), lambda j: (0, j + shift))
    bsp = lambda shift: pl.BlockSpec((1, LANES), lambda j: (0, j + shift))
    dx_shape = jax.ShapeDtypeStruct((rows, D_FF), BF16)
    dw_shape = jax.ShapeDtypeStruct((FFN_CONV, D_FF), F32)
    db_shape = jax.ShapeDtypeStruct((1, D_FF), F32)
    return pl.pallas_call(
        body, name=name, out_shape=(dx_shape, dx_shape, dw_shape, dw_shape, db_shape, db_shape), grid=(nt,),
        in_specs=[col(0), col(nt), col(0), wsp(0), wsp(nt), bsp(0), bsp(nt)],
        out_specs=(col(0), col(0), wsp(0), wsp(0), bsp(0), bsp(0)),
        compiler_params=_cparams(("parallel",)),
    )(up, up, dact, cw, cw, cb, cb)


def _dt_fwd(name, dtr, bias):
    rows = dtr.shape[0]
    tm = _row_tile(rows, LANES)

    def body(d_ref, b_ref, o_ref):
        v = d_ref[...] + b_ref[...]
        sp = jnp.maximum(v, 0.0) + jnp.log1p(jnp.exp(-jnp.abs(v)))
        lane = lax.broadcasted_iota(jnp.int32, (tm, LANES), 1)
        ok = _rows_mask(pl.program_id(0), tm) & (lane < SSM_HEADS)
        o_ref[...] = jnp.where(ok, sp, 0.0)

    return pl.pallas_call(
        body, name=name, out_shape=jax.ShapeDtypeStruct((rows, LANES), F32), grid=(rows // tm,),
        in_specs=[pl.BlockSpec((tm, LANES), lambda i: (i, 0)), pl.BlockSpec((1, LANES), lambda i: (0, 0))],
        out_specs=pl.BlockSpec((tm, LANES), lambda i: (i, 0)), compiler_params=_cparams(("parallel",)),
    )(dtr, bias)


def _dt_bwd(name, ddt, dtr, bias):
    rows = dtr.shape[0]
    tm = _row_tile(rows, LANES)

    def body(g_ref, d_ref, b_ref, o_ref, db_ref):
        i = pl.program_id(0)
        lane = lax.broadcasted_iota(jnp.int32, (tm, LANES), 1)
        ok = _rows_mask(i, tm) & (lane < SSM_HEADS)
        dv = jnp.where(ok, g_ref[...] * _sigmoid(d_ref[...] + b_ref[...]), 0.0)
        o_ref[...] = dv.astype(BF16)

        @pl.when(i == 0)
        def _():
            db_ref[...] = jnp.zeros_like(db_ref)

        db_ref[...] += jnp.sum(dv, axis=0, keepdims=True)

    row_spec = pl.BlockSpec((tm, LANES), lambda i: (i, 0))
    vec_spec = pl.BlockSpec((1, LANES), lambda i: (0, 0))
    return pl.pallas_call(
        body, name=name,
        out_shape=(jax.ShapeDtypeStruct((rows, LANES), BF16), jax.ShapeDtypeStruct((1, LANES), F32)),
        grid=(rows // tm,), in_specs=[row_spec, row_spec, vec_spec], out_specs=(row_spec, vec_spec),
        compiler_params=_cparams(("arbitrary",)),
    )(ddt, dtr, bias)


def _gate_fwd(name, y, zx, w):
    rows = y.shape[0]
    tm = _row_tile(rows, D_INNER)

    def body(y_ref, z_ref, w_ref, o_ref):
        z = z_ref[...]
        g = y_ref[...] * (z * _sigmoid(z))
        r = lax.rsqrt(jnp.mean(g * g, axis=-1, keepdims=True) + RMS_EPS)
        o_ref[...] = (g * r * w_ref[...]).astype(BF16)

    row_spec = pl.BlockSpec((tm, D_INNER), lambda i: (i, 0))
    return pl.pallas_call(
        body, name=name, out_shape=jax.ShapeDtypeStruct((rows, D_INNER), BF16), grid=(rows // tm,),
        in_specs=[row_spec, row_spec, pl.BlockSpec((1, D_INNER), lambda i: (0, 0))],
        out_specs=row_spec, compiler_params=_cparams(("parallel",)),
    )(y, zx, w)


def _gate_bwd(name, dyn, y, zx, w):
    rows = y.shape[0]
    tm = _row_tile(rows, D_INNER)

    def body(d_ref, y_ref, z_ref, w_ref, dy_ref, dz_ref, dw_ref):
        i = pl.program_id(0)
        z, yv = z_ref[...], y_ref[...]
        sig = _sigmoid(z)
        sz = z * sig
        g = yv * sz
        r = lax.rsqrt(jnp.mean(g * g, axis=-1, keepdims=True) + RMS_EPS)
        ghat = g * r
        dn = d_ref[...]
        dghat = dn * w_ref[...]
        dg = r * (dghat - ghat * jnp.mean(dghat * ghat, axis=-1, keepdims=True))
        dy_ref[...] = dg * sz
        dz_ref[...] = (dg * yv * sig * (1.0 + z * (1.0 - sig))).astype(BF16)

        @pl.when(i == 0)
        def _():
            dw_ref[...] = jnp.zeros_like(dw_ref)

        dw_ref[...] += jnp.sum(dn * ghat, axis=0, keepdims=True)

    row_spec = pl.BlockSpec((tm, D_INNER), lambda i: (i, 0))
    vec_spec = pl.BlockSpec((1, D_INNER), lambda i: (0, 0))
    return pl.pallas_call(
        body, name=name,
        out_shape=(jax.ShapeDtypeStruct((rows, D_INNER), F32), jax.ShapeDtypeStruct((rows, D_INNER), BF16),
                   jax.ShapeDtypeStruct((1, D_INNER), F32)),
        grid=(rows // tm,), in_specs=[row_spec, row_spec, row_spec, vec_spec],
        out_specs=(row_spec, row_spec, vec_spec), compiler_params=_cparams(("arbitrary",)),
    )(dyn, y, zx, w)


def _split3(x):
    hi = x.astype(BF16)
    r1 = x - hi.astype(F32)
    mid = r1.astype(BF16)
    lo = (r1 - mid.astype(F32)).astype(BF16)
    return hi, mid, lo


def _dot3_data_lhs(x, sel):
    sel16 = sel.astype(F32).astype(BF16)
    hi, mid, lo = _split3(x)
    return _dot(hi, sel16) + _dot(mid, sel16) + _dot(lo, sel16)


def _dot3_data_rhs(sel, x):
    sel16 = sel.astype(F32).astype(BF16)
    hi, mid, lo = _split3(x)
    return _dot(sel16, hi) + _dot(sel16, mid) + _dot(sel16, lo)


def _causal_masks():
    r = lax.broadcasted_iota(jnp.int32, (CHUNK, CHUNK), 0)
    c = lax.broadcasted_iota(jnp.int32, (CHUNK, CHUNK), 1)
    return r >= c, r <= c


def _expand_heads_matrix():
    k = lax.broadcasted_iota(jnp.int32, (LANES, GROUP_W), 0)
    j = lax.broadcasted_iota(jnp.int32, (LANES, GROUP_W), 1)
    return jnp.right_shift(j, 6) == k


def _reduce_heads_matrix():
    j = lax.broadcasted_iota(jnp.int32, (GROUP_W, LANES), 0)
    k = lax.broadcasted_iota(jnp.int32, (GROUP_W, LANES), 1)
    return jnp.right_shift(j, 6) == k


def _reduce_pair_matrix(p):
    j = lax.broadcasted_iota(jnp.int32, (LANES, LANES), 0)
    k = lax.broadcasted_iota(jnp.int32, (LANES, LANES), 1)
    return (2 * p + jnp.right_shift(j, 6)) == k


def _ssd_prep(name, dt4, a128):
    rows = dt4.shape[1]
    nc = rows // CHUNK

    def body(dt_ref, a_ref, dte_ref, acs_ref):
        causal, _ = _causal_masks()
        expand = _expand_heads_matrix()
        dt = dt_ref[...]
        acs = _dot3_data_rhs(causal, dt) * a_ref[...]
        dte_ref[...] = _dot3_data_lhs(dt, expand)
        acs_ref[...] = _dot3_data_lhs(acs, expand)

    blk = pl.BlockSpec((CHUNK, GROUP_W), lambda g, c: (c, g))
    shp = jax.ShapeDtypeStruct((rows, D_INNER), F32)
    return pl.pallas_call(
        body, name=name, out_shape=(shp, shp), grid=(N_GROUPS, nc),
        in_specs=[pl.BlockSpec((None, CHUNK, LANES), lambda g, c: (g, c, 0)),
                  pl.BlockSpec((None, 1, LANES), lambda g, c: (g, 0, 0))],
        out_specs=(blk, blk), compiler_params=_cparams(("parallel", "parallel")),
    )(dt4, a128)


def _ssd_common(x_ref, b_ref, c_ref, dte_ref, acs_ref):
    x = x_ref[...]
    dt_exp = dte_ref[...]
    acs_exp = acs_ref[...]
    tot_exp = acs_ref[pl.ds(CHUNK - 1, 1), :]
    xdt = x * dt_exp
    e_exp = jnp.exp(acs_exp)
    f_exp = jnp.exp(tot_exp - acs_exp)
    return _causal_masks(), x, dt_exp, acs_exp, tot_exp, xdt, e_exp, f_exp, b_ref[...], c_ref[...]


def _pair_decay(acs_pair, e, causal):
    lane = lax.broadcasted_iota(jnp.int32, (CHUNK, LANES), 1)
    mine = (lane < HEAD_DIM) if e == 0 else (lane >= HEAD_DIM)
    a_l = jnp.where(mine, acs_pair, pltpu.roll(acs_pair, HEAD_DIM, 1))
    seg = a_l - a_l.T
    dm = jnp.where(causal[0], jnp.exp(jnp.minimum(seg, 0.0)), 0.0)
    dmt = jnp.where(causal[1], jnp.exp(jnp.minimum(-seg, 0.0)), 0.0)
    return dm, dmt


def _ssd_fwd(name, xbc, dt_exp, acs_exp, dskexp):
    rows = xbc.shape[0]
    nc = rows // CHUNK
    bcol = D_INNER // LANES

    def body(x_ref, b_ref, c_ref, dte_ref, acs_ref, dsk_ref, y_ref, st_ref, s_scr):
        @pl.when(pl.program_id(1) == 0)
        def _():
            s_scr[...] = jnp.zeros_like(s_scr)

        causal, x, _, acs_exp_v, tot_exp, xdt, e_exp, f_exp, bm, cm = _ssd_common(x_ref, b_ref, c_ref, dte_ref, acs_ref)
        state = s_scr[...]
        st_ref[...] = state
        cb16, bb16 = cm.astype(BF16), bm.astype(BF16)
        cb = _dot_nt(cb16, bb16)
        base = e_exp * _dot(cb16, state.astype(BF16)) + dsk_ref[...] * x
        lane = lax.broadcasted_iota(jnp.int32, (CHUNK, LANES), 1)
        for p in range(HEADS_PER_GROUP // 2):
            sl = slice(p * LANES, (p + 1) * LANES)
            xp = xdt[:, sl].astype(BF16)
            yd = []
            for e in range(2):
                dm, _ = _pair_decay(acs_exp_v[:, sl], e, causal)
                yd.append(_dot((cb * dm).astype(BF16), xp))
            y_ref[:, sl] = jnp.where(lane < HEAD_DIM, yd[0], yd[1]) + base[:, sl]
        s_scr[...] = jnp.exp(tot_exp) * state + _dot(bm.T.astype(BF16), (f_exp * xdt).astype(BF16))

    blk = pl.BlockSpec((CHUNK, GROUP_W), lambda g, c: (c, g))
    return pl.pallas_call(
        body, name=name,
        out_shape=(jax.ShapeDtypeStruct((rows, D_INNER), F32),
                   jax.ShapeDtypeStruct((N_GROUPS, nc, D_STATE, GROUP_W), F32)),
        grid=(N_GROUPS, nc),
        in_specs=[blk,
                  pl.BlockSpec((CHUNK, LANES), lambda g, c: (c, bcol + g)),
                  pl.BlockSpec((CHUNK, LANES), lambda g, c: (c, bcol + N_GROUPS + g)),
                  blk, blk, pl.BlockSpec((None, 1, GROUP_W), lambda g, c: (g, 0, 0))],
        out_specs=(blk, pl.BlockSpec((None, None, D_STATE, GROUP_W), lambda g, c: (g, c, 0, 0))),
        scratch_shapes=[pltpu.VMEM((D_STATE, GROUP_W), F32)],
        compiler_params=_cparams(("parallel", "arbitrary")),
    )(xbc, xbc, xbc, dt_exp, acs_exp, dskexp)


def _ssd_bwd(name, xbc, dt_exp, acs_exp, dt4, a128, dskexp, dy, states):
    rows = xbc.shape[0]
    nc = rows // CHUNK
    bcol = D_INNER // LANES
    last = nc - 1

    def body(x_ref, b_ref, c_ref, dte_ref, acs_ref, dt_ref, a128_ref, dsk_ref, dy_ref, st_ref,
             dx_ref, db_ref, dc_ref, ddt_ref, dalog_ref, ddsk_ref, ds_scr):
        first = pl.program_id(1) == 0

        @pl.when(first)
        def _():
            ds_scr[...] = jnp.zeros_like(ds_scr)
            dalog_ref[...] = jnp.zeros_like(dalog_ref)
            ddsk_ref[...] = jnp.zeros_like(ddsk_ref)

        causal, x, dt_exp, acs_exp_v, tot_exp, xdt, e_exp, f_exp, bm, cm = _ssd_common(
            x_ref, b_ref, c_ref, dte_ref, acs_ref)
        dt = dt_ref[...]
        reduce_heads = _reduce_heads_matrix()
        state, dstate = st_ref[...], ds_scr[...]
        dyv = dy_ref[...]
        cb16, bb16 = cm.astype(BF16), bm.astype(BF16)
        s16, ds16 = state.astype(BF16), dstate.astype(BF16)
        cb = _dot_nt(cb16, bb16)
        cbt = _dot_nt(bb16, cb16)
        cs = _dot(cb16, s16)
        bds = _dot(bb16, ds16)
        edy = e_exp * dyv
        fx = f_exp * xdt
        dxdt_base = f_exp * bds
        dc_acc = _dot_nt(edy.astype(BF16), s16)
        db_acc = _dot_nt(fx.astype(BF16), ds16)
        ds_scr[...] = jnp.exp(tot_exp) * dstate + _dot(cm.T.astype(BF16), edy.astype(BF16))
        q = fx * bds
        dacs = _dot3_data_lhs(edy * cs - q, reduce_heads)
        dtot = jnp.sum(_dot3_data_lhs(q + jnp.exp(tot_exp) * dstate * state, reduce_heads), axis=0, keepdims=True)
        ddsk_ref[...] += jnp.sum(_dot3_data_lhs(dyv * x, reduce_heads), axis=0, keepdims=True)
        lane = lax.broadcasted_iota(jnp.int32, (CHUNK, LANES), 1)
        dcb = jnp.zeros((CHUNK, CHUNK), F32)
        dcbt = jnp.zeros((CHUNK, CHUNK), F32)
        ddt_x = jnp.zeros((CHUNK, LANES), F32)
        for p in range(HEADS_PER_GROUP // 2):
            sl = slice(p * LANES, (p + 1) * LANES)
            xp, dyp = xdt[:, sl], dyv[:, sl]
            xp16, dyp16 = xp.astype(BF16), dyp.astype(BF16)
            dxh = []
            for e in range(2):
                h = 2 * p + e
                mine = (lane < HEAD_DIM) if e == 0 else (lane >= HEAD_DIM)
                dm, dmt = _pair_decay(acs_exp_v[:, sl], e, causal)
                m, mt = cb * dm, cbt * dmt
                xh16 = jnp.where(mine, xp, 0.0).astype(BF16)
                dyh16 = jnp.where(mine, dyp, 0.0).astype(BF16)
                d_m = _dot_nt(dyh16, xp16)
                d_mt = _dot_nt(xh16, dyp16)
                dacs_h = (jnp.sum(d_m * m, axis=-1, keepdims=True)
                          - jnp.sum(d_mt * mt, axis=-1, keepdims=True))
                dacs = dacs + jnp.where(lane == h, dacs_h, 0.0)
                dcb = dcb + d_m * dm
                dcbt = dcbt + d_mt * dmt
                dxh.append(_dot(mt.astype(BF16), dyp16))
            dxdt = jnp.where(lane < HEAD_DIM, dxh[0], dxh[1]) + dxdt_base[:, sl]
            dx_ref[:, sl] = dxdt * dt_exp[:, sl] + dsk_ref[:, sl] * dyp
            ddt_x = ddt_x + _dot3_data_lhs(dxdt * x[:, sl], _reduce_pair_matrix(p))
        dc_ref[...] = dc_acc + _dot(dcb.astype(BF16), bb16)
        db_ref[...] = db_acc + _dot(dcbt.astype(BF16), cb16)
        row = lax.broadcasted_iota(jnp.int32, (CHUNK, LANES), 0)
        dacs = dacs + jnp.where(row == CHUNK - 1, dtot, 0.0)
        da = _dot3_data_rhs(causal[1], dacs)
        ddt_ref[...] = da * a128_ref[...] + ddt_x
        dalog_ref[...] += jnp.sum(da * dt, axis=0, keepdims=True) * a128_ref[...]

    vec = lambda w: pl.BlockSpec((None, 1, w), lambda g, c: (g, 0, 0))
    blk = pl.BlockSpec((CHUNK, GROUP_W), lambda g, c: (last - c, g))
    return pl.pallas_call(
        body, name=name,
        out_shape=(jax.ShapeDtypeStruct((rows, D_INNER), F32), jax.ShapeDtypeStruct((rows, D_BC), F32),
                   jax.ShapeDtypeStruct((rows, D_BC), F32), jax.ShapeDtypeStruct((N_GROUPS, rows, LANES), F32),
                   jax.ShapeDtypeStruct((N_GROUPS, 1, LANES), F32), jax.ShapeDtypeStruct((N_GROUPS, 1, LANES), F32)),
        grid=(N_GROUPS, nc),
        in_specs=[blk,
                  pl.BlockSpec((CHUNK, LANES), lambda g, c: (last - c, bcol + g)),
                  pl.BlockSpec((CHUNK, LANES), lambda g, c: (last - c, bcol + N_GROUPS + g)),
                  blk, blk,
                  pl.BlockSpec((None, CHUNK, LANES), lambda g, c: (g, last - c, 0)),
                  vec(LANES), vec(GROUP_W), blk,
                  pl.BlockSpec((None, None, D_STATE, GROUP_W), lambda g, c: (g, last - c, 0, 0))],
        out_specs=(blk,
                   pl.BlockSpec((CHUNK, LANES), lambda g, c: (last - c, g)),
                   pl.BlockSpec((CHUNK, LANES), lambda g, c: (last - c, g)),
                   pl.BlockSpec((None, CHUNK, LANES), lambda g, c: (g, last - c, 0)),
                   vec(LANES), vec(LANES)),
        scratch_shapes=[pltpu.VMEM((D_STATE, GROUP_W), F32)],
        compiler_params=_cparams(("parallel", "arbitrary")),
    )(xbc, xbc, xbc, dt_exp, acs_exp, dt4, a128, dskexp, dy, states)


def _attn_visible(b, heads=1):
    row = jnp.bitwise_and(lax.broadcasted_iota(jnp.int32, (heads * CHUNK, 3 * CHUNK), 0), CHUNK - 1)
    col = lax.broadcasted_iota(jnp.int32, (heads * CHUNK, 3 * CHUNK), 1)
    bb = b + jnp.zeros_like(col)
    meta = (col < CHUNK) & (bb >= 1) & (col >= PAD_ROWS)
    prev = (col >= CHUNK) & (col < 2 * CHUNK) & (bb >= 2) & ((col - CHUNK) > row)
    cur = (col >= 2 * CHUNK) & ((col - 2 * CHUNK) <= row) & ((bb >= 1) | ((col - 2 * CHUNK) >= PAD_ROWS))
    return meta | prev | cur


def _attn_visible4(b):
    return _attn_visible(b, 4)


def _stack_heads(q_ref, sink_ref, kvh, scale):
    lane = lax.broadcasted_iota(jnp.int32, (CHUNK, LANES), 1)
    parts, sinks = [], []
    for pp in range(2):
        pair = kvh * 2 + pp
        qp = q_ref[:, pair * LANES:(pair + 1) * LANES] * scale
        for e in range(2):
            mine = (lane < HEAD_DIM) if e == 0 else (lane >= HEAD_DIM)
            parts.append(jnp.where(mine, qp, 0.0).astype(BF16))
            sinks.append(jnp.full((CHUNK, 1), sink_ref[2 * pair + e], F32))
    return jnp.concatenate(parts, axis=0), jnp.concatenate(sinks, axis=0)


def _attn_probs(qm16, kc16, visible, sink):
    s = jnp.where(visible, _dot_nt(qm16, kc16), NEG_INF)
    m = jnp.maximum(jnp.max(s, axis=-1, keepdims=True), sink)
    pe = jnp.exp(s - m)
    pe_sink = jnp.exp(sink - m)
    inv = 1.0 / (jnp.sum(pe, axis=-1, keepdims=True) + pe_sink)
    return pe * inv, pe_sink * inv


def _attn_specs(colblock):
    blk = lambda f: pl.BlockSpec((CHUNK, 2 * D_KV), f)
    return [blk(lambda b: (0, colblock)), blk(lambda b: (jnp.maximum(b - 1, 0), colblock)), blk(lambda b: (b, colblock))]


def _attn_fwd(name, q, kv2, sinks):
    rows = q.shape[0]

    def body(q_ref, k0, kp, kc, v0, vp, vc, sink_ref, o_ref):
        visible = _attn_visible4(pl.program_id(0))
        lane = lax.broadcasted_iota(jnp.int32, (CHUNK, LANES), 1)
        for kvh in range(N_KV_HEADS):
            ksl = slice(kvh * LANES, (kvh + 1) * LANES)
            kcat = jnp.concatenate([k0[:, ksl], kp[:, ksl], kc[:, ksl]], axis=0).astype(BF16)
            vcat = jnp.concatenate([v0[:, ksl], vp[:, ksl], vc[:, ksl]], axis=0).astype(BF16)
            q4, sink4 = _stack_heads(q_ref, sink_ref, kvh, ATTN_SCALE)
            pn, _ = _attn_probs(q4, kcat, visible, sink4)
            o4 = _dot(pn.astype(BF16), vcat)
            for pp in range(2):
                qsl = slice((kvh * 2 + pp) * LANES, (kvh * 2 + pp + 1) * LANES)
                o_ref[:, qsl] = jnp.where(lane < HEAD_DIM, o4[(2 * pp) * CHUNK:(2 * pp + 1) * CHUNK],
                                          o4[(2 * pp + 1) * CHUNK:(2 * pp + 2) * CHUNK]).astype(BF16)

    return pl.pallas_call(
        body, name=name, out_shape=jax.ShapeDtypeStruct((rows, D_MODEL), BF16), grid=(rows // CHUNK,),
        in_specs=[pl.BlockSpec((CHUNK, D_MODEL), lambda b: (b, 0))] + _attn_specs(0) + _attn_specs(1)
        + [pl.BlockSpec(memory_space=pltpu.SMEM)],
        out_specs=pl.BlockSpec((CHUNK, D_MODEL), lambda b: (b, 0)), compiler_params=_cparams(("parallel",)),
    )(q, kv2, kv2, kv2, kv2, kv2, kv2, sinks)


def _attn_bwd(name, q, kv2, sinks, do):
    rows = q.shape[0]

    def body(q_ref, k0, kp, kc, v0, vp, vc, sink_ref, do_ref,
             dq_ref, dkc_ref, dkp_ref, dvc_ref, dvp_ref, dkm_ref, dvm_ref, dsink_ref):
        @pl.when(pl.program_id(0) == 0)
        def _():
            dkm_ref[...] = jnp.zeros_like(dkm_ref)
            dvm_ref[...] = jnp.zeros_like(dvm_ref)
            dsink_ref[...] = jnp.zeros_like(dsink_ref)

        visible = _attn_visible4(pl.program_id(0))
        lane = lax.broadcasted_iota(jnp.int32, (CHUNK, LANES), 1)
        lane1 = lax.broadcasted_iota(jnp.int32, (1, LANES), 1)
        dsink = jnp.zeros((1, LANES), F32)
        for kvh in range(N_KV_HEADS):
            ksl = slice(kvh * LANES, (kvh + 1) * LANES)
            kcat = jnp.concatenate([k0[:, ksl], kp[:, ksl], kc[:, ksl]], axis=0).astype(BF16)
            vcat = jnp.concatenate([v0[:, ksl], vp[:, ksl], vc[:, ksl]], axis=0).astype(BF16)
            q4, sink4 = _stack_heads(q_ref, sink_ref, kvh, ATTN_SCALE)
            do4, _ = _stack_heads(do_ref, sink_ref, kvh, 1.0)
            pn, psink = _attn_probs(q4, kcat, visible, sink4)
            dp = _dot_nt(do4, vcat)
            delta = jnp.sum(pn * dp, axis=-1, keepdims=True)
            ds = pn * (dp - delta)
            sink_terms = psink * delta
            dq4 = _dot(ds.astype(BF16), kcat)
            dk_acc = _dot(ds.T.astype(BF16), q4)
            dv_acc = _dot(pn.T.astype(BF16), do4)
            for j in range(4):
                part = jnp.sum(sink_terms[j * CHUNK:(j + 1) * CHUNK], axis=0, keepdims=True)
                dsink = dsink - jnp.where(lane1 == kvh * 4 + j, part, 0.0)
            for pp in range(2):
                qsl = slice((kvh * 2 + pp) * LANES, (kvh * 2 + pp + 1) * LANES)
                dq_pair = jnp.where(lane < HEAD_DIM, dq4[(2 * pp) * CHUNK:(2 * pp + 1) * CHUNK],
                                    dq4[(2 * pp + 1) * CHUNK:(2 * pp + 2) * CHUNK])
                dq_ref[:, qsl] = (dq_pair * ATTN_SCALE).astype(BF16)
            dkm_ref[:, ksl] += dk_acc[0:CHUNK]
            dvm_ref[:, ksl] += dv_acc[0:CHUNK]
            dkp_ref[:, ksl] = dk_acc[CHUNK:2 * CHUNK]
            dvp_ref[:, ksl] = dv_acc[CHUNK:2 * CHUNK]
            dkc_ref[:, ksl] = dk_acc[2 * CHUNK:3 * CHUNK]
            dvc_ref[:, ksl] = dv_acc[2 * CHUNK:3 * CHUNK]
        dsink_ref[...] += dsink

    qspec = pl.BlockSpec((CHUNK, D_MODEL), lambda b: (b, 0))
    kvspec = pl.BlockSpec((CHUNK, 2 * D_KV), lambda b: (b, 0))
    fixed = pl.BlockSpec((CHUNK, 2 * D_KV), lambda b: (0, 0))
    kv_shape = jax.ShapeDtypeStruct((rows, 2 * D_KV), F32)
    meta_shape = jax.ShapeDtypeStruct((CHUNK, 2 * D_KV), F32)
    return pl.pallas_call(
        body, name=name,
        out_shape=(jax.ShapeDtypeStruct((rows, D_MODEL), BF16), kv_shape, kv_shape, kv_shape, kv_shape,
                   meta_shape, meta_shape, jax.ShapeDtypeStruct((1, LANES), F32)),
        grid=(rows // CHUNK,),
        in_specs=[qspec] + _attn_specs(0) + _attn_specs(1) + [pl.BlockSpec(memory_space=pltpu.SMEM), qspec],
        out_specs=(qspec, kvspec, kvspec, kvspec, kvspec, fixed, fixed, pl.BlockSpec((1, LANES), lambda b: (0, 0))),
        compiler_params=_cparams(("arbitrary",)),
    )(q, kv2, kv2, kv2, kv2, kv2, kv2, sinks, do)


def _kv_grad_combine(name, dk_cur, dk_prev, dk_meta, dv_cur, dv_prev, dv_meta):
    rows = dk_cur.shape[0]
    nb = rows // CHUNK
    width = 2 * D_KV

    def body(kc_ref, kp_ref, km_ref, vc_ref, vp_ref, vm_ref, o_ref):
        jj = pl.program_id(0) + jnp.zeros((CHUNK, 1), jnp.int32)
        for half, (c_ref, p_ref, m_ref) in enumerate(((kc_ref, kp_ref, km_ref), (vc_ref, vp_ref, vm_ref))):
            total = c_ref[...] + jnp.where(jj < nb - 1, p_ref[...], 0.0) + jnp.where(jj == 0, m_ref[...], 0.0)
            o_ref[:, half * width:(half + 1) * width] = total.astype(BF16)

    blk = lambda f: pl.BlockSpec((CHUNK, width), f)
    three = lambda: [blk(lambda j: (j, 0)), blk(lambda j: (jnp.minimum(j + 1, nb - 1), 0)), blk(lambda j: (0, 0))]
    return pl.pallas_call(
        body, name=name, out_shape=jax.ShapeDtypeStruct((rows, 2 * width), BF16), grid=(nb,),
        in_specs=three() + three(), out_specs=pl.BlockSpec((CHUNK, 2 * width), lambda j: (j, 0)),
        compiler_params=_cparams(("parallel",)),
    )(dk_cur, dk_prev, dk_meta, dv_cur, dv_prev, dv_meta)


def _loss_head(name, h, target):
    rows = h.shape[0]

    def body(h_ref, t_ref, dh_ref, loss_ref):
        i = pl.program_id(0)
        real = (i + jnp.zeros((CHUNK, 1), jnp.int32)) >= 1
        diff = jnp.where(real, h_ref[...] - t_ref[...], 0.0)
        dh_ref[...] = diff * (1.0 / D_MODEL)

        @pl.when(i == 0)
        def _():
            loss_ref[...] = jnp.zeros_like(loss_ref)

        loss_ref[...] += jnp.sum(diff * diff) * (0.5 / D_MODEL)

    blk = pl.BlockSpec((CHUNK, D_MODEL), lambda i: (i, 0))
    return pl.pallas_call(
        body, name=name,
        out_shape=(jax.ShapeDtypeStruct((rows, D_MODEL), F32), jax.ShapeDtypeStruct((1, LANES), F32)),
        grid=(rows // CHUNK,),
        in_specs=[blk, pl.BlockSpec((CHUNK, D_MODEL), lambda i: (jnp.maximum(i - 1, 0), 0))],
        out_specs=(blk, pl.BlockSpec((1, LANES), lambda i: (0, 0))), compiler_params=_cparams(("arbitrary",)),
    )(h, target)


def _adamw(name, w, g, m, v):
    rows, width = w.shape
    tr = rows
    for cand in range(8, rows + 1, 8):
        if rows % cand == 0 and cand * width * 4 <= (1 << 20):
            tr = cand

    def body(w_ref, g_ref, m_ref, v_ref, d_ref, mo_ref, vo_ref):
        gv = g_ref[...]
        mn = ADAM_B1 * m_ref[...] + (1.0 - ADAM_B1) * gv
        vn = ADAM_B2 * v_ref[...] + (1.0 - ADAM_B2) * (gv * gv)
        m_hat = mn / (1.0 - ADAM_B1 ** ADAM_STEP)
        v_hat = vn / (1.0 - ADAM_B2 ** ADAM_STEP)
        d_ref[...] = -ADAM_LR * (m_hat / (jnp.sqrt(v_hat) + ADAM_EPS) + ADAM_WD * w_ref[...])
        mo_ref[...] = mn
        vo_ref[...] = vn

    blk = pl.BlockSpec((tr, width), lambda i: (i, 0))
    shp = jax.ShapeDtypeStruct((rows, width), F32)
    return pl.pallas_call(
        body, name=name, out_shape=(shp, shp, shp), grid=(rows // tr,), in_specs=[blk] * 4, out_specs=(blk,) * 3,
        compiler_params=_cparams(("parallel",)),
    )(w, g, m, v)


def _ffn_fwd(tag, h, p, i):
    hn = _rms_fwd(f"ffn{tag}_norm", h, p["f_norm_pre"][i:i + 1])
    up = _mm_nn_bychip(f"ffn{tag}_up", hn, p["f_w_up"][i])
    act = _ffn_conv_fwd(f"ffn{tag}_conv", up, p["f_conv_w"][i], p["f_conv_b"][i:i + 1])
    pre = _mm(f"ffn{tag}_down", act, p["f_w_down"][i], "nn")
    h_new = _resid_norm_fwd(f"ffn{tag}_resid", h, pre, p["f_norm_post"][i:i + 1])
    return h_new, (h, hn, up, act, pre)


def _ffn_bwd(tag, dh, saved, p, i):
    h, hn, up, act, pre = saved
    dpre, g_post = _resid_norm_bwd(f"ffn{tag}_resid_bwd", dh, pre, p["f_norm_post"][i:i + 1])
    dact = _mm(f"ffn{tag}_down_dx", dpre, p["f_w_down"][i], "nt")
    g_down = _mm(f"ffn{tag}_down_dw", act, dpre, "tn", out_dtype=BF16)
    dug, duv, gwg, gwv, gbg, gbv = _ffn_conv_bwd(f"ffn{tag}_conv_bwd", up, dact, p["f_conv_w"][i], p["f_conv_b"][i:i + 1])
    g_cw, g_cb = jnp.concatenate([gwg, gwv], axis=1), jnp.concatenate([gbg, gbv], axis=1)
    w_up = p["f_w_up"][i]
    n = w_up.shape[2]
    dhn = _mm_nt_bychip(f"ffn{tag}_up_dx_gate", dug, w_up, 0)
    dhn = _mm_nt_bychip(f"ffn{tag}_up_dx_val", duv, w_up, N_CHIPS // 2, acc=dhn)
    g_up = _mm_tn_bychip(f"ffn{tag}_up_dw_gate", hn, dug, n, 0)
    g_up = _mm_tn_bychip(f"ffn{tag}_up_dw_val", hn, duv, n, N_CHIPS // 2, into=g_up)
    dh_new, g_pre = _norm_bwd_add(f"ffn{tag}_norm_bwd", dh, dhn, h, p["f_norm_pre"][i:i + 1])
    return dh_new, dict(f_norm_post=g_post, f_w_down=g_down, f_conv_w=g_cw, f_conv_b=g_cb, f_w_up=g_up, f_norm_pre=g_pre)


def _lanes_pad(a, width=LANES):
    return jnp.pad(a, [(0, 0)] * (a.ndim - 1) + [(0, width - a.shape[-1])])


def _dup_heads(w):
    rows = w.shape[0]
    w = w.reshape(rows, 2 * N_KV_HEADS, 1, HEAD_DIM)
    return jnp.broadcast_to(w, (rows, 2 * N_KV_HEADS, 2, HEAD_DIM)).reshape(rows, 4 * D_KV)


def _undup_heads(g):
    rows = g.shape[0]
    return g.reshape(rows, 2 * N_KV_HEADS, 2, HEAD_DIM).sum(axis=2).reshape(rows, 2 * D_KV)


def _local_step(x2, target, p):
    seq = x2.shape[0]
    rows = seq + CHUNK
    g = {}

    h0 = jnp.concatenate([jnp.zeros((PAD_ROWS, D_MODEL), F32), p["meta_tokens"], x2], axis=0)

    w_in = p["a_w_in"]
    w_dt = _lanes_pad(w_in[:, D_MAIN:])
    dt_bias = _lanes_pad(p["a_dt_bias"])
    a_neg = -jnp.exp(p["a_a_log"].reshape(N_GROUPS, HEADS_PER_GROUP))
    a128 = _lanes_pad(a_neg.reshape(N_GROUPS, 1, HEADS_PER_GROUP))
    dskexp = jnp.repeat(p["a_d_skip"].reshape(N_GROUPS, HEADS_PER_GROUP), HEAD_DIM, axis=1).reshape(N_GROUPS, 1, GROUP_W)

    hn0 = _rms_fwd("a_norm", h0, p["a_norm_pre"])
    zx = _mm("a_in_main", hn0, w_in, "nn", n_cols=D_MAIN)
    dtr = _mm("a_in_dt", hn0, w_dt, "nn")
    xbc = _conv4_fwd("a_conv", zx, p["a_conv_w"], p["a_conv_b"])
    dt = _dt_fwd("a_dt", dtr, dt_bias)
    dt4 = _lanes_pad(dt[:, :SSM_HEADS].reshape(rows, N_GROUPS, HEADS_PER_GROUP).transpose(1, 0, 2))
    dt_exp, acs_exp = _ssd_prep("a_ssd_prep", dt4, a128)
    y, states = _ssd_fwd("a_ssd", xbc, dt_exp, acs_exp, dskexp)
    yn = _gate_fwd("a_gate", y, zx, p["a_gate_norm"])
    mix = _mm("a_out", yn, p["a_w_out"], "nn")
    h1 = _resid_norm_fwd("a_resid", h0, mix, p["a_norm_post"])

    h2, ffn0 = _ffn_fwd("0", h1, p, 0)

    hkv = _rms_fwd("kv_norm", h2, p["kv_norm"])
    w_kv2 = _dup_heads(p["w_kv"])
    kv2 = _mm("kv_proj", hkv, w_kv2, "nn")
    hn2 = _rms_fwd("b_norm", h2, p["b_norm_pre"])
    q = _mm("b_q", hn2, p["b_w_q"], "nn")
    sinks = p["b_sinks"].reshape(N_Q_HEADS)
    o = _attn_fwd("b_attn", q, kv2, sinks)
    attn = _mm("b_o", o, p["b_w_o"], "nn")
    h3 = _resid_norm_fwd("b_resid", h2, attn, p["b_norm_post"])

    h4, ffn1 = _ffn_fwd("1", h3, p, 1)

    dh, loss_vec = _loss_head("loss", h4, target)
    loss = loss_vec[0, 0]

    dh, g1 = _ffn_bwd("1", dh, ffn1, p, 1)

    dpre, g["b_norm_post"] = _resid_norm_bwd("b_resid_bwd", dh, attn, p["b_norm_post"])
    do = _mm("b_o_dx", dpre, p["b_w_o"], "nt")
    g["b_w_o"] = _mm("b_o_dw", o, dpre, "tn", out_dtype=BF16)
    dq, dkc, dkp, dvc, dvp, dkm, dvm, dsink = _attn_bwd("b_attn_bwd", q, kv2, sinks, do)
    g["b_sinks"] = dsink[:, :N_Q_HEADS]
    dhn2 = _mm("b_q_dx", dq, p["b_w_q"], "nt")
    g["b_w_q"] = _mm("b_q_dw", hn2, dq, "tn", out_dtype=BF16)
    dh, g["b_norm_pre"] = _norm_bwd_add("b_norm_bwd", dh, dhn2, h2, p["b_norm_pre"])
    dkv2 = _kv_grad_combine("kv_grad", dkc, dkp, dkm, dvc, dvp, dvm)
    dhkv = _mm("kv_proj_dx", dkv2, w_kv2, "nt")
    g["w_kv"] = _undup_heads(_mm("kv_proj_dw", hkv, dkv2, "tn"))
    dh, g["kv_norm"] = _norm_bwd_add("kv_norm_bwd", dh, dhkv, h2, p["kv_norm"])

    dh, g0 = _ffn_bwd("0", dh, ffn0, p, 0)
    for name in g0:
        if name in ("f_w_up", "f_w_down"):
            g[name] = [g0[name], g1[name]]
        elif g0[name].shape[0] == 1:
            g[name] = jnp.concatenate([g0[name], g1[name]], axis=0)
        else:
            g[name] = jnp.stack([g0[name], g1[name]])

    dpre, g["a_norm_post"] = _resid_norm_bwd("a_resid_bwd", dh, mix, p["a_norm_post"])
    dyn = _mm("a_out_dx", dpre, p["a_w_out"], "nt")
    g["a_w_out"] = _mm("a_out_dw", yn, dpre, "tn", out_dtype=BF16)
    dy, dz, g["a_gate_norm"] = _gate_bwd("a_gate_bwd", dyn, y, zx, p["a_gate_norm"])
    dxs, dbm, dcm, ddt4, dalog, ddsk = _ssd_bwd("a_ssd_bwd", xbc, dt_exp, acs_exp, dt4, a128, dskexp, dy, states)
    g["a_a_log"] = dalog[:, 0, :HEADS_PER_GROUP].reshape(1, SSM_HEADS)
    g["a_d_skip"] = ddsk[:, 0, :HEADS_PER_GROUP].reshape(1, SSM_HEADS)
    ddt = _lanes_pad(ddt4[:, :, :HEADS_PER_GROUP].transpose(1, 0, 2).reshape(rows, SSM_HEADS))
    ddtr, dbias = _dt_bwd("a_dt_bwd", ddt, dtr, dt_bias)
    g["a_dt_bias"] = dbias[:, :SSM_HEADS]
    dxp, gw_x, gb_x = _conv4_bwd("a_conv_bwd_x", zx, dxs, p["a_conv_w"], p["a_conv_b"], 0)
    dbp, gw_b, gb_b = _conv4_bwd("a_conv_bwd_b", zx, dbm, p["a_conv_w"], p["a_conv_b"], D_INNER)
    dcp, gw_c, gb_c = _conv4_bwd("a_conv_bwd_c", zx, dcm, p["a_conv_w"], p["a_conv_b"], D_INNER + D_BC)
    g["a_conv_w"] = jnp.concatenate([gw_x, gw_b, gw_c], axis=1)
    g["a_conv_b"] = jnp.concatenate([gb_x, gb_b, gb_c], axis=1)
    dzx = jnp.concatenate([dz, dxp, dbp, dcp], axis=1)
    g_main = _mm("a_in_main_dw", hn0, dzx, "tn", out_dtype=BF16)
    g_dt = _mm("a_in_dt_dw", hn0, ddtr, "tn", out_dtype=BF16)
    g["a_w_in"] = jnp.concatenate([g_main, g_dt[:, :SSM_HEADS]], axis=1)
    dhn0 = _mm("a_in_main_dx", dzx, w_in, "nt")
    dhn0 = _mm("a_in_dt_dx", ddtr, w_dt, "nt", acc=dhn0)
    dh, g["a_norm_pre"] = _norm_bwd_add("a_norm_bwd", dh, dhn0, h0, p["a_norm_pre"])

    g["meta_tokens"] = dh[PAD_ROWS:CHUNK]
    return loss, dh[CHUNK:], g


def _place():
    return lax.axis_index("x"), lax.axis_index("y"), lax.axis_index("c")


def _other_chips(x, y):
    return [(1 - x, y), (x, 1 - y), (1 - x, 1 - y)]


ANY = pl.BlockSpec(memory_space=pl.ANY)
VMEM_SPEC = pl.BlockSpec(memory_space=pltpu.VMEM)


def _allgather_small(name, shard):
    rows = shard.shape[0]

    def body(s_ref, o_ref, send_sems, recv_sems):
        x, y, c = _place()
        me = 2 * x + y
        o_ref[me] = s_ref[...]
        chips = _other_chips(x, y)
        sends = [pltpu.make_async_remote_copy(s_ref, o_ref.at[me], send_sems.at[j], recv_sems.at[j],
                                              device_id=(cx, cy, c), device_id_type=MESH)
                 for j, (cx, cy) in enumerate(chips)]
        for cp in sends:
            cp.start()
        for j, (cx, cy) in enumerate(chips):
            pltpu.make_async_remote_copy(s_ref, o_ref.at[2 * cx + cy], send_sems.at[j], recv_sems.at[j],
                                         device_id=(cx, cy, c), device_id_type=MESH).wait_recv()
        for cp in sends:
            cp.wait_send()

    return pl.pallas_call(
        body, name=name, out_shape=jax.ShapeDtypeStruct((N_CHIPS, rows, LANES), F32),
        in_specs=[VMEM_SPEC], out_specs=VMEM_SPEC,
        scratch_shapes=[pltpu.SemaphoreType.DMA((3,)), pltpu.SemaphoreType.DMA((3,))],
        compiler_params=pltpu.CompilerParams(vmem_limit_bytes=VMEM_LIMIT),
    )(shard)


def _row_block(rows, width, itemsize, align, budget=2 << 20):
    best = None
    for cand in range(align, rows + 1, align):
        if rows % cand == 0 and cand * width * itemsize <= budget:
            best = cand
    assert best is not None, (rows, width)
    return best


def _cast_into_slot(name, chip, w2d):
    rows, width = w2d.shape
    tr = _row_block(rows, width, 4, 16)

    def body(chip_ref, w_ref, o_ref):
        o_ref[...] = w_ref[...].astype(BF16)

    return pl.pallas_call(
        body, name=name, out_shape=jax.ShapeDtypeStruct((N_CHIPS, rows, width), BF16),
        grid_spec=pltpu.PrefetchScalarGridSpec(
            num_scalar_prefetch=1, grid=(rows // tr,),
            in_specs=[pl.BlockSpec((tr, width), lambda i, chip_ref: (i, 0))],
            out_specs=pl.BlockSpec((None, tr, width), lambda i, chip_ref: (chip_ref[0], i, 0))),
        compiler_params=_cparams(("parallel",)),
    )(chip, w2d)


def _allgather_big(name, bufs):
    n = len(bufs)

    def body(*refs):
        outs = refs[n:2 * n]
        send_sems, recv_sems = refs[2 * n], refs[2 * n + 1]
        x, y, c = _place()
        me = 2 * x + y
        chips = _other_chips(x, y)
        sibling = (x, y, 1 - c)

        def rows_of(o, which):
            hr = o.shape[1] // 2
            return pl.ds(which * hr, hr)

        def ici(k, j, cx, cy, idx):
            o = outs[k]
            part = o.at[idx, rows_of(o, c)]
            return pltpu.make_async_remote_copy(part, part, send_sems.at[6 * k + j], recv_sems.at[6 * k + j],
                                                device_id=(cx, cy, c), device_id_type=MESH)

        def d2d(k, j, idx, which):
            o = outs[k]
            part = o.at[idx, rows_of(o, which)]
            return pltpu.make_async_remote_copy(part, part, send_sems.at[6 * k + 3 + j], recv_sems.at[6 * k + 3 + j],
                                                device_id=sibling, device_id_type=MESH)

        started = []
        for k in range(n):
            for j, (cx, cy) in enumerate(chips):
                cp = ici(k, j, cx, cy, me)
                cp.start()
                started.append(cp)
        for k in range(n):
            for j, (cx, cy) in enumerate(chips):
                ici(k, j, cx, cy, 2 * cx + cy).wait_recv()
                fwd = d2d(k, j, 2 * cx + cy, c)
                fwd.start()
                started.append(fwd)
        for k in range(n):
            for j, (cx, cy) in enumerate(chips):
                d2d(k, j, 2 * cx + cy, 1 - c).wait_recv()
        for cp in started:
            cp.wait_send()

    return pl.pallas_call(
        body, name=name, out_shape=[jax.ShapeDtypeStruct(b.shape, b.dtype) for b in bufs],
        in_specs=[ANY] * n, out_specs=[ANY] * n, input_output_aliases={k: k for k in range(n)},
        scratch_shapes=[pltpu.SemaphoreType.DMA((6 * n,)), pltpu.SemaphoreType.DMA((6 * n,))],
        compiler_params=pltpu.CompilerParams(vmem_limit_bytes=VMEM_LIMIT),
    )(*bufs)


def _allreduce_small(name, vec):
    rows = vec.shape[0]

    def body(v_ref, o_ref, buf, send_sems, recv_sems):
        x, y, c = _place()
        me = 4 * x + 2 * y + c
        buf[me] = v_ref[...]

        def peer(k):
            kx, ky, kc = (k >> 2) & 1, (k >> 1) & 1, k & 1
            return (1 - x if kx else x, 1 - y if ky else y, 1 - c if kc else c)

        sends = []
        for k in range(1, N_DEV):
            cp = pltpu.make_async_remote_copy(v_ref, buf.at[me], send_sems.at[k - 1], recv_sems.at[k - 1],
                                              device_id=peer(k), device_id_type=MESH)
            cp.start()
            sends.append(cp)
        for k in range(1, N_DEV):
            px, py, pc = peer(k)
            pltpu.make_async_remote_copy(v_ref, buf.at[4 * px + 2 * py + pc], send_sems.at[k - 1], recv_sems.at[k - 1],
                                         device_id=(px, py, pc), device_id_type=MESH).wait_recv()
        for cp in sends:
            cp.wait_send()
        acc = buf[0]
        for d in range(1, N_DEV):
            acc = acc + buf[d]
        o_ref[...] = acc

    return pl.pallas_call(
        body, name=name, out_shape=jax.ShapeDtypeStruct((rows, LANES), F32),
        in_specs=[VMEM_SPEC], out_specs=VMEM_SPEC,
        scratch_shapes=[pltpu.VMEM((N_DEV, rows, LANES), F32), pltpu.SemaphoreType.DMA((N_DEV - 1,)),
                        pltpu.SemaphoreType.DMA((N_DEV - 1,))],
        compiler_params=pltpu.CompilerParams(vmem_limit_bytes=VMEM_LIMIT),
    )(vec)


def _rs_pair_exchange(name, grads):
    n = len(grads)

    def body(*refs):
        ins, outs = refs[:n], refs[n:2 * n]
        send_sems, recv_sems = refs[2 * n], refs[2 * n + 1]
        x, y, c = _place()
        copies = []
        for k in range(n):
            hr = ins[k].shape[1] // 2
            cp = pltpu.make_async_remote_copy(ins[k].at[:, pl.ds((1 - c) * hr, hr)], outs[k], send_sems.at[k],
                                              recv_sems.at[k], device_id=(x, y, 1 - c), device_id_type=MESH)
            cp.start()
            copies.append(cp)
        for cp in copies:
            cp.wait()

    return pl.pallas_call(
        body, name=name,
        out_shape=[jax.ShapeDtypeStruct((N_CHIPS, g.shape[1] // 2, g.shape[2]), g.dtype) for g in grads],
        in_specs=[ANY] * n, out_specs=[ANY] * n,
        scratch_shapes=[pltpu.SemaphoreType.DMA((n,)), pltpu.SemaphoreType.DMA((n,))],
        compiler_params=pltpu.CompilerParams(vmem_limit_bytes=VMEM_LIMIT),
    )(*grads)


def _rs_pair_add(name, place, grads, partner):
    _, half_rows, width = partner.shape
    tr = _row_block(half_rows, width, 2, 16)
    nb = half_rows // tr

    def body(place_ref, g_ref, p_ref, o_ref):
        o_ref[...] = (g_ref[...].astype(F32) + p_ref[...].astype(F32)).astype(BF16)

    return pl.pallas_call(
        body, name=name, out_shape=jax.ShapeDtypeStruct(partner.shape, BF16),
        grid_spec=pltpu.PrefetchScalarGridSpec(
            num_scalar_prefetch=1, grid=(N_CHIPS, nb),
            in_specs=[pl.BlockSpec((None, tr, width), lambda s, i, pr: (s, pr[1] * nb + i, 0)),
                      pl.BlockSpec((None, tr, width), lambda s, i, pr: (s, i, 0))],
            out_specs=pl.BlockSpec((None, tr, width), lambda s, i, pr: (s, i, 0))),
        compiler_params=_cparams(("parallel", "parallel")),
    )(place, grads, partner)


def _rs_chip_exchange(name, partials):
    n = len(partials)

    def body(*refs):
        ins, outs = refs[:n], refs[n:2 * n]
        send_sems, recv_sems = refs[2 * n], refs[2 * n + 1]
        x, y, c = _place()
        chips = _other_chips(x, y)
        sends = []
        for k in range(n):
            for j, (cx, cy) in enumerate(chips):
                cp = pltpu.make_async_remote_copy(ins[k].at[2 * cx + cy], outs[k].at[j], send_sems.at[3 * k + j],
                                                  recv_sems.at[3 * k + j], device_id=(cx, cy, c), device_id_type=MESH)
                cp.start()
                sends.append(cp)
        for cp in sends:
            cp.wait()

    return pl.pallas_call(
        body, name=name,
        out_shape=[jax.ShapeDtypeStruct((3,) + q.shape[1:], q.dtype) for q in partials],
        in_specs=[ANY] * n, out_specs=[ANY] * n,
        scratch_shapes=[pltpu.SemaphoreType.DMA((3 * n,)), pltpu.SemaphoreType.DMA((3 * n,))],
        compiler_params=pltpu.CompilerParams(vmem_limit_bytes=VMEM_LIMIT),
    )(*partials)


def _rs_chip_add(name, place, mine, others):
    _, half_rows, width = mine.shape
    tr = _row_block(half_rows, width, 4, 16, budget=1 << 20)
    nb = half_rows // tr

    def body(place_ref, q_ref, r_ref, o_ref):
        acc = q_ref[...].astype(F32)
        for j in range(3):
            acc = acc + r_ref[j].astype(F32)
        o_ref[...] = acc

    return pl.pallas_call(
        body, name=name, out_shape=jax.ShapeDtypeStruct((2 * half_rows, width), F32),
        grid_spec=pltpu.PrefetchScalarGridSpec(
            num_scalar_prefetch=1, grid=(nb,),
            in_specs=[pl.BlockSpec((None, tr, width), lambda i, pr: (pr[0], i, 0)),
                      pl.BlockSpec((3, tr, width), lambda i, pr: (0, i, 0))],
            out_specs=pl.BlockSpec((tr, width), lambda i, pr: (pr[1] * nb + i, 0))),
        compiler_params=_cparams(("parallel",)),
    )(place, mine, others)


def _rs_pair_gather(name, shards):
    n = len(shards)

    def body(*refs):
        outs = refs[n:2 * n]
        send_sems, recv_sems = refs[2 * n], refs[2 * n + 1]
        x, y, c = _place()
        copies = []
        for k in range(n):
            hr = outs[k].shape[0] // 2
            part = outs[k].at[pl.ds(c * hr, hr)]
            cp = pltpu.make_async_remote_copy(part, part, send_sems.at[k], recv_sems.at[k],
                                              device_id=(x, y, 1 - c), device_id_type=MESH)
            cp.start()
            copies.append(cp)
        for k in range(n):
            hr = outs[k].shape[0] // 2
            theirs = outs[k].at[pl.ds((1 - c) * hr, hr)]
            pltpu.make_async_remote_copy(theirs, theirs, send_sems.at[k], recv_sems.at[k],
                                         device_id=(x, y, 1 - c), device_id_type=MESH).wait_recv()
        for cp in copies:
            cp.wait_send()

    return pl.pallas_call(
        body, name=name, out_shape=[jax.ShapeDtypeStruct(s.shape, s.dtype) for s in shards],
        in_specs=[ANY] * n, out_specs=[ANY] * n, input_output_aliases={k: k for k in range(n)},
        scratch_shapes=[pltpu.SemaphoreType.DMA((n,)), pltpu.SemaphoreType.DMA((n,))],
        compiler_params=pltpu.CompilerParams(vmem_limit_bytes=VMEM_LIMIT),
    )(*shards)


WEIGHTS = ["meta_tokens", "a_norm_pre", "a_w_in", "a_conv_w", "a_conv_b", "a_dt_bias", "a_a_log", "a_d_skip",
           "a_gate_norm", "a_w_out", "a_norm_post", "kv_norm", "w_kv", "b_norm_pre", "b_w_q", "b_sinks", "b_w_o",
           "b_norm_post", "f_norm_pre", "f_w_up", "f_conv_w", "f_conv_b", "f_w_down", "f_norm_post"]
FULL_SHAPE = {
    "meta_tokens": (16, 1024), "a_norm_pre": (1, 1024), "a_w_in": (1, 1024, 5152), "a_conv_w": (1, 4, 3072),
    "a_conv_b": (1, 3072), "a_dt_bias": (1, 32), "a_a_log": (1, 32), "a_d_skip": (1, 32), "a_gate_norm": (1, 2048),
    "a_w_out": (1, 2048, 1024), "a_norm_post": (1, 1024), "kv_norm": (1024,), "w_kv": (1024, 512),
    "b_norm_pre": (1, 1024), "b_w_q": (1, 1024, 1024), "b_sinks": (1, 16), "b_w_o": (1, 1024, 1024),
    "b_norm_post": (1, 1024), "f_norm_pre": (2, 1024), "f_w_up": (2, 1024, 5632), "f_conv_w": (2, 3, 5632),
    "f_conv_b": (2, 5632), "f_w_down": (2, 2816, 1024), "f_norm_post": (2, 1024),
}
SHARD_AXIS = {
    "meta_tokens": 1, "a_norm_pre": 1, "a_w_in": 2, "a_conv_w": 2, "a_conv_b": 1, "a_dt_bias": None, "a_a_log": None,
    "a_d_skip": None, "a_gate_norm": 1, "a_w_out": 1, "a_norm_post": 1, "kv_norm": None, "w_kv": 0, "b_norm_pre": None,
    "b_w_q": 1, "b_sinks": None, "b_w_o": 1, "b_norm_post": None, "f_norm_pre": None, "f_w_up": 2, "f_conv_w": 2,
    "f_conv_b": None, "f_w_down": 1, "f_norm_post": None,
}
BIG = ["a_w_in", "a_w_out", "w_kv", "b_w_q", "b_w_o", "f_w_up", "f_w_down"]
SMALL = [n for n in WEIGHTS if n not in BIG]
SMALL_SHARDED = [n for n in SMALL if SHARD_AXIS[n] is not None]


def _shard_shape(name):
    shape = list(FULL_SHAPE[name])
    if SHARD_AXIS[name] is not None:
        shape[SHARD_AXIS[name]] //= N_CHIPS
    return tuple(shape)


def _numel(shape):
    return int(math.prod(shape))


SUBLANES = 8


def _packed_rows(shape):
    rows = -(-_numel(shape) // LANES)
    return -(-rows // SUBLANES) * SUBLANES


def _pack(arrays):
    parts = []
    for a in arrays:
        size, rows = _numel(a.shape), _packed_rows(a.shape)
        if size % LANES == 0:
            part = jnp.pad(a.reshape(size // LANES, LANES), ((0, rows - size // LANES), (0, 0)))
        else:
            part = jnp.pad(a.reshape(-1), (0, rows * LANES - size)).reshape(rows, LANES)
        parts.append(part)
    return jnp.concatenate(parts, axis=0)


def _unpack(packed, names, shape_of):
    out, off = {}, 0
    lead = packed.shape[:-2]
    for n in names:
        shape = tuple(shape_of(n))
        size, rows = _numel(shape), _packed_rows(shape)
        part = packed[..., off:off + rows, :]
        if size % LANES == 0:
            out[n] = part[..., :size // LANES, :].reshape(lead + shape)
        else:
            out[n] = part.reshape(lead + (rows * LANES,))[..., :size].reshape(lead + shape)
        off += rows
    return out


def _split_chips(name, full):
    ax = SHARD_AXIS[name]
    shape = full.shape
    cut = shape[:ax] + (N_CHIPS, shape[ax] // N_CHIPS) + shape[ax + 1:]
    return jnp.moveaxis(full.reshape(cut), ax, 0)


def _join_chips(name, stacked):
    ax = SHARD_AXIS[name]
    moved = jnp.moveaxis(stacked, 0, ax)
    shape = moved.shape
    return moved.reshape(shape[:ax] + (shape[ax] * shape[ax + 1],) + shape[ax + 2:])


def _as2d(a):
    return a.reshape(-1, a.shape[-1])


BUFFERS = [("a_w_in", "a_w_in", None), ("a_w_out", "a_w_out", None), ("w_kv", "w_kv", None),
           ("b_w_q", "b_w_q", None), ("b_w_o", "b_w_o", None), ("f_w_up0", "f_w_up", 0), ("f_w_up1", "f_w_up", 1),
           ("f_w_down0", "f_w_down", 0), ("f_w_down1", "f_w_down", 1)]


def _local_shard(arrays, weight, layer):
    return _as2d(arrays[weight]) if layer is None else arrays[weight][layer]


def _weight_from_gathered(weight, buf):
    if weight == "a_w_in":
        return buf.transpose(1, 0, 2).reshape(buf.shape[1], N_CHIPS * buf.shape[2])
    if weight == "f_w_up":
        return buf
    return buf.reshape(N_CHIPS * buf.shape[1], buf.shape[2])


def _gathered_from_grad(weight, g):
    if weight == "a_w_in":
        rows = g.shape[0]
        return g.reshape(rows, N_CHIPS, g.shape[1] // N_CHIPS).transpose(1, 0, 2).astype(BF16)
    if weight == "f_w_up":
        return g
    return g.reshape(N_CHIPS, g.shape[0] // N_CHIPS, g.shape[1]).astype(BF16)


def kernel(x, meta_tokens, a_norm_pre, a_w_in, a_conv_w, a_conv_b, a_dt_bias, a_a_log, a_d_skip, a_gate_norm, a_w_out, a_norm_post, kv_norm, w_kv, b_norm_pre, b_w_q, b_sinks, b_w_o, b_norm_post, f_norm_pre, f_w_up, f_conv_w, f_conv_b, f_w_down, f_norm_post, loss_target, m_meta_tokens, m_a_norm_pre, m_a_w_in, m_a_conv_w, m_a_conv_b, m_a_dt_bias, m_a_a_log, m_a_d_skip, m_a_gate_norm, m_a_w_out, m_a_norm_post, m_kv_norm, m_w_kv, m_b_norm_pre, m_b_w_q, m_b_sinks, m_b_w_o, m_b_norm_post, m_f_norm_pre, m_f_w_up, m_f_conv_w, m_f_conv_b, m_f_w_down, m_f_norm_post, v_meta_tokens, v_a_norm_pre, v_a_w_in, v_a_conv_w, v_a_conv_b, v_a_dt_bias, v_a_a_log, v_a_d_skip, v_a_gate_norm, v_a_w_out, v_a_norm_post, v_kv_norm, v_w_kv, v_b_norm_pre, v_b_w_q, v_b_sinks, v_b_w_o, v_b_norm_post, v_f_norm_pre, v_f_w_up, v_f_conv_w, v_f_conv_b, v_f_w_down, v_f_norm_post):
    given = dict(locals())
    w = {n: given[n] for n in WEIGHTS}
    mom = {n: given["m_" + n] for n in WEIGHTS}
    var = {n: given["v_" + n] for n in WEIGHTS}
    chip = 2 * lax.axis_index("x") + lax.axis_index("y")
    core = lax.axis_index("c")
    place = jnp.stack([chip, core]).astype(jnp.int32)

    small_all = _allgather_small("gather_small", _pack([w[n] for n in SMALL_SHARDED]))
    small_parts = _unpack(small_all, SMALL_SHARDED, _shard_shape)
    slots = [_cast_into_slot("cast_" + b, place, _local_shard(w, wn, layer)) for b, wn, layer in BUFFERS]
    gathered = _allgather_big("gather_big", slots)
    p = {}
    for n in SMALL:
        p[n] = _join_chips(n, small_parts[n]) if n in SMALL_SHARDED else w[n]
    for (b, wn, layer), buf in zip(BUFFERS, gathered):
        operand = _weight_from_gathered(wn, buf)
        if layer is None:
            p[wn] = operand
        else:
            p.setdefault(wn, [None, None])[layer] = operand
    p["a_conv_w"] = p["a_conv_w"][0]
    p["kv_norm"] = p["kv_norm"].reshape(1, D_MODEL)

    loss_local, grad_x, g = _local_step(x[0], loss_target[0], p)
    loss = lax.psum(loss_local, ("x", "y", "c"))

    small_sum = _allreduce_small("reduce_small", _pack([g[n].reshape(FULL_SHAPE[n]) for n in SMALL]))
    small_red = _unpack(small_sum, SMALL, lambda n: FULL_SHAPE[n])
    grads = {}
    for n in SMALL:
        if SHARD_AXIS[n] is None:
            grads[n] = small_red[n]
        else:
            grads[n] = lax.dynamic_index_in_dim(_split_chips(n, small_red[n]), chip, 0, keepdims=False)

    names = [b for b, _, _ in BUFFERS]
    by_chip = [_gathered_from_grad(wn, g[wn] if layer is None else g[wn][layer]) for _, wn, layer in BUFFERS]
    partner = _rs_pair_exchange("reduce_pair_send", by_chip)
    pair_sum = [_rs_pair_add("reduce_pair_add_" + b, place, gk, pk) for b, gk, pk in zip(names, by_chip, partner)]
    from_chips = _rs_chip_exchange("reduce_chip_send", pair_sum)
    half_sum = [_rs_chip_add("reduce_chip_add_" + b, place, qk, rk) for b, qk, rk in zip(names, pair_sum, from_chips)]
    shard_sum = dict(zip(names, _rs_pair_gather("reduce_pair_gather", half_sum)))
    for n in BIG:
        if n in ("f_w_up", "f_w_down"):
            grads[n] = jnp.stack([shard_sum[n + "0"], shard_sum[n + "1"]])
        else:
            grads[n] = shard_sum[n].reshape(_shard_shape(n))

    delta, new_m, new_v = {}, {}, {}
    for n in BIG:
        shape = _shard_shape(n)
        d, m2, v2 = _adamw("adamw_" + n, _as2d(w[n]), _as2d(grads[n]), _as2d(mom[n]), _as2d(var[n]))
        delta[n], new_m[n], new_v[n] = d.reshape(shape), m2.reshape(shape), v2.reshape(shape)
    packed = [_pack([src[n].reshape(_shard_shape(n)) for n in SMALL]) for src in (w, grads, mom, var)]
    outs = _adamw("adamw_small", *packed)
    for dst, flat in zip((delta, new_m, new_v), outs):
        dst.update(_unpack(flat, SMALL, _shard_shape))

    return (loss, grad_x[None], *[grads[n].reshape(_shard_shape(n)) for n in WEIGHTS],
            *[delta[n] for n in WEIGHTS], *[new_m[n] for n in WEIGHTS], *[new_v[n] for n in WEIGHTS])
```

```python
import functools
import math

import jax
import jax.numpy as jnp
from jax import lax
from jax.experimental import pallas as pl
from jax.experimental.pallas import tpu as pltpu

F32, BF16 = jnp.float32, jnp.bfloat16
MESH = pl.DeviceIdType.MESH

D_MODEL = 1024
N_META = 16
CHUNK = 128
PAD_ROWS = CHUNK - N_META
D_INNER = 2048
D_STATE = 128
N_GROUPS = 4
HEADS_PER_GROUP = 8
SSM_HEADS = 32
HEAD_DIM = 64
D_BC = N_GROUPS * D_STATE
D_XBC = D_INNER + 2 * D_BC
D_MAIN = D_INNER + D_XBC
D_IN_PROJ = D_MAIN + SSM_HEADS
GROUP_W = HEADS_PER_GROUP * HEAD_DIM
SSM_CONV = 4
D_FF = 2816
FFN_CONV = 3
N_Q_HEADS = 16
N_KV_HEADS = 4
D_KV = 256
ATTN_SCALE = 1.0 / math.sqrt(HEAD_DIM)
RMS_EPS = 1e-6
NEG_INF = -1e30
LANES = 128
VMEM_LIMIT = 48 * 1024 * 1024

ADAM_LR, ADAM_B1, ADAM_B2, ADAM_EPS, ADAM_WD, ADAM_STEP = 0.001, 0.9, 0.999, 1e-08, 0.01, 10

N_CHIPS = 4
N_DEV = 8


def _cparams(sem=None):
    return pltpu.CompilerParams(dimension_semantics=sem, vmem_limit_bytes=VMEM_LIMIT)


def _tile(n, cands=(512, 256, 128)):
    for t in cands:
        if n % t == 0:
            return t
    return n


def _row_tile(rows, width):
    for t in (544, 272):
        if rows % t == 0 and t * width * 4 <= (3 << 20):
            return t
    return 128


def _rows_mask(i, tm):
    rows = i * tm + lax.broadcasted_iota(jnp.int32, (tm, 1), 0)
    return rows >= PAD_ROWS


def _dot(a, b):
    return jnp.dot(a, b, preferred_element_type=F32)


def _dot_nt(a, b):
    return lax.dot_general(a, b, (((1,), (1,)), ((), ())), preferred_element_type=F32)


def _dot_tn(a, b):
    return lax.dot_general(a, b, (((0,), (0,)), ((), ())), preferred_element_type=F32)


def _sigmoid(x):
    return 1.0 / (1.0 + jnp.exp(-x))


def _place():
    return lax.axis_index("x"), lax.axis_index("y"), lax.axis_index("c")


def _other_chips(x, y):
    return [(1 - x, y), (x, 1 - y), (1 - x, 1 - y)]


class _Step:
    def __init__(self, ins, outs, aliases, n_sems, start, finish):
        self.ins, self.outs, self.aliases, self.n_sems = list(ins), list(outs), dict(aliases), n_sems
        self.start, self.finish = start, finish
        self.results = None


def _like(a):
    return jax.ShapeDtypeStruct(a.shape, a.dtype)


def _remote(src, dst, send_sems, recv_sems, k, device):
    return pltpu.make_async_remote_copy(src, dst, send_sems.at[k], recv_sems.at[k], device_id=device, device_id_type=MESH)


def _half_rows(ref, axis, which):
    hr = ref.shape[axis] // 2
    return pl.ds(which * hr, hr)


def _step_gather_ici(bufs):
    def copies(outs, send_sems, recv_sems, received):
        x, y, c = _place()
        me = 2 * x + y
        for k, o in enumerate(outs):
            for j, (cx, cy) in enumerate(_other_chips(x, y)):
                part = o.at[2 * cx + cy if received else me, _half_rows(o, 1, c)]
                yield _remote(part, part, send_sems, recv_sems, 3 * k + j, (cx, cy, c))

    def start(ins, outs, send_sems, recv_sems):
        for cp in copies(outs, send_sems, recv_sems, False):
            cp.start()

    def finish(ins, outs, send_sems, recv_sems):
        for cp in copies(outs, send_sems, recv_sems, True):
            cp.wait_recv()
        for cp in copies(outs, send_sems, recv_sems, False):
            cp.wait_send()

    return _Step(bufs, [_like(b) for b in bufs], {k: k for k in range(len(bufs))}, 3 * len(bufs), start, finish)


def _step_gather_d2d(bufs):
    def copies(outs, send_sems, recv_sems, received):
        x, y, c = _place()
        for k, o in enumerate(outs):
            for j, (cx, cy) in enumerate(_other_chips(x, y)):
                part = o.at[2 * cx + cy, _half_rows(o, 1, 1 - c if received else c)]
                yield _remote(part, part, send_sems, recv_sems, 3 * k + j, (x, y, 1 - c))

    def start(ins, outs, send_sems, recv_sems):
        for cp in copies(outs, send_sems, recv_sems, False):
            cp.start()

    def finish(ins, outs, send_sems, recv_sems):
        for cp in copies(outs, send_sems, recv_sems, True):
            cp.wait_recv()
        for cp in copies(outs, send_sems, recv_sems, False):
            cp.wait_send()

    return _Step(bufs, [_like(b) for b in bufs], {k: k for k in range(len(bufs))}, 3 * len(bufs), start, finish)


def _step_gather_full(bufs):
    n = len(bufs)

    def ici(outs, send_sems, recv_sems, received):
        x, y, c = _place()
        me = 2 * x + y
        for k, o in enumerate(outs):
            for j, (cx, cy) in enumerate(_other_chips(x, y)):
                part = o.at[2 * cx + cy if received else me, _half_rows(o, 1, c)]
                yield _remote(part, part, send_sems, recv_sems, 3 * k + j, (cx, cy, c))

    def d2d(outs, send_sems, recv_sems, received):
        x, y, c = _place()
        for k, o in enumerate(outs):
            for j, (cx, cy) in enumerate(_other_chips(x, y)):
                part = o.at[2 * cx + cy, _half_rows(o, 1, 1 - c if received else c)]
                yield _remote(part, part, send_sems, recv_sems, 3 * n + 3 * k + j, (x, y, 1 - c))

    def start(ins, outs, send_sems, recv_sems):
        for cp in ici(outs, send_sems, recv_sems, False):
            cp.start()

    def finish(ins, outs, send_sems, recv_sems):
        for arrived, onward in zip(ici(outs, send_sems, recv_sems, True), d2d(outs, send_sems, recv_sems, False)):
            arrived.wait_recv()
            onward.start()
        for cp in d2d(outs, send_sems, recv_sems, True):
            cp.wait_recv()
        for cp in ici(outs, send_sems, recv_sems, False):
            cp.wait_send()
        for cp in d2d(outs, send_sems, recv_sems, False):
            cp.wait_send()

    return _Step(bufs, [_like(b) for b in bufs], {k: k for k in range(n)}, 6 * n, start, finish)


def _step_pair_exchange(grads):
    def copies(ins, outs, send_sems, recv_sems):
        x, y, c = _place()
        for k, (g, o) in enumerate(zip(ins, outs)):
            yield _remote(g.at[:, _half_rows(g, 1, 1 - c)], o, send_sems, recv_sems, k, (x, y, 1 - c))

    def start(ins, outs, send_sems, recv_sems):
        for cp in copies(ins, outs, send_sems, recv_sems):
            cp.start()

    def finish(ins, outs, send_sems, recv_sems):
        for cp in copies(ins, outs, send_sems, recv_sems):
            cp.wait()

    outs = [jax.ShapeDtypeStruct((N_CHIPS, g.shape[1] // 2, g.shape[2]), g.dtype) for g in grads]
    return _Step(grads, outs, {}, len(grads), start, finish)


def _step_chip_exchange(partials):
    def copies(ins, outs, send_sems, recv_sems):
        x, y, c = _place()
        for k, (q, o) in enumerate(zip(ins, outs)):
            for j, (cx, cy) in enumerate(_other_chips(x, y)):
                yield _remote(q.at[2 * cx + cy], o.at[j], send_sems, recv_sems, 3 * k + j, (cx, cy, c))

    def start(ins, outs, send_sems, recv_sems):
        for cp in copies(ins, outs, send_sems, recv_sems):
            cp.start()

    def finish(ins, outs, send_sems, recv_sems):
        for cp in copies(ins, outs, send_sems, recv_sems):
            cp.wait()

    outs = [jax.ShapeDtypeStruct((3,) + q.shape[1:], q.dtype) for q in partials]
    return _Step(partials, outs, {}, 3 * len(partials), start, finish)


def _step_pair_gather(shards):
    def copies(outs, send_sems, recv_sems, received):
        x, y, c = _place()
        for k, o in enumerate(outs):
            part = o.at[_half_rows(o, 0, 1 - c if received else c)]
            yield _remote(part, part, send_sems, recv_sems, k, (x, y, 1 - c))

    def start(ins, outs, send_sems, recv_sems):
        for cp in copies(outs, send_sems, recv_sems, False):
            cp.start()

    def finish(ins, outs, send_sems, recv_sems):
        for cp in copies(outs, send_sems, recv_sems, True):
            cp.wait_recv()
        for cp in copies(outs, send_sems, recv_sems, False):
            cp.wait_send()

    return _Step(shards, [_like(s) for s in shards], {k: k for k in range(len(shards))}, len(shards), start, finish)


def _call(body, *, name, out_shape, grid, in_specs, out_specs, operands, scratch_shapes=(), semantics=None, steps=()):
    single = not isinstance(out_shape, (tuple, list))
    out_shapes = [out_shape] if single else list(out_shape)
    out_spec_list = [out_specs] if single else list(out_specs)
    steps = list(steps)
    if not steps:
        res = pl.pallas_call(body, name=name, out_shape=out_shapes, grid=grid, in_specs=list(in_specs),
                             out_specs=out_spec_list, scratch_shapes=list(scratch_shapes),
                             compiler_params=_cparams(semantics))(*operands)
        return res[0] if single else res
    n_in, n_out, n_scr = len(operands), len(out_shapes), len(scratch_shapes)
    x_in = [a for s in steps for a in s.ins]
    x_out = [o for s in steps for o in s.outs]
    aliases, in_off, out_off = {}, 0, 0
    for s in steps:
        for i, o in s.aliases.items():
            aliases[n_in + in_off + i] = n_out + out_off + o
        in_off += len(s.ins)
        out_off += len(s.outs)
    sems = []
    for s in steps:
        sems += [pltpu.SemaphoreType.DMA((s.n_sems,)), pltpu.SemaphoreType.DMA((s.n_sems,))]
    any_spec = pl.BlockSpec(memory_space=pl.ANY)

    def carried(*refs):
        pos = 0
        ins = refs[pos:pos + n_in]; pos += n_in
        xi = refs[pos:pos + len(x_in)]; pos += len(x_in)
        outs = refs[pos:pos + n_out]; pos += n_out
        xo = refs[pos:pos + len(x_out)]; pos += len(x_out)
        scr = refs[pos:pos + n_scr]; pos += n_scr
        sem_refs = refs[pos:]

        def each(action):
            i0 = o0 = 0
            for k, s in enumerate(steps):
                getattr(s, action)(xi[i0:i0 + len(s.ins)], xo[o0:o0 + len(s.outs)], sem_refs[2 * k], sem_refs[2 * k + 1])
                i0 += len(s.ins)
                o0 += len(s.outs)

        if grid:
            first = functools.reduce(jnp.logical_and, [pl.program_id(d) == 0 for d in range(len(grid))])
            last = functools.reduce(jnp.logical_and, [pl.program_id(d) == grid[d] - 1 for d in range(len(grid))])
            pl.when(first)(lambda: each("start"))
            body(*ins, *outs, *scr)
            pl.when(last)(lambda: each("finish"))
        else:
            each("start")
            body(*ins, *outs, *scr)
            each("finish")

    res = pl.pallas_call(
        carried, name=name, out_shape=out_shapes + x_out, grid=grid,
        in_specs=list(in_specs) + [any_spec] * len(x_in), out_specs=out_spec_list + [any_spec] * len(x_out),
        scratch_shapes=list(scratch_shapes) + sems, input_output_aliases=aliases,
        compiler_params=_cparams(None if semantics is None else ("arbitrary",) * len(grid)),
    )(*operands, *x_in)
    o0 = n_out
    for s in steps:
        s.results = list(res[o0:o0 + len(s.outs)])
        o0 += len(s.outs)
    return res[0] if single else tuple(res[:n_out])


def _run_steps(name, steps):
    _call(lambda: None, name=name, out_shape=[], grid=(), in_specs=[], out_specs=[], operands=[], steps=steps)
    return [s.results for s in steps]


def _mm(name, a, b, mode, out_dtype=F32, acc=None, b_colblock=0, n_cols=None, steps=()):
    resident_bytes = 8 << 20
    if mode == "nn":
        m, k = a.shape
        n = n_cols or b.shape[1]
        tm = m
        while tm * k * 2 > resident_bytes and tm % 32 == 0:
            tm //= 2
        tn = _tile(n)
        grid = (m // tm, n // tn)
        in_specs = [pl.BlockSpec((tm, k), lambda i, j: (i, 0)), pl.BlockSpec((k, tn), lambda i, j: (0, j))]
        out_shape, out_block = (m, n), (tm, tn)
    elif mode == "nt":
        m, n = a.shape
        k = b.shape[0]
        tm = m
        while tm * n * 2 > resident_bytes and tm % 32 == 0:
            tm //= 2
        tk = _tile(k)
        grid = (m // tm, k // tk)
        in_specs = [pl.BlockSpec((tm, n), lambda i, j: (i, 0)), pl.BlockSpec((tk, n), lambda i, j: (j, b_colblock))]
        out_shape, out_block = (m, k), (tm, tk)
    else:
        m, k = a.shape
        n = b.shape[1]
        tk, tn = _tile(k), _tile(n)
        grid = (k // tk, n // tn)
        in_specs = [pl.BlockSpec((m, tk), lambda i, j: (0, i)), pl.BlockSpec((m, tn), lambda i, j: (0, j))]
        out_shape, out_block = (k, n), (tk, tn)
    out_spec = pl.BlockSpec(out_block, lambda i, j: (i, j))
    has_acc = acc is not None

    def body(*refs):
        a_ref, b_ref = refs[0], refs[1]
        o_ref = refs[-1]
        av, bv = a_ref[...], b_ref[...]
        if mode == "nn":
            r = _dot(av, bv)
        elif mode == "nt":
            r = _dot_nt(av, bv)
        else:
            r = _dot_tn(av, bv)
        if has_acc:
            r = r + refs[2][...]
        o_ref[...] = r.astype(o_ref.dtype)

    operands = [a, b]
    if has_acc:
        in_specs = in_specs + [out_spec]
        operands.append(acc)
    return _call(body, name=name, out_shape=jax.ShapeDtypeStruct(out_shape, out_dtype), grid=grid, in_specs=in_specs,
                 out_specs=out_spec, operands=operands, semantics=("parallel", "parallel"), steps=steps)


def _fit_rows(m, row_bytes, budget=8 << 20):
    tm = m
    while tm * row_bytes > budget and tm % 32 == 0:
        tm //= 2
    return tm


def _mm_nn_bychip(name, a, bc):
    m, k = a.shape
    n = bc.shape[2]
    tm = min(_fit_rows(m, k * 2), _fit_rows(m, n * 4))

    def body(a_ref, b_ref, o_ref):
        o_ref[...] = _dot(a_ref[...], b_ref[...])

    return pl.pallas_call(
        body, name=name, out_shape=jax.ShapeDtypeStruct((m, N_CHIPS * n), F32), grid=(m // tm, N_CHIPS),
        in_specs=[pl.BlockSpec((tm, k), lambda i, c: (i, 0)), pl.BlockSpec((None, k, n), lambda i, c: (c, 0, 0))],
        out_specs=pl.BlockSpec((tm, n), lambda i, c: (i, c)), compiler_params=_cparams(("parallel", "parallel")),
    )(a, bc)


def _mm_nt_bychip(name, a, bc, chip0, acc=None):
    m = a.shape[0]
    _, k, n = bc.shape
    nch = a.shape[1] // n
    tm, tk = _fit_rows(m, n * 2), _tile(k)
    has_acc = acc is not None

    def body(*refs):
        a_ref, b_ref, o_ref = refs[0], refs[1], refs[-1]

        @pl.when(pl.program_id(2) == 0)
        def _():
            o_ref[...] = refs[2][...] if has_acc else jnp.zeros_like(o_ref)

        o_ref[...] += _dot_nt(a_ref[...], b_ref[...])

    out_spec = pl.BlockSpec((tm, tk), lambda i, j, c: (i, j))
    in_specs = [pl.BlockSpec((tm, n), lambda i, j, c: (i, c)),
                pl.BlockSpec((None, tk, n), lambda i, j, c: (chip0 + c, j, 0))]
    operands = [a, bc]
    if has_acc:
        in_specs.append(out_spec)
        operands.append(acc)
    return pl.pallas_call(
        body, name=name, out_shape=jax.ShapeDtypeStruct((m, k), F32), grid=(m // tm, k // tk, nch),
        in_specs=in_specs, out_specs=out_spec, compiler_params=_cparams(("parallel", "parallel", "arbitrary")),
    )(*operands)


def _mm_tn_bychip(name, a, dy, n, chip0, into=None):
    m, k = a.shape
    nch = dy.shape[1] // n
    tk = _tile(k)

    def body(*refs):
        a_ref, d_ref, o_ref = refs[0], refs[1], refs[-1]
        o_ref[...] = _dot_tn(a_ref[...], d_ref[...]).astype(BF16)

    in_specs = [pl.BlockSpec((m, tk), lambda i, c: (0, i)), pl.BlockSpec((m, n), lambda i, c: (0, c))]
    operands = [a, dy]
    aliases = {}
    if into is not None:
        in_specs.append(pl.BlockSpec(memory_space=pl.ANY))
        operands.append(into)
        aliases = {2: 0}
    return pl.pallas_call(
        body, name=name, out_shape=jax.ShapeDtypeStruct((N_CHIPS, k, n), BF16), grid=(k // tk, nch),
        in_specs=in_specs, out_specs=pl.BlockSpec((None, tk, n), lambda i, c: (chip0 + c, i, 0)),
        input_output_aliases=aliases, compiler_params=_cparams(("parallel", "parallel")),
    )(*operands)


def _rms_fwd(name, h, w):
    rows, width = h.shape
    tm = _row_tile(rows, width)

    def body(h_ref, w_ref, o_ref):
        x = h_ref[...]
        r = lax.rsqrt(jnp.mean(x * x, axis=-1, keepdims=True) + RMS_EPS)
        o_ref[...] = (x * r * w_ref[...]).astype(BF16)

    return pl.pallas_call(
        body, name=name, out_shape=jax.ShapeDtypeStruct((rows, width), BF16), grid=(rows // tm,),
        in_specs=[pl.BlockSpec((tm, width), lambda i: (i, 0)), pl.BlockSpec((1, width), lambda i: (0, 0))],
        out_specs=pl.BlockSpec((tm, width), lambda i: (i, 0)), compiler_params=_cparams(("parallel",)),
    )(h, w)


def _resid_norm_fwd(name, h, pre, w):
    rows, width = h.shape
    tm = _row_tile(rows, width)

    def body(h_ref, p_ref, w_ref, o_ref):
        p = p_ref[...]
        r = lax.rsqrt(jnp.mean(p * p, axis=-1, keepdims=True) + RMS_EPS)
        o_ref[...] = h_ref[...] + jnp.where(_rows_mask(pl.program_id(0), tm), p * r * w_ref[...], 0.0)

    row_spec = pl.BlockSpec((tm, width), lambda i: (i, 0))
    return pl.pallas_call(
        body, name=name, out_shape=jax.ShapeDtypeStruct((rows, width), F32), grid=(rows // tm,),
        in_specs=[row_spec, row_spec, pl.BlockSpec((1, width), lambda i: (0, 0))],
        out_specs=row_spec, compiler_params=_cparams(("parallel",)),
    )(h, pre, w)


def _resid_norm_bwd(name, dh, pre, w):
    rows, width = dh.shape
    tm = _row_tile(rows, width)

    def body(dh_ref, p_ref, w_ref, dp_ref, dw_ref):
        i = pl.program_id(0)
        dy = jnp.where(_rows_mask(i, tm), dh_ref[...], 0.0)
        p = p_ref[...]
        r = lax.rsqrt(jnp.mean(p * p, axis=-1, keepdims=True) + RMS_EPS)
        xhat = p * r
        dxhat = dy * w_ref[...]
        dp = r * (dxhat - xhat * jnp.mean(dxhat * xhat, axis=-1, keepdims=True))
        dp_ref[...] = dp.astype(BF16)

        @pl.when(i == 0)
        def _():
            dw_ref[...] = jnp.zeros_like(dw_ref)

        dw_ref[...] += jnp.sum(dy * xhat, axis=0, keepdims=True)

    row_spec = pl.BlockSpec((tm, width), lambda i: (i, 0))
    vec_spec = pl.BlockSpec((1, width), lambda i: (0, 0))
    return pl.pallas_call(
        body, name=name,
        out_shape=(jax.ShapeDtypeStruct((rows, width), BF16), jax.ShapeDtypeStruct((1, width), F32)),
        grid=(rows // tm,), in_specs=[row_spec, row_spec, vec_spec], out_specs=(row_spec, vec_spec),
        compiler_params=_cparams(("arbitrary",)),
    )(dh, pre, w)


def _norm_bwd_add(name, dh, dhn, h, w):
    rows, width = dh.shape
    tm = _row_tile(rows, width)

    def body(dh_ref, dhn_ref, h_ref, w_ref, o_ref, dw_ref):
        i = pl.program_id(0)
        x = h_ref[...]
        dy = dhn_ref[...]
        r = lax.rsqrt(jnp.mean(x * x, axis=-1, keepdims=True) + RMS_EPS)
        xhat = x * r
        dxhat = dy * w_ref[...]
        dx = r * (dxhat - xhat * jnp.mean(dxhat * xhat, axis=-1, keepdims=True))
        o_ref[...] = dh_ref[...] + jnp.where(_rows_mask(i, tm), dx, 0.0)

        @pl.when(i == 0)
        def _():
            dw_ref[...] = jnp.zeros_like(dw_ref)

        dw_ref[...] += jnp.sum(dy * xhat, axis=0, keepdims=True)

    row_spec = pl.BlockSpec((tm, width), lambda i: (i, 0))
    vec_spec = pl.BlockSpec((1, width), lambda i: (0, 0))
    return pl.pallas_call(
        body, name=name,
        out_shape=(jax.ShapeDtypeStruct((rows, width), F32), jax.ShapeDtypeStruct((1, width), F32)),
        grid=(rows // tm,), in_specs=[row_spec, row_spec, row_spec, vec_spec], out_specs=(row_spec, vec_spec),
        compiler_params=_cparams(("arbitrary",)),
    )(dh, dhn, h, w)


def _shift_down(x, s, rows):
    return pltpu.roll(x, s, 0) if s else x


def _shift_up(x, s, rows):
    return pltpu.roll(x, rows - s, 0) if s else x


def _conv4_fwd(name, zx, cw, cb, steps=()):
    rows = zx.shape[0]
    off = D_INNER // LANES

    def body(x_ref, w_ref, b_ref, o_ref):
        x = x_ref[...]
        acc = b_ref[...] + w_ref[pl.ds(SSM_CONV - 1, 1), :] * x
        for s in range(1, SSM_CONV):
            acc = acc + w_ref[pl.ds(SSM_CONV - 1 - s, 1), :] * _shift_down(x, s, rows)
        valid = lax.broadcasted_iota(jnp.int32, (rows, 1), 0) >= PAD_ROWS
        o_ref[...] = jnp.where(valid, acc * _sigmoid(acc), 0.0)

    return _call(
        body, name=name, out_shape=jax.ShapeDtypeStruct((rows, D_XBC), F32), grid=(D_XBC // LANES,),
        in_specs=[pl.BlockSpec((rows, LANES), lambda j: (0, j + off)),
                  pl.BlockSpec((SSM_CONV, LANES), lambda j: (0, j)),
                  pl.BlockSpec((1, LANES), lambda j: (0, j))],
        out_specs=pl.BlockSpec((rows, LANES), lambda j: (0, j)), operands=[zx, cw, cb],
        semantics=("parallel",), steps=steps)


def _conv4_bwd(name, zx, dout, cw, cb, col0):
    rows, width = dout.shape
    zoff = (D_INNER + col0) // LANES
    woff = col0 // LANES

    def body(x_ref, d_ref, w_ref, b_ref, dx_ref, dw_ref, db_ref):
        x = x_ref[...]
        shifted = [_shift_down(x, s, rows) for s in range(SSM_CONV)]
        acc = b_ref[...]
        for s in range(SSM_CONV):
            acc = acc + w_ref[pl.ds(SSM_CONV - 1 - s, 1), :] * shifted[s]
        sig = _sigmoid(acc)
        valid = lax.broadcasted_iota(jnp.int32, (rows, 1), 0) >= PAD_ROWS
        dpre = jnp.where(valid, d_ref[...] * sig * (1.0 + acc * (1.0 - sig)), 0.0)
        dx = w_ref[pl.ds(SSM_CONV - 1, 1), :] * dpre
        for s in range(1, SSM_CONV):
            dx = dx + w_ref[pl.ds(SSM_CONV - 1 - s, 1), :] * _shift_up(dpre, s, rows)
        dx_ref[...] = dx.astype(BF16)
        for s in range(SSM_CONV):
            dw_ref[pl.ds(SSM_CONV - 1 - s, 1), :] = jnp.sum(dpre * shifted[s], axis=0, keepdims=True)
        db_ref[...] = jnp.sum(dpre, axis=0, keepdims=True)

    return pl.pallas_call(
        body, name=name,
        out_shape=(jax.ShapeDtypeStruct((rows, width), BF16), jax.ShapeDtypeStruct((SSM_CONV, width), F32),
                   jax.ShapeDtypeStruct((1, width), F32)),
        grid=(width // LANES,),
        in_specs=[pl.BlockSpec((rows, LANES), lambda j: (0, j + zoff)),
                  pl.BlockSpec((rows, LANES), lambda j: (0, j)),
                  pl.BlockSpec((SSM_CONV, LANES), lambda j: (0, j + woff)),
                  pl.BlockSpec((1, LANES), lambda j: (0, j + woff))],
        out_specs=(pl.BlockSpec((rows, LANES), lambda j: (0, j)),
                   pl.BlockSpec((SSM_CONV, LANES), lambda j: (0, j)),
                   pl.BlockSpec((1, LANES), lambda j: (0, j))),
        compiler_params=_cparams(("parallel",)),
    )(zx, dout, cw, cb)


def _ffn_conv_fwd(name, up, cw, cb, steps=()):
    rows = up.shape[0]
    nt = D_FF // LANES

    def body(g_ref, v_ref, wg_ref, wv_ref, bg_ref, bv_ref, o_ref):
        g, v = g_ref[...], v_ref[...]
        ug, uv = bg_ref[...], bv_ref[...]
        for s in range(FFN_CONV):
            ug = ug + wg_ref[pl.ds(FFN_CONV - 1 - s, 1), :] * _shift_down(g, s, rows)
            uv = uv + wv_ref[pl.ds(FFN_CONV - 1 - s, 1), :] * _shift_down(v, s, rows)
        valid = lax.broadcasted_iota(jnp.int32, (rows, 1), 0) >= PAD_ROWS
        o_ref[...] = jnp.where(valid, ug * _sigmoid(ug) * uv, 0.0).astype(BF16)

    col = lambda shift: pl.BlockSpec((rows, LANES), lambda j: (0, j + shift))
    wsp = lambda shift: pl.BlockSpec((FFN_CONV, LANES), lambda j: (0, j + shift))
    bsp = lambda shift: pl.BlockSpec((1, LANES), lambda j: (0, j + shift))
    return _call(
        body, name=name, out_shape=jax.ShapeDtypeStruct((rows, D_FF), BF16), grid=(nt,),
        in_specs=[col(0), col(nt), wsp(0), wsp(nt), bsp(0), bsp(nt)],
        out_specs=pl.BlockSpec((rows, LANES), lambda j: (0, j)), operands=[up, up, cw, cw, cb, cb],
        semantics=("parallel",), steps=steps)


def _ffn_conv_bwd(name, up, dact, cw, cb, steps=()):
    rows = up.shape[0]
    nt = D_FF // LANES

    def body(g_ref, v_ref, d_ref, wg_ref, wv_ref, bg_ref, bv_ref, dxg_ref, dxv_ref, dwg_ref, dwv_ref, dbg_ref, dbv_ref):
        g, v = g_ref[...], v_ref[...]
        gs = [_shift_down(g, s, rows) for s in range(FFN_CONV)]
        vs = [_shift_down(v, s, rows) for s in range(FFN_CONV)]
        ug, uv = bg_ref[...], bv_ref[...]
        for s in range(FFN_CONV):
            ug = ug + wg_ref[pl.ds(FFN_CONV - 1 - s, 1), :] * gs[s]
            uv = uv + wv_ref[pl.ds(FFN_CONV - 1 - s, 1), :] * vs[s]
        sig = _sigmoid(ug)
        valid = lax.broadcasted_iota(jnp.int32, (rows, 1), 0) >= PAD_ROWS
        d = jnp.where(valid, d_ref[...], 0.0)
        dsig = d * sig
        for dpre, src, w_ref, dx_ref, dw_ref, db_ref in (
                (dsig * uv * (1.0 + ug * (1.0 - sig)), gs, wg_ref, dxg_ref, dwg_ref, dbg_ref),
                (dsig * ug, vs, wv_ref, dxv_ref, dwv_ref, dbv_ref)):
            dx = w_ref[pl.ds(FFN_CONV - 1, 1), :] * dpre
            for s in range(1, FFN_CONV):
                dx = dx + w_ref[pl.ds(FFN_CONV - 1 - s, 1), :] * _shift_up(dpre, s, rows)
            dx_ref[...] = dx.astype(BF16)
            for s in range(FFN_CONV):
                dw_ref[pl.ds(FFN_CONV - 1 - s, 1), :] = jnp.sum(dpre * src[s], axis=0, keepdims=True)
            db_ref[...] = jnp.sum(dpre, axis=0, keepdims=True)

    col = lambda shift: pl.BlockSpec((rows, LANES), lambda j: (0, j + shift))
    wsp = lambda shift: pl.BlockSpec((FFN_CONV, LANES), lambda j: (0, j + shift))
    bsp = lambda shift: pl.BlockSpec((1, LANES), lambda j: (0, j + shift))
    dx_shape = jax.ShapeDtypeStruct((rows, D_FF), BF16)
    dw_shape = jax.ShapeDtypeStruct((FFN_CONV, D_FF), F32)
    db_shape = jax.ShapeDtypeStruct((1, D_FF), F32)
    return _call(
        body, name=name, out_shape=(dx_shape, dx_shape, dw_shape, dw_shape, db_shape, db_shape), grid=(nt,),
        in_specs=[col(0), col(nt), col(0), wsp(0), wsp(nt), bsp(0), bsp(nt)],
        out_specs=(col(0), col(0), wsp(0), wsp(0), bsp(0), bsp(0)),
        operands=[up, up, dact, cw, cw, cb, cb], semantics=("parallel",), steps=steps)


def _dt_fwd(name, dtr, bias):
    rows = dtr.shape[0]
    tm = _row_tile(rows, LANES)

    def body(d_ref, b_ref, o_ref):
        v = d_ref[...] + b_ref[...]
        sp = jnp.maximum(v, 0.0) + jnp.log1p(jnp.exp(-jnp.abs(v)))
        lane = lax.broadcasted_iota(jnp.int32, (tm, LANES), 1)
        ok = _rows_mask(pl.program_id(0), tm) & (lane < SSM_HEADS)
        o_ref[...] = jnp.where(ok, sp, 0.0)

    return pl.pallas_call(
        body, name=name, out_shape=jax.ShapeDtypeStruct((rows, LANES), F32), grid=(rows // tm,),
        in_specs=[pl.BlockSpec((tm, LANES), lambda i: (i, 0)), pl.BlockSpec((1, LANES), lambda i: (0, 0))],
        out_specs=pl.BlockSpec((tm, LANES), lambda i: (i, 0)), compiler_params=_cparams(("parallel",)),
    )(dtr, bias)


def _dt_bwd(name, ddt, dtr, bias):
    rows = dtr.shape[0]
    tm = _row_tile(rows, LANES)

    def body(g_ref, d_ref, b_ref, o_ref, db_ref):
        i = pl.program_id(0)
        lane = lax.broadcasted_iota(jnp.int32, (tm, LANES), 1)
        ok = _rows_mask(i, tm) & (lane < SSM_HEADS)
        dv = jnp.where(ok, g_ref[...] * _sigmoid(d_ref[...] + b_ref[...]), 0.0)
        o_ref[...] = dv.astype(BF16)

        @pl.when(i == 0)
        def _():
            db_ref[...] = jnp.zeros_like(db_ref)

        db_ref[...] += jnp.sum(dv, axis=0, keepdims=True)

    row_spec = pl.BlockSpec((tm, LANES), lambda i: (i, 0))
    vec_spec = pl.BlockSpec((1, LANES), lambda i: (0, 0))
    return pl.pallas_call(
        body, name=name,
        out_shape=(jax.ShapeDtypeStruct((rows, LANES), BF16), jax.ShapeDtypeStruct((1, LANES), F32)),
        grid=(rows // tm,), in_specs=[row_spec, row_spec, vec_spec], out_specs=(row_spec, vec_spec),
        compiler_params=_cparams(("arbitrary",)),
    )(ddt, dtr, bias)


def _gate_fwd(name, y, zx, w, steps=()):
    rows = y.shape[0]
    tm = _row_tile(rows, D_INNER)

    def body(y_ref, z_ref, w_ref, o_ref):
        z = z_ref[...]
        g = y_ref[...] * (z * _sigmoid(z))
        r = lax.rsqrt(jnp.mean(g * g, axis=-1, keepdims=True) + RMS_EPS)
        o_ref[...] = (g * r * w_ref[...]).astype(BF16)

    row_spec = pl.BlockSpec((tm, D_INNER), lambda i: (i, 0))
    return _call(
        body, name=name, out_shape=jax.ShapeDtypeStruct((rows, D_INNER), BF16), grid=(rows // tm,),
        in_specs=[row_spec, row_spec, pl.BlockSpec((1, D_INNER), lambda i: (0, 0))],
        out_specs=row_spec, operands=[y, zx, w], semantics=("parallel",), steps=steps)


def _gate_bwd(name, dyn, y, zx, w):
    rows = y.shape[0]
    tm = _row_tile(rows, D_INNER)

    def body(d_ref, y_ref, z_ref, w_ref, dy_ref, dz_ref, dw_ref):
        i = pl.program_id(0)
        z, yv = z_ref[...], y_ref[...]
        sig = _sigmoid(z)
        sz = z * sig
        g = yv * sz
        r = lax.rsqrt(jnp.mean(g * g, axis=-1, keepdims=True) + RMS_EPS)
        ghat = g * r
        dn = d_ref[...]
        dghat = dn * w_ref[...]
        dg = r * (dghat - ghat * jnp.mean(dghat * ghat, axis=-1, keepdims=True))
        dy_ref[...] = dg * sz
        dz_ref[...] = (dg * yv * sig * (1.0 + z * (1.0 - sig))).astype(BF16)

        @pl.when(i == 0)
        def _():
            dw_ref[...] = jnp.zeros_like(dw_ref)

        dw_ref[...] += jnp.sum(dn * ghat, axis=0, keepdims=True)

    row_spec = pl.BlockSpec((tm, D_INNER), lambda i: (i, 0))
    vec_spec = pl.BlockSpec((1, D_INNER), lambda i: (0, 0))
    return pl.pallas_call(
        body, name=name,
        out_shape=(jax.ShapeDtypeStruct((rows, D_INNER), F32), jax.ShapeDtypeStruct((rows, D_INNER), BF16),
                   jax.ShapeDtypeStruct((1, D_INNER), F32)),
        grid=(rows // tm,), in_specs=[row_spec, row_spec, row_spec, vec_spec],
        out_specs=(row_spec, row_spec, vec_spec), compiler_params=_cparams(("arbitrary",)),
    )(dyn, y, zx, w)


def _split3(x):
    hi = x.astype(BF16)
    r1 = x - hi.astype(F32)
    mid = r1.astype(BF16)
    lo = (r1 - mid.astype(F32)).astype(BF16)
    return hi, mid, lo


def _dot3_data_lhs(x, sel):
    sel16 = sel.astype(F32).astype(BF16)
    hi, mid, lo = _split3(x)
    return _dot(hi, sel16) + _dot(mid, sel16) + _dot(lo, sel16)


def _dot3_data_rhs(sel, x):
    sel16 = sel.astype(F32).astype(BF16)
    hi, mid, lo = _split3(x)
    return _dot(sel16, hi) + _dot(sel16, mid) + _dot(sel16, lo)


def _causal_masks():
    r = lax.broadcasted_iota(jnp.int32, (CHUNK, CHUNK), 0)
    c = lax.broadcasted_iota(jnp.int32, (CHUNK, CHUNK), 1)
    return r >= c, r <= c


def _expand_heads_matrix():
    k = lax.broadcasted_iota(jnp.int32, (LANES, GROUP_W), 0)
    j = lax.broadcasted_iota(jnp.int32, (LANES, GROUP_W), 1)
    return jnp.right_shift(j, 6) == k


def _reduce_heads_matrix():
    j = lax.broadcasted_iota(jnp.int32, (GROUP_W, LANES), 0)
    k = lax.broadcasted_iota(jnp.int32, (GROUP_W, LANES), 1)
    return jnp.right_shift(j, 6) == k


def _reduce_pair_matrix(p):
    j = lax.broadcasted_iota(jnp.int32, (LANES, LANES), 0)
    k = lax.broadcasted_iota(jnp.int32, (LANES, LANES), 1)
    return (2 * p + jnp.right_shift(j, 6)) == k


def _ssd_prep(name, dt4, a128, steps=()):
    rows = dt4.shape[1]
    nc = rows // CHUNK

    def body(dt_ref, a_ref, dte_ref, acs_ref):
        causal, _ = _causal_masks()
        expand = _expand_heads_matrix()
        dt = dt_ref[...]
        acs = _dot3_data_rhs(causal, dt) * a_ref[...]
        dte_ref[...] = _dot3_data_lhs(dt, expand)
        acs_ref[...] = _dot3_data_lhs(acs, expand)

    blk = pl.BlockSpec((CHUNK, GROUP_W), lambda g, c: (c, g))
    shp = jax.ShapeDtypeStruct((rows, D_INNER), F32)
    return _call(
        body, name=name, out_shape=(shp, shp), grid=(N_GROUPS, nc),
        in_specs=[pl.BlockSpec((None, CHUNK, LANES), lambda g, c: (g, c, 0)),
                  pl.BlockSpec((None, 1, LANES), lambda g, c: (g, 0, 0))],
        out_specs=(blk, blk), operands=[dt4, a128], semantics=("parallel", "parallel"), steps=steps)


def _ssd_common(x_ref, b_ref, c_ref, dte_ref, acs_ref):
    x = x_ref[...]
    dt_exp = dte_ref[...]
    acs_exp = acs_ref[...]
    tot_exp = acs_ref[pl.ds(CHUNK - 1, 1), :]
    xdt = x * dt_exp
    e_exp = jnp.exp(acs_exp)
    f_exp = jnp.exp(tot_exp - acs_exp)
    return _causal_masks(), x, dt_exp, acs_exp, tot_exp, xdt, e_exp, f_exp, b_ref[...], c_ref[...]


def _pair_decay(acs_pair, e, causal):
    lane = lax.broadcasted_iota(jnp.int32, (CHUNK, LANES), 1)
    mine = (lane < HEAD_DIM) if e == 0 else (lane >= HEAD_DIM)
    a_l = jnp.where(mine, acs_pair, pltpu.roll(acs_pair, HEAD_DIM, 1))
    seg = a_l - a_l.T
    dm = jnp.where(causal[0], jnp.exp(jnp.minimum(seg, 0.0)), 0.0)
    dmt = jnp.where(causal[1], jnp.exp(jnp.minimum(-seg, 0.0)), 0.0)
    return dm, dmt


def _ssd_fwd(name, xbc, dt_exp, acs_exp, dskexp, steps=()):
    rows = xbc.shape[0]
    nc = rows // CHUNK
    bcol = D_INNER // LANES

    def body(x_ref, b_ref, c_ref, dte_ref, acs_ref, dsk_ref, y_ref, st_ref, s_scr):
        @pl.when(pl.program_id(1) == 0)
        def _():
            s_scr[...] = jnp.zeros_like(s_scr)

        causal, x, _, acs_exp_v, tot_exp, xdt, e_exp, f_exp, bm, cm = _ssd_common(x_ref, b_ref, c_ref, dte_ref, acs_ref)
        state = s_scr[...]
        st_ref[...] = state
        cb16, bb16 = cm.astype(BF16), bm.astype(BF16)
        cb = _dot_nt(cb16, bb16)
        base = e_exp * _dot(cb16, state.astype(BF16)) + dsk_ref[...] * x
        lane = lax.broadcasted_iota(jnp.int32, (CHUNK, LANES), 1)
        for p in range(HEADS_PER_GROUP // 2):
            sl = slice(p * LANES, (p + 1) * LANES)
            xp = xdt[:, sl].astype(BF16)
            yd = []
            for e in range(2):
                dm, _ = _pair_decay(acs_exp_v[:, sl], e, causal)
                yd.append(_dot((cb * dm).astype(BF16), xp))
            y_ref[:, sl] = jnp.where(lane < HEAD_DIM, yd[0], yd[1]) + base[:, sl]
        s_scr[...] = jnp.exp(tot_exp) * state + _dot(bm.T.astype(BF16), (f_exp * xdt).astype(BF16))

    blk = pl.BlockSpec((CHUNK, GROUP_W), lambda g, c: (c, g))
    return _call(
        body, name=name,
        out_shape=(jax.ShapeDtypeStruct((rows, D_INNER), F32),
                   jax.ShapeDtypeStruct((N_GROUPS, nc, D_STATE, GROUP_W), F32)),
        grid=(N_GROUPS, nc),
        in_specs=[blk,
                  pl.BlockSpec((CHUNK, LANES), lambda g, c: (c, bcol + g)),
                  pl.BlockSpec((CHUNK, LANES), lambda g, c: (c, bcol + N_GROUPS + g)),
                  blk, blk, pl.BlockSpec((None, 1, GROUP_W), lambda g, c: (g, 0, 0))],
        out_specs=(blk, pl.BlockSpec((None, None, D_STATE, GROUP_W), lambda g, c: (g, c, 0, 0))),
        scratch_shapes=[pltpu.VMEM((D_STATE, GROUP_W), F32)],
        operands=[xbc, xbc, xbc, dt_exp, acs_exp, dskexp], semantics=("parallel", "arbitrary"), steps=steps)


def _ssd_bwd(name, xbc, dt_exp, acs_exp, dt4, a128, dskexp, dy, states, steps=()):
    rows = xbc.shape[0]
    nc = rows // CHUNK
    bcol = D_INNER // LANES
    last = nc - 1

    def body(x_ref, b_ref, c_ref, dte_ref, acs_ref, dt_ref, a128_ref, dsk_ref, dy_ref, st_ref,
             dx_ref, db_ref, dc_ref, ddt_ref, dalog_ref, ddsk_ref, ds_scr):
        first = pl.program_id(1) == 0

        @pl.when(first)
        def _():
            ds_scr[...] = jnp.zeros_like(ds_scr)
            dalog_ref[...] = jnp.zeros_like(dalog_ref)
            ddsk_ref[...] = jnp.zeros_like(ddsk_ref)

        causal, x, dt_exp, acs_exp_v, tot_exp, xdt, e_exp, f_exp, bm, cm = _ssd_common(
            x_ref, b_ref, c_ref, dte_ref, acs_ref)
        dt = dt_ref[...]
        reduce_heads = _reduce_heads_matrix()
        state, dstate = st_ref[...], ds_scr[...]
        dyv = dy_ref[...]
        cb16, bb16 = cm.astype(BF16), bm.astype(BF16)
        s16, ds16 = state.astype(BF16), dstate.astype(BF16)
        cb = _dot_nt(cb16, bb16)
        cbt = _dot_nt(bb16, cb16)
        cs = _dot(cb16, s16)
        bds = _dot(bb16, ds16)
        edy = e_exp * dyv
        fx = f_exp * xdt
        dxdt_base = f_exp * bds
        dc_acc = _dot_nt(edy.astype(BF16), s16)
        db_acc = _dot_nt(fx.astype(BF16), ds16)
        ds_scr[...] = jnp.exp(tot_exp) * dstate + _dot(cm.T.astype(BF16), edy.astype(BF16))
        q = fx * bds
        dacs = _dot3_data_lhs(edy * cs - q, reduce_heads)
        dtot = jnp.sum(_dot3_data_lhs(q + jnp.exp(tot_exp) * dstate * state, reduce_heads), axis=0, keepdims=True)
        ddsk_ref[...] += jnp.sum(_dot3_data_lhs(dyv * x, reduce_heads), axis=0, keepdims=True)
        lane = lax.broadcasted_iota(jnp.int32, (CHUNK, LANES), 1)
        dcb = jnp.zeros((CHUNK, CHUNK), F32)
        dcbt = jnp.zeros((CHUNK, CHUNK), F32)
        ddt_x = jnp.zeros((CHUNK, LANES), F32)
        for p in range(HEADS_PER_GROUP // 2):
            sl = slice(p * LANES, (p + 1) * LANES)
            xp, dyp = xdt[:, sl], dyv[:, sl]
            xp16, dyp16 = xp.astype(BF16), dyp.astype(BF16)
            dxh = []
            for e in range(2):
                h = 2 * p + e
                mine = (lane < HEAD_DIM) if e == 0 else (lane >= HEAD_DIM)
                dm, dmt = _pair_decay(acs_exp_v[:, sl], e, causal)
                m, mt = cb * dm, cbt * dmt
                xh16 = jnp.where(mine, xp, 0.0).astype(BF16)
                dyh16 = jnp.where(mine, dyp, 0.0).astype(BF16)
                d_m = _dot_nt(dyh16, xp16)
                d_mt = _dot_nt(xh16, dyp16)
                dacs_h = (jnp.sum(d_m * m, axis=-1, keepdims=True)
                          - jnp.sum(d_mt * mt, axis=-1, keepdims=True))
                dacs = dacs + jnp.where(lane == h, dacs_h, 0.0)
                dcb = dcb + d_m * dm
                dcbt = dcbt + d_mt * dmt
                dxh.append(_dot(mt.astype(BF16), dyp16))
            dxdt = jnp.where(lane < HEAD_DIM, dxh[0], dxh[1]) + dxdt_base[:, sl]
            dx_ref[:, sl] = dxdt * dt_exp[:, sl] + dsk_ref[:, sl] * dyp
            ddt_x = ddt_x + _dot3_data_lhs(dxdt * x[:, sl], _reduce_pair_matrix(p))
        dc_ref[...] = dc_acc + _dot(dcb.astype(BF16), bb16)
        db_ref[...] = db_acc + _dot(dcbt.astype(BF16), cb16)
        row = lax.broadcasted_iota(jnp.int32, (CHUNK, LANES), 0)
        dacs = dacs + jnp.where(row == CHUNK - 1, dtot, 0.0)
        da = _dot3_data_rhs(causal[1], dacs)
        ddt_ref[...] = da * a128_ref[...] + ddt_x
        dalog_ref[...] += jnp.sum(da * dt, axis=0, keepdims=True) * a128_ref[...]

    vec = lambda w: pl.BlockSpec((None, 1, w), lambda g, c: (g, 0, 0))
    blk = pl.BlockSpec((CHUNK, GROUP_W), lambda g, c: (last - c, g))
    return _call(
        body, name=name,
        out_shape=(jax.ShapeDtypeStruct((rows, D_INNER), F32), jax.ShapeDtypeStruct((rows, D_BC), F32),
                   jax.ShapeDtypeStruct((rows, D_BC), F32), jax.ShapeDtypeStruct((N_GROUPS, rows, LANES), F32),
                   jax.ShapeDtypeStruct((N_GROUPS, 1, LANES), F32), jax.ShapeDtypeStruct((N_GROUPS, 1, LANES), F32)),
        grid=(N_GROUPS, nc),
        in_specs=[blk,
                  pl.BlockSpec((CHUNK, LANES), lambda g, c: (last - c, bcol + g)),
                  pl.BlockSpec((CHUNK, LANES), lambda g, c: (last - c, bcol + N_GROUPS + g)),
                  blk, blk,
                  pl.BlockSpec((None, CHUNK, LANES), lambda g, c: (g, last - c, 0)),
                  vec(LANES), vec(GROUP_W), blk,
                  pl.BlockSpec((None, None, D_STATE, GROUP_W), lambda g, c: (g, last - c, 0, 0))],
        out_specs=(blk,
                   pl.BlockSpec((CHUNK, LANES), lambda g, c: (last - c, g)),
                   pl.BlockSpec((CHUNK, LANES), lambda g, c: (last - c, g)),
                   pl.BlockSpec((None, CHUNK, LANES), lambda g, c: (g, last - c, 0)),
                   vec(LANES), vec(LANES)),
        scratch_shapes=[pltpu.VMEM((D_STATE, GROUP_W), F32)],
        operands=[xbc, xbc, xbc, dt_exp, acs_exp, dt4, a128, dskexp, dy, states],
        semantics=("parallel", "arbitrary"), steps=steps)


def _attn_visible(b, heads=1):
    row = jnp.bitwise_and(lax.broadcasted_iota(jnp.int32, (heads * CHUNK, 3 * CHUNK), 0), CHUNK - 1)
    col = lax.broadcasted_iota(jnp.int32, (heads * CHUNK, 3 * CHUNK), 1)
    bb = b + jnp.zeros_like(col)
    meta = (col < CHUNK) & (bb >= 1) & (col >= PAD_ROWS)
    prev = (col >= CHUNK) & (col < 2 * CHUNK) & (bb >= 2) & ((col - CHUNK) > row)
    cur = (col >= 2 * CHUNK) & ((col - 2 * CHUNK) <= row) & ((bb >= 1) | ((col - 2 * CHUNK) >= PAD_ROWS))
    return meta | prev | cur


def _attn_visible4(b):
    return _attn_visible(b, 4)


def _stack_heads(q_ref, sink_ref, kvh, scale):
    lane = lax.broadcasted_iota(jnp.int32, (CHUNK, LANES), 1)
    parts, sinks = [], []
    for pp in range(2):
        pair = kvh * 2 + pp
        qp = q_ref[:, pair * LANES:(pair + 1) * LANES] * scale
        for e in range(2):
            mine = (lane < HEAD_DIM) if e == 0 else (lane >= HEAD_DIM)
            parts.append(jnp.where(mine, qp, 0.0).astype(BF16))
            sinks.append(jnp.full((CHUNK, 1), sink_ref[2 * pair + e], F32))
    return jnp.concatenate(parts, axis=0), jnp.concatenate(sinks, axis=0)


def _attn_probs(qm16, kc16, visible, sink):
    s = jnp.where(visible, _dot_nt(qm16, kc16), NEG_INF)
    m = jnp.maximum(jnp.max(s, axis=-1, keepdims=True), sink)
    pe = jnp.exp(s - m)
    pe_sink = jnp.exp(sink - m)
    inv = 1.0 / (jnp.sum(pe, axis=-1, keepdims=True) + pe_sink)
    return pe * inv, pe_sink * inv


def _attn_specs(colblock):
    blk = lambda f: pl.BlockSpec((CHUNK, 2 * D_KV), f)
    return [blk(lambda b: (0, colblock)), blk(lambda b: (jnp.maximum(b - 1, 0), colblock)), blk(lambda b: (b, colblock))]


def _attn_fwd(name, q, kv2, sinks, steps=()):
    rows = q.shape[0]

    def body(q_ref, k0, kp, kc, v0, vp, vc, sink_ref, o_ref):
        visible = _attn_visible4(pl.program_id(0))
        lane = lax.broadcasted_iota(jnp.int32, (CHUNK, LANES), 1)
        for kvh in range(N_KV_HEADS):
            ksl = slice(kvh * LANES, (kvh + 1) * LANES)
            kcat = jnp.concatenate([k0[:, ksl], kp[:, ksl], kc[:, ksl]], axis=0).astype(BF16)
            vcat = jnp.concatenate([v0[:, ksl], vp[:, ksl], vc[:, ksl]], axis=0).astype(BF16)
            q4, sink4 = _stack_heads(q_ref, sink_ref, kvh, ATTN_SCALE)
            pn, _ = _attn_probs(q4, kcat, visible, sink4)
            o4 = _dot(pn.astype(BF16), vcat)
            for pp in range(2):
                qsl = slice((kvh * 2 + pp) * LANES, (kvh * 2 + pp + 1) * LANES)
                o_ref[:, qsl] = jnp.where(lane < HEAD_DIM, o4[(2 * pp) * CHUNK:(2 * pp + 1) * CHUNK],
                                          o4[(2 * pp + 1) * CHUNK:(2 * pp + 2) * CHUNK]).astype(BF16)

    return _call(
        body, name=name, out_shape=jax.ShapeDtypeStruct((rows, D_MODEL), BF16), grid=(rows // CHUNK,),
        in_specs=[pl.BlockSpec((CHUNK, D_MODEL), lambda b: (b, 0))] + _attn_specs(0) + _attn_specs(1)
        + [pl.BlockSpec(memory_space=pltpu.SMEM)],
        out_specs=pl.BlockSpec((CHUNK, D_MODEL), lambda b: (b, 0)),
        operands=[q, kv2, kv2, kv2, kv2, kv2, kv2, sinks], semantics=("parallel",), steps=steps)


def _attn_bwd(name, q, kv2, sinks, do, steps=()):
    rows = q.shape[0]

    def body(q_ref, k0, kp, kc, v0, vp, vc, sink_ref, do_ref,
             dq_ref, dkc_ref, dkp_ref, dvc_ref, dvp_ref, dkm_ref, dvm_ref, dsink_ref):
        @pl.when(pl.program_id(0) == 0)
        def _():
            dkm_ref[...] = jnp.zeros_like(dkm_ref)
            dvm_ref[...] = jnp.zeros_like(dvm_ref)
            dsink_ref[...] = jnp.zeros_like(dsink_ref)

        visible = _attn_visible4(pl.program_id(0))
        lane = lax.broadcasted_iota(jnp.int32, (CHUNK, LANES), 1)
        lane1 = lax.broadcasted_iota(jnp.int32, (1, LANES), 1)
        dsink = jnp.zeros((1, LANES), F32)
        for kvh in range(N_KV_HEADS):
            ksl = slice(kvh * LANES, (kvh + 1) * LANES)
            kcat = jnp.concatenate([k0[:, ksl], kp[:, ksl], kc[:, ksl]], axis=0).astype(BF16)
            vcat = jnp.concatenate([v0[:, ksl], vp[:, ksl], vc[:, ksl]], axis=0).astype(BF16)
            q4, sink4 = _stack_heads(q_ref, sink_ref, kvh, ATTN_SCALE)
            do4, _ = _stack_heads(do_ref, sink_ref, kvh, 1.0)
            pn, psink = _attn_probs(q4, kcat, visible, sink4)
            dp = _dot_nt(do4, vcat)
            delta = jnp.sum(pn * dp, axis=-1, keepdims=True)
            ds = pn * (dp - delta)
            sink_terms = psink * delta
            dq4 = _dot(ds.astype(BF16), kcat)
            dk_acc = _dot(ds.T.astype(BF16), q4)
            dv_acc = _dot(pn.T.astype(BF16), do4)
            for j in range(4):
                part = jnp.sum(sink_terms[j * CHUNK:(j + 1) * CHUNK], axis=0, keepdims=True)
                dsink = dsink - jnp.where(lane1 == kvh * 4 + j, part, 0.0)
            for pp in range(2):
                qsl = slice((kvh * 2 + pp) * LANES, (kvh * 2 + pp + 1) * LANES)
                dq_pair = jnp.where(lane < HEAD_DIM, dq4[(2 * pp) * CHUNK:(2 * pp + 1) * CHUNK],
                                    dq4[(2 * pp + 1) * CHUNK:(2 * pp + 2) * CHUNK])
                dq_ref[:, qsl] = (dq_pair * ATTN_SCALE).astype(BF16)
            dkm_ref[:, ksl] += dk_acc[0:CHUNK]
            dvm_ref[:, ksl] += dv_acc[0:CHUNK]
            dkp_ref[:, ksl] = dk_acc[CHUNK:2 * CHUNK]
            dvp_ref[:, ksl] = dv_acc[CHUNK:2 * CHUNK]
            dkc_ref[:, ksl] = dk_acc[2 * CHUNK:3 * CHUNK]
            dvc_ref[:, ksl] = dv_acc[2 * CHUNK:3 * CHUNK]
        dsink_ref[...] += dsink

    qspec = pl.BlockSpec((CHUNK, D_MODEL), lambda b: (b, 0))
    kvspec = pl.BlockSpec((CHUNK, 2 * D_KV), lambda b: (b, 0))
    fixed = pl.BlockSpec((CHUNK, 2 * D_KV), lambda b: (0, 0))
    kv_shape = jax.ShapeDtypeStruct((rows, 2 * D_KV), F32)
    meta_shape = jax.ShapeDtypeStruct((CHUNK, 2 * D_KV), F32)
    return _call(
        body, name=name,
        out_shape=(jax.ShapeDtypeStruct((rows, D_MODEL), BF16), kv_shape, kv_shape, kv_shape, kv_shape,
                   meta_shape, meta_shape, jax.ShapeDtypeStruct((1, LANES), F32)),
        grid=(rows // CHUNK,),
        in_specs=[qspec] + _attn_specs(0) + _attn_specs(1) + [pl.BlockSpec(memory_space=pltpu.SMEM), qspec],
        out_specs=(qspec, kvspec, kvspec, kvspec, kvspec, fixed, fixed, pl.BlockSpec((1, LANES), lambda b: (0, 0))),
        operands=[q, kv2, kv2, kv2, kv2, kv2, kv2, sinks, do], semantics=("arbitrary",), steps=steps)


def _kv_grad_combine(name, dk_cur, dk_prev, dk_meta, dv_cur, dv_prev, dv_meta):
    rows = dk_cur.shape[0]
    nb = rows // CHUNK
    width = 2 * D_KV

    def body(kc_ref, kp_ref, km_ref, vc_ref, vp_ref, vm_ref, o_ref):
        jj = pl.program_id(0) + jnp.zeros((CHUNK, 1), jnp.int32)
        for half, (c_ref, p_ref, m_ref) in enumerate(((kc_ref, kp_ref, km_ref), (vc_ref, vp_ref, vm_ref))):
            total = c_ref[...] + jnp.where(jj < nb - 1, p_ref[...], 0.0) + jnp.where(jj == 0, m_ref[...], 0.0)
            o_ref[:, half * width:(half + 1) * width] = total.astype(BF16)

    blk = lambda f: pl.BlockSpec((CHUNK, width), f)
    three = lambda: [blk(lambda j: (j, 0)), blk(lambda j: (jnp.minimum(j + 1, nb - 1), 0)), blk(lambda j: (0, 0))]
    return pl.pallas_call(
        body, name=name, out_shape=jax.ShapeDtypeStruct((rows, 2 * width), BF16), grid=(nb,),
        in_specs=three() + three(), out_specs=pl.BlockSpec((CHUNK, 2 * width), lambda j: (j, 0)),
        compiler_params=_cparams(("parallel",)),
    )(dk_cur, dk_prev, dk_meta, dv_cur, dv_prev, dv_meta)


def _loss_head(name, h, target):
    rows = h.shape[0]

    def body(h_ref, t_ref, dh_ref, loss_ref):
        i = pl.program_id(0)
        real = (i + jnp.zeros((CHUNK, 1), jnp.int32)) >= 1
        diff = jnp.where(real, h_ref[...] - t_ref[...], 0.0)
        dh_ref[...] = diff * (1.0 / D_MODEL)

        @pl.when(i == 0)
        def _():
            loss_ref[...] = jnp.zeros_like(loss_ref)

        loss_ref[...] += jnp.sum(diff * diff) * (0.5 / D_MODEL)

    blk = pl.BlockSpec((CHUNK, D_MODEL), lambda i: (i, 0))
    return pl.pallas_call(
        body, name=name,
        out_shape=(jax.ShapeDtypeStruct((rows, D_MODEL), F32), jax.ShapeDtypeStruct((1, LANES), F32)),
        grid=(rows // CHUNK,),
        in_specs=[blk, pl.BlockSpec((CHUNK, D_MODEL), lambda i: (jnp.maximum(i - 1, 0), 0))],
        out_specs=(blk, pl.BlockSpec((1, LANES), lambda i: (0, 0))), compiler_params=_cparams(("arbitrary",)),
    )(h, target)


def _adamw(name, w, g, m, v):
    rows, width = w.shape
    tr = rows
    for cand in range(8, rows + 1, 8):
        if rows % cand == 0 and cand * width * 4 <= (1 << 20):
            tr = cand

    def body(w_ref, g_ref, m_ref, v_ref, d_ref, mo_ref, vo_ref):
        gv = g_ref[...]
        mn = ADAM_B1 * m_ref[...] + (1.0 - ADAM_B1) * gv
        vn = ADAM_B2 * v_ref[...] + (1.0 - ADAM_B2) * (gv * gv)
        m_hat = mn / (1.0 - ADAM_B1 ** ADAM_STEP)
        v_hat = vn / (1.0 - ADAM_B2 ** ADAM_STEP)
        d_ref[...] = -ADAM_LR * (m_hat / (jnp.sqrt(v_hat) + ADAM_EPS) + ADAM_WD * w_ref[...])
        mo_ref[...] = mn
        vo_ref[...] = vn

    blk = pl.BlockSpec((tr, width), lambda i: (i, 0))
    shp = jax.ShapeDtypeStruct((rows, width), F32)
    return pl.pallas_call(
        body, name=name, out_shape=(shp, shp, shp), grid=(rows // tr,), in_specs=[blk] * 4, out_specs=(blk,) * 3,
        compiler_params=_cparams(("parallel",)),
    )(w, g, m, v)


class _GivenWeights:
    def __init__(self, p):
        self.p = p
        self.grads = {}

    def weight(self, name, layer=None):
        return self.p[name] if layer is None else self.p[name][layer]

    def steps(self, kernel):
        return ()

    def grad(self, name, layer, g):
        self.grads[(name, layer)] = g


def _ffn_fwd(tag, h, p, i, plan):
    hn = _rms_fwd(f"ffn{tag}_norm", h, p["f_norm_pre"][i:i + 1])
    up = _mm_nn_bychip(f"ffn{tag}_up", hn, plan.weight("f_w_up", i))
    act = _ffn_conv_fwd(f"ffn{tag}_conv", up, p["f_conv_w"][i], p["f_conv_b"][i:i + 1], steps=plan.steps(f"ffn{tag}_conv"))
    pre = _mm(f"ffn{tag}_down", act, plan.weight("f_w_down", i), "nn")
    h_new = _resid_norm_fwd(f"ffn{tag}_resid", h, pre, p["f_norm_post"][i:i + 1])
    return h_new, (h, hn, up, act, pre)


def _ffn_bwd(tag, dh, saved, p, i, plan):
    h, hn, up, act, pre = saved
    dpre, g_post = _resid_norm_bwd(f"ffn{tag}_resid_bwd", dh, pre, p["f_norm_post"][i:i + 1])
    dact = _mm(f"ffn{tag}_down_dx", dpre, plan.weight("f_w_down", i), "nt")
    plan.grad("f_w_down", i, _mm(f"ffn{tag}_down_dw", act, dpre, "tn", out_dtype=BF16))
    dug, duv, gwg, gwv, gbg, gbv = _ffn_conv_bwd(f"ffn{tag}_conv_bwd", up, dact, p["f_conv_w"][i], p["f_conv_b"][i:i + 1],
                                                 steps=plan.steps(f"ffn{tag}_conv_bwd"))
    g_cw, g_cb = jnp.concatenate([gwg, gwv], axis=1), jnp.concatenate([gbg, gbv], axis=1)
    w_up = plan.weight("f_w_up", i)
    n = w_up.shape[2]
    dhn = _mm_nt_bychip(f"ffn{tag}_up_dx_gate", dug, w_up, 0)
    dhn = _mm_nt_bychip(f"ffn{tag}_up_dx_val", duv, w_up, N_CHIPS // 2, acc=dhn)
    g_up = _mm_tn_bychip(f"ffn{tag}_up_dw_gate", hn, dug, n, 0)
    plan.grad("f_w_up", i, _mm_tn_bychip(f"ffn{tag}_up_dw_val", hn, duv, n, N_CHIPS // 2, into=g_up))
    dh_new, g_pre = _norm_bwd_add(f"ffn{tag}_norm_bwd", dh, dhn, h, p["f_norm_pre"][i:i + 1])
    return dh_new, dict(f_norm_post=g_post, f_conv_w=g_cw, f_conv_b=g_cb, f_norm_pre=g_pre)


def _lanes_pad(a, width=LANES):
    return jnp.pad(a, [(0, 0)] * (a.ndim - 1) + [(0, width - a.shape[-1])])


def _dup_heads(w):
    rows = w.shape[0]
    w = w.reshape(rows, 2 * N_KV_HEADS, 1, HEAD_DIM)
    return jnp.broadcast_to(w, (rows, 2 * N_KV_HEADS, 2, HEAD_DIM)).reshape(rows, 4 * D_KV)


def _undup_heads(g):
    rows = g.shape[0]
    return g.reshape(rows, 2 * N_KV_HEADS, 2, HEAD_DIM).sum(axis=2).reshape(rows, 2 * D_KV)


def _local_step(x2, target, p, plan):
    seq = x2.shape[0]
    rows = seq + CHUNK
    g = {}

    h0 = jnp.concatenate([jnp.zeros((PAD_ROWS, D_MODEL), F32), p["meta_tokens"], x2], axis=0)

    w_in = plan.weight("a_w_in")
    w_dt = _lanes_pad(w_in[:, D_MAIN:])
    dt_bias = _lanes_pad(p["a_dt_bias"])
    a_neg = -jnp.exp(p["a_a_log"].reshape(N_GROUPS, HEADS_PER_GROUP))
    a128 = _lanes_pad(a_neg.reshape(N_GROUPS, 1, HEADS_PER_GROUP))
    dskexp = jnp.repeat(p["a_d_skip"].reshape(N_GROUPS, HEADS_PER_GROUP), HEAD_DIM, axis=1).reshape(N_GROUPS, 1, GROUP_W)

    hn0 = _rms_fwd("a_norm", h0, p["a_norm_pre"])
    zx = _mm("a_in_main", hn0, w_in, "nn", n_cols=D_MAIN, steps=plan.steps("a_in_main"))
    dtr = _mm("a_in_dt", hn0, w_dt, "nn")
    xbc = _conv4_fwd("a_conv", zx, p["a_conv_w"], p["a_conv_b"], steps=plan.steps("a_conv"))
    dt = _dt_fwd("a_dt", dtr, dt_bias)
    dt4 = _lanes_pad(dt[:, :SSM_HEADS].reshape(rows, N_GROUPS, HEADS_PER_GROUP).transpose(1, 0, 2))
    dt_exp, acs_exp = _ssd_prep("a_ssd_prep", dt4, a128, steps=plan.steps("a_ssd_prep"))
    y, states = _ssd_fwd("a_ssd", xbc, dt_exp, acs_exp, dskexp, steps=plan.steps("a_ssd"))
    yn = _gate_fwd("a_gate", y, zx, p["a_gate_norm"], steps=plan.steps("a_gate"))
    mix = _mm("a_out", yn, plan.weight("a_w_out"), "nn")
    h1 = _resid_norm_fwd("a_resid", h0, mix, p["a_norm_post"])

    h2, ffn0 = _ffn_fwd("0", h1, p, 0, plan)

    hkv = _rms_fwd("kv_norm", h2, p["kv_norm"])
    w_kv2 = _dup_heads(plan.weight("w_kv"))
    kv2 = _mm("kv_proj", hkv, w_kv2, "nn")
    hn2 = _rms_fwd("b_norm", h2, p["b_norm_pre"])
    q = _mm("b_q", hn2, plan.weight("b_w_q"), "nn")
    sinks = p["b_sinks"].reshape(N_Q_HEADS)
    o = _attn_fwd("b_attn", q, kv2, sinks, steps=plan.steps("b_attn"))
    attn = _mm("b_o", o, plan.weight("b_w_o"), "nn", steps=plan.steps("b_o"))
    h3 = _resid_norm_fwd("b_resid", h2, attn, p["b_norm_post"])

    h4, ffn1 = _ffn_fwd("1", h3, p, 1, plan)

    dh, loss_vec = _loss_head("loss", h4, target)
    loss = loss_vec[0, 0]

    dh, g1 = _ffn_bwd("1", dh, ffn1, p, 1, plan)

    dpre, g["b_norm_post"] = _resid_norm_bwd("b_resid_bwd", dh, attn, p["b_norm_post"])
    do = _mm("b_o_dx", dpre, plan.weight("b_w_o"), "nt")
    plan.grad("b_w_o", None, _mm("b_o_dw", o, dpre, "tn", out_dtype=BF16))
    dq, dkc, dkp, dvc, dvp, dkm, dvm, dsink = _attn_bwd("b_attn_bwd", q, kv2, sinks, do, steps=plan.steps("b_attn_bwd"))
    g["b_sinks"] = dsink[:, :N_Q_HEADS]
    dhn2 = _mm("b_q_dx", dq, plan.weight("b_w_q"), "nt")
    plan.grad("b_w_q", None, _mm("b_q_dw", hn2, dq, "tn", out_dtype=BF16))
    dh, g["b_norm_pre"] = _norm_bwd_add("b_norm_bwd", dh, dhn2, h2, p["b_norm_pre"])
    dkv2 = _kv_grad_combine("kv_grad", dkc, dkp, dkm, dvc, dvp, dvm)
    dhkv = _mm("kv_proj_dx", dkv2, w_kv2, "nt")
    plan.grad("w_kv", None, _undup_heads(_mm("kv_proj_dw", hkv, dkv2, "tn")))
    dh, g["kv_norm"] = _norm_bwd_add("kv_norm_bwd", dh, dhkv, h2, p["kv_norm"])

    dh, g0 = _ffn_bwd("0", dh, ffn0, p, 0, plan)
    for name in g0:
        if g0[name].shape[0] == 1:
            g[name] = jnp.concatenate([g0[name], g1[name]], axis=0)
        else:
            g[name] = jnp.stack([g0[name], g1[name]])

    dpre, g["a_norm_post"] = _resid_norm_bwd("a_resid_bwd", dh, mix, p["a_norm_post"])
    dyn = _mm("a_out_dx", dpre, plan.weight("a_w_out"), "nt")
    plan.grad("a_w_out", None, _mm("a_out_dw", yn, dpre, "tn", out_dtype=BF16))
    dy, dz, g["a_gate_norm"] = _gate_bwd("a_gate_bwd", dyn, y, zx, p["a_gate_norm"])
    dxs, dbm, dcm, ddt4, dalog, ddsk = _ssd_bwd("a_ssd_bwd", xbc, dt_exp, acs_exp, dt4, a128, dskexp, dy, states,
                                               steps=plan.steps("a_ssd_bwd"))
    g["a_a_log"] = dalog[:, 0, :HEADS_PER_GROUP].reshape(1, SSM_HEADS)
    g["a_d_skip"] = ddsk[:, 0, :HEADS_PER_GROUP].reshape(1, SSM_HEADS)
    ddt = _lanes_pad(ddt4[:, :, :HEADS_PER_GROUP].transpose(1, 0, 2).reshape(rows, SSM_HEADS))
    ddtr, dbias = _dt_bwd("a_dt_bwd", ddt, dtr, dt_bias)
    g["a_dt_bias"] = dbias[:, :SSM_HEADS]
    dxp, gw_x, gb_x = _conv4_bwd("a_conv_bwd_x", zx, dxs, p["a_conv_w"], p["a_conv_b"], 0)
    dbp, gw_b, gb_b = _conv4_bwd("a_conv_bwd_b", zx, dbm, p["a_conv_w"], p["a_conv_b"], D_INNER)
    dcp, gw_c, gb_c = _conv4_bwd("a_conv_bwd_c", zx, dcm, p["a_conv_w"], p["a_conv_b"], D_INNER + D_BC)
    g["a_conv_w"] = jnp.concatenate([gw_x, gw_b, gw_c], axis=1)
    g["a_conv_b"] = jnp.concatenate([gb_x, gb_b, gb_c], axis=1)
    dzx = jnp.concatenate([dz, dxp, dbp, dcp], axis=1)
    g_main = _mm("a_in_main_dw", hn0, dzx, "tn", out_dtype=BF16)
    g_dt = _mm("a_in_dt_dw", hn0, ddtr, "tn", out_dtype=BF16)
    plan.grad("a_w_in", None, jnp.concatenate([g_main, g_dt[:, :SSM_HEADS]], axis=1))
    dhn0 = _mm("a_in_main_dx", dzx, w_in, "nt")
    dhn0 = _mm("a_in_dt_dx", ddtr, w_dt, "nt", acc=dhn0)
    dh, g["a_norm_pre"] = _norm_bwd_add("a_norm_bwd", dh, dhn0, h0, p["a_norm_pre"])

    g["meta_tokens"] = dh[PAD_ROWS:CHUNK]
    return loss, dh[CHUNK:], g


ANY = pl.BlockSpec(memory_space=pl.ANY)
VMEM_SPEC = pl.BlockSpec(memory_space=pltpu.VMEM)


def _allgather_small(name, shard):
    rows = shard.shape[0]

    def body(s_ref, o_ref, send_sems, recv_sems):
        x, y, c = _place()
        me = 2 * x + y
        o_ref[me] = s_ref[...]
        chips = _other_chips(x, y)
        sends = [pltpu.make_async_remote_copy(s_ref, o_ref.at[me], send_sems.at[j], recv_sems.at[j],
                                              device_id=(cx, cy, c), device_id_type=MESH)
                 for j, (cx, cy) in enumerate(chips)]
        for cp in sends:
            cp.start()
        for j, (cx, cy) in enumerate(chips):
            pltpu.make_async_remote_copy(s_ref, o_ref.at[2 * cx + cy], send_sems.at[j], recv_sems.at[j],
                                         device_id=(cx, cy, c), device_id_type=MESH).wait_recv()
        for cp in sends:
            cp.wait_send()

    return pl.pallas_call(
        body, name=name, out_shape=jax.ShapeDtypeStruct((N_CHIPS, rows, LANES), F32),
        in_specs=[VMEM_SPEC], out_specs=VMEM_SPEC,
        scratch_shapes=[pltpu.SemaphoreType.DMA((3,)), pltpu.SemaphoreType.DMA((3,))],
        compiler_params=pltpu.CompilerParams(vmem_limit_bytes=VMEM_LIMIT),
    )(shard)


def _row_block(rows, width, itemsize, align, budget=2 << 20):
    best = None
    for cand in range(align, rows + 1, align):
        if rows % cand == 0 and cand * width * itemsize <= budget:
            best = cand
    assert best is not None, (rows, width)
    return best


def _cast_into_slot(name, chip, w2d):
    rows, width = w2d.shape
    tr = _row_block(rows, width, 4, 16)

    def body(chip_ref, w_ref, o_ref):
        o_ref[...] = w_ref[...].astype(BF16)

    return pl.pallas_call(
        body, name=name, out_shape=jax.ShapeDtypeStruct((N_CHIPS, rows, width), BF16),
        grid_spec=pltpu.PrefetchScalarGridSpec(
            num_scalar_prefetch=1, grid=(rows // tr,),
            in_specs=[pl.BlockSpec((tr, width), lambda i, chip_ref: (i, 0))],
            out_specs=pl.BlockSpec((None, tr, width), lambda i, chip_ref: (chip_ref[0], i, 0))),
        compiler_params=_cparams(("parallel",)),
    )(chip, w2d)


def _allreduce_small(name, vec):
    rows = vec.shape[0]

    def body(v_ref, o_ref, buf, send_sems, recv_sems):
        x, y, c = _place()
        me = 4 * x + 2 * y + c
        buf[me] = v_ref[...]

        def peer(k):
            kx, ky, kc = (k >> 2) & 1, (k >> 1) & 1, k & 1
            return (1 - x if kx else x, 1 - y if ky else y, 1 - c if kc else c)

        sends = []
        for k in range(1, N_DEV):
            cp = pltpu.make_async_remote_copy(v_ref, buf.at[me], send_sems.at[k - 1], recv_sems.at[k - 1],
                                              device_id=peer(k), device_id_type=MESH)
            cp.start()
            sends.append(cp)
        for k in range(1, N_DEV):
            px, py, pc = peer(k)
            pltpu.make_async_remote_copy(v_ref, buf.at[4 * px + 2 * py + pc], send_sems.at[k - 1], recv_sems.at[k - 1],
                                         device_id=(px, py, pc), device_id_type=MESH).wait_recv()
        for cp in sends:
            cp.wait_send()
        acc = buf[0]
        for d in range(1, N_DEV):
            acc = acc + buf[d]
        o_ref[...] = acc

    return pl.pallas_call(
        body, name=name, out_shape=jax.ShapeDtypeStruct((rows, LANES), F32),
        in_specs=[VMEM_SPEC], out_specs=VMEM_SPEC,
        scratch_shapes=[pltpu.VMEM((N_DEV, rows, LANES), F32), pltpu.SemaphoreType.DMA((N_DEV - 1,)),
                        pltpu.SemaphoreType.DMA((N_DEV - 1,))],
        compiler_params=pltpu.CompilerParams(vmem_limit_bytes=VMEM_LIMIT),
    )(vec)


def _rs_pair_add(name, place, grads, partner):
    _, half_rows, width = partner.shape
    tr = _row_block(half_rows, width, 2, 16)
    nb = half_rows // tr

    def body(place_ref, g_ref, p_ref, o_ref):
        o_ref[...] = (g_ref[...].astype(F32) + p_ref[...].astype(F32)).astype(BF16)

    return pl.pallas_call(
        body, name=name, out_shape=jax.ShapeDtypeStruct(partner.shape, BF16),
        grid_spec=pltpu.PrefetchScalarGridSpec(
            num_scalar_prefetch=1, grid=(N_CHIPS, nb),
            in_specs=[pl.BlockSpec((None, tr, width), lambda s, i, pr: (s, pr[1] * nb + i, 0)),
                      pl.BlockSpec((None, tr, width), lambda s, i, pr: (s, i, 0))],
            out_specs=pl.BlockSpec((None, tr, width), lambda s, i, pr: (s, i, 0))),
        compiler_params=_cparams(("parallel", "parallel")),
    )(place, grads, partner)


def _rs_chip_add(name, place, mine, others):
    _, half_rows, width = mine.shape
    tr = _row_block(half_rows, width, 4, 16, budget=1 << 20)
    nb = half_rows // tr

    def body(place_ref, q_ref, r_ref, o_ref):
        acc = q_ref[...].astype(F32)
        for j in range(3):
            acc = acc + r_ref[j].astype(F32)
        o_ref[...] = acc

    return pl.pallas_call(
        body, name=name, out_shape=jax.ShapeDtypeStruct((2 * half_rows, width), F32),
        grid_spec=pltpu.PrefetchScalarGridSpec(
            num_scalar_prefetch=1, grid=(nb,),
            in_specs=[pl.BlockSpec((None, tr, width), lambda i, pr: (pr[0], i, 0)),
                      pl.BlockSpec((3, tr, width), lambda i, pr: (0, i, 0))],
            out_specs=pl.BlockSpec((tr, width), lambda i, pr: (pr[1] * nb + i, 0))),
        compiler_params=_cparams(("parallel",)),
    )(place, mine, others)


WEIGHTS = ["meta_tokens", "a_norm_pre", "a_w_in", "a_conv_w", "a_conv_b", "a_dt_bias", "a_a_log", "a_d_skip",
           "a_gate_norm", "a_w_out", "a_norm_post", "kv_norm", "w_kv", "b_norm_pre", "b_w_q", "b_sinks", "b_w_o",
           "b_norm_post", "f_norm_pre", "f_w_up", "f_conv_w", "f_conv_b", "f_w_down", "f_norm_post"]
FULL_SHAPE = {
    "meta_tokens": (16, 1024), "a_norm_pre": (1, 1024), "a_w_in": (1, 1024, 5152), "a_conv_w": (1, 4, 3072),
    "a_conv_b": (1, 3072), "a_dt_bias": (1, 32), "a_a_log": (1, 32), "a_d_skip": (1, 32), "a_gate_norm": (1, 2048),
    "a_w_out": (1, 2048, 1024), "a_norm_post": (1, 1024), "kv_norm": (1024,), "w_kv": (1024, 512),
    "b_norm_pre": (1, 1024), "b_w_q": (1, 1024, 1024), "b_sinks": (1, 16), "b_w_o": (1, 1024, 1024),
    "b_norm_post": (1, 1024), "f_norm_pre": (2, 1024), "f_w_up": (2, 1024, 5632), "f_conv_w": (2, 3, 5632),
    "f_conv_b": (2, 5632), "f_w_down": (2, 2816, 1024), "f_norm_post": (2, 1024),
}
SHARD_AXIS = {
    "meta_tokens": 1, "a_norm_pre": 1, "a_w_in": 2, "a_conv_w": 2, "a_conv_b": 1, "a_dt_bias": None, "a_a_log": None,
    "a_d_skip": None, "a_gate_norm": 1, "a_w_out": 1, "a_norm_post": 1, "kv_norm": None, "w_kv": 0, "b_norm_pre": None,
    "b_w_q": 1, "b_sinks": None, "b_w_o": 1, "b_norm_post": None, "f_norm_pre": None, "f_w_up": 2, "f_conv_w": 2,
    "f_conv_b": None, "f_w_down": 1, "f_norm_post": None,
}
BIG = ["a_w_in", "a_w_out", "w_kv", "b_w_q", "b_w_o", "f_w_up", "f_w_down"]
SMALL = [n for n in WEIGHTS if n not in BIG]
SMALL_SHARDED = [n for n in SMALL if SHARD_AXIS[n] is not None]


def _shard_shape(name):
    shape = list(FULL_SHAPE[name])
    if SHARD_AXIS[name] is not None:
        shape[SHARD_AXIS[name]] //= N_CHIPS
    return tuple(shape)


def _numel(shape):
    return int(math.prod(shape))


SUBLANES = 8


def _packed_rows(shape):
    rows = -(-_numel(shape) // LANES)
    return -(-rows // SUBLANES) * SUBLANES


def _pack(arrays):
    parts = []
    for a in arrays:
        size, rows = _numel(a.shape), _packed_rows(a.shape)
        if size % LANES == 0:
            part = jnp.pad(a.reshape(size // LANES, LANES), ((0, rows - size // LANES), (0, 0)))
        else:
            part = jnp.pad(a.reshape(-1), (0, rows * LANES - size)).reshape(rows, LANES)
        parts.append(part)
    return jnp.concatenate(parts, axis=0)


def _unpack(packed, names, shape_of):
    out, off = {}, 0
    lead = packed.shape[:-2]
    for n in names:
        shape = tuple(shape_of(n))
        size, rows = _numel(shape), _packed_rows(shape)
        part = packed[..., off:off + rows, :]
        if size % LANES == 0:
            out[n] = part[..., :size // LANES, :].reshape(lead + shape)
        else:
            out[n] = part.reshape(lead + (rows * LANES,))[..., :size].reshape(lead + shape)
        off += rows
    return out


def _split_chips(name, full):
    ax = SHARD_AXIS[name]
    shape = full.shape
    cut = shape[:ax] + (N_CHIPS, shape[ax] // N_CHIPS) + shape[ax + 1:]
    return jnp.moveaxis(full.reshape(cut), ax, 0)


def _join_chips(name, stacked):
    ax = SHARD_AXIS[name]
    moved = jnp.moveaxis(stacked, 0, ax)
    shape = moved.shape
    return moved.reshape(shape[:ax] + (shape[ax] * shape[ax + 1],) + shape[ax + 2:])


def _as2d(a):
    return a.reshape(-1, a.shape[-1])


BUFFERS = [("a_w_in", "a_w_in", None), ("a_w_out", "a_w_out", None), ("w_kv", "w_kv", None),
           ("b_w_q", "b_w_q", None), ("b_w_o", "b_w_o", None), ("f_w_up0", "f_w_up", 0), ("f_w_up1", "f_w_up", 1),
           ("f_w_down0", "f_w_down", 0), ("f_w_down1", "f_w_down", 1)]


def _local_shard(arrays, weight, layer):
    return _as2d(arrays[weight]) if layer is None else arrays[weight][layer]


def _weight_from_gathered(weight, buf):
    if weight == "a_w_in":
        return buf.transpose(1, 0, 2).reshape(buf.shape[1], N_CHIPS * buf.shape[2])
    if weight == "f_w_up":
        return buf
    return buf.reshape(N_CHIPS * buf.shape[1], buf.shape[2])


def _gathered_from_grad(weight, g):
    if weight == "a_w_in":
        rows = g.shape[0]
        return g.reshape(rows, N_CHIPS, g.shape[1] // N_CHIPS).transpose(1, 0, 2).astype(BF16)
    if weight == "f_w_up":
        return g
    return g.reshape(N_CHIPS, g.shape[0] // N_CHIPS, g.shape[1]).astype(BF16)


GATHER_SCHEDULE = {
    "a_in_main": [("ici", ["a_w_out"])],
    "a_conv": [("d2d", ["a_w_out"]), ("ici", ["f_w_down0"])],
    "a_ssd_prep": [("d2d", ["f_w_down0"]), ("ici", ["w_kv", "b_w_q", "b_w_o"])],
    "a_ssd": [("d2d", ["w_kv", "b_w_q", "b_w_o"]), ("ici", ["f_w_up0"])],
    "a_gate": [("d2d", ["f_w_up0"])],
    "ffn0_conv": [("ici", ["f_w_down1"])],
    "b_attn": [("d2d", ["f_w_down1"]), ("ici", ["f_w_up1"])],
    "b_o": [("d2d", ["f_w_up1"])],
}
REDUCE_SCHEDULE = {
    "ffn1_conv_bwd": ["f_w_down1"],
    "b_attn_bwd": ["f_w_up1", "b_w_o"],
    "ffn0_conv_bwd": ["b_w_q", "w_kv", "f_w_down0"],
    "a_ssd_bwd": ["f_w_up0", "a_w_out"],
}
REDUCE_LAST = ["a_w_in"]


def _buffer_of(weight, layer):
    return weight if layer is None else f"{weight}{layer}"


class _Pipeline:
    def __init__(self, place, slots):
        self.place = place
        self.slots = dict(slots)
        self.running = []
        self.grads = {}
        self.partials = {}
        self.peers = {}

    def _collect(self):
        for step, buffers, table in self.running:
            table.update(zip(buffers, step.results))
        self.running = []

    def gather_now(self, name, buffers):
        step = _step_gather_full([self.slots[b] for b in buffers])
        _run_steps(name, [step])
        self.slots.update(zip(buffers, step.results))

    def weight(self, name, layer=None):
        self._collect()
        return _weight_from_gathered(name, self.slots[_buffer_of(name, layer)])

    def grad(self, name, layer, g):
        self.grads[_buffer_of(name, layer)] = _gathered_from_grad(name, g)

    def _pair_sums(self, name, buffers):
        step = _step_pair_exchange([self.grads[b] for b in buffers])
        _run_steps(name, [step])
        for b, theirs in zip(buffers, step.results):
            self.partials[b] = _rs_pair_add("reduce_pair_add_" + b, self.place, self.grads[b], theirs)

    def steps(self, kernel):
        self._collect()
        steps = []
        for phase, buffers in GATHER_SCHEDULE.get(kernel, []):
            make = _step_gather_ici if phase == "ici" else _step_gather_d2d
            step = make([self.slots[b] for b in buffers])
            self.running.append((step, buffers, self.slots))
            steps.append(step)
        buffers = REDUCE_SCHEDULE.get(kernel)
        if buffers:
            self._pair_sums("reduce_pair_send_" + kernel, buffers)
            step = _step_chip_exchange([self.partials[b] for b in buffers])
            self.running.append((step, buffers, self.peers))
            steps.append(step)
        return steps

    def finish(self):
        self._collect()
        self._pair_sums("reduce_pair_send_last", REDUCE_LAST)
        step = _step_chip_exchange([self.partials[b] for b in REDUCE_LAST])
        _run_steps("reduce_chip_send_last", [step])
        self.peers.update(zip(REDUCE_LAST, step.results))
        names = [b for b, _, _ in BUFFERS]
        halves = [_rs_chip_add("reduce_chip_add_" + b, self.place, self.partials[b], self.peers[b]) for b in names]
        step = _step_pair_gather(halves)
        _run_steps("reduce_pair_gather", [step])
        return dict(zip(names, step.results))


def kernel(x, meta_tokens, a_norm_pre, a_w_in, a_conv_w, a_conv_b, a_dt_bias, a_a_log, a_d_skip, a_gate_norm, a_w_out, a_norm_post, kv_norm, w_kv, b_norm_pre, b_w_q, b_sinks, b_w_o, b_norm_post, f_norm_pre, f_w_up, f_conv_w, f_conv_b, f_w_down, f_norm_post, loss_target, m_meta_tokens, m_a_norm_pre, m_a_w_in, m_a_conv_w, m_a_conv_b, m_a_dt_bias, m_a_a_log, m_a_d_skip, m_a_gate_norm, m_a_w_out, m_a_norm_post, m_kv_norm, m_w_kv, m_b_norm_pre, m_b_w_q, m_b_sinks, m_b_w_o, m_b_norm_post, m_f_norm_pre, m_f_w_up, m_f_conv_w, m_f_conv_b, m_f_w_down, m_f_norm_post, v_meta_tokens, v_a_norm_pre, v_a_w_in, v_a_conv_w, v_a_conv_b, v_a_dt_bias, v_a_a_log, v_a_d_skip, v_a_gate_norm, v_a_w_out, v_a_norm_post, v_kv_norm, v_w_kv, v_b_norm_pre, v_b_w_q, v_b_sinks, v_b_w_o, v_b_norm_post, v_f_norm_pre, v_f_w_up, v_f_conv_w, v_f_conv_b, v_f_w_down, v_f_norm_post):
    given = dict(locals())
    w = {n: given[n] for n in WEIGHTS}
    mom = {n: given["m_" + n] for n in WEIGHTS}
    var = {n: given["v_" + n] for n in WEIGHTS}
    chip = 2 * lax.axis_index("x") + lax.axis_index("y")
    core = lax.axis_index("c")
    place = jnp.stack([chip, core]).astype(jnp.int32)

    small_all = _allgather_small("gather_small", _pack([w[n] for n in SMALL_SHARDED]))
    small_parts = _unpack(small_all, SMALL_SHARDED, _shard_shape)
    slots = {b: _cast_into_slot("cast_" + b, place, _local_shard(w, wn, layer)) for b, wn, layer in BUFFERS}
    pipeline = _Pipeline(place, slots)
    pipeline.gather_now("gather_first", ["a_w_in"])
    p = {}
    for n in SMALL:
        p[n] = _join_chips(n, small_parts[n]) if n in SMALL_SHARDED else w[n]
    p["a_conv_w"] = p["a_conv_w"][0]
    p["kv_norm"] = p["kv_norm"].reshape(1, D_MODEL)

    loss_local, grad_x, g = _local_step(x[0], loss_target[0], p, pipeline)
    loss = lax.psum(loss_local, ("x", "y", "c"))

    small_sum = _allreduce_small("reduce_small", _pack([g[n].reshape(FULL_SHAPE[n]) for n in SMALL]))
    small_red = _unpack(small_sum, SMALL, lambda n: FULL_SHAPE[n])
    grads = {}
    for n in SMALL:
        if SHARD_AXIS[n] is None:
            grads[n] = small_red[n]
        else:
            grads[n] = lax.dynamic_index_in_dim(_split_chips(n, small_red[n]), chip, 0, keepdims=False)

    shard_sum = pipeline.finish()
    for n in BIG:
        if n in ("f_w_up", "f_w_down"):
            grads[n] = jnp.stack([shard_sum[n + "0"], shard_sum[n + "1"]])
        else:
            grads[n] = shard_sum[n].reshape(_shard_shape(n))

    delta, new_m, new_v = {}, {}, {}
    for n in BIG:
        shape = _shard_shape(n)
        d, m2, v2 = _adamw("adamw_" + n, _as2d(w[n]), _as2d(grads[n]), _as2d(mom[n]), _as2d(var[n]))
        delta[n], new_m[n], new_v[n] = d.reshape(shape), m2.reshape(shape), v2.reshape(shape)
    packed = [_pack([src[n].reshape(_shard_shape(n)) for n in SMALL]) for src in (w, grads, mom, var)]
    outs = _adamw("adamw_small", *packed)
    for dst, flat in zip((delta, new_m, new_v), outs):
        dst.update(_unpack(flat, SMALL, _shard_shape))

    return (loss, grad_x[None], *[grads[n].reshape(_shard_shape(n)) for n in WEIGHTS],
            *[delta[n] for n in WEIGHTS], *[new_m[n] for n in WEIGHTS], *[new_v[n] for n in WEIGHTS])
```

```python
import functools
import math

import jax
import jax.numpy as jnp
from jax import lax
from jax.experimental import pallas as pl
from jax.experimental.pallas import tpu as pltpu

F32, BF16 = jnp.float32, jnp.bfloat16
MESH = pl.DeviceIdType.MESH

D_MODEL = 1024
N_META = 16
CHUNK = 128
PAD_ROWS = CHUNK - N_META
D_INNER = 2048
D_STATE = 128
N_GROUPS = 4
HEADS_PER_GROUP = 8
SSM_HEADS = 32
HEAD_DIM = 64
D_BC = N_GROUPS * D_STATE
D_XBC = D_INNER + 2 * D_BC
D_MAIN = D_INNER + D_XBC
D_IN_PROJ = D_MAIN + SSM_HEADS
GROUP_W = HEADS_PER_GROUP * HEAD_DIM
SSM_CONV = 4
D_FF = 2816
FFN_CONV = 3
N_Q_HEADS = 16
N_KV_HEADS = 4
D_KV = 256
ATTN_SCALE = 1.0 / math.sqrt(HEAD_DIM)
RMS_EPS = 1e-6
NEG_INF = -1e30
LANES = 128
VMEM_LIMIT = 48 * 1024 * 1024

ADAM_LR, ADAM_B1, ADAM_B2, ADAM_EPS, ADAM_WD, ADAM_STEP = 0.001, 0.9, 0.999, 1e-08, 0.01, 10

N_CHIPS = 4
N_DEV = 8


def _cparams(sem=None):
    return pltpu.CompilerParams(dimension_semantics=sem, vmem_limit_bytes=VMEM_LIMIT)


def _tile(n, cands=(512, 256, 128)):
    for t in cands:
        if n % t == 0:
            return t
    return n


def _row_tile(rows, width):
    for t in (544, 272):
        if rows % t == 0 and t * width * 4 <= (3 << 20):
            return t
    return 128


def _rows_mask(i, tm):
    rows = i * tm + lax.broadcasted_iota(jnp.int32, (tm, 1), 0)
    return rows >= PAD_ROWS


def _dot(a, b):
    return jnp.dot(a, b, preferred_element_type=F32)


def _dot_nt(a, b):
    return lax.dot_general(a, b, (((1,), (1,)), ((), ())), preferred_element_type=F32)


def _dot_tn(a, b):
    return lax.dot_general(a, b, (((0,), (0,)), ((), ())), preferred_element_type=F32)


def _sigmoid(x):
    return 1.0 / (1.0 + jnp.exp(-x))


def _place():
    return lax.axis_index("x"), lax.axis_index("y"), lax.axis_index("c")


def _other_chips(x, y):
    return [(1 - x, y), (x, 1 - y), (1 - x, 1 - y)]


class _Step:
    def __init__(self, ins, outs, aliases, n_sems, start, finish):
        self.ins, self.outs, self.aliases, self.n_sems = list(ins), list(outs), dict(aliases), n_sems
        self.start, self.finish = start, finish
        self.results = None


def _like(a):
    return jax.ShapeDtypeStruct(a.shape, a.dtype)


def _remote(src, dst, send_sems, recv_sems, k, device):
    return pltpu.make_async_remote_copy(src, dst, send_sems.at[k], recv_sems.at[k], device_id=device, device_id_type=MESH)


def _half_rows(ref, axis, which):
    hr = ref.shape[axis] // 2
    return pl.ds(which * hr, hr)


def _step_gather_ici(bufs):
    def copies(outs, send_sems, recv_sems, received):
        x, y, c = _place()
        me = 2 * x + y
        for k, o in enumerate(outs):
            for j, (cx, cy) in enumerate(_other_chips(x, y)):
                part = o.at[2 * cx + cy if received else me, _half_rows(o, 1, c)]
                yield _remote(part, part, send_sems, recv_sems, 3 * k + j, (cx, cy, c))

    def start(ins, outs, send_sems, recv_sems):
        for cp in copies(outs, send_sems, recv_sems, False):
            cp.start()

    def finish(ins, outs, send_sems, recv_sems):
        for cp in copies(outs, send_sems, recv_sems, True):
            cp.wait_recv()
        for cp in copies(outs, send_sems, recv_sems, False):
            cp.wait_send()

    return _Step(bufs, [_like(b) for b in bufs], {k: k for k in range(len(bufs))}, 3 * len(bufs), start, finish)


def _step_gather_d2d(bufs):
    def copies(outs, send_sems, recv_sems, received):
        x, y, c = _place()
        for k, o in enumerate(outs):
            for j, (cx, cy) in enumerate(_other_chips(x, y)):
                part = o.at[2 * cx + cy, _half_rows(o, 1, 1 - c if received else c)]
                yield _remote(part, part, send_sems, recv_sems, 3 * k + j, (x, y, 1 - c))

    def start(ins, outs, send_sems, recv_sems):
        for cp in copies(outs, send_sems, recv_sems, False):
            cp.start()

    def finish(ins, outs, send_sems, recv_sems):
        for cp in copies(outs, send_sems, recv_sems, True):
            cp.wait_recv()
        for cp in copies(outs, send_sems, recv_sems, False):
            cp.wait_send()

    return _Step(bufs, [_like(b) for b in bufs], {k: k for k in range(len(bufs))}, 3 * len(bufs), start, finish)


def _step_gather_full(bufs):
    n = len(bufs)

    def ici(outs, send_sems, recv_sems, received):
        x, y, c = _place()
        me = 2 * x + y
        for k, o in enumerate(outs):
            for j, (cx, cy) in enumerate(_other_chips(x, y)):
                part = o.at[2 * cx + cy if received else me, _half_rows(o, 1, c)]
                yield _remote(part, part, send_sems, recv_sems, 3 * k + j, (cx, cy, c))

    def d2d(outs, send_sems, recv_sems, received):
        x, y, c = _place()
        for k, o in enumerate(outs):
            for j, (cx, cy) in enumerate(_other_chips(x, y)):
                part = o.at[2 * cx + cy, _half_rows(o, 1, 1 - c if received else c)]
                yield _remote(part, part, send_sems, recv_sems, 3 * n + 3 * k + j, (x, y, 1 - c))

    def start(ins, outs, send_sems, recv_sems):
        for cp in ici(outs, send_sems, recv_sems, False):
            cp.start()

    def finish(ins, outs, send_sems, recv_sems):
        for arrived, onward in zip(ici(outs, send_sems, recv_sems, True), d2d(outs, send_sems, recv_sems, False)):
            arrived.wait_recv()
            onward.start()
        for cp in d2d(outs, send_sems, recv_sems, True):
            cp.wait_recv()
        for cp in ici(outs, send_sems, recv_sems, False):
            cp.wait_send()
        for cp in d2d(outs, send_sems, recv_sems, False):
            cp.wait_send()

    return _Step(bufs, [_like(b) for b in bufs], {k: k for k in range(n)}, 6 * n, start, finish)


def _step_pair_exchange(grads):
    def copies(ins, outs, send_sems, recv_sems):
        x, y, c = _place()
        for k, (g, o) in enumerate(zip(ins, outs)):
            yield _remote(g.at[:, _half_rows(g, 1, 1 - c)], o, send_sems, recv_sems, k, (x, y, 1 - c))

    def start(ins, outs, send_sems, recv_sems):
        for cp in copies(ins, outs, send_sems, recv_sems):
            cp.start()

    def finish(ins, outs, send_sems, recv_sems):
        for cp in copies(ins, outs, send_sems, recv_sems):
            cp.wait()

    outs = [jax.ShapeDtypeStruct((N_CHIPS, g.shape[1] // 2, g.shape[2]), g.dtype) for g in grads]
    return _Step(grads, outs, {}, len(grads), start, finish)


def _step_chip_exchange(partials):
    def copies(ins, outs, send_sems, recv_sems):
        x, y, c = _place()
        for k, (q, o) in enumerate(zip(ins, outs)):
            for j, (cx, cy) in enumerate(_other_chips(x, y)):
                yield _remote(q.at[2 * cx + cy], o.at[j], send_sems, recv_sems, 3 * k + j, (cx, cy, c))

    def start(ins, outs, send_sems, recv_sems):
        for cp in copies(ins, outs, send_sems, recv_sems):
            cp.start()

    def finish(ins, outs, send_sems, recv_sems):
        for cp in copies(ins, outs, send_sems, recv_sems):
            cp.wait()

    outs = [jax.ShapeDtypeStruct((3,) + q.shape[1:], q.dtype) for q in partials]
    return _Step(partials, outs, {}, 3 * len(partials), start, finish)


def _step_pair_gather(shards):
    def copies(outs, send_sems, recv_sems, received):
        x, y, c = _place()
        for k, o in enumerate(outs):
            part = o.at[_half_rows(o, 0, 1 - c if received else c)]
            yield _remote(part, part, send_sems, recv_sems, k, (x, y, 1 - c))

    def start(ins, outs, send_sems, recv_sems):
        for cp in copies(outs, send_sems, recv_sems, False):
            cp.start()

    def finish(ins, outs, send_sems, recv_sems):
        for cp in copies(outs, send_sems, recv_sems, True):
            cp.wait_recv()
        for cp in copies(outs, send_sems, recv_sems, False):
            cp.wait_send()

    return _Step(shards, [_like(s) for s in shards], {k: k for k in range(len(shards))}, len(shards), start, finish)


def _call(body, *, name, out_shape, grid, in_specs, out_specs, operands, scratch_shapes=(), semantics=None, steps=()):
    single = not isinstance(out_shape, (tuple, list))
    out_shapes = [out_shape] if single else list(out_shape)
    out_spec_list = [out_specs] if single else list(out_specs)
    steps = list(steps)
    if not steps:
        res = pl.pallas_call(body, name=name, out_shape=out_shapes, grid=grid, in_specs=list(in_specs),
                             out_specs=out_spec_list, scratch_shapes=list(scratch_shapes),
                             compiler_params=_cparams(semantics))(*operands)
        return res[0] if single else res
    n_in, n_out, n_scr = len(operands), len(out_shapes), len(scratch_shapes)
    x_in = [a for s in steps for a in s.ins]
    x_out = [o for s in steps for o in s.outs]
    aliases, in_off, out_off = {}, 0, 0
    for s in steps:
        for i, o in s.aliases.items():
            aliases[n_in + in_off + i] = n_out + out_off + o
        in_off += len(s.ins)
        out_off += len(s.outs)
    sems = []
    for s in steps:
        sems += [pltpu.SemaphoreType.DMA((s.n_sems,)), pltpu.SemaphoreType.DMA((s.n_sems,))]
    any_spec = pl.BlockSpec(memory_space=pl.ANY)

    def carried(*refs):
        pos = 0
        ins = refs[pos:pos + n_in]; pos += n_in
        xi = refs[pos:pos + len(x_in)]; pos += len(x_in)
        outs = refs[pos:pos + n_out]; pos += n_out
        xo = refs[pos:pos + len(x_out)]; pos += len(x_out)
        scr = refs[pos:pos + n_scr]; pos += n_scr
        sem_refs = refs[pos:]

        def each(action):
            i0 = o0 = 0
            for k, s in enumerate(steps):
                getattr(s, action)(xi[i0:i0 + len(s.ins)], xo[o0:o0 + len(s.outs)], sem_refs[2 * k], sem_refs[2 * k + 1])
                i0 += len(s.ins)
                o0 += len(s.outs)

        if grid:
            first = functools.reduce(jnp.logical_and, [pl.program_id(d) == 0 for d in range(len(grid))])
            last = functools.reduce(jnp.logical_and, [pl.program_id(d) == grid[d] - 1 for d in range(len(grid))])
            pl.when(first)(lambda: each("start"))
            body(*ins, *outs, *scr)
            pl.when(last)(lambda: each("finish"))
        else:
            each("start")
            body(*ins, *outs, *scr)
            each("finish")

    res = pl.pallas_call(
        carried, name=name, out_shape=out_shapes + x_out, grid=grid,
        in_specs=list(in_specs) + [any_spec] * len(x_in), out_specs=out_spec_list + [any_spec] * len(x_out),
        scratch_shapes=list(scratch_shapes) + sems, input_output_aliases=aliases,
        compiler_params=_cparams(None if semantics is None else ("arbitrary",) * len(grid)),
    )(*operands, *x_in)
    o0 = n_out
    for s in steps:
        s.results = list(res[o0:o0 + len(s.outs)])
        o0 += len(s.outs)
    return res[0] if single else tuple(res[:n_out])


def _run_steps(name, steps):
    _call(lambda: None, name=name, out_shape=[], grid=(), in_specs=[], out_specs=[], operands=[], steps=steps)
    return [s.results for s in steps]


def _mm(name, a, b, mode, out_dtype=F32, acc=None, b_colblock=0, n_cols=None, steps=()):
    resident_bytes = 8 << 20
    if mode == "nn":
        m, k = a.shape
        n = n_cols or b.shape[1]
        tm = m
        while tm * k * 2 > resident_bytes and tm % 32 == 0:
            tm //= 2
        tn = _tile(n)
        grid = (m // tm, n // tn)
        in_specs = [pl.BlockSpec((tm, k), lambda i, j: (i, 0)), pl.BlockSpec((k, tn), lambda i, j: (0, j))]
        out_shape, out_block = (m, n), (tm, tn)
    elif mode == "nt":
        m, n = a.shape
        k = b.shape[0]
        tm = m
        while tm * n * 2 > resident_bytes and tm % 32 == 0:
            tm //= 2
        tk = _tile(k)
        grid = (m // tm, k // tk)
        in_specs = [pl.BlockSpec((tm, n), lambda i, j: (i, 0)), pl.BlockSpec((tk, n), lambda i, j: (j, b_colblock))]
        out_shape, out_block = (m, k), (tm, tk)
    else:
        m, k = a.shape
        n = b.shape[1]
        tk, tn = _tile(k), _tile(n)
        grid = (k // tk, n // tn)
        in_specs = [pl.BlockSpec((m, tk), lambda i, j: (0, i)), pl.BlockSpec((m, tn), lambda i, j: (0, j))]
        out_shape, out_block = (k, n), (tk, tn)
    out_spec = pl.BlockSpec(out_block, lambda i, j: (i, j))
    has_acc = acc is not None

    def body(*refs):
        a_ref, b_ref = refs[0], refs[1]
        o_ref = refs[-1]
        av, bv = a_ref[...], b_ref[...]
        if mode == "nn":
            r = _dot(av, bv)
        elif mode == "nt":
            r = _dot_nt(av, bv)
        else:
            r = _dot_tn(av, bv)
        if has_acc:
            r = r + refs[2][...]
        o_ref[...] = r.astype(o_ref.dtype)

    operands = [a, b]
    if has_acc:
        in_specs = in_specs + [out_spec]
        operands.append(acc)
    return _call(body, name=name, out_shape=jax.ShapeDtypeStruct(out_shape, out_dtype), grid=grid, in_specs=in_specs,
                 out_specs=out_spec, operands=operands, semantics=("parallel", "parallel"), steps=steps)


def _fit_rows(m, row_bytes, budget=8 << 20):
    tm = m
    while tm * row_bytes > budget and tm % 32 == 0:
        tm //= 2
    return tm


def _mm_nn_bychip(name, a, bc):
    m, k = a.shape
    n = bc.shape[2]
    tm = min(_fit_rows(m, k * 2), _fit_rows(m, n * 4))

    def body(a_ref, b_ref, o_ref):
        o_ref[...] = _dot(a_ref[...], b_ref[...])

    return pl.pallas_call(
        body, name=name, out_shape=jax.ShapeDtypeStruct((m, N_CHIPS * n), F32), grid=(m // tm, N_CHIPS),
        in_specs=[pl.BlockSpec((tm, k), lambda i, c: (i, 0)), pl.BlockSpec((None, k, n), lambda i, c: (c, 0, 0))],
        out_specs=pl.BlockSpec((tm, n), lambda i, c: (i, c)), compiler_params=_cparams(("parallel", "parallel")),
    )(a, bc)


def _mm_nt_bychip(name, a, bc, chip0, acc=None):
    m = a.shape[0]
    _, k, n = bc.shape
    nch = a.shape[1] // n
    tm, tk = _fit_rows(m, n * 2), _tile(k)
    has_acc = acc is not None

    def body(*refs):
        a_ref, b_ref, o_ref = refs[0], refs[1], refs[-1]

        @pl.when(pl.program_id(2) == 0)
        def _():
            o_ref[...] = refs[2][...] if has_acc else jnp.zeros_like(o_ref)

        o_ref[...] += _dot_nt(a_ref[...], b_ref[...])

    out_spec = pl.BlockSpec((tm, tk), lambda i, j, c: (i, j))
    in_specs = [pl.BlockSpec((tm, n), lambda i, j, c: (i, c)),
                pl.BlockSpec((None, tk, n), lambda i, j, c: (chip0 + c, j, 0))]
    operands = [a, bc]
    if has_acc:
        in_specs.append(out_spec)
        operands.append(acc)
    return pl.pallas_call(
        body, name=name, out_shape=jax.ShapeDtypeStruct((m, k), F32), grid=(m // tm, k // tk, nch),
        in_specs=in_specs, out_specs=out_spec, compiler_params=_cparams(("parallel", "parallel", "arbitrary")),
    )(*operands)


def _mm_tn_bychip(name, a, dy, n, chip0, into=None):
    m, k = a.shape
    nch = dy.shape[1] // n
    tk = _tile(k)

    def body(*refs):
        a_ref, d_ref, o_ref = refs[0], refs[1], refs[-1]
        o_ref[...] = _dot_tn(a_ref[...], d_ref[...]).astype(BF16)

    in_specs = [pl.BlockSpec((m, tk), lambda i, c: (0, i)), pl.BlockSpec((m, n), lambda i, c: (0, c))]
    operands = [a, dy]
    aliases = {}
    if into is not None:
        in_specs.append(pl.BlockSpec(memory_space=pl.ANY))
        operands.append(into)
        aliases = {2: 0}
    return pl.pallas_call(
        body, name=name, out_shape=jax.ShapeDtypeStruct((N_CHIPS, k, n), BF16), grid=(k // tk, nch),
        in_specs=in_specs, out_specs=pl.BlockSpec((None, tk, n), lambda i, c: (chip0 + c, i, 0)),
        input_output_aliases=aliases, compiler_params=_cparams(("parallel", "parallel")),
    )(*operands)


def _rms_fwd(name, h, w):
    rows, width = h.shape
    tm = _row_tile(rows, width)

    def body(h_ref, w_ref, o_ref):
        x = h_ref[...]
        r = lax.rsqrt(jnp.mean(x * x, axis=-1, keepdims=True) + RMS_EPS)
        o_ref[...] = (x * r * w_ref[...]).astype(BF16)

    return pl.pallas_call(
        body, name=name, out_shape=jax.ShapeDtypeStruct((rows, width), BF16), grid=(rows // tm,),
        in_specs=[pl.BlockSpec((tm, width), lambda i: (i, 0)), pl.BlockSpec((1, width), lambda i: (0, 0))],
        out_specs=pl.BlockSpec((tm, width), lambda i: (i, 0)), compiler_params=_cparams(("parallel",)),
    )(h, w)


def _resid_norm_fwd(name, h, pre, w):
    rows, width = h.shape
    tm = _row_tile(rows, width)

    def body(h_ref, p_ref, w_ref, o_ref):
        p = p_ref[...]
        r = lax.rsqrt(jnp.mean(p * p, axis=-1, keepdims=True) + RMS_EPS)
        o_ref[...] = h_ref[...] + jnp.where(_rows_mask(pl.program_id(0), tm), p * r * w_ref[...], 0.0)

    row_spec = pl.BlockSpec((tm, width), lambda i: (i, 0))
    return pl.pallas_call(
        body, name=name, out_shape=jax.ShapeDtypeStruct((rows, width), F32), grid=(rows // tm,),
        in_specs=[row_spec, row_spec, pl.BlockSpec((1, width), lambda i: (0, 0))],
        out_specs=row_spec, compiler_params=_cparams(("parallel",)),
    )(h, pre, w)


def _resid_norm_bwd(name, dh, pre, w):
    rows, width = dh.shape
    tm = _row_tile(rows, width)

    def body(dh_ref, p_ref, w_ref, dp_ref, dw_ref):
        i = pl.program_id(0)
        dy = jnp.where(_rows_mask(i, tm), dh_ref[...], 0.0)
        p = p_ref[...]
        r = lax.rsqrt(jnp.mean(p * p, axis=-1, keepdims=True) + RMS_EPS)
        xhat = p * r
        dxhat = dy * w_ref[...]
        dp = r * (dxhat - xhat * jnp.mean(dxhat * xhat, axis=-1, keepdims=True))
        dp_ref[...] = dp.astype(BF16)

        @pl.when(i == 0)
        def _():
            dw_ref[...] = jnp.zeros_like(dw_ref)

        dw_ref[...] += jnp.sum(dy * xhat, axis=0, keepdims=True)

    row_spec = pl.BlockSpec((tm, width), lambda i: (i, 0))
    vec_spec = pl.BlockSpec((1, width), lambda i: (0, 0))
    return pl.pallas_call(
        body, name=name,
        out_shape=(jax.ShapeDtypeStruct((rows, width), BF16), jax.ShapeDtypeStruct((1, width), F32)),
        grid=(rows // tm,), in_specs=[row_spec, row_spec, vec_spec], out_specs=(row_spec, vec_spec),
        compiler_params=_cparams(("arbitrary",)),
    )(dh, pre, w)


def _norm_bwd_add(name, dh, dhn, h, w):
    rows, width = dh.shape
    tm = _row_tile(rows, width)

    def body(dh_ref, dhn_ref, h_ref, w_ref, o_ref, dw_ref):
        i = pl.program_id(0)
        x = h_ref[...]
        dy = dhn_ref[...]
        r = lax.rsqrt(jnp.mean(x * x, axis=-1, keepdims=True) + RMS_EPS)
        xhat = x * r
        dxhat = dy * w_ref[...]
        dx = r * (dxhat - xhat * jnp.mean(dxhat * xhat, axis=-1, keepdims=True))
        o_ref[...] = dh_ref[...] + jnp.where(_rows_mask(i, tm), dx, 0.0)

        @pl.when(i == 0)
        def _():
            dw_ref[...] = jnp.zeros_like(dw_ref)

        dw_ref[...] += jnp.sum(dy * xhat, axis=0, keepdims=True)

    row_spec = pl.BlockSpec((tm, width), lambda i: (i, 0))
    vec_spec = pl.BlockSpec((1, width), lambda i: (0, 0))
    return pl.pallas_call(
        body, name=name,
        out_shape=(jax.ShapeDtypeStruct((rows, width), F32), jax.ShapeDtypeStruct((1, width), F32)),
        grid=(rows // tm,), in_specs=[row_spec, row_spec, row_spec, vec_spec], out_specs=(row_spec, vec_spec),
        compiler_params=_cparams(("arbitrary",)),
    )(dh, dhn, h, w)


def _shift_down(x, s, rows):
    return pltpu.roll(x, s, 0) if s else x


def _shift_up(x, s, rows):
    return pltpu.roll(x, rows - s, 0) if s else x


def _conv4_fwd(name, zx, cw, cb, steps=()):
    rows = zx.shape[0]
    off = D_INNER // LANES

    def body(x_ref, w_ref, b_ref, o_ref):
        x = x_ref[...]
        acc = b_ref[...] + w_ref[pl.ds(SSM_CONV - 1, 1), :] * x
        for s in range(1, SSM_CONV):
            acc = acc + w_ref[pl.ds(SSM_CONV - 1 - s, 1), :] * _shift_down(x, s, rows)
        valid = lax.broadcasted_iota(jnp.int32, (rows, 1), 0) >= PAD_ROWS
        o_ref[...] = jnp.where(valid, acc * _sigmoid(acc), 0.0)

    return _call(
        body, name=name, out_shape=jax.ShapeDtypeStruct((rows, D_XBC), F32), grid=(D_XBC // LANES,),
        in_specs=[pl.BlockSpec((rows, LANES), lambda j: (0, j + off)),
                  pl.BlockSpec((SSM_CONV, LANES), lambda j: (0, j)),
                  pl.BlockSpec((1, LANES), lambda j: (0, j))],
        out_specs=pl.BlockSpec((rows, LANES), lambda j: (0, j)), operands=[zx, cw, cb],
        semantics=("parallel",), steps=steps)


def _conv4_bwd(name, zx, dout, cw, cb, col0):
    rows, width = dout.shape
    zoff = (D_INNER + col0) // LANES
    woff = col0 // LANES

    def body(x_ref, d_ref, w_ref, b_ref, dx_ref, dw_ref, db_ref):
        x = x_ref[...]
        shifted = [_shift_down(x, s, rows) for s in range(SSM_CONV)]
        acc = b_ref[...]
        for s in range(SSM_CONV):
            acc = acc + w_ref[pl.ds(SSM_CONV - 1 - s, 1), :] * shifted[s]
        sig = _sigmoid(acc)
        valid = lax.broadcasted_iota(jnp.int32, (rows, 1), 0) >= PAD_ROWS
        dpre = jnp.where(valid, d_ref[...] * sig * (1.0 + acc * (1.0 - sig)), 0.0)
        dx = w_ref[pl.ds(SSM_CONV - 1, 1), :] * dpre
        for s in range(1, SSM_CONV):
            dx = dx + w_ref[pl.ds(SSM_CONV - 1 - s, 1), :] * _shift_up(dpre, s, rows)
        dx_ref[...] = dx.astype(BF16)
        for s in range(SSM_CONV):
            dw_ref[pl.ds(SSM_CONV - 1 - s, 1), :] = jnp.sum(dpre * shifted[s], axis=0, keepdims=True)
        db_ref[...] = jnp.sum(dpre, axis=0, keepdims=True)

    return pl.pallas_call(
        body, name=name,
        out_shape=(jax.ShapeDtypeStruct((rows, width), BF16), jax.ShapeDtypeStruct((SSM_CONV, width), F32),
                   jax.ShapeDtypeStruct((1, width), F32)),
        grid=(width // LANES,),
        in_specs=[pl.BlockSpec((rows, LANES), lambda j: (0, j + zoff)),
                  pl.BlockSpec((rows, LANES), lambda j: (0, j)),
                  pl.BlockSpec((SSM_CONV, LANES), lambda j: (0, j + woff)),
                  pl.BlockSpec((1, LANES), lambda j: (0, j + woff))],
        out_specs=(pl.BlockSpec((rows, LANES), lambda j: (0, j)),
                   pl.BlockSpec((SSM_CONV, LANES), lambda j: (0, j)),
                   pl.BlockSpec((1, LANES), lambda j: (0, j))),
        compiler_params=_cparams(("parallel",)),
    )(zx, dout, cw, cb)


def _ffn_conv_fwd(name, up, cw, cb, steps=()):
    rows = up.shape[0]
    nt = D_FF // LANES

    def body(g_ref, v_ref, wg_ref, wv_ref, bg_ref, bv_ref, o_ref):
        g, v = g_ref[...], v_ref[...]
        ug, uv = bg_ref[...], bv_ref[...]
        for s in range(FFN_CONV):
            ug = ug + wg_ref[pl.ds(FFN_CONV - 1 - s, 1), :] * _shift_down(g, s, rows)
            uv = uv + wv_ref[pl.ds(FFN_CONV - 1 - s, 1), :] * _shift_down(v, s, rows)
        valid = lax.broadcasted_iota(jnp.int32, (rows, 1), 0) >= PAD_ROWS
        o_ref[...] = jnp.where(valid, ug * _sigmoid(ug) * uv, 0.0).astype(BF16)

    col = lambda shift: pl.BlockSpec((rows, LANES), lambda j: (0, j + shift))
    wsp = lambda shift: pl.BlockSpec((FFN_CONV, LANES), lambda j: (0, j + shift))
    bsp = lambda shift: pl.BlockSpec((1, LANES), lambda j: (0, j + shift))
    return _call(
        body, name=name, out_shape=jax.ShapeDtypeStruct((rows, D_FF), BF16), grid=(nt,),
        in_specs=[col(0), col(nt), wsp(0), wsp(nt), bsp(0), bsp(nt)],
        out_specs=pl.BlockSpec((rows, LANES), lambda j: (0, j)), operands=[up, up, cw, cw, cb, cb],
        semantics=("parallel",), steps=steps)


def _ffn_conv_bwd(name, up, dact, cw, cb, steps=()):
    rows = up.shape[0]
    nt = D_FF // LANES

    def body(g_ref, v_ref, d_ref, wg_ref, wv_ref, bg_ref, bv_ref, dxg_ref, dxv_ref, dwg_ref, dwv_ref, dbg_ref, dbv_ref):
        g, v = g_ref[...], v_ref[...]
        gs = [_shift_down(g, s, rows) for s in range(FFN_CONV)]
        vs = [_shift_down(v, s, rows) for s in range(FFN_CONV)]
        ug, uv = bg_ref[...], bv_ref[...]
        for s in range(FFN_CONV):
            ug = ug + wg_ref[pl.ds(FFN_CONV - 1 - s, 1), :] * gs[s]
            uv = uv + wv_ref[pl.ds(FFN_CONV - 1 - s, 1), :] * vs[s]
        sig = _sigmoid(ug)
        valid = lax.broadcasted_iota(jnp.int32, (rows, 1), 0) >= PAD_ROWS
        d = jnp.where(valid, d_ref[...], 0.0)
        dsig = d * sig
        for dpre, src, w_ref, dx_ref, dw_ref, db_ref in (
                (dsig * uv * (1.0 + ug * (1.0 - sig)), gs, wg_ref, dxg_ref, dwg_ref, dbg_ref),
                (dsig * ug, vs, wv_ref, dxv_ref, dwv_ref, dbv_ref)):
            dx = w_ref[pl.ds(FFN_CONV - 1, 1), :] * dpre
            for s in range(1, FFN_CONV):
                dx = dx + w_ref[pl.ds(FFN_CONV - 1 - s, 1), :] * _shift_up(dpre, s, rows)
            dx_ref[...] = dx.astype(BF16)
            for s in range(FFN_CONV):
                dw_ref[pl.ds(FFN_CONV - 1 - s, 1), :] = jnp.sum(dpre * src[s], axis=0, keepdims=True)
            db_ref[...] = jnp.sum(dpre, axis=0, keepdims=True)

    col = lambda shift: pl.BlockSpec((rows, LANES), lambda j: (0, j + shift))
    wsp = lambda shift: pl.BlockSpec((FFN_CONV, LANES), lambda j: (0, j + shift))
    bsp = lambda shift: pl.BlockSpec((1, LANES), lambda j: (0, j + shift))
    dx_shape = jax.ShapeDtypeStruct((rows, D_FF), BF16)
    dw_shape = jax.ShapeDtypeStruct((FFN_CONV, D_FF), F32)
    db_shape = jax.ShapeDtypeStruct((1, D_FF), F32)
    return _call(
        body, name=name, out_shape=(dx_shape, dx_shape, dw_shape, dw_shape, db_shape, db_shape), grid=(nt,),
        in_specs=[col(0), col(nt), col(0), wsp(0), wsp(nt), bsp(0), bsp(nt)],
        out_specs=(col(0), col(0), wsp(0), wsp(0), bsp(0), bsp(0)),
        operands=[up, up, dact, cw, cw, cb, cb], semantics=("parallel",), steps=steps)


def _dt_fwd(name, dtr, bias):
    rows = dtr.shape[0]
    tm = _row_tile(rows, LANES)

    def body(d_ref, b_ref, o_ref):
        v = d_ref[...] + b_ref[...]
        sp = jnp.maximum(v, 0.0) + jnp.log1p(jnp.exp(-jnp.abs(v)))
        lane = lax.broadcasted_iota(jnp.int32, (tm, LANES), 1)
        ok = _rows_mask(pl.program_id(0), tm) & (lane < SSM_HEADS)
        o_ref[...] = jnp.where(ok, sp, 0.0)

    return pl.pallas_call(
        body, name=name, out_shape=jax.ShapeDtypeStruct((rows, LANES), F32), grid=(rows // tm,),
        in_specs=[pl.BlockSpec((tm, LANES), lambda i: (i, 0)), pl.BlockSpec((1, LANES), lambda i: (0, 0))],
        out_specs=pl.BlockSpec((tm, LANES), lambda i: (i, 0)), compiler_params=_cparams(("parallel",)),
    )(dtr, bias)


def _dt_bwd(name, ddt, dtr, bias):
    rows = dtr.shape[0]
    tm = _row_tile(rows, LANES)

    def body(g_ref, d_ref, b_ref, o_ref, db_ref):
        i = pl.program_id(0)
        lane = lax.broadcasted_iota(jnp.int32, (tm, LANES), 1)
        ok = _rows_mask(i, tm) & (lane < SSM_HEADS)
        dv = jnp.where(ok, g_ref[...] * _sigmoid(d_ref[...] + b_ref[...]), 0.0)
        o_ref[...] = dv.astype(BF16)

        @pl.when(i == 0)
        def _():
            db_ref[...] = jnp.zeros_like(db_ref)

        db_ref[...] += jnp.sum(dv, axis=0, keepdims=True)

    row_spec = pl.BlockSpec((tm, LANES), lambda i: (i, 0))
    vec_spec = pl.BlockSpec((1, LANES), lambda i: (0, 0))
    return pl.pallas_call(
        body, name=name,
        out_shape=(jax.ShapeDtypeStruct((rows, LANES), BF16), jax.ShapeDtypeStruct((1, LANES), F32)),
        grid=(rows // tm,), in_specs=[row_spec, row_spec, vec_spec], out_specs=(row_spec, vec_spec),
        compiler_params=_cparams(("arbitrary",)),
    )(ddt, dtr, bias)


def _gate_fwd(name, y, zx, w, steps=()):
    rows = y.shape[0]
    tm = _row_tile(rows, D_INNER)

    def body(y_ref, z_ref, w_ref, o_ref):
        z = z_ref[...]
        g = y_ref[...] * (z * _sigmoid(z))
        r = lax.rsqrt(jnp.mean(g * g, axis=-1, keepdims=True) + RMS_EPS)
        o_ref[...] = (g * r * w_ref[...]).astype(BF16)

    row_spec = pl.BlockSpec((tm, D_INNER), lambda i: (i, 0))
    return _call(
        body, name=name, out_shape=jax.ShapeDtypeStruct((rows, D_INNER), BF16), grid=(rows // tm,),
        in_specs=[row_spec, row_spec, pl.BlockSpec((1, D_INNER), lambda i: (0, 0))],
        out_specs=row_spec, operands=[y, zx, w], semantics=("parallel",), steps=steps)


def _gate_bwd(name, dyn, y, zx, w):
    rows = y.shape[0]
    tm = _row_tile(rows, D_INNER)

    def body(d_ref, y_ref, z_ref, w_ref, dy_ref, dz_ref, dw_ref):
        i = pl.program_id(0)
        z, yv = z_ref[...], y_ref[...]
        sig = _sigmoid(z)
        sz = z * sig
        g = yv * sz
        r = lax.rsqrt(jnp.mean(g * g, axis=-1, keepdims=True) + RMS_EPS)
        ghat = g * r
        dn = d_ref[...]
        dghat = dn * w_ref[...]
        dg = r * (dghat - ghat * jnp.mean(dghat * ghat, axis=-1, keepdims=True))
        dy_ref[...] = dg * sz
        dz_ref[...] = (dg * yv * sig * (1.0 + z * (1.0 - sig))).astype(BF16)

        @pl.when(i == 0)
        def _():
            dw_ref[...] = jnp.zeros_like(dw_ref)

        dw_ref[...] += jnp.sum(dn * ghat, axis=0, keepdims=True)

    row_spec = pl.BlockSpec((tm, D_INNER), lambda i: (i, 0))
    vec_spec = pl.BlockSpec((1, D_INNER), lambda i: (0, 0))
    return pl.pallas_call(
        body, name=name,
        out_shape=(jax.ShapeDtypeStruct((rows, D_INNER), F32), jax.ShapeDtypeStruct((rows, D_INNER), BF16),
                   jax.ShapeDtypeStruct((1, D_INNER), F32)),
        grid=(rows // tm,), in_specs=[row_spec, row_spec, row_spec, vec_spec],
        out_specs=(row_spec, row_spec, vec_spec), compiler_params=_cparams(("arbitrary",)),
    )(dyn, y, zx, w)


def _split3(x):
    hi = x.astype(BF16)
    r1 = x - hi.astype(F32)
    mid = r1.astype(BF16)
    lo = (r1 - mid.astype(F32)).astype(BF16)
    return hi, mid, lo


def _dot3_data_lhs(x, sel):
    sel16 = sel.astype(F32).astype(BF16)
    hi, mid, lo = _split3(x)
    return _dot(hi, sel16) + _dot(mid, sel16) + _dot(lo, sel16)


def _dot3_data_rhs(sel, x):
    sel16 = sel.astype(F32).astype(BF16)
    hi, mid, lo = _split3(x)
    return _dot(sel16, hi) + _dot(sel16, mid) + _dot(sel16, lo)


def _causal_masks():
    r = lax.broadcasted_iota(jnp.int32, (CHUNK, CHUNK), 0)
    c = lax.broadcasted_iota(jnp.int32, (CHUNK, CHUNK), 1)
    return r >= c, r <= c


def _expand_heads_matrix():
    k = lax.broadcasted_iota(jnp.int32, (LANES, GROUP_W), 0)
    j = lax.broadcasted_iota(jnp.int32, (LANES, GROUP_W), 1)
    return jnp.right_shift(j, 6) == k


def _reduce_heads_matrix():
    j = lax.broadcasted_iota(jnp.int32, (GROUP_W, LANES), 0)
    k = lax.broadcasted_iota(jnp.int32, (GROUP_W, LANES), 1)
    return jnp.right_shift(j, 6) == k


def _reduce_pair_matrix(p):
    j = lax.broadcasted_iota(jnp.int32, (LANES, LANES), 0)
    k = lax.broadcasted_iota(jnp.int32, (LANES, LANES), 1)
    return (2 * p + jnp.right_shift(j, 6)) == k


def _ssd_prep(name, dt4, a128, steps=()):
    rows = dt4.shape[1]
    nc = rows // CHUNK

    def body(dt_ref, a_ref, dte_ref, acs_ref):
        causal, _ = _causal_masks()
        expand = _expand_heads_matrix()
        dt = dt_ref[...]
        acs = _dot3_data_rhs(causal, dt) * a_ref[...]
        dte_ref[...] = _dot3_data_lhs(dt, expand)
        acs_ref[...] = _dot3_data_lhs(acs, expand)

    blk = pl.BlockSpec((CHUNK, GROUP_W), lambda g, c: (c, g))
    shp = jax.ShapeDtypeStruct((rows, D_INNER), F32)
    return _call(
        body, name=name, out_shape=(shp, shp), grid=(N_GROUPS, nc),
        in_specs=[pl.BlockSpec((None, CHUNK, LANES), lambda g, c: (g, c, 0)),
                  pl.BlockSpec((None, 1, LANES), lambda g, c: (g, 0, 0))],
        out_specs=(blk, blk), operands=[dt4, a128], semantics=("parallel", "parallel"), steps=steps)


def _ssd_common(x_ref, b_ref, c_ref, dte_ref, acs_ref):
    x = x_ref[...]
    dt_exp = dte_ref[...]
    acs_exp = acs_ref[...]
    tot_exp = acs_ref[pl.ds(CHUNK - 1, 1), :]
    xdt = x * dt_exp
    e_exp = jnp.exp(acs_exp)
    f_exp = jnp.exp(tot_exp - acs_exp)
    return _causal_masks(), x, dt_exp, acs_exp, tot_exp, xdt, e_exp, f_exp, b_ref[...], c_ref[...]


def _pair_decay(acs_pair, e, causal):
    lane = lax.broadcasted_iota(jnp.int32, (CHUNK, LANES), 1)
    mine = (lane < HEAD_DIM) if e == 0 else (lane >= HEAD_DIM)
    a_l = jnp.where(mine, acs_pair, pltpu.roll(acs_pair, HEAD_DIM, 1))
    seg = a_l - a_l.T
    dm = jnp.where(causal[0], jnp.exp(jnp.minimum(seg, 0.0)), 0.0)
    dmt = jnp.where(causal[1], jnp.exp(jnp.minimum(-seg, 0.0)), 0.0)
    return dm, dmt


def _ssd_fwd(name, xbc, dt_exp, acs_exp, dskexp, steps=()):
    rows = xbc.shape[0]
    nc = rows // CHUNK
    bcol = D_INNER // LANES

    def body(x_ref, b_ref, c_ref, dte_ref, acs_ref, dsk_ref, y_ref, st_ref, s_scr):
        @pl.when(pl.program_id(1) == 0)
        def _():
            s_scr[...] = jnp.zeros_like(s_scr)

        causal, x, _, acs_exp_v, tot_exp, xdt, e_exp, f_exp, bm, cm = _ssd_common(x_ref, b_ref, c_ref, dte_ref, acs_ref)
        state = s_scr[...]
        st_ref[...] = state
        cb16, bb16 = cm.astype(BF16), bm.astype(BF16)
        cb = _dot_nt(cb16, bb16)
        base = e_exp * _dot(cb16, state.astype(BF16)) + dsk_ref[...] * x
        lane = lax.broadcasted_iota(jnp.int32, (CHUNK, LANES), 1)
        for p in range(HEADS_PER_GROUP // 2):
            sl = slice(p * LANES, (p + 1) * LANES)
            xp = xdt[:, sl].astype(BF16)
            yd = []
            for e in range(2):
                dm, _ = _pair_decay(acs_exp_v[:, sl], e, causal)
                yd.append(_dot((cb * dm).astype(BF16), xp))
            y_ref[:, sl] = jnp.where(lane < HEAD_DIM, yd[0], yd[1]) + base[:, sl]
        s_scr[...] = jnp.exp(tot_exp) * state + _dot(bm.T.astype(BF16), (f_exp * xdt).astype(BF16))

    blk = pl.BlockSpec((CHUNK, GROUP_W), lambda g, c: (c, g))
    return _call(
        body, name=name,
        out_shape=(jax.ShapeDtypeStruct((rows, D_INNER), F32),
                   jax.ShapeDtypeStruct((N_GROUPS, nc, D_STATE, GROUP_W), F32)),
        grid=(N_GROUPS, nc),
        in_specs=[blk,
                  pl.BlockSpec((CHUNK, LANES), lambda g, c: (c, bcol + g)),
                  pl.BlockSpec((CHUNK, LANES), lambda g, c: (c, bcol + N_GROUPS + g)),
                  blk, blk, pl.BlockSpec((None, 1, GROUP_W), lambda g, c: (g, 0, 0))],
        out_specs=(blk, pl.BlockSpec((None, None, D_STATE, GROUP_W), lambda g, c: (g, c, 0, 0))),
        scratch_shapes=[pltpu.VMEM((D_STATE, GROUP_W), F32)],
        operands=[xbc, xbc, xbc, dt_exp, acs_exp, dskexp], semantics=("parallel", "arbitrary"), steps=steps)


def _ssd_bwd(name, xbc, dt_exp, acs_exp, dt4, a128, dskexp, dy, states, steps=()):
    rows = xbc.shape[0]
    nc = rows // CHUNK
    bcol = D_INNER // LANES
    last = nc - 1

    def body(x_ref, b_ref, c_ref, dte_ref, acs_ref, dt_ref, a128_ref, dsk_ref, dy_ref, st_ref,
             dx_ref, db_ref, dc_ref, ddt_ref, dalog_ref, ddsk_ref, ds_scr):
        first = pl.program_id(1) == 0

        @pl.when(first)
        def _():
            ds_scr[...] = jnp.zeros_like(ds_scr)
            dalog_ref[...] = jnp.zeros_like(dalog_ref)
            ddsk_ref[...] = jnp.zeros_like(ddsk_ref)

        causal, x, dt_exp, acs_exp_v, tot_exp, xdt, e_exp, f_exp, bm, cm = _ssd_common(
            x_ref, b_ref, c_ref, dte_ref, acs_ref)
        dt = dt_ref[...]
        reduce_heads = _reduce_heads_matrix()
        state, dstate = st_ref[...], ds_scr[...]
        dyv = dy_ref[...]
        cb16, bb16 = cm.astype(BF16), bm.astype(BF16)
        s16, ds16 = state.astype(BF16), dstate.astype(BF16)
        cb = _dot_nt(cb16, bb16)
        cbt = _dot_nt(bb16, cb16)
        cs = _dot(cb16, s16)
        bds = _dot(bb16, ds16)
        edy = e_exp * dyv
        fx = f_exp * xdt
        dxdt_base = f_exp * bds
        dc_acc = _dot_nt(edy.astype(BF16), s16)
        db_acc = _dot_nt(fx.astype(BF16), ds16)
        ds_scr[...] = jnp.exp(tot_exp) * dstate + _dot(cm.T.astype(BF16), edy.astype(BF16))
        q = fx * bds
        dacs = _dot3_data_lhs(edy * cs - q, reduce_heads)
        dtot = jnp.sum(_dot3_data_lhs(q + jnp.exp(tot_exp) * dstate * state, reduce_heads), axis=0, keepdims=True)
        ddsk_ref[...] += jnp.sum(_dot3_data_lhs(dyv * x, reduce_heads), axis=0, keepdims=True)
        lane = lax.broadcasted_iota(jnp.int32, (CHUNK, LANES), 1)
        dcb = jnp.zeros((CHUNK, CHUNK), F32)
        dcbt = jnp.zeros((CHUNK, CHUNK), F32)
        ddt_x = jnp.zeros((CHUNK, LANES), F32)
        for p in range(HEADS_PER_GROUP // 2):
            sl = slice(p * LANES, (p + 1) * LANES)
            xp, dyp = xdt[:, sl], dyv[:, sl]
            xp16, dyp16 = xp.astype(BF16), dyp.astype(BF16)
            dxh = []
            for e in range(2):
                h = 2 * p + e
                mine = (lane < HEAD_DIM) if e == 0 else (lane >= HEAD_DIM)
                dm, dmt = _pair_decay(acs_exp_v[:, sl], e, causal)
                m, mt = cb * dm, cbt * dmt
                xh16 = jnp.where(mine, xp, 0.0).astype(BF16)
                dyh16 = jnp.where(mine, dyp, 0.0).astype(BF16)
                d_m = _dot_nt(dyh16, xp16)
                d_mt = _dot_nt(xh16, dyp16)
                dacs_h = (jnp.sum(d_m * m, axis=-1, keepdims=True)
                          - jnp.sum(d_mt * mt, axis=-1, keepdims=True))
                dacs = dacs + jnp.where(lane == h, dacs_h, 0.0)
                dcb = dcb + d_m * dm
                dcbt = dcbt + d_mt * dmt
                dxh.append(_dot(mt.astype(BF16), dyp16))
            dxdt = jnp.where(lane < HEAD_DIM, dxh[0], dxh[1]) + dxdt_base[:, sl]
            dx_ref[:, sl] = dxdt * dt_exp[:, sl] + dsk_ref[:, sl] * dyp
            ddt_x = ddt_x + _dot3_data_lhs(dxdt * x[:, sl], _reduce_pair_matrix(p))
        dc_ref[...] = dc_acc + _dot(dcb.astype(BF16), bb16)
        db_ref[...] = db_acc + _dot(dcbt.astype(BF16), cb16)
        row = lax.broadcasted_iota(jnp.int32, (CHUNK, LANES), 0)
        dacs = dacs + jnp.where(row == CHUNK - 1, dtot, 0.0)
        da = _dot3_data_rhs(causal[1], dacs)
        ddt_ref[...] = da * a128_ref[...] + ddt_x
        dalog_ref[...] += jnp.sum(da * dt, axis=0, keepdims=True) * a128_ref[...]

    vec = lambda w: pl.BlockSpec((None, 1, w), lambda g, c: (g, 0, 0))
    blk = pl.BlockSpec((CHUNK, GROUP_W), lambda g, c: (last - c, g))
    return _call(
        body, name=name,
        out_shape=(jax.ShapeDtypeStruct((rows, D_INNER), F32), jax.ShapeDtypeStruct((rows, D_BC), F32),
                   jax.ShapeDtypeStruct((rows, D_BC), F32), jax.ShapeDtypeStruct((N_GROUPS, rows, LANES), F32),
                   jax.ShapeDtypeStruct((N_GROUPS, 1, LANES), F32), jax.ShapeDtypeStruct((N_GROUPS, 1, LANES), F32)),
        grid=(N_GROUPS, nc),
        in_specs=[blk,
                  pl.BlockSpec((CHUNK, LANES), lambda g, c: (last - c, bcol + g)),
                  pl.BlockSpec((CHUNK, LANES), lambda g, c: (last - c, bcol + N_GROUPS + g)),
                  blk, blk,
                  pl.BlockSpec((None, CHUNK, LANES), lambda g, c: (g, last - c, 0)),
                  vec(LANES), vec(GROUP_W), blk,
                  pl.BlockSpec((None, None, D_STATE, GROUP_W), lambda g, c: (g, last - c, 0, 0))],
        out_specs=(blk,
                   pl.BlockSpec((CHUNK, LANES), lambda g, c: (last - c, g)),
                   pl.BlockSpec((CHUNK, LANES), lambda g, c: (last - c, g)),
                   pl.BlockSpec((None, CHUNK, LANES), lambda g, c: (g, last - c, 0)),
                   vec(LANES), vec(LANES)),
        scratch_shapes=[pltpu.VMEM((D_STATE, GROUP_W), F32)],
        operands=[xbc, xbc, xbc, dt_exp, acs_exp, dt4, a128, dskexp, dy, states],
        semantics=("parallel", "arbitrary"), steps=steps)


def _attn_visible(b, heads=1):
    row = jnp.bitwise_and(lax.broadcasted_iota(jnp.int32, (heads * CHUNK, 3 * CHUNK), 0), CHUNK - 1)
    col = lax.broadcasted_iota(jnp.int32, (heads * CHUNK, 3 * CHUNK), 1)
    bb = b + jnp.zeros_like(col)
    meta = (col < CHUNK) & (bb >= 1) & (col >= PAD_ROWS)
    prev = (col >= CHUNK) & (col < 2 * CHUNK) & (bb >= 2) & ((col - CHUNK) > row)
    cur = (col >= 2 * CHUNK) & ((col - 2 * CHUNK) <= row) & ((bb >= 1) | ((col - 2 * CHUNK) >= PAD_ROWS))
    return meta | prev | cur


def _attn_visible4(b):
    return _attn_visible(b, 4)


def _stack_heads(q_ref, sink_ref, kvh, scale):
    lane = lax.broadcasted_iota(jnp.int32, (CHUNK, LANES), 1)
    parts, sinks = [], []
    for pp in range(2):
        pair = kvh * 2 + pp
        qp = q_ref[:, pair * LANES:(pair + 1) * LANES] * scale
        for e in range(2):
            mine = (lane < HEAD_DIM) if e == 0 else (lane >= HEAD_DIM)
            parts.append(jnp.where(mine, qp, 0.0).astype(BF16))
            sinks.append(jnp.full((CHUNK, 1), sink_ref[2 * pair + e], F32))
    return jnp.concatenate(parts, axis=0), jnp.concatenate(sinks, axis=0)


def _attn_probs(qm16, kc16, visible, sink):
    s = jnp.where(visible, _dot_nt(qm16, kc16), NEG_INF)
    m = jnp.maximum(jnp.max(s, axis=-1, keepdims=True), sink)
    pe = jnp.exp(s - m)
    pe_sink = jnp.exp(sink - m)
    inv = 1.0 / (jnp.sum(pe, axis=-1, keepdims=True) + pe_sink)
    return pe * inv, pe_sink * inv


def _attn_specs(colblock):
    blk = lambda f: pl.BlockSpec((CHUNK, 2 * D_KV), f)
    return [blk(lambda b: (0, colblock)), blk(lambda b: (jnp.maximum(b - 1, 0), colblock)), blk(lambda b: (b, colblock))]


def _attn_fwd(name, q, kv2, sinks, steps=()):
    rows = q.shape[0]

    def body(q_ref, k0, kp, kc, v0, vp, vc, sink_ref, o_ref):
        visible = _attn_visible4(pl.program_id(0))
        lane = lax.broadcasted_iota(jnp.int32, (CHUNK, LANES), 1)
        for kvh in range(N_KV_HEADS):
            ksl = slice(kvh * LANES, (kvh + 1) * LANES)
            kcat = jnp.concatenate([k0[:, ksl], kp[:, ksl], kc[:, ksl]], axis=0).astype(BF16)
            vcat = jnp.concatenate([v0[:, ksl], vp[:, ksl], vc[:, ksl]], axis=0).astype(BF16)
            q4, sink4 = _stack_heads(q_ref, sink_ref, kvh, ATTN_SCALE)
            pn, _ = _attn_probs(q4, kcat, visible, sink4)
            o4 = _dot(pn.astype(BF16), vcat)
            for pp in range(2):
                qsl = slice((kvh * 2 + pp) * LANES, (kvh * 2 + pp + 1) * LANES)
                o_ref[:, qsl] = jnp.where(lane < HEAD_DIM, o4[(2 * pp) * CHUNK:(2 * pp + 1) * CHUNK],
                                          o4[(2 * pp + 1) * CHUNK:(2 * pp + 2) * CHUNK]).astype(BF16)

    return _call(
        body, name=name, out_shape=jax.ShapeDtypeStruct((rows, D_MODEL), BF16), grid=(rows // CHUNK,),
        in_specs=[pl.BlockSpec((CHUNK, D_MODEL), lambda b: (b, 0))] + _attn_specs(0) + _attn_specs(1)
        + [pl.BlockSpec(memory_space=pltpu.SMEM)],
        out_specs=pl.BlockSpec((CHUNK, D_MODEL), lambda b: (b, 0)),
        operands=[q, kv2, kv2, kv2, kv2, kv2, kv2, sinks], semantics=("parallel",), steps=steps)


def _attn_bwd(name, q, kv2, sinks, do, steps=()):
    rows = q.shape[0]

    def body(q_ref, k0, kp, kc, v0, vp, vc, sink_ref, do_ref,
             dq_ref, dkc_ref, dkp_ref, dvc_ref, dvp_ref, dkm_ref, dvm_ref, dsink_ref):
        @pl.when(pl.program_id(0) == 0)
        def _():
            dkm_ref[...] = jnp.zeros_like(dkm_ref)
            dvm_ref[...] = jnp.zeros_like(dvm_ref)
            dsink_ref[...] = jnp.zeros_like(dsink_ref)

        visible = _attn_visible4(pl.program_id(0))
        lane = lax.broadcasted_iota(jnp.int32, (CHUNK, LANES), 1)
        lane1 = lax.broadcasted_iota(jnp.int32, (1, LANES), 1)
        dsink = jnp.zeros((1, LANES), F32)
        for kvh in range(N_KV_HEADS):
            ksl = slice(kvh * LANES, (kvh + 1) * LANES)
            kcat = jnp.concatenate([k0[:, ksl], kp[:, ksl], kc[:, ksl]], axis=0).astype(BF16)
            vcat = jnp.concatenate([v0[:, ksl], vp[:, ksl], vc[:, ksl]], axis=0).astype(BF16)
            q4, sink4 = _stack_heads(q_ref, sink_ref, kvh, ATTN_SCALE)
            do4, _ = _stack_heads(do_ref, sink_ref, kvh, 1.0)
            pn, psink = _attn_probs(q4, kcat, visible, sink4)
            dp = _dot_nt(do4, vcat)
            delta = jnp.sum(pn * dp, axis=-1, keepdims=True)
            ds = pn * (dp - delta)
            sink_terms = psink * delta
            dq4 = _dot(ds.astype(BF16), kcat)
            dk_acc = _dot(ds.T.astype(BF16), q4)
            dv_acc = _dot(pn.T.astype(BF16), do4)
            for j in range(4):
                part = jnp.sum(sink_terms[j * CHUNK:(j + 1) * CHUNK], axis=0, keepdims=True)
                dsink = dsink - jnp.where(lane1 == kvh * 4 + j, part, 0.0)
            for pp in range(2):
                qsl = slice((kvh * 2 + pp) * LANES, (kvh * 2 + pp + 1) * LANES)
                dq_pair = jnp.where(lane < HEAD_DIM, dq4[(2 * pp) * CHUNK:(2 * pp + 1) * CHUNK],
                                    dq4[(2 * pp + 1) * CHUNK:(2 * pp + 2) * CHUNK])
                dq_ref[:, qsl] = (dq_pair * ATTN_SCALE).astype(BF16)
            dkm_ref[:, ksl] += dk_acc[0:CHUNK]
            dvm_ref[:, ksl] += dv_acc[0:CHUNK]
            dkp_ref[:, ksl] = dk_acc[CHUNK:2 * CHUNK]
            dvp_ref[:, ksl] = dv_acc[CHUNK:2 * CHUNK]
            dkc_ref[:, ksl] = dk_acc[2 * CHUNK:3 * CHUNK]
            dvc_ref[:, ksl] = dv_acc[2 * CHUNK:3 * CHUNK]
        dsink_ref[...] += dsink

    qspec = pl.BlockSpec((CHUNK, D_MODEL), lambda b: (b, 0))
    kvspec = pl.BlockSpec((CHUNK, 2 * D_KV), lambda b: (b, 0))
    fixed = pl.BlockSpec((CHUNK, 2 * D_KV), lambda b: (0, 0))
    kv_shape = jax.ShapeDtypeStruct((rows, 2 * D_KV), F32)
    meta_shape = jax.ShapeDtypeStruct((CHUNK, 2 * D_KV), F32)
    return _call(
        body, name=name,
        out_shape=(jax.ShapeDtypeStruct((rows, D_MODEL), BF16), kv_shape, kv_shape, kv_shape, kv_shape,
                   meta_shape, meta_shape, jax.ShapeDtypeStruct((1, LANES), F32)),
        grid=(rows // CHUNK,),
        in_specs=[qspec] + _attn_specs(0) + _attn_specs(1) + [pl.BlockSpec(memory_space=pltpu.SMEM), qspec],
        out_specs=(qspec, kvspec, kvspec, kvspec, kvspec, fixed, fixed, pl.BlockSpec((1, LANES), lambda b: (0, 0))),
        operands=[q, kv2, kv2, kv2, kv2, kv2, kv2, sinks, do], semantics=("arbitrary",), steps=steps)


def _kv_grad_combine(name, dk_cur, dk_prev, dk_meta, dv_cur, dv_prev, dv_meta):
    rows = dk_cur.shape[0]
    nb = rows // CHUNK
    width = 2 * D_KV

    def body(kc_ref, kp_ref, km_ref, vc_ref, vp_ref, vm_ref, o_ref):
        jj = pl.program_id(0) + jnp.zeros((CHUNK, 1), jnp.int32)
        for half, (c_ref, p_ref, m_ref) in enumerate(((kc_ref, kp_ref, km_ref), (vc_ref, vp_ref, vm_ref))):
            total = c_ref[...] + jnp.where(jj < nb - 1, p_ref[...], 0.0) + jnp.where(jj == 0, m_ref[...], 0.0)
            o_ref[:, half * width:(half + 1) * width] = total.astype(BF16)

    blk = lambda f: pl.BlockSpec((CHUNK, width), f)
    three = lambda: [blk(lambda j: (j, 0)), blk(lambda j: (jnp.minimum(j + 1, nb - 1), 0)), blk(lambda j: (0, 0))]
    return pl.pallas_call(
        body, name=name, out_shape=jax.ShapeDtypeStruct((rows, 2 * width), BF16), grid=(nb,),
        in_specs=three() + three(), out_specs=pl.BlockSpec((CHUNK, 2 * width), lambda j: (j, 0)),
        compiler_params=_cparams(("parallel",)),
    )(dk_cur, dk_prev, dk_meta, dv_cur, dv_prev, dv_meta)


def _loss_head(name, h, target):
    rows = h.shape[0]

    def body(h_ref, t_ref, dh_ref, loss_ref):
        i = pl.program_id(0)
        real = (i + jnp.zeros((CHUNK, 1), jnp.int32)) >= 1
        diff = jnp.where(real, h_ref[...] - t_ref[...], 0.0)
        dh_ref[...] = diff * (1.0 / D_MODEL)

        @pl.when(i == 0)
        def _():
            loss_ref[...] = jnp.zeros_like(loss_ref)

        loss_ref[...] += jnp.sum(diff * diff) * (0.5 / D_MODEL)

    blk = pl.BlockSpec((CHUNK, D_MODEL), lambda i: (i, 0))
    return pl.pallas_call(
        body, name=name,
        out_shape=(jax.ShapeDtypeStruct((rows, D_MODEL), F32), jax.ShapeDtypeStruct((1, LANES), F32)),
        grid=(rows // CHUNK,),
        in_specs=[blk, pl.BlockSpec((CHUNK, D_MODEL), lambda i: (jnp.maximum(i - 1, 0), 0))],
        out_specs=(blk, pl.BlockSpec((1, LANES), lambda i: (0, 0))), compiler_params=_cparams(("arbitrary",)),
    )(h, target)


def _adamw(name, w, g, m, v):
    rows, width = w.shape
    tr = rows
    for cand in range(8, rows + 1, 8):
        if rows % cand == 0 and cand * width * 4 <= (1 << 20):
            tr = cand

    def body(w_ref, g_ref, m_ref, v_ref, d_ref, mo_ref, vo_ref):
        gv = g_ref[...]
        mn = ADAM_B1 * m_ref[...] + (1.0 - ADAM_B1) * gv
        vn = ADAM_B2 * v_ref[...] + (1.0 - ADAM_B2) * (gv * gv)
        m_hat = mn / (1.0 - ADAM_B1 ** ADAM_STEP)
        v_hat = vn / (1.0 - ADAM_B2 ** ADAM_STEP)
        d_ref[...] = -ADAM_LR * (m_hat / (jnp.sqrt(v_hat) + ADAM_EPS) + ADAM_WD * w_ref[...])
        mo_ref[...] = mn
        vo_ref[...] = vn

    blk = pl.BlockSpec((tr, width), lambda i: (i, 0))
    shp = jax.ShapeDtypeStruct((rows, width), F32)
    return pl.pallas_call(
        body, name=name, out_shape=(shp, shp, shp), grid=(rows // tr,), in_specs=[blk] * 4, out_specs=(blk,) * 3,
        compiler_params=_cparams(("parallel",)),
    )(w, g, m, v)


class _GivenWeights:
    def __init__(self, p):
        self.p = p
        self.grads = {}

    def weight(self, name, layer=None):
        return self.p[name] if layer is None else self.p[name][layer]

    def steps(self, kernel):
        return ()

    def grad(self, name, layer, g):
        self.grads[(name, layer)] = g


def _ffn_fwd(tag, h, p, i, plan):
    hn = _rms_fwd(f"ffn{tag}_norm", h, p["f_norm_pre"][i:i + 1])
    up = _mm_nn_bychip(f"ffn{tag}_up", hn, plan.weight("f_w_up", i))
    act = _ffn_conv_fwd(f"ffn{tag}_conv", up, p["f_conv_w"][i], p["f_conv_b"][i:i + 1], steps=plan.steps(f"ffn{tag}_conv"))
    pre = _mm(f"ffn{tag}_down", act, plan.weight("f_w_down", i), "nn")
    h_new = _resid_norm_fwd(f"ffn{tag}_resid", h, pre, p["f_norm_post"][i:i + 1])
    return h_new, (h, hn, up, act, pre)


def _ffn_bwd(tag, dh, saved, p, i, plan):
    h, hn, up, act, pre = saved
    dpre, g_post = _resid_norm_bwd(f"ffn{tag}_resid_bwd", dh, pre, p["f_norm_post"][i:i + 1])
    plan.grad("f_w_down", i, _mm(f"ffn{tag}_down_dw", act, dpre, "tn", out_dtype=BF16))
    dact = _mm(f"ffn{tag}_down_dx", dpre, plan.weight("f_w_down", i), "nt", steps=plan.steps(f"ffn{tag}_down_dx"))
    dug, duv, gwg, gwv, gbg, gbv = _ffn_conv_bwd(f"ffn{tag}_conv_bwd", up, dact, p["f_conv_w"][i], p["f_conv_b"][i:i + 1],
                                                 steps=plan.steps(f"ffn{tag}_conv_bwd"))
    g_cw, g_cb = jnp.concatenate([gwg, gwv], axis=1), jnp.concatenate([gbg, gbv], axis=1)
    w_up = plan.weight("f_w_up", i)
    n = w_up.shape[2]
    dhn = _mm_nt_bychip(f"ffn{tag}_up_dx_gate", dug, w_up, 0)
    dhn = _mm_nt_bychip(f"ffn{tag}_up_dx_val", duv, w_up, N_CHIPS // 2, acc=dhn)
    g_up = _mm_tn_bychip(f"ffn{tag}_up_dw_gate", hn, dug, n, 0)
    plan.grad("f_w_up", i, _mm_tn_bychip(f"ffn{tag}_up_dw_val", hn, duv, n, N_CHIPS // 2, into=g_up))
    dh_new, g_pre = _norm_bwd_add(f"ffn{tag}_norm_bwd", dh, dhn, h, p["f_norm_pre"][i:i + 1])
    return dh_new, dict(f_norm_post=g_post, f_conv_w=g_cw, f_conv_b=g_cb, f_norm_pre=g_pre)


def _lanes_pad(a, width=LANES):
    return jnp.pad(a, [(0, 0)] * (a.ndim - 1) + [(0, width - a.shape[-1])])


def _dup_heads(w):
    rows = w.shape[0]
    w = w.reshape(rows, 2 * N_KV_HEADS, 1, HEAD_DIM)
    return jnp.broadcast_to(w, (rows, 2 * N_KV_HEADS, 2, HEAD_DIM)).reshape(rows, 4 * D_KV)


def _undup_heads(g):
    rows = g.shape[0]
    return g.reshape(rows, 2 * N_KV_HEADS, 2, HEAD_DIM).sum(axis=2).reshape(rows, 2 * D_KV)


def _local_step(x2, target, p, plan):
    seq = x2.shape[0]
    rows = seq + CHUNK
    g = {}

    h0 = jnp.concatenate([jnp.zeros((PAD_ROWS, D_MODEL), F32), p["meta_tokens"], x2], axis=0)

    w_in = plan.weight("a_w_in")
    w_dt = _lanes_pad(w_in[:, D_MAIN:])
    dt_bias = _lanes_pad(p["a_dt_bias"])
    a_neg = -jnp.exp(p["a_a_log"].reshape(N_GROUPS, HEADS_PER_GROUP))
    a128 = _lanes_pad(a_neg.reshape(N_GROUPS, 1, HEADS_PER_GROUP))
    dskexp = jnp.repeat(p["a_d_skip"].reshape(N_GROUPS, HEADS_PER_GROUP), HEAD_DIM, axis=1).reshape(N_GROUPS, 1, GROUP_W)

    hn0 = _rms_fwd("a_norm", h0, p["a_norm_pre"])
    zx = _mm("a_in_main", hn0, w_in, "nn", n_cols=D_MAIN, steps=plan.steps("a_in_main"))
    dtr = _mm("a_in_dt", hn0, w_dt, "nn")
    xbc = _conv4_fwd("a_conv", zx, p["a_conv_w"], p["a_conv_b"], steps=plan.steps("a_conv"))
    dt = _dt_fwd("a_dt", dtr, dt_bias)
    dt4 = _lanes_pad(dt[:, :SSM_HEADS].reshape(rows, N_GROUPS, HEADS_PER_GROUP).transpose(1, 0, 2))
    dt_exp, acs_exp = _ssd_prep("a_ssd_prep", dt4, a128, steps=plan.steps("a_ssd_prep"))
    y, states = _ssd_fwd("a_ssd", xbc, dt_exp, acs_exp, dskexp, steps=plan.steps("a_ssd"))
    yn = _gate_fwd("a_gate", y, zx, p["a_gate_norm"], steps=plan.steps("a_gate"))
    mix = _mm("a_out", yn, plan.weight("a_w_out"), "nn")
    h1 = _resid_norm_fwd("a_resid", h0, mix, p["a_norm_post"])

    h2, ffn0 = _ffn_fwd("0", h1, p, 0, plan)

    hkv = _rms_fwd("kv_norm", h2, p["kv_norm"])
    w_kv2 = _dup_heads(plan.weight("w_kv"))
    kv2 = _mm("kv_proj", hkv, w_kv2, "nn")
    hn2 = _rms_fwd("b_norm", h2, p["b_norm_pre"])
    q = _mm("b_q", hn2, plan.weight("b_w_q"), "nn")
    sinks = p["b_sinks"].reshape(N_Q_HEADS)
    o = _attn_fwd("b_attn", q, kv2, sinks, steps=plan.steps("b_attn"))
    attn = _mm("b_o", o, plan.weight("b_w_o"), "nn", steps=plan.steps("b_o"))
    h3 = _resid_norm_fwd("b_resid", h2, attn, p["b_norm_post"])

    h4, ffn1 = _ffn_fwd("1", h3, p, 1, plan)

    dh, loss_vec = _loss_head("loss", h4, target)
    loss = loss_vec[0, 0]

    dh, g1 = _ffn_bwd("1", dh, ffn1, p, 1, plan)

    dpre, g["b_norm_post"] = _resid_norm_bwd("b_resid_bwd", dh, attn, p["b_norm_post"])
    plan.grad("b_w_o", None, _mm("b_o_dw", o, dpre, "tn", out_dtype=BF16))
    do = _mm("b_o_dx", dpre, plan.weight("b_w_o"), "nt", steps=plan.steps("b_o_dx"))
    dq, dkc, dkp, dvc, dvp, dkm, dvm, dsink = _attn_bwd("b_attn_bwd", q, kv2, sinks, do, steps=plan.steps("b_attn_bwd"))
    g["b_sinks"] = dsink[:, :N_Q_HEADS]
    dhn2 = _mm("b_q_dx", dq, plan.weight("b_w_q"), "nt")
    plan.grad("b_w_q", None, _mm("b_q_dw", hn2, dq, "tn", out_dtype=BF16))
    dh, g["b_norm_pre"] = _norm_bwd_add("b_norm_bwd", dh, dhn2, h2, p["b_norm_pre"])
    dkv2 = _kv_grad_combine("kv_grad", dkc, dkp, dkm, dvc, dvp, dvm)
    dhkv = _mm("kv_proj_dx", dkv2, w_kv2, "nt")
    plan.grad("w_kv", None, _undup_heads(_mm("kv_proj_dw", hkv, dkv2, "tn")))
    dh, g["kv_norm"] = _norm_bwd_add("kv_norm_bwd", dh, dhkv, h2, p["kv_norm"])

    dh, g0 = _ffn_bwd("0", dh, ffn0, p, 0, plan)
    for name in g0:
        if g0[name].shape[0] == 1:
            g[name] = jnp.concatenate([g0[name], g1[name]], axis=0)
        else:
            g[name] = jnp.stack([g0[name], g1[name]])

    dpre, g["a_norm_post"] = _resid_norm_bwd("a_resid_bwd", dh, mix, p["a_norm_post"])
    plan.grad("a_w_out", None, _mm("a_out_dw", yn, dpre, "tn", out_dtype=BF16))
    dyn = _mm("a_out_dx", dpre, plan.weight("a_w_out"), "nt", steps=plan.steps("a_out_dx"))
    dy, dz, g["a_gate_norm"] = _gate_bwd("a_gate_bwd", dyn, y, zx, p["a_gate_norm"])
    dxs, dbm, dcm, ddt4, dalog, ddsk = _ssd_bwd("a_ssd_bwd", xbc, dt_exp, acs_exp, dt4, a128, dskexp, dy, states,
                                               steps=plan.steps("a_ssd_bwd"))
    g["a_a_log"] = dalog[:, 0, :HEADS_PER_GROUP].reshape(1, SSM_HEADS)
    g["a_d_skip"] = ddsk[:, 0, :HEADS_PER_GROUP].reshape(1, SSM_HEADS)
    ddt = _lanes_pad(ddt4[:, :, :HEADS_PER_GROUP].transpose(1, 0, 2).reshape(rows, SSM_HEADS))
    ddtr, dbias = _dt_bwd("a_dt_bwd", ddt, dtr, dt_bias)
    g["a_dt_bias"] = dbias[:, :SSM_HEADS]
    dxp, gw_x, gb_x = _conv4_bwd("a_conv_bwd_x", zx, dxs, p["a_conv_w"], p["a_conv_b"], 0)
    dbp, gw_b, gb_b = _conv4_bwd("a_conv_bwd_b", zx, dbm, p["a_conv_w"], p["a_conv_b"], D_INNER)
    dcp, gw_c, gb_c = _conv4_bwd("a_conv_bwd_c", zx, dcm, p["a_conv_w"], p["a_conv_b"], D_INNER + D_BC)
    g["a_conv_w"] = jnp.concatenate([gw_x, gw_b, gw_c], axis=1)
    g["a_conv_b"] = jnp.concatenate([gb_x, gb_b, gb_c], axis=1)
    dzx = jnp.concatenate([dz, dxp, dbp, dcp], axis=1)
    g_main = _mm("a_in_main_dw", hn0, dzx, "tn", out_dtype=BF16, steps=plan.steps("a_in_main_dw"))
    g_dt = _mm("a_in_dt_dw", hn0, ddtr, "tn", out_dtype=BF16)
    plan.grad("a_w_in", None, jnp.concatenate([g_main, g_dt[:, :SSM_HEADS]], axis=1))
    dhn0 = _mm("a_in_dt_dx", ddtr, w_dt, "nt", steps=plan.steps("a_in_dt_dx"))
    dhn0 = _mm("a_in_main_dx", dzx, w_in, "nt", acc=dhn0, steps=plan.steps("a_in_main_dx"))
    dh, g["a_norm_pre"] = _norm_bwd_add("a_norm_bwd", dh, dhn0, h0, p["a_norm_pre"])

    g["meta_tokens"] = dh[PAD_ROWS:CHUNK]
    return loss, dh[CHUNK:], g


ANY = pl.BlockSpec(memory_space=pl.ANY)
VMEM_SPEC = pl.BlockSpec(memory_space=pltpu.VMEM)


def _allgather_small(name, shard):
    rows = shard.shape[0]

    def body(s_ref, o_ref, send_sems, recv_sems):
        x, y, c = _place()
        me = 2 * x + y
        o_ref[me] = s_ref[...]
        chips = _other_chips(x, y)
        sends = [pltpu.make_async_remote_copy(s_ref, o_ref.at[me], send_sems.at[j], recv_sems.at[j],
                                              device_id=(cx, cy, c), device_id_type=MESH)
                 for j, (cx, cy) in enumerate(chips)]
        for cp in sends:
            cp.start()
        for j, (cx, cy) in enumerate(chips):
            pltpu.make_async_remote_copy(s_ref, o_ref.at[2 * cx + cy], send_sems.at[j], recv_sems.at[j],
                                         device_id=(cx, cy, c), device_id_type=MESH).wait_recv()
        for cp in sends:
            cp.wait_send()

    return pl.pallas_call(
        body, name=name, out_shape=jax.ShapeDtypeStruct((N_CHIPS, rows, LANES), F32),
        in_specs=[VMEM_SPEC], out_specs=VMEM_SPEC,
        scratch_shapes=[pltpu.SemaphoreType.DMA((3,)), pltpu.SemaphoreType.DMA((3,))],
        compiler_params=pltpu.CompilerParams(vmem_limit_bytes=VMEM_LIMIT),
    )(shard)


def _row_block(rows, width, itemsize, align, budget=2 << 20):
    best = None
    for cand in range(align, rows + 1, align):
        if rows % cand == 0 and cand * width * itemsize <= budget:
            best = cand
    assert best is not None, (rows, width)
    return best


def _cast_into_slot(name, chip, w2d):
    rows, width = w2d.shape
    tr = _row_block(rows, width, 4, 16)

    def body(chip_ref, w_ref, o_ref):
        o_ref[...] = w_ref[...].astype(BF16)

    return pl.pallas_call(
        body, name=name, out_shape=jax.ShapeDtypeStruct((N_CHIPS, rows, width), BF16),
        grid_spec=pltpu.PrefetchScalarGridSpec(
            num_scalar_prefetch=1, grid=(rows // tr,),
            in_specs=[pl.BlockSpec((tr, width), lambda i, chip_ref: (i, 0))],
            out_specs=pl.BlockSpec((None, tr, width), lambda i, chip_ref: (chip_ref[0], i, 0))),
        compiler_params=_cparams(("parallel",)),
    )(chip, w2d)


def _allreduce_small(name, vec):
    rows = vec.shape[0]

    def body(v_ref, o_ref, buf, send_sems, recv_sems):
        x, y, c = _place()
        me = 4 * x + 2 * y + c
        buf[me] = v_ref[...]

        def peer(k):
            kx, ky, kc = (k >> 2) & 1, (k >> 1) & 1, k & 1
            return (1 - x if kx else x, 1 - y if ky else y, 1 - c if kc else c)

        sends = []
        for k in range(1, N_DEV):
            cp = pltpu.make_async_remote_copy(v_ref, buf.at[me], send_sems.at[k - 1], recv_sems.at[k - 1],
                                              device_id=peer(k), device_id_type=MESH)
            cp.start()
            sends.append(cp)
        for k in range(1, N_DEV):
            px, py, pc = peer(k)
            pltpu.make_async_remote_copy(v_ref, buf.at[4 * px + 2 * py + pc], send_sems.at[k - 1], recv_sems.at[k - 1],
                                         device_id=(px, py, pc), device_id_type=MESH).wait_recv()
        for cp in sends:
            cp.wait_send()
        acc = buf[0]
        for d in range(1, N_DEV):
            acc = acc + buf[d]
        o_ref[...] = acc

    return pl.pallas_call(
        body, name=name, out_shape=jax.ShapeDtypeStruct((rows, LANES), F32),
        in_specs=[VMEM_SPEC], out_specs=VMEM_SPEC,
        scratch_shapes=[pltpu.VMEM((N_DEV, rows, LANES), F32), pltpu.SemaphoreType.DMA((N_DEV - 1,)),
                        pltpu.SemaphoreType.DMA((N_DEV - 1,))],
        compiler_params=pltpu.CompilerParams(vmem_limit_bytes=VMEM_LIMIT),
    )(vec)


def _rs_pair_add(name, place, grads, partner):
    _, half_rows, width = partner.shape
    tr = _row_block(half_rows, width, 2, 16)
    nb = half_rows // tr

    def body(place_ref, g_ref, p_ref, o_ref):
        o_ref[...] = (g_ref[...].astype(F32) + p_ref[...].astype(F32)).astype(BF16)

    return pl.pallas_call(
        body, name=name, out_shape=jax.ShapeDtypeStruct(partner.shape, BF16),
        grid_spec=pltpu.PrefetchScalarGridSpec(
            num_scalar_prefetch=1, grid=(N_CHIPS, nb),
            in_specs=[pl.BlockSpec((None, tr, width), lambda s, i, pr: (s, pr[1] * nb + i, 0)),
                      pl.BlockSpec((None, tr, width), lambda s, i, pr: (s, i, 0))],
            out_specs=pl.BlockSpec((None, tr, width), lambda s, i, pr: (s, i, 0))),
        compiler_params=_cparams(("parallel", "parallel")),
    )(place, grads, partner)


def _rs_chip_add(name, place, mine, others):
    _, half_rows, width = mine.shape
    tr = _row_block(half_rows, width, 4, 16, budget=1 << 20)
    nb = half_rows // tr

    def body(place_ref, q_ref, r_ref, o_ref):
        acc = q_ref[...].astype(F32)
        for j in range(3):
            acc = acc + r_ref[j].astype(F32)
        o_ref[...] = acc

    return pl.pallas_call(
        body, name=name, out_shape=jax.ShapeDtypeStruct((2 * half_rows, width), F32),
        grid_spec=pltpu.PrefetchScalarGridSpec(
            num_scalar_prefetch=1, grid=(nb,),
            in_specs=[pl.BlockSpec((None, tr, width), lambda i, pr: (pr[0], i, 0)),
                      pl.BlockSpec((3, tr, width), lambda i, pr: (0, i, 0))],
            out_specs=pl.BlockSpec((tr, width), lambda i, pr: (pr[1] * nb + i, 0))),
        compiler_params=_cparams(("parallel",)),
    )(place, mine, others)


WEIGHTS = ["meta_tokens", "a_norm_pre", "a_w_in", "a_conv_w", "a_conv_b", "a_dt_bias", "a_a_log", "a_d_skip",
           "a_gate_norm", "a_w_out", "a_norm_post", "kv_norm", "w_kv", "b_norm_pre", "b_w_q", "b_sinks", "b_w_o",
           "b_norm_post", "f_norm_pre", "f_w_up", "f_conv_w", "f_conv_b", "f_w_down", "f_norm_post"]
FULL_SHAPE = {
    "meta_tokens": (16, 1024), "a_norm_pre": (1, 1024), "a_w_in": (1, 1024, 5152), "a_conv_w": (1, 4, 3072),
    "a_conv_b": (1, 3072), "a_dt_bias": (1, 32), "a_a_log": (1, 32), "a_d_skip": (1, 32), "a_gate_norm": (1, 2048),
    "a_w_out": (1, 2048, 1024), "a_norm_post": (1, 1024), "kv_norm": (1024,), "w_kv": (1024, 512),
    "b_norm_pre": (1, 1024), "b_w_q": (1, 1024, 1024), "b_sinks": (1, 16), "b_w_o": (1, 1024, 1024),
    "b_norm_post": (1, 1024), "f_norm_pre": (2, 1024), "f_w_up": (2, 1024, 5632), "f_conv_w": (2, 3, 5632),
    "f_conv_b": (2, 5632), "f_w_down": (2, 2816, 1024), "f_norm_post": (2, 1024),
}
SHARD_AXIS = {
    "meta_tokens": 1, "a_norm_pre": 1, "a_w_in": 2, "a_conv_w": 2, "a_conv_b": 1, "a_dt_bias": None, "a_a_log": None,
    "a_d_skip": None, "a_gate_norm": 1, "a_w_out": 1, "a_norm_post": 1, "kv_norm": None, "w_kv": 0, "b_norm_pre": None,
    "b_w_q": 1, "b_sinks": None, "b_w_o": 1, "b_norm_post": None, "f_norm_pre": None, "f_w_up": 2, "f_conv_w": 2,
    "f_conv_b": None, "f_w_down": 1, "f_norm_post": None,
}
BIG = ["a_w_in", "a_w_out", "w_kv", "b_w_q", "b_w_o", "f_w_up", "f_w_down"]
SMALL = [n for n in WEIGHTS if n not in BIG]
SMALL_SHARDED = [n for n in SMALL if SHARD_AXIS[n] is not None]


def _shard_shape(name):
    shape = list(FULL_SHAPE[name])
    if SHARD_AXIS[name] is not None:
        shape[SHARD_AXIS[name]] //= N_CHIPS
    return tuple(shape)


def _numel(shape):
    return int(math.prod(shape))


SUBLANES = 8


def _packed_rows(shape):
    rows = -(-_numel(shape) // LANES)
    return -(-rows // SUBLANES) * SUBLANES


def _pack(arrays):
    parts = []
    for a in arrays:
        size, rows = _numel(a.shape), _packed_rows(a.shape)
        if size % LANES == 0:
            part = jnp.pad(a.reshape(size // LANES, LANES), ((0, rows - size // LANES), (0, 0)))
        else:
            part = jnp.pad(a.reshape(-1), (0, rows * LANES - size)).reshape(rows, LANES)
        parts.append(part)
    return jnp.concatenate(parts, axis=0)


def _unpack(packed, names, shape_of):
    out, off = {}, 0
    lead = packed.shape[:-2]
    for n in names:
        shape = tuple(shape_of(n))
        size, rows = _numel(shape), _packed_rows(shape)
        part = packed[..., off:off + rows, :]
        if size % LANES == 0:
            out[n] = part[..., :size // LANES, :].reshape(lead + shape)
        else:
            out[n] = part.reshape(lead + (rows * LANES,))[..., :size].reshape(lead + shape)
        off += rows
    return out


def _split_chips(name, full):
    ax = SHARD_AXIS[name]
    shape = full.shape
    cut = shape[:ax] + (N_CHIPS, shape[ax] // N_CHIPS) + shape[ax + 1:]
    return jnp.moveaxis(full.reshape(cut), ax, 0)


def _join_chips(name, stacked):
    ax = SHARD_AXIS[name]
    moved = jnp.moveaxis(stacked, 0, ax)
    shape = moved.shape
    return moved.reshape(shape[:ax] + (shape[ax] * shape[ax + 1],) + shape[ax + 2:])


def _as2d(a):
    return a.reshape(-1, a.shape[-1])


BUFFERS = [("a_w_in", "a_w_in", None), ("a_w_out", "a_w_out", None), ("w_kv", "w_kv", None),
           ("b_w_q", "b_w_q", None), ("b_w_o", "b_w_o", None), ("f_w_up0", "f_w_up", 0), ("f_w_up1", "f_w_up", 1),
           ("f_w_down0", "f_w_down", 0), ("f_w_down1", "f_w_down", 1)]


def _local_shard(arrays, weight, layer):
    return _as2d(arrays[weight]) if layer is None else arrays[weight][layer]


def _weight_from_gathered(weight, buf):
    if weight == "a_w_in":
        return buf.transpose(1, 0, 2).reshape(buf.shape[1], N_CHIPS * buf.shape[2])
    if weight == "f_w_up":
        return buf
    return buf.reshape(N_CHIPS * buf.shape[1], buf.shape[2])


def _gathered_from_grad(weight, g):
    if weight == "a_w_in":
        rows = g.shape[0]
        return g.reshape(rows, N_CHIPS, g.shape[1] // N_CHIPS).transpose(1, 0, 2).astype(BF16)
    if weight == "f_w_up":
        return g
    return g.reshape(N_CHIPS, g.shape[0] // N_CHIPS, g.shape[1]).astype(BF16)


GATHER_SCHEDULE = {
    "a_in_main": [("ici", ["a_w_out"])],
    "a_conv": [("d2d", ["a_w_out"]), ("ici", ["f_w_down0"])],
    "a_ssd_prep": [("d2d", ["f_w_down0"]), ("ici", ["w_kv", "b_w_q", "b_w_o"])],
    "a_ssd": [("d2d", ["w_kv", "b_w_q", "b_w_o"]), ("ici", ["f_w_up0"])],
    "a_gate": [("d2d", ["f_w_up0"])],
    "ffn0_conv": [("ici", ["f_w_down1"])],
    "b_attn": [("d2d", ["f_w_down1"]), ("ici", ["f_w_up1"])],
    "b_o": [("d2d", ["f_w_up1"])],
}
REDUCE_SCHEDULE = {
    "ffn1_conv_bwd": ["f_w_down1"],
    "b_attn_bwd": ["f_w_up1", "b_w_o"],
    "ffn0_conv_bwd": ["b_w_q", "w_kv", "f_w_down0"],
    "a_ssd_bwd": ["f_w_up0", "a_w_out"],
    "a_in_main_dx": ["a_w_in"],
}
PAIR_SCHEDULE = {
    "ffn1_down_dx": ["f_w_down1"],
    "b_o_dx": ["f_w_up1", "b_w_o"],
    "ffn0_down_dx": ["b_w_q", "w_kv", "f_w_down0"],
    "a_out_dx": ["f_w_up0", "a_w_out"],
    "a_in_dt_dx": ["a_w_in"],
}
SWAP_SCHEDULE = {"a_in_main_dw": ["f_w_down1", "f_w_up1", "b_w_o", "b_w_q", "w_kv", "f_w_down0", "f_w_up0", "a_w_out"]}


def _buffer_of(weight, layer):
    return weight if layer is None else f"{weight}{layer}"


class _Pipeline:
    def __init__(self, place, slots):
        self.place = place
        self.slots = dict(slots)
        self.running = []
        self.grads = {}
        self.theirs = {}
        self.partials = {}
        self.peers = {}
        self.reduced = {}

    def _collect(self):
        for step, buffers, table in self.running:
            table.update(zip(buffers, step.results))
        self.running = []

    def gather_now(self, name, buffers):
        step = _step_gather_full([self.slots[b] for b in buffers])
        _run_steps(name, [step])
        self.slots.update(zip(buffers, step.results))

    def weight(self, name, layer=None):
        self._collect()
        return _weight_from_gathered(name, self.slots[_buffer_of(name, layer)])

    def grad(self, name, layer, g):
        self.grads[_buffer_of(name, layer)] = _gathered_from_grad(name, g)

    def steps(self, kernel):
        self._collect()
        steps = []
        for phase, buffers in GATHER_SCHEDULE.get(kernel, []):
            make = _step_gather_ici if phase == "ici" else _step_gather_d2d
            step = make([self.slots[b] for b in buffers])
            self.running.append((step, buffers, self.slots))
            steps.append(step)
        buffers = PAIR_SCHEDULE.get(kernel)
        if buffers:
            step = _step_pair_exchange([self.grads[b] for b in buffers])
            self.running.append((step, buffers, self.theirs))
            steps.append(step)
        buffers = REDUCE_SCHEDULE.get(kernel)
        if buffers:
            for b in buffers:
                self.partials[b] = _rs_pair_add("reduce_pair_add_" + b, self.place, self.grads[b], self.theirs[b])
            step = _step_chip_exchange([self.partials[b] for b in buffers])
            self.running.append((step, buffers, self.peers))
            steps.append(step)
        buffers = SWAP_SCHEDULE.get(kernel)
        if buffers:
            step = self._swap_step(buffers)
            self.running.append((step, buffers, self.reduced))
            steps.append(step)
        return steps

    def _swap_step(self, buffers):
        halves = [_rs_chip_add("reduce_chip_add_" + b, self.place, self.partials[b], self.peers[b]) for b in buffers]
        return _step_pair_gather(halves)

    def finish(self):
        self._collect()
        rest = [b for b, _, _ in BUFFERS if b not in self.reduced]
        step = self._swap_step(rest)
        _run_steps("reduce_pair_gather", [step])
        self.reduced.update(zip(rest, step.results))
        return self.reduced


def kernel(x, meta_tokens, a_norm_pre, a_w_in, a_conv_w, a_conv_b, a_dt_bias, a_a_log, a_d_skip, a_gate_norm, a_w_out, a_norm_post, kv_norm, w_kv, b_norm_pre, b_w_q, b_sinks, b_w_o, b_norm_post, f_norm_pre, f_w_up, f_conv_w, f_conv_b, f_w_down, f_norm_post, loss_target, m_meta_tokens, m_a_norm_pre, m_a_w_in, m_a_conv_w, m_a_conv_b, m_a_dt_bias, m_a_a_log, m_a_d_skip, m_a_gate_norm, m_a_w_out, m_a_norm_post, m_kv_norm, m_w_kv, m_b_norm_pre, m_b_w_q, m_b_sinks, m_b_w_o, m_b_norm_post, m_f_norm_pre, m_f_w_up, m_f_conv_w, m_f_conv_b, m_f_w_down, m_f_norm_post, v_meta_tokens, v_a_norm_pre, v_a_w_in, v_a_conv_w, v_a_conv_b, v_a_dt_bias, v_a_a_log, v_a_d_skip, v_a_gate_norm, v_a_w_out, v_a_norm_post, v_kv_norm, v_w_kv, v_b_norm_pre, v_b_w_q, v_b_sinks, v_b_w_o, v_b_norm_post, v_f_norm_pre, v_f_w_up, v_f_conv_w, v_f_conv_b, v_f_w_down, v_f_norm_post):
    given = dict(locals())
    w = {n: given[n] for n in WEIGHTS}
    mom = {n: given["m_" + n] for n in WEIGHTS}
    var = {n: given["v_" + n] for n in WEIGHTS}
    chip = 2 * lax.axis_index("x") + lax.axis_index("y")
    core = lax.axis_index("c")
    place = jnp.stack([chip, core]).astype(jnp.int32)

    small_all = _allgather_small("gather_small", _pack([w[n] for n in SMALL_SHARDED]))
    small_parts = _unpack(small_all, SMALL_SHARDED, _shard_shape)
    slots = {b: _cast_into_slot("cast_" + b, place, _local_shard(w, wn, layer)) for b, wn, layer in BUFFERS}
    pipeline = _Pipeline(place, slots)
    pipeline.gather_now("gather_first", ["a_w_in"])
    p = {}
    for n in SMALL:
        p[n] = _join_chips(n, small_parts[n]) if n in SMALL_SHARDED else w[n]
    p["a_conv_w"] = p["a_conv_w"][0]
    p["kv_norm"] = p["kv_norm"].reshape(1, D_MODEL)

    loss_local, grad_x, g = _local_step(x[0], loss_target[0], p, pipeline)
    loss = lax.psum(loss_local, ("x", "y", "c"))

    small_sum = _allreduce_small("reduce_small", _pack([g[n].reshape(FULL_SHAPE[n]) for n in SMALL]))
    small_red = _unpack(small_sum, SMALL, lambda n: FULL_SHAPE[n])
    grads = {}
    for n in SMALL:
        if SHARD_AXIS[n] is None:
            grads[n] = small_red[n]
        else:
            grads[n] = lax.dynamic_index_in_dim(_split_chips(n, small_red[n]), chip, 0, keepdims=False)

    shard_sum = pipeline.finish()
    for n in BIG:
        if n in ("f_w_up", "f_w_down"):
            grads[n] = jnp.stack([shard_sum[n + "0"], shard_sum[n + "1"]])
        else:
            grads[n] = shard_sum[n].reshape(_shard_shape(n))

    delta, new_m, new_v = {}, {}, {}
    for n in BIG:
        shape = _shard_shape(n)
        d, m2, v2 = _adamw("adamw_" + n, _as2d(w[n]), _as2d(grads[n]), _as2d(mom[n]), _as2d(var[n]))
        delta[n], new_m[n], new_v[n] = d.reshape(shape), m2.reshape(shape), v2.reshape(shape)
    packed = [_pack([src[n].reshape(_shard_shape(n)) for n in SMALL]) for src in (w, grads, mom, var)]
    outs = _adamw("adamw_small", *packed)
    for dst, flat in zip((delta, new_m, new_v), outs):
        dst.update(_unpack(flat, SMALL, _shard_shape))

    return (loss, grad_x[None], *[grads[n].reshape(_shard_shape(n)) for n in WEIGHTS],
            *[delta[n] for n in WEIGHTS], *[new_m[n] for n in WEIGHTS], *[new_v[n] for n in WEIGHTS])
```

```python
import functools
import math

import jax
import jax.numpy as jnp
from jax import lax
from jax.experimental import pallas as pl
from jax.experimental.pallas import tpu as pltpu

F32, BF16 = jnp.float32, jnp.bfloat16
MESH = pl.DeviceIdType.MESH

D_MODEL = 1024
N_META = 16
CHUNK = 128
PAD_ROWS = CHUNK - N_META
D_INNER = 2048
D_STATE = 128
N_GROUPS = 4
HEADS_PER_GROUP = 8
SSM_HEADS = 32
HEAD_DIM = 64
D_BC = N_GROUPS * D_STATE
D_XBC = D_INNER + 2 * D_BC
D_MAIN = D_INNER + D_XBC
D_IN_PROJ = D_MAIN + SSM_HEADS
GROUP_W = HEADS_PER_GROUP * HEAD_DIM
SSM_CONV = 4
D_FF = 2816
FFN_CONV = 3
N_Q_HEADS = 16
N_KV_HEADS = 4
D_KV = 256
ATTN_SCALE = 1.0 / math.sqrt(HEAD_DIM)
RMS_EPS = 1e-6
NEG_INF = -1e30
LANES = 128
VMEM_LIMIT = 48 * 1024 * 1024

ADAM_LR, ADAM_B1, ADAM_B2, ADAM_EPS, ADAM_WD, ADAM_STEP = 0.001, 0.9, 0.999, 1e-08, 0.01, 10

N_CHIPS = 4
N_DEV = 8


def _cparams(sem=None):
    return pltpu.CompilerParams(dimension_semantics=sem, vmem_limit_bytes=VMEM_LIMIT)


def _tile(n, cands=(512, 256, 128)):
    for t in cands:
        if n % t == 0:
            return t
    return n


def _row_tile(rows, width):
    for t in (544, 272):
        if rows % t == 0 and t * width * 4 <= (3 << 20):
            return t
    return 128


def _rows_mask(i, tm):
    rows = i * tm + lax.broadcasted_iota(jnp.int32, (tm, 1), 0)
    return rows >= PAD_ROWS


def _dot(a, b):
    return jnp.dot(a, b, preferred_element_type=F32)


def _dot_nt(a, b):
    return lax.dot_general(a, b, (((1,), (1,)), ((), ())), preferred_element_type=F32)


def _dot_tn(a, b):
    return lax.dot_general(a, b, (((0,), (0,)), ((), ())), preferred_element_type=F32)


def _sigmoid(x):
    return 1.0 / (1.0 + jnp.exp(-x))


def _place():
    return lax.axis_index("x"), lax.axis_index("y"), lax.axis_index("c")


def _other_chips(x, y):
    return [(1 - x, y), (x, 1 - y), (1 - x, 1 - y)]


class _Step:
    def __init__(self, ins, outs, aliases, n_sems, start, finish):
        self.ins, self.outs, self.aliases, self.n_sems = list(ins), list(outs), dict(aliases), n_sems
        self.start, self.finish = start, finish
        self.results = None


def _like(a):
    return jax.ShapeDtypeStruct(a.shape, a.dtype)


def _remote(src, dst, send_sems, recv_sems, k, device):
    return pltpu.make_async_remote_copy(src, dst, send_sems.at[k], recv_sems.at[k], device_id=device, device_id_type=MESH)


def _half_rows(ref, axis, which):
    hr = ref.shape[axis] // 2
    return pl.ds(which * hr, hr)


def _step_gather_ici(bufs):
    def copies(outs, send_sems, recv_sems, received):
        x, y, c = _place()
        me = 2 * x + y
        for k, o in enumerate(outs):
            for j, (cx, cy) in enumerate(_other_chips(x, y)):
                part = o.at[2 * cx + cy if received else me, _half_rows(o, 1, c)]
                yield _remote(part, part, send_sems, recv_sems, 3 * k + j, (cx, cy, c))

    def start(ins, outs, send_sems, recv_sems):
        for cp in copies(outs, send_sems, recv_sems, False):
            cp.start()

    def finish(ins, outs, send_sems, recv_sems):
        for cp in copies(outs, send_sems, recv_sems, True):
            cp.wait_recv()
        for cp in copies(outs, send_sems, recv_sems, False):
            cp.wait_send()

    return _Step(bufs, [_like(b) for b in bufs], {k: k for k in range(len(bufs))}, 3 * len(bufs), start, finish)


def _step_gather_d2d(bufs):
    def copies(outs, send_sems, recv_sems, received):
        x, y, c = _place()
        for k, o in enumerate(outs):
            for j, (cx, cy) in enumerate(_other_chips(x, y)):
                part = o.at[2 * cx + cy, _half_rows(o, 1, 1 - c if received else c)]
                yield _remote(part, part, send_sems, recv_sems, 3 * k + j, (x, y, 1 - c))

    def start(ins, outs, send_sems, recv_sems):
        for cp in copies(outs, send_sems, recv_sems, False):
            cp.start()

    def finish(ins, outs, send_sems, recv_sems):
        for cp in copies(outs, send_sems, recv_sems, True):
            cp.wait_recv()
        for cp in copies(outs, send_sems, recv_sems, False):
            cp.wait_send()

    return _Step(bufs, [_like(b) for b in bufs], {k: k for k in range(len(bufs))}, 3 * len(bufs), start, finish)


def _step_gather_full(bufs):
    n = len(bufs)

    def ici(outs, send_sems, recv_sems, received):
        x, y, c = _place()
        me = 2 * x + y
        for k, o in enumerate(outs):
            for j, (cx, cy) in enumerate(_other_chips(x, y)):
                part = o.at[2 * cx + cy if received else me, _half_rows(o, 1, c)]
                yield _remote(part, part, send_sems, recv_sems, 3 * k + j, (cx, cy, c))

    def d2d(outs, send_sems, recv_sems, received):
        x, y, c = _place()
        for k, o in enumerate(outs):
            for j, (cx, cy) in enumerate(_other_chips(x, y)):
                part = o.at[2 * cx + cy, _half_rows(o, 1, 1 - c if received else c)]
                yield _remote(part, part, send_sems, recv_sems, 3 * n + 3 * k + j, (x, y, 1 - c))

    def start(ins, outs, send_sems, recv_sems):
        for cp in ici(outs, send_sems, recv_sems, False):
            cp.start()

    def finish(ins, outs, send_sems, recv_sems):
        for arrived, onward in zip(ici(outs, send_sems, recv_sems, True), d2d(outs, send_sems, recv_sems, False)):
            arrived.wait_recv()
            onward.start()
        for cp in d2d(outs, send_sems, recv_sems, True):
            cp.wait_recv()
        for cp in ici(outs, send_sems, recv_sems, False):
            cp.wait_send()
        for cp in d2d(outs, send_sems, recv_sems, False):
            cp.wait_send()

    return _Step(bufs, [_like(b) for b in bufs], {k: k for k in range(n)}, 6 * n, start, finish)


def _step_pair_exchange(grads):
    def copies(ins, outs, send_sems, recv_sems):
        x, y, c = _place()
        for k, (g, o) in enumerate(zip(ins, outs)):
            yield _remote(g.at[:, _half_rows(g, 1, 1 - c)], o, send_sems, recv_sems, k, (x, y, 1 - c))

    def start(ins, outs, send_sems, recv_sems):
        for cp in copies(ins, outs, send_sems, recv_sems):
            cp.start()

    def finish(ins, outs, send_sems, recv_sems):
        for cp in copies(ins, outs, send_sems, recv_sems):
            cp.wait()

    outs = [jax.ShapeDtypeStruct((N_CHIPS, g.shape[1] // 2, g.shape[2]), g.dtype) for g in grads]
    return _Step(grads, outs, {}, len(grads), start, finish)


def _step_chip_exchange(partials):
    def copies(ins, outs, send_sems, recv_sems):
        x, y, c = _place()
        for k, (q, o) in enumerate(zip(ins, outs)):
            for j, (cx, cy) in enumerate(_other_chips(x, y)):
                yield _remote(q.at[2 * cx + cy], o.at[j], send_sems, recv_sems, 3 * k + j, (cx, cy, c))

    def start(ins, outs, send_sems, recv_sems):
        for cp in copies(ins, outs, send_sems, recv_sems):
            cp.start()

    def finish(ins, outs, send_sems, recv_sems):
        for cp in copies(ins, outs, send_sems, recv_sems):
            cp.wait()

    outs = [jax.ShapeDtypeStruct((3,) + q.shape[1:], q.dtype) for q in partials]
    return _Step(partials, outs, {}, 3 * len(partials), start, finish)


def _step_pair_gather(shards):
    def copies(outs, send_sems, recv_sems, received):
        x, y, c = _place()
        for k, o in enumerate(outs):
            part = o.at[_half_rows(o, 0, 1 - c if received else c)]
            yield _remote(part, part, send_sems, recv_sems, k, (x, y, 1 - c))

    def start(ins, outs, send_sems, recv_sems):
        for cp in copies(outs, send_sems, recv_sems, False):
            cp.start()

    def finish(ins, outs, send_sems, recv_sems):
        for cp in copies(outs, send_sems, recv_sems, True):
            cp.wait_recv()
        for cp in copies(outs, send_sems, recv_sems, False):
            cp.wait_send()

    return _Step(shards, [_like(s) for s in shards], {k: k for k in range(len(shards))}, len(shards), start, finish)


def _call(body, *, name, out_shape, grid, in_specs, out_specs, operands, scratch_shapes=(), semantics=None, steps=()):
    single = not isinstance(out_shape, (tuple, list))
    out_shapes = [out_shape] if single else list(out_shape)
    out_spec_list = [out_specs] if single else list(out_specs)
    steps = list(steps)
    if not steps:
        res = pl.pallas_call(body, name=name, out_shape=out_shapes, grid=grid, in_specs=list(in_specs),
                             out_specs=out_spec_list, scratch_shapes=list(scratch_shapes),
                             compiler_params=_cparams(semantics))(*operands)
        return res[0] if single else res
    n_in, n_out, n_scr = len(operands), len(out_shapes), len(scratch_shapes)
    x_in = [a for s in steps for a in s.ins]
    x_out = [o for s in steps for o in s.outs]
    aliases, in_off, out_off = {}, 0, 0
    for s in steps:
        for i, o in s.aliases.items():
            aliases[n_in + in_off + i] = n_out + out_off + o
        in_off += len(s.ins)
        out_off += len(s.outs)
    sems = []
    for s in steps:
        sems += [pltpu.SemaphoreType.DMA((s.n_sems,)), pltpu.SemaphoreType.DMA((s.n_sems,))]
    any_spec = pl.BlockSpec(memory_space=pl.ANY)

    def carried(*refs):
        pos = 0
        ins = refs[pos:pos + n_in]; pos += n_in
        xi = refs[pos:pos + len(x_in)]; pos += len(x_in)
        outs = refs[pos:pos + n_out]; pos += n_out
        xo = refs[pos:pos + len(x_out)]; pos += len(x_out)
        scr = refs[pos:pos + n_scr]; pos += n_scr
        sem_refs = refs[pos:]

        def each(action):
            i0 = o0 = 0
            for k, s in enumerate(steps):
                getattr(s, action)(xi[i0:i0 + len(s.ins)], xo[o0:o0 + len(s.outs)], sem_refs[2 * k], sem_refs[2 * k + 1])
                i0 += len(s.ins)
                o0 += len(s.outs)

        if grid:
            first = functools.reduce(jnp.logical_and, [pl.program_id(d) == 0 for d in range(len(grid))])
            last = functools.reduce(jnp.logical_and, [pl.program_id(d) == grid[d] - 1 for d in range(len(grid))])
            pl.when(first)(lambda: each("start"))
            body(*ins, *outs, *scr)
            pl.when(last)(lambda: each("finish"))
        else:
            each("start")
            body(*ins, *outs, *scr)
            each("finish")

    res = pl.pallas_call(
        carried, name=name, out_shape=out_shapes + x_out, grid=grid,
        in_specs=list(in_specs) + [any_spec] * len(x_in), out_specs=out_spec_list + [any_spec] * len(x_out),
        scratch_shapes=list(scratch_shapes) + sems, input_output_aliases=aliases,
        compiler_params=_cparams(None if semantics is None else ("arbitrary",) * len(grid)),
    )(*operands, *x_in)
    o0 = n_out
    for s in steps:
        s.results = list(res[o0:o0 + len(s.outs)])
        o0 += len(s.outs)
    return res[0] if single else tuple(res[:n_out])


def _run_steps(name, steps):
    _call(lambda: None, name=name, out_shape=[], grid=(), in_specs=[], out_specs=[], operands=[], steps=steps)
    return [s.results for s in steps]


def _mm(name, a, b, mode, out_dtype=F32, acc=None, b_colblock=0, n_cols=None, steps=()):
    resident_bytes = 8 << 20
    if mode == "nn":
        m, k = a.shape
        n = n_cols or b.shape[1]
        tm = m
        while tm * k * 2 > resident_bytes and tm % 32 == 0:
            tm //= 2
        tn = _tile(n)
        grid = (m // tm, n // tn)
        in_specs = [pl.BlockSpec((tm, k), lambda i, j: (i, 0)), pl.BlockSpec((k, tn), lambda i, j: (0, j))]
        out_shape, out_block = (m, n), (tm, tn)
    elif mode == "nt":
        m, n = a.shape
        k = b.shape[0]
        tm = m
        while tm * n * 2 > resident_bytes and tm % 32 == 0:
            tm //= 2
        tk = _tile(k)
        grid = (m // tm, k // tk)
        in_specs = [pl.BlockSpec((tm, n), lambda i, j: (i, 0)), pl.BlockSpec((tk, n), lambda i, j: (j, b_colblock))]
        out_shape, out_block = (m, k), (tm, tk)
    else:
        m, k = a.shape
        n = b.shape[1]
        tk, tn = _tile(k), _tile(n)
        grid = (k // tk, n // tn)
        in_specs = [pl.BlockSpec((m, tk), lambda i, j: (0, i)), pl.BlockSpec((m, tn), lambda i, j: (0, j))]
        out_shape, out_block = (k, n), (tk, tn)
    out_spec = pl.BlockSpec(out_block, lambda i, j: (i, j))
    has_acc = acc is not None

    def body(*refs):
        a_ref, b_ref = refs[0], refs[1]
        o_ref = refs[-1]
        av, bv = a_ref[...], b_ref[...]
        if mode == "nn":
            r = _dot(av, bv)
        elif mode == "nt":
            r = _dot_nt(av, bv)
        else:
            r = _dot_tn(av, bv)
        if has_acc:
            r = r + refs[2][...]
        o_ref[...] = r.astype(o_ref.dtype)

    operands = [a, b]
    if has_acc:
        in_specs = in_specs + [out_spec]
        operands.append(acc)
    return _call(body, name=name, out_shape=jax.ShapeDtypeStruct(out_shape, out_dtype), grid=grid, in_specs=in_specs,
                 out_specs=out_spec, operands=operands, semantics=("parallel", "parallel"), steps=steps)


def _fit_rows(m, row_bytes, budget=8 << 20):
    tm = m
    while tm * row_bytes > budget and tm % 32 == 0:
        tm //= 2
    return tm


def _mm_nn_bychip(name, a, bc):
    m, k = a.shape
    n = bc.shape[2]
    tm = min(_fit_rows(m, k * 2), _fit_rows(m, n * 4))

    def body(a_ref, b_ref, o_ref):
        o_ref[...] = _dot(a_ref[...], b_ref[...])

    return pl.pallas_call(
        body, name=name, out_shape=jax.ShapeDtypeStruct((m, N_CHIPS * n), F32), grid=(m // tm, N_CHIPS),
        in_specs=[pl.BlockSpec((tm, k), lambda i, c: (i, 0)), pl.BlockSpec((None, k, n), lambda i, c: (c, 0, 0))],
        out_specs=pl.BlockSpec((tm, n), lambda i, c: (i, c)), compiler_params=_cparams(("parallel", "parallel")),
    )(a, bc)


def _mm_nt_bychip(name, a, bc, chip0, acc=None):
    m = a.shape[0]
    _, k, n = bc.shape
    nch = a.shape[1] // n
    tm, tk = _fit_rows(m, n * 2), _tile(k)
    has_acc = acc is not None

    def body(*refs):
        a_ref, b_ref, o_ref = refs[0], refs[1], refs[-1]

        @pl.when(pl.program_id(2) == 0)
        def _():
            o_ref[...] = refs[2][...] if has_acc else jnp.zeros_like(o_ref)

        o_ref[...] += _dot_nt(a_ref[...], b_ref[...])

    out_spec = pl.BlockSpec((tm, tk), lambda i, j, c: (i, j))
    in_specs = [pl.BlockSpec((tm, n), lambda i, j, c: (i, c)),
                pl.BlockSpec((None, tk, n), lambda i, j, c: (chip0 + c, j, 0))]
    operands = [a, bc]
    if has_acc:
        in_specs.append(out_spec)
        operands.append(acc)
    return pl.pallas_call(
        body, name=name, out_shape=jax.ShapeDtypeStruct((m, k), F32), grid=(m // tm, k // tk, nch),
        in_specs=in_specs, out_specs=out_spec, compiler_params=_cparams(("parallel", "parallel", "arbitrary")),
    )(*operands)


def _mm_tn_bychip(name, a, dy, n, chip0, into=None):
    m, k = a.shape
    nch = dy.shape[1] // n
    tk = _tile(k)

    def body(*refs):
        a_ref, d_ref, o_ref = refs[0], refs[1], refs[-1]
        o_ref[...] = _dot_tn(a_ref[...], d_ref[...]).astype(BF16)

    in_specs = [pl.BlockSpec((m, tk), lambda i, c: (0, i)), pl.BlockSpec((m, n), lambda i, c: (0, c))]
    operands = [a, dy]
    aliases = {}
    if into is not None:
        in_specs.append(pl.BlockSpec(memory_space=pl.ANY))
        operands.append(into)
        aliases = {2: 0}
    return pl.pallas_call(
        body, name=name, out_shape=jax.ShapeDtypeStruct((N_CHIPS, k, n), BF16), grid=(k // tk, nch),
        in_specs=in_specs, out_specs=pl.BlockSpec((None, tk, n), lambda i, c: (chip0 + c, i, 0)),
        input_output_aliases=aliases, compiler_params=_cparams(("parallel", "parallel")),
    )(*operands)


def _rms_fwd(name, h, w):
    rows, width = h.shape
    tm = _row_tile(rows, width)

    def body(h_ref, w_ref, o_ref):
        x = h_ref[...]
        r = lax.rsqrt(jnp.mean(x * x, axis=-1, keepdims=True) + RMS_EPS)
        o_ref[...] = (x * r * w_ref[...]).astype(BF16)

    return pl.pallas_call(
        body, name=name, out_shape=jax.ShapeDtypeStruct((rows, width), BF16), grid=(rows // tm,),
        in_specs=[pl.BlockSpec((tm, width), lambda i: (i, 0)), pl.BlockSpec((1, width), lambda i: (0, 0))],
        out_specs=pl.BlockSpec((tm, width), lambda i: (i, 0)), compiler_params=_cparams(("parallel",)),
    )(h, w)


def _resid_norm_fwd(name, h, pre, w):
    rows, width = h.shape
    tm = _row_tile(rows, width)

    def body(h_ref, p_ref, w_ref, o_ref):
        p = p_ref[...]
        r = lax.rsqrt(jnp.mean(p * p, axis=-1, keepdims=True) + RMS_EPS)
        o_ref[...] = h_ref[...] + jnp.where(_rows_mask(pl.program_id(0), tm), p * r * w_ref[...], 0.0)

    row_spec = pl.BlockSpec((tm, width), lambda i: (i, 0))
    return pl.pallas_call(
        body, name=name, out_shape=jax.ShapeDtypeStruct((rows, width), F32), grid=(rows // tm,),
        in_specs=[row_spec, row_spec, pl.BlockSpec((1, width), lambda i: (0, 0))],
        out_specs=row_spec, compiler_params=_cparams(("parallel",)),
    )(h, pre, w)


def _resid_norm_bwd(name, dh, pre, w):
    rows, width = dh.shape
    tm = _row_tile(rows, width)

    def body(dh_ref, p_ref, w_ref, dp_ref, dw_ref):
        i = pl.program_id(0)
        dy = jnp.where(_rows_mask(i, tm), dh_ref[...], 0.0)
        p = p_ref[...]
        r = lax.rsqrt(jnp.mean(p * p, axis=-1, keepdims=True) + RMS_EPS)
        xhat = p * r
        dxhat = dy * w_ref[...]
        dp = r * (dxhat - xhat * jnp.mean(dxhat * xhat, axis=-1, keepdims=True))
        dp_ref[...] = dp.astype(BF16)

        @pl.when(i == 0)
        def _():
            dw_ref[...] = jnp.zeros_like(dw_ref)

        dw_ref[...] += jnp.sum(dy * xhat, axis=0, keepdims=True)

    row_spec = pl.BlockSpec((tm, width), lambda i: (i, 0))
    vec_spec = pl.BlockSpec((1, width), lambda i: (0, 0))
    return pl.pallas_call(
        body, name=name,
        out_shape=(jax.ShapeDtypeStruct((rows, width), BF16), jax.ShapeDtypeStruct((1, width), F32)),
        grid=(rows // tm,), in_specs=[row_spec, row_spec, vec_spec], out_specs=(row_spec, vec_spec),
        compiler_params=_cparams(("arbitrary",)),
    )(dh, pre, w)


def _norm_bwd_add(name, dh, dhn, h, w):
    rows, width = dh.shape
    tm = _row_tile(rows, width)

    def body(dh_ref, dhn_ref, h_ref, w_ref, o_ref, dw_ref):
        i = pl.program_id(0)
        x = h_ref[...]
        dy = dhn_ref[...]
        r = lax.rsqrt(jnp.mean(x * x, axis=-1, keepdims=True) + RMS_EPS)
        xhat = x * r
        dxhat = dy * w_ref[...]
        dx = r * (dxhat - xhat * jnp.mean(dxhat * xhat, axis=-1, keepdims=True))
        o_ref[...] = dh_ref[...] + jnp.where(_rows_mask(i, tm), dx, 0.0)

        @pl.when(i == 0)
        def _():
            dw_ref[...] = jnp.zeros_like(dw_ref)

        dw_ref[...] += jnp.sum(dy * xhat, axis=0, keepdims=True)

    row_spec = pl.BlockSpec((tm, width), lambda i: (i, 0))
    vec_spec = pl.BlockSpec((1, width), lambda i: (0, 0))
    return pl.pallas_call(
        body, name=name,
        out_shape=(jax.ShapeDtypeStruct((rows, width), F32), jax.ShapeDtypeStruct((1, width), F32)),
        grid=(rows // tm,), in_specs=[row_spec, row_spec, row_spec, vec_spec], out_specs=(row_spec, vec_spec),
        compiler_params=_cparams(("arbitrary",)),
    )(dh, dhn, h, w)


def _shift_down(x, s, rows):
    return pltpu.roll(x, s, 0) if s else x


def _shift_up(x, s, rows):
    return pltpu.roll(x, rows - s, 0) if s else x


def _conv4_fwd(name, zx, cw, cb, steps=()):
    rows = zx.shape[0]
    off = D_INNER // LANES

    def body(x_ref, w_ref, b_ref, o_ref):
        x = x_ref[...]
        acc = b_ref[...] + w_ref[pl.ds(SSM_CONV - 1, 1), :] * x
        for s in range(1, SSM_CONV):
            acc = acc + w_ref[pl.ds(SSM_CONV - 1 - s, 1), :] * _shift_down(x, s, rows)
        valid = lax.broadcasted_iota(jnp.int32, (rows, 1), 0) >= PAD_ROWS
        o_ref[...] = jnp.where(valid, acc * _sigmoid(acc), 0.0)

    return _call(
        body, name=name, out_shape=jax.ShapeDtypeStruct((rows, D_XBC), F32), grid=(D_XBC // LANES,),
        in_specs=[pl.BlockSpec((rows, LANES), lambda j: (0, j + off)),
                  pl.BlockSpec((SSM_CONV, LANES), lambda j: (0, j)),
                  pl.BlockSpec((1, LANES), lambda j: (0, j))],
        out_specs=pl.BlockSpec((rows, LANES), lambda j: (0, j)), operands=[zx, cw, cb],
        semantics=("parallel",), steps=steps)


def _conv4_bwd(name, zx, dout, cw, cb, col0):
    rows, width = dout.shape
    zoff = (D_INNER + col0) // LANES
    woff = col0 // LANES

    def body(x_ref, d_ref, w_ref, b_ref, dx_ref, dw_ref, db_ref):
        x = x_ref[...]
        shifted = [_shift_down(x, s, rows) for s in range(SSM_CONV)]
        acc = b_ref[...]
        for s in range(SSM_CONV):
            acc = acc + w_ref[pl.ds(SSM_CONV - 1 - s, 1), :] * shifted[s]
        sig = _sigmoid(acc)
        valid = lax.broadcasted_iota(jnp.int32, (rows, 1), 0) >= PAD_ROWS
        dpre = jnp.where(valid, d_ref[...] * sig * (1.0 + acc * (1.0 - sig)), 0.0)
        dx = w_ref[pl.ds(SSM_CONV - 1, 1), :] * dpre
        for s in range(1, SSM_CONV):
            dx = dx + w_ref[pl.ds(SSM_CONV - 1 - s, 1), :] * _shift_up(dpre, s, rows)
        dx_ref[...] = dx.astype(BF16)
        for s in range(SSM_CONV):
            dw_ref[pl.ds(SSM_CONV - 1 - s, 1), :] = jnp.sum(dpre * shifted[s], axis=0, keepdims=True)
        db_ref[...] = jnp.sum(dpre, axis=0, keepdims=True)

    return pl.pallas_call(
        body, name=name,
        out_shape=(jax.ShapeDtypeStruct((rows, width), BF16), jax.ShapeDtypeStruct((SSM_CONV, width), F32),
                   jax.ShapeDtypeStruct((1, width), F32)),
        grid=(width // LANES,),
        in_specs=[pl.BlockSpec((rows, LANES), lambda j: (0, j + zoff)),
                  pl.BlockSpec((rows, LANES), lambda j: (0, j)),
                  pl.BlockSpec((SSM_CONV, LANES), lambda j: (0, j + woff)),
                  pl.BlockSpec((1, LANES), lambda j: (0, j + woff))],
        out_specs=(pl.BlockSpec((rows, LANES), lambda j: (0, j)),
                   pl.BlockSpec((SSM_CONV, LANES), lambda j: (0, j)),
                   pl.BlockSpec((1, LANES), lambda j: (0, j))),
        compiler_params=_cparams(("parallel",)),
    )(zx, dout, cw, cb)


def _ffn_conv_fwd(name, up, cw, cb, steps=()):
    rows = up.shape[0]
    nt = D_FF // LANES

    def body(g_ref, v_ref, wg_ref, wv_ref, bg_ref, bv_ref, o_ref):
        g, v = g_ref[...], v_ref[...]
        ug, uv = bg_ref[...], bv_ref[...]
        for s in range(FFN_CONV):
            ug = ug + wg_ref[pl.ds(FFN_CONV - 1 - s, 1), :] * _shift_down(g, s, rows)
            uv = uv + wv_ref[pl.ds(FFN_CONV - 1 - s, 1), :] * _shift_down(v, s, rows)
        valid = lax.broadcasted_iota(jnp.int32, (rows, 1), 0) >= PAD_ROWS
        o_ref[...] = jnp.where(valid, ug * _sigmoid(ug) * uv, 0.0).astype(BF16)

    col = lambda shift: pl.BlockSpec((rows, LANES), lambda j: (0, j + shift))
    wsp = lambda shift: pl.BlockSpec((FFN_CONV, LANES), lambda j: (0, j + shift))
    bsp = lambda shift: pl.BlockSpec((1, LANES), lambda j: (0, j + shift))
    return _call(
        body, name=name, out_shape=jax.ShapeDtypeStruct((rows, D_FF), BF16), grid=(nt,),
        in_specs=[col(0), col(nt), wsp(0), wsp(nt), bsp(0), bsp(nt)],
        out_specs=pl.BlockSpec((rows, LANES), lambda j: (0, j)), operands=[up, up, cw, cw, cb, cb],
        semantics=("parallel",), steps=steps)


def _ffn_conv_bwd(name, up, dact, cw, cb, steps=()):
    rows = up.shape[0]
    nt = D_FF // LANES

    def body(g_ref, v_ref, d_ref, wg_ref, wv_ref, bg_ref, bv_ref, dxg_ref, dxv_ref, dwg_ref, dwv_ref, dbg_ref, dbv_ref):
        g, v = g_ref[...], v_ref[...]
        gs = [_shift_down(g, s, rows) for s in range(FFN_CONV)]
        vs = [_shift_down(v, s, rows) for s in range(FFN_CONV)]
        ug, uv = bg_ref[...], bv_ref[...]
        for s in range(FFN_CONV):
            ug = ug + wg_ref[pl.ds(FFN_CONV - 1 - s, 1), :] * gs[s]
            uv = uv + wv_ref[pl.ds(FFN_CONV - 1 - s, 1), :] * vs[s]
        sig = _sigmoid(ug)
        valid = lax.broadcasted_iota(jnp.int32, (rows, 1), 0) >= PAD_ROWS
        d = jnp.where(valid, d_ref[...], 0.0)
        dsig = d * sig
        for dpre, src, w_ref, dx_ref, dw_ref, db_ref in (
                (dsig * uv * (1.0 + ug * (1.0 - sig)), gs, wg_ref, dxg_ref, dwg_ref, dbg_ref),
                (dsig * ug, vs, wv_ref, dxv_ref, dwv_ref, dbv_ref)):
            dx = w_ref[pl.ds(FFN_CONV - 1, 1), :] * dpre
            for s in range(1, FFN_CONV):
                dx = dx + w_ref[pl.ds(FFN_CONV - 1 - s, 1), :] * _shift_up(dpre, s, rows)
            dx_ref[...] = dx.astype(BF16)
            for s in range(FFN_CONV):
                dw_ref[pl.ds(FFN_CONV - 1 - s, 1), :] = jnp.sum(dpre * src[s], axis=0, keepdims=True)
            db_ref[...] = jnp.sum(dpre, axis=0, keepdims=True)

    col = lambda shift: pl.BlockSpec((rows, LANES), lambda j: (0, j + shift))
    wsp = lambda shift: pl.BlockSpec((FFN_CONV, LANES), lambda j: (0, j + shift))
    bsp = lambda shift: pl.BlockSpec((1, LANES), lambda j: (0, j + shift))
    dx_shape = jax.ShapeDtypeStruct((rows, D_FF), BF16)
    dw_shape = jax.ShapeDtypeStruct((FFN_CONV, D_FF), F32)
    db_shape = jax.ShapeDtypeStruct((1, D_FF), F32)
    return _call(
        body, name=name, out_shape=(dx_shape, dx_shape, dw_shape, dw_shape, db_shape, db_shape), grid=(nt,),
        in_specs=[col(0), col(nt), col(0), wsp(0), wsp(nt), bsp(0), bsp(nt)],
        out_specs=(col(0), col(0), wsp(0), wsp(0), bsp(0), bsp(0)),
        operands=[up, up, dact, cw, cw, cb, cb], semantics=("parallel",), steps=steps)


def _dt_fwd(name, dtr, bias):
    rows = dtr.shape[0]
    tm = _row_tile(rows, LANES)

    def body(d_ref, b_ref, o_ref):
        v = d_ref[...] + b_ref[...]
        sp = jnp.maximum(v, 0.0) + jnp.log1p(jnp.exp(-jnp.abs(v)))
        lane = lax.broadcasted_iota(jnp.int32, (tm, LANES), 1)
        ok = _rows_mask(pl.program_id(0), tm) & (lane < SSM_HEADS)
        o_ref[...] = jnp.where(ok, sp, 0.0)

    return pl.pallas_call(
        body, name=name, out_shape=jax.ShapeDtypeStruct((rows, LANES), F32), grid=(rows // tm,),
        in_specs=[pl.BlockSpec((tm, LANES), lambda i: (i, 0)), pl.BlockSpec((1, LANES), lambda i: (0, 0))],
        out_specs=pl.BlockSpec((tm, LANES), lambda i: (i, 0)), compiler_params=_cparams(("parallel",)),
    )(dtr, bias)


def _dt_bwd(name, ddt, dtr, bias):
    rows = dtr.shape[0]
    tm = _row_tile(rows, LANES)

    def body(g_ref, d_ref, b_ref, o_ref, db_ref):
        i = pl.program_id(0)
        lane = lax.broadcasted_iota(jnp.int32, (tm, LANES), 1)
        ok = _rows_mask(i, tm) & (lane < SSM_HEADS)
        dv = jnp.where(ok, g_ref[...] * _sigmoid(d_ref[...] + b_ref[...]), 0.0)
        o_ref[...] = dv.astype(BF16)

        @pl.when(i == 0)
        def _():
            db_ref[...] = jnp.zeros_like(db_ref)

        db_ref[...] += jnp.sum(dv, axis=0, keepdims=True)

    row_spec = pl.BlockSpec((tm, LANES), lambda i: (i, 0))
    vec_spec = pl.BlockSpec((1, LANES), lambda i: (0, 0))
    return pl.pallas_call(
        body, name=name,
        out_shape=(jax.ShapeDtypeStruct((rows, LANES), BF16), jax.ShapeDtypeStruct((1, LANES), F32)),
        grid=(rows // tm,), in_specs=[row_spec, row_spec, vec_spec], out_specs=(row_spec, vec_spec),
        compiler_params=_cparams(("arbitrary",)),
    )(ddt, dtr, bias)


def _gate_fwd(name, y, zx, w, steps=()):
    rows = y.shape[0]
    tm = _row_tile(rows, D_INNER)

    def body(y_ref, z_ref, w_ref, o_ref):
        z = z_ref[...]
        g = y_ref[...] * (z * _sigmoid(z))
        r = lax.rsqrt(jnp.mean(g * g, axis=-1, keepdims=True) + RMS_EPS)
        o_ref[...] = (g * r * w_ref[...]).astype(BF16)

    row_spec = pl.BlockSpec((tm, D_INNER), lambda i: (i, 0))
    return _call(
        body, name=name, out_shape=jax.ShapeDtypeStruct((rows, D_INNER), BF16), grid=(rows // tm,),
        in_specs=[row_spec, row_spec, pl.BlockSpec((1, D_INNER), lambda i: (0, 0))],
        out_specs=row_spec, operands=[y, zx, w], semantics=("parallel",), steps=steps)


def _gate_bwd(name, dyn, y, zx, w):
    rows = y.shape[0]
    tm = _row_tile(rows, D_INNER)

    def body(d_ref, y_ref, z_ref, w_ref, dy_ref, dz_ref, dw_ref):
        i = pl.program_id(0)
        z, yv = z_ref[...], y_ref[...]
        sig = _sigmoid(z)
        sz = z * sig
        g = yv * sz
        r = lax.rsqrt(jnp.mean(g * g, axis=-1, keepdims=True) + RMS_EPS)
        ghat = g * r
        dn = d_ref[...]
        dghat = dn * w_ref[...]
        dg = r * (dghat - ghat * jnp.mean(dghat * ghat, axis=-1, keepdims=True))
        dy_ref[...] = dg * sz
        dz_ref[...] = (dg * yv * sig * (1.0 + z * (1.0 - sig))).astype(BF16)

        @pl.when(i == 0)
        def _():
            dw_ref[...] = jnp.zeros_like(dw_ref)

        dw_ref[...] += jnp.sum(dn * ghat, axis=0, keepdims=True)

    row_spec = pl.BlockSpec((tm, D_INNER), lambda i: (i, 0))
    vec_spec = pl.BlockSpec((1, D_INNER), lambda i: (0, 0))
    return pl.pallas_call(
        body, name=name,
        out_shape=(jax.ShapeDtypeStruct((rows, D_INNER), F32), jax.ShapeDtypeStruct((rows, D_INNER), BF16),
                   jax.ShapeDtypeStruct((1, D_INNER), F32)),
        grid=(rows // tm,), in_specs=[row_spec, row_spec, row_spec, vec_spec],
        out_specs=(row_spec, row_spec, vec_spec), compiler_params=_cparams(("arbitrary",)),
    )(dyn, y, zx, w)


def _split3(x):
    hi = x.astype(BF16)
    r1 = x - hi.astype(F32)
    mid = r1.astype(BF16)
    lo = (r1 - mid.astype(F32)).astype(BF16)
    return hi, mid, lo


def _dot3_data_lhs(x, sel):
    sel16 = sel.astype(F32).astype(BF16)
    hi, mid, lo = _split3(x)
    return _dot(hi, sel16) + _dot(mid, sel16) + _dot(lo, sel16)


def _dot3_data_rhs(sel, x):
    sel16 = sel.astype(F32).astype(BF16)
    hi, mid, lo = _split3(x)
    return _dot(sel16, hi) + _dot(sel16, mid) + _dot(sel16, lo)


def _causal_masks():
    r = lax.broadcasted_iota(jnp.int32, (CHUNK, CHUNK), 0)
    c = lax.broadcasted_iota(jnp.int32, (CHUNK, CHUNK), 1)
    return r >= c, r <= c


def _expand_heads_matrix():
    k = lax.broadcasted_iota(jnp.int32, (LANES, GROUP_W), 0)
    j = lax.broadcasted_iota(jnp.int32, (LANES, GROUP_W), 1)
    return jnp.right_shift(j, 6) == k


def _reduce_heads_matrix():
    j = lax.broadcasted_iota(jnp.int32, (GROUP_W, LANES), 0)
    k = lax.broadcasted_iota(jnp.int32, (GROUP_W, LANES), 1)
    return jnp.right_shift(j, 6) == k


def _reduce_pair_matrix(p):
    j = lax.broadcasted_iota(jnp.int32, (LANES, LANES), 0)
    k = lax.broadcasted_iota(jnp.int32, (LANES, LANES), 1)
    return (2 * p + jnp.right_shift(j, 6)) == k


def _group_cols(ref, g, width):
    return ref.at[:, pl.ds(g * width, width)]


def _ssd_prep(name, dt4, a128, steps=()):
    rows = dt4.shape[1]
    nc = rows // CHUNK

    def body(dt_ref, a_ref, dte_ref, acs_ref, acst_ref):
        causal, _ = _causal_masks()
        expand = _expand_heads_matrix()
        for g in range(N_GROUPS):
            dt = dt_ref[g]
            acs = _dot3_data_rhs(causal, dt) * a_ref[g]
            _group_cols(dte_ref, g, GROUP_W)[...] = _dot3_data_lhs(dt, expand)
            _group_cols(acs_ref, g, GROUP_W)[...] = _dot3_data_lhs(acs, expand)
            acst_ref[pl.ds(g * HEADS_PER_GROUP, HEADS_PER_GROUP), :] = acs.T[0:HEADS_PER_GROUP]

    blk = pl.BlockSpec((CHUNK, D_INNER), lambda c: (c, 0))
    shp = jax.ShapeDtypeStruct((rows, D_INNER), F32)
    return _call(
        body, name=name, out_shape=(shp, shp, jax.ShapeDtypeStruct((nc, SSM_HEADS, CHUNK), F32)), grid=(nc,),
        in_specs=[pl.BlockSpec((N_GROUPS, CHUNK, LANES), lambda c: (0, c, 0)),
                  pl.BlockSpec((N_GROUPS, 1, LANES), lambda c: (0, 0, 0))],
        out_specs=(blk, blk, pl.BlockSpec((None, SSM_HEADS, CHUNK), lambda c: (c, 0, 0))),
        operands=[dt4, a128], semantics=("parallel",), steps=steps)


def _ssd_common(x_ref, b_ref, c_ref, dte_ref, acs_ref):
    x = x_ref[...]
    dt_exp = dte_ref[...]
    acs_exp = acs_ref[...]
    tot_exp = acs_ref[pl.ds(CHUNK - 1, 1), :]
    xdt = x * dt_exp
    e_exp = jnp.exp(acs_exp)
    f_exp = jnp.exp(tot_exp - acs_exp)
    return _causal_masks(), x, dt_exp, acs_exp, tot_exp, xdt, e_exp, f_exp, b_ref[...], c_ref[...]


def _pair_decay(acs_pair, acs_row, e, causal):
    lane = lax.broadcasted_iota(jnp.int32, (CHUNK, LANES), 1)
    mine = (lane < HEAD_DIM) if e == 0 else (lane >= HEAD_DIM)
    a_l = jnp.where(mine, acs_pair, pltpu.roll(acs_pair, HEAD_DIM, 1))
    seg = a_l - acs_row
    dm = jnp.where(causal[0], jnp.exp(jnp.minimum(seg, 0.0)), 0.0)
    dmt = jnp.where(causal[1], jnp.exp(jnp.minimum(-seg, 0.0)), 0.0)
    return dm, dmt


def _ssd_specs(index_of_chunk):
    wide = pl.BlockSpec((CHUNK, D_INNER), lambda c: (index_of_chunk(c), 0))
    b_spec = pl.BlockSpec((CHUNK, D_BC), lambda c: (index_of_chunk(c), D_INNER // D_BC))
    c_spec = pl.BlockSpec((CHUNK, D_BC), lambda c: (index_of_chunk(c), D_INNER // D_BC + 1))
    rows_spec = pl.BlockSpec((None, SSM_HEADS, CHUNK), lambda c: (index_of_chunk(c), 0, 0))
    state_spec = pl.BlockSpec((N_GROUPS, None, D_STATE, GROUP_W), lambda c: (0, index_of_chunk(c), 0, 0))
    return wide, b_spec, c_spec, rows_spec, state_spec


def _ssd_fwd(name, xbc, dt_exp, acs_exp, acs_rows, dskexp, steps=()):
    rows = xbc.shape[0]
    nc = rows // CHUNK

    def body(x_ref, b_ref, c_ref, dte_ref, acs_ref, acst_ref, dsk_ref, y_ref, st_ref, s_scr):
        @pl.when(pl.program_id(0) == 0)
        def _():
            s_scr[...] = jnp.zeros_like(s_scr)

        lane = lax.broadcasted_iota(jnp.int32, (CHUNK, LANES), 1)
        for g in range(N_GROUPS):
            y_g = _group_cols(y_ref, g, GROUP_W)
            causal, x, _, acs_exp_v, tot_exp, xdt, e_exp, f_exp, bm, cm = _ssd_common(
                _group_cols(x_ref, g, GROUP_W), _group_cols(b_ref, g, D_STATE), _group_cols(c_ref, g, D_STATE),
                _group_cols(dte_ref, g, GROUP_W), _group_cols(acs_ref, g, GROUP_W))
            state = s_scr[g]
            st_ref[g] = state
            cb16, bb16 = cm.astype(BF16), bm.astype(BF16)
            cb = _dot_nt(cb16, bb16)
            base = e_exp * _dot(cb16, state.astype(BF16)) + _group_cols(dsk_ref, g, GROUP_W)[...] * x
            for p in range(HEADS_PER_GROUP // 2):
                sl = slice(p * LANES, (p + 1) * LANES)
                xp = xdt[:, sl].astype(BF16)
                yd = []
                for e in range(2):
                    acs_row = acst_ref[pl.ds(g * HEADS_PER_GROUP + 2 * p + e, 1), :]
                    dm, _ = _pair_decay(acs_exp_v[:, sl], acs_row, e, causal)
                    yd.append(_dot((cb * dm).astype(BF16), xp))
                y_g[:, sl] = jnp.where(lane < HEAD_DIM, yd[0], yd[1]) + base[:, sl]
            s_scr[g] = jnp.exp(tot_exp) * state + _dot_tn(bb16, (f_exp * xdt).astype(BF16))

    wide, b_spec, c_spec, rows_spec, state_spec = _ssd_specs(lambda c: c)
    return _call(
        body, name=name,
        out_shape=(jax.ShapeDtypeStruct((rows, D_INNER), F32),
                   jax.ShapeDtypeStruct((N_GROUPS, nc, D_STATE, GROUP_W), F32)),
        grid=(nc,),
        in_specs=[wide, b_spec, c_spec, wide, wide, rows_spec, pl.BlockSpec((1, D_INNER), lambda c: (0, 0))],
        out_specs=(wide, state_spec),
        scratch_shapes=[pltpu.VMEM((N_GROUPS, D_STATE, GROUP_W), F32)],
        operands=[xbc, xbc, xbc, dt_exp, acs_exp, acs_rows, dskexp], semantics=("arbitrary",), steps=steps)


def _ssd_bwd(name, xbc, dt_exp, acs_exp, acs_rows, dt4, a128, dskexp, dy, states, steps=()):
    rows = xbc.shape[0]
    nc = rows // CHUNK
    last = nc - 1

    def body(x_ref, b_ref, c_ref, dte_ref, acs_ref, acst_ref, dt_all, a128_all, dsk_all, dy_all, st_all,
             dx_all, db_all, dc_all, ddt_all, dalog_all, ddsk_all, ds_all):
        @pl.when(pl.program_id(0) == 0)
        def _():
            ds_all[...] = jnp.zeros_like(ds_all)
            dalog_all[...] = jnp.zeros_like(dalog_all)
            ddsk_all[...] = jnp.zeros_like(ddsk_all)

        for g in range(N_GROUPS):
            group(g, _group_cols(x_ref, g, GROUP_W), _group_cols(b_ref, g, D_STATE), _group_cols(c_ref, g, D_STATE),
                  _group_cols(dte_ref, g, GROUP_W), _group_cols(acs_ref, g, GROUP_W), acst_ref, dt_all.at[g],
                  a128_all.at[g], _group_cols(dsk_all, g, GROUP_W), _group_cols(dy_all, g, GROUP_W), st_all.at[g],
                  _group_cols(dx_all, g, GROUP_W), _group_cols(db_all, g, D_STATE), _group_cols(dc_all, g, D_STATE),
                  ddt_all.at[g], dalog_all.at[g], ddsk_all.at[g], ds_all.at[g])

    def group(g, x_ref, b_ref, c_ref, dte_ref, acs_ref, acst_ref, dt_ref, a128_ref, dsk_ref, dy_ref, st_ref,
              dx_ref, db_ref, dc_ref, ddt_ref, dalog_ref, ddsk_ref, ds_scr):
        causal, x, dt_exp, acs_exp_v, tot_exp, xdt, e_exp, f_exp, bm, cm = _ssd_common(
            x_ref, b_ref, c_ref, dte_ref, acs_ref)
        dt = dt_ref[...]
        reduce_heads = _reduce_heads_matrix()
        state, dstate = st_ref[...], ds_scr[...]
        dyv = dy_ref[...]
        cb16, bb16 = cm.astype(BF16), bm.astype(BF16)
        s16, ds16 = state.astype(BF16), dstate.astype(BF16)
        cb = _dot_nt(cb16, bb16)
        cbt = _dot_nt(bb16, cb16)
        cs = _dot(cb16, s16)
        bds = _dot(bb16, ds16)
        edy = e_exp * dyv
        fx = f_exp * xdt
        dxdt_base = f_exp * bds
        dc_acc = _dot_nt(edy.astype(BF16), s16)
        db_acc = _dot_nt(fx.astype(BF16), ds16)
        ds_scr[...] = jnp.exp(tot_exp) * dstate + _dot_tn(cb16, edy.astype(BF16))
        q = fx * bds
        dacs = _dot3_data_lhs(edy * cs - q, reduce_heads)
        dtot = jnp.sum(_dot3_data_lhs(q + jnp.exp(tot_exp) * dstate * state, reduce_heads), axis=0, keepdims=True)
        ddsk_ref[...] += jnp.sum(_dot3_data_lhs(dyv * x, reduce_heads), axis=0, keepdims=True)
        lane = lax.broadcasted_iota(jnp.int32, (CHUNK, LANES), 1)
        dcb = jnp.zeros((CHUNK, CHUNK), F32)
        dcbt = jnp.zeros((CHUNK, CHUNK), F32)
        ddt_x = jnp.zeros((CHUNK, LANES), F32)
        for p in range(HEADS_PER_GROUP // 2):
            sl = slice(p * LANES, (p + 1) * LANES)
            xp, dyp = xdt[:, sl], dyv[:, sl]
            xp16, dyp16 = xp.astype(BF16), dyp.astype(BF16)
            dxh = []
            for e in range(2):
                h = 2 * p + e
                mine = (lane < HEAD_DIM) if e == 0 else (lane >= HEAD_DIM)
                acs_row = acst_ref[pl.ds(g * HEADS_PER_GROUP + h, 1), :]
                dm, dmt = _pair_decay(acs_exp_v[:, sl], acs_row, e, causal)
                m, mt = cb * dm, cbt * dmt
                xh16 = jnp.where(mine, xp, 0.0).astype(BF16)
                dyh16 = jnp.where(mine, dyp, 0.0).astype(BF16)
                d_m = _dot_nt(dyh16, xp16)
                d_mt = _dot_nt(xh16, dyp16)
                dacs_h = (jnp.sum(d_m * m, axis=-1, keepdims=True)
                          - jnp.sum(d_mt * mt, axis=-1, keepdims=True))
                dacs = dacs + jnp.where(lane == h, dacs_h, 0.0)
                dcb = dcb + d_m * dm
                dcbt = dcbt + d_mt * dmt
                dxh.append(_dot(mt.astype(BF16), dyp16))
            dxdt = jnp.where(lane < HEAD_DIM, dxh[0], dxh[1]) + dxdt_base[:, sl]
            dx_ref[:, sl] = dxdt * dt_exp[:, sl] + dsk_ref[:, sl] * dyp
            ddt_x = ddt_x + _dot3_data_lhs(dxdt * x[:, sl], _reduce_pair_matrix(p))
        dc_ref[...] = dc_acc + _dot(dcb.astype(BF16), bb16)
        db_ref[...] = db_acc + _dot(dcbt.astype(BF16), cb16)
        row = lax.broadcasted_iota(jnp.int32, (CHUNK, LANES), 0)
        dacs = dacs + jnp.where(row == CHUNK - 1, dtot, 0.0)
        da = _dot3_data_rhs(causal[1], dacs)
        ddt_ref[...] = da * a128_ref[...] + ddt_x
        dalog_ref[...] += jnp.sum(da * dt, axis=0, keepdims=True) * a128_ref[...]

    wide, b_spec, c_spec, rows_spec, state_spec = _ssd_specs(lambda c: last - c)
    heads_spec = pl.BlockSpec((N_GROUPS, CHUNK, LANES), lambda c: (0, last - c, 0))
    vec_spec = pl.BlockSpec((N_GROUPS, 1, LANES), lambda c: (0, 0, 0))
    bc_out = pl.BlockSpec((CHUNK, D_BC), lambda c: (last - c, 0))
    vec_shape = jax.ShapeDtypeStruct((N_GROUPS, 1, LANES), F32)
    return _call(
        body, name=name,
        out_shape=(jax.ShapeDtypeStruct((rows, D_INNER), F32), jax.ShapeDtypeStruct((rows, D_BC), F32),
                   jax.ShapeDtypeStruct((rows, D_BC), F32), jax.ShapeDtypeStruct((N_GROUPS, rows, LANES), F32),
                   vec_shape, vec_shape),
        grid=(nc,),
        in_specs=[wide, b_spec, c_spec, wide, wide, rows_spec, heads_spec, vec_spec,
                  pl.BlockSpec((1, D_INNER), lambda c: (0, 0)), wide, state_spec],
        out_specs=(wide, bc_out, bc_out, heads_spec, vec_spec, vec_spec),
        scratch_shapes=[pltpu.VMEM((N_GROUPS, D_STATE, GROUP_W), F32)],
        operands=[xbc, xbc, xbc, dt_exp, acs_exp, acs_rows, dt4, a128, dskexp, dy, states],
        semantics=("arbitrary",), steps=steps)


def _attn_visible(b, heads=1):
    row = jnp.bitwise_and(lax.broadcasted_iota(jnp.int32, (heads * CHUNK, 3 * CHUNK), 0), CHUNK - 1)
    col = lax.broadcasted_iota(jnp.int32, (heads * CHUNK, 3 * CHUNK), 1)
    bb = b + jnp.zeros_like(col)
    meta = (col < CHUNK) & (bb >= 1) & (col >= PAD_ROWS)
    prev = (col >= CHUNK) & (col < 2 * CHUNK) & (bb >= 2) & ((col - CHUNK) > row)
    cur = (col >= 2 * CHUNK) & ((col - 2 * CHUNK) <= row) & ((bb >= 1) | ((col - 2 * CHUNK) >= PAD_ROWS))
    return meta | prev | cur


def _attn_visible4(b):
    return _attn_visible(b, 4)


def _stack_heads(q_ref, sink_ref, kvh, scale):
    lane = lax.broadcasted_iota(jnp.int32, (CHUNK, LANES), 1)
    parts, sinks = [], []
    for pp in range(2):
        pair = kvh * 2 + pp
        qp = q_ref[:, pair * LANES:(pair + 1) * LANES] * scale
        for e in range(2):
            mine = (lane < HEAD_DIM) if e == 0 else (lane >= HEAD_DIM)
            parts.append(jnp.where(mine, qp, 0.0).astype(BF16))
            sinks.append(jnp.full((CHUNK, 1), sink_ref[2 * pair + e], F32))
    return jnp.concatenate(parts, axis=0), jnp.concatenate(sinks, axis=0)


def _attn_operands(q_ref, k0, kp, kc, v0, vp, vc, sink_ref):
    kcat, vcat, q4, sink4 = [], [], [], []
    for kvh in range(N_KV_HEADS):
        ksl = slice(kvh * LANES, (kvh + 1) * LANES)
        kcat.append(jnp.concatenate([k0[:, ksl], kp[:, ksl], kc[:, ksl]], axis=0).astype(BF16))
        vcat.append(jnp.concatenate([v0[:, ksl], vp[:, ksl], vc[:, ksl]], axis=0).astype(BF16))
        stacked, sinks = _stack_heads(q_ref, sink_ref, kvh, ATTN_SCALE)
        q4.append(stacked)
        sink4.append(sinks)
    return kcat, vcat, q4, sink4


def _attn_probs(q4, kcat, visible, sink4):
    heads = range(N_KV_HEADS)
    s = [jnp.where(visible, _dot_nt(q4[h], kcat[h]), NEG_INF) for h in heads]
    m = [jnp.maximum(jnp.max(s[h], axis=-1, keepdims=True), sink4[h]) for h in heads]
    pe = [jnp.exp(s[h] - m[h]) for h in heads]
    pe_sink = [jnp.exp(sink4[h] - m[h]) for h in heads]
    inv = [1.0 / (jnp.sum(pe[h], axis=-1, keepdims=True) + pe_sink[h]) for h in heads]
    return [pe[h] * inv[h] for h in heads], [pe_sink[h] * inv[h] for h in heads]


def _unstack_pairs(stacked, pp):
    lane = lax.broadcasted_iota(jnp.int32, (CHUNK, LANES), 1)
    return jnp.where(lane < HEAD_DIM, stacked[(2 * pp) * CHUNK:(2 * pp + 1) * CHUNK],
                     stacked[(2 * pp + 1) * CHUNK:(2 * pp + 2) * CHUNK])


def _attn_specs(colblock):
    blk = lambda f: pl.BlockSpec((CHUNK, 2 * D_KV), f)
    return [blk(lambda b: (0, colblock)), blk(lambda b: (jnp.maximum(b - 1, 0), colblock)), blk(lambda b: (b, colblock))]


def _attn_fwd(name, q, kv2, sinks, steps=()):
    rows = q.shape[0]

    def body(q_ref, k0, kp, kc, v0, vp, vc, sink_ref, o_ref):
        visible = _attn_visible4(pl.program_id(0))
        kcat, vcat, q4, sink4 = _attn_operands(q_ref, k0, kp, kc, v0, vp, vc, sink_ref)
        pn, _ = _attn_probs(q4, kcat, visible, sink4)
        o4 = [_dot(pn[h].astype(BF16), vcat[h]) for h in range(N_KV_HEADS)]
        for kvh in range(N_KV_HEADS):
            for pp in range(2):
                qsl = slice((kvh * 2 + pp) * LANES, (kvh * 2 + pp + 1) * LANES)
                o_ref[:, qsl] = _unstack_pairs(o4[kvh], pp).astype(BF16)

    return _call(
        body, name=name, out_shape=jax.ShapeDtypeStruct((rows, D_MODEL), BF16), grid=(rows // CHUNK,),
        in_specs=[pl.BlockSpec((CHUNK, D_MODEL), lambda b: (b, 0))] + _attn_specs(0) + _attn_specs(1)
        + [pl.BlockSpec(memory_space=pltpu.SMEM)],
        out_specs=pl.BlockSpec((CHUNK, D_MODEL), lambda b: (b, 0)),
        operands=[q, kv2, kv2, kv2, kv2, kv2, kv2, sinks], semantics=("parallel",), steps=steps)


def _attn_bwd(name, q, kv2, sinks, do, steps=()):
    rows = q.shape[0]

    def body(q_ref, k0, kp, kc, v0, vp, vc, sink_ref, do_ref,
             dq_ref, dkc_ref, dkp_ref, dvc_ref, dvp_ref, dkm_ref, dvm_ref, dsink_ref):
        @pl.when(pl.program_id(0) == 0)
        def _():
            dkm_ref[...] = jnp.zeros_like(dkm_ref)
            dvm_ref[...] = jnp.zeros_like(dvm_ref)
            dsink_ref[...] = jnp.zeros_like(dsink_ref)

        visible = _attn_visible4(pl.program_id(0))
        heads = range(N_KV_HEADS)
        lane1 = lax.broadcasted_iota(jnp.int32, (1, LANES), 1)
        kcat, vcat, q4, sink4 = _attn_operands(q_ref, k0, kp, kc, v0, vp, vc, sink_ref)
        do4 = [_stack_heads(do_ref, sink_ref, h, 1.0)[0] for h in heads]
        pn, psink = _attn_probs(q4, kcat, visible, sink4)
        dp = [_dot_nt(do4[h], vcat[h]) for h in heads]
        delta = [jnp.sum(pn[h] * dp[h], axis=-1, keepdims=True) for h in heads]
        ds16 = [(pn[h] * (dp[h] - delta[h])).astype(BF16) for h in heads]
        dq4 = [_dot(ds16[h], kcat[h]) for h in heads]
        dk_acc = [_dot_tn(ds16[h], q4[h]) for h in heads]
        dv_acc = [_dot_tn(pn[h].astype(BF16), do4[h]) for h in heads]
        dsink = jnp.zeros((1, LANES), F32)
        for kvh in heads:
            ksl = slice(kvh * LANES, (kvh + 1) * LANES)
            sink_terms = psink[kvh] * delta[kvh]
            for j in range(4):
                part = jnp.sum(sink_terms[j * CHUNK:(j + 1) * CHUNK], axis=0, keepdims=True)
                dsink = dsink - jnp.where(lane1 == kvh * 4 + j, part, 0.0)
            for pp in range(2):
                qsl = slice((kvh * 2 + pp) * LANES, (kvh * 2 + pp + 1) * LANES)
                dq_ref[:, qsl] = (_unstack_pairs(dq4[kvh], pp) * ATTN_SCALE).astype(BF16)
            dkm_ref[:, ksl] += dk_acc[kvh][0:CHUNK]
            dvm_ref[:, ksl] += dv_acc[kvh][0:CHUNK]
            dkp_ref[:, ksl] = dk_acc[kvh][CHUNK:2 * CHUNK]
            dvp_ref[:, ksl] = dv_acc[kvh][CHUNK:2 * CHUNK]
            dkc_ref[:, ksl] = dk_acc[kvh][2 * CHUNK:3 * CHUNK]
            dvc_ref[:, ksl] = dv_acc[kvh][2 * CHUNK:3 * CHUNK]
        dsink_ref[...] += dsink

    qspec = pl.BlockSpec((CHUNK, D_MODEL), lambda b: (b, 0))
    kvspec = pl.BlockSpec((CHUNK, 2 * D_KV), lambda b: (b, 0))
    fixed = pl.BlockSpec((CHUNK, 2 * D_KV), lambda b: (0, 0))
    kv_shape = jax.ShapeDtypeStruct((rows, 2 * D_KV), F32)
    meta_shape = jax.ShapeDtypeStruct((CHUNK, 2 * D_KV), F32)
    return _call(
        body, name=name,
        out_shape=(jax.ShapeDtypeStruct((rows, D_MODEL), BF16), kv_shape, kv_shape, kv_shape, kv_shape,
                   meta_shape, meta_shape, jax.ShapeDtypeStruct((1, LANES), F32)),
        grid=(rows // CHUNK,),
        in_specs=[qspec] + _attn_specs(0) + _attn_specs(1) + [pl.BlockSpec(memory_space=pltpu.SMEM), qspec],
        out_specs=(qspec, kvspec, kvspec, kvspec, kvspec, fixed, fixed, pl.BlockSpec((1, LANES), lambda b: (0, 0))),
        operands=[q, kv2, kv2, kv2, kv2, kv2, kv2, sinks, do], semantics=("arbitrary",), steps=steps)


def _kv_grad_combine(name, dk_cur, dk_prev, dk_meta, dv_cur, dv_prev, dv_meta):
    rows = dk_cur.shape[0]
    nb = rows // CHUNK
    width = 2 * D_KV

    def body(kc_ref, kp_ref, km_ref, vc_ref, vp_ref, vm_ref, o_ref):
        jj = pl.program_id(0) + jnp.zeros((CHUNK, 1), jnp.int32)
        for half, (c_ref, p_ref, m_ref) in enumerate(((kc_ref, kp_ref, km_ref), (vc_ref, vp_ref, vm_ref))):
            total = c_ref[...] + jnp.where(jj < nb - 1, p_ref[...], 0.0) + jnp.where(jj == 0, m_ref[...], 0.0)
            o_ref[:, half * width:(half + 1) * width] = total.astype(BF16)

    blk = lambda f: pl.BlockSpec((CHUNK, width), f)
    three = lambda: [blk(lambda j: (j, 0)), blk(lambda j: (jnp.minimum(j + 1, nb - 1), 0)), blk(lambda j: (0, 0))]
    return pl.pallas_call(
        body, name=name, out_shape=jax.ShapeDtypeStruct((rows, 2 * width), BF16), grid=(nb,),
        in_specs=three() + three(), out_specs=pl.BlockSpec((CHUNK, 2 * width), lambda j: (j, 0)),
        compiler_params=_cparams(("parallel",)),
    )(dk_cur, dk_prev, dk_meta, dv_cur, dv_prev, dv_meta)


def _loss_head(name, h, target):
    rows = h.shape[0]

    def body(h_ref, t_ref, dh_ref, loss_ref):
        i = pl.program_id(0)
        real = (i + jnp.zeros((CHUNK, 1), jnp.int32)) >= 1
        diff = jnp.where(real, h_ref[...] - t_ref[...], 0.0)
        dh_ref[...] = diff * (1.0 / D_MODEL)

        @pl.when(i == 0)
        def _():
            loss_ref[...] = jnp.zeros_like(loss_ref)

        loss_ref[...] += jnp.sum(diff * diff) * (0.5 / D_MODEL)

    blk = pl.BlockSpec((CHUNK, D_MODEL), lambda i: (i, 0))
    return pl.pallas_call(
        body, name=name,
        out_shape=(jax.ShapeDtypeStruct((rows, D_MODEL), F32), jax.ShapeDtypeStruct((1, LANES), F32)),
        grid=(rows // CHUNK,),
        in_specs=[blk, pl.BlockSpec((CHUNK, D_MODEL), lambda i: (jnp.maximum(i - 1, 0), 0))],
        out_specs=(blk, pl.BlockSpec((1, LANES), lambda i: (0, 0))), compiler_params=_cparams(("arbitrary",)),
    )(h, target)


def _adamw(name, w, g, m, v):
    rows, width = w.shape
    tr = rows
    for cand in range(8, rows + 1, 8):
        if rows % cand == 0 and cand * width * 4 <= (1 << 20):
            tr = cand

    def body(w_ref, g_ref, m_ref, v_ref, d_ref, mo_ref, vo_ref):
        gv = g_ref[...]
        mn = ADAM_B1 * m_ref[...] + (1.0 - ADAM_B1) * gv
        vn = ADAM_B2 * v_ref[...] + (1.0 - ADAM_B2) * (gv * gv)
        m_hat = mn / (1.0 - ADAM_B1 ** ADAM_STEP)
        v_hat = vn / (1.0 - ADAM_B2 ** ADAM_STEP)
        d_ref[...] = -ADAM_LR * (m_hat / (jnp.sqrt(v_hat) + ADAM_EPS) + ADAM_WD * w_ref[...])
        mo_ref[...] = mn
        vo_ref[...] = vn

    blk = pl.BlockSpec((tr, width), lambda i: (i, 0))
    shp = jax.ShapeDtypeStruct((rows, width), F32)
    return pl.pallas_call(
        body, name=name, out_shape=(shp, shp, shp), grid=(rows // tr,), in_specs=[blk] * 4, out_specs=(blk,) * 3,
        compiler_params=_cparams(("parallel",)),
    )(w, g, m, v)


class _GivenWeights:
    def __init__(self, p):
        self.p = p
        self.grads = {}

    def weight(self, name, layer=None):
        return self.p[name] if layer is None else self.p[name][layer]

    def steps(self, kernel):
        return ()

    def grad(self, name, layer, g):
        self.grads[(name, layer)] = g


def _ffn_fwd(tag, h, p, i, plan):
    hn = _rms_fwd(f"ffn{tag}_norm", h, p["f_norm_pre"][i:i + 1])
    up = _mm_nn_bychip(f"ffn{tag}_up", hn, plan.weight("f_w_up", i))
    act = _ffn_conv_fwd(f"ffn{tag}_conv", up, p["f_conv_w"][i], p["f_conv_b"][i:i + 1], steps=plan.steps(f"ffn{tag}_conv"))
    pre = _mm(f"ffn{tag}_down", act, plan.weight("f_w_down", i), "nn")
    h_new = _resid_norm_fwd(f"ffn{tag}_resid", h, pre, p["f_norm_post"][i:i + 1])
    return h_new, (h, hn, up, act, pre)


def _ffn_bwd(tag, dh, saved, p, i, plan):
    h, hn, up, act, pre = saved
    dpre, g_post = _resid_norm_bwd(f"ffn{tag}_resid_bwd", dh, pre, p["f_norm_post"][i:i + 1])
    plan.grad("f_w_down", i, _mm(f"ffn{tag}_down_dw", act, dpre, "tn", out_dtype=BF16))
    dact = _mm(f"ffn{tag}_down_dx", dpre, plan.weight("f_w_down", i), "nt", steps=plan.steps(f"ffn{tag}_down_dx"))
    dug, duv, gwg, gwv, gbg, gbv = _ffn_conv_bwd(f"ffn{tag}_conv_bwd", up, dact, p["f_conv_w"][i], p["f_conv_b"][i:i + 1],
                                                 steps=plan.steps(f"ffn{tag}_conv_bwd"))
    g_cw, g_cb = jnp.concatenate([gwg, gwv], axis=1), jnp.concatenate([gbg, gbv], axis=1)
    w_up = plan.weight("f_w_up", i)
    n = w_up.shape[2]
    dhn = _mm_nt_bychip(f"ffn{tag}_up_dx_gate", dug, w_up, 0)
    dhn = _mm_nt_bychip(f"ffn{tag}_up_dx_val", duv, w_up, N_CHIPS // 2, acc=dhn)
    g_up = _mm_tn_bychip(f"ffn{tag}_up_dw_gate", hn, dug, n, 0)
    plan.grad("f_w_up", i, _mm_tn_bychip(f"ffn{tag}_up_dw_val", hn, duv, n, N_CHIPS // 2, into=g_up))
    dh_new, g_pre = _norm_bwd_add(f"ffn{tag}_norm_bwd", dh, dhn, h, p["f_norm_pre"][i:i + 1])
    return dh_new, dict(f_norm_post=g_post, f_conv_w=g_cw, f_conv_b=g_cb, f_norm_pre=g_pre)


def _lanes_pad(a, width=LANES):
    return jnp.pad(a, [(0, 0)] * (a.ndim - 1) + [(0, width - a.shape[-1])])


def _dup_heads(w):
    rows = w.shape[0]
    w = w.reshape(rows, 2 * N_KV_HEADS, 1, HEAD_DIM)
    return jnp.broadcast_to(w, (rows, 2 * N_KV_HEADS, 2, HEAD_DIM)).reshape(rows, 4 * D_KV)


def _undup_heads(g):
    rows = g.shape[0]
    return g.reshape(rows, 2 * N_KV_HEADS, 2, HEAD_DIM).sum(axis=2).reshape(rows, 2 * D_KV)


def _local_step(x2, target, p, plan):
    seq = x2.shape[0]
    rows = seq + CHUNK
    g = {}

    h0 = jnp.concatenate([jnp.zeros((PAD_ROWS, D_MODEL), F32), p["meta_tokens"], x2], axis=0)

    w_in = plan.weight("a_w_in")
    w_dt = _lanes_pad(w_in[:, D_MAIN:])
    dt_bias = _lanes_pad(p["a_dt_bias"])
    a_neg = -jnp.exp(p["a_a_log"].reshape(N_GROUPS, HEADS_PER_GROUP))
    a128 = _lanes_pad(a_neg.reshape(N_GROUPS, 1, HEADS_PER_GROUP))
    dskexp = jnp.repeat(p["a_d_skip"].reshape(SSM_HEADS), HEAD_DIM).reshape(1, D_INNER)

    hn0 = _rms_fwd("a_norm", h0, p["a_norm_pre"])
    zx = _mm("a_in_main", hn0, w_in, "nn", n_cols=D_MAIN, steps=plan.steps("a_in_main"))
    dtr = _mm("a_in_dt", hn0, w_dt, "nn")
    xbc = _conv4_fwd("a_conv", zx, p["a_conv_w"], p["a_conv_b"], steps=plan.steps("a_conv"))
    dt = _dt_fwd("a_dt", dtr, dt_bias)
    dt4 = _lanes_pad(dt[:, :SSM_HEADS].reshape(rows, N_GROUPS, HEADS_PER_GROUP).transpose(1, 0, 2))
    dt_exp, acs_exp, acs_rows = _ssd_prep("a_ssd_prep", dt4, a128, steps=plan.steps("a_ssd_prep"))
    y, states = _ssd_fwd("a_ssd", xbc, dt_exp, acs_exp, acs_rows, dskexp, steps=plan.steps("a_ssd"))
    yn = _gate_fwd("a_gate", y, zx, p["a_gate_norm"], steps=plan.steps("a_gate"))
    mix = _mm("a_out", yn, plan.weight("a_w_out"), "nn")
    h1 = _resid_norm_fwd("a_resid", h0, mix, p["a_norm_post"])

    h2, ffn0 = _ffn_fwd("0", h1, p, 0, plan)

    hkv = _rms_fwd("kv_norm", h2, p["kv_norm"])
    w_kv2 = _dup_heads(plan.weight("w_kv"))
    kv2 = _mm("kv_proj", hkv, w_kv2, "nn")
    hn2 = _rms_fwd("b_norm", h2, p["b_norm_pre"])
    q = _mm("b_q", hn2, plan.weight("b_w_q"), "nn")
    sinks = p["b_sinks"].reshape(N_Q_HEADS)
    o = _attn_fwd("b_attn", q, kv2, sinks, steps=plan.steps("b_attn"))
    attn = _mm("b_o", o, plan.weight("b_w_o"), "nn", steps=plan.steps("b_o"))
    h3 = _resid_norm_fwd("b_resid", h2, attn, p["b_norm_post"])

    h4, ffn1 = _ffn_fwd("1", h3, p, 1, plan)

    dh, loss_vec = _loss_head("loss", h4, target)
    loss = loss_vec[0, 0]

    dh, g1 = _ffn_bwd("1", dh, ffn1, p, 1, plan)

    dpre, g["b_norm_post"] = _resid_norm_bwd("b_resid_bwd", dh, attn, p["b_norm_post"])
    plan.grad("b_w_o", None, _mm("b_o_dw", o, dpre, "tn", out_dtype=BF16))
    do = _mm("b_o_dx", dpre, plan.weight("b_w_o"), "nt", steps=plan.steps("b_o_dx"))
    dq, dkc, dkp, dvc, dvp, dkm, dvm, dsink = _attn_bwd("b_attn_bwd", q, kv2, sinks, do, steps=plan.steps("b_attn_bwd"))
    g["b_sinks"] = dsink[:, :N_Q_HEADS]
    dhn2 = _mm("b_q_dx", dq, plan.weight("b_w_q"), "nt")
    plan.grad("b_w_q", None, _mm("b_q_dw", hn2, dq, "tn", out_dtype=BF16))
    dh, g["b_norm_pre"] = _norm_bwd_add("b_norm_bwd", dh, dhn2, h2, p["b_norm_pre"])
    dkv2 = _kv_grad_combine("kv_grad", dkc, dkp, dkm, dvc, dvp, dvm)
    dhkv = _mm("kv_proj_dx", dkv2, w_kv2, "nt")
    plan.grad("w_kv", None, _undup_heads(_mm("kv_proj_dw", hkv, dkv2, "tn")))
    dh, g["kv_norm"] = _norm_bwd_add("kv_norm_bwd", dh, dhkv, h2, p["kv_norm"])

    dh, g0 = _ffn_bwd("0", dh, ffn0, p, 0, plan)
    for name in g0:
        if g0[name].shape[0] == 1:
            g[name] = jnp.concatenate([g0[name], g1[name]], axis=0)
        else:
            g[name] = jnp.stack([g0[name], g1[name]])

    dpre, g["a_norm_post"] = _resid_norm_bwd("a_resid_bwd", dh, mix, p["a_norm_post"])
    plan.grad("a_w_out", None, _mm("a_out_dw", yn, dpre, "tn", out_dtype=BF16))
    dyn = _mm("a_out_dx", dpre, plan.weight("a_w_out"), "nt", steps=plan.steps("a_out_dx"))
    dy, dz, g["a_gate_norm"] = _gate_bwd("a_gate_bwd", dyn, y, zx, p["a_gate_norm"])
    dxs, dbm, dcm, ddt4, dalog, ddsk = _ssd_bwd("a_ssd_bwd", xbc, dt_exp, acs_exp, acs_rows, dt4, a128, dskexp, dy, states,
                                               steps=plan.steps("a_ssd_bwd"))
    g["a_a_log"] = dalog[:, 0, :HEADS_PER_GROUP].reshape(1, SSM_HEADS)
    g["a_d_skip"] = ddsk[:, 0, :HEADS_PER_GROUP].reshape(1, SSM_HEADS)
    ddt = _lanes_pad(ddt4[:, :, :HEADS_PER_GROUP].transpose(1, 0, 2).reshape(rows, SSM_HEADS))
    ddtr, dbias = _dt_bwd("a_dt_bwd", ddt, dtr, dt_bias)
    g["a_dt_bias"] = dbias[:, :SSM_HEADS]
    dxp, gw_x, gb_x = _conv4_bwd("a_conv_bwd_x", zx, dxs, p["a_conv_w"], p["a_conv_b"], 0)
    dbp, gw_b, gb_b = _conv4_bwd("a_conv_bwd_b", zx, dbm, p["a_conv_w"], p["a_conv_b"], D_INNER)
    dcp, gw_c, gb_c = _conv4_bwd("a_conv_bwd_c", zx, dcm, p["a_conv_w"], p["a_conv_b"], D_INNER + D_BC)
    g["a_conv_w"] = jnp.concatenate([gw_x, gw_b, gw_c], axis=1)
    g["a_conv_b"] = jnp.concatenate([gb_x, gb_b, gb_c], axis=1)
    dzx = jnp.concatenate([dz, dxp, dbp, dcp], axis=1)
    g_main = _mm("a_in_main_dw", hn0, dzx, "tn", out_dtype=BF16, steps=plan.steps("a_in_main_dw"))
    g_dt = _mm("a_in_dt_dw", hn0, ddtr, "tn", out_dtype=BF16)
    plan.grad("a_w_in", None, jnp.concatenate([g_main, g_dt[:, :SSM_HEADS]], axis=1))
    dhn0 = _mm("a_in_dt_dx", ddtr, w_dt, "nt", steps=plan.steps("a_in_dt_dx"))
    dhn0 = _mm("a_in_main_dx", dzx, w_in, "nt", acc=dhn0, steps=plan.steps("a_in_main_dx"))
    dh, g["a_norm_pre"] = _norm_bwd_add("a_norm_bwd", dh, dhn0, h0, p["a_norm_pre"])

    g["meta_tokens"] = dh[PAD_ROWS:CHUNK]
    return loss, dh[CHUNK:], g


ANY = pl.BlockSpec(memory_space=pl.ANY)
VMEM_SPEC = pl.BlockSpec(memory_space=pltpu.VMEM)


def _allgather_small(name, shard):
    rows = shard.shape[0]

    def body(s_ref, o_ref, send_sems, recv_sems):
        x, y, c = _place()
        me = 2 * x + y
        o_ref[me] = s_ref[...]
        chips = _other_chips(x, y)
        sends = [pltpu.make_async_remote_copy(s_ref, o_ref.at[me], send_sems.at[j], recv_sems.at[j],
                                              device_id=(cx, cy, c), device_id_type=MESH)
                 for j, (cx, cy) in enumerate(chips)]
        for cp in sends:
            cp.start()
        for j, (cx, cy) in enumerate(chips):
            pltpu.make_async_remote_copy(s_ref, o_ref.at[2 * cx + cy], send_sems.at[j], recv_sems.at[j],
                                         device_id=(cx, cy, c), device_id_type=MESH).wait_recv()
        for cp in sends:
            cp.wait_send()

    return pl.pallas_call(
        body, name=name, out_shape=jax.ShapeDtypeStruct((N_CHIPS, rows, LANES), F32),
        in_specs=[VMEM_SPEC], out_specs=VMEM_SPEC,
        scratch_shapes=[pltpu.SemaphoreType.DMA((3,)), pltpu.SemaphoreType.DMA((3,))],
        compiler_params=pltpu.CompilerParams(vmem_limit_bytes=VMEM_LIMIT),
    )(shard)


def _row_block(rows, width, itemsize, align, budget=2 << 20):
    best = None
    for cand in range(align, rows + 1, align):
        if rows % cand == 0 and cand * width * itemsize <= budget:
            best = cand
    assert best is not None, (rows, width)
    return best


def _cast_into_slot(name, chip, w2d):
    rows, width = w2d.shape
    tr = _row_block(rows, width, 4, 16)

    def body(chip_ref, w_ref, o_ref):
        o_ref[...] = w_ref[...].astype(BF16)

    return pl.pallas_call(
        body, name=name, out_shape=jax.ShapeDtypeStruct((N_CHIPS, rows, width), BF16),
        grid_spec=pltpu.PrefetchScalarGridSpec(
            num_scalar_prefetch=1, grid=(rows // tr,),
            in_specs=[pl.BlockSpec((tr, width), lambda i, chip_ref: (i, 0))],
            out_specs=pl.BlockSpec((None, tr, width), lambda i, chip_ref: (chip_ref[0], i, 0))),
        compiler_params=_cparams(("parallel",)),
    )(chip, w2d)


def _allreduce_small(name, vec):
    rows = vec.shape[0]

    def body(v_ref, o_ref, buf, send_sems, recv_sems):
        x, y, c = _place()
        me = 4 * x + 2 * y + c
        buf[me] = v_ref[...]

        def peer(k):
            kx, ky, kc = (k >> 2) & 1, (k >> 1) & 1, k & 1
            return (1 - x if kx else x, 1 - y if ky else y, 1 - c if kc else c)

        sends = []
        for k in range(1, N_DEV):
            cp = pltpu.make_async_remote_copy(v_ref, buf.at[me], send_sems.at[k - 1], recv_sems.at[k - 1],
                                              device_id=peer(k), device_id_type=MESH)
            cp.start()
            sends.append(cp)
        for k in range(1, N_DEV):
            px, py, pc = peer(k)
            pltpu.make_async_remote_copy(v_ref, buf.at[4 * px + 2 * py + pc], send_sems.at[k - 1], recv_sems.at[k - 1],
                                         device_id=(px, py, pc), device_id_type=MESH).wait_recv()
        for cp in sends:
            cp.wait_send()
        acc = buf[0]
        for d in range(1, N_DEV):
            acc = acc + buf[d]
        o_ref[...] = acc

    return pl.pallas_call(
        body, name=name, out_shape=jax.ShapeDtypeStruct((rows, LANES), F32),
        in_specs=[VMEM_SPEC], out_specs=VMEM_SPEC,
        scratch_shapes=[pltpu.VMEM((N_DEV, rows, LANES), F32), pltpu.SemaphoreType.DMA((N_DEV - 1,)),
                        pltpu.SemaphoreType.DMA((N_DEV - 1,))],
        compiler_params=pltpu.CompilerParams(vmem_limit_bytes=VMEM_LIMIT),
    )(vec)


def _rs_pair_add(name, place, grads, partner):
    _, half_rows, width = partner.shape
    tr = _row_block(half_rows, width, 2, 16)
    nb = half_rows // tr

    def body(place_ref, g_ref, p_ref, o_ref):
        o_ref[...] = (g_ref[...].astype(F32) + p_ref[...].astype(F32)).astype(BF16)

    return pl.pallas_call(
        body, name=name, out_shape=jax.ShapeDtypeStruct(partner.shape, BF16),
        grid_spec=pltpu.PrefetchScalarGridSpec(
            num_scalar_prefetch=1, grid=(N_CHIPS, nb),
            in_specs=[pl.BlockSpec((None, tr, width), lambda s, i, pr: (s, pr[1] * nb + i, 0)),
                      pl.BlockSpec((None, tr, width), lambda s, i, pr: (s, i, 0))],
            out_specs=pl.BlockSpec((None, tr, width), lambda s, i, pr: (s, i, 0))),
        compiler_params=_cparams(("parallel", "parallel")),
    )(place, grads, partner)


def _rs_chip_add(name, place, mine, others):
    _, half_rows, width = mine.shape
    tr = _row_block(half_rows, width, 4, 16, budget=1 << 20)
    nb = half_rows // tr

    def body(place_ref, q_ref, r_ref, o_ref):
        acc = q_ref[...].astype(F32)
        for j in range(3):
            acc = acc + r_ref[j].astype(F32)
        o_ref[...] = acc

    return pl.pallas_call(
        body, name=name, out_shape=jax.ShapeDtypeStruct((2 * half_rows, width), F32),
        grid_spec=pltpu.PrefetchScalarGridSpec(
            num_scalar_prefetch=1, grid=(nb,),
            in_specs=[pl.BlockSpec((None, tr, width), lambda i, pr: (pr[0], i, 0)),
                      pl.BlockSpec((3, tr, width), lambda i, pr: (0, i, 0))],
            out_specs=pl.BlockSpec((tr, width), lambda i, pr: (pr[1] * nb + i, 0))),
        compiler_params=_cparams(("parallel",)),
    )(place, mine, others)


WEIGHTS = ["meta_tokens", "a_norm_pre", "a_w_in", "a_conv_w", "a_conv_b", "a_dt_bias", "a_a_log", "a_d_skip",
           "a_gate_norm", "a_w_out", "a_norm_post", "kv_norm", "w_kv", "b_norm_pre", "b_w_q", "b_sinks", "b_w_o",
           "b_norm_post", "f_norm_pre", "f_w_up", "f_conv_w", "f_conv_b", "f_w_down", "f_norm_post"]
FULL_SHAPE = {
    "meta_tokens": (16, 1024), "a_norm_pre": (1, 1024), "a_w_in": (1, 1024, 5152), "a_conv_w": (1, 4, 3072),
    "a_conv_b": (1, 3072), "a_dt_bias": (1, 32), "a_a_log": (1, 32), "a_d_skip": (1, 32), "a_gate_norm": (1, 2048),
    "a_w_out": (1, 2048, 1024), "a_norm_post": (1, 1024), "kv_norm": (1024,), "w_kv": (1024, 512),
    "b_norm_pre": (1, 1024), "b_w_q": (1, 1024, 1024), "b_sinks": (1, 16), "b_w_o": (1, 1024, 1024),
    "b_norm_post": (1, 1024), "f_norm_pre": (2, 1024), "f_w_up": (2, 1024, 5632), "f_conv_w": (2, 3, 5632),
    "f_conv_b": (2, 5632), "f_w_down": (2, 2816, 1024), "f_norm_post": (2, 1024),
}
SHARD_AXIS = {
    "meta_tokens": 1, "a_norm_pre": 1, "a_w_in": 2, "a_conv_w": 2, "a_conv_b": 1, "a_dt_bias": None, "a_a_log": None,
    "a_d_skip": None, "a_gate_norm": 1, "a_w_out": 1, "a_norm_post": 1, "kv_norm": None, "w_kv": 0, "b_norm_pre": None,
    "b_w_q": 1, "b_sinks": None, "b_w_o": 1, "b_norm_post": None, "f_norm_pre": None, "f_w_up": 2, "f_conv_w": 2,
    "f_conv_b": None, "f_w_down": 1, "f_norm_post": None,
}
BIG = ["a_w_in", "a_w_out", "w_kv", "b_w_q", "b_w_o", "f_w_up", "f_w_down"]
SMALL = [n for n in WEIGHTS if n not in BIG]
SMALL_SHARDED = [n for n in SMALL if SHARD_AXIS[n] is not None]


def _shard_shape(name):
    shape = list(FULL_SHAPE[name])
    if SHARD_AXIS[name] is not None:
        shape[SHARD_AXIS[name]] //= N_CHIPS
    return tuple(shape)


def _numel(shape):
    return int(math.prod(shape))


SUBLANES = 8


def _packed_rows(shape):
    rows = -(-_numel(shape) // LANES)
    return -(-rows // SUBLANES) * SUBLANES


def _pack(arrays):
    parts = []
    for a in arrays:
        size, rows = _numel(a.shape), _packed_rows(a.shape)
        if size % LANES == 0:
            part = jnp.pad(a.reshape(size // LANES, LANES), ((0, rows - size // LANES), (0, 0)))
        else:
            part = jnp.pad(a.reshape(-1), (0, rows * LANES - size)).reshape(rows, LANES)
        parts.append(part)
    return jnp.concatenate(parts, axis=0)


def _unpack(packed, names, shape_of):
    out, off = {}, 0
    lead = packed.shape[:-2]
    for n in names:
        shape = tuple(shape_of(n))
        size, rows = _numel(shape), _packed_rows(shape)
        part = packed[..., off:off + rows, :]
        if size % LANES == 0:
            out[n] = part[..., :size // LANES, :].reshape(lead + shape)
        else:
            out[n] = part.reshape(lead + (rows * LANES,))[..., :size].reshape(lead + shape)
        off += rows
    return out


def _split_chips(name, full):
    ax = SHARD_AXIS[name]
    shape = full.shape
    cut = shape[:ax] + (N_CHIPS, shape[ax] // N_CHIPS) + shape[ax + 1:]
    return jnp.moveaxis(full.reshape(cut), ax, 0)


def _join_chips(name, stacked):
    ax = SHARD_AXIS[name]
    moved = jnp.moveaxis(stacked, 0, ax)
    shape = moved.shape
    return moved.reshape(shape[:ax] + (shape[ax] * shape[ax + 1],) + shape[ax + 2:])


def _as2d(a):
    return a.reshape(-1, a.shape[-1])


BUFFERS = [("a_w_in", "a_w_in", None), ("a_w_out", "a_w_out", None), ("w_kv", "w_kv", None),
           ("b_w_q", "b_w_q", None), ("b_w_o", "b_w_o", None), ("f_w_up0", "f_w_up", 0), ("f_w_up1", "f_w_up", 1),
           ("f_w_down0", "f_w_down", 0), ("f_w_down1", "f_w_down", 1)]


def _local_shard(arrays, weight, layer):
    return _as2d(arrays[weight]) if layer is None else arrays[weight][layer]


def _weight_from_gathered(weight, buf):
    if weight == "a_w_in":
        return buf.transpose(1, 0, 2).reshape(buf.shape[1], N_CHIPS * buf.shape[2])
    if weight == "f_w_up":
        return buf
    return buf.reshape(N_CHIPS * buf.shape[1], buf.shape[2])


def _gathered_from_grad(weight, g):
    if weight == "a_w_in":
        rows = g.shape[0]
        return g.reshape(rows, N_CHIPS, g.shape[1] // N_CHIPS).transpose(1, 0, 2).astype(BF16)
    if weight == "f_w_up":
        return g
    return g.reshape(N_CHIPS, g.shape[0] // N_CHIPS, g.shape[1]).astype(BF16)


GATHER_SCHEDULE = {
    "a_in_main": [("ici", ["a_w_out"])],
    "a_conv": [("d2d", ["a_w_out"]), ("ici", ["f_w_down0"])],
    "a_ssd_prep": [("d2d", ["f_w_down0"]), ("ici", ["w_kv", "b_w_q", "b_w_o"])],
    "a_ssd": [("d2d", ["w_kv", "b_w_q", "b_w_o"]), ("ici", ["f_w_up0"])],
    "a_gate": [("d2d", ["f_w_up0"])],
    "ffn0_conv": [("ici", ["f_w_down1"])],
    "b_attn": [("d2d", ["f_w_down1"]), ("ici", ["f_w_up1"])],
    "b_o": [("d2d", ["f_w_up1"])],
}
REDUCE_SCHEDULE = {
    "ffn1_conv_bwd": ["f_w_down1"],
    "b_attn_bwd": ["f_w_up1", "b_w_o"],
    "ffn0_conv_bwd": ["b_w_q", "w_kv", "f_w_down0"],
    "a_ssd_bwd": ["f_w_up0", "a_w_out"],
    "a_in_main_dx": ["a_w_in"],
}
PAIR_SCHEDULE = {
    "ffn1_down_dx": ["f_w_down1"],
    "b_o_dx": ["f_w_up1", "b_w_o"],
    "ffn0_down_dx": ["b_w_q", "w_kv", "f_w_down0"],
    "a_out_dx": ["f_w_up0", "a_w_out"],
    "a_in_dt_dx": ["a_w_in"],
}
SWAP_SCHEDULE = {"a_in_main_dw": ["f_w_down1", "f_w_up1", "b_w_o", "b_w_q", "w_kv", "f_w_down0", "f_w_up0", "a_w_out"]}


def _buffer_of(weight, layer):
    return weight if layer is None else f"{weight}{layer}"


class _Pipeline:
    def __init__(self, place, slots):
        self.place = place
        self.slots = dict(slots)
        self.running = []
        self.grads = {}
        self.theirs = {}
        self.partials = {}
        self.peers = {}
        self.reduced = {}

    def _collect(self):
        for step, buffers, table in self.running:
            table.update(zip(buffers, step.results))
        self.running = []

    def gather_now(self, name, buffers):
        step = _step_gather_full([self.slots[b] for b in buffers])
        _run_steps(name, [step])
        self.slots.update(zip(buffers, step.results))

    def weight(self, name, layer=None):
        self._collect()
        return _weight_from_gathered(name, self.slots[_buffer_of(name, layer)])

    def grad(self, name, layer, g):
        self.grads[_buffer_of(name, layer)] = _gathered_from_grad(name, g)

    def steps(self, kernel):
        self._collect()
        steps = []
        for phase, buffers in GATHER_SCHEDULE.get(kernel, []):
            make = _step_gather_ici if phase == "ici" else _step_gather_d2d
            step = make([self.slots[b] for b in buffers])
            self.running.append((step, buffers, self.slots))
            steps.append(step)
        buffers = PAIR_SCHEDULE.get(kernel)
        if buffers:
            step = _step_pair_exchange([self.grads[b] for b in buffers])
            self.running.append((step, buffers, self.theirs))
            steps.append(step)
        buffers = REDUCE_SCHEDULE.get(kernel)
        if buffers:
            for b in buffers:
                self.partials[b] = _rs_pair_add("reduce_pair_add_" + b, self.place, self.grads[b], self.theirs[b])
            step = _step_chip_exchange([self.partials[b] for b in buffers])
            self.running.append((step, buffers, self.peers))
            steps.append(step)
        buffers = SWAP_SCHEDULE.get(kernel)
        if buffers:
            step = self._swap_step(buffers)
            self.running.append((step, buffers, self.reduced))
            steps.append(step)
        return steps

    def _swap_step(self, buffers):
        halves = [_rs_chip_add("reduce_chip_add_" + b, self.place, self.partials[b], self.peers[b]) for b in buffers]
        return _step_pair_gather(halves)

    def finish(self):
        self._collect()
        rest = [b for b, _, _ in BUFFERS if b not in self.reduced]
        step = self._swap_step(rest)
        _run_steps("reduce_pair_gather", [step])
        self.reduced.update(zip(rest, step.results))
        return self.reduced


def kernel(x, meta_tokens, a_norm_pre, a_w_in, a_conv_w, a_conv_b, a_dt_bias, a_a_log, a_d_skip, a_gate_norm, a_w_out, a_norm_post, kv_norm, w_kv, b_norm_pre, b_w_q, b_sinks, b_w_o, b_norm_post, f_norm_pre, f_w_up, f_conv_w, f_conv_b, f_w_down, f_norm_post, loss_target, m_meta_tokens, m_a_norm_pre, m_a_w_in, m_a_conv_w, m_a_conv_b, m_a_dt_bias, m_a_a_log, m_a_d_skip, m_a_gate_norm, m_a_w_out, m_a_norm_post, m_kv_norm, m_w_kv, m_b_norm_pre, m_b_w_q, m_b_sinks, m_b_w_o, m_b_norm_post, m_f_norm_pre, m_f_w_up, m_f_conv_w, m_f_conv_b, m_f_w_down, m_f_norm_post, v_meta_tokens, v_a_norm_pre, v_a_w_in, v_a_conv_w, v_a_conv_b, v_a_dt_bias, v_a_a_log, v_a_d_skip, v_a_gate_norm, v_a_w_out, v_a_norm_post, v_kv_norm, v_w_kv, v_b_norm_pre, v_b_w_q, v_b_sinks, v_b_w_o, v_b_norm_post, v_f_norm_pre, v_f_w_up, v_f_conv_w, v_f_conv_b, v_f_w_down, v_f_norm_post):
    given = dict(locals())
    w = {n: given[n] for n in WEIGHTS}
    mom = {n: given["m_" + n] for n in WEIGHTS}
    var = {n: given["v_" + n] for n in WEIGHTS}
    chip = 2 * lax.axis_index("x") + lax.axis_index("y")
    core = lax.axis_index("c")
    place = jnp.stack([chip, core]).astype(jnp.int32)

    small_all = _allgather_small("gather_small", _pack([w[n] for n in SMALL_SHARDED]))
    small_parts = _unpack(small_all, SMALL_SHARDED, _shard_shape)
    slots = {b: _cast_into_slot("cast_" + b, place, _local_shard(w, wn, layer)) for b, wn, layer in BUFFERS}
    pipeline = _Pipeline(place, slots)
    pipeline.gather_now("gather_first", ["a_w_in"])
    p = {}
    for n in SMALL:
        p[n] = _join_chips(n, small_parts[n]) if n in SMALL_SHARDED else w[n]
    p["a_conv_w"] = p["a_conv_w"][0]
    p["kv_norm"] = p["kv_norm"].reshape(1, D_MODEL)

    loss_local, grad_x, g = _local_step(x[0], loss_target[0], p, pipeline)
    loss = lax.psum(loss_local, ("x", "y", "c"))

    small_sum = _allreduce_small("reduce_small", _pack([g[n].reshape(FULL_SHAPE[n]) for n in SMALL]))
    small_red = _unpack(small_sum, SMALL, lambda n: FULL_SHAPE[n])
    grads = {}
    for n in SMALL:
        if SHARD_AXIS[n] is None:
            grads[n] = small_red[n]
        else:
            grads[n] = lax.dynamic_index_in_dim(_split_chips(n, small_red[n]), chip, 0, keepdims=False)

    shard_sum = pipeline.finish()
    for n in BIG:
        if n in ("f_w_up", "f_w_down"):
            grads[n] = jnp.stack([shard_sum[n + "0"], shard_sum[n + "1"]])
        else:
            grads[n] = shard_sum[n].reshape(_shard_shape(n))

    delta, new_m, new_v = {}, {}, {}
    for n in BIG:
        shape = _shard_shape(n)
        d, m2, v2 = _adamw("adamw_" + n, _as2d(w[n]), _as2d(grads[n]), _as2d(mom[n]), _as2d(var[n]))
        delta[n], new_m[n], new_v[n] = d.reshape(shape), m2.reshape(shape), v2.reshape(shape)
    packed = [_pack([src[n].reshape(_shard_shape(n)) for n in SMALL]) for src in (w, grads, mom, var)]
    outs = _adamw("adamw_small", *packed)
    for dst, flat in zip((delta, new_m, new_v), outs):
        dst.update(_unpack(flat, SMALL, _shard_shape))

    return (loss, grad_x[None], *[grads[n].reshape(_shard_shape(n)) for n in WEIGHTS],
            *[delta[n] for n in WEIGHTS], *[new_m[n] for n in WEIGHTS], *[new_v[n] for n in WEIGHTS])
```

```python
import functools
import math

import jax
import jax.numpy as jnp
from jax import lax
from jax.experimental import pallas as pl
from jax.experimental.pallas import tpu as pltpu

F32, BF16 = jnp.float32, jnp.bfloat16
MESH = pl.DeviceIdType.MESH

D_MODEL = 1024
N_META = 16
CHUNK = 128
PAD_ROWS = CHUNK - N_META
D_INNER = 2048
D_STATE = 128
N_GROUPS = 4
HEADS_PER_GROUP = 8
SSM_HEADS = 32
HEAD_DIM = 64
D_BC = N_GROUPS * D_STATE
D_XBC = D_INNER + 2 * D_BC
D_MAIN = D_INNER + D_XBC
D_IN_PROJ = D_MAIN + SSM_HEADS
GROUP_W = HEADS_PER_GROUP * HEAD_DIM
SSM_CONV = 4
D_FF = 2816
FFN_CONV = 3
N_Q_HEADS = 16
N_KV_HEADS = 4
D_KV = 256
ATTN_SCALE = 1.0 / math.sqrt(HEAD_DIM)
RMS_EPS = 1e-6
NEG_INF = -1e30
LANES = 128
VMEM_LIMIT = 48 * 1024 * 1024

ADAM_LR, ADAM_B1, ADAM_B2, ADAM_EPS, ADAM_WD, ADAM_STEP = 0.001, 0.9, 0.999, 1e-08, 0.01, 10

N_CHIPS = 4
N_DEV = 8


def _cparams(sem=None):
    return pltpu.CompilerParams(dimension_semantics=sem, vmem_limit_bytes=VMEM_LIMIT)


def _tile(n, cands=(512, 256, 128)):
    for t in cands:
        if n % t == 0:
            return t
    return n


def _row_tile(rows, width):
    for t in (544, 272):
        if rows % t == 0 and t * width * 4 <= (3 << 20):
            return t
    return 128


def _rows_mask(i, tm):
    rows = i * tm + lax.broadcasted_iota(jnp.int32, (tm, 1), 0)
    return rows >= PAD_ROWS


def _dot(a, b):
    return jnp.dot(a, b, preferred_element_type=F32)


def _dot_nt(a, b):
    return lax.dot_general(a, b, (((1,), (1,)), ((), ())), preferred_element_type=F32)


def _dot_tn(a, b):
    return lax.dot_general(a, b, (((0,), (0,)), ((), ())), preferred_element_type=F32)


def _sigmoid(x):
    return 1.0 / (1.0 + jnp.exp(-x))


def _place():
    return lax.axis_index("x"), lax.axis_index("y"), lax.axis_index("c")


def _other_chips(x, y):
    return [(1 - x, y), (x, 1 - y), (1 - x, 1 - y)]


class _Step:
    def __init__(self, ins, outs, aliases, n_sems, start, finish):
        self.ins, self.outs, self.aliases, self.n_sems = list(ins), list(outs), dict(aliases), n_sems
        self.start, self.finish = start, finish
        self.results = None


def _like(a):
    return jax.ShapeDtypeStruct(a.shape, a.dtype)


def _remote(src, dst, send_sems, recv_sems, k, device):
    return pltpu.make_async_remote_copy(src, dst, send_sems.at[k], recv_sems.at[k], device_id=device, device_id_type=MESH)


def _half(ref, split, which, lead=()):
    if split == "rows":
        hr = ref.shape[-2] // 2
        return ref.at[lead + (pl.ds(which * hr, hr),)]
    hc = ref.shape[-1] // 2
    return ref.at[lead + (slice(None), pl.ds(which * hc, hc))]


def _splits(bufs, splits):
    return list(splits) if splits is not None else ["rows"] * len(bufs)


def _step_gather_ici(bufs, splits=None):
    splits = _splits(bufs, splits)

    def copies(outs, send_sems, recv_sems, received):
        x, y, c = _place()
        me = 2 * x + y
        for k, o in enumerate(outs):
            for j, (cx, cy) in enumerate(_other_chips(x, y)):
                part = _half(o, splits[k], c, (2 * cx + cy if received else me,))
                yield _remote(part, part, send_sems, recv_sems, 3 * k + j, (cx, cy, c))

    def start(ins, outs, send_sems, recv_sems):
        for cp in copies(outs, send_sems, recv_sems, False):
            cp.start()

    def finish(ins, outs, send_sems, recv_sems):
        for cp in copies(outs, send_sems, recv_sems, True):
            cp.wait_recv()
        for cp in copies(outs, send_sems, recv_sems, False):
            cp.wait_send()

    return _Step(bufs, [_like(b) for b in bufs], {k: k for k in range(len(bufs))}, 3 * len(bufs), start, finish)


def _step_gather_d2d(bufs, splits=None):
    splits = _splits(bufs, splits)

    def copies(outs, send_sems, recv_sems, received):
        x, y, c = _place()
        for k, o in enumerate(outs):
            for j, (cx, cy) in enumerate(_other_chips(x, y)):
                part = _half(o, splits[k], 1 - c if received else c, (2 * cx + cy,))
                yield _remote(part, part, send_sems, recv_sems, 3 * k + j, (x, y, 1 - c))

    def start(ins, outs, send_sems, recv_sems):
        for cp in copies(outs, send_sems, recv_sems, False):
            cp.start()

    def finish(ins, outs, send_sems, recv_sems):
        for cp in copies(outs, send_sems, recv_sems, True):
            cp.wait_recv()
        for cp in copies(outs, send_sems, recv_sems, False):
            cp.wait_send()

    return _Step(bufs, [_like(b) for b in bufs], {k: k for k in range(len(bufs))}, 3 * len(bufs), start, finish)


def _step_gather_full(bufs, splits=None):
    n = len(bufs)
    splits = _splits(bufs, splits)

    def ici(outs, send_sems, recv_sems, received):
        x, y, c = _place()
        me = 2 * x + y
        for k, o in enumerate(outs):
            for j, (cx, cy) in enumerate(_other_chips(x, y)):
                part = _half(o, splits[k], c, (2 * cx + cy if received else me,))
                yield _remote(part, part, send_sems, recv_sems, 3 * k + j, (cx, cy, c))

    def d2d(outs, send_sems, recv_sems, received):
        x, y, c = _place()
        for k, o in enumerate(outs):
            for j, (cx, cy) in enumerate(_other_chips(x, y)):
                part = _half(o, splits[k], 1 - c if received else c, (2 * cx + cy,))
                yield _remote(part, part, send_sems, recv_sems, 3 * n + 3 * k + j, (x, y, 1 - c))

    def start(ins, outs, send_sems, recv_sems):
        for cp in ici(outs, send_sems, recv_sems, False):
            cp.start()

    def finish(ins, outs, send_sems, recv_sems):
        for arrived, onward in zip(ici(outs, send_sems, recv_sems, True), d2d(outs, send_sems, recv_sems, False)):
            arrived.wait_recv()
            onward.start()
        for cp in d2d(outs, send_sems, recv_sems, True):
            cp.wait_recv()
        for cp in ici(outs, send_sems, recv_sems, False):
            cp.wait_send()
        for cp in d2d(outs, send_sems, recv_sems, False):
            cp.wait_send()

    return _Step(bufs, [_like(b) for b in bufs], {k: k for k in range(n)}, 6 * n, start, finish)


def _half_shape(shape, split):
    return shape[:-2] + ((shape[-2] // 2, shape[-1]) if split == "rows" else (shape[-2], shape[-1] // 2))


def _step_pair_exchange(grads, splits=None):
    splits = _splits(grads, splits)

    def copies(ins, outs, send_sems, recv_sems):
        x, y, c = _place()
        for k, (g, o) in enumerate(zip(ins, outs)):
            yield _remote(_half(g, splits[k], 1 - c, (slice(None),)), o, send_sems, recv_sems, k, (x, y, 1 - c))

    def start(ins, outs, send_sems, recv_sems):
        for cp in copies(ins, outs, send_sems, recv_sems):
            cp.start()

    def finish(ins, outs, send_sems, recv_sems):
        for cp in copies(ins, outs, send_sems, recv_sems):
            cp.wait()

    outs = [jax.ShapeDtypeStruct(_half_shape(g.shape, s), g.dtype) for g, s in zip(grads, splits)]
    return _Step(grads, outs, {}, len(grads), start, finish)


def _step_chip_exchange(partials):
    def copies(ins, outs, send_sems, recv_sems):
        x, y, c = _place()
        for k, (q, o) in enumerate(zip(ins, outs)):
            for j, (cx, cy) in enumerate(_other_chips(x, y)):
                yield _remote(q.at[2 * cx + cy], o.at[j], send_sems, recv_sems, 3 * k + j, (cx, cy, c))

    def start(ins, outs, send_sems, recv_sems):
        for cp in copies(ins, outs, send_sems, recv_sems):
            cp.start()

    def finish(ins, outs, send_sems, recv_sems):
        for cp in copies(ins, outs, send_sems, recv_sems):
            cp.wait()

    outs = [jax.ShapeDtypeStruct((3,) + q.shape[1:], q.dtype) for q in partials]
    return _Step(partials, outs, {}, 3 * len(partials), start, finish)


def _step_pair_gather(shards, splits=None):
    splits = _splits(shards, splits)

    def copies(outs, send_sems, recv_sems, received):
        x, y, c = _place()
        for k, o in enumerate(outs):
            part = _half(o, splits[k], 1 - c if received else c)
            yield _remote(part, part, send_sems, recv_sems, k, (x, y, 1 - c))

    def start(ins, outs, send_sems, recv_sems):
        for cp in copies(outs, send_sems, recv_sems, False):
            cp.start()

    def finish(ins, outs, send_sems, recv_sems):
        for cp in copies(outs, send_sems, recv_sems, True):
            cp.wait_recv()
        for cp in copies(outs, send_sems, recv_sems, False):
            cp.wait_send()

    return _Step(shards, [_like(s) for s in shards], {k: k for k in range(len(shards))}, len(shards), start, finish)


def _call(body, *, name, out_shape, grid, in_specs, out_specs, operands, scratch_shapes=(), semantics=None, steps=()):
    single = not isinstance(out_shape, (tuple, list))
    out_shapes = [out_shape] if single else list(out_shape)
    out_spec_list = [out_specs] if single else list(out_specs)
    steps = list(steps)
    if not steps:
        res = pl.pallas_call(body, name=name, out_shape=out_shapes, grid=grid, in_specs=list(in_specs),
                             out_specs=out_spec_list, scratch_shapes=list(scratch_shapes),
                             compiler_params=_cparams(semantics))(*operands)
        return res[0] if single else res
    n_in, n_out, n_scr = len(operands), len(out_shapes), len(scratch_shapes)
    x_in = [a for s in steps for a in s.ins]
    x_out = [o for s in steps for o in s.outs]
    aliases, in_off, out_off = {}, 0, 0
    for s in steps:
        for i, o in s.aliases.items():
            aliases[n_in + in_off + i] = n_out + out_off + o
        in_off += len(s.ins)
        out_off += len(s.outs)
    sems = []
    for s in steps:
        sems += [pltpu.SemaphoreType.DMA((s.n_sems,)), pltpu.SemaphoreType.DMA((s.n_sems,))]
    any_spec = pl.BlockSpec(memory_space=pl.ANY)

    def carried(*refs):
        pos = 0
        ins = refs[pos:pos + n_in]; pos += n_in
        xi = refs[pos:pos + len(x_in)]; pos += len(x_in)
        outs = refs[pos:pos + n_out]; pos += n_out
        xo = refs[pos:pos + len(x_out)]; pos += len(x_out)
        scr = refs[pos:pos + n_scr]; pos += n_scr
        sem_refs = refs[pos:]

        def each(action):
            i0 = o0 = 0
            for k, s in enumerate(steps):
                getattr(s, action)(xi[i0:i0 + len(s.ins)], xo[o0:o0 + len(s.outs)], sem_refs[2 * k], sem_refs[2 * k + 1])
                i0 += len(s.ins)
                o0 += len(s.outs)

        if grid:
            first = functools.reduce(jnp.logical_and, [pl.program_id(d) == 0 for d in range(len(grid))])
            last = functools.reduce(jnp.logical_and, [pl.program_id(d) == grid[d] - 1 for d in range(len(grid))])
            pl.when(first)(lambda: each("start"))
            body(*ins, *outs, *scr)
            pl.when(last)(lambda: each("finish"))
        else:
            each("start")
            body(*ins, *outs, *scr)
            each("finish")

    res = pl.pallas_call(
        carried, name=name, out_shape=out_shapes + x_out, grid=grid,
        in_specs=list(in_specs) + [any_spec] * len(x_in), out_specs=out_spec_list + [any_spec] * len(x_out),
        scratch_shapes=list(scratch_shapes) + sems, input_output_aliases=aliases,
        compiler_params=_cparams(None if semantics is None else ("arbitrary",) * len(grid)),
    )(*operands, *x_in)
    o0 = n_out
    for s in steps:
        s.results = list(res[o0:o0 + len(s.outs)])
        o0 += len(s.outs)
    return res[0] if single else tuple(res[:n_out])


def _run_steps(name, steps):
    _call(lambda: None, name=name, out_shape=[], grid=(), in_specs=[], out_specs=[], operands=[], steps=steps)
    return [s.results for s in steps]


def _mm(name, a, b, mode, out_dtype=F32, acc=None, b_colblock=0, k_rows=None, steps=()):
    resident_bytes = 8 << 20
    if mode == "nn":
        m, k = a.shape
        n = b.shape[1]
        tm = m
        while tm * k * 2 > resident_bytes and tm % 32 == 0:
            tm //= 2
        tn = _tile(n)
        grid = (m // tm, n // tn)
        in_specs = [pl.BlockSpec((tm, k), lambda i, j: (i, 0)), pl.BlockSpec((k, tn), lambda i, j: (0, j))]
        out_shape, out_block = (m, n), (tm, tn)
    elif mode == "nt":
        m, n = a.shape
        k = k_rows or b.shape[0]
        tm = m
        while tm * n * 2 > resident_bytes and tm % 32 == 0:
            tm //= 2
        tk = _tile(k)
        grid = (m // tm, k // tk)
        in_specs = [pl.BlockSpec((tm, n), lambda i, j: (i, 0)), pl.BlockSpec((tk, n), lambda i, j: (j, b_colblock))]
        out_shape, out_block = (m, k), (tm, tk)
    else:
        m, k = a.shape
        n = b.shape[1]
        tk, tn = _tile(k), _tile(n)
        grid = (k // tk, n // tn)
        in_specs = [pl.BlockSpec((m, tk), lambda i, j: (0, i)), pl.BlockSpec((m, tn), lambda i, j: (0, j))]
        out_shape, out_block = (k, n), (tk, tn)
    out_spec = pl.BlockSpec(out_block, lambda i, j: (i, j))
    has_acc = acc is not None

    def body(*refs):
        a_ref, b_ref = refs[0], refs[1]
        o_ref = refs[-1]
        av, bv = a_ref[...], b_ref[...]
        if mode == "nn":
            r = _dot(av, bv)
        elif mode == "nt":
            r = _dot_nt(av, bv)
        else:
            r = _dot_tn(av, bv)
        if has_acc:
            r = r + refs[2][...]
        o_ref[...] = r.astype(o_ref.dtype)

    operands = [a, b]
    if has_acc:
        in_specs = in_specs + [out_spec]
        operands.append(acc)
    return _call(body, name=name, out_shape=jax.ShapeDtypeStruct(out_shape, out_dtype), grid=grid, in_specs=in_specs,
                 out_specs=out_spec, operands=operands, semantics=("parallel", "parallel"), steps=steps)


def _fit_rows(m, row_bytes, budget=8 << 20):
    tm = m
    while tm * row_bytes > budget and tm % 32 == 0:
        tm //= 2
    return tm


def _mm_nn_bychip(name, a, bc):
    m, k = a.shape
    n = bc.shape[2]
    tm = min(_fit_rows(m, k * 2), _fit_rows(m, n * 4))

    def body(a_ref, b_ref, o_ref):
        o_ref[...] = _dot(a_ref[...], b_ref[...])

    return pl.pallas_call(
        body, name=name, out_shape=jax.ShapeDtypeStruct((m, N_CHIPS * n), F32), grid=(m // tm, N_CHIPS),
        in_specs=[pl.BlockSpec((tm, k), lambda i, c: (i, 0)), pl.BlockSpec((None, k, n), lambda i, c: (c, 0, 0))],
        out_specs=pl.BlockSpec((tm, n), lambda i, c: (i, c)), compiler_params=_cparams(("parallel", "parallel")),
    )(a, bc)


def _mm_nt_bychip(name, a, bc, chip0, acc=None):
    m = a.shape[0]
    _, k, n = bc.shape
    nch = a.shape[1] // n
    tm, tk = _fit_rows(m, n * 2), _tile(k)
    has_acc = acc is not None

    def body(*refs):
        a_ref, b_ref, o_ref = refs[0], refs[1], refs[-1]

        @pl.when(pl.program_id(2) == 0)
        def _():
            o_ref[...] = refs[2][...] if has_acc else jnp.zeros_like(o_ref)

        o_ref[...] += _dot_nt(a_ref[...], b_ref[...])

    out_spec = pl.BlockSpec((tm, tk), lambda i, j, c: (i, j))
    in_specs = [pl.BlockSpec((tm, n), lambda i, j, c: (i, c)),
                pl.BlockSpec((None, tk, n), lambda i, j, c: (chip0 + c, j, 0))]
    operands = [a, bc]
    if has_acc:
        in_specs.append(out_spec)
        operands.append(acc)
    return pl.pallas_call(
        body, name=name, out_shape=jax.ShapeDtypeStruct((m, k), F32), grid=(m // tm, k // tk, nch),
        in_specs=in_specs, out_specs=out_spec, compiler_params=_cparams(("parallel", "parallel", "arbitrary")),
    )(*operands)


def _mm_tn_bychip(name, a, dy, n, chip0, into=None):
    m, k = a.shape
    nch = dy.shape[1] // n
    tk = _tile(k)

    def body(*refs):
        a_ref, d_ref, o_ref = refs[0], refs[1], refs[-1]
        o_ref[...] = _dot_tn(a_ref[...], d_ref[...]).astype(BF16)

    in_specs = [pl.BlockSpec((m, tk), lambda i, c: (0, i)), pl.BlockSpec((m, n), lambda i, c: (0, c))]
    operands = [a, dy]
    aliases = {}
    if into is not None:
        in_specs.append(pl.BlockSpec(memory_space=pl.ANY))
        operands.append(into)
        aliases = {2: 0}
    return pl.pallas_call(
        body, name=name, out_shape=jax.ShapeDtypeStruct((N_CHIPS, k, n), BF16), grid=(k // tk, nch),
        in_specs=in_specs, out_specs=pl.BlockSpec((None, tk, n), lambda i, c: (chip0 + c, i, 0)),
        input_output_aliases=aliases, compiler_params=_cparams(("parallel", "parallel")),
    )(*operands)


def _rms_fwd(name, h, w):
    rows, width = h.shape
    tm = _row_tile(rows, width)

    def body(h_ref, w_ref, o_ref):
        x = h_ref[...]
        r = lax.rsqrt(jnp.mean(x * x, axis=-1, keepdims=True) + RMS_EPS)
        o_ref[...] = (x * r * w_ref[...]).astype(BF16)

    return pl.pallas_call(
        body, name=name, out_shape=jax.ShapeDtypeStruct((rows, width), BF16), grid=(rows // tm,),
        in_specs=[pl.BlockSpec((tm, width), lambda i: (i, 0)), pl.BlockSpec((1, width), lambda i: (0, 0))],
        out_specs=pl.BlockSpec((tm, width), lambda i: (i, 0)), compiler_params=_cparams(("parallel",)),
    )(h, w)


def _resid_norm_fwd(name, h, pre, w):
    rows, width = h.shape
    tm = _row_tile(rows, width)

    def body(h_ref, p_ref, w_ref, o_ref):
        p = p_ref[...]
        r = lax.rsqrt(jnp.mean(p * p, axis=-1, keepdims=True) + RMS_EPS)
        o_ref[...] = h_ref[...] + jnp.where(_rows_mask(pl.program_id(0), tm), p * r * w_ref[...], 0.0)

    row_spec = pl.BlockSpec((tm, width), lambda i: (i, 0))
    return pl.pallas_call(
        body, name=name, out_shape=jax.ShapeDtypeStruct((rows, width), F32), grid=(rows // tm,),
        in_specs=[row_spec, row_spec, pl.BlockSpec((1, width), lambda i: (0, 0))],
        out_specs=row_spec, compiler_params=_cparams(("parallel",)),
    )(h, pre, w)


def _resid_norm_bwd(name, dh, pre, w):
    rows, width = dh.shape
    tm = _row_tile(rows, width)

    def body(dh_ref, p_ref, w_ref, dp_ref, dw_ref):
        i = pl.program_id(0)
        dy = jnp.where(_rows_mask(i, tm), dh_ref[...], 0.0)
        p = p_ref[...]
        r = lax.rsqrt(jnp.mean(p * p, axis=-1, keepdims=True) + RMS_EPS)
        xhat = p * r
        dxhat = dy * w_ref[...]
        dp = r * (dxhat - xhat * jnp.mean(dxhat * xhat, axis=-1, keepdims=True))
        dp_ref[...] = dp.astype(BF16)

        @pl.when(i == 0)
        def _():
            dw_ref[...] = jnp.zeros_like(dw_ref)

        dw_ref[...] += jnp.sum(dy * xhat, axis=0, keepdims=True)

    row_spec = pl.BlockSpec((tm, width), lambda i: (i, 0))
    vec_spec = pl.BlockSpec((1, width), lambda i: (0, 0))
    return pl.pallas_call(
        body, name=name,
        out_shape=(jax.ShapeDtypeStruct((rows, width), BF16), jax.ShapeDtypeStruct((1, width), F32)),
        grid=(rows // tm,), in_specs=[row_spec, row_spec, vec_spec], out_specs=(row_spec, vec_spec),
        compiler_params=_cparams(("arbitrary",)),
    )(dh, pre, w)


def _norm_bwd_add(name, dh, dhn, h, w):
    rows, width = dh.shape
    tm = _row_tile(rows, width)

    def body(dh_ref, dhn_ref, h_ref, w_ref, o_ref, dw_ref):
        i = pl.program_id(0)
        x = h_ref[...]
        dy = dhn_ref[...]
        r = lax.rsqrt(jnp.mean(x * x, axis=-1, keepdims=True) + RMS_EPS)
        xhat = x * r
        dxhat = dy * w_ref[...]
        dx = r * (dxhat - xhat * jnp.mean(dxhat * xhat, axis=-1, keepdims=True))
        o_ref[...] = dh_ref[...] + jnp.where(_rows_mask(i, tm), dx, 0.0)

        @pl.when(i == 0)
        def _():
            dw_ref[...] = jnp.zeros_like(dw_ref)

        dw_ref[...] += jnp.sum(dy * xhat, axis=0, keepdims=True)

    row_spec = pl.BlockSpec((tm, width), lambda i: (i, 0))
    vec_spec = pl.BlockSpec((1, width), lambda i: (0, 0))
    return pl.pallas_call(
        body, name=name,
        out_shape=(jax.ShapeDtypeStruct((rows, width), F32), jax.ShapeDtypeStruct((1, width), F32)),
        grid=(rows // tm,), in_specs=[row_spec, row_spec, row_spec, vec_spec], out_specs=(row_spec, vec_spec),
        compiler_params=_cparams(("arbitrary",)),
    )(dh, dhn, h, w)


def _shift_down(x, s, rows):
    return pltpu.roll(x, s, 0) if s else x


def _shift_up(x, s, rows):
    return pltpu.roll(x, rows - s, 0) if s else x


def _conv4_fwd(name, zx, cw, cb, steps=()):
    rows = zx.shape[0]
    off = D_INNER // LANES

    def body(x_ref, w_ref, b_ref, o_ref):
        x = x_ref[...]
        acc = b_ref[...] + w_ref[pl.ds(SSM_CONV - 1, 1), :] * x
        for s in range(1, SSM_CONV):
            acc = acc + w_ref[pl.ds(SSM_CONV - 1 - s, 1), :] * _shift_down(x, s, rows)
        valid = lax.broadcasted_iota(jnp.int32, (rows, 1), 0) >= PAD_ROWS
        o_ref[...] = jnp.where(valid, acc * _sigmoid(acc), 0.0)

    return _call(
        body, name=name, out_shape=jax.ShapeDtypeStruct((rows, D_XBC), F32), grid=(D_XBC // LANES,),
        in_specs=[pl.BlockSpec((rows, LANES), lambda j: (0, j + off)),
                  pl.BlockSpec((SSM_CONV, LANES), lambda j: (0, j)),
                  pl.BlockSpec((1, LANES), lambda j: (0, j))],
        out_specs=pl.BlockSpec((rows, LANES), lambda j: (0, j)), operands=[zx, cw, cb],
        semantics=("parallel",), steps=steps)


def _conv4_bwd(name, zx, dout, cw, cb, col0):
    rows, width = dout.shape
    zoff = (D_INNER + col0) // LANES
    woff = col0 // LANES

    def body(x_ref, d_ref, w_ref, b_ref, dx_ref, dw_ref, db_ref):
        x = x_ref[...]
        shifted = [_shift_down(x, s, rows) for s in range(SSM_CONV)]
        acc = b_ref[...]
        for s in range(SSM_CONV):
            acc = acc + w_ref[pl.ds(SSM_CONV - 1 - s, 1), :] * shifted[s]
        sig = _sigmoid(acc)
        valid = lax.broadcasted_iota(jnp.int32, (rows, 1), 0) >= PAD_ROWS
        dpre = jnp.where(valid, d_ref[...] * sig * (1.0 + acc * (1.0 - sig)), 0.0)
        dx = w_ref[pl.ds(SSM_CONV - 1, 1), :] * dpre
        for s in range(1, SSM_CONV):
            dx = dx + w_ref[pl.ds(SSM_CONV - 1 - s, 1), :] * _shift_up(dpre, s, rows)
        dx_ref[...] = dx.astype(BF16)
        for s in range(SSM_CONV):
            dw_ref[pl.ds(SSM_CONV - 1 - s, 1), :] = jnp.sum(dpre * shifted[s], axis=0, keepdims=True)
        db_ref[...] = jnp.sum(dpre, axis=0, keepdims=True)

    return pl.pallas_call(
        body, name=name,
        out_shape=(jax.ShapeDtypeStruct((rows, width), BF16), jax.ShapeDtypeStruct((SSM_CONV, width), F32),
                   jax.ShapeDtypeStruct((1, width), F32)),
        grid=(width // LANES,),
        in_specs=[pl.BlockSpec((rows, LANES), lambda j: (0, j + zoff)),
                  pl.BlockSpec((rows, LANES), lambda j: (0, j)),
                  pl.BlockSpec((SSM_CONV, LANES), lambda j: (0, j + woff)),
                  pl.BlockSpec((1, LANES), lambda j: (0, j + woff))],
        out_specs=(pl.BlockSpec((rows, LANES), lambda j: (0, j)),
                   pl.BlockSpec((SSM_CONV, LANES), lambda j: (0, j)),
                   pl.BlockSpec((1, LANES), lambda j: (0, j))),
        compiler_params=_cparams(("parallel",)),
    )(zx, dout, cw, cb)


def _ffn_conv_fwd(name, up, cw, cb, steps=()):
    rows = up.shape[0]
    nt = D_FF // LANES

    def body(g_ref, v_ref, wg_ref, wv_ref, bg_ref, bv_ref, o_ref):
        g, v = g_ref[...], v_ref[...]
        ug, uv = bg_ref[...], bv_ref[...]
        for s in range(FFN_CONV):
            ug = ug + wg_ref[pl.ds(FFN_CONV - 1 - s, 1), :] * _shift_down(g, s, rows)
            uv = uv + wv_ref[pl.ds(FFN_CONV - 1 - s, 1), :] * _shift_down(v, s, rows)
        o_ref[...] = (ug * _sigmoid(ug) * uv).astype(BF16)

    col = lambda shift: pl.BlockSpec((rows, LANES), lambda j: (0, j + shift))
    wsp = lambda shift: pl.BlockSpec((FFN_CONV, LANES), lambda j: (0, j + shift))
    bsp = lambda shift: pl.BlockSpec((1, LANES), lambda j: (0, j + shift))
    return _call(
        body, name=name, out_shape=jax.ShapeDtypeStruct((rows, D_FF), BF16), grid=(nt,),
        in_specs=[col(0), col(nt), wsp(0), wsp(nt), bsp(0), bsp(nt)],
        out_specs=pl.BlockSpec((rows, LANES), lambda j: (0, j)), operands=[up, up, cw, cw, cb, cb],
        semantics=("parallel",), steps=steps)


def _ffn_conv_bwd(name, up, dact, cw, cb, steps=()):
    rows = up.shape[0]
    nt = D_FF // LANES

    def body(g_ref, v_ref, d_ref, wg_ref, wv_ref, bg_ref, bv_ref, dxg_ref, dxv_ref, dwg_ref, dwv_ref, dbg_ref, dbv_ref):
        g, v = g_ref[...], v_ref[...]
        gs = [_shift_down(g, s, rows) for s in range(FFN_CONV)]
        vs = [_shift_down(v, s, rows) for s in range(FFN_CONV)]
        ug, uv = bg_ref[...], bv_ref[...]
        for s in range(FFN_CONV):
            ug = ug + wg_ref[pl.ds(FFN_CONV - 1 - s, 1), :] * gs[s]
            uv = uv + wv_ref[pl.ds(FFN_CONV - 1 - s, 1), :] * vs[s]
        sig = _sigmoid(ug)
        dsig = d_ref[...] * sig
        for dpre, src, w_ref, dx_ref, dw_ref, db_ref in (
                (dsig * uv * (1.0 + ug * (1.0 - sig)), gs, wg_ref, dxg_ref, dwg_ref, dbg_ref),
                (dsig * ug, vs, wv_ref, dxv_ref, dwv_ref, dbv_ref)):
            dx = w_ref[pl.ds(FFN_CONV - 1, 1), :] * dpre
            for s in range(1, FFN_CONV):
                dx = dx + w_ref[pl.ds(FFN_CONV - 1 - s, 1), :] * _shift_up(dpre, s, rows)
            dx_ref[...] = dx.astype(BF16)
            for s in range(FFN_CONV):
                dw_ref[pl.ds(FFN_CONV - 1 - s, 1), :] = jnp.sum(dpre * src[s], axis=0, keepdims=True)
            db_ref[...] = jnp.sum(dpre, axis=0, keepdims=True)

    col = lambda shift: pl.BlockSpec((rows, LANES), lambda j: (0, j + shift))
    wsp = lambda shift: pl.BlockSpec((FFN_CONV, LANES), lambda j: (0, j + shift))
    bsp = lambda shift: pl.BlockSpec((1, LANES), lambda j: (0, j + shift))
    dx_shape = jax.ShapeDtypeStruct((rows, D_FF), BF16)
    dw_shape = jax.ShapeDtypeStruct((FFN_CONV, D_FF), F32)
    db_shape = jax.ShapeDtypeStruct((1, D_FF), F32)
    return _call(
        body, name=name, out_shape=(dx_shape, dx_shape, dw_shape, dw_shape, db_shape, db_shape), grid=(nt,),
        in_specs=[col(0), col(nt), col(0), wsp(0), wsp(nt), bsp(0), bsp(nt)],
        out_specs=(col(0), col(0), wsp(0), wsp(0), bsp(0), bsp(0)),
        operands=[up, up, dact, cw, cw, cb, cb], semantics=("parallel",), steps=steps)


def _dt_fwd(name, dtr, bias):
    rows = dtr.shape[0]
    tm = _row_tile(rows, LANES)

    def body(d_ref, b_ref, o_ref):
        v = d_ref[...] + b_ref[...]
        sp = jnp.maximum(v, 0.0) + jnp.log1p(jnp.exp(-jnp.abs(v)))
        lane = lax.broadcasted_iota(jnp.int32, (tm, LANES), 1)
        ok = _rows_mask(pl.program_id(0), tm) & (lane < SSM_HEADS)
        o_ref[...] = jnp.where(ok, sp, 0.0)

    return pl.pallas_call(
        body, name=name, out_shape=jax.ShapeDtypeStruct((rows, LANES), F32), grid=(rows // tm,),
        in_specs=[pl.BlockSpec((tm, LANES), lambda i: (i, 0)), pl.BlockSpec((1, LANES), lambda i: (0, 0))],
        out_specs=pl.BlockSpec((tm, LANES), lambda i: (i, 0)), compiler_params=_cparams(("parallel",)),
    )(dtr, bias)


def _dt_bwd(name, ddt, dtr, bias):
    rows = dtr.shape[0]
    tm = _row_tile(rows, LANES)

    def body(g_ref, d_ref, b_ref, o_ref, db_ref):
        i = pl.program_id(0)
        lane = lax.broadcasted_iota(jnp.int32, (tm, LANES), 1)
        ok = _rows_mask(i, tm) & (lane < SSM_HEADS)
        dv = jnp.where(ok, g_ref[...] * _sigmoid(d_ref[...] + b_ref[...]), 0.0)
        o_ref[...] = dv.astype(BF16)

        @pl.when(i == 0)
        def _():
            db_ref[...] = jnp.zeros_like(db_ref)

        db_ref[...] += jnp.sum(dv, axis=0, keepdims=True)

    row_spec = pl.BlockSpec((tm, LANES), lambda i: (i, 0))
    vec_spec = pl.BlockSpec((1, LANES), lambda i: (0, 0))
    return pl.pallas_call(
        body, name=name,
        out_shape=(jax.ShapeDtypeStruct((rows, LANES), BF16), jax.ShapeDtypeStruct((1, LANES), F32)),
        grid=(rows // tm,), in_specs=[row_spec, row_spec, vec_spec], out_specs=(row_spec, vec_spec),
        compiler_params=_cparams(("arbitrary",)),
    )(ddt, dtr, bias)


def _gate_fwd(name, y, zx, w, steps=()):
    rows = y.shape[0]
    tm = _row_tile(rows, D_INNER)

    def body(y_ref, z_ref, w_ref, o_ref):
        z = z_ref[...]
        g = y_ref[...] * (z * _sigmoid(z))
        r = lax.rsqrt(jnp.mean(g * g, axis=-1, keepdims=True) + RMS_EPS)
        o_ref[...] = (g * r * w_ref[...]).astype(BF16)

    row_spec = pl.BlockSpec((tm, D_INNER), lambda i: (i, 0))
    return _call(
        body, name=name, out_shape=jax.ShapeDtypeStruct((rows, D_INNER), BF16), grid=(rows // tm,),
        in_specs=[row_spec, row_spec, pl.BlockSpec((1, D_INNER), lambda i: (0, 0))],
        out_specs=row_spec, operands=[y, zx, w], semantics=("parallel",), steps=steps)


def _gate_bwd(name, dyn, y, zx, w):
    rows = y.shape[0]
    tm = _row_tile(rows, D_INNER)

    def body(d_ref, y_ref, z_ref, w_ref, dy_ref, dz_ref, dw_ref):
        i = pl.program_id(0)
        z, yv = z_ref[...], y_ref[...]
        sig = _sigmoid(z)
        sz = z * sig
        g = yv * sz
        r = lax.rsqrt(jnp.mean(g * g, axis=-1, keepdims=True) + RMS_EPS)
        ghat = g * r
        dn = d_ref[...]
        dghat = dn * w_ref[...]
        dg = r * (dghat - ghat * jnp.mean(dghat * ghat, axis=-1, keepdims=True))
        dy_ref[...] = dg * sz
        dz_ref[...] = (dg * yv * sig * (1.0 + z * (1.0 - sig))).astype(BF16)

        @pl.when(i == 0)
        def _():
            dw_ref[...] = jnp.zeros_like(dw_ref)

        dw_ref[...] += jnp.sum(dn * ghat, axis=0, keepdims=True)

    row_spec = pl.BlockSpec((tm, D_INNER), lambda i: (i, 0))
    vec_spec = pl.BlockSpec((1, D_INNER), lambda i: (0, 0))
    return pl.pallas_call(
        body, name=name,
        out_shape=(jax.ShapeDtypeStruct((rows, D_INNER), F32), jax.ShapeDtypeStruct((rows, D_INNER), BF16),
                   jax.ShapeDtypeStruct((1, D_INNER), F32)),
        grid=(rows // tm,), in_specs=[row_spec, row_spec, row_spec, vec_spec],
        out_specs=(row_spec, row_spec, vec_spec), compiler_params=_cparams(("arbitrary",)),
    )(dyn, y, zx, w)


def _split3(x):
    hi = x.astype(BF16)
    r1 = x - hi.astype(F32)
    mid = r1.astype(BF16)
    lo = (r1 - mid.astype(F32)).astype(BF16)
    return hi, mid, lo


def _dot3_data_lhs(x, sel):
    sel16 = sel.astype(F32).astype(BF16)
    hi, mid, lo = _split3(x)
    return _dot(hi, sel16) + _dot(mid, sel16) + _dot(lo, sel16)


def _dot3_data_rhs(sel, x):
    sel16 = sel.astype(F32).astype(BF16)
    hi, mid, lo = _split3(x)
    return _dot(sel16, hi) + _dot(sel16, mid) + _dot(sel16, lo)


def _causal_masks():
    r = lax.broadcasted_iota(jnp.int32, (CHUNK, CHUNK), 0)
    c = lax.broadcasted_iota(jnp.int32, (CHUNK, CHUNK), 1)
    return r >= c, r <= c


def _expand_heads_matrix():
    k = lax.broadcasted_iota(jnp.int32, (LANES, GROUP_W), 0)
    j = lax.broadcasted_iota(jnp.int32, (LANES, GROUP_W), 1)
    return jnp.right_shift(j, 6) == k


def _reduce_heads_matrix():
    j = lax.broadcasted_iota(jnp.int32, (GROUP_W, LANES), 0)
    k = lax.broadcasted_iota(jnp.int32, (GROUP_W, LANES), 1)
    return jnp.right_shift(j, 6) == k


def _reduce_pair_matrix(p):
    j = lax.broadcasted_iota(jnp.int32, (LANES, LANES), 0)
    k = lax.broadcasted_iota(jnp.int32, (LANES, LANES), 1)
    return (2 * p + jnp.right_shift(j, 6)) == k


def _group_cols(ref, g, width):
    return ref.at[:, pl.ds(g * width, width)]


def _ssd_prep(name, dt4, a128, steps=()):
    rows = dt4.shape[1]
    nc = rows // CHUNK

    def body(dt_ref, a_ref, dte_ref, acs_ref, acst_ref):
        causal, _ = _causal_masks()
        expand = _expand_heads_matrix()
        for g in range(N_GROUPS):
            dt = dt_ref[g]
            acs = _dot3_data_rhs(causal, dt) * a_ref[g]
            _group_cols(dte_ref, g, GROUP_W)[...] = _dot3_data_lhs(dt, expand)
            _group_cols(acs_ref, g, GROUP_W)[...] = _dot3_data_lhs(acs, expand)
            acst_ref[pl.ds(g * HEADS_PER_GROUP, HEADS_PER_GROUP), :] = acs.T[0:HEADS_PER_GROUP]

    blk = pl.BlockSpec((CHUNK, D_INNER), lambda c: (c, 0))
    shp = jax.ShapeDtypeStruct((rows, D_INNER), F32)
    return _call(
        body, name=name, out_shape=(shp, shp, jax.ShapeDtypeStruct((nc, SSM_HEADS, CHUNK), F32)), grid=(nc,),
        in_specs=[pl.BlockSpec((N_GROUPS, CHUNK, LANES), lambda c: (0, c, 0)),
                  pl.BlockSpec((N_GROUPS, 1, LANES), lambda c: (0, 0, 0))],
        out_specs=(blk, blk, pl.BlockSpec((None, SSM_HEADS, CHUNK), lambda c: (c, 0, 0))),
        operands=[dt4, a128], semantics=("parallel",), steps=steps)


def _ssd_common(x_ref, b_ref, c_ref, dte_ref, acs_ref):
    x = x_ref[...]
    dt_exp = dte_ref[...]
    acs_exp = acs_ref[...]
    tot_exp = acs_ref[pl.ds(CHUNK - 1, 1), :]
    xdt = x * dt_exp
    e_exp = jnp.exp(acs_exp)
    f_exp = jnp.exp(tot_exp - acs_exp)
    return _causal_masks(), x, dt_exp, acs_exp, tot_exp, xdt, e_exp, f_exp, b_ref[...], c_ref[...]


def _pair_decay(acs_pair, acs_row, e, causal):
    lane = lax.broadcasted_iota(jnp.int32, (CHUNK, LANES), 1)
    mine = (lane < HEAD_DIM) if e == 0 else (lane >= HEAD_DIM)
    a_l = jnp.where(mine, acs_pair, pltpu.roll(acs_pair, HEAD_DIM, 1))
    seg = a_l - acs_row
    dm = jnp.where(causal[0], jnp.exp(jnp.minimum(seg, 0.0)), 0.0)
    dmt = jnp.where(causal[1], jnp.exp(jnp.minimum(-seg, 0.0)), 0.0)
    return dm, dmt


def _ssd_specs(index_of_chunk):
    wide = pl.BlockSpec((CHUNK, D_INNER), lambda c: (index_of_chunk(c), 0))
    b_spec = pl.BlockSpec((CHUNK, D_BC), lambda c: (index_of_chunk(c), D_INNER // D_BC))
    c_spec = pl.BlockSpec((CHUNK, D_BC), lambda c: (index_of_chunk(c), D_INNER // D_BC + 1))
    rows_spec = pl.BlockSpec((None, SSM_HEADS, CHUNK), lambda c: (index_of_chunk(c), 0, 0))
    state_spec = pl.BlockSpec((N_GROUPS, None, D_STATE, GROUP_W), lambda c: (0, index_of_chunk(c), 0, 0))
    return wide, b_spec, c_spec, rows_spec, state_spec


def _ssd_fwd(name, xbc, dt_exp, acs_exp, acs_rows, dskexp, steps=()):
    rows = xbc.shape[0]
    nc = rows // CHUNK

    def body(x_ref, b_ref, c_ref, dte_ref, acs_ref, acst_ref, dsk_ref, y_ref, st_ref, s_scr):
        @pl.when(pl.program_id(0) == 0)
        def _():
            s_scr[...] = jnp.zeros_like(s_scr)

        lane = lax.broadcasted_iota(jnp.int32, (CHUNK, LANES), 1)
        for g in range(N_GROUPS):
            y_g = _group_cols(y_ref, g, GROUP_W)
            causal, x, _, acs_exp_v, tot_exp, xdt, e_exp, f_exp, bm, cm = _ssd_common(
                _group_cols(x_ref, g, GROUP_W), _group_cols(b_ref, g, D_STATE), _group_cols(c_ref, g, D_STATE),
                _group_cols(dte_ref, g, GROUP_W), _group_cols(acs_ref, g, GROUP_W))
            state = s_scr[g]
            st_ref[g] = state
            cb16, bb16 = cm.astype(BF16), bm.astype(BF16)
            cb = _dot_nt(cb16, bb16)
            base = e_exp * _dot(cb16, state.astype(BF16)) + _group_cols(dsk_ref, g, GROUP_W)[...] * x
            for p in range(HEADS_PER_GROUP // 2):
                sl = slice(p * LANES, (p + 1) * LANES)
                xp = xdt[:, sl].astype(BF16)
                yd = []
                for e in range(2):
                    acs_row = acst_ref[pl.ds(g * HEADS_PER_GROUP + 2 * p + e, 1), :]
                    dm, _ = _pair_decay(acs_exp_v[:, sl], acs_row, e, causal)
                    yd.append(_dot((cb * dm).astype(BF16), xp))
                y_g[:, sl] = jnp.where(lane < HEAD_DIM, yd[0], yd[1]) + base[:, sl]
            s_scr[g] = jnp.exp(tot_exp) * state + _dot_tn(bb16, (f_exp * xdt).astype(BF16))

    wide, b_spec, c_spec, rows_spec, state_spec = _ssd_specs(lambda c: c)
    return _call(
        body, name=name,
        out_shape=(jax.ShapeDtypeStruct((rows, D_INNER), F32),
                   jax.ShapeDtypeStruct((N_GROUPS, nc, D_STATE, GROUP_W), F32)),
        grid=(nc,),
        in_specs=[wide, b_spec, c_spec, wide, wide, rows_spec, pl.BlockSpec((1, D_INNER), lambda c: (0, 0))],
        out_specs=(wide, state_spec),
        scratch_shapes=[pltpu.VMEM((N_GROUPS, D_STATE, GROUP_W), F32)],
        operands=[xbc, xbc, xbc, dt_exp, acs_exp, acs_rows, dskexp], semantics=("arbitrary",), steps=steps)


def _ssd_bwd(name, xbc, dt_exp, acs_exp, acs_rows, dt4, a128, dskexp, dy, states, steps=()):
    rows = xbc.shape[0]
    nc = rows // CHUNK
    last = nc - 1

    def body(x_ref, b_ref, c_ref, dte_ref, acs_ref, acst_ref, dt_all, a128_all, dsk_all, dy_all, st_all,
             dx_all, db_all, dc_all, ddt_all, dalog_all, ddsk_all, ds_all):
        @pl.when(pl.program_id(0) == 0)
        def _():
            ds_all[...] = jnp.zeros_like(ds_all)
            dalog_all[...] = jnp.zeros_like(dalog_all)
            ddsk_all[...] = jnp.zeros_like(ddsk_all)

        for g in range(N_GROUPS):
            group(g, _group_cols(x_ref, g, GROUP_W), _group_cols(b_ref, g, D_STATE), _group_cols(c_ref, g, D_STATE),
                  _group_cols(dte_ref, g, GROUP_W), _group_cols(acs_ref, g, GROUP_W), acst_ref, dt_all.at[g],
                  a128_all.at[g], _group_cols(dsk_all, g, GROUP_W), _group_cols(dy_all, g, GROUP_W), st_all.at[g],
                  _group_cols(dx_all, g, GROUP_W), _group_cols(db_all, g, D_STATE), _group_cols(dc_all, g, D_STATE),
                  ddt_all.at[g], dalog_all.at[g], ddsk_all.at[g], ds_all.at[g])

    def group(g, x_ref, b_ref, c_ref, dte_ref, acs_ref, acst_ref, dt_ref, a128_ref, dsk_ref, dy_ref, st_ref,
              dx_ref, db_ref, dc_ref, ddt_ref, dalog_ref, ddsk_ref, ds_scr):
        causal, x, dt_exp, acs_exp_v, tot_exp, xdt, e_exp, f_exp, bm, cm = _ssd_common(
            x_ref, b_ref, c_ref, dte_ref, acs_ref)
        dt = dt_ref[...]
        reduce_heads = _reduce_heads_matrix()
        state, dstate = st_ref[...], ds_scr[...]
        dyv = dy_ref[...]
        cb16, bb16 = cm.astype(BF16), bm.astype(BF16)
        s16, ds16 = state.astype(BF16), dstate.astype(BF16)
        cb = _dot_nt(cb16, bb16)
        cbt = _dot_nt(bb16, cb16)
        cs = _dot(cb16, s16)
        bds = _dot(bb16, ds16)
        edy = e_exp * dyv
        fx = f_exp * xdt
        dxdt_base = f_exp * bds
        dc_acc = _dot_nt(edy.astype(BF16), s16)
        db_acc = _dot_nt(fx.astype(BF16), ds16)
        ds_scr[...] = jnp.exp(tot_exp) * dstate + _dot_tn(cb16, edy.astype(BF16))
        q = fx * bds
        dacs = _dot3_data_lhs(edy * cs - q, reduce_heads)
        dtot = jnp.sum(_dot3_data_lhs(q + jnp.exp(tot_exp) * dstate * state, reduce_heads), axis=0, keepdims=True)
        ddsk_ref[...] += jnp.sum(_dot3_data_lhs(dyv * x, reduce_heads), axis=0, keepdims=True)
        lane = lax.broadcasted_iota(jnp.int32, (CHUNK, LANES), 1)
        dcb = jnp.zeros((CHUNK, CHUNK), F32)
        dcbt = jnp.zeros((CHUNK, CHUNK), F32)
        ddt_x = jnp.zeros((CHUNK, LANES), F32)
        for p in range(HEADS_PER_GROUP // 2):
            sl = slice(p * LANES, (p + 1) * LANES)
            xp, dyp = xdt[:, sl], dyv[:, sl]
            xp16, dyp16 = xp.astype(BF16), dyp.astype(BF16)
            dxh = []
            for e in range(2):
                h = 2 * p + e
                mine = (lane < HEAD_DIM) if e == 0 else (lane >= HEAD_DIM)
                acs_row = acst_ref[pl.ds(g * HEADS_PER_GROUP + h, 1), :]
                dm, dmt = _pair_decay(acs_exp_v[:, sl], acs_row, e, causal)
                m, mt = cb * dm, cbt * dmt
                xh16 = jnp.where(mine, xp, 0.0).astype(BF16)
                dyh16 = jnp.where(mine, dyp, 0.0).astype(BF16)
                d_m = _dot_nt(dyh16, xp16)
                d_mt = _dot_nt(xh16, dyp16)
                dacs_h = (jnp.sum(d_m * m, axis=-1, keepdims=True)
                          - jnp.sum(d_mt * mt, axis=-1, keepdims=True))
                dacs = dacs + jnp.where(lane == h, dacs_h, 0.0)
                dcb = dcb + d_m * dm
                dcbt = dcbt + d_mt * dmt
                dxh.append(_dot(mt.astype(BF16), dyp16))
            dxdt = jnp.where(lane < HEAD_DIM, dxh[0], dxh[1]) + dxdt_base[:, sl]
            dx_ref[:, sl] = dxdt * dt_exp[:, sl] + dsk_ref[:, sl] * dyp
            ddt_x = ddt_x + _dot3_data_lhs(dxdt * x[:, sl], _reduce_pair_matrix(p))
        dc_ref[...] = dc_acc + _dot(dcb.astype(BF16), bb16)
        db_ref[...] = db_acc + _dot(dcbt.astype(BF16), cb16)
        row = lax.broadcasted_iota(jnp.int32, (CHUNK, LANES), 0)
        dacs = dacs + jnp.where(row == CHUNK - 1, dtot, 0.0)
        da = _dot3_data_rhs(causal[1], dacs)
        ddt_ref[...] = da * a128_ref[...] + ddt_x
        dalog_ref[...] += jnp.sum(da * dt, axis=0, keepdims=True) * a128_ref[...]

    wide, b_spec, c_spec, rows_spec, state_spec = _ssd_specs(lambda c: last - c)
    heads_spec = pl.BlockSpec((N_GROUPS, CHUNK, LANES), lambda c: (0, last - c, 0))
    vec_spec = pl.BlockSpec((N_GROUPS, 1, LANES), lambda c: (0, 0, 0))
    bc_out = pl.BlockSpec((CHUNK, D_BC), lambda c: (last - c, 0))
    vec_shape = jax.ShapeDtypeStruct((N_GROUPS, 1, LANES), F32)
    return _call(
        body, name=name,
        out_shape=(jax.ShapeDtypeStruct((rows, D_INNER), F32), jax.ShapeDtypeStruct((rows, D_BC), F32),
                   jax.ShapeDtypeStruct((rows, D_BC), F32), jax.ShapeDtypeStruct((N_GROUPS, rows, LANES), F32),
                   vec_shape, vec_shape),
        grid=(nc,),
        in_specs=[wide, b_spec, c_spec, wide, wide, rows_spec, heads_spec, vec_spec,
                  pl.BlockSpec((1, D_INNER), lambda c: (0, 0)), wide, state_spec],
        out_specs=(wide, bc_out, bc_out, heads_spec, vec_spec, vec_spec),
        scratch_shapes=[pltpu.VMEM((N_GROUPS, D_STATE, GROUP_W), F32)],
        operands=[xbc, xbc, xbc, dt_exp, acs_exp, acs_rows, dt4, a128, dskexp, dy, states],
        semantics=("arbitrary",), steps=steps)


def _attn_visible(b, heads=1):
    row = jnp.bitwise_and(lax.broadcasted_iota(jnp.int32, (heads * CHUNK, 3 * CHUNK), 0), CHUNK - 1)
    col = lax.broadcasted_iota(jnp.int32, (heads * CHUNK, 3 * CHUNK), 1)
    bb = b + jnp.zeros_like(col)
    meta = (col < CHUNK) & (bb >= 1) & (col >= PAD_ROWS)
    prev = (col >= CHUNK) & (col < 2 * CHUNK) & (bb >= 2) & ((col - CHUNK) > row)
    cur = (col >= 2 * CHUNK) & ((col - 2 * CHUNK) <= row) & ((bb >= 1) | ((col - 2 * CHUNK) >= PAD_ROWS))
    return meta | prev | cur


def _attn_visible4(b):
    return _attn_visible(b, 4)


def _stack_heads(q_ref, sink_ref, kvh, scale):
    lane = lax.broadcasted_iota(jnp.int32, (CHUNK, LANES), 1)
    parts, sinks = [], []
    for pp in range(2):
        pair = kvh * 2 + pp
        qp = q_ref[:, pair * LANES:(pair + 1) * LANES] * scale
        for e in range(2):
            mine = (lane < HEAD_DIM) if e == 0 else (lane >= HEAD_DIM)
            parts.append(jnp.where(mine, qp, 0.0).astype(BF16))
            sinks.append(jnp.full((CHUNK, 1), sink_ref[2 * pair + e], F32))
    return jnp.concatenate(parts, axis=0), jnp.concatenate(sinks, axis=0)


def _attn_operands(q_ref, k0, kp, kc, v0, vp, vc, sink_ref):
    kcat, vcat, q4, sink4 = [], [], [], []
    for kvh in range(N_KV_HEADS):
        ksl = slice(kvh * LANES, (kvh + 1) * LANES)
        kcat.append(jnp.concatenate([k0[:, ksl], kp[:, ksl], kc[:, ksl]], axis=0).astype(BF16))
        vcat.append(jnp.concatenate([v0[:, ksl], vp[:, ksl], vc[:, ksl]], axis=0).astype(BF16))
        stacked, sinks = _stack_heads(q_ref, sink_ref, kvh, ATTN_SCALE)
        q4.append(stacked)
        sink4.append(sinks)
    return kcat, vcat, q4, sink4


def _attn_probs(q4, kcat, visible, sink4):
    heads = range(N_KV_HEADS)
    s = [jnp.where(visible, _dot_nt(q4[h], kcat[h]), NEG_INF) for h in heads]
    m = [jnp.maximum(jnp.max(s[h], axis=-1, keepdims=True), sink4[h]) for h in heads]
    pe = [jnp.exp(s[h] - m[h]) for h in heads]
    pe_sink = [jnp.exp(sink4[h] - m[h]) for h in heads]
    inv = [1.0 / (jnp.sum(pe[h], axis=-1, keepdims=True) + pe_sink[h]) for h in heads]
    return [pe[h] * inv[h] for h in heads], [pe_sink[h] * inv[h] for h in heads]


def _unstack_pairs(stacked, pp):
    lane = lax.broadcasted_iota(jnp.int32, (CHUNK, LANES), 1)
    return jnp.where(lane < HEAD_DIM, stacked[(2 * pp) * CHUNK:(2 * pp + 1) * CHUNK],
                     stacked[(2 * pp + 1) * CHUNK:(2 * pp + 2) * CHUNK])


def _attn_specs(colblock):
    blk = lambda f: pl.BlockSpec((CHUNK, 2 * D_KV), f)
    return [blk(lambda b: (0, colblock)), blk(lambda b: (jnp.maximum(b - 1, 0), colblock)), blk(lambda b: (b, colblock))]


def _attn_fwd(name, q, kv2, sinks, steps=()):
    rows = q.shape[0]

    def body(q_ref, k0, kp, kc, v0, vp, vc, sink_ref, o_ref):
        visible = _attn_visible4(pl.program_id(0))
        kcat, vcat, q4, sink4 = _attn_operands(q_ref, k0, kp, kc, v0, vp, vc, sink_ref)
        pn, _ = _attn_probs(q4, kcat, visible, sink4)
        o4 = [_dot(pn[h].astype(BF16), vcat[h]) for h in range(N_KV_HEADS)]
        for kvh in range(N_KV_HEADS):
            for pp in range(2):
                qsl = slice((kvh * 2 + pp) * LANES, (kvh * 2 + pp + 1) * LANES)
                o_ref[:, qsl] = _unstack_pairs(o4[kvh], pp).astype(BF16)

    return _call(
        body, name=name, out_shape=jax.ShapeDtypeStruct((rows, D_MODEL), BF16), grid=(rows // CHUNK,),
        in_specs=[pl.BlockSpec((CHUNK, D_MODEL), lambda b: (b, 0))] + _attn_specs(0) + _attn_specs(1)
        + [pl.BlockSpec(memory_space=pltpu.SMEM)],
        out_specs=pl.BlockSpec((CHUNK, D_MODEL), lambda b: (b, 0)),
        operands=[q, kv2, kv2, kv2, kv2, kv2, kv2, sinks], semantics=("parallel",), steps=steps)


def _attn_bwd(name, q, kv2, sinks, do, steps=()):
    rows = q.shape[0]

    def body(q_ref, k0, kp, kc, v0, vp, vc, sink_ref, do_ref,
             dq_ref, dkc_ref, dkp_ref, dvc_ref, dvp_ref, dkm_ref, dvm_ref, dsink_ref):
        @pl.when(pl.program_id(0) == 0)
        def _():
            dkm_ref[...] = jnp.zeros_like(dkm_ref)
            dvm_ref[...] = jnp.zeros_like(dvm_ref)
            dsink_ref[...] = jnp.zeros_like(dsink_ref)

        visible = _attn_visible4(pl.program_id(0))
        heads = range(N_KV_HEADS)
        lane1 = lax.broadcasted_iota(jnp.int32, (1, LANES), 1)
        kcat, vcat, q4, sink4 = _attn_operands(q_ref, k0, kp, kc, v0, vp, vc, sink_ref)
        do4 = [_stack_heads(do_ref, sink_ref, h, 1.0)[0] for h in heads]
        pn, psink = _attn_probs(q4, kcat, visible, sink4)
        dp = [_dot_nt(do4[h], vcat[h]) for h in heads]
        delta = [jnp.sum(pn[h] * dp[h], axis=-1, keepdims=True) for h in heads]
        ds16 = [(pn[h] * (dp[h] - delta[h])).astype(BF16) for h in heads]
        dq4 = [_dot(ds16[h], kcat[h]) for h in heads]
        dk_acc = [_dot_tn(ds16[h], q4[h]) for h in heads]
        dv_acc = [_dot_tn(pn[h].astype(BF16), do4[h]) for h in heads]
        dsink = jnp.zeros((1, LANES), F32)
        for kvh in heads:
            ksl = slice(kvh * LANES, (kvh + 1) * LANES)
            sink_terms = psink[kvh] * delta[kvh]
            for j in range(4):
                part = jnp.sum(sink_terms[j * CHUNK:(j + 1) * CHUNK], axis=0, keepdims=True)
                dsink = dsink - jnp.where(lane1 == kvh * 4 + j, part, 0.0)
            for pp in range(2):
                qsl = slice((kvh * 2 + pp) * LANES, (kvh * 2 + pp + 1) * LANES)
                dq_ref[:, qsl] = (_unstack_pairs(dq4[kvh], pp) * ATTN_SCALE).astype(BF16)
            dkm_ref[:, ksl] += dk_acc[kvh][0:CHUNK]
            dvm_ref[:, ksl] += dv_acc[kvh][0:CHUNK]
            dkp_ref[:, ksl] = dk_acc[kvh][CHUNK:2 * CHUNK]
            dvp_ref[:, ksl] = dv_acc[kvh][CHUNK:2 * CHUNK]
            dkc_ref[:, ksl] = dk_acc[kvh][2 * CHUNK:3 * CHUNK]
            dvc_ref[:, ksl] = dv_acc[kvh][2 * CHUNK:3 * CHUNK]
        dsink_ref[...] += dsink

    qspec = pl.BlockSpec((CHUNK, D_MODEL), lambda b: (b, 0))
    kvspec = pl.BlockSpec((CHUNK, 2 * D_KV), lambda b: (b, 0))
    fixed = pl.BlockSpec((CHUNK, 2 * D_KV), lambda b: (0, 0))
    kv_shape = jax.ShapeDtypeStruct((rows, 2 * D_KV), F32)
    meta_shape = jax.ShapeDtypeStruct((CHUNK, 2 * D_KV), F32)
    return _call(
        body, name=name,
        out_shape=(jax.ShapeDtypeStruct((rows, D_MODEL), BF16), kv_shape, kv_shape, kv_shape, kv_shape,
                   meta_shape, meta_shape, jax.ShapeDtypeStruct((1, LANES), F32)),
        grid=(rows // CHUNK,),
        in_specs=[qspec] + _attn_specs(0) + _attn_specs(1) + [pl.BlockSpec(memory_space=pltpu.SMEM), qspec],
        out_specs=(qspec, kvspec, kvspec, kvspec, kvspec, fixed, fixed, pl.BlockSpec((1, LANES), lambda b: (0, 0))),
        operands=[q, kv2, kv2, kv2, kv2, kv2, kv2, sinks, do], semantics=("arbitrary",), steps=steps)


def _kv_grad_combine(name, dk_cur, dk_prev, dk_meta, dv_cur, dv_prev, dv_meta):
    rows = dk_cur.shape[0]
    nb = rows // CHUNK
    width = 2 * D_KV

    def body(kc_ref, kp_ref, km_ref, vc_ref, vp_ref, vm_ref, o_ref):
        jj = pl.program_id(0) + jnp.zeros((CHUNK, 1), jnp.int32)
        for half, (c_ref, p_ref, m_ref) in enumerate(((kc_ref, kp_ref, km_ref), (vc_ref, vp_ref, vm_ref))):
            total = c_ref[...] + jnp.where(jj < nb - 1, p_ref[...], 0.0) + jnp.where(jj == 0, m_ref[...], 0.0)
            o_ref[:, half * width:(half + 1) * width] = total.astype(BF16)

    blk = lambda f: pl.BlockSpec((CHUNK, width), f)
    three = lambda: [blk(lambda j: (j, 0)), blk(lambda j: (jnp.minimum(j + 1, nb - 1), 0)), blk(lambda j: (0, 0))]
    return pl.pallas_call(
        body, name=name, out_shape=jax.ShapeDtypeStruct((rows, 2 * width), BF16), grid=(nb,),
        in_specs=three() + three(), out_specs=pl.BlockSpec((CHUNK, 2 * width), lambda j: (j, 0)),
        compiler_params=_cparams(("parallel",)),
    )(dk_cur, dk_prev, dk_meta, dv_cur, dv_prev, dv_meta)


def _loss_head(name, h, target):
    rows = h.shape[0]

    def body(h_ref, t_ref, dh_ref, loss_ref):
        i = pl.program_id(0)
        real = (i + jnp.zeros((CHUNK, 1), jnp.int32)) >= 1
        diff = jnp.where(real, h_ref[...] - t_ref[...], 0.0)
        dh_ref[...] = diff * (1.0 / D_MODEL)

        @pl.when(i == 0)
        def _():
            loss_ref[...] = jnp.zeros_like(loss_ref)

        loss_ref[...] += jnp.sum(diff * diff) * (0.5 / D_MODEL)

    blk = pl.BlockSpec((CHUNK, D_MODEL), lambda i: (i, 0))
    return pl.pallas_call(
        body, name=name,
        out_shape=(jax.ShapeDtypeStruct((rows, D_MODEL), F32), jax.ShapeDtypeStruct((1, LANES), F32)),
        grid=(rows // CHUNK,),
        in_specs=[blk, pl.BlockSpec((CHUNK, D_MODEL), lambda i: (jnp.maximum(i - 1, 0), 0))],
        out_specs=(blk, pl.BlockSpec((1, LANES), lambda i: (0, 0))), compiler_params=_cparams(("arbitrary",)),
    )(h, target)


def _adamw(name, w, g, m, v):
    rows, width = w.shape
    tr = rows
    for cand in range(8, rows + 1, 8):
        if rows % cand == 0 and cand * width * 4 <= (1 << 20):
            tr = cand

    def body(w_ref, g_ref, m_ref, v_ref, d_ref, mo_ref, vo_ref):
        gv = g_ref[...]
        mn = ADAM_B1 * m_ref[...] + (1.0 - ADAM_B1) * gv
        vn = ADAM_B2 * v_ref[...] + (1.0 - ADAM_B2) * (gv * gv)
        m_hat = mn / (1.0 - ADAM_B1 ** ADAM_STEP)
        v_hat = vn / (1.0 - ADAM_B2 ** ADAM_STEP)
        d_ref[...] = -ADAM_LR * (m_hat / (jnp.sqrt(v_hat) + ADAM_EPS) + ADAM_WD * w_ref[...])
        mo_ref[...] = mn
        vo_ref[...] = vn

    blk = pl.BlockSpec((tr, width), lambda i: (i, 0))
    shp = jax.ShapeDtypeStruct((rows, width), F32)
    return pl.pallas_call(
        body, name=name, out_shape=(shp, shp, shp), grid=(rows // tr,), in_specs=[blk] * 4, out_specs=(blk,) * 3,
        compiler_params=_cparams(("parallel",)),
    )(w, g, m, v)


class _GivenWeights:
    def __init__(self, p):
        self.p = p
        self.grads = {}

    def weight(self, name, layer=None):
        return self.p[name] if layer is None else self.p[name][layer]

    def steps(self, kernel):
        return ()

    def grad(self, name, layer, g):
        self.grads[(name, layer)] = g


def _ffn_fwd(tag, h, p, i, plan):
    hn = _rms_fwd(f"ffn{tag}_norm", h, p["f_norm_pre"][i:i + 1])
    up = _mm_nn_bychip(f"ffn{tag}_up", hn, plan.weight("f_w_up", i))
    act = _ffn_conv_fwd(f"ffn{tag}_conv", up, p["f_conv_w"][i], p["f_conv_b"][i:i + 1], steps=plan.steps(f"ffn{tag}_conv"))
    pre = _mm(f"ffn{tag}_down", act, plan.weight("f_w_down", i), "nn")
    h_new = _resid_norm_fwd(f"ffn{tag}_resid", h, pre, p["f_norm_post"][i:i + 1])
    return h_new, (h, hn, up, act, pre)


def _ffn_bwd(tag, dh, saved, p, i, plan):
    h, hn, up, act, pre = saved
    dpre, g_post = _resid_norm_bwd(f"ffn{tag}_resid_bwd", dh, pre, p["f_norm_post"][i:i + 1])
    plan.grad("f_w_down", i, _mm(f"ffn{tag}_down_dw", act, dpre, "tn", out_dtype=BF16))
    dact = _mm(f"ffn{tag}_down_dx", dpre, plan.weight("f_w_down", i), "nt", steps=plan.steps(f"ffn{tag}_down_dx"))
    dug, duv, gwg, gwv, gbg, gbv = _ffn_conv_bwd(f"ffn{tag}_conv_bwd", up, dact, p["f_conv_w"][i], p["f_conv_b"][i:i + 1],
                                                 steps=plan.steps(f"ffn{tag}_conv_bwd"))
    g_cw, g_cb = jnp.concatenate([gwg, gwv], axis=1), jnp.concatenate([gbg, gbv], axis=1)
    w_up = plan.weight("f_w_up", i)
    n = w_up.shape[2]
    dhn = _mm_nt_bychip(f"ffn{tag}_up_dx_gate", dug, w_up, 0)
    dhn = _mm_nt_bychip(f"ffn{tag}_up_dx_val", duv, w_up, N_CHIPS // 2, acc=dhn)
    g_up = _mm_tn_bychip(f"ffn{tag}_up_dw_gate", hn, dug, n, 0)
    plan.grad("f_w_up", i, _mm_tn_bychip(f"ffn{tag}_up_dw_val", hn, duv, n, N_CHIPS // 2, into=g_up))
    dh_new, g_pre = _norm_bwd_add(f"ffn{tag}_norm_bwd", dh, dhn, h, p["f_norm_pre"][i:i + 1])
    return dh_new, dict(f_norm_post=g_post, f_conv_w=g_cw, f_conv_b=g_cb, f_norm_pre=g_pre)


def _lanes_pad(a, width=LANES):
    return jnp.pad(a, [(0, 0)] * (a.ndim - 1) + [(0, width - a.shape[-1])])


def _dup_heads(w):
    rows = w.shape[0]
    w = w.reshape(rows, 2 * N_KV_HEADS, 1, HEAD_DIM)
    return jnp.broadcast_to(w, (rows, 2 * N_KV_HEADS, 2, HEAD_DIM)).reshape(rows, 4 * D_KV)


def _undup_heads(g):
    rows = g.shape[0]
    return g.reshape(rows, 2 * N_KV_HEADS, 2, HEAD_DIM).sum(axis=2).reshape(rows, 2 * D_KV)


def _local_step(x2, target, p, plan):
    seq = x2.shape[0]
    rows = seq + CHUNK
    g = {}

    h0 = jnp.concatenate([jnp.zeros((PAD_ROWS, D_MODEL), F32), p["meta_tokens"], x2], axis=0)

    w_in = plan.weight("a_w_in")
    w_dt = jnp.pad(w_in[D_MAIN:], ((0, LANES - SSM_HEADS), (0, 0)))
    dt_bias = _lanes_pad(p["a_dt_bias"])
    a_neg = -jnp.exp(p["a_a_log"].reshape(N_GROUPS, HEADS_PER_GROUP))
    a128 = _lanes_pad(a_neg.reshape(N_GROUPS, 1, HEADS_PER_GROUP))
    dskexp = jnp.repeat(p["a_d_skip"].reshape(SSM_HEADS), HEAD_DIM).reshape(1, D_INNER)

    hn0 = _rms_fwd("a_norm", h0, p["a_norm_pre"])
    zx = _mm("a_in_main", hn0, w_in, "nt", k_rows=D_MAIN, steps=plan.steps("a_in_main"))
    dtr = _mm("a_in_dt", hn0, w_dt, "nt")
    xbc = _conv4_fwd("a_conv", zx, p["a_conv_w"], p["a_conv_b"], steps=plan.steps("a_conv"))
    dt = _dt_fwd("a_dt", dtr, dt_bias)
    dt4 = _lanes_pad(dt[:, :SSM_HEADS].reshape(rows, N_GROUPS, HEADS_PER_GROUP).transpose(1, 0, 2))
    dt_exp, acs_exp, acs_rows = _ssd_prep("a_ssd_prep", dt4, a128, steps=plan.steps("a_ssd_prep"))
    y, states = _ssd_fwd("a_ssd", xbc, dt_exp, acs_exp, acs_rows, dskexp, steps=plan.steps("a_ssd"))
    yn = _gate_fwd("a_gate", y, zx, p["a_gate_norm"], steps=plan.steps("a_gate"))
    mix = _mm("a_out", yn, plan.weight("a_w_out"), "nn")
    h1 = _resid_norm_fwd("a_resid", h0, mix, p["a_norm_post"])

    h2, ffn0 = _ffn_fwd("0", h1, p, 0, plan)

    hkv = _rms_fwd("kv_norm", h2, p["kv_norm"])
    w_kv2 = _dup_heads(plan.weight("w_kv"))
    kv2 = _mm("kv_proj", hkv, w_kv2, "nn")
    hn2 = _rms_fwd("b_norm", h2, p["b_norm_pre"])
    q = _mm("b_q", hn2, plan.weight("b_w_q"), "nn")
    sinks = p["b_sinks"].reshape(N_Q_HEADS)
    o = _attn_fwd("b_attn", q, kv2, sinks, steps=plan.steps("b_attn"))
    attn = _mm("b_o", o, plan.weight("b_w_o"), "nn", steps=plan.steps("b_o"))
    h3 = _resid_norm_fwd("b_resid", h2, attn, p["b_norm_post"])

    h4, ffn1 = _ffn_fwd("1", h3, p, 1, plan)

    dh, loss_vec = _loss_head("loss", h4, target)
    loss = loss_vec[0, 0]

    dh, g1 = _ffn_bwd("1", dh, ffn1, p, 1, plan)

    dpre, g["b_norm_post"] = _resid_norm_bwd("b_resid_bwd", dh, attn, p["b_norm_post"])
    plan.grad("b_w_o", None, _mm("b_o_dw", o, dpre, "tn", out_dtype=BF16))
    do = _mm("b_o_dx", dpre, plan.weight("b_w_o"), "nt", steps=plan.steps("b_o_dx"))
    dq, dkc, dkp, dvc, dvp, dkm, dvm, dsink = _attn_bwd("b_attn_bwd", q, kv2, sinks, do, steps=plan.steps("b_attn_bwd"))
    g["b_sinks"] = dsink[:, :N_Q_HEADS]
    dhn2 = _mm("b_q_dx", dq, plan.weight("b_w_q"), "nt")
    plan.grad("b_w_q", None, _mm("b_q_dw", hn2, dq, "tn", out_dtype=BF16))
    dh, g["b_norm_pre"] = _norm_bwd_add("b_norm_bwd", dh, dhn2, h2, p["b_norm_pre"])
    dkv2 = _kv_grad_combine("kv_grad", dkc, dkp, dkm, dvc, dvp, dvm)
    dhkv = _mm("kv_proj_dx", dkv2, w_kv2, "nt")
    plan.grad("w_kv", None, _undup_heads(_mm("kv_proj_dw", hkv, dkv2, "tn")))
    dh, g["kv_norm"] = _norm_bwd_add("kv_norm_bwd", dh, dhkv, h2, p["kv_norm"])

    dh, g0 = _ffn_bwd("0", dh, ffn0, p, 0, plan)
    for name in g0:
        if g0[name].shape[0] == 1:
            g[name] = jnp.concatenate([g0[name], g1[name]], axis=0)
        else:
            g[name] = jnp.stack([g0[name], g1[name]])

    dpre, g["a_norm_post"] = _resid_norm_bwd("a_resid_bwd", dh, mix, p["a_norm_post"])
    plan.grad("a_w_out", None, _mm("a_out_dw", yn, dpre, "tn", out_dtype=BF16))
    dyn = _mm("a_out_dx", dpre, plan.weight("a_w_out"), "nt", steps=plan.steps("a_out_dx"))
    dy, dz, g["a_gate_norm"] = _gate_bwd("a_gate_bwd", dyn, y, zx, p["a_gate_norm"])
    dxs, dbm, dcm, ddt4, dalog, ddsk = _ssd_bwd("a_ssd_bwd", xbc, dt_exp, acs_exp, acs_rows, dt4, a128, dskexp, dy, states,
                                               steps=plan.steps("a_ssd_bwd"))
    g["a_a_log"] = dalog[:, 0, :HEADS_PER_GROUP].reshape(1, SSM_HEADS)
    g["a_d_skip"] = ddsk[:, 0, :HEADS_PER_GROUP].reshape(1, SSM_HEADS)
    ddt = _lanes_pad(ddt4[:, :, :HEADS_PER_GROUP].transpose(1, 0, 2).reshape(rows, SSM_HEADS))
    ddtr, dbias = _dt_bwd("a_dt_bwd", ddt, dtr, dt_bias)
    g["a_dt_bias"] = dbias[:, :SSM_HEADS]
    dxp, gw_x, gb_x = _conv4_bwd("a_conv_bwd_x", zx, dxs, p["a_conv_w"], p["a_conv_b"], 0)
    dbp, gw_b, gb_b = _conv4_bwd("a_conv_bwd_b", zx, dbm, p["a_conv_w"], p["a_conv_b"], D_INNER)
    dcp, gw_c, gb_c = _conv4_bwd("a_conv_bwd_c", zx, dcm, p["a_conv_w"], p["a_conv_b"], D_INNER + D_BC)
    g["a_conv_w"] = jnp.concatenate([gw_x, gw_b, gw_c], axis=1)
    g["a_conv_b"] = jnp.concatenate([gb_x, gb_b, gb_c], axis=1)
    dzx = jnp.concatenate([dz, dxp, dbp, dcp], axis=1)
    g_main = _mm("a_in_main_dw", dzx, hn0, "tn", out_dtype=BF16, steps=plan.steps("a_in_main_dw"))
    g_dt = _mm("a_in_dt_dw", ddtr, hn0, "tn", out_dtype=BF16)
    plan.grad("a_w_in", None, jnp.concatenate([g_main, g_dt[:SSM_HEADS]], axis=0))
    dhn0 = _mm("a_in_dt_dx", ddtr, w_dt, "nn", steps=plan.steps("a_in_dt_dx"))
    dhn0 = _mm("a_in_main_dx", dzx, w_in, "nn", acc=dhn0, steps=plan.steps("a_in_main_dx"))
    dh, g["a_norm_pre"] = _norm_bwd_add("a_norm_bwd", dh, dhn0, h0, p["a_norm_pre"])

    g["meta_tokens"] = dh[PAD_ROWS:CHUNK]
    return loss, dh[CHUNK:], g


ANY = pl.BlockSpec(memory_space=pl.ANY)
VMEM_SPEC = pl.BlockSpec(memory_space=pltpu.VMEM)


def _allgather_small(name, shard):
    rows = shard.shape[0]

    def body(s_ref, o_ref, send_sems, recv_sems):
        x, y, c = _place()
        me = 2 * x + y
        o_ref[me] = s_ref[...]
        chips = _other_chips(x, y)
        sends = [pltpu.make_async_remote_copy(s_ref, o_ref.at[me], send_sems.at[j], recv_sems.at[j],
                                              device_id=(cx, cy, c), device_id_type=MESH)
                 for j, (cx, cy) in enumerate(chips)]
        for cp in sends:
            cp.start()
        for j, (cx, cy) in enumerate(chips):
            pltpu.make_async_remote_copy(s_ref, o_ref.at[2 * cx + cy], send_sems.at[j], recv_sems.at[j],
                                         device_id=(cx, cy, c), device_id_type=MESH).wait_recv()
        for cp in sends:
            cp.wait_send()

    return pl.pallas_call(
        body, name=name, out_shape=jax.ShapeDtypeStruct((N_CHIPS, rows, LANES), F32),
        in_specs=[VMEM_SPEC], out_specs=VMEM_SPEC,
        scratch_shapes=[pltpu.SemaphoreType.DMA((3,)), pltpu.SemaphoreType.DMA((3,))],
        compiler_params=pltpu.CompilerParams(vmem_limit_bytes=VMEM_LIMIT),
    )(shard)


def _row_block(rows, width, itemsize, align, budget=2 << 20):
    best = rows
    for cand in range(align, rows + 1, align):
        if rows % cand == 0 and cand * width * itemsize <= budget:
            best = cand
    return best


def _cast_into_slot(name, chip, w2d):
    rows, width = w2d.shape
    tr = _row_block(rows, width, 4, 16)

    def body(chip_ref, w_ref, o_ref):
        o_ref[...] = w_ref[...].astype(BF16)

    return pl.pallas_call(
        body, name=name, out_shape=jax.ShapeDtypeStruct((N_CHIPS, rows, width), BF16),
        grid_spec=pltpu.PrefetchScalarGridSpec(
            num_scalar_prefetch=1, grid=(rows // tr,),
            in_specs=[pl.BlockSpec((tr, width), lambda i, chip_ref: (i, 0))],
            out_specs=pl.BlockSpec((None, tr, width), lambda i, chip_ref: (chip_ref[0], i, 0))),
        compiler_params=_cparams(("parallel",)),
    )(chip, w2d)


def _allreduce_small(name, vec):
    rows = vec.shape[0]

    def body(v_ref, o_ref, buf, send_sems, recv_sems):
        x, y, c = _place()
        me = 4 * x + 2 * y + c
        buf[me] = v_ref[...]

        def peer(k):
            kx, ky, kc = (k >> 2) & 1, (k >> 1) & 1, k & 1
            return (1 - x if kx else x, 1 - y if ky else y, 1 - c if kc else c)

        sends = []
        for k in range(1, N_DEV):
            cp = pltpu.make_async_remote_copy(v_ref, buf.at[me], send_sems.at[k - 1], recv_sems.at[k - 1],
                                              device_id=peer(k), device_id_type=MESH)
            cp.start()
            sends.append(cp)
        for k in range(1, N_DEV):
            px, py, pc = peer(k)
            pltpu.make_async_remote_copy(v_ref, buf.at[4 * px + 2 * py + pc], send_sems.at[k - 1], recv_sems.at[k - 1],
                                         device_id=(px, py, pc), device_id_type=MESH).wait_recv()
        for cp in sends:
            cp.wait_send()
        acc = buf[0]
        for d in range(1, N_DEV):
            acc = acc + buf[d]
        o_ref[...] = acc

    return pl.pallas_call(
        body, name=name, out_shape=jax.ShapeDtypeStruct((rows, LANES), F32),
        in_specs=[VMEM_SPEC], out_specs=VMEM_SPEC,
        scratch_shapes=[pltpu.VMEM((N_DEV, rows, LANES), F32), pltpu.SemaphoreType.DMA((N_DEV - 1,)),
                        pltpu.SemaphoreType.DMA((N_DEV - 1,))],
        compiler_params=pltpu.CompilerParams(vmem_limit_bytes=VMEM_LIMIT),
    )(vec)


def _rs_pair_add(name, place, grads, partner, split="rows"):
    _, half_rows, width = partner.shape
    tr = _row_block(half_rows, width, 2, 16)
    nb = half_rows // tr
    if split == "rows":
        mine = pl.BlockSpec((None, tr, width), lambda s, i, pr: (s, pr[1] * nb + i, 0))
    else:
        mine = pl.BlockSpec((None, tr, width), lambda s, i, pr: (s, i, pr[1]))

    def body(place_ref, g_ref, p_ref, o_ref):
        o_ref[...] = (g_ref[...].astype(F32) + p_ref[...].astype(F32)).astype(BF16)

    return pl.pallas_call(
        body, name=name, out_shape=jax.ShapeDtypeStruct(partner.shape, BF16),
        grid_spec=pltpu.PrefetchScalarGridSpec(
            num_scalar_prefetch=1, grid=(N_CHIPS, nb),
            in_specs=[mine, pl.BlockSpec((None, tr, width), lambda s, i, pr: (s, i, 0))],
            out_specs=pl.BlockSpec((None, tr, width), lambda s, i, pr: (s, i, 0))),
        compiler_params=_cparams(("parallel", "parallel")),
    )(place, grads, partner)


def _rs_chip_add(name, place, mine, others, split="rows"):
    _, half_rows, width = mine.shape
    tr = _row_block(half_rows, width, 4, 16, budget=1 << 20)
    nb = half_rows // tr
    if split == "rows":
        out_shape, out_spec = (2 * half_rows, width), pl.BlockSpec((tr, width), lambda i, pr: (pr[1] * nb + i, 0))
    else:
        out_shape, out_spec = (half_rows, 2 * width), pl.BlockSpec((tr, width), lambda i, pr: (i, pr[1]))

    def body(place_ref, q_ref, r_ref, o_ref):
        acc = q_ref[...].astype(F32)
        for j in range(3):
            acc = acc + r_ref[j].astype(F32)
        o_ref[...] = acc

    return pl.pallas_call(
        body, name=name, out_shape=jax.ShapeDtypeStruct(out_shape, F32),
        grid_spec=pltpu.PrefetchScalarGridSpec(
            num_scalar_prefetch=1, grid=(nb,),
            in_specs=[pl.BlockSpec((None, tr, width), lambda i, pr: (pr[0], i, 0)),
                      pl.BlockSpec((3, tr, width), lambda i, pr: (0, i, 0))],
            out_specs=out_spec),
        compiler_params=_cparams(("parallel",)),
    )(place, mine, others)


WEIGHTS = ["meta_tokens", "a_norm_pre", "a_w_in", "a_conv_w", "a_conv_b", "a_dt_bias", "a_a_log", "a_d_skip",
           "a_gate_norm", "a_w_out", "a_norm_post", "kv_norm", "w_kv", "b_norm_pre", "b_w_q", "b_sinks", "b_w_o",
           "b_norm_post", "f_norm_pre", "f_w_up", "f_conv_w", "f_conv_b", "f_w_down", "f_norm_post"]
FULL_SHAPE = {
    "meta_tokens": (16, 1024), "a_norm_pre": (1, 1024), "a_w_in": (1, 1024, 5152), "a_conv_w": (1, 4, 3072),
    "a_conv_b": (1, 3072), "a_dt_bias": (1, 32), "a_a_log": (1, 32), "a_d_skip": (1, 32), "a_gate_norm": (1, 2048),
    "a_w_out": (1, 2048, 1024), "a_norm_post": (1, 1024), "kv_norm": (1024,), "w_kv": (1024, 512),
    "b_norm_pre": (1, 1024), "b_w_q": (1, 1024, 1024), "b_sinks": (1, 16), "b_w_o": (1, 1024, 1024),
    "b_norm_post": (1, 1024), "f_norm_pre": (2, 1024), "f_w_up": (2, 1024, 5632), "f_conv_w": (2, 3, 5632),
    "f_conv_b": (2, 5632), "f_w_down": (2, 2816, 1024), "f_norm_post": (2, 1024),
}
SHARD_AXIS = {
    "meta_tokens": 1, "a_norm_pre": 1, "a_w_in": 2, "a_conv_w": 2, "a_conv_b": 1, "a_dt_bias": None, "a_a_log": None,
    "a_d_skip": None, "a_gate_norm": 1, "a_w_out": 1, "a_norm_post": 1, "kv_norm": None, "w_kv": 0, "b_norm_pre": None,
    "b_w_q": 1, "b_sinks": None, "b_w_o": 1, "b_norm_post": None, "f_norm_pre": None, "f_w_up": 2, "f_conv_w": 2,
    "f_conv_b": None, "f_w_down": 1, "f_norm_post": None,
}
BIG = ["a_w_in", "a_w_out", "w_kv", "b_w_q", "b_w_o", "f_w_up", "f_w_down"]
SMALL = [n for n in WEIGHTS if n not in BIG]
SMALL_SHARDED = [n for n in SMALL if SHARD_AXIS[n] is not None]


def _shard_shape(name):
    shape = list(FULL_SHAPE[name])
    if SHARD_AXIS[name] is not None:
        shape[SHARD_AXIS[name]] //= N_CHIPS
    return tuple(shape)


def _numel(shape):
    return int(math.prod(shape))


SUBLANES = 8


def _packed_rows(shape):
    rows = -(-_numel(shape) // LANES)
    return -(-rows // SUBLANES) * SUBLANES


def _pack(arrays):
    parts = []
    for a in arrays:
        size, rows = _numel(a.shape), _packed_rows(a.shape)
        if size % LANES == 0:
            part = jnp.pad(a.reshape(size // LANES, LANES), ((0, rows - size // LANES), (0, 0)))
        else:
            part = jnp.pad(a.reshape(-1), (0, rows * LANES - size)).reshape(rows, LANES)
        parts.append(part)
    return jnp.concatenate(parts, axis=0)


def _unpack(packed, names, shape_of):
    out, off = {}, 0
    lead = packed.shape[:-2]
    for n in names:
        shape = tuple(shape_of(n))
        size, rows = _numel(shape), _packed_rows(shape)
        part = packed[..., off:off + rows, :]
        if size % LANES == 0:
            out[n] = part[..., :size // LANES, :].reshape(lead + shape)
        else:
            out[n] = part.reshape(lead + (rows * LANES,))[..., :size].reshape(lead + shape)
        off += rows
    return out


def _split_chips(name, full):
    ax = SHARD_AXIS[name]
    shape = full.shape
    cut = shape[:ax] + (N_CHIPS, shape[ax] // N_CHIPS) + shape[ax + 1:]
    return jnp.moveaxis(full.reshape(cut), ax, 0)


def _join_chips(name, stacked):
    ax = SHARD_AXIS[name]
    moved = jnp.moveaxis(stacked, 0, ax)
    shape = moved.shape
    return moved.reshape(shape[:ax] + (shape[ax] * shape[ax + 1],) + shape[ax + 2:])


def _as2d(a):
    return a.reshape(-1, a.shape[-1])


BUFFERS = [("a_w_in", "a_w_in", None), ("a_w_out", "a_w_out", None), ("w_kv", "w_kv", None),
           ("b_w_q", "b_w_q", None), ("b_w_o", "b_w_o", None), ("f_w_up0", "f_w_up", 0), ("f_w_up1", "f_w_up", 1),
           ("f_w_down0", "f_w_down", 0), ("f_w_down1", "f_w_down", 1)]


TRANSPOSED = ("a_w_in",)
SPLIT = {"a_w_in": "cols"}


def _local_shard(arrays, weight, layer):
    if weight in TRANSPOSED:
        return arrays[weight][0].T
    return _as2d(arrays[weight]) if layer is None else arrays[weight][layer]


def _weight_from_gathered(weight, buf):
    if weight == "f_w_up":
        return buf
    return buf.reshape(N_CHIPS * buf.shape[1], buf.shape[2])


def _gathered_from_grad(weight, g):
    if weight == "f_w_up":
        return g
    return g.reshape(N_CHIPS, g.shape[0] // N_CHIPS, g.shape[1]).astype(BF16)


GATHER_SCHEDULE = {
    "a_in_main": [("ici", ["a_w_out"])],
    "a_conv": [("d2d", ["a_w_out"]), ("ici", ["f_w_down0"])],
    "a_ssd_prep": [("d2d", ["f_w_down0"]), ("ici", ["w_kv", "b_w_q", "b_w_o"])],
    "a_ssd": [("d2d", ["w_kv", "b_w_q", "b_w_o"]), ("ici", ["f_w_up0"])],
    "a_gate": [("d2d", ["f_w_up0"])],
    "ffn0_conv": [("ici", ["f_w_down1"])],
    "b_attn": [("d2d", ["f_w_down1"]), ("ici", ["f_w_up1"])],
    "b_o": [("d2d", ["f_w_up1"])],
}
REDUCE_SCHEDULE = {
    "ffn1_conv_bwd": ["f_w_down1"],
    "b_attn_bwd": ["f_w_up1", "b_w_o"],
    "ffn0_conv_bwd": ["b_w_q", "w_kv", "f_w_down0"],
    "a_ssd_bwd": ["f_w_up0", "a_w_out"],
    "a_in_main_dx": ["a_w_in"],
}
PAIR_SCHEDULE = {
    "ffn1_down_dx": ["f_w_down1"],
    "b_o_dx": ["f_w_up1", "b_w_o"],
    "ffn0_down_dx": ["b_w_q", "w_kv", "f_w_down0"],
    "a_out_dx": ["f_w_up0", "a_w_out"],
    "a_in_dt_dx": ["a_w_in"],
}
SWAP_SCHEDULE = {"a_in_main_dw": ["f_w_down1", "f_w_up1", "b_w_o", "b_w_q", "w_kv", "f_w_down0", "f_w_up0", "a_w_out"]}


def _buffer_of(weight, layer):
    return weight if layer is None else f"{weight}{layer}"


class _Pipeline:
    def __init__(self, place, slots):
        self.place = place
        self.slots = dict(slots)
        self.running = []
        self.grads = {}
        self.theirs = {}
        self.partials = {}
        self.peers = {}
        self.reduced = {}

    def _collect(self):
        for step, buffers, table in self.running:
            table.update(zip(buffers, step.results))
        self.running = []

    @staticmethod
    def _splits(buffers):
        return [SPLIT.get(b, "rows") for b in buffers]

    def gather_now(self, name, buffers):
        step = _step_gather_full([self.slots[b] for b in buffers], self._splits(buffers))
        _run_steps(name, [step])
        self.slots.update(zip(buffers, step.results))

    def weight(self, name, layer=None):
        self._collect()
        return _weight_from_gathered(name, self.slots[_buffer_of(name, layer)])

    def grad(self, name, layer, g):
        self.grads[_buffer_of(name, layer)] = _gathered_from_grad(name, g)

    def steps(self, kernel):
        self._collect()
        steps = []
        for phase, buffers in GATHER_SCHEDULE.get(kernel, []):
            make = _step_gather_ici if phase == "ici" else _step_gather_d2d
            step = make([self.slots[b] for b in buffers], self._splits(buffers))
            self.running.append((step, buffers, self.slots))
            steps.append(step)
        buffers = PAIR_SCHEDULE.get(kernel)
        if buffers:
            step = _step_pair_exchange([self.grads[b] for b in buffers], self._splits(buffers))
            self.running.append((step, buffers, self.theirs))
            steps.append(step)
        buffers = REDUCE_SCHEDULE.get(kernel)
        if buffers:
            for b in buffers:
                self.partials[b] = _rs_pair_add("reduce_pair_add_" + b, self.place, self.grads[b], self.theirs[b],
                                                SPLIT.get(b, "rows"))
            step = _step_chip_exchange([self.partials[b] for b in buffers])
            self.running.append((step, buffers, self.peers))
            steps.append(step)
        buffers = SWAP_SCHEDULE.get(kernel)
        if buffers:
            step = self._swap_step(buffers)
            self.running.append((step, buffers, self.reduced))
            steps.append(step)
        return steps

    def _swap_step(self, buffers):
        halves = [_rs_chip_add("reduce_chip_add_" + b, self.place, self.partials[b], self.peers[b], SPLIT.get(b, "rows"))
                  for b in buffers]
        return _step_pair_gather(halves, self._splits(buffers))

    def finish(self):
        self._collect()
        rest = [b for b, _, _ in BUFFERS if b not in self.reduced]
        step = self._swap_step(rest)
        _run_steps("reduce_pair_gather", [step])
        self.reduced.update(zip(rest, step.results))
        return self.reduced


def kernel(x, meta_tokens, a_norm_pre, a_w_in, a_conv_w, a_conv_b, a_dt_bias, a_a_log, a_d_skip, a_gate_norm, a_w_out, a_norm_post, kv_norm, w_kv, b_norm_pre, b_w_q, b_sinks, b_w_o, b_norm_post, f_norm_pre, f_w_up, f_conv_w, f_conv_b, f_w_down, f_norm_post, loss_target, m_meta_tokens, m_a_norm_pre, m_a_w_in, m_a_conv_w, m_a_conv_b, m_a_dt_bias, m_a_a_log, m_a_d_skip, m_a_gate_norm, m_a_w_out, m_a_norm_post, m_kv_norm, m_w_kv, m_b_norm_pre, m_b_w_q, m_b_sinks, m_b_w_o, m_b_norm_post, m_f_norm_pre, m_f_w_up, m_f_conv_w, m_f_conv_b, m_f_w_down, m_f_norm_post, v_meta_tokens, v_a_norm_pre, v_a_w_in, v_a_conv_w, v_a_conv_b, v_a_dt_bias, v_a_a_log, v_a_d_skip, v_a_gate_norm, v_a_w_out, v_a_norm_post, v_kv_norm, v_w_kv, v_b_norm_pre, v_b_w_q, v_b_sinks, v_b_w_o, v_b_norm_post, v_f_norm_pre, v_f_w_up, v_f_conv_w, v_f_conv_b, v_f_w_down, v_f_norm_post):
    given = dict(locals())
    w = {n: given[n] for n in WEIGHTS}
    mom = {n: given["m_" + n] for n in WEIGHTS}
    var = {n: given["v_" + n] for n in WEIGHTS}
    chip = 2 * lax.axis_index("x") + lax.axis_index("y")
    core = lax.axis_index("c")
    place = jnp.stack([chip, core]).astype(jnp.int32)

    small_all = _allgather_small("gather_small", _pack([w[n] for n in SMALL_SHARDED]))
    small_parts = _unpack(small_all, SMALL_SHARDED, _shard_shape)
    slots = {b: _cast_into_slot("cast_" + b, place, _local_shard(w, wn, layer)) for b, wn, layer in BUFFERS}
    pipeline = _Pipeline(place, slots)
    pipeline.gather_now("gather_first", ["a_w_in"])
    p = {}
    for n in SMALL:
        p[n] = _join_chips(n, small_parts[n]) if n in SMALL_SHARDED else w[n]
    p["a_conv_w"] = p["a_conv_w"][0]
    p["kv_norm"] = p["kv_norm"].reshape(1, D_MODEL)

    loss_local, grad_x, g = _local_step(x[0], loss_target[0], p, pipeline)
    loss = lax.psum(loss_local, ("x", "y", "c"))

    small_sum = _allreduce_small("reduce_small", _pack([g[n].reshape(FULL_SHAPE[n]) for n in SMALL]))
    small_red = _unpack(small_sum, SMALL, lambda n: FULL_SHAPE[n])
    grads = {}
    for n in SMALL:
        if SHARD_AXIS[n] is None:
            grads[n] = small_red[n]
        else:
            grads[n] = lax.dynamic_index_in_dim(_split_chips(n, small_red[n]), chip, 0, keepdims=False)

    shard_sum = pipeline.finish()

    delta, new_m, new_v = {}, {}, {}
    for n in BIG:
        shape = _shard_shape(n)
        if n in TRANSPOSED:
            g2d = shard_sum[n]
            w2d, m2d, v2d = (arrays[n][0].T for arrays in (w, mom, var))
            back = lambda a: a.T.reshape(shape)
        else:
            g2d = (jnp.concatenate([shard_sum[n + "0"], shard_sum[n + "1"]], axis=0) if n in ("f_w_up", "f_w_down")
                   else shard_sum[n])
            w2d, m2d, v2d = (_as2d(arrays[n]) for arrays in (w, mom, var))
            back = lambda a: a.reshape(shape)
        d, m2, v2 = _adamw("adamw_" + n, w2d, g2d, m2d, v2d)
        grads[n], delta[n], new_m[n], new_v[n] = back(g2d), back(d), back(m2), back(v2)
    packed = [_pack([src[n].reshape(_shard_shape(n)) for n in SMALL]) for src in (w, grads, mom, var)]
    outs = _adamw("adamw_small", *packed)
    for dst, flat in zip((delta, new_m, new_v), outs):
        dst.update(_unpack(flat, SMALL, _shard_shape))

    return (loss, grad_x[None], *[grads[n].reshape(_shard_shape(n)) for n in WEIGHTS],
            *[delta[n] for n in WEIGHTS], *[new_m[n] for n in WEIGHTS], *[new_v[n] for n in WEIGHTS])
```

```python
import functools
import math

import jax
import jax.numpy as jnp
from jax import lax
from jax.experimental import pallas as pl
from jax.experimental.pallas import tpu as pltpu

F32, BF16 = jnp.float32, jnp.bfloat16
MESH = pl.DeviceIdType.MESH

D_MODEL = 1024
N_META = 16
CHUNK = 128
PAD_ROWS = CHUNK - N_META
D_INNER = 2048
D_STATE = 128
N_GROUPS = 4
HEADS_PER_GROUP = 8
SSM_HEADS = 32
HEAD_DIM = 64
D_BC = N_GROUPS * D_STATE
D_XBC = D_INNER + 2 * D_BC
D_MAIN = D_INNER + D_XBC
D_IN_PROJ = D_MAIN + SSM_HEADS
GROUP_W = HEADS_PER_GROUP * HEAD_DIM
SSM_CONV = 4
D_FF = 2816
FFN_CONV = 3
N_Q_HEADS = 16
N_KV_HEADS = 4
D_KV = 256
ATTN_SCALE = 1.0 / math.sqrt(HEAD_DIM)
RMS_EPS = 1e-6
NEG_INF = -1e30
LANES = 128
VMEM_LIMIT = 48 * 1024 * 1024

ADAM_LR, ADAM_B1, ADAM_B2, ADAM_EPS, ADAM_WD, ADAM_STEP = 0.001, 0.9, 0.999, 1e-08, 0.01, 10

N_CHIPS = 4
N_DEV = 8


def _cparams(sem=None):
    return pltpu.CompilerParams(dimension_semantics=sem, vmem_limit_bytes=VMEM_LIMIT)


def _tile(n, cands=(512, 256, 128)):
    for t in cands:
        if n % t == 0:
            return t
    return n


def _row_tile(rows, width):
    for t in (544, 272):
        if rows % t == 0 and t * width * 4 <= (3 << 20):
            return t
    return 128


def _rows_mask(i, tm):
    rows = i * tm + lax.broadcasted_iota(jnp.int32, (tm, 1), 0)
    return rows >= PAD_ROWS


def _dot(a, b):
    return jnp.dot(a, b, preferred_element_type=F32)


def _dot_nt(a, b):
    return lax.dot_general(a, b, (((1,), (1,)), ((), ())), preferred_element_type=F32)


def _dot_tn(a, b):
    return lax.dot_general(a, b, (((0,), (0,)), ((), ())), preferred_element_type=F32)


def _sigmoid(x):
    return 1.0 / (1.0 + jnp.exp(-x))


def _place():
    return lax.axis_index("x"), lax.axis_index("y"), lax.axis_index("c")


def _other_chips(x, y):
    return [(1 - x, y), (x, 1 - y), (1 - x, 1 - y)]


class _Step:
    def __init__(self, ins, outs, aliases, n_sems, start, finish):
        self.ins, self.outs, self.aliases, self.n_sems = list(ins), list(outs), dict(aliases), n_sems
        self.start, self.finish = start, finish
        self.results = None


def _like(a):
    return jax.ShapeDtypeStruct(a.shape, a.dtype)


def _remote(src, dst, send_sems, recv_sems, k, device):
    return pltpu.make_async_remote_copy(src, dst, send_sems.at[k], recv_sems.at[k], device_id=device, device_id_type=MESH)


def _half(ref, split, which, lead=()):
    if split == "rows":
        hr = ref.shape[-2] // 2
        return ref.at[lead + (pl.ds(which * hr, hr),)]
    hc = ref.shape[-1] // 2
    return ref.at[lead + (slice(None), pl.ds(which * hc, hc))]


def _splits(bufs, splits):
    return list(splits) if splits is not None else ["rows"] * len(bufs)


def _step_gather_ici(bufs, splits=None):
    splits = _splits(bufs, splits)

    def copies(outs, send_sems, recv_sems, received):
        x, y, c = _place()
        me = 2 * x + y
        for k, o in enumerate(outs):
            for j, (cx, cy) in enumerate(_other_chips(x, y)):
                part = _half(o, splits[k], c, (2 * cx + cy if received else me,))
                yield _remote(part, part, send_sems, recv_sems, 3 * k + j, (cx, cy, c))

    def start(ins, outs, send_sems, recv_sems):
        for cp in copies(outs, send_sems, recv_sems, False):
            cp.start()

    def finish(ins, outs, send_sems, recv_sems):
        for cp in copies(outs, send_sems, recv_sems, True):
            cp.wait_recv()
        for cp in copies(outs, send_sems, recv_sems, False):
            cp.wait_send()

    return _Step(bufs, [_like(b) for b in bufs], {k: k for k in range(len(bufs))}, 3 * len(bufs), start, finish)


def _step_gather_d2d(bufs, splits=None):
    splits = _splits(bufs, splits)

    def copies(outs, send_sems, recv_sems, received):
        x, y, c = _place()
        for k, o in enumerate(outs):
            for j, (cx, cy) in enumerate(_other_chips(x, y)):
                part = _half(o, splits[k], 1 - c if received else c, (2 * cx + cy,))
                yield _remote(part, part, send_sems, recv_sems, 3 * k + j, (x, y, 1 - c))

    def start(ins, outs, send_sems, recv_sems):
        for cp in copies(outs, send_sems, recv_sems, False):
            cp.start()

    def finish(ins, outs, send_sems, recv_sems):
        for cp in copies(outs, send_sems, recv_sems, True):
            cp.wait_recv()
        for cp in copies(outs, send_sems, recv_sems, False):
            cp.wait_send()

    return _Step(bufs, [_like(b) for b in bufs], {k: k for k in range(len(bufs))}, 3 * len(bufs), start, finish)


def _step_gather_full(bufs, splits=None):
    n = len(bufs)
    splits = _splits(bufs, splits)

    def ici(outs, send_sems, recv_sems, received):
        x, y, c = _place()
        me = 2 * x + y
        for k, o in enumerate(outs):
            for j, (cx, cy) in enumerate(_other_chips(x, y)):
                part = _half(o, splits[k], c, (2 * cx + cy if received else me,))
                yield _remote(part, part, send_sems, recv_sems, 3 * k + j, (cx, cy, c))

    def d2d(outs, send_sems, recv_sems, received):
        x, y, c = _place()
        for k, o in enumerate(outs):
            for j, (cx, cy) in enumerate(_other_chips(x, y)):
                part = _half(o, splits[k], 1 - c if received else c, (2 * cx + cy,))
                yield _remote(part, part, send_sems, recv_sems, 3 * n + 3 * k + j, (x, y, 1 - c))

    def start(ins, outs, send_sems, recv_sems):
        for cp in ici(outs, send_sems, recv_sems, False):
            cp.start()

    def finish(ins, outs, send_sems, recv_sems):
        for arrived, onward in zip(ici(outs, send_sems, recv_sems, True), d2d(outs, send_sems, recv_sems, False)):
            arrived.wait_recv()
            onward.start()
        for cp in d2d(outs, send_sems, recv_sems, True):
            cp.wait_recv()
        for cp in ici(outs, send_sems, recv_sems, False):
            cp.wait_send()
        for cp in d2d(outs, send_sems, recv_sems, False):
            cp.wait_send()

    return _Step(bufs, [_like(b) for b in bufs], {k: k for k in range(n)}, 6 * n, start, finish)


def _half_shape(shape, split):
    return shape[:-2] + ((shape[-2] // 2, shape[-1]) if split == "rows" else (shape[-2], shape[-1] // 2))


def _step_pair_exchange(grads, splits=None):
    splits = _splits(grads, splits)

    def copies(ins, outs, send_sems, recv_sems):
        x, y, c = _place()
        for k, (g, o) in enumerate(zip(ins, outs)):
            yield _remote(_half(g, splits[k], 1 - c, (slice(None),)), o, send_sems, recv_sems, k, (x, y, 1 - c))

    def start(ins, outs, send_sems, recv_sems):
        for cp in copies(ins, outs, send_sems, recv_sems):
            cp.start()

    def finish(ins, outs, send_sems, recv_sems):
        for cp in copies(ins, outs, send_sems, recv_sems):
            cp.wait()

    outs = [jax.ShapeDtypeStruct(_half_shape(g.shape, s), g.dtype) for g, s in zip(grads, splits)]
    return _Step(grads, outs, {}, len(grads), start, finish)


def _step_chip_exchange(partials):
    def copies(ins, outs, send_sems, recv_sems):
        x, y, c = _place()
        for k, (q, o) in enumerate(zip(ins, outs)):
            for j, (cx, cy) in enumerate(_other_chips(x, y)):
                yield _remote(q.at[2 * cx + cy], o.at[j], send_sems, recv_sems, 3 * k + j, (cx, cy, c))

    def start(ins, outs, send_sems, recv_sems):
        for cp in copies(ins, outs, send_sems, recv_sems):
            cp.start()

    def finish(ins, outs, send_sems, recv_sems):
        for cp in copies(ins, outs, send_sems, recv_sems):
            cp.wait()

    outs = [jax.ShapeDtypeStruct((3,) + q.shape[1:], q.dtype) for q in partials]
    return _Step(partials, outs, {}, 3 * len(partials), start, finish)


def _step_pair_gather(shards, splits=None):
    splits = _splits(shards, splits)

    def copies(outs, send_sems, recv_sems, received):
        x, y, c = _place()
        for k, o in enumerate(outs):
            part = _half(o, splits[k], 1 - c if received else c)
            yield _remote(part, part, send_sems, recv_sems, k, (x, y, 1 - c))

    def start(ins, outs, send_sems, recv_sems):
        for cp in copies(outs, send_sems, recv_sems, False):
            cp.start()

    def finish(ins, outs, send_sems, recv_sems):
        for cp in copies(outs, send_sems, recv_sems, True):
            cp.wait_recv()
        for cp in copies(outs, send_sems, recv_sems, False):
            cp.wait_send()

    return _Step(shards, [_like(s) for s in shards], {k: k for k in range(len(shards))}, len(shards), start, finish)


def _call(body, *, name, out_shape, grid, in_specs, out_specs, operands, scratch_shapes=(), semantics=None, steps=()):
    single = not isinstance(out_shape, (tuple, list))
    out_shapes = [out_shape] if single else list(out_shape)
    out_spec_list = [out_specs] if single else list(out_specs)
    steps = list(steps)
    if not steps:
        res = pl.pallas_call(body, name=name, out_shape=out_shapes, grid=grid, in_specs=list(in_specs),
                             out_specs=out_spec_list, scratch_shapes=list(scratch_shapes),
                             compiler_params=_cparams(semantics))(*operands)
        return res[0] if single else res
    n_in, n_out, n_scr = len(operands), len(out_shapes), len(scratch_shapes)
    x_in = [a for s in steps for a in s.ins]
    x_out = [o for s in steps for o in s.outs]
    aliases, in_off, out_off = {}, 0, 0
    for s in steps:
        for i, o in s.aliases.items():
            aliases[n_in + in_off + i] = n_out + out_off + o
        in_off += len(s.ins)
        out_off += len(s.outs)
    sems = []
    for s in steps:
        sems += [pltpu.SemaphoreType.DMA((s.n_sems,)), pltpu.SemaphoreType.DMA((s.n_sems,))]
    any_spec = pl.BlockSpec(memory_space=pl.ANY)

    def carried(*refs):
        pos = 0
        ins = refs[pos:pos + n_in]; pos += n_in
        xi = refs[pos:pos + len(x_in)]; pos += len(x_in)
        outs = refs[pos:pos + n_out]; pos += n_out
        xo = refs[pos:pos + len(x_out)]; pos += len(x_out)
        scr = refs[pos:pos + n_scr]; pos += n_scr
        sem_refs = refs[pos:]

        def each(action):
            i0 = o0 = 0
            for k, s in enumerate(steps):
                getattr(s, action)(xi[i0:i0 + len(s.ins)], xo[o0:o0 + len(s.outs)], sem_refs[2 * k], sem_refs[2 * k + 1])
                i0 += len(s.ins)
                o0 += len(s.outs)

        if grid:
            first = functools.reduce(jnp.logical_and, [pl.program_id(d) == 0 for d in range(len(grid))])
            last = functools.reduce(jnp.logical_and, [pl.program_id(d) == grid[d] - 1 for d in range(len(grid))])
            pl.when(first)(lambda: each("start"))
            body(*ins, *outs, *scr)
            pl.when(last)(lambda: each("finish"))
        else:
            each("start")
            body(*ins, *outs, *scr)
            each("finish")

    res = pl.pallas_call(
        carried, name=name, out_shape=out_shapes + x_out, grid=grid,
        in_specs=list(in_specs) + [any_spec] * len(x_in), out_specs=out_spec_list + [any_spec] * len(x_out),
        scratch_shapes=list(scratch_shapes) + sems, input_output_aliases=aliases,
        compiler_params=_cparams(None if semantics is None else ("arbitrary",) * len(grid)),
    )(*operands, *x_in)
    o0 = n_out
    for s in steps:
        s.results = list(res[o0:o0 + len(s.outs)])
        o0 += len(s.outs)
    return res[0] if single else tuple(res[:n_out])


def _run_steps(name, steps):
    _call(lambda: None, name=name, out_shape=[], grid=(), in_specs=[], out_specs=[], operands=[], steps=steps)
    return [s.results for s in steps]


def _mm(name, a, b, mode, out_dtype=F32, acc=None, b_colblock=0, k_rows=None, steps=()):
    resident_bytes = 8 << 20
    if mode == "nn":
        m, k = a.shape
        n = b.shape[1]
        tm = m
        while tm * k * 2 > resident_bytes and tm % 32 == 0:
            tm //= 2
        tn = _tile(n)
        grid = (m // tm, n // tn)
        in_specs = [pl.BlockSpec((tm, k), lambda i, j: (i, 0)), pl.BlockSpec((k, tn), lambda i, j: (0, j))]
        out_shape, out_block = (m, n), (tm, tn)
    elif mode == "nt":
        m, n = a.shape
        k = k_rows or b.shape[0]
        tm = m
        while tm * n * 2 > resident_bytes and tm % 32 == 0:
            tm //= 2
        tk = _tile(k)
        grid = (m // tm, k // tk)
        in_specs = [pl.BlockSpec((tm, n), lambda i, j: (i, 0)), pl.BlockSpec((tk, n), lambda i, j: (j, b_colblock))]
        out_shape, out_block = (m, k), (tm, tk)
    else:
        m, k = a.shape
        n = b.shape[1]
        tk, tn = _tile(k), (n if m * n * 2 <= resident_bytes else _tile(n))
        grid = (k // tk, n // tn)
        in_specs = [pl.BlockSpec((m, tk), lambda i, j: (0, i)), pl.BlockSpec((m, tn), lambda i, j: (0, j))]
        out_shape, out_block = (k, n), (tk, tn)
    out_spec = pl.BlockSpec(out_block, lambda i, j: (i, j))
    has_acc = acc is not None

    def body(*refs):
        a_ref, b_ref = refs[0], refs[1]
        o_ref = refs[-1]
        av, bv = a_ref[...], b_ref[...]
        if mode == "nn":
            r = _dot(av, bv)
        elif mode == "nt":
            r = _dot_nt(av, bv)
        else:
            r = _dot_tn(av, bv)
        if has_acc:
            r = r + refs[2][...]
        o_ref[...] = r.astype(o_ref.dtype)

    operands = [a, b]
    if has_acc:
        in_specs = in_specs + [out_spec]
        operands.append(acc)
    return _call(body, name=name, out_shape=jax.ShapeDtypeStruct(out_shape, out_dtype), grid=grid, in_specs=in_specs,
                 out_specs=out_spec, operands=operands, semantics=("parallel", "parallel"), steps=steps)


def _fit_rows(m, row_bytes, budget=8 << 20):
    tm = m
    while tm * row_bytes > budget and tm % 32 == 0:
        tm //= 2
    return tm


def _mm_nn_bychip(name, a, bc):
    m, k = a.shape
    n = bc.shape[2]
    tm = min(_fit_rows(m, k * 2), _fit_rows(m, n * 4))

    def body(a_ref, b_ref, o_ref):
        o_ref[...] = _dot(a_ref[...], b_ref[...])

    return pl.pallas_call(
        body, name=name, out_shape=jax.ShapeDtypeStruct((m, N_CHIPS * n), F32), grid=(m // tm, N_CHIPS),
        in_specs=[pl.BlockSpec((tm, k), lambda i, c: (i, 0)), pl.BlockSpec((None, k, n), lambda i, c: (c, 0, 0))],
        out_specs=pl.BlockSpec((tm, n), lambda i, c: (i, c)), compiler_params=_cparams(("parallel", "parallel")),
    )(a, bc)


def _mm_nt_bychip(name, a, bc, chip0, acc=None):
    m = a.shape[0]
    _, k, n = bc.shape
    nch = a.shape[1] // n
    tm, tk = _fit_rows(m, n * 2), _tile(k)
    has_acc = acc is not None

    def body(*refs):
        a_ref, b_ref, o_ref = refs[0], refs[1], refs[-1]

        @pl.when(pl.program_id(2) == 0)
        def _():
            o_ref[...] = refs[2][...] if has_acc else jnp.zeros_like(o_ref)

        o_ref[...] += _dot_nt(a_ref[...], b_ref[...])

    out_spec = pl.BlockSpec((tm, tk), lambda i, j, c: (i, j))
    in_specs = [pl.BlockSpec((tm, n), lambda i, j, c: (i, c)),
                pl.BlockSpec((None, tk, n), lambda i, j, c: (chip0 + c, j, 0))]
    operands = [a, bc]
    if has_acc:
        in_specs.append(out_spec)
        operands.append(acc)
    return pl.pallas_call(
        body, name=name, out_shape=jax.ShapeDtypeStruct((m, k), F32), grid=(m // tm, k // tk, nch),
        in_specs=in_specs, out_specs=out_spec, compiler_params=_cparams(("parallel", "parallel", "arbitrary")),
    )(*operands)


def _mm_tn_bychip(name, a, dy, n, chip0, into=None):
    m, k = a.shape
    nch = dy.shape[1] // n
    tk = _tile(k)

    def body(*refs):
        a_ref, d_ref, o_ref = refs[0], refs[1], refs[-1]
        o_ref[...] = _dot_tn(a_ref[...], d_ref[...]).astype(BF16)

    in_specs = [pl.BlockSpec((m, tk), lambda i, c: (0, i)), pl.BlockSpec((m, n), lambda i, c: (0, c))]
    operands = [a, dy]
    aliases = {}
    if into is not None:
        in_specs.append(pl.BlockSpec(memory_space=pl.ANY))
        operands.append(into)
        aliases = {2: 0}
    return pl.pallas_call(
        body, name=name, out_shape=jax.ShapeDtypeStruct((N_CHIPS, k, n), BF16), grid=(k // tk, nch),
        in_specs=in_specs, out_specs=pl.BlockSpec((None, tk, n), lambda i, c: (chip0 + c, i, 0)),
        input_output_aliases=aliases, compiler_params=_cparams(("parallel", "parallel")),
    )(*operands)


def _rms_fwd(name, h, w):
    rows, width = h.shape
    tm = _row_tile(rows, width)

    def body(h_ref, w_ref, o_ref):
        x = h_ref[...]
        r = lax.rsqrt(jnp.mean(x * x, axis=-1, keepdims=True) + RMS_EPS)
        o_ref[...] = (x * r * w_ref[...]).astype(BF16)

    return pl.pallas_call(
        body, name=name, out_shape=jax.ShapeDtypeStruct((rows, width), BF16), grid=(rows // tm,),
        in_specs=[pl.BlockSpec((tm, width), lambda i: (i, 0)), pl.BlockSpec((1, width), lambda i: (0, 0))],
        out_specs=pl.BlockSpec((tm, width), lambda i: (i, 0)), compiler_params=_cparams(("parallel",)),
    )(h, w)


def _resid_norm_fwd(name, h, pre, w, next_norms=()):
    rows, width = h.shape
    tm = _row_tile(rows, width)
    n_next = len(next_norms)

    def body(*refs):
        h_ref, p_ref, w_ref = refs[:3]
        v_refs = refs[3:3 + n_next]
        o_ref = refs[3 + n_next]
        n_refs = refs[4 + n_next:]
        p = p_ref[...]
        r = lax.rsqrt(jnp.mean(p * p, axis=-1, keepdims=True) + RMS_EPS)
        x = h_ref[...] + jnp.where(_rows_mask(pl.program_id(0), tm), p * r * w_ref[...], 0.0)
        o_ref[...] = x
        if n_next:
            rx = lax.rsqrt(jnp.mean(x * x, axis=-1, keepdims=True) + RMS_EPS)
            for v_ref, n_ref in zip(v_refs, n_refs):
                n_ref[...] = (x * rx * v_ref[...]).astype(BF16)

    row_spec = pl.BlockSpec((tm, width), lambda i: (i, 0))
    vec_spec = pl.BlockSpec((1, width), lambda i: (0, 0))
    outs = pl.pallas_call(
        body, name=name,
        out_shape=[jax.ShapeDtypeStruct((rows, width), F32)] + [jax.ShapeDtypeStruct((rows, width), BF16)] * n_next,
        grid=(rows // tm,), in_specs=[row_spec, row_spec, vec_spec] + [vec_spec] * n_next,
        out_specs=[row_spec] * (1 + n_next), compiler_params=_cparams(("parallel",)),
    )(h, pre, w, *next_norms)
    return outs[0], list(outs[1:])


def _resid_norm_loss(name, h, pre, w, target):
    rows, width = h.shape

    def body(h_ref, p_ref, w_ref, t_ref, dh_ref, loss_ref, dp_ref, dw_ref):
        i = pl.program_id(0)
        p = p_ref[...]
        r = lax.rsqrt(jnp.mean(p * p, axis=-1, keepdims=True) + RMS_EPS)
        x = h_ref[...] + p * r * w_ref[...]
        real = (i + jnp.zeros((CHUNK, 1), jnp.int32)) >= 1
        diff = jnp.where(real, x - t_ref[...], 0.0)
        dh = diff * (1.0 / D_MODEL)
        dh_ref[...] = dh
        dp, dw_rows = _rms_bwd(dh, p, w_ref[...])
        dp_ref[...] = dp.astype(BF16)

        @pl.when(i == 0)
        def _():
            loss_ref[...] = jnp.zeros_like(loss_ref)
            dw_ref[...] = jnp.zeros_like(dw_ref)

        loss_ref[...] += jnp.sum(diff * diff) * (0.5 / D_MODEL)
        dw_ref[...] += jnp.sum(dw_rows, axis=0, keepdims=True)

    blk = pl.BlockSpec((CHUNK, width), lambda i: (i, 0))
    vec_spec = pl.BlockSpec((1, width), lambda i: (0, 0))
    return pl.pallas_call(
        body, name=name,
        out_shape=(jax.ShapeDtypeStruct((rows, width), F32), jax.ShapeDtypeStruct((1, LANES), F32),
                   jax.ShapeDtypeStruct((rows, width), BF16), jax.ShapeDtypeStruct((1, width), F32)),
        grid=(rows // CHUNK,),
        in_specs=[blk, blk, vec_spec, pl.BlockSpec((CHUNK, width), lambda i: (jnp.maximum(i - 1, 0), 0))],
        out_specs=(blk, pl.BlockSpec((1, LANES), lambda i: (0, 0)), blk, vec_spec),
        compiler_params=_cparams(("arbitrary",)),
    )(h, pre, w, target)


def _rms_bwd(dy, x, w):
    r = lax.rsqrt(jnp.mean(x * x, axis=-1, keepdims=True) + RMS_EPS)
    xhat = x * r
    dxhat = dy * w
    return r * (dxhat - xhat * jnp.mean(dxhat * xhat, axis=-1, keepdims=True)), dy * xhat


def _norm_bwd_add(name, dh, dhn, h, w, then=None):
    rows, width = dh.shape
    tm = _row_tile(rows, width)
    fused = then is not None

    def body(*refs):
        dh_ref, dhn_ref, h_ref, w_ref = refs[:4]
        o_ref, dw_ref = refs[6:8] if fused else refs[4:6]
        i = pl.program_id(0)
        valid = _rows_mask(i, tm)
        dx, dw_rows = _rms_bwd(dhn_ref[...], h_ref[...], w_ref[...])
        dh_new = dh_ref[...] + jnp.where(valid, dx, 0.0)
        o_ref[...] = dh_new

        @pl.when(i == 0)
        def _():
            dw_ref[...] = jnp.zeros_like(dw_ref)

        dw_ref[...] += jnp.sum(dw_rows, axis=0, keepdims=True)
        if fused:
            p_ref, wp_ref, dp_ref, dwp_ref = refs[4], refs[5], refs[8], refs[9]
            dp, dwp_rows = _rms_bwd(jnp.where(valid, dh_new, 0.0), p_ref[...], wp_ref[...])
            dp_ref[...] = dp.astype(BF16)

            @pl.when(i == 0)
            def _():
                dwp_ref[...] = jnp.zeros_like(dwp_ref)

            dwp_ref[...] += jnp.sum(dwp_rows, axis=0, keepdims=True)

    row_spec = pl.BlockSpec((tm, width), lambda i: (i, 0))
    vec_spec = pl.BlockSpec((1, width), lambda i: (0, 0))
    row_f32, vec_f32 = jax.ShapeDtypeStruct((rows, width), F32), jax.ShapeDtypeStruct((1, width), F32)
    in_specs, operands = [row_spec, row_spec, row_spec, vec_spec], [dh, dhn, h, w]
    out_shape, out_specs = [row_f32, vec_f32], [row_spec, vec_spec]
    if fused:
        in_specs += [row_spec, vec_spec]
        operands += list(then)
        out_shape += [jax.ShapeDtypeStruct((rows, width), BF16), vec_f32]
        out_specs += [row_spec, vec_spec]
    return pl.pallas_call(
        body, name=name, out_shape=out_shape, grid=(rows // tm,), in_specs=in_specs, out_specs=out_specs,
        compiler_params=_cparams(("arbitrary",)),
    )(*operands)


def _shift_down(x, s, rows):
    return pltpu.roll(x, s, 0) if s else x


def _shift_up(x, s, rows):
    return pltpu.roll(x, rows - s, 0) if s else x


def _conv4_fwd(name, zx, cw, cb, steps=()):
    rows = zx.shape[0]
    off = D_INNER // LANES

    def body(x_ref, w_ref, b_ref, o_ref):
        x = x_ref[...]
        acc = b_ref[...] + w_ref[pl.ds(SSM_CONV - 1, 1), :] * x
        for s in range(1, SSM_CONV):
            acc = acc + w_ref[pl.ds(SSM_CONV - 1 - s, 1), :] * _shift_down(x, s, rows)
        valid = lax.broadcasted_iota(jnp.int32, (rows, 1), 0) >= PAD_ROWS
        o_ref[...] = jnp.where(valid, acc * _sigmoid(acc), 0.0)

    return _call(
        body, name=name, out_shape=jax.ShapeDtypeStruct((rows, D_XBC), F32), grid=(D_XBC // LANES,),
        in_specs=[pl.BlockSpec((rows, LANES), lambda j: (0, j + off)),
                  pl.BlockSpec((SSM_CONV, LANES), lambda j: (0, j)),
                  pl.BlockSpec((1, LANES), lambda j: (0, j))],
        out_specs=pl.BlockSpec((rows, LANES), lambda j: (0, j)), operands=[zx, cw, cb],
        semantics=("parallel",), steps=steps)


def _conv4_bwd(name, zx, dout, cw, cb, col0):
    rows, width = dout.shape
    zoff = (D_INNER + col0) // LANES
    woff = col0 // LANES

    def body(x_ref, d_ref, w_ref, b_ref, dx_ref, dw_ref, db_ref):
        x = x_ref[...]
        shifted = [_shift_down(x, s, rows) for s in range(SSM_CONV)]
        acc = b_ref[...]
        for s in range(SSM_CONV):
            acc = acc + w_ref[pl.ds(SSM_CONV - 1 - s, 1), :] * shifted[s]
        sig = _sigmoid(acc)
        valid = lax.broadcasted_iota(jnp.int32, (rows, 1), 0) >= PAD_ROWS
        dpre = jnp.where(valid, d_ref[...] * sig * (1.0 + acc * (1.0 - sig)), 0.0)
        dx = w_ref[pl.ds(SSM_CONV - 1, 1), :] * dpre
        for s in range(1, SSM_CONV):
            dx = dx + w_ref[pl.ds(SSM_CONV - 1 - s, 1), :] * _shift_up(dpre, s, rows)
        dx_ref[...] = dx.astype(BF16)
        for s in range(SSM_CONV):
            dw_ref[pl.ds(SSM_CONV - 1 - s, 1), :] = jnp.sum(dpre * shifted[s], axis=0, keepdims=True)
        db_ref[...] = jnp.sum(dpre, axis=0, keepdims=True)

    return pl.pallas_call(
        body, name=name,
        out_shape=(jax.ShapeDtypeStruct((rows, width), BF16), jax.ShapeDtypeStruct((SSM_CONV, width), F32),
                   jax.ShapeDtypeStruct((1, width), F32)),
        grid=(width // LANES,),
        in_specs=[pl.BlockSpec((rows, LANES), lambda j: (0, j + zoff)),
                  pl.BlockSpec((rows, LANES), lambda j: (0, j)),
                  pl.BlockSpec((SSM_CONV, LANES), lambda j: (0, j + woff)),
                  pl.BlockSpec((1, LANES), lambda j: (0, j + woff))],
        out_specs=(pl.BlockSpec((rows, LANES), lambda j: (0, j)),
                   pl.BlockSpec((SSM_CONV, LANES), lambda j: (0, j)),
                   pl.BlockSpec((1, LANES), lambda j: (0, j))),
        compiler_params=_cparams(("parallel",)),
    )(zx, dout, cw, cb)


def _ffn_conv_fwd(name, up, cw, cb, steps=()):
    rows = up.shape[0]
    nt = D_FF // LANES

    def body(g_ref, v_ref, wg_ref, wv_ref, bg_ref, bv_ref, o_ref):
        g, v = g_ref[...], v_ref[...]
        ug, uv = bg_ref[...], bv_ref[...]
        for s in range(FFN_CONV):
            ug = ug + wg_ref[pl.ds(FFN_CONV - 1 - s, 1), :] * _shift_down(g, s, rows)
            uv = uv + wv_ref[pl.ds(FFN_CONV - 1 - s, 1), :] * _shift_down(v, s, rows)
        o_ref[...] = (ug * _sigmoid(ug) * uv).astype(BF16)

    col = lambda shift: pl.BlockSpec((rows, LANES), lambda j: (0, j + shift))
    wsp = lambda shift: pl.BlockSpec((FFN_CONV, LANES), lambda j: (0, j + shift))
    bsp = lambda shift: pl.BlockSpec((1, LANES), lambda j: (0, j + shift))
    return _call(
        body, name=name, out_shape=jax.ShapeDtypeStruct((rows, D_FF), BF16), grid=(nt,),
        in_specs=[col(0), col(nt), wsp(0), wsp(nt), bsp(0), bsp(nt)],
        out_specs=pl.BlockSpec((rows, LANES), lambda j: (0, j)), operands=[up, up, cw, cw, cb, cb],
        semantics=("parallel",), steps=steps)


def _ffn_conv_bwd(name, up, dact, cw, cb, steps=()):
    rows = up.shape[0]
    nt = D_FF // LANES

    def body(g_ref, v_ref, d_ref, wg_ref, wv_ref, bg_ref, bv_ref, dxg_ref, dxv_ref, dwg_ref, dwv_ref, dbg_ref, dbv_ref):
        g, v = g_ref[...], v_ref[...]
        gs = [_shift_down(g, s, rows) for s in range(FFN_CONV)]
        vs = [_shift_down(v, s, rows) for s in range(FFN_CONV)]
        ug, uv = bg_ref[...], bv_ref[...]
        for s in range(FFN_CONV):
            ug = ug + wg_ref[pl.ds(FFN_CONV - 1 - s, 1), :] * gs[s]
            uv = uv + wv_ref[pl.ds(FFN_CONV - 1 - s, 1), :] * vs[s]
        sig = _sigmoid(ug)
        dsig = d_ref[...] * sig
        for dpre, src, w_ref, dx_ref, dw_ref, db_ref in (
                (dsig * uv * (1.0 + ug * (1.0 - sig)), gs, wg_ref, dxg_ref, dwg_ref, dbg_ref),
                (dsig * ug, vs, wv_ref, dxv_ref, dwv_ref, dbv_ref)):
            dx = w_ref[pl.ds(FFN_CONV - 1, 1), :] * dpre
            for s in range(1, FFN_CONV):
                dx = dx + w_ref[pl.ds(FFN_CONV - 1 - s, 1), :] * _shift_up(dpre, s, rows)
            dx_ref[...] = dx.astype(BF16)
            for s in range(FFN_CONV):
                dw_ref[pl.ds(FFN_CONV - 1 - s, 1), :] = jnp.sum(dpre * src[s], axis=0, keepdims=True)
            db_ref[...] = jnp.sum(dpre, axis=0, keepdims=True)

    col = lambda shift: pl.BlockSpec((rows, LANES), lambda j: (0, j + shift))
    wsp = lambda shift: pl.BlockSpec((FFN_CONV, LANES), lambda j: (0, j + shift))
    bsp = lambda shift: pl.BlockSpec((1, LANES), lambda j: (0, j + shift))
    dx_shape = jax.ShapeDtypeStruct((rows, D_FF), BF16)
    dw_shape = jax.ShapeDtypeStruct((FFN_CONV, D_FF), F32)
    db_shape = jax.ShapeDtypeStruct((1, D_FF), F32)
    return _call(
        body, name=name, out_shape=(dx_shape, dx_shape, dw_shape, dw_shape, db_shape, db_shape), grid=(nt,),
        in_specs=[col(0), col(nt), col(0), wsp(0), wsp(nt), bsp(0), bsp(nt)],
        out_specs=(col(0), col(0), wsp(0), wsp(0), bsp(0), bsp(0)),
        operands=[up, up, dact, cw, cw, cb, cb], semantics=("parallel",), steps=steps)


def _dt_fwd(name, dtr, bias):
    rows = dtr.shape[0]
    tm = _row_tile(rows, LANES)

    def body(d_ref, b_ref, o_ref):
        v = d_ref[...] + b_ref[...]
        sp = jnp.maximum(v, 0.0) + jnp.log1p(jnp.exp(-jnp.abs(v)))
        lane = lax.broadcasted_iota(jnp.int32, (tm, LANES), 1)
        ok = _rows_mask(pl.program_id(0), tm) & (lane < SSM_HEADS)
        o_ref[...] = jnp.where(ok, sp, 0.0)

    return pl.pallas_call(
        body, name=name, out_shape=jax.ShapeDtypeStruct((rows, LANES), F32), grid=(rows // tm,),
        in_specs=[pl.BlockSpec((tm, LANES), lambda i: (i, 0)), pl.BlockSpec((1, LANES), lambda i: (0, 0))],
        out_specs=pl.BlockSpec((tm, LANES), lambda i: (i, 0)), compiler_params=_cparams(("parallel",)),
    )(dtr, bias)


def _dt_bwd(name, ddt, dtr, bias):
    rows = dtr.shape[0]
    tm = _row_tile(rows, LANES)

    def body(g_ref, d_ref, b_ref, o_ref, db_ref):
        i = pl.program_id(0)
        lane = lax.broadcasted_iota(jnp.int32, (tm, LANES), 1)
        ok = _rows_mask(i, tm) & (lane < SSM_HEADS)
        dv = jnp.where(ok, g_ref[...] * _sigmoid(d_ref[...] + b_ref[...]), 0.0)
        o_ref[...] = dv.astype(BF16)

        @pl.when(i == 0)
        def _():
            db_ref[...] = jnp.zeros_like(db_ref)

        db_ref[...] += jnp.sum(dv, axis=0, keepdims=True)

    row_spec = pl.BlockSpec((tm, LANES), lambda i: (i, 0))
    vec_spec = pl.BlockSpec((1, LANES), lambda i: (0, 0))
    return pl.pallas_call(
        body, name=name,
        out_shape=(jax.ShapeDtypeStruct((rows, LANES), BF16), jax.ShapeDtypeStruct((1, LANES), F32)),
        grid=(rows // tm,), in_specs=[row_spec, row_spec, vec_spec], out_specs=(row_spec, vec_spec),
        compiler_params=_cparams(("arbitrary",)),
    )(ddt, dtr, bias)


def _gate_fwd(name, y, zx, w, steps=()):
    rows = y.shape[0]
    tm = _row_tile(rows, D_INNER)

    def body(y_ref, z_ref, w_ref, o_ref):
        z = z_ref[...]
        g = y_ref[...] * (z * _sigmoid(z))
        r = lax.rsqrt(jnp.mean(g * g, axis=-1, keepdims=True) + RMS_EPS)
        o_ref[...] = (g * r * w_ref[...]).astype(BF16)

    row_spec = pl.BlockSpec((tm, D_INNER), lambda i: (i, 0))
    return _call(
        body, name=name, out_shape=jax.ShapeDtypeStruct((rows, D_INNER), BF16), grid=(rows // tm,),
        in_specs=[row_spec, row_spec, pl.BlockSpec((1, D_INNER), lambda i: (0, 0))],
        out_specs=row_spec, operands=[y, zx, w], semantics=("parallel",), steps=steps)


def _gate_bwd(name, dyn, y, zx, w):
    rows = y.shape[0]
    tm = _row_tile(rows, D_INNER)

    def body(d_ref, y_ref, z_ref, w_ref, dy_ref, dz_ref, dw_ref):
        i = pl.program_id(0)
        z, yv = z_ref[...], y_ref[...]
        sig = _sigmoid(z)
        sz = z * sig
        g = yv * sz
        r = lax.rsqrt(jnp.mean(g * g, axis=-1, keepdims=True) + RMS_EPS)
        ghat = g * r
        dn = d_ref[...]
        dghat = dn * w_ref[...]
        dg = r * (dghat - ghat * jnp.mean(dghat * ghat, axis=-1, keepdims=True))
        dy_ref[...] = dg * sz
        dz_ref[...] = (dg * yv * sig * (1.0 + z * (1.0 - sig))).astype(BF16)

        @pl.when(i == 0)
        def _():
            dw_ref[...] = jnp.zeros_like(dw_ref)

        dw_ref[...] += jnp.sum(dn * ghat, axis=0, keepdims=True)

    row_spec = pl.BlockSpec((tm, D_INNER), lambda i: (i, 0))
    vec_spec = pl.BlockSpec((1, D_INNER), lambda i: (0, 0))
    return pl.pallas_call(
        body, name=name,
        out_shape=(jax.ShapeDtypeStruct((rows, D_INNER), F32), jax.ShapeDtypeStruct((rows, D_INNER), BF16),
                   jax.ShapeDtypeStruct((1, D_INNER), F32)),
        grid=(rows // tm,), in_specs=[row_spec, row_spec, row_spec, vec_spec],
        out_specs=(row_spec, row_spec, vec_spec), compiler_params=_cparams(("arbitrary",)),
    )(dyn, y, zx, w)


def _split3(x):
    hi = x.astype(BF16)
    r1 = x - hi.astype(F32)
    mid = r1.astype(BF16)
    lo = (r1 - mid.astype(F32)).astype(BF16)
    return hi, mid, lo


def _dot3_data_lhs(x, sel):
    sel16 = sel.astype(F32).astype(BF16)
    hi, mid, lo = _split3(x)
    return _dot(hi, sel16) + _dot(mid, sel16) + _dot(lo, sel16)


def _dot3_data_rhs(sel, x):
    sel16 = sel.astype(F32).astype(BF16)
    hi, mid, lo = _split3(x)
    return _dot(sel16, hi) + _dot(sel16, mid) + _dot(sel16, lo)


def _causal_masks():
    r = lax.broadcasted_iota(jnp.int32, (CHUNK, CHUNK), 0)
    c = lax.broadcasted_iota(jnp.int32, (CHUNK, CHUNK), 1)
    return r >= c, r <= c


def _expand_heads_matrix():
    k = lax.broadcasted_iota(jnp.int32, (LANES, GROUP_W), 0)
    j = lax.broadcasted_iota(jnp.int32, (LANES, GROUP_W), 1)
    return jnp.right_shift(j, 6) == k


def _reduce_heads_matrix():
    j = lax.broadcasted_iota(jnp.int32, (GROUP_W, LANES), 0)
    k = lax.broadcasted_iota(jnp.int32, (GROUP_W, LANES), 1)
    return jnp.right_shift(j, 6) == k


def _reduce_pair_matrix(p):
    j = lax.broadcasted_iota(jnp.int32, (LANES, LANES), 0)
    k = lax.broadcasted_iota(jnp.int32, (LANES, LANES), 1)
    return (2 * p + jnp.right_shift(j, 6)) == k


def _group_cols(ref, g, width):
    return ref.at[:, pl.ds(g * width, width)]


def _ssd_prep(name, dt4, a128, steps=()):
    rows = dt4.shape[1]
    nc = rows // CHUNK

    def body(dt_ref, a_ref, dte_ref, acs_ref, acst_ref):
        causal, _ = _causal_masks()
        expand = _expand_heads_matrix()
        for g in range(N_GROUPS):
            dt = dt_ref[g]
            acs = _dot3_data_rhs(causal, dt) * a_ref[g]
            _group_cols(dte_ref, g, GROUP_W)[...] = _dot3_data_lhs(dt, expand)
            _group_cols(acs_ref, g, GROUP_W)[...] = _dot3_data_lhs(acs, expand)
            acst_ref[pl.ds(g * HEADS_PER_GROUP, HEADS_PER_GROUP), :] = acs.T[0:HEADS_PER_GROUP]

    blk = pl.BlockSpec((CHUNK, D_INNER), lambda c: (c, 0))
    shp = jax.ShapeDtypeStruct((rows, D_INNER), F32)
    return _call(
        body, name=name, out_shape=(shp, shp, jax.ShapeDtypeStruct((nc, SSM_HEADS, CHUNK), F32)), grid=(nc,),
        in_specs=[pl.BlockSpec((N_GROUPS, CHUNK, LANES), lambda c: (0, c, 0)),
                  pl.BlockSpec((N_GROUPS, 1, LANES), lambda c: (0, 0, 0))],
        out_specs=(blk, blk, pl.BlockSpec((None, SSM_HEADS, CHUNK), lambda c: (c, 0, 0))),
        operands=[dt4, a128], semantics=("parallel",), steps=steps)


def _ssd_common(x_ref, b_ref, c_ref, dte_ref, acs_ref):
    x = x_ref[...]
    dt_exp = dte_ref[...]
    acs_exp = acs_ref[...]
    tot_exp = acs_ref[pl.ds(CHUNK - 1, 1), :]
    xdt = x * dt_exp
    e_exp = jnp.exp(acs_exp)
    f_exp = jnp.exp(tot_exp - acs_exp)
    return _causal_masks(), x, dt_exp, acs_exp, tot_exp, xdt, e_exp, f_exp, b_ref[...], c_ref[...]


def _pair_decay(acs_pair, acs_row, e, causal):
    lane = lax.broadcasted_iota(jnp.int32, (CHUNK, LANES), 1)
    mine = (lane < HEAD_DIM) if e == 0 else (lane >= HEAD_DIM)
    a_l = jnp.where(mine, acs_pair, pltpu.roll(acs_pair, HEAD_DIM, 1))
    seg = a_l - acs_row
    dm = jnp.where(causal[0], jnp.exp(jnp.minimum(seg, 0.0)), 0.0)
    dmt = jnp.where(causal[1], jnp.exp(jnp.minimum(-seg, 0.0)), 0.0)
    return dm, dmt


def _ssd_specs(index_of_chunk):
    wide = pl.BlockSpec((CHUNK, D_INNER), lambda c: (index_of_chunk(c), 0))
    b_spec = pl.BlockSpec((CHUNK, D_BC), lambda c: (index_of_chunk(c), D_INNER // D_BC))
    c_spec = pl.BlockSpec((CHUNK, D_BC), lambda c: (index_of_chunk(c), D_INNER // D_BC + 1))
    rows_spec = pl.BlockSpec((None, SSM_HEADS, CHUNK), lambda c: (index_of_chunk(c), 0, 0))
    state_spec = pl.BlockSpec((N_GROUPS, None, D_STATE, GROUP_W), lambda c: (0, index_of_chunk(c), 0, 0))
    return wide, b_spec, c_spec, rows_spec, state_spec


def _ssd_fwd(name, xbc, dt_exp, acs_exp, acs_rows, dskexp, steps=()):
    rows = xbc.shape[0]
    nc = rows // CHUNK

    def body(x_ref, b_ref, c_ref, dte_ref, acs_ref, acst_ref, dsk_ref, y_ref, st_ref, s_scr):
        @pl.when(pl.program_id(0) == 0)
        def _():
            s_scr[...] = jnp.zeros_like(s_scr)

        lane = lax.broadcasted_iota(jnp.int32, (CHUNK, LANES), 1)
        for g in range(N_GROUPS):
            y_g = _group_cols(y_ref, g, GROUP_W)
            causal, x, _, acs_exp_v, tot_exp, xdt, e_exp, f_exp, bm, cm = _ssd_common(
                _group_cols(x_ref, g, GROUP_W), _group_cols(b_ref, g, D_STATE), _group_cols(c_ref, g, D_STATE),
                _group_cols(dte_ref, g, GROUP_W), _group_cols(acs_ref, g, GROUP_W))
            state = s_scr[g]
            st_ref[g] = state
            cb16, bb16 = cm.astype(BF16), bm.astype(BF16)
            cb = _dot_nt(cb16, bb16)
            base = e_exp * _dot(cb16, state.astype(BF16)) + _group_cols(dsk_ref, g, GROUP_W)[...] * x
            for p in range(HEADS_PER_GROUP // 2):
                sl = slice(p * LANES, (p + 1) * LANES)
                xp = xdt[:, sl].astype(BF16)
                yd = []
                for e in range(2):
                    acs_row = acst_ref[pl.ds(g * HEADS_PER_GROUP + 2 * p + e, 1), :]
                    dm, _ = _pair_decay(acs_exp_v[:, sl], acs_row, e, causal)
                    yd.append(_dot((cb * dm).astype(BF16), xp))
                y_g[:, sl] = jnp.where(lane < HEAD_DIM, yd[0], yd[1]) + base[:, sl]
            s_scr[g] = jnp.exp(tot_exp) * state + _dot_tn(bb16, (f_exp * xdt).astype(BF16))

    wide, b_spec, c_spec, rows_spec, state_spec = _ssd_specs(lambda c: c)
    return _call(
        body, name=name,
        out_shape=(jax.ShapeDtypeStruct((rows, D_INNER), F32),
                   jax.ShapeDtypeStruct((N_GROUPS, nc, D_STATE, GROUP_W), F32)),
        grid=(nc,),
        in_specs=[wide, b_spec, c_spec, wide, wide, rows_spec, pl.BlockSpec((1, D_INNER), lambda c: (0, 0))],
        out_specs=(wide, state_spec),
        scratch_shapes=[pltpu.VMEM((N_GROUPS, D_STATE, GROUP_W), F32)],
        operands=[xbc, xbc, xbc, dt_exp, acs_exp, acs_rows, dskexp], semantics=("arbitrary",), steps=steps)


def _ssd_bwd(name, xbc, dt_exp, acs_exp, acs_rows, dt4, a128, dskexp, dy, states, steps=()):
    rows = xbc.shape[0]
    nc = rows // CHUNK
    last = nc - 1

    def body(x_ref, b_ref, c_ref, dte_ref, acs_ref, acst_ref, dt_all, a128_all, dsk_all, dy_all, st_all,
             dx_all, db_all, dc_all, ddt_all, dalog_all, ddsk_all, ds_all):
        @pl.when(pl.program_id(0) == 0)
        def _():
            ds_all[...] = jnp.zeros_like(ds_all)
            dalog_all[...] = jnp.zeros_like(dalog_all)
            ddsk_all[...] = jnp.zeros_like(ddsk_all)

        for g in range(N_GROUPS):
            group(g, _group_cols(x_ref, g, GROUP_W), _group_cols(b_ref, g, D_STATE), _group_cols(c_ref, g, D_STATE),
                  _group_cols(dte_ref, g, GROUP_W), _group_cols(acs_ref, g, GROUP_W), acst_ref, dt_all.at[g],
                  a128_all.at[g], _group_cols(dsk_all, g, GROUP_W), _group_cols(dy_all, g, GROUP_W), st_all.at[g],
                  _group_cols(dx_all, g, GROUP_W), _group_cols(db_all, g, D_STATE), _group_cols(dc_all, g, D_STATE),
                  ddt_all.at[g], dalog_all.at[g], ddsk_all.at[g], ds_all.at[g])

    def group(g, x_ref, b_ref, c_ref, dte_ref, acs_ref, acst_ref, dt_ref, a128_ref, dsk_ref, dy_ref, st_ref,
              dx_ref, db_ref, dc_ref, ddt_ref, dalog_ref, ddsk_ref, ds_scr):
        causal, x, dt_exp, acs_exp_v, tot_exp, xdt, e_exp, f_exp, bm, cm = _ssd_common(
            x_ref, b_ref, c_ref, dte_ref, acs_ref)
        dt = dt_ref[...]
        reduce_heads = _reduce_heads_matrix()
        state, dstate = st_ref[...], ds_scr[...]
        dyv = dy_ref[...]
        cb16, bb16 = cm.astype(BF16), bm.astype(BF16)
        s16, ds16 = state.astype(BF16), dstate.astype(BF16)
        cb = _dot_nt(cb16, bb16)
        cbt = _dot_nt(bb16, cb16)
        cs = _dot(cb16, s16)
        bds = _dot(bb16, ds16)
        edy = e_exp * dyv
        fx = f_exp * xdt
        dxdt_base = f_exp * bds
        dc_acc = _dot_nt(edy.astype(BF16), s16)
        db_acc = _dot_nt(fx.astype(BF16), ds16)
        ds_scr[...] = jnp.exp(tot_exp) * dstate + _dot_tn(cb16, edy.astype(BF16))
        q = fx * bds
        dacs = _dot3_data_lhs(edy * cs - q, reduce_heads)
        dtot = jnp.sum(_dot3_data_lhs(q + jnp.exp(tot_exp) * dstate * state, reduce_heads), axis=0, keepdims=True)
        ddsk_ref[...] += jnp.sum(_dot3_data_lhs(dyv * x, reduce_heads), axis=0, keepdims=True)
        lane = lax.broadcasted_iota(jnp.int32, (CHUNK, LANES), 1)
        dcb = jnp.zeros((CHUNK, CHUNK), F32)
        dcbt = jnp.zeros((CHUNK, CHUNK), F32)
        ddt_x = jnp.zeros((CHUNK, LANES), F32)
        for p in range(HEADS_PER_GROUP // 2):
            sl = slice(p * LANES, (p + 1) * LANES)
            xp, dyp = xdt[:, sl], dyv[:, sl]
            xp16, dyp16 = xp.astype(BF16), dyp.astype(BF16)
            dxh = []
            for e in range(2):
                h = 2 * p + e
                mine = (lane < HEAD_DIM) if e == 0 else (lane >= HEAD_DIM)
                acs_row = acst_ref[pl.ds(g * HEADS_PER_GROUP + h, 1), :]
                dm, dmt = _pair_decay(acs_exp_v[:, sl], acs_row, e, causal)
                m, mt = cb * dm, cbt * dmt
                xh16 = jnp.where(mine, xp, 0.0).astype(BF16)
                dyh16 = jnp.where(mine, dyp, 0.0).astype(BF16)
                d_m = _dot_nt(dyh16, xp16)
                d_mt = _dot_nt(xh16, dyp16)
                dacs_h = (jnp.sum(d_m * m, axis=-1, keepdims=True)
                          - jnp.sum(d_mt * mt, axis=-1, keepdims=True))
                dacs = dacs + jnp.where(lane == h, dacs_h, 0.0)
                dcb = dcb + d_m * dm
                dcbt = dcbt + d_mt * dmt
                dxh.append(_dot(mt.astype(BF16), dyp16))
            dxdt = jnp.where(lane < HEAD_DIM, dxh[0], dxh[1]) + dxdt_base[:, sl]
            dx_ref[:, sl] = dxdt * dt_exp[:, sl] + dsk_ref[:, sl] * dyp
            ddt_x = ddt_x + _dot3_data_lhs(dxdt * x[:, sl], _reduce_pair_matrix(p))
        dc_ref[...] = dc_acc + _dot(dcb.astype(BF16), bb16)
        db_ref[...] = db_acc + _dot(dcbt.astype(BF16), cb16)
        row = lax.broadcasted_iota(jnp.int32, (CHUNK, LANES), 0)
        dacs = dacs + jnp.where(row == CHUNK - 1, dtot, 0.0)
        da = _dot3_data_rhs(causal[1], dacs)
        ddt_ref[...] = da * a128_ref[...] + ddt_x
        dalog_ref[...] += jnp.sum(da * dt, axis=0, keepdims=True) * a128_ref[...]

    wide, b_spec, c_spec, rows_spec, state_spec = _ssd_specs(lambda c: last - c)
    heads_spec = pl.BlockSpec((N_GROUPS, CHUNK, LANES), lambda c: (0, last - c, 0))
    vec_spec = pl.BlockSpec((N_GROUPS, 1, LANES), lambda c: (0, 0, 0))
    bc_out = pl.BlockSpec((CHUNK, D_BC), lambda c: (last - c, 0))
    vec_shape = jax.ShapeDtypeStruct((N_GROUPS, 1, LANES), F32)
    return _call(
        body, name=name,
        out_shape=(jax.ShapeDtypeStruct((rows, D_INNER), F32), jax.ShapeDtypeStruct((rows, D_BC), F32),
                   jax.ShapeDtypeStruct((rows, D_BC), F32), jax.ShapeDtypeStruct((N_GROUPS, rows, LANES), F32),
                   vec_shape, vec_shape),
        grid=(nc,),
        in_specs=[wide, b_spec, c_spec, wide, wide, rows_spec, heads_spec, vec_spec,
                  pl.BlockSpec((1, D_INNER), lambda c: (0, 0)), wide, state_spec],
        out_specs=(wide, bc_out, bc_out, heads_spec, vec_spec, vec_spec),
        scratch_shapes=[pltpu.VMEM((N_GROUPS, D_STATE, GROUP_W), F32)],
        operands=[xbc, xbc, xbc, dt_exp, acs_exp, acs_rows, dt4, a128, dskexp, dy, states],
        semantics=("arbitrary",), steps=steps)


def _attn_visible(b, heads=1):
    row = jnp.bitwise_and(lax.broadcasted_iota(jnp.int32, (heads * CHUNK, 3 * CHUNK), 0), CHUNK - 1)
    col = lax.broadcasted_iota(jnp.int32, (heads * CHUNK, 3 * CHUNK), 1)
    bb = b + jnp.zeros_like(col)
    meta = (col < CHUNK) & (bb >= 1) & (col >= PAD_ROWS)
    prev = (col >= CHUNK) & (col < 2 * CHUNK) & (bb >= 2) & ((col - CHUNK) > row)
    cur = (col >= 2 * CHUNK) & ((col - 2 * CHUNK) <= row) & ((bb >= 1) | ((col - 2 * CHUNK) >= PAD_ROWS))
    return meta | prev | cur


def _attn_visible4(b):
    return _attn_visible(b, 4)


def _stack_heads(q_ref, sink_ref, kvh, scale):
    lane = lax.broadcasted_iota(jnp.int32, (CHUNK, LANES), 1)
    parts, sinks = [], []
    for pp in range(2):
        pair = kvh * 2 + pp
        qp = q_ref[:, pair * LANES:(pair + 1) * LANES] * scale
        for e in range(2):
            mine = (lane < HEAD_DIM) if e == 0 else (lane >= HEAD_DIM)
            parts.append(jnp.where(mine, qp, 0.0).astype(BF16))
            sinks.append(jnp.full((CHUNK, 1), sink_ref[2 * pair + e], F32))
    return jnp.concatenate(parts, axis=0), jnp.concatenate(sinks, axis=0)


def _attn_operands(q_ref, k0, kp, kc, v0, vp, vc, sink_ref):
    kcat, vcat, q4, sink4 = [], [], [], []
    for kvh in range(N_KV_HEADS):
        ksl = slice(kvh * LANES, (kvh + 1) * LANES)
        kcat.append(jnp.concatenate([k0[:, ksl], kp[:, ksl], kc[:, ksl]], axis=0).astype(BF16))
        vcat.append(jnp.concatenate([v0[:, ksl], vp[:, ksl], vc[:, ksl]], axis=0).astype(BF16))
        stacked, sinks = _stack_heads(q_ref, sink_ref, kvh, ATTN_SCALE)
        q4.append(stacked)
        sink4.append(sinks)
    return kcat, vcat, q4, sink4


def _attn_probs(q4, kcat, visible, sink4):
    heads = range(N_KV_HEADS)
    s = [jnp.where(visible, _dot_nt(q4[h], kcat[h]), NEG_INF) for h in heads]
    m = [jnp.maximum(jnp.max(s[h], axis=-1, keepdims=True), sink4[h]) for h in heads]
    pe = [jnp.exp(s[h] - m[h]) for h in heads]
    pe_sink = [jnp.exp(sink4[h] - m[h]) for h in heads]
    inv = [1.0 / (jnp.sum(pe[h], axis=-1, keepdims=True) + pe_sink[h]) for h in heads]
    return [pe[h] * inv[h] for h in heads], [pe_sink[h] * inv[h] for h in heads]


def _unstack_pairs(stacked, pp):
    lane = lax.broadcasted_iota(jnp.int32, (CHUNK, LANES), 1)
    return jnp.where(lane < HEAD_DIM, stacked[(2 * pp) * CHUNK:(2 * pp + 1) * CHUNK],
                     stacked[(2 * pp + 1) * CHUNK:(2 * pp + 2) * CHUNK])


def _attn_specs(colblock):
    blk = lambda f: pl.BlockSpec((CHUNK, 2 * D_KV), f)
    return [blk(lambda b: (0, colblock)), blk(lambda b: (jnp.maximum(b - 1, 0), colblock)), blk(lambda b: (b, colblock))]


def _attn_fwd(name, q, kv2, sinks, steps=()):
    rows = q.shape[0]

    def body(q_ref, k0, kp, kc, v0, vp, vc, sink_ref, o_ref):
        visible = _attn_visible4(pl.program_id(0))
        kcat, vcat, q4, sink4 = _attn_operands(q_ref, k0, kp, kc, v0, vp, vc, sink_ref)
        pn, _ = _attn_probs(q4, kcat, visible, sink4)
        o4 = [_dot(pn[h].astype(BF16), vcat[h]) for h in range(N_KV_HEADS)]
        for kvh in range(N_KV_HEADS):
            for pp in range(2):
                qsl = slice((kvh * 2 + pp) * LANES, (kvh * 2 + pp + 1) * LANES)
                o_ref[:, qsl] = _unstack_pairs(o4[kvh], pp).astype(BF16)

    return _call(
        body, name=name, out_shape=jax.ShapeDtypeStruct((rows, D_MODEL), BF16), grid=(rows // CHUNK,),
        in_specs=[pl.BlockSpec((CHUNK, D_MODEL), lambda b: (b, 0))] + _attn_specs(0) + _attn_specs(1)
        + [pl.BlockSpec(memory_space=pltpu.SMEM)],
        out_specs=pl.BlockSpec((CHUNK, D_MODEL), lambda b: (b, 0)),
        operands=[q, kv2, kv2, kv2, kv2, kv2, kv2, sinks], semantics=("parallel",), steps=steps)


def _attn_bwd(name, q, kv2, sinks, do, steps=()):
    rows = q.shape[0]

    def body(q_ref, k0, kp, kc, v0, vp, vc, sink_ref, do_ref,
             dq_ref, dkc_ref, dkp_ref, dvc_ref, dvp_ref, dkm_ref, dvm_ref, dsink_ref):
        @pl.when(pl.program_id(0) == 0)
        def _():
            dkm_ref[...] = jnp.zeros_like(dkm_ref)
            dvm_ref[...] = jnp.zeros_like(dvm_ref)
            dsink_ref[...] = jnp.zeros_like(dsink_ref)

        visible = _attn_visible4(pl.program_id(0))
        heads = range(N_KV_HEADS)
        lane1 = lax.broadcasted_iota(jnp.int32, (1, LANES), 1)
        kcat, vcat, q4, sink4 = _attn_operands(q_ref, k0, kp, kc, v0, vp, vc, sink_ref)
        do4 = [_stack_heads(do_ref, sink_ref, h, 1.0)[0] for h in heads]
        pn, psink = _attn_probs(q4, kcat, visible, sink4)
        dp = [_dot_nt(do4[h], vcat[h]) for h in heads]
        delta = [jnp.sum(pn[h] * dp[h], axis=-1, keepdims=True) for h in heads]
        ds16 = [(pn[h] * (dp[h] - delta[h])).astype(BF16) for h in heads]
        dq4 = [_dot(ds16[h], kcat[h]) for h in heads]
        dk_acc = [_dot_tn(ds16[h], q4[h]) for h in heads]
        dv_acc = [_dot_tn(pn[h].astype(BF16), do4[h]) for h in heads]
        dsink = jnp.zeros((1, LANES), F32)
        for kvh in heads:
            ksl = slice(kvh * LANES, (kvh + 1) * LANES)
            sink_terms = psink[kvh] * delta[kvh]
            for j in range(4):
                part = jnp.sum(sink_terms[j * CHUNK:(j + 1) * CHUNK], axis=0, keepdims=True)
                dsink = dsink - jnp.where(lane1 == kvh * 4 + j, part, 0.0)
            for pp in range(2):
                qsl = slice((kvh * 2 + pp) * LANES, (kvh * 2 + pp + 1) * LANES)
                dq_ref[:, qsl] = (_unstack_pairs(dq4[kvh], pp) * ATTN_SCALE).astype(BF16)
            dkm_ref[:, ksl] += dk_acc[kvh][0:CHUNK]
            dvm_ref[:, ksl] += dv_acc[kvh][0:CHUNK]
            dkp_ref[:, ksl] = dk_acc[kvh][CHUNK:2 * CHUNK]
            dvp_ref[:, ksl] = dv_acc[kvh][CHUNK:2 * CHUNK]
            dkc_ref[:, ksl] = dk_acc[kvh][2 * CHUNK:3 * CHUNK]
            dvc_ref[:, ksl] = dv_acc[kvh][2 * CHUNK:3 * CHUNK]
        dsink_ref[...] += dsink

    qspec = pl.BlockSpec((CHUNK, D_MODEL), lambda b: (b, 0))
    kvspec = pl.BlockSpec((CHUNK, 2 * D_KV), lambda b: (b, 0))
    fixed = pl.BlockSpec((CHUNK, 2 * D_KV), lambda b: (0, 0))
    kv_shape = jax.ShapeDtypeStruct((rows, 2 * D_KV), F32)
    meta_shape = jax.ShapeDtypeStruct((CHUNK, 2 * D_KV), F32)
    return _call(
        body, name=name,
        out_shape=(jax.ShapeDtypeStruct((rows, D_MODEL), BF16), kv_shape, kv_shape, kv_shape, kv_shape,
                   meta_shape, meta_shape, jax.ShapeDtypeStruct((1, LANES), F32)),
        grid=(rows // CHUNK,),
        in_specs=[qspec] + _attn_specs(0) + _attn_specs(1) + [pl.BlockSpec(memory_space=pltpu.SMEM), qspec],
        out_specs=(qspec, kvspec, kvspec, kvspec, kvspec, fixed, fixed, pl.BlockSpec((1, LANES), lambda b: (0, 0))),
        operands=[q, kv2, kv2, kv2, kv2, kv2, kv2, sinks, do], semantics=("arbitrary",), steps=steps)


def _kv_grad_combine(name, dk_cur, dk_prev, dk_meta, dv_cur, dv_prev, dv_meta):
    rows = dk_cur.shape[0]
    nb = rows // CHUNK
    width = 2 * D_KV

    def body(kc_ref, kp_ref, km_ref, vc_ref, vp_ref, vm_ref, o_ref):
        jj = pl.program_id(0) + jnp.zeros((CHUNK, 1), jnp.int32)
        for half, (c_ref, p_ref, m_ref) in enumerate(((kc_ref, kp_ref, km_ref), (vc_ref, vp_ref, vm_ref))):
            total = c_ref[...] + jnp.where(jj < nb - 1, p_ref[...], 0.0) + jnp.where(jj == 0, m_ref[...], 0.0)
            o_ref[:, half * width:(half + 1) * width] = total.astype(BF16)

    blk = lambda f: pl.BlockSpec((CHUNK, width), f)
    three = lambda: [blk(lambda j: (j, 0)), blk(lambda j: (jnp.minimum(j + 1, nb - 1), 0)), blk(lambda j: (0, 0))]
    return pl.pallas_call(
        body, name=name, out_shape=jax.ShapeDtypeStruct((rows, 2 * width), BF16), grid=(nb,),
        in_specs=three() + three(), out_specs=pl.BlockSpec((CHUNK, 2 * width), lambda j: (j, 0)),
        compiler_params=_cparams(("parallel",)),
    )(dk_cur, dk_prev, dk_meta, dv_cur, dv_prev, dv_meta)


def _adamw(name, w, g, m, v):
    rows, width = w.shape
    tr = rows
    for cand in range(8, rows + 1, 8):
        if rows % cand == 0 and cand * width * 4 <= (1 << 20):
            tr = cand

    def body(w_ref, g_ref, m_ref, v_ref, d_ref, mo_ref, vo_ref):
        gv = g_ref[...]
        mn = ADAM_B1 * m_ref[...] + (1.0 - ADAM_B1) * gv
        vn = ADAM_B2 * v_ref[...] + (1.0 - ADAM_B2) * (gv * gv)
        m_hat = mn / (1.0 - ADAM_B1 ** ADAM_STEP)
        v_hat = vn / (1.0 - ADAM_B2 ** ADAM_STEP)
        d_ref[...] = -ADAM_LR * (m_hat / (jnp.sqrt(v_hat) + ADAM_EPS) + ADAM_WD * w_ref[...])
        mo_ref[...] = mn
        vo_ref[...] = vn

    blk = pl.BlockSpec((tr, width), lambda i: (i, 0))
    shp = jax.ShapeDtypeStruct((rows, width), F32)
    return pl.pallas_call(
        body, name=name, out_shape=(shp, shp, shp), grid=(rows // tr,), in_specs=[blk] * 4, out_specs=(blk,) * 3,
        compiler_params=_cparams(("parallel",)),
    )(w, g, m, v)


class _GivenWeights:
    def __init__(self, p):
        self.p = p
        self.grads = {}

    def weight(self, name, layer=None):
        return self.p[name] if layer is None else self.p[name][layer]

    def steps(self, kernel):
        return ()

    def grad(self, name, layer, g):
        self.grads[(name, layer)] = g


def _ffn_fwd(tag, h, hn, p, i, plan):
    up = _mm_nn_bychip(f"ffn{tag}_up", hn, plan.weight("f_w_up", i))
    act = _ffn_conv_fwd(f"ffn{tag}_conv", up, p["f_conv_w"][i], p["f_conv_b"][i:i + 1], steps=plan.steps(f"ffn{tag}_conv"))
    pre = _mm(f"ffn{tag}_down", act, plan.weight("f_w_down", i), "nn")
    return pre, (h, hn, up, act, pre)


def _ffn_bwd(tag, dpre, saved, p, i, plan):
    h, hn, up, act, pre = saved
    plan.grad("f_w_down", i, _mm(f"ffn{tag}_down_dw", act, dpre, "tn", out_dtype=BF16))
    dact = _mm(f"ffn{tag}_down_dx", dpre, plan.weight("f_w_down", i), "nt", steps=plan.steps(f"ffn{tag}_down_dx"))
    dug, duv, gwg, gwv, gbg, gbv = _ffn_conv_bwd(f"ffn{tag}_conv_bwd", up, dact, p["f_conv_w"][i], p["f_conv_b"][i:i + 1],
                                                 steps=plan.steps(f"ffn{tag}_conv_bwd"))
    g_cw, g_cb = jnp.concatenate([gwg, gwv], axis=1), jnp.concatenate([gbg, gbv], axis=1)
    w_up = plan.weight("f_w_up", i)
    n = w_up.shape[2]
    dhn = _mm_nt_bychip(f"ffn{tag}_up_dx_gate", dug, w_up, 0)
    dhn = _mm_nt_bychip(f"ffn{tag}_up_dx_val", duv, w_up, N_CHIPS // 2, acc=dhn)
    g_up = _mm_tn_bychip(f"ffn{tag}_up_dw_gate", hn, dug, n, 0)
    plan.grad("f_w_up", i, _mm_tn_bychip(f"ffn{tag}_up_dw_val", hn, duv, n, N_CHIPS // 2, into=g_up))
    return dhn, dict(f_conv_w=g_cw, f_conv_b=g_cb)


def _lanes_pad(a, width=LANES):
    return jnp.pad(a, [(0, 0)] * (a.ndim - 1) + [(0, width - a.shape[-1])])


def _dup_heads(w):
    rows = w.shape[0]
    w = w.reshape(rows, 2 * N_KV_HEADS, 1, HEAD_DIM)
    return jnp.broadcast_to(w, (rows, 2 * N_KV_HEADS, 2, HEAD_DIM)).reshape(rows, 4 * D_KV)


def _undup_heads(g):
    rows = g.shape[0]
    return g.reshape(rows, 2 * N_KV_HEADS, 2, HEAD_DIM).sum(axis=2).reshape(rows, 2 * D_KV)


def _local_step(x2, target, p, plan):
    seq = x2.shape[0]
    rows = seq + CHUNK
    g = {}

    h0 = jnp.concatenate([jnp.zeros((PAD_ROWS, D_MODEL), F32), p["meta_tokens"], x2], axis=0)

    w_in = plan.weight("a_w_in")
    w_dt = jnp.pad(w_in[D_MAIN:], ((0, LANES - SSM_HEADS), (0, 0)))
    dt_bias = _lanes_pad(p["a_dt_bias"])
    a_neg = -jnp.exp(p["a_a_log"].reshape(N_GROUPS, HEADS_PER_GROUP))
    a128 = _lanes_pad(a_neg.reshape(N_GROUPS, 1, HEADS_PER_GROUP))
    dskexp = jnp.repeat(p["a_d_skip"].reshape(SSM_HEADS), HEAD_DIM).reshape(1, D_INNER)

    hn0 = _rms_fwd("a_norm", h0, p["a_norm_pre"])
    zx = _mm("a_in_main", hn0, w_in, "nt", k_rows=D_MAIN, steps=plan.steps("a_in_main"))
    dtr = _mm("a_in_dt", hn0, w_dt, "nt")
    xbc = _conv4_fwd("a_conv", zx, p["a_conv_w"], p["a_conv_b"], steps=plan.steps("a_conv"))
    dt = _dt_fwd("a_dt", dtr, dt_bias)
    dt4 = _lanes_pad(dt[:, :SSM_HEADS].reshape(rows, N_GROUPS, HEADS_PER_GROUP).transpose(1, 0, 2))
    dt_exp, acs_exp, acs_rows = _ssd_prep("a_ssd_prep", dt4, a128, steps=plan.steps("a_ssd_prep"))
    y, states = _ssd_fwd("a_ssd", xbc, dt_exp, acs_exp, acs_rows, dskexp, steps=plan.steps("a_ssd"))
    yn = _gate_fwd("a_gate", y, zx, p["a_gate_norm"], steps=plan.steps("a_gate"))
    mix = _mm("a_out", yn, plan.weight("a_w_out"), "nn")
    h1, (hn_f0,) = _resid_norm_fwd("a_resid", h0, mix, p["a_norm_post"], [p["f_norm_pre"][0:1]])

    pre_f0, ffn0 = _ffn_fwd("0", h1, hn_f0, p, 0, plan)
    h2, (hkv, hn2) = _resid_norm_fwd("ffn0_resid", h1, pre_f0, p["f_norm_post"][0:1], [p["kv_norm"], p["b_norm_pre"]])

    w_kv2 = _dup_heads(plan.weight("w_kv"))
    kv2 = _mm("kv_proj", hkv, w_kv2, "nn")
    q = _mm("b_q", hn2, plan.weight("b_w_q"), "nn")
    sinks = p["b_sinks"].reshape(N_Q_HEADS)
    o = _attn_fwd("b_attn", q, kv2, sinks, steps=plan.steps("b_attn"))
    attn = _mm("b_o", o, plan.weight("b_w_o"), "nn", steps=plan.steps("b_o"))
    h3, (hn_f1,) = _resid_norm_fwd("b_resid", h2, attn, p["b_norm_post"], [p["f_norm_pre"][1:2]])

    pre_f1, ffn1 = _ffn_fwd("1", h3, hn_f1, p, 1, plan)
    dh, loss_vec, dpre_f1, g_post1 = _resid_norm_loss("ffn1_resid_loss", h3, pre_f1, p["f_norm_post"][1:2], target)
    loss = loss_vec[0, 0]

    dhn_f1, g1 = _ffn_bwd("1", dpre_f1, ffn1, p, 1, plan)
    dh, g_pre1, dpre, g["b_norm_post"] = _norm_bwd_add("ffn1_norm_bwd", dh, dhn_f1, h3, p["f_norm_pre"][1:2],
                                                        then=(attn, p["b_norm_post"]))
    plan.grad("b_w_o", None, _mm("b_o_dw", o, dpre, "tn", out_dtype=BF16))
    do = _mm("b_o_dx", dpre, plan.weight("b_w_o"), "nt", steps=plan.steps("b_o_dx"))
    dq, dkc, dkp, dvc, dvp, dkm, dvm, dsink = _attn_bwd("b_attn_bwd", q, kv2, sinks, do, steps=plan.steps("b_attn_bwd"))
    g["b_sinks"] = dsink[:, :N_Q_HEADS]
    dhn2 = _mm("b_q_dx", dq, plan.weight("b_w_q"), "nt")
    plan.grad("b_w_q", None, _mm("b_q_dw", hn2, dq, "tn", out_dtype=BF16))
    dh, g["b_norm_pre"] = _norm_bwd_add("b_norm_bwd", dh, dhn2, h2, p["b_norm_pre"])
    dkv2 = _kv_grad_combine("kv_grad", dkc, dkp, dkm, dvc, dvp, dvm)
    dhkv = _mm("kv_proj_dx", dkv2, w_kv2, "nt")
    plan.grad("w_kv", None, _undup_heads(_mm("kv_proj_dw", hkv, dkv2, "tn")))
    dh, g["kv_norm"], dpre_f0, g_post0 = _norm_bwd_add("kv_norm_bwd", dh, dhkv, h2, p["kv_norm"],
                                                       then=(pre_f0, p["f_norm_post"][0:1]))

    dhn_f0, g0 = _ffn_bwd("0", dpre_f0, ffn0, p, 0, plan)
    dh, g_pre0, dpre, g["a_norm_post"] = _norm_bwd_add("ffn0_norm_bwd", dh, dhn_f0, h1, p["f_norm_pre"][0:1],
                                                        then=(mix, p["a_norm_post"]))
    g["f_norm_post"] = jnp.concatenate([g_post0, g_post1], axis=0)
    g["f_norm_pre"] = jnp.concatenate([g_pre0, g_pre1], axis=0)
    g["f_conv_w"] = jnp.stack([g0["f_conv_w"], g1["f_conv_w"]])
    g["f_conv_b"] = jnp.concatenate([g0["f_conv_b"], g1["f_conv_b"]], axis=0)
    plan.grad("a_w_out", None, _mm("a_out_dw", yn, dpre, "tn", out_dtype=BF16))
    dyn = _mm("a_out_dx", dpre, plan.weight("a_w_out"), "nt", steps=plan.steps("a_out_dx"))
    dy, dz, g["a_gate_norm"] = _gate_bwd("a_gate_bwd", dyn, y, zx, p["a_gate_norm"])
    dxs, dbm, dcm, ddt4, dalog, ddsk = _ssd_bwd("a_ssd_bwd", xbc, dt_exp, acs_exp, acs_rows, dt4, a128, dskexp, dy, states,
                                               steps=plan.steps("a_ssd_bwd"))
    g["a_a_log"] = dalog[:, 0, :HEADS_PER_GROUP].reshape(1, SSM_HEADS)
    g["a_d_skip"] = ddsk[:, 0, :HEADS_PER_GROUP].reshape(1, SSM_HEADS)
    ddt = _lanes_pad(ddt4[:, :, :HEADS_PER_GROUP].transpose(1, 0, 2).reshape(rows, SSM_HEADS))
    ddtr, dbias = _dt_bwd("a_dt_bwd", ddt, dtr, dt_bias)
    g["a_dt_bias"] = dbias[:, :SSM_HEADS]
    dxp, gw_x, gb_x = _conv4_bwd("a_conv_bwd_x", zx, dxs, p["a_conv_w"], p["a_conv_b"], 0)
    dbp, gw_b, gb_b = _conv4_bwd("a_conv_bwd_b", zx, dbm, p["a_conv_w"], p["a_conv_b"], D_INNER)
    dcp, gw_c, gb_c = _conv4_bwd("a_conv_bwd_c", zx, dcm, p["a_conv_w"], p["a_conv_b"], D_INNER + D_BC)
    g["a_conv_w"] = jnp.concatenate([gw_x, gw_b, gw_c], axis=1)
    g["a_conv_b"] = jnp.concatenate([gb_x, gb_b, gb_c], axis=1)
    dzx = jnp.concatenate([dz, dxp, dbp, dcp], axis=1)
    g_main = _mm("a_in_main_dw", dzx, hn0, "tn", out_dtype=BF16, steps=plan.steps("a_in_main_dw"))
    g_dt = _mm("a_in_dt_dw", ddtr, hn0, "tn", out_dtype=BF16)
    plan.grad("a_w_in", None, jnp.concatenate([g_main, g_dt[:SSM_HEADS]], axis=0))
    dhn0 = _mm("a_in_dt_dx", ddtr, w_dt, "nn", steps=plan.steps("a_in_dt_dx"))
    dhn0 = _mm("a_in_main_dx", dzx, w_in, "nn", acc=dhn0, steps=plan.steps("a_in_main_dx"))
    dh, g["a_norm_pre"] = _norm_bwd_add("a_norm_bwd", dh, dhn0, h0, p["a_norm_pre"])

    g["meta_tokens"] = dh[PAD_ROWS:CHUNK]
    return loss, dh[CHUNK:], g


ANY = pl.BlockSpec(memory_space=pl.ANY)
VMEM_SPEC = pl.BlockSpec(memory_space=pltpu.VMEM)


def _allgather_small(name, shard):
    rows = shard.shape[0]

    def body(s_ref, o_ref, send_sems, recv_sems):
        x, y, c = _place()
        me = 2 * x + y
        o_ref[me] = s_ref[...]
        chips = _other_chips(x, y)
        sends = [pltpu.make_async_remote_copy(s_ref, o_ref.at[me], send_sems.at[j], recv_sems.at[j],
                                              device_id=(cx, cy, c), device_id_type=MESH)
                 for j, (cx, cy) in enumerate(chips)]
        for cp in sends:
            cp.start()
        for j, (cx, cy) in enumerate(chips):
            pltpu.make_async_remote_copy(s_ref, o_ref.at[2 * cx + cy], send_sems.at[j], recv_sems.at[j],
                                         device_id=(cx, cy, c), device_id_type=MESH).wait_recv()
        for cp in sends:
            cp.wait_send()

    return pl.pallas_call(
        body, name=name, out_shape=jax.ShapeDtypeStruct((N_CHIPS, rows, LANES), F32),
        in_specs=[VMEM_SPEC], out_specs=VMEM_SPEC,
        scratch_shapes=[pltpu.SemaphoreType.DMA((3,)), pltpu.SemaphoreType.DMA((3,))],
        compiler_params=pltpu.CompilerParams(vmem_limit_bytes=VMEM_LIMIT),
    )(shard)


def _row_block(rows, width, itemsize, align, budget=2 << 20):
    best = rows
    for cand in range(align, rows + 1, align):
        if rows % cand == 0 and cand * width * itemsize <= budget:
            best = cand
    return best


def _cast_into_slot(name, chip, w2d):
    rows, width = w2d.shape
    tr = _row_block(rows, width, 4, 16)

    def body(chip_ref, w_ref, o_ref):
        o_ref[...] = w_ref[...].astype(BF16)

    return pl.pallas_call(
        body, name=name, out_shape=jax.ShapeDtypeStruct((N_CHIPS, rows, width), BF16),
        grid_spec=pltpu.PrefetchScalarGridSpec(
            num_scalar_prefetch=1, grid=(rows // tr,),
            in_specs=[pl.BlockSpec((tr, width), lambda i, chip_ref: (i, 0))],
            out_specs=pl.BlockSpec((None, tr, width), lambda i, chip_ref: (chip_ref[0], i, 0))),
        compiler_params=_cparams(("parallel",)),
    )(chip, w2d)


def _allreduce_small(name, vec):
    rows = vec.shape[0]

    def body(v_ref, o_ref, buf, send_sems, recv_sems):
        x, y, c = _place()
        me = 4 * x + 2 * y + c
        buf[me] = v_ref[...]

        def peer(k):
            kx, ky, kc = (k >> 2) & 1, (k >> 1) & 1, k & 1
            return (1 - x if kx else x, 1 - y if ky else y, 1 - c if kc else c)

        sends = []
        for k in range(1, N_DEV):
            cp = pltpu.make_async_remote_copy(v_ref, buf.at[me], send_sems.at[k - 1], recv_sems.at[k - 1],
                                              device_id=peer(k), device_id_type=MESH)
            cp.start()
            sends.append(cp)
        for k in range(1, N_DEV):
            px, py, pc = peer(k)
            pltpu.make_async_remote_copy(v_ref, buf.at[4 * px + 2 * py + pc], send_sems.at[k - 1], recv_sems.at[k - 1],
                                         device_id=(px, py, pc), device_id_type=MESH).wait_recv()
        for cp in sends:
            cp.wait_send()
        acc = buf[0]
        for d in range(1, N_DEV):
            acc = acc + buf[d]
        o_ref[...] = acc

    return pl.pallas_call(
        body, name=name, out_shape=jax.ShapeDtypeStruct((rows, LANES), F32),
        in_specs=[VMEM_SPEC], out_specs=VMEM_SPEC,
        scratch_shapes=[pltpu.VMEM((N_DEV, rows, LANES), F32), pltpu.SemaphoreType.DMA((N_DEV - 1,)),
                        pltpu.SemaphoreType.DMA((N_DEV - 1,))],
        compiler_params=pltpu.CompilerParams(vmem_limit_bytes=VMEM_LIMIT),
    )(vec)


def _rs_pair_add(name, place, grads, partner, split="rows"):
    _, half_rows, width = partner.shape
    tr = _row_block(half_rows, width, 2, 16)
    nb = half_rows // tr
    if split == "rows":
        mine = pl.BlockSpec((None, tr, width), lambda s, i, pr: (s, pr[1] * nb + i, 0))
    else:
        mine = pl.BlockSpec((None, tr, width), lambda s, i, pr: (s, i, pr[1]))

    def body(place_ref, g_ref, p_ref, o_ref):
        o_ref[...] = (g_ref[...].astype(F32) + p_ref[...].astype(F32)).astype(BF16)

    return pl.pallas_call(
        body, name=name, out_shape=jax.ShapeDtypeStruct(partner.shape, BF16),
        grid_spec=pltpu.PrefetchScalarGridSpec(
            num_scalar_prefetch=1, grid=(N_CHIPS, nb),
            in_specs=[mine, pl.BlockSpec((None, tr, width), lambda s, i, pr: (s, i, 0))],
            out_specs=pl.BlockSpec((None, tr, width), lambda s, i, pr: (s, i, 0))),
        compiler_params=_cparams(("parallel", "parallel")),
    )(place, grads, partner)


def _rs_chip_add(name, place, mine, others, split="rows"):
    _, half_rows, width = mine.shape
    tr = _row_block(half_rows, width, 4, 16, budget=1 << 20)
    nb = half_rows // tr
    if split == "rows":
        out_shape, out_spec = (2 * half_rows, width), pl.BlockSpec((tr, width), lambda i, pr: (pr[1] * nb + i, 0))
    else:
        out_shape, out_spec = (half_rows, 2 * width), pl.BlockSpec((tr, width), lambda i, pr: (i, pr[1]))

    def body(place_ref, q_ref, r_ref, o_ref):
        acc = q_ref[...].astype(F32)
        for j in range(3):
            acc = acc + r_ref[j].astype(F32)
        o_ref[...] = acc

    return pl.pallas_call(
        body, name=name, out_shape=jax.ShapeDtypeStruct(out_shape, F32),
        grid_spec=pltpu.PrefetchScalarGridSpec(
            num_scalar_prefetch=1, grid=(nb,),
            in_specs=[pl.BlockSpec((None, tr, width), lambda i, pr: (pr[0], i, 0)),
                      pl.BlockSpec((3, tr, width), lambda i, pr: (0, i, 0))],
            out_specs=out_spec),
        compiler_params=_cparams(("parallel",)),
    )(place, mine, others)


WEIGHTS = ["meta_tokens", "a_norm_pre", "a_w_in", "a_conv_w", "a_conv_b", "a_dt_bias", "a_a_log", "a_d_skip",
           "a_gate_norm", "a_w_out", "a_norm_post", "kv_norm", "w_kv", "b_norm_pre", "b_w_q", "b_sinks", "b_w_o",
           "b_norm_post", "f_norm_pre", "f_w_up", "f_conv_w", "f_conv_b", "f_w_down", "f_norm_post"]
FULL_SHAPE = {
    "meta_tokens": (16, 1024), "a_norm_pre": (1, 1024), "a_w_in": (1, 1024, 5152), "a_conv_w": (1, 4, 3072),
    "a_conv_b": (1, 3072), "a_dt_bias": (1, 32), "a_a_log": (1, 32), "a_d_skip": (1, 32), "a_gate_norm": (1, 2048),
    "a_w_out": (1, 2048, 1024), "a_norm_post": (1, 1024), "kv_norm": (1024,), "w_kv": (1024, 512),
    "b_norm_pre": (1, 1024), "b_w_q": (1, 1024, 1024), "b_sinks": (1, 16), "b_w_o": (1, 1024, 1024),
    "b_norm_post": (1, 1024), "f_norm_pre": (2, 1024), "f_w_up": (2, 1024, 5632), "f_conv_w": (2, 3, 5632),
    "f_conv_b": (2, 5632), "f_w_down": (2, 2816, 1024), "f_norm_post": (2, 1024),
}
SHARD_AXIS = {
    "meta_tokens": 1, "a_norm_pre": 1, "a_w_in": 2, "a_conv_w": 2, "a_conv_b": 1, "a_dt_bias": None, "a_a_log": None,
    "a_d_skip": None, "a_gate_norm": 1, "a_w_out": 1, "a_norm_post": 1, "kv_norm": None, "w_kv": 0, "b_norm_pre": None,
    "b_w_q": 1, "b_sinks": None, "b_w_o": 1, "b_norm_post": None, "f_norm_pre": None, "f_w_up": 2, "f_conv_w": 2,
    "f_conv_b": None, "f_w_down": 1, "f_norm_post": None,
}
BIG = ["a_w_in", "a_w_out", "w_kv", "b_w_q", "b_w_o", "f_w_up", "f_w_down"]
SMALL = [n for n in WEIGHTS if n not in BIG]
SMALL_SHARDED = [n for n in SMALL if SHARD_AXIS[n] is not None]


def _shard_shape(name):
    shape = list(FULL_SHAPE[name])
    if SHARD_AXIS[name] is not None:
        shape[SHARD_AXIS[name]] //= N_CHIPS
    return tuple(shape)


def _numel(shape):
    return int(math.prod(shape))


SUBLANES = 8


def _packed_rows(shape):
    rows = -(-_numel(shape) // LANES)
    return -(-rows // SUBLANES) * SUBLANES


def _pack(arrays):
    parts = []
    for a in arrays:
        size, rows = _numel(a.shape), _packed_rows(a.shape)
        if size % LANES == 0:
            part = jnp.pad(a.reshape(size // LANES, LANES), ((0, rows - size // LANES), (0, 0)))
        else:
            part = jnp.pad(a.reshape(-1), (0, rows * LANES - size)).reshape(rows, LANES)
        parts.append(part)
    return jnp.concatenate(parts, axis=0)


def _unpack(packed, names, shape_of):
    out, off = {}, 0
    lead = packed.shape[:-2]
    for n in names:
        shape = tuple(shape_of(n))
        size, rows = _numel(shape), _packed_rows(shape)
        part = packed[..., off:off + rows, :]
        if size % LANES == 0:
            out[n] = part[..., :size // LANES, :].reshape(lead + shape)
        else:
            out[n] = part.reshape(lead + (rows * LANES,))[..., :size].reshape(lead + shape)
        off += rows
    return out


def _split_chips(name, full):
    ax = SHARD_AXIS[name]
    shape = full.shape
    cut = shape[:ax] + (N_CHIPS, shape[ax] // N_CHIPS) + shape[ax + 1:]
    return jnp.moveaxis(full.reshape(cut), ax, 0)


def _join_chips(name, stacked):
    ax = SHARD_AXIS[name]
    moved = jnp.moveaxis(stacked, 0, ax)
    shape = moved.shape
    return moved.reshape(shape[:ax] + (shape[ax] * shape[ax + 1],) + shape[ax + 2:])


def _as2d(a):
    return a.reshape(-1, a.shape[-1])


BUFFERS = [("a_w_in", "a_w_in", None), ("a_w_out", "a_w_out", None), ("w_kv", "w_kv", None),
           ("b_w_q", "b_w_q", None), ("b_w_o", "b_w_o", None), ("f_w_up0", "f_w_up", 0), ("f_w_up1", "f_w_up", 1),
           ("f_w_down0", "f_w_down", 0), ("f_w_down1", "f_w_down", 1)]


TRANSPOSED = ("a_w_in",)
SPLIT = {"a_w_in": "cols"}


def _local_shard(arrays, weight, layer):
    if weight in TRANSPOSED:
        return arrays[weight][0].T
    return _as2d(arrays[weight]) if layer is None else arrays[weight][layer]


def _weight_from_gathered(weight, buf):
    if weight == "f_w_up":
        return buf
    return buf.reshape(N_CHIPS * buf.shape[1], buf.shape[2])


def _gathered_from_grad(weight, g):
    if weight == "f_w_up":
        return g
    return g.reshape(N_CHIPS, g.shape[0] // N_CHIPS, g.shape[1]).astype(BF16)


GATHER_SCHEDULE = {
    "a_in_main": [("ici", ["a_w_out"])],
    "a_conv": [("d2d", ["a_w_out"]), ("ici", ["f_w_down0"])],
    "a_ssd_prep": [("d2d", ["f_w_down0"]), ("ici", ["w_kv", "b_w_q", "b_w_o"])],
    "a_ssd": [("d2d", ["w_kv", "b_w_q", "b_w_o"]), ("ici", ["f_w_up0"])],
    "a_gate": [("d2d", ["f_w_up0"])],
    "ffn0_conv": [("ici", ["f_w_down1"])],
    "b_attn": [("d2d", ["f_w_down1"]), ("ici", ["f_w_up1"])],
    "b_o": [("d2d", ["f_w_up1"])],
}
REDUCE_SCHEDULE = {
    "ffn1_conv_bwd": ["f_w_down1"],
    "b_attn_bwd": ["f_w_up1", "b_w_o"],
    "ffn0_conv_bwd": ["b_w_q", "w_kv", "f_w_down0"],
    "a_ssd_bwd": ["f_w_up0", "a_w_out"],
    "a_in_main_dx": ["a_w_in"],
}
PAIR_SCHEDULE = {
    "ffn1_down_dx": ["f_w_down1"],
    "b_o_dx": ["f_w_up1", "b_w_o"],
    "ffn0_down_dx": ["b_w_q", "w_kv", "f_w_down0"],
    "a_out_dx": ["f_w_up0", "a_w_out"],
    "a_in_dt_dx": ["a_w_in"],
}
SWAP_SCHEDULE = {"a_in_main_dw": ["f_w_down1", "f_w_up1", "b_w_o", "b_w_q", "w_kv", "f_w_down0", "f_w_up0", "a_w_out"]}


def _buffer_of(weight, layer):
    return weight if layer is None else f"{weight}{layer}"


class _Pipeline:
    def __init__(self, place, slots):
        self.place = place
        self.slots = dict(slots)
        self.running = []
        self.grads = {}
        self.theirs = {}
        self.partials = {}
        self.peers = {}
        self.reduced = {}

    def _collect(self):
        for step, buffers, table in self.running:
            table.update(zip(buffers, step.results))
        self.running = []

    @staticmethod
    def _splits(buffers):
        return [SPLIT.get(b, "rows") for b in buffers]

    def gather_now(self, name, buffers):
        step = _step_gather_full([self.slots[b] for b in buffers], self._splits(buffers))
        _run_steps(name, [step])
        self.slots.update(zip(buffers, step.results))

    def weight(self, name, layer=None):
        self._collect()
        return _weight_from_gathered(name, self.slots[_buffer_of(name, layer)])

    def grad(self, name, layer, g):
        self.grads[_buffer_of(name, layer)] = _gathered_from_grad(name, g)

    def steps(self, kernel):
        self._collect()
        steps = []
        for phase, buffers in GATHER_SCHEDULE.get(kernel, []):
            make = _step_gather_ici if phase == "ici" else _step_gather_d2d
            step = make([self.slots[b] for b in buffers], self._splits(buffers))
            self.running.append((step, buffers, self.slots))
            steps.append(step)
        buffers = PAIR_SCHEDULE.get(kernel)
        if buffers:
            step = _step_pair_exchange([self.grads[b] for b in buffers], self._splits(buffers))
            self.running.append((step, buffers, self.theirs))
            steps.append(step)
        buffers = REDUCE_SCHEDULE.get(kernel)
        if buffers:
            for b in buffers:
                self.partials[b] = _rs_pair_add("reduce_pair_add_" + b, self.place, self.grads[b], self.theirs[b],
                                                SPLIT.get(b, "rows"))
            step = _step_chip_exchange([self.partials[b] for b in buffers])
            self.running.append((step, buffers, self.peers))
            steps.append(step)
        buffers = SWAP_SCHEDULE.get(kernel)
        if buffers:
            step = self._swap_step(buffers)
            self.running.append((step, buffers, self.reduced))
            steps.append(step)
        return steps

    def _swap_step(self, buffers):
        halves = [_rs_chip_add("reduce_chip_add_" + b, self.place, self.partials[b], self.peers[b], SPLIT.get(b, "rows"))
                  for b in buffers]
        return _step_pair_gather(halves, self._splits(buffers))

    def finish(self):
        self._collect()
        rest = [b for b, _, _ in BUFFERS if b not in self.reduced]
        step = self._swap_step(rest)
        _run_steps("reduce_pair_gather", [step])
        self.reduced.update(zip(rest, step.results))
        return self.reduced


def kernel(x, meta_tokens, a_norm_pre, a_w_in, a_conv_w, a_conv_b, a_dt_bias, a_a_log, a_d_skip, a_gate_norm, a_w_out, a_norm_post, kv_norm, w_kv, b_norm_pre, b_w_q, b_sinks, b_w_o, b_norm_post, f_norm_pre, f_w_up, f_conv_w, f_conv_b, f_w_down, f_norm_post, loss_target, m_meta_tokens, m_a_norm_pre, m_a_w_in, m_a_conv_w, m_a_conv_b, m_a_dt_bias, m_a_a_log, m_a_d_skip, m_a_gate_norm, m_a_w_out, m_a_norm_post, m_kv_norm, m_w_kv, m_b_norm_pre, m_b_w_q, m_b_sinks, m_b_w_o, m_b_norm_post, m_f_norm_pre, m_f_w_up, m_f_conv_w, m_f_conv_b, m_f_w_down, m_f_norm_post, v_meta_tokens, v_a_norm_pre, v_a_w_in, v_a_conv_w, v_a_conv_b, v_a_dt_bias, v_a_a_log, v_a_d_skip, v_a_gate_norm, v_a_w_out, v_a_norm_post, v_kv_norm, v_w_kv, v_b_norm_pre, v_b_w_q, v_b_sinks, v_b_w_o, v_b_norm_post, v_f_norm_pre, v_f_w_up, v_f_conv_w, v_f_conv_b, v_f_w_down, v_f_norm_post):
    given = dict(locals())
    w = {n: given[n] for n in WEIGHTS}
    mom = {n: given["m_" + n] for n in WEIGHTS}
    var = {n: given["v_" + n] for n in WEIGHTS}
    chip = 2 * lax.axis_index("x") + lax.axis_index("y")
    core = lax.axis_index("c")
    place = jnp.stack([chip, core]).astype(jnp.int32)

    small_all = _allgather_small("gather_small", _pack([w[n] for n in SMALL_SHARDED]))
    small_parts = _unpack(small_all, SMALL_SHARDED, _shard_shape)
    slots = {b: _cast_into_slot("cast_" + b, place, _local_shard(w, wn, layer)) for b, wn, layer in BUFFERS}
    pipeline = _Pipeline(place, slots)
    pipeline.gather_now("gather_first", ["a_w_in"])
    p = {}
    for n in SMALL:
        p[n] = _join_chips(n, small_parts[n]) if n in SMALL_SHARDED else w[n]
    p["a_conv_w"] = p["a_conv_w"][0]
    p["kv_norm"] = p["kv_norm"].reshape(1, D_MODEL)

    loss_local, grad_x, g = _local_step(x[0], loss_target[0], p, pipeline)
    loss = lax.psum(loss_local, ("x", "y", "c"))

    small_sum = _allreduce_small("reduce_small", _pack([g[n].reshape(FULL_SHAPE[n]) for n in SMALL]))
    small_red = _unpack(small_sum, SMALL, lambda n: FULL_SHAPE[n])
    grads = {}
    for n in SMALL:
        if SHARD_AXIS[n] is None:
            grads[n] = small_red[n]
        else:
            grads[n] = lax.dynamic_index_in_dim(_split_chips(n, small_red[n]), chip, 0, keepdims=False)

    shard_sum = pipeline.finish()

    delta, new_m, new_v = {}, {}, {}
    for n in BIG:
        shape = _shard_shape(n)
        if n in TRANSPOSED:
            g2d = shard_sum[n]
            w2d, m2d, v2d = (arrays[n][0].T for arrays in (w, mom, var))
            back = lambda a: a.T.reshape(shape)
        else:
            g2d = (jnp.concatenate([shard_sum[n + "0"], shard_sum[n + "1"]], axis=0) if n in ("f_w_up", "f_w_down")
                   else shard_sum[n])
            w2d, m2d, v2d = (_as2d(arrays[n]) for arrays in (w, mom, var))
            back = lambda a: a.reshape(shape)
        d, m2, v2 = _adamw("adamw_" + n, w2d, g2d, m2d, v2d)
        grads[n], delta[n], new_m[n], new_v[n] = back(g2d), back(d), back(m2), back(v2)
    packed = [_pack([src[n].reshape(_shard_shape(n)) for n in SMALL]) for src in (w, grads, mom, var)]
    outs = _adamw("adamw_small", *packed)
    for dst, flat in zip((delta, new_m, new_v), outs):
        dst.update(_unpack(flat, SMALL, _shard_shape))

    return (loss, grad_x[None], *[grads[n].reshape(_shard_shape(n)) for n in WEIGHTS],
            *[delta[n] for n in WEIGHTS], *[new_m[n] for n in WEIGHTS], *[new_v[n] for n in WEIGHTS])
```

```python
import functools
import math

import jax
import jax.numpy as jnp
from jax import lax
from jax.experimental import pallas as pl
from jax.experimental.pallas import tpu as pltpu

F32, BF16 = jnp.float32, jnp.bfloat16
MESH = pl.DeviceIdType.MESH

D_MODEL = 1024
N_META = 16
CHUNK = 128
PAD_ROWS = CHUNK - N_META
D_INNER = 2048
D_STATE = 128
N_GROUPS = 4
HEADS_PER_GROUP = 8
SSM_HEADS = 32
HEAD_DIM = 64
D_BC = N_GROUPS * D_STATE
D_XBC = D_INNER + 2 * D_BC
D_MAIN = D_INNER + D_XBC
D_IN_PROJ = D_MAIN + SSM_HEADS
GROUP_W = HEADS_PER_GROUP * HEAD_DIM
SSM_CONV = 4
D_FF = 2816
FFN_CONV = 3
N_Q_HEADS = 16
N_KV_HEADS = 4
D_KV = 256
ATTN_SCALE = 1.0 / math.sqrt(HEAD_DIM)
RMS_EPS = 1e-6
NEG_INF = -1e30
LANES = 128
VMEM_LIMIT = 48 * 1024 * 1024

ADAM_LR, ADAM_B1, ADAM_B2, ADAM_EPS, ADAM_WD, ADAM_STEP = 0.001, 0.9, 0.999, 1e-08, 0.01, 10

N_CHIPS = 4
N_DEV = 8


def _cparams(sem=None):
    return pltpu.CompilerParams(dimension_semantics=sem, vmem_limit_bytes=VMEM_LIMIT)


def _tile(n, cands=(512, 256, 128)):
    for t in cands:
        if n % t == 0:
            return t
    return n


def _row_tile(rows, width):
    for t in (544, 272):
        if rows % t == 0 and t * width * 4 <= (3 << 20):
            return t
    return 128


def _rows_mask(i, tm):
    rows = i * tm + lax.broadcasted_iota(jnp.int32, (tm, 1), 0)
    return rows >= PAD_ROWS


def _dot(a, b):
    return jnp.dot(a, b, preferred_element_type=F32)


def _dot_nt(a, b):
    return lax.dot_general(a, b, (((1,), (1,)), ((), ())), preferred_element_type=F32)


def _dot_tn(a, b):
    return lax.dot_general(a, b, (((0,), (0,)), ((), ())), preferred_element_type=F32)


def _sigmoid(x):
    return 1.0 / (1.0 + jnp.exp(-x))


def _place():
    return lax.axis_index("x"), lax.axis_index("y"), lax.axis_index("c")


def _other_chips(x, y):
    return [(1 - x, y), (x, 1 - y), (1 - x, 1 - y)]


class _Step:
    def __init__(self, ins, outs, aliases, n_sems, start, finish):
        self.ins, self.outs, self.aliases, self.n_sems = list(ins), list(outs), dict(aliases), n_sems
        self.start, self.finish = start, finish
        self.results = None


def _like(a):
    return jax.ShapeDtypeStruct(a.shape, a.dtype)


def _remote(src, dst, send_sems, recv_sems, k, device):
    return pltpu.make_async_remote_copy(src, dst, send_sems.at[k], recv_sems.at[k], device_id=device, device_id_type=MESH)


def _half(ref, split, which, lead=()):
    if split == "rows":
        hr = ref.shape[-2] // 2
        return ref.at[lead + (pl.ds(which * hr, hr),)]
    hc = ref.shape[-1] // 2
    return ref.at[lead + (slice(None), pl.ds(which * hc, hc))]


def _splits(bufs, splits):
    return list(splits) if splits is not None else ["rows"] * len(bufs)


def _step_gather_ici(bufs, splits=None):
    splits = _splits(bufs, splits)

    def copies(outs, send_sems, recv_sems, received):
        x, y, c = _place()
        me = 2 * x + y
        for k, o in enumerate(outs):
            for j, (cx, cy) in enumerate(_other_chips(x, y)):
                part = _half(o, splits[k], c, (2 * cx + cy if received else me,))
                yield _remote(part, part, send_sems, recv_sems, 3 * k + j, (cx, cy, c))

    def start(ins, outs, send_sems, recv_sems):
        for cp in copies(outs, send_sems, recv_sems, False):
            cp.start()

    def finish(ins, outs, send_sems, recv_sems):
        for cp in copies(outs, send_sems, recv_sems, True):
            cp.wait_recv()
        for cp in copies(outs, send_sems, recv_sems, False):
            cp.wait_send()

    return _Step(bufs, [_like(b) for b in bufs], {k: k for k in range(len(bufs))}, 3 * len(bufs), start, finish)


def _step_gather_d2d(bufs, splits=None):
    splits = _splits(bufs, splits)

    def copies(outs, send_sems, recv_sems, received):
        x, y, c = _place()
        for k, o in enumerate(outs):
            for j, (cx, cy) in enumerate(_other_chips(x, y)):
                part = _half(o, splits[k], 1 - c if received else c, (2 * cx + cy,))
                yield _remote(part, part, send_sems, recv_sems, 3 * k + j, (x, y, 1 - c))

    def start(ins, outs, send_sems, recv_sems):
        for cp in copies(outs, send_sems, recv_sems, False):
            cp.start()

    def finish(ins, outs, send_sems, recv_sems):
        for cp in copies(outs, send_sems, recv_sems, True):
            cp.wait_recv()
        for cp in copies(outs, send_sems, recv_sems, False):
            cp.wait_send()

    return _Step(bufs, [_like(b) for b in bufs], {k: k for k in range(len(bufs))}, 3 * len(bufs), start, finish)


def _step_gather_full(bufs, splits=None):
    n = len(bufs)
    splits = _splits(bufs, splits)

    def ici(outs, send_sems, recv_sems, received):
        x, y, c = _place()
        me = 2 * x + y
        for k, o in enumerate(outs):
            for j, (cx, cy) in enumerate(_other_chips(x, y)):
                part = _half(o, splits[k], c, (2 * cx + cy if received else me,))
                yield _remote(part, part, send_sems, recv_sems, 3 * k + j, (cx, cy, c))

    def d2d(outs, send_sems, recv_sems, received):
        x, y, c = _place()
        for k, o in enumerate(outs):
            for j, (cx, cy) in enumerate(_other_chips(x, y)):
                part = _half(o, splits[k], 1 - c if received else c, (2 * cx + cy,))
                yield _remote(part, part, send_sems, recv_sems, 3 * n + 3 * k + j, (x, y, 1 - c))

    def start(ins, outs, send_sems, recv_sems):
        for cp in ici(outs, send_sems, recv_sems, False):
            cp.start()

    def finish(ins, outs, send_sems, recv_sems):
        for arrived, onward in zip(ici(outs, send_sems, recv_sems, True), d2d(outs, send_sems, recv_sems, False)):
            arrived.wait_recv()
            onward.start()
        for cp in d2d(outs, send_sems, recv_sems, True):
            cp.wait_recv()
        for cp in ici(outs, send_sems, recv_sems, False):
            cp.wait_send()
        for cp in d2d(outs, send_sems, recv_sems, False):
            cp.wait_send()

    return _Step(bufs, [_like(b) for b in bufs], {k: k for k in range(n)}, 6 * n, start, finish)


def _half_shape(shape, split):
    return shape[:-2] + ((shape[-2] // 2, shape[-1]) if split == "rows" else (shape[-2], shape[-1] // 2))


def _step_pair_exchange(grads, splits=None):
    splits = _splits(grads, splits)

    def copies(ins, outs, send_sems, recv_sems):
        x, y, c = _place()
        for k, (g, o) in enumerate(zip(ins, outs)):
            yield _remote(_half(g, splits[k], 1 - c, (slice(None),)), o, send_sems, recv_sems, k, (x, y, 1 - c))

    def start(ins, outs, send_sems, recv_sems):
        for cp in copies(ins, outs, send_sems, recv_sems):
            cp.start()

    def finish(ins, outs, send_sems, recv_sems):
        for cp in copies(ins, outs, send_sems, recv_sems):
            cp.wait()

    outs = [jax.ShapeDtypeStruct(_half_shape(g.shape, s), g.dtype) for g, s in zip(grads, splits)]
    return _Step(grads, outs, {}, len(grads), start, finish)


def _step_chip_exchange(partials):
    def copies(ins, outs, send_sems, recv_sems):
        x, y, c = _place()
        for k, (q, o) in enumerate(zip(ins, outs)):
            for j, (cx, cy) in enumerate(_other_chips(x, y)):
                yield _remote(q.at[2 * cx + cy], o.at[j], send_sems, recv_sems, 3 * k + j, (cx, cy, c))

    def start(ins, outs, send_sems, recv_sems):
        for cp in copies(ins, outs, send_sems, recv_sems):
            cp.start()

    def finish(ins, outs, send_sems, recv_sems):
        for cp in copies(ins, outs, send_sems, recv_sems):
            cp.wait()

    outs = [jax.ShapeDtypeStruct((3,) + q.shape[1:], q.dtype) for q in partials]
    return _Step(partials, outs, {}, 3 * len(partials), start, finish)


def _step_pair_gather(shards, splits=None):
    splits = _splits(shards, splits)

    def copies(outs, send_sems, recv_sems, received):
        x, y, c = _place()
        for k, o in enumerate(outs):
            part = _half(o, splits[k], 1 - c if received else c)
            yield _remote(part, part, send_sems, recv_sems, k, (x, y, 1 - c))

    def start(ins, outs, send_sems, recv_sems):
        for cp in copies(outs, send_sems, recv_sems, False):
            cp.start()

    def finish(ins, outs, send_sems, recv_sems):
        for cp in copies(outs, send_sems, recv_sems, True):
            cp.wait_recv()
        for cp in copies(outs, send_sems, recv_sems, False):
            cp.wait_send()

    return _Step(shards, [_like(s) for s in shards], {k: k for k in range(len(shards))}, len(shards), start, finish)


def _call(body, *, name, out_shape, grid, in_specs, out_specs, operands, scratch_shapes=(), semantics=None, steps=()):
    single = not isinstance(out_shape, (tuple, list))
    out_shapes = [out_shape] if single else list(out_shape)
    out_spec_list = [out_specs] if single else list(out_specs)
    steps = list(steps)
    if not steps:
        res = pl.pallas_call(body, name=name, out_shape=out_shapes, grid=grid, in_specs=list(in_specs),
                             out_specs=out_spec_list, scratch_shapes=list(scratch_shapes),
                             compiler_params=_cparams(semantics))(*operands)
        return res[0] if single else res
    n_in, n_out, n_scr = len(operands), len(out_shapes), len(scratch_shapes)
    x_in = [a for s in steps for a in s.ins]
    x_out = [o for s in steps for o in s.outs]
    aliases, in_off, out_off = {}, 0, 0
    for s in steps:
        for i, o in s.aliases.items():
            aliases[n_in + in_off + i] = n_out + out_off + o
        in_off += len(s.ins)
        out_off += len(s.outs)
    sems = []
    for s in steps:
        sems += [pltpu.SemaphoreType.DMA((s.n_sems,)), pltpu.SemaphoreType.DMA((s.n_sems,))]
    any_spec = pl.BlockSpec(memory_space=pl.ANY)

    def carried(*refs):
        pos = 0
        ins = refs[pos:pos + n_in]; pos += n_in
        xi = refs[pos:pos + len(x_in)]; pos += len(x_in)
        outs = refs[pos:pos + n_out]; pos += n_out
        xo = refs[pos:pos + len(x_out)]; pos += len(x_out)
        scr = refs[pos:pos + n_scr]; pos += n_scr
        sem_refs = refs[pos:]

        def each(action):
            i0 = o0 = 0
            for k, s in enumerate(steps):
                getattr(s, action)(xi[i0:i0 + len(s.ins)], xo[o0:o0 + len(s.outs)], sem_refs[2 * k], sem_refs[2 * k + 1])
                i0 += len(s.ins)
                o0 += len(s.outs)

        if grid:
            first = functools.reduce(jnp.logical_and, [pl.program_id(d) == 0 for d in range(len(grid))])
            last = functools.reduce(jnp.logical_and, [pl.program_id(d) == grid[d] - 1 for d in range(len(grid))])
            pl.when(first)(lambda: each("start"))
            body(*ins, *outs, *scr)
            pl.when(last)(lambda: each("finish"))
        else:
            each("start")
            body(*ins, *outs, *scr)
            each("finish")

    res = pl.pallas_call(
        carried, name=name, out_shape=out_shapes + x_out, grid=grid,
        in_specs=list(in_specs) + [any_spec] * len(x_in), out_specs=out_spec_list + [any_spec] * len(x_out),
        scratch_shapes=list(scratch_shapes) + sems, input_output_aliases=aliases,
        compiler_params=_cparams(None if semantics is None else ("arbitrary",) * len(grid)),
    )(*operands, *x_in)
    o0 = n_out
    for s in steps:
        s.results = list(res[o0:o0 + len(s.outs)])
        o0 += len(s.outs)
    return res[0] if single else tuple(res[:n_out])


def _run_steps(name, steps):
    _call(lambda: None, name=name, out_shape=[], grid=(), in_specs=[], out_specs=[], operands=[], steps=steps)
    return [s.results for s in steps]


def _mm(name, a, b, mode, out_dtype=F32, acc=None, b_colblock=0, k_rows=None, out_rows=None, steps=()):
    resident_bytes = 8 << 20
    if mode == "nn":
        m, k = a.shape
        n = b.shape[1]
        tm = m
        while tm * k * 2 > resident_bytes and tm % 32 == 0:
            tm //= 2
        tn = _tile(n)
        grid = (m // tm, n // tn)
        in_specs = [pl.BlockSpec((tm, k), lambda i, j: (i, 0)), pl.BlockSpec((k, tn), lambda i, j: (0, j))]
        out_shape, out_block = (m, n), (tm, tn)
    elif mode == "nt":
        m, n = a.shape
        k = k_rows or b.shape[0]
        tm = m
        while tm * n * 2 > resident_bytes and tm % 32 == 0:
            tm //= 2
        tk = _tile(k)
        grid = (m // tm, k // tk)
        in_specs = [pl.BlockSpec((tm, n), lambda i, j: (i, 0)), pl.BlockSpec((tk, n), lambda i, j: (j, b_colblock))]
        out_shape, out_block = (m, k), (tm, tk)
    else:
        m, k = a.shape
        n = b.shape[1]
        tk, tn = _tile(k), (n if m * n * 2 <= resident_bytes else _tile(n))
        grid = (k // tk, n // tn)
        in_specs = [pl.BlockSpec((m, tk), lambda i, j: (0, i)), pl.BlockSpec((m, tn), lambda i, j: (0, j))]
        out_shape, out_block = (out_rows or k, n), (tk, tn)
    out_spec = pl.BlockSpec(out_block, lambda i, j: (i, j))
    has_acc = acc is not None

    def body(*refs):
        a_ref, b_ref = refs[0], refs[1]
        o_ref = refs[-1]
        av, bv = a_ref[...], b_ref[...]
        if mode == "nn":
            r = _dot(av, bv)
        elif mode == "nt":
            r = _dot_nt(av, bv)
        else:
            r = _dot_tn(av, bv)
        if has_acc:
            r = r + refs[2][...]
        o_ref[...] = r.astype(o_ref.dtype)

    operands = [a, b]
    if has_acc:
        in_specs = in_specs + [out_spec]
        operands.append(acc)
    return _call(body, name=name, out_shape=jax.ShapeDtypeStruct(out_shape, out_dtype), grid=grid, in_specs=in_specs,
                 out_specs=out_spec, operands=operands, semantics=("parallel", "parallel"), steps=steps)


def _tn_rows_into(name, a, b, into, row0, nrows):
    m, k = a.shape
    n = b.shape[1]

    def body(a_ref, b_ref, into_ref, o_ref):
        o_ref[...] = _dot_tn(a_ref[...], b_ref[...])[0:nrows].astype(o_ref.dtype)

    return pl.pallas_call(
        body, name=name, out_shape=jax.ShapeDtypeStruct(into.shape, into.dtype), grid=(1,),
        in_specs=[pl.BlockSpec((m, k), lambda i: (0, 0)), pl.BlockSpec((m, n), lambda i: (0, 0)),
                  pl.BlockSpec(memory_space=pl.ANY)],
        out_specs=pl.BlockSpec((nrows, n), lambda i: (row0 // nrows, 0)),
        input_output_aliases={2: 0}, compiler_params=_cparams(("arbitrary",)),
    )(a, b, into)


def _fit_rows(m, row_bytes, budget=8 << 20):
    tm = m
    while tm * row_bytes > budget and tm % 32 == 0:
        tm //= 2
    return tm


def _mm_nn_bychip(name, a, bc):
    m, k = a.shape
    n = bc.shape[2]
    tm = min(_fit_rows(m, k * 2), _fit_rows(m, n * 4))

    def body(a_ref, b_ref, o_ref):
        o_ref[...] = _dot(a_ref[...], b_ref[...])

    return pl.pallas_call(
        body, name=name, out_shape=jax.ShapeDtypeStruct((m, N_CHIPS * n), F32), grid=(m // tm, N_CHIPS),
        in_specs=[pl.BlockSpec((tm, k), lambda i, c: (i, 0)), pl.BlockSpec((None, k, n), lambda i, c: (c, 0, 0))],
        out_specs=pl.BlockSpec((tm, n), lambda i, c: (i, c)), compiler_params=_cparams(("parallel", "parallel")),
    )(a, bc)


def _mm_nt_bychip(name, a, bc, chip0, acc=None):
    m = a.shape[0]
    _, k, n = bc.shape
    nch = a.shape[1] // n
    tm, tk = _fit_rows(m, n * 2), _tile(k)
    has_acc = acc is not None

    def body(*refs):
        a_ref, b_ref, o_ref = refs[0], refs[1], refs[-1]

        @pl.when(pl.program_id(2) == 0)
        def _():
            o_ref[...] = refs[2][...] if has_acc else jnp.zeros_like(o_ref)

        o_ref[...] += _dot_nt(a_ref[...], b_ref[...])

    out_spec = pl.BlockSpec((tm, tk), lambda i, j, c: (i, j))
    in_specs = [pl.BlockSpec((tm, n), lambda i, j, c: (i, c)),
                pl.BlockSpec((None, tk, n), lambda i, j, c: (chip0 + c, j, 0))]
    operands = [a, bc]
    if has_acc:
        in_specs.append(out_spec)
        operands.append(acc)
    return pl.pallas_call(
        body, name=name, out_shape=jax.ShapeDtypeStruct((m, k), F32), grid=(m // tm, k // tk, nch),
        in_specs=in_specs, out_specs=out_spec, compiler_params=_cparams(("parallel", "parallel", "arbitrary")),
    )(*operands)


def _mm_tn_bychip(name, a, dy, n, chip0, into=None):
    m, k = a.shape
    nch = dy.shape[1] // n
    tk = _tile(k)

    def body(*refs):
        a_ref, d_ref, o_ref = refs[0], refs[1], refs[-1]
        o_ref[...] = _dot_tn(a_ref[...], d_ref[...]).astype(BF16)

    in_specs = [pl.BlockSpec((m, tk), lambda i, c: (0, i)), pl.BlockSpec((m, n), lambda i, c: (0, c))]
    operands = [a, dy]
    aliases = {}
    if into is not None:
        in_specs.append(pl.BlockSpec(memory_space=pl.ANY))
        operands.append(into)
        aliases = {2: 0}
    return pl.pallas_call(
        body, name=name, out_shape=jax.ShapeDtypeStruct((N_CHIPS, k, n), BF16), grid=(k // tk, nch),
        in_specs=in_specs, out_specs=pl.BlockSpec((None, tk, n), lambda i, c: (chip0 + c, i, 0)),
        input_output_aliases=aliases, compiler_params=_cparams(("parallel", "parallel")),
    )(*operands)


def _rms_fwd(name, h, w):
    rows, width = h.shape
    tm = _row_tile(rows, width)

    def body(h_ref, w_ref, o_ref):
        x = h_ref[...]
        r = lax.rsqrt(jnp.mean(x * x, axis=-1, keepdims=True) + RMS_EPS)
        o_ref[...] = (x * r * w_ref[...]).astype(BF16)

    return pl.pallas_call(
        body, name=name, out_shape=jax.ShapeDtypeStruct((rows, width), BF16), grid=(rows // tm,),
        in_specs=[pl.BlockSpec((tm, width), lambda i: (i, 0)), pl.BlockSpec((1, width), lambda i: (0, 0))],
        out_specs=pl.BlockSpec((tm, width), lambda i: (i, 0)), compiler_params=_cparams(("parallel",)),
    )(h, w)


def _resid_norm_fwd(name, h, pre, w, next_norms=()):
    rows, width = h.shape
    tm = _row_tile(rows, width)
    n_next = len(next_norms)

    def body(*refs):
        h_ref, p_ref, w_ref = refs[:3]
        v_refs = refs[3:3 + n_next]
        o_ref = refs[3 + n_next]
        n_refs = refs[4 + n_next:]
        p = p_ref[...]
        r = lax.rsqrt(jnp.mean(p * p, axis=-1, keepdims=True) + RMS_EPS)
        x = h_ref[...] + jnp.where(_rows_mask(pl.program_id(0), tm), p * r * w_ref[...], 0.0)
        o_ref[...] = x
        if n_next:
            rx = lax.rsqrt(jnp.mean(x * x, axis=-1, keepdims=True) + RMS_EPS)
            for v_ref, n_ref in zip(v_refs, n_refs):
                n_ref[...] = (x * rx * v_ref[...]).astype(BF16)

    row_spec = pl.BlockSpec((tm, width), lambda i: (i, 0))
    vec_spec = pl.BlockSpec((1, width), lambda i: (0, 0))
    outs = pl.pallas_call(
        body, name=name,
        out_shape=[jax.ShapeDtypeStruct((rows, width), F32)] + [jax.ShapeDtypeStruct((rows, width), BF16)] * n_next,
        grid=(rows // tm,), in_specs=[row_spec, row_spec, vec_spec] + [vec_spec] * n_next,
        out_specs=[row_spec] * (1 + n_next), compiler_params=_cparams(("parallel",)),
    )(h, pre, w, *next_norms)
    return outs[0], list(outs[1:])


def _resid_norm_loss(name, h, pre, w, target):
    rows, width = h.shape

    def body(h_ref, p_ref, w_ref, t_ref, dh_ref, loss_ref, dp_ref, dw_ref):
        i = pl.program_id(0)
        p = p_ref[...]
        r = lax.rsqrt(jnp.mean(p * p, axis=-1, keepdims=True) + RMS_EPS)
        x = h_ref[...] + p * r * w_ref[...]
        real = (i + jnp.zeros((CHUNK, 1), jnp.int32)) >= 1
        diff = jnp.where(real, x - t_ref[...], 0.0)
        dh = diff * (1.0 / D_MODEL)
        dh_ref[...] = dh
        dp, dw_rows = _rms_bwd(dh, p, w_ref[...])
        dp_ref[...] = dp.astype(BF16)

        @pl.when(i == 0)
        def _():
            loss_ref[...] = jnp.zeros_like(loss_ref)
            dw_ref[...] = jnp.zeros_like(dw_ref)

        loss_ref[...] += jnp.sum(diff * diff) * (0.5 / D_MODEL)
        dw_ref[...] += jnp.sum(dw_rows, axis=0, keepdims=True)

    blk = pl.BlockSpec((CHUNK, width), lambda i: (i, 0))
    vec_spec = pl.BlockSpec((1, width), lambda i: (0, 0))
    return pl.pallas_call(
        body, name=name,
        out_shape=(jax.ShapeDtypeStruct((rows, width), F32), jax.ShapeDtypeStruct((1, LANES), F32),
                   jax.ShapeDtypeStruct((rows, width), BF16), jax.ShapeDtypeStruct((1, width), F32)),
        grid=(rows // CHUNK,),
        in_specs=[blk, blk, vec_spec, pl.BlockSpec((CHUNK, width), lambda i: (jnp.maximum(i - 1, 0), 0))],
        out_specs=(blk, pl.BlockSpec((1, LANES), lambda i: (0, 0)), blk, vec_spec),
        compiler_params=_cparams(("arbitrary",)),
    )(h, pre, w, target)


def _rms_bwd(dy, x, w):
    r = lax.rsqrt(jnp.mean(x * x, axis=-1, keepdims=True) + RMS_EPS)
    xhat = x * r
    dxhat = dy * w
    return r * (dxhat - xhat * jnp.mean(dxhat * xhat, axis=-1, keepdims=True)), dy * xhat


def _norm_bwd_add(name, dh, dhn, h, w, then=None):
    rows, width = dh.shape
    tm = _row_tile(rows, width)
    fused = then is not None

    def body(*refs):
        dh_ref, dhn_ref, h_ref, w_ref = refs[:4]
        o_ref, dw_ref = refs[6:8] if fused else refs[4:6]
        i = pl.program_id(0)
        valid = _rows_mask(i, tm)
        dx, dw_rows = _rms_bwd(dhn_ref[...], h_ref[...], w_ref[...])
        dh_new = dh_ref[...] + jnp.where(valid, dx, 0.0)
        o_ref[...] = dh_new

        @pl.when(i == 0)
        def _():
            dw_ref[...] = jnp.zeros_like(dw_ref)

        dw_ref[...] += jnp.sum(dw_rows, axis=0, keepdims=True)
        if fused:
            p_ref, wp_ref, dp_ref, dwp_ref = refs[4], refs[5], refs[8], refs[9]
            dp, dwp_rows = _rms_bwd(jnp.where(valid, dh_new, 0.0), p_ref[...], wp_ref[...])
            dp_ref[...] = dp.astype(BF16)

            @pl.when(i == 0)
            def _():
                dwp_ref[...] = jnp.zeros_like(dwp_ref)

            dwp_ref[...] += jnp.sum(dwp_rows, axis=0, keepdims=True)

    row_spec = pl.BlockSpec((tm, width), lambda i: (i, 0))
    vec_spec = pl.BlockSpec((1, width), lambda i: (0, 0))
    row_f32, vec_f32 = jax.ShapeDtypeStruct((rows, width), F32), jax.ShapeDtypeStruct((1, width), F32)
    in_specs, operands = [row_spec, row_spec, row_spec, vec_spec], [dh, dhn, h, w]
    out_shape, out_specs = [row_f32, vec_f32], [row_spec, vec_spec]
    if fused:
        in_specs += [row_spec, vec_spec]
        operands += list(then)
        out_shape += [jax.ShapeDtypeStruct((rows, width), BF16), vec_f32]
        out_specs += [row_spec, vec_spec]
    return pl.pallas_call(
        body, name=name, out_shape=out_shape, grid=(rows // tm,), in_specs=in_specs, out_specs=out_specs,
        compiler_params=_cparams(("arbitrary",)),
    )(*operands)


def _shift_down(x, s, rows):
    return pltpu.roll(x, s, 0) if s else x


def _shift_up(x, s, rows):
    return pltpu.roll(x, rows - s, 0) if s else x


def _conv4_fwd(name, zx, cw, cb, steps=()):
    rows = zx.shape[0]
    off = D_INNER // LANES

    def body(x_ref, w_ref, b_ref, o_ref):
        x = x_ref[...]
        acc = b_ref[...] + w_ref[pl.ds(SSM_CONV - 1, 1), :] * x
        for s in range(1, SSM_CONV):
            acc = acc + w_ref[pl.ds(SSM_CONV - 1 - s, 1), :] * _shift_down(x, s, rows)
        valid = lax.broadcasted_iota(jnp.int32, (rows, 1), 0) >= PAD_ROWS
        o_ref[...] = jnp.where(valid, acc * _sigmoid(acc), 0.0)

    return _call(
        body, name=name, out_shape=jax.ShapeDtypeStruct((rows, D_XBC), F32), grid=(D_XBC // LANES,),
        in_specs=[pl.BlockSpec((rows, LANES), lambda j: (0, j + off)),
                  pl.BlockSpec((SSM_CONV, LANES), lambda j: (0, j)),
                  pl.BlockSpec((1, LANES), lambda j: (0, j))],
        out_specs=pl.BlockSpec((rows, LANES), lambda j: (0, j)), operands=[zx, cw, cb],
        semantics=("parallel",), steps=steps)


def _conv4_bwd(name, zx, dout, cw, cb, col0):
    rows, width = dout.shape
    zoff = (D_INNER + col0) // LANES
    woff = col0 // LANES

    def body(x_ref, d_ref, w_ref, b_ref, dx_ref, dw_ref, db_ref):
        x = x_ref[...]
        shifted = [_shift_down(x, s, rows) for s in range(SSM_CONV)]
        acc = b_ref[...]
        for s in range(SSM_CONV):
            acc = acc + w_ref[pl.ds(SSM_CONV - 1 - s, 1), :] * shifted[s]
        sig = _sigmoid(acc)
        valid = lax.broadcasted_iota(jnp.int32, (rows, 1), 0) >= PAD_ROWS
        dpre = jnp.where(valid, d_ref[...] * sig * (1.0 + acc * (1.0 - sig)), 0.0)
        dx = w_ref[pl.ds(SSM_CONV - 1, 1), :] * dpre
        for s in range(1, SSM_CONV):
            dx = dx + w_ref[pl.ds(SSM_CONV - 1 - s, 1), :] * _shift_up(dpre, s, rows)
        dx_ref[...] = dx.astype(BF16)
        for s in range(SSM_CONV):
            dw_ref[pl.ds(SSM_CONV - 1 - s, 1), :] = jnp.sum(dpre * shifted[s], axis=0, keepdims=True)
        db_ref[...] = jnp.sum(dpre, axis=0, keepdims=True)

    return pl.pallas_call(
        body, name=name,
        out_shape=(jax.ShapeDtypeStruct((rows, width), BF16), jax.ShapeDtypeStruct((SSM_CONV, width), F32),
                   jax.ShapeDtypeStruct((1, width), F32)),
        grid=(width // LANES,),
        in_specs=[pl.BlockSpec((rows, LANES), lambda j: (0, j + zoff)),
                  pl.BlockSpec((rows, LANES), lambda j: (0, j)),
                  pl.BlockSpec((SSM_CONV, LANES), lambda j: (0, j + woff)),
                  pl.BlockSpec((1, LANES), lambda j: (0, j + woff))],
        out_specs=(pl.BlockSpec((rows, LANES), lambda j: (0, j)),
                   pl.BlockSpec((SSM_CONV, LANES), lambda j: (0, j)),
                   pl.BlockSpec((1, LANES), lambda j: (0, j))),
        compiler_params=_cparams(("parallel",)),
    )(zx, dout, cw, cb)


def _ffn_conv_fwd(name, up, cw, cb, steps=()):
    rows = up.shape[0]
    nt = D_FF // LANES

    def body(g_ref, v_ref, wg_ref, wv_ref, bg_ref, bv_ref, o_ref):
        g, v = g_ref[...], v_ref[...]
        ug, uv = bg_ref[...], bv_ref[...]
        for s in range(FFN_CONV):
            ug = ug + wg_ref[pl.ds(FFN_CONV - 1 - s, 1), :] * _shift_down(g, s, rows)
            uv = uv + wv_ref[pl.ds(FFN_CONV - 1 - s, 1), :] * _shift_down(v, s, rows)
        o_ref[...] = (ug * _sigmoid(ug) * uv).astype(BF16)

    col = lambda shift: pl.BlockSpec((rows, LANES), lambda j: (0, j + shift))
    wsp = lambda shift: pl.BlockSpec((FFN_CONV, LANES), lambda j: (0, j + shift))
    bsp = lambda shift: pl.BlockSpec((1, LANES), lambda j: (0, j + shift))
    return _call(
        body, name=name, out_shape=jax.ShapeDtypeStruct((rows, D_FF), BF16), grid=(nt,),
        in_specs=[col(0), col(nt), wsp(0), wsp(nt), bsp(0), bsp(nt)],
        out_specs=pl.BlockSpec((rows, LANES), lambda j: (0, j)), operands=[up, up, cw, cw, cb, cb],
        semantics=("parallel",), steps=steps)


def _ffn_conv_bwd(name, up, dact, cw, cb, steps=()):
    rows = up.shape[0]
    nt = D_FF // LANES

    def body(g_ref, v_ref, d_ref, wg_ref, wv_ref, bg_ref, bv_ref, dxg_ref, dxv_ref, dwg_ref, dwv_ref, dbg_ref, dbv_ref):
        g, v = g_ref[...], v_ref[...]
        gs = [_shift_down(g, s, rows) for s in range(FFN_CONV)]
        vs = [_shift_down(v, s, rows) for s in range(FFN_CONV)]
        ug, uv = bg_ref[...], bv_ref[...]
        for s in range(FFN_CONV):
            ug = ug + wg_ref[pl.ds(FFN_CONV - 1 - s, 1), :] * gs[s]
            uv = uv + wv_ref[pl.ds(FFN_CONV - 1 - s, 1), :] * vs[s]
        sig = _sigmoid(ug)
        dsig = d_ref[...] * sig
        for dpre, src, w_ref, dx_ref, dw_ref, db_ref in (
                (dsig * uv * (1.0 + ug * (1.0 - sig)), gs, wg_ref, dxg_ref, dwg_ref, dbg_ref),
                (dsig * ug, vs, wv_ref, dxv_ref, dwv_ref, dbv_ref)):
            dx = w_ref[pl.ds(FFN_CONV - 1, 1), :] * dpre
            for s in range(1, FFN_CONV):
                dx = dx + w_ref[pl.ds(FFN_CONV - 1 - s, 1), :] * _shift_up(dpre, s, rows)
            dx_ref[...] = dx.astype(BF16)
            for s in range(FFN_CONV):
                dw_ref[pl.ds(FFN_CONV - 1 - s, 1), :] = jnp.sum(dpre * src[s], axis=0, keepdims=True)
            db_ref[...] = jnp.sum(dpre, axis=0, keepdims=True)

    col = lambda shift: pl.BlockSpec((rows, LANES), lambda j: (0, j + shift))
    wsp = lambda shift: pl.BlockSpec((FFN_CONV, LANES), lambda j: (0, j + shift))
    bsp = lambda shift: pl.BlockSpec((1, LANES), lambda j: (0, j + shift))
    dx_shape = jax.ShapeDtypeStruct((rows, D_FF), BF16)
    dw_shape = jax.ShapeDtypeStruct((FFN_CONV, D_FF), F32)
    db_shape = jax.ShapeDtypeStruct((1, D_FF), F32)
    return _call(
        body, name=name, out_shape=(dx_shape, dx_shape, dw_shape, dw_shape, db_shape, db_shape), grid=(nt,),
        in_specs=[col(0), col(nt), col(0), wsp(0), wsp(nt), bsp(0), bsp(nt)],
        out_specs=(col(0), col(0), wsp(0), wsp(0), bsp(0), bsp(0)),
        operands=[up, up, dact, cw, cw, cb, cb], semantics=("parallel",), steps=steps)


def _dt_fwd(name, dtr, bias):
    rows = dtr.shape[0]
    tm = _row_tile(rows, LANES)

    def body(d_ref, b_ref, o_ref):
        v = d_ref[...] + b_ref[...]
        sp = jnp.maximum(v, 0.0) + jnp.log1p(jnp.exp(-jnp.abs(v)))
        lane = lax.broadcasted_iota(jnp.int32, (tm, LANES), 1)
        ok = _rows_mask(pl.program_id(0), tm) & (lane < SSM_HEADS)
        o_ref[...] = jnp.where(ok, sp, 0.0)

    return pl.pallas_call(
        body, name=name, out_shape=jax.ShapeDtypeStruct((rows, LANES), F32), grid=(rows // tm,),
        in_specs=[pl.BlockSpec((tm, LANES), lambda i: (i, 0)), pl.BlockSpec((1, LANES), lambda i: (0, 0))],
        out_specs=pl.BlockSpec((tm, LANES), lambda i: (i, 0)), compiler_params=_cparams(("parallel",)),
    )(dtr, bias)


def _dt_bwd(name, ddt, dtr, bias):
    rows = dtr.shape[0]
    tm = _row_tile(rows, LANES)

    def body(g_ref, d_ref, b_ref, o_ref, db_ref):
        i = pl.program_id(0)
        lane = lax.broadcasted_iota(jnp.int32, (tm, LANES), 1)
        ok = _rows_mask(i, tm) & (lane < SSM_HEADS)
        dv = jnp.where(ok, g_ref[...] * _sigmoid(d_ref[...] + b_ref[...]), 0.0)
        o_ref[...] = dv.astype(BF16)

        @pl.when(i == 0)
        def _():
            db_ref[...] = jnp.zeros_like(db_ref)

        db_ref[...] += jnp.sum(dv, axis=0, keepdims=True)

    row_spec = pl.BlockSpec((tm, LANES), lambda i: (i, 0))
    vec_spec = pl.BlockSpec((1, LANES), lambda i: (0, 0))
    return pl.pallas_call(
        body, name=name,
        out_shape=(jax.ShapeDtypeStruct((rows, LANES), BF16), jax.ShapeDtypeStruct((1, LANES), F32)),
        grid=(rows // tm,), in_specs=[row_spec, row_spec, vec_spec], out_specs=(row_spec, vec_spec),
        compiler_params=_cparams(("arbitrary",)),
    )(ddt, dtr, bias)


def _gate_fwd(name, y, zx, w, steps=()):
    rows = y.shape[0]
    tm = _row_tile(rows, D_INNER)

    def body(y_ref, z_ref, w_ref, o_ref):
        z = z_ref[...]
        g = y_ref[...] * (z * _sigmoid(z))
        r = lax.rsqrt(jnp.mean(g * g, axis=-1, keepdims=True) + RMS_EPS)
        o_ref[...] = (g * r * w_ref[...]).astype(BF16)

    row_spec = pl.BlockSpec((tm, D_INNER), lambda i: (i, 0))
    return _call(
        body, name=name, out_shape=jax.ShapeDtypeStruct((rows, D_INNER), BF16), grid=(rows // tm,),
        in_specs=[row_spec, row_spec, pl.BlockSpec((1, D_INNER), lambda i: (0, 0))],
        out_specs=row_spec, operands=[y, zx, w], semantics=("parallel",), steps=steps)


def _gate_bwd(name, dyn, y, zx, w):
    rows = y.shape[0]
    tm = _row_tile(rows, D_INNER)

    def body(d_ref, y_ref, z_ref, w_ref, dy_ref, dz_ref, dw_ref):
        i = pl.program_id(0)
        z, yv = z_ref[...], y_ref[...]
        sig = _sigmoid(z)
        sz = z * sig
        g = yv * sz
        r = lax.rsqrt(jnp.mean(g * g, axis=-1, keepdims=True) + RMS_EPS)
        ghat = g * r
        dn = d_ref[...]
        dghat = dn * w_ref[...]
        dg = r * (dghat - ghat * jnp.mean(dghat * ghat, axis=-1, keepdims=True))
        dy_ref[...] = dg * sz
        dz_ref[...] = (dg * yv * sig * (1.0 + z * (1.0 - sig))).astype(BF16)

        @pl.when(i == 0)
        def _():
            dw_ref[...] = jnp.zeros_like(dw_ref)

        dw_ref[...] += jnp.sum(dn * ghat, axis=0, keepdims=True)

    row_spec = pl.BlockSpec((tm, D_INNER), lambda i: (i, 0))
    vec_spec = pl.BlockSpec((1, D_INNER), lambda i: (0, 0))
    return pl.pallas_call(
        body, name=name,
        out_shape=(jax.ShapeDtypeStruct((rows, D_INNER), F32), jax.ShapeDtypeStruct((rows, D_INNER), BF16),
                   jax.ShapeDtypeStruct((1, D_INNER), F32)),
        grid=(rows // tm,), in_specs=[row_spec, row_spec, row_spec, vec_spec],
        out_specs=(row_spec, row_spec, vec_spec), compiler_params=_cparams(("arbitrary",)),
    )(dyn, y, zx, w)


def _split3(x):
    hi = x.astype(BF16)
    r1 = x - hi.astype(F32)
    mid = r1.astype(BF16)
    lo = (r1 - mid.astype(F32)).astype(BF16)
    return hi, mid, lo


def _dot3_data_lhs(x, sel):
    sel16 = sel.astype(F32).astype(BF16)
    hi, mid, lo = _split3(x)
    return _dot(hi, sel16) + _dot(mid, sel16) + _dot(lo, sel16)


def _dot3_data_rhs(sel, x):
    sel16 = sel.astype(F32).astype(BF16)
    hi, mid, lo = _split3(x)
    return _dot(sel16, hi) + _dot(sel16, mid) + _dot(sel16, lo)


def _causal_masks():
    r = lax.broadcasted_iota(jnp.int32, (CHUNK, CHUNK), 0)
    c = lax.broadcasted_iota(jnp.int32, (CHUNK, CHUNK), 1)
    return r >= c, r <= c


def _expand_heads_matrix(g):
    k = lax.broadcasted_iota(jnp.int32, (LANES, GROUP_W), 0)
    j = lax.broadcasted_iota(jnp.int32, (LANES, GROUP_W), 1)
    return HEADS_PER_GROUP * g + jnp.right_shift(j, 6) == k


def _reduce_heads_matrix(g):
    j = lax.broadcasted_iota(jnp.int32, (GROUP_W, LANES), 0)
    k = lax.broadcasted_iota(jnp.int32, (GROUP_W, LANES), 1)
    return HEADS_PER_GROUP * g + jnp.right_shift(j, 6) == k


def _reduce_pair_matrix(g, p):
    j = lax.broadcasted_iota(jnp.int32, (LANES, LANES), 0)
    k = lax.broadcasted_iota(jnp.int32, (LANES, LANES), 1)
    return HEADS_PER_GROUP * g + 2 * p + jnp.right_shift(j, 6) == k


def _group_cols(ref, g, width):
    return ref.at[:, pl.ds(g * width, width)]


def _ssd_prep(name, dt, a128, steps=()):
    rows = dt.shape[0]
    nc = rows // CHUNK

    def body(dt_ref, a_ref, dte_ref, acs_ref, acst_ref):
        causal, _ = _causal_masks()
        dtv = dt_ref[...]
        acs = _dot3_data_rhs(causal, dtv) * a_ref[...]
        acst_ref[...] = acs.T[0:SSM_HEADS]
        for g in range(N_GROUPS):
            expand = _expand_heads_matrix(g)
            _group_cols(dte_ref, g, GROUP_W)[...] = _dot3_data_lhs(dtv, expand)
            _group_cols(acs_ref, g, GROUP_W)[...] = _dot3_data_lhs(acs, expand)

    blk = pl.BlockSpec((CHUNK, D_INNER), lambda c: (c, 0))
    shp = jax.ShapeDtypeStruct((rows, D_INNER), F32)
    return _call(
        body, name=name, out_shape=(shp, shp, jax.ShapeDtypeStruct((nc, SSM_HEADS, CHUNK), F32)), grid=(nc,),
        in_specs=[pl.BlockSpec((CHUNK, LANES), lambda c: (c, 0)), pl.BlockSpec((1, LANES), lambda c: (0, 0))],
        out_specs=(blk, blk, pl.BlockSpec((None, SSM_HEADS, CHUNK), lambda c: (c, 0, 0))),
        operands=[dt, a128], semantics=("parallel",), steps=steps)


def _ssd_common(x_ref, b_ref, c_ref, dte_ref, acs_ref):
    x = x_ref[...]
    dt_exp = dte_ref[...]
    acs_exp = acs_ref[...]
    tot_exp = acs_ref[pl.ds(CHUNK - 1, 1), :]
    xdt = x * dt_exp
    e_exp = jnp.exp(acs_exp)
    f_exp = jnp.exp(tot_exp - acs_exp)
    return _causal_masks(), x, dt_exp, acs_exp, tot_exp, xdt, e_exp, f_exp, b_ref[...], c_ref[...]


def _pair_decay(acs_pair, acs_row, e, causal):
    lane = lax.broadcasted_iota(jnp.int32, (CHUNK, LANES), 1)
    mine = (lane < HEAD_DIM) if e == 0 else (lane >= HEAD_DIM)
    a_l = jnp.where(mine, acs_pair, pltpu.roll(acs_pair, HEAD_DIM, 1))
    seg = a_l - acs_row
    dm = jnp.where(causal[0], jnp.exp(jnp.minimum(seg, 0.0)), 0.0)
    dmt = jnp.where(causal[1], jnp.exp(jnp.minimum(-seg, 0.0)), 0.0)
    return dm, dmt


def _ssd_specs(index_of_chunk):
    wide = pl.BlockSpec((CHUNK, D_INNER), lambda c: (index_of_chunk(c), 0))
    b_spec = pl.BlockSpec((CHUNK, D_BC), lambda c: (index_of_chunk(c), D_INNER // D_BC))
    c_spec = pl.BlockSpec((CHUNK, D_BC), lambda c: (index_of_chunk(c), D_INNER // D_BC + 1))
    rows_spec = pl.BlockSpec((None, SSM_HEADS, CHUNK), lambda c: (index_of_chunk(c), 0, 0))
    state_spec = pl.BlockSpec((N_GROUPS, None, D_STATE, GROUP_W), lambda c: (0, index_of_chunk(c), 0, 0))
    return wide, b_spec, c_spec, rows_spec, state_spec


def _ssd_fwd(name, xbc, dt_exp, acs_exp, acs_rows, dskexp, steps=()):
    rows = xbc.shape[0]
    nc = rows // CHUNK

    def body(x_ref, b_ref, c_ref, dte_ref, acs_ref, acst_ref, dsk_ref, y_ref, st_ref, s_scr):
        @pl.when(pl.program_id(0) == 0)
        def _():
            s_scr[...] = jnp.zeros_like(s_scr)

        lane = lax.broadcasted_iota(jnp.int32, (CHUNK, LANES), 1)
        for g in range(N_GROUPS):
            y_g = _group_cols(y_ref, g, GROUP_W)
            causal, x, _, acs_exp_v, tot_exp, xdt, e_exp, f_exp, bm, cm = _ssd_common(
                _group_cols(x_ref, g, GROUP_W), _group_cols(b_ref, g, D_STATE), _group_cols(c_ref, g, D_STATE),
                _group_cols(dte_ref, g, GROUP_W), _group_cols(acs_ref, g, GROUP_W))
            state = s_scr[g]
            st_ref[g] = state
            cb16, bb16 = cm.astype(BF16), bm.astype(BF16)
            cb = _dot_nt(cb16, bb16)
            base = e_exp * _dot(cb16, state.astype(BF16)) + _group_cols(dsk_ref, g, GROUP_W)[...] * x
            for p in range(HEADS_PER_GROUP // 2):
                sl = slice(p * LANES, (p + 1) * LANES)
                xp = xdt[:, sl].astype(BF16)
                yd = []
                for e in range(2):
                    acs_row = acst_ref[pl.ds(g * HEADS_PER_GROUP + 2 * p + e, 1), :]
                    dm, _ = _pair_decay(acs_exp_v[:, sl], acs_row, e, causal)
                    yd.append(_dot((cb * dm).astype(BF16), xp))
                y_g[:, sl] = jnp.where(lane < HEAD_DIM, yd[0], yd[1]) + base[:, sl]
            s_scr[g] = jnp.exp(tot_exp) * state + _dot_tn(bb16, (f_exp * xdt).astype(BF16))

    wide, b_spec, c_spec, rows_spec, state_spec = _ssd_specs(lambda c: c)
    return _call(
        body, name=name,
        out_shape=(jax.ShapeDtypeStruct((rows, D_INNER), F32),
                   jax.ShapeDtypeStruct((N_GROUPS, nc, D_STATE, GROUP_W), F32)),
        grid=(nc,),
        in_specs=[wide, b_spec, c_spec, wide, wide, rows_spec, pl.BlockSpec((1, D_INNER), lambda c: (0, 0))],
        out_specs=(wide, state_spec),
        scratch_shapes=[pltpu.VMEM((N_GROUPS, D_STATE, GROUP_W), F32)],
        operands=[xbc, xbc, xbc, dt_exp, acs_exp, acs_rows, dskexp], semantics=("arbitrary",), steps=steps)


def _ssd_bwd(name, xbc, dt_exp, acs_exp, acs_rows, dt, a128, dskexp, dy, states, steps=()):
    rows = xbc.shape[0]
    nc = rows // CHUNK
    last = nc - 1

    def body(x_ref, b_ref, c_ref, dte_ref, acs_ref, acst_ref, dt_ref, a128_ref, dsk_all, dy_all, st_all,
             dx_all, db_all, dc_all, ddt_ref, dalog_ref, ddsk_ref, ds_all):
        @pl.when(pl.program_id(0) == 0)
        def _():
            ds_all[...] = jnp.zeros_like(ds_all)
            dalog_ref[...] = jnp.zeros_like(dalog_ref)
            ddsk_ref[...] = jnp.zeros_like(ddsk_ref)

        dacs = jnp.zeros((CHUNK, LANES), F32)
        ddt_x = jnp.zeros((CHUNK, LANES), F32)
        for g in range(N_GROUPS):
            dacs_g, ddt_x_g = group(
                g, _group_cols(x_ref, g, GROUP_W), _group_cols(b_ref, g, D_STATE), _group_cols(c_ref, g, D_STATE),
                _group_cols(dte_ref, g, GROUP_W), _group_cols(acs_ref, g, GROUP_W), acst_ref,
                _group_cols(dsk_all, g, GROUP_W), _group_cols(dy_all, g, GROUP_W), st_all.at[g],
                _group_cols(dx_all, g, GROUP_W), _group_cols(db_all, g, D_STATE), _group_cols(dc_all, g, D_STATE),
                ddsk_ref, ds_all.at[g])
            dacs, ddt_x = dacs + dacs_g, ddt_x + ddt_x_g
        _, causal_t = _causal_masks()
        da = _dot3_data_rhs(causal_t, dacs)
        ddt_ref[...] = da * a128_ref[...] + ddt_x
        dalog_ref[...] += jnp.sum(da * dt_ref[...], axis=0, keepdims=True) * a128_ref[...]

    def group(g, x_ref, b_ref, c_ref, dte_ref, acs_ref, acst_ref, dsk_ref, dy_ref, st_ref,
              dx_ref, db_ref, dc_ref, ddsk_ref, ds_scr):
        causal, x, dt_exp, acs_exp_v, tot_exp, xdt, e_exp, f_exp, bm, cm = _ssd_common(
            x_ref, b_ref, c_ref, dte_ref, acs_ref)
        reduce_heads = _reduce_heads_matrix(g)
        state, dstate = st_ref[...], ds_scr[...]
        dyv = dy_ref[...]
        cb16, bb16 = cm.astype(BF16), bm.astype(BF16)
        s16, ds16 = state.astype(BF16), dstate.astype(BF16)
        cb = _dot_nt(cb16, bb16)
        cbt = _dot_nt(bb16, cb16)
        cs = _dot(cb16, s16)
        bds = _dot(bb16, ds16)
        edy = e_exp * dyv
        fx = f_exp * xdt
        dxdt_base = f_exp * bds
        dc_acc = _dot_nt(edy.astype(BF16), s16)
        db_acc = _dot_nt(fx.astype(BF16), ds16)
        ds_scr[...] = jnp.exp(tot_exp) * dstate + _dot_tn(cb16, edy.astype(BF16))
        q = fx * bds
        dacs = _dot3_data_lhs(edy * cs - q, reduce_heads)
        dtot = jnp.sum(_dot3_data_lhs(q + jnp.exp(tot_exp) * dstate * state, reduce_heads), axis=0, keepdims=True)
        ddsk_ref[...] += jnp.sum(_dot3_data_lhs(dyv * x, reduce_heads), axis=0, keepdims=True)
        lane = lax.broadcasted_iota(jnp.int32, (CHUNK, LANES), 1)
        dcb = jnp.zeros((CHUNK, CHUNK), F32)
        dcbt = jnp.zeros((CHUNK, CHUNK), F32)
        ddt_x = jnp.zeros((CHUNK, LANES), F32)
        for p in range(HEADS_PER_GROUP // 2):
            sl = slice(p * LANES, (p + 1) * LANES)
            xp, dyp = xdt[:, sl], dyv[:, sl]
            xp16, dyp16 = xp.astype(BF16), dyp.astype(BF16)
            dxh = []
            for e in range(2):
                h = 2 * p + e
                mine = (lane < HEAD_DIM) if e == 0 else (lane >= HEAD_DIM)
                acs_row = acst_ref[pl.ds(g * HEADS_PER_GROUP + h, 1), :]
                dm, dmt = _pair_decay(acs_exp_v[:, sl], acs_row, e, causal)
                m, mt = cb * dm, cbt * dmt
                xh16 = jnp.where(mine, xp, 0.0).astype(BF16)
                dyh16 = jnp.where(mine, dyp, 0.0).astype(BF16)
                d_m = _dot_nt(dyh16, xp16)
                d_mt = _dot_nt(xh16, dyp16)
                dacs_h = (jnp.sum(d_m * m, axis=-1, keepdims=True)
                          - jnp.sum(d_mt * mt, axis=-1, keepdims=True))
                dacs = dacs + jnp.where(lane == HEADS_PER_GROUP * g + h, dacs_h, 0.0)
                dcb = dcb + d_m * dm
                dcbt = dcbt + d_mt * dmt
                dxh.append(_dot(mt.astype(BF16), dyp16))
            dxdt = jnp.where(lane < HEAD_DIM, dxh[0], dxh[1]) + dxdt_base[:, sl]
            dx_ref[:, sl] = dxdt * dt_exp[:, sl] + dsk_ref[:, sl] * dyp
            ddt_x = ddt_x + _dot3_data_lhs(dxdt * x[:, sl], _reduce_pair_matrix(g, p))
        dc_ref[...] = dc_acc + _dot(dcb.astype(BF16), bb16)
        db_ref[...] = db_acc + _dot(dcbt.astype(BF16), cb16)
        row = lax.broadcasted_iota(jnp.int32, (CHUNK, LANES), 0)
        return dacs + jnp.where(row == CHUNK - 1, dtot, 0.0), ddt_x

    wide, b_spec, c_spec, rows_spec, state_spec = _ssd_specs(lambda c: last - c)
    heads_spec = pl.BlockSpec((CHUNK, LANES), lambda c: (last - c, 0))
    vec_spec = pl.BlockSpec((1, LANES), lambda c: (0, 0))
    bc_out = pl.BlockSpec((CHUNK, D_BC), lambda c: (last - c, 0))
    vec_shape = jax.ShapeDtypeStruct((1, LANES), F32)
    return _call(
        body, name=name,
        out_shape=(jax.ShapeDtypeStruct((rows, D_INNER), F32), jax.ShapeDtypeStruct((rows, D_BC), F32),
                   jax.ShapeDtypeStruct((rows, D_BC), F32), jax.ShapeDtypeStruct((rows, LANES), F32),
                   vec_shape, vec_shape),
        grid=(nc,),
        in_specs=[wide, b_spec, c_spec, wide, wide, rows_spec, heads_spec, vec_spec,
                  pl.BlockSpec((1, D_INNER), lambda c: (0, 0)), wide, state_spec],
        out_specs=(wide, bc_out, bc_out, heads_spec, vec_spec, vec_spec),
        scratch_shapes=[pltpu.VMEM((N_GROUPS, D_STATE, GROUP_W), F32)],
        operands=[xbc, xbc, xbc, dt_exp, acs_exp, acs_rows, dt, a128, dskexp, dy, states],
        semantics=("arbitrary",), steps=steps)


def _attn_visible(b, heads=1):
    row = jnp.bitwise_and(lax.broadcasted_iota(jnp.int32, (heads * CHUNK, 3 * CHUNK), 0), CHUNK - 1)
    col = lax.broadcasted_iota(jnp.int32, (heads * CHUNK, 3 * CHUNK), 1)
    bb = b + jnp.zeros_like(col)
    meta = (col < CHUNK) & (bb >= 1) & (col >= PAD_ROWS)
    prev = (col >= CHUNK) & (col < 2 * CHUNK) & (bb >= 2) & ((col - CHUNK) > row)
    cur = (col >= 2 * CHUNK) & ((col - 2 * CHUNK) <= row) & ((bb >= 1) | ((col - 2 * CHUNK) >= PAD_ROWS))
    return meta | prev | cur


def _attn_visible4(b):
    return _attn_visible(b, 4)


def _stack_heads(q_ref, sink_ref, kvh, scale):
    lane = lax.broadcasted_iota(jnp.int32, (CHUNK, LANES), 1)
    parts, sinks = [], []
    for pp in range(2):
        pair = kvh * 2 + pp
        qp = q_ref[:, pair * LANES:(pair + 1) * LANES] * scale
        for e in range(2):
            mine = (lane < HEAD_DIM) if e == 0 else (lane >= HEAD_DIM)
            parts.append(jnp.where(mine, qp, 0.0).astype(BF16))
            sinks.append(jnp.full((CHUNK, 1), sink_ref[2 * pair + e], F32))
    return jnp.concatenate(parts, axis=0), jnp.concatenate(sinks, axis=0)


def _attn_operands(q_ref, k0, kp, kc, v0, vp, vc, sink_ref):
    kcat, vcat, q4, sink4 = [], [], [], []
    for kvh in range(N_KV_HEADS):
        ksl = slice(kvh * LANES, (kvh + 1) * LANES)
        kcat.append(jnp.concatenate([k0[:, ksl], kp[:, ksl], kc[:, ksl]], axis=0).astype(BF16))
        vcat.append(jnp.concatenate([v0[:, ksl], vp[:, ksl], vc[:, ksl]], axis=0).astype(BF16))
        stacked, sinks = _stack_heads(q_ref, sink_ref, kvh, ATTN_SCALE)
        q4.append(stacked)
        sink4.append(sinks)
    return kcat, vcat, q4, sink4


def _attn_probs(q4, kcat, visible, sink4):
    heads = range(N_KV_HEADS)
    s = [jnp.where(visible, _dot_nt(q4[h], kcat[h]), NEG_INF) for h in heads]
    m = [jnp.maximum(jnp.max(s[h], axis=-1, keepdims=True), sink4[h]) for h in heads]
    pe = [jnp.exp(s[h] - m[h]) for h in heads]
    pe_sink = [jnp.exp(sink4[h] - m[h]) for h in heads]
    inv = [1.0 / (jnp.sum(pe[h], axis=-1, keepdims=True) + pe_sink[h]) for h in heads]
    return [pe[h] * inv[h] for h in heads], [pe_sink[h] * inv[h] for h in heads]


def _unstack_pairs(stacked, pp):
    lane = lax.broadcasted_iota(jnp.int32, (CHUNK, LANES), 1)
    return jnp.where(lane < HEAD_DIM, stacked[(2 * pp) * CHUNK:(2 * pp + 1) * CHUNK],
                     stacked[(2 * pp + 1) * CHUNK:(2 * pp + 2) * CHUNK])


def _attn_specs(colblock):
    blk = lambda f: pl.BlockSpec((CHUNK, 2 * D_KV), f)
    return [blk(lambda b: (0, colblock)), blk(lambda b: (jnp.maximum(b - 1, 0), colblock)), blk(lambda b: (b, colblock))]


def _attn_fwd(name, q, kv2, sinks, steps=()):
    rows = q.shape[0]

    def body(q_ref, k0, kp, kc, v0, vp, vc, sink_ref, o_ref):
        visible = _attn_visible4(pl.program_id(0))
        kcat, vcat, q4, sink4 = _attn_operands(q_ref, k0, kp, kc, v0, vp, vc, sink_ref)
        pn, _ = _attn_probs(q4, kcat, visible, sink4)
        o4 = [_dot(pn[h].astype(BF16), vcat[h]) for h in range(N_KV_HEADS)]
        for kvh in range(N_KV_HEADS):
            for pp in range(2):
                qsl = slice((kvh * 2 + pp) * LANES, (kvh * 2 + pp + 1) * LANES)
                o_ref[:, qsl] = _unstack_pairs(o4[kvh], pp).astype(BF16)

    return _call(
        body, name=name, out_shape=jax.ShapeDtypeStruct((rows, D_MODEL), BF16), grid=(rows // CHUNK,),
        in_specs=[pl.BlockSpec((CHUNK, D_MODEL), lambda b: (b, 0))] + _attn_specs(0) + _attn_specs(1)
        + [pl.BlockSpec(memory_space=pltpu.SMEM)],
        out_specs=pl.BlockSpec((CHUNK, D_MODEL), lambda b: (b, 0)),
        operands=[q, kv2, kv2, kv2, kv2, kv2, kv2, sinks], semantics=("parallel",), steps=steps)


def _attn_bwd(name, q, kv2, sinks, do, steps=()):
    rows = q.shape[0]

    def body(q_ref, k0, kp, kc, v0, vp, vc, sink_ref, do_ref,
             dq_ref, dkc_ref, dkp_ref, dvc_ref, dvp_ref, dkm_ref, dvm_ref, dsink_ref):
        @pl.when(pl.program_id(0) == 0)
        def _():
            dkm_ref[...] = jnp.zeros_like(dkm_ref)
            dvm_ref[...] = jnp.zeros_like(dvm_ref)
            dsink_ref[...] = jnp.zeros_like(dsink_ref)

        visible = _attn_visible4(pl.program_id(0))
        heads = range(N_KV_HEADS)
        lane1 = lax.broadcasted_iota(jnp.int32, (1, LANES), 1)
        kcat, vcat, q4, sink4 = _attn_operands(q_ref, k0, kp, kc, v0, vp, vc, sink_ref)
        do4 = [_stack_heads(do_ref, sink_ref, h, 1.0)[0] for h in heads]
        pn, psink = _attn_probs(q4, kcat, visible, sink4)
        dp = [_dot_nt(do4[h], vcat[h]) for h in heads]
        delta = [jnp.sum(pn[h] * dp[h], axis=-1, keepdims=True) for h in heads]
        ds16 = [(pn[h] * (dp[h] - delta[h])).astype(BF16) for h in heads]
        dq4 = [_dot(ds16[h], kcat[h]) for h in heads]
        dk_acc = [_dot_tn(ds16[h], q4[h]) for h in heads]
        dv_acc = [_dot_tn(pn[h].astype(BF16), do4[h]) for h in heads]
        dsink = jnp.zeros((1, LANES), F32)
        for kvh in heads:
            ksl = slice(kvh * LANES, (kvh + 1) * LANES)
            sink_terms = psink[kvh] * delta[kvh]
            for j in range(4):
                part = jnp.sum(sink_terms[j * CHUNK:(j + 1) * CHUNK], axis=0, keepdims=True)
                dsink = dsink - jnp.where(lane1 == kvh * 4 + j, part, 0.0)
            for pp in range(2):
                qsl = slice((kvh * 2 + pp) * LANES, (kvh * 2 + pp + 1) * LANES)
                dq_ref[:, qsl] = (_unstack_pairs(dq4[kvh], pp) * ATTN_SCALE).astype(BF16)
            dkm_ref[:, ksl] += dk_acc[kvh][0:CHUNK]
            dvm_ref[:, ksl] += dv_acc[kvh][0:CHUNK]
            dkp_ref[:, ksl] = dk_acc[kvh][CHUNK:2 * CHUNK]
            dvp_ref[:, ksl] = dv_acc[kvh][CHUNK:2 * CHUNK]
            dkc_ref[:, ksl] = dk_acc[kvh][2 * CHUNK:3 * CHUNK]
            dvc_ref[:, ksl] = dv_acc[kvh][2 * CHUNK:3 * CHUNK]
        dsink_ref[...] += dsink

    qspec = pl.BlockSpec((CHUNK, D_MODEL), lambda b: (b, 0))
    kvspec = pl.BlockSpec((CHUNK, 2 * D_KV), lambda b: (b, 0))
    fixed = pl.BlockSpec((CHUNK, 2 * D_KV), lambda b: (0, 0))
    kv_shape = jax.ShapeDtypeStruct((rows, 2 * D_KV), F32)
    meta_shape = jax.ShapeDtypeStruct((CHUNK, 2 * D_KV), F32)
    return _call(
        body, name=name,
        out_shape=(jax.ShapeDtypeStruct((rows, D_MODEL), BF16), kv_shape, kv_shape, kv_shape, kv_shape,
                   meta_shape, meta_shape, jax.ShapeDtypeStruct((1, LANES), F32)),
        grid=(rows // CHUNK,),
        in_specs=[qspec] + _attn_specs(0) + _attn_specs(1) + [pl.BlockSpec(memory_space=pltpu.SMEM), qspec],
        out_specs=(qspec, kvspec, kvspec, kvspec, kvspec, fixed, fixed, pl.BlockSpec((1, LANES), lambda b: (0, 0))),
        operands=[q, kv2, kv2, kv2, kv2, kv2, kv2, sinks, do], semantics=("arbitrary",), steps=steps)


def _kv_grad_combine(name, dk_cur, dk_prev, dk_meta, dv_cur, dv_prev, dv_meta):
    rows = dk_cur.shape[0]
    nb = rows // CHUNK
    width = 2 * D_KV

    def body(kc_ref, kp_ref, km_ref, vc_ref, vp_ref, vm_ref, o_ref):
        jj = pl.program_id(0) + jnp.zeros((CHUNK, 1), jnp.int32)
        for half, (c_ref, p_ref, m_ref) in enumerate(((kc_ref, kp_ref, km_ref), (vc_ref, vp_ref, vm_ref))):
            total = c_ref[...] + jnp.where(jj < nb - 1, p_ref[...], 0.0) + jnp.where(jj == 0, m_ref[...], 0.0)
            o_ref[:, half * width:(half + 1) * width] = total.astype(BF16)

    blk = lambda f: pl.BlockSpec((CHUNK, width), f)
    three = lambda: [blk(lambda j: (j, 0)), blk(lambda j: (jnp.minimum(j + 1, nb - 1), 0)), blk(lambda j: (0, 0))]
    return pl.pallas_call(
        body, name=name, out_shape=jax.ShapeDtypeStruct((rows, 2 * width), BF16), grid=(nb,),
        in_specs=three() + three(), out_specs=pl.BlockSpec((CHUNK, 2 * width), lambda j: (j, 0)),
        compiler_params=_cparams(("parallel",)),
    )(dk_cur, dk_prev, dk_meta, dv_cur, dv_prev, dv_meta)


def _adamw(name, w, g, m, v):
    rows, width = w.shape
    tr = rows
    for cand in range(8, rows + 1, 8):
        if rows % cand == 0 and cand * width * 4 <= (1 << 20):
            tr = cand

    def body(w_ref, g_ref, m_ref, v_ref, d_ref, mo_ref, vo_ref):
        gv = g_ref[...]
        mn = ADAM_B1 * m_ref[...] + (1.0 - ADAM_B1) * gv
        vn = ADAM_B2 * v_ref[...] + (1.0 - ADAM_B2) * (gv * gv)
        m_hat = mn / (1.0 - ADAM_B1 ** ADAM_STEP)
        v_hat = vn / (1.0 - ADAM_B2 ** ADAM_STEP)
        d_ref[...] = -ADAM_LR * (m_hat / (jnp.sqrt(v_hat) + ADAM_EPS) + ADAM_WD * w_ref[...])
        mo_ref[...] = mn
        vo_ref[...] = vn

    blk = pl.BlockSpec((tr, width), lambda i: (i, 0))
    shp = jax.ShapeDtypeStruct((rows, width), F32)
    return pl.pallas_call(
        body, name=name, out_shape=(shp, shp, shp), grid=(rows // tr,), in_specs=[blk] * 4, out_specs=(blk,) * 3,
        compiler_params=_cparams(("parallel",)),
    )(w, g, m, v)


class _GivenWeights:
    def __init__(self, p):
        self.p = p
        self.grads = {}

    def weight(self, name, layer=None):
        return self.p[name] if layer is None else self.p[name][layer]

    def steps(self, kernel):
        return ()

    def grad(self, name, layer, g):
        self.grads[(name, layer)] = g


def _ffn_fwd(tag, h, hn, p, i, plan):
    up = _mm_nn_bychip(f"ffn{tag}_up", hn, plan.weight("f_w_up", i))
    act = _ffn_conv_fwd(f"ffn{tag}_conv", up, p["f_conv_w"][i], p["f_conv_b"][i:i + 1], steps=plan.steps(f"ffn{tag}_conv"))
    pre = _mm(f"ffn{tag}_down", act, plan.weight("f_w_down", i), "nn")
    return pre, (h, hn, up, act, pre)


def _ffn_bwd(tag, dpre, saved, p, i, plan):
    h, hn, up, act, pre = saved
    plan.grad("f_w_down", i, _mm(f"ffn{tag}_down_dw", act, dpre, "tn", out_dtype=BF16))
    dact = _mm(f"ffn{tag}_down_dx", dpre, plan.weight("f_w_down", i), "nt", steps=plan.steps(f"ffn{tag}_down_dx"))
    dug, duv, gwg, gwv, gbg, gbv = _ffn_conv_bwd(f"ffn{tag}_conv_bwd", up, dact, p["f_conv_w"][i], p["f_conv_b"][i:i + 1],
                                                 steps=plan.steps(f"ffn{tag}_conv_bwd"))
    g_cw, g_cb = jnp.concatenate([gwg, gwv], axis=1), jnp.concatenate([gbg, gbv], axis=1)
    w_up = plan.weight("f_w_up", i)
    n = w_up.shape[2]
    dhn = _mm_nt_bychip(f"ffn{tag}_up_dx_gate", dug, w_up, 0)
    dhn = _mm_nt_bychip(f"ffn{tag}_up_dx_val", duv, w_up, N_CHIPS // 2, acc=dhn)
    g_up = _mm_tn_bychip(f"ffn{tag}_up_dw_gate", hn, dug, n, 0)
    plan.grad("f_w_up", i, _mm_tn_bychip(f"ffn{tag}_up_dw_val", hn, duv, n, N_CHIPS // 2, into=g_up))
    return dhn, dict(f_conv_w=g_cw, f_conv_b=g_cb)


def _lanes_pad(a, width=LANES):
    return jnp.pad(a, [(0, 0)] * (a.ndim - 1) + [(0, width - a.shape[-1])])


def _dup_heads(w):
    rows = w.shape[0]
    w = w.reshape(rows, 2 * N_KV_HEADS, 1, HEAD_DIM)
    return jnp.broadcast_to(w, (rows, 2 * N_KV_HEADS, 2, HEAD_DIM)).reshape(rows, 4 * D_KV)


def _undup_heads(g):
    rows = g.shape[0]
    return g.reshape(rows, 2 * N_KV_HEADS, 2, HEAD_DIM).sum(axis=2).reshape(rows, 2 * D_KV)


def _local_step(x2, target, p, plan):
    seq = x2.shape[0]
    rows = seq + CHUNK
    g = {}

    h0 = jnp.concatenate([jnp.zeros((PAD_ROWS, D_MODEL), F32), p["meta_tokens"], x2], axis=0)

    w_in = plan.weight("a_w_in")
    w_dt = jnp.pad(w_in[D_MAIN:], ((0, LANES - SSM_HEADS), (0, 0)))
    dt_bias = _lanes_pad(p["a_dt_bias"])
    a128 = _lanes_pad(-jnp.exp(p["a_a_log"]))
    dskexp = jnp.repeat(p["a_d_skip"].reshape(SSM_HEADS), HEAD_DIM).reshape(1, D_INNER)

    hn0 = _rms_fwd("a_norm", h0, p["a_norm_pre"])
    zx = _mm("a_in_main", hn0, w_in, "nt", k_rows=D_MAIN, steps=plan.steps("a_in_main"))
    dtr = _mm("a_in_dt", hn0, w_dt, "nt")
    xbc = _conv4_fwd("a_conv", zx, p["a_conv_w"], p["a_conv_b"], steps=plan.steps("a_conv"))
    dt = _dt_fwd("a_dt", dtr, dt_bias)
    dt_exp, acs_exp, acs_rows = _ssd_prep("a_ssd_prep", dt, a128, steps=plan.steps("a_ssd_prep"))
    y, states = _ssd_fwd("a_ssd", xbc, dt_exp, acs_exp, acs_rows, dskexp, steps=plan.steps("a_ssd"))
    yn = _gate_fwd("a_gate", y, zx, p["a_gate_norm"], steps=plan.steps("a_gate"))
    mix = _mm("a_out", yn, plan.weight("a_w_out"), "nn")
    h1, (hn_f0,) = _resid_norm_fwd("a_resid", h0, mix, p["a_norm_post"], [p["f_norm_pre"][0:1]])

    pre_f0, ffn0 = _ffn_fwd("0", h1, hn_f0, p, 0, plan)
    h2, (hkv, hn2) = _resid_norm_fwd("ffn0_resid", h1, pre_f0, p["f_norm_post"][0:1], [p["kv_norm"], p["b_norm_pre"]])

    w_kv2 = _dup_heads(plan.weight("w_kv"))
    kv2 = _mm("kv_proj", hkv, w_kv2, "nn")
    q = _mm("b_q", hn2, plan.weight("b_w_q"), "nn")
    sinks = p["b_sinks"].reshape(N_Q_HEADS)
    o = _attn_fwd("b_attn", q, kv2, sinks, steps=plan.steps("b_attn"))
    attn = _mm("b_o", o, plan.weight("b_w_o"), "nn", steps=plan.steps("b_o"))
    h3, (hn_f1,) = _resid_norm_fwd("b_resid", h2, attn, p["b_norm_post"], [p["f_norm_pre"][1:2]])

    pre_f1, ffn1 = _ffn_fwd("1", h3, hn_f1, p, 1, plan)
    dh, loss_vec, dpre_f1, g_post1 = _resid_norm_loss("ffn1_resid_loss", h3, pre_f1, p["f_norm_post"][1:2], target)
    loss = loss_vec[0, 0]

    dhn_f1, g1 = _ffn_bwd("1", dpre_f1, ffn1, p, 1, plan)
    dh, g_pre1, dpre, g["b_norm_post"] = _norm_bwd_add("ffn1_norm_bwd", dh, dhn_f1, h3, p["f_norm_pre"][1:2],
                                                        then=(attn, p["b_norm_post"]))
    plan.grad("b_w_o", None, _mm("b_o_dw", o, dpre, "tn", out_dtype=BF16))
    do = _mm("b_o_dx", dpre, plan.weight("b_w_o"), "nt", steps=plan.steps("b_o_dx"))
    dq, dkc, dkp, dvc, dvp, dkm, dvm, dsink = _attn_bwd("b_attn_bwd", q, kv2, sinks, do, steps=plan.steps("b_attn_bwd"))
    g["b_sinks"] = dsink[:, :N_Q_HEADS]
    dhn2 = _mm("b_q_dx", dq, plan.weight("b_w_q"), "nt")
    plan.grad("b_w_q", None, _mm("b_q_dw", hn2, dq, "tn", out_dtype=BF16))
    dh, g["b_norm_pre"] = _norm_bwd_add("b_norm_bwd", dh, dhn2, h2, p["b_norm_pre"])
    dkv2 = _kv_grad_combine("kv_grad", dkc, dkp, dkm, dvc, dvp, dvm)
    dhkv = _mm("kv_proj_dx", dkv2, w_kv2, "nt")
    plan.grad("w_kv", None, _undup_heads(_mm("kv_proj_dw", hkv, dkv2, "tn")))
    dh, g["kv_norm"], dpre_f0, g_post0 = _norm_bwd_add("kv_norm_bwd", dh, dhkv, h2, p["kv_norm"],
                                                       then=(pre_f0, p["f_norm_post"][0:1]))

    dhn_f0, g0 = _ffn_bwd("0", dpre_f0, ffn0, p, 0, plan)
    dh, g_pre0, dpre, g["a_norm_post"] = _norm_bwd_add("ffn0_norm_bwd", dh, dhn_f0, h1, p["f_norm_pre"][0:1],
                                                        then=(mix, p["a_norm_post"]))
    g["f_norm_post"] = jnp.concatenate([g_post0, g_post1], axis=0)
    g["f_norm_pre"] = jnp.concatenate([g_pre0, g_pre1], axis=0)
    g["f_conv_w"] = jnp.stack([g0["f_conv_w"], g1["f_conv_w"]])
    g["f_conv_b"] = jnp.concatenate([g0["f_conv_b"], g1["f_conv_b"]], axis=0)
    plan.grad("a_w_out", None, _mm("a_out_dw", yn, dpre, "tn", out_dtype=BF16))
    dyn = _mm("a_out_dx", dpre, plan.weight("a_w_out"), "nt", steps=plan.steps("a_out_dx"))
    dy, dz, g["a_gate_norm"] = _gate_bwd("a_gate_bwd", dyn, y, zx, p["a_gate_norm"])
    dxs, dbm, dcm, ddt, dalog, ddsk = _ssd_bwd("a_ssd_bwd", xbc, dt_exp, acs_exp, acs_rows, dt, a128, dskexp, dy, states,
                                              steps=plan.steps("a_ssd_bwd"))
    g["a_a_log"] = dalog[:, :SSM_HEADS]
    g["a_d_skip"] = ddsk[:, :SSM_HEADS]
    ddtr, dbias = _dt_bwd("a_dt_bwd", ddt, dtr, dt_bias)
    g["a_dt_bias"] = dbias[:, :SSM_HEADS]
    dxp, gw_x, gb_x = _conv4_bwd("a_conv_bwd_x", zx, dxs, p["a_conv_w"], p["a_conv_b"], 0)
    dbp, gw_b, gb_b = _conv4_bwd("a_conv_bwd_b", zx, dbm, p["a_conv_w"], p["a_conv_b"], D_INNER)
    dcp, gw_c, gb_c = _conv4_bwd("a_conv_bwd_c", zx, dcm, p["a_conv_w"], p["a_conv_b"], D_INNER + D_BC)
    g["a_conv_w"] = jnp.concatenate([gw_x, gw_b, gw_c], axis=1)
    g["a_conv_b"] = jnp.concatenate([gb_x, gb_b, gb_c], axis=1)
    dzx = jnp.concatenate([dz, dxp, dbp, dcp], axis=1)
    g_in = _mm("a_in_main_dw", dzx, hn0, "tn", out_dtype=BF16, out_rows=D_IN_PROJ, steps=plan.steps("a_in_main_dw"))
    plan.grad("a_w_in", None, _tn_rows_into("a_in_dt_dw", ddtr, hn0, g_in, D_MAIN, SSM_HEADS))
    dhn0 = _mm("a_in_dt_dx", ddtr, w_dt, "nn", steps=plan.steps("a_in_dt_dx"))
    dhn0 = _mm("a_in_main_dx", dzx, w_in, "nn", acc=dhn0, steps=plan.steps("a_in_main_dx"))
    dh, g["a_norm_pre"] = _norm_bwd_add("a_norm_bwd", dh, dhn0, h0, p["a_norm_pre"])

    g["meta_tokens"] = dh[PAD_ROWS:CHUNK]
    return loss, dh[CHUNK:], g


ANY = pl.BlockSpec(memory_space=pl.ANY)
VMEM_SPEC = pl.BlockSpec(memory_space=pltpu.VMEM)


def _allgather_small(name, shard):
    rows = shard.shape[0]

    def body(s_ref, o_ref, send_sems, recv_sems):
        x, y, c = _place()
        me = 2 * x + y
        o_ref[me] = s_ref[...]
        chips = _other_chips(x, y)
        sends = [pltpu.make_async_remote_copy(s_ref, o_ref.at[me], send_sems.at[j], recv_sems.at[j],
                                              device_id=(cx, cy, c), device_id_type=MESH)
                 for j, (cx, cy) in enumerate(chips)]
        for cp in sends:
            cp.start()
        for j, (cx, cy) in enumerate(chips):
            pltpu.make_async_remote_copy(s_ref, o_ref.at[2 * cx + cy], send_sems.at[j], recv_sems.at[j],
                                         device_id=(cx, cy, c), device_id_type=MESH).wait_recv()
        for cp in sends:
            cp.wait_send()

    return pl.pallas_call(
        body, name=name, out_shape=jax.ShapeDtypeStruct((N_CHIPS, rows, LANES), F32),
        in_specs=[VMEM_SPEC], out_specs=VMEM_SPEC,
        scratch_shapes=[pltpu.SemaphoreType.DMA((3,)), pltpu.SemaphoreType.DMA((3,))],
        compiler_params=pltpu.CompilerParams(vmem_limit_bytes=VMEM_LIMIT),
    )(shard)


def _row_block(rows, width, itemsize, align, budget=2 << 20):
    best = rows
    for cand in range(align, rows + 1, align):
        if rows % cand == 0 and cand * width * itemsize <= budget:
            best = cand
    return best


def _cast_into_slot(name, chip, w, layer=None):
    rows, width = w.shape[-2:]
    tr = _row_block(rows, width, 4, 16)
    if layer is None:
        in_spec = pl.BlockSpec((tr, width), lambda i, chip_ref: (i, 0))
    else:
        in_spec = pl.BlockSpec((None, tr, width), lambda i, chip_ref: (layer, i, 0))

    def body(chip_ref, w_ref, o_ref):
        o_ref[...] = w_ref[...].astype(BF16)

    return pl.pallas_call(
        body, name=name, out_shape=jax.ShapeDtypeStruct((N_CHIPS, rows, width), BF16),
        grid_spec=pltpu.PrefetchScalarGridSpec(
            num_scalar_prefetch=1, grid=(rows // tr,), in_specs=[in_spec],
            out_specs=pl.BlockSpec((None, tr, width), lambda i, chip_ref: (chip_ref[0], i, 0))),
        compiler_params=_cparams(("parallel",)),
    )(chip, w)


def _allreduce_small(name, vec):
    rows = vec.shape[0]

    def body(v_ref, o_ref, buf, send_sems, recv_sems):
        x, y, c = _place()
        me = 4 * x + 2 * y + c
        buf[me] = v_ref[...]

        def peer(k):
            kx, ky, kc = (k >> 2) & 1, (k >> 1) & 1, k & 1
            return (1 - x if kx else x, 1 - y if ky else y, 1 - c if kc else c)

        sends = []
        for k in range(1, N_DEV):
            cp = pltpu.make_async_remote_copy(v_ref, buf.at[me], send_sems.at[k - 1], recv_sems.at[k - 1],
                                              device_id=peer(k), device_id_type=MESH)
            cp.start()
            sends.append(cp)
        for k in range(1, N_DEV):
            px, py, pc = peer(k)
            pltpu.make_async_remote_copy(v_ref, buf.at[4 * px + 2 * py + pc], send_sems.at[k - 1], recv_sems.at[k - 1],
                                         device_id=(px, py, pc), device_id_type=MESH).wait_recv()
        for cp in sends:
            cp.wait_send()
        acc = buf[0]
        for d in range(1, N_DEV):
            acc = acc + buf[d]
        o_ref[...] = acc

    return pl.pallas_call(
        body, name=name, out_shape=jax.ShapeDtypeStruct((rows, LANES), F32),
        in_specs=[VMEM_SPEC], out_specs=VMEM_SPEC,
        scratch_shapes=[pltpu.VMEM((N_DEV, rows, LANES), F32), pltpu.SemaphoreType.DMA((N_DEV - 1,)),
                        pltpu.SemaphoreType.DMA((N_DEV - 1,))],
        compiler_params=pltpu.CompilerParams(vmem_limit_bytes=VMEM_LIMIT),
    )(vec)


def _rs_pair_add(name, place, grads, partner, split="rows"):
    _, half_rows, width = partner.shape
    tr = _row_block(half_rows, width, 2, 16)
    nb = half_rows // tr
    if split == "rows":
        mine = pl.BlockSpec((None, tr, width), lambda s, i, pr: (s, pr[1] * nb + i, 0))
    else:
        mine = pl.BlockSpec((None, tr, width), lambda s, i, pr: (s, i, pr[1]))

    def body(place_ref, g_ref, p_ref, o_ref):
        o_ref[...] = (g_ref[...].astype(F32) + p_ref[...].astype(F32)).astype(BF16)

    return pl.pallas_call(
        body, name=name, out_shape=jax.ShapeDtypeStruct(partner.shape, BF16),
        grid_spec=pltpu.PrefetchScalarGridSpec(
            num_scalar_prefetch=1, grid=(N_CHIPS, nb),
            in_specs=[mine, pl.BlockSpec((None, tr, width), lambda s, i, pr: (s, i, 0))],
            out_specs=pl.BlockSpec((None, tr, width), lambda s, i, pr: (s, i, 0))),
        compiler_params=_cparams(("parallel", "parallel")),
    )(place, grads, partner)


def _rs_chip_add(name, place, mine, others, split="rows"):
    _, half_rows, width = mine.shape
    tr = _row_block(half_rows, width, 4, 16, budget=1 << 20)
    nb = half_rows // tr
    if split == "rows":
        out_shape, out_spec = (2 * half_rows, width), pl.BlockSpec((tr, width), lambda i, pr: (pr[1] * nb + i, 0))
    else:
        out_shape, out_spec = (half_rows, 2 * width), pl.BlockSpec((tr, width), lambda i, pr: (i, pr[1]))

    def body(place_ref, q_ref, r_ref, o_ref):
        acc = q_ref[...].astype(F32)
        for j in range(3):
            acc = acc + r_ref[j].astype(F32)
        o_ref[...] = acc

    return pl.pallas_call(
        body, name=name, out_shape=jax.ShapeDtypeStruct(out_shape, F32),
        grid_spec=pltpu.PrefetchScalarGridSpec(
            num_scalar_prefetch=1, grid=(nb,),
            in_specs=[pl.BlockSpec((None, tr, width), lambda i, pr: (pr[0], i, 0)),
                      pl.BlockSpec((3, tr, width), lambda i, pr: (0, i, 0))],
            out_specs=out_spec),
        compiler_params=_cparams(("parallel",)),
    )(place, mine, others)


WEIGHTS = ["meta_tokens", "a_norm_pre", "a_w_in", "a_conv_w", "a_conv_b", "a_dt_bias", "a_a_log", "a_d_skip",
           "a_gate_norm", "a_w_out", "a_norm_post", "kv_norm", "w_kv", "b_norm_pre", "b_w_q", "b_sinks", "b_w_o",
           "b_norm_post", "f_norm_pre", "f_w_up", "f_conv_w", "f_conv_b", "f_w_down", "f_norm_post"]
FULL_SHAPE = {
    "meta_tokens": (16, 1024), "a_norm_pre": (1, 1024), "a_w_in": (1, 1024, 5152), "a_conv_w": (1, 4, 3072),
    "a_conv_b": (1, 3072), "a_dt_bias": (1, 32), "a_a_log": (1, 32), "a_d_skip": (1, 32), "a_gate_norm": (1, 2048),
    "a_w_out": (1, 2048, 1024), "a_norm_post": (1, 1024), "kv_norm": (1024,), "w_kv": (1024, 512),
    "b_norm_pre": (1, 1024), "b_w_q": (1, 1024, 1024), "b_sinks": (1, 16), "b_w_o": (1, 1024, 1024),
    "b_norm_post": (1, 1024), "f_norm_pre": (2, 1024), "f_w_up": (2, 1024, 5632), "f_conv_w": (2, 3, 5632),
    "f_conv_b": (2, 5632), "f_w_down": (2, 2816, 1024), "f_norm_post": (2, 1024),
}
SHARD_AXIS = {
    "meta_tokens": 1, "a_norm_pre": 1, "a_w_in": 2, "a_conv_w": 2, "a_conv_b": 1, "a_dt_bias": None, "a_a_log": None,
    "a_d_skip": None, "a_gate_norm": 1, "a_w_out": 1, "a_norm_post": 1, "kv_norm": None, "w_kv": 0, "b_norm_pre": None,
    "b_w_q": 1, "b_sinks": None, "b_w_o": 1, "b_norm_post": None, "f_norm_pre": None, "f_w_up": 2, "f_conv_w": 2,
    "f_conv_b": None, "f_w_down": 1, "f_norm_post": None,
}
BIG = ["a_w_in", "a_w_out", "w_kv", "b_w_q", "b_w_o", "f_w_up", "f_w_down"]
SMALL = [n for n in WEIGHTS if n not in BIG]
SMALL_SHARDED = [n for n in SMALL if SHARD_AXIS[n] is not None]


def _shard_shape(name):
    shape = list(FULL_SHAPE[name])
    if SHARD_AXIS[name] is not None:
        shape[SHARD_AXIS[name]] //= N_CHIPS
    return tuple(shape)


def _numel(shape):
    return int(math.prod(shape))


SUBLANES = 8


def _packed_rows(shape):
    rows = -(-_numel(shape) // LANES)
    return -(-rows // SUBLANES) * SUBLANES


def _pack(arrays):
    parts = []
    for a in arrays:
        size, rows = _numel(a.shape), _packed_rows(a.shape)
        if size % LANES == 0:
            part = jnp.pad(a.reshape(size // LANES, LANES), ((0, rows - size // LANES), (0, 0)))
        else:
            part = jnp.pad(a.reshape(-1), (0, rows * LANES - size)).reshape(rows, LANES)
        parts.append(part)
    return jnp.concatenate(parts, axis=0)


def _unpack(packed, names, shape_of):
    out, off = {}, 0
    lead = packed.shape[:-2]
    for n in names:
        shape = tuple(shape_of(n))
        size, rows = _numel(shape), _packed_rows(shape)
        part = packed[..., off:off + rows, :]
        if size % LANES == 0:
            out[n] = part[..., :size // LANES, :].reshape(lead + shape)
        else:
            out[n] = part.reshape(lead + (rows * LANES,))[..., :size].reshape(lead + shape)
        off += rows
    return out


def _split_chips(name, full):
    ax = SHARD_AXIS[name]
    shape = full.shape
    cut = shape[:ax] + (N_CHIPS, shape[ax] // N_CHIPS) + shape[ax + 1:]
    return jnp.moveaxis(full.reshape(cut), ax, 0)


def _join_chips(name, stacked):
    ax = SHARD_AXIS[name]
    moved = jnp.moveaxis(stacked, 0, ax)
    shape = moved.shape
    return moved.reshape(shape[:ax] + (shape[ax] * shape[ax + 1],) + shape[ax + 2:])


def _as2d(a):
    return a.reshape(-1, a.shape[-1])


BUFFERS = [("a_w_in", "a_w_in", None), ("a_w_out", "a_w_out", None), ("w_kv", "w_kv", None),
           ("b_w_q", "b_w_q", None), ("b_w_o", "b_w_o", None), ("f_w_up0", "f_w_up", 0), ("f_w_up1", "f_w_up", 1),
           ("f_w_down0", "f_w_down", 0), ("f_w_down1", "f_w_down", 1)]


TRANSPOSED = ("a_w_in",)
SPLIT = {"a_w_in": "cols"}


def _local_shard(arrays, weight, layer):
    if weight in TRANSPOSED:
        return arrays[weight][0].T
    return _as2d(arrays[weight]) if layer is None else arrays[weight]


def _weight_from_gathered(weight, buf):
    if weight == "f_w_up":
        return buf
    return buf.reshape(N_CHIPS * buf.shape[1], buf.shape[2])


def _gathered_from_grad(weight, g):
    if weight == "f_w_up":
        return g
    return g.reshape(N_CHIPS, g.shape[0] // N_CHIPS, g.shape[1]).astype(BF16)


GATHER_SCHEDULE = {
    "a_in_main": [("ici", ["a_w_out"])],
    "a_conv": [("d2d", ["a_w_out"]), ("ici", ["f_w_down0"])],
    "a_ssd_prep": [("d2d", ["f_w_down0"]), ("ici", ["w_kv", "b_w_q", "b_w_o"])],
    "a_ssd": [("d2d", ["w_kv", "b_w_q", "b_w_o"]), ("ici", ["f_w_up0"])],
    "a_gate": [("d2d", ["f_w_up0"])],
    "ffn0_conv": [("ici", ["f_w_down1"])],
    "b_attn": [("d2d", ["f_w_down1"]), ("ici", ["f_w_up1"])],
    "b_o": [("d2d", ["f_w_up1"])],
}
REDUCE_SCHEDULE = {
    "ffn1_conv_bwd": ["f_w_down1"],
    "b_attn_bwd": ["f_w_up1", "b_w_o"],
    "ffn0_conv_bwd": ["b_w_q", "w_kv", "f_w_down0"],
    "a_ssd_bwd": ["f_w_up0", "a_w_out"],
    "a_in_main_dx": ["a_w_in"],
}
PAIR_SCHEDULE = {
    "ffn1_down_dx": ["f_w_down1"],
    "b_o_dx": ["f_w_up1", "b_w_o"],
    "ffn0_down_dx": ["b_w_q", "w_kv", "f_w_down0"],
    "a_out_dx": ["f_w_up0", "a_w_out"],
    "a_in_dt_dx": ["a_w_in"],
}
SWAP_SCHEDULE = {"a_in_main_dw": ["f_w_down1", "f_w_up1", "b_w_o", "b_w_q", "w_kv", "f_w_down0", "f_w_up0", "a_w_out"]}


def _buffer_of(weight, layer):
    return weight if layer is None else f"{weight}{layer}"


class _Pipeline:
    def __init__(self, place, slots):
        self.place = place
        self.slots = dict(slots)
        self.running = []
        self.grads = {}
        self.theirs = {}
        self.partials = {}
        self.peers = {}
        self.reduced = {}

    def _collect(self):
        for step, buffers, table in self.running:
            table.update(zip(buffers, step.results))
        self.running = []

    @staticmethod
    def _splits(buffers):
        return [SPLIT.get(b, "rows") for b in buffers]

    def gather_now(self, name, buffers):
        step = _step_gather_full([self.slots[b] for b in buffers], self._splits(buffers))
        _run_steps(name, [step])
        self.slots.update(zip(buffers, step.results))

    def weight(self, name, layer=None):
        self._collect()
        return _weight_from_gathered(name, self.slots[_buffer_of(name, layer)])

    def grad(self, name, layer, g):
        self.grads[_buffer_of(name, layer)] = _gathered_from_grad(name, g)

    def steps(self, kernel):
        self._collect()
        steps = []
        for phase, buffers in GATHER_SCHEDULE.get(kernel, []):
            make = _step_gather_ici if phase == "ici" else _step_gather_d2d
            step = make([self.slots[b] for b in buffers], self._splits(buffers))
            self.running.append((step, buffers, self.slots))
            steps.append(step)
        buffers = PAIR_SCHEDULE.get(kernel)
        if buffers:
            step = _step_pair_exchange([self.grads[b] for b in buffers], self._splits(buffers))
            self.running.append((step, buffers, self.theirs))
            steps.append(step)
        buffers = REDUCE_SCHEDULE.get(kernel)
        if buffers:
            for b in buffers:
                self.partials[b] = _rs_pair_add("reduce_pair_add_" + b, self.place, self.grads[b], self.theirs[b],
                                                SPLIT.get(b, "rows"))
            step = _step_chip_exchange([self.partials[b] for b in buffers])
            self.running.append((step, buffers, self.peers))
            steps.append(step)
        buffers = SWAP_SCHEDULE.get(kernel)
        if buffers:
            step = self._swap_step(buffers)
            self.running.append((step, buffers, self.reduced))
            steps.append(step)
        return steps

    def _swap_step(self, buffers):
        halves = [_rs_chip_add("reduce_chip_add_" + b, self.place, self.partials[b], self.peers[b], SPLIT.get(b, "rows"))
                  for b in buffers]
        return _step_pair_gather(halves, self._splits(buffers))

    def finish(self):
        self._collect()
        rest = [b for b, _, _ in BUFFERS if b not in self.reduced]
        step = self._swap_step(rest)
        _run_steps("reduce_pair_gather", [step])
        self.reduced.update(zip(rest, step.results))
        return self.reduced


def kernel(x, meta_tokens, a_norm_pre, a_w_in, a_conv_w, a_conv_b, a_dt_bias, a_a_log, a_d_skip, a_gate_norm, a_w_out, a_norm_post, kv_norm, w_kv, b_norm_pre, b_w_q, b_sinks, b_w_o, b_norm_post, f_norm_pre, f_w_up, f_conv_w, f_conv_b, f_w_down, f_norm_post, loss_target, m_meta_tokens, m_a_norm_pre, m_a_w_in, m_a_conv_w, m_a_conv_b, m_a_dt_bias, m_a_a_log, m_a_d_skip, m_a_gate_norm, m_a_w_out, m_a_norm_post, m_kv_norm, m_w_kv, m_b_norm_pre, m_b_w_q, m_b_sinks, m_b_w_o, m_b_norm_post, m_f_norm_pre, m_f_w_up, m_f_conv_w, m_f_conv_b, m_f_w_down, m_f_norm_post, v_meta_tokens, v_a_norm_pre, v_a_w_in, v_a_conv_w, v_a_conv_b, v_a_dt_bias, v_a_a_log, v_a_d_skip, v_a_gate_norm, v_a_w_out, v_a_norm_post, v_kv_norm, v_w_kv, v_b_norm_pre, v_b_w_q, v_b_sinks, v_b_w_o, v_b_norm_post, v_f_norm_pre, v_f_w_up, v_f_conv_w, v_f_conv_b, v_f_w_down, v_f_norm_post):
    given = dict(locals())
    w = {n: given[n] for n in WEIGHTS}
    mom = {n: given["m_" + n] for n in WEIGHTS}
    var = {n: given["v_" + n] for n in WEIGHTS}
    chip = 2 * lax.axis_index("x") + lax.axis_index("y")
    core = lax.axis_index("c")
    place = jnp.stack([chip, core]).astype(jnp.int32)

    small_all = _allgather_small("gather_small", _pack([w[n] for n in SMALL_SHARDED]))
    small_parts = _unpack(small_all, SMALL_SHARDED, _shard_shape)
    slots = {b: _cast_into_slot("cast_" + b, place, _local_shard(w, wn, layer), layer) for b, wn, layer in BUFFERS}
    pipeline = _Pipeline(place, slots)
    pipeline.gather_now("gather_first", ["a_w_in"])
    p = {}
    for n in SMALL:
        p[n] = _join_chips(n, small_parts[n]) if n in SMALL_SHARDED else w[n]
    p["a_conv_w"] = p["a_conv_w"][0]
    p["kv_norm"] = p["kv_norm"].reshape(1, D_MODEL)

    loss_local, grad_x, g = _local_step(x[0], loss_target[0], p, pipeline)

    small_sum = _allreduce_small("reduce_small", _pack([g[n].reshape(FULL_SHAPE[n]) for n in SMALL]
                                                       + [loss_local.reshape(1, 1)]))
    small_red = _unpack(small_sum, SMALL + ["loss"], lambda n: (1, 1) if n == "loss" else FULL_SHAPE[n])
    loss = small_red["loss"][0, 0]
    grads = {}
    for n in SMALL:
        if SHARD_AXIS[n] is None:
            grads[n] = small_red[n]
        else:
            grads[n] = lax.dynamic_index_in_dim(_split_chips(n, small_red[n]), chip, 0, keepdims=False)

    shard_sum = pipeline.finish()

    delta, new_m, new_v = {}, {}, {}
    for n in BIG:
        shape = _shard_shape(n)
        if n in TRANSPOSED:
            g2d = shard_sum[n]
            w2d, m2d, v2d = (arrays[n][0].T for arrays in (w, mom, var))
            back = lambda a: a.T.reshape(shape)
        else:
            g2d = (jnp.concatenate([shard_sum[n + "0"], shard_sum[n + "1"]], axis=0) if n in ("f_w_up", "f_w_down")
                   else shard_sum[n])
            w2d, m2d, v2d = (_as2d(arrays[n]) for arrays in (w, mom, var))
            back = lambda a: a.reshape(shape)
        d, m2, v2 = _adamw("adamw_" + n, w2d, g2d, m2d, v2d)
        grads[n], delta[n], new_m[n], new_v[n] = back(g2d), back(d), back(m2), back(v2)
    packed = [_pack([src[n].reshape(_shard_shape(n)) for n in SMALL]) for src in (w, grads, mom, var)]
    outs = _adamw("adamw_small", *packed)
    for dst, flat in zip((delta, new_m, new_v), outs):
        dst.update(_unpack(flat, SMALL, _shard_shape))

    return (loss, grad_x[None], *[grads[n].reshape(_shard_shape(n)) for n in WEIGHTS],
            *[delta[n] for n in WEIGHTS], *[new_m[n] for n in WEIGHTS], *[new_v[n] for n in WEIGHTS])
```

```python
import functools
import math

import jax
import jax.numpy as jnp
from jax import lax
from jax.experimental import pallas as pl
from jax.experimental.pallas import tpu as pltpu

F32, BF16 = jnp.float32, jnp.bfloat16
MESH = pl.DeviceIdType.MESH

D_MODEL = 1024
N_META = 16
CHUNK = 128
PAD_ROWS = CHUNK - N_META
D_INNER = 2048
D_STATE = 128
N_GROUPS = 4
HEADS_PER_GROUP = 8
SSM_HEADS = 32
HEAD_DIM = 64
D_BC = N_GROUPS * D_STATE
D_XBC = D_INNER + 2 * D_BC
D_MAIN = D_INNER + D_XBC
D_IN_PROJ = D_MAIN + SSM_HEADS
GROUP_W = HEADS_PER_GROUP * HEAD_DIM
SSM_CONV = 4
D_FF = 2816
FFN_CONV = 3
N_Q_HEADS = 16
N_KV_HEADS = 4
D_KV = 256
ATTN_SCALE = 1.0 / math.sqrt(HEAD_DIM)
RMS_EPS = 1e-6
NEG_INF = -1e30
LANES = 128
VMEM_LIMIT = 48 * 1024 * 1024

ADAM_LR, ADAM_B1, ADAM_B2, ADAM_EPS, ADAM_WD, ADAM_STEP = 0.001, 0.9, 0.999, 1e-08, 0.01, 10

N_CHIPS = 4
N_DEV = 8


def _cparams(sem=None):
    return pltpu.CompilerParams(dimension_semantics=sem, vmem_limit_bytes=VMEM_LIMIT)


def _tile(n, cands=(512, 256, 128)):
    for t in cands:
        if n % t == 0:
            return t
    return n


def _row_tile(rows, width):
    for t in (544, 272):
        if rows % t == 0 and t * width * 4 <= (3 << 20):
            return t
    return 128


def _rows_mask(i, tm):
    rows = i * tm + lax.broadcasted_iota(jnp.int32, (tm, 1), 0)
    return rows >= PAD_ROWS


def _dot(a, b):
    return jnp.dot(a, b, preferred_element_type=F32)


def _dot_nt(a, b):
    return lax.dot_general(a, b, (((1,), (1,)), ((), ())), preferred_element_type=F32)


def _dot_tn(a, b):
    return lax.dot_general(a, b, (((0,), (0,)), ((), ())), preferred_element_type=F32)


def _sigmoid(x):
    return 1.0 / (1.0 + jnp.exp(-x))


def _place():
    return lax.axis_index("x"), lax.axis_index("y"), lax.axis_index("c")


def _other_chips(x, y):
    return [(1 - x, y), (x, 1 - y), (1 - x, 1 - y)]


class _Step:
    def __init__(self, ins, outs, aliases, n_sems, start, finish):
        self.ins, self.outs, self.aliases, self.n_sems = list(ins), list(outs), dict(aliases), n_sems
        self.start, self.finish = start, finish
        self.results = None


def _like(a):
    return jax.ShapeDtypeStruct(a.shape, a.dtype)


def _remote(src, dst, send_sems, recv_sems, k, device):
    return pltpu.make_async_remote_copy(src, dst, send_sems.at[k], recv_sems.at[k], device_id=device, device_id_type=MESH)


def _half(ref, split, which, lead=()):
    if split == "rows":
        hr = ref.shape[-2] // 2
        return ref.at[lead + (pl.ds(which * hr, hr),)]
    hc = ref.shape[-1] // 2
    return ref.at[lead + (slice(None), pl.ds(which * hc, hc))]


def _splits(bufs, splits):
    return list(splits) if splits is not None else ["rows"] * len(bufs)


ALL_PEERS = (0, 1, 2)
NEAR_PEERS = (0, 1)
FAR_PEERS = (2,)


def _step_gather_ici(bufs, splits=None, peers=ALL_PEERS):
    splits = _splits(bufs, splits)

    def copies(outs, send_sems, recv_sems, received):
        x, y, c = _place()
        me = 2 * x + y
        for k, o in enumerate(outs):
            for j, (cx, cy) in enumerate(_other_chips(x, y)):
                if j in peers:
                    part = _half(o, splits[k], c, (2 * cx + cy if received else me,))
                    yield _remote(part, part, send_sems, recv_sems, 3 * k + j, (cx, cy, c))

    def start(ins, outs, send_sems, recv_sems):
        for cp in copies(outs, send_sems, recv_sems, False):
            cp.start()

    def finish(ins, outs, send_sems, recv_sems):
        for cp in copies(outs, send_sems, recv_sems, True):
            cp.wait_recv()
        for cp in copies(outs, send_sems, recv_sems, False):
            cp.wait_send()

    return _Step(bufs, [_like(b) for b in bufs], {k: k for k in range(len(bufs))}, 3 * len(bufs), start, finish)


def _step_gather_d2d(bufs, splits=None):
    splits = _splits(bufs, splits)

    def copies(outs, send_sems, recv_sems, received):
        x, y, c = _place()
        for k, o in enumerate(outs):
            for j, (cx, cy) in enumerate(_other_chips(x, y)):
                part = _half(o, splits[k], 1 - c if received else c, (2 * cx + cy,))
                yield _remote(part, part, send_sems, recv_sems, 3 * k + j, (x, y, 1 - c))

    def start(ins, outs, send_sems, recv_sems):
        for cp in copies(outs, send_sems, recv_sems, False):
            cp.start()

    def finish(ins, outs, send_sems, recv_sems):
        for cp in copies(outs, send_sems, recv_sems, True):
            cp.wait_recv()
        for cp in copies(outs, send_sems, recv_sems, False):
            cp.wait_send()

    return _Step(bufs, [_like(b) for b in bufs], {k: k for k in range(len(bufs))}, 3 * len(bufs), start, finish)


def _step_gather_full(bufs, splits=None):
    n = len(bufs)
    splits = _splits(bufs, splits)

    def ici(outs, send_sems, recv_sems, received):
        x, y, c = _place()
        me = 2 * x + y
        for k, o in enumerate(outs):
            for j, (cx, cy) in enumerate(_other_chips(x, y)):
                part = _half(o, splits[k], c, (2 * cx + cy if received else me,))
                yield _remote(part, part, send_sems, recv_sems, 3 * k + j, (cx, cy, c))

    def d2d(outs, send_sems, recv_sems, received):
        x, y, c = _place()
        for k, o in enumerate(outs):
            for j, (cx, cy) in enumerate(_other_chips(x, y)):
                part = _half(o, splits[k], 1 - c if received else c, (2 * cx + cy,))
                yield _remote(part, part, send_sems, recv_sems, 3 * n + 3 * k + j, (x, y, 1 - c))

    def start(ins, outs, send_sems, recv_sems):
        for cp in ici(outs, send_sems, recv_sems, False):
            cp.start()

    def finish(ins, outs, send_sems, recv_sems):
        for arrived, onward in zip(ici(outs, send_sems, recv_sems, True), d2d(outs, send_sems, recv_sems, False)):
            arrived.wait_recv()
            onward.start()
        for cp in d2d(outs, send_sems, recv_sems, True):
            cp.wait_recv()
        for cp in ici(outs, send_sems, recv_sems, False):
            cp.wait_send()
        for cp in d2d(outs, send_sems, recv_sems, False):
            cp.wait_send()

    return _Step(bufs, [_like(b) for b in bufs], {k: k for k in range(n)}, 6 * n, start, finish)


def _half_shape(shape, split):
    return shape[:-2] + ((shape[-2] // 2, shape[-1]) if split == "rows" else (shape[-2], shape[-1] // 2))


def _step_pair_exchange(grads, splits=None):
    splits = _splits(grads, splits)

    def copies(ins, outs, send_sems, recv_sems):
        x, y, c = _place()
        for k, (g, o) in enumerate(zip(ins, outs)):
            yield _remote(_half(g, splits[k], 1 - c, (slice(None),)), o, send_sems, recv_sems, k, (x, y, 1 - c))

    def start(ins, outs, send_sems, recv_sems):
        for cp in copies(ins, outs, send_sems, recv_sems):
            cp.start()

    def finish(ins, outs, send_sems, recv_sems):
        for cp in copies(ins, outs, send_sems, recv_sems):
            cp.wait()

    outs = [jax.ShapeDtypeStruct(_half_shape(g.shape, s), g.dtype) for g, s in zip(grads, splits)]
    return _Step(grads, outs, {}, len(grads), start, finish)


def _step_chip_exchange(partials, peers=ALL_PEERS, into=None):
    n = len(partials)

    def copies(ins, outs, send_sems, recv_sems):
        x, y, c = _place()
        for k, (q, o) in enumerate(zip(ins[:n], outs)):
            for j, (cx, cy) in enumerate(_other_chips(x, y)):
                if j in peers:
                    yield _remote(q.at[2 * cx + cy], o.at[j], send_sems, recv_sems, 3 * k + j, (cx, cy, c))

    def start(ins, outs, send_sems, recv_sems):
        for cp in copies(ins, outs, send_sems, recv_sems):
            cp.start()

    def finish(ins, outs, send_sems, recv_sems):
        for cp in copies(ins, outs, send_sems, recv_sems):
            cp.wait()

    outs = [jax.ShapeDtypeStruct((3,) + q.shape[1:], q.dtype) for q in partials]
    if into is None:
        return _Step(partials, outs, {}, 3 * n, start, finish)
    return _Step(list(partials) + list(into), outs, {n + k: k for k in range(n)}, 3 * n, start, finish)


def _step_pair_gather(shards, splits=None):
    splits = _splits(shards, splits)

    def copies(outs, send_sems, recv_sems, received):
        x, y, c = _place()
        for k, o in enumerate(outs):
            part = _half(o, splits[k], 1 - c if received else c)
            yield _remote(part, part, send_sems, recv_sems, k, (x, y, 1 - c))

    def start(ins, outs, send_sems, recv_sems):
        for cp in copies(outs, send_sems, recv_sems, False):
            cp.start()

    def finish(ins, outs, send_sems, recv_sems):
        for cp in copies(outs, send_sems, recv_sems, True):
            cp.wait_recv()
        for cp in copies(outs, send_sems, recv_sems, False):
            cp.wait_send()

    return _Step(shards, [_like(s) for s in shards], {k: k for k in range(len(shards))}, len(shards), start, finish)


def _call(body, *, name, out_shape, grid, in_specs, out_specs, operands, scratch_shapes=(), semantics=None, steps=()):
    single = not isinstance(out_shape, (tuple, list))
    out_shapes = [out_shape] if single else list(out_shape)
    out_spec_list = [out_specs] if single else list(out_specs)
    steps = list(steps)
    if not steps:
        res = pl.pallas_call(body, name=name, out_shape=out_shapes, grid=grid, in_specs=list(in_specs),
                             out_specs=out_spec_list, scratch_shapes=list(scratch_shapes),
                             compiler_params=_cparams(semantics))(*operands)
        return res[0] if single else res
    n_in, n_out, n_scr = len(operands), len(out_shapes), len(scratch_shapes)
    x_in = [a for s in steps for a in s.ins]
    x_out = [o for s in steps for o in s.outs]
    aliases, in_off, out_off = {}, 0, 0
    for s in steps:
        for i, o in s.aliases.items():
            aliases[n_in + in_off + i] = n_out + out_off + o
        in_off += len(s.ins)
        out_off += len(s.outs)
    sems = []
    for s in steps:
        sems += [pltpu.SemaphoreType.DMA((s.n_sems,)), pltpu.SemaphoreType.DMA((s.n_sems,))]
    any_spec = pl.BlockSpec(memory_space=pl.ANY)

    def carried(*refs):
        pos = 0
        ins = refs[pos:pos + n_in]; pos += n_in
        xi = refs[pos:pos + len(x_in)]; pos += len(x_in)
        outs = refs[pos:pos + n_out]; pos += n_out
        xo = refs[pos:pos + len(x_out)]; pos += len(x_out)
        scr = refs[pos:pos + n_scr]; pos += n_scr
        sem_refs = refs[pos:]

        def each(action):
            i0 = o0 = 0
            for k, s in enumerate(steps):
                getattr(s, action)(xi[i0:i0 + len(s.ins)], xo[o0:o0 + len(s.outs)], sem_refs[2 * k], sem_refs[2 * k + 1])
                i0 += len(s.ins)
                o0 += len(s.outs)

        if grid:
            first = functools.reduce(jnp.logical_and, [pl.program_id(d) == 0 for d in range(len(grid))])
            last = functools.reduce(jnp.logical_and, [pl.program_id(d) == grid[d] - 1 for d in range(len(grid))])
            pl.when(first)(lambda: each("start"))
            body(*ins, *outs, *scr)
            pl.when(last)(lambda: each("finish"))
        else:
            each("start")
            body(*ins, *outs, *scr)
            each("finish")

    res = pl.pallas_call(
        carried, name=name, out_shape=out_shapes + x_out, grid=grid,
        in_specs=list(in_specs) + [any_spec] * len(x_in), out_specs=out_spec_list + [any_spec] * len(x_out),
        scratch_shapes=list(scratch_shapes) + sems, input_output_aliases=aliases,
        compiler_params=_cparams(None if semantics is None else ("arbitrary",) * len(grid)),
    )(*operands, *x_in)
    o0 = n_out
    for s in steps:
        s.results = list(res[o0:o0 + len(s.outs)])
        o0 += len(s.outs)
    return res[0] if single else tuple(res[:n_out])


def _run_steps(name, steps):
    _call(lambda: None, name=name, out_shape=[], grid=(), in_specs=[], out_specs=[], operands=[], steps=steps)
    return [s.results for s in steps]


def _mm(name, a, b, mode, out_dtype=F32, acc=None, b_colblock=0, k_rows=None, out_rows=None, steps=()):
    resident_bytes = 8 << 20
    if mode == "nn":
        m, k = a.shape
        n = b.shape[1]
        tm = m
        while tm * k * 2 > resident_bytes and tm % 32 == 0:
            tm //= 2
        tn = _tile(n)
        grid = (m // tm, n // tn)
        in_specs = [pl.BlockSpec((tm, k), lambda i, j: (i, 0)), pl.BlockSpec((k, tn), lambda i, j: (0, j))]
        out_shape, out_block = (m, n), (tm, tn)
    elif mode == "nt":
        m, n = a.shape
        k = k_rows or b.shape[0]
        tm = m
        while tm * n * 2 > resident_bytes and tm % 32 == 0:
            tm //= 2
        tk = _tile(k)
        grid = (m // tm, k // tk)
        in_specs = [pl.BlockSpec((tm, n), lambda i, j: (i, 0)), pl.BlockSpec((tk, n), lambda i, j: (j, b_colblock))]
        out_shape, out_block = (m, k), (tm, tk)
    else:
        m, k = a.shape
        n = b.shape[1]
        tk, tn = _tile(k), (n if m * n * 2 <= resident_bytes else _tile(n))
        grid = (k // tk, n // tn)
        in_specs = [pl.BlockSpec((m, tk), lambda i, j: (0, i)), pl.BlockSpec((m, tn), lambda i, j: (0, j))]
        out_shape, out_block = (out_rows or k, n), (tk, tn)
    out_spec = pl.BlockSpec(out_block, lambda i, j: (i, j))
    has_acc = acc is not None

    def body(*refs):
        a_ref, b_ref = refs[0], refs[1]
        o_ref = refs[-1]
        av, bv = a_ref[...], b_ref[...]
        if mode == "nn":
            r = _dot(av, bv)
        elif mode == "nt":
            r = _dot_nt(av, bv)
        else:
            r = _dot_tn(av, bv)
        if has_acc:
            r = r + refs[2][...]
        o_ref[...] = r.astype(o_ref.dtype)

    operands = [a, b]
    if has_acc:
        in_specs = in_specs + [out_spec]
        operands.append(acc)
    return _call(body, name=name, out_shape=jax.ShapeDtypeStruct(out_shape, out_dtype), grid=grid, in_specs=in_specs,
                 out_specs=out_spec, operands=operands, semantics=("parallel", "parallel"), steps=steps)


def _tn_rows_into(name, a, b, into, row0, nrows):
    m, k = a.shape
    n = b.shape[1]

    def body(a_ref, b_ref, into_ref, o_ref):
        o_ref[...] = _dot_tn(a_ref[...], b_ref[...])[0:nrows].astype(o_ref.dtype)

    return pl.pallas_call(
        body, name=name, out_shape=jax.ShapeDtypeStruct(into.shape, into.dtype), grid=(1,),
        in_specs=[pl.BlockSpec((m, k), lambda i: (0, 0)), pl.BlockSpec((m, n), lambda i: (0, 0)),
                  pl.BlockSpec(memory_space=pl.ANY)],
        out_specs=pl.BlockSpec((nrows, n), lambda i: (row0 // nrows, 0)),
        input_output_aliases={2: 0}, compiler_params=_cparams(("arbitrary",)),
    )(a, b, into)


def _fit_rows(m, row_bytes, budget=8 << 20):
    tm = m
    while tm * row_bytes > budget and tm % 32 == 0:
        tm //= 2
    return tm


def _mm_nn_bychip(name, a, bc, steps=()):
    m, k = a.shape
    n = bc.shape[2]
    tm = min(_fit_rows(m, k * 2), _fit_rows(m, n * 4))

    def body(a_ref, b_ref, o_ref):
        o_ref[...] = _dot(a_ref[...], b_ref[...])

    return _call(
        body, name=name, out_shape=jax.ShapeDtypeStruct((m, N_CHIPS * n), F32), grid=(m // tm, N_CHIPS),
        in_specs=[pl.BlockSpec((tm, k), lambda i, c: (i, 0)), pl.BlockSpec((None, k, n), lambda i, c: (c, 0, 0))],
        out_specs=pl.BlockSpec((tm, n), lambda i, c: (i, c)), operands=[a, bc],
        semantics=("parallel", "parallel"), steps=steps)


def _mm_nt_bychip(name, a, bc, chip0, acc=None):
    m = a.shape[0]
    _, k, n = bc.shape
    nch = a.shape[1] // n
    tm, tk = _fit_rows(m, n * 2), _tile(k)
    has_acc = acc is not None

    def body(*refs):
        a_ref, b_ref, o_ref = refs[0], refs[1], refs[-1]

        @pl.when(pl.program_id(2) == 0)
        def _():
            o_ref[...] = refs[2][...] if has_acc else jnp.zeros_like(o_ref)

        o_ref[...] += _dot_nt(a_ref[...], b_ref[...])

    out_spec = pl.BlockSpec((tm, tk), lambda i, j, c: (i, j))
    in_specs = [pl.BlockSpec((tm, n), lambda i, j, c: (i, c)),
                pl.BlockSpec((None, tk, n), lambda i, j, c: (chip0 + c, j, 0))]
    operands = [a, bc]
    if has_acc:
        in_specs.append(out_spec)
        operands.append(acc)
    return pl.pallas_call(
        body, name=name, out_shape=jax.ShapeDtypeStruct((m, k), F32), grid=(m // tm, k // tk, nch),
        in_specs=in_specs, out_specs=out_spec, compiler_params=_cparams(("parallel", "parallel", "arbitrary")),
    )(*operands)


def _mm_tn_bychip(name, a, dy, n, chip0, into=None):
    m, k = a.shape
    nch = dy.shape[1] // n
    tk = _tile(k)

    def body(*refs):
        a_ref, d_ref, o_ref = refs[0], refs[1], refs[-1]
        o_ref[...] = _dot_tn(a_ref[...], d_ref[...]).astype(BF16)

    in_specs = [pl.BlockSpec((m, tk), lambda i, c: (0, i)), pl.BlockSpec((m, n), lambda i, c: (0, c))]
    operands = [a, dy]
    aliases = {}
    if into is not None:
        in_specs.append(pl.BlockSpec(memory_space=pl.ANY))
        operands.append(into)
        aliases = {2: 0}
    return pl.pallas_call(
        body, name=name, out_shape=jax.ShapeDtypeStruct((N_CHIPS, k, n), BF16), grid=(k // tk, nch),
        in_specs=in_specs, out_specs=pl.BlockSpec((None, tk, n), lambda i, c: (chip0 + c, i, 0)),
        input_output_aliases=aliases, compiler_params=_cparams(("parallel", "parallel")),
    )(*operands)


def _rms_fwd(name, h, w):
    rows, width = h.shape
    tm = _row_tile(rows, width)

    def body(h_ref, w_ref, o_ref):
        x = h_ref[...]
        r = lax.rsqrt(jnp.mean(x * x, axis=-1, keepdims=True) + RMS_EPS)
        o_ref[...] = (x * r * w_ref[...]).astype(BF16)

    return pl.pallas_call(
        body, name=name, out_shape=jax.ShapeDtypeStruct((rows, width), BF16), grid=(rows // tm,),
        in_specs=[pl.BlockSpec((tm, width), lambda i: (i, 0)), pl.BlockSpec((1, width), lambda i: (0, 0))],
        out_specs=pl.BlockSpec((tm, width), lambda i: (i, 0)), compiler_params=_cparams(("parallel",)),
    )(h, w)


def _resid_norm_fwd(name, h, pre, w, next_norms=()):
    rows, width = h.shape
    tm = _row_tile(rows, width)
    n_next = len(next_norms)

    def body(*refs):
        h_ref, p_ref, w_ref = refs[:3]
        v_refs = refs[3:3 + n_next]
        o_ref = refs[3 + n_next]
        n_refs = refs[4 + n_next:]
        p = p_ref[...]
        r = lax.rsqrt(jnp.mean(p * p, axis=-1, keepdims=True) + RMS_EPS)
        x = h_ref[...] + jnp.where(_rows_mask(pl.program_id(0), tm), p * r * w_ref[...], 0.0)
        o_ref[...] = x
        if n_next:
            rx = lax.rsqrt(jnp.mean(x * x, axis=-1, keepdims=True) + RMS_EPS)
            for v_ref, n_ref in zip(v_refs, n_refs):
                n_ref[...] = (x * rx * v_ref[...]).astype(BF16)

    row_spec = pl.BlockSpec((tm, width), lambda i: (i, 0))
    vec_spec = pl.BlockSpec((1, width), lambda i: (0, 0))
    outs = pl.pallas_call(
        body, name=name,
        out_shape=[jax.ShapeDtypeStruct((rows, width), F32)] + [jax.ShapeDtypeStruct((rows, width), BF16)] * n_next,
        grid=(rows // tm,), in_specs=[row_spec, row_spec, vec_spec] + [vec_spec] * n_next,
        out_specs=[row_spec] * (1 + n_next), compiler_params=_cparams(("parallel",)),
    )(h, pre, w, *next_norms)
    return outs[0], list(outs[1:])


def _resid_norm_loss(name, h, pre, w, target):
    rows, width = h.shape

    def body(h_ref, p_ref, w_ref, t_ref, dh_ref, loss_ref, dp_ref, dw_ref):
        i = pl.program_id(0)
        p = p_ref[...]
        r = lax.rsqrt(jnp.mean(p * p, axis=-1, keepdims=True) + RMS_EPS)
        x = h_ref[...] + p * r * w_ref[...]
        real = (i + jnp.zeros((CHUNK, 1), jnp.int32)) >= 1
        diff = jnp.where(real, x - t_ref[...], 0.0)
        dh = diff * (1.0 / D_MODEL)
        dh_ref[...] = dh
        dp, dw_rows = _rms_bwd(dh, p, w_ref[...])
        dp_ref[...] = dp.astype(BF16)

        @pl.when(i == 0)
        def _():
            loss_ref[...] = jnp.zeros_like(loss_ref)
            dw_ref[...] = jnp.zeros_like(dw_ref)

        loss_ref[...] += jnp.sum(diff * diff) * (0.5 / D_MODEL)
        dw_ref[...] += jnp.sum(dw_rows, axis=0, keepdims=True)

    blk = pl.BlockSpec((CHUNK, width), lambda i: (i, 0))
    vec_spec = pl.BlockSpec((1, width), lambda i: (0, 0))
    return pl.pallas_call(
        body, name=name,
        out_shape=(jax.ShapeDtypeStruct((rows, width), F32), jax.ShapeDtypeStruct((1, LANES), F32),
                   jax.ShapeDtypeStruct((rows, width), BF16), jax.ShapeDtypeStruct((1, width), F32)),
        grid=(rows // CHUNK,),
        in_specs=[blk, blk, vec_spec, pl.BlockSpec((CHUNK, width), lambda i: (jnp.maximum(i - 1, 0), 0))],
        out_specs=(blk, pl.BlockSpec((1, LANES), lambda i: (0, 0)), blk, vec_spec),
        compiler_params=_cparams(("arbitrary",)),
    )(h, pre, w, target)


def _rms_bwd(dy, x, w):
    r = lax.rsqrt(jnp.mean(x * x, axis=-1, keepdims=True) + RMS_EPS)
    xhat = x * r
    dxhat = dy * w
    return r * (dxhat - xhat * jnp.mean(dxhat * xhat, axis=-1, keepdims=True)), dy * xhat


def _norm_bwd_add(name, dh, dhn, h, w, then=None, steps=()):
    rows, width = dh.shape
    tm = _row_tile(rows, width)
    fused = then is not None

    def body(*refs):
        dh_ref, dhn_ref, h_ref, w_ref = refs[:4]
        o_ref, dw_ref = refs[6:8] if fused else refs[4:6]
        i = pl.program_id(0)
        valid = _rows_mask(i, tm)
        dx, dw_rows = _rms_bwd(dhn_ref[...], h_ref[...], w_ref[...])
        dh_new = dh_ref[...] + jnp.where(valid, dx, 0.0)
        o_ref[...] = dh_new

        @pl.when(i == 0)
        def _():
            dw_ref[...] = jnp.zeros_like(dw_ref)

        dw_ref[...] += jnp.sum(dw_rows, axis=0, keepdims=True)
        if fused:
            p_ref, wp_ref, dp_ref, dwp_ref = refs[4], refs[5], refs[8], refs[9]
            dp, dwp_rows = _rms_bwd(jnp.where(valid, dh_new, 0.0), p_ref[...], wp_ref[...])
            dp_ref[...] = dp.astype(BF16)

            @pl.when(i == 0)
            def _():
                dwp_ref[...] = jnp.zeros_like(dwp_ref)

            dwp_ref[...] += jnp.sum(dwp_rows, axis=0, keepdims=True)

    row_spec = pl.BlockSpec((tm, width), lambda i: (i, 0))
    vec_spec = pl.BlockSpec((1, width), lambda i: (0, 0))
    row_f32, vec_f32 = jax.ShapeDtypeStruct((rows, width), F32), jax.ShapeDtypeStruct((1, width), F32)
    in_specs, operands = [row_spec, row_spec, row_spec, vec_spec], [dh, dhn, h, w]
    out_shape, out_specs = [row_f32, vec_f32], [row_spec, vec_spec]
    if fused:
        in_specs += [row_spec, vec_spec]
        operands += list(then)
        out_shape += [jax.ShapeDtypeStruct((rows, width), BF16), vec_f32]
        out_specs += [row_spec, vec_spec]
    return _call(body, name=name, out_shape=out_shape, grid=(rows // tm,), in_specs=in_specs, out_specs=out_specs,
                 operands=operands, semantics=("arbitrary",), steps=steps)


def _shift_down(x, s, rows):
    return pltpu.roll(x, s, 0) if s else x


def _shift_up(x, s, rows):
    return pltpu.roll(x, rows - s, 0) if s else x


def _conv4_fwd(name, zx, cw, cb, steps=()):
    rows = zx.shape[0]
    off = D_INNER // LANES

    def body(x_ref, w_ref, b_ref, o_ref):
        x = x_ref[...]
        acc = b_ref[...] + w_ref[pl.ds(SSM_CONV - 1, 1), :] * x
        for s in range(1, SSM_CONV):
            acc = acc + w_ref[pl.ds(SSM_CONV - 1 - s, 1), :] * _shift_down(x, s, rows)
        valid = lax.broadcasted_iota(jnp.int32, (rows, 1), 0) >= PAD_ROWS
        o_ref[...] = jnp.where(valid, acc * _sigmoid(acc), 0.0)

    return _call(
        body, name=name, out_shape=jax.ShapeDtypeStruct((rows, D_XBC), F32), grid=(D_XBC // LANES,),
        in_specs=[pl.BlockSpec((rows, LANES), lambda j: (0, j + off)),
                  pl.BlockSpec((SSM_CONV, LANES), lambda j: (0, j)),
                  pl.BlockSpec((1, LANES), lambda j: (0, j))],
        out_specs=pl.BlockSpec((rows, LANES), lambda j: (0, j)), operands=[zx, cw, cb],
        semantics=("parallel",), steps=steps)


def _conv4_bwd(name, zx, dout, cw, cb, col0):
    rows, width = dout.shape
    zoff = (D_INNER + col0) // LANES
    woff = col0 // LANES

    def body(x_ref, d_ref, w_ref, b_ref, dx_ref, dw_ref, db_ref):
        x = x_ref[...]
        shifted = [_shift_down(x, s, rows) for s in range(SSM_CONV)]
        acc = b_ref[...]
        for s in range(SSM_CONV):
            acc = acc + w_ref[pl.ds(SSM_CONV - 1 - s, 1), :] * shifted[s]
        sig = _sigmoid(acc)
        valid = lax.broadcasted_iota(jnp.int32, (rows, 1), 0) >= PAD_ROWS
        dpre = jnp.where(valid, d_ref[...] * sig * (1.0 + acc * (1.0 - sig)), 0.0)
        dx = w_ref[pl.ds(SSM_CONV - 1, 1), :] * dpre
        for s in range(1, SSM_CONV):
            dx = dx + w_ref[pl.ds(SSM_CONV - 1 - s, 1), :] * _shift_up(dpre, s, rows)
        dx_ref[...] = dx.astype(BF16)
        for s in range(SSM_CONV):
            dw_ref[pl.ds(SSM_CONV - 1 - s, 1), :] = jnp.sum(dpre * shifted[s], axis=0, keepdims=True)
        db_ref[...] = jnp.sum(dpre, axis=0, keepdims=True)

    return pl.pallas_call(
        body, name=name,
        out_shape=(jax.ShapeDtypeStruct((rows, width), BF16), jax.ShapeDtypeStruct((SSM_CONV, width), F32),
                   jax.ShapeDtypeStruct((1, width), F32)),
        grid=(width // LANES,),
        in_specs=[pl.BlockSpec((rows, LANES), lambda j: (0, j + zoff)),
                  pl.BlockSpec((rows, LANES), lambda j: (0, j)),
                  pl.BlockSpec((SSM_CONV, LANES), lambda j: (0, j + woff)),
                  pl.BlockSpec((1, LANES), lambda j: (0, j + woff))],
        out_specs=(pl.BlockSpec((rows, LANES), lambda j: (0, j)),
                   pl.BlockSpec((SSM_CONV, LANES), lambda j: (0, j)),
                   pl.BlockSpec((1, LANES), lambda j: (0, j))),
        compiler_params=_cparams(("parallel",)),
    )(zx, dout, cw, cb)


def _ffn_conv_fwd(name, up, cw, cb, steps=()):
    rows = up.shape[0]
    nt = D_FF // LANES

    def body(g_ref, v_ref, wg_ref, wv_ref, bg_ref, bv_ref, o_ref):
        g, v = g_ref[...], v_ref[...]
        ug, uv = bg_ref[...], bv_ref[...]
        for s in range(FFN_CONV):
            ug = ug + wg_ref[pl.ds(FFN_CONV - 1 - s, 1), :] * _shift_down(g, s, rows)
            uv = uv + wv_ref[pl.ds(FFN_CONV - 1 - s, 1), :] * _shift_down(v, s, rows)
        o_ref[...] = (ug * _sigmoid(ug) * uv).astype(BF16)

    col = lambda shift: pl.BlockSpec((rows, LANES), lambda j: (0, j + shift))
    wsp = lambda shift: pl.BlockSpec((FFN_CONV, LANES), lambda j: (0, j + shift))
    bsp = lambda shift: pl.BlockSpec((1, LANES), lambda j: (0, j + shift))
    return _call(
        body, name=name, out_shape=jax.ShapeDtypeStruct((rows, D_FF), BF16), grid=(nt,),
        in_specs=[col(0), col(nt), wsp(0), wsp(nt), bsp(0), bsp(nt)],
        out_specs=pl.BlockSpec((rows, LANES), lambda j: (0, j)), operands=[up, up, cw, cw, cb, cb],
        semantics=("parallel",), steps=steps)


def _ffn_conv_bwd(name, up, dact, cw, cb, steps=()):
    rows = up.shape[0]
    nt = D_FF // LANES

    def body(g_ref, v_ref, d_ref, wg_ref, wv_ref, bg_ref, bv_ref, dxg_ref, dxv_ref, dwg_ref, dwv_ref, dbg_ref, dbv_ref):
        g, v = g_ref[...], v_ref[...]
        gs = [_shift_down(g, s, rows) for s in range(FFN_CONV)]
        vs = [_shift_down(v, s, rows) for s in range(FFN_CONV)]
        ug, uv = bg_ref[...], bv_ref[...]
        for s in range(FFN_CONV):
            ug = ug + wg_ref[pl.ds(FFN_CONV - 1 - s, 1), :] * gs[s]
            uv = uv + wv_ref[pl.ds(FFN_CONV - 1 - s, 1), :] * vs[s]
        sig = _sigmoid(ug)
        dsig = d_ref[...] * sig
        for dpre, src, w_ref, dx_ref, dw_ref, db_ref in (
                (dsig * uv * (1.0 + ug * (1.0 - sig)), gs, wg_ref, dxg_ref, dwg_ref, dbg_ref),
                (dsig * ug, vs, wv_ref, dxv_ref, dwv_ref, dbv_ref)):
            dx = w_ref[pl.ds(FFN_CONV - 1, 1), :] * dpre
            for s in range(1, FFN_CONV):
                dx = dx + w_ref[pl.ds(FFN_CONV - 1 - s, 1), :] * _shift_up(dpre, s, rows)
            dx_ref[...] = dx.astype(BF16)
            for s in range(FFN_CONV):
                dw_ref[pl.ds(FFN_CONV - 1 - s, 1), :] = jnp.sum(dpre * src[s], axis=0, keepdims=True)
            db_ref[...] = jnp.sum(dpre, axis=0, keepdims=True)

    col = lambda shift: pl.BlockSpec((rows, LANES), lambda j: (0, j + shift))
    wsp = lambda shift: pl.BlockSpec((FFN_CONV, LANES), lambda j: (0, j + shift))
    bsp = lambda shift: pl.BlockSpec((1, LANES), lambda j: (0, j + shift))
    dx_shape = jax.ShapeDtypeStruct((rows, D_FF), BF16)
    dw_shape = jax.ShapeDtypeStruct((FFN_CONV, D_FF), F32)
    db_shape = jax.ShapeDtypeStruct((1, D_FF), F32)
    return _call(
        body, name=name, out_shape=(dx_shape, dx_shape, dw_shape, dw_shape, db_shape, db_shape), grid=(nt,),
        in_specs=[col(0), col(nt), col(0), wsp(0), wsp(nt), bsp(0), bsp(nt)],
        out_specs=(col(0), col(0), wsp(0), wsp(0), bsp(0), bsp(0)),
        operands=[up, up, dact, cw, cw, cb, cb], semantics=("parallel",), steps=steps)


def _dt_fwd(name, dtr, bias):
    rows = dtr.shape[0]
    tm = _row_tile(rows, LANES)

    def body(d_ref, b_ref, o_ref):
        v = d_ref[...] + b_ref[...]
        sp = jnp.maximum(v, 0.0) + jnp.log1p(jnp.exp(-jnp.abs(v)))
        lane = lax.broadcasted_iota(jnp.int32, (tm, LANES), 1)
        ok = _rows_mask(pl.program_id(0), tm) & (lane < SSM_HEADS)
        o_ref[...] = jnp.where(ok, sp, 0.0)

    return pl.pallas_call(
        body, name=name, out_shape=jax.ShapeDtypeStruct((rows, LANES), F32), grid=(rows // tm,),
        in_specs=[pl.BlockSpec((tm, LANES), lambda i: (i, 0)), pl.BlockSpec((1, LANES), lambda i: (0, 0))],
        out_specs=pl.BlockSpec((tm, LANES), lambda i: (i, 0)), compiler_params=_cparams(("parallel",)),
    )(dtr, bias)


def _dt_bwd(name, ddt, dtr, bias):
    rows = dtr.shape[0]
    tm = _row_tile(rows, LANES)

    def body(g_ref, d_ref, b_ref, o_ref, db_ref):
        i = pl.program_id(0)
        lane = lax.broadcasted_iota(jnp.int32, (tm, LANES), 1)
        ok = _rows_mask(i, tm) & (lane < SSM_HEADS)
        dv = jnp.where(ok, g_ref[...] * _sigmoid(d_ref[...] + b_ref[...]), 0.0)
        o_ref[...] = dv.astype(BF16)

        @pl.when(i == 0)
        def _():
            db_ref[...] = jnp.zeros_like(db_ref)

        db_ref[...] += jnp.sum(dv, axis=0, keepdims=True)

    row_spec = pl.BlockSpec((tm, LANES), lambda i: (i, 0))
    vec_spec = pl.BlockSpec((1, LANES), lambda i: (0, 0))
    return pl.pallas_call(
        body, name=name,
        out_shape=(jax.ShapeDtypeStruct((rows, LANES), BF16), jax.ShapeDtypeStruct((1, LANES), F32)),
        grid=(rows // tm,), in_specs=[row_spec, row_spec, vec_spec], out_specs=(row_spec, vec_spec),
        compiler_params=_cparams(("arbitrary",)),
    )(ddt, dtr, bias)


def _gate_fwd(name, y, zx, w, steps=()):
    rows = y.shape[0]
    tm = _row_tile(rows, D_INNER)

    def body(y_ref, z_ref, w_ref, o_ref):
        z = z_ref[...]
        g = y_ref[...] * (z * _sigmoid(z))
        r = lax.rsqrt(jnp.mean(g * g, axis=-1, keepdims=True) + RMS_EPS)
        o_ref[...] = (g * r * w_ref[...]).astype(BF16)

    row_spec = pl.BlockSpec((tm, D_INNER), lambda i: (i, 0))
    return _call(
        body, name=name, out_shape=jax.ShapeDtypeStruct((rows, D_INNER), BF16), grid=(rows // tm,),
        in_specs=[row_spec, row_spec, pl.BlockSpec((1, D_INNER), lambda i: (0, 0))],
        out_specs=row_spec, operands=[y, zx, w], semantics=("parallel",), steps=steps)


def _gate_bwd(name, dyn, y, zx, w):
    rows = y.shape[0]
    tm = _row_tile(rows, D_INNER)

    def body(d_ref, y_ref, z_ref, w_ref, dy_ref, dz_ref, dw_ref):
        i = pl.program_id(0)
        z, yv = z_ref[...], y_ref[...]
        sig = _sigmoid(z)
        sz = z * sig
        g = yv * sz
        r = lax.rsqrt(jnp.mean(g * g, axis=-1, keepdims=True) + RMS_EPS)
        ghat = g * r
        dn = d_ref[...]
        dghat = dn * w_ref[...]
        dg = r * (dghat - ghat * jnp.mean(dghat * ghat, axis=-1, keepdims=True))
        dy_ref[...] = dg * sz
        dz_ref[...] = (dg * yv * sig * (1.0 + z * (1.0 - sig))).astype(BF16)

        @pl.when(i == 0)
        def _():
            dw_ref[...] = jnp.zeros_like(dw_ref)

        dw_ref[...] += jnp.sum(dn * ghat, axis=0, keepdims=True)

    row_spec = pl.BlockSpec((tm, D_INNER), lambda i: (i, 0))
    vec_spec = pl.BlockSpec((1, D_INNER), lambda i: (0, 0))
    return pl.pallas_call(
        body, name=name,
        out_shape=(jax.ShapeDtypeStruct((rows, D_INNER), F32), jax.ShapeDtypeStruct((rows, D_INNER), BF16),
                   jax.ShapeDtypeStruct((1, D_INNER), F32)),
        grid=(rows // tm,), in_specs=[row_spec, row_spec, row_spec, vec_spec],
        out_specs=(row_spec, row_spec, vec_spec), compiler_params=_cparams(("arbitrary",)),
    )(dyn, y, zx, w)


def _split3(x):
    hi = x.astype(BF16)
    r1 = x - hi.astype(F32)
    mid = r1.astype(BF16)
    lo = (r1 - mid.astype(F32)).astype(BF16)
    return hi, mid, lo


def _dot3_data_lhs(x, sel):
    sel16 = sel.astype(F32).astype(BF16)
    hi, mid, lo = _split3(x)
    return _dot(hi, sel16) + _dot(mid, sel16) + _dot(lo, sel16)


def _dot3_data_rhs(sel, x):
    sel16 = sel.astype(F32).astype(BF16)
    hi, mid, lo = _split3(x)
    return _dot(sel16, hi) + _dot(sel16, mid) + _dot(sel16, lo)


def _causal_masks():
    r = lax.broadcasted_iota(jnp.int32, (CHUNK, CHUNK), 0)
    c = lax.broadcasted_iota(jnp.int32, (CHUNK, CHUNK), 1)
    return r >= c, r <= c


def _expand_heads_matrix(g):
    k = lax.broadcasted_iota(jnp.int32, (LANES, GROUP_W), 0)
    j = lax.broadcasted_iota(jnp.int32, (LANES, GROUP_W), 1)
    return HEADS_PER_GROUP * g + jnp.right_shift(j, 6) == k


def _reduce_heads_matrix(g):
    j = lax.broadcasted_iota(jnp.int32, (GROUP_W, LANES), 0)
    k = lax.broadcasted_iota(jnp.int32, (GROUP_W, LANES), 1)
    return HEADS_PER_GROUP * g + jnp.right_shift(j, 6) == k


def _reduce_pair_matrix(g, p):
    j = lax.broadcasted_iota(jnp.int32, (LANES, LANES), 0)
    k = lax.broadcasted_iota(jnp.int32, (LANES, LANES), 1)
    return HEADS_PER_GROUP * g + 2 * p + jnp.right_shift(j, 6) == k


def _group_cols(ref, g, width):
    return ref.at[:, pl.ds(g * width, width)]


def _ssd_prep(name, dt, a128, steps=()):
    rows = dt.shape[0]
    nc = rows // CHUNK

    def body(dt_ref, a_ref, dte_ref, acs_ref, acst_ref):
        causal, _ = _causal_masks()
        dtv = dt_ref[...]
        acs = _dot3_data_rhs(causal, dtv) * a_ref[...]
        acst_ref[...] = acs.T[0:SSM_HEADS]
        for g in range(N_GROUPS):
            expand = _expand_heads_matrix(g)
            _group_cols(dte_ref, g, GROUP_W)[...] = _dot3_data_lhs(dtv, expand)
            _group_cols(acs_ref, g, GROUP_W)[...] = _dot3_data_lhs(acs, expand)

    blk = pl.BlockSpec((CHUNK, D_INNER), lambda c: (c, 0))
    shp = jax.ShapeDtypeStruct((rows, D_INNER), F32)
    return _call(
        body, name=name, out_shape=(shp, shp, jax.ShapeDtypeStruct((nc, SSM_HEADS, CHUNK), F32)), grid=(nc,),
        in_specs=[pl.BlockSpec((CHUNK, LANES), lambda c: (c, 0)), pl.BlockSpec((1, LANES), lambda c: (0, 0))],
        out_specs=(blk, blk, pl.BlockSpec((None, SSM_HEADS, CHUNK), lambda c: (c, 0, 0))),
        operands=[dt, a128], semantics=("parallel",), steps=steps)


def _ssd_common(x_ref, b_ref, c_ref, dte_ref, acs_ref):
    x = x_ref[...]
    dt_exp = dte_ref[...]
    acs_exp = acs_ref[...]
    tot_exp = acs_ref[pl.ds(CHUNK - 1, 1), :]
    xdt = x * dt_exp
    e_exp = jnp.exp(acs_exp)
    f_exp = jnp.exp(tot_exp - acs_exp)
    return _causal_masks(), x, dt_exp, acs_exp, tot_exp, xdt, e_exp, f_exp, b_ref[...], c_ref[...]


def _pair_decay(acs_pair, acs_row, e, causal):
    lane = lax.broadcasted_iota(jnp.int32, (CHUNK, LANES), 1)
    mine = (lane < HEAD_DIM) if e == 0 else (lane >= HEAD_DIM)
    a_l = jnp.where(mine, acs_pair, pltpu.roll(acs_pair, HEAD_DIM, 1))
    seg = a_l - acs_row
    dm = jnp.where(causal[0], jnp.exp(jnp.minimum(seg, 0.0)), 0.0)
    dmt = jnp.where(causal[1], jnp.exp(jnp.minimum(-seg, 0.0)), 0.0)
    return dm, dmt


def _ssd_specs(index_of_chunk):
    wide = pl.BlockSpec((CHUNK, D_INNER), lambda c: (index_of_chunk(c), 0))
    b_spec = pl.BlockSpec((CHUNK, D_BC), lambda c: (index_of_chunk(c), D_INNER // D_BC))
    c_spec = pl.BlockSpec((CHUNK, D_BC), lambda c: (index_of_chunk(c), D_INNER // D_BC + 1))
    rows_spec = pl.BlockSpec((None, SSM_HEADS, CHUNK), lambda c: (index_of_chunk(c), 0, 0))
    state_spec = pl.BlockSpec((N_GROUPS, None, D_STATE, GROUP_W), lambda c: (0, index_of_chunk(c), 0, 0))
    return wide, b_spec, c_spec, rows_spec, state_spec


def _ssd_fwd(name, xbc, dt_exp, acs_exp, acs_rows, dskexp, steps=()):
    rows = xbc.shape[0]
    nc = rows // CHUNK

    def body(x_ref, b_ref, c_ref, dte_ref, acs_ref, acst_ref, dsk_ref, y_ref, st_ref, s_scr):
        @pl.when(pl.program_id(0) == 0)
        def _():
            s_scr[...] = jnp.zeros_like(s_scr)

        lane = lax.broadcasted_iota(jnp.int32, (CHUNK, LANES), 1)
        for g in range(N_GROUPS):
            y_g = _group_cols(y_ref, g, GROUP_W)
            causal, x, _, acs_exp_v, tot_exp, xdt, e_exp, f_exp, bm, cm = _ssd_common(
                _group_cols(x_ref, g, GROUP_W), _group_cols(b_ref, g, D_STATE), _group_cols(c_ref, g, D_STATE),
                _group_cols(dte_ref, g, GROUP_W), _group_cols(acs_ref, g, GROUP_W))
            state = s_scr[g]
            st_ref[g] = state
            cb16, bb16 = cm.astype(BF16), bm.astype(BF16)
            cb = _dot_nt(cb16, bb16)
            base = e_exp * _dot(cb16, state.astype(BF16)) + _group_cols(dsk_ref, g, GROUP_W)[...] * x
            for p in range(HEADS_PER_GROUP // 2):
                sl = slice(p * LANES, (p + 1) * LANES)
                xp = xdt[:, sl].astype(BF16)
                yd = []
                for e in range(2):
                    acs_row = acst_ref[pl.ds(g * HEADS_PER_GROUP + 2 * p + e, 1), :]
                    dm, _ = _pair_decay(acs_exp_v[:, sl], acs_row, e, causal)
                    yd.append(_dot((cb * dm).astype(BF16), xp))
                y_g[:, sl] = jnp.where(lane < HEAD_DIM, yd[0], yd[1]) + base[:, sl]
            s_scr[g] = jnp.exp(tot_exp) * state + _dot_tn(bb16, (f_exp * xdt).astype(BF16))

    wide, b_spec, c_spec, rows_spec, state_spec = _ssd_specs(lambda c: c)
    return _call(
        body, name=name,
        out_shape=(jax.ShapeDtypeStruct((rows, D_INNER), F32),
                   jax.ShapeDtypeStruct((N_GROUPS, nc, D_STATE, GROUP_W), F32)),
        grid=(nc,),
        in_specs=[wide, b_spec, c_spec, wide, wide, rows_spec, pl.BlockSpec((1, D_INNER), lambda c: (0, 0))],
        out_specs=(wide, state_spec),
        scratch_shapes=[pltpu.VMEM((N_GROUPS, D_STATE, GROUP_W), F32)],
        operands=[xbc, xbc, xbc, dt_exp, acs_exp, acs_rows, dskexp], semantics=("arbitrary",), steps=steps)


def _ssd_bwd(name, xbc, dt_exp, acs_exp, acs_rows, dt, a128, dskexp, dy, states, steps=()):
    rows = xbc.shape[0]
    nc = rows // CHUNK
    last = nc - 1

    def body(x_ref, b_ref, c_ref, dte_ref, acs_ref, acst_ref, dt_ref, a128_ref, dsk_all, dy_all, st_all,
             dx_all, db_all, dc_all, ddt_ref, dalog_ref, ddsk_ref, ds_all):
        @pl.when(pl.program_id(0) == 0)
        def _():
            ds_all[...] = jnp.zeros_like(ds_all)
            dalog_ref[...] = jnp.zeros_like(dalog_ref)
            ddsk_ref[...] = jnp.zeros_like(ddsk_ref)

        dacs = jnp.zeros((CHUNK, LANES), F32)
        ddt_x = jnp.zeros((CHUNK, LANES), F32)
        for g in range(N_GROUPS):
            dacs_g, ddt_x_g = group(
                g, _group_cols(x_ref, g, GROUP_W), _group_cols(b_ref, g, D_STATE), _group_cols(c_ref, g, D_STATE),
                _group_cols(dte_ref, g, GROUP_W), _group_cols(acs_ref, g, GROUP_W), acst_ref,
                _group_cols(dsk_all, g, GROUP_W), _group_cols(dy_all, g, GROUP_W), st_all.at[g],
                _group_cols(dx_all, g, GROUP_W), _group_cols(db_all, g, D_STATE), _group_cols(dc_all, g, D_STATE),
                ddsk_ref, ds_all.at[g])
            dacs, ddt_x = dacs + dacs_g, ddt_x + ddt_x_g
        _, causal_t = _causal_masks()
        da = _dot3_data_rhs(causal_t, dacs)
        ddt_ref[...] = da * a128_ref[...] + ddt_x
        dalog_ref[...] += jnp.sum(da * dt_ref[...], axis=0, keepdims=True) * a128_ref[...]

    def group(g, x_ref, b_ref, c_ref, dte_ref, acs_ref, acst_ref, dsk_ref, dy_ref, st_ref,
              dx_ref, db_ref, dc_ref, ddsk_ref, ds_scr):
        causal, x, dt_exp, acs_exp_v, tot_exp, xdt, e_exp, f_exp, bm, cm = _ssd_common(
            x_ref, b_ref, c_ref, dte_ref, acs_ref)
        reduce_heads = _reduce_heads_matrix(g)
        state, dstate = st_ref[...], ds_scr[...]
        dyv = dy_ref[...]
        cb16, bb16 = cm.astype(BF16), bm.astype(BF16)
        s16, ds16 = state.astype(BF16), dstate.astype(BF16)
        cb = _dot_nt(cb16, bb16)
        cbt = _dot_nt(bb16, cb16)
        cs = _dot(cb16, s16)
        bds = _dot(bb16, ds16)
        edy = e_exp * dyv
        fx = f_exp * xdt
        dxdt_base = f_exp * bds
        dc_acc = _dot_nt(edy.astype(BF16), s16)
        db_acc = _dot_nt(fx.astype(BF16), ds16)
        ds_scr[...] = jnp.exp(tot_exp) * dstate + _dot_tn(cb16, edy.astype(BF16))
        q = fx * bds
        dacs = _dot3_data_lhs(edy * cs - q, reduce_heads)
        dtot = jnp.sum(_dot3_data_lhs(q + jnp.exp(tot_exp) * dstate * state, reduce_heads), axis=0, keepdims=True)
        ddsk_ref[...] += jnp.sum(_dot3_data_lhs(dyv * x, reduce_heads), axis=0, keepdims=True)
        lane = lax.broadcasted_iota(jnp.int32, (CHUNK, LANES), 1)
        dcb = jnp.zeros((CHUNK, CHUNK), F32)
        dcbt = jnp.zeros((CHUNK, CHUNK), F32)
        ddt_x = jnp.zeros((CHUNK, LANES), F32)
        for p in range(HEADS_PER_GROUP // 2):
            sl = slice(p * LANES, (p + 1) * LANES)
            xp, dyp = xdt[:, sl], dyv[:, sl]
            xp16, dyp16 = xp.astype(BF16), dyp.astype(BF16)
            dxh = []
            for e in range(2):
                h = 2 * p + e
                mine = (lane < HEAD_DIM) if e == 0 else (lane >= HEAD_DIM)
                acs_row = acst_ref[pl.ds(g * HEADS_PER_GROUP + h, 1), :]
                dm, dmt = _pair_decay(acs_exp_v[:, sl], acs_row, e, causal)
                m, mt = cb * dm, cbt * dmt
                xh16 = jnp.where(mine, xp, 0.0).astype(BF16)
                dyh16 = jnp.where(mine, dyp, 0.0).astype(BF16)
                d_m = _dot_nt(dyh16, xp16)
                d_mt = _dot_nt(xh16, dyp16)
                dacs_h = (jnp.sum(d_m * m, axis=-1, keepdims=True)
                          - jnp.sum(d_mt * mt, axis=-1, keepdims=True))
                dacs = dacs + jnp.where(lane == HEADS_PER_GROUP * g + h, dacs_h, 0.0)
                dcb = dcb + d_m * dm
                dcbt = dcbt + d_mt * dmt
                dxh.append(_dot(mt.astype(BF16), dyp16))
            dxdt = jnp.where(lane < HEAD_DIM, dxh[0], dxh[1]) + dxdt_base[:, sl]
            dx_ref[:, sl] = dxdt * dt_exp[:, sl] + dsk_ref[:, sl] * dyp
            ddt_x = ddt_x + _dot3_data_lhs(dxdt * x[:, sl], _reduce_pair_matrix(g, p))
        dc_ref[...] = dc_acc + _dot(dcb.astype(BF16), bb16)
        db_ref[...] = db_acc + _dot(dcbt.astype(BF16), cb16)
        row = lax.broadcasted_iota(jnp.int32, (CHUNK, LANES), 0)
        return dacs + jnp.where(row == CHUNK - 1, dtot, 0.0), ddt_x

    wide, b_spec, c_spec, rows_spec, state_spec = _ssd_specs(lambda c: last - c)
    heads_spec = pl.BlockSpec((CHUNK, LANES), lambda c: (last - c, 0))
    vec_spec = pl.BlockSpec((1, LANES), lambda c: (0, 0))
    bc_out = pl.BlockSpec((CHUNK, D_BC), lambda c: (last - c, 0))
    vec_shape = jax.ShapeDtypeStruct((1, LANES), F32)
    return _call(
        body, name=name,
        out_shape=(jax.ShapeDtypeStruct((rows, D_INNER), F32), jax.ShapeDtypeStruct((rows, D_BC), F32),
                   jax.ShapeDtypeStruct((rows, D_BC), F32), jax.ShapeDtypeStruct((rows, LANES), F32),
                   vec_shape, vec_shape),
        grid=(nc,),
        in_specs=[wide, b_spec, c_spec, wide, wide, rows_spec, heads_spec, vec_spec,
                  pl.BlockSpec((1, D_INNER), lambda c: (0, 0)), wide, state_spec],
        out_specs=(wide, bc_out, bc_out, heads_spec, vec_spec, vec_spec),
        scratch_shapes=[pltpu.VMEM((N_GROUPS, D_STATE, GROUP_W), F32)],
        operands=[xbc, xbc, xbc, dt_exp, acs_exp, acs_rows, dt, a128, dskexp, dy, states],
        semantics=("arbitrary",), steps=steps)


def _attn_visible(b, heads=1):
    row = jnp.bitwise_and(lax.broadcasted_iota(jnp.int32, (heads * CHUNK, 3 * CHUNK), 0), CHUNK - 1)
    col = lax.broadcasted_iota(jnp.int32, (heads * CHUNK, 3 * CHUNK), 1)
    bb = b + jnp.zeros_like(col)
    meta = (col < CHUNK) & (bb >= 1) & (col >= PAD_ROWS)
    prev = (col >= CHUNK) & (col < 2 * CHUNK) & (bb >= 2) & ((col - CHUNK) > row)
    cur = (col >= 2 * CHUNK) & ((col - 2 * CHUNK) <= row) & ((bb >= 1) | ((col - 2 * CHUNK) >= PAD_ROWS))
    return meta | prev | cur


def _attn_visible4(b):
    return _attn_visible(b, 4)


def _stack_heads(q_ref, sink_ref, kvh, scale):
    lane = lax.broadcasted_iota(jnp.int32, (CHUNK, LANES), 1)
    parts, sinks = [], []
    for pp in range(2):
        pair = kvh * 2 + pp
        qp = q_ref[:, pair * LANES:(pair + 1) * LANES] * scale
        for e in range(2):
            mine = (lane < HEAD_DIM) if e == 0 else (lane >= HEAD_DIM)
            parts.append(jnp.where(mine, qp, 0.0).astype(BF16))
            sinks.append(jnp.full((CHUNK, 1), sink_ref[2 * pair + e], F32))
    return jnp.concatenate(parts, axis=0), jnp.concatenate(sinks, axis=0)


def _attn_operands(q_ref, k0, kp, kc, v0, vp, vc, sink_ref):
    kcat, vcat, q4, sink4 = [], [], [], []
    for kvh in range(N_KV_HEADS):
        ksl = slice(kvh * LANES, (kvh + 1) * LANES)
        kcat.append(jnp.concatenate([k0[:, ksl], kp[:, ksl], kc[:, ksl]], axis=0).astype(BF16))
        vcat.append(jnp.concatenate([v0[:, ksl], vp[:, ksl], vc[:, ksl]], axis=0).astype(BF16))
        stacked, sinks = _stack_heads(q_ref, sink_ref, kvh, ATTN_SCALE)
        q4.append(stacked)
        sink4.append(sinks)
    return kcat, vcat, q4, sink4


def _attn_probs(q4, kcat, visible, sink4):
    heads = range(N_KV_HEADS)
    s = [jnp.where(visible, _dot_nt(q4[h], kcat[h]), NEG_INF) for h in heads]
    m = [jnp.maximum(jnp.max(s[h], axis=-1, keepdims=True), sink4[h]) for h in heads]
    pe = [jnp.exp(s[h] - m[h]) for h in heads]
    pe_sink = [jnp.exp(sink4[h] - m[h]) for h in heads]
    inv = [1.0 / (jnp.sum(pe[h], axis=-1, keepdims=True) + pe_sink[h]) for h in heads]
    return [pe[h] * inv[h] for h in heads], [pe_sink[h] * inv[h] for h in heads]


def _unstack_pairs(stacked, pp):
    lane = lax.broadcasted_iota(jnp.int32, (CHUNK, LANES), 1)
    return jnp.where(lane < HEAD_DIM, stacked[(2 * pp) * CHUNK:(2 * pp + 1) * CHUNK],
                     stacked[(2 * pp + 1) * CHUNK:(2 * pp + 2) * CHUNK])


def _attn_specs(colblock):
    blk = lambda f: pl.BlockSpec((CHUNK, 2 * D_KV), f)
    return [blk(lambda b: (0, colblock)), blk(lambda b: (jnp.maximum(b - 1, 0), colblock)), blk(lambda b: (b, colblock))]


def _attn_fwd(name, q, kv2, sinks, steps=()):
    rows = q.shape[0]

    def body(q_ref, k0, kp, kc, v0, vp, vc, sink_ref, o_ref):
        visible = _attn_visible4(pl.program_id(0))
        kcat, vcat, q4, sink4 = _attn_operands(q_ref, k0, kp, kc, v0, vp, vc, sink_ref)
        pn, _ = _attn_probs(q4, kcat, visible, sink4)
        o4 = [_dot(pn[h].astype(BF16), vcat[h]) for h in range(N_KV_HEADS)]
        for kvh in range(N_KV_HEADS):
            for pp in range(2):
                qsl = slice((kvh * 2 + pp) * LANES, (kvh * 2 + pp + 1) * LANES)
                o_ref[:, qsl] = _unstack_pairs(o4[kvh], pp).astype(BF16)

    return _call(
        body, name=name, out_shape=jax.ShapeDtypeStruct((rows, D_MODEL), BF16), grid=(rows // CHUNK,),
        in_specs=[pl.BlockSpec((CHUNK, D_MODEL), lambda b: (b, 0))] + _attn_specs(0) + _attn_specs(1)
        + [pl.BlockSpec(memory_space=pltpu.SMEM)],
        out_specs=pl.BlockSpec((CHUNK, D_MODEL), lambda b: (b, 0)),
        operands=[q, kv2, kv2, kv2, kv2, kv2, kv2, sinks], semantics=("parallel",), steps=steps)


def _attn_bwd(name, q, kv2, sinks, do, steps=()):
    rows = q.shape[0]

    def body(q_ref, k0, kp, kc, v0, vp, vc, sink_ref, do_ref,
             dq_ref, dkc_ref, dkp_ref, dvc_ref, dvp_ref, dkm_ref, dvm_ref, dsink_ref):
        @pl.when(pl.program_id(0) == 0)
        def _():
            dkm_ref[...] = jnp.zeros_like(dkm_ref)
            dvm_ref[...] = jnp.zeros_like(dvm_ref)
            dsink_ref[...] = jnp.zeros_like(dsink_ref)

        visible = _attn_visible4(pl.program_id(0))
        heads = range(N_KV_HEADS)
        lane1 = lax.broadcasted_iota(jnp.int32, (1, LANES), 1)
        kcat, vcat, q4, sink4 = _attn_operands(q_ref, k0, kp, kc, v0, vp, vc, sink_ref)
        do4 = [_stack_heads(do_ref, sink_ref, h, 1.0)[0] for h in heads]
        pn, psink = _attn_probs(q4, kcat, visible, sink4)
        dp = [_dot_nt(do4[h], vcat[h]) for h in heads]
        delta = [jnp.sum(pn[h] * dp[h], axis=-1, keepdims=True) for h in heads]
        ds16 = [(pn[h] * (dp[h] - delta[h])).astype(BF16) for h in heads]
        dq4 = [_dot(ds16[h], kcat[h]) for h in heads]
        dk_acc = [_dot_tn(ds16[h], q4[h]) for h in heads]
        dv_acc = [_dot_tn(pn[h].astype(BF16), do4[h]) for h in heads]
        dsink = jnp.zeros((1, LANES), F32)
        for kvh in heads:
            ksl = slice(kvh * LANES, (kvh + 1) * LANES)
            sink_terms = psink[kvh] * delta[kvh]
            for j in range(4):
                part = jnp.sum(sink_terms[j * CHUNK:(j + 1) * CHUNK], axis=0, keepdims=True)
                dsink = dsink - jnp.where(lane1 == kvh * 4 + j, part, 0.0)
            for pp in range(2):
                qsl = slice((kvh * 2 + pp) * LANES, (kvh * 2 + pp + 1) * LANES)
                dq_ref[:, qsl] = (_unstack_pairs(dq4[kvh], pp) * ATTN_SCALE).astype(BF16)
            dkm_ref[:, ksl] += dk_acc[kvh][0:CHUNK]
            dvm_ref[:, ksl] += dv_acc[kvh][0:CHUNK]
            dkp_ref[:, ksl] = dk_acc[kvh][CHUNK:2 * CHUNK]
            dvp_ref[:, ksl] = dv_acc[kvh][CHUNK:2 * CHUNK]
            dkc_ref[:, ksl] = dk_acc[kvh][2 * CHUNK:3 * CHUNK]
            dvc_ref[:, ksl] = dv_acc[kvh][2 * CHUNK:3 * CHUNK]
        dsink_ref[...] += dsink

    qspec = pl.BlockSpec((CHUNK, D_MODEL), lambda b: (b, 0))
    kvspec = pl.BlockSpec((CHUNK, 2 * D_KV), lambda b: (b, 0))
    fixed = pl.BlockSpec((CHUNK, 2 * D_KV), lambda b: (0, 0))
    kv_shape = jax.ShapeDtypeStruct((rows, 2 * D_KV), F32)
    meta_shape = jax.ShapeDtypeStruct((CHUNK, 2 * D_KV), F32)
    return _call(
        body, name=name,
        out_shape=(jax.ShapeDtypeStruct((rows, D_MODEL), BF16), kv_shape, kv_shape, kv_shape, kv_shape,
                   meta_shape, meta_shape, jax.ShapeDtypeStruct((1, LANES), F32)),
        grid=(rows // CHUNK,),
        in_specs=[qspec] + _attn_specs(0) + _attn_specs(1) + [pl.BlockSpec(memory_space=pltpu.SMEM), qspec],
        out_specs=(qspec, kvspec, kvspec, kvspec, kvspec, fixed, fixed, pl.BlockSpec((1, LANES), lambda b: (0, 0))),
        operands=[q, kv2, kv2, kv2, kv2, kv2, kv2, sinks, do], semantics=("arbitrary",), steps=steps)


def _kv_grad_combine(name, dk_cur, dk_prev, dk_meta, dv_cur, dv_prev, dv_meta):
    rows = dk_cur.shape[0]
    nb = rows // CHUNK
    width = 2 * D_KV

    def body(kc_ref, kp_ref, km_ref, vc_ref, vp_ref, vm_ref, o_ref):
        jj = pl.program_id(0) + jnp.zeros((CHUNK, 1), jnp.int32)
        for half, (c_ref, p_ref, m_ref) in enumerate(((kc_ref, kp_ref, km_ref), (vc_ref, vp_ref, vm_ref))):
            total = c_ref[...] + jnp.where(jj < nb - 1, p_ref[...], 0.0) + jnp.where(jj == 0, m_ref[...], 0.0)
            o_ref[:, half * width:(half + 1) * width] = total.astype(BF16)

    blk = lambda f: pl.BlockSpec((CHUNK, width), f)
    three = lambda: [blk(lambda j: (j, 0)), blk(lambda j: (jnp.minimum(j + 1, nb - 1), 0)), blk(lambda j: (0, 0))]
    return pl.pallas_call(
        body, name=name, out_shape=jax.ShapeDtypeStruct((rows, 2 * width), BF16), grid=(nb,),
        in_specs=three() + three(), out_specs=pl.BlockSpec((CHUNK, 2 * width), lambda j: (j, 0)),
        compiler_params=_cparams(("parallel",)),
    )(dk_cur, dk_prev, dk_meta, dv_cur, dv_prev, dv_meta)


def _adamw(name, w, g, m, v):
    rows, width = w.shape
    tr = rows
    for cand in range(8, rows + 1, 8):
        if rows % cand == 0 and cand * width * 4 <= (1 << 20):
            tr = cand

    def body(w_ref, g_ref, m_ref, v_ref, d_ref, mo_ref, vo_ref):
        gv = g_ref[...]
        mn = ADAM_B1 * m_ref[...] + (1.0 - ADAM_B1) * gv
        vn = ADAM_B2 * v_ref[...] + (1.0 - ADAM_B2) * (gv * gv)
        m_hat = mn / (1.0 - ADAM_B1 ** ADAM_STEP)
        v_hat = vn / (1.0 - ADAM_B2 ** ADAM_STEP)
        d_ref[...] = -ADAM_LR * (m_hat / (jnp.sqrt(v_hat) + ADAM_EPS) + ADAM_WD * w_ref[...])
        mo_ref[...] = mn
        vo_ref[...] = vn

    blk = pl.BlockSpec((tr, width), lambda i: (i, 0))
    shp = jax.ShapeDtypeStruct((rows, width), F32)
    return pl.pallas_call(
        body, name=name, out_shape=(shp, shp, shp), grid=(rows // tr,), in_specs=[blk] * 4, out_specs=(blk,) * 3,
        compiler_params=_cparams(("parallel",)),
    )(w, g, m, v)


class _GivenWeights:
    def __init__(self, p):
        self.p = p
        self.grads = {}

    def weight(self, name, layer=None):
        return self.p[name] if layer is None else self.p[name][layer]

    def steps(self, kernel):
        return ()

    def grad(self, name, layer, g):
        self.grads[(name, layer)] = g


def _ffn_fwd(tag, h, hn, p, i, plan):
    up = _mm_nn_bychip(f"ffn{tag}_up", hn, plan.weight("f_w_up", i), steps=plan.steps(f"ffn{tag}_up"))
    act = _ffn_conv_fwd(f"ffn{tag}_conv", up, p["f_conv_w"][i], p["f_conv_b"][i:i + 1], steps=plan.steps(f"ffn{tag}_conv"))
    pre = _mm(f"ffn{tag}_down", act, plan.weight("f_w_down", i), "nn")
    return pre, (h, hn, up, act, pre)


def _ffn_bwd(tag, dpre, saved, p, i, plan):
    h, hn, up, act, pre = saved
    plan.grad("f_w_down", i, _mm(f"ffn{tag}_down_dw", act, dpre, "tn", out_dtype=BF16))
    dact = _mm(f"ffn{tag}_down_dx", dpre, plan.weight("f_w_down", i), "nt", steps=plan.steps(f"ffn{tag}_down_dx"))
    dug, duv, gwg, gwv, gbg, gbv = _ffn_conv_bwd(f"ffn{tag}_conv_bwd", up, dact, p["f_conv_w"][i], p["f_conv_b"][i:i + 1],
                                                 steps=plan.steps(f"ffn{tag}_conv_bwd"))
    g_cw, g_cb = jnp.concatenate([gwg, gwv], axis=1), jnp.concatenate([gbg, gbv], axis=1)
    w_up = plan.weight("f_w_up", i)
    n = w_up.shape[2]
    dhn = _mm_nt_bychip(f"ffn{tag}_up_dx_gate", dug, w_up, 0)
    dhn = _mm_nt_bychip(f"ffn{tag}_up_dx_val", duv, w_up, N_CHIPS // 2, acc=dhn)
    g_up = _mm_tn_bychip(f"ffn{tag}_up_dw_gate", hn, dug, n, 0)
    plan.grad("f_w_up", i, _mm_tn_bychip(f"ffn{tag}_up_dw_val", hn, duv, n, N_CHIPS // 2, into=g_up))
    return dhn, dict(f_conv_w=g_cw, f_conv_b=g_cb)


def _lanes_pad(a, width=LANES):
    return jnp.pad(a, [(0, 0)] * (a.ndim - 1) + [(0, width - a.shape[-1])])


def _dup_heads(w):
    rows = w.shape[0]
    w = w.reshape(rows, 2 * N_KV_HEADS, 1, HEAD_DIM)
    return jnp.broadcast_to(w, (rows, 2 * N_KV_HEADS, 2, HEAD_DIM)).reshape(rows, 4 * D_KV)


def _undup_heads(g):
    rows = g.shape[0]
    return g.reshape(rows, 2 * N_KV_HEADS, 2, HEAD_DIM).sum(axis=2).reshape(rows, 2 * D_KV)


def _local_step(x2, target, p, plan):
    seq = x2.shape[0]
    rows = seq + CHUNK
    g = {}

    h0 = jnp.concatenate([jnp.zeros((PAD_ROWS, D_MODEL), F32), p["meta_tokens"], x2], axis=0)

    w_in = plan.weight("a_w_in")
    w_dt = jnp.pad(w_in[D_MAIN:], ((0, LANES - SSM_HEADS), (0, 0)))
    dt_bias = _lanes_pad(p["a_dt_bias"])
    a128 = _lanes_pad(-jnp.exp(p["a_a_log"]))
    dskexp = jnp.repeat(p["a_d_skip"].reshape(SSM_HEADS), HEAD_DIM).reshape(1, D_INNER)

    hn0 = _rms_fwd("a_norm", h0, p["a_norm_pre"])
    zx = _mm("a_in_main", hn0, w_in, "nt", k_rows=D_MAIN, steps=plan.steps("a_in_main"))
    dtr = _mm("a_in_dt", hn0, w_dt, "nt")
    xbc = _conv4_fwd("a_conv", zx, p["a_conv_w"], p["a_conv_b"], steps=plan.steps("a_conv"))
    dt = _dt_fwd("a_dt", dtr, dt_bias)
    dt_exp, acs_exp, acs_rows = _ssd_prep("a_ssd_prep", dt, a128, steps=plan.steps("a_ssd_prep"))
    y, states = _ssd_fwd("a_ssd", xbc, dt_exp, acs_exp, acs_rows, dskexp, steps=plan.steps("a_ssd"))
    yn = _gate_fwd("a_gate", y, zx, p["a_gate_norm"], steps=plan.steps("a_gate"))
    mix = _mm("a_out", yn, plan.weight("a_w_out"), "nn", steps=plan.steps("a_out"))
    h1, (hn_f0,) = _resid_norm_fwd("a_resid", h0, mix, p["a_norm_post"], [p["f_norm_pre"][0:1]])

    pre_f0, ffn0 = _ffn_fwd("0", h1, hn_f0, p, 0, plan)
    h2, (hkv, hn2) = _resid_norm_fwd("ffn0_resid", h1, pre_f0, p["f_norm_post"][0:1], [p["kv_norm"], p["b_norm_pre"]])

    w_kv2 = _dup_heads(plan.weight("w_kv"))
    kv2 = _mm("kv_proj", hkv, w_kv2, "nn")
    q = _mm("b_q", hn2, plan.weight("b_w_q"), "nn")
    sinks = p["b_sinks"].reshape(N_Q_HEADS)
    o = _attn_fwd("b_attn", q, kv2, sinks, steps=plan.steps("b_attn"))
    attn = _mm("b_o", o, plan.weight("b_w_o"), "nn", steps=plan.steps("b_o"))
    h3, (hn_f1,) = _resid_norm_fwd("b_resid", h2, attn, p["b_norm_post"], [p["f_norm_pre"][1:2]])

    pre_f1, ffn1 = _ffn_fwd("1", h3, hn_f1, p, 1, plan)
    dh, loss_vec, dpre_f1, g_post1 = _resid_norm_loss("ffn1_resid_loss", h3, pre_f1, p["f_norm_post"][1:2], target)
    loss = loss_vec[0, 0]

    dhn_f1, g1 = _ffn_bwd("1", dpre_f1, ffn1, p, 1, plan)
    dh, g_pre1, dpre, g["b_norm_post"] = _norm_bwd_add("ffn1_norm_bwd", dh, dhn_f1, h3, p["f_norm_pre"][1:2],
                                                        then=(attn, p["b_norm_post"]))
    plan.grad("b_w_o", None, _mm("b_o_dw", o, dpre, "tn", out_dtype=BF16))
    do = _mm("b_o_dx", dpre, plan.weight("b_w_o"), "nt", steps=plan.steps("b_o_dx"))
    dq, dkc, dkp, dvc, dvp, dkm, dvm, dsink = _attn_bwd("b_attn_bwd", q, kv2, sinks, do, steps=plan.steps("b_attn_bwd"))
    g["b_sinks"] = dsink[:, :N_Q_HEADS]
    dhn2 = _mm("b_q_dx", dq, plan.weight("b_w_q"), "nt")
    plan.grad("b_w_q", None, _mm("b_q_dw", hn2, dq, "tn", out_dtype=BF16))
    dh, g["b_norm_pre"] = _norm_bwd_add("b_norm_bwd", dh, dhn2, h2, p["b_norm_pre"])
    dkv2 = _kv_grad_combine("kv_grad", dkc, dkp, dkm, dvc, dvp, dvm)
    dhkv = _mm("kv_proj_dx", dkv2, w_kv2, "nt")
    plan.grad("w_kv", None, _undup_heads(_mm("kv_proj_dw", hkv, dkv2, "tn")))
    dh, g["kv_norm"], dpre_f0, g_post0 = _norm_bwd_add("kv_norm_bwd", dh, dhkv, h2, p["kv_norm"],
                                                       then=(pre_f0, p["f_norm_post"][0:1]))

    dhn_f0, g0 = _ffn_bwd("0", dpre_f0, ffn0, p, 0, plan)
    dh, g_pre0, dpre, g["a_norm_post"] = _norm_bwd_add("ffn0_norm_bwd", dh, dhn_f0, h1, p["f_norm_pre"][0:1],
                                                        then=(mix, p["a_norm_post"]))
    g["f_norm_post"] = jnp.concatenate([g_post0, g_post1], axis=0)
    g["f_norm_pre"] = jnp.concatenate([g_pre0, g_pre1], axis=0)
    g["f_conv_w"] = jnp.stack([g0["f_conv_w"], g1["f_conv_w"]])
    g["f_conv_b"] = jnp.concatenate([g0["f_conv_b"], g1["f_conv_b"]], axis=0)
    plan.grad("a_w_out", None, _mm("a_out_dw", yn, dpre, "tn", out_dtype=BF16))
    dyn = _mm("a_out_dx", dpre, plan.weight("a_w_out"), "nt", steps=plan.steps("a_out_dx"))
    dy, dz, g["a_gate_norm"] = _gate_bwd("a_gate_bwd", dyn, y, zx, p["a_gate_norm"])
    dxs, dbm, dcm, ddt, dalog, ddsk = _ssd_bwd("a_ssd_bwd", xbc, dt_exp, acs_exp, acs_rows, dt, a128, dskexp, dy, states,
                                              steps=plan.steps("a_ssd_bwd"))
    g["a_a_log"] = dalog[:, :SSM_HEADS]
    g["a_d_skip"] = ddsk[:, :SSM_HEADS]
    ddtr, dbias = _dt_bwd("a_dt_bwd", ddt, dtr, dt_bias)
    g["a_dt_bias"] = dbias[:, :SSM_HEADS]
    dxp, gw_x, gb_x = _conv4_bwd("a_conv_bwd_x", zx, dxs, p["a_conv_w"], p["a_conv_b"], 0)
    dbp, gw_b, gb_b = _conv4_bwd("a_conv_bwd_b", zx, dbm, p["a_conv_w"], p["a_conv_b"], D_INNER)
    dcp, gw_c, gb_c = _conv4_bwd("a_conv_bwd_c", zx, dcm, p["a_conv_w"], p["a_conv_b"], D_INNER + D_BC)
    g["a_conv_w"] = jnp.concatenate([gw_x, gw_b, gw_c], axis=1)
    g["a_conv_b"] = jnp.concatenate([gb_x, gb_b, gb_c], axis=1)
    dzx = jnp.concatenate([dz, dxp, dbp, dcp], axis=1)
    g_in = _mm("a_in_main_dw", dzx, hn0, "tn", out_dtype=BF16, out_rows=D_IN_PROJ, steps=plan.steps("a_in_main_dw"))
    plan.grad("a_w_in", None, _tn_rows_into("a_in_dt_dw", ddtr, hn0, g_in, D_MAIN, SSM_HEADS))
    dhn0 = _mm("a_in_dt_dx", ddtr, w_dt, "nn", steps=plan.steps("a_in_dt_dx"))
    dhn0 = _mm("a_in_main_dx", dzx, w_in, "nn", acc=dhn0, steps=plan.steps("a_in_main_dx"))
    dh, g["a_norm_pre"] = _norm_bwd_add("a_norm_bwd", dh, dhn0, h0, p["a_norm_pre"], steps=plan.steps("a_norm_bwd"))

    g["meta_tokens"] = dh[PAD_ROWS:CHUNK]
    return loss, dh[CHUNK:], g


ANY = pl.BlockSpec(memory_space=pl.ANY)
VMEM_SPEC = pl.BlockSpec(memory_space=pltpu.VMEM)


def _allgather_small(name, shard):
    rows = shard.shape[0]

    def body(s_ref, o_ref, send_sems, recv_sems):
        x, y, c = _place()
        me = 2 * x + y
        o_ref[me] = s_ref[...]
        chips = _other_chips(x, y)
        sends = [pltpu.make_async_remote_copy(s_ref, o_ref.at[me], send_sems.at[j], recv_sems.at[j],
                                              device_id=(cx, cy, c), device_id_type=MESH)
                 for j, (cx, cy) in enumerate(chips)]
        for cp in sends:
            cp.start()
        for j, (cx, cy) in enumerate(chips):
            pltpu.make_async_remote_copy(s_ref, o_ref.at[2 * cx + cy], send_sems.at[j], recv_sems.at[j],
                                         device_id=(cx, cy, c), device_id_type=MESH).wait_recv()
        for cp in sends:
            cp.wait_send()

    return pl.pallas_call(
        body, name=name, out_shape=jax.ShapeDtypeStruct((N_CHIPS, rows, LANES), F32),
        in_specs=[VMEM_SPEC], out_specs=VMEM_SPEC,
        scratch_shapes=[pltpu.SemaphoreType.DMA((3,)), pltpu.SemaphoreType.DMA((3,))],
        compiler_params=pltpu.CompilerParams(vmem_limit_bytes=VMEM_LIMIT),
    )(shard)


def _row_block(rows, width, itemsize, align, budget=2 << 20):
    best = rows
    for cand in range(align, rows + 1, align):
        if rows % cand == 0 and cand * width * itemsize <= budget:
            best = cand
    return best


def _cast_into_slot(name, chip, w, layer=None):
    rows, width = w.shape[-2:]
    tr = _row_block(rows, width, 4, 16)
    if layer is None:
        in_spec = pl.BlockSpec((tr, width), lambda i, chip_ref: (i, 0))
    else:
        in_spec = pl.BlockSpec((None, tr, width), lambda i, chip_ref: (layer, i, 0))

    def body(chip_ref, w_ref, o_ref):
        o_ref[...] = w_ref[...].astype(BF16)

    return pl.pallas_call(
        body, name=name, out_shape=jax.ShapeDtypeStruct((N_CHIPS, rows, width), BF16),
        grid_spec=pltpu.PrefetchScalarGridSpec(
            num_scalar_prefetch=1, grid=(rows // tr,), in_specs=[in_spec],
            out_specs=pl.BlockSpec((None, tr, width), lambda i, chip_ref: (chip_ref[0], i, 0))),
        compiler_params=_cparams(("parallel",)),
    )(chip, w)


def _allreduce_small(name, vec):
    rows = vec.shape[0]

    def body(v_ref, o_ref, buf, send_sems, recv_sems):
        x, y, c = _place()
        me = 4 * x + 2 * y + c
        buf[me] = v_ref[...]

        def peer(k):
            kx, ky, kc = (k >> 2) & 1, (k >> 1) & 1, k & 1
            return (1 - x if kx else x, 1 - y if ky else y, 1 - c if kc else c)

        sends = []
        for k in range(1, N_DEV):
            cp = pltpu.make_async_remote_copy(v_ref, buf.at[me], send_sems.at[k - 1], recv_sems.at[k - 1],
                                              device_id=peer(k), device_id_type=MESH)
            cp.start()
            sends.append(cp)
        for k in range(1, N_DEV):
            px, py, pc = peer(k)
            pltpu.make_async_remote_copy(v_ref, buf.at[4 * px + 2 * py + pc], send_sems.at[k - 1], recv_sems.at[k - 1],
                                         device_id=(px, py, pc), device_id_type=MESH).wait_recv()
        for cp in sends:
            cp.wait_send()
        acc = buf[0]
        for d in range(1, N_DEV):
            acc = acc + buf[d]
        o_ref[...] = acc

    return pl.pallas_call(
        body, name=name, out_shape=jax.ShapeDtypeStruct((rows, LANES), F32),
        in_specs=[VMEM_SPEC], out_specs=VMEM_SPEC,
        scratch_shapes=[pltpu.VMEM((N_DEV, rows, LANES), F32), pltpu.SemaphoreType.DMA((N_DEV - 1,)),
                        pltpu.SemaphoreType.DMA((N_DEV - 1,))],
        compiler_params=pltpu.CompilerParams(vmem_limit_bytes=VMEM_LIMIT),
    )(vec)


def _rs_pair_add(name, place, grads, partner, split="rows"):
    _, half_rows, width = partner.shape
    tr = _row_block(half_rows, width, 2, 16)
    nb = half_rows // tr
    if split == "rows":
        mine = pl.BlockSpec((None, tr, width), lambda s, i, pr: (s, pr[1] * nb + i, 0))
    else:
        mine = pl.BlockSpec((None, tr, width), lambda s, i, pr: (s, i, pr[1]))

    def body(place_ref, g_ref, p_ref, o_ref):
        o_ref[...] = (g_ref[...].astype(F32) + p_ref[...].astype(F32)).astype(BF16)

    return pl.pallas_call(
        body, name=name, out_shape=jax.ShapeDtypeStruct(partner.shape, BF16),
        grid_spec=pltpu.PrefetchScalarGridSpec(
            num_scalar_prefetch=1, grid=(N_CHIPS, nb),
            in_specs=[mine, pl.BlockSpec((None, tr, width), lambda s, i, pr: (s, i, 0))],
            out_specs=pl.BlockSpec((None, tr, width), lambda s, i, pr: (s, i, 0))),
        compiler_params=_cparams(("parallel", "parallel")),
    )(place, grads, partner)


def _rs_chip_add(name, place, mine, others, split="rows"):
    _, half_rows, width = mine.shape
    tr = _row_block(half_rows, width, 4, 16, budget=1 << 20)
    nb = half_rows // tr
    if split == "rows":
        out_shape, out_spec = (2 * half_rows, width), pl.BlockSpec((tr, width), lambda i, pr: (pr[1] * nb + i, 0))
    else:
        out_shape, out_spec = (half_rows, 2 * width), pl.BlockSpec((tr, width), lambda i, pr: (i, pr[1]))

    def body(place_ref, q_ref, r_ref, o_ref):
        acc = q_ref[...].astype(F32)
        for j in range(3):
            acc = acc + r_ref[j].astype(F32)
        o_ref[...] = acc

    return pl.pallas_call(
        body, name=name, out_shape=jax.ShapeDtypeStruct(out_shape, F32),
        grid_spec=pltpu.PrefetchScalarGridSpec(
            num_scalar_prefetch=1, grid=(nb,),
            in_specs=[pl.BlockSpec((None, tr, width), lambda i, pr: (pr[0], i, 0)),
                      pl.BlockSpec((3, tr, width), lambda i, pr: (0, i, 0))],
            out_specs=out_spec),
        compiler_params=_cparams(("parallel",)),
    )(place, mine, others)


WEIGHTS = ["meta_tokens", "a_norm_pre", "a_w_in", "a_conv_w", "a_conv_b", "a_dt_bias", "a_a_log", "a_d_skip",
           "a_gate_norm", "a_w_out", "a_norm_post", "kv_norm", "w_kv", "b_norm_pre", "b_w_q", "b_sinks", "b_w_o",
           "b_norm_post", "f_norm_pre", "f_w_up", "f_conv_w", "f_conv_b", "f_w_down", "f_norm_post"]
FULL_SHAPE = {
    "meta_tokens": (16, 1024), "a_norm_pre": (1, 1024), "a_w_in": (1, 1024, 5152), "a_conv_w": (1, 4, 3072),
    "a_conv_b": (1, 3072), "a_dt_bias": (1, 32), "a_a_log": (1, 32), "a_d_skip": (1, 32), "a_gate_norm": (1, 2048),
    "a_w_out": (1, 2048, 1024), "a_norm_post": (1, 1024), "kv_norm": (1024,), "w_kv": (1024, 512),
    "b_norm_pre": (1, 1024), "b_w_q": (1, 1024, 1024), "b_sinks": (1, 16), "b_w_o": (1, 1024, 1024),
    "b_norm_post": (1, 1024), "f_norm_pre": (2, 1024), "f_w_up": (2, 1024, 5632), "f_conv_w": (2, 3, 5632),
    "f_conv_b": (2, 5632), "f_w_down": (2, 2816, 1024), "f_norm_post": (2, 1024),
}
SHARD_AXIS = {
    "meta_tokens": 1, "a_norm_pre": 1, "a_w_in": 2, "a_conv_w": 2, "a_conv_b": 1, "a_dt_bias": None, "a_a_log": None,
    "a_d_skip": None, "a_gate_norm": 1, "a_w_out": 1, "a_norm_post": 1, "kv_norm": None, "w_kv": 0, "b_norm_pre": None,
    "b_w_q": 1, "b_sinks": None, "b_w_o": 1, "b_norm_post": None, "f_norm_pre": None, "f_w_up": 2, "f_conv_w": 2,
    "f_conv_b": None, "f_w_down": 1, "f_norm_post": None,
}
BIG = ["a_w_in", "a_w_out", "w_kv", "b_w_q", "b_w_o", "f_w_up", "f_w_down"]
SMALL = [n for n in WEIGHTS if n not in BIG]
SMALL_SHARDED = [n for n in SMALL if SHARD_AXIS[n] is not None]


def _shard_shape(name):
    shape = list(FULL_SHAPE[name])
    if SHARD_AXIS[name] is not None:
        shape[SHARD_AXIS[name]] //= N_CHIPS
    return tuple(shape)


def _numel(shape):
    return int(math.prod(shape))


SUBLANES = 8


def _packed_rows(shape):
    rows = -(-_numel(shape) // LANES)
    return -(-rows // SUBLANES) * SUBLANES


def _pack(arrays):
    parts = []
    for a in arrays:
        size, rows = _numel(a.shape), _packed_rows(a.shape)
        if size % LANES == 0:
            part = jnp.pad(a.reshape(size // LANES, LANES), ((0, rows - size // LANES), (0, 0)))
        else:
            part = jnp.pad(a.reshape(-1), (0, rows * LANES - size)).reshape(rows, LANES)
        parts.append(part)
    return jnp.concatenate(parts, axis=0)


def _unpack(packed, names, shape_of):
    out, off = {}, 0
    lead = packed.shape[:-2]
    for n in names:
        shape = tuple(shape_of(n))
        size, rows = _numel(shape), _packed_rows(shape)
        part = packed[..., off:off + rows, :]
        if size % LANES == 0:
            out[n] = part[..., :size // LANES, :].reshape(lead + shape)
        else:
            out[n] = part.reshape(lead + (rows * LANES,))[..., :size].reshape(lead + shape)
        off += rows
    return out


def _split_chips(name, full):
    ax = SHARD_AXIS[name]
    shape = full.shape
    cut = shape[:ax] + (N_CHIPS, shape[ax] // N_CHIPS) + shape[ax + 1:]
    return jnp.moveaxis(full.reshape(cut), ax, 0)


def _join_chips(name, stacked):
    ax = SHARD_AXIS[name]
    moved = jnp.moveaxis(stacked, 0, ax)
    shape = moved.shape
    return moved.reshape(shape[:ax] + (shape[ax] * shape[ax + 1],) + shape[ax + 2:])


def _as2d(a):
    return a.reshape(-1, a.shape[-1])


BUFFERS = [("a_w_in", "a_w_in", None), ("a_w_out", "a_w_out", None), ("w_kv", "w_kv", None),
           ("b_w_q", "b_w_q", None), ("b_w_o", "b_w_o", None), ("f_w_up0", "f_w_up", 0), ("f_w_up1", "f_w_up", 1),
           ("f_w_down0", "f_w_down", 0), ("f_w_down1", "f_w_down", 1)]


TRANSPOSED = ("a_w_in",)
SPLIT = {"a_w_in": "cols"}


def _local_shard(arrays, weight, layer):
    if weight in TRANSPOSED:
        return arrays[weight][0].T
    return _as2d(arrays[weight]) if layer is None else arrays[weight]


def _weight_from_gathered(weight, buf):
    if weight == "f_w_up":
        return buf
    return buf.reshape(N_CHIPS * buf.shape[1], buf.shape[2])


def _gathered_from_grad(weight, g):
    if weight == "f_w_up":
        return g
    return g.reshape(N_CHIPS, g.shape[0] // N_CHIPS, g.shape[1]).astype(BF16)


GATHER_SCHEDULE = {
    "a_in_main": [("ici", ["a_w_out"])],
    "a_conv": [("d2d", ["a_w_out"]), ("ici", ["f_w_down0"])],
    "a_ssd_prep": [("d2d", ["f_w_down0"]), ("ici_near", ["f_w_up0"])],
    "a_ssd": [("ici_far", ["f_w_up0"])],
    "a_gate": [("d2d", ["f_w_up0"]), ("ici", ["w_kv", "b_w_q", "b_w_o"])],
    "a_out": [("d2d", ["w_kv", "b_w_q", "b_w_o"])],
    "ffn0_up": [("ici", ["f_w_down1"])],
    "ffn0_conv": [("d2d", ["f_w_down1"]), ("ici_near", ["f_w_up1"])],
    "b_attn": [("ici_far", ["f_w_up1"])],
    "b_o": [("d2d", ["f_w_up1"])],
}
REDUCE_SCHEDULE = {
    "ffn1_conv_bwd": [("all", ["f_w_down1"])],
    "b_attn_bwd": [("all", ["f_w_up1", "b_w_o"])],
    "ffn0_conv_bwd": [("all", ["b_w_q", "w_kv", "f_w_down0"])],
    "a_ssd_bwd": [("all", ["f_w_up0", "a_w_out"])],
    "a_in_main_dx": [("near", ["a_w_in"])],
    "a_norm_bwd": [("far", ["a_w_in"])],
}
ICI_PEERS = {"ici": ALL_PEERS, "ici_near": NEAR_PEERS, "ici_far": FAR_PEERS,
             "all": ALL_PEERS, "near": NEAR_PEERS, "far": FAR_PEERS}
PAIR_SCHEDULE = {
    "ffn1_down_dx": ["f_w_down1"],
    "b_o_dx": ["f_w_up1", "b_w_o"],
    "ffn0_down_dx": ["b_w_q", "w_kv", "f_w_down0"],
    "a_out_dx": ["f_w_up0", "a_w_out"],
    "a_in_dt_dx": ["a_w_in"],
}
SWAP_SCHEDULE = {"a_in_main_dw": ["f_w_down1", "f_w_up1", "b_w_o", "b_w_q", "w_kv", "f_w_down0", "f_w_up0", "a_w_out"]}


def _buffer_of(weight, layer):
    return weight if layer is None else f"{weight}{layer}"


class _Pipeline:
    def __init__(self, place, slots):
        self.place = place
        self.slots = dict(slots)
        self.running = []
        self.grads = {}
        self.theirs = {}
        self.partials = {}
        self.peers = {}
        self.reduced = {}

    def _collect(self):
        for step, buffers, table in self.running:
            table.update(zip(buffers, step.results))
        self.running = []

    @staticmethod
    def _splits(buffers):
        return [SPLIT.get(b, "rows") for b in buffers]

    def gather_now(self, name, buffers):
        step = _step_gather_full([self.slots[b] for b in buffers], self._splits(buffers))
        _run_steps(name, [step])
        self.slots.update(zip(buffers, step.results))

    def weight(self, name, layer=None):
        self._collect()
        return _weight_from_gathered(name, self.slots[_buffer_of(name, layer)])

    def grad(self, name, layer, g):
        self.grads[_buffer_of(name, layer)] = _gathered_from_grad(name, g)

    def steps(self, kernel):
        self._collect()
        steps = []
        for phase, buffers in GATHER_SCHEDULE.get(kernel, []):
            bufs, splits = [self.slots[b] for b in buffers], self._splits(buffers)
            step = (_step_gather_d2d(bufs, splits) if phase == "d2d"
                    else _step_gather_ici(bufs, splits, ICI_PEERS[phase]))
            self.running.append((step, buffers, self.slots))
            steps.append(step)
        buffers = PAIR_SCHEDULE.get(kernel)
        if buffers:
            step = _step_pair_exchange([self.grads[b] for b in buffers], self._splits(buffers))
            self.running.append((step, buffers, self.theirs))
            steps.append(step)
        for part, buffers in REDUCE_SCHEDULE.get(kernel, []):
            for b in buffers:
                if b not in self.partials:
                    self.partials[b] = _rs_pair_add("reduce_pair_add_" + b, self.place, self.grads[b], self.theirs[b],
                                                    SPLIT.get(b, "rows"))
            started = [self.peers[b] for b in buffers] if all(b in self.peers for b in buffers) else None
            step = _step_chip_exchange([self.partials[b] for b in buffers], ICI_PEERS[part], into=started)
            self.running.append((step, buffers, self.peers))
            steps.append(step)
        buffers = SWAP_SCHEDULE.get(kernel)
        if buffers:
            step = self._swap_step(buffers)
            self.running.append((step, buffers, self.reduced))
            steps.append(step)
        return steps

    def _swap_step(self, buffers):
        halves = [_rs_chip_add("reduce_chip_add_" + b, self.place, self.partials[b], self.peers[b], SPLIT.get(b, "rows"))
                  for b in buffers]
        return _step_pair_gather(halves, self._splits(buffers))

    def finish(self):
        self._collect()
        rest = [b for b, _, _ in BUFFERS if b not in self.reduced]
        step = self._swap_step(rest)
        _run_steps("reduce_pair_gather", [step])
        self.reduced.update(zip(rest, step.results))
        return self.reduced


def kernel(x, meta_tokens, a_norm_pre, a_w_in, a_conv_w, a_conv_b, a_dt_bias, a_a_log, a_d_skip, a_gate_norm, a_w_out, a_norm_post, kv_norm, w_kv, b_norm_pre, b_w_q, b_sinks, b_w_o, b_norm_post, f_norm_pre, f_w_up, f_conv_w, f_conv_b, f_w_down, f_norm_post, loss_target, m_meta_tokens, m_a_norm_pre, m_a_w_in, m_a_conv_w, m_a_conv_b, m_a_dt_bias, m_a_a_log, m_a_d_skip, m_a_gate_norm, m_a_w_out, m_a_norm_post, m_kv_norm, m_w_kv, m_b_norm_pre, m_b_w_q, m_b_sinks, m_b_w_o, m_b_norm_post, m_f_norm_pre, m_f_w_up, m_f_conv_w, m_f_conv_b, m_f_w_down, m_f_norm_post, v_meta_tokens, v_a_norm_pre, v_a_w_in, v_a_conv_w, v_a_conv_b, v_a_dt_bias, v_a_a_log, v_a_d_skip, v_a_gate_norm, v_a_w_out, v_a_norm_post, v_kv_norm, v_w_kv, v_b_norm_pre, v_b_w_q, v_b_sinks, v_b_w_o, v_b_norm_post, v_f_norm_pre, v_f_w_up, v_f_conv_w, v_f_conv_b, v_f_w_down, v_f_norm_post):
    given = dict(locals())
    w = {n: given[n] for n in WEIGHTS}
    mom = {n: given["m_" + n] for n in WEIGHTS}
    var = {n: given["v_" + n] for n in WEIGHTS}
    chip = 2 * lax.axis_index("x") + lax.axis_index("y")
    core = lax.axis_index("c")
    place = jnp.stack([chip, core]).astype(jnp.int32)

    small_all = _allgather_small("gather_small", _pack([w[n] for n in SMALL_SHARDED]))
    small_parts = _unpack(small_all, SMALL_SHARDED, _shard_shape)
    slots = {b: _cast_into_slot("cast_" + b, place, _local_shard(w, wn, layer), layer) for b, wn, layer in BUFFERS}
    pipeline = _Pipeline(place, slots)
    pipeline.gather_now("gather_first", ["a_w_in"])
    p = {}
    for n in SMALL:
        p[n] = _join_chips(n, small_parts[n]) if n in SMALL_SHARDED else w[n]
    p["a_conv_w"] = p["a_conv_w"][0]
    p["kv_norm"] = p["kv_norm"].reshape(1, D_MODEL)

    loss_local, grad_x, g = _local_step(x[0], loss_target[0], p, pipeline)

    small_sum = _allreduce_small("reduce_small", _pack([g[n].reshape(FULL_SHAPE[n]) for n in SMALL]
                                                       + [loss_local.reshape(1, 1)]))
    small_red = _unpack(small_sum, SMALL + ["loss"], lambda n: (1, 1) if n == "loss" else FULL_SHAPE[n])
    loss = small_red["loss"][0, 0]
    grads = {}
    for n in SMALL:
        if SHARD_AXIS[n] is None:
            grads[n] = small_red[n]
        else:
            grads[n] = lax.dynamic_index_in_dim(_split_chips(n, small_red[n]), chip, 0, keepdims=False)

    shard_sum = pipeline.finish()

    delta, new_m, new_v = {}, {}, {}
    for n in BIG:
        shape = _shard_shape(n)
        if n in TRANSPOSED:
            g2d = shard_sum[n]
            w2d, m2d, v2d = (arrays[n][0].T for arrays in (w, mom, var))
            back = lambda a: a.T.reshape(shape)
        else:
            g2d = (jnp.concatenate([shard_sum[n + "0"], shard_sum[n + "1"]], axis=0) if n in ("f_w_up", "f_w_down")
                   else shard_sum[n])
            w2d, m2d, v2d = (_as2d(arrays[n]) for arrays in (w, mom, var))
            back = lambda a: a.reshape(shape)
        d, m2, v2 = _adamw("adamw_" + n, w2d, g2d, m2d, v2d)
        grads[n], delta[n], new_m[n], new_v[n] = back(g2d), back(d), back(m2), back(v2)
    packed = [_pack([src[n].reshape(_shard_shape(n)) for n in SMALL]) for src in (w, grads, mom, var)]
    outs = _adamw("adamw_small", *packed)
    for dst, flat in zip((delta, new_m, new_v), outs):
        dst.update(_unpack(flat, SMALL, _shard_shape))

    return (loss, grad_x[None], *[grads[n].reshape(_shard_shape(n)) for n in WEIGHTS],
            *[delta[n] for n in WEIGHTS], *[new_m[n] for n in WEIGHTS], *[new_v[n] for n in WEIGHTS])
```

```python
import functools
import math

import jax
import jax.numpy as jnp
from jax import lax
from jax.experimental import pallas as pl
from jax.experimental.pallas import tpu as pltpu

F32, BF16 = jnp.float32, jnp.bfloat16
MESH = pl.DeviceIdType.MESH

D_MODEL = 1024
N_META = 16
CHUNK = 128
PAD_ROWS = CHUNK - N_META
D_INNER = 2048
D_STATE = 128
N_GROUPS = 4
HEADS_PER_GROUP = 8
SSM_HEADS = 32
HEAD_DIM = 64
D_BC = N_GROUPS * D_STATE
D_XBC = D_INNER + 2 * D_BC
D_MAIN = D_INNER + D_XBC
D_IN_PROJ = D_MAIN + SSM_HEADS
GROUP_W = HEADS_PER_GROUP * HEAD_DIM
SSM_CONV = 4
D_FF = 2816
FFN_CONV = 3
N_Q_HEADS = 16
N_KV_HEADS = 4
D_KV = 256
ATTN_SCALE = 1.0 / math.sqrt(HEAD_DIM)
RMS_EPS = 1e-6
NEG_INF = -1e30
LANES = 128
VMEM_LIMIT = 48 * 1024 * 1024

ADAM_LR, ADAM_B1, ADAM_B2, ADAM_EPS, ADAM_WD, ADAM_STEP = 0.001, 0.9, 0.999, 1e-08, 0.01, 10

N_CHIPS = 4
N_DEV = 8


def _cparams(sem=None):
    return pltpu.CompilerParams(dimension_semantics=sem, vmem_limit_bytes=VMEM_LIMIT)


def _tile(n, cands=(512, 256, 128)):
    for t in cands:
        if n % t == 0:
            return t
    return n


def _row_tile(rows, width):
    for t in (544, 272):
        if rows % t == 0 and t * width * 4 <= (3 << 20):
            return t
    return 128


def _rows_mask(i, tm):
    rows = i * tm + lax.broadcasted_iota(jnp.int32, (tm, 1), 0)
    return rows >= PAD_ROWS


def _dot(a, b):
    return jnp.dot(a, b, preferred_element_type=F32)


def _dot_nt(a, b):
    return lax.dot_general(a, b, (((1,), (1,)), ((), ())), preferred_element_type=F32)


def _dot_tn(a, b):
    return lax.dot_general(a, b, (((0,), (0,)), ((), ())), preferred_element_type=F32)


def _sigmoid(x):
    return 1.0 / (1.0 + jnp.exp(-x))


def _place():
    return lax.axis_index("x"), lax.axis_index("y"), lax.axis_index("c")


def _other_chips(x, y):
    return [(1 - x, y), (x, 1 - y), (1 - x, 1 - y)]


class _Step:
    def __init__(self, ins, outs, aliases, n_sems, start, finish):
        self.ins, self.outs, self.aliases, self.n_sems = list(ins), list(outs), dict(aliases), n_sems
        self.start, self.finish = start, finish
        self.results = None


def _like(a):
    return jax.ShapeDtypeStruct(a.shape, a.dtype)


def _remote(src, dst, send_sems, recv_sems, k, device):
    return pltpu.make_async_remote_copy(src, dst, send_sems.at[k], recv_sems.at[k], device_id=device, device_id_type=MESH)


def _half(ref, split, which, lead=()):
    if split == "rows":
        hr = ref.shape[-2] // 2
        return ref.at[lead + (pl.ds(which * hr, hr),)]
    hc = ref.shape[-1] // 2
    return ref.at[lead + (slice(None), pl.ds(which * hc, hc))]


def _splits(bufs, splits):
    return list(splits) if splits is not None else ["rows"] * len(bufs)


ALL_PEERS = (0, 1, 2)
NEAR_PEERS = (0, 1)
FAR_PEERS = (2,)


def _step_gather_ici(bufs, splits=None, peers=ALL_PEERS):
    splits = _splits(bufs, splits)

    def copies(outs, send_sems, recv_sems, received):
        x, y, c = _place()
        me = 2 * x + y
        for k, o in enumerate(outs):
            for j, (cx, cy) in enumerate(_other_chips(x, y)):
                if j in peers:
                    part = _half(o, splits[k], c, (2 * cx + cy if received else me,))
                    yield _remote(part, part, send_sems, recv_sems, 3 * k + j, (cx, cy, c))

    def start(ins, outs, send_sems, recv_sems):
        for cp in copies(outs, send_sems, recv_sems, False):
            cp.start()

    def finish(ins, outs, send_sems, recv_sems):
        for cp in copies(outs, send_sems, recv_sems, True):
            cp.wait_recv()
        for cp in copies(outs, send_sems, recv_sems, False):
            cp.wait_send()

    return _Step(bufs, [_like(b) for b in bufs], {k: k for k in range(len(bufs))}, 3 * len(bufs), start, finish)


def _step_gather_d2d(bufs, splits=None):
    splits = _splits(bufs, splits)

    def copies(outs, send_sems, recv_sems, received):
        x, y, c = _place()
        for k, o in enumerate(outs):
            for j, (cx, cy) in enumerate(_other_chips(x, y)):
                part = _half(o, splits[k], 1 - c if received else c, (2 * cx + cy,))
                yield _remote(part, part, send_sems, recv_sems, 3 * k + j, (x, y, 1 - c))

    def start(ins, outs, send_sems, recv_sems):
        for cp in copies(outs, send_sems, recv_sems, False):
            cp.start()

    def finish(ins, outs, send_sems, recv_sems):
        for cp in copies(outs, send_sems, recv_sems, True):
            cp.wait_recv()
        for cp in copies(outs, send_sems, recv_sems, False):
            cp.wait_send()

    return _Step(bufs, [_like(b) for b in bufs], {k: k for k in range(len(bufs))}, 3 * len(bufs), start, finish)


def _step_gather_full(bufs, splits=None):
    n = len(bufs)
    splits = _splits(bufs, splits)

    def ici(outs, send_sems, recv_sems, received):
        x, y, c = _place()
        me = 2 * x + y
        for k, o in enumerate(outs):
            for j, (cx, cy) in enumerate(_other_chips(x, y)):
                part = _half(o, splits[k], c, (2 * cx + cy if received else me,))
                yield _remote(part, part, send_sems, recv_sems, 3 * k + j, (cx, cy, c))

    def d2d(outs, send_sems, recv_sems, received):
        x, y, c = _place()
        for k, o in enumerate(outs):
            for j, (cx, cy) in enumerate(_other_chips(x, y)):
                part = _half(o, splits[k], 1 - c if received else c, (2 * cx + cy,))
                yield _remote(part, part, send_sems, recv_sems, 3 * n + 3 * k + j, (x, y, 1 - c))

    def start(ins, outs, send_sems, recv_sems):
        for cp in ici(outs, send_sems, recv_sems, False):
            cp.start()

    def finish(ins, outs, send_sems, recv_sems):
        for arrived, onward in zip(ici(outs, send_sems, recv_sems, True), d2d(outs, send_sems, recv_sems, False)):
            arrived.wait_recv()
            onward.start()
        for cp in d2d(outs, send_sems, recv_sems, True):
            cp.wait_recv()
        for cp in ici(outs, send_sems, recv_sems, False):
            cp.wait_send()
        for cp in d2d(outs, send_sems, recv_sems, False):
            cp.wait_send()

    return _Step(bufs, [_like(b) for b in bufs], {k: k for k in range(n)}, 6 * n, start, finish)


def _half_shape(shape, split):
    return shape[:-2] + ((shape[-2] // 2, shape[-1]) if split == "rows" else (shape[-2], shape[-1] // 2))


def _step_pair_exchange(grads, splits=None):
    splits = _splits(grads, splits)

    def copies(ins, outs, send_sems, recv_sems):
        x, y, c = _place()
        for k, (g, o) in enumerate(zip(ins, outs)):
            yield _remote(_half(g, splits[k], 1 - c, (slice(None),)), o, send_sems, recv_sems, k, (x, y, 1 - c))

    def start(ins, outs, send_sems, recv_sems):
        for cp in copies(ins, outs, send_sems, recv_sems):
            cp.start()

    def finish(ins, outs, send_sems, recv_sems):
        for cp in copies(ins, outs, send_sems, recv_sems):
            cp.wait()

    outs = [jax.ShapeDtypeStruct(_half_shape(g.shape, s), g.dtype) for g, s in zip(grads, splits)]
    return _Step(grads, outs, {}, len(grads), start, finish)


def _step_chip_exchange(partials, peers=ALL_PEERS, into=None):
    n = len(partials)

    def copies(ins, outs, send_sems, recv_sems):
        x, y, c = _place()
        for k, (q, o) in enumerate(zip(ins[:n], outs)):
            for j, (cx, cy) in enumerate(_other_chips(x, y)):
                if j in peers:
                    yield _remote(q.at[2 * cx + cy], o.at[j], send_sems, recv_sems, 3 * k + j, (cx, cy, c))

    def start(ins, outs, send_sems, recv_sems):
        for cp in copies(ins, outs, send_sems, recv_sems):
            cp.start()

    def finish(ins, outs, send_sems, recv_sems):
        for cp in copies(ins, outs, send_sems, recv_sems):
            cp.wait()

    outs = [jax.ShapeDtypeStruct((3,) + q.shape[1:], q.dtype) for q in partials]
    if into is None:
        return _Step(partials, outs, {}, 3 * n, start, finish)
    return _Step(list(partials) + list(into), outs, {n + k: k for k in range(n)}, 3 * n, start, finish)


def _step_pair_gather(shards, splits=None):
    splits = _splits(shards, splits)

    def copies(outs, send_sems, recv_sems, received):
        x, y, c = _place()
        for k, o in enumerate(outs):
            part = _half(o, splits[k], 1 - c if received else c)
            yield _remote(part, part, send_sems, recv_sems, k, (x, y, 1 - c))

    def start(ins, outs, send_sems, recv_sems):
        for cp in copies(outs, send_sems, recv_sems, False):
            cp.start()

    def finish(ins, outs, send_sems, recv_sems):
        for cp in copies(outs, send_sems, recv_sems, True):
            cp.wait_recv()
        for cp in copies(outs, send_sems, recv_sems, False):
            cp.wait_send()

    return _Step(shards, [_like(s) for s in shards], {k: k for k in range(len(shards))}, len(shards), start, finish)


def _call(body, *, name, out_shape, grid, in_specs, out_specs, operands, scratch_shapes=(), semantics=None, steps=()):
    single = not isinstance(out_shape, (tuple, list))
    out_shapes = [out_shape] if single else list(out_shape)
    out_spec_list = [out_specs] if single else list(out_specs)
    steps = list(steps)
    if not steps:
        res = pl.pallas_call(body, name=name, out_shape=out_shapes, grid=grid, in_specs=list(in_specs),
                             out_specs=out_spec_list, scratch_shapes=list(scratch_shapes),
                             compiler_params=_cparams(semantics))(*operands)
        return res[0] if single else res
    n_in, n_out, n_scr = len(operands), len(out_shapes), len(scratch_shapes)
    x_in = [a for s in steps for a in s.ins]
    x_out = [o for s in steps for o in s.outs]
    aliases, in_off, out_off = {}, 0, 0
    for s in steps:
        for i, o in s.aliases.items():
            aliases[n_in + in_off + i] = n_out + out_off + o
        in_off += len(s.ins)
        out_off += len(s.outs)
    sems = []
    for s in steps:
        sems += [pltpu.SemaphoreType.DMA((s.n_sems,)), pltpu.SemaphoreType.DMA((s.n_sems,))]
    any_spec = pl.BlockSpec(memory_space=pl.ANY)

    def carried(*refs):
        pos = 0
        ins = refs[pos:pos + n_in]; pos += n_in
        xi = refs[pos:pos + len(x_in)]; pos += len(x_in)
        outs = refs[pos:pos + n_out]; pos += n_out
        xo = refs[pos:pos + len(x_out)]; pos += len(x_out)
        scr = refs[pos:pos + n_scr]; pos += n_scr
        sem_refs = refs[pos:]

        def each(action):
            i0 = o0 = 0
            for k, s in enumerate(steps):
                getattr(s, action)(xi[i0:i0 + len(s.ins)], xo[o0:o0 + len(s.outs)], sem_refs[2 * k], sem_refs[2 * k + 1])
                i0 += len(s.ins)
                o0 += len(s.outs)

        if grid:
            first = functools.reduce(jnp.logical_and, [pl.program_id(d) == 0 for d in range(len(grid))])
            last = functools.reduce(jnp.logical_and, [pl.program_id(d) == grid[d] - 1 for d in range(len(grid))])
            pl.when(first)(lambda: each("start"))
            body(*ins, *outs, *scr)
            pl.when(last)(lambda: each("finish"))
        else:
            each("start")
            body(*ins, *outs, *scr)
            each("finish")

    res = pl.pallas_call(
        carried, name=name, out_shape=out_shapes + x_out, grid=grid,
        in_specs=list(in_specs) + [any_spec] * len(x_in), out_specs=out_spec_list + [any_spec] * len(x_out),
        scratch_shapes=list(scratch_shapes) + sems, input_output_aliases=aliases,
        compiler_params=_cparams(None if semantics is None else ("arbitrary",) * len(grid)),
    )(*operands, *x_in)
    o0 = n_out
    for s in steps:
        s.results = list(res[o0:o0 + len(s.outs)])
        o0 += len(s.outs)
    return res[0] if single else tuple(res[:n_out])


def _run_steps(name, steps):
    _call(lambda: None, name=name, out_shape=[], grid=(), in_specs=[], out_specs=[], operands=[], steps=steps)
    return [s.results for s in steps]


def _mm(name, a, b, mode, out_dtype=F32, acc=None, b_colblock=0, k_rows=None, out_rows=None, steps=()):
    resident_bytes = 8 << 20
    if mode == "nn":
        m, k = a.shape
        n = b.shape[1]
        tm = m
        while tm * k * 2 > resident_bytes and tm % 32 == 0:
            tm //= 2
        tn = _tile(n)
        grid = (m // tm, n // tn)
        in_specs = [pl.BlockSpec((tm, k), lambda i, j: (i, 0)), pl.BlockSpec((k, tn), lambda i, j: (0, j))]
        out_shape, out_block = (m, n), (tm, tn)
    elif mode == "nt":
        m, n = a.shape
        k = k_rows or b.shape[0]
        tm = m
        while tm * n * 2 > resident_bytes and tm % 32 == 0:
            tm //= 2
        tk = _tile(k)
        grid = (m // tm, k // tk)
        in_specs = [pl.BlockSpec((tm, n), lambda i, j: (i, 0)), pl.BlockSpec((tk, n), lambda i, j: (j, b_colblock))]
        out_shape, out_block = (m, k), (tm, tk)
    else:
        m, k = a.shape
        n = b.shape[1]
        tk, tn = _tile(k), (n if m * n * 2 <= resident_bytes else _tile(n))
        grid = (k // tk, n // tn)
        in_specs = [pl.BlockSpec((m, tk), lambda i, j: (0, i)), pl.BlockSpec((m, tn), lambda i, j: (0, j))]
        out_shape, out_block = (out_rows or k, n), (tk, tn)
    out_spec = pl.BlockSpec(out_block, lambda i, j: (i, j))
    has_acc = acc is not None

    def body(*refs):
        a_ref, b_ref = refs[0], refs[1]
        o_ref = refs[-1]
        av, bv = a_ref[...], b_ref[...]
        if mode == "nn":
            r = _dot(av, bv)
        elif mode == "nt":
            r = _dot_nt(av, bv)
        else:
            r = _dot_tn(av, bv)
        if has_acc:
            r = r + refs[2][...]
        o_ref[...] = r.astype(o_ref.dtype)

    operands = [a, b]
    if has_acc:
        in_specs = in_specs + [out_spec]
        operands.append(acc)
    return _call(body, name=name, out_shape=jax.ShapeDtypeStruct(out_shape, out_dtype), grid=grid, in_specs=in_specs,
                 out_specs=out_spec, operands=operands, semantics=("parallel", "parallel"), steps=steps)


def _tn_rows_into(name, a, b, into, row0, nrows):
    m, k = a.shape
    n = b.shape[1]

    def body(a_ref, b_ref, into_ref, o_ref):
        o_ref[...] = _dot_tn(a_ref[...], b_ref[...])[0:nrows].astype(o_ref.dtype)

    return pl.pallas_call(
        body, name=name, out_shape=jax.ShapeDtypeStruct(into.shape, into.dtype), grid=(1,),
        in_specs=[pl.BlockSpec((m, k), lambda i: (0, 0)), pl.BlockSpec((m, n), lambda i: (0, 0)),
                  pl.BlockSpec(memory_space=pl.ANY)],
        out_specs=pl.BlockSpec((nrows, n), lambda i: (row0 // nrows, 0)),
        input_output_aliases={2: 0}, compiler_params=_cparams(("arbitrary",)),
    )(a, b, into)


def _fit_rows(m, row_bytes, budget=8 << 20):
    tm = m
    while tm * row_bytes > budget and tm % 32 == 0:
        tm //= 2
    return tm


def _mm_nn_bychip(name, a, bc, steps=()):
    m, k = a.shape
    n = bc.shape[2]
    tm = min(_fit_rows(m, k * 2), _fit_rows(m, n * 4))

    def body(a_ref, b_ref, o_ref):
        o_ref[...] = _dot(a_ref[...], b_ref[...])

    return _call(
        body, name=name, out_shape=jax.ShapeDtypeStruct((m, N_CHIPS * n), F32), grid=(m // tm, N_CHIPS),
        in_specs=[pl.BlockSpec((tm, k), lambda i, c: (i, 0)), pl.BlockSpec((None, k, n), lambda i, c: (c, 0, 0))],
        out_specs=pl.BlockSpec((tm, n), lambda i, c: (i, c)), operands=[a, bc],
        semantics=("parallel", "parallel"), steps=steps)


def _mm_nt_bychip(name, a, bc, chip0, acc=None):
    m = a.shape[0]
    _, k, n = bc.shape
    nch = a.shape[1] // n
    tm, tk = _fit_rows(m, n * 2), _tile(k)
    has_acc = acc is not None

    def body(*refs):
        a_ref, b_ref, o_ref = refs[0], refs[1], refs[-1]

        @pl.when(pl.program_id(2) == 0)
        def _():
            o_ref[...] = refs[2][...] if has_acc else jnp.zeros_like(o_ref)

        o_ref[...] += _dot_nt(a_ref[...], b_ref[...])

    out_spec = pl.BlockSpec((tm, tk), lambda i, j, c: (i, j))
    in_specs = [pl.BlockSpec((tm, n), lambda i, j, c: (i, c)),
                pl.BlockSpec((None, tk, n), lambda i, j, c: (chip0 + c, j, 0))]
    operands = [a, bc]
    if has_acc:
        in_specs.append(out_spec)
        operands.append(acc)
    return pl.pallas_call(
        body, name=name, out_shape=jax.ShapeDtypeStruct((m, k), F32), grid=(m // tm, k // tk, nch),
        in_specs=in_specs, out_specs=out_spec, compiler_params=_cparams(("parallel", "parallel", "arbitrary")),
    )(*operands)


def _mm_tn_bychip(name, a, dy, n, chip0, into=None):
    m, k = a.shape
    nch = dy.shape[1] // n
    tk = _tile(k)

    def body(*refs):
        a_ref, d_ref, o_ref = refs[0], refs[1], refs[-1]
        o_ref[...] = _dot_tn(a_ref[...], d_ref[...]).astype(BF16)

    in_specs = [pl.BlockSpec((m, tk), lambda i, c: (0, i)), pl.BlockSpec((m, n), lambda i, c: (0, c))]
    operands = [a, dy]
    aliases = {}
    if into is not None:
        in_specs.append(pl.BlockSpec(memory_space=pl.ANY))
        operands.append(into)
        aliases = {2: 0}
    return pl.pallas_call(
        body, name=name, out_shape=jax.ShapeDtypeStruct((N_CHIPS, k, n), BF16), grid=(k // tk, nch),
        in_specs=in_specs, out_specs=pl.BlockSpec((None, tk, n), lambda i, c: (chip0 + c, i, 0)),
        input_output_aliases=aliases, compiler_params=_cparams(("parallel", "parallel")),
    )(*operands)


def _rms_fwd(name, h, w):
    rows, width = h.shape
    tm = _row_tile(rows, width)

    def body(h_ref, w_ref, o_ref):
        x = h_ref[...]
        r = lax.rsqrt(jnp.mean(x * x, axis=-1, keepdims=True) + RMS_EPS)
        o_ref[...] = (x * r * w_ref[...]).astype(BF16)

    return pl.pallas_call(
        body, name=name, out_shape=jax.ShapeDtypeStruct((rows, width), BF16), grid=(rows // tm,),
        in_specs=[pl.BlockSpec((tm, width), lambda i: (i, 0)), pl.BlockSpec((1, width), lambda i: (0, 0))],
        out_specs=pl.BlockSpec((tm, width), lambda i: (i, 0)), compiler_params=_cparams(("parallel",)),
    )(h, w)


def _resid_norm_fwd(name, h, pre, w, next_norms=()):
    rows, width = h.shape
    tm = _row_tile(rows, width)
    n_next = len(next_norms)

    def body(*refs):
        h_ref, p_ref, w_ref = refs[:3]
        v_refs = refs[3:3 + n_next]
        o_ref = refs[3 + n_next]
        n_refs = refs[4 + n_next:]
        p = p_ref[...]
        r = lax.rsqrt(jnp.mean(p * p, axis=-1, keepdims=True) + RMS_EPS)
        x = h_ref[...] + jnp.where(_rows_mask(pl.program_id(0), tm), p * r * w_ref[...], 0.0)
        o_ref[...] = x
        if n_next:
            rx = lax.rsqrt(jnp.mean(x * x, axis=-1, keepdims=True) + RMS_EPS)
            for v_ref, n_ref in zip(v_refs, n_refs):
                n_ref[...] = (x * rx * v_ref[...]).astype(BF16)

    row_spec = pl.BlockSpec((tm, width), lambda i: (i, 0))
    vec_spec = pl.BlockSpec((1, width), lambda i: (0, 0))
    outs = pl.pallas_call(
        body, name=name,
        out_shape=[jax.ShapeDtypeStruct((rows, width), F32)] + [jax.ShapeDtypeStruct((rows, width), BF16)] * n_next,
        grid=(rows // tm,), in_specs=[row_spec, row_spec, vec_spec] + [vec_spec] * n_next,
        out_specs=[row_spec] * (1 + n_next), compiler_params=_cparams(("parallel",)),
    )(h, pre, w, *next_norms)
    return outs[0], list(outs[1:])


def _resid_norm_loss(name, h, pre, w, target):
    rows, width = h.shape

    def body(h_ref, p_ref, w_ref, t_ref, dh_ref, loss_ref, dp_ref, dw_ref):
        i = pl.program_id(0)
        p = p_ref[...]
        r = lax.rsqrt(jnp.mean(p * p, axis=-1, keepdims=True) + RMS_EPS)
        x = h_ref[...] + p * r * w_ref[...]
        real = (i + jnp.zeros((CHUNK, 1), jnp.int32)) >= 1
        diff = jnp.where(real, x - t_ref[...], 0.0)
        dh = diff * (1.0 / D_MODEL)
        dh_ref[...] = dh
        dp, dw_rows = _rms_bwd(dh, p, w_ref[...])
        dp_ref[...] = dp.astype(BF16)

        @pl.when(i == 0)
        def _():
            loss_ref[...] = jnp.zeros_like(loss_ref)
            dw_ref[...] = jnp.zeros_like(dw_ref)

        loss_ref[...] += jnp.sum(diff * diff) * (0.5 / D_MODEL)
        dw_ref[...] += jnp.sum(dw_rows, axis=0, keepdims=True)

    blk = pl.BlockSpec((CHUNK, width), lambda i: (i, 0))
    vec_spec = pl.BlockSpec((1, width), lambda i: (0, 0))
    return pl.pallas_call(
        body, name=name,
        out_shape=(jax.ShapeDtypeStruct((rows, width), F32), jax.ShapeDtypeStruct((1, LANES), F32),
                   jax.ShapeDtypeStruct((rows, width), BF16), jax.ShapeDtypeStruct((1, width), F32)),
        grid=(rows // CHUNK,),
        in_specs=[blk, blk, vec_spec, pl.BlockSpec((CHUNK, width), lambda i: (jnp.maximum(i - 1, 0), 0))],
        out_specs=(blk, pl.BlockSpec((1, LANES), lambda i: (0, 0)), blk, vec_spec),
        compiler_params=_cparams(("arbitrary",)),
    )(h, pre, w, target)


def _rms_bwd(dy, x, w):
    r = lax.rsqrt(jnp.mean(x * x, axis=-1, keepdims=True) + RMS_EPS)
    xhat = x * r
    dxhat = dy * w
    return r * (dxhat - xhat * jnp.mean(dxhat * xhat, axis=-1, keepdims=True)), dy * xhat


def _norm_bwd_add(name, dh, dhn, h, w, then=None, steps=()):
    rows, width = dh.shape
    tm = _row_tile(rows, width)
    fused = then is not None

    def body(*refs):
        dh_ref, dhn_ref, h_ref, w_ref = refs[:4]
        o_ref, dw_ref = refs[6:8] if fused else refs[4:6]
        i = pl.program_id(0)
        valid = _rows_mask(i, tm)
        dx, dw_rows = _rms_bwd(dhn_ref[...], h_ref[...], w_ref[...])
        dh_new = dh_ref[...] + jnp.where(valid, dx, 0.0)
        o_ref[...] = dh_new

        @pl.when(i == 0)
        def _():
            dw_ref[...] = jnp.zeros_like(dw_ref)

        dw_ref[...] += jnp.sum(dw_rows, axis=0, keepdims=True)
        if fused:
            p_ref, wp_ref, dp_ref, dwp_ref = refs[4], refs[5], refs[8], refs[9]
            dp, dwp_rows = _rms_bwd(jnp.where(valid, dh_new, 0.0), p_ref[...], wp_ref[...])
            dp_ref[...] = dp.astype(BF16)

            @pl.when(i == 0)
            def _():
                dwp_ref[...] = jnp.zeros_like(dwp_ref)

            dwp_ref[...] += jnp.sum(dwp_rows, axis=0, keepdims=True)

    row_spec = pl.BlockSpec((tm, width), lambda i: (i, 0))
    vec_spec = pl.BlockSpec((1, width), lambda i: (0, 0))
    row_f32, vec_f32 = jax.ShapeDtypeStruct((rows, width), F32), jax.ShapeDtypeStruct((1, width), F32)
    in_specs, operands = [row_spec, row_spec, row_spec, vec_spec], [dh, dhn, h, w]
    out_shape, out_specs = [row_f32, vec_f32], [row_spec, vec_spec]
    if fused:
        in_specs += [row_spec, vec_spec]
        operands += list(then)
        out_shape += [jax.ShapeDtypeStruct((rows, width), BF16), vec_f32]
        out_specs += [row_spec, vec_spec]
    return _call(body, name=name, out_shape=out_shape, grid=(rows // tm,), in_specs=in_specs, out_specs=out_specs,
                 operands=operands, semantics=("arbitrary",), steps=steps)


def _shift_down(x, s, rows):
    return pltpu.roll(x, s, 0) if s else x


def _shift_up(x, s, rows):
    return pltpu.roll(x, rows - s, 0) if s else x


def _conv4_fwd(name, zx, cw, cb, steps=()):
    rows = zx.shape[0]
    off = D_INNER // LANES

    def body(x_ref, w_ref, b_ref, o_ref):
        x = x_ref[...]
        acc = b_ref[...] + w_ref[pl.ds(SSM_CONV - 1, 1), :] * x
        for s in range(1, SSM_CONV):
            acc = acc + w_ref[pl.ds(SSM_CONV - 1 - s, 1), :] * _shift_down(x, s, rows)
        valid = lax.broadcasted_iota(jnp.int32, (rows, 1), 0) >= PAD_ROWS
        o_ref[...] = jnp.where(valid, acc * _sigmoid(acc), 0.0)

    return _call(
        body, name=name, out_shape=jax.ShapeDtypeStruct((rows, D_XBC), F32), grid=(D_XBC // LANES,),
        in_specs=[pl.BlockSpec((rows, LANES), lambda j: (0, j + off)),
                  pl.BlockSpec((SSM_CONV, LANES), lambda j: (0, j)),
                  pl.BlockSpec((1, LANES), lambda j: (0, j))],
        out_specs=pl.BlockSpec((rows, LANES), lambda j: (0, j)), operands=[zx, cw, cb],
        semantics=("parallel",), steps=steps)


def _conv4_bwd(name, zx, dout, cw, cb, col0):
    rows, width = dout.shape
    zoff = (D_INNER + col0) // LANES
    woff = col0 // LANES

    def body(x_ref, d_ref, w_ref, b_ref, dx_ref, dw_ref, db_ref):
        x = x_ref[...]
        shifted = [_shift_down(x, s, rows) for s in range(SSM_CONV)]
        acc = b_ref[...]
        for s in range(SSM_CONV):
            acc = acc + w_ref[pl.ds(SSM_CONV - 1 - s, 1), :] * shifted[s]
        sig = _sigmoid(acc)
        valid = lax.broadcasted_iota(jnp.int32, (rows, 1), 0) >= PAD_ROWS
        dpre = jnp.where(valid, d_ref[...] * sig * (1.0 + acc * (1.0 - sig)), 0.0)
        dx = w_ref[pl.ds(SSM_CONV - 1, 1), :] * dpre
        for s in range(1, SSM_CONV):
            dx = dx + w_ref[pl.ds(SSM_CONV - 1 - s, 1), :] * _shift_up(dpre, s, rows)
        dx_ref[...] = dx.astype(BF16)
        for s in range(SSM_CONV):
            dw_ref[pl.ds(SSM_CONV - 1 - s, 1), :] = jnp.sum(dpre * shifted[s], axis=0, keepdims=True)
        db_ref[...] = jnp.sum(dpre, axis=0, keepdims=True)

    return pl.pallas_call(
        body, name=name,
        out_shape=(jax.ShapeDtypeStruct((rows, width), BF16), jax.ShapeDtypeStruct((SSM_CONV, width), F32),
                   jax.ShapeDtypeStruct((1, width), F32)),
        grid=(width // LANES,),
        in_specs=[pl.BlockSpec((rows, LANES), lambda j: (0, j + zoff)),
                  pl.BlockSpec((rows, LANES), lambda j: (0, j)),
                  pl.BlockSpec((SSM_CONV, LANES), lambda j: (0, j + woff)),
                  pl.BlockSpec((1, LANES), lambda j: (0, j + woff))],
        out_specs=(pl.BlockSpec((rows, LANES), lambda j: (0, j)),
                   pl.BlockSpec((SSM_CONV, LANES), lambda j: (0, j)),
                   pl.BlockSpec((1, LANES), lambda j: (0, j))),
        compiler_params=_cparams(("parallel",)),
    )(zx, dout, cw, cb)


def _ffn_conv_fwd(name, up, cw, cb, steps=()):
    rows = up.shape[0]
    nt = D_FF // LANES

    def body(g_ref, v_ref, wg_ref, wv_ref, bg_ref, bv_ref, o_ref):
        g, v = g_ref[...], v_ref[...]
        ug, uv = bg_ref[...], bv_ref[...]
        for s in range(FFN_CONV):
            ug = ug + wg_ref[pl.ds(FFN_CONV - 1 - s, 1), :] * _shift_down(g, s, rows)
            uv = uv + wv_ref[pl.ds(FFN_CONV - 1 - s, 1), :] * _shift_down(v, s, rows)
        o_ref[...] = (ug * _sigmoid(ug) * uv).astype(BF16)

    col = lambda shift: pl.BlockSpec((rows, LANES), lambda j: (0, j + shift))
    wsp = lambda shift: pl.BlockSpec((FFN_CONV, LANES), lambda j: (0, j + shift))
    bsp = lambda shift: pl.BlockSpec((1, LANES), lambda j: (0, j + shift))
    return _call(
        body, name=name, out_shape=jax.ShapeDtypeStruct((rows, D_FF), BF16), grid=(nt,),
        in_specs=[col(0), col(nt), wsp(0), wsp(nt), bsp(0), bsp(nt)],
        out_specs=pl.BlockSpec((rows, LANES), lambda j: (0, j)), operands=[up, up, cw, cw, cb, cb],
        semantics=("parallel",), steps=steps)


def _ffn_conv_bwd(name, up, dact, cw, cb, steps=()):
    rows = up.shape[0]
    nt = D_FF // LANES

    def body(g_ref, v_ref, d_ref, wg_ref, wv_ref, bg_ref, bv_ref, dxg_ref, dxv_ref, dwg_ref, dwv_ref, dbg_ref, dbv_ref):
        g, v = g_ref[...], v_ref[...]
        gs = [_shift_down(g, s, rows) for s in range(FFN_CONV)]
        vs = [_shift_down(v, s, rows) for s in range(FFN_CONV)]
        ug, uv = bg_ref[...], bv_ref[...]
        for s in range(FFN_CONV):
            ug = ug + wg_ref[pl.ds(FFN_CONV - 1 - s, 1), :] * gs[s]
            uv = uv + wv_ref[pl.ds(FFN_CONV - 1 - s, 1), :] * vs[s]
        sig = _sigmoid(ug)
        dsig = d_ref[...] * sig
        for dpre, src, w_ref, dx_ref, dw_ref, db_ref in (
                (dsig * uv * (1.0 + ug * (1.0 - sig)), gs, wg_ref, dxg_ref, dwg_ref, dbg_ref),
                (dsig * ug, vs, wv_ref, dxv_ref, dwv_ref, dbv_ref)):
            dx = w_ref[pl.ds(FFN_CONV - 1, 1), :] * dpre
            for s in range(1, FFN_CONV):
                dx = dx + w_ref[pl.ds(FFN_CONV - 1 - s, 1), :] * _shift_up(dpre, s, rows)
            dx_ref[...] = dx.astype(BF16)
            for s in range(FFN_CONV):
                dw_ref[pl.ds(FFN_CONV - 1 - s, 1), :] = jnp.sum(dpre * src[s], axis=0, keepdims=True)
            db_ref[...] = jnp.sum(dpre, axis=0, keepdims=True)

    col = lambda shift: pl.BlockSpec((rows, LANES), lambda j: (0, j + shift))
    wsp = lambda shift: pl.BlockSpec((FFN_CONV, LANES), lambda j: (0, j + shift))
    bsp = lambda shift: pl.BlockSpec((1, LANES), lambda j: (0, j + shift))
    dx_shape = jax.ShapeDtypeStruct((rows, D_FF), BF16)
    dw_shape = jax.ShapeDtypeStruct((FFN_CONV, D_FF), F32)
    db_shape = jax.ShapeDtypeStruct((1, D_FF), F32)
    return _call(
        body, name=name, out_shape=(dx_shape, dx_shape, dw_shape, dw_shape, db_shape, db_shape), grid=(nt,),
        in_specs=[col(0), col(nt), col(0), wsp(0), wsp(nt), bsp(0), bsp(nt)],
        out_specs=(col(0), col(0), wsp(0), wsp(0), bsp(0), bsp(0)),
        operands=[up, up, dact, cw, cw, cb, cb], semantics=("parallel",), steps=steps)


def _dt_fwd(name, dtr, bias):
    rows = dtr.shape[0]
    tm = _row_tile(rows, LANES)

    def body(d_ref, b_ref, o_ref):
        v = d_ref[...] + b_ref[...]
        sp = jnp.maximum(v, 0.0) + jnp.log1p(jnp.exp(-jnp.abs(v)))
        lane = lax.broadcasted_iota(jnp.int32, (tm, LANES), 1)
        ok = _rows_mask(pl.program_id(0), tm) & (lane < SSM_HEADS)
        o_ref[...] = jnp.where(ok, sp, 0.0)

    return pl.pallas_call(
        body, name=name, out_shape=jax.ShapeDtypeStruct((rows, LANES), F32), grid=(rows // tm,),
        in_specs=[pl.BlockSpec((tm, LANES), lambda i: (i, 0)), pl.BlockSpec((1, LANES), lambda i: (0, 0))],
        out_specs=pl.BlockSpec((tm, LANES), lambda i: (i, 0)), compiler_params=_cparams(("parallel",)),
    )(dtr, bias)


def _dt_bwd(name, ddt, dtr, bias):
    rows = dtr.shape[0]
    tm = _row_tile(rows, LANES)

    def body(g_ref, d_ref, b_ref, o_ref, db_ref):
        i = pl.program_id(0)
        lane = lax.broadcasted_iota(jnp.int32, (tm, LANES), 1)
        ok = _rows_mask(i, tm) & (lane < SSM_HEADS)
        dv = jnp.where(ok, g_ref[...] * _sigmoid(d_ref[...] + b_ref[...]), 0.0)
        o_ref[...] = dv.astype(BF16)

        @pl.when(i == 0)
        def _():
            db_ref[...] = jnp.zeros_like(db_ref)

        db_ref[...] += jnp.sum(dv, axis=0, keepdims=True)

    row_spec = pl.BlockSpec((tm, LANES), lambda i: (i, 0))
    vec_spec = pl.BlockSpec((1, LANES), lambda i: (0, 0))
    return pl.pallas_call(
        body, name=name,
        out_shape=(jax.ShapeDtypeStruct((rows, LANES), BF16), jax.ShapeDtypeStruct((1, LANES), F32)),
        grid=(rows // tm,), in_specs=[row_spec, row_spec, vec_spec], out_specs=(row_spec, vec_spec),
        compiler_params=_cparams(("arbitrary",)),
    )(ddt, dtr, bias)


def _gate_fwd(name, y, zx, w, steps=()):
    rows = y.shape[0]
    tm = _row_tile(rows, D_INNER)

    def body(y_ref, z_ref, w_ref, o_ref):
        z = z_ref[...]
        g = y_ref[...] * (z * _sigmoid(z))
        r = lax.rsqrt(jnp.mean(g * g, axis=-1, keepdims=True) + RMS_EPS)
        o_ref[...] = (g * r * w_ref[...]).astype(BF16)

    row_spec = pl.BlockSpec((tm, D_INNER), lambda i: (i, 0))
    return _call(
        body, name=name, out_shape=jax.ShapeDtypeStruct((rows, D_INNER), BF16), grid=(rows // tm,),
        in_specs=[row_spec, row_spec, pl.BlockSpec((1, D_INNER), lambda i: (0, 0))],
        out_specs=row_spec, operands=[y, zx, w], semantics=("parallel",), steps=steps)


def _gate_bwd(name, dyn, y, zx, w):
    rows = y.shape[0]
    tm = _row_tile(rows, D_INNER)

    def body(d_ref, y_ref, z_ref, w_ref, dy_ref, dz_ref, dw_ref):
        i = pl.program_id(0)
        z, yv = z_ref[...], y_ref[...]
        sig = _sigmoid(z)
        sz = z * sig
        g = yv * sz
        r = lax.rsqrt(jnp.mean(g * g, axis=-1, keepdims=True) + RMS_EPS)
        ghat = g * r
        dn = d_ref[...]
        dghat = dn * w_ref[...]
        dg = r * (dghat - ghat * jnp.mean(dghat * ghat, axis=-1, keepdims=True))
        dy_ref[...] = dg * sz
        dz_ref[...] = (dg * yv * sig * (1.0 + z * (1.0 - sig))).astype(BF16)

        @pl.when(i == 0)
        def _():
            dw_ref[...] = jnp.zeros_like(dw_ref)

        dw_ref[...] += jnp.sum(dn * ghat, axis=0, keepdims=True)

    row_spec = pl.BlockSpec((tm, D_INNER), lambda i: (i, 0))
    vec_spec = pl.BlockSpec((1, D_INNER), lambda i: (0, 0))
    return pl.pallas_call(
        body, name=name,
        out_shape=(jax.ShapeDtypeStruct((rows, D_INNER), F32), jax.ShapeDtypeStruct((rows, D_INNER), BF16),
                   jax.ShapeDtypeStruct((1, D_INNER), F32)),
        grid=(rows // tm,), in_specs=[row_spec, row_spec, row_spec, vec_spec],
        out_specs=(row_spec, row_spec, vec_spec), compiler_params=_cparams(("arbitrary",)),
    )(dyn, y, zx, w)


def _split3(x):
    hi = x.astype(BF16)
    r1 = x - hi.astype(F32)
    mid = r1.astype(BF16)
    lo = (r1 - mid.astype(F32)).astype(BF16)
    return hi, mid, lo


def _dot3_data_lhs(x, sel):
    sel16 = sel.astype(F32).astype(BF16)
    hi, mid, lo = _split3(x)
    return _dot(hi, sel16) + _dot(mid, sel16) + _dot(lo, sel16)


def _dot3_data_rhs(sel, x):
    sel16 = sel.astype(F32).astype(BF16)
    hi, mid, lo = _split3(x)
    return _dot(sel16, hi) + _dot(sel16, mid) + _dot(sel16, lo)


def _causal_masks():
    r = lax.broadcasted_iota(jnp.int32, (CHUNK, CHUNK), 0)
    c = lax.broadcasted_iota(jnp.int32, (CHUNK, CHUNK), 1)
    return r >= c, r <= c


def _expand_heads_matrix(g):
    k = lax.broadcasted_iota(jnp.int32, (LANES, GROUP_W), 0)
    j = lax.broadcasted_iota(jnp.int32, (LANES, GROUP_W), 1)
    return HEADS_PER_GROUP * g + jnp.right_shift(j, 6) == k


def _reduce_heads_matrix(g):
    j = lax.broadcasted_iota(jnp.int32, (GROUP_W, LANES), 0)
    k = lax.broadcasted_iota(jnp.int32, (GROUP_W, LANES), 1)
    return HEADS_PER_GROUP * g + jnp.right_shift(j, 6) == k


def _reduce_pair_matrix(g, p):
    j = lax.broadcasted_iota(jnp.int32, (LANES, LANES), 0)
    k = lax.broadcasted_iota(jnp.int32, (LANES, LANES), 1)
    return HEADS_PER_GROUP * g + 2 * p + jnp.right_shift(j, 6) == k


def _group_cols(ref, g, width):
    return ref.at[:, pl.ds(g * width, width)]


def _ssd_prep(name, dt, a128, steps=()):
    rows = dt.shape[0]
    nc = rows // CHUNK

    def body(dt_ref, a_ref, dte_ref, acs_ref, acst_ref):
        causal, _ = _causal_masks()
        dtv = dt_ref[...]
        acs = _dot3_data_rhs(causal, dtv) * a_ref[...]
        acst_ref[...] = acs.T[0:SSM_HEADS]
        for g in range(N_GROUPS):
            expand = _expand_heads_matrix(g)
            _group_cols(dte_ref, g, GROUP_W)[...] = _dot3_data_lhs(dtv, expand)
            _group_cols(acs_ref, g, GROUP_W)[...] = _dot3_data_lhs(acs, expand)

    blk = pl.BlockSpec((CHUNK, D_INNER), lambda c: (c, 0))
    shp = jax.ShapeDtypeStruct((rows, D_INNER), F32)
    return _call(
        body, name=name, out_shape=(shp, shp, jax.ShapeDtypeStruct((nc, SSM_HEADS, CHUNK), F32)), grid=(nc,),
        in_specs=[pl.BlockSpec((CHUNK, LANES), lambda c: (c, 0)), pl.BlockSpec((1, LANES), lambda c: (0, 0))],
        out_specs=(blk, blk, pl.BlockSpec((None, SSM_HEADS, CHUNK), lambda c: (c, 0, 0))),
        operands=[dt, a128], semantics=("parallel",), steps=steps)


def _ssd_common(x_ref, b_ref, c_ref, dte_ref, acs_ref):
    x = x_ref[...]
    dt_exp = dte_ref[...]
    acs_exp = acs_ref[...]
    tot_exp = acs_ref[pl.ds(CHUNK - 1, 1), :]
    xdt = x * dt_exp
    e_exp = jnp.exp(acs_exp)
    f_exp = jnp.exp(tot_exp - acs_exp)
    return _causal_masks(), x, dt_exp, acs_exp, tot_exp, xdt, e_exp, f_exp, b_ref[...], c_ref[...]


def _pair_decay(acs_pair, acs_row, e, causal):
    lane = lax.broadcasted_iota(jnp.int32, (CHUNK, LANES), 1)
    mine = (lane < HEAD_DIM) if e == 0 else (lane >= HEAD_DIM)
    a_l = jnp.where(mine, acs_pair, pltpu.roll(acs_pair, HEAD_DIM, 1))
    seg = a_l - acs_row
    dm = jnp.where(causal[0], jnp.exp(jnp.minimum(seg, 0.0)), 0.0)
    dmt = jnp.where(causal[1], jnp.exp(jnp.minimum(-seg, 0.0)), 0.0)
    return dm, dmt


def _ssd_specs(index_of_chunk):
    wide = pl.BlockSpec((CHUNK, D_INNER), lambda c: (index_of_chunk(c), 0))
    b_spec = pl.BlockSpec((CHUNK, D_BC), lambda c: (index_of_chunk(c), D_INNER // D_BC))
    c_spec = pl.BlockSpec((CHUNK, D_BC), lambda c: (index_of_chunk(c), D_INNER // D_BC + 1))
    rows_spec = pl.BlockSpec((None, SSM_HEADS, CHUNK), lambda c: (index_of_chunk(c), 0, 0))
    state_spec = pl.BlockSpec((N_GROUPS, None, D_STATE, GROUP_W), lambda c: (0, index_of_chunk(c), 0, 0))
    return wide, b_spec, c_spec, rows_spec, state_spec


def _ssd_fwd(name, xbc, dt_exp, acs_exp, acs_rows, dskexp, steps=()):
    rows = xbc.shape[0]
    nc = rows // CHUNK

    def body(x_ref, b_ref, c_ref, dte_ref, acs_ref, acst_ref, dsk_ref, y_ref, st_ref, s_scr):
        @pl.when(pl.program_id(0) == 0)
        def _():
            s_scr[...] = jnp.zeros_like(s_scr)

        lane = lax.broadcasted_iota(jnp.int32, (CHUNK, LANES), 1)
        for g in range(N_GROUPS):
            y_g = _group_cols(y_ref, g, GROUP_W)
            causal, x, _, acs_exp_v, tot_exp, xdt, e_exp, f_exp, bm, cm = _ssd_common(
                _group_cols(x_ref, g, GROUP_W), _group_cols(b_ref, g, D_STATE), _group_cols(c_ref, g, D_STATE),
                _group_cols(dte_ref, g, GROUP_W), _group_cols(acs_ref, g, GROUP_W))
            state = s_scr[g]
            st_ref[g] = state
            cb16, bb16 = cm.astype(BF16), bm.astype(BF16)
            cb = _dot_nt(cb16, bb16)
            base = e_exp * _dot(cb16, state.astype(BF16)) + _group_cols(dsk_ref, g, GROUP_W)[...] * x
            for p in range(HEADS_PER_GROUP // 2):
                sl = slice(p * LANES, (p + 1) * LANES)
                xp = xdt[:, sl].astype(BF16)
                yd = []
                for e in range(2):
                    acs_row = acst_ref[pl.ds(g * HEADS_PER_GROUP + 2 * p + e, 1), :]
                    dm, _ = _pair_decay(acs_exp_v[:, sl], acs_row, e, causal)
                    yd.append(_dot((cb * dm).astype(BF16), xp))
                y_g[:, sl] = jnp.where(lane < HEAD_DIM, yd[0], yd[1]) + base[:, sl]
            s_scr[g] = jnp.exp(tot_exp) * state + _dot_tn(bb16, (f_exp * xdt).astype(BF16))

    wide, b_spec, c_spec, rows_spec, state_spec = _ssd_specs(lambda c: c)
    return _call(
        body, name=name,
        out_shape=(jax.ShapeDtypeStruct((rows, D_INNER), F32),
                   jax.ShapeDtypeStruct((N_GROUPS, nc, D_STATE, GROUP_W), F32)),
        grid=(nc,),
        in_specs=[wide, b_spec, c_spec, wide, wide, rows_spec, pl.BlockSpec((1, D_INNER), lambda c: (0, 0))],
        out_specs=(wide, state_spec),
        scratch_shapes=[pltpu.VMEM((N_GROUPS, D_STATE, GROUP_W), F32)],
        operands=[xbc, xbc, xbc, dt_exp, acs_exp, acs_rows, dskexp], semantics=("arbitrary",), steps=steps)


def _ssd_bwd(name, xbc, dt_exp, acs_exp, acs_rows, dt, a128, dskexp, dy, states, steps=()):
    rows = xbc.shape[0]
    nc = rows // CHUNK
    last = nc - 1

    def body(x_ref, b_ref, c_ref, dte_ref, acs_ref, acst_ref, dt_ref, a128_ref, dsk_all, dy_all, st_all,
             dx_all, db_all, dc_all, ddt_ref, dalog_ref, ddsk_ref, ds_all):
        @pl.when(pl.program_id(0) == 0)
        def _():
            ds_all[...] = jnp.zeros_like(ds_all)
            dalog_ref[...] = jnp.zeros_like(dalog_ref)
            ddsk_ref[...] = jnp.zeros_like(ddsk_ref)

        dacs = jnp.zeros((CHUNK, LANES), F32)
        ddt_x = jnp.zeros((CHUNK, LANES), F32)
        for g in range(N_GROUPS):
            dacs_g, ddt_x_g = group(
                g, _group_cols(x_ref, g, GROUP_W), _group_cols(b_ref, g, D_STATE), _group_cols(c_ref, g, D_STATE),
                _group_cols(dte_ref, g, GROUP_W), _group_cols(acs_ref, g, GROUP_W), acst_ref,
                _group_cols(dsk_all, g, GROUP_W), _group_cols(dy_all, g, GROUP_W), st_all.at[g],
                _group_cols(dx_all, g, GROUP_W), _group_cols(db_all, g, D_STATE), _group_cols(dc_all, g, D_STATE),
                ddsk_ref, ds_all.at[g])
            dacs, ddt_x = dacs + dacs_g, ddt_x + ddt_x_g
        _, causal_t = _causal_masks()
        da = _dot3_data_rhs(causal_t, dacs)
        ddt_ref[...] = da * a128_ref[...] + ddt_x
        dalog_ref[...] += jnp.sum(da * dt_ref[...], axis=0, keepdims=True) * a128_ref[...]

    def group(g, x_ref, b_ref, c_ref, dte_ref, acs_ref, acst_ref, dsk_ref, dy_ref, st_ref,
              dx_ref, db_ref, dc_ref, ddsk_ref, ds_scr):
        causal, x, dt_exp, acs_exp_v, tot_exp, xdt, e_exp, f_exp, bm, cm = _ssd_common(
            x_ref, b_ref, c_ref, dte_ref, acs_ref)
        reduce_heads = _reduce_heads_matrix(g)
        state, dstate = st_ref[...], ds_scr[...]
        dyv = dy_ref[...]
        cb16, bb16 = cm.astype(BF16), bm.astype(BF16)
        s16, ds16 = state.astype(BF16), dstate.astype(BF16)
        cb = _dot_nt(cb16, bb16)
        cbt = _dot_nt(bb16, cb16)
        cs = _dot(cb16, s16)
        bds = _dot(bb16, ds16)
        edy = e_exp * dyv
        fx = f_exp * xdt
        dxdt_base = f_exp * bds
        dc_acc = _dot_nt(edy.astype(BF16), s16)
        db_acc = _dot_nt(fx.astype(BF16), ds16)
        ds_scr[...] = jnp.exp(tot_exp) * dstate + _dot_tn(cb16, edy.astype(BF16))
        q = fx * bds
        dacs = _dot3_data_lhs(edy * cs - q, reduce_heads)
        dtot = jnp.sum(_dot3_data_lhs(q + jnp.exp(tot_exp) * dstate * state, reduce_heads), axis=0, keepdims=True)
        ddsk_ref[...] += jnp.sum(_dot3_data_lhs(dyv * x, reduce_heads), axis=0, keepdims=True)
        lane = lax.broadcasted_iota(jnp.int32, (CHUNK, LANES), 1)
        dcb = jnp.zeros((CHUNK, CHUNK), F32)
        dcbt = jnp.zeros((CHUNK, CHUNK), F32)
        ddt_x = jnp.zeros((CHUNK, LANES), F32)
        for p in range(HEADS_PER_GROUP // 2):
            sl = slice(p * LANES, (p + 1) * LANES)
            xp, dyp = xdt[:, sl], dyv[:, sl]
            xp16, dyp16 = xp.astype(BF16), dyp.astype(BF16)
            dxh = []
            for e in range(2):
                h = 2 * p + e
                mine = (lane < HEAD_DIM) if e == 0 else (lane >= HEAD_DIM)
                acs_row = acst_ref[pl.ds(g * HEADS_PER_GROUP + h, 1), :]
                dm, dmt = _pair_decay(acs_exp_v[:, sl], acs_row, e, causal)
                m, mt = cb * dm, cbt * dmt
                xh16 = jnp.where(mine, xp, 0.0).astype(BF16)
                dyh16 = jnp.where(mine, dyp, 0.0).astype(BF16)
                d_m = _dot_nt(dyh16, xp16)
                d_mt = _dot_nt(xh16, dyp16)
                dacs_h = (jnp.sum(d_m * m, axis=-1, keepdims=True)
                          - jnp.sum(d_mt * mt, axis=-1, keepdims=True))
                dacs = dacs + jnp.where(lane == HEADS_PER_GROUP * g + h, dacs_h, 0.0)
                dcb = dcb + d_m * dm
                dcbt = dcbt + d_mt * dmt
                dxh.append(_dot(mt.astype(BF16), dyp16))
            dxdt = jnp.where(lane < HEAD_DIM, dxh[0], dxh[1]) + dxdt_base[:, sl]
            dx_ref[:, sl] = dxdt * dt_exp[:, sl] + dsk_ref[:, sl] * dyp
            ddt_x = ddt_x + _dot3_data_lhs(dxdt * x[:, sl], _reduce_pair_matrix(g, p))
        dc_ref[...] = dc_acc + _dot(dcb.astype(BF16), bb16)
        db_ref[...] = db_acc + _dot(dcbt.astype(BF16), cb16)
        row = lax.broadcasted_iota(jnp.int32, (CHUNK, LANES), 0)
        return dacs + jnp.where(row == CHUNK - 1, dtot, 0.0), ddt_x

    wide, b_spec, c_spec, rows_spec, state_spec = _ssd_specs(lambda c: last - c)
    heads_spec = pl.BlockSpec((CHUNK, LANES), lambda c: (last - c, 0))
    vec_spec = pl.BlockSpec((1, LANES), lambda c: (0, 0))
    bc_out = pl.BlockSpec((CHUNK, D_BC), lambda c: (last - c, 0))
    vec_shape = jax.ShapeDtypeStruct((1, LANES), F32)
    return _call(
        body, name=name,
        out_shape=(jax.ShapeDtypeStruct((rows, D_INNER), F32), jax.ShapeDtypeStruct((rows, D_BC), F32),
                   jax.ShapeDtypeStruct((rows, D_BC), F32), jax.ShapeDtypeStruct((rows, LANES), F32),
                   vec_shape, vec_shape),
        grid=(nc,),
        in_specs=[wide, b_spec, c_spec, wide, wide, rows_spec, heads_spec, vec_spec,
                  pl.BlockSpec((1, D_INNER), lambda c: (0, 0)), wide, state_spec],
        out_specs=(wide, bc_out, bc_out, heads_spec, vec_spec, vec_spec),
        scratch_shapes=[pltpu.VMEM((N_GROUPS, D_STATE, GROUP_W), F32)],
        operands=[xbc, xbc, xbc, dt_exp, acs_exp, acs_rows, dt, a128, dskexp, dy, states],
        semantics=("arbitrary",), steps=steps)


def _attn_visible(b, heads=1):
    row = jnp.bitwise_and(lax.broadcasted_iota(jnp.int32, (heads * CHUNK, 3 * CHUNK), 0), CHUNK - 1)
    col = lax.broadcasted_iota(jnp.int32, (heads * CHUNK, 3 * CHUNK), 1)
    bb = b + jnp.zeros_like(col)
    meta = (col < CHUNK) & (bb >= 1) & (col >= PAD_ROWS)
    prev = (col >= CHUNK) & (col < 2 * CHUNK) & (bb >= 2) & ((col - CHUNK) > row)
    cur = (col >= 2 * CHUNK) & ((col - 2 * CHUNK) <= row) & ((bb >= 1) | ((col - 2 * CHUNK) >= PAD_ROWS))
    return meta | prev | cur


def _attn_visible4(b):
    return _attn_visible(b, 4)


def _stack_heads(q_ref, sink_ref, kvh, scale):
    lane = lax.broadcasted_iota(jnp.int32, (CHUNK, LANES), 1)
    parts, sinks = [], []
    for pp in range(2):
        pair = kvh * 2 + pp
        qp = q_ref[:, pair * LANES:(pair + 1) * LANES] * scale
        for e in range(2):
            mine = (lane < HEAD_DIM) if e == 0 else (lane >= HEAD_DIM)
            parts.append(jnp.where(mine, qp, 0.0).astype(BF16))
            sinks.append(jnp.full((CHUNK, 1), sink_ref[2 * pair + e], F32))
    return jnp.concatenate(parts, axis=0), jnp.concatenate(sinks, axis=0)


def _attn_operands(q_ref, k0, kp, kc, v0, vp, vc, sink_ref):
    kcat, vcat, q4, sink4 = [], [], [], []
    for kvh in range(N_KV_HEADS):
        ksl = slice(kvh * LANES, (kvh + 1) * LANES)
        kcat.append(jnp.concatenate([k0[:, ksl], kp[:, ksl], kc[:, ksl]], axis=0).astype(BF16))
        vcat.append(jnp.concatenate([v0[:, ksl], vp[:, ksl], vc[:, ksl]], axis=0).astype(BF16))
        stacked, sinks = _stack_heads(q_ref, sink_ref, kvh, ATTN_SCALE)
        q4.append(stacked)
        sink4.append(sinks)
    return kcat, vcat, q4, sink4


def _attn_probs(q4, kcat, visible, sink4):
    heads = range(N_KV_HEADS)
    s = [jnp.where(visible, _dot_nt(q4[h], kcat[h]), NEG_INF) for h in heads]
    m = [jnp.maximum(jnp.max(s[h], axis=-1, keepdims=True), sink4[h]) for h in heads]
    pe = [jnp.exp(s[h] - m[h]) for h in heads]
    pe_sink = [jnp.exp(sink4[h] - m[h]) for h in heads]
    inv = [1.0 / (jnp.sum(pe[h], axis=-1, keepdims=True) + pe_sink[h]) for h in heads]
    return [pe[h] * inv[h] for h in heads], [pe_sink[h] * inv[h] for h in heads]


def _unstack_pairs(stacked, pp):
    lane = lax.broadcasted_iota(jnp.int32, (CHUNK, LANES), 1)
    return jnp.where(lane < HEAD_DIM, stacked[(2 * pp) * CHUNK:(2 * pp + 1) * CHUNK],
                     stacked[(2 * pp + 1) * CHUNK:(2 * pp + 2) * CHUNK])


def _attn_specs(colblock):
    blk = lambda f: pl.BlockSpec((CHUNK, 2 * D_KV), f)
    return [blk(lambda b: (0, colblock)), blk(lambda b: (jnp.maximum(b - 1, 0), colblock)), blk(lambda b: (b, colblock))]


def _attn_fwd(name, q, kv2, sinks, steps=()):
    rows = q.shape[0]

    def body(q_ref, k0, kp, kc, v0, vp, vc, sink_ref, o_ref):
        visible = _attn_visible4(pl.program_id(0))
        kcat, vcat, q4, sink4 = _attn_operands(q_ref, k0, kp, kc, v0, vp, vc, sink_ref)
        pn, _ = _attn_probs(q4, kcat, visible, sink4)
        o4 = [_dot(pn[h].astype(BF16), vcat[h]) for h in range(N_KV_HEADS)]
        for kvh in range(N_KV_HEADS):
            for pp in range(2):
                qsl = slice((kvh * 2 + pp) * LANES, (kvh * 2 + pp + 1) * LANES)
                o_ref[:, qsl] = _unstack_pairs(o4[kvh], pp).astype(BF16)

    return _call(
        body, name=name, out_shape=jax.ShapeDtypeStruct((rows, D_MODEL), BF16), grid=(rows // CHUNK,),
        in_specs=[pl.BlockSpec((CHUNK, D_MODEL), lambda b: (b, 0))] + _attn_specs(0) + _attn_specs(1)
        + [pl.BlockSpec(memory_space=pltpu.SMEM)],
        out_specs=pl.BlockSpec((CHUNK, D_MODEL), lambda b: (b, 0)),
        operands=[q, kv2, kv2, kv2, kv2, kv2, kv2, sinks], semantics=("parallel",), steps=steps)


def _attn_bwd(name, q, kv2, sinks, do, steps=()):
    rows = q.shape[0]

    def body(q_ref, k0, kp, kc, v0, vp, vc, sink_ref, do_ref,
             dq_ref, dkc_ref, dkp_ref, dvc_ref, dvp_ref, dkm_ref, dvm_ref, dsink_ref):
        @pl.when(pl.program_id(0) == 0)
        def _():
            dkm_ref[...] = jnp.zeros_like(dkm_ref)
            dvm_ref[...] = jnp.zeros_like(dvm_ref)
            dsink_ref[...] = jnp.zeros_like(dsink_ref)

        visible = _attn_visible4(pl.program_id(0))
        heads = range(N_KV_HEADS)
        lane1 = lax.broadcasted_iota(jnp.int32, (1, LANES), 1)
        kcat, vcat, q4, sink4 = _attn_operands(q_ref, k0, kp, kc, v0, vp, vc, sink_ref)
        do4 = [_stack_heads(do_ref, sink_ref, h, 1.0)[0] for h in heads]
        pn, psink = _attn_probs(q4, kcat, visible, sink4)
        dp = [_dot_nt(do4[h], vcat[h]) for h in heads]
        delta = [jnp.sum(pn[h] * dp[h], axis=-1, keepdims=True) for h in heads]
        ds16 = [(pn[h] * (dp[h] - delta[h])).astype(BF16) for h in heads]
        dq4 = [_dot(ds16[h], kcat[h]) for h in heads]
        dk_acc = [_dot_tn(ds16[h], q4[h]) for h in heads]
        dv_acc = [_dot_tn(pn[h].astype(BF16), do4[h]) for h in heads]
        dsink = jnp.zeros((1, LANES), F32)
        for kvh in heads:
            ksl = slice(kvh * LANES, (kvh + 1) * LANES)
            sink_terms = psink[kvh] * delta[kvh]
            for j in range(4):
                part = jnp.sum(sink_terms[j * CHUNK:(j + 1) * CHUNK], axis=0, keepdims=True)
                dsink = dsink - jnp.where(lane1 == kvh * 4 + j, part, 0.0)
            for pp in range(2):
                qsl = slice((kvh * 2 + pp) * LANES, (kvh * 2 + pp + 1) * LANES)
                dq_ref[:, qsl] = (_unstack_pairs(dq4[kvh], pp) * ATTN_SCALE).astype(BF16)
            dkm_ref[:, ksl] += dk_acc[kvh][0:CHUNK]
            dvm_ref[:, ksl] += dv_acc[kvh][0:CHUNK]
            dkp_ref[:, ksl] = dk_acc[kvh][CHUNK:2 * CHUNK]
            dvp_ref[:, ksl] = dv_acc[kvh][CHUNK:2 * CHUNK]
            dkc_ref[:, ksl] = dk_acc[kvh][2 * CHUNK:3 * CHUNK]
            dvc_ref[:, ksl] = dv_acc[kvh][2 * CHUNK:3 * CHUNK]
        dsink_ref[...] += dsink

    qspec = pl.BlockSpec((CHUNK, D_MODEL), lambda b: (b, 0))
    kvspec = pl.BlockSpec((CHUNK, 2 * D_KV), lambda b: (b, 0))
    fixed = pl.BlockSpec((CHUNK, 2 * D_KV), lambda b: (0, 0))
    kv_shape = jax.ShapeDtypeStruct((rows, 2 * D_KV), F32)
    meta_shape = jax.ShapeDtypeStruct((CHUNK, 2 * D_KV), F32)
    return _call(
        body, name=name,
        out_shape=(jax.ShapeDtypeStruct((rows, D_MODEL), BF16), kv_shape, kv_shape, kv_shape, kv_shape,
                   meta_shape, meta_shape, jax.ShapeDtypeStruct((1, LANES), F32)),
        grid=(rows // CHUNK,),
        in_specs=[qspec] + _attn_specs(0) + _attn_specs(1) + [pl.BlockSpec(memory_space=pltpu.SMEM), qspec],
        out_specs=(qspec, kvspec, kvspec, kvspec, kvspec, fixed, fixed, pl.BlockSpec((1, LANES), lambda b: (0, 0))),
        operands=[q, kv2, kv2, kv2, kv2, kv2, kv2, sinks, do], semantics=("arbitrary",), steps=steps)


def _kv_grad_combine(name, dk_cur, dk_prev, dk_meta, dv_cur, dv_prev, dv_meta):
    rows = dk_cur.shape[0]
    nb = rows // CHUNK
    width = 2 * D_KV

    def body(kc_ref, kp_ref, km_ref, vc_ref, vp_ref, vm_ref, o_ref):
        jj = pl.program_id(0) + jnp.zeros((CHUNK, 1), jnp.int32)
        for half, (c_ref, p_ref, m_ref) in enumerate(((kc_ref, kp_ref, km_ref), (vc_ref, vp_ref, vm_ref))):
            total = c_ref[...] + jnp.where(jj < nb - 1, p_ref[...], 0.0) + jnp.where(jj == 0, m_ref[...], 0.0)
            o_ref[:, half * width:(half + 1) * width] = total.astype(BF16)

    blk = lambda f: pl.BlockSpec((CHUNK, width), f)
    three = lambda: [blk(lambda j: (j, 0)), blk(lambda j: (jnp.minimum(j + 1, nb - 1), 0)), blk(lambda j: (0, 0))]
    return pl.pallas_call(
        body, name=name, out_shape=jax.ShapeDtypeStruct((rows, 2 * width), BF16), grid=(nb,),
        in_specs=three() + three(), out_specs=pl.BlockSpec((CHUNK, 2 * width), lambda j: (j, 0)),
        compiler_params=_cparams(("parallel",)),
    )(dk_cur, dk_prev, dk_meta, dv_cur, dv_prev, dv_meta)


def _adamw(name, w, g, m, v, steps=()):
    rows, width = w.shape
    tr = rows
    for cand in range(8, rows + 1, 8):
        if rows % cand == 0 and cand * width * 4 <= (1 << 20):
            tr = cand

    def body(w_ref, g_ref, m_ref, v_ref, d_ref, mo_ref, vo_ref):
        gv = g_ref[...]
        mn = ADAM_B1 * m_ref[...] + (1.0 - ADAM_B1) * gv
        vn = ADAM_B2 * v_ref[...] + (1.0 - ADAM_B2) * (gv * gv)
        m_hat = mn / (1.0 - ADAM_B1 ** ADAM_STEP)
        v_hat = vn / (1.0 - ADAM_B2 ** ADAM_STEP)
        d_ref[...] = -ADAM_LR * (m_hat / (jnp.sqrt(v_hat) + ADAM_EPS) + ADAM_WD * w_ref[...])
        mo_ref[...] = mn
        vo_ref[...] = vn

    blk = pl.BlockSpec((tr, width), lambda i: (i, 0))
    shp = jax.ShapeDtypeStruct((rows, width), F32)
    return _call(body, name=name, out_shape=(shp, shp, shp), grid=(rows // tr,), in_specs=[blk] * 4,
                 out_specs=(blk,) * 3, operands=[w, g, m, v], semantics=("parallel",), steps=steps)


class _GivenWeights:
    def __init__(self, p):
        self.p = p
        self.grads = {}

    def weight(self, name, layer=None):
        return self.p[name] if layer is None else self.p[name][layer]

    def steps(self, kernel):
        return ()

    def grad(self, name, layer, g):
        self.grads[(name, layer)] = g


def _ffn_fwd(tag, h, hn, p, i, plan):
    up = _mm_nn_bychip(f"ffn{tag}_up", hn, plan.weight("f_w_up", i), steps=plan.steps(f"ffn{tag}_up"))
    act = _ffn_conv_fwd(f"ffn{tag}_conv", up, p["f_conv_w"][i], p["f_conv_b"][i:i + 1], steps=plan.steps(f"ffn{tag}_conv"))
    pre = _mm(f"ffn{tag}_down", act, plan.weight("f_w_down", i), "nn")
    return pre, (h, hn, up, act, pre)


def _ffn_bwd(tag, dpre, saved, p, i, plan):
    h, hn, up, act, pre = saved
    plan.grad("f_w_down", i, _mm(f"ffn{tag}_down_dw", act, dpre, "tn", out_dtype=BF16))
    dact = _mm(f"ffn{tag}_down_dx", dpre, plan.weight("f_w_down", i), "nt", steps=plan.steps(f"ffn{tag}_down_dx"))
    dug, duv, gwg, gwv, gbg, gbv = _ffn_conv_bwd(f"ffn{tag}_conv_bwd", up, dact, p["f_conv_w"][i], p["f_conv_b"][i:i + 1],
                                                 steps=plan.steps(f"ffn{tag}_conv_bwd"))
    g_cw, g_cb = jnp.concatenate([gwg, gwv], axis=1), jnp.concatenate([gbg, gbv], axis=1)
    w_up = plan.weight("f_w_up", i)
    n = w_up.shape[2]
    dhn = _mm_nt_bychip(f"ffn{tag}_up_dx_gate", dug, w_up, 0)
    dhn = _mm_nt_bychip(f"ffn{tag}_up_dx_val", duv, w_up, N_CHIPS // 2, acc=dhn)
    g_up = _mm_tn_bychip(f"ffn{tag}_up_dw_gate", hn, dug, n, 0)
    plan.grad("f_w_up", i, _mm_tn_bychip(f"ffn{tag}_up_dw_val", hn, duv, n, N_CHIPS // 2, into=g_up))
    return dhn, dict(f_conv_w=g_cw, f_conv_b=g_cb)


def _lanes_pad(a, width=LANES):
    return jnp.pad(a, [(0, 0)] * (a.ndim - 1) + [(0, width - a.shape[-1])])


def _dup_heads(w):
    rows = w.shape[0]
    w = w.reshape(rows, 2 * N_KV_HEADS, 1, HEAD_DIM)
    return jnp.broadcast_to(w, (rows, 2 * N_KV_HEADS, 2, HEAD_DIM)).reshape(rows, 4 * D_KV)


def _undup_heads(g):
    rows = g.shape[0]
    return g.reshape(rows, 2 * N_KV_HEADS, 2, HEAD_DIM).sum(axis=2).reshape(rows, 2 * D_KV)


def _local_step(x2, target, p, plan):
    seq = x2.shape[0]
    rows = seq + CHUNK
    g = {}

    h0 = jnp.concatenate([jnp.zeros((PAD_ROWS, D_MODEL), F32), p["meta_tokens"], x2], axis=0)

    w_in = plan.weight("a_w_in")
    w_dt = jnp.pad(w_in[D_MAIN:], ((0, LANES - SSM_HEADS), (0, 0)))
    dt_bias = _lanes_pad(p["a_dt_bias"])
    a128 = _lanes_pad(-jnp.exp(p["a_a_log"]))
    dskexp = jnp.repeat(p["a_d_skip"].reshape(SSM_HEADS), HEAD_DIM).reshape(1, D_INNER)

    hn0 = _rms_fwd("a_norm", h0, p["a_norm_pre"])
    zx = _mm("a_in_main", hn0, w_in, "nt", k_rows=D_MAIN, steps=plan.steps("a_in_main"))
    dtr = _mm("a_in_dt", hn0, w_dt, "nt")
    xbc = _conv4_fwd("a_conv", zx, p["a_conv_w"], p["a_conv_b"], steps=plan.steps("a_conv"))
    dt = _dt_fwd("a_dt", dtr, dt_bias)
    dt_exp, acs_exp, acs_rows = _ssd_prep("a_ssd_prep", dt, a128, steps=plan.steps("a_ssd_prep"))
    y, states = _ssd_fwd("a_ssd", xbc, dt_exp, acs_exp, acs_rows, dskexp, steps=plan.steps("a_ssd"))
    yn = _gate_fwd("a_gate", y, zx, p["a_gate_norm"], steps=plan.steps("a_gate"))
    mix = _mm("a_out", yn, plan.weight("a_w_out"), "nn", steps=plan.steps("a_out"))
    h1, (hn_f0,) = _resid_norm_fwd("a_resid", h0, mix, p["a_norm_post"], [p["f_norm_pre"][0:1]])

    pre_f0, ffn0 = _ffn_fwd("0", h1, hn_f0, p, 0, plan)
    h2, (hkv, hn2) = _resid_norm_fwd("ffn0_resid", h1, pre_f0, p["f_norm_post"][0:1], [p["kv_norm"], p["b_norm_pre"]])

    w_kv2 = _dup_heads(plan.weight("w_kv"))
    kv2 = _mm("kv_proj", hkv, w_kv2, "nn")
    q = _mm("b_q", hn2, plan.weight("b_w_q"), "nn")
    sinks = p["b_sinks"].reshape(N_Q_HEADS)
    o = _attn_fwd("b_attn", q, kv2, sinks, steps=plan.steps("b_attn"))
    attn = _mm("b_o", o, plan.weight("b_w_o"), "nn", steps=plan.steps("b_o"))
    h3, (hn_f1,) = _resid_norm_fwd("b_resid", h2, attn, p["b_norm_post"], [p["f_norm_pre"][1:2]])

    pre_f1, ffn1 = _ffn_fwd("1", h3, hn_f1, p, 1, plan)
    dh, loss_vec, dpre_f1, g_post1 = _resid_norm_loss("ffn1_resid_loss", h3, pre_f1, p["f_norm_post"][1:2], target)
    loss = loss_vec[0, 0]

    dhn_f1, g1 = _ffn_bwd("1", dpre_f1, ffn1, p, 1, plan)
    dh, g_pre1, dpre, g["b_norm_post"] = _norm_bwd_add("ffn1_norm_bwd", dh, dhn_f1, h3, p["f_norm_pre"][1:2],
                                                        then=(attn, p["b_norm_post"]))
    plan.grad("b_w_o", None, _mm("b_o_dw", o, dpre, "tn", out_dtype=BF16))
    do = _mm("b_o_dx", dpre, plan.weight("b_w_o"), "nt", steps=plan.steps("b_o_dx"))
    dq, dkc, dkp, dvc, dvp, dkm, dvm, dsink = _attn_bwd("b_attn_bwd", q, kv2, sinks, do, steps=plan.steps("b_attn_bwd"))
    g["b_sinks"] = dsink[:, :N_Q_HEADS]
    dhn2 = _mm("b_q_dx", dq, plan.weight("b_w_q"), "nt")
    plan.grad("b_w_q", None, _mm("b_q_dw", hn2, dq, "tn", out_dtype=BF16))
    dh, g["b_norm_pre"] = _norm_bwd_add("b_norm_bwd", dh, dhn2, h2, p["b_norm_pre"])
    dkv2 = _kv_grad_combine("kv_grad", dkc, dkp, dkm, dvc, dvp, dvm)
    dhkv = _mm("kv_proj_dx", dkv2, w_kv2, "nt")
    plan.grad("w_kv", None, _undup_heads(_mm("kv_proj_dw", hkv, dkv2, "tn")))
    dh, g["kv_norm"], dpre_f0, g_post0 = _norm_bwd_add("kv_norm_bwd", dh, dhkv, h2, p["kv_norm"],
                                                       then=(pre_f0, p["f_norm_post"][0:1]))

    dhn_f0, g0 = _ffn_bwd("0", dpre_f0, ffn0, p, 0, plan)
    dh, g_pre0, dpre, g["a_norm_post"] = _norm_bwd_add("ffn0_norm_bwd", dh, dhn_f0, h1, p["f_norm_pre"][0:1],
                                                        then=(mix, p["a_norm_post"]))
    g["f_norm_post"] = jnp.concatenate([g_post0, g_post1], axis=0)
    g["f_norm_pre"] = jnp.concatenate([g_pre0, g_pre1], axis=0)
    g["f_conv_w"] = jnp.stack([g0["f_conv_w"], g1["f_conv_w"]])
    g["f_conv_b"] = jnp.concatenate([g0["f_conv_b"], g1["f_conv_b"]], axis=0)
    plan.grad("a_w_out", None, _mm("a_out_dw", yn, dpre, "tn", out_dtype=BF16))
    dyn = _mm("a_out_dx", dpre, plan.weight("a_w_out"), "nt", steps=plan.steps("a_out_dx"))
    dy, dz, g["a_gate_norm"] = _gate_bwd("a_gate_bwd", dyn, y, zx, p["a_gate_norm"])
    dxs, dbm, dcm, ddt, dalog, ddsk = _ssd_bwd("a_ssd_bwd", xbc, dt_exp, acs_exp, acs_rows, dt, a128, dskexp, dy, states,
                                              steps=plan.steps("a_ssd_bwd"))
    g["a_a_log"] = dalog[:, :SSM_HEADS]
    g["a_d_skip"] = ddsk[:, :SSM_HEADS]
    ddtr, dbias = _dt_bwd("a_dt_bwd", ddt, dtr, dt_bias)
    g["a_dt_bias"] = dbias[:, :SSM_HEADS]
    dxp, gw_x, gb_x = _conv4_bwd("a_conv_bwd_x", zx, dxs, p["a_conv_w"], p["a_conv_b"], 0)
    dbp, gw_b, gb_b = _conv4_bwd("a_conv_bwd_b", zx, dbm, p["a_conv_w"], p["a_conv_b"], D_INNER)
    dcp, gw_c, gb_c = _conv4_bwd("a_conv_bwd_c", zx, dcm, p["a_conv_w"], p["a_conv_b"], D_INNER + D_BC)
    g["a_conv_w"] = jnp.concatenate([gw_x, gw_b, gw_c], axis=1)
    g["a_conv_b"] = jnp.concatenate([gb_x, gb_b, gb_c], axis=1)
    dzx = jnp.concatenate([dz, dxp, dbp, dcp], axis=1)
    g_in = _mm("a_in_main_dw", dzx, hn0, "tn", out_dtype=BF16, out_rows=D_IN_PROJ, steps=plan.steps("a_in_main_dw"))
    plan.grad("a_w_in", None, _tn_rows_into("a_in_dt_dw", ddtr, hn0, g_in, D_MAIN, SSM_HEADS))
    dhn0 = _mm("a_in_dt_dx", ddtr, w_dt, "nn", steps=plan.steps("a_in_dt_dx"))
    dhn0 = _mm("a_in_main_dx", dzx, w_in, "nn", acc=dhn0, steps=plan.steps("a_in_main_dx"))
    dh, g["a_norm_pre"] = _norm_bwd_add("a_norm_bwd", dh, dhn0, h0, p["a_norm_pre"], steps=plan.steps("a_norm_bwd"))

    g["meta_tokens"] = dh[PAD_ROWS:CHUNK]
    return loss, dh[CHUNK:], g


ANY = pl.BlockSpec(memory_space=pl.ANY)
VMEM_SPEC = pl.BlockSpec(memory_space=pltpu.VMEM)


def _allgather_small(name, shard):
    rows = shard.shape[0]

    def body(s_ref, o_ref, send_sems, recv_sems):
        x, y, c = _place()
        me = 2 * x + y
        o_ref[me] = s_ref[...]
        chips = _other_chips(x, y)
        sends = [pltpu.make_async_remote_copy(s_ref, o_ref.at[me], send_sems.at[j], recv_sems.at[j],
                                              device_id=(cx, cy, c), device_id_type=MESH)
                 for j, (cx, cy) in enumerate(chips)]
        for cp in sends:
            cp.start()
        for j, (cx, cy) in enumerate(chips):
            pltpu.make_async_remote_copy(s_ref, o_ref.at[2 * cx + cy], send_sems.at[j], recv_sems.at[j],
                                         device_id=(cx, cy, c), device_id_type=MESH).wait_recv()
        for cp in sends:
            cp.wait_send()

    return pl.pallas_call(
        body, name=name, out_shape=jax.ShapeDtypeStruct((N_CHIPS, rows, LANES), F32),
        in_specs=[VMEM_SPEC], out_specs=VMEM_SPEC,
        scratch_shapes=[pltpu.SemaphoreType.DMA((3,)), pltpu.SemaphoreType.DMA((3,))],
        compiler_params=pltpu.CompilerParams(vmem_limit_bytes=VMEM_LIMIT),
    )(shard)


def _row_block(rows, width, itemsize, align, budget=2 << 20):
    best = rows
    for cand in range(align, rows + 1, align):
        if rows % cand == 0 and cand * width * itemsize <= budget:
            best = cand
    return best


def _cast_into_slot(name, chip, w, layer=None):
    rows, width = w.shape[-2:]
    tr = _row_block(rows, width, 4, 16)
    if layer is None:
        in_spec = pl.BlockSpec((tr, width), lambda i, chip_ref: (i, 0))
    else:
        in_spec = pl.BlockSpec((None, tr, width), lambda i, chip_ref: (layer, i, 0))

    def body(chip_ref, w_ref, o_ref):
        o_ref[...] = w_ref[...].astype(BF16)

    return pl.pallas_call(
        body, name=name, out_shape=jax.ShapeDtypeStruct((N_CHIPS, rows, width), BF16),
        grid_spec=pltpu.PrefetchScalarGridSpec(
            num_scalar_prefetch=1, grid=(rows // tr,), in_specs=[in_spec],
            out_specs=pl.BlockSpec((None, tr, width), lambda i, chip_ref: (chip_ref[0], i, 0))),
        compiler_params=_cparams(("parallel",)),
    )(chip, w)


def _allreduce_small(name, vec):
    rows = vec.shape[0]

    def body(v_ref, o_ref, buf, send_sems, recv_sems):
        x, y, c = _place()
        me = 4 * x + 2 * y + c
        buf[me] = v_ref[...]

        def peer(k):
            kx, ky, kc = (k >> 2) & 1, (k >> 1) & 1, k & 1
            return (1 - x if kx else x, 1 - y if ky else y, 1 - c if kc else c)

        sends = []
        for k in range(1, N_DEV):
            cp = pltpu.make_async_remote_copy(v_ref, buf.at[me], send_sems.at[k - 1], recv_sems.at[k - 1],
                                              device_id=peer(k), device_id_type=MESH)
            cp.start()
            sends.append(cp)
        for k in range(1, N_DEV):
            px, py, pc = peer(k)
            pltpu.make_async_remote_copy(v_ref, buf.at[4 * px + 2 * py + pc], send_sems.at[k - 1], recv_sems.at[k - 1],
                                         device_id=(px, py, pc), device_id_type=MESH).wait_recv()
        for cp in sends:
            cp.wait_send()
        acc = buf[0]
        for d in range(1, N_DEV):
            acc = acc + buf[d]
        o_ref[...] = acc

    return pl.pallas_call(
        body, name=name, out_shape=jax.ShapeDtypeStruct((rows, LANES), F32),
        in_specs=[VMEM_SPEC], out_specs=VMEM_SPEC,
        scratch_shapes=[pltpu.VMEM((N_DEV, rows, LANES), F32), pltpu.SemaphoreType.DMA((N_DEV - 1,)),
                        pltpu.SemaphoreType.DMA((N_DEV - 1,))],
        compiler_params=pltpu.CompilerParams(vmem_limit_bytes=VMEM_LIMIT),
    )(vec)


def _rs_pair_add(name, place, grads, partner, split="rows"):
    _, half_rows, width = partner.shape
    tr = _row_block(half_rows, width, 2, 16)
    nb = half_rows // tr
    if split == "rows":
        mine = pl.BlockSpec((None, tr, width), lambda s, i, pr: (s, pr[1] * nb + i, 0))
    else:
        mine = pl.BlockSpec((None, tr, width), lambda s, i, pr: (s, i, pr[1]))

    def body(place_ref, g_ref, p_ref, o_ref):
        o_ref[...] = (g_ref[...].astype(F32) + p_ref[...].astype(F32)).astype(BF16)

    return pl.pallas_call(
        body, name=name, out_shape=jax.ShapeDtypeStruct(partner.shape, BF16),
        grid_spec=pltpu.PrefetchScalarGridSpec(
            num_scalar_prefetch=1, grid=(N_CHIPS, nb),
            in_specs=[mine, pl.BlockSpec((None, tr, width), lambda s, i, pr: (s, i, 0))],
            out_specs=pl.BlockSpec((None, tr, width), lambda s, i, pr: (s, i, 0))),
        compiler_params=_cparams(("parallel", "parallel")),
    )(place, grads, partner)


def _rs_chip_add(name, place, mine, others, split="rows"):
    _, half_rows, width = mine.shape
    tr = _row_block(half_rows, width, 4, 16, budget=1 << 20)
    nb = half_rows // tr
    if split == "rows":
        out_shape, out_spec = (2 * half_rows, width), pl.BlockSpec((tr, width), lambda i, pr: (pr[1] * nb + i, 0))
    else:
        out_shape, out_spec = (half_rows, 2 * width), pl.BlockSpec((tr, width), lambda i, pr: (i, pr[1]))

    def body(place_ref, q_ref, r_ref, o_ref):
        acc = q_ref[...].astype(F32)
        for j in range(3):
            acc = acc + r_ref[j].astype(F32)
        o_ref[...] = acc

    return pl.pallas_call(
        body, name=name, out_shape=jax.ShapeDtypeStruct(out_shape, F32),
        grid_spec=pltpu.PrefetchScalarGridSpec(
            num_scalar_prefetch=1, grid=(nb,),
            in_specs=[pl.BlockSpec((None, tr, width), lambda i, pr: (pr[0], i, 0)),
                      pl.BlockSpec((3, tr, width), lambda i, pr: (0, i, 0))],
            out_specs=out_spec),
        compiler_params=_cparams(("parallel",)),
    )(place, mine, others)


WEIGHTS = ["meta_tokens", "a_norm_pre", "a_w_in", "a_conv_w", "a_conv_b", "a_dt_bias", "a_a_log", "a_d_skip",
           "a_gate_norm", "a_w_out", "a_norm_post", "kv_norm", "w_kv", "b_norm_pre", "b_w_q", "b_sinks", "b_w_o",
           "b_norm_post", "f_norm_pre", "f_w_up", "f_conv_w", "f_conv_b", "f_w_down", "f_norm_post"]
FULL_SHAPE = {
    "meta_tokens": (16, 1024), "a_norm_pre": (1, 1024), "a_w_in": (1, 1024, 5152), "a_conv_w": (1, 4, 3072),
    "a_conv_b": (1, 3072), "a_dt_bias": (1, 32), "a_a_log": (1, 32), "a_d_skip": (1, 32), "a_gate_norm": (1, 2048),
    "a_w_out": (1, 2048, 1024), "a_norm_post": (1, 1024), "kv_norm": (1024,), "w_kv": (1024, 512),
    "b_norm_pre": (1, 1024), "b_w_q": (1, 1024, 1024), "b_sinks": (1, 16), "b_w_o": (1, 1024, 1024),
    "b_norm_post": (1, 1024), "f_norm_pre": (2, 1024), "f_w_up": (2, 1024, 5632), "f_conv_w": (2, 3, 5632),
    "f_conv_b": (2, 5632), "f_w_down": (2, 2816, 1024), "f_norm_post": (2, 1024),
}
SHARD_AXIS = {
    "meta_tokens": 1, "a_norm_pre": 1, "a_w_in": 2, "a_conv_w": 2, "a_conv_b": 1, "a_dt_bias": None, "a_a_log": None,
    "a_d_skip": None, "a_gate_norm": 1, "a_w_out": 1, "a_norm_post": 1, "kv_norm": None, "w_kv": 0, "b_norm_pre": None,
    "b_w_q": 1, "b_sinks": None, "b_w_o": 1, "b_norm_post": None, "f_norm_pre": None, "f_w_up": 2, "f_conv_w": 2,
    "f_conv_b": None, "f_w_down": 1, "f_norm_post": None,
}
BIG = ["a_w_in", "a_w_out", "w_kv", "b_w_q", "b_w_o", "f_w_up", "f_w_down"]
SMALL = [n for n in WEIGHTS if n not in BIG]
SMALL_SHARDED = [n for n in SMALL if SHARD_AXIS[n] is not None]


def _shard_shape(name):
    shape = list(FULL_SHAPE[name])
    if SHARD_AXIS[name] is not None:
        shape[SHARD_AXIS[name]] //= N_CHIPS
    return tuple(shape)


def _numel(shape):
    return int(math.prod(shape))


SUBLANES = 8


def _packed_rows(shape):
    rows = -(-_numel(shape) // LANES)
    return -(-rows // SUBLANES) * SUBLANES


def _pack(arrays):
    parts = []
    for a in arrays:
        size, rows = _numel(a.shape), _packed_rows(a.shape)
        if size % LANES == 0:
            part = jnp.pad(a.reshape(size // LANES, LANES), ((0, rows - size // LANES), (0, 0)))
        else:
            part = jnp.pad(a.reshape(-1), (0, rows * LANES - size)).reshape(rows, LANES)
        parts.append(part)
    return jnp.concatenate(parts, axis=0)


def _unpack(packed, names, shape_of):
    out, off = {}, 0
    lead = packed.shape[:-2]
    for n in names:
        shape = tuple(shape_of(n))
        size, rows = _numel(shape), _packed_rows(shape)
        part = packed[..., off:off + rows, :]
        if size % LANES == 0:
            out[n] = part[..., :size // LANES, :].reshape(lead + shape)
        else:
            out[n] = part.reshape(lead + (rows * LANES,))[..., :size].reshape(lead + shape)
        off += rows
    return out


def _split_chips(name, full):
    ax = SHARD_AXIS[name]
    shape = full.shape
    cut = shape[:ax] + (N_CHIPS, shape[ax] // N_CHIPS) + shape[ax + 1:]
    return jnp.moveaxis(full.reshape(cut), ax, 0)


def _join_chips(name, stacked):
    ax = SHARD_AXIS[name]
    moved = jnp.moveaxis(stacked, 0, ax)
    shape = moved.shape
    return moved.reshape(shape[:ax] + (shape[ax] * shape[ax + 1],) + shape[ax + 2:])


def _as2d(a):
    return a.reshape(-1, a.shape[-1])


BUFFERS = [("a_w_in", "a_w_in", None), ("a_w_out", "a_w_out", None), ("w_kv", "w_kv", None),
           ("b_w_q", "b_w_q", None), ("b_w_o", "b_w_o", None), ("f_w_up0", "f_w_up", 0), ("f_w_up1", "f_w_up", 1),
           ("f_w_down0", "f_w_down", 0), ("f_w_down1", "f_w_down", 1)]


TRANSPOSED = ("a_w_in",)
SPLIT = {"a_w_in": "cols"}


def _local_shard(arrays, weight, layer):
    if weight in TRANSPOSED:
        return arrays[weight][0].T
    return _as2d(arrays[weight]) if layer is None else arrays[weight]


def _weight_from_gathered(weight, buf):
    if weight == "f_w_up":
        return buf
    return buf.reshape(N_CHIPS * buf.shape[1], buf.shape[2])


def _gathered_from_grad(weight, g):
    if weight == "f_w_up":
        return g
    return g.reshape(N_CHIPS, g.shape[0] // N_CHIPS, g.shape[1]).astype(BF16)


GATHER_SCHEDULE = {
    "a_in_main": [("ici", ["a_w_out"])],
    "a_conv": [("d2d", ["a_w_out"]), ("ici", ["f_w_down0"])],
    "a_ssd_prep": [("d2d", ["f_w_down0"]), ("ici_near", ["f_w_up0"])],
    "a_ssd": [("ici_far", ["f_w_up0"])],
    "a_gate": [("d2d", ["f_w_up0"]), ("ici", ["w_kv", "b_w_q", "b_w_o"])],
    "ffn0_up": [("d2d", ["w_kv", "b_w_q", "b_w_o"]), ("ici", ["f_w_down1"])],
    "ffn0_conv": [("d2d", ["f_w_down1"]), ("ici_near", ["f_w_up1"])],
    "b_attn": [("ici_far", ["f_w_up1"])],
    "b_o": [("d2d", ["f_w_up1"])],
}
REDUCE_SCHEDULE = {
    "b_attn_bwd": [("all", ["f_w_down1", "f_w_up1", "b_w_o"])],
    "ffn0_conv_bwd": [("all", ["b_w_q", "w_kv", "f_w_down0"])],
    "a_ssd_bwd": [("all", ["f_w_up0", "a_w_out"])],
    "a_in_main_dx": [("near", ["a_w_in"])],
    "adamw_f_w_up": [("far", ["a_w_in"])],
}
REDUCE_LAST = ("a_w_in",)
ICI_PEERS = {"ici": ALL_PEERS, "ici_near": NEAR_PEERS, "ici_far": FAR_PEERS,
             "all": ALL_PEERS, "near": NEAR_PEERS, "far": FAR_PEERS}
PAIR_SCHEDULE = {
    "b_o_dx": ["f_w_down1", "f_w_up1", "b_w_o"],
    "ffn0_down_dx": ["b_w_q", "w_kv", "f_w_down0"],
    "a_out_dx": ["f_w_up0", "a_w_out"],
    "a_in_dt_dx": ["a_w_in"],
}
SWAP_SCHEDULE = {"a_in_main_dw": ["f_w_down1", "f_w_up1", "b_w_o", "b_w_q", "w_kv", "f_w_down0", "f_w_up0", "a_w_out"]}


def _buffer_of(weight, layer):
    return weight if layer is None else f"{weight}{layer}"


class _Pipeline:
    def __init__(self, place, slots):
        self.place = place
        self.slots = dict(slots)
        self.running = []
        self.grads = {}
        self.theirs = {}
        self.partials = {}
        self.peers = {}
        self.reduced = {}

    def _collect(self):
        for step, buffers, table in self.running:
            table.update(zip(buffers, step.results))
        self.running = []

    @staticmethod
    def _splits(buffers):
        return [SPLIT.get(b, "rows") for b in buffers]

    def gather_now(self, name, buffers):
        step = _step_gather_full([self.slots[b] for b in buffers], self._splits(buffers))
        _run_steps(name, [step])
        self.slots.update(zip(buffers, step.results))

    def weight(self, name, layer=None):
        self._collect()
        return _weight_from_gathered(name, self.slots[_buffer_of(name, layer)])

    def grad(self, name, layer, g):
        self.grads[_buffer_of(name, layer)] = _gathered_from_grad(name, g)

    def steps(self, kernel):
        self._collect()
        steps = []
        for phase, buffers in GATHER_SCHEDULE.get(kernel, []):
            bufs, splits = [self.slots[b] for b in buffers], self._splits(buffers)
            step = (_step_gather_d2d(bufs, splits) if phase == "d2d"
                    else _step_gather_ici(bufs, splits, ICI_PEERS[phase]))
            self.running.append((step, buffers, self.slots))
            steps.append(step)
        buffers = PAIR_SCHEDULE.get(kernel)
        if buffers:
            step = _step_pair_exchange([self.grads[b] for b in buffers], self._splits(buffers))
            self.running.append((step, buffers, self.theirs))
            steps.append(step)
        for part, buffers in REDUCE_SCHEDULE.get(kernel, []):
            for b in buffers:
                if b not in self.partials:
                    self.partials[b] = _rs_pair_add("reduce_pair_add_" + b, self.place, self.grads[b], self.theirs[b],
                                                    SPLIT.get(b, "rows"))
            started = [self.peers[b] for b in buffers] if all(b in self.peers for b in buffers) else None
            step = _step_chip_exchange([self.partials[b] for b in buffers], ICI_PEERS[part], into=started)
            self.running.append((step, buffers, self.peers))
            steps.append(step)
        buffers = SWAP_SCHEDULE.get(kernel)
        if buffers:
            step = self._swap_step(buffers)
            self.running.append((step, buffers, self.reduced))
            steps.append(step)
        return steps

    def _swap_step(self, buffers):
        halves = [_rs_chip_add("reduce_chip_add_" + b, self.place, self.partials[b], self.peers[b], SPLIT.get(b, "rows"))
                  for b in buffers]
        return _step_pair_gather(halves, self._splits(buffers))

    def shard(self, buffer):
        self._collect()
        return self.reduced[buffer]

    def finish(self):
        self._collect()
        rest = [b for b, _, _ in BUFFERS if b not in self.reduced]
        step = self._swap_step(rest)
        _run_steps("reduce_pair_gather", [step])
        self.reduced.update(zip(rest, step.results))


def kernel(x, meta_tokens, a_norm_pre, a_w_in, a_conv_w, a_conv_b, a_dt_bias, a_a_log, a_d_skip, a_gate_norm, a_w_out, a_norm_post, kv_norm, w_kv, b_norm_pre, b_w_q, b_sinks, b_w_o, b_norm_post, f_norm_pre, f_w_up, f_conv_w, f_conv_b, f_w_down, f_norm_post, loss_target, m_meta_tokens, m_a_norm_pre, m_a_w_in, m_a_conv_w, m_a_conv_b, m_a_dt_bias, m_a_a_log, m_a_d_skip, m_a_gate_norm, m_a_w_out, m_a_norm_post, m_kv_norm, m_w_kv, m_b_norm_pre, m_b_w_q, m_b_sinks, m_b_w_o, m_b_norm_post, m_f_norm_pre, m_f_w_up, m_f_conv_w, m_f_conv_b, m_f_w_down, m_f_norm_post, v_meta_tokens, v_a_norm_pre, v_a_w_in, v_a_conv_w, v_a_conv_b, v_a_dt_bias, v_a_a_log, v_a_d_skip, v_a_gate_norm, v_a_w_out, v_a_norm_post, v_kv_norm, v_w_kv, v_b_norm_pre, v_b_w_q, v_b_sinks, v_b_w_o, v_b_norm_post, v_f_norm_pre, v_f_w_up, v_f_conv_w, v_f_conv_b, v_f_w_down, v_f_norm_post):
    given = dict(locals())
    w = {n: given[n] for n in WEIGHTS}
    mom = {n: given["m_" + n] for n in WEIGHTS}
    var = {n: given["v_" + n] for n in WEIGHTS}
    chip = 2 * lax.axis_index("x") + lax.axis_index("y")
    core = lax.axis_index("c")
    place = jnp.stack([chip, core]).astype(jnp.int32)

    small_all = _allgather_small("gather_small", _pack([w[n] for n in SMALL_SHARDED]))
    small_parts = _unpack(small_all, SMALL_SHARDED, _shard_shape)
    slots = {b: _cast_into_slot("cast_" + b, place, _local_shard(w, wn, layer), layer) for b, wn, layer in BUFFERS}
    pipeline = _Pipeline(place, slots)
    pipeline.gather_now("gather_first", ["a_w_in"])
    p = {}
    for n in SMALL:
        p[n] = _join_chips(n, small_parts[n]) if n in SMALL_SHARDED else w[n]
    p["a_conv_w"] = p["a_conv_w"][0]
    p["kv_norm"] = p["kv_norm"].reshape(1, D_MODEL)

    loss_local, grad_x, g = _local_step(x[0], loss_target[0], p, pipeline)

    small_sum = _allreduce_small("reduce_small", _pack([g[n].reshape(FULL_SHAPE[n]) for n in SMALL]
                                                       + [loss_local.reshape(1, 1)]))
    small_red = _unpack(small_sum, SMALL + ["loss"], lambda n: (1, 1) if n == "loss" else FULL_SHAPE[n])
    loss = small_red["loss"][0, 0]
    grads = {}
    for n in SMALL:
        if SHARD_AXIS[n] is None:
            grads[n] = small_red[n]
        else:
            grads[n] = lax.dynamic_index_in_dim(_split_chips(n, small_red[n]), chip, 0, keepdims=False)

    delta, new_m, new_v = {}, {}, {}
    for n in sorted(BIG, key=lambda name: name in REDUCE_LAST):
        shape = _shard_shape(n)
        if n in REDUCE_LAST:
            pipeline.finish()
        if n in TRANSPOSED:
            g2d = pipeline.shard(n)
            w2d, m2d, v2d = (arrays[n][0].T for arrays in (w, mom, var))
            back = lambda a: a.T.reshape(shape)
        else:
            g2d = (jnp.concatenate([pipeline.shard(n + "0"), pipeline.shard(n + "1")], axis=0)
                   if n in ("f_w_up", "f_w_down") else pipeline.shard(n))
            w2d, m2d, v2d = (_as2d(arrays[n]) for arrays in (w, mom, var))
            back = lambda a: a.reshape(shape)
        d, m2, v2 = _adamw("adamw_" + n, w2d, g2d, m2d, v2d, steps=pipeline.steps("adamw_" + n))
        grads[n], delta[n], new_m[n], new_v[n] = back(g2d), back(d), back(m2), back(v2)
    packed = [_pack([src[n].reshape(_shard_shape(n)) for n in SMALL]) for src in (w, grads, mom, var)]
    outs = _adamw("adamw_small", *packed)
    for dst, flat in zip((delta, new_m, new_v), outs):
        dst.update(_unpack(flat, SMALL, _shard_shape))

    return (loss, grad_x[None], *[grads[n].reshape(_shard_shape(n)) for n in WEIGHTS],
            *[delta[n] for n in WEIGHTS], *[new_m[n] for n in WEIGHTS], *[new_v[n] for n in WEIGHTS])
```

```python
import functools
import math

import jax
import jax.numpy as jnp
from jax import lax
from jax.experimental import pallas as pl
from jax.experimental.pallas import tpu as pltpu

F32, BF16 = jnp.float32, jnp.bfloat16
MESH = pl.DeviceIdType.MESH

D_MODEL = 1024
N_META = 16
CHUNK = 128
PAD_ROWS = CHUNK - N_META
D_INNER = 2048
D_STATE = 128
N_GROUPS = 4
HEADS_PER_GROUP = 8
SSM_HEADS = 32
HEAD_DIM = 64
D_BC = N_GROUPS * D_STATE
D_XBC = D_INNER + 2 * D_BC
D_MAIN = D_INNER + D_XBC
D_IN_PROJ = D_MAIN + SSM_HEADS
GROUP_W = HEADS_PER_GROUP * HEAD_DIM
SSM_CONV = 4
D_FF = 2816
FFN_CONV = 3
N_Q_HEADS = 16
N_KV_HEADS = 4
D_KV = 256
ATTN_SCALE = 1.0 / math.sqrt(HEAD_DIM)
RMS_EPS = 1e-6
NEG_INF = -1e30
LANES = 128
VMEM_LIMIT = 48 * 1024 * 1024

ADAM_LR, ADAM_B1, ADAM_B2, ADAM_EPS, ADAM_WD, ADAM_STEP = 0.001, 0.9, 0.999, 1e-08, 0.01, 10

N_CHIPS = 4
N_DEV = 8


def _cparams(sem=None):
    return pltpu.CompilerParams(dimension_semantics=sem, vmem_limit_bytes=VMEM_LIMIT)


def _tile(n, cands=(512, 256, 128)):
    for t in cands:
        if n % t == 0:
            return t
    return n


def _row_tile(rows, width):
    for t in (544, 272):
        if rows % t == 0 and t * width * 4 <= (3 << 20):
            return t
    return 128


def _rows_mask(i, tm):
    rows = i * tm + lax.broadcasted_iota(jnp.int32, (tm, 1), 0)
    return rows >= PAD_ROWS


def _dot(a, b):
    return jnp.dot(a, b, preferred_element_type=F32)


def _dot_nt(a, b):
    return lax.dot_general(a, b, (((1,), (1,)), ((), ())), preferred_element_type=F32)


def _dot_tn(a, b):
    return lax.dot_general(a, b, (((0,), (0,)), ((), ())), preferred_element_type=F32)


def _sigmoid(x):
    return 1.0 / (1.0 + jnp.exp(-x))


def _place():
    return lax.axis_index("x"), lax.axis_index("y"), lax.axis_index("c")


def _other_chips(x, y):
    return [(1 - x, y), (x, 1 - y), (1 - x, 1 - y)]


class _Step:
    def __init__(self, ins, outs, aliases, n_sems, start, finish):
        self.ins, self.outs, self.aliases, self.n_sems = list(ins), list(outs), dict(aliases), n_sems
        self.start, self.finish = start, finish
        self.results = None


def _like(a):
    return jax.ShapeDtypeStruct(a.shape, a.dtype)


def _remote(src, dst, send_sems, recv_sems, k, device):
    return pltpu.make_async_remote_copy(src, dst, send_sems.at[k], recv_sems.at[k], device_id=device, device_id_type=MESH)


def _half(ref, split, which, lead=()):
    if split == "rows":
        hr = ref.shape[-2] // 2
        return ref.at[lead + (pl.ds(which * hr, hr),)]
    hc = ref.shape[-1] // 2
    return ref.at[lead + (slice(None), pl.ds(which * hc, hc))]


def _splits(bufs, splits):
    return list(splits) if splits is not None else ["rows"] * len(bufs)


ALL_PEERS = (0, 1, 2)
NEAR_PEERS = (0, 1)
FAR_PEERS = (2,)


def _step_gather_ici(bufs, splits=None, peers=ALL_PEERS):
    splits = _splits(bufs, splits)

    def copies(outs, send_sems, recv_sems, received):
        x, y, c = _place()
        me = 2 * x + y
        for k, o in enumerate(outs):
            for j, (cx, cy) in enumerate(_other_chips(x, y)):
                if j in peers:
                    part = _half(o, splits[k], c, (2 * cx + cy if received else me,))
                    yield _remote(part, part, send_sems, recv_sems, 3 * k + j, (cx, cy, c))

    def start(ins, outs, send_sems, recv_sems):
        for cp in copies(outs, send_sems, recv_sems, False):
            cp.start()

    def finish(ins, outs, send_sems, recv_sems):
        for cp in copies(outs, send_sems, recv_sems, True):
            cp.wait_recv()
        for cp in copies(outs, send_sems, recv_sems, False):
            cp.wait_send()

    return _Step(bufs, [_like(b) for b in bufs], {k: k for k in range(len(bufs))}, 3 * len(bufs), start, finish)


def _step_gather_d2d(bufs, splits=None):
    splits = _splits(bufs, splits)

    def copies(outs, send_sems, recv_sems, received):
        x, y, c = _place()
        for k, o in enumerate(outs):
            for j, (cx, cy) in enumerate(_other_chips(x, y)):
                part = _half(o, splits[k], 1 - c if received else c, (2 * cx + cy,))
                yield _remote(part, part, send_sems, recv_sems, 3 * k + j, (x, y, 1 - c))

    def start(ins, outs, send_sems, recv_sems):
        for cp in copies(outs, send_sems, recv_sems, False):
            cp.start()

    def finish(ins, outs, send_sems, recv_sems):
        for cp in copies(outs, send_sems, recv_sems, True):
            cp.wait_recv()
        for cp in copies(outs, send_sems, recv_sems, False):
            cp.wait_send()

    return _Step(bufs, [_like(b) for b in bufs], {k: k for k in range(len(bufs))}, 3 * len(bufs), start, finish)


def _step_gather_full(bufs, splits=None):
    n = len(bufs)
    splits = _splits(bufs, splits)

    def ici(outs, send_sems, recv_sems, received):
        x, y, c = _place()
        me = 2 * x + y
        for k, o in enumerate(outs):
            for j, (cx, cy) in enumerate(_other_chips(x, y)):
                part = _half(o, splits[k], c, (2 * cx + cy if received else me,))
                yield _remote(part, part, send_sems, recv_sems, 3 * k + j, (cx, cy, c))

    def d2d(outs, send_sems, recv_sems, received):
        x, y, c = _place()
        for k, o in enumerate(outs):
            for j, (cx, cy) in enumerate(_other_chips(x, y)):
                part = _half(o, splits[k], 1 - c if received else c, (2 * cx + cy,))
                yield _remote(part, part, send_sems, recv_sems, 3 * n + 3 * k + j, (x, y, 1 - c))

    def start(ins, outs, send_sems, recv_sems):
        for cp in ici(outs, send_sems, recv_sems, False):
            cp.start()

    def finish(ins, outs, send_sems, recv_sems):
        for arrived, onward in zip(ici(outs, send_sems, recv_sems, True), d2d(outs, send_sems, recv_sems, False)):
            arrived.wait_recv()
            onward.start()
        for cp in d2d(outs, send_sems, recv_sems, True):
            cp.wait_recv()
        for cp in ici(outs, send_sems, recv_sems, False):
            cp.wait_send()
        for cp in d2d(outs, send_sems, recv_sems, False):
            cp.wait_send()

    return _Step(bufs, [_like(b) for b in bufs], {k: k for k in range(n)}, 6 * n, start, finish)


def _half_shape(shape, split):
    return shape[:-2] + ((shape[-2] // 2, shape[-1]) if split == "rows" else (shape[-2], shape[-1] // 2))


def _step_pair_exchange(grads, splits=None):
    splits = _splits(grads, splits)

    def copies(ins, outs, send_sems, recv_sems):
        x, y, c = _place()
        for k, (g, o) in enumerate(zip(ins, outs)):
            yield _remote(_half(g, splits[k], 1 - c, (slice(None),)), o, send_sems, recv_sems, k, (x, y, 1 - c))

    def start(ins, outs, send_sems, recv_sems):
        for cp in copies(ins, outs, send_sems, recv_sems):
            cp.start()

    def finish(ins, outs, send_sems, recv_sems):
        for cp in copies(ins, outs, send_sems, recv_sems):
            cp.wait()

    outs = [jax.ShapeDtypeStruct(_half_shape(g.shape, s), g.dtype) for g, s in zip(grads, splits)]
    return _Step(grads, outs, {}, len(grads), start, finish)


def _step_chip_exchange(partials, peers=ALL_PEERS, into=None):
    n = len(partials)

    def copies(ins, outs, send_sems, recv_sems):
        x, y, c = _place()
        for k, (q, o) in enumerate(zip(ins[:n], outs)):
            for j, (cx, cy) in enumerate(_other_chips(x, y)):
                if j in peers:
                    yield _remote(q.at[2 * cx + cy], o.at[j], send_sems, recv_sems, 3 * k + j, (cx, cy, c))

    def start(ins, outs, send_sems, recv_sems):
        for cp in copies(ins, outs, send_sems, recv_sems):
            cp.start()

    def finish(ins, outs, send_sems, recv_sems):
        for cp in copies(ins, outs, send_sems, recv_sems):
            cp.wait()

    outs = [jax.ShapeDtypeStruct((3,) + q.shape[1:], q.dtype) for q in partials]
    if into is None:
        return _Step(partials, outs, {}, 3 * n, start, finish)
    return _Step(list(partials) + list(into), outs, {n + k: k for k in range(n)}, 3 * n, start, finish)


def _step_pair_gather(shards, splits=None):
    splits = _splits(shards, splits)

    def copies(outs, send_sems, recv_sems, received):
        x, y, c = _place()
        for k, o in enumerate(outs):
            part = _half(o, splits[k], 1 - c if received else c)
            yield _remote(part, part, send_sems, recv_sems, k, (x, y, 1 - c))

    def start(ins, outs, send_sems, recv_sems):
        for cp in copies(outs, send_sems, recv_sems, False):
            cp.start()

    def finish(ins, outs, send_sems, recv_sems):
        for cp in copies(outs, send_sems, recv_sems, True):
            cp.wait_recv()
        for cp in copies(outs, send_sems, recv_sems, False):
            cp.wait_send()

    return _Step(shards, [_like(s) for s in shards], {k: k for k in range(len(shards))}, len(shards), start, finish)


def _call(body, *, name, out_shape, grid, in_specs, out_specs, operands, scratch_shapes=(), semantics=None, steps=()):
    single = not isinstance(out_shape, (tuple, list))
    out_shapes = [out_shape] if single else list(out_shape)
    out_spec_list = [out_specs] if single else list(out_specs)
    steps = list(steps)
    if not steps:
        res = pl.pallas_call(body, name=name, out_shape=out_shapes, grid=grid, in_specs=list(in_specs),
                             out_specs=out_spec_list, scratch_shapes=list(scratch_shapes),
                             compiler_params=_cparams(semantics))(*operands)
        return res[0] if single else res
    n_in, n_out, n_scr = len(operands), len(out_shapes), len(scratch_shapes)
    x_in = [a for s in steps for a in s.ins]
    x_out = [o for s in steps for o in s.outs]
    aliases, in_off, out_off = {}, 0, 0
    for s in steps:
        for i, o in s.aliases.items():
            aliases[n_in + in_off + i] = n_out + out_off + o
        in_off += len(s.ins)
        out_off += len(s.outs)
    sems = []
    for s in steps:
        sems += [pltpu.SemaphoreType.DMA((s.n_sems,)), pltpu.SemaphoreType.DMA((s.n_sems,))]
    any_spec = pl.BlockSpec(memory_space=pl.ANY)

    def carried(*refs):
        pos = 0
        ins = refs[pos:pos + n_in]; pos += n_in
        xi = refs[pos:pos + len(x_in)]; pos += len(x_in)
        outs = refs[pos:pos + n_out]; pos += n_out
        xo = refs[pos:pos + len(x_out)]; pos += len(x_out)
        scr = refs[pos:pos + n_scr]; pos += n_scr
        sem_refs = refs[pos:]

        def each(action):
            i0 = o0 = 0
            for k, s in enumerate(steps):
                getattr(s, action)(xi[i0:i0 + len(s.ins)], xo[o0:o0 + len(s.outs)], sem_refs[2 * k], sem_refs[2 * k + 1])
                i0 += len(s.ins)
                o0 += len(s.outs)

        if grid:
            first = functools.reduce(jnp.logical_and, [pl.program_id(d) == 0 for d in range(len(grid))])
            last = functools.reduce(jnp.logical_and, [pl.program_id(d) == grid[d] - 1 for d in range(len(grid))])
            pl.when(first)(lambda: each("start"))
            body(*ins, *outs, *scr)
            pl.when(last)(lambda: each("finish"))
        else:
            each("start")
            body(*ins, *outs, *scr)
            each("finish")

    res = pl.pallas_call(
        carried, name=name, out_shape=out_shapes + x_out, grid=grid,
        in_specs=list(in_specs) + [any_spec] * len(x_in), out_specs=out_spec_list + [any_spec] * len(x_out),
        scratch_shapes=list(scratch_shapes) + sems, input_output_aliases=aliases,
        compiler_params=_cparams(None if semantics is None else ("arbitrary",) * len(grid)),
    )(*operands, *x_in)
    o0 = n_out
    for s in steps:
        s.results = list(res[o0:o0 + len(s.outs)])
        o0 += len(s.outs)
    return res[0] if single else tuple(res[:n_out])


def _run_steps(name, steps):
    _call(lambda: None, name=name, out_shape=[], grid=(), in_specs=[], out_specs=[], operands=[], steps=steps)
    return [s.results for s in steps]


def _mm(name, a, b, mode, out_dtype=F32, acc=None, b_colblock=0, k_rows=None, out_rows=None, steps=()):
    resident_bytes = 8 << 20
    if mode == "nn":
        m, k = a.shape
        n = b.shape[1]
        tm = m
        while tm * k * 2 > resident_bytes and tm % 32 == 0:
            tm //= 2
        tn = _tile(n)
        grid = (m // tm, n // tn)
        in_specs = [pl.BlockSpec((tm, k), lambda i, j: (i, 0)), pl.BlockSpec((k, tn), lambda i, j: (0, j))]
        out_shape, out_block = (m, n), (tm, tn)
    elif mode == "nt":
        m, n = a.shape
        k = k_rows or b.shape[0]
        tm = m
        while tm * n * 2 > resident_bytes and tm % 32 == 0:
            tm //= 2
        tk = _tile(k)
        grid = (m // tm, k // tk)
        in_specs = [pl.BlockSpec((tm, n), lambda i, j: (i, 0)), pl.BlockSpec((tk, n), lambda i, j: (j, b_colblock))]
        out_shape, out_block = (m, k), (tm, tk)
    else:
        m, k = a.shape
        n = b.shape[1]
        tk, tn = _tile(k), (n if m * n * 2 <= resident_bytes else _tile(n))
        grid = (k // tk, n // tn)
        in_specs = [pl.BlockSpec((m, tk), lambda i, j: (0, i)), pl.BlockSpec((m, tn), lambda i, j: (0, j))]
        out_shape, out_block = (out_rows or k, n), (tk, tn)
    out_spec = pl.BlockSpec(out_block, lambda i, j: (i, j))
    has_acc = acc is not None

    def body(*refs):
        a_ref, b_ref = refs[0], refs[1]
        o_ref = refs[-1]
        av, bv = a_ref[...], b_ref[...]
        if mode == "nn":
            r = _dot(av, bv)
        elif mode == "nt":
            r = _dot_nt(av, bv)
        else:
            r = _dot_tn(av, bv)
        if has_acc:
            r = r + refs[2][...]
        o_ref[...] = r.astype(o_ref.dtype)

    operands = [a, b]
    if has_acc:
        in_specs = in_specs + [out_spec]
        operands.append(acc)
    return _call(body, name=name, out_shape=jax.ShapeDtypeStruct(out_shape, out_dtype), grid=grid, in_specs=in_specs,
                 out_specs=out_spec, operands=operands, semantics=("parallel", "parallel"), steps=steps)


def _tn_rows_into(name, a, b, into, row0, nrows):
    m, k = a.shape
    n = b.shape[1]

    def body(a_ref, b_ref, into_ref, o_ref):
        o_ref[...] = _dot_tn(a_ref[...], b_ref[...])[0:nrows].astype(o_ref.dtype)

    return pl.pallas_call(
        body, name=name, out_shape=jax.ShapeDtypeStruct(into.shape, into.dtype), grid=(1,),
        in_specs=[pl.BlockSpec((m, k), lambda i: (0, 0)), pl.BlockSpec((m, n), lambda i: (0, 0)),
                  pl.BlockSpec(memory_space=pl.ANY)],
        out_specs=pl.BlockSpec((nrows, n), lambda i: (row0 // nrows, 0)),
        input_output_aliases={2: 0}, compiler_params=_cparams(("arbitrary",)),
    )(a, b, into)


def _fit_rows(m, row_bytes, budget=8 << 20):
    tm = m
    while tm * row_bytes > budget and tm % 32 == 0:
        tm //= 2
    return tm


def _mm_nn_bychip(name, a, bc, steps=()):
    m, k = a.shape
    n = bc.shape[2]
    tm = min(_fit_rows(m, k * 2), _fit_rows(m, n * 4))

    def body(a_ref, b_ref, o_ref):
        o_ref[...] = _dot(a_ref[...], b_ref[...])

    return _call(
        body, name=name, out_shape=jax.ShapeDtypeStruct((m, N_CHIPS * n), F32), grid=(m // tm, N_CHIPS),
        in_specs=[pl.BlockSpec((tm, k), lambda i, c: (i, 0)), pl.BlockSpec((None, k, n), lambda i, c: (c, 0, 0))],
        out_specs=pl.BlockSpec((tm, n), lambda i, c: (i, c)), operands=[a, bc],
        semantics=("parallel", "parallel"), steps=steps)


def _mm_nt_bychip(name, a, bc, chip0, acc=None):
    m = a.shape[0]
    _, k, n = bc.shape
    nch = a.shape[1] // n
    tm, tk = _fit_rows(m, n * 2), _tile(k)
    has_acc = acc is not None

    def body(*refs):
        a_ref, b_ref, o_ref = refs[0], refs[1], refs[-1]

        @pl.when(pl.program_id(2) == 0)
        def _():
            o_ref[...] = refs[2][...] if has_acc else jnp.zeros_like(o_ref)

        o_ref[...] += _dot_nt(a_ref[...], b_ref[...])

    out_spec = pl.BlockSpec((tm, tk), lambda i, j, c: (i, j))
    in_specs = [pl.BlockSpec((tm, n), lambda i, j, c: (i, c)),
                pl.BlockSpec((None, tk, n), lambda i, j, c: (chip0 + c, j, 0))]
    operands = [a, bc]
    if has_acc:
        in_specs.append(out_spec)
        operands.append(acc)
    return pl.pallas_call(
        body, name=name, out_shape=jax.ShapeDtypeStruct((m, k), F32), grid=(m // tm, k // tk, nch),
        in_specs=in_specs, out_specs=out_spec, compiler_params=_cparams(("parallel", "parallel", "arbitrary")),
    )(*operands)


def _mm_tn_bychip(name, a, dy, n, chip0, into=None):
    m, k = a.shape
    nch = dy.shape[1] // n
    tk = _tile(k)

    def body(*refs):
        a_ref, d_ref, o_ref = refs[0], refs[1], refs[-1]
        o_ref[...] = _dot_tn(a_ref[...], d_ref[...]).astype(BF16)

    in_specs = [pl.BlockSpec((m, tk), lambda i, c: (0, i)), pl.BlockSpec((m, n), lambda i, c: (0, c))]
    operands = [a, dy]
    aliases = {}
    if into is not None:
        in_specs.append(pl.BlockSpec(memory_space=pl.ANY))
        operands.append(into)
        aliases = {2: 0}
    return pl.pallas_call(
        body, name=name, out_shape=jax.ShapeDtypeStruct((N_CHIPS, k, n), BF16), grid=(k // tk, nch),
        in_specs=in_specs, out_specs=pl.BlockSpec((None, tk, n), lambda i, c: (chip0 + c, i, 0)),
        input_output_aliases=aliases, compiler_params=_cparams(("parallel", "parallel")),
    )(*operands)


def _rms_fwd(name, h, w):
    rows, width = h.shape
    tm = _row_tile(rows, width)

    def body(h_ref, w_ref, o_ref):
        x = h_ref[...]
        r = lax.rsqrt(jnp.mean(x * x, axis=-1, keepdims=True) + RMS_EPS)
        o_ref[...] = (x * r * w_ref[...]).astype(BF16)

    return pl.pallas_call(
        body, name=name, out_shape=jax.ShapeDtypeStruct((rows, width), BF16), grid=(rows // tm,),
        in_specs=[pl.BlockSpec((tm, width), lambda i: (i, 0)), pl.BlockSpec((1, width), lambda i: (0, 0))],
        out_specs=pl.BlockSpec((tm, width), lambda i: (i, 0)), compiler_params=_cparams(("parallel",)),
    )(h, w)


def _resid_norm_fwd(name, h, pre, w, next_norms=()):
    rows, width = h.shape
    tm = _row_tile(rows, width)
    n_next = len(next_norms)

    def body(*refs):
        h_ref, p_ref, w_ref = refs[:3]
        v_refs = refs[3:3 + n_next]
        o_ref = refs[3 + n_next]
        n_refs = refs[4 + n_next:]
        p = p_ref[...]
        r = lax.rsqrt(jnp.mean(p * p, axis=-1, keepdims=True) + RMS_EPS)
        x = h_ref[...] + jnp.where(_rows_mask(pl.program_id(0), tm), p * r * w_ref[...], 0.0)
        o_ref[...] = x
        if n_next:
            rx = lax.rsqrt(jnp.mean(x * x, axis=-1, keepdims=True) + RMS_EPS)
            for v_ref, n_ref in zip(v_refs, n_refs):
                n_ref[...] = (x * rx * v_ref[...]).astype(BF16)

    row_spec = pl.BlockSpec((tm, width), lambda i: (i, 0))
    vec_spec = pl.BlockSpec((1, width), lambda i: (0, 0))
    outs = pl.pallas_call(
        body, name=name,
        out_shape=[jax.ShapeDtypeStruct((rows, width), F32)] + [jax.ShapeDtypeStruct((rows, width), BF16)] * n_next,
        grid=(rows // tm,), in_specs=[row_spec, row_spec, vec_spec] + [vec_spec] * n_next,
        out_specs=[row_spec] * (1 + n_next), compiler_params=_cparams(("parallel",)),
    )(h, pre, w, *next_norms)
    return outs[0], list(outs[1:])


def _resid_norm_loss(name, h, pre, w, target):
    rows, width = h.shape

    def body(h_ref, p_ref, w_ref, t_ref, dh_ref, loss_ref, dp_ref, dw_ref):
        i = pl.program_id(0)
        p = p_ref[...]
        r = lax.rsqrt(jnp.mean(p * p, axis=-1, keepdims=True) + RMS_EPS)
        x = h_ref[...] + p * r * w_ref[...]
        real = (i + jnp.zeros((CHUNK, 1), jnp.int32)) >= 1
        diff = jnp.where(real, x - t_ref[...], 0.0)
        dh = diff * (1.0 / D_MODEL)
        dh_ref[...] = dh
        dp, dw_rows = _rms_bwd(dh, p, w_ref[...])
        dp_ref[...] = dp.astype(BF16)

        @pl.when(i == 0)
        def _():
            loss_ref[...] = jnp.zeros_like(loss_ref)
            dw_ref[...] = jnp.zeros_like(dw_ref)

        loss_ref[...] += jnp.sum(diff * diff) * (0.5 / D_MODEL)
        dw_ref[...] += jnp.sum(dw_rows, axis=0, keepdims=True)

    blk = pl.BlockSpec((CHUNK, width), lambda i: (i, 0))
    vec_spec = pl.BlockSpec((1, width), lambda i: (0, 0))
    return pl.pallas_call(
        body, name=name,
        out_shape=(jax.ShapeDtypeStruct((rows, width), F32), jax.ShapeDtypeStruct((1, LANES), F32),
                   jax.ShapeDtypeStruct((rows, width), BF16), jax.ShapeDtypeStruct((1, width), F32)),
        grid=(rows // CHUNK,),
        in_specs=[blk, blk, vec_spec, pl.BlockSpec((CHUNK, width), lambda i: (jnp.maximum(i - 1, 0), 0))],
        out_specs=(blk, pl.BlockSpec((1, LANES), lambda i: (0, 0)), blk, vec_spec),
        compiler_params=_cparams(("arbitrary",)),
    )(h, pre, w, target)


def _rms_bwd(dy, x, w):
    r = lax.rsqrt(jnp.mean(x * x, axis=-1, keepdims=True) + RMS_EPS)
    xhat = x * r
    dxhat = dy * w
    return r * (dxhat - xhat * jnp.mean(dxhat * xhat, axis=-1, keepdims=True)), dy * xhat


def _norm_bwd_add(name, dh, dhn, h, w, then=None, steps=()):
    rows, width = dh.shape
    tm = _row_tile(rows, width)
    fused = then is not None

    def body(*refs):
        dh_ref, dhn_ref, h_ref, w_ref = refs[:4]
        o_ref, dw_ref = refs[6:8] if fused else refs[4:6]
        i = pl.program_id(0)
        valid = _rows_mask(i, tm)
        dx, dw_rows = _rms_bwd(dhn_ref[...], h_ref[...], w_ref[...])
        dh_new = dh_ref[...] + jnp.where(valid, dx, 0.0)
        o_ref[...] = dh_new

        @pl.when(i == 0)
        def _():
            dw_ref[...] = jnp.zeros_like(dw_ref)

        dw_ref[...] += jnp.sum(dw_rows, axis=0, keepdims=True)
        if fused:
            p_ref, wp_ref, dp_ref, dwp_ref = refs[4], refs[5], refs[8], refs[9]
            dp, dwp_rows = _rms_bwd(jnp.where(valid, dh_new, 0.0), p_ref[...], wp_ref[...])
            dp_ref[...] = dp.astype(BF16)

            @pl.when(i == 0)
            def _():
                dwp_ref[...] = jnp.zeros_like(dwp_ref)

            dwp_ref[...] += jnp.sum(dwp_rows, axis=0, keepdims=True)

    row_spec = pl.BlockSpec((tm, width), lambda i: (i, 0))
    vec_spec = pl.BlockSpec((1, width), lambda i: (0, 0))
    row_f32, vec_f32 = jax.ShapeDtypeStruct((rows, width), F32), jax.ShapeDtypeStruct((1, width), F32)
    in_specs, operands = [row_spec, row_spec, row_spec, vec_spec], [dh, dhn, h, w]
    out_shape, out_specs = [row_f32, vec_f32], [row_spec, vec_spec]
    if fused:
        in_specs += [row_spec, vec_spec]
        operands += list(then)
        out_shape += [jax.ShapeDtypeStruct((rows, width), BF16), vec_f32]
        out_specs += [row_spec, vec_spec]
    return _call(body, name=name, out_shape=out_shape, grid=(rows // tm,), in_specs=in_specs, out_specs=out_specs,
                 operands=operands, semantics=("arbitrary",), steps=steps)


def _shift_down(x, s, rows):
    return pltpu.roll(x, s, 0) if s else x


def _shift_up(x, s, rows):
    return pltpu.roll(x, rows - s, 0) if s else x


def _conv4_fwd(name, zx, cw, cb, steps=()):
    rows = zx.shape[0]
    off = D_INNER // LANES

    def body(x_ref, w_ref, b_ref, o_ref):
        x = x_ref[...]
        acc = b_ref[...] + w_ref[pl.ds(SSM_CONV - 1, 1), :] * x
        for s in range(1, SSM_CONV):
            acc = acc + w_ref[pl.ds(SSM_CONV - 1 - s, 1), :] * _shift_down(x, s, rows)
        valid = lax.broadcasted_iota(jnp.int32, (rows, 1), 0) >= PAD_ROWS
        o_ref[...] = jnp.where(valid, acc * _sigmoid(acc), 0.0)

    return _call(
        body, name=name, out_shape=jax.ShapeDtypeStruct((rows, D_XBC), F32), grid=(D_XBC // LANES,),
        in_specs=[pl.BlockSpec((rows, LANES), lambda j: (0, j + off)),
                  pl.BlockSpec((SSM_CONV, LANES), lambda j: (0, j)),
                  pl.BlockSpec((1, LANES), lambda j: (0, j))],
        out_specs=pl.BlockSpec((rows, LANES), lambda j: (0, j)), operands=[zx, cw, cb],
        semantics=("parallel",), steps=steps)


def _conv4_bwd(name, zx, dout, cw, cb, col0):
    rows, width = dout.shape
    zoff = (D_INNER + col0) // LANES
    woff = col0 // LANES

    def body(x_ref, d_ref, w_ref, b_ref, dx_ref, dw_ref, db_ref):
        x = x_ref[...]
        shifted = [_shift_down(x, s, rows) for s in range(SSM_CONV)]
        acc = b_ref[...]
        for s in range(SSM_CONV):
            acc = acc + w_ref[pl.ds(SSM_CONV - 1 - s, 1), :] * shifted[s]
        sig = _sigmoid(acc)
        valid = lax.broadcasted_iota(jnp.int32, (rows, 1), 0) >= PAD_ROWS
        dpre = jnp.where(valid, d_ref[...] * sig * (1.0 + acc * (1.0 - sig)), 0.0)
        dx = w_ref[pl.ds(SSM_CONV - 1, 1), :] * dpre
        for s in range(1, SSM_CONV):
            dx = dx + w_ref[pl.ds(SSM_CONV - 1 - s, 1), :] * _shift_up(dpre, s, rows)
        dx_ref[...] = dx.astype(BF16)
        for s in range(SSM_CONV):
            dw_ref[pl.ds(SSM_CONV - 1 - s, 1), :] = jnp.sum(dpre * shifted[s], axis=0, keepdims=True)
        db_ref[...] = jnp.sum(dpre, axis=0, keepdims=True)

    return pl.pallas_call(
        body, name=name,
        out_shape=(jax.ShapeDtypeStruct((rows, width), BF16), jax.ShapeDtypeStruct((SSM_CONV, width), F32),
                   jax.ShapeDtypeStruct((1, width), F32)),
        grid=(width // LANES,),
        in_specs=[pl.BlockSpec((rows, LANES), lambda j: (0, j + zoff)),
                  pl.BlockSpec((rows, LANES), lambda j: (0, j)),
                  pl.BlockSpec((SSM_CONV, LANES), lambda j: (0, j + woff)),
                  pl.BlockSpec((1, LANES), lambda j: (0, j + woff))],
        out_specs=(pl.BlockSpec((rows, LANES), lambda j: (0, j)),
                   pl.BlockSpec((SSM_CONV, LANES), lambda j: (0, j)),
                   pl.BlockSpec((1, LANES), lambda j: (0, j))),
        compiler_params=_cparams(("parallel",)),
    )(zx, dout, cw, cb)


def _ffn_conv_fwd(name, up, cw, cb, steps=()):
    rows = up.shape[0]
    nt = D_FF // LANES

    def body(g_ref, v_ref, wg_ref, wv_ref, bg_ref, bv_ref, o_ref):
        g, v = g_ref[...], v_ref[...]
        ug, uv = bg_ref[...], bv_ref[...]
        for s in range(FFN_CONV):
            ug = ug + wg_ref[pl.ds(FFN_CONV - 1 - s, 1), :] * _shift_down(g, s, rows)
            uv = uv + wv_ref[pl.ds(FFN_CONV - 1 - s, 1), :] * _shift_down(v, s, rows)
        o_ref[...] = (ug * _sigmoid(ug) * uv).astype(BF16)

    col = lambda shift: pl.BlockSpec((rows, LANES), lambda j: (0, j + shift))
    wsp = lambda shift: pl.BlockSpec((FFN_CONV, LANES), lambda j: (0, j + shift))
    bsp = lambda shift: pl.BlockSpec((1, LANES), lambda j: (0, j + shift))
    return _call(
        body, name=name, out_shape=jax.ShapeDtypeStruct((rows, D_FF), BF16), grid=(nt,),
        in_specs=[col(0), col(nt), wsp(0), wsp(nt), bsp(0), bsp(nt)],
        out_specs=pl.BlockSpec((rows, LANES), lambda j: (0, j)), operands=[up, up, cw, cw, cb, cb],
        semantics=("parallel",), steps=steps)


def _ffn_conv_bwd(name, up, dact, cw, cb, steps=()):
    rows = up.shape[0]
    nt = D_FF // LANES

    def body(g_ref, v_ref, d_ref, wg_ref, wv_ref, bg_ref, bv_ref, dxg_ref, dxv_ref, dwg_ref, dwv_ref, dbg_ref, dbv_ref):
        g, v = g_ref[...], v_ref[...]
        gs = [_shift_down(g, s, rows) for s in range(FFN_CONV)]
        vs = [_shift_down(v, s, rows) for s in range(FFN_CONV)]
        ug, uv = bg_ref[...], bv_ref[...]
        for s in range(FFN_CONV):
            ug = ug + wg_ref[pl.ds(FFN_CONV - 1 - s, 1), :] * gs[s]
            uv = uv + wv_ref[pl.ds(FFN_CONV - 1 - s, 1), :] * vs[s]
        sig = _sigmoid(ug)
        dsig = d_ref[...] * sig
        for dpre, src, w_ref, dx_ref, dw_ref, db_ref in (
                (dsig * uv * (1.0 + ug * (1.0 - sig)), gs, wg_ref, dxg_ref, dwg_ref, dbg_ref),
                (dsig * ug, vs, wv_ref, dxv_ref, dwv_ref, dbv_ref)):
            dx = w_ref[pl.ds(FFN_CONV - 1, 1), :] * dpre
            for s in range(1, FFN_CONV):
                dx = dx + w_ref[pl.ds(FFN_CONV - 1 - s, 1), :] * _shift_up(dpre, s, rows)
            dx_ref[...] = dx.astype(BF16)
            for s in range(FFN_CONV):
                dw_ref[pl.ds(FFN_CONV - 1 - s, 1), :] = jnp.sum(dpre * src[s], axis=0, keepdims=True)
            db_ref[...] = jnp.sum(dpre, axis=0, keepdims=True)

    col = lambda shift: pl.BlockSpec((rows, LANES), lambda j: (0, j + shift))
    wsp = lambda shift: pl.BlockSpec((FFN_CONV, LANES), lambda j: (0, j + shift))
    bsp = lambda shift: pl.BlockSpec((1, LANES), lambda j: (0, j + shift))
    dx_shape = jax.ShapeDtypeStruct((rows, D_FF), BF16)
    dw_shape = jax.ShapeDtypeStruct((FFN_CONV, D_FF), F32)
    db_shape = jax.ShapeDtypeStruct((1, D_FF), F32)
    return _call(
        body, name=name, out_shape=(dx_shape, dx_shape, dw_shape, dw_shape, db_shape, db_shape), grid=(nt,),
        in_specs=[col(0), col(nt), col(0), wsp(0), wsp(nt), bsp(0), bsp(nt)],
        out_specs=(col(0), col(0), wsp(0), wsp(0), bsp(0), bsp(0)),
        operands=[up, up, dact, cw, cw, cb, cb], semantics=("parallel",), steps=steps)


def _dt_fwd(name, dtr, bias):
    rows = dtr.shape[0]
    tm = _row_tile(rows, LANES)

    def body(d_ref, b_ref, o_ref):
        v = d_ref[...] + b_ref[...]
        sp = jnp.maximum(v, 0.0) + jnp.log1p(jnp.exp(-jnp.abs(v)))
        lane = lax.broadcasted_iota(jnp.int32, (tm, LANES), 1)
        ok = _rows_mask(pl.program_id(0), tm) & (lane < SSM_HEADS)
        o_ref[...] = jnp.where(ok, sp, 0.0)

    return pl.pallas_call(
        body, name=name, out_shape=jax.ShapeDtypeStruct((rows, LANES), F32), grid=(rows // tm,),
        in_specs=[pl.BlockSpec((tm, LANES), lambda i: (i, 0)), pl.BlockSpec((1, LANES), lambda i: (0, 0))],
        out_specs=pl.BlockSpec((tm, LANES), lambda i: (i, 0)), compiler_params=_cparams(("parallel",)),
    )(dtr, bias)


def _dt_bwd(name, ddt, dtr, bias):
    rows = dtr.shape[0]
    tm = _row_tile(rows, LANES)

    def body(g_ref, d_ref, b_ref, o_ref, db_ref):
        i = pl.program_id(0)
        lane = lax.broadcasted_iota(jnp.int32, (tm, LANES), 1)
        ok = _rows_mask(i, tm) & (lane < SSM_HEADS)
        dv = jnp.where(ok, g_ref[...] * _sigmoid(d_ref[...] + b_ref[...]), 0.0)
        o_ref[...] = dv.astype(BF16)

        @pl.when(i == 0)
        def _():
            db_ref[...] = jnp.zeros_like(db_ref)

        db_ref[...] += jnp.sum(dv, axis=0, keepdims=True)

    row_spec = pl.BlockSpec((tm, LANES), lambda i: (i, 0))
    vec_spec = pl.BlockSpec((1, LANES), lambda i: (0, 0))
    return pl.pallas_call(
        body, name=name,
        out_shape=(jax.ShapeDtypeStruct((rows, LANES), BF16), jax.ShapeDtypeStruct((1, LANES), F32)),
        grid=(rows // tm,), in_specs=[row_spec, row_spec, vec_spec], out_specs=(row_spec, vec_spec),
        compiler_params=_cparams(("arbitrary",)),
    )(ddt, dtr, bias)


def _gate_fwd(name, y, zx, w, steps=()):
    rows = y.shape[0]
    tm = _row_tile(rows, D_INNER)

    def body(y_ref, z_ref, w_ref, o_ref):
        z = z_ref[...]
        g = y_ref[...] * (z * _sigmoid(z))
        r = lax.rsqrt(jnp.mean(g * g, axis=-1, keepdims=True) + RMS_EPS)
        o_ref[...] = (g * r * w_ref[...]).astype(BF16)

    row_spec = pl.BlockSpec((tm, D_INNER), lambda i: (i, 0))
    return _call(
        body, name=name, out_shape=jax.ShapeDtypeStruct((rows, D_INNER), BF16), grid=(rows // tm,),
        in_specs=[row_spec, row_spec, pl.BlockSpec((1, D_INNER), lambda i: (0, 0))],
        out_specs=row_spec, operands=[y, zx, w], semantics=("parallel",), steps=steps)


def _gate_bwd(name, dyn, y, zx, w):
    rows = y.shape[0]
    tm = _row_tile(rows, D_INNER)

    def body(d_ref, y_ref, z_ref, w_ref, dy_ref, dz_ref, dw_ref):
        i = pl.program_id(0)
        z, yv = z_ref[...], y_ref[...]
        sig = _sigmoid(z)
        sz = z * sig
        g = yv * sz
        r = lax.rsqrt(jnp.mean(g * g, axis=-1, keepdims=True) + RMS_EPS)
        ghat = g * r
        dn = d_ref[...]
        dghat = dn * w_ref[...]
        dg = r * (dghat - ghat * jnp.mean(dghat * ghat, axis=-1, keepdims=True))
        dy_ref[...] = dg * sz
        dz_ref[...] = (dg * yv * sig * (1.0 + z * (1.0 - sig))).astype(BF16)

        @pl.when(i == 0)
        def _():
            dw_ref[...] = jnp.zeros_like(dw_ref)

        dw_ref[...] += jnp.sum(dn * ghat, axis=0, keepdims=True)

    row_spec = pl.BlockSpec((tm, D_INNER), lambda i: (i, 0))
    vec_spec = pl.BlockSpec((1, D_INNER), lambda i: (0, 0))
    return pl.pallas_call(
        body, name=name,
        out_shape=(jax.ShapeDtypeStruct((rows, D_INNER), F32), jax.ShapeDtypeStruct((rows, D_INNER), BF16),
                   jax.ShapeDtypeStruct((1, D_INNER), F32)),
        grid=(rows // tm,), in_specs=[row_spec, row_spec, row_spec, vec_spec],
        out_specs=(row_spec, row_spec, vec_spec), compiler_params=_cparams(("arbitrary",)),
    )(dyn, y, zx, w)


def _split3(x):
    hi = x.astype(BF16)
    r1 = x - hi.astype(F32)
    mid = r1.astype(BF16)
    lo = (r1 - mid.astype(F32)).astype(BF16)
    return hi, mid, lo


def _dot3_data_lhs(x, sel):
    sel16 = sel.astype(F32).astype(BF16)
    hi, mid, lo = _split3(x)
    return _dot(hi, sel16) + _dot(mid, sel16) + _dot(lo, sel16)


def _dot3_data_rhs(sel, x):
    sel16 = sel.astype(F32).astype(BF16)
    hi, mid, lo = _split3(x)
    return _dot(sel16, hi) + _dot(sel16, mid) + _dot(sel16, lo)


def _causal_masks():
    r = lax.broadcasted_iota(jnp.int32, (CHUNK, CHUNK), 0)
    c = lax.broadcasted_iota(jnp.int32, (CHUNK, CHUNK), 1)
    return r >= c, r <= c


def _expand_heads_matrix(g):
    k = lax.broadcasted_iota(jnp.int32, (LANES, GROUP_W), 0)
    j = lax.broadcasted_iota(jnp.int32, (LANES, GROUP_W), 1)
    return HEADS_PER_GROUP * g + jnp.right_shift(j, 6) == k


def _reduce_heads_matrix(g):
    j = lax.broadcasted_iota(jnp.int32, (GROUP_W, LANES), 0)
    k = lax.broadcasted_iota(jnp.int32, (GROUP_W, LANES), 1)
    return HEADS_PER_GROUP * g + jnp.right_shift(j, 6) == k


def _reduce_pair_matrix(g, p):
    j = lax.broadcasted_iota(jnp.int32, (LANES, LANES), 0)
    k = lax.broadcasted_iota(jnp.int32, (LANES, LANES), 1)
    return HEADS_PER_GROUP * g + 2 * p + jnp.right_shift(j, 6) == k


def _group_cols(ref, g, width):
    return ref.at[:, pl.ds(g * width, width)]


def _ssd_prep(name, dt, a128, steps=()):
    rows = dt.shape[0]
    nc = rows // CHUNK

    def body(dt_ref, a_ref, dte_ref, acs_ref, acst_ref):
        causal, _ = _causal_masks()
        dtv = dt_ref[...]
        acs = _dot3_data_rhs(causal, dtv) * a_ref[...]
        acst_ref[...] = acs.T[0:SSM_HEADS]
        for g in range(N_GROUPS):
            expand = _expand_heads_matrix(g)
            _group_cols(dte_ref, g, GROUP_W)[...] = _dot3_data_lhs(dtv, expand)
            _group_cols(acs_ref, g, GROUP_W)[...] = _dot3_data_lhs(acs, expand)

    blk = pl.BlockSpec((CHUNK, D_INNER), lambda c: (c, 0))
    shp = jax.ShapeDtypeStruct((rows, D_INNER), F32)
    return _call(
        body, name=name, out_shape=(shp, shp, jax.ShapeDtypeStruct((nc, SSM_HEADS, CHUNK), F32)), grid=(nc,),
        in_specs=[pl.BlockSpec((CHUNK, LANES), lambda c: (c, 0)), pl.BlockSpec((1, LANES), lambda c: (0, 0))],
        out_specs=(blk, blk, pl.BlockSpec((None, SSM_HEADS, CHUNK), lambda c: (c, 0, 0))),
        operands=[dt, a128], semantics=("parallel",), steps=steps)


def _ssd_common(x_ref, b_ref, c_ref, dte_ref, acs_ref):
    x = x_ref[...]
    dt_exp = dte_ref[...]
    acs_exp = acs_ref[...]
    tot_exp = acs_ref[pl.ds(CHUNK - 1, 1), :]
    xdt = x * dt_exp
    e_exp = jnp.exp(acs_exp)
    f_exp = jnp.exp(tot_exp - acs_exp)
    return _causal_masks(), x, dt_exp, acs_exp, tot_exp, xdt, e_exp, f_exp, b_ref[...], c_ref[...]


def _pair_decay(acs_pair, acs_row, e, causal):
    lane = lax.broadcasted_iota(jnp.int32, (CHUNK, LANES), 1)
    mine = (lane < HEAD_DIM) if e == 0 else (lane >= HEAD_DIM)
    a_l = jnp.where(mine, acs_pair, pltpu.roll(acs_pair, HEAD_DIM, 1))
    seg = a_l - acs_row
    dm = jnp.where(causal[0], jnp.exp(jnp.minimum(seg, 0.0)), 0.0)
    dmt = jnp.where(causal[1], jnp.exp(jnp.minimum(-seg, 0.0)), 0.0)
    return dm, dmt


def _ssd_specs(index_of_chunk):
    wide = pl.BlockSpec((CHUNK, D_INNER), lambda c: (index_of_chunk(c), 0))
    b_spec = pl.BlockSpec((CHUNK, D_BC), lambda c: (index_of_chunk(c), D_INNER // D_BC))
    c_spec = pl.BlockSpec((CHUNK, D_BC), lambda c: (index_of_chunk(c), D_INNER // D_BC + 1))
    rows_spec = pl.BlockSpec((None, SSM_HEADS, CHUNK), lambda c: (index_of_chunk(c), 0, 0))
    state_spec = pl.BlockSpec((N_GROUPS, None, D_STATE, GROUP_W), lambda c: (0, index_of_chunk(c), 0, 0))
    return wide, b_spec, c_spec, rows_spec, state_spec


def _ssd_fwd(name, xbc, dt_exp, acs_exp, acs_rows, dskexp, steps=()):
    rows = xbc.shape[0]
    nc = rows // CHUNK

    def body(x_ref, b_ref, c_ref, dte_ref, acs_ref, acst_ref, dsk_ref, y_ref, st_ref, s_scr):
        @pl.when(pl.program_id(0) == 0)
        def _():
            s_scr[...] = jnp.zeros_like(s_scr)

        lane = lax.broadcasted_iota(jnp.int32, (CHUNK, LANES), 1)
        for g in range(N_GROUPS):
            y_g = _group_cols(y_ref, g, GROUP_W)
            causal, x, _, acs_exp_v, tot_exp, xdt, e_exp, f_exp, bm, cm = _ssd_common(
                _group_cols(x_ref, g, GROUP_W), _group_cols(b_ref, g, D_STATE), _group_cols(c_ref, g, D_STATE),
                _group_cols(dte_ref, g, GROUP_W), _group_cols(acs_ref, g, GROUP_W))
            state = s_scr[g]
            st_ref[g] = state
            cb16, bb16 = cm.astype(BF16), bm.astype(BF16)
            cb = _dot_nt(cb16, bb16)
            base = e_exp * _dot(cb16, state.astype(BF16)) + _group_cols(dsk_ref, g, GROUP_W)[...] * x
            for p in range(HEADS_PER_GROUP // 2):
                sl = slice(p * LANES, (p + 1) * LANES)
                xp = xdt[:, sl].astype(BF16)
                yd = []
                for e in range(2):
                    acs_row = acst_ref[pl.ds(g * HEADS_PER_GROUP + 2 * p + e, 1), :]
                    dm, _ = _pair_decay(acs_exp_v[:, sl], acs_row, e, causal)
                    yd.append(_dot((cb * dm).astype(BF16), xp))
                y_g[:, sl] = jnp.where(lane < HEAD_DIM, yd[0], yd[1]) + base[:, sl]
            s_scr[g] = jnp.exp(tot_exp) * state + _dot_tn(bb16, (f_exp * xdt).astype(BF16))

    wide, b_spec, c_spec, rows_spec, state_spec = _ssd_specs(lambda c: c)
    return _call(
        body, name=name,
        out_shape=(jax.ShapeDtypeStruct((rows, D_INNER), F32),
                   jax.ShapeDtypeStruct((N_GROUPS, nc, D_STATE, GROUP_W), F32)),
        grid=(nc,),
        in_specs=[wide, b_spec, c_spec, wide, wide, rows_spec, pl.BlockSpec((1, D_INNER), lambda c: (0, 0))],
        out_specs=(wide, state_spec),
        scratch_shapes=[pltpu.VMEM((N_GROUPS, D_STATE, GROUP_W), F32)],
        operands=[xbc, xbc, xbc, dt_exp, acs_exp, acs_rows, dskexp], semantics=("arbitrary",), steps=steps)


def _ssd_bwd(name, xbc, dt_exp, acs_exp, acs_rows, dt, a128, dskexp, dy, states, steps=()):
    rows = xbc.shape[0]
    nc = rows // CHUNK
    last = nc - 1

    def body(x_ref, b_ref, c_ref, dte_ref, acs_ref, acst_ref, dt_ref, a128_ref, dsk_all, dy_all, st_all,
             dx_all, db_all, dc_all, ddt_ref, dalog_ref, ddsk_ref, ds_all):
        @pl.when(pl.program_id(0) == 0)
        def _():
            ds_all[...] = jnp.zeros_like(ds_all)
            dalog_ref[...] = jnp.zeros_like(dalog_ref)
            ddsk_ref[...] = jnp.zeros_like(ddsk_ref)

        dacs = jnp.zeros((CHUNK, LANES), F32)
        ddt_x = jnp.zeros((CHUNK, LANES), F32)
        for g in range(N_GROUPS):
            dacs_g, ddt_x_g = group(
                g, _group_cols(x_ref, g, GROUP_W), _group_cols(b_ref, g, D_STATE), _group_cols(c_ref, g, D_STATE),
                _group_cols(dte_ref, g, GROUP_W), _group_cols(acs_ref, g, GROUP_W), acst_ref,
                _group_cols(dsk_all, g, GROUP_W), _group_cols(dy_all, g, GROUP_W), st_all.at[g],
                _group_cols(dx_all, g, GROUP_W), _group_cols(db_all, g, D_STATE), _group_cols(dc_all, g, D_STATE),
                ddsk_ref, ds_all.at[g])
            dacs, ddt_x = dacs + dacs_g, ddt_x + ddt_x_g
        _, causal_t = _causal_masks()
        da = _dot3_data_rhs(causal_t, dacs)
        ddt_ref[...] = da * a128_ref[...] + ddt_x
        dalog_ref[...] += jnp.sum(da * dt_ref[...], axis=0, keepdims=True) * a128_ref[...]

    def group(g, x_ref, b_ref, c_ref, dte_ref, acs_ref, acst_ref, dsk_ref, dy_ref, st_ref,
              dx_ref, db_ref, dc_ref, ddsk_ref, ds_scr):
        causal, x, dt_exp, acs_exp_v, tot_exp, xdt, e_exp, f_exp, bm, cm = _ssd_common(
            x_ref, b_ref, c_ref, dte_ref, acs_ref)
        reduce_heads = _reduce_heads_matrix(g)
        state, dstate = st_ref[...], ds_scr[...]
        dyv = dy_ref[...]
        cb16, bb16 = cm.astype(BF16), bm.astype(BF16)
        s16, ds16 = state.astype(BF16), dstate.astype(BF16)
        cb = _dot_nt(cb16, bb16)
        cbt = _dot_nt(bb16, cb16)
        cs = _dot(cb16, s16)
        bds = _dot(bb16, ds16)
        edy = e_exp * dyv
        fx = f_exp * xdt
        dxdt_base = f_exp * bds
        dc_acc = _dot_nt(edy.astype(BF16), s16)
        db_acc = _dot_nt(fx.astype(BF16), ds16)
        ds_scr[...] = jnp.exp(tot_exp) * dstate + _dot_tn(cb16, edy.astype(BF16))
        q = fx * bds
        dacs = _dot3_data_lhs(edy * cs - q, reduce_heads)
        dtot = jnp.sum(_dot3_data_lhs(q + jnp.exp(tot_exp) * dstate * state, reduce_heads), axis=0, keepdims=True)
        ddsk_ref[...] += jnp.sum(_dot3_data_lhs(dyv * x, reduce_heads), axis=0, keepdims=True)
        lane = lax.broadcasted_iota(jnp.int32, (CHUNK, LANES), 1)
        dcb = jnp.zeros((CHUNK, CHUNK), F32)
        dcbt = jnp.zeros((CHUNK, CHUNK), F32)
        ddt_x = jnp.zeros((CHUNK, LANES), F32)
        for p in range(HEADS_PER_GROUP // 2):
            sl = slice(p * LANES, (p + 1) * LANES)
            xp, dyp = xdt[:, sl], dyv[:, sl]
            xp16, dyp16 = xp.astype(BF16), dyp.astype(BF16)
            dxh = []
            for e in range(2):
                h = 2 * p + e
                mine = (lane < HEAD_DIM) if e == 0 else (lane >= HEAD_DIM)
                acs_row = acst_ref[pl.ds(g * HEADS_PER_GROUP + h, 1), :]
                dm, dmt = _pair_decay(acs_exp_v[:, sl], acs_row, e, causal)
                m, mt = cb * dm, cbt * dmt
                xh16 = jnp.where(mine, xp, 0.0).astype(BF16)
                dyh16 = jnp.where(mine, dyp, 0.0).astype(BF16)
                d_m = _dot_nt(dyh16, xp16)
                d_mt = _dot_nt(xh16, dyp16)
                dacs_h = (jnp.sum(d_m * m, axis=-1, keepdims=True)
                          - jnp.sum(d_mt * mt, axis=-1, keepdims=True))
                dacs = dacs + jnp.where(lane == HEADS_PER_GROUP * g + h, dacs_h, 0.0)
                dcb = dcb + d_m * dm
                dcbt = dcbt + d_mt * dmt
                dxh.append(_dot(mt.astype(BF16), dyp16))
            dxdt = jnp.where(lane < HEAD_DIM, dxh[0], dxh[1]) + dxdt_base[:, sl]
            dx_ref[:, sl] = dxdt * dt_exp[:, sl] + dsk_ref[:, sl] * dyp
            ddt_x = ddt_x + _dot3_data_lhs(dxdt * x[:, sl], _reduce_pair_matrix(g, p))
        dc_ref[...] = dc_acc + _dot(dcb.astype(BF16), bb16)
        db_ref[...] = db_acc + _dot(dcbt.astype(BF16), cb16)
        row = lax.broadcasted_iota(jnp.int32, (CHUNK, LANES), 0)
        return dacs + jnp.where(row == CHUNK - 1, dtot, 0.0), ddt_x

    wide, b_spec, c_spec, rows_spec, state_spec = _ssd_specs(lambda c: last - c)
    heads_spec = pl.BlockSpec((CHUNK, LANES), lambda c: (last - c, 0))
    vec_spec = pl.BlockSpec((1, LANES), lambda c: (0, 0))
    bc_out = pl.BlockSpec((CHUNK, D_BC), lambda c: (last - c, 0))
    vec_shape = jax.ShapeDtypeStruct((1, LANES), F32)
    return _call(
        body, name=name,
        out_shape=(jax.ShapeDtypeStruct((rows, D_INNER), F32), jax.ShapeDtypeStruct((rows, D_BC), F32),
                   jax.ShapeDtypeStruct((rows, D_BC), F32), jax.ShapeDtypeStruct((rows, LANES), F32),
                   vec_shape, vec_shape),
        grid=(nc,),
        in_specs=[wide, b_spec, c_spec, wide, wide, rows_spec, heads_spec, vec_spec,
                  pl.BlockSpec((1, D_INNER), lambda c: (0, 0)), wide, state_spec],
        out_specs=(wide, bc_out, bc_out, heads_spec, vec_spec, vec_spec),
        scratch_shapes=[pltpu.VMEM((N_GROUPS, D_STATE, GROUP_W), F32)],
        operands=[xbc, xbc, xbc, dt_exp, acs_exp, acs_rows, dt, a128, dskexp, dy, states],
        semantics=("arbitrary",), steps=steps)


def _attn_visible(b, heads=1):
    row = jnp.bitwise_and(lax.broadcasted_iota(jnp.int32, (heads * CHUNK, 3 * CHUNK), 0), CHUNK - 1)
    col = lax.broadcasted_iota(jnp.int32, (heads * CHUNK, 3 * CHUNK), 1)
    bb = b + jnp.zeros_like(col)
    meta = (col < CHUNK) & (bb >= 1) & (col >= PAD_ROWS)
    prev = (col >= CHUNK) & (col < 2 * CHUNK) & (bb >= 2) & ((col - CHUNK) > row)
    cur = (col >= 2 * CHUNK) & ((col - 2 * CHUNK) <= row) & ((bb >= 1) | ((col - 2 * CHUNK) >= PAD_ROWS))
    return meta | prev | cur


def _attn_visible4(b):
    return _attn_visible(b, 4)


def _stack_heads(q_ref, sink_ref, kvh, scale):
    lane = lax.broadcasted_iota(jnp.int32, (CHUNK, LANES), 1)
    parts, sinks = [], []
    for pp in range(2):
        pair = kvh * 2 + pp
        qp = q_ref[:, pair * LANES:(pair + 1) * LANES] * scale
        for e in range(2):
            mine = (lane < HEAD_DIM) if e == 0 else (lane >= HEAD_DIM)
            parts.append(jnp.where(mine, qp, 0.0).astype(BF16))
            sinks.append(jnp.full((CHUNK, 1), sink_ref[2 * pair + e], F32))
    return jnp.concatenate(parts, axis=0), jnp.concatenate(sinks, axis=0)


def _attn_operands(q_ref, k0, kp, kc, v0, vp, vc, sink_ref):
    kcat, vcat, q4, sink4 = [], [], [], []
    for kvh in range(N_KV_HEADS):
        ksl = slice(kvh * LANES, (kvh + 1) * LANES)
        kcat.append(jnp.concatenate([k0[:, ksl], kp[:, ksl], kc[:, ksl]], axis=0).astype(BF16))
        vcat.append(jnp.concatenate([v0[:, ksl], vp[:, ksl], vc[:, ksl]], axis=0).astype(BF16))
        stacked, sinks = _stack_heads(q_ref, sink_ref, kvh, ATTN_SCALE)
        q4.append(stacked)
        sink4.append(sinks)
    return kcat, vcat, q4, sink4


def _attn_probs(q4, kcat, visible, sink4):
    heads = range(N_KV_HEADS)
    s = [jnp.where(visible, _dot_nt(q4[h], kcat[h]), NEG_INF) for h in heads]
    m = [jnp.maximum(jnp.max(s[h], axis=-1, keepdims=True), sink4[h]) for h in heads]
    pe = [jnp.exp(s[h] - m[h]) for h in heads]
    pe_sink = [jnp.exp(sink4[h] - m[h]) for h in heads]
    inv = [1.0 / (jnp.sum(pe[h], axis=-1, keepdims=True) + pe_sink[h]) for h in heads]
    return [pe[h] * inv[h] for h in heads], [pe_sink[h] * inv[h] for h in heads]


def _unstack_pairs(stacked, pp):
    lane = lax.broadcasted_iota(jnp.int32, (CHUNK, LANES), 1)
    return jnp.where(lane < HEAD_DIM, stacked[(2 * pp) * CHUNK:(2 * pp + 1) * CHUNK],
                     stacked[(2 * pp + 1) * CHUNK:(2 * pp + 2) * CHUNK])


def _attn_specs(colblock):
    blk = lambda f: pl.BlockSpec((CHUNK, 2 * D_KV), f)
    return [blk(lambda b: (0, colblock)), blk(lambda b: (jnp.maximum(b - 1, 0), colblock)), blk(lambda b: (b, colblock))]


def _attn_fwd(name, q, kv2, sinks, steps=()):
    rows = q.shape[0]

    def body(q_ref, k0, kp, kc, v0, vp, vc, sink_ref, o_ref):
        visible = _attn_visible4(pl.program_id(0))
        kcat, vcat, q4, sink4 = _attn_operands(q_ref, k0, kp, kc, v0, vp, vc, sink_ref)
        pn, _ = _attn_probs(q4, kcat, visible, sink4)
        o4 = [_dot(pn[h].astype(BF16), vcat[h]) for h in range(N_KV_HEADS)]
        for kvh in range(N_KV_HEADS):
            for pp in range(2):
                qsl = slice((kvh * 2 + pp) * LANES, (kvh * 2 + pp + 1) * LANES)
                o_ref[:, qsl] = _unstack_pairs(o4[kvh], pp).astype(BF16)

    return _call(
        body, name=name, out_shape=jax.ShapeDtypeStruct((rows, D_MODEL), BF16), grid=(rows // CHUNK,),
        in_specs=[pl.BlockSpec((CHUNK, D_MODEL), lambda b: (b, 0))] + _attn_specs(0) + _attn_specs(1)
        + [pl.BlockSpec(memory_space=pltpu.SMEM)],
        out_specs=pl.BlockSpec((CHUNK, D_MODEL), lambda b: (b, 0)),
        operands=[q, kv2, kv2, kv2, kv2, kv2, kv2, sinks], semantics=("parallel",), steps=steps)


def _attn_bwd(name, q, kv2, sinks, do, steps=()):
    rows = q.shape[0]

    def body(q_ref, k0, kp, kc, v0, vp, vc, sink_ref, do_ref,
             dq_ref, dkc_ref, dkp_ref, dvc_ref, dvp_ref, dkm_ref, dvm_ref, dsink_ref):
        @pl.when(pl.program_id(0) == 0)
        def _():
            dkm_ref[...] = jnp.zeros_like(dkm_ref)
            dvm_ref[...] = jnp.zeros_like(dvm_ref)
            dsink_ref[...] = jnp.zeros_like(dsink_ref)

        visible = _attn_visible4(pl.program_id(0))
        heads = range(N_KV_HEADS)
        lane1 = lax.broadcasted_iota(jnp.int32, (1, LANES), 1)
        kcat, vcat, q4, sink4 = _attn_operands(q_ref, k0, kp, kc, v0, vp, vc, sink_ref)
        do4 = [_stack_heads(do_ref, sink_ref, h, 1.0)[0] for h in heads]
        pn, psink = _attn_probs(q4, kcat, visible, sink4)
        dp = [_dot_nt(do4[h], vcat[h]) for h in heads]
        delta = [jnp.sum(pn[h] * dp[h], axis=-1, keepdims=True) for h in heads]
        ds16 = [(pn[h] * (dp[h] - delta[h])).astype(BF16) for h in heads]
        dq4 = [_dot(ds16[h], kcat[h]) for h in heads]
        dk_acc = [_dot_tn(ds16[h], q4[h]) for h in heads]
        dv_acc = [_dot_tn(pn[h].astype(BF16), do4[h]) for h in heads]
        dsink = jnp.zeros((1, LANES), F32)
        for kvh in heads:
            ksl = slice(kvh * LANES, (kvh + 1) * LANES)
            sink_terms = psink[kvh] * delta[kvh]
            for j in range(4):
                part = jnp.sum(sink_terms[j * CHUNK:(j + 1) * CHUNK], axis=0, keepdims=True)
                dsink = dsink - jnp.where(lane1 == kvh * 4 + j, part, 0.0)
            for pp in range(2):
                qsl = slice((kvh * 2 + pp) * LANES, (kvh * 2 + pp + 1) * LANES)
                dq_ref[:, qsl] = (_unstack_pairs(dq4[kvh], pp) * ATTN_SCALE).astype(BF16)
            dkm_ref[:, ksl] += dk_acc[kvh][0:CHUNK]
            dvm_ref[:, ksl] += dv_acc[kvh][0:CHUNK]
            dkp_ref[:, ksl] = dk_acc[kvh][CHUNK:2 * CHUNK]
            dvp_ref[:, ksl] = dv_acc[kvh][CHUNK:2 * CHUNK]
            dkc_ref[:, ksl] = dk_acc[kvh][2 * CHUNK:3 * CHUNK]
            dvc_ref[:, ksl] = dv_acc[kvh][2 * CHUNK:3 * CHUNK]
        dsink_ref[...] += dsink

    qspec = pl.BlockSpec((CHUNK, D_MODEL), lambda b: (b, 0))
    kvspec = pl.BlockSpec((CHUNK, 2 * D_KV), lambda b: (b, 0))
    fixed = pl.BlockSpec((CHUNK, 2 * D_KV), lambda b: (0, 0))
    kv_shape = jax.ShapeDtypeStruct((rows, 2 * D_KV), F32)
    meta_shape = jax.ShapeDtypeStruct((CHUNK, 2 * D_KV), F32)
    return _call(
        body, name=name,
        out_shape=(jax.ShapeDtypeStruct((rows, D_MODEL), BF16), kv_shape, kv_shape, kv_shape, kv_shape,
                   meta_shape, meta_shape, jax.ShapeDtypeStruct((1, LANES), F32)),
        grid=(rows // CHUNK,),
        in_specs=[qspec] + _attn_specs(0) + _attn_specs(1) + [pl.BlockSpec(memory_space=pltpu.SMEM), qspec],
        out_specs=(qspec, kvspec, kvspec, kvspec, kvspec, fixed, fixed, pl.BlockSpec((1, LANES), lambda b: (0, 0))),
        operands=[q, kv2, kv2, kv2, kv2, kv2, kv2, sinks, do], semantics=("arbitrary",), steps=steps)


def _kv_grad_combine(name, dk_cur, dk_prev, dk_meta, dv_cur, dv_prev, dv_meta):
    rows = dk_cur.shape[0]
    nb = rows // CHUNK
    width = 2 * D_KV

    def body(kc_ref, kp_ref, km_ref, vc_ref, vp_ref, vm_ref, o_ref):
        jj = pl.program_id(0) + jnp.zeros((CHUNK, 1), jnp.int32)
        for half, (c_ref, p_ref, m_ref) in enumerate(((kc_ref, kp_ref, km_ref), (vc_ref, vp_ref, vm_ref))):
            total = c_ref[...] + jnp.where(jj < nb - 1, p_ref[...], 0.0) + jnp.where(jj == 0, m_ref[...], 0.0)
            o_ref[:, half * width:(half + 1) * width] = total.astype(BF16)

    blk = lambda f: pl.BlockSpec((CHUNK, width), f)
    three = lambda: [blk(lambda j: (j, 0)), blk(lambda j: (jnp.minimum(j + 1, nb - 1), 0)), blk(lambda j: (0, 0))]
    return pl.pallas_call(
        body, name=name, out_shape=jax.ShapeDtypeStruct((rows, 2 * width), BF16), grid=(nb,),
        in_specs=three() + three(), out_specs=pl.BlockSpec((CHUNK, 2 * width), lambda j: (j, 0)),
        compiler_params=_cparams(("parallel",)),
    )(dk_cur, dk_prev, dk_meta, dv_cur, dv_prev, dv_meta)


def _adamw(name, w, g, m, v, steps=()):
    rows, width = w.shape
    tr = rows
    for cand in range(8, rows + 1, 8):
        if rows % cand == 0 and cand * width * 4 <= (1 << 20):
            tr = cand

    def body(w_ref, g_ref, m_ref, v_ref, d_ref, mo_ref, vo_ref):
        gv = g_ref[...]
        mn = ADAM_B1 * m_ref[...] + (1.0 - ADAM_B1) * gv
        vn = ADAM_B2 * v_ref[...] + (1.0 - ADAM_B2) * (gv * gv)
        m_hat = mn / (1.0 - ADAM_B1 ** ADAM_STEP)
        v_hat = vn / (1.0 - ADAM_B2 ** ADAM_STEP)
        d_ref[...] = -ADAM_LR * (m_hat / (jnp.sqrt(v_hat) + ADAM_EPS) + ADAM_WD * w_ref[...])
        mo_ref[...] = mn
        vo_ref[...] = vn

    blk = pl.BlockSpec((tr, width), lambda i: (i, 0))
    shp = jax.ShapeDtypeStruct((rows, width), F32)
    return _call(body, name=name, out_shape=(shp, shp, shp), grid=(rows // tr,), in_specs=[blk] * 4,
                 out_specs=(blk,) * 3, operands=[w, g, m, v], semantics=("parallel",), steps=steps)


class _GivenWeights:
    def __init__(self, p):
        self.p = p
        self.grads = {}

    def weight(self, name, layer=None):
        return self.p[name] if layer is None else self.p[name][layer]

    def steps(self, kernel):
        return ()

    def grad(self, name, layer, g):
        self.grads[(name, layer)] = g


def _ffn_fwd(tag, h, hn, p, i, plan):
    up = _mm_nn_bychip(f"ffn{tag}_up", hn, plan.weight("f_w_up", i), steps=plan.steps(f"ffn{tag}_up"))
    act = _ffn_conv_fwd(f"ffn{tag}_conv", up, p["f_conv_w"][i], p["f_conv_b"][i:i + 1], steps=plan.steps(f"ffn{tag}_conv"))
    pre = _mm(f"ffn{tag}_down", act, plan.weight("f_w_down", i), "nn")
    return pre, (h, hn, up, act, pre)


def _ffn_bwd(tag, dpre, saved, p, i, plan):
    h, hn, up, act, pre = saved
    plan.grad("f_w_down", i, _mm(f"ffn{tag}_down_dw", act, dpre, "tn", out_dtype=BF16))
    dact = _mm(f"ffn{tag}_down_dx", dpre, plan.weight("f_w_down", i), "nt", steps=plan.steps(f"ffn{tag}_down_dx"))
    dug, duv, gwg, gwv, gbg, gbv = _ffn_conv_bwd(f"ffn{tag}_conv_bwd", up, dact, p["f_conv_w"][i], p["f_conv_b"][i:i + 1],
                                                 steps=plan.steps(f"ffn{tag}_conv_bwd"))
    g_cw, g_cb = jnp.concatenate([gwg, gwv], axis=1), jnp.concatenate([gbg, gbv], axis=1)
    w_up = plan.weight("f_w_up", i)
    n = w_up.shape[2]
    dhn = _mm_nt_bychip(f"ffn{tag}_up_dx_gate", dug, w_up, 0)
    dhn = _mm_nt_bychip(f"ffn{tag}_up_dx_val", duv, w_up, N_CHIPS // 2, acc=dhn)
    g_up = _mm_tn_bychip(f"ffn{tag}_up_dw_gate", hn, dug, n, 0)
    plan.grad("f_w_up", i, _mm_tn_bychip(f"ffn{tag}_up_dw_val", hn, duv, n, N_CHIPS // 2, into=g_up))
    return dhn, dict(f_conv_w=g_cw, f_conv_b=g_cb)


def _lanes_pad(a, width=LANES):
    return jnp.pad(a, [(0, 0)] * (a.ndim - 1) + [(0, width - a.shape[-1])])


def _dup_heads(w):
    rows = w.shape[0]
    w = w.reshape(rows, 2 * N_KV_HEADS, 1, HEAD_DIM)
    return jnp.broadcast_to(w, (rows, 2 * N_KV_HEADS, 2, HEAD_DIM)).reshape(rows, 4 * D_KV)


def _undup_heads(g):
    rows = g.shape[0]
    return g.reshape(rows, 2 * N_KV_HEADS, 2, HEAD_DIM).sum(axis=2).reshape(rows, 2 * D_KV)


def _local_step(x2, target, p, plan):
    seq = x2.shape[0]
    rows = seq + CHUNK
    g = {}

    h0 = jnp.concatenate([jnp.zeros((PAD_ROWS, D_MODEL), F32), p["meta_tokens"], x2], axis=0)

    w_in = plan.weight("a_w_in")
    w_dt = jnp.pad(w_in[D_MAIN:], ((0, LANES - SSM_HEADS), (0, 0)))
    dt_bias = _lanes_pad(p["a_dt_bias"])
    a128 = _lanes_pad(-jnp.exp(p["a_a_log"]))
    dskexp = jnp.repeat(p["a_d_skip"].reshape(SSM_HEADS), HEAD_DIM).reshape(1, D_INNER)

    hn0 = _rms_fwd("a_norm", h0, p["a_norm_pre"])
    zx = _mm("a_in_main", hn0, w_in, "nt", k_rows=D_MAIN, steps=plan.steps("a_in_main"))
    dtr = _mm("a_in_dt", hn0, w_dt, "nt")
    xbc = _conv4_fwd("a_conv", zx, p["a_conv_w"], p["a_conv_b"], steps=plan.steps("a_conv"))
    dt = _dt_fwd("a_dt", dtr, dt_bias)
    dt_exp, acs_exp, acs_rows = _ssd_prep("a_ssd_prep", dt, a128, steps=plan.steps("a_ssd_prep"))
    y, states = _ssd_fwd("a_ssd", xbc, dt_exp, acs_exp, acs_rows, dskexp, steps=plan.steps("a_ssd"))
    yn = _gate_fwd("a_gate", y, zx, p["a_gate_norm"], steps=plan.steps("a_gate"))
    mix = _mm("a_out", yn, plan.weight("a_w_out"), "nn", steps=plan.steps("a_out"))
    h1, (hn_f0,) = _resid_norm_fwd("a_resid", h0, mix, p["a_norm_post"], [p["f_norm_pre"][0:1]])

    pre_f0, ffn0 = _ffn_fwd("0", h1, hn_f0, p, 0, plan)
    h2, (hkv, hn2) = _resid_norm_fwd("ffn0_resid", h1, pre_f0, p["f_norm_post"][0:1], [p["kv_norm"], p["b_norm_pre"]])

    w_kv2 = _dup_heads(plan.weight("w_kv"))
    kv2 = _mm("kv_proj", hkv, w_kv2, "nn")
    q = _mm("b_q", hn2, plan.weight("b_w_q"), "nn")
    sinks = p["b_sinks"].reshape(N_Q_HEADS)
    o = _attn_fwd("b_attn", q, kv2, sinks, steps=plan.steps("b_attn"))
    attn = _mm("b_o", o, plan.weight("b_w_o"), "nn", steps=plan.steps("b_o"))
    h3, (hn_f1,) = _resid_norm_fwd("b_resid", h2, attn, p["b_norm_post"], [p["f_norm_pre"][1:2]])

    pre_f1, ffn1 = _ffn_fwd("1", h3, hn_f1, p, 1, plan)
    dh, loss_vec, dpre_f1, g_post1 = _resid_norm_loss("ffn1_resid_loss", h3, pre_f1, p["f_norm_post"][1:2], target)
    loss = loss_vec[0, 0]

    dhn_f1, g1 = _ffn_bwd("1", dpre_f1, ffn1, p, 1, plan)
    dh, g_pre1, dpre, g["b_norm_post"] = _norm_bwd_add("ffn1_norm_bwd", dh, dhn_f1, h3, p["f_norm_pre"][1:2],
                                                        then=(attn, p["b_norm_post"]))
    plan.grad("b_w_o", None, _mm("b_o_dw", o, dpre, "tn", out_dtype=BF16))
    do = _mm("b_o_dx", dpre, plan.weight("b_w_o"), "nt", steps=plan.steps("b_o_dx"))
    dq, dkc, dkp, dvc, dvp, dkm, dvm, dsink = _attn_bwd("b_attn_bwd", q, kv2, sinks, do, steps=plan.steps("b_attn_bwd"))
    g["b_sinks"] = dsink[:, :N_Q_HEADS]
    dhn2 = _mm("b_q_dx", dq, plan.weight("b_w_q"), "nt")
    plan.grad("b_w_q", None, _mm("b_q_dw", hn2, dq, "tn", out_dtype=BF16))
    dh, g["b_norm_pre"] = _norm_bwd_add("b_norm_bwd", dh, dhn2, h2, p["b_norm_pre"])
    dkv2 = _kv_grad_combine("kv_grad", dkc, dkp, dkm, dvc, dvp, dvm)
    dhkv = _mm("kv_proj_dx", dkv2, w_kv2, "nt")
    plan.grad("w_kv", None, _undup_heads(_mm("kv_proj_dw", hkv, dkv2, "tn")))
    dh, g["kv_norm"], dpre_f0, g_post0 = _norm_bwd_add("kv_norm_bwd", dh, dhkv, h2, p["kv_norm"],
                                                       then=(pre_f0, p["f_norm_post"][0:1]))

    dhn_f0, g0 = _ffn_bwd("0", dpre_f0, ffn0, p, 0, plan)
    dh, g_pre0, dpre, g["a_norm_post"] = _norm_bwd_add("ffn0_norm_bwd", dh, dhn_f0, h1, p["f_norm_pre"][0:1],
                                                        then=(mix, p["a_norm_post"]))
    g["f_norm_post"] = jnp.concatenate([g_post0, g_post1], axis=0)
    g["f_norm_pre"] = jnp.concatenate([g_pre0, g_pre1], axis=0)
    g["f_conv_w"] = jnp.stack([g0["f_conv_w"], g1["f_conv_w"]])
    g["f_conv_b"] = jnp.concatenate([g0["f_conv_b"], g1["f_conv_b"]], axis=0)
    plan.grad("a_w_out", None, _mm("a_out_dw", yn, dpre, "tn", out_dtype=BF16))
    dyn = _mm("a_out_dx", dpre, plan.weight("a_w_out"), "nt", steps=plan.steps("a_out_dx"))
    dy, dz, g["a_gate_norm"] = _gate_bwd("a_gate_bwd", dyn, y, zx, p["a_gate_norm"])
    dxs, dbm, dcm, ddt, dalog, ddsk = _ssd_bwd("a_ssd_bwd", xbc, dt_exp, acs_exp, acs_rows, dt, a128, dskexp, dy, states,
                                              steps=plan.steps("a_ssd_bwd"))
    g["a_a_log"] = dalog[:, :SSM_HEADS]
    g["a_d_skip"] = ddsk[:, :SSM_HEADS]
    ddtr, dbias = _dt_bwd("a_dt_bwd", ddt, dtr, dt_bias)
    g["a_dt_bias"] = dbias[:, :SSM_HEADS]
    dxp, gw_x, gb_x = _conv4_bwd("a_conv_bwd_x", zx, dxs, p["a_conv_w"], p["a_conv_b"], 0)
    dbp, gw_b, gb_b = _conv4_bwd("a_conv_bwd_b", zx, dbm, p["a_conv_w"], p["a_conv_b"], D_INNER)
    dcp, gw_c, gb_c = _conv4_bwd("a_conv_bwd_c", zx, dcm, p["a_conv_w"], p["a_conv_b"], D_INNER + D_BC)
    g["a_conv_w"] = jnp.concatenate([gw_x, gw_b, gw_c], axis=1)
    g["a_conv_b"] = jnp.concatenate([gb_x, gb_b, gb_c], axis=1)
    dzx = jnp.concatenate([dz, dxp, dbp, dcp], axis=1)
    g_in = _mm("a_in_main_dw", dzx, hn0, "tn", out_dtype=BF16, out_rows=D_IN_PROJ, steps=plan.steps("a_in_main_dw"))
    plan.grad("a_w_in", None, _tn_rows_into("a_in_dt_dw", ddtr, hn0, g_in, D_MAIN, SSM_HEADS))
    dhn0 = _mm("a_in_dt_dx", ddtr, w_dt, "nn", steps=plan.steps("a_in_dt_dx"))
    dhn0 = _mm("a_in_main_dx", dzx, w_in, "nn", acc=dhn0, steps=plan.steps("a_in_main_dx"))
    dh, g["a_norm_pre"] = _norm_bwd_add("a_norm_bwd", dh, dhn0, h0, p["a_norm_pre"], steps=plan.steps("a_norm_bwd"))

    g["meta_tokens"] = dh[PAD_ROWS:CHUNK]
    return loss, dh[CHUNK:], g


ANY = pl.BlockSpec(memory_space=pl.ANY)
VMEM_SPEC = pl.BlockSpec(memory_space=pltpu.VMEM)


def _allgather_small(name, shard):
    rows = shard.shape[0]

    def body(s_ref, o_ref, send_sems, recv_sems):
        x, y, c = _place()
        me = 2 * x + y
        o_ref[me] = s_ref[...]
        chips = _other_chips(x, y)
        sends = [pltpu.make_async_remote_copy(s_ref, o_ref.at[me], send_sems.at[j], recv_sems.at[j],
                                              device_id=(cx, cy, c), device_id_type=MESH)
                 for j, (cx, cy) in enumerate(chips)]
        for cp in sends:
            cp.start()
        for j, (cx, cy) in enumerate(chips):
            pltpu.make_async_remote_copy(s_ref, o_ref.at[2 * cx + cy], send_sems.at[j], recv_sems.at[j],
                                         device_id=(cx, cy, c), device_id_type=MESH).wait_recv()
        for cp in sends:
            cp.wait_send()

    return pl.pallas_call(
        body, name=name, out_shape=jax.ShapeDtypeStruct((N_CHIPS, rows, LANES), F32),
        in_specs=[VMEM_SPEC], out_specs=VMEM_SPEC,
        scratch_shapes=[pltpu.SemaphoreType.DMA((3,)), pltpu.SemaphoreType.DMA((3,))],
        compiler_params=pltpu.CompilerParams(vmem_limit_bytes=VMEM_LIMIT),
    )(shard)


def _row_block(rows, width, itemsize, align, budget=2 << 20):
    best = rows
    for cand in range(align, rows + 1, align):
        if rows % cand == 0 and cand * width * itemsize <= budget:
            best = cand
    return best


def _cast_into_slot(name, chip, w, layer=None):
    rows, width = w.shape[-2:]
    tr = _row_block(rows, width, 4, 16)
    if layer is None:
        in_spec = pl.BlockSpec((tr, width), lambda i, chip_ref: (i, 0))
    else:
        in_spec = pl.BlockSpec((None, tr, width), lambda i, chip_ref: (layer, i, 0))

    def body(chip_ref, w_ref, o_ref):
        o_ref[...] = w_ref[...].astype(BF16)

    return pl.pallas_call(
        body, name=name, out_shape=jax.ShapeDtypeStruct((N_CHIPS, rows, width), BF16),
        grid_spec=pltpu.PrefetchScalarGridSpec(
            num_scalar_prefetch=1, grid=(rows // tr,), in_specs=[in_spec],
            out_specs=pl.BlockSpec((None, tr, width), lambda i, chip_ref: (chip_ref[0], i, 0))),
        compiler_params=_cparams(("parallel",)),
    )(chip, w)


def _allreduce_small(name, vec):
    rows = vec.shape[0]

    def body(v_ref, o_ref, buf, send_sems, recv_sems):
        x, y, c = _place()
        me = 4 * x + 2 * y + c
        buf[me] = v_ref[...]

        def peer(k):
            kx, ky, kc = (k >> 2) & 1, (k >> 1) & 1, k & 1
            return (1 - x if kx else x, 1 - y if ky else y, 1 - c if kc else c)

        sends = []
        for k in range(1, N_DEV):
            cp = pltpu.make_async_remote_copy(v_ref, buf.at[me], send_sems.at[k - 1], recv_sems.at[k - 1],
                                              device_id=peer(k), device_id_type=MESH)
            cp.start()
            sends.append(cp)
        for k in range(1, N_DEV):
            px, py, pc = peer(k)
            pltpu.make_async_remote_copy(v_ref, buf.at[4 * px + 2 * py + pc], send_sems.at[k - 1], recv_sems.at[k - 1],
                                         device_id=(px, py, pc), device_id_type=MESH).wait_recv()
        for cp in sends:
            cp.wait_send()
        acc = buf[0]
        for d in range(1, N_DEV):
            acc = acc + buf[d]
        o_ref[...] = acc

    return pl.pallas_call(
        body, name=name, out_shape=jax.ShapeDtypeStruct((rows, LANES), F32),
        in_specs=[VMEM_SPEC], out_specs=VMEM_SPEC,
        scratch_shapes=[pltpu.VMEM((N_DEV, rows, LANES), F32), pltpu.SemaphoreType.DMA((N_DEV - 1,)),
                        pltpu.SemaphoreType.DMA((N_DEV - 1,))],
        compiler_params=pltpu.CompilerParams(vmem_limit_bytes=VMEM_LIMIT),
    )(vec)


def _rs_pair_add(name, place, grads, partner, split="rows"):
    _, half_rows, width = partner.shape
    tr = _row_block(half_rows, width, 2, 16)
    nb = half_rows // tr
    if split == "rows":
        mine = pl.BlockSpec((None, tr, width), lambda s, i, pr: (s, pr[1] * nb + i, 0))
    else:
        mine = pl.BlockSpec((None, tr, width), lambda s, i, pr: (s, i, pr[1]))

    def body(place_ref, g_ref, p_ref, o_ref):
        o_ref[...] = (g_ref[...].astype(F32) + p_ref[...].astype(F32)).astype(BF16)

    return pl.pallas_call(
        body, name=name, out_shape=jax.ShapeDtypeStruct(partner.shape, BF16),
        grid_spec=pltpu.PrefetchScalarGridSpec(
            num_scalar_prefetch=1, grid=(N_CHIPS, nb),
            in_specs=[mine, pl.BlockSpec((None, tr, width), lambda s, i, pr: (s, i, 0))],
            out_specs=pl.BlockSpec((None, tr, width), lambda s, i, pr: (s, i, 0))),
        compiler_params=_cparams(("parallel", "parallel")),
    )(place, grads, partner)


def _rs_chip_add(name, place, mine, others, split="rows"):
    _, half_rows, width = mine.shape
    tr = _row_block(half_rows, width, 4, 16, budget=1 << 20)
    nb = half_rows // tr
    if split == "rows":
        out_shape, out_spec = (2 * half_rows, width), pl.BlockSpec((tr, width), lambda i, pr: (pr[1] * nb + i, 0))
    else:
        out_shape, out_spec = (half_rows, 2 * width), pl.BlockSpec((tr, width), lambda i, pr: (i, pr[1]))

    def body(place_ref, q_ref, r_ref, o_ref):
        acc = q_ref[...].astype(F32)
        for j in range(3):
            acc = acc + r_ref[j].astype(F32)
        o_ref[...] = acc

    return pl.pallas_call(
        body, name=name, out_shape=jax.ShapeDtypeStruct(out_shape, F32),
        grid_spec=pltpu.PrefetchScalarGridSpec(
            num_scalar_prefetch=1, grid=(nb,),
            in_specs=[pl.BlockSpec((None, tr, width), lambda i, pr: (pr[0], i, 0)),
                      pl.BlockSpec((3, tr, width), lambda i, pr: (0, i, 0))],
            out_specs=out_spec),
        compiler_params=_cparams(("parallel",)),
    )(place, mine, others)


WEIGHTS = ["meta_tokens", "a_norm_pre", "a_w_in", "a_conv_w", "a_conv_b", "a_dt_bias", "a_a_log", "a_d_skip",
           "a_gate_norm", "a_w_out", "a_norm_post", "kv_norm", "w_kv", "b_norm_pre", "b_w_q", "b_sinks", "b_w_o",
           "b_norm_post", "f_norm_pre", "f_w_up", "f_conv_w", "f_conv_b", "f_w_down", "f_norm_post"]
FULL_SHAPE = {
    "meta_tokens": (16, 1024), "a_norm_pre": (1, 1024), "a_w_in": (1, 1024, 5152), "a_conv_w": (1, 4, 3072),
    "a_conv_b": (1, 3072), "a_dt_bias": (1, 32), "a_a_log": (1, 32), "a_d_skip": (1, 32), "a_gate_norm": (1, 2048),
    "a_w_out": (1, 2048, 1024), "a_norm_post": (1, 1024), "kv_norm": (1024,), "w_kv": (1024, 512),
    "b_norm_pre": (1, 1024), "b_w_q": (1, 1024, 1024), "b_sinks": (1, 16), "b_w_o": (1, 1024, 1024),
    "b_norm_post": (1, 1024), "f_norm_pre": (2, 1024), "f_w_up": (2, 1024, 5632), "f_conv_w": (2, 3, 5632),
    "f_conv_b": (2, 5632), "f_w_down": (2, 2816, 1024), "f_norm_post": (2, 1024),
}
SHARD_AXIS = {
    "meta_tokens": 1, "a_norm_pre": 1, "a_w_in": 2, "a_conv_w": 2, "a_conv_b": 1, "a_dt_bias": None, "a_a_log": None,
    "a_d_skip": None, "a_gate_norm": 1, "a_w_out": 1, "a_norm_post": 1, "kv_norm": None, "w_kv": 0, "b_norm_pre": None,
    "b_w_q": 1, "b_sinks": None, "b_w_o": 1, "b_norm_post": None, "f_norm_pre": None, "f_w_up": 2, "f_conv_w": 2,
    "f_conv_b": None, "f_w_down": 1, "f_norm_post": None,
}
BIG = ["a_w_in", "a_w_out", "w_kv", "b_w_q", "b_w_o", "f_w_up", "f_w_down"]
SMALL = [n for n in WEIGHTS if n not in BIG]
SMALL_SHARDED = [n for n in SMALL if SHARD_AXIS[n] is not None]


def _shard_shape(name):
    shape = list(FULL_SHAPE[name])
    if SHARD_AXIS[name] is not None:
        shape[SHARD_AXIS[name]] //= N_CHIPS
    return tuple(shape)


def _numel(shape):
    return int(math.prod(shape))


SUBLANES = 8


def _packed_rows(shape):
    rows = -(-_numel(shape) // LANES)
    return -(-rows // SUBLANES) * SUBLANES


def _pack(arrays):
    parts = []
    for a in arrays:
        size, rows = _numel(a.shape), _packed_rows(a.shape)
        if size % LANES == 0:
            part = jnp.pad(a.reshape(size // LANES, LANES), ((0, rows - size // LANES), (0, 0)))
        else:
            part = jnp.pad(a.reshape(-1), (0, rows * LANES - size)).reshape(rows, LANES)
        parts.append(part)
    return jnp.concatenate(parts, axis=0)


def _unpack(packed, names, shape_of):
    out, off = {}, 0
    lead = packed.shape[:-2]
    for n in names:
        shape = tuple(shape_of(n))
        size, rows = _numel(shape), _packed_rows(shape)
        part = packed[..., off:off + rows, :]
        if size % LANES == 0:
            out[n] = part[..., :size // LANES, :].reshape(lead + shape)
        else:
            out[n] = part.reshape(lead + (rows * LANES,))[..., :size].reshape(lead + shape)
        off += rows
    return out


def _split_chips(name, full):
    ax = SHARD_AXIS[name]
    shape = full.shape
    cut = shape[:ax] + (N_CHIPS, shape[ax] // N_CHIPS) + shape[ax + 1:]
    return jnp.moveaxis(full.reshape(cut), ax, 0)


def _join_chips(name, stacked):
    ax = SHARD_AXIS[name]
    moved = jnp.moveaxis(stacked, 0, ax)
    shape = moved.shape
    return moved.reshape(shape[:ax] + (shape[ax] * shape[ax + 1],) + shape[ax + 2:])


def _as2d(a):
    return a.reshape(-1, a.shape[-1])


BUFFERS = [("a_w_in", "a_w_in", None), ("a_w_out", "a_w_out", None), ("w_kv", "w_kv", None),
           ("b_w_q", "b_w_q", None), ("b_w_o", "b_w_o", None), ("f_w_up0", "f_w_up", 0), ("f_w_up1", "f_w_up", 1),
           ("f_w_down0", "f_w_down", 0), ("f_w_down1", "f_w_down", 1)]


TRANSPOSED = ("a_w_in",)
SPLIT = {"a_w_in": "cols"}


def _local_shard(arrays, weight, layer):
    if weight in TRANSPOSED:
        return arrays[weight][0].T
    return _as2d(arrays[weight]) if layer is None else arrays[weight]


def _weight_from_gathered(weight, buf):
    if weight == "f_w_up":
        return buf
    return buf.reshape(N_CHIPS * buf.shape[1], buf.shape[2])


def _gathered_from_grad(weight, g):
    if weight == "f_w_up":
        return g
    return g.reshape(N_CHIPS, g.shape[0] // N_CHIPS, g.shape[1]).astype(BF16)


GATHER_SCHEDULE = {
    "a_in_main": [("ici", ["a_w_out"])],
    "a_conv": [("d2d", ["a_w_out"]), ("ici", ["f_w_down0"])],
    "a_ssd_prep": [("d2d", ["f_w_down0"]), ("ici_near", ["f_w_up0"])],
    "a_ssd": [("ici_far", ["f_w_up0"])],
    "a_gate": [("d2d", ["f_w_up0"]), ("ici", ["w_kv", "b_w_q", "b_w_o"])],
    "ffn0_up": [("d2d", ["w_kv", "b_w_q", "b_w_o"]), ("ici", ["f_w_down1"])],
    "ffn0_conv": [("d2d", ["f_w_down1"]), ("ici_near", ["f_w_up1"])],
    "b_attn": [("ici_far", ["f_w_up1"])],
    "b_o": [("d2d", ["f_w_up1"])],
}
REDUCE_SCHEDULE = {
    "b_attn_bwd": [("all", ["f_w_down1", "f_w_up1", "b_w_o"])],
    "ffn0_conv_bwd": [("all", ["b_w_q", "w_kv", "f_w_down0"])],
    "a_ssd_bwd": [("all", ["f_w_up0", "a_w_out"])],
    "a_in_main_dx": [("near", ["a_w_in"])],
    "a_norm_bwd": [("far", ["a_w_in"])],
}
REDUCE_LAST = ("a_w_in",)
ICI_PEERS = {"ici": ALL_PEERS, "ici_near": NEAR_PEERS, "ici_far": FAR_PEERS,
             "all": ALL_PEERS, "near": NEAR_PEERS, "far": FAR_PEERS}
PAIR_SCHEDULE = {
    "b_o_dx": ["f_w_down1", "f_w_up1", "b_w_o"],
    "ffn0_down_dx": ["b_w_q", "w_kv", "f_w_down0"],
    "a_out_dx": ["f_w_up0", "a_w_out"],
    "a_in_dt_dx": ["a_w_in"],
}
SWAP_SCHEDULE = {"a_in_main_dw": ["f_w_down1", "f_w_up1", "b_w_o", "b_w_q", "w_kv", "f_w_down0", "f_w_up0", "a_w_out"]}


def _buffer_of(weight, layer):
    return weight if layer is None else f"{weight}{layer}"


class _Pipeline:
    def __init__(self, place, slots):
        self.place = place
        self.slots = dict(slots)
        self.running = []
        self.grads = {}
        self.theirs = {}
        self.partials = {}
        self.peers = {}
        self.reduced = {}

    def _collect(self):
        for step, buffers, table in self.running:
            table.update(zip(buffers, step.results))
        self.running = []

    @staticmethod
    def _splits(buffers):
        return [SPLIT.get(b, "rows") for b in buffers]

    def gather_now(self, name, buffers):
        step = _step_gather_full([self.slots[b] for b in buffers], self._splits(buffers))
        _run_steps(name, [step])
        self.slots.update(zip(buffers, step.results))

    def weight(self, name, layer=None):
        self._collect()
        return _weight_from_gathered(name, self.slots[_buffer_of(name, layer)])

    def grad(self, name, layer, g):
        self.grads[_buffer_of(name, layer)] = _gathered_from_grad(name, g)

    def steps(self, kernel):
        self._collect()
        steps = []
        for phase, buffers in GATHER_SCHEDULE.get(kernel, []):
            bufs, splits = [self.slots[b] for b in buffers], self._splits(buffers)
            step = (_step_gather_d2d(bufs, splits) if phase == "d2d"
                    else _step_gather_ici(bufs, splits, ICI_PEERS[phase]))
            self.running.append((step, buffers, self.slots))
            steps.append(step)
        buffers = PAIR_SCHEDULE.get(kernel)
        if buffers:
            step = _step_pair_exchange([self.grads[b] for b in buffers], self._splits(buffers))
            self.running.append((step, buffers, self.theirs))
            steps.append(step)
        for part, buffers in REDUCE_SCHEDULE.get(kernel, []):
            for b in buffers:
                if b not in self.partials:
                    self.partials[b] = _rs_pair_add("reduce_pair_add_" + b, self.place, self.grads[b], self.theirs[b],
                                                    SPLIT.get(b, "rows"))
            started = [self.peers[b] for b in buffers] if all(b in self.peers for b in buffers) else None
            step = _step_chip_exchange([self.partials[b] for b in buffers], ICI_PEERS[part], into=started)
            self.running.append((step, buffers, self.peers))
            steps.append(step)
        buffers = SWAP_SCHEDULE.get(kernel)
        if buffers:
            step = self._swap_step(buffers)
            self.running.append((step, buffers, self.reduced))
            steps.append(step)
        return steps

    def _swap_step(self, buffers):
        halves = [_rs_chip_add("reduce_chip_add_" + b, self.place, self.partials[b], self.peers[b], SPLIT.get(b, "rows"))
                  for b in buffers]
        return _step_pair_gather(halves, self._splits(buffers))

    def shard(self, buffer):
        self._collect()
        return self.reduced[buffer]

    def finish(self):
        self._collect()
        rest = [b for b, _, _ in BUFFERS if b not in self.reduced]
        step = self._swap_step(rest)
        _run_steps("reduce_pair_gather", [step])
        self.reduced.update(zip(rest, step.results))


def kernel(x, meta_tokens, a_norm_pre, a_w_in, a_conv_w, a_conv_b, a_dt_bias, a_a_log, a_d_skip, a_gate_norm, a_w_out, a_norm_post, kv_norm, w_kv, b_norm_pre, b_w_q, b_sinks, b_w_o, b_norm_post, f_norm_pre, f_w_up, f_conv_w, f_conv_b, f_w_down, f_norm_post, loss_target, m_meta_tokens, m_a_norm_pre, m_a_w_in, m_a_conv_w, m_a_conv_b, m_a_dt_bias, m_a_a_log, m_a_d_skip, m_a_gate_norm, m_a_w_out, m_a_norm_post, m_kv_norm, m_w_kv, m_b_norm_pre, m_b_w_q, m_b_sinks, m_b_w_o, m_b_norm_post, m_f_norm_pre, m_f_w_up, m_f_conv_w, m_f_conv_b, m_f_w_down, m_f_norm_post, v_meta_tokens, v_a_norm_pre, v_a_w_in, v_a_conv_w, v_a_conv_b, v_a_dt_bias, v_a_a_log, v_a_d_skip, v_a_gate_norm, v_a_w_out, v_a_norm_post, v_kv_norm, v_w_kv, v_b_norm_pre, v_b_w_q, v_b_sinks, v_b_w_o, v_b_norm_post, v_f_norm_pre, v_f_w_up, v_f_conv_w, v_f_conv_b, v_f_w_down, v_f_norm_post):
    given = dict(locals())
    w = {n: given[n] for n in WEIGHTS}
    mom = {n: given["m_" + n] for n in WEIGHTS}
    var = {n: given["v_" + n] for n in WEIGHTS}
    chip = 2 * lax.axis_index("x") + lax.axis_index("y")
    core = lax.axis_index("c")
    place = jnp.stack([chip, core]).astype(jnp.int32)

    small_all = _allgather_small("gather_small", _pack([w[n] for n in SMALL_SHARDED]))
    small_parts = _unpack(small_all, SMALL_SHARDED, _shard_shape)
    slots = {b: _cast_into_slot("cast_" + b, place, _local_shard(w, wn, layer), layer) for b, wn, layer in BUFFERS}
    pipeline = _Pipeline(place, slots)
    pipeline.gather_now("gather_first", ["a_w_in"])
    p = {}
    for n in SMALL:
        p[n] = _join_chips(n, small_parts[n]) if n in SMALL_SHARDED else w[n]
    p["a_conv_w"] = p["a_conv_w"][0]
    p["kv_norm"] = p["kv_norm"].reshape(1, D_MODEL)

    loss_local, grad_x, g = _local_step(x[0], loss_target[0], p, pipeline)

    small_sum = _allreduce_small("reduce_small", _pack([g[n].reshape(FULL_SHAPE[n]) for n in SMALL]
                                                       + [loss_local.reshape(1, 1)]))
    small_red = _unpack(small_sum, SMALL + ["loss"], lambda n: (1, 1) if n == "loss" else FULL_SHAPE[n])
    loss = small_red["loss"][0, 0]
    grads = {}
    for n in SMALL:
        if SHARD_AXIS[n] is None:
            grads[n] = small_red[n]
        else:
            grads[n] = lax.dynamic_index_in_dim(_split_chips(n, small_red[n]), chip, 0, keepdims=False)

    delta, new_m, new_v = {}, {}, {}
    for n in sorted(BIG, key=lambda name: name in REDUCE_LAST):
        shape = _shard_shape(n)
        if n in REDUCE_LAST:
            pipeline.finish()
        if n in TRANSPOSED:
            g2d = pipeline.shard(n)
            w2d, m2d, v2d = (arrays[n][0].T for arrays in (w, mom, var))
            back = lambda a: a.T.reshape(shape)
        else:
            g2d = (jnp.concatenate([pipeline.shard(n + "0"), pipeline.shard(n + "1")], axis=0)
                   if n in ("f_w_up", "f_w_down") else pipeline.shard(n))
            w2d, m2d, v2d = (_as2d(arrays[n]) for arrays in (w, mom, var))
            back = lambda a: a.reshape(shape)
        d, m2, v2 = _adamw("adamw_" + n, w2d, g2d, m2d, v2d, steps=pipeline.steps("adamw_" + n))
        grads[n], delta[n], new_m[n], new_v[n] = back(g2d), back(d), back(m2), back(v2)
    packed = [_pack([src[n].reshape(_shard_shape(n)) for n in SMALL]) for src in (w, grads, mom, var)]
    outs = _adamw("adamw_small", *packed)
    for dst, flat in zip((delta, new_m, new_v), outs):
        dst.update(_unpack(flat, SMALL, _shard_shape))

    return (loss, grad_x[None], *[grads[n].reshape(_shard_shape(n)) for n in WEIGHTS],
            *[delta[n] for n in WEIGHTS], *[new_m[n] for n in WEIGHTS], *[new_v[n] for n in WEIGHTS])
```

```python
import functools
import math

import jax
import jax.numpy as jnp
from jax import lax
from jax.experimental import pallas as pl
from jax.experimental.pallas import tpu as pltpu

F32, BF16 = jnp.float32, jnp.bfloat16
MESH = pl.DeviceIdType.MESH

D_MODEL = 1024
N_META = 16
CHUNK = 128
PAD_ROWS = CHUNK - N_META
D_INNER = 2048
D_STATE = 128
N_GROUPS = 4
HEADS_PER_GROUP = 8
SSM_HEADS = 32
HEAD_DIM = 64
D_BC = N_GROUPS * D_STATE
D_XBC = D_INNER + 2 * D_BC
D_MAIN = D_INNER + D_XBC
D_IN_PROJ = D_MAIN + SSM_HEADS
GROUP_W = HEADS_PER_GROUP * HEAD_DIM
SSM_CONV = 4
D_FF = 2816
FFN_CONV = 3
N_Q_HEADS = 16
N_KV_HEADS = 4
D_KV = 256
ATTN_SCALE = 1.0 / math.sqrt(HEAD_DIM)
RMS_EPS = 1e-6
NEG_INF = -1e30
LANES = 128
VMEM_LIMIT = 48 * 1024 * 1024

ADAM_LR, ADAM_B1, ADAM_B2, ADAM_EPS, ADAM_WD, ADAM_STEP = 0.001, 0.9, 0.999, 1e-08, 0.01, 10

N_CHIPS = 4
N_DEV = 8


def _cparams(sem=None):
    return pltpu.CompilerParams(dimension_semantics=sem, vmem_limit_bytes=VMEM_LIMIT)


def _tile(n, cands=(512, 256, 128)):
    for t in cands:
        if n % t == 0:
            return t
    return n


def _row_tile(rows, width):
    for t in (544, 272):
        if rows % t == 0 and t * width * 4 <= (3 << 20):
            return t
    return 128


def _rows_mask(i, tm):
    rows = i * tm + lax.broadcasted_iota(jnp.int32, (tm, 1), 0)
    return rows >= PAD_ROWS


def _dot(a, b):
    return jnp.dot(a, b, preferred_element_type=F32)


def _dot_nt(a, b):
    return lax.dot_general(a, b, (((1,), (1,)), ((), ())), preferred_element_type=F32)


def _dot_tn(a, b):
    return lax.dot_general(a, b, (((0,), (0,)), ((), ())), preferred_element_type=F32)


def _sigmoid(x):
    return 1.0 / (1.0 + jnp.exp(-x))


def _place():
    return lax.axis_index("x"), lax.axis_index("y"), lax.axis_index("c")


def _other_chips(x, y):
    return [(1 - x, y), (x, 1 - y), (1 - x, 1 - y)]


class _Step:
    def __init__(self, ins, outs, aliases, n_sems, start, finish):
        self.ins, self.outs, self.aliases, self.n_sems = list(ins), list(outs), dict(aliases), n_sems
        self.start, self.finish = start, finish
        self.results = None


def _like(a):
    return jax.ShapeDtypeStruct(a.shape, a.dtype)


def _remote(src, dst, send_sems, recv_sems, k, device):
    return pltpu.make_async_remote_copy(src, dst, send_sems.at[k], recv_sems.at[k], device_id=device, device_id_type=MESH)


def _half(ref, split, which, lead=()):
    if split == "rows":
        hr = ref.shape[-2] // 2
        return ref.at[lead + (pl.ds(which * hr, hr),)]
    hc = ref.shape[-1] // 2
    return ref.at[lead + (slice(None), pl.ds(which * hc, hc))]


def _splits(bufs, splits):
    return list(splits) if splits is not None else ["rows"] * len(bufs)


ALL_PEERS = (0, 1, 2)
NEAR_PEERS = (0, 1)
FAR_PEERS = (2,)


def _step_gather_ici(bufs, splits=None, peers=ALL_PEERS):
    splits = _splits(bufs, splits)

    def copies(outs, send_sems, recv_sems, received):
        x, y, c = _place()
        me = 2 * x + y
        for k, o in enumerate(outs):
            for j, (cx, cy) in enumerate(_other_chips(x, y)):
                if j in peers:
                    part = _half(o, splits[k], c, (2 * cx + cy if received else me,))
                    yield _remote(part, part, send_sems, recv_sems, 3 * k + j, (cx, cy, c))

    def start(ins, outs, send_sems, recv_sems):
        for cp in copies(outs, send_sems, recv_sems, False):
            cp.start()

    def finish(ins, outs, send_sems, recv_sems):
        for cp in copies(outs, send_sems, recv_sems, True):
            cp.wait_recv()
        for cp in copies(outs, send_sems, recv_sems, False):
            cp.wait_send()

    return _Step(bufs, [_like(b) for b in bufs], {k: k for k in range(len(bufs))}, 3 * len(bufs), start, finish)


def _step_gather_d2d(bufs, splits=None):
    splits = _splits(bufs, splits)

    def copies(outs, send_sems, recv_sems, received):
        x, y, c = _place()
        for k, o in enumerate(outs):
            for j, (cx, cy) in enumerate(_other_chips(x, y)):
                part = _half(o, splits[k], 1 - c if received else c, (2 * cx + cy,))
                yield _remote(part, part, send_sems, recv_sems, 3 * k + j, (x, y, 1 - c))

    def start(ins, outs, send_sems, recv_sems):
        for cp in copies(outs, send_sems, recv_sems, False):
            cp.start()

    def finish(ins, outs, send_sems, recv_sems):
        for cp in copies(outs, send_sems, recv_sems, True):
            cp.wait_recv()
        for cp in copies(outs, send_sems, recv_sems, False):
            cp.wait_send()

    return _Step(bufs, [_like(b) for b in bufs], {k: k for k in range(len(bufs))}, 3 * len(bufs), start, finish)


def _step_gather_full(bufs, splits=None):
    n = len(bufs)
    splits = _splits(bufs, splits)

    def ici(outs, send_sems, recv_sems, received):
        x, y, c = _place()
        me = 2 * x + y
        for k, o in enumerate(outs):
            for j, (cx, cy) in enumerate(_other_chips(x, y)):
                part = _half(o, splits[k], c, (2 * cx + cy if received else me,))
                yield _remote(part, part, send_sems, recv_sems, 3 * k + j, (cx, cy, c))

    def d2d(outs, send_sems, recv_sems, received):
        x, y, c = _place()
        for k, o in enumerate(outs):
            for j, (cx, cy) in enumerate(_other_chips(x, y)):
                part = _half(o, splits[k], 1 - c if received else c, (2 * cx + cy,))
                yield _remote(part, part, send_sems, recv_sems, 3 * n + 3 * k + j, (x, y, 1 - c))

    def start(ins, outs, send_sems, recv_sems):
        for cp in ici(outs, send_sems, recv_sems, False):
            cp.start()

    def finish(ins, outs, send_sems, recv_sems):
        for arrived, onward in zip(ici(outs, send_sems, recv_sems, True), d2d(outs, send_sems, recv_sems, False)):
            arrived.wait_recv()
            onward.start()
        for cp in d2d(outs, send_sems, recv_sems, True):
            cp.wait_recv()
        for cp in ici(outs, send_sems, recv_sems, False):
            cp.wait_send()
        for cp in d2d(outs, send_sems, recv_sems, False):
            cp.wait_send()

    return _Step(bufs, [_like(b) for b in bufs], {k: k for k in range(n)}, 6 * n, start, finish)


def _half_shape(shape, split):
    return shape[:-2] + ((shape[-2] // 2, shape[-1]) if split == "rows" else (shape[-2], shape[-1] // 2))


def _step_pair_exchange(grads, splits=None):
    splits = _splits(grads, splits)

    def copies(ins, outs, send_sems, recv_sems):
        x, y, c = _place()
        for k, (g, o) in enumerate(zip(ins, outs)):
            yield _remote(_half(g, splits[k], 1 - c, (slice(None),)), o, send_sems, recv_sems, k, (x, y, 1 - c))

    def start(ins, outs, send_sems, recv_sems):
        for cp in copies(ins, outs, send_sems, recv_sems):
            cp.start()

    def finish(ins, outs, send_sems, recv_sems):
        for cp in copies(ins, outs, send_sems, recv_sems):
            cp.wait()

    outs = [jax.ShapeDtypeStruct(_half_shape(g.shape, s), g.dtype) for g, s in zip(grads, splits)]
    return _Step(grads, outs, {}, len(grads), start, finish)


def _step_chip_exchange(partials, peers=ALL_PEERS, into=None):
    n = len(partials)

    def copies(ins, outs, send_sems, recv_sems):
        x, y, c = _place()
        for k, (q, o) in enumerate(zip(ins[:n], outs)):
            for j, (cx, cy) in enumerate(_other_chips(x, y)):
                if j in peers:
                    yield _remote(q.at[2 * cx + cy], o.at[j], send_sems, recv_sems, 3 * k + j, (cx, cy, c))

    def start(ins, outs, send_sems, recv_sems):
        for cp in copies(ins, outs, send_sems, recv_sems):
            cp.start()

    def finish(ins, outs, send_sems, recv_sems):
        for cp in copies(ins, outs, send_sems, recv_sems):
            cp.wait()

    outs = [jax.ShapeDtypeStruct((3,) + q.shape[1:], q.dtype) for q in partials]
    if into is None:
        return _Step(partials, outs, {}, 3 * n, start, finish)
    return _Step(list(partials) + list(into), outs, {n + k: k for k in range(n)}, 3 * n, start, finish)


def _step_pair_gather(shards, splits=None):
    splits = _splits(shards, splits)

    def copies(outs, send_sems, recv_sems, received):
        x, y, c = _place()
        for k, o in enumerate(outs):
            part = _half(o, splits[k], 1 - c if received else c)
            yield _remote(part, part, send_sems, recv_sems, k, (x, y, 1 - c))

    def start(ins, outs, send_sems, recv_sems):
        for cp in copies(outs, send_sems, recv_sems, False):
            cp.start()

    def finish(ins, outs, send_sems, recv_sems):
        for cp in copies(outs, send_sems, recv_sems, True):
            cp.wait_recv()
        for cp in copies(outs, send_sems, recv_sems, False):
            cp.wait_send()

    return _Step(shards, [_like(s) for s in shards], {k: k for k in range(len(shards))}, len(shards), start, finish)


def _call(body, *, name, out_shape, grid, in_specs, out_specs, operands, scratch_shapes=(), semantics=None, steps=()):
    single = not isinstance(out_shape, (tuple, list))
    out_shapes = [out_shape] if single else list(out_shape)
    out_spec_list = [out_specs] if single else list(out_specs)
    steps = list(steps)
    if not steps:
        res = pl.pallas_call(body, name=name, out_shape=out_shapes, grid=grid, in_specs=list(in_specs),
                             out_specs=out_spec_list, scratch_shapes=list(scratch_shapes),
                             compiler_params=_cparams(semantics))(*operands)
        return res[0] if single else res
    n_in, n_out, n_scr = len(operands), len(out_shapes), len(scratch_shapes)
    x_in = [a for s in steps for a in s.ins]
    x_out = [o for s in steps for o in s.outs]
    aliases, in_off, out_off = {}, 0, 0
    for s in steps:
        for i, o in s.aliases.items():
            aliases[n_in + in_off + i] = n_out + out_off + o
        in_off += len(s.ins)
        out_off += len(s.outs)
    sems = []
    for s in steps:
        sems += [pltpu.SemaphoreType.DMA((s.n_sems,)), pltpu.SemaphoreType.DMA((s.n_sems,))]
    any_spec = pl.BlockSpec(memory_space=pl.ANY)

    def carried(*refs):
        pos = 0
        ins = refs[pos:pos + n_in]; pos += n_in
        xi = refs[pos:pos + len(x_in)]; pos += len(x_in)
        outs = refs[pos:pos + n_out]; pos += n_out
        xo = refs[pos:pos + len(x_out)]; pos += len(x_out)
        scr = refs[pos:pos + n_scr]; pos += n_scr
        sem_refs = refs[pos:]

        def each(action):
            i0 = o0 = 0
            for k, s in enumerate(steps):
                getattr(s, action)(xi[i0:i0 + len(s.ins)], xo[o0:o0 + len(s.outs)], sem_refs[2 * k], sem_refs[2 * k + 1])
                i0 += len(s.ins)
                o0 += len(s.outs)

        if grid:
            first = functools.reduce(jnp.logical_and, [pl.program_id(d) == 0 for d in range(len(grid))])
            last = functools.reduce(jnp.logical_and, [pl.program_id(d) == grid[d] - 1 for d in range(len(grid))])
            pl.when(first)(lambda: each("start"))
            body(*ins, *outs, *scr)
            pl.when(last)(lambda: each("finish"))
        else:
            each("start")
            body(*ins, *outs, *scr)
            each("finish")

    res = pl.pallas_call(
        carried, name=name, out_shape=out_shapes + x_out, grid=grid,
        in_specs=list(in_specs) + [any_spec] * len(x_in), out_specs=out_spec_list + [any_spec] * len(x_out),
        scratch_shapes=list(scratch_shapes) + sems, input_output_aliases=aliases,
        compiler_params=_cparams(None if semantics is None else ("arbitrary",) * len(grid)),
    )(*operands, *x_in)
    o0 = n_out
    for s in steps:
        s.results = list(res[o0:o0 + len(s.outs)])
        o0 += len(s.outs)
    return res[0] if single else tuple(res[:n_out])


def _run_steps(name, steps):
    _call(lambda: None, name=name, out_shape=[], grid=(), in_specs=[], out_specs=[], operands=[], steps=steps)
    return [s.results for s in steps]


def _mm(name, a, b, mode, out_dtype=F32, acc=None, b_colblock=0, k_rows=None, out_rows=None, steps=()):
    resident_bytes = 8 << 20
    if mode == "nn":
        m, k = a.shape
        n = b.shape[1]
        tm = m
        while tm * k * 2 > resident_bytes and tm % 32 == 0:
            tm //= 2
        tn = _tile(n)
        grid = (m // tm, n // tn)
        in_specs = [pl.BlockSpec((tm, k), lambda i, j: (i, 0)), pl.BlockSpec((k, tn), lambda i, j: (0, j))]
        out_shape, out_block = (m, n), (tm, tn)
    elif mode == "nt":
        m, n = a.shape
        k = k_rows or b.shape[0]
        tm = m
        while tm * n * 2 > resident_bytes and tm % 32 == 0:
            tm //= 2
        tk = _tile(k)
        grid = (m // tm, k // tk)
        in_specs = [pl.BlockSpec((tm, n), lambda i, j: (i, 0)), pl.BlockSpec((tk, n), lambda i, j: (j, b_colblock))]
        out_shape, out_block = (m, k), (tm, tk)
    else:
        m, k = a.shape
        n = b.shape[1]
        tk, tn = _tile(k), (n if m * n * 2 <= resident_bytes else _tile(n))
        grid = (k // tk, n // tn)
        in_specs = [pl.BlockSpec((m, tk), lambda i, j: (0, i)), pl.BlockSpec((m, tn), lambda i, j: (0, j))]
        out_shape, out_block = (out_rows or k, n), (tk, tn)
    out_spec = pl.BlockSpec(out_block, lambda i, j: (i, j))
    has_acc = acc is not None

    def body(*refs):
        a_ref, b_ref = refs[0], refs[1]
        o_ref = refs[-1]
        av, bv = a_ref[...], b_ref[...]
        if mode == "nn":
            r = _dot(av, bv)
        elif mode == "nt":
            r = _dot_nt(av, bv)
        else:
            r = _dot_tn(av, bv)
        if has_acc:
            r = r + refs[2][...]
        o_ref[...] = r.astype(o_ref.dtype)

    operands = [a, b]
    if has_acc:
        in_specs = in_specs + [out_spec]
        operands.append(acc)
    return _call(body, name=name, out_shape=jax.ShapeDtypeStruct(out_shape, out_dtype), grid=grid, in_specs=in_specs,
                 out_specs=out_spec, operands=operands, semantics=("parallel", "parallel"), steps=steps)


def _tn_rows_into(name, a, b, into, row0, nrows):
    m, k = a.shape
    n = b.shape[1]

    def body(a_ref, b_ref, into_ref, o_ref):
        o_ref[...] = _dot_tn(a_ref[...], b_ref[...])[0:nrows].astype(o_ref.dtype)

    return pl.pallas_call(
        body, name=name, out_shape=jax.ShapeDtypeStruct(into.shape, into.dtype), grid=(1,),
        in_specs=[pl.BlockSpec((m, k), lambda i: (0, 0)), pl.BlockSpec((m, n), lambda i: (0, 0)),
                  pl.BlockSpec(memory_space=pl.ANY)],
        out_specs=pl.BlockSpec((nrows, n), lambda i: (row0 // nrows, 0)),
        input_output_aliases={2: 0}, compiler_params=_cparams(("arbitrary",)),
    )(a, b, into)


def _fit_rows(m, row_bytes, budget=8 << 20):
    tm = m
    while tm * row_bytes > budget and tm % 32 == 0:
        tm //= 2
    return tm


def _mm_nn_bychip(name, a, bc, steps=()):
    m, k = a.shape
    n = bc.shape[2]
    tm = min(_fit_rows(m, k * 2), _fit_rows(m, n * 4, budget=13 << 20))

    def body(a_ref, b_ref, o_ref):
        o_ref[...] = _dot(a_ref[...], b_ref[...])

    return _call(
        body, name=name, out_shape=jax.ShapeDtypeStruct((m, N_CHIPS * n), F32), grid=(m // tm, N_CHIPS),
        in_specs=[pl.BlockSpec((tm, k), lambda i, c: (i, 0)), pl.BlockSpec((None, k, n), lambda i, c: (c, 0, 0))],
        out_specs=pl.BlockSpec((tm, n), lambda i, c: (i, c)), operands=[a, bc],
        semantics=("parallel", "parallel"), steps=steps)


def _mm_nt_bychip(name, a, bc, chip0, acc=None):
    m = a.shape[0]
    _, k, n = bc.shape
    nch = a.shape[1] // n
    tm, tk = _fit_rows(m, n * 2), _tile(k)
    has_acc = acc is not None

    def body(*refs):
        a_ref, b_ref, o_ref = refs[0], refs[1], refs[-1]

        @pl.when(pl.program_id(2) == 0)
        def _():
            o_ref[...] = refs[2][...] if has_acc else jnp.zeros_like(o_ref)

        o_ref[...] += _dot_nt(a_ref[...], b_ref[...])

    out_spec = pl.BlockSpec((tm, tk), lambda i, j, c: (i, j))
    in_specs = [pl.BlockSpec((tm, n), lambda i, j, c: (i, c)),
                pl.BlockSpec((None, tk, n), lambda i, j, c: (chip0 + c, j, 0))]
    operands = [a, bc]
    if has_acc:
        in_specs.append(out_spec)
        operands.append(acc)
    return pl.pallas_call(
        body, name=name, out_shape=jax.ShapeDtypeStruct((m, k), F32), grid=(m // tm, k // tk, nch),
        in_specs=in_specs, out_specs=out_spec, compiler_params=_cparams(("parallel", "parallel", "arbitrary")),
    )(*operands)


def _mm_tn_bychip(name, a, dy, n, chip0, into=None):
    m, k = a.shape
    nch = dy.shape[1] // n
    tk = _tile(k)

    def body(*refs):
        a_ref, d_ref, o_ref = refs[0], refs[1], refs[-1]
        o_ref[...] = _dot_tn(a_ref[...], d_ref[...]).astype(BF16)

    in_specs = [pl.BlockSpec((m, tk), lambda i, c: (0, i)), pl.BlockSpec((m, n), lambda i, c: (0, c))]
    operands = [a, dy]
    aliases = {}
    if into is not None:
        in_specs.append(pl.BlockSpec(memory_space=pl.ANY))
        operands.append(into)
        aliases = {2: 0}
    return pl.pallas_call(
        body, name=name, out_shape=jax.ShapeDtypeStruct((N_CHIPS, k, n), BF16), grid=(k // tk, nch),
        in_specs=in_specs, out_specs=pl.BlockSpec((None, tk, n), lambda i, c: (chip0 + c, i, 0)),
        input_output_aliases=aliases, compiler_params=_cparams(("parallel", "parallel")),
    )(*operands)


def _rms_fwd(name, h, w):
    rows, width = h.shape
    tm = _row_tile(rows, width)

    def body(h_ref, w_ref, o_ref):
        x = h_ref[...]
        r = lax.rsqrt(jnp.mean(x * x, axis=-1, keepdims=True) + RMS_EPS)
        o_ref[...] = (x * r * w_ref[...]).astype(BF16)

    return pl.pallas_call(
        body, name=name, out_shape=jax.ShapeDtypeStruct((rows, width), BF16), grid=(rows // tm,),
        in_specs=[pl.BlockSpec((tm, width), lambda i: (i, 0)), pl.BlockSpec((1, width), lambda i: (0, 0))],
        out_specs=pl.BlockSpec((tm, width), lambda i: (i, 0)), compiler_params=_cparams(("parallel",)),
    )(h, w)


def _resid_norm_fwd(name, h, pre, w, next_norms=()):
    rows, width = h.shape
    tm = _row_tile(rows, width)
    n_next = len(next_norms)

    def body(*refs):
        h_ref, p_ref, w_ref = refs[:3]
        v_refs = refs[3:3 + n_next]
        o_ref = refs[3 + n_next]
        n_refs = refs[4 + n_next:]
        p = p_ref[...]
        r = lax.rsqrt(jnp.mean(p * p, axis=-1, keepdims=True) + RMS_EPS)
        x = h_ref[...] + jnp.where(_rows_mask(pl.program_id(0), tm), p * r * w_ref[...], 0.0)
        o_ref[...] = x
        if n_next:
            rx = lax.rsqrt(jnp.mean(x * x, axis=-1, keepdims=True) + RMS_EPS)
            for v_ref, n_ref in zip(v_refs, n_refs):
                n_ref[...] = (x * rx * v_ref[...]).astype(BF16)

    row_spec = pl.BlockSpec((tm, width), lambda i: (i, 0))
    vec_spec = pl.BlockSpec((1, width), lambda i: (0, 0))
    outs = pl.pallas_call(
        body, name=name,
        out_shape=[jax.ShapeDtypeStruct((rows, width), F32)] + [jax.ShapeDtypeStruct((rows, width), BF16)] * n_next,
        grid=(rows // tm,), in_specs=[row_spec, row_spec, vec_spec] + [vec_spec] * n_next,
        out_specs=[row_spec] * (1 + n_next), compiler_params=_cparams(("parallel",)),
    )(h, pre, w, *next_norms)
    return outs[0], list(outs[1:])


def _resid_norm_loss(name, h, pre, w, target):
    rows, width = h.shape

    def body(h_ref, p_ref, w_ref, t_ref, dh_ref, loss_ref, dp_ref, dw_ref):
        i = pl.program_id(0)
        p = p_ref[...]
        r = lax.rsqrt(jnp.mean(p * p, axis=-1, keepdims=True) + RMS_EPS)
        x = h_ref[...] + p * r * w_ref[...]
        real = (i + jnp.zeros((CHUNK, 1), jnp.int32)) >= 1
        diff = jnp.where(real, x - t_ref[...], 0.0)
        dh = diff * (1.0 / D_MODEL)
        dh_ref[...] = dh
        dp, dw_rows = _rms_bwd(dh, p, w_ref[...])
        dp_ref[...] = dp.astype(BF16)

        @pl.when(i == 0)
        def _():
            loss_ref[...] = jnp.zeros_like(loss_ref)
            dw_ref[...] = jnp.zeros_like(dw_ref)

        loss_ref[...] += jnp.sum(diff * diff) * (0.5 / D_MODEL)
        dw_ref[...] += jnp.sum(dw_rows, axis=0, keepdims=True)

    blk = pl.BlockSpec((CHUNK, width), lambda i: (i, 0))
    vec_spec = pl.BlockSpec((1, width), lambda i: (0, 0))
    return pl.pallas_call(
        body, name=name,
        out_shape=(jax.ShapeDtypeStruct((rows, width), F32), jax.ShapeDtypeStruct((1, LANES), F32),
                   jax.ShapeDtypeStruct((rows, width), BF16), jax.ShapeDtypeStruct((1, width), F32)),
        grid=(rows // CHUNK,),
        in_specs=[blk, blk, vec_spec, pl.BlockSpec((CHUNK, width), lambda i: (jnp.maximum(i - 1, 0), 0))],
        out_specs=(blk, pl.BlockSpec((1, LANES), lambda i: (0, 0)), blk, vec_spec),
        compiler_params=_cparams(("arbitrary",)),
    )(h, pre, w, target)


def _rms_bwd(dy, x, w):
    r = lax.rsqrt(jnp.mean(x * x, axis=-1, keepdims=True) + RMS_EPS)
    xhat = x * r
    dxhat = dy * w
    return r * (dxhat - xhat * jnp.mean(dxhat * xhat, axis=-1, keepdims=True)), dy * xhat


def _norm_bwd_add(name, dh, dhn, h, w, then=None, steps=()):
    rows, width = dh.shape
    tm = _row_tile(rows, width)
    fused = then is not None

    def body(*refs):
        dh_ref, dhn_ref, h_ref, w_ref = refs[:4]
        o_ref, dw_ref = refs[6:8] if fused else refs[4:6]
        i = pl.program_id(0)
        valid = _rows_mask(i, tm)
        dx, dw_rows = _rms_bwd(dhn_ref[...], h_ref[...], w_ref[...])
        dh_new = dh_ref[...] + jnp.where(valid, dx, 0.0)
        o_ref[...] = dh_new

        @pl.when(i == 0)
        def _():
            dw_ref[...] = jnp.zeros_like(dw_ref)

        dw_ref[...] += jnp.sum(dw_rows, axis=0, keepdims=True)
        if fused:
            p_ref, wp_ref, dp_ref, dwp_ref = refs[4], refs[5], refs[8], refs[9]
            dp, dwp_rows = _rms_bwd(jnp.where(valid, dh_new, 0.0), p_ref[...], wp_ref[...])
            dp_ref[...] = dp.astype(BF16)

            @pl.when(i == 0)
            def _():
                dwp_ref[...] = jnp.zeros_like(dwp_ref)

            dwp_ref[...] += jnp.sum(dwp_rows, axis=0, keepdims=True)

    row_spec = pl.BlockSpec((tm, width), lambda i: (i, 0))
    vec_spec = pl.BlockSpec((1, width), lambda i: (0, 0))
    row_f32, vec_f32 = jax.ShapeDtypeStruct((rows, width), F32), jax.ShapeDtypeStruct((1, width), F32)
    in_specs, operands = [row_spec, row_spec, row_spec, vec_spec], [dh, dhn, h, w]
    out_shape, out_specs = [row_f32, vec_f32], [row_spec, vec_spec]
    if fused:
        in_specs += [row_spec, vec_spec]
        operands += list(then)
        out_shape += [jax.ShapeDtypeStruct((rows, width), BF16), vec_f32]
        out_specs += [row_spec, vec_spec]
    return _call(body, name=name, out_shape=out_shape, grid=(rows // tm,), in_specs=in_specs, out_specs=out_specs,
                 operands=operands, semantics=("arbitrary",), steps=steps)


def _shift_down(x, s, rows):
    return pltpu.roll(x, s, 0) if s else x


def _shift_up(x, s, rows):
    return pltpu.roll(x, rows - s, 0) if s else x


def _conv4_fwd(name, zx, cw, cb, steps=()):
    rows = zx.shape[0]
    off = D_INNER // LANES

    def body(x_ref, w_ref, b_ref, o_ref):
        x = x_ref[...]
        acc = b_ref[...] + w_ref[pl.ds(SSM_CONV - 1, 1), :] * x
        for s in range(1, SSM_CONV):
            acc = acc + w_ref[pl.ds(SSM_CONV - 1 - s, 1), :] * _shift_down(x, s, rows)
        valid = lax.broadcasted_iota(jnp.int32, (rows, 1), 0) >= PAD_ROWS
        o_ref[...] = jnp.where(valid, acc * _sigmoid(acc), 0.0)

    return _call(
        body, name=name, out_shape=jax.ShapeDtypeStruct((rows, D_XBC), F32), grid=(D_XBC // LANES,),
        in_specs=[pl.BlockSpec((rows, LANES), lambda j: (0, j + off)),
                  pl.BlockSpec((SSM_CONV, LANES), lambda j: (0, j)),
                  pl.BlockSpec((1, LANES), lambda j: (0, j))],
        out_specs=pl.BlockSpec((rows, LANES), lambda j: (0, j)), operands=[zx, cw, cb],
        semantics=("parallel",), steps=steps)


def _conv4_bwd(name, zx, dout, cw, cb, col0):
    rows, width = dout.shape
    zoff = (D_INNER + col0) // LANES
    woff = col0 // LANES

    def body(x_ref, d_ref, w_ref, b_ref, dx_ref, dw_ref, db_ref):
        x = x_ref[...]
        shifted = [_shift_down(x, s, rows) for s in range(SSM_CONV)]
        acc = b_ref[...]
        for s in range(SSM_CONV):
            acc = acc + w_ref[pl.ds(SSM_CONV - 1 - s, 1), :] * shifted[s]
        sig = _sigmoid(acc)
        valid = lax.broadcasted_iota(jnp.int32, (rows, 1), 0) >= PAD_ROWS
        dpre = jnp.where(valid, d_ref[...] * sig * (1.0 + acc * (1.0 - sig)), 0.0)
        dx = w_ref[pl.ds(SSM_CONV - 1, 1), :] * dpre
        for s in range(1, SSM_CONV):
            dx = dx + w_ref[pl.ds(SSM_CONV - 1 - s, 1), :] * _shift_up(dpre, s, rows)
        dx_ref[...] = dx.astype(BF16)
        for s in range(SSM_CONV):
            dw_ref[pl.ds(SSM_CONV - 1 - s, 1), :] = jnp.sum(dpre * shifted[s], axis=0, keepdims=True)
        db_ref[...] = jnp.sum(dpre, axis=0, keepdims=True)

    return pl.pallas_call(
        body, name=name,
        out_shape=(jax.ShapeDtypeStruct((rows, width), BF16), jax.ShapeDtypeStruct((SSM_CONV, width), F32),
                   jax.ShapeDtypeStruct((1, width), F32)),
        grid=(width // LANES,),
        in_specs=[pl.BlockSpec((rows, LANES), lambda j: (0, j + zoff)),
                  pl.BlockSpec((rows, LANES), lambda j: (0, j)),
                  pl.BlockSpec((SSM_CONV, LANES), lambda j: (0, j + woff)),
                  pl.BlockSpec((1, LANES), lambda j: (0, j + woff))],
        out_specs=(pl.BlockSpec((rows, LANES), lambda j: (0, j)),
                   pl.BlockSpec((SSM_CONV, LANES), lambda j: (0, j)),
                   pl.BlockSpec((1, LANES), lambda j: (0, j))),
        compiler_params=_cparams(("parallel",)),
    )(zx, dout, cw, cb)


def _ffn_conv_fwd(name, up, cw, cb, steps=()):
    rows = up.shape[0]
    nt = D_FF // LANES

    def body(g_ref, v_ref, wg_ref, wv_ref, bg_ref, bv_ref, o_ref):
        g, v = g_ref[...], v_ref[...]
        ug, uv = bg_ref[...], bv_ref[...]
        for s in range(FFN_CONV):
            ug = ug + wg_ref[pl.ds(FFN_CONV - 1 - s, 1), :] * _shift_down(g, s, rows)
            uv = uv + wv_ref[pl.ds(FFN_CONV - 1 - s, 1), :] * _shift_down(v, s, rows)
        o_ref[...] = (ug * _sigmoid(ug) * uv).astype(BF16)

    col = lambda shift: pl.BlockSpec((rows, LANES), lambda j: (0, j + shift))
    wsp = lambda shift: pl.BlockSpec((FFN_CONV, LANES), lambda j: (0, j + shift))
    bsp = lambda shift: pl.BlockSpec((1, LANES), lambda j: (0, j + shift))
    return _call(
        body, name=name, out_shape=jax.ShapeDtypeStruct((rows, D_FF), BF16), grid=(nt,),
        in_specs=[col(0), col(nt), wsp(0), wsp(nt), bsp(0), bsp(nt)],
        out_specs=pl.BlockSpec((rows, LANES), lambda j: (0, j)), operands=[up, up, cw, cw, cb, cb],
        semantics=("parallel",), steps=steps)


def _ffn_conv_bwd(name, up, dact, cw, cb, steps=()):
    rows = up.shape[0]
    nt = D_FF // LANES

    def body(g_ref, v_ref, d_ref, wg_ref, wv_ref, bg_ref, bv_ref, dxg_ref, dxv_ref, dwg_ref, dwv_ref, dbg_ref, dbv_ref):
        g, v = g_ref[...], v_ref[...]
        gs = [_shift_down(g, s, rows) for s in range(FFN_CONV)]
        vs = [_shift_down(v, s, rows) for s in range(FFN_CONV)]
        ug, uv = bg_ref[...], bv_ref[...]
        for s in range(FFN_CONV):
            ug = ug + wg_ref[pl.ds(FFN_CONV - 1 - s, 1), :] * gs[s]
            uv = uv + wv_ref[pl.ds(FFN_CONV - 1 - s, 1), :] * vs[s]
        sig = _sigmoid(ug)
        dsig = d_ref[...] * sig
        for dpre, src, w_ref, dx_ref, dw_ref, db_ref in (
                (dsig * uv * (1.0 + ug * (1.0 - sig)), gs, wg_ref, dxg_ref, dwg_ref, dbg_ref),
                (dsig * ug, vs, wv_ref, dxv_ref, dwv_ref, dbv_ref)):
            dx = w_ref[pl.ds(FFN_CONV - 1, 1), :] * dpre
            for s in range(1, FFN_CONV):
                dx = dx + w_ref[pl.ds(FFN_CONV - 1 - s, 1), :] * _shift_up(dpre, s, rows)
            dx_ref[...] = dx.astype(BF16)
            for s in range(FFN_CONV):
                dw_ref[pl.ds(FFN_CONV - 1 - s, 1), :] = jnp.sum(dpre * src[s], axis=0, keepdims=True)
            db_ref[...] = jnp.sum(dpre, axis=0, keepdims=True)

    col = lambda shift: pl.BlockSpec((rows, LANES), lambda j: (0, j + shift))
    wsp = lambda shift: pl.BlockSpec((FFN_CONV, LANES), lambda j: (0, j + shift))
    bsp = lambda shift: pl.BlockSpec((1, LANES), lambda j: (0, j + shift))
    dx_shape = jax.ShapeDtypeStruct((rows, D_FF), BF16)
    dw_shape = jax.ShapeDtypeStruct((FFN_CONV, D_FF), F32)
    db_shape = jax.ShapeDtypeStruct((1, D_FF), F32)
    return _call(
        body, name=name, out_shape=(dx_shape, dx_shape, dw_shape, dw_shape, db_shape, db_shape), grid=(nt,),
        in_specs=[col(0), col(nt), col(0), wsp(0), wsp(nt), bsp(0), bsp(nt)],
        out_specs=(col(0), col(0), wsp(0), wsp(0), bsp(0), bsp(0)),
        operands=[up, up, dact, cw, cw, cb, cb], semantics=("parallel",), steps=steps)


def _dt_fwd(name, dtr, bias):
    rows = dtr.shape[0]
    tm = _row_tile(rows, LANES)

    def body(d_ref, b_ref, o_ref):
        v = d_ref[...] + b_ref[...]
        sp = jnp.maximum(v, 0.0) + jnp.log1p(jnp.exp(-jnp.abs(v)))
        lane = lax.broadcasted_iota(jnp.int32, (tm, LANES), 1)
        ok = _rows_mask(pl.program_id(0), tm) & (lane < SSM_HEADS)
        o_ref[...] = jnp.where(ok, sp, 0.0)

    return pl.pallas_call(
        body, name=name, out_shape=jax.ShapeDtypeStruct((rows, LANES), F32), grid=(rows // tm,),
        in_specs=[pl.BlockSpec((tm, LANES), lambda i: (i, 0)), pl.BlockSpec((1, LANES), lambda i: (0, 0))],
        out_specs=pl.BlockSpec((tm, LANES), lambda i: (i, 0)), compiler_params=_cparams(("parallel",)),
    )(dtr, bias)


def _dt_bwd(name, ddt, dtr, bias):
    rows = dtr.shape[0]
    tm = _row_tile(rows, LANES)

    def body(g_ref, d_ref, b_ref, o_ref, db_ref):
        i = pl.program_id(0)
        lane = lax.broadcasted_iota(jnp.int32, (tm, LANES), 1)
        ok = _rows_mask(i, tm) & (lane < SSM_HEADS)
        dv = jnp.where(ok, g_ref[...] * _sigmoid(d_ref[...] + b_ref[...]), 0.0)
        o_ref[...] = dv.astype(BF16)

        @pl.when(i == 0)
        def _():
            db_ref[...] = jnp.zeros_like(db_ref)

        db_ref[...] += jnp.sum(dv, axis=0, keepdims=True)

    row_spec = pl.BlockSpec((tm, LANES), lambda i: (i, 0))
    vec_spec = pl.BlockSpec((1, LANES), lambda i: (0, 0))
    return pl.pallas_call(
        body, name=name,
        out_shape=(jax.ShapeDtypeStruct((rows, LANES), BF16), jax.ShapeDtypeStruct((1, LANES), F32)),
        grid=(rows // tm,), in_specs=[row_spec, row_spec, vec_spec], out_specs=(row_spec, vec_spec),
        compiler_params=_cparams(("arbitrary",)),
    )(ddt, dtr, bias)


def _gate_fwd(name, y, zx, w, steps=()):
    rows = y.shape[0]
    tm = _row_tile(rows, D_INNER)

    def body(y_ref, z_ref, w_ref, o_ref):
        z = z_ref[...]
        g = y_ref[...] * (z * _sigmoid(z))
        r = lax.rsqrt(jnp.mean(g * g, axis=-1, keepdims=True) + RMS_EPS)
        o_ref[...] = (g * r * w_ref[...]).astype(BF16)

    row_spec = pl.BlockSpec((tm, D_INNER), lambda i: (i, 0))
    return _call(
        body, name=name, out_shape=jax.ShapeDtypeStruct((rows, D_INNER), BF16), grid=(rows // tm,),
        in_specs=[row_spec, row_spec, pl.BlockSpec((1, D_INNER), lambda i: (0, 0))],
        out_specs=row_spec, operands=[y, zx, w], semantics=("parallel",), steps=steps)


def _gate_bwd(name, dyn, y, zx, w):
    rows = y.shape[0]
    tm = _row_tile(rows, D_INNER)

    def body(d_ref, y_ref, z_ref, w_ref, dy_ref, dz_ref, dw_ref):
        i = pl.program_id(0)
        z, yv = z_ref[...], y_ref[...]
        sig = _sigmoid(z)
        sz = z * sig
        g = yv * sz
        r = lax.rsqrt(jnp.mean(g * g, axis=-1, keepdims=True) + RMS_EPS)
        ghat = g * r
        dn = d_ref[...]
        dghat = dn * w_ref[...]
        dg = r * (dghat - ghat * jnp.mean(dghat * ghat, axis=-1, keepdims=True))
        dy_ref[...] = dg * sz
        dz_ref[...] = (dg * yv * sig * (1.0 + z * (1.0 - sig))).astype(BF16)

        @pl.when(i == 0)
        def _():
            dw_ref[...] = jnp.zeros_like(dw_ref)

        dw_ref[...] += jnp.sum(dn * ghat, axis=0, keepdims=True)

    row_spec = pl.BlockSpec((tm, D_INNER), lambda i: (i, 0))
    vec_spec = pl.BlockSpec((1, D_INNER), lambda i: (0, 0))
    return pl.pallas_call(
        body, name=name,
        out_shape=(jax.ShapeDtypeStruct((rows, D_INNER), F32), jax.ShapeDtypeStruct((rows, D_INNER), BF16),
                   jax.ShapeDtypeStruct((1, D_INNER), F32)),
        grid=(rows // tm,), in_specs=[row_spec, row_spec, row_spec, vec_spec],
        out_specs=(row_spec, row_spec, vec_spec), compiler_params=_cparams(("arbitrary",)),
    )(dyn, y, zx, w)


def _split3(x):
    hi = x.astype(BF16)
    r1 = x - hi.astype(F32)
    mid = r1.astype(BF16)
    lo = (r1 - mid.astype(F32)).astype(BF16)
    return hi, mid, lo


def _dot3_data_lhs(x, sel):
    sel16 = sel.astype(F32).astype(BF16)
    hi, mid, lo = _split3(x)
    return _dot(hi, sel16) + _dot(mid, sel16) + _dot(lo, sel16)


def _dot2_data_lhs(x, sel):
    sel16 = sel.astype(F32).astype(BF16)
    hi = x.astype(BF16)
    mid = (x - hi.astype(F32)).astype(BF16)
    return _dot(hi, sel16) + _dot(mid, sel16)


def _dot3_data_rhs(sel, x):
    sel16 = sel.astype(F32).astype(BF16)
    hi, mid, lo = _split3(x)
    return _dot(sel16, hi) + _dot(sel16, mid) + _dot(sel16, lo)


def _causal_masks():
    r = lax.broadcasted_iota(jnp.int32, (CHUNK, CHUNK), 0)
    c = lax.broadcasted_iota(jnp.int32, (CHUNK, CHUNK), 1)
    return r >= c, r <= c


def _expand_heads_matrix(g):
    k = lax.broadcasted_iota(jnp.int32, (LANES, GROUP_W), 0)
    j = lax.broadcasted_iota(jnp.int32, (LANES, GROUP_W), 1)
    return HEADS_PER_GROUP * g + jnp.right_shift(j, 6) == k


def _reduce_heads_matrix(g):
    j = lax.broadcasted_iota(jnp.int32, (GROUP_W, LANES), 0)
    k = lax.broadcasted_iota(jnp.int32, (GROUP_W, LANES), 1)
    return HEADS_PER_GROUP * g + jnp.right_shift(j, 6) == k


def _reduce_pair_matrix(g, p):
    j = lax.broadcasted_iota(jnp.int32, (LANES, LANES), 0)
    k = lax.broadcasted_iota(jnp.int32, (LANES, LANES), 1)
    return HEADS_PER_GROUP * g + 2 * p + jnp.right_shift(j, 6) == k


def _group_cols(ref, g, width):
    return ref.at[:, pl.ds(g * width, width)]


def _ssd_prep(name, dt, a128, steps=()):
    rows = dt.shape[0]
    nc = rows // CHUNK

    def body(dt_ref, a_ref, dte_ref, acs_ref, acst_ref):
        causal, _ = _causal_masks()
        dtv = dt_ref[...]
        acs = _dot3_data_rhs(causal, dtv) * a_ref[...]
        acst_ref[...] = acs.T[0:SSM_HEADS]
        for g in range(N_GROUPS):
            expand = _expand_heads_matrix(g)
            _group_cols(dte_ref, g, GROUP_W)[...] = _dot3_data_lhs(dtv, expand)
            _group_cols(acs_ref, g, GROUP_W)[...] = _dot3_data_lhs(acs, expand)

    blk = pl.BlockSpec((CHUNK, D_INNER), lambda c: (c, 0))
    shp = jax.ShapeDtypeStruct((rows, D_INNER), F32)
    return _call(
        body, name=name, out_shape=(shp, shp, jax.ShapeDtypeStruct((nc, SSM_HEADS, CHUNK), F32)), grid=(nc,),
        in_specs=[pl.BlockSpec((CHUNK, LANES), lambda c: (c, 0)), pl.BlockSpec((1, LANES), lambda c: (0, 0))],
        out_specs=(blk, blk, pl.BlockSpec((None, SSM_HEADS, CHUNK), lambda c: (c, 0, 0))),
        operands=[dt, a128], semantics=("parallel",), steps=steps)


def _ssd_common(x_ref, b_ref, c_ref, dte_ref, acs_ref):
    x = x_ref[...]
    dt_exp = dte_ref[...]
    acs_exp = acs_ref[...]
    tot_exp = acs_ref[pl.ds(CHUNK - 1, 1), :]
    xdt = x * dt_exp
    e_exp = jnp.exp(acs_exp)
    f_exp = jnp.exp(tot_exp - acs_exp)
    return _causal_masks(), x, dt_exp, acs_exp, tot_exp, xdt, e_exp, f_exp, b_ref[...], c_ref[...]


def _pair_decay(acs_pair, acs_row, e, causal):
    lane = lax.broadcasted_iota(jnp.int32, (CHUNK, LANES), 1)
    mine = (lane < HEAD_DIM) if e == 0 else (lane >= HEAD_DIM)
    a_l = jnp.where(mine, acs_pair, pltpu.roll(acs_pair, HEAD_DIM, 1))
    seg = a_l - acs_row
    dm = jnp.where(causal[0], jnp.exp(jnp.minimum(seg, 0.0)), 0.0)
    dmt = jnp.where(causal[1], jnp.exp(jnp.minimum(-seg, 0.0)), 0.0)
    return dm, dmt


def _ssd_specs(index_of_chunk):
    wide = pl.BlockSpec((CHUNK, D_INNER), lambda c: (index_of_chunk(c), 0))
    b_spec = pl.BlockSpec((CHUNK, D_BC), lambda c: (index_of_chunk(c), D_INNER // D_BC))
    c_spec = pl.BlockSpec((CHUNK, D_BC), lambda c: (index_of_chunk(c), D_INNER // D_BC + 1))
    rows_spec = pl.BlockSpec((None, SSM_HEADS, CHUNK), lambda c: (index_of_chunk(c), 0, 0))
    state_spec = pl.BlockSpec((N_GROUPS, None, D_STATE, GROUP_W), lambda c: (0, index_of_chunk(c), 0, 0))
    return wide, b_spec, c_spec, rows_spec, state_spec


def _ssd_fwd(name, xbc, dt_exp, acs_exp, acs_rows, dskexp, steps=()):
    rows = xbc.shape[0]
    nc = rows // CHUNK

    def body(x_ref, b_ref, c_ref, dte_ref, acs_ref, acst_ref, dsk_ref, y_ref, st_ref, s_scr):
        @pl.when(pl.program_id(0) == 0)
        def _():
            s_scr[...] = jnp.zeros_like(s_scr)

        lane = lax.broadcasted_iota(jnp.int32, (CHUNK, LANES), 1)
        for g in range(N_GROUPS):
            y_g = _group_cols(y_ref, g, GROUP_W)
            causal, x, _, acs_exp_v, tot_exp, xdt, e_exp, f_exp, bm, cm = _ssd_common(
                _group_cols(x_ref, g, GROUP_W), _group_cols(b_ref, g, D_STATE), _group_cols(c_ref, g, D_STATE),
                _group_cols(dte_ref, g, GROUP_W), _group_cols(acs_ref, g, GROUP_W))
            state = s_scr[g]
            st_ref[g] = state
            cb16, bb16 = cm.astype(BF16), bm.astype(BF16)
            cb = _dot_nt(cb16, bb16)
            base = e_exp * _dot(cb16, state.astype(BF16)) + _group_cols(dsk_ref, g, GROUP_W)[...] * x
            for p in range(HEADS_PER_GROUP // 2):
                sl = slice(p * LANES, (p + 1) * LANES)
                xp = xdt[:, sl].astype(BF16)
                yd = []
                for e in range(2):
                    acs_row = acst_ref[pl.ds(g * HEADS_PER_GROUP + 2 * p + e, 1), :]
                    dm, _ = _pair_decay(acs_exp_v[:, sl], acs_row, e, causal)
                    yd.append(_dot((cb * dm).astype(BF16), xp))
                y_g[:, sl] = jnp.where(lane < HEAD_DIM, yd[0], yd[1]) + base[:, sl]
            s_scr[g] = jnp.exp(tot_exp) * state + _dot_tn(bb16, (f_exp * xdt).astype(BF16))

    wide, b_spec, c_spec, rows_spec, state_spec = _ssd_specs(lambda c: c)
    return _call(
        body, name=name,
        out_shape=(jax.ShapeDtypeStruct((rows, D_INNER), F32),
                   jax.ShapeDtypeStruct((N_GROUPS, nc, D_STATE, GROUP_W), F32)),
        grid=(nc,),
        in_specs=[wide, b_spec, c_spec, wide, wide, rows_spec, pl.BlockSpec((1, D_INNER), lambda c: (0, 0))],
        out_specs=(wide, state_spec),
        scratch_shapes=[pltpu.VMEM((N_GROUPS, D_STATE, GROUP_W), F32)],
        operands=[xbc, xbc, xbc, dt_exp, acs_exp, acs_rows, dskexp], semantics=("arbitrary",), steps=steps)


def _ssd_bwd(name, xbc, dt_exp, acs_exp, acs_rows, dt, a128, dskexp, dy, states, steps=()):
    rows = xbc.shape[0]
    nc = rows // CHUNK
    last = nc - 1

    def body(x_ref, b_ref, c_ref, dte_ref, acs_ref, acst_ref, dt_ref, a128_ref, dsk_all, dy_all, st_all,
             dx_all, db_all, dc_all, ddt_ref, dalog_ref, ddsk_ref, ds_all):
        @pl.when(pl.program_id(0) == 0)
        def _():
            ds_all[...] = jnp.zeros_like(ds_all)
            dalog_ref[...] = jnp.zeros_like(dalog_ref)
            ddsk_ref[...] = jnp.zeros_like(ddsk_ref)

        dacs = jnp.zeros((CHUNK, LANES), F32)
        ddt_x = jnp.zeros((CHUNK, LANES), F32)
        for g in range(N_GROUPS):
            dacs_g, ddt_x_g = group(
                g, _group_cols(x_ref, g, GROUP_W), _group_cols(b_ref, g, D_STATE), _group_cols(c_ref, g, D_STATE),
                _group_cols(dte_ref, g, GROUP_W), _group_cols(acs_ref, g, GROUP_W), acst_ref,
                _group_cols(dsk_all, g, GROUP_W), _group_cols(dy_all, g, GROUP_W), st_all.at[g],
                _group_cols(dx_all, g, GROUP_W), _group_cols(db_all, g, D_STATE), _group_cols(dc_all, g, D_STATE),
                ddsk_ref, ds_all.at[g])
            dacs, ddt_x = dacs + dacs_g, ddt_x + ddt_x_g
        _, causal_t = _causal_masks()
        da = _dot3_data_rhs(causal_t, dacs)
        ddt_ref[...] = da * a128_ref[...] + ddt_x
        dalog_ref[...] += jnp.sum(da * dt_ref[...], axis=0, keepdims=True) * a128_ref[...]

    def group(g, x_ref, b_ref, c_ref, dte_ref, acs_ref, acst_ref, dsk_ref, dy_ref, st_ref,
              dx_ref, db_ref, dc_ref, ddsk_ref, ds_scr):
        causal, x, dt_exp, acs_exp_v, tot_exp, xdt, e_exp, f_exp, bm, cm = _ssd_common(
            x_ref, b_ref, c_ref, dte_ref, acs_ref)
        reduce_heads = _reduce_heads_matrix(g)
        state, dstate = st_ref[...], ds_scr[...]
        dyv = dy_ref[...]
        cb16, bb16 = cm.astype(BF16), bm.astype(BF16)
        s16, ds16 = state.astype(BF16), dstate.astype(BF16)
        cb = _dot_nt(cb16, bb16)
        cbt = _dot_nt(bb16, cb16)
        cs = _dot(cb16, s16)
        bds = _dot(bb16, ds16)
        edy = e_exp * dyv
        fx = f_exp * xdt
        dxdt_base = f_exp * bds
        dc_acc = _dot_nt(edy.astype(BF16), s16)
        db_acc = _dot_nt(fx.astype(BF16), ds16)
        ds_scr[...] = jnp.exp(tot_exp) * dstate + _dot_tn(cb16, edy.astype(BF16))
        q = fx * bds
        dacs = _dot2_data_lhs(edy * cs - q, reduce_heads)
        dtot = jnp.sum(_dot2_data_lhs(q + jnp.exp(tot_exp) * dstate * state, reduce_heads), axis=0, keepdims=True)
        ddsk_ref[...] += jnp.sum(_dot2_data_lhs(dyv * x, reduce_heads), axis=0, keepdims=True)
        lane = lax.broadcasted_iota(jnp.int32, (CHUNK, LANES), 1)
        dcb = jnp.zeros((CHUNK, CHUNK), F32)
        dcbt = jnp.zeros((CHUNK, CHUNK), F32)
        ddt_x = jnp.zeros((CHUNK, LANES), F32)
        for p in range(HEADS_PER_GROUP // 2):
            sl = slice(p * LANES, (p + 1) * LANES)
            xp, dyp = xdt[:, sl], dyv[:, sl]
            xp16, dyp16 = xp.astype(BF16), dyp.astype(BF16)
            dxh = []
            for e in range(2):
                h = 2 * p + e
                mine = (lane < HEAD_DIM) if e == 0 else (lane >= HEAD_DIM)
                acs_row = acst_ref[pl.ds(g * HEADS_PER_GROUP + h, 1), :]
                dm, dmt = _pair_decay(acs_exp_v[:, sl], acs_row, e, causal)
                m, mt = cb * dm, cbt * dmt
                xh16 = jnp.where(mine, xp, 0.0).astype(BF16)
                dyh16 = jnp.where(mine, dyp, 0.0).astype(BF16)
                d_m = _dot_nt(dyh16, xp16)
                d_mt = _dot_nt(xh16, dyp16)
                dacs_h = (jnp.sum(d_m * m, axis=-1, keepdims=True)
                          - jnp.sum(d_mt * mt, axis=-1, keepdims=True))
                dacs = dacs + jnp.where(lane == HEADS_PER_GROUP * g + h, dacs_h, 0.0)
                dcb = dcb + d_m * dm
                dcbt = dcbt + d_mt * dmt
                dxh.append(_dot(mt.astype(BF16), dyp16))
            dxdt = jnp.where(lane < HEAD_DIM, dxh[0], dxh[1]) + dxdt_base[:, sl]
            dx_ref[:, sl] = dxdt * dt_exp[:, sl] + dsk_ref[:, sl] * dyp
            ddt_x = ddt_x + _dot2_data_lhs(dxdt * x[:, sl], _reduce_pair_matrix(g, p))
        dc_ref[...] = dc_acc + _dot(dcb.astype(BF16), bb16)
        db_ref[...] = db_acc + _dot(dcbt.astype(BF16), cb16)
        row = lax.broadcasted_iota(jnp.int32, (CHUNK, LANES), 0)
        return dacs + jnp.where(row == CHUNK - 1, dtot, 0.0), ddt_x

    wide, b_spec, c_spec, rows_spec, state_spec = _ssd_specs(lambda c: last - c)
    heads_spec = pl.BlockSpec((CHUNK, LANES), lambda c: (last - c, 0))
    vec_spec = pl.BlockSpec((1, LANES), lambda c: (0, 0))
    bc_out = pl.BlockSpec((CHUNK, D_BC), lambda c: (last - c, 0))
    vec_shape = jax.ShapeDtypeStruct((1, LANES), F32)
    return _call(
        body, name=name,
        out_shape=(jax.ShapeDtypeStruct((rows, D_INNER), F32), jax.ShapeDtypeStruct((rows, D_BC), F32),
                   jax.ShapeDtypeStruct((rows, D_BC), F32), jax.ShapeDtypeStruct((rows, LANES), F32),
                   vec_shape, vec_shape),
        grid=(nc,),
        in_specs=[wide, b_spec, c_spec, wide, wide, rows_spec, heads_spec, vec_spec,
                  pl.BlockSpec((1, D_INNER), lambda c: (0, 0)), wide, state_spec],
        out_specs=(wide, bc_out, bc_out, heads_spec, vec_spec, vec_spec),
        scratch_shapes=[pltpu.VMEM((N_GROUPS, D_STATE, GROUP_W), F32)],
        operands=[xbc, xbc, xbc, dt_exp, acs_exp, acs_rows, dt, a128, dskexp, dy, states],
        semantics=("arbitrary",), steps=steps)


def _attn_visible(b, heads=1):
    row = jnp.bitwise_and(lax.broadcasted_iota(jnp.int32, (heads * CHUNK, 3 * CHUNK), 0), CHUNK - 1)
    col = lax.broadcasted_iota(jnp.int32, (heads * CHUNK, 3 * CHUNK), 1)
    bb = b + jnp.zeros_like(col)
    meta = (col < CHUNK) & (bb >= 1) & (col >= PAD_ROWS)
    prev = (col >= CHUNK) & (col < 2 * CHUNK) & (bb >= 2) & ((col - CHUNK) > row)
    cur = (col >= 2 * CHUNK) & ((col - 2 * CHUNK) <= row) & ((bb >= 1) | ((col - 2 * CHUNK) >= PAD_ROWS))
    return meta | prev | cur


def _attn_visible4(b):
    return _attn_visible(b, 4)


def _stack_heads(q_ref, sink_ref, kvh, scale):
    lane = lax.broadcasted_iota(jnp.int32, (CHUNK, LANES), 1)
    parts, sinks = [], []
    for pp in range(2):
        pair = kvh * 2 + pp
        qp = q_ref[:, pair * LANES:(pair + 1) * LANES] * scale
        for e in range(2):
            mine = (lane < HEAD_DIM) if e == 0 else (lane >= HEAD_DIM)
            parts.append(jnp.where(mine, qp, 0.0).astype(BF16))
            sinks.append(jnp.full((CHUNK, 1), sink_ref[2 * pair + e], F32))
    return jnp.concatenate(parts, axis=0), jnp.concatenate(sinks, axis=0)


def _attn_operands(q_ref, k0, kp, kc, v0, vp, vc, sink_ref):
    kcat, vcat, q4, sink4 = [], [], [], []
    for kvh in range(N_KV_HEADS):
        ksl = slice(kvh * LANES, (kvh + 1) * LANES)
        kcat.append(jnp.concatenate([k0[:, ksl], kp[:, ksl], kc[:, ksl]], axis=0).astype(BF16))
        vcat.append(jnp.concatenate([v0[:, ksl], vp[:, ksl], vc[:, ksl]], axis=0).astype(BF16))
        stacked, sinks = _stack_heads(q_ref, sink_ref, kvh, ATTN_SCALE)
        q4.append(stacked)
        sink4.append(sinks)
    return kcat, vcat, q4, sink4


def _attn_probs(q4, kcat, visible, sink4):
    heads = range(N_KV_HEADS)
    s = [jnp.where(visible, _dot_nt(q4[h], kcat[h]), NEG_INF) for h in heads]
    m = [jnp.maximum(jnp.max(s[h], axis=-1, keepdims=True), sink4[h]) for h in heads]
    pe = [jnp.exp(s[h] - m[h]) for h in heads]
    pe_sink = [jnp.exp(sink4[h] - m[h]) for h in heads]
    inv = [1.0 / (jnp.sum(pe[h], axis=-1, keepdims=True) + pe_sink[h]) for h in heads]
    return [pe[h] * inv[h] for h in heads], [pe_sink[h] * inv[h] for h in heads]


def _unstack_pairs(stacked, pp):
    lane = lax.broadcasted_iota(jnp.int32, (CHUNK, LANES), 1)
    return jnp.where(lane < HEAD_DIM, stacked[(2 * pp) * CHUNK:(2 * pp + 1) * CHUNK],
                     stacked[(2 * pp + 1) * CHUNK:(2 * pp + 2) * CHUNK])


def _attn_specs(colblock):
    blk = lambda f: pl.BlockSpec((CHUNK, 2 * D_KV), f)
    return [blk(lambda b: (0, colblock)), blk(lambda b: (jnp.maximum(b - 1, 0), colblock)), blk(lambda b: (b, colblock))]


def _attn_fwd(name, q, kv2, sinks, steps=()):
    rows = q.shape[0]

    def body(q_ref, k0, kp, kc, v0, vp, vc, sink_ref, o_ref):
        visible = _attn_visible4(pl.program_id(0))
        kcat, vcat, q4, sink4 = _attn_operands(q_ref, k0, kp, kc, v0, vp, vc, sink_ref)
        pn, _ = _attn_probs(q4, kcat, visible, sink4)
        o4 = [_dot(pn[h].astype(BF16), vcat[h]) for h in range(N_KV_HEADS)]
        for kvh in range(N_KV_HEADS):
            for pp in range(2):
                qsl = slice((kvh * 2 + pp) * LANES, (kvh * 2 + pp + 1) * LANES)
                o_ref[:, qsl] = _unstack_pairs(o4[kvh], pp).astype(BF16)

    return _call(
        body, name=name, out_shape=jax.ShapeDtypeStruct((rows, D_MODEL), BF16), grid=(rows // CHUNK,),
        in_specs=[pl.BlockSpec((CHUNK, D_MODEL), lambda b: (b, 0))] + _attn_specs(0) + _attn_specs(1)
        + [pl.BlockSpec(memory_space=pltpu.SMEM)],
        out_specs=pl.BlockSpec((CHUNK, D_MODEL), lambda b: (b, 0)),
        operands=[q, kv2, kv2, kv2, kv2, kv2, kv2, sinks], semantics=("parallel",), steps=steps)


def _attn_bwd(name, q, kv2, sinks, do, steps=()):
    rows = q.shape[0]

    def body(q_ref, k0, kp, kc, v0, vp, vc, sink_ref, do_ref,
             dq_ref, dkc_ref, dkp_ref, dvc_ref, dvp_ref, dkm_ref, dvm_ref, dsink_ref):
        @pl.when(pl.program_id(0) == 0)
        def _():
            dkm_ref[...] = jnp.zeros_like(dkm_ref)
            dvm_ref[...] = jnp.zeros_like(dvm_ref)
            dsink_ref[...] = jnp.zeros_like(dsink_ref)

        visible = _attn_visible4(pl.program_id(0))
        heads = range(N_KV_HEADS)
        lane1 = lax.broadcasted_iota(jnp.int32, (1, LANES), 1)
        kcat, vcat, q4, sink4 = _attn_operands(q_ref, k0, kp, kc, v0, vp, vc, sink_ref)
        do4 = [_stack_heads(do_ref, sink_ref, h, 1.0)[0] for h in heads]
        pn, psink = _attn_probs(q4, kcat, visible, sink4)
        dp = [_dot_nt(do4[h], vcat[h]) for h in heads]
        delta = [jnp.sum(pn[h] * dp[h], axis=-1, keepdims=True) for h in heads]
        ds16 = [(pn[h] * (dp[h] - delta[h])).astype(BF16) for h in heads]
        dq4 = [_dot(ds16[h], kcat[h]) for h in heads]
        dk_acc = [_dot_tn(ds16[h], q4[h]) for h in heads]
        dv_acc = [_dot_tn(pn[h].astype(BF16), do4[h]) for h in heads]
        dsink = jnp.zeros((1, LANES), F32)
        for kvh in heads:
            ksl = slice(kvh * LANES, (kvh + 1) * LANES)
            sink_terms = psink[kvh] * delta[kvh]
            for j in range(4):
                part = jnp.sum(sink_terms[j * CHUNK:(j + 1) * CHUNK], axis=0, keepdims=True)
                dsink = dsink - jnp.where(lane1 == kvh * 4 + j, part, 0.0)
            for pp in range(2):
                qsl = slice((kvh * 2 + pp) * LANES, (kvh * 2 + pp + 1) * LANES)
                dq_ref[:, qsl] = (_unstack_pairs(dq4[kvh], pp) * ATTN_SCALE).astype(BF16)
            dkm_ref[:, ksl] += dk_acc[kvh][0:CHUNK]
            dvm_ref[:, ksl] += dv_acc[kvh][0:CHUNK]
            dkp_ref[:, ksl] = dk_acc[kvh][CHUNK:2 * CHUNK]
            dvp_ref[:, ksl] = dv_acc[kvh][CHUNK:2 * CHUNK]
            dkc_ref[:, ksl] = dk_acc[kvh][2 * CHUNK:3 * CHUNK]
            dvc_ref[:, ksl] = dv_acc[kvh][2 * CHUNK:3 * CHUNK]
        dsink_ref[...] += dsink

    qspec = pl.BlockSpec((CHUNK, D_MODEL), lambda b: (b, 0))
    kvspec = pl.BlockSpec((CHUNK, 2 * D_KV), lambda b: (b, 0))
    fixed = pl.BlockSpec((CHUNK, 2 * D_KV), lambda b: (0, 0))
    kv_shape = jax.ShapeDtypeStruct((rows, 2 * D_KV), F32)
    meta_shape = jax.ShapeDtypeStruct((CHUNK, 2 * D_KV), F32)
    return _call(
        body, name=name,
        out_shape=(jax.ShapeDtypeStruct((rows, D_MODEL), BF16), kv_shape, kv_shape, kv_shape, kv_shape,
                   meta_shape, meta_shape, jax.ShapeDtypeStruct((1, LANES), F32)),
        grid=(rows // CHUNK,),
        in_specs=[qspec] + _attn_specs(0) + _attn_specs(1) + [pl.BlockSpec(memory_space=pltpu.SMEM), qspec],
        out_specs=(qspec, kvspec, kvspec, kvspec, kvspec, fixed, fixed, pl.BlockSpec((1, LANES), lambda b: (0, 0))),
        operands=[q, kv2, kv2, kv2, kv2, kv2, kv2, sinks, do], semantics=("arbitrary",), steps=steps)


def _kv_grad_combine(name, dk_cur, dk_prev, dk_meta, dv_cur, dv_prev, dv_meta):
    rows = dk_cur.shape[0]
    nb = rows // CHUNK
    width = 2 * D_KV

    def body(kc_ref, kp_ref, km_ref, vc_ref, vp_ref, vm_ref, o_ref):
        jj = pl.program_id(0) + jnp.zeros((CHUNK, 1), jnp.int32)
        for half, (c_ref, p_ref, m_ref) in enumerate(((kc_ref, kp_ref, km_ref), (vc_ref, vp_ref, vm_ref))):
            total = c_ref[...] + jnp.where(jj < nb - 1, p_ref[...], 0.0) + jnp.where(jj == 0, m_ref[...], 0.0)
            o_ref[:, half * width:(half + 1) * width] = total.astype(BF16)

    blk = lambda f: pl.BlockSpec((CHUNK, width), f)
    three = lambda: [blk(lambda j: (j, 0)), blk(lambda j: (jnp.minimum(j + 1, nb - 1), 0)), blk(lambda j: (0, 0))]
    return pl.pallas_call(
        body, name=name, out_shape=jax.ShapeDtypeStruct((rows, 2 * width), BF16), grid=(nb,),
        in_specs=three() + three(), out_specs=pl.BlockSpec((CHUNK, 2 * width), lambda j: (j, 0)),
        compiler_params=_cparams(("parallel",)),
    )(dk_cur, dk_prev, dk_meta, dv_cur, dv_prev, dv_meta)


def _adamw(name, w, g, m, v, steps=()):
    rows, width = w.shape
    tr = rows
    for cand in range(8, rows + 1, 8):
        if rows % cand == 0 and cand * width * 4 <= (1 << 20):
            tr = cand

    def body(w_ref, g_ref, m_ref, v_ref, d_ref, mo_ref, vo_ref):
        gv = g_ref[...]
        mn = ADAM_B1 * m_ref[...] + (1.0 - ADAM_B1) * gv
        vn = ADAM_B2 * v_ref[...] + (1.0 - ADAM_B2) * (gv * gv)
        m_hat = mn / (1.0 - ADAM_B1 ** ADAM_STEP)
        v_hat = vn / (1.0 - ADAM_B2 ** ADAM_STEP)
        d_ref[...] = -ADAM_LR * (m_hat / (jnp.sqrt(v_hat) + ADAM_EPS) + ADAM_WD * w_ref[...])
        mo_ref[...] = mn
        vo_ref[...] = vn

    blk = pl.BlockSpec((tr, width), lambda i: (i, 0))
    shp = jax.ShapeDtypeStruct((rows, width), F32)
    return _call(body, name=name, out_shape=(shp, shp, shp), grid=(rows // tr,), in_specs=[blk] * 4,
                 out_specs=(blk,) * 3, operands=[w, g, m, v], semantics=("parallel",), steps=steps)


class _GivenWeights:
    def __init__(self, p):
        self.p = p
        self.grads = {}

    def weight(self, name, layer=None):
        return self.p[name] if layer is None else self.p[name][layer]

    def steps(self, kernel):
        return ()

    def grad(self, name, layer, g):
        self.grads[(name, layer)] = g


def _ffn_fwd(tag, h, hn, p, i, plan):
    up = _mm_nn_bychip(f"ffn{tag}_up", hn, plan.weight("f_w_up", i), steps=plan.steps(f"ffn{tag}_up"))
    act = _ffn_conv_fwd(f"ffn{tag}_conv", up, p["f_conv_w"][i], p["f_conv_b"][i:i + 1], steps=plan.steps(f"ffn{tag}_conv"))
    pre = _mm(f"ffn{tag}_down", act, plan.weight("f_w_down", i), "nn")
    return pre, (h, hn, up, act, pre)


def _ffn_bwd(tag, dpre, saved, p, i, plan):
    h, hn, up, act, pre = saved
    plan.grad("f_w_down", i, _mm(f"ffn{tag}_down_dw", act, dpre, "tn", out_dtype=BF16))
    dact = _mm(f"ffn{tag}_down_dx", dpre, plan.weight("f_w_down", i), "nt", steps=plan.steps(f"ffn{tag}_down_dx"))
    dug, duv, gwg, gwv, gbg, gbv = _ffn_conv_bwd(f"ffn{tag}_conv_bwd", up, dact, p["f_conv_w"][i], p["f_conv_b"][i:i + 1],
                                                 steps=plan.steps(f"ffn{tag}_conv_bwd"))
    g_cw, g_cb = jnp.concatenate([gwg, gwv], axis=1), jnp.concatenate([gbg, gbv], axis=1)
    w_up = plan.weight("f_w_up", i)
    n = w_up.shape[2]
    dhn = _mm_nt_bychip(f"ffn{tag}_up_dx_gate", dug, w_up, 0)
    dhn = _mm_nt_bychip(f"ffn{tag}_up_dx_val", duv, w_up, N_CHIPS // 2, acc=dhn)
    g_up = _mm_tn_bychip(f"ffn{tag}_up_dw_gate", hn, dug, n, 0)
    plan.grad("f_w_up", i, _mm_tn_bychip(f"ffn{tag}_up_dw_val", hn, duv, n, N_CHIPS // 2, into=g_up))
    return dhn, dict(f_conv_w=g_cw, f_conv_b=g_cb)


def _lanes_pad(a, width=LANES):
    return jnp.pad(a, [(0, 0)] * (a.ndim - 1) + [(0, width - a.shape[-1])])


def _dup_heads(w):
    rows = w.shape[0]
    w = w.reshape(rows, 2 * N_KV_HEADS, 1, HEAD_DIM)
    return jnp.broadcast_to(w, (rows, 2 * N_KV_HEADS, 2, HEAD_DIM)).reshape(rows, 4 * D_KV)


def _undup_heads(g):
    rows = g.shape[0]
    return g.reshape(rows, 2 * N_KV_HEADS, 2, HEAD_DIM).sum(axis=2).reshape(rows, 2 * D_KV)


def _local_step(x2, target, p, plan):
    seq = x2.shape[0]
    rows = seq + CHUNK
    g = {}

    h0 = jnp.concatenate([jnp.zeros((PAD_ROWS, D_MODEL), F32), p["meta_tokens"], x2], axis=0)

    w_in = plan.weight("a_w_in")
    w_dt = jnp.pad(w_in[D_MAIN:], ((0, LANES - SSM_HEADS), (0, 0)))
    dt_bias = _lanes_pad(p["a_dt_bias"])
    a128 = _lanes_pad(-jnp.exp(p["a_a_log"]))
    dskexp = jnp.repeat(p["a_d_skip"].reshape(SSM_HEADS), HEAD_DIM).reshape(1, D_INNER)

    hn0 = _rms_fwd("a_norm", h0, p["a_norm_pre"])
    zx = _mm("a_in_main", hn0, w_in, "nt", k_rows=D_MAIN, steps=plan.steps("a_in_main"))
    dtr = _mm("a_in_dt", hn0, w_dt, "nt")
    xbc = _conv4_fwd("a_conv", zx, p["a_conv_w"], p["a_conv_b"], steps=plan.steps("a_conv"))
    dt = _dt_fwd("a_dt", dtr, dt_bias)
    dt_exp, acs_exp, acs_rows = _ssd_prep("a_ssd_prep", dt, a128, steps=plan.steps("a_ssd_prep"))
    y, states = _ssd_fwd("a_ssd", xbc, dt_exp, acs_exp, acs_rows, dskexp, steps=plan.steps("a_ssd"))
    yn = _gate_fwd("a_gate", y, zx, p["a_gate_norm"], steps=plan.steps("a_gate"))
    mix = _mm("a_out", yn, plan.weight("a_w_out"), "nn", steps=plan.steps("a_out"))
    h1, (hn_f0,) = _resid_norm_fwd("a_resid", h0, mix, p["a_norm_post"], [p["f_norm_pre"][0:1]])

    pre_f0, ffn0 = _ffn_fwd("0", h1, hn_f0, p, 0, plan)
    h2, (hkv, hn2) = _resid_norm_fwd("ffn0_resid", h1, pre_f0, p["f_norm_post"][0:1], [p["kv_norm"], p["b_norm_pre"]])

    w_kv2 = _dup_heads(plan.weight("w_kv"))
    kv2 = _mm("kv_proj", hkv, w_kv2, "nn")
    q = _mm("b_q", hn2, plan.weight("b_w_q"), "nn")
    sinks = p["b_sinks"].reshape(N_Q_HEADS)
    o = _attn_fwd("b_attn", q, kv2, sinks, steps=plan.steps("b_attn"))
    attn = _mm("b_o", o, plan.weight("b_w_o"), "nn", steps=plan.steps("b_o"))
    h3, (hn_f1,) = _resid_norm_fwd("b_resid", h2, attn, p["b_norm_post"], [p["f_norm_pre"][1:2]])

    pre_f1, ffn1 = _ffn_fwd("1", h3, hn_f1, p, 1, plan)
    dh, loss_vec, dpre_f1, g_post1 = _resid_norm_loss("ffn1_resid_loss", h3, pre_f1, p["f_norm_post"][1:2], target)
    loss = loss_vec[0, 0]

    dhn_f1, g1 = _ffn_bwd("1", dpre_f1, ffn1, p, 1, plan)
    dh, g_pre1, dpre, g["b_norm_post"] = _norm_bwd_add("ffn1_norm_bwd", dh, dhn_f1, h3, p["f_norm_pre"][1:2],
                                                        then=(attn, p["b_norm_post"]))
    plan.grad("b_w_o", None, _mm("b_o_dw", o, dpre, "tn", out_dtype=BF16))
    do = _mm("b_o_dx", dpre, plan.weight("b_w_o"), "nt", steps=plan.steps("b_o_dx"))
    dq, dkc, dkp, dvc, dvp, dkm, dvm, dsink = _attn_bwd("b_attn_bwd", q, kv2, sinks, do, steps=plan.steps("b_attn_bwd"))
    g["b_sinks"] = dsink[:, :N_Q_HEADS]
    dhn2 = _mm("b_q_dx", dq, plan.weight("b_w_q"), "nt")
    plan.grad("b_w_q", None, _mm("b_q_dw", hn2, dq, "tn", out_dtype=BF16))
    dh, g["b_norm_pre"] = _norm_bwd_add("b_norm_bwd", dh, dhn2, h2, p["b_norm_pre"])
    dkv2 = _kv_grad_combine("kv_grad", dkc, dkp, dkm, dvc, dvp, dvm)
    dhkv = _mm("kv_proj_dx", dkv2, w_kv2, "nt")
    plan.grad("w_kv", None, _undup_heads(_mm("kv_proj_dw", hkv, dkv2, "tn")))
    dh, g["kv_norm"], dpre_f0, g_post0 = _norm_bwd_add("kv_norm_bwd", dh, dhkv, h2, p["kv_norm"],
                                                       then=(pre_f0, p["f_norm_post"][0:1]))

    dhn_f0, g0 = _ffn_bwd("0", dpre_f0, ffn0, p, 0, plan)
    dh, g_pre0, dpre, g["a_norm_post"] = _norm_bwd_add("ffn0_norm_bwd", dh, dhn_f0, h1, p["f_norm_pre"][0:1],
                                                        then=(mix, p["a_norm_post"]))
    g["f_norm_post"] = jnp.concatenate([g_post0, g_post1], axis=0)
    g["f_norm_pre"] = jnp.concatenate([g_pre0, g_pre1], axis=0)
    g["f_conv_w"] = jnp.stack([g0["f_conv_w"], g1["f_conv_w"]])
    g["f_conv_b"] = jnp.concatenate([g0["f_conv_b"], g1["f_conv_b"]], axis=0)
    plan.grad("a_w_out", None, _mm("a_out_dw", yn, dpre, "tn", out_dtype=BF16))
    dyn = _mm("a_out_dx", dpre, plan.weight("a_w_out"), "nt", steps=plan.steps("a_out_dx"))
    dy, dz, g["a_gate_norm"] = _gate_bwd("a_gate_bwd", dyn, y, zx, p["a_gate_norm"])
    dxs, dbm, dcm, ddt, dalog, ddsk = _ssd_bwd("a_ssd_bwd", xbc, dt_exp, acs_exp, acs_rows, dt, a128, dskexp, dy, states,
                                              steps=plan.steps("a_ssd_bwd"))
    g["a_a_log"] = dalog[:, :SSM_HEADS]
    g["a_d_skip"] = ddsk[:, :SSM_HEADS]
    ddtr, dbias = _dt_bwd("a_dt_bwd", ddt, dtr, dt_bias)
    g["a_dt_bias"] = dbias[:, :SSM_HEADS]
    dxp, gw_x, gb_x = _conv4_bwd("a_conv_bwd_x", zx, dxs, p["a_conv_w"], p["a_conv_b"], 0)
    dbp, gw_b, gb_b = _conv4_bwd("a_conv_bwd_b", zx, dbm, p["a_conv_w"], p["a_conv_b"], D_INNER)
    dcp, gw_c, gb_c = _conv4_bwd("a_conv_bwd_c", zx, dcm, p["a_conv_w"], p["a_conv_b"], D_INNER + D_BC)
    g["a_conv_w"] = jnp.concatenate([gw_x, gw_b, gw_c], axis=1)
    g["a_conv_b"] = jnp.concatenate([gb_x, gb_b, gb_c], axis=1)
    dzx = jnp.concatenate([dz, dxp, dbp, dcp], axis=1)
    g_in = _mm("a_in_main_dw", dzx, hn0, "tn", out_dtype=BF16, out_rows=D_IN_PROJ, steps=plan.steps("a_in_main_dw"))
    plan.grad("a_w_in", None, _tn_rows_into("a_in_dt_dw", ddtr, hn0, g_in, D_MAIN, SSM_HEADS))
    dhn0 = _mm("a_in_dt_dx", ddtr, w_dt, "nn", steps=plan.steps("a_in_dt_dx"))
    dhn0 = _mm("a_in_main_dx", dzx, w_in, "nn", acc=dhn0, steps=plan.steps("a_in_main_dx"))
    dh, g["a_norm_pre"] = _norm_bwd_add("a_norm_bwd", dh, dhn0, h0, p["a_norm_pre"], steps=plan.steps("a_norm_bwd"))

    g["meta_tokens"] = dh[PAD_ROWS:CHUNK]
    return loss, dh[CHUNK:], g


ANY = pl.BlockSpec(memory_space=pl.ANY)
VMEM_SPEC = pl.BlockSpec(memory_space=pltpu.VMEM)


def _allgather_small(name, shard):
    rows = shard.shape[0]

    def body(s_ref, o_ref, send_sems, recv_sems):
        x, y, c = _place()
        me = 2 * x + y
        o_ref[me] = s_ref[...]
        chips = _other_chips(x, y)
        sends = [pltpu.make_async_remote_copy(s_ref, o_ref.at[me], send_sems.at[j], recv_sems.at[j],
                                              device_id=(cx, cy, c), device_id_type=MESH)
                 for j, (cx, cy) in enumerate(chips)]
        for cp in sends:
            cp.start()
        for j, (cx, cy) in enumerate(chips):
            pltpu.make_async_remote_copy(s_ref, o_ref.at[2 * cx + cy], send_sems.at[j], recv_sems.at[j],
                                         device_id=(cx, cy, c), device_id_type=MESH).wait_recv()
        for cp in sends:
            cp.wait_send()

    return pl.pallas_call(
        body, name=name, out_shape=jax.ShapeDtypeStruct((N_CHIPS, rows, LANES), F32),
        in_specs=[VMEM_SPEC], out_specs=VMEM_SPEC,
        scratch_shapes=[pltpu.SemaphoreType.DMA((3,)), pltpu.SemaphoreType.DMA((3,))],
        compiler_params=pltpu.CompilerParams(vmem_limit_bytes=VMEM_LIMIT),
    )(shard)


def _row_block(rows, width, itemsize, align, budget=2 << 20):
    best = rows
    for cand in range(align, rows + 1, align):
        if rows % cand == 0 and cand * width * itemsize <= budget:
            best = cand
    return best


def _cast_into_slot(name, chip, w, layer=None):
    rows, width = w.shape[-2:]
    tr = _row_block(rows, width, 4, 16)
    if layer is None:
        in_spec = pl.BlockSpec((tr, width), lambda i, chip_ref: (i, 0))
    else:
        in_spec = pl.BlockSpec((None, tr, width), lambda i, chip_ref: (layer, i, 0))

    def body(chip_ref, w_ref, o_ref):
        o_ref[...] = w_ref[...].astype(BF16)

    return pl.pallas_call(
        body, name=name, out_shape=jax.ShapeDtypeStruct((N_CHIPS, rows, width), BF16),
        grid_spec=pltpu.PrefetchScalarGridSpec(
            num_scalar_prefetch=1, grid=(rows // tr,), in_specs=[in_spec],
            out_specs=pl.BlockSpec((None, tr, width), lambda i, chip_ref: (chip_ref[0], i, 0))),
        compiler_params=_cparams(("parallel",)),
    )(chip, w)


def _allreduce_small(name, vec):
    rows = -(-vec.shape[0] // (2 * SUBLANES)) * (2 * SUBLANES)
    hr = rows // 2
    padded = jnp.pad(vec, ((0, rows - vec.shape[0]), (0, 0)))

    def body(v_ref, o_ref, theirs, pair, by_chip, send_sems, recv_sems):
        x, y, c = _place()
        me = 2 * x + y
        sibling = (x, y, 1 - c)
        mine = pl.ds(pl.multiple_of(c * hr, SUBLANES), hr)
        other = pl.ds(pl.multiple_of((1 - c) * hr, SUBLANES), hr)

        swap = _remote(v_ref, theirs, send_sems, recv_sems, 0, sibling)
        swap.start()
        swap.wait()
        south = (c + jnp.zeros((1, 1), jnp.int32)) == 0
        pair[...] = jnp.where(south, v_ref[...], theirs[...]) + jnp.where(south, theirs[...], v_ref[...])

        by_chip[me] = pair[mine, :]
        sends = [_remote(by_chip.at[me], by_chip.at[me], send_sems, recv_sems, 1 + j, (cx, cy, c))
                 for j, (cx, cy) in enumerate(_other_chips(x, y))]
        for cp in sends:
            cp.start()
        for j, (cx, cy) in enumerate(_other_chips(x, y)):
            _remote(by_chip.at[me], by_chip.at[2 * cx + cy], send_sems, recv_sems, 1 + j, (cx, cy, c)).wait_recv()
        for cp in sends:
            cp.wait_send()
        total = by_chip[0]
        for s in range(1, N_CHIPS):
            total = total + by_chip[s]

        o_ref[mine, :] = total
        back = _remote(o_ref.at[mine], o_ref.at[mine], send_sems, recv_sems, 4, sibling)
        back.start()
        _remote(o_ref.at[other], o_ref.at[other], send_sems, recv_sems, 4, sibling).wait_recv()
        back.wait_send()

    out = pl.pallas_call(
        body, name=name, out_shape=jax.ShapeDtypeStruct((rows, LANES), F32),
        in_specs=[VMEM_SPEC], out_specs=VMEM_SPEC,
        scratch_shapes=[pltpu.VMEM((rows, LANES), F32), pltpu.VMEM((rows, LANES), F32),
                        pltpu.VMEM((N_CHIPS, hr, LANES), F32), pltpu.SemaphoreType.DMA((5,)),
                        pltpu.SemaphoreType.DMA((5,))],
        compiler_params=pltpu.CompilerParams(vmem_limit_bytes=VMEM_LIMIT),
    )(padded)
    return out[:vec.shape[0]]


def _rs_pair_add(name, place, grads, partner, split="rows"):
    _, half_rows, width = partner.shape
    tr = _row_block(half_rows, width, 2, 16)
    nb = half_rows // tr
    if split == "rows":
        mine = pl.BlockSpec((None, tr, width), lambda s, i, pr: (s, pr[1] * nb + i, 0))
    else:
        mine = pl.BlockSpec((None, tr, width), lambda s, i, pr: (s, i, pr[1]))

    def body(place_ref, g_ref, p_ref, o_ref):
        o_ref[...] = (g_ref[...].astype(F32) + p_ref[...].astype(F32)).astype(BF16)

    return pl.pallas_call(
        body, name=name, out_shape=jax.ShapeDtypeStruct(partner.shape, BF16),
        grid_spec=pltpu.PrefetchScalarGridSpec(
            num_scalar_prefetch=1, grid=(N_CHIPS, nb),
            in_specs=[mine, pl.BlockSpec((None, tr, width), lambda s, i, pr: (s, i, 0))],
            out_specs=pl.BlockSpec((None, tr, width), lambda s, i, pr: (s, i, 0))),
        compiler_params=_cparams(("parallel", "parallel")),
    )(place, grads, partner)


def _rs_chip_add(name, place, mine, others, split="rows"):
    _, half_rows, width = mine.shape
    tr = _row_block(half_rows, width, 4, 16, budget=1 << 20)
    nb = half_rows // tr
    if split == "rows":
        out_shape, out_spec = (2 * half_rows, width), pl.BlockSpec((tr, width), lambda i, pr: (pr[1] * nb + i, 0))
    else:
        out_shape, out_spec = (half_rows, 2 * width), pl.BlockSpec((tr, width), lambda i, pr: (i, pr[1]))

    def body(place_ref, q_ref, r_ref, o_ref):
        acc = q_ref[...].astype(F32)
        for j in range(3):
            acc = acc + r_ref[j].astype(F32)
        o_ref[...] = acc

    return pl.pallas_call(
        body, name=name, out_shape=jax.ShapeDtypeStruct(out_shape, F32),
        grid_spec=pltpu.PrefetchScalarGridSpec(
            num_scalar_prefetch=1, grid=(nb,),
            in_specs=[pl.BlockSpec((None, tr, width), lambda i, pr: (pr[0], i, 0)),
                      pl.BlockSpec((3, tr, width), lambda i, pr: (0, i, 0))],
            out_specs=out_spec),
        compiler_params=_cparams(("parallel",)),
    )(place, mine, others)


WEIGHTS = ["meta_tokens", "a_norm_pre", "a_w_in", "a_conv_w", "a_conv_b", "a_dt_bias", "a_a_log", "a_d_skip",
           "a_gate_norm", "a_w_out", "a_norm_post", "kv_norm", "w_kv", "b_norm_pre", "b_w_q", "b_sinks", "b_w_o",
           "b_norm_post", "f_norm_pre", "f_w_up", "f_conv_w", "f_conv_b", "f_w_down", "f_norm_post"]
FULL_SHAPE = {
    "meta_tokens": (16, 1024), "a_norm_pre": (1, 1024), "a_w_in": (1, 1024, 5152), "a_conv_w": (1, 4, 3072),
    "a_conv_b": (1, 3072), "a_dt_bias": (1, 32), "a_a_log": (1, 32), "a_d_skip": (1, 32), "a_gate_norm": (1, 2048),
    "a_w_out": (1, 2048, 1024), "a_norm_post": (1, 1024), "kv_norm": (1024,), "w_kv": (1024, 512),
    "b_norm_pre": (1, 1024), "b_w_q": (1, 1024, 1024), "b_sinks": (1, 16), "b_w_o": (1, 1024, 1024),
    "b_norm_post": (1, 1024), "f_norm_pre": (2, 1024), "f_w_up": (2, 1024, 5632), "f_conv_w": (2, 3, 5632),
    "f_conv_b": (2, 5632), "f_w_down": (2, 2816, 1024), "f_norm_post": (2, 1024),
}
SHARD_AXIS = {
    "meta_tokens": 1, "a_norm_pre": 1, "a_w_in": 2, "a_conv_w": 2, "a_conv_b": 1, "a_dt_bias": None, "a_a_log": None,
    "a_d_skip": None, "a_gate_norm": 1, "a_w_out": 1, "a_norm_post": 1, "kv_norm": None, "w_kv": 0, "b_norm_pre": None,
    "b_w_q": 1, "b_sinks": None, "b_w_o": 1, "b_norm_post": None, "f_norm_pre": None, "f_w_up": 2, "f_conv_w": 2,
    "f_conv_b": None, "f_w_down": 1, "f_norm_post": None,
}
BIG = ["a_w_in", "a_w_out", "w_kv", "b_w_q", "b_w_o", "f_w_up", "f_w_down"]
SMALL = [n for n in WEIGHTS if n not in BIG]
SMALL_SHARDED = [n for n in SMALL if SHARD_AXIS[n] is not None]


def _shard_shape(name):
    shape = list(FULL_SHAPE[name])
    if SHARD_AXIS[name] is not None:
        shape[SHARD_AXIS[name]] //= N_CHIPS
    return tuple(shape)


def _numel(shape):
    return int(math.prod(shape))


SUBLANES = 8


def _packed_rows(shape):
    rows = -(-_numel(shape) // LANES)
    return -(-rows // SUBLANES) * SUBLANES


def _pack(arrays):
    parts = []
    for a in arrays:
        size, rows = _numel(a.shape), _packed_rows(a.shape)
        if size % LANES == 0:
            part = jnp.pad(a.reshape(size // LANES, LANES), ((0, rows - size // LANES), (0, 0)))
        else:
            part = jnp.pad(a.reshape(-1), (0, rows * LANES - size)).reshape(rows, LANES)
        parts.append(part)
    return jnp.concatenate(parts, axis=0)


def _unpack(packed, names, shape_of):
    out, off = {}, 0
    lead = packed.shape[:-2]
    for n in names:
        shape = tuple(shape_of(n))
        size, rows = _numel(shape), _packed_rows(shape)
        part = packed[..., off:off + rows, :]
        if size % LANES == 0:
            out[n] = part[..., :size // LANES, :].reshape(lead + shape)
        else:
            out[n] = part.reshape(lead + (rows * LANES,))[..., :size].reshape(lead + shape)
        off += rows
    return out


def _split_chips(name, full):
    ax = SHARD_AXIS[name]
    shape = full.shape
    cut = shape[:ax] + (N_CHIPS, shape[ax] // N_CHIPS) + shape[ax + 1:]
    return jnp.moveaxis(full.reshape(cut), ax, 0)


def _join_chips(name, stacked):
    ax = SHARD_AXIS[name]
    moved = jnp.moveaxis(stacked, 0, ax)
    shape = moved.shape
    return moved.reshape(shape[:ax] + (shape[ax] * shape[ax + 1],) + shape[ax + 2:])


def _as2d(a):
    return a.reshape(-1, a.shape[-1])


BUFFERS = [("a_w_in", "a_w_in", None), ("a_w_out", "a_w_out", None), ("w_kv", "w_kv", None),
           ("b_w_q", "b_w_q", None), ("b_w_o", "b_w_o", None), ("f_w_up0", "f_w_up", 0), ("f_w_up1", "f_w_up", 1),
           ("f_w_down0", "f_w_down", 0), ("f_w_down1", "f_w_down", 1)]


TRANSPOSED = ("a_w_in",)
SPLIT = {"a_w_in": "cols"}


def _local_shard(arrays, weight, layer):
    if weight in TRANSPOSED:
        return arrays[weight][0].T
    return _as2d(arrays[weight]) if layer is None else arrays[weight]


def _weight_from_gathered(weight, buf):
    if weight == "f_w_up":
        return buf
    return buf.reshape(N_CHIPS * buf.shape[1], buf.shape[2])


def _gathered_from_grad(weight, g):
    if weight == "f_w_up":
        return g
    return g.reshape(N_CHIPS, g.shape[0] // N_CHIPS, g.shape[1]).astype(BF16)


GATHER_SCHEDULE = {
    "a_in_main": [("ici", ["a_w_out"])],
    "a_conv": [("d2d", ["a_w_out"]), ("ici", ["f_w_down0"])],
    "a_ssd_prep": [("d2d", ["f_w_down0"]), ("ici_near", ["f_w_up0"])],
    "a_ssd": [("ici_far", ["f_w_up0"])],
    "a_gate": [("d2d", ["f_w_up0"]), ("ici", ["w_kv", "b_w_q", "b_w_o"])],
    "ffn0_up": [("d2d", ["w_kv", "b_w_q", "b_w_o"]), ("ici", ["f_w_down1"])],
    "ffn0_conv": [("d2d", ["f_w_down1"]), ("ici_near", ["f_w_up1"])],
    "b_attn": [("ici_far", ["f_w_up1"])],
    "b_o": [("d2d", ["f_w_up1"])],
}
REDUCE_SCHEDULE = {
    "b_attn_bwd": [("all", ["f_w_down1", "f_w_up1", "b_w_o"])],
    "ffn0_conv_bwd": [("all", ["b_w_q", "w_kv", "f_w_down0"])],
    "a_ssd_bwd": [("all", ["f_w_up0", "a_w_out"])],
    "a_in_main_dx": [("near", ["a_w_in"])],
    "a_norm_bwd": [("far", ["a_w_in"])],
}
REDUCE_LAST = ("a_w_in",)
ICI_PEERS = {"ici": ALL_PEERS, "ici_near": NEAR_PEERS, "ici_far": FAR_PEERS,
             "all": ALL_PEERS, "near": NEAR_PEERS, "far": FAR_PEERS}
PAIR_SCHEDULE = {
    "b_o_dx": ["f_w_down1", "f_w_up1", "b_w_o"],
    "ffn0_down_dx": ["b_w_q", "w_kv", "f_w_down0"],
    "a_out_dx": ["f_w_up0", "a_w_out"],
    "a_in_dt_dx": ["a_w_in"],
}
SWAP_SCHEDULE = {"a_in_main_dw": ["f_w_down1", "f_w_up1", "b_w_o", "b_w_q", "w_kv", "f_w_down0", "f_w_up0", "a_w_out"]}


def _buffer_of(weight, layer):
    return weight if layer is None else f"{weight}{layer}"


class _Pipeline:
    def __init__(self, place, slots):
        self.place = place
        self.slots = dict(slots)
        self.running = []
        self.grads = {}
        self.theirs = {}
        self.partials = {}
        self.peers = {}
        self.reduced = {}

    def _collect(self):
        for step, buffers, table in self.running:
            table.update(zip(buffers, step.results))
        self.running = []

    @staticmethod
    def _splits(buffers):
        return [SPLIT.get(b, "rows") for b in buffers]

    def gather_now(self, name, buffers):
        step = _step_gather_full([self.slots[b] for b in buffers], self._splits(buffers))
        _run_steps(name, [step])
        self.slots.update(zip(buffers, step.results))

    def weight(self, name, layer=None):
        self._collect()
        return _weight_from_gathered(name, self.slots[_buffer_of(name, layer)])

    def grad(self, name, layer, g):
        self.grads[_buffer_of(name, layer)] = _gathered_from_grad(name, g)

    def steps(self, kernel):
        self._collect()
        steps = []
        for phase, buffers in GATHER_SCHEDULE.get(kernel, []):
            bufs, splits = [self.slots[b] for b in buffers], self._splits(buffers)
            step = (_step_gather_d2d(bufs, splits) if phase == "d2d"
                    else _step_gather_ici(bufs, splits, ICI_PEERS[phase]))
            self.running.append((step, buffers, self.slots))
            steps.append(step)
        buffers = PAIR_SCHEDULE.get(kernel)
        if buffers:
            step = _step_pair_exchange([self.grads[b] for b in buffers], self._splits(buffers))
            self.running.append((step, buffers, self.theirs))
            steps.append(step)
        for part, buffers in REDUCE_SCHEDULE.get(kernel, []):
            for b in buffers:
                if b not in self.partials:
                    self.partials[b] = _rs_pair_add("reduce_pair_add_" + b, self.place, self.grads[b], self.theirs[b],
                                                    SPLIT.get(b, "rows"))
            started = [self.peers[b] for b in buffers] if all(b in self.peers for b in buffers) else None
            step = _step_chip_exchange([self.partials[b] for b in buffers], ICI_PEERS[part], into=started)
            self.running.append((step, buffers, self.peers))
            steps.append(step)
        buffers = SWAP_SCHEDULE.get(kernel)
        if buffers:
            step = self._swap_step(buffers)
            self.running.append((step, buffers, self.reduced))
            steps.append(step)
        return steps

    def _swap_step(self, buffers):
        halves = [_rs_chip_add("reduce_chip_add_" + b, self.place, self.partials[b], self.peers[b], SPLIT.get(b, "rows"))
                  for b in buffers]
        return _step_pair_gather(halves, self._splits(buffers))

    def shard(self, buffer):
        self._collect()
        return self.reduced[buffer]

    def finish(self):
        self._collect()
        rest = [b for b, _, _ in BUFFERS if b not in self.reduced]
        step = self._swap_step(rest)
        _run_steps("reduce_pair_gather", [step])
        self.reduced.update(zip(rest, step.results))


def kernel(x, meta_tokens, a_norm_pre, a_w_in, a_conv_w, a_conv_b, a_dt_bias, a_a_log, a_d_skip, a_gate_norm, a_w_out, a_norm_post, kv_norm, w_kv, b_norm_pre, b_w_q, b_sinks, b_w_o, b_norm_post, f_norm_pre, f_w_up, f_conv_w, f_conv_b, f_w_down, f_norm_post, loss_target, m_meta_tokens, m_a_norm_pre, m_a_w_in, m_a_conv_w, m_a_conv_b, m_a_dt_bias, m_a_a_log, m_a_d_skip, m_a_gate_norm, m_a_w_out, m_a_norm_post, m_kv_norm, m_w_kv, m_b_norm_pre, m_b_w_q, m_b_sinks, m_b_w_o, m_b_norm_post, m_f_norm_pre, m_f_w_up, m_f_conv_w, m_f_conv_b, m_f_w_down, m_f_norm_post, v_meta_tokens, v_a_norm_pre, v_a_w_in, v_a_conv_w, v_a_conv_b, v_a_dt_bias, v_a_a_log, v_a_d_skip, v_a_gate_norm, v_a_w_out, v_a_norm_post, v_kv_norm, v_w_kv, v_b_norm_pre, v_b_w_q, v_b_sinks, v_b_w_o, v_b_norm_post, v_f_norm_pre, v_f_w_up, v_f_conv_w, v_f_conv_b, v_f_w_down, v_f_norm_post):
    given = dict(locals())
    w = {n: given[n] for n in WEIGHTS}
    mom = {n: given["m_" + n] for n in WEIGHTS}
    var = {n: given["v_" + n] for n in WEIGHTS}
    chip = 2 * lax.axis_index("x") + lax.axis_index("y")
    core = lax.axis_index("c")
    place = jnp.stack([chip, core]).astype(jnp.int32)

    small_all = _allgather_small("gather_small", _pack([w[n] for n in SMALL_SHARDED]))
    small_parts = _unpack(small_all, SMALL_SHARDED, _shard_shape)
    slots = {b: _cast_into_slot("cast_" + b, place, _local_shard(w, wn, layer), layer) for b, wn, layer in BUFFERS}
    pipeline = _Pipeline(place, slots)
    pipeline.gather_now("gather_first", ["a_w_in"])
    p = {}
    for n in SMALL:
        p[n] = _join_chips(n, small_parts[n]) if n in SMALL_SHARDED else w[n]
    p["a_conv_w"] = p["a_conv_w"][0]
    p["kv_norm"] = p["kv_norm"].reshape(1, D_MODEL)

    loss_local, grad_x, g = _local_step(x[0], loss_target[0], p, pipeline)

    small_sum = _allreduce_small("reduce_small", _pack([g[n].reshape(FULL_SHAPE[n]) for n in SMALL]
                                                       + [loss_local.reshape(1, 1)]))
    small_red = _unpack(small_sum, SMALL + ["loss"], lambda n: (1, 1) if n == "loss" else FULL_SHAPE[n])
    loss = small_red["loss"][0, 0]
    grads = {}
    for n in SMALL:
        if SHARD_AXIS[n] is None:
            grads[n] = small_red[n]
        else:
            grads[n] = lax.dynamic_index_in_dim(_split_chips(n, small_red[n]), chip, 0, keepdims=False)

    delta, new_m, new_v = {}, {}, {}
    for n in sorted(BIG, key=lambda name: name in REDUCE_LAST):
        shape = _shard_shape(n)
        if n in REDUCE_LAST:
            pipeline.finish()
        if n in TRANSPOSED:
            g2d = pipeline.shard(n)
            w2d, m2d, v2d = (arrays[n][0].T for arrays in (w, mom, var))
            back = lambda a: a.T.reshape(shape)
        else:
            g2d = (jnp.concatenate([pipeline.shard(n + "0"), pipeline.shard(n + "1")], axis=0)
                   if n in ("f_w_up", "f_w_down") else pipeline.shard(n))
            w2d, m2d, v2d = (_as2d(arrays[n]) for arrays in (w, mom, var))
            back = lambda a: a.reshape(shape)
        d, m2, v2 = _adamw("adamw_" + n, w2d, g2d, m2d, v2d, steps=pipeline.steps("adamw_" + n))
        grads[n], delta[n], new_m[n], new_v[n] = back(g2d), back(d), back(m2), back(v2)
    packed = [_pack([src[n].reshape(_shard_shape(n)) for n in SMALL]) for src in (w, grads, mom, var)]
    outs = _adamw("adamw_small", *packed)
    for dst, flat in zip((delta, new_m, new_v), outs):
        dst.update(_unpack(flat, SMALL, _shard_shape))

    return (loss, grad_x[None], *[grads[n].reshape(_shard_shape(n)) for n in WEIGHTS],
            *[delta[n] for n in WEIGHTS], *[new_m[n] for n in WEIGHTS], *[new_v[n] for n in WEIGHTS])
```

```python
import functools
import math

import jax
import jax.numpy as jnp
from jax import lax
from jax.experimental import pallas as pl
from jax.experimental.pallas import tpu as pltpu

F32, BF16 = jnp.float32, jnp.bfloat16
MESH = pl.DeviceIdType.MESH

D_MODEL = 1024
N_META = 16
CHUNK = 128
PAD_ROWS = CHUNK - N_META
D_INNER = 2048
D_STATE = 128
N_GROUPS = 4
HEADS_PER_GROUP = 8
SSM_HEADS = 32
HEAD_DIM = 64
D_BC = N_GROUPS * D_STATE
D_XBC = D_INNER + 2 * D_BC
D_MAIN = D_INNER + D_XBC
D_IN_PROJ = D_MAIN + SSM_HEADS
GROUP_W = HEADS_PER_GROUP * HEAD_DIM
SSM_CONV = 4
D_FF = 2816
FFN_CONV = 3
N_Q_HEADS = 16
N_KV_HEADS = 4
D_KV = 256
ATTN_SCALE = 1.0 / math.sqrt(HEAD_DIM)
RMS_EPS = 1e-6
NEG_INF = -1e30
LANES = 128
VMEM_LIMIT = 48 * 1024 * 1024

ADAM_LR, ADAM_B1, ADAM_B2, ADAM_EPS, ADAM_WD, ADAM_STEP = 0.001, 0.9, 0.999, 1e-08, 0.01, 10

N_CHIPS = 4


def _cparams(sem=None):
    return pltpu.CompilerParams(dimension_semantics=sem, vmem_limit_bytes=VMEM_LIMIT)


def _tile(n, cands=(512, 256, 128)):
    for t in cands:
        if n % t == 0:
            return t
    return n


def _row_tile(rows, width):
    for t in (544, 272):
        if rows % t == 0 and t * width * 4 <= (3 << 20):
            return t
    return 128


def _rows_mask(i, tm):
    rows = i * tm + lax.broadcasted_iota(jnp.int32, (tm, 1), 0)
    return rows >= PAD_ROWS


def _dot(a, b):
    return jnp.dot(a, b, preferred_element_type=F32)


def _dot_nt(a, b):
    return lax.dot_general(a, b, (((1,), (1,)), ((), ())), preferred_element_type=F32)


def _dot_tn(a, b):
    return lax.dot_general(a, b, (((0,), (0,)), ((), ())), preferred_element_type=F32)


def _sigmoid(x):
    return 1.0 / (1.0 + jnp.exp(-x))


def _place():
    return lax.axis_index("x"), lax.axis_index("y"), lax.axis_index("c")


def _other_chips(x, y):
    return [(1 - x, y), (x, 1 - y), (1 - x, 1 - y)]


class _Step:
    def __init__(self, ins, outs, aliases, n_sems, start, finish):
        self.ins, self.outs, self.aliases, self.n_sems = list(ins), list(outs), dict(aliases), n_sems
        self.start, self.finish = start, finish
        self.results = None


def _like(a):
    return jax.ShapeDtypeStruct(a.shape, a.dtype)


def _remote(src, dst, send_sems, recv_sems, k, device):
    return pltpu.make_async_remote_copy(src, dst, send_sems.at[k], recv_sems.at[k], device_id=device, device_id_type=MESH)


def _half(ref, split, which, lead=()):
    if split == "rows":
        hr = ref.shape[-2] // 2
        return ref.at[lead + (pl.ds(which * hr, hr),)]
    hc = ref.shape[-1] // 2
    return ref.at[lead + (slice(None), pl.ds(which * hc, hc))]


def _splits(bufs, splits):
    return list(splits) if splits is not None else ["rows"] * len(bufs)


ALL_PEERS = (0, 1, 2)
NEAR_PEERS = (0, 1)
FAR_PEERS = (2,)


def _step_gather_ici(bufs, splits=None, peers=ALL_PEERS):
    splits = _splits(bufs, splits)

    def copies(outs, send_sems, recv_sems, received):
        x, y, c = _place()
        me = 2 * x + y
        for k, o in enumerate(outs):
            for j, (cx, cy) in enumerate(_other_chips(x, y)):
                if j in peers:
                    part = _half(o, splits[k], c, (2 * cx + cy if received else me,))
                    yield _remote(part, part, send_sems, recv_sems, 3 * k + j, (cx, cy, c))

    def start(ins, outs, send_sems, recv_sems):
        for cp in copies(outs, send_sems, recv_sems, False):
            cp.start()

    def finish(ins, outs, send_sems, recv_sems):
        for cp in copies(outs, send_sems, recv_sems, True):
            cp.wait_recv()
        for cp in copies(outs, send_sems, recv_sems, False):
            cp.wait_send()

    return _Step(bufs, [_like(b) for b in bufs], {k: k for k in range(len(bufs))}, 3 * len(bufs), start, finish)


def _step_gather_d2d(bufs, splits=None):
    splits = _splits(bufs, splits)

    def copies(outs, send_sems, recv_sems, received):
        x, y, c = _place()
        for k, o in enumerate(outs):
            for j, (cx, cy) in enumerate(_other_chips(x, y)):
                part = _half(o, splits[k], 1 - c if received else c, (2 * cx + cy,))
                yield _remote(part, part, send_sems, recv_sems, 3 * k + j, (x, y, 1 - c))

    def start(ins, outs, send_sems, recv_sems):
        for cp in copies(outs, send_sems, recv_sems, False):
            cp.start()

    def finish(ins, outs, send_sems, recv_sems):
        for cp in copies(outs, send_sems, recv_sems, True):
            cp.wait_recv()
        for cp in copies(outs, send_sems, recv_sems, False):
            cp.wait_send()

    return _Step(bufs, [_like(b) for b in bufs], {k: k for k in range(len(bufs))}, 3 * len(bufs), start, finish)


def _step_gather_full(bufs, splits=None):
    n = len(bufs)
    splits = _splits(bufs, splits)

    def ici(outs, send_sems, recv_sems, received):
        x, y, c = _place()
        me = 2 * x + y
        for k, o in enumerate(outs):
            for j, (cx, cy) in enumerate(_other_chips(x, y)):
                part = _half(o, splits[k], c, (2 * cx + cy if received else me,))
                yield _remote(part, part, send_sems, recv_sems, 3 * k + j, (cx, cy, c))

    def d2d(outs, send_sems, recv_sems, received):
        x, y, c = _place()
        for k, o in enumerate(outs):
            for j, (cx, cy) in enumerate(_other_chips(x, y)):
                part = _half(o, splits[k], 1 - c if received else c, (2 * cx + cy,))
                yield _remote(part, part, send_sems, recv_sems, 3 * n + 3 * k + j, (x, y, 1 - c))

    def start(ins, outs, send_sems, recv_sems):
        for cp in ici(outs, send_sems, recv_sems, False):
            cp.start()

    def finish(ins, outs, send_sems, recv_sems):
        for arrived, onward in zip(ici(outs, send_sems, recv_sems, True), d2d(outs, send_sems, recv_sems, False)):
            arrived.wait_recv()
            onward.start()
        for cp in d2d(outs, send_sems, recv_sems, True):
            cp.wait_recv()
        for cp in ici(outs, send_sems, recv_sems, False):
            cp.wait_send()
        for cp in d2d(outs, send_sems, recv_sems, False):
            cp.wait_send()

    return _Step(bufs, [_like(b) for b in bufs], {k: k for k in range(n)}, 6 * n, start, finish)


def _half_shape(shape, split):
    return shape[:-2] + ((shape[-2] // 2, shape[-1]) if split == "rows" else (shape[-2], shape[-1] // 2))


def _step_pair_exchange(grads, splits=None):
    splits = _splits(grads, splits)

    def copies(ins, outs, send_sems, recv_sems):
        x, y, c = _place()
        for k, (g, o) in enumerate(zip(ins, outs)):
            yield _remote(_half(g, splits[k], 1 - c, (slice(None),)), o, send_sems, recv_sems, k, (x, y, 1 - c))

    def start(ins, outs, send_sems, recv_sems):
        for cp in copies(ins, outs, send_sems, recv_sems):
            cp.start()

    def finish(ins, outs, send_sems, recv_sems):
        for cp in copies(ins, outs, send_sems, recv_sems):
            cp.wait()

    outs = [jax.ShapeDtypeStruct(_half_shape(g.shape, s), g.dtype) for g, s in zip(grads, splits)]
    return _Step(grads, outs, {}, len(grads), start, finish)


def _step_chip_exchange(partials, peers=ALL_PEERS, into=None):
    n = len(partials)

    def copies(ins, outs, send_sems, recv_sems):
        x, y, c = _place()
        for k, (q, o) in enumerate(zip(ins[:n], outs)):
            for j, (cx, cy) in enumerate(_other_chips(x, y)):
                if j in peers:
                    yield _remote(q.at[2 * cx + cy], o.at[j], send_sems, recv_sems, 3 * k + j, (cx, cy, c))

    def start(ins, outs, send_sems, recv_sems):
        for cp in copies(ins, outs, send_sems, recv_sems):
            cp.start()

    def finish(ins, outs, send_sems, recv_sems):
        for cp in copies(ins, outs, send_sems, recv_sems):
            cp.wait()

    outs = [jax.ShapeDtypeStruct((3,) + q.shape[1:], q.dtype) for q in partials]
    if into is None:
        return _Step(partials, outs, {}, 3 * n, start, finish)
    return _Step(list(partials) + list(into), outs, {n + k: k for k in range(n)}, 3 * n, start, finish)


def _step_pair_gather(shards, splits=None):
    splits = _splits(shards, splits)

    def copies(outs, send_sems, recv_sems, received):
        x, y, c = _place()
        for k, o in enumerate(outs):
            part = _half(o, splits[k], 1 - c if received else c)
            yield _remote(part, part, send_sems, recv_sems, k, (x, y, 1 - c))

    def start(ins, outs, send_sems, recv_sems):
        for cp in copies(outs, send_sems, recv_sems, False):
            cp.start()

    def finish(ins, outs, send_sems, recv_sems):
        for cp in copies(outs, send_sems, recv_sems, True):
            cp.wait_recv()
        for cp in copies(outs, send_sems, recv_sems, False):
            cp.wait_send()

    return _Step(shards, [_like(s) for s in shards], {k: k for k in range(len(shards))}, len(shards), start, finish)


def _call(body, *, name, out_shape, grid, in_specs, out_specs, operands, scratch_shapes=(), semantics=None, steps=()):
    single = not isinstance(out_shape, (tuple, list))
    out_shapes = [out_shape] if single else list(out_shape)
    out_spec_list = [out_specs] if single else list(out_specs)
    steps = list(steps)
    if not steps:
        res = pl.pallas_call(body, name=name, out_shape=out_shapes, grid=grid, in_specs=list(in_specs),
                             out_specs=out_spec_list, scratch_shapes=list(scratch_shapes),
                             compiler_params=_cparams(semantics))(*operands)
        return res[0] if single else res
    n_in, n_out, n_scr = len(operands), len(out_shapes), len(scratch_shapes)
    x_in = [a for s in steps for a in s.ins]
    x_out = [o for s in steps for o in s.outs]
    aliases, in_off, out_off = {}, 0, 0
    for s in steps:
        for i, o in s.aliases.items():
            aliases[n_in + in_off + i] = n_out + out_off + o
        in_off += len(s.ins)
        out_off += len(s.outs)
    sems = []
    for s in steps:
        sems += [pltpu.SemaphoreType.DMA((s.n_sems,)), pltpu.SemaphoreType.DMA((s.n_sems,))]
    any_spec = pl.BlockSpec(memory_space=pl.ANY)

    def carried(*refs):
        pos = 0
        ins = refs[pos:pos + n_in]; pos += n_in
        xi = refs[pos:pos + len(x_in)]; pos += len(x_in)
        outs = refs[pos:pos + n_out]; pos += n_out
        xo = refs[pos:pos + len(x_out)]; pos += len(x_out)
        scr = refs[pos:pos + n_scr]; pos += n_scr
        sem_refs = refs[pos:]

        def each(action):
            i0 = o0 = 0
            for k, s in enumerate(steps):
                getattr(s, action)(xi[i0:i0 + len(s.ins)], xo[o0:o0 + len(s.outs)], sem_refs[2 * k], sem_refs[2 * k + 1])
                i0 += len(s.ins)
                o0 += len(s.outs)

        if grid:
            first = functools.reduce(jnp.logical_and, [pl.program_id(d) == 0 for d in range(len(grid))])
            last = functools.reduce(jnp.logical_and, [pl.program_id(d) == grid[d] - 1 for d in range(len(grid))])
            pl.when(first)(lambda: each("start"))
            body(*ins, *outs, *scr)
            pl.when(last)(lambda: each("finish"))
        else:
            each("start")
            body(*ins, *outs, *scr)
            each("finish")

    res = pl.pallas_call(
        carried, name=name, out_shape=out_shapes + x_out, grid=grid,
        in_specs=list(in_specs) + [any_spec] * len(x_in), out_specs=out_spec_list + [any_spec] * len(x_out),
        scratch_shapes=list(scratch_shapes) + sems, input_output_aliases=aliases,
        compiler_params=_cparams(None if semantics is None else ("arbitrary",) * len(grid)),
    )(*operands, *x_in)
    o0 = n_out
    for s in steps:
        s.results = list(res[o0:o0 + len(s.outs)])
        o0 += len(s.outs)
    return res[0] if single else tuple(res[:n_out])


def _run_steps(name, steps):
    _call(lambda: None, name=name, out_shape=[], grid=(), in_specs=[], out_specs=[], operands=[], steps=steps)
    return [s.results for s in steps]


def _mm(name, a, b, mode, out_dtype=F32, acc=None, b_colblock=0, k_rows=None, out_rows=None, steps=()):
    resident_bytes = 8 << 20
    if mode == "nn":
        m, k = a.shape
        n = b.shape[1]
        tm = m
        while tm * k * 2 > resident_bytes and tm % 32 == 0:
            tm //= 2
        tn = _tile(n)
        grid = (m // tm, n // tn)
        in_specs = [pl.BlockSpec((tm, k), lambda i, j: (i, 0)), pl.BlockSpec((k, tn), lambda i, j: (0, j))]
        out_shape, out_block = (m, n), (tm, tn)
    elif mode == "nt":
        m, n = a.shape
        k = k_rows or b.shape[0]
        tm = m
        while tm * n * 2 > resident_bytes and tm % 32 == 0:
            tm //= 2
        tk = _tile(k)
        grid = (m // tm, k // tk)
        in_specs = [pl.BlockSpec((tm, n), lambda i, j: (i, 0)), pl.BlockSpec((tk, n), lambda i, j: (j, b_colblock))]
        out_shape, out_block = (m, k), (tm, tk)
    else:
        m, k = a.shape
        n = b.shape[1]
        tk, tn = _tile(k), (n if m * n * 2 <= resident_bytes else _tile(n))
        grid = (k // tk, n // tn)
        in_specs = [pl.BlockSpec((m, tk), lambda i, j: (0, i)), pl.BlockSpec((m, tn), lambda i, j: (0, j))]
        out_shape, out_block = (out_rows or k, n), (tk, tn)
    out_spec = pl.BlockSpec(out_block, lambda i, j: (i, j))
    has_acc = acc is not None

    def body(*refs):
        a_ref, b_ref = refs[0], refs[1]
        o_ref = refs[-1]
        av, bv = a_ref[...], b_ref[...]
        if mode == "nn":
            r = _dot(av, bv)
        elif mode == "nt":
            r = _dot_nt(av, bv)
        else:
            r = _dot_tn(av, bv)
        if has_acc:
            r = r + refs[2][...]
        o_ref[...] = r.astype(o_ref.dtype)

    operands = [a, b]
    if has_acc:
        in_specs = in_specs + [out_spec]
        operands.append(acc)
    return _call(body, name=name, out_shape=jax.ShapeDtypeStruct(out_shape, out_dtype), grid=grid, in_specs=in_specs,
                 out_specs=out_spec, operands=operands, semantics=("parallel", "parallel"), steps=steps)


def _tn_rows_into(name, a, b, into, row0, nrows):
    m, k = a.shape
    n = b.shape[1]

    def body(a_ref, b_ref, into_ref, o_ref):
        o_ref[...] = _dot_tn(a_ref[...], b_ref[...])[0:nrows].astype(o_ref.dtype)

    return pl.pallas_call(
        body, name=name, out_shape=jax.ShapeDtypeStruct(into.shape, into.dtype), grid=(1,),
        in_specs=[pl.BlockSpec((m, k), lambda i: (0, 0)), pl.BlockSpec((m, n), lambda i: (0, 0)),
                  pl.BlockSpec(memory_space=pl.ANY)],
        out_specs=pl.BlockSpec((nrows, n), lambda i: (row0 // nrows, 0)),
        input_output_aliases={2: 0}, compiler_params=_cparams(("arbitrary",)),
    )(a, b, into)


def _fit_rows(m, row_bytes, budget=8 << 20):
    tm = m
    while tm * row_bytes > budget and tm % 32 == 0:
        tm //= 2
    return tm


def _mm_nt_bychip(name, a, bc, chip0, acc=None):
    m = a.shape[0]
    _, k, n = bc.shape
    nch = a.shape[1] // n
    tm, tk = _fit_rows(m, n * 2), _tile(k)
    has_acc = acc is not None

    def body(*refs):
        a_ref, b_ref, o_ref = refs[0], refs[1], refs[-1]

        @pl.when(pl.program_id(2) == 0)
        def _():
            o_ref[...] = refs[2][...] if has_acc else jnp.zeros_like(o_ref)

        o_ref[...] += _dot_nt(a_ref[...], b_ref[...])

    out_spec = pl.BlockSpec((tm, tk), lambda i, j, c: (i, j))
    in_specs = [pl.BlockSpec((tm, n), lambda i, j, c: (i, c)),
                pl.BlockSpec((None, tk, n), lambda i, j, c: (chip0 + c, j, 0))]
    operands = [a, bc]
    if has_acc:
        in_specs.append(out_spec)
        operands.append(acc)
    return pl.pallas_call(
        body, name=name, out_shape=jax.ShapeDtypeStruct((m, k), F32), grid=(m // tm, k // tk, nch),
        in_specs=in_specs, out_specs=out_spec, compiler_params=_cparams(("parallel", "parallel", "arbitrary")),
    )(*operands)


def _mm_tn_bychip(name, a, dy, n, chip0, into=None):
    m, k = a.shape
    nch = dy.shape[1] // n
    tk = _tile(k)

    def body(*refs):
        a_ref, d_ref, o_ref = refs[0], refs[1], refs[-1]
        o_ref[...] = _dot_tn(a_ref[...], d_ref[...]).astype(BF16)

    in_specs = [pl.BlockSpec((m, tk), lambda i, c: (0, i)), pl.BlockSpec((m, n), lambda i, c: (0, c))]
    operands = [a, dy]
    aliases = {}
    if into is not None:
        in_specs.append(pl.BlockSpec(memory_space=pl.ANY))
        operands.append(into)
        aliases = {2: 0}
    return pl.pallas_call(
        body, name=name, out_shape=jax.ShapeDtypeStruct((N_CHIPS, k, n), BF16), grid=(k // tk, nch),
        in_specs=in_specs, out_specs=pl.BlockSpec((None, tk, n), lambda i, c: (chip0 + c, i, 0)),
        input_output_aliases=aliases, compiler_params=_cparams(("parallel", "parallel")),
    )(*operands)


def _rms_fwd(name, h, w):
    rows, width = h.shape
    tm = _row_tile(rows, width)

    def body(h_ref, w_ref, o_ref):
        x = h_ref[...]
        r = lax.rsqrt(jnp.mean(x * x, axis=-1, keepdims=True) + RMS_EPS)
        o_ref[...] = (x * r * w_ref[...]).astype(BF16)

    return pl.pallas_call(
        body, name=name, out_shape=jax.ShapeDtypeStruct((rows, width), BF16), grid=(rows // tm,),
        in_specs=[pl.BlockSpec((tm, width), lambda i: (i, 0)), pl.BlockSpec((1, width), lambda i: (0, 0))],
        out_specs=pl.BlockSpec((tm, width), lambda i: (i, 0)), compiler_params=_cparams(("parallel",)),
    )(h, w)


def _resid_norm_fwd(name, h, pre, w, next_norms=()):
    rows, width = h.shape
    tm = _row_tile(rows, width)
    n_next = len(next_norms)

    def body(*refs):
        h_ref, p_ref, w_ref = refs[:3]
        v_refs = refs[3:3 + n_next]
        o_ref = refs[3 + n_next]
        n_refs = refs[4 + n_next:]
        p = p_ref[...]
        r = lax.rsqrt(jnp.mean(p * p, axis=-1, keepdims=True) + RMS_EPS)
        x = h_ref[...] + jnp.where(_rows_mask(pl.program_id(0), tm), p * r * w_ref[...], 0.0)
        o_ref[...] = x
        if n_next:
            rx = lax.rsqrt(jnp.mean(x * x, axis=-1, keepdims=True) + RMS_EPS)
            for v_ref, n_ref in zip(v_refs, n_refs):
                n_ref[...] = (x * rx * v_ref[...]).astype(BF16)

    row_spec = pl.BlockSpec((tm, width), lambda i: (i, 0))
    vec_spec = pl.BlockSpec((1, width), lambda i: (0, 0))
    outs = pl.pallas_call(
        body, name=name,
        out_shape=[jax.ShapeDtypeStruct((rows, width), F32)] + [jax.ShapeDtypeStruct((rows, width), BF16)] * n_next,
        grid=(rows // tm,), in_specs=[row_spec, row_spec, vec_spec] + [vec_spec] * n_next,
        out_specs=[row_spec] * (1 + n_next), compiler_params=_cparams(("parallel",)),
    )(h, pre, w, *next_norms)
    return outs[0], list(outs[1:])


def _resid_norm_loss(name, h, pre, w, target):
    rows, width = h.shape

    def body(h_ref, p_ref, w_ref, t_ref, dh_ref, loss_ref, dp_ref, dw_ref):
        i = pl.program_id(0)
        p = p_ref[...]
        r = lax.rsqrt(jnp.mean(p * p, axis=-1, keepdims=True) + RMS_EPS)
        x = h_ref[...] + p * r * w_ref[...]
        real = (i + jnp.zeros((CHUNK, 1), jnp.int32)) >= 1
        diff = jnp.where(real, x - t_ref[...], 0.0)
        dh = diff * (1.0 / D_MODEL)
        dh_ref[...] = dh
        dp, dw_rows = _rms_bwd(dh, p, w_ref[...])
        dp_ref[...] = dp.astype(BF16)

        @pl.when(i == 0)
        def _():
            loss_ref[...] = jnp.zeros_like(loss_ref)
            dw_ref[...] = jnp.zeros_like(dw_ref)

        loss_ref[...] += jnp.sum(diff * diff) * (0.5 / D_MODEL)
        dw_ref[...] += jnp.sum(dw_rows, axis=0, keepdims=True)

    blk = pl.BlockSpec((CHUNK, width), lambda i: (i, 0))
    vec_spec = pl.BlockSpec((1, width), lambda i: (0, 0))
    return pl.pallas_call(
        body, name=name,
        out_shape=(jax.ShapeDtypeStruct((rows, width), F32), jax.ShapeDtypeStruct((1, LANES), F32),
                   jax.ShapeDtypeStruct((rows, width), BF16), jax.ShapeDtypeStruct((1, width), F32)),
        grid=(rows // CHUNK,),
        in_specs=[blk, blk, vec_spec, pl.BlockSpec((CHUNK, width), lambda i: (jnp.maximum(i - 1, 0), 0))],
        out_specs=(blk, pl.BlockSpec((1, LANES), lambda i: (0, 0)), blk, vec_spec),
        compiler_params=_cparams(("arbitrary",)),
    )(h, pre, w, target)


def _rms_bwd(dy, x, w):
    r = lax.rsqrt(jnp.mean(x * x, axis=-1, keepdims=True) + RMS_EPS)
    xhat = x * r
    dxhat = dy * w
    return r * (dxhat - xhat * jnp.mean(dxhat * xhat, axis=-1, keepdims=True)), dy * xhat


def _norm_bwd_add(name, dh, dhn, h, w, then=None, steps=()):
    rows, width = dh.shape
    tm = _row_tile(rows, width)
    fused = then is not None

    def body(*refs):
        dh_ref, dhn_ref, h_ref, w_ref = refs[:4]
        o_ref, dw_ref = refs[6:8] if fused else refs[4:6]
        i = pl.program_id(0)
        valid = _rows_mask(i, tm)
        dx, dw_rows = _rms_bwd(dhn_ref[...], h_ref[...], w_ref[...])
        dh_new = dh_ref[...] + jnp.where(valid, dx, 0.0)
        o_ref[...] = dh_new

        @pl.when(i == 0)
        def _():
            dw_ref[...] = jnp.zeros_like(dw_ref)

        dw_ref[...] += jnp.sum(dw_rows, axis=0, keepdims=True)
        if fused:
            p_ref, wp_ref, dp_ref, dwp_ref = refs[4], refs[5], refs[8], refs[9]
            dp, dwp_rows = _rms_bwd(jnp.where(valid, dh_new, 0.0), p_ref[...], wp_ref[...])
            dp_ref[...] = dp.astype(BF16)

            @pl.when(i == 0)
            def _():
                dwp_ref[...] = jnp.zeros_like(dwp_ref)

            dwp_ref[...] += jnp.sum(dwp_rows, axis=0, keepdims=True)

    row_spec = pl.BlockSpec((tm, width), lambda i: (i, 0))
    vec_spec = pl.BlockSpec((1, width), lambda i: (0, 0))
    row_f32, vec_f32 = jax.ShapeDtypeStruct((rows, width), F32), jax.ShapeDtypeStruct((1, width), F32)
    in_specs, operands = [row_spec, row_spec, row_spec, vec_spec], [dh, dhn, h, w]
    out_shape, out_specs = [row_f32, vec_f32], [row_spec, vec_spec]
    if fused:
        in_specs += [row_spec, vec_spec]
        operands += list(then)
        out_shape += [jax.ShapeDtypeStruct((rows, width), BF16), vec_f32]
        out_specs += [row_spec, vec_spec]
    return _call(body, name=name, out_shape=out_shape, grid=(rows // tm,), in_specs=in_specs, out_specs=out_specs,
                 operands=operands, semantics=("arbitrary",), steps=steps)


def _shift_down(x, s, rows):
    return pltpu.roll(x, s, 0) if s else x


def _shift_up(x, s, rows):
    return pltpu.roll(x, rows - s, 0) if s else x


def _conv4_fwd(name, zx, cw, cb, steps=()):
    rows = zx.shape[0]
    off = D_INNER // LANES

    def body(x_ref, w_ref, b_ref, o_ref):
        x = x_ref[...]
        acc = b_ref[...] + w_ref[pl.ds(SSM_CONV - 1, 1), :] * x
        for s in range(1, SSM_CONV):
            acc = acc + w_ref[pl.ds(SSM_CONV - 1 - s, 1), :] * _shift_down(x, s, rows)
        valid = lax.broadcasted_iota(jnp.int32, (rows, 1), 0) >= PAD_ROWS
        o_ref[...] = jnp.where(valid, acc * _sigmoid(acc), 0.0)

    return _call(
        body, name=name, out_shape=jax.ShapeDtypeStruct((rows, D_XBC), F32), grid=(D_XBC // LANES,),
        in_specs=[pl.BlockSpec((rows, LANES), lambda j: (0, j + off)),
                  pl.BlockSpec((SSM_CONV, LANES), lambda j: (0, j)),
                  pl.BlockSpec((1, LANES), lambda j: (0, j))],
        out_specs=pl.BlockSpec((rows, LANES), lambda j: (0, j)), operands=[zx, cw, cb],
        semantics=("parallel",), steps=steps)


def _conv4_bwd(name, zx, dout, cw, cb, col0):
    rows, width = dout.shape
    zoff = (D_INNER + col0) // LANES
    woff = col0 // LANES

    def body(x_ref, d_ref, w_ref, b_ref, dx_ref, dw_ref, db_ref):
        x = x_ref[...]
        shifted = [_shift_down(x, s, rows) for s in range(SSM_CONV)]
        acc = b_ref[...]
        for s in range(SSM_CONV):
            acc = acc + w_ref[pl.ds(SSM_CONV - 1 - s, 1), :] * shifted[s]
        sig = _sigmoid(acc)
        valid = lax.broadcasted_iota(jnp.int32, (rows, 1), 0) >= PAD_ROWS
        dpre = jnp.where(valid, d_ref[...] * sig * (1.0 + acc * (1.0 - sig)), 0.0)
        dx = w_ref[pl.ds(SSM_CONV - 1, 1), :] * dpre
        for s in range(1, SSM_CONV):
            dx = dx + w_ref[pl.ds(SSM_CONV - 1 - s, 1), :] * _shift_up(dpre, s, rows)
        dx_ref[...] = dx.astype(BF16)
        for s in range(SSM_CONV):
            dw_ref[pl.ds(SSM_CONV - 1 - s, 1), :] = jnp.sum(dpre * shifted[s], axis=0, keepdims=True)
        db_ref[...] = jnp.sum(dpre, axis=0, keepdims=True)

    return pl.pallas_call(
        body, name=name,
        out_shape=(jax.ShapeDtypeStruct((rows, width), BF16), jax.ShapeDtypeStruct((SSM_CONV, width), F32),
                   jax.ShapeDtypeStruct((1, width), F32)),
        grid=(width // LANES,),
        in_specs=[pl.BlockSpec((rows, LANES), lambda j: (0, j + zoff)),
                  pl.BlockSpec((rows, LANES), lambda j: (0, j)),
                  pl.BlockSpec((SSM_CONV, LANES), lambda j: (0, j + woff)),
                  pl.BlockSpec((1, LANES), lambda j: (0, j + woff))],
        out_specs=(pl.BlockSpec((rows, LANES), lambda j: (0, j)),
                   pl.BlockSpec((SSM_CONV, LANES), lambda j: (0, j)),
                   pl.BlockSpec((1, LANES), lambda j: (0, j))),
        compiler_params=_cparams(("parallel",)),
    )(zx, dout, cw, cb)


FFN_TILE = 2 * LANES


def _ffn_up_conv(name, hn, w_up, cw, cb, steps=()):
    rows, k = hn.shape
    chip_blocks = w_up.shape[2] // LANES
    half_blocks = D_FF // LANES
    nt = D_FF // FFN_TILE

    def weight_block(offset):
        return pl.BlockSpec((None, k, LANES), lambda j: ((2 * j + offset) // chip_blocks, 0, (2 * j + offset) % chip_blocks))

    def body(a_ref, g0, g1, v0, v1, wg_ref, wv_ref, bg_ref, bv_ref, upg_ref, upv_ref, act_ref):
        a = a_ref[...]
        g = _dot(a, jnp.concatenate([g0[...], g1[...]], axis=1))
        v = _dot(a, jnp.concatenate([v0[...], v1[...]], axis=1))
        upg_ref[...] = g
        upv_ref[...] = v
        ug, uv = bg_ref[...], bv_ref[...]
        for s in range(FFN_CONV):
            ug = ug + wg_ref[pl.ds(FFN_CONV - 1 - s, 1), :] * _shift_down(g, s, rows)
            uv = uv + wv_ref[pl.ds(FFN_CONV - 1 - s, 1), :] * _shift_down(v, s, rows)
        act_ref[...] = (ug * _sigmoid(ug) * uv).astype(BF16)

    col = pl.BlockSpec((rows, FFN_TILE), lambda j: (0, j))
    wsp = lambda shift: pl.BlockSpec((FFN_CONV, FFN_TILE), lambda j: (0, j + shift))
    bsp = lambda shift: pl.BlockSpec((1, FFN_TILE), lambda j: (0, j + shift))
    half = jax.ShapeDtypeStruct((rows, D_FF), F32)
    return _call(
        body, name=name, out_shape=(half, half, jax.ShapeDtypeStruct((rows, D_FF), BF16)), grid=(nt,),
        in_specs=[pl.BlockSpec((rows, k), lambda j: (0, 0)), weight_block(0), weight_block(1),
                  weight_block(half_blocks), weight_block(half_blocks + 1), wsp(0), wsp(nt), bsp(0), bsp(nt)],
        out_specs=(col, col, col), operands=[hn, w_up, w_up, w_up, w_up, cw, cw, cb, cb],
        semantics=("parallel",), steps=steps)


def _ffn_conv_bwd(name, up_g, up_v, dact, cw, cb, steps=()):
    rows = up_g.shape[0]
    nt = D_FF // LANES

    def body(g_ref, v_ref, d_ref, wg_ref, wv_ref, bg_ref, bv_ref, dxg_ref, dxv_ref, dwg_ref, dwv_ref, dbg_ref, dbv_ref):
        g, v = g_ref[...], v_ref[...]
        gs = [_shift_down(g, s, rows) for s in range(FFN_CONV)]
        vs = [_shift_down(v, s, rows) for s in range(FFN_CONV)]
        ug, uv = bg_ref[...], bv_ref[...]
        for s in range(FFN_CONV):
            ug = ug + wg_ref[pl.ds(FFN_CONV - 1 - s, 1), :] * gs[s]
            uv = uv + wv_ref[pl.ds(FFN_CONV - 1 - s, 1), :] * vs[s]
        sig = _sigmoid(ug)
        dsig = d_ref[...] * sig
        for dpre, src, w_ref, dx_ref, dw_ref, db_ref in (
                (dsig * uv * (1.0 + ug * (1.0 - sig)), gs, wg_ref, dxg_ref, dwg_ref, dbg_ref),
                (dsig * ug, vs, wv_ref, dxv_ref, dwv_ref, dbv_ref)):
            dx = w_ref[pl.ds(FFN_CONV - 1, 1), :] * dpre
            for s in range(1, FFN_CONV):
                dx = dx + w_ref[pl.ds(FFN_CONV - 1 - s, 1), :] * _shift_up(dpre, s, rows)
            dx_ref[...] = dx.astype(BF16)
            for s in range(FFN_CONV):
                dw_ref[pl.ds(FFN_CONV - 1 - s, 1), :] = jnp.sum(dpre * src[s], axis=0, keepdims=True)
            db_ref[...] = jnp.sum(dpre, axis=0, keepdims=True)

    col = lambda shift: pl.BlockSpec((rows, LANES), lambda j: (0, j + shift))
    wsp = lambda shift: pl.BlockSpec((FFN_CONV, LANES), lambda j: (0, j + shift))
    bsp = lambda shift: pl.BlockSpec((1, LANES), lambda j: (0, j + shift))
    dx_shape = jax.ShapeDtypeStruct((rows, D_FF), BF16)
    dw_shape = jax.ShapeDtypeStruct((FFN_CONV, D_FF), F32)
    db_shape = jax.ShapeDtypeStruct((1, D_FF), F32)
    return _call(
        body, name=name, out_shape=(dx_shape, dx_shape, dw_shape, dw_shape, db_shape, db_shape), grid=(nt,),
        in_specs=[col(0), col(0), col(0), wsp(0), wsp(nt), bsp(0), bsp(nt)],
        out_specs=(col(0), col(0), wsp(0), wsp(0), bsp(0), bsp(0)),
        operands=[up_g, up_v, dact, cw, cw, cb, cb], semantics=("parallel",), steps=steps)


def _dt_fwd(name, dtr, bias):
    rows = dtr.shape[0]
    tm = _row_tile(rows, LANES)

    def body(d_ref, b_ref, o_ref):
        v = d_ref[...] + b_ref[...]
        sp = jnp.maximum(v, 0.0) + jnp.log1p(jnp.exp(-jnp.abs(v)))
        lane = lax.broadcasted_iota(jnp.int32, (tm, LANES), 1)
        ok = _rows_mask(pl.program_id(0), tm) & (lane < SSM_HEADS)
        o_ref[...] = jnp.where(ok, sp, 0.0)

    return pl.pallas_call(
        body, name=name, out_shape=jax.ShapeDtypeStruct((rows, LANES), F32), grid=(rows // tm,),
        in_specs=[pl.BlockSpec((tm, LANES), lambda i: (i, 0)), pl.BlockSpec((1, LANES), lambda i: (0, 0))],
        out_specs=pl.BlockSpec((tm, LANES), lambda i: (i, 0)), compiler_params=_cparams(("parallel",)),
    )(dtr, bias)


def _dt_bwd(name, ddt, dtr, bias):
    rows = dtr.shape[0]
    tm = _row_tile(rows, LANES)

    def body(g_ref, d_ref, b_ref, o_ref, db_ref):
        i = pl.program_id(0)
        lane = lax.broadcasted_iota(jnp.int32, (tm, LANES), 1)
        ok = _rows_mask(i, tm) & (lane < SSM_HEADS)
        dv = jnp.where(ok, g_ref[...] * _sigmoid(d_ref[...] + b_ref[...]), 0.0)
        o_ref[...] = dv.astype(BF16)

        @pl.when(i == 0)
        def _():
            db_ref[...] = jnp.zeros_like(db_ref)

        db_ref[...] += jnp.sum(dv, axis=0, keepdims=True)

    row_spec = pl.BlockSpec((tm, LANES), lambda i: (i, 0))
    vec_spec = pl.BlockSpec((1, LANES), lambda i: (0, 0))
    return pl.pallas_call(
        body, name=name,
        out_shape=(jax.ShapeDtypeStruct((rows, LANES), BF16), jax.ShapeDtypeStruct((1, LANES), F32)),
        grid=(rows // tm,), in_specs=[row_spec, row_spec, vec_spec], out_specs=(row_spec, vec_spec),
        compiler_params=_cparams(("arbitrary",)),
    )(ddt, dtr, bias)


def _gate_fwd(name, y, zx, w, steps=()):
    rows = y.shape[0]
    tm = _row_tile(rows, D_INNER)

    def body(y_ref, z_ref, w_ref, o_ref):
        z = z_ref[...]
        g = y_ref[...] * (z * _sigmoid(z))
        r = lax.rsqrt(jnp.mean(g * g, axis=-1, keepdims=True) + RMS_EPS)
        o_ref[...] = (g * r * w_ref[...]).astype(BF16)

    row_spec = pl.BlockSpec((tm, D_INNER), lambda i: (i, 0))
    return _call(
        body, name=name, out_shape=jax.ShapeDtypeStruct((rows, D_INNER), BF16), grid=(rows // tm,),
        in_specs=[row_spec, row_spec, pl.BlockSpec((1, D_INNER), lambda i: (0, 0))],
        out_specs=row_spec, operands=[y, zx, w], semantics=("parallel",), steps=steps)


def _gate_bwd(name, dyn, y, zx, w):
    rows = y.shape[0]
    tm = _row_tile(rows, D_INNER)

    def body(d_ref, y_ref, z_ref, w_ref, dy_ref, dz_ref, dw_ref):
        i = pl.program_id(0)
        z, yv = z_ref[...], y_ref[...]
        sig = _sigmoid(z)
        sz = z * sig
        g = yv * sz
        r = lax.rsqrt(jnp.mean(g * g, axis=-1, keepdims=True) + RMS_EPS)
        ghat = g * r
        dn = d_ref[...]
        dghat = dn * w_ref[...]
        dg = r * (dghat - ghat * jnp.mean(dghat * ghat, axis=-1, keepdims=True))
        dy_ref[...] = dg * sz
        dz_ref[...] = (dg * yv * sig * (1.0 + z * (1.0 - sig))).astype(BF16)

        @pl.when(i == 0)
        def _():
            dw_ref[...] = jnp.zeros_like(dw_ref)

        dw_ref[...] += jnp.sum(dn * ghat, axis=0, keepdims=True)

    row_spec = pl.BlockSpec((tm, D_INNER), lambda i: (i, 0))
    vec_spec = pl.BlockSpec((1, D_INNER), lambda i: (0, 0))
    return pl.pallas_call(
        body, name=name,
        out_shape=(jax.ShapeDtypeStruct((rows, D_INNER), F32), jax.ShapeDtypeStruct((rows, D_INNER), BF16),
                   jax.ShapeDtypeStruct((1, D_INNER), F32)),
        grid=(rows // tm,), in_specs=[row_spec, row_spec, row_spec, vec_spec],
        out_specs=(row_spec, row_spec, vec_spec), compiler_params=_cparams(("arbitrary",)),
    )(dyn, y, zx, w)


def _split3(x):
    hi = x.astype(BF16)
    r1 = x - hi.astype(F32)
    mid = r1.astype(BF16)
    lo = (r1 - mid.astype(F32)).astype(BF16)
    return hi, mid, lo


def _dot3_data_lhs(x, sel):
    sel16 = sel.astype(F32).astype(BF16)
    hi, mid, lo = _split3(x)
    return _dot(hi, sel16) + _dot(mid, sel16) + _dot(lo, sel16)


def _dot2_data_lhs(x, sel):
    sel16 = sel.astype(F32).astype(BF16)
    hi = x.astype(BF16)
    mid = (x - hi.astype(F32)).astype(BF16)
    return _dot(hi, sel16) + _dot(mid, sel16)


def _dot3_data_rhs(sel, x):
    sel16 = sel.astype(F32).astype(BF16)
    hi, mid, lo = _split3(x)
    return _dot(sel16, hi) + _dot(sel16, mid) + _dot(sel16, lo)


def _causal_masks():
    r = lax.broadcasted_iota(jnp.int32, (CHUNK, CHUNK), 0)
    c = lax.broadcasted_iota(jnp.int32, (CHUNK, CHUNK), 1)
    return r >= c, r <= c


def _expand_heads_matrix(g):
    k = lax.broadcasted_iota(jnp.int32, (LANES, GROUP_W), 0)
    j = lax.broadcasted_iota(jnp.int32, (LANES, GROUP_W), 1)
    return HEADS_PER_GROUP * g + jnp.right_shift(j, 6) == k


def _reduce_heads_matrix(g):
    j = lax.broadcasted_iota(jnp.int32, (GROUP_W, LANES), 0)
    k = lax.broadcasted_iota(jnp.int32, (GROUP_W, LANES), 1)
    return HEADS_PER_GROUP * g + jnp.right_shift(j, 6) == k


def _reduce_pair_matrix(g, p):
    j = lax.broadcasted_iota(jnp.int32, (LANES, LANES), 0)
    k = lax.broadcasted_iota(jnp.int32, (LANES, LANES), 1)
    return HEADS_PER_GROUP * g + 2 * p + jnp.right_shift(j, 6) == k


def _group_cols(ref, g, width):
    return ref.at[:, pl.ds(g * width, width)]


def _ssd_prep(name, dt, a128, steps=()):
    rows = dt.shape[0]
    nc = rows // CHUNK

    def body(dt_ref, a_ref, dte_ref, acs_ref, acst_ref):
        causal, _ = _causal_masks()
        dtv = dt_ref[...]
        acs = _dot3_data_rhs(causal, dtv) * a_ref[...]
        acst_ref[...] = acs.T[0:SSM_HEADS]
        for g in range(N_GROUPS):
            expand = _expand_heads_matrix(g)
            _group_cols(dte_ref, g, GROUP_W)[...] = _dot3_data_lhs(dtv, expand)
            _group_cols(acs_ref, g, GROUP_W)[...] = _dot3_data_lhs(acs, expand)

    blk = pl.BlockSpec((CHUNK, D_INNER), lambda c: (c, 0))
    shp = jax.ShapeDtypeStruct((rows, D_INNER), F32)
    return _call(
        body, name=name, out_shape=(shp, shp, jax.ShapeDtypeStruct((nc, SSM_HEADS, CHUNK), F32)), grid=(nc,),
        in_specs=[pl.BlockSpec((CHUNK, LANES), lambda c: (c, 0)), pl.BlockSpec((1, LANES), lambda c: (0, 0))],
        out_specs=(blk, blk, pl.BlockSpec((None, SSM_HEADS, CHUNK), lambda c: (c, 0, 0))),
        operands=[dt, a128], semantics=("parallel",), steps=steps)


def _ssd_common(x_ref, b_ref, c_ref, dte_ref, acs_ref):
    x = x_ref[...]
    dt_exp = dte_ref[...]
    acs_exp = acs_ref[...]
    tot_exp = acs_ref[pl.ds(CHUNK - 1, 1), :]
    xdt = x * dt_exp
    e_exp = jnp.exp(acs_exp)
    f_exp = jnp.exp(tot_exp - acs_exp)
    return _causal_masks(), x, dt_exp, acs_exp, tot_exp, xdt, e_exp, f_exp, b_ref[...], c_ref[...]


def _pair_decay(acs_pair, acs_row, e, causal):
    lane = lax.broadcasted_iota(jnp.int32, (CHUNK, LANES), 1)
    mine = (lane < HEAD_DIM) if e == 0 else (lane >= HEAD_DIM)
    a_l = jnp.where(mine, acs_pair, pltpu.roll(acs_pair, HEAD_DIM, 1))
    seg = a_l - acs_row
    dm = jnp.where(causal[0], jnp.exp(jnp.minimum(seg, 0.0)), 0.0)
    dmt = jnp.where(causal[1], jnp.exp(jnp.minimum(-seg, 0.0)), 0.0)
    return dm, dmt


def _ssd_specs(index_of_chunk):
    wide = pl.BlockSpec((CHUNK, D_INNER), lambda c: (index_of_chunk(c), 0))
    b_spec = pl.BlockSpec((CHUNK, D_BC), lambda c: (index_of_chunk(c), D_INNER // D_BC))
    c_spec = pl.BlockSpec((CHUNK, D_BC), lambda c: (index_of_chunk(c), D_INNER // D_BC + 1))
    rows_spec = pl.BlockSpec((None, SSM_HEADS, CHUNK), lambda c: (index_of_chunk(c), 0, 0))
    state_spec = pl.BlockSpec((N_GROUPS, None, D_STATE, GROUP_W), lambda c: (0, index_of_chunk(c), 0, 0))
    return wide, b_spec, c_spec, rows_spec, state_spec


def _ssd_fwd(name, xbc, dt_exp, acs_exp, acs_rows, dskexp, steps=()):
    rows = xbc.shape[0]
    nc = rows // CHUNK

    def body(x_ref, b_ref, c_ref, dte_ref, acs_ref, acst_ref, dsk_ref, y_ref, st_ref, s_scr):
        @pl.when(pl.program_id(0) == 0)
        def _():
            s_scr[...] = jnp.zeros_like(s_scr)

        lane = lax.broadcasted_iota(jnp.int32, (CHUNK, LANES), 1)
        for g in range(N_GROUPS):
            y_g = _group_cols(y_ref, g, GROUP_W)
            causal, x, _, acs_exp_v, tot_exp, xdt, e_exp, f_exp, bm, cm = _ssd_common(
                _group_cols(x_ref, g, GROUP_W), _group_cols(b_ref, g, D_STATE), _group_cols(c_ref, g, D_STATE),
                _group_cols(dte_ref, g, GROUP_W), _group_cols(acs_ref, g, GROUP_W))
            state = s_scr[g]
            st_ref[g] = state
            cb16, bb16 = cm.astype(BF16), bm.astype(BF16)
            cb = _dot_nt(cb16, bb16)
            base = e_exp * _dot(cb16, state.astype(BF16)) + _group_cols(dsk_ref, g, GROUP_W)[...] * x
            for p in range(HEADS_PER_GROUP // 2):
                sl = slice(p * LANES, (p + 1) * LANES)
                xp = xdt[:, sl].astype(BF16)
                yd = []
                for e in range(2):
                    acs_row = acst_ref[pl.ds(g * HEADS_PER_GROUP + 2 * p + e, 1), :]
                    dm, _ = _pair_decay(acs_exp_v[:, sl], acs_row, e, causal)
                    yd.append(_dot((cb * dm).astype(BF16), xp))
                y_g[:, sl] = jnp.where(lane < HEAD_DIM, yd[0], yd[1]) + base[:, sl]
            s_scr[g] = jnp.exp(tot_exp) * state + _dot_tn(bb16, (f_exp * xdt).astype(BF16))

    wide, b_spec, c_spec, rows_spec, state_spec = _ssd_specs(lambda c: c)
    return _call(
        body, name=name,
        out_shape=(jax.ShapeDtypeStruct((rows, D_INNER), F32),
                   jax.ShapeDtypeStruct((N_GROUPS, nc, D_STATE, GROUP_W), F32)),
        grid=(nc,),
        in_specs=[wide, b_spec, c_spec, wide, wide, rows_spec, pl.BlockSpec((1, D_INNER), lambda c: (0, 0))],
        out_specs=(wide, state_spec),
        scratch_shapes=[pltpu.VMEM((N_GROUPS, D_STATE, GROUP_W), F32)],
        operands=[xbc, xbc, xbc, dt_exp, acs_exp, acs_rows, dskexp], semantics=("arbitrary",), steps=steps)


def _ssd_bwd(name, xbc, dt_exp, acs_exp, acs_rows, dt, a128, dskexp, dy, states, steps=()):
    rows = xbc.shape[0]
    nc = rows // CHUNK
    last = nc - 1

    def body(x_ref, b_ref, c_ref, dte_ref, acs_ref, acst_ref, dt_ref, a128_ref, dsk_all, dy_all, st_all,
             dx_all, db_all, dc_all, ddt_ref, dalog_ref, ddsk_ref, ds_all):
        @pl.when(pl.program_id(0) == 0)
        def _():
            ds_all[...] = jnp.zeros_like(ds_all)
            dalog_ref[...] = jnp.zeros_like(dalog_ref)
            ddsk_ref[...] = jnp.zeros_like(ddsk_ref)

        dacs = jnp.zeros((CHUNK, LANES), F32)
        ddt_x = jnp.zeros((CHUNK, LANES), F32)
        for g in range(N_GROUPS):
            dacs_g, ddt_x_g = group(
                g, _group_cols(x_ref, g, GROUP_W), _group_cols(b_ref, g, D_STATE), _group_cols(c_ref, g, D_STATE),
                _group_cols(dte_ref, g, GROUP_W), _group_cols(acs_ref, g, GROUP_W), acst_ref,
                _group_cols(dsk_all, g, GROUP_W), _group_cols(dy_all, g, GROUP_W), st_all.at[g],
                _group_cols(dx_all, g, GROUP_W), _group_cols(db_all, g, D_STATE), _group_cols(dc_all, g, D_STATE),
                ddsk_ref, ds_all.at[g])
            dacs, ddt_x = dacs + dacs_g, ddt_x + ddt_x_g
        _, causal_t = _causal_masks()
        da = _dot3_data_rhs(causal_t, dacs)
        ddt_ref[...] = da * a128_ref[...] + ddt_x
        dalog_ref[...] += jnp.sum(da * dt_ref[...], axis=0, keepdims=True) * a128_ref[...]

    def group(g, x_ref, b_ref, c_ref, dte_ref, acs_ref, acst_ref, dsk_ref, dy_ref, st_ref,
              dx_ref, db_ref, dc_ref, ddsk_ref, ds_scr):
        causal, x, dt_exp, acs_exp_v, tot_exp, xdt, e_exp, f_exp, bm, cm = _ssd_common(
            x_ref, b_ref, c_ref, dte_ref, acs_ref)
        reduce_heads = _reduce_heads_matrix(g)
        state, dstate = st_ref[...], ds_scr[...]
        dyv = dy_ref[...]
        cb16, bb16 = cm.astype(BF16), bm.astype(BF16)
        s16, ds16 = state.astype(BF16), dstate.astype(BF16)
        cb = _dot_nt(cb16, bb16)
        cbt = _dot_nt(bb16, cb16)
        cs = _dot(cb16, s16)
        bds = _dot(bb16, ds16)
        edy = e_exp * dyv
        fx = f_exp * xdt
        dxdt_base = f_exp * bds
        dc_acc = _dot_nt(edy.astype(BF16), s16)
        db_acc = _dot_nt(fx.astype(BF16), ds16)
        ds_scr[...] = jnp.exp(tot_exp) * dstate + _dot_tn(cb16, edy.astype(BF16))
        q = fx * bds
        dacs = _dot2_data_lhs(edy * cs - q, reduce_heads)
        dtot = jnp.sum(_dot2_data_lhs(q + jnp.exp(tot_exp) * dstate * state, reduce_heads), axis=0, keepdims=True)
        ddsk_ref[...] += jnp.sum(_dot2_data_lhs(dyv * x, reduce_heads), axis=0, keepdims=True)
        lane = lax.broadcasted_iota(jnp.int32, (CHUNK, LANES), 1)
        dcb = jnp.zeros((CHUNK, CHUNK), F32)
        dcbt = jnp.zeros((CHUNK, CHUNK), F32)
        ddt_x = jnp.zeros((CHUNK, LANES), F32)
        for p in range(HEADS_PER_GROUP // 2):
            sl = slice(p * LANES, (p + 1) * LANES)
            xp, dyp = xdt[:, sl], dyv[:, sl]
            xp16, dyp16 = xp.astype(BF16), dyp.astype(BF16)
            dxh = []
            for e in range(2):
                h = 2 * p + e
                mine = (lane < HEAD_DIM) if e == 0 else (lane >= HEAD_DIM)
                acs_row = acst_ref[pl.ds(g * HEADS_PER_GROUP + h, 1), :]
                dm, dmt = _pair_decay(acs_exp_v[:, sl], acs_row, e, causal)
                m, mt = cb * dm, cbt * dmt
                xh16 = jnp.where(mine, xp, 0.0).astype(BF16)
                dyh16 = jnp.where(mine, dyp, 0.0).astype(BF16)
                d_m = _dot_nt(dyh16, xp16)
                d_mt = _dot_nt(xh16, dyp16)
                dacs_h = (jnp.sum(d_m * m, axis=-1, keepdims=True)
                          - jnp.sum(d_mt * mt, axis=-1, keepdims=True))
                dacs = dacs + jnp.where(lane == HEADS_PER_GROUP * g + h, dacs_h, 0.0)
                dcb = dcb + d_m * dm
                dcbt = dcbt + d_mt * dmt
                dxh.append(_dot(mt.astype(BF16), dyp16))
            dxdt = jnp.where(lane < HEAD_DIM, dxh[0], dxh[1]) + dxdt_base[:, sl]
            dx_ref[:, sl] = dxdt * dt_exp[:, sl] + dsk_ref[:, sl] * dyp
            ddt_x = ddt_x + _dot2_data_lhs(dxdt * x[:, sl], _reduce_pair_matrix(g, p))
        dc_ref[...] = dc_acc + _dot(dcb.astype(BF16), bb16)
        db_ref[...] = db_acc + _dot(dcbt.astype(BF16), cb16)
        row = lax.broadcasted_iota(jnp.int32, (CHUNK, LANES), 0)
        return dacs + jnp.where(row == CHUNK - 1, dtot, 0.0), ddt_x

    wide, b_spec, c_spec, rows_spec, state_spec = _ssd_specs(lambda c: last - c)
    heads_spec = pl.BlockSpec((CHUNK, LANES), lambda c: (last - c, 0))
    vec_spec = pl.BlockSpec((1, LANES), lambda c: (0, 0))
    bc_out = pl.BlockSpec((CHUNK, D_BC), lambda c: (last - c, 0))
    vec_shape = jax.ShapeDtypeStruct((1, LANES), F32)
    return _call(
        body, name=name,
        out_shape=(jax.ShapeDtypeStruct((rows, D_INNER), F32), jax.ShapeDtypeStruct((rows, D_BC), F32),
                   jax.ShapeDtypeStruct((rows, D_BC), F32), jax.ShapeDtypeStruct((rows, LANES), F32),
                   vec_shape, vec_shape),
        grid=(nc,),
        in_specs=[wide, b_spec, c_spec, wide, wide, rows_spec, heads_spec, vec_spec,
                  pl.BlockSpec((1, D_INNER), lambda c: (0, 0)), wide, state_spec],
        out_specs=(wide, bc_out, bc_out, heads_spec, vec_spec, vec_spec),
        scratch_shapes=[pltpu.VMEM((N_GROUPS, D_STATE, GROUP_W), F32)],
        operands=[xbc, xbc, xbc, dt_exp, acs_exp, acs_rows, dt, a128, dskexp, dy, states],
        semantics=("arbitrary",), steps=steps)


def _attn_visible(b, heads=1):
    row = jnp.bitwise_and(lax.broadcasted_iota(jnp.int32, (heads * CHUNK, 3 * CHUNK), 0), CHUNK - 1)
    col = lax.broadcasted_iota(jnp.int32, (heads * CHUNK, 3 * CHUNK), 1)
    bb = b + jnp.zeros_like(col)
    meta = (col < CHUNK) & (bb >= 1) & (col >= PAD_ROWS)
    prev = (col >= CHUNK) & (col < 2 * CHUNK) & (bb >= 2) & ((col - CHUNK) > row)
    cur = (col >= 2 * CHUNK) & ((col - 2 * CHUNK) <= row) & ((bb >= 1) | ((col - 2 * CHUNK) >= PAD_ROWS))
    return meta | prev | cur


def _attn_visible4(b):
    return _attn_visible(b, 4)


def _stack_heads(q_ref, sink_ref, kvh, scale):
    lane = lax.broadcasted_iota(jnp.int32, (CHUNK, LANES), 1)
    parts, sinks = [], []
    for pp in range(2):
        pair = kvh * 2 + pp
        qp = q_ref[:, pair * LANES:(pair + 1) * LANES] * scale
        for e in range(2):
            mine = (lane < HEAD_DIM) if e == 0 else (lane >= HEAD_DIM)
            parts.append(jnp.where(mine, qp, 0.0).astype(BF16))
            sinks.append(jnp.full((CHUNK, 1), sink_ref[2 * pair + e], F32))
    return jnp.concatenate(parts, axis=0), jnp.concatenate(sinks, axis=0)


def _attn_operands(q_ref, k0, kp, kc, v0, vp, vc, sink_ref):
    kcat, vcat, q4, sink4 = [], [], [], []
    for kvh in range(N_KV_HEADS):
        ksl = slice(kvh * LANES, (kvh + 1) * LANES)
        kcat.append(jnp.concatenate([k0[:, ksl], kp[:, ksl], kc[:, ksl]], axis=0).astype(BF16))
        vcat.append(jnp.concatenate([v0[:, ksl], vp[:, ksl], vc[:, ksl]], axis=0).astype(BF16))
        stacked, sinks = _stack_heads(q_ref, sink_ref, kvh, ATTN_SCALE)
        q4.append(stacked)
        sink4.append(sinks)
    return kcat, vcat, q4, sink4


def _attn_probs(q4, kcat, visible, sink4):
    heads = range(N_KV_HEADS)
    s = [jnp.where(visible, _dot_nt(q4[h], kcat[h]), NEG_INF) for h in heads]
    m = [jnp.maximum(jnp.max(s[h], axis=-1, keepdims=True), sink4[h]) for h in heads]
    pe = [jnp.exp(s[h] - m[h]) for h in heads]
    pe_sink = [jnp.exp(sink4[h] - m[h]) for h in heads]
    inv = [1.0 / (jnp.sum(pe[h], axis=-1, keepdims=True) + pe_sink[h]) for h in heads]
    return [pe[h] * inv[h] for h in heads], [pe_sink[h] * inv[h] for h in heads]


def _unstack_pairs(stacked, pp):
    lane = lax.broadcasted_iota(jnp.int32, (CHUNK, LANES), 1)
    return jnp.where(lane < HEAD_DIM, stacked[(2 * pp) * CHUNK:(2 * pp + 1) * CHUNK],
                     stacked[(2 * pp + 1) * CHUNK:(2 * pp + 2) * CHUNK])


def _attn_specs(colblock):
    blk = lambda f: pl.BlockSpec((CHUNK, 2 * D_KV), f)
    return [blk(lambda b: (0, colblock)), blk(lambda b: (jnp.maximum(b - 1, 0), colblock)), blk(lambda b: (b, colblock))]


def _attn_fwd(name, q, kv2, sinks, steps=()):
    rows = q.shape[0]

    def body(q_ref, k0, kp, kc, v0, vp, vc, sink_ref, o_ref):
        visible = _attn_visible4(pl.program_id(0))
        kcat, vcat, q4, sink4 = _attn_operands(q_ref, k0, kp, kc, v0, vp, vc, sink_ref)
        pn, _ = _attn_probs(q4, kcat, visible, sink4)
        o4 = [_dot(pn[h].astype(BF16), vcat[h]) for h in range(N_KV_HEADS)]
        for kvh in range(N_KV_HEADS):
            for pp in range(2):
                qsl = slice((kvh * 2 + pp) * LANES, (kvh * 2 + pp + 1) * LANES)
                o_ref[:, qsl] = _unstack_pairs(o4[kvh], pp).astype(BF16)

    return _call(
        body, name=name, out_shape=jax.ShapeDtypeStruct((rows, D_MODEL), BF16), grid=(rows // CHUNK,),
        in_specs=[pl.BlockSpec((CHUNK, D_MODEL), lambda b: (b, 0))] + _attn_specs(0) + _attn_specs(1)
        + [pl.BlockSpec(memory_space=pltpu.SMEM)],
        out_specs=pl.BlockSpec((CHUNK, D_MODEL), lambda b: (b, 0)),
        operands=[q, kv2, kv2, kv2, kv2, kv2, kv2, sinks], semantics=("parallel",), steps=steps)


def _attn_bwd(name, q, kv2, sinks, do, steps=()):
    rows = q.shape[0]

    def body(q_ref, k0, kp, kc, v0, vp, vc, sink_ref, do_ref,
             dq_ref, dkc_ref, dkp_ref, dvc_ref, dvp_ref, dkm_ref, dvm_ref, dsink_ref):
        @pl.when(pl.program_id(0) == 0)
        def _():
            dkm_ref[...] = jnp.zeros_like(dkm_ref)
            dvm_ref[...] = jnp.zeros_like(dvm_ref)
            dsink_ref[...] = jnp.zeros_like(dsink_ref)

        visible = _attn_visible4(pl.program_id(0))
        heads = range(N_KV_HEADS)
        lane1 = lax.broadcasted_iota(jnp.int32, (1, LANES), 1)
        kcat, vcat, q4, sink4 = _attn_operands(q_ref, k0, kp, kc, v0, vp, vc, sink_ref)
        do4 = [_stack_heads(do_ref, sink_ref, h, 1.0)[0] for h in heads]
        pn, psink = _attn_probs(q4, kcat, visible, sink4)
        dp = [_dot_nt(do4[h], vcat[h]) for h in heads]
        delta = [jnp.sum(pn[h] * dp[h], axis=-1, keepdims=True) for h in heads]
        ds16 = [(pn[h] * (dp[h] - delta[h])).astype(BF16) for h in heads]
        dq4 = [_dot(ds16[h], kcat[h]) for h in heads]
        dk_acc = [_dot_tn(ds16[h], q4[h]) for h in heads]
        dv_acc = [_dot_tn(pn[h].astype(BF16), do4[h]) for h in heads]
        dsink = jnp.zeros((1, LANES), F32)
        for kvh in heads:
            ksl = slice(kvh * LANES, (kvh + 1) * LANES)
            sink_terms = psink[kvh] * delta[kvh]
            for j in range(4):
                part = jnp.sum(sink_terms[j * CHUNK:(j + 1) * CHUNK], axis=0, keepdims=True)
                dsink = dsink - jnp.where(lane1 == kvh * 4 + j, part, 0.0)
            for pp in range(2):
                qsl = slice((kvh * 2 + pp) * LANES, (kvh * 2 + pp + 1) * LANES)
                dq_ref[:, qsl] = (_unstack_pairs(dq4[kvh], pp) * ATTN_SCALE).astype(BF16)
            dkm_ref[:, ksl] += dk_acc[kvh][0:CHUNK]
            dvm_ref[:, ksl] += dv_acc[kvh][0:CHUNK]
            dkp_ref[:, ksl] = dk_acc[kvh][CHUNK:2 * CHUNK]
            dvp_ref[:, ksl] = dv_acc[kvh][CHUNK:2 * CHUNK]
            dkc_ref[:, ksl] = dk_acc[kvh][2 * CHUNK:3 * CHUNK]
            dvc_ref[:, ksl] = dv_acc[kvh][2 * CHUNK:3 * CHUNK]
        dsink_ref[...] += dsink

    qspec = pl.BlockSpec((CHUNK, D_MODEL), lambda b: (b, 0))
    kvspec = pl.BlockSpec((CHUNK, 2 * D_KV), lambda b: (b, 0))
    fixed = pl.BlockSpec((CHUNK, 2 * D_KV), lambda b: (0, 0))
    kv_shape = jax.ShapeDtypeStruct((rows, 2 * D_KV), F32)
    meta_shape = jax.ShapeDtypeStruct((CHUNK, 2 * D_KV), F32)
    return _call(
        body, name=name,
        out_shape=(jax.ShapeDtypeStruct((rows, D_MODEL), BF16), kv_shape, kv_shape, kv_shape, kv_shape,
                   meta_shape, meta_shape, jax.ShapeDtypeStruct((1, LANES), F32)),
        grid=(rows // CHUNK,),
        in_specs=[qspec] + _attn_specs(0) + _attn_specs(1) + [pl.BlockSpec(memory_space=pltpu.SMEM), qspec],
        out_specs=(qspec, kvspec, kvspec, kvspec, kvspec, fixed, fixed, pl.BlockSpec((1, LANES), lambda b: (0, 0))),
        operands=[q, kv2, kv2, kv2, kv2, kv2, kv2, sinks, do], semantics=("arbitrary",), steps=steps)


def _kv_grad_combine(name, dk_cur, dk_prev, dk_meta, dv_cur, dv_prev, dv_meta):
    rows = dk_cur.shape[0]
    nb = rows // CHUNK
    width = 2 * D_KV

    def body(kc_ref, kp_ref, km_ref, vc_ref, vp_ref, vm_ref, o_ref):
        jj = pl.program_id(0) + jnp.zeros((CHUNK, 1), jnp.int32)
        for half, (c_ref, p_ref, m_ref) in enumerate(((kc_ref, kp_ref, km_ref), (vc_ref, vp_ref, vm_ref))):
            total = c_ref[...] + jnp.where(jj < nb - 1, p_ref[...], 0.0) + jnp.where(jj == 0, m_ref[...], 0.0)
            o_ref[:, half * width:(half + 1) * width] = total.astype(BF16)

    blk = lambda f: pl.BlockSpec((CHUNK, width), f)
    three = lambda: [blk(lambda j: (j, 0)), blk(lambda j: (jnp.minimum(j + 1, nb - 1), 0)), blk(lambda j: (0, 0))]
    return pl.pallas_call(
        body, name=name, out_shape=jax.ShapeDtypeStruct((rows, 2 * width), BF16), grid=(nb,),
        in_specs=three() + three(), out_specs=pl.BlockSpec((CHUNK, 2 * width), lambda j: (j, 0)),
        compiler_params=_cparams(("parallel",)),
    )(dk_cur, dk_prev, dk_meta, dv_cur, dv_prev, dv_meta)


def _adamw(name, w, g, m, v, steps=()):
    rows, width = w.shape
    tr = rows
    for cand in range(8, rows + 1, 8):
        if rows % cand == 0 and cand * width * 4 <= (1 << 20):
            tr = cand

    def body(w_ref, g_ref, m_ref, v_ref, d_ref, mo_ref, vo_ref):
        gv = g_ref[...]
        mn = ADAM_B1 * m_ref[...] + (1.0 - ADAM_B1) * gv
        vn = ADAM_B2 * v_ref[...] + (1.0 - ADAM_B2) * (gv * gv)
        m_hat = mn / (1.0 - ADAM_B1 ** ADAM_STEP)
        v_hat = vn / (1.0 - ADAM_B2 ** ADAM_STEP)
        d_ref[...] = -ADAM_LR * (m_hat / (jnp.sqrt(v_hat) + ADAM_EPS) + ADAM_WD * w_ref[...])
        mo_ref[...] = mn
        vo_ref[...] = vn

    blk = pl.BlockSpec((tr, width), lambda i: (i, 0))
    shp = jax.ShapeDtypeStruct((rows, width), F32)
    return _call(body, name=name, out_shape=(shp, shp, shp), grid=(rows // tr,), in_specs=[blk] * 4,
                 out_specs=(blk,) * 3, operands=[w, g, m, v], semantics=("parallel",), steps=steps)


def _ffn_fwd(tag, h, hn, p, i, plan):
    up_g, up_v, act = _ffn_up_conv(f"ffn{tag}_up", hn, plan.weight("f_w_up", i), p["f_conv_w"][i],
                                   p["f_conv_b"][i:i + 1], steps=plan.steps(f"ffn{tag}_up"))
    pre = _mm(f"ffn{tag}_down", act, plan.weight("f_w_down", i), "nn", steps=plan.steps(f"ffn{tag}_down"))
    return pre, (h, hn, up_g, up_v, act, pre)


def _ffn_bwd(tag, dpre, saved, p, i, plan):
    h, hn, up_g, up_v, act, pre = saved
    plan.grad("f_w_down", i, _mm(f"ffn{tag}_down_dw", act, dpre, "tn", out_dtype=BF16))
    dact = _mm(f"ffn{tag}_down_dx", dpre, plan.weight("f_w_down", i), "nt", steps=plan.steps(f"ffn{tag}_down_dx"))
    dug, duv, gwg, gwv, gbg, gbv = _ffn_conv_bwd(f"ffn{tag}_conv_bwd", up_g, up_v, dact, p["f_conv_w"][i],
                                                 p["f_conv_b"][i:i + 1], steps=plan.steps(f"ffn{tag}_conv_bwd"))
    g_cw, g_cb = jnp.concatenate([gwg, gwv], axis=1), jnp.concatenate([gbg, gbv], axis=1)
    w_up = plan.weight("f_w_up", i)
    n = w_up.shape[2]
    dhn = _mm_nt_bychip(f"ffn{tag}_up_dx_gate", dug, w_up, 0)
    dhn = _mm_nt_bychip(f"ffn{tag}_up_dx_val", duv, w_up, N_CHIPS // 2, acc=dhn)
    g_up = _mm_tn_bychip(f"ffn{tag}_up_dw_gate", hn, dug, n, 0)
    plan.grad("f_w_up", i, _mm_tn_bychip(f"ffn{tag}_up_dw_val", hn, duv, n, N_CHIPS // 2, into=g_up))
    return dhn, dict(f_conv_w=g_cw, f_conv_b=g_cb)


def _lanes_pad(a, width=LANES):
    return jnp.pad(a, [(0, 0)] * (a.ndim - 1) + [(0, width - a.shape[-1])])


def _dup_heads(w):
    rows = w.shape[0]
    w = w.reshape(rows, 2 * N_KV_HEADS, 1, HEAD_DIM)
    return jnp.broadcast_to(w, (rows, 2 * N_KV_HEADS, 2, HEAD_DIM)).reshape(rows, 4 * D_KV)


def _undup_heads(g):
    rows = g.shape[0]
    return g.reshape(rows, 2 * N_KV_HEADS, 2, HEAD_DIM).sum(axis=2).reshape(rows, 2 * D_KV)


def _local_step(x2, target, p, plan):
    seq = x2.shape[0]
    rows = seq + CHUNK
    g = {}

    h0 = jnp.concatenate([jnp.zeros((PAD_ROWS, D_MODEL), F32), p["meta_tokens"], x2], axis=0)

    w_in = plan.weight("a_w_in")
    w_dt = jnp.pad(w_in[D_MAIN:], ((0, LANES - SSM_HEADS), (0, 0)))
    dt_bias = _lanes_pad(p["a_dt_bias"])
    a128 = _lanes_pad(-jnp.exp(p["a_a_log"]))
    dskexp = jnp.repeat(p["a_d_skip"].reshape(SSM_HEADS), HEAD_DIM).reshape(1, D_INNER)

    hn0 = _rms_fwd("a_norm", h0, p["a_norm_pre"])
    zx = _mm("a_in_main", hn0, w_in, "nt", k_rows=D_MAIN, steps=plan.steps("a_in_main"))
    dtr = _mm("a_in_dt", hn0, w_dt, "nt")
    xbc = _conv4_fwd("a_conv", zx, p["a_conv_w"], p["a_conv_b"], steps=plan.steps("a_conv"))
    dt = _dt_fwd("a_dt", dtr, dt_bias)
    dt_exp, acs_exp, acs_rows = _ssd_prep("a_ssd_prep", dt, a128, steps=plan.steps("a_ssd_prep"))
    y, states = _ssd_fwd("a_ssd", xbc, dt_exp, acs_exp, acs_rows, dskexp, steps=plan.steps("a_ssd"))
    yn = _gate_fwd("a_gate", y, zx, p["a_gate_norm"], steps=plan.steps("a_gate"))
    mix = _mm("a_out", yn, plan.weight("a_w_out"), "nn", steps=plan.steps("a_out"))
    h1, (hn_f0,) = _resid_norm_fwd("a_resid", h0, mix, p["a_norm_post"], [p["f_norm_pre"][0:1]])

    pre_f0, ffn0 = _ffn_fwd("0", h1, hn_f0, p, 0, plan)
    h2, (hkv, hn2) = _resid_norm_fwd("ffn0_resid", h1, pre_f0, p["f_norm_post"][0:1], [p["kv_norm"], p["b_norm_pre"]])

    w_kv2 = _dup_heads(plan.weight("w_kv"))
    kv2 = _mm("kv_proj", hkv, w_kv2, "nn")
    q = _mm("b_q", hn2, plan.weight("b_w_q"), "nn")
    sinks = p["b_sinks"].reshape(N_Q_HEADS)
    o = _attn_fwd("b_attn", q, kv2, sinks, steps=plan.steps("b_attn"))
    attn = _mm("b_o", o, plan.weight("b_w_o"), "nn", steps=plan.steps("b_o"))
    h3, (hn_f1,) = _resid_norm_fwd("b_resid", h2, attn, p["b_norm_post"], [p["f_norm_pre"][1:2]])

    pre_f1, ffn1 = _ffn_fwd("1", h3, hn_f1, p, 1, plan)
    dh, loss_vec, dpre_f1, g_post1 = _resid_norm_loss("ffn1_resid_loss", h3, pre_f1, p["f_norm_post"][1:2], target)
    loss = loss_vec[0, 0]

    dhn_f1, g1 = _ffn_bwd("1", dpre_f1, ffn1, p, 1, plan)
    dh, g_pre1, dpre, g["b_norm_post"] = _norm_bwd_add("ffn1_norm_bwd", dh, dhn_f1, h3, p["f_norm_pre"][1:2],
                                                        then=(attn, p["b_norm_post"]))
    plan.grad("b_w_o", None, _mm("b_o_dw", o, dpre, "tn", out_dtype=BF16))
    do = _mm("b_o_dx", dpre, plan.weight("b_w_o"), "nt", steps=plan.steps("b_o_dx"))
    dq, dkc, dkp, dvc, dvp, dkm, dvm, dsink = _attn_bwd("b_attn_bwd", q, kv2, sinks, do, steps=plan.steps("b_attn_bwd"))
    g["b_sinks"] = dsink[:, :N_Q_HEADS]
    dhn2 = _mm("b_q_dx", dq, plan.weight("b_w_q"), "nt")
    plan.grad("b_w_q", None, _mm("b_q_dw", hn2, dq, "tn", out_dtype=BF16))
    dh, g["b_norm_pre"] = _norm_bwd_add("b_norm_bwd", dh, dhn2, h2, p["b_norm_pre"])
    dkv2 = _kv_grad_combine("kv_grad", dkc, dkp, dkm, dvc, dvp, dvm)
    dhkv = _mm("kv_proj_dx", dkv2, w_kv2, "nt")
    plan.grad("w_kv", None, _undup_heads(_mm("kv_proj_dw", hkv, dkv2, "tn")))
    dh, g["kv_norm"], dpre_f0, g_post0 = _norm_bwd_add("kv_norm_bwd", dh, dhkv, h2, p["kv_norm"],
                                                       then=(pre_f0, p["f_norm_post"][0:1]))

    dhn_f0, g0 = _ffn_bwd("0", dpre_f0, ffn0, p, 0, plan)
    dh, g_pre0, dpre, g["a_norm_post"] = _norm_bwd_add("ffn0_norm_bwd", dh, dhn_f0, h1, p["f_norm_pre"][0:1],
                                                        then=(mix, p["a_norm_post"]))
    g["f_norm_post"] = jnp.concatenate([g_post0, g_post1], axis=0)
    g["f_norm_pre"] = jnp.concatenate([g_pre0, g_pre1], axis=0)
    g["f_conv_w"] = jnp.stack([g0["f_conv_w"], g1["f_conv_w"]])
    g["f_conv_b"] = jnp.concatenate([g0["f_conv_b"], g1["f_conv_b"]], axis=0)
    plan.grad("a_w_out", None, _mm("a_out_dw", yn, dpre, "tn", out_dtype=BF16))
    dyn = _mm("a_out_dx", dpre, plan.weight("a_w_out"), "nt", steps=plan.steps("a_out_dx"))
    dy, dz, g["a_gate_norm"] = _gate_bwd("a_gate_bwd", dyn, y, zx, p["a_gate_norm"])
    dxs, dbm, dcm, ddt, dalog, ddsk = _ssd_bwd("a_ssd_bwd", xbc, dt_exp, acs_exp, acs_rows, dt, a128, dskexp, dy, states,
                                              steps=plan.steps("a_ssd_bwd"))
    g["a_a_log"] = dalog[:, :SSM_HEADS]
    g["a_d_skip"] = ddsk[:, :SSM_HEADS]
    ddtr, dbias = _dt_bwd("a_dt_bwd", ddt, dtr, dt_bias)
    g["a_dt_bias"] = dbias[:, :SSM_HEADS]
    dxp, gw_x, gb_x = _conv4_bwd("a_conv_bwd_x", zx, dxs, p["a_conv_w"], p["a_conv_b"], 0)
    dbp, gw_b, gb_b = _conv4_bwd("a_conv_bwd_b", zx, dbm, p["a_conv_w"], p["a_conv_b"], D_INNER)
    dcp, gw_c, gb_c = _conv4_bwd("a_conv_bwd_c", zx, dcm, p["a_conv_w"], p["a_conv_b"], D_INNER + D_BC)
    g["a_conv_w"] = jnp.concatenate([gw_x, gw_b, gw_c], axis=1)
    g["a_conv_b"] = jnp.concatenate([gb_x, gb_b, gb_c], axis=1)
    dzx = jnp.concatenate([dz, dxp, dbp, dcp], axis=1)
    g_in = _mm("a_in_main_dw", dzx, hn0, "tn", out_dtype=BF16, out_rows=D_IN_PROJ, steps=plan.steps("a_in_main_dw"))
    plan.grad("a_w_in", None, _tn_rows_into("a_in_dt_dw", ddtr, hn0, g_in, D_MAIN, SSM_HEADS))
    dhn0 = _mm("a_in_dt_dx", ddtr, w_dt, "nn", steps=plan.steps("a_in_dt_dx"))
    dhn0 = _mm("a_in_main_dx", dzx, w_in, "nn", acc=dhn0, steps=plan.steps("a_in_main_dx"))
    dh, g["a_norm_pre"] = _norm_bwd_add("a_norm_bwd", dh, dhn0, h0, p["a_norm_pre"], steps=plan.steps("a_norm_bwd"))

    g["meta_tokens"] = dh[PAD_ROWS:CHUNK]
    return loss, dh[CHUNK:], g


ANY = pl.BlockSpec(memory_space=pl.ANY)
VMEM_SPEC = pl.BlockSpec(memory_space=pltpu.VMEM)


def _allgather_small(name, shard):
    rows = shard.shape[0]

    def body(s_ref, o_ref, send_sems, recv_sems):
        x, y, c = _place()
        me = 2 * x + y
        o_ref[me] = s_ref[...]
        chips = _other_chips(x, y)
        sends = [pltpu.make_async_remote_copy(s_ref, o_ref.at[me], send_sems.at[j], recv_sems.at[j],
                                              device_id=(cx, cy, c), device_id_type=MESH)
                 for j, (cx, cy) in enumerate(chips)]
        for cp in sends:
            cp.start()
        for j, (cx, cy) in enumerate(chips):
            pltpu.make_async_remote_copy(s_ref, o_ref.at[2 * cx + cy], send_sems.at[j], recv_sems.at[j],
                                         device_id=(cx, cy, c), device_id_type=MESH).wait_recv()
        for cp in sends:
            cp.wait_send()

    return pl.pallas_call(
        body, name=name, out_shape=jax.ShapeDtypeStruct((N_CHIPS, rows, LANES), F32),
        in_specs=[VMEM_SPEC], out_specs=VMEM_SPEC,
        scratch_shapes=[pltpu.SemaphoreType.DMA((3,)), pltpu.SemaphoreType.DMA((3,))],
        compiler_params=pltpu.CompilerParams(vmem_limit_bytes=VMEM_LIMIT),
    )(shard)


def _row_block(rows, width, itemsize, align, budget=2 << 20):
    best = rows
    for cand in range(align, rows + 1, align):
        if rows % cand == 0 and cand * width * itemsize <= budget:
            best = cand
    return best


def _cast_into_slot(name, chip, w, layer=None):
    rows, width = w.shape[-2:]
    tr = _row_block(rows, width, 4, 16)
    if layer is None:
        in_spec = pl.BlockSpec((tr, width), lambda i, chip_ref: (i, 0))
    else:
        in_spec = pl.BlockSpec((None, tr, width), lambda i, chip_ref: (layer, i, 0))

    def body(chip_ref, w_ref, o_ref):
        o_ref[...] = w_ref[...].astype(BF16)

    return pl.pallas_call(
        body, name=name, out_shape=jax.ShapeDtypeStruct((N_CHIPS, rows, width), BF16),
        grid_spec=pltpu.PrefetchScalarGridSpec(
            num_scalar_prefetch=1, grid=(rows // tr,), in_specs=[in_spec],
            out_specs=pl.BlockSpec((None, tr, width), lambda i, chip_ref: (chip_ref[0], i, 0))),
        compiler_params=_cparams(("parallel",)),
    )(chip, w)


def _allreduce_small(name, vec):
    rows = -(-vec.shape[0] // (2 * SUBLANES)) * (2 * SUBLANES)
    hr = rows // 2
    padded = jnp.pad(vec, ((0, rows - vec.shape[0]), (0, 0)))

    def body(v_ref, o_ref, theirs, pair, by_chip, send_sems, recv_sems):
        x, y, c = _place()
        me = 2 * x + y
        sibling = (x, y, 1 - c)
        mine = pl.ds(pl.multiple_of(c * hr, SUBLANES), hr)
        other = pl.ds(pl.multiple_of((1 - c) * hr, SUBLANES), hr)

        swap = _remote(v_ref, theirs, send_sems, recv_sems, 0, sibling)
        swap.start()
        swap.wait()
        south = (c + jnp.zeros((1, 1), jnp.int32)) == 0
        pair[...] = jnp.where(south, v_ref[...], theirs[...]) + jnp.where(south, theirs[...], v_ref[...])

        by_chip[me] = pair[mine, :]
        sends = [_remote(by_chip.at[me], by_chip.at[me], send_sems, recv_sems, 1 + j, (cx, cy, c))
                 for j, (cx, cy) in enumerate(_other_chips(x, y))]
        for cp in sends:
            cp.start()
        for j, (cx, cy) in enumerate(_other_chips(x, y)):
            _remote(by_chip.at[me], by_chip.at[2 * cx + cy], send_sems, recv_sems, 1 + j, (cx, cy, c)).wait_recv()
        for cp in sends:
            cp.wait_send()
        total = by_chip[0]
        for s in range(1, N_CHIPS):
            total = total + by_chip[s]

        o_ref[mine, :] = total
        back = _remote(o_ref.at[mine], o_ref.at[mine], send_sems, recv_sems, 4, sibling)
        back.start()
        _remote(o_ref.at[other], o_ref.at[other], send_sems, recv_sems, 4, sibling).wait_recv()
        back.wait_send()

    out = pl.pallas_call(
        body, name=name, out_shape=jax.ShapeDtypeStruct((rows, LANES), F32),
        in_specs=[VMEM_SPEC], out_specs=VMEM_SPEC,
        scratch_shapes=[pltpu.VMEM((rows, LANES), F32), pltpu.VMEM((rows, LANES), F32),
                        pltpu.VMEM((N_CHIPS, hr, LANES), F32), pltpu.SemaphoreType.DMA((5,)),
                        pltpu.SemaphoreType.DMA((5,))],
        compiler_params=pltpu.CompilerParams(vmem_limit_bytes=VMEM_LIMIT),
    )(padded)
    return out[:vec.shape[0]]


def _rs_pair_add(name, place, grads, partner, split="rows"):
    _, half_rows, width = partner.shape
    tr = _row_block(half_rows, width, 2, 16)
    nb = half_rows // tr
    if split == "rows":
        mine = pl.BlockSpec((None, tr, width), lambda s, i, pr: (s, pr[1] * nb + i, 0))
    else:
        mine = pl.BlockSpec((None, tr, width), lambda s, i, pr: (s, i, pr[1]))

    def body(place_ref, g_ref, p_ref, o_ref):
        o_ref[...] = (g_ref[...].astype(F32) + p_ref[...].astype(F32)).astype(BF16)

    return pl.pallas_call(
        body, name=name, out_shape=jax.ShapeDtypeStruct(partner.shape, BF16),
        grid_spec=pltpu.PrefetchScalarGridSpec(
            num_scalar_prefetch=1, grid=(N_CHIPS, nb),
            in_specs=[mine, pl.BlockSpec((None, tr, width), lambda s, i, pr: (s, i, 0))],
            out_specs=pl.BlockSpec((None, tr, width), lambda s, i, pr: (s, i, 0))),
        compiler_params=_cparams(("parallel", "parallel")),
    )(place, grads, partner)


def _rs_chip_add(name, place, mine, others, split="rows"):
    _, half_rows, width = mine.shape
    tr = _row_block(half_rows, width, 4, 16, budget=1 << 20)
    nb = half_rows // tr
    if split == "rows":
        out_shape, out_spec = (2 * half_rows, width), pl.BlockSpec((tr, width), lambda i, pr: (pr[1] * nb + i, 0))
    else:
        out_shape, out_spec = (half_rows, 2 * width), pl.BlockSpec((tr, width), lambda i, pr: (i, pr[1]))

    def body(place_ref, q_ref, r_ref, o_ref):
        acc = q_ref[...].astype(F32)
        for j in range(3):
            acc = acc + r_ref[j].astype(F32)
        o_ref[...] = acc

    return pl.pallas_call(
        body, name=name, out_shape=jax.ShapeDtypeStruct(out_shape, F32),
        grid_spec=pltpu.PrefetchScalarGridSpec(
            num_scalar_prefetch=1, grid=(nb,),
            in_specs=[pl.BlockSpec((None, tr, width), lambda i, pr: (pr[0], i, 0)),
                      pl.BlockSpec((3, tr, width), lambda i, pr: (0, i, 0))],
            out_specs=out_spec),
        compiler_params=_cparams(("parallel",)),
    )(place, mine, others)


WEIGHTS = ["meta_tokens", "a_norm_pre", "a_w_in", "a_conv_w", "a_conv_b", "a_dt_bias", "a_a_log", "a_d_skip",
           "a_gate_norm", "a_w_out", "a_norm_post", "kv_norm", "w_kv", "b_norm_pre", "b_w_q", "b_sinks", "b_w_o",
           "b_norm_post", "f_norm_pre", "f_w_up", "f_conv_w", "f_conv_b", "f_w_down", "f_norm_post"]
FULL_SHAPE = {
    "meta_tokens": (16, 1024), "a_norm_pre": (1, 1024), "a_w_in": (1, 1024, 5152), "a_conv_w": (1, 4, 3072),
    "a_conv_b": (1, 3072), "a_dt_bias": (1, 32), "a_a_log": (1, 32), "a_d_skip": (1, 32), "a_gate_norm": (1, 2048),
    "a_w_out": (1, 2048, 1024), "a_norm_post": (1, 1024), "kv_norm": (1024,), "w_kv": (1024, 512),
    "b_norm_pre": (1, 1024), "b_w_q": (1, 1024, 1024), "b_sinks": (1, 16), "b_w_o": (1, 1024, 1024),
    "b_norm_post": (1, 1024), "f_norm_pre": (2, 1024), "f_w_up": (2, 1024, 5632), "f_conv_w": (2, 3, 5632),
    "f_conv_b": (2, 5632), "f_w_down": (2, 2816, 1024), "f_norm_post": (2, 1024),
}
SHARD_AXIS = {
    "meta_tokens": 1, "a_norm_pre": 1, "a_w_in": 2, "a_conv_w": 2, "a_conv_b": 1, "a_dt_bias": None, "a_a_log": None,
    "a_d_skip": None, "a_gate_norm": 1, "a_w_out": 1, "a_norm_post": 1, "kv_norm": None, "w_kv": 0, "b_norm_pre": None,
    "b_w_q": 1, "b_sinks": None, "b_w_o": 1, "b_norm_post": None, "f_norm_pre": None, "f_w_up": 2, "f_conv_w": 2,
    "f_conv_b": None, "f_w_down": 1, "f_norm_post": None,
}
BIG = ["a_w_in", "a_w_out", "w_kv", "b_w_q", "b_w_o", "f_w_up", "f_w_down"]
SMALL = [n for n in WEIGHTS if n not in BIG]
SMALL_SHARDED = [n for n in SMALL if SHARD_AXIS[n] is not None]


def _shard_shape(name):
    shape = list(FULL_SHAPE[name])
    if SHARD_AXIS[name] is not None:
        shape[SHARD_AXIS[name]] //= N_CHIPS
    return tuple(shape)


def _numel(shape):
    return int(math.prod(shape))


SUBLANES = 8


def _packed_rows(shape):
    rows = -(-_numel(shape) // LANES)
    return -(-rows // SUBLANES) * SUBLANES


def _pack(arrays):
    parts = []
    for a in arrays:
        size, rows = _numel(a.shape), _packed_rows(a.shape)
        if size % LANES == 0:
            part = jnp.pad(a.reshape(size // LANES, LANES), ((0, rows - size // LANES), (0, 0)))
        else:
            part = jnp.pad(a.reshape(-1), (0, rows * LANES - size)).reshape(rows, LANES)
        parts.append(part)
    return jnp.concatenate(parts, axis=0)


def _unpack(packed, names, shape_of):
    out, off = {}, 0
    lead = packed.shape[:-2]
    for n in names:
        shape = tuple(shape_of(n))
        size, rows = _numel(shape), _packed_rows(shape)
        part = packed[..., off:off + rows, :]
        if size % LANES == 0:
            out[n] = part[..., :size // LANES, :].reshape(lead + shape)
        else:
            out[n] = part.reshape(lead + (rows * LANES,))[..., :size].reshape(lead + shape)
        off += rows
    return out


def _split_chips(name, full):
    ax = SHARD_AXIS[name]
    shape = full.shape
    cut = shape[:ax] + (N_CHIPS, shape[ax] // N_CHIPS) + shape[ax + 1:]
    return jnp.moveaxis(full.reshape(cut), ax, 0)


def _join_chips(name, stacked):
    ax = SHARD_AXIS[name]
    moved = jnp.moveaxis(stacked, 0, ax)
    shape = moved.shape
    return moved.reshape(shape[:ax] + (shape[ax] * shape[ax + 1],) + shape[ax + 2:])


def _as2d(a):
    return a.reshape(-1, a.shape[-1])


BUFFERS = [("a_w_in", "a_w_in", None), ("a_w_out", "a_w_out", None), ("w_kv", "w_kv", None),
           ("b_w_q", "b_w_q", None), ("b_w_o", "b_w_o", None), ("f_w_up0", "f_w_up", 0), ("f_w_up1", "f_w_up", 1),
           ("f_w_down0", "f_w_down", 0), ("f_w_down1", "f_w_down", 1)]


TRANSPOSED = ("a_w_in",)
SPLIT = {"a_w_in": "cols"}


def _local_shard(arrays, weight, layer):
    if weight in TRANSPOSED:
        return arrays[weight][0].T
    return _as2d(arrays[weight]) if layer is None else arrays[weight]


def _weight_from_gathered(weight, buf):
    if weight == "f_w_up":
        return buf
    return buf.reshape(N_CHIPS * buf.shape[1], buf.shape[2])


def _gathered_from_grad(weight, g):
    if weight == "f_w_up":
        return g
    return g.reshape(N_CHIPS, g.shape[0] // N_CHIPS, g.shape[1]).astype(BF16)


GATHER_SCHEDULE = {
    "a_in_main": [("ici", ["a_w_out"])],
    "a_conv": [("d2d", ["a_w_out"]), ("ici", ["f_w_down0"])],
    "a_ssd_prep": [("d2d", ["f_w_down0"]), ("ici_near", ["f_w_up0"])],
    "a_ssd": [("ici_far", ["f_w_up0"])],
    "a_gate": [("d2d", ["f_w_up0"]), ("ici", ["w_kv", "b_w_q", "b_w_o"])],
    "ffn0_up": [("d2d", ["w_kv", "b_w_q", "b_w_o"]), ("ici", ["f_w_down1"])],
    "ffn0_down": [("d2d", ["f_w_down1"])],
    "b_attn": [("ici", ["f_w_up1"])],
    "b_o": [("d2d", ["f_w_up1"])],
}
REDUCE_SCHEDULE = {
    "b_attn_bwd": [("all", ["f_w_down1", "f_w_up1", "b_w_o"])],
    "ffn0_conv_bwd": [("all", ["b_w_q", "w_kv", "f_w_down0"])],
    "a_ssd_bwd": [("all", ["f_w_up0", "a_w_out"])],
    "a_in_main_dx": [("near", ["a_w_in"])],
    "a_norm_bwd": [("far", ["a_w_in"])],
}
REDUCE_LAST = ("a_w_in",)
ICI_PEERS = {"ici": ALL_PEERS, "ici_near": NEAR_PEERS, "ici_far": FAR_PEERS,
             "all": ALL_PEERS, "near": NEAR_PEERS, "far": FAR_PEERS}
PAIR_SCHEDULE = {
    "b_o_dx": ["f_w_down1", "f_w_up1", "b_w_o"],
    "ffn0_down_dx": ["b_w_q", "w_kv", "f_w_down0"],
    "a_out_dx": ["f_w_up0", "a_w_out"],
    "a_in_dt_dx": ["a_w_in"],
}
SWAP_SCHEDULE = {"a_in_main_dw": ["f_w_down1", "f_w_up1", "b_w_o", "b_w_q", "w_kv", "f_w_down0", "f_w_up0", "a_w_out"]}


def _buffer_of(weight, layer):
    return weight if layer is None else f"{weight}{layer}"


class _Pipeline:
    def __init__(self, place, slots):
        self.place = place
        self.slots = dict(slots)
        self.running = []
        self.grads = {}
        self.theirs = {}
        self.partials = {}
        self.peers = {}
        self.reduced = {}

    def _collect(self):
        for step, buffers, table in self.running:
            table.update(zip(buffers, step.results))
        self.running = []

    @staticmethod
    def _splits(buffers):
        return [SPLIT.get(b, "rows") for b in buffers]

    def gather_now(self, name, buffers):
        step = _step_gather_full([self.slots[b] for b in buffers], self._splits(buffers))
        _run_steps(name, [step])
        self.slots.update(zip(buffers, step.results))

    def weight(self, name, layer=None):
        self._collect()
        return _weight_from_gathered(name, self.slots[_buffer_of(name, layer)])

    def grad(self, name, layer, g):
        self.grads[_buffer_of(name, layer)] = _gathered_from_grad(name, g)

    def steps(self, kernel):
        self._collect()
        steps = []
        for phase, buffers in GATHER_SCHEDULE.get(kernel, []):
            bufs, splits = [self.slots[b] for b in buffers], self._splits(buffers)
            step = (_step_gather_d2d(bufs, splits) if phase == "d2d"
                    else _step_gather_ici(bufs, splits, ICI_PEERS[phase]))
            self.running.append((step, buffers, self.slots))
            steps.append(step)
        buffers = PAIR_SCHEDULE.get(kernel)
        if buffers:
            step = _step_pair_exchange([self.grads[b] for b in buffers], self._splits(buffers))
            self.running.append((step, buffers, self.theirs))
            steps.append(step)
        for part, buffers in REDUCE_SCHEDULE.get(kernel, []):
            for b in buffers:
                if b not in self.partials:
                    self.partials[b] = _rs_pair_add("reduce_pair_add_" + b, self.place, self.grads[b], self.theirs[b],
                                                    SPLIT.get(b, "rows"))
            started = [self.peers[b] for b in buffers] if all(b in self.peers for b in buffers) else None
            step = _step_chip_exchange([self.partials[b] for b in buffers], ICI_PEERS[part], into=started)
            self.running.append((step, buffers, self.peers))
            steps.append(step)
        buffers = SWAP_SCHEDULE.get(kernel)
        if buffers:
            step = self._swap_step(buffers)
            self.running.append((step, buffers, self.reduced))
            steps.append(step)
        return steps

    def _swap_step(self, buffers):
        halves = [_rs_chip_add("reduce_chip_add_" + b, self.place, self.partials[b], self.peers[b], SPLIT.get(b, "rows"))
                  for b in buffers]
        return _step_pair_gather(halves, self._splits(buffers))

    def shard(self, buffer):
        self._collect()
        return self.reduced[buffer]

    def finish(self):
        self._collect()
        rest = [b for b, _, _ in BUFFERS if b not in self.reduced]
        step = self._swap_step(rest)
        _run_steps("reduce_pair_gather", [step])
        self.reduced.update(zip(rest, step.results))


def kernel(x, meta_tokens, a_norm_pre, a_w_in, a_conv_w, a_conv_b, a_dt_bias, a_a_log, a_d_skip, a_gate_norm, a_w_out, a_norm_post, kv_norm, w_kv, b_norm_pre, b_w_q, b_sinks, b_w_o, b_norm_post, f_norm_pre, f_w_up, f_conv_w, f_conv_b, f_w_down, f_norm_post, loss_target, m_meta_tokens, m_a_norm_pre, m_a_w_in, m_a_conv_w, m_a_conv_b, m_a_dt_bias, m_a_a_log, m_a_d_skip, m_a_gate_norm, m_a_w_out, m_a_norm_post, m_kv_norm, m_w_kv, m_b_norm_pre, m_b_w_q, m_b_sinks, m_b_w_o, m_b_norm_post, m_f_norm_pre, m_f_w_up, m_f_conv_w, m_f_conv_b, m_f_w_down, m_f_norm_post, v_meta_tokens, v_a_norm_pre, v_a_w_in, v_a_conv_w, v_a_conv_b, v_a_dt_bias, v_a_a_log, v_a_d_skip, v_a_gate_norm, v_a_w_out, v_a_norm_post, v_kv_norm, v_w_kv, v_b_norm_pre, v_b_w_q, v_b_sinks, v_b_w_o, v_b_norm_post, v_f_norm_pre, v_f_w_up, v_f_conv_w, v_f_conv_b, v_f_w_down, v_f_norm_post):
    given = dict(locals())
    w = {n: given[n] for n in WEIGHTS}
    mom = {n: given["m_" + n] for n in WEIGHTS}
    var = {n: given["v_" + n] for n in WEIGHTS}
    chip = 2 * lax.axis_index("x") + lax.axis_index("y")
    core = lax.axis_index("c")
    place = jnp.stack([chip, core]).astype(jnp.int32)

    small_all = _allgather_small("gather_small", _pack([w[n] for n in SMALL_SHARDED]))
    small_parts = _unpack(small_all, SMALL_SHARDED, _shard_shape)
    slots = {b: _cast_into_slot("cast_" + b, place, _local_shard(w, wn, layer), layer) for b, wn, layer in BUFFERS}
    pipeline = _Pipeline(place, slots)
    pipeline.gather_now("gather_first", ["a_w_in"])
    p = {}
    for n in SMALL:
        p[n] = _join_chips(n, small_parts[n]) if n in SMALL_SHARDED else w[n]
    p["a_conv_w"] = p["a_conv_w"][0]
    p["kv_norm"] = p["kv_norm"].reshape(1, D_MODEL)

    loss_local, grad_x, g = _local_step(x[0], loss_target[0], p, pipeline)

    small_sum = _allreduce_small("reduce_small", _pack([g[n].reshape(FULL_SHAPE[n]) for n in SMALL]
                                                       + [loss_local.reshape(1, 1)]))
    small_red = _unpack(small_sum, SMALL + ["loss"], lambda n: (1, 1) if n == "loss" else FULL_SHAPE[n])
    loss = small_red["loss"][0, 0]
    grads = {}
    for n in SMALL:
        if SHARD_AXIS[n] is None:
            grads[n] = small_red[n]
        else:
            grads[n] = lax.dynamic_index_in_dim(_split_chips(n, small_red[n]), chip, 0, keepdims=False)

    delta, new_m, new_v = {}, {}, {}
    for n in sorted(BIG, key=lambda name: name in REDUCE_LAST):
        shape = _shard_shape(n)
        if n in REDUCE_LAST:
            pipeline.finish()
        if n in TRANSPOSED:
            g2d = pipeline.shard(n)
            w2d, m2d, v2d = (arrays[n][0].T for arrays in (w, mom, var))
            back = lambda a: a.T.reshape(shape)
        else:
            g2d = (jnp.concatenate([pipeline.shard(n + "0"), pipeline.shard(n + "1")], axis=0)
                   if n in ("f_w_up", "f_w_down") else pipeline.shard(n))
            w2d, m2d, v2d = (_as2d(arrays[n]) for arrays in (w, mom, var))
            back = lambda a: a.reshape(shape)
        d, m2, v2 = _adamw("adamw_" + n, w2d, g2d, m2d, v2d, steps=pipeline.steps("adamw_" + n))
        grads[n], delta[n], new_m[n], new_v[n] = back(g2d), back(d), back(m2), back(v2)
    packed = [_pack([src[n].reshape(_shard_shape(n)) for n in SMALL]) for src in (w, grads, mom, var)]
    outs = _adamw("adamw_small", *packed)
    for dst, flat in zip((delta, new_m, new_v), outs):
        dst.update(_unpack(flat, SMALL, _shard_shape))

    return (loss, grad_x[None], *[grads[n].reshape(_shard_shape(n)) for n in WEIGHTS],
            *[delta[n] for n in WEIGHTS], *[new_m[n] for n in WEIGHTS], *[new_v[n] for n in WEIGHTS])
```

```python
import functools
import math

import jax
import jax.numpy as jnp
from jax import lax
from jax.experimental import pallas as pl
from jax.experimental.pallas import tpu as pltpu

F32, BF16 = jnp.float32, jnp.bfloat16
MESH = pl.DeviceIdType.MESH

D_MODEL = 1024
N_META = 16
CHUNK = 128
PAD_ROWS = CHUNK - N_META
D_INNER = 2048
D_STATE = 128
N_GROUPS = 4
HEADS_PER_GROUP = 8
SSM_HEADS = 32
HEAD_DIM = 64
D_BC = N_GROUPS * D_STATE
D_XBC = D_INNER + 2 * D_BC
D_MAIN = D_INNER + D_XBC
D_IN_PROJ = D_MAIN + SSM_HEADS
GROUP_W = HEADS_PER_GROUP * HEAD_DIM
SSM_CONV = 4
D_FF = 2816
FFN_CONV = 3
N_Q_HEADS = 16
N_KV_HEADS = 4
D_KV = 256
ATTN_SCALE = 1.0 / math.sqrt(HEAD_DIM)
RMS_EPS = 1e-6
NEG_INF = -1e30
LANES = 128
VMEM_LIMIT = 48 * 1024 * 1024

ADAM_LR, ADAM_B1, ADAM_B2, ADAM_EPS, ADAM_WD, ADAM_STEP = 0.001, 0.9, 0.999, 1e-08, 0.01, 10

N_CHIPS = 4


def _cparams(sem=None):
    return pltpu.CompilerParams(dimension_semantics=sem, vmem_limit_bytes=VMEM_LIMIT)


def _tile(n, cands=(512, 256, 128)):
    for t in cands:
        if n % t == 0:
            return t
    return n


def _row_tile(rows, width):
    for t in (544, 272):
        if rows % t == 0 and t * width * 4 <= (3 << 20):
            return t
    return 128


def _rows_mask(i, tm):
    rows = i * tm + lax.broadcasted_iota(jnp.int32, (tm, 1), 0)
    return rows >= PAD_ROWS


def _dot(a, b):
    return jnp.dot(a, b, preferred_element_type=F32)


def _dot_nt(a, b):
    return lax.dot_general(a, b, (((1,), (1,)), ((), ())), preferred_element_type=F32)


def _dot_tn(a, b):
    return lax.dot_general(a, b, (((0,), (0,)), ((), ())), preferred_element_type=F32)


def _sigmoid(x):
    return 1.0 / (1.0 + jnp.exp(-x))


def _place():
    return lax.axis_index("x"), lax.axis_index("y"), lax.axis_index("c")


def _other_chips(x, y):
    return [(1 - x, y), (x, 1 - y), (1 - x, 1 - y)]


class _Step:
    def __init__(self, ins, outs, aliases, n_sems, start, finish):
        self.ins, self.outs, self.aliases, self.n_sems = list(ins), list(outs), dict(aliases), n_sems
        self.start, self.finish = start, finish
        self.results = None


def _like(a):
    return jax.ShapeDtypeStruct(a.shape, a.dtype)


def _remote(src, dst, send_sems, recv_sems, k, device):
    return pltpu.make_async_remote_copy(src, dst, send_sems.at[k], recv_sems.at[k], device_id=device, device_id_type=MESH)


def _half(ref, split, which, lead=()):
    if split == "rows":
        hr = ref.shape[-2] // 2
        return ref.at[lead + (pl.ds(which * hr, hr),)]
    hc = ref.shape[-1] // 2
    return ref.at[lead + (slice(None), pl.ds(which * hc, hc))]


def _splits(bufs, splits):
    return list(splits) if splits is not None else ["rows"] * len(bufs)


ALL_PEERS = (0, 1, 2)
NEAR_PEERS = (0, 1)
FAR_PEERS = (2,)


def _step_gather_ici(bufs, splits=None, peers=ALL_PEERS):
    splits = _splits(bufs, splits)

    def copies(outs, send_sems, recv_sems, received):
        x, y, c = _place()
        me = 2 * x + y
        for k, o in enumerate(outs):
            for j, (cx, cy) in enumerate(_other_chips(x, y)):
                if j in peers:
                    part = _half(o, splits[k], c, (2 * cx + cy if received else me,))
                    yield _remote(part, part, send_sems, recv_sems, 3 * k + j, (cx, cy, c))

    def start(ins, outs, send_sems, recv_sems):
        for cp in copies(outs, send_sems, recv_sems, False):
            cp.start()

    def finish(ins, outs, send_sems, recv_sems):
        for cp in copies(outs, send_sems, recv_sems, True):
            cp.wait_recv()
        for cp in copies(outs, send_sems, recv_sems, False):
            cp.wait_send()

    return _Step(bufs, [_like(b) for b in bufs], {k: k for k in range(len(bufs))}, 3 * len(bufs), start, finish)


def _step_gather_d2d(bufs, splits=None):
    splits = _splits(bufs, splits)

    def copies(outs, send_sems, recv_sems, received):
        x, y, c = _place()
        for k, o in enumerate(outs):
            for j, (cx, cy) in enumerate(_other_chips(x, y)):
                part = _half(o, splits[k], 1 - c if received else c, (2 * cx + cy,))
                yield _remote(part, part, send_sems, recv_sems, 3 * k + j, (x, y, 1 - c))

    def start(ins, outs, send_sems, recv_sems):
        for cp in copies(outs, send_sems, recv_sems, False):
            cp.start()

    def finish(ins, outs, send_sems, recv_sems):
        for cp in copies(outs, send_sems, recv_sems, True):
            cp.wait_recv()
        for cp in copies(outs, send_sems, recv_sems, False):
            cp.wait_send()

    return _Step(bufs, [_like(b) for b in bufs], {k: k for k in range(len(bufs))}, 3 * len(bufs), start, finish)


def _step_gather_full(bufs, splits=None):
    n = len(bufs)
    splits = _splits(bufs, splits)

    def ici(outs, send_sems, recv_sems, received):
        x, y, c = _place()
        me = 2 * x + y
        for k, o in enumerate(outs):
            for j, (cx, cy) in enumerate(_other_chips(x, y)):
                part = _half(o, splits[k], c, (2 * cx + cy if received else me,))
                yield _remote(part, part, send_sems, recv_sems, 3 * k + j, (cx, cy, c))

    def d2d(outs, send_sems, recv_sems, received):
        x, y, c = _place()
        for k, o in enumerate(outs):
            for j, (cx, cy) in enumerate(_other_chips(x, y)):
                part = _half(o, splits[k], 1 - c if received else c, (2 * cx + cy,))
                yield _remote(part, part, send_sems, recv_sems, 3 * n + 3 * k + j, (x, y, 1 - c))

    def start(ins, outs, send_sems, recv_sems):
        for cp in ici(outs, send_sems, recv_sems, False):
            cp.start()

    def finish(ins, outs, send_sems, recv_sems):
        for arrived, onward in zip(ici(outs, send_sems, recv_sems, True), d2d(outs, send_sems, recv_sems, False)):
            arrived.wait_recv()
            onward.start()
        for cp in d2d(outs, send_sems, recv_sems, True):
            cp.wait_recv()
        for cp in ici(outs, send_sems, recv_sems, False):
            cp.wait_send()
        for cp in d2d(outs, send_sems, recv_sems, False):
            cp.wait_send()

    return _Step(bufs, [_like(b) for b in bufs], {k: k for k in range(n)}, 6 * n, start, finish)


def _half_shape(shape, split):
    return shape[:-2] + ((shape[-2] // 2, shape[-1]) if split == "rows" else (shape[-2], shape[-1] // 2))


def _step_pair_exchange(grads, splits=None):
    splits = _splits(grads, splits)

    def copies(ins, outs, send_sems, recv_sems):
        x, y, c = _place()
        for k, (g, o) in enumerate(zip(ins, outs)):
            yield _remote(_half(g, splits[k], 1 - c, (slice(None),)), o, send_sems, recv_sems, k, (x, y, 1 - c))

    def start(ins, outs, send_sems, recv_sems):
        for cp in copies(ins, outs, send_sems, recv_sems):
            cp.start()

    def finish(ins, outs, send_sems, recv_sems):
        for cp in copies(ins, outs, send_sems, recv_sems):
            cp.wait()

    outs = [jax.ShapeDtypeStruct(_half_shape(g.shape, s), g.dtype) for g, s in zip(grads, splits)]
    return _Step(grads, outs, {}, len(grads), start, finish)


def _step_chip_exchange(partials, peers=ALL_PEERS, into=None):
    n = len(partials)

    def copies(ins, outs, send_sems, recv_sems):
        x, y, c = _place()
        for k, (q, o) in enumerate(zip(ins[:n], outs)):
            for j, (cx, cy) in enumerate(_other_chips(x, y)):
                if j in peers:
                    yield _remote(q.at[2 * cx + cy], o.at[j], send_sems, recv_sems, 3 * k + j, (cx, cy, c))

    def start(ins, outs, send_sems, recv_sems):
        for cp in copies(ins, outs, send_sems, recv_sems):
            cp.start()

    def finish(ins, outs, send_sems, recv_sems):
        for cp in copies(ins, outs, send_sems, recv_sems):
            cp.wait()

    outs = [jax.ShapeDtypeStruct((3,) + q.shape[1:], q.dtype) for q in partials]
    if into is None:
        return _Step(partials, outs, {}, 3 * n, start, finish)
    return _Step(list(partials) + list(into), outs, {n + k: k for k in range(n)}, 3 * n, start, finish)


def _step_pair_gather(shards, splits=None):
    splits = _splits(shards, splits)

    def copies(outs, send_sems, recv_sems, received):
        x, y, c = _place()
        for k, o in enumerate(outs):
            part = _half(o, splits[k], 1 - c if received else c)
            yield _remote(part, part, send_sems, recv_sems, k, (x, y, 1 - c))

    def start(ins, outs, send_sems, recv_sems):
        for cp in copies(outs, send_sems, recv_sems, False):
            cp.start()

    def finish(ins, outs, send_sems, recv_sems):
        for cp in copies(outs, send_sems, recv_sems, True):
            cp.wait_recv()
        for cp in copies(outs, send_sems, recv_sems, False):
            cp.wait_send()

    return _Step(shards, [_like(s) for s in shards], {k: k for k in range(len(shards))}, len(shards), start, finish)


def _call(body, *, name, out_shape, grid, in_specs, out_specs, operands, scratch_shapes=(), semantics=None, steps=()):
    single = not isinstance(out_shape, (tuple, list))
    out_shapes = [out_shape] if single else list(out_shape)
    out_spec_list = [out_specs] if single else list(out_specs)
    steps = list(steps)
    if not steps:
        res = pl.pallas_call(body, name=name, out_shape=out_shapes, grid=grid, in_specs=list(in_specs),
                             out_specs=out_spec_list, scratch_shapes=list(scratch_shapes),
                             compiler_params=_cparams(semantics))(*operands)
        return res[0] if single else res
    n_in, n_out, n_scr = len(operands), len(out_shapes), len(scratch_shapes)
    x_in = [a for s in steps for a in s.ins]
    x_out = [o for s in steps for o in s.outs]
    aliases, in_off, out_off = {}, 0, 0
    for s in steps:
        for i, o in s.aliases.items():
            aliases[n_in + in_off + i] = n_out + out_off + o
        in_off += len(s.ins)
        out_off += len(s.outs)
    sems = []
    for s in steps:
        sems += [pltpu.SemaphoreType.DMA((s.n_sems,)), pltpu.SemaphoreType.DMA((s.n_sems,))]
    any_spec = pl.BlockSpec(memory_space=pl.ANY)

    def carried(*refs):
        pos = 0
        ins = refs[pos:pos + n_in]; pos += n_in
        xi = refs[pos:pos + len(x_in)]; pos += len(x_in)
        outs = refs[pos:pos + n_out]; pos += n_out
        xo = refs[pos:pos + len(x_out)]; pos += len(x_out)
        scr = refs[pos:pos + n_scr]; pos += n_scr
        sem_refs = refs[pos:]

        def each(action):
            i0 = o0 = 0
            for k, s in enumerate(steps):
                getattr(s, action)(xi[i0:i0 + len(s.ins)], xo[o0:o0 + len(s.outs)], sem_refs[2 * k], sem_refs[2 * k + 1])
                i0 += len(s.ins)
                o0 += len(s.outs)

        if grid:
            first = functools.reduce(jnp.logical_and, [pl.program_id(d) == 0 for d in range(len(grid))])
            last = functools.reduce(jnp.logical_and, [pl.program_id(d) == grid[d] - 1 for d in range(len(grid))])
            pl.when(first)(lambda: each("start"))
            body(*ins, *outs, *scr)
            pl.when(last)(lambda: each("finish"))
        else:
            each("start")
            body(*ins, *outs, *scr)
            each("finish")

    res = pl.pallas_call(
        carried, name=name, out_shape=out_shapes + x_out, grid=grid,
        in_specs=list(in_specs) + [any_spec] * len(x_in), out_specs=out_spec_list + [any_spec] * len(x_out),
        scratch_shapes=list(scratch_shapes) + sems, input_output_aliases=aliases,
        compiler_params=_cparams(None if semantics is None else ("arbitrary",) * len(grid)),
    )(*operands, *x_in)
    o0 = n_out
    for s in steps:
        s.results = list(res[o0:o0 + len(s.outs)])
        o0 += len(s.outs)
    return res[0] if single else tuple(res[:n_out])


def _run_steps(name, steps):
    _call(lambda: None, name=name, out_shape=[], grid=(), in_specs=[], out_specs=[], operands=[], steps=steps)
    return [s.results for s in steps]


def _mm(name, a, b, mode, out_dtype=F32, acc=None, b_colblock=0, k_rows=None, out_rows=None, steps=()):
    resident_bytes = 8 << 20
    if mode == "nn":
        m, k = a.shape
        n = b.shape[1]
        tm = m
        while tm * k * 2 > resident_bytes and tm % 32 == 0:
            tm //= 2
        tn = _tile(n)
        grid = (m // tm, n // tn)
        in_specs = [pl.BlockSpec((tm, k), lambda i, j: (i, 0)), pl.BlockSpec((k, tn), lambda i, j: (0, j))]
        out_shape, out_block = (m, n), (tm, tn)
    elif mode == "nt":
        m, n = a.shape
        k = k_rows or b.shape[0]
        tm = m
        while tm * n * 2 > resident_bytes and tm % 32 == 0:
            tm //= 2
        tk = _tile(k)
        grid = (m // tm, k // tk)
        in_specs = [pl.BlockSpec((tm, n), lambda i, j: (i, 0)), pl.BlockSpec((tk, n), lambda i, j: (j, b_colblock))]
        out_shape, out_block = (m, k), (tm, tk)
    else:
        m, k = a.shape
        n = b.shape[1]
        tk, tn = _tile(k), (n if m * n * 2 <= resident_bytes else _tile(n))
        grid = (k // tk, n // tn)
        in_specs = [pl.BlockSpec((m, tk), lambda i, j: (0, i)), pl.BlockSpec((m, tn), lambda i, j: (0, j))]
        out_shape, out_block = (out_rows or k, n), (tk, tn)
    out_spec = pl.BlockSpec(out_block, lambda i, j: (i, j))
    has_acc = acc is not None

    def body(*refs):
        a_ref, b_ref = refs[0], refs[1]
        o_ref = refs[-1]
        av, bv = a_ref[...], b_ref[...]
        if mode == "nn":
            r = _dot(av, bv)
        elif mode == "nt":
            r = _dot_nt(av, bv)
        else:
            r = _dot_tn(av, bv)
        if has_acc:
            r = r + refs[2][...]
        o_ref[...] = r.astype(o_ref.dtype)

    operands = [a, b]
    if has_acc:
        in_specs = in_specs + [out_spec]
        operands.append(acc)
    return _call(body, name=name, out_shape=jax.ShapeDtypeStruct(out_shape, out_dtype), grid=grid, in_specs=in_specs,
                 out_specs=out_spec, operands=operands, semantics=("parallel", "parallel"), steps=steps)


def _tn_rows_into(name, a, b, into, row0, nrows):
    m, k = a.shape
    n = b.shape[1]

    def body(a_ref, b_ref, into_ref, o_ref):
        o_ref[...] = _dot_tn(a_ref[...], b_ref[...])[0:nrows].astype(o_ref.dtype)

    return pl.pallas_call(
        body, name=name, out_shape=jax.ShapeDtypeStruct(into.shape, into.dtype), grid=(1,),
        in_specs=[pl.BlockSpec((m, k), lambda i: (0, 0)), pl.BlockSpec((m, n), lambda i: (0, 0)),
                  pl.BlockSpec(memory_space=pl.ANY)],
        out_specs=pl.BlockSpec((nrows, n), lambda i: (row0 // nrows, 0)),
        input_output_aliases={2: 0}, compiler_params=_cparams(("arbitrary",)),
    )(a, b, into)


def _rms_fwd(name, h, w):
    rows, width = h.shape
    tm = _row_tile(rows, width)

    def body(h_ref, w_ref, o_ref):
        x = h_ref[...]
        r = lax.rsqrt(jnp.mean(x * x, axis=-1, keepdims=True) + RMS_EPS)
        o_ref[...] = (x * r * w_ref[...]).astype(BF16)

    return pl.pallas_call(
        body, name=name, out_shape=jax.ShapeDtypeStruct((rows, width), BF16), grid=(rows // tm,),
        in_specs=[pl.BlockSpec((tm, width), lambda i: (i, 0)), pl.BlockSpec((1, width), lambda i: (0, 0))],
        out_specs=pl.BlockSpec((tm, width), lambda i: (i, 0)), compiler_params=_cparams(("parallel",)),
    )(h, w)


def _resid_norm_fwd(name, h, pre, w, next_norms=()):
    rows, width = h.shape
    tm = _row_tile(rows, width)
    n_next = len(next_norms)

    def body(*refs):
        h_ref, p_ref, w_ref = refs[:3]
        v_refs = refs[3:3 + n_next]
        o_ref = refs[3 + n_next]
        n_refs = refs[4 + n_next:]
        p = p_ref[...]
        r = lax.rsqrt(jnp.mean(p * p, axis=-1, keepdims=True) + RMS_EPS)
        x = h_ref[...] + jnp.where(_rows_mask(pl.program_id(0), tm), p * r * w_ref[...], 0.0)
        o_ref[...] = x
        if n_next:
            rx = lax.rsqrt(jnp.mean(x * x, axis=-1, keepdims=True) + RMS_EPS)
            for v_ref, n_ref in zip(v_refs, n_refs):
                n_ref[...] = (x * rx * v_ref[...]).astype(BF16)

    row_spec = pl.BlockSpec((tm, width), lambda i: (i, 0))
    vec_spec = pl.BlockSpec((1, width), lambda i: (0, 0))
    outs = pl.pallas_call(
        body, name=name,
        out_shape=[jax.ShapeDtypeStruct((rows, width), F32)] + [jax.ShapeDtypeStruct((rows, width), BF16)] * n_next,
        grid=(rows // tm,), in_specs=[row_spec, row_spec, vec_spec] + [vec_spec] * n_next,
        out_specs=[row_spec] * (1 + n_next), compiler_params=_cparams(("parallel",)),
    )(h, pre, w, *next_norms)
    return outs[0], list(outs[1:])


def _resid_norm_loss(name, h, pre, w, target):
    rows, width = h.shape

    def body(h_ref, p_ref, w_ref, t_ref, dh_ref, loss_ref, dp_ref, dw_ref):
        i = pl.program_id(0)
        p = p_ref[...]
        r = lax.rsqrt(jnp.mean(p * p, axis=-1, keepdims=True) + RMS_EPS)
        x = h_ref[...] + p * r * w_ref[...]
        real = (i + jnp.zeros((CHUNK, 1), jnp.int32)) >= 1
        diff = jnp.where(real, x - t_ref[...], 0.0)
        dh = diff * (1.0 / D_MODEL)
        dh_ref[...] = dh
        dp, dw_rows = _rms_bwd(dh, p, w_ref[...])
        dp_ref[...] = dp.astype(BF16)

        @pl.when(i == 0)
        def _():
            loss_ref[...] = jnp.zeros_like(loss_ref)
            dw_ref[...] = jnp.zeros_like(dw_ref)

        loss_ref[...] += jnp.sum(diff * diff) * (0.5 / D_MODEL)
        dw_ref[...] += jnp.sum(dw_rows, axis=0, keepdims=True)

    blk = pl.BlockSpec((CHUNK, width), lambda i: (i, 0))
    vec_spec = pl.BlockSpec((1, width), lambda i: (0, 0))
    return pl.pallas_call(
        body, name=name,
        out_shape=(jax.ShapeDtypeStruct((rows, width), F32), jax.ShapeDtypeStruct((1, LANES), F32),
                   jax.ShapeDtypeStruct((rows, width), BF16), jax.ShapeDtypeStruct((1, width), F32)),
        grid=(rows // CHUNK,),
        in_specs=[blk, blk, vec_spec, pl.BlockSpec((CHUNK, width), lambda i: (jnp.maximum(i - 1, 0), 0))],
        out_specs=(blk, pl.BlockSpec((1, LANES), lambda i: (0, 0)), blk, vec_spec),
        compiler_params=_cparams(("arbitrary",)),
    )(h, pre, w, target)


def _rms_bwd(dy, x, w):
    r = lax.rsqrt(jnp.mean(x * x, axis=-1, keepdims=True) + RMS_EPS)
    xhat = x * r
    dxhat = dy * w
    return r * (dxhat - xhat * jnp.mean(dxhat * xhat, axis=-1, keepdims=True)), dy * xhat


def _norm_bwd_add(name, dh, dhn, h, w, then=None, steps=()):
    rows, width = dh.shape
    tm = _row_tile(rows, width)
    fused = then is not None

    def body(*refs):
        dh_ref, dhn_ref, h_ref, w_ref = refs[:4]
        o_ref, dw_ref = refs[6:8] if fused else refs[4:6]
        i = pl.program_id(0)
        valid = _rows_mask(i, tm)
        dx, dw_rows = _rms_bwd(dhn_ref[...], h_ref[...], w_ref[...])
        dh_new = dh_ref[...] + jnp.where(valid, dx, 0.0)
        o_ref[...] = dh_new

        @pl.when(i == 0)
        def _():
            dw_ref[...] = jnp.zeros_like(dw_ref)

        dw_ref[...] += jnp.sum(dw_rows, axis=0, keepdims=True)
        if fused:
            p_ref, wp_ref, dp_ref, dwp_ref = refs[4], refs[5], refs[8], refs[9]
            dp, dwp_rows = _rms_bwd(jnp.where(valid, dh_new, 0.0), p_ref[...], wp_ref[...])
            dp_ref[...] = dp.astype(BF16)

            @pl.when(i == 0)
            def _():
                dwp_ref[...] = jnp.zeros_like(dwp_ref)

            dwp_ref[...] += jnp.sum(dwp_rows, axis=0, keepdims=True)

    row_spec = pl.BlockSpec((tm, width), lambda i: (i, 0))
    vec_spec = pl.BlockSpec((1, width), lambda i: (0, 0))
    row_f32, vec_f32 = jax.ShapeDtypeStruct((rows, width), F32), jax.ShapeDtypeStruct((1, width), F32)
    in_specs, operands = [row_spec, row_spec, row_spec, vec_spec], [dh, dhn, h, w]
    out_shape, out_specs = [row_f32, vec_f32], [row_spec, vec_spec]
    if fused:
        in_specs += [row_spec, vec_spec]
        operands += list(then)
        out_shape += [jax.ShapeDtypeStruct((rows, width), BF16), vec_f32]
        out_specs += [row_spec, vec_spec]
    return _call(body, name=name, out_shape=out_shape, grid=(rows // tm,), in_specs=in_specs, out_specs=out_specs,
                 operands=operands, semantics=("arbitrary",), steps=steps)


def _shift_down(x, s, rows):
    return pltpu.roll(x, s, 0) if s else x


def _shift_up(x, s, rows):
    return pltpu.roll(x, rows - s, 0) if s else x


def _conv4_fwd(name, zx, cw, cb, steps=()):
    rows = zx.shape[0]
    off = D_INNER // LANES

    def body(x_ref, w_ref, b_ref, o_ref):
        x = x_ref[...]
        acc = b_ref[...] + w_ref[pl.ds(SSM_CONV - 1, 1), :] * x
        for s in range(1, SSM_CONV):
            acc = acc + w_ref[pl.ds(SSM_CONV - 1 - s, 1), :] * _shift_down(x, s, rows)
        valid = lax.broadcasted_iota(jnp.int32, (rows, 1), 0) >= PAD_ROWS
        o_ref[...] = jnp.where(valid, acc * _sigmoid(acc), 0.0)

    return _call(
        body, name=name, out_shape=jax.ShapeDtypeStruct((rows, D_XBC), F32), grid=(D_XBC // LANES,),
        in_specs=[pl.BlockSpec((rows, LANES), lambda j: (0, j + off)),
                  pl.BlockSpec((SSM_CONV, LANES), lambda j: (0, j)),
                  pl.BlockSpec((1, LANES), lambda j: (0, j))],
        out_specs=pl.BlockSpec((rows, LANES), lambda j: (0, j)), operands=[zx, cw, cb],
        semantics=("parallel",), steps=steps)


def _conv4_bwd(name, zx, dout, cw, cb, col0):
    rows, width = dout.shape
    zoff = (D_INNER + col0) // LANES
    woff = col0 // LANES

    def body(x_ref, d_ref, w_ref, b_ref, dx_ref, dw_ref, db_ref):
        x = x_ref[...]
        shifted = [_shift_down(x, s, rows) for s in range(SSM_CONV)]
        acc = b_ref[...]
        for s in range(SSM_CONV):
            acc = acc + w_ref[pl.ds(SSM_CONV - 1 - s, 1), :] * shifted[s]
        sig = _sigmoid(acc)
        valid = lax.broadcasted_iota(jnp.int32, (rows, 1), 0) >= PAD_ROWS
        dpre = jnp.where(valid, d_ref[...] * sig * (1.0 + acc * (1.0 - sig)), 0.0)
        dx = w_ref[pl.ds(SSM_CONV - 1, 1), :] * dpre
        for s in range(1, SSM_CONV):
            dx = dx + w_ref[pl.ds(SSM_CONV - 1 - s, 1), :] * _shift_up(dpre, s, rows)
        dx_ref[...] = dx.astype(BF16)
        for s in range(SSM_CONV):
            dw_ref[pl.ds(SSM_CONV - 1 - s, 1), :] = jnp.sum(dpre * shifted[s], axis=0, keepdims=True)
        db_ref[...] = jnp.sum(dpre, axis=0, keepdims=True)

    return pl.pallas_call(
        body, name=name,
        out_shape=(jax.ShapeDtypeStruct((rows, width), BF16), jax.ShapeDtypeStruct((SSM_CONV, width), F32),
                   jax.ShapeDtypeStruct((1, width), F32)),
        grid=(width // LANES,),
        in_specs=[pl.BlockSpec((rows, LANES), lambda j: (0, j + zoff)),
                  pl.BlockSpec((rows, LANES), lambda j: (0, j)),
                  pl.BlockSpec((SSM_CONV, LANES), lambda j: (0, j + woff)),
                  pl.BlockSpec((1, LANES), lambda j: (0, j + woff))],
        out_specs=(pl.BlockSpec((rows, LANES), lambda j: (0, j)),
                   pl.BlockSpec((SSM_CONV, LANES), lambda j: (0, j)),
                   pl.BlockSpec((1, LANES), lambda j: (0, j))),
        compiler_params=_cparams(("parallel",)),
    )(zx, dout, cw, cb)


FFN_TILE = 2 * LANES


def _ffn_up_conv(name, hn, w_up, cw, cb, steps=()):
    rows, k = hn.shape
    chip_blocks = w_up.shape[2] // LANES
    half_blocks = D_FF // LANES
    nt = D_FF // FFN_TILE

    def weight_block(offset):
        return pl.BlockSpec((None, k, LANES), lambda j: ((2 * j + offset) // chip_blocks, 0, (2 * j + offset) % chip_blocks))

    def body(a_ref, g0, g1, v0, v1, wg_ref, wv_ref, bg_ref, bv_ref, upg_ref, upv_ref, act_ref):
        a = a_ref[...]
        g = _dot(a, jnp.concatenate([g0[...], g1[...]], axis=1))
        v = _dot(a, jnp.concatenate([v0[...], v1[...]], axis=1))
        upg_ref[...] = g
        upv_ref[...] = v
        ug, uv = bg_ref[...], bv_ref[...]
        for s in range(FFN_CONV):
            ug = ug + wg_ref[pl.ds(FFN_CONV - 1 - s, 1), :] * _shift_down(g, s, rows)
            uv = uv + wv_ref[pl.ds(FFN_CONV - 1 - s, 1), :] * _shift_down(v, s, rows)
        act_ref[...] = (ug * _sigmoid(ug) * uv).astype(BF16)

    col = pl.BlockSpec((rows, FFN_TILE), lambda j: (0, j))
    wsp = lambda shift: pl.BlockSpec((FFN_CONV, FFN_TILE), lambda j: (0, j + shift))
    bsp = lambda shift: pl.BlockSpec((1, FFN_TILE), lambda j: (0, j + shift))
    half = jax.ShapeDtypeStruct((rows, D_FF), F32)
    return _call(
        body, name=name, out_shape=(half, half, jax.ShapeDtypeStruct((rows, D_FF), BF16)), grid=(nt,),
        in_specs=[pl.BlockSpec((rows, k), lambda j: (0, 0)), weight_block(0), weight_block(1),
                  weight_block(half_blocks), weight_block(half_blocks + 1), wsp(0), wsp(nt), bsp(0), bsp(nt)],
        out_specs=(col, col, col), operands=[hn, w_up, w_up, w_up, w_up, cw, cw, cb, cb],
        semantics=("parallel",), steps=steps)


def _ffn_conv_bwd(name, up_g, up_v, dact, cw, cb, hn, w_up, steps=()):
    rows, k = hn.shape
    chip_blocks = w_up.shape[2] // LANES
    nt = D_FF // LANES

    def weight_block(shift):
        return pl.BlockSpec((None, k, LANES), lambda j: ((j + shift) // chip_blocks, 0, (j + shift) % chip_blocks))

    def body(g_ref, v_ref, d_ref, wg_ref, wv_ref, bg_ref, bv_ref, upg_ref, upv_ref, hn_ref,
             dwg_ref, dwv_ref, dbg_ref, dbv_ref, dhn_ref, dup_ref, acc, hn_scr, hnt_scr, dup_scr, sems):
        j = pl.program_id(0)
        hn_copy = pltpu.make_async_copy(hn_ref, hn_scr, sems.at[0])
        dhn_copy = pltpu.make_async_copy(acc, dhn_ref, sems.at[0])

        def dup_copy(step, half):
            block, slot = step + half * nt, 2 * (step % 2) + half
            cols = pl.ds(pl.multiple_of((block % chip_blocks) * LANES, LANES), LANES)
            return pltpu.make_async_copy(dup_scr.at[slot], dup_ref.at[block // chip_blocks, :, cols], sems.at[1 + slot])

        @pl.when(j == 0)
        def _():
            hn_copy.start()
            acc[...] = jnp.zeros_like(acc)
            hn_copy.wait()
            for r in range(0, rows, LANES):
                hnt_scr[:, r:r + LANES] = hn_scr[r:r + LANES, :].T

        @pl.when(j >= 2)
        def _():
            dup_copy(j - 2, 0).wait()
            dup_copy(j - 2, 1).wait()

        g, v = g_ref[...], v_ref[...]
        gs = [_shift_down(g, s, rows) for s in range(FFN_CONV)]
        vs = [_shift_down(v, s, rows) for s in range(FFN_CONV)]
        ug, uv = bg_ref[...], bv_ref[...]
        for s in range(FFN_CONV):
            ug = ug + wg_ref[pl.ds(FFN_CONV - 1 - s, 1), :] * gs[s]
            uv = uv + wv_ref[pl.ds(FFN_CONV - 1 - s, 1), :] * vs[s]
        sig = _sigmoid(ug)
        dsig = d_ref[...] * sig
        dup = []
        for dpre, src, w_ref, dw_ref, db_ref in (
                (dsig * uv * (1.0 + ug * (1.0 - sig)), gs, wg_ref, dwg_ref, dbg_ref),
                (dsig * ug, vs, wv_ref, dwv_ref, dbv_ref)):
            dx = w_ref[pl.ds(FFN_CONV - 1, 1), :] * dpre
            for s in range(1, FFN_CONV):
                dx = dx + w_ref[pl.ds(FFN_CONV - 1 - s, 1), :] * _shift_up(dpre, s, rows)
            dup.append(dx.astype(BF16))
            for s in range(FFN_CONV):
                dw_ref[pl.ds(FFN_CONV - 1 - s, 1), :] = jnp.sum(dpre * src[s], axis=0, keepdims=True)
            db_ref[...] = jnp.sum(dpre, axis=0, keepdims=True)
        dup = jnp.concatenate(dup, axis=1)
        acc[...] += _dot_nt(dup, jnp.concatenate([upg_ref[...], upv_ref[...]], axis=1))
        dw = _dot(hnt_scr[...], dup)
        slot = 2 * (j % 2)
        dup_scr[slot] = dw[:, :LANES].astype(BF16)
        dup_scr[slot + 1] = dw[:, LANES:].astype(BF16)
        dup_copy(j, 0).start()
        dup_copy(j, 1).start()

        @pl.when(j == nt - 1)
        def _():
            dhn_copy.start()
            for step in (j - 1, j):
                dup_copy(step, 0).wait()
                dup_copy(step, 1).wait()
            dhn_copy.wait()

    col = pl.BlockSpec((rows, LANES), lambda j: (0, j))
    wsp = lambda shift: pl.BlockSpec((FFN_CONV, LANES), lambda j: (0, j + shift))
    bsp = lambda shift: pl.BlockSpec((1, LANES), lambda j: (0, j + shift))
    any_spec = pl.BlockSpec(memory_space=pl.ANY)
    dw_shape = jax.ShapeDtypeStruct((FFN_CONV, D_FF), F32)
    db_shape = jax.ShapeDtypeStruct((1, D_FF), F32)
    return _call(
        body, name=name, grid=(nt,),
        out_shape=(dw_shape, dw_shape, db_shape, db_shape, jax.ShapeDtypeStruct((rows, k), F32),
                   jax.ShapeDtypeStruct(w_up.shape, BF16)),
        in_specs=[col, col, col, wsp(0), wsp(nt), bsp(0), bsp(nt), weight_block(0), weight_block(nt), any_spec],
        out_specs=(wsp(0), wsp(0), bsp(0), bsp(0), any_spec, any_spec),
        operands=[up_g, up_v, dact, cw, cw, cb, cb, w_up, w_up, hn],
        scratch_shapes=[pltpu.VMEM((rows, k), F32), pltpu.VMEM((rows, k), BF16), pltpu.VMEM((k, rows), BF16),
                        pltpu.VMEM((4, k, LANES), BF16), pltpu.SemaphoreType.DMA((5,))],
        semantics=("arbitrary",), steps=steps)


def _dt_fwd(name, dtr, bias):
    rows = dtr.shape[0]
    tm = _row_tile(rows, LANES)

    def body(d_ref, b_ref, o_ref):
        v = d_ref[...] + b_ref[...]
        sp = jnp.maximum(v, 0.0) + jnp.log1p(jnp.exp(-jnp.abs(v)))
        lane = lax.broadcasted_iota(jnp.int32, (tm, LANES), 1)
        ok = _rows_mask(pl.program_id(0), tm) & (lane < SSM_HEADS)
        o_ref[...] = jnp.where(ok, sp, 0.0)

    return pl.pallas_call(
        body, name=name, out_shape=jax.ShapeDtypeStruct((rows, LANES), F32), grid=(rows // tm,),
        in_specs=[pl.BlockSpec((tm, LANES), lambda i: (i, 0)), pl.BlockSpec((1, LANES), lambda i: (0, 0))],
        out_specs=pl.BlockSpec((tm, LANES), lambda i: (i, 0)), compiler_params=_cparams(("parallel",)),
    )(dtr, bias)


def _dt_bwd(name, ddt, dtr, bias):
    rows = dtr.shape[0]
    tm = _row_tile(rows, LANES)

    def body(g_ref, d_ref, b_ref, o_ref, db_ref):
        i = pl.program_id(0)
        lane = lax.broadcasted_iota(jnp.int32, (tm, LANES), 1)
        ok = _rows_mask(i, tm) & (lane < SSM_HEADS)
        dv = jnp.where(ok, g_ref[...] * _sigmoid(d_ref[...] + b_ref[...]), 0.0)
        o_ref[...] = dv.astype(BF16)

        @pl.when(i == 0)
        def _():
            db_ref[...] = jnp.zeros_like(db_ref)

        db_ref[...] += jnp.sum(dv, axis=0, keepdims=True)

    row_spec = pl.BlockSpec((tm, LANES), lambda i: (i, 0))
    vec_spec = pl.BlockSpec((1, LANES), lambda i: (0, 0))
    return pl.pallas_call(
        body, name=name,
        out_shape=(jax.ShapeDtypeStruct((rows, LANES), BF16), jax.ShapeDtypeStruct((1, LANES), F32)),
        grid=(rows // tm,), in_specs=[row_spec, row_spec, vec_spec], out_specs=(row_spec, vec_spec),
        compiler_params=_cparams(("arbitrary",)),
    )(ddt, dtr, bias)


def _gate_fwd(name, y, zx, w, steps=()):
    rows = y.shape[0]
    tm = _row_tile(rows, D_INNER)

    def body(y_ref, z_ref, w_ref, o_ref):
        z = z_ref[...]
        g = y_ref[...] * (z * _sigmoid(z))
        r = lax.rsqrt(jnp.mean(g * g, axis=-1, keepdims=True) + RMS_EPS)
        o_ref[...] = (g * r * w_ref[...]).astype(BF16)

    row_spec = pl.BlockSpec((tm, D_INNER), lambda i: (i, 0))
    return _call(
        body, name=name, out_shape=jax.ShapeDtypeStruct((rows, D_INNER), BF16), grid=(rows // tm,),
        in_specs=[row_spec, row_spec, pl.BlockSpec((1, D_INNER), lambda i: (0, 0))],
        out_specs=row_spec, operands=[y, zx, w], semantics=("parallel",), steps=steps)


def _gate_bwd(name, dyn, y, zx, w):
    rows = y.shape[0]
    tm = _row_tile(rows, D_INNER)

    def body(d_ref, y_ref, z_ref, w_ref, dy_ref, dz_ref, dw_ref):
        i = pl.program_id(0)
        z, yv = z_ref[...], y_ref[...]
        sig = _sigmoid(z)
        sz = z * sig
        g = yv * sz
        r = lax.rsqrt(jnp.mean(g * g, axis=-1, keepdims=True) + RMS_EPS)
        ghat = g * r
        dn = d_ref[...]
        dghat = dn * w_ref[...]
        dg = r * (dghat - ghat * jnp.mean(dghat * ghat, axis=-1, keepdims=True))
        dy_ref[...] = dg * sz
        dz_ref[...] = (dg * yv * sig * (1.0 + z * (1.0 - sig))).astype(BF16)

        @pl.when(i == 0)
        def _():
            dw_ref[...] = jnp.zeros_like(dw_ref)

        dw_ref[...] += jnp.sum(dn * ghat, axis=0, keepdims=True)

    row_spec = pl.BlockSpec((tm, D_INNER), lambda i: (i, 0))
    vec_spec = pl.BlockSpec((1, D_INNER), lambda i: (0, 0))
    return pl.pallas_call(
        body, name=name,
        out_shape=(jax.ShapeDtypeStruct((rows, D_INNER), F32), jax.ShapeDtypeStruct((rows, D_INNER), BF16),
                   jax.ShapeDtypeStruct((1, D_INNER), F32)),
        grid=(rows // tm,), in_specs=[row_spec, row_spec, row_spec, vec_spec],
        out_specs=(row_spec, row_spec, vec_spec), compiler_params=_cparams(("arbitrary",)),
    )(dyn, y, zx, w)


def _split3(x):
    hi = x.astype(BF16)
    r1 = x - hi.astype(F32)
    mid = r1.astype(BF16)
    lo = (r1 - mid.astype(F32)).astype(BF16)
    return hi, mid, lo


def _dot3_data_lhs(x, sel):
    sel16 = sel.astype(F32).astype(BF16)
    hi, mid, lo = _split3(x)
    return _dot(hi, sel16) + _dot(mid, sel16) + _dot(lo, sel16)


def _dot2_data_lhs(x, sel):
    sel16 = sel.astype(F32).astype(BF16)
    hi = x.astype(BF16)
    mid = (x - hi.astype(F32)).astype(BF16)
    return _dot(hi, sel16) + _dot(mid, sel16)


def _dot3_data_rhs(sel, x):
    sel16 = sel.astype(F32).astype(BF16)
    hi, mid, lo = _split3(x)
    return _dot(sel16, hi) + _dot(sel16, mid) + _dot(sel16, lo)


def _causal_masks():
    r = lax.broadcasted_iota(jnp.int32, (CHUNK, CHUNK), 0)
    c = lax.broadcasted_iota(jnp.int32, (CHUNK, CHUNK), 1)
    return r >= c, r <= c


def _expand_heads_matrix(g):
    k = lax.broadcasted_iota(jnp.int32, (LANES, GROUP_W), 0)
    j = lax.broadcasted_iota(jnp.int32, (LANES, GROUP_W), 1)
    return HEADS_PER_GROUP * g + jnp.right_shift(j, 6) == k


def _reduce_heads_matrix(g):
    j = lax.broadcasted_iota(jnp.int32, (GROUP_W, LANES), 0)
    k = lax.broadcasted_iota(jnp.int32, (GROUP_W, LANES), 1)
    return HEADS_PER_GROUP * g + jnp.right_shift(j, 6) == k


def _reduce_pair_matrix(g, p):
    j = lax.broadcasted_iota(jnp.int32, (LANES, LANES), 0)
    k = lax.broadcasted_iota(jnp.int32, (LANES, LANES), 1)
    return HEADS_PER_GROUP * g + 2 * p + jnp.right_shift(j, 6) == k


def _group_cols(ref, g, width):
    return ref.at[:, pl.ds(g * width, width)]


def _ssd_prep(name, dt, a128, steps=()):
    rows = dt.shape[0]
    nc = rows // CHUNK

    def body(dt_ref, a_ref, dte_ref, acs_ref, acst_ref):
        causal, _ = _causal_masks()
        dtv = dt_ref[...]
        acs = _dot3_data_rhs(causal, dtv) * a_ref[...]
        acst_ref[...] = acs.T[0:SSM_HEADS]
        for g in range(N_GROUPS):
            expand = _expand_heads_matrix(g)
            _group_cols(dte_ref, g, GROUP_W)[...] = _dot3_data_lhs(dtv, expand)
            _group_cols(acs_ref, g, GROUP_W)[...] = _dot3_data_lhs(acs, expand)

    blk = pl.BlockSpec((CHUNK, D_INNER), lambda c: (c, 0))
    shp = jax.ShapeDtypeStruct((rows, D_INNER), F32)
    return _call(
        body, name=name, out_shape=(shp, shp, jax.ShapeDtypeStruct((nc, SSM_HEADS, CHUNK), F32)), grid=(nc,),
        in_specs=[pl.BlockSpec((CHUNK, LANES), lambda c: (c, 0)), pl.BlockSpec((1, LANES), lambda c: (0, 0))],
        out_specs=(blk, blk, pl.BlockSpec((None, SSM_HEADS, CHUNK), lambda c: (c, 0, 0))),
        operands=[dt, a128], semantics=("parallel",), steps=steps)


def _ssd_common(x_ref, b_ref, c_ref, dte_ref, acs_ref):
    x = x_ref[...]
    dt_exp = dte_ref[...]
    acs_exp = acs_ref[...]
    tot_exp = acs_ref[pl.ds(CHUNK - 1, 1), :]
    xdt = x * dt_exp
    e_exp = jnp.exp(acs_exp)
    f_exp = jnp.exp(tot_exp - acs_exp)
    return _causal_masks(), x, dt_exp, acs_exp, tot_exp, xdt, e_exp, f_exp, b_ref[...], c_ref[...]


def _pair_decay(acs_pair, acs_row, e, causal):
    lane = lax.broadcasted_iota(jnp.int32, (CHUNK, LANES), 1)
    mine = (lane < HEAD_DIM) if e == 0 else (lane >= HEAD_DIM)
    a_l = jnp.where(mine, acs_pair, pltpu.roll(acs_pair, HEAD_DIM, 1))
    seg = a_l - acs_row
    dm = jnp.where(causal[0], jnp.exp(jnp.minimum(seg, 0.0)), 0.0)
    dmt = jnp.where(causal[1], jnp.exp(jnp.minimum(-seg, 0.0)), 0.0)
    return dm, dmt


def _ssd_specs(index_of_chunk):
    wide = pl.BlockSpec((CHUNK, D_INNER), lambda c: (index_of_chunk(c), 0))
    b_spec = pl.BlockSpec((CHUNK, D_BC), lambda c: (index_of_chunk(c), D_INNER // D_BC))
    c_spec = pl.BlockSpec((CHUNK, D_BC), lambda c: (index_of_chunk(c), D_INNER // D_BC + 1))
    rows_spec = pl.BlockSpec((None, SSM_HEADS, CHUNK), lambda c: (index_of_chunk(c), 0, 0))
    state_spec = pl.BlockSpec((N_GROUPS, None, D_STATE, GROUP_W), lambda c: (0, index_of_chunk(c), 0, 0))
    return wide, b_spec, c_spec, rows_spec, state_spec


def _ssd_fwd(name, xbc, dt_exp, acs_exp, acs_rows, dskexp, steps=()):
    rows = xbc.shape[0]
    nc = rows // CHUNK

    def body(x_ref, b_ref, c_ref, dte_ref, acs_ref, acst_ref, dsk_ref, y_ref, st_ref, s_scr):
        @pl.when(pl.program_id(0) == 0)
        def _():
            s_scr[...] = jnp.zeros_like(s_scr)

        lane = lax.broadcasted_iota(jnp.int32, (CHUNK, LANES), 1)
        for g in range(N_GROUPS):
            y_g = _group_cols(y_ref, g, GROUP_W)
            causal, x, _, acs_exp_v, tot_exp, xdt, e_exp, f_exp, bm, cm = _ssd_common(
                _group_cols(x_ref, g, GROUP_W), _group_cols(b_ref, g, D_STATE), _group_cols(c_ref, g, D_STATE),
                _group_cols(dte_ref, g, GROUP_W), _group_cols(acs_ref, g, GROUP_W))
            state = s_scr[g]
            st_ref[g] = state
            cb16, bb16 = cm.astype(BF16), bm.astype(BF16)
            cb = _dot_nt(cb16, bb16)
            base = e_exp * _dot(cb16, state.astype(BF16)) + _group_cols(dsk_ref, g, GROUP_W)[...] * x
            for p in range(HEADS_PER_GROUP // 2):
                sl = slice(p * LANES, (p + 1) * LANES)
                xp = xdt[:, sl].astype(BF16)
                yd = []
                for e in range(2):
                    acs_row = acst_ref[pl.ds(g * HEADS_PER_GROUP + 2 * p + e, 1), :]
                    dm, _ = _pair_decay(acs_exp_v[:, sl], acs_row, e, causal)
                    yd.append(_dot((cb * dm).astype(BF16), xp))
                y_g[:, sl] = jnp.where(lane < HEAD_DIM, yd[0], yd[1]) + base[:, sl]
            s_scr[g] = jnp.exp(tot_exp) * state + _dot_tn(bb16, (f_exp * xdt).astype(BF16))

    wide, b_spec, c_spec, rows_spec, state_spec = _ssd_specs(lambda c: c)
    return _call(
        body, name=name,
        out_shape=(jax.ShapeDtypeStruct((rows, D_INNER), F32),
                   jax.ShapeDtypeStruct((N_GROUPS, nc, D_STATE, GROUP_W), F32)),
        grid=(nc,),
        in_specs=[wide, b_spec, c_spec, wide, wide, rows_spec, pl.BlockSpec((1, D_INNER), lambda c: (0, 0))],
        out_specs=(wide, state_spec),
        scratch_shapes=[pltpu.VMEM((N_GROUPS, D_STATE, GROUP_W), F32)],
        operands=[xbc, xbc, xbc, dt_exp, acs_exp, acs_rows, dskexp], semantics=("arbitrary",), steps=steps)


def _ssd_bwd(name, xbc, dt_exp, acs_exp, acs_rows, dt, a128, dskexp, dy, states, steps=()):
    rows = xbc.shape[0]
    nc = rows // CHUNK
    last = nc - 1

    def body(x_ref, b_ref, c_ref, dte_ref, acs_ref, acst_ref, dt_ref, a128_ref, dsk_all, dy_all, st_all,
             dx_all, db_all, dc_all, ddt_ref, dalog_ref, ddsk_ref, ds_all):
        @pl.when(pl.program_id(0) == 0)
        def _():
            ds_all[...] = jnp.zeros_like(ds_all)
            dalog_ref[...] = jnp.zeros_like(dalog_ref)
            ddsk_ref[...] = jnp.zeros_like(ddsk_ref)

        dacs = jnp.zeros((CHUNK, LANES), F32)
        ddt_x = jnp.zeros((CHUNK, LANES), F32)
        for g in range(N_GROUPS):
            dacs_g, ddt_x_g = group(
                g, _group_cols(x_ref, g, GROUP_W), _group_cols(b_ref, g, D_STATE), _group_cols(c_ref, g, D_STATE),
                _group_cols(dte_ref, g, GROUP_W), _group_cols(acs_ref, g, GROUP_W), acst_ref,
                _group_cols(dsk_all, g, GROUP_W), _group_cols(dy_all, g, GROUP_W), st_all.at[g],
                _group_cols(dx_all, g, GROUP_W), _group_cols(db_all, g, D_STATE), _group_cols(dc_all, g, D_STATE),
                ddsk_ref, ds_all.at[g])
            dacs, ddt_x = dacs + dacs_g, ddt_x + ddt_x_g
        _, causal_t = _causal_masks()
        da = _dot3_data_rhs(causal_t, dacs)
        ddt_ref[...] = da * a128_ref[...] + ddt_x
        dalog_ref[...] += jnp.sum(da * dt_ref[...], axis=0, keepdims=True) * a128_ref[...]

    def group(g, x_ref, b_ref, c_ref, dte_ref, acs_ref, acst_ref, dsk_ref, dy_ref, st_ref,
              dx_ref, db_ref, dc_ref, ddsk_ref, ds_scr):
        causal, x, dt_exp, acs_exp_v, tot_exp, xdt, e_exp, f_exp, bm, cm = _ssd_common(
            x_ref, b_ref, c_ref, dte_ref, acs_ref)
        reduce_heads = _reduce_heads_matrix(g)
        state, dstate = st_ref[...], ds_scr[...]
        dyv = dy_ref[...]
        cb16, bb16 = cm.astype(BF16), bm.astype(BF16)
        s16, ds16 = state.astype(BF16), dstate.astype(BF16)
        cb = _dot_nt(cb16, bb16)
        cbt = _dot_nt(bb16, cb16)
        cs = _dot(cb16, s16)
        bds = _dot(bb16, ds16)
        edy = e_exp * dyv
        fx = f_exp * xdt
        dxdt_base = f_exp * bds
        dc_acc = _dot_nt(edy.astype(BF16), s16)
        db_acc = _dot_nt(fx.astype(BF16), ds16)
        ds_scr[...] = jnp.exp(tot_exp) * dstate + _dot_tn(cb16, edy.astype(BF16))
        q = fx * bds
        dacs = _dot2_data_lhs(edy * cs - q, reduce_heads)
        dtot = jnp.sum(_dot2_data_lhs(q + jnp.exp(tot_exp) * dstate * state, reduce_heads), axis=0, keepdims=True)
        ddsk_ref[...] += jnp.sum(_dot2_data_lhs(dyv * x, reduce_heads), axis=0, keepdims=True)
        lane = lax.broadcasted_iota(jnp.int32, (CHUNK, LANES), 1)
        dcb = jnp.zeros((CHUNK, CHUNK), F32)
        dcbt = jnp.zeros((CHUNK, CHUNK), F32)
        ddt_x = jnp.zeros((CHUNK, LANES), F32)
        for p in range(HEADS_PER_GROUP // 2):
            sl = slice(p * LANES, (p + 1) * LANES)
            xp, dyp = xdt[:, sl], dyv[:, sl]
            xp16, dyp16 = xp.astype(BF16), dyp.astype(BF16)
            dxh = []
            for e in range(2):
                h = 2 * p + e
                mine = (lane < HEAD_DIM) if e == 0 else (lane >= HEAD_DIM)
                acs_row = acst_ref[pl.ds(g * HEADS_PER_GROUP + h, 1), :]
                dm, dmt = _pair_decay(acs_exp_v[:, sl], acs_row, e, causal)
                m, mt = cb * dm, cbt * dmt
                xh16 = jnp.where(mine, xp, 0.0).astype(BF16)
                dyh16 = jnp.where(mine, dyp, 0.0).astype(BF16)
                d_m = _dot_nt(dyh16, xp16)
                d_mt = _dot_nt(xh16, dyp16)
                dacs_h = (jnp.sum(d_m * m, axis=-1, keepdims=True)
                          - jnp.sum(d_mt * mt, axis=-1, keepdims=True))
                dacs = dacs + jnp.where(lane == HEADS_PER_GROUP * g + h, dacs_h, 0.0)
                dcb = dcb + d_m * dm
                dcbt = dcbt + d_mt * dmt
                dxh.append(_dot(mt.astype(BF16), dyp16))
            dxdt = jnp.where(lane < HEAD_DIM, dxh[0], dxh[1]) + dxdt_base[:, sl]
            dx_ref[:, sl] = dxdt * dt_exp[:, sl] + dsk_ref[:, sl] * dyp
            ddt_x = ddt_x + _dot2_data_lhs(dxdt * x[:, sl], _reduce_pair_matrix(g, p))
        dc_ref[...] = dc_acc + _dot(dcb.astype(BF16), bb16)
        db_ref[...] = db_acc + _dot(dcbt.astype(BF16), cb16)
        row = lax.broadcasted_iota(jnp.int32, (CHUNK, LANES), 0)
        return dacs + jnp.where(row == CHUNK - 1, dtot, 0.0), ddt_x

    wide, b_spec, c_spec, rows_spec, state_spec = _ssd_specs(lambda c: last - c)
    heads_spec = pl.BlockSpec((CHUNK, LANES), lambda c: (last - c, 0))
    vec_spec = pl.BlockSpec((1, LANES), lambda c: (0, 0))
    bc_out = pl.BlockSpec((CHUNK, D_BC), lambda c: (last - c, 0))
    vec_shape = jax.ShapeDtypeStruct((1, LANES), F32)
    return _call(
        body, name=name,
        out_shape=(jax.ShapeDtypeStruct((rows, D_INNER), F32), jax.ShapeDtypeStruct((rows, D_BC), F32),
                   jax.ShapeDtypeStruct((rows, D_BC), F32), jax.ShapeDtypeStruct((rows, LANES), F32),
                   vec_shape, vec_shape),
        grid=(nc,),
        in_specs=[wide, b_spec, c_spec, wide, wide, rows_spec, heads_spec, vec_spec,
                  pl.BlockSpec((1, D_INNER), lambda c: (0, 0)), wide, state_spec],
        out_specs=(wide, bc_out, bc_out, heads_spec, vec_spec, vec_spec),
        scratch_shapes=[pltpu.VMEM((N_GROUPS, D_STATE, GROUP_W), F32)],
        operands=[xbc, xbc, xbc, dt_exp, acs_exp, acs_rows, dt, a128, dskexp, dy, states],
        semantics=("arbitrary",), steps=steps)


def _attn_visible(b, heads=1):
    row = jnp.bitwise_and(lax.broadcasted_iota(jnp.int32, (heads * CHUNK, 3 * CHUNK), 0), CHUNK - 1)
    col = lax.broadcasted_iota(jnp.int32, (heads * CHUNK, 3 * CHUNK), 1)
    bb = b + jnp.zeros_like(col)
    meta = (col < CHUNK) & (bb >= 1) & (col >= PAD_ROWS)
    prev = (col >= CHUNK) & (col < 2 * CHUNK) & (bb >= 2) & ((col - CHUNK) > row)
    cur = (col >= 2 * CHUNK) & ((col - 2 * CHUNK) <= row) & ((bb >= 1) | ((col - 2 * CHUNK) >= PAD_ROWS))
    return meta | prev | cur


def _attn_visible4(b):
    return _attn_visible(b, 4)


def _stack_heads(q_ref, sink_ref, kvh, scale):
    lane = lax.broadcasted_iota(jnp.int32, (CHUNK, LANES), 1)
    parts, sinks = [], []
    for pp in range(2):
        pair = kvh * 2 + pp
        qp = q_ref[:, pair * LANES:(pair + 1) * LANES] * scale
        for e in range(2):
            mine = (lane < HEAD_DIM) if e == 0 else (lane >= HEAD_DIM)
            parts.append(jnp.where(mine, qp, 0.0).astype(BF16))
            sinks.append(jnp.full((CHUNK, 1), sink_ref[2 * pair + e], F32))
    return jnp.concatenate(parts, axis=0), jnp.concatenate(sinks, axis=0)


def _attn_operands(q_ref, k0, kp, kc, v0, vp, vc, sink_ref):
    kcat, vcat, q4, sink4 = [], [], [], []
    for kvh in range(N_KV_HEADS):
        ksl = slice(kvh * LANES, (kvh + 1) * LANES)
        kcat.append(jnp.concatenate([k0[:, ksl], kp[:, ksl], kc[:, ksl]], axis=0).astype(BF16))
        vcat.append(jnp.concatenate([v0[:, ksl], vp[:, ksl], vc[:, ksl]], axis=0).astype(BF16))
        stacked, sinks = _stack_heads(q_ref, sink_ref, kvh, ATTN_SCALE)
        q4.append(stacked)
        sink4.append(sinks)
    return kcat, vcat, q4, sink4


def _attn_probs(q4, kcat, visible, sink4):
    heads = range(N_KV_HEADS)
    s = [jnp.where(visible, _dot_nt(q4[h], kcat[h]), NEG_INF) for h in heads]
    m = [jnp.maximum(jnp.max(s[h], axis=-1, keepdims=True), sink4[h]) for h in heads]
    pe = [jnp.exp(s[h] - m[h]) for h in heads]
    pe_sink = [jnp.exp(sink4[h] - m[h]) for h in heads]
    inv = [1.0 / (jnp.sum(pe[h], axis=-1, keepdims=True) + pe_sink[h]) for h in heads]
    return [pe[h] * inv[h] for h in heads], [pe_sink[h] * inv[h] for h in heads]


def _unstack_pairs(stacked, pp):
    lane = lax.broadcasted_iota(jnp.int32, (CHUNK, LANES), 1)
    return jnp.where(lane < HEAD_DIM, stacked[(2 * pp) * CHUNK:(2 * pp + 1) * CHUNK],
                     stacked[(2 * pp + 1) * CHUNK:(2 * pp + 2) * CHUNK])


def _attn_specs(colblock):
    blk = lambda f: pl.BlockSpec((CHUNK, 2 * D_KV), f)
    return [blk(lambda b: (0, colblock)), blk(lambda b: (jnp.maximum(b - 1, 0), colblock)), blk(lambda b: (b, colblock))]


def _attn_fwd(name, q, kv2, sinks, steps=()):
    rows = q.shape[0]

    def body(q_ref, k0, kp, kc, v0, vp, vc, sink_ref, o_ref):
        visible = _attn_visible4(pl.program_id(0))
        kcat, vcat, q4, sink4 = _attn_operands(q_ref, k0, kp, kc, v0, vp, vc, sink_ref)
        pn, _ = _attn_probs(q4, kcat, visible, sink4)
        o4 = [_dot(pn[h].astype(BF16), vcat[h]) for h in range(N_KV_HEADS)]
        for kvh in range(N_KV_HEADS):
            for pp in range(2):
                qsl = slice((kvh * 2 + pp) * LANES, (kvh * 2 + pp + 1) * LANES)
                o_ref[:, qsl] = _unstack_pairs(o4[kvh], pp).astype(BF16)

    return _call(
        body, name=name, out_shape=jax.ShapeDtypeStruct((rows, D_MODEL), BF16), grid=(rows // CHUNK,),
        in_specs=[pl.BlockSpec((CHUNK, D_MODEL), lambda b: (b, 0))] + _attn_specs(0) + _attn_specs(1)
        + [pl.BlockSpec(memory_space=pltpu.SMEM)],
        out_specs=pl.BlockSpec((CHUNK, D_MODEL), lambda b: (b, 0)),
        operands=[q, kv2, kv2, kv2, kv2, kv2, kv2, sinks], semantics=("parallel",), steps=steps)


def _attn_bwd(name, q, kv2, sinks, do, steps=()):
    rows = q.shape[0]

    def body(q_ref, k0, kp, kc, v0, vp, vc, sink_ref, do_ref,
             dq_ref, dkc_ref, dkp_ref, dvc_ref, dvp_ref, dkm_ref, dvm_ref, dsink_ref):
        @pl.when(pl.program_id(0) == 0)
        def _():
            dkm_ref[...] = jnp.zeros_like(dkm_ref)
            dvm_ref[...] = jnp.zeros_like(dvm_ref)
            dsink_ref[...] = jnp.zeros_like(dsink_ref)

        visible = _attn_visible4(pl.program_id(0))
        heads = range(N_KV_HEADS)
        lane1 = lax.broadcasted_iota(jnp.int32, (1, LANES), 1)
        kcat, vcat, q4, sink4 = _attn_operands(q_ref, k0, kp, kc, v0, vp, vc, sink_ref)
        do4 = [_stack_heads(do_ref, sink_ref, h, 1.0)[0] for h in heads]
        pn, psink = _attn_probs(q4, kcat, visible, sink4)
        dp = [_dot_nt(do4[h], vcat[h]) for h in heads]
        delta = [jnp.sum(pn[h] * dp[h], axis=-1, keepdims=True) for h in heads]
        ds16 = [(pn[h] * (dp[h] - delta[h])).astype(BF16) for h in heads]
        dq4 = [_dot(ds16[h], kcat[h]) for h in heads]
        dk_acc = [_dot_tn(ds16[h], q4[h]) for h in heads]
        dv_acc = [_dot_tn(pn[h].astype(BF16), do4[h]) for h in heads]
        dsink = jnp.zeros((1, LANES), F32)
        for kvh in heads:
            ksl = slice(kvh * LANES, (kvh + 1) * LANES)
            sink_terms = psink[kvh] * delta[kvh]
            for j in range(4):
                part = jnp.sum(sink_terms[j * CHUNK:(j + 1) * CHUNK], axis=0, keepdims=True)
                dsink = dsink - jnp.where(lane1 == kvh * 4 + j, part, 0.0)
            for pp in range(2):
                qsl = slice((kvh * 2 + pp) * LANES, (kvh * 2 + pp + 1) * LANES)
                dq_ref[:, qsl] = (_unstack_pairs(dq4[kvh], pp) * ATTN_SCALE).astype(BF16)
            dkm_ref[:, ksl] += dk_acc[kvh][0:CHUNK]
            dvm_ref[:, ksl] += dv_acc[kvh][0:CHUNK]
            dkp_ref[:, ksl] = dk_acc[kvh][CHUNK:2 * CHUNK]
            dvp_ref[:, ksl] = dv_acc[kvh][CHUNK:2 * CHUNK]
            dkc_ref[:, ksl] = dk_acc[kvh][2 * CHUNK:3 * CHUNK]
            dvc_ref[:, ksl] = dv_acc[kvh][2 * CHUNK:3 * CHUNK]
        dsink_ref[...] += dsink

    qspec = pl.BlockSpec((CHUNK, D_MODEL), lambda b: (b, 0))
    kvspec = pl.BlockSpec((CHUNK, 2 * D_KV), lambda b: (b, 0))
    fixed = pl.BlockSpec((CHUNK, 2 * D_KV), lambda b: (0, 0))
    kv_shape = jax.ShapeDtypeStruct((rows, 2 * D_KV), F32)
    meta_shape = jax.ShapeDtypeStruct((CHUNK, 2 * D_KV), F32)
    return _call(
        body, name=name,
        out_shape=(jax.ShapeDtypeStruct((rows, D_MODEL), BF16), kv_shape, kv_shape, kv_shape, kv_shape,
                   meta_shape, meta_shape, jax.ShapeDtypeStruct((1, LANES), F32)),
        grid=(rows // CHUNK,),
        in_specs=[qspec] + _attn_specs(0) + _attn_specs(1) + [pl.BlockSpec(memory_space=pltpu.SMEM), qspec],
        out_specs=(qspec, kvspec, kvspec, kvspec, kvspec, fixed, fixed, pl.BlockSpec((1, LANES), lambda b: (0, 0))),
        operands=[q, kv2, kv2, kv2, kv2, kv2, kv2, sinks, do], semantics=("arbitrary",), steps=steps)


def _kv_grad_combine(name, dk_cur, dk_prev, dk_meta, dv_cur, dv_prev, dv_meta):
    rows = dk_cur.shape[0]
    nb = rows // CHUNK
    width = 2 * D_KV

    def body(kc_ref, kp_ref, km_ref, vc_ref, vp_ref, vm_ref, o_ref):
        jj = pl.program_id(0) + jnp.zeros((CHUNK, 1), jnp.int32)
        for half, (c_ref, p_ref, m_ref) in enumerate(((kc_ref, kp_ref, km_ref), (vc_ref, vp_ref, vm_ref))):
            total = c_ref[...] + jnp.where(jj < nb - 1, p_ref[...], 0.0) + jnp.where(jj == 0, m_ref[...], 0.0)
            o_ref[:, half * width:(half + 1) * width] = total.astype(BF16)

    blk = lambda f: pl.BlockSpec((CHUNK, width), f)
    three = lambda: [blk(lambda j: (j, 0)), blk(lambda j: (jnp.minimum(j + 1, nb - 1), 0)), blk(lambda j: (0, 0))]
    return pl.pallas_call(
        body, name=name, out_shape=jax.ShapeDtypeStruct((rows, 2 * width), BF16), grid=(nb,),
        in_specs=three() + three(), out_specs=pl.BlockSpec((CHUNK, 2 * width), lambda j: (j, 0)),
        compiler_params=_cparams(("parallel",)),
    )(dk_cur, dk_prev, dk_meta, dv_cur, dv_prev, dv_meta)


def _adamw(name, w, g, m, v, steps=()):
    rows, width = w.shape
    tr = rows
    for cand in range(8, rows + 1, 8):
        if rows % cand == 0 and cand * width * 4 <= (1 << 20):
            tr = cand

    def body(w_ref, g_ref, m_ref, v_ref, d_ref, mo_ref, vo_ref):
        gv = g_ref[...]
        mn = ADAM_B1 * m_ref[...] + (1.0 - ADAM_B1) * gv
        vn = ADAM_B2 * v_ref[...] + (1.0 - ADAM_B2) * (gv * gv)
        m_hat = mn / (1.0 - ADAM_B1 ** ADAM_STEP)
        v_hat = vn / (1.0 - ADAM_B2 ** ADAM_STEP)
        d_ref[...] = -ADAM_LR * (m_hat / (jnp.sqrt(v_hat) + ADAM_EPS) + ADAM_WD * w_ref[...])
        mo_ref[...] = mn
        vo_ref[...] = vn

    blk = pl.BlockSpec((tr, width), lambda i: (i, 0))
    shp = jax.ShapeDtypeStruct((rows, width), F32)
    return _call(body, name=name, out_shape=(shp, shp, shp), grid=(rows // tr,), in_specs=[blk] * 4,
                 out_specs=(blk,) * 3, operands=[w, g, m, v], semantics=("parallel",), steps=steps)


def _ffn_fwd(tag, h, hn, p, i, plan):
    up_g, up_v, act = _ffn_up_conv(f"ffn{tag}_up", hn, plan.weight("f_w_up", i), p["f_conv_w"][i],
                                   p["f_conv_b"][i:i + 1], steps=plan.steps(f"ffn{tag}_up"))
    pre = _mm(f"ffn{tag}_down", act, plan.weight("f_w_down", i), "nn", steps=plan.steps(f"ffn{tag}_down"))
    return pre, (h, hn, up_g, up_v, act, pre)


def _ffn_bwd(tag, dpre, saved, p, i, plan):
    h, hn, up_g, up_v, act, pre = saved
    plan.grad("f_w_down", i, _mm(f"ffn{tag}_down_dw", act, dpre, "tn", out_dtype=BF16))
    dact = _mm(f"ffn{tag}_down_dx", dpre, plan.weight("f_w_down", i), "nt", steps=plan.steps(f"ffn{tag}_down_dx"))
    gwg, gwv, gbg, gbv, dhn, g_up = _ffn_conv_bwd(
        f"ffn{tag}_conv_bwd", up_g, up_v, dact, p["f_conv_w"][i], p["f_conv_b"][i:i + 1], hn,
        plan.weight("f_w_up", i), steps=plan.steps(f"ffn{tag}_conv_bwd"))
    g_cw, g_cb = jnp.concatenate([gwg, gwv], axis=1), jnp.concatenate([gbg, gbv], axis=1)
    plan.grad("f_w_up", i, g_up)
    return dhn, dict(f_conv_w=g_cw, f_conv_b=g_cb)


def _lanes_pad(a, width=LANES):
    return jnp.pad(a, [(0, 0)] * (a.ndim - 1) + [(0, width - a.shape[-1])])


def _dup_heads(w):
    rows = w.shape[0]
    w = w.reshape(rows, 2 * N_KV_HEADS, 1, HEAD_DIM)
    return jnp.broadcast_to(w, (rows, 2 * N_KV_HEADS, 2, HEAD_DIM)).reshape(rows, 4 * D_KV)


def _undup_heads(g):
    rows = g.shape[0]
    return g.reshape(rows, 2 * N_KV_HEADS, 2, HEAD_DIM).sum(axis=2).reshape(rows, 2 * D_KV)


def _local_step(x2, target, p, plan):
    seq = x2.shape[0]
    rows = seq + CHUNK
    g = {}

    h0 = jnp.concatenate([jnp.zeros((PAD_ROWS, D_MODEL), F32), p["meta_tokens"], x2], axis=0)

    w_in = plan.weight("a_w_in")
    w_dt = jnp.pad(w_in[D_MAIN:], ((0, LANES - SSM_HEADS), (0, 0)))
    dt_bias = _lanes_pad(p["a_dt_bias"])
    a128 = _lanes_pad(-jnp.exp(p["a_a_log"]))
    dskexp = jnp.repeat(p["a_d_skip"].reshape(SSM_HEADS), HEAD_DIM).reshape(1, D_INNER)

    hn0 = _rms_fwd("a_norm", h0, p["a_norm_pre"])
    zx = _mm("a_in_main", hn0, w_in, "nt", k_rows=D_MAIN, steps=plan.steps("a_in_main"))
    dtr = _mm("a_in_dt", hn0, w_dt, "nt")
    xbc = _conv4_fwd("a_conv", zx, p["a_conv_w"], p["a_conv_b"], steps=plan.steps("a_conv"))
    dt = _dt_fwd("a_dt", dtr, dt_bias)
    dt_exp, acs_exp, acs_rows = _ssd_prep("a_ssd_prep", dt, a128, steps=plan.steps("a_ssd_prep"))
    y, states = _ssd_fwd("a_ssd", xbc, dt_exp, acs_exp, acs_rows, dskexp, steps=plan.steps("a_ssd"))
    yn = _gate_fwd("a_gate", y, zx, p["a_gate_norm"], steps=plan.steps("a_gate"))
    mix = _mm("a_out", yn, plan.weight("a_w_out"), "nn", steps=plan.steps("a_out"))
    h1, (hn_f0,) = _resid_norm_fwd("a_resid", h0, mix, p["a_norm_post"], [p["f_norm_pre"][0:1]])

    pre_f0, ffn0 = _ffn_fwd("0", h1, hn_f0, p, 0, plan)
    h2, (hkv, hn2) = _resid_norm_fwd("ffn0_resid", h1, pre_f0, p["f_norm_post"][0:1], [p["kv_norm"], p["b_norm_pre"]])

    w_kv2 = _dup_heads(plan.weight("w_kv"))
    kv2 = _mm("kv_proj", hkv, w_kv2, "nn")
    q = _mm("b_q", hn2, plan.weight("b_w_q"), "nn")
    sinks = p["b_sinks"].reshape(N_Q_HEADS)
    o = _attn_fwd("b_attn", q, kv2, sinks, steps=plan.steps("b_attn"))
    attn = _mm("b_o", o, plan.weight("b_w_o"), "nn", steps=plan.steps("b_o"))
    h3, (hn_f1,) = _resid_norm_fwd("b_resid", h2, attn, p["b_norm_post"], [p["f_norm_pre"][1:2]])

    pre_f1, ffn1 = _ffn_fwd("1", h3, hn_f1, p, 1, plan)
    dh, loss_vec, dpre_f1, g_post1 = _resid_norm_loss("ffn1_resid_loss", h3, pre_f1, p["f_norm_post"][1:2], target)
    loss = loss_vec[0, 0]

    dhn_f1, g1 = _ffn_bwd("1", dpre_f1, ffn1, p, 1, plan)
    dh, g_pre1, dpre, g["b_norm_post"] = _norm_bwd_add("ffn1_norm_bwd", dh, dhn_f1, h3, p["f_norm_pre"][1:2],
                                                        then=(attn, p["b_norm_post"]))
    plan.grad("b_w_o", None, _mm("b_o_dw", o, dpre, "tn", out_dtype=BF16))
    do = _mm("b_o_dx", dpre, plan.weight("b_w_o"), "nt", steps=plan.steps("b_o_dx"))
    dq, dkc, dkp, dvc, dvp, dkm, dvm, dsink = _attn_bwd("b_attn_bwd", q, kv2, sinks, do, steps=plan.steps("b_attn_bwd"))
    g["b_sinks"] = dsink[:, :N_Q_HEADS]
    dhn2 = _mm("b_q_dx", dq, plan.weight("b_w_q"), "nt")
    plan.grad("b_w_q", None, _mm("b_q_dw", hn2, dq, "tn", out_dtype=BF16))
    dh, g["b_norm_pre"] = _norm_bwd_add("b_norm_bwd", dh, dhn2, h2, p["b_norm_pre"])
    dkv2 = _kv_grad_combine("kv_grad", dkc, dkp, dkm, dvc, dvp, dvm)
    dhkv = _mm("kv_proj_dx", dkv2, w_kv2, "nt")
    plan.grad("w_kv", None, _undup_heads(_mm("kv_proj_dw", hkv, dkv2, "tn")))
    dh, g["kv_norm"], dpre_f0, g_post0 = _norm_bwd_add("kv_norm_bwd", dh, dhkv, h2, p["kv_norm"],
                                                       then=(pre_f0, p["f_norm_post"][0:1]))

    dhn_f0, g0 = _ffn_bwd("0", dpre_f0, ffn0, p, 0, plan)
    dh, g_pre0, dpre, g["a_norm_post"] = _norm_bwd_add("ffn0_norm_bwd", dh, dhn_f0, h1, p["f_norm_pre"][0:1],
                                                        then=(mix, p["a_norm_post"]))
    g["f_norm_post"] = jnp.concatenate([g_post0, g_post1], axis=0)
    g["f_norm_pre"] = jnp.concatenate([g_pre0, g_pre1], axis=0)
    g["f_conv_w"] = jnp.stack([g0["f_conv_w"], g1["f_conv_w"]])
    g["f_conv_b"] = jnp.concatenate([g0["f_conv_b"], g1["f_conv_b"]], axis=0)
    plan.grad("a_w_out", None, _mm("a_out_dw", yn, dpre, "tn", out_dtype=BF16))
    dyn = _mm("a_out_dx", dpre, plan.weight("a_w_out"), "nt", steps=plan.steps("a_out_dx"))
    dy, dz, g["a_gate_norm"] = _gate_bwd("a_gate_bwd", dyn, y, zx, p["a_gate_norm"])
    dxs, dbm, dcm, ddt, dalog, ddsk = _ssd_bwd("a_ssd_bwd", xbc, dt_exp, acs_exp, acs_rows, dt, a128, dskexp, dy, states,
                                              steps=plan.steps("a_ssd_bwd"))
    g["a_a_log"] = dalog[:, :SSM_HEADS]
    g["a_d_skip"] = ddsk[:, :SSM_HEADS]
    ddtr, dbias = _dt_bwd("a_dt_bwd", ddt, dtr, dt_bias)
    g["a_dt_bias"] = dbias[:, :SSM_HEADS]
    dxp, gw_x, gb_x = _conv4_bwd("a_conv_bwd_x", zx, dxs, p["a_conv_w"], p["a_conv_b"], 0)
    dbp, gw_b, gb_b = _conv4_bwd("a_conv_bwd_b", zx, dbm, p["a_conv_w"], p["a_conv_b"], D_INNER)
    dcp, gw_c, gb_c = _conv4_bwd("a_conv_bwd_c", zx, dcm, p["a_conv_w"], p["a_conv_b"], D_INNER + D_BC)
    g["a_conv_w"] = jnp.concatenate([gw_x, gw_b, gw_c], axis=1)
    g["a_conv_b"] = jnp.concatenate([gb_x, gb_b, gb_c], axis=1)
    dzx = jnp.concatenate([dz, dxp, dbp, dcp], axis=1)
    g_in = _mm("a_in_main_dw", dzx, hn0, "tn", out_dtype=BF16, out_rows=D_IN_PROJ, steps=plan.steps("a_in_main_dw"))
    plan.grad("a_w_in", None, _tn_rows_into("a_in_dt_dw", ddtr, hn0, g_in, D_MAIN, SSM_HEADS))
    dhn0 = _mm("a_in_dt_dx", ddtr, w_dt, "nn", steps=plan.steps("a_in_dt_dx"))
    dhn0 = _mm("a_in_main_dx", dzx, w_in, "nn", acc=dhn0, steps=plan.steps("a_in_main_dx"))
    dh, g["a_norm_pre"] = _norm_bwd_add("a_norm_bwd", dh, dhn0, h0, p["a_norm_pre"], steps=plan.steps("a_norm_bwd"))

    g["meta_tokens"] = dh[PAD_ROWS:CHUNK]
    return loss, dh[CHUNK:], g


ANY = pl.BlockSpec(memory_space=pl.ANY)
VMEM_SPEC = pl.BlockSpec(memory_space=pltpu.VMEM)


def _allgather_small(name, shard):
    rows = shard.shape[0]

    def body(s_ref, o_ref, send_sems, recv_sems):
        x, y, c = _place()
        me = 2 * x + y
        o_ref[me] = s_ref[...]
        chips = _other_chips(x, y)
        sends = [pltpu.make_async_remote_copy(s_ref, o_ref.at[me], send_sems.at[j], recv_sems.at[j],
                                              device_id=(cx, cy, c), device_id_type=MESH)
                 for j, (cx, cy) in enumerate(chips)]
        for cp in sends:
            cp.start()
        for j, (cx, cy) in enumerate(chips):
            pltpu.make_async_remote_copy(s_ref, o_ref.at[2 * cx + cy], send_sems.at[j], recv_sems.at[j],
                                         device_id=(cx, cy, c), device_id_type=MESH).wait_recv()
        for cp in sends:
            cp.wait_send()

    return pl.pallas_call(
        body, name=name, out_shape=jax.ShapeDtypeStruct((N_CHIPS, rows, LANES), F32),
        in_specs=[VMEM_SPEC], out_specs=VMEM_SPEC,
        scratch_shapes=[pltpu.SemaphoreType.DMA((3,)), pltpu.SemaphoreType.DMA((3,))],
        compiler_params=pltpu.CompilerParams(vmem_limit_bytes=VMEM_LIMIT),
    )(shard)


def _row_block(rows, width, itemsize, align, budget=2 << 20):
    best = rows
    for cand in range(align, rows + 1, align):
        if rows % cand == 0 and cand * width * itemsize <= budget:
            best = cand
    return best


def _cast_into_slot(name, chip, w, layer=None):
    rows, width = w.shape[-2:]
    tr = _row_block(rows, width, 4, 16)
    if layer is None:
        in_spec = pl.BlockSpec((tr, width), lambda i, chip_ref: (i, 0))
    else:
        in_spec = pl.BlockSpec((None, tr, width), lambda i, chip_ref: (layer, i, 0))

    def body(chip_ref, w_ref, o_ref):
        o_ref[...] = w_ref[...].astype(BF16)

    return pl.pallas_call(
        body, name=name, out_shape=jax.ShapeDtypeStruct((N_CHIPS, rows, width), BF16),
        grid_spec=pltpu.PrefetchScalarGridSpec(
            num_scalar_prefetch=1, grid=(rows // tr,), in_specs=[in_spec],
            out_specs=pl.BlockSpec((None, tr, width), lambda i, chip_ref: (chip_ref[0], i, 0))),
        compiler_params=_cparams(("parallel",)),
    )(chip, w)


def _allreduce_small(name, vec):
    rows = -(-vec.shape[0] // (2 * SUBLANES)) * (2 * SUBLANES)
    hr = rows // 2
    padded = jnp.pad(vec, ((0, rows - vec.shape[0]), (0, 0)))

    def body(v_ref, o_ref, theirs, pair, by_chip, send_sems, recv_sems):
        x, y, c = _place()
        me = 2 * x + y
        sibling = (x, y, 1 - c)
        mine = pl.ds(pl.multiple_of(c * hr, SUBLANES), hr)
        other = pl.ds(pl.multiple_of((1 - c) * hr, SUBLANES), hr)

        swap = _remote(v_ref, theirs, send_sems, recv_sems, 0, sibling)
        swap.start()
        swap.wait()
        south = (c + jnp.zeros((1, 1), jnp.int32)) == 0
        pair[...] = jnp.where(south, v_ref[...], theirs[...]) + jnp.where(south, theirs[...], v_ref[...])

        by_chip[me] = pair[mine, :]
        sends = [_remote(by_chip.at[me], by_chip.at[me], send_sems, recv_sems, 1 + j, (cx, cy, c))
                 for j, (cx, cy) in enumerate(_other_chips(x, y))]
        for cp in sends:
            cp.start()
        for j, (cx, cy) in enumerate(_other_chips(x, y)):
            _remote(by_chip.at[me], by_chip.at[2 * cx + cy], send_sems, recv_sems, 1 + j, (cx, cy, c)).wait_recv()
        for cp in sends:
            cp.wait_send()
        total = by_chip[0]
        for s in range(1, N_CHIPS):
            total = total + by_chip[s]

        o_ref[mine, :] = total
        back = _remote(o_ref.at[mine], o_ref.at[mine], send_sems, recv_sems, 4, sibling)
        back.start()
        _remote(o_ref.at[other], o_ref.at[other], send_sems, recv_sems, 4, sibling).wait_recv()
        back.wait_send()

    out = pl.pallas_call(
        body, name=name, out_shape=jax.ShapeDtypeStruct((rows, LANES), F32),
        in_specs=[VMEM_SPEC], out_specs=VMEM_SPEC,
        scratch_shapes=[pltpu.VMEM((rows, LANES), F32), pltpu.VMEM((rows, LANES), F32),
                        pltpu.VMEM((N_CHIPS, hr, LANES), F32), pltpu.SemaphoreType.DMA((5,)),
                        pltpu.SemaphoreType.DMA((5,))],
        compiler_params=pltpu.CompilerParams(vmem_limit_bytes=VMEM_LIMIT),
    )(padded)
    return out[:vec.shape[0]]


def _rs_pair_add(name, place, grads, partner, split="rows"):
    _, half_rows, width = partner.shape
    tr = _row_block(half_rows, width, 2, 16)
    nb = half_rows // tr
    if split == "rows":
        mine = pl.BlockSpec((None, tr, width), lambda s, i, pr: (s, pr[1] * nb + i, 0))
    else:
        mine = pl.BlockSpec((None, tr, width), lambda s, i, pr: (s, i, pr[1]))

    def body(place_ref, g_ref, p_ref, o_ref):
        o_ref[...] = (g_ref[...].astype(F32) + p_ref[...].astype(F32)).astype(BF16)

    return pl.pallas_call(
        body, name=name, out_shape=jax.ShapeDtypeStruct(partner.shape, BF16),
        grid_spec=pltpu.PrefetchScalarGridSpec(
            num_scalar_prefetch=1, grid=(N_CHIPS, nb),
            in_specs=[mine, pl.BlockSpec((None, tr, width), lambda s, i, pr: (s, i, 0))],
            out_specs=pl.BlockSpec((None, tr, width), lambda s, i, pr: (s, i, 0))),
        compiler_params=_cparams(("parallel", "parallel")),
    )(place, grads, partner)


def _rs_chip_add(name, place, mine, others, split="rows"):
    _, half_rows, width = mine.shape
    tr = _row_block(half_rows, width, 4, 16, budget=1 << 20)
    nb = half_rows // tr
    if split == "rows":
        out_shape, out_spec = (2 * half_rows, width), pl.BlockSpec((tr, width), lambda i, pr: (pr[1] * nb + i, 0))
    else:
        out_shape, out_spec = (half_rows, 2 * width), pl.BlockSpec((tr, width), lambda i, pr: (i, pr[1]))

    def body(place_ref, q_ref, r_ref, o_ref):
        acc = q_ref[...].astype(F32)
        for j in range(3):
            acc = acc + r_ref[j].astype(F32)
        o_ref[...] = acc

    return pl.pallas_call(
        body, name=name, out_shape=jax.ShapeDtypeStruct(out_shape, F32),
        grid_spec=pltpu.PrefetchScalarGridSpec(
            num_scalar_prefetch=1, grid=(nb,),
            in_specs=[pl.BlockSpec((None, tr, width), lambda i, pr: (pr[0], i, 0)),
                      pl.BlockSpec((3, tr, width), lambda i, pr: (0, i, 0))],
            out_specs=out_spec),
        compiler_params=_cparams(("parallel",)),
    )(place, mine, others)


WEIGHTS = ["meta_tokens", "a_norm_pre", "a_w_in", "a_conv_w", "a_conv_b", "a_dt_bias", "a_a_log", "a_d_skip",
           "a_gate_norm", "a_w_out", "a_norm_post", "kv_norm", "w_kv", "b_norm_pre", "b_w_q", "b_sinks", "b_w_o",
           "b_norm_post", "f_norm_pre", "f_w_up", "f_conv_w", "f_conv_b", "f_w_down", "f_norm_post"]
FULL_SHAPE = {
    "meta_tokens": (16, 1024), "a_norm_pre": (1, 1024), "a_w_in": (1, 1024, 5152), "a_conv_w": (1, 4, 3072),
    "a_conv_b": (1, 3072), "a_dt_bias": (1, 32), "a_a_log": (1, 32), "a_d_skip": (1, 32), "a_gate_norm": (1, 2048),
    "a_w_out": (1, 2048, 1024), "a_norm_post": (1, 1024), "kv_norm": (1024,), "w_kv": (1024, 512),
    "b_norm_pre": (1, 1024), "b_w_q": (1, 1024, 1024), "b_sinks": (1, 16), "b_w_o": (1, 1024, 1024),
    "b_norm_post": (1, 1024), "f_norm_pre": (2, 1024), "f_w_up": (2, 1024, 5632), "f_conv_w": (2, 3, 5632),
    "f_conv_b": (2, 5632), "f_w_down": (2, 2816, 1024), "f_norm_post": (2, 1024),
}
SHARD_AXIS = {
    "meta_tokens": 1, "a_norm_pre": 1, "a_w_in": 2, "a_conv_w": 2, "a_conv_b": 1, "a_dt_bias": None, "a_a_log": None,
    "a_d_skip": None, "a_gate_norm": 1, "a_w_out": 1, "a_norm_post": 1, "kv_norm": None, "w_kv": 0, "b_norm_pre": None,
    "b_w_q": 1, "b_sinks": None, "b_w_o": 1, "b_norm_post": None, "f_norm_pre": None, "f_w_up": 2, "f_conv_w": 2,
    "f_conv_b": None, "f_w_down": 1, "f_norm_post": None,
}
BIG = ["a_w_in", "a_w_out", "w_kv", "b_w_q", "b_w_o", "f_w_up", "f_w_down"]
SMALL = [n for n in WEIGHTS if n not in BIG]
SMALL_SHARDED = [n for n in SMALL if SHARD_AXIS[n] is not None]


def _shard_shape(name):
    shape = list(FULL_SHAPE[name])
    if SHARD_AXIS[name] is not None:
        shape[SHARD_AXIS[name]] //= N_CHIPS
    return tuple(shape)


def _numel(shape):
    return int(math.prod(shape))


SUBLANES = 8


def _packed_rows(shape):
    rows = -(-_numel(shape) // LANES)
    return -(-rows // SUBLANES) * SUBLANES


def _pack(arrays):
    parts = []
    for a in arrays:
        size, rows = _numel(a.shape), _packed_rows(a.shape)
        if size % LANES == 0:
            part = jnp.pad(a.reshape(size // LANES, LANES), ((0, rows - size // LANES), (0, 0)))
        else:
            part = jnp.pad(a.reshape(-1), (0, rows * LANES - size)).reshape(rows, LANES)
        parts.append(part)
    return jnp.concatenate(parts, axis=0)


def _unpack(packed, names, shape_of):
    out, off = {}, 0
    lead = packed.shape[:-2]
    for n in names:
        shape = tuple(shape_of(n))
        size, rows = _numel(shape), _packed_rows(shape)
        part = packed[..., off:off + rows, :]
        if size % LANES == 0:
            out[n] = part[..., :size // LANES, :].reshape(lead + shape)
        else:
            out[n] = part.reshape(lead + (rows * LANES,))[..., :size].reshape(lead + shape)
        off += rows
    return out


def _split_chips(name, full):
    ax = SHARD_AXIS[name]
    shape = full.shape
    cut = shape[:ax] + (N_CHIPS, shape[ax] // N_CHIPS) + shape[ax + 1:]
    return jnp.moveaxis(full.reshape(cut), ax, 0)


def _join_chips(name, stacked):
    ax = SHARD_AXIS[name]
    moved = jnp.moveaxis(stacked, 0, ax)
    shape = moved.shape
    return moved.reshape(shape[:ax] + (shape[ax] * shape[ax + 1],) + shape[ax + 2:])


def _as2d(a):
    return a.reshape(-1, a.shape[-1])


BUFFERS = [("a_w_in", "a_w_in", None), ("a_w_out", "a_w_out", None), ("w_kv", "w_kv", None),
           ("b_w_q", "b_w_q", None), ("b_w_o", "b_w_o", None), ("f_w_up0", "f_w_up", 0), ("f_w_up1", "f_w_up", 1),
           ("f_w_down0", "f_w_down", 0), ("f_w_down1", "f_w_down", 1)]


TRANSPOSED = ("a_w_in",)
SPLIT = {"a_w_in": "cols"}


def _local_shard(arrays, weight, layer):
    if weight in TRANSPOSED:
        return arrays[weight][0].T
    return _as2d(arrays[weight]) if layer is None else arrays[weight]


def _weight_from_gathered(weight, buf):
    if weight == "f_w_up":
        return buf
    return buf.reshape(N_CHIPS * buf.shape[1], buf.shape[2])


def _gathered_from_grad(weight, g):
    if weight == "f_w_up":
        return g
    return g.reshape(N_CHIPS, g.shape[0] // N_CHIPS, g.shape[1]).astype(BF16)


GATHER_SCHEDULE = {
    "a_in_main": [("ici", ["a_w_out"])],
    "a_conv": [("d2d", ["a_w_out"]), ("ici", ["f_w_down0"])],
    "a_ssd_prep": [("d2d", ["f_w_down0"]), ("ici_near", ["f_w_up0"])],
    "a_ssd": [("ici_far", ["f_w_up0"])],
    "a_gate": [("d2d", ["f_w_up0"]), ("ici", ["w_kv", "b_w_q", "b_w_o"])],
    "ffn0_up": [("d2d", ["w_kv", "b_w_q", "b_w_o"]), ("ici", ["f_w_down1"])],
    "ffn0_down": [("d2d", ["f_w_down1"])],
    "b_attn": [("ici", ["f_w_up1"])],
    "b_o": [("d2d", ["f_w_up1"])],
}
REDUCE_SCHEDULE = {
    "b_attn_bwd": [("all", ["f_w_down1", "f_w_up1", "b_w_o"])],
    "ffn0_conv_bwd": [("all", ["b_w_q", "w_kv", "f_w_down0"])],
    "a_ssd_bwd": [("all", ["f_w_up0", "a_w_out"])],
    "a_in_main_dx": [("near", ["a_w_in"])],
    "a_norm_bwd": [("far", ["a_w_in"])],
}
REDUCE_LAST = ("a_w_in",)
ICI_PEERS = {"ici": ALL_PEERS, "ici_near": NEAR_PEERS, "ici_far": FAR_PEERS,
             "all": ALL_PEERS, "near": NEAR_PEERS, "far": FAR_PEERS}
PAIR_SCHEDULE = {
    "b_o_dx": ["f_w_down1", "f_w_up1", "b_w_o"],
    "ffn0_down_dx": ["b_w_q", "w_kv", "f_w_down0"],
    "a_out_dx": ["f_w_up0", "a_w_out"],
    "a_in_dt_dx": ["a_w_in"],
}
SWAP_SCHEDULE = {"a_in_main_dw": ["f_w_down1", "f_w_up1", "b_w_o", "b_w_q", "w_kv", "f_w_down0", "f_w_up0", "a_w_out"]}


def _buffer_of(weight, layer):
    return weight if layer is None else f"{weight}{layer}"


class _Pipeline:
    def __init__(self, place, slots):
        self.place = place
        self.slots = dict(slots)
        self.running = []
        self.grads = {}
        self.theirs = {}
        self.partials = {}
        self.peers = {}
        self.reduced = {}

    def _collect(self):
        for step, buffers, table in self.running:
            table.update(zip(buffers, step.results))
        self.running = []

    @staticmethod
    def _splits(buffers):
        return [SPLIT.get(b, "rows") for b in buffers]

    def gather_now(self, name, buffers):
        step = _step_gather_full([self.slots[b] for b in buffers], self._splits(buffers))
        _run_steps(name, [step])
        self.slots.update(zip(buffers, step.results))

    def weight(self, name, layer=None):
        self._collect()
        return _weight_from_gathered(name, self.slots[_buffer_of(name, layer)])

    def grad(self, name, layer, g):
        self.grads[_buffer_of(name, layer)] = _gathered_from_grad(name, g)

    def steps(self, kernel):
        self._collect()
        steps = []
        for phase, buffers in GATHER_SCHEDULE.get(kernel, []):
            bufs, splits = [self.slots[b] for b in buffers], self._splits(buffers)
            step = (_step_gather_d2d(bufs, splits) if phase == "d2d"
                    else _step_gather_ici(bufs, splits, ICI_PEERS[phase]))
            self.running.append((step, buffers, self.slots))
            steps.append(step)
        buffers = PAIR_SCHEDULE.get(kernel)
        if buffers:
            step = _step_pair_exchange([self.grads[b] for b in buffers], self._splits(buffers))
            self.running.append((step, buffers, self.theirs))
            steps.append(step)
        for part, buffers in REDUCE_SCHEDULE.get(kernel, []):
            for b in buffers:
                if b not in self.partials:
                    self.partials[b] = _rs_pair_add("reduce_pair_add_" + b, self.place, self.grads[b], self.theirs[b],
                                                    SPLIT.get(b, "rows"))
            started = [self.peers[b] for b in buffers] if all(b in self.peers for b in buffers) else None
            step = _step_chip_exchange([self.partials[b] for b in buffers], ICI_PEERS[part], into=started)
            self.running.append((step, buffers, self.peers))
            steps.append(step)
        buffers = SWAP_SCHEDULE.get(kernel)
        if buffers:
            step = self._swap_step(buffers)
            self.running.append((step, buffers, self.reduced))
            steps.append(step)
        return steps

    def _swap_step(self, buffers):
        halves = [_rs_chip_add("reduce_chip_add_" + b, self.place, self.partials[b], self.peers[b], SPLIT.get(b, "rows"))
                  for b in buffers]
        return _step_pair_gather(halves, self._splits(buffers))

    def shard(self, buffer):
        self._collect()
        return self.reduced[buffer]

    def finish(self):
        self._collect()
        rest = [b for b, _, _ in BUFFERS if b not in self.reduced]
        step = self._swap_step(rest)
        _run_steps("reduce_pair_gather", [step])
        self.reduced.update(zip(rest, step.results))


def kernel(x, meta_tokens, a_norm_pre, a_w_in, a_conv_w, a_conv_b, a_dt_bias, a_a_log, a_d_skip, a_gate_norm, a_w_out, a_norm_post, kv_norm, w_kv, b_norm_pre, b_w_q, b_sinks, b_w_o, b_norm_post, f_norm_pre, f_w_up, f_conv_w, f_conv_b, f_w_down, f_norm_post, loss_target, m_meta_tokens, m_a_norm_pre, m_a_w_in, m_a_conv_w, m_a_conv_b, m_a_dt_bias, m_a_a_log, m_a_d_skip, m_a_gate_norm, m_a_w_out, m_a_norm_post, m_kv_norm, m_w_kv, m_b_norm_pre, m_b_w_q, m_b_sinks, m_b_w_o, m_b_norm_post, m_f_norm_pre, m_f_w_up, m_f_conv_w, m_f_conv_b, m_f_w_down, m_f_norm_post, v_meta_tokens, v_a_norm_pre, v_a_w_in, v_a_conv_w, v_a_conv_b, v_a_dt_bias, v_a_a_log, v_a_d_skip, v_a_gate_norm, v_a_w_out, v_a_norm_post, v_kv_norm, v_w_kv, v_b_norm_pre, v_b_w_q, v_b_sinks, v_b_w_o, v_b_norm_post, v_f_norm_pre, v_f_w_up, v_f_conv_w, v_f_conv_b, v_f_w_down, v_f_norm_post):
    given = dict(locals())
    w = {n: given[n] for n in WEIGHTS}
    mom = {n: given["m_" + n] for n in WEIGHTS}
    var = {n: given["v_" + n] for n in WEIGHTS}
    chip = 2 * lax.axis_index("x") + lax.axis_index("y")
    core = lax.axis_index("c")
    place = jnp.stack([chip, core]).astype(jnp.int32)

    small_all = _allgather_small("gather_small", _pack([w[n] for n in SMALL_SHARDED]))
    small_parts = _unpack(small_all, SMALL_SHARDED, _shard_shape)
    slots = {b: _cast_into_slot("cast_" + b, place, _local_shard(w, wn, layer), layer) for b, wn, layer in BUFFERS}
    pipeline = _Pipeline(place, slots)
    pipeline.gather_now("gather_first", ["a_w_in"])
    p = {}
    for n in SMALL:
        p[n] = _join_chips(n, small_parts[n]) if n in SMALL_SHARDED else w[n]
    p["a_conv_w"] = p["a_conv_w"][0]
    p["kv_norm"] = p["kv_norm"].reshape(1, D_MODEL)

    loss_local, grad_x, g = _local_step(x[0], loss_target[0], p, pipeline)

    small_sum = _allreduce_small("reduce_small", _pack([g[n].reshape(FULL_SHAPE[n]) for n in SMALL]
                                                       + [loss_local.reshape(1, 1)]))
    small_red = _unpack(small_sum, SMALL + ["loss"], lambda n: (1, 1) if n == "loss" else FULL_SHAPE[n])
    loss = small_red["loss"][0, 0]
    grads = {}
    for n in SMALL:
        if SHARD_AXIS[n] is None:
            grads[n] = small_red[n]
        else:
            grads[n] = lax.dynamic_index_in_dim(_split_chips(n, small_red[n]), chip, 0, keepdims=False)

    delta, new_m, new_v = {}, {}, {}
    for n in sorted(BIG, key=lambda name: name in REDUCE_LAST):
        shape = _shard_shape(n)
        if n in REDUCE_LAST:
            pipeline.finish()
        if n in TRANSPOSED:
            g2d = pipeline.shard(n)
            w2d, m2d, v2d = (arrays[n][0].T for arrays in (w, mom, var))
            back = lambda a: a.T.reshape(shape)
        else:
            g2d = (jnp.concatenate([pipeline.shard(n + "0"), pipeline.shard(n + "1")], axis=0)
                   if n in ("f_w_up", "f_w_down") else pipeline.shard(n))
            w2d, m2d, v2d = (_as2d(arrays[n]) for arrays in (w, mom, var))
            back = lambda a: a.reshape(shape)
        d, m2, v2 = _adamw("adamw_" + n, w2d, g2d, m2d, v2d, steps=pipeline.steps("adamw_" + n))
        grads[n], delta[n], new_m[n], new_v[n] = back(g2d), back(d), back(m2), back(v2)
    packed = [_pack([src[n].reshape(_shard_shape(n)) for n in SMALL]) for src in (w, grads, mom, var)]
    outs = _adamw("adamw_small", *packed)
    for dst, flat in zip((delta, new_m, new_v), outs):
        dst.update(_unpack(flat, SMALL, _shard_shape))

    return (loss, grad_x[None], *[grads[n].reshape(_shard_shape(n)) for n in WEIGHTS],
            *[delta[n] for n in WEIGHTS], *[new_m[n] for n in WEIGHTS], *[new_v[n] for n in WEIGHTS])
```

```python
import functools
import math

import jax
import jax.numpy as jnp
from jax import lax
from jax.experimental import pallas as pl
from jax.experimental.pallas import tpu as pltpu

F32, BF16 = jnp.float32, jnp.bfloat16
MESH = pl.DeviceIdType.MESH

D_MODEL = 1024
N_META = 16
CHUNK = 128
PAD_ROWS = CHUNK - N_META
D_INNER = 2048
D_STATE = 128
N_GROUPS = 4
HEADS_PER_GROUP = 8
SSM_HEADS = 32
HEAD_DIM = 64
D_BC = N_GROUPS * D_STATE
D_XBC = D_INNER + 2 * D_BC
D_MAIN = D_INNER + D_XBC
D_IN_PROJ = D_MAIN + SSM_HEADS
GROUP_W = HEADS_PER_GROUP * HEAD_DIM
SSM_CONV = 4
D_FF = 2816
FFN_CONV = 3
N_Q_HEADS = 16
N_KV_HEADS = 4
D_KV = 256
ATTN_SCALE = 1.0 / math.sqrt(HEAD_DIM)
RMS_EPS = 1e-6
NEG_INF = -1e30
LANES = 128
VMEM_LIMIT = 48 * 1024 * 1024

ADAM_LR, ADAM_B1, ADAM_B2, ADAM_EPS, ADAM_WD, ADAM_STEP = 0.001, 0.9, 0.999, 1e-08, 0.01, 10

N_CHIPS = 4


def _cparams(sem=None):
    return pltpu.CompilerParams(dimension_semantics=sem, vmem_limit_bytes=VMEM_LIMIT)


def _tile(n, cands=(512, 256, 128)):
    for t in cands:
        if n % t == 0:
            return t
    return n


def _row_tile(rows, width):
    for t in (544, 272):
        if rows % t == 0 and t * width * 4 <= (3 << 20):
            return t
    return 128


def _rows_mask(i, tm):
    rows = i * tm + lax.broadcasted_iota(jnp.int32, (tm, 1), 0)
    return rows >= PAD_ROWS


def _dot(a, b):
    return jnp.dot(a, b, preferred_element_type=F32)


def _dot_nt(a, b):
    return lax.dot_general(a, b, (((1,), (1,)), ((), ())), preferred_element_type=F32)


def _dot_tn(a, b):
    return lax.dot_general(a, b, (((0,), (0,)), ((), ())), preferred_element_type=F32)


def _sigmoid(x):
    return 1.0 / (1.0 + jnp.exp(-x))


def _place():
    return lax.axis_index("x"), lax.axis_index("y"), lax.axis_index("c")


def _other_chips(x, y):
    return [(1 - x, y), (x, 1 - y), (1 - x, 1 - y)]


class _Step:
    def __init__(self, ins, outs, aliases, n_sems, start, finish):
        self.ins, self.outs, self.aliases, self.n_sems = list(ins), list(outs), dict(aliases), n_sems
        self.start, self.finish = start, finish
        self.results = None


def _like(a):
    return jax.ShapeDtypeStruct(a.shape, a.dtype)


def _remote(src, dst, send_sems, recv_sems, k, device):
    return pltpu.make_async_remote_copy(src, dst, send_sems.at[k], recv_sems.at[k], device_id=device, device_id_type=MESH)


def _half(ref, split, which, lead=()):
    if split == "rows":
        hr = ref.shape[-2] // 2
        return ref.at[lead + (pl.ds(which * hr, hr),)]
    hc = ref.shape[-1] // 2
    return ref.at[lead + (slice(None), pl.ds(which * hc, hc))]


def _splits(bufs, splits):
    return list(splits) if splits is not None else ["rows"] * len(bufs)


ALL_PEERS = (0, 1, 2)
NEAR_PEERS = (0, 1)
FAR_PEERS = (2,)


def _step_gather_ici(bufs, splits=None, peers=ALL_PEERS):
    splits = _splits(bufs, splits)

    def copies(outs, send_sems, recv_sems, received):
        x, y, c = _place()
        me = 2 * x + y
        for k, o in enumerate(outs):
            for j, (cx, cy) in enumerate(_other_chips(x, y)):
                if j in peers:
                    part = _half(o, splits[k], c, (2 * cx + cy if received else me,))
                    yield _remote(part, part, send_sems, recv_sems, 3 * k + j, (cx, cy, c))

    def start(ins, outs, send_sems, recv_sems):
        for cp in copies(outs, send_sems, recv_sems, False):
            cp.start()

    def finish(ins, outs, send_sems, recv_sems):
        for cp in copies(outs, send_sems, recv_sems, True):
            cp.wait_recv()
        for cp in copies(outs, send_sems, recv_sems, False):
            cp.wait_send()

    return _Step(bufs, [_like(b) for b in bufs], {k: k for k in range(len(bufs))}, 3 * len(bufs), start, finish)


def _step_gather_d2d(bufs, splits=None):
    splits = _splits(bufs, splits)

    def copies(outs, send_sems, recv_sems, received):
        x, y, c = _place()
        for k, o in enumerate(outs):
            for j, (cx, cy) in enumerate(_other_chips(x, y)):
                part = _half(o, splits[k], 1 - c if received else c, (2 * cx + cy,))
                yield _remote(part, part, send_sems, recv_sems, 3 * k + j, (x, y, 1 - c))

    def start(ins, outs, send_sems, recv_sems):
        for cp in copies(outs, send_sems, recv_sems, False):
            cp.start()

    def finish(ins, outs, send_sems, recv_sems):
        for cp in copies(outs, send_sems, recv_sems, True):
            cp.wait_recv()
        for cp in copies(outs, send_sems, recv_sems, False):
            cp.wait_send()

    return _Step(bufs, [_like(b) for b in bufs], {k: k for k in range(len(bufs))}, 3 * len(bufs), start, finish)


def _step_gather_full(bufs, splits=None):
    n = len(bufs)
    splits = _splits(bufs, splits)

    def ici(outs, send_sems, recv_sems, received):
        x, y, c = _place()
        me = 2 * x + y
        for k, o in enumerate(outs):
            for j, (cx, cy) in enumerate(_other_chips(x, y)):
                part = _half(o, splits[k], c, (2 * cx + cy if received else me,))
                yield _remote(part, part, send_sems, recv_sems, 3 * k + j, (cx, cy, c))

    def d2d(outs, send_sems, recv_sems, received):
        x, y, c = _place()
        for k, o in enumerate(outs):
            for j, (cx, cy) in enumerate(_other_chips(x, y)):
                part = _half(o, splits[k], 1 - c if received else c, (2 * cx + cy,))
                yield _remote(part, part, send_sems, recv_sems, 3 * n + 3 * k + j, (x, y, 1 - c))

    def start(ins, outs, send_sems, recv_sems):
        for cp in ici(outs, send_sems, recv_sems, False):
            cp.start()

    def finish(ins, outs, send_sems, recv_sems):
        for arrived, onward in zip(ici(outs, send_sems, recv_sems, True), d2d(outs, send_sems, recv_sems, False)):
            arrived.wait_recv()
            onward.start()
        for cp in d2d(outs, send_sems, recv_sems, True):
            cp.wait_recv()
        for cp in ici(outs, send_sems, recv_sems, False):
            cp.wait_send()
        for cp in d2d(outs, send_sems, recv_sems, False):
            cp.wait_send()

    return _Step(bufs, [_like(b) for b in bufs], {k: k for k in range(n)}, 6 * n, start, finish)


def _half_shape(shape, split):
    return shape[:-2] + ((shape[-2] // 2, shape[-1]) if split == "rows" else (shape[-2], shape[-1] // 2))


def _step_pair_exchange(grads, splits=None):
    splits = _splits(grads, splits)

    def copies(ins, outs, send_sems, recv_sems):
        x, y, c = _place()
        for k, (g, o) in enumerate(zip(ins, outs)):
            yield _remote(_half(g, splits[k], 1 - c, (slice(None),)), o, send_sems, recv_sems, k, (x, y, 1 - c))

    def start(ins, outs, send_sems, recv_sems):
        for cp in copies(ins, outs, send_sems, recv_sems):
            cp.start()

    def finish(ins, outs, send_sems, recv_sems):
        for cp in copies(ins, outs, send_sems, recv_sems):
            cp.wait()

    outs = [jax.ShapeDtypeStruct(_half_shape(g.shape, s), g.dtype) for g, s in zip(grads, splits)]
    return _Step(grads, outs, {}, len(grads), start, finish)


def _step_chip_exchange(partials, peers=ALL_PEERS, into=None):
    n = len(partials)

    def copies(ins, outs, send_sems, recv_sems):
        x, y, c = _place()
        for k, (q, o) in enumerate(zip(ins[:n], outs)):
            for j, (cx, cy) in enumerate(_other_chips(x, y)):
                if j in peers:
                    yield _remote(q.at[2 * cx + cy], o.at[j], send_sems, recv_sems, 3 * k + j, (cx, cy, c))

    def start(ins, outs, send_sems, recv_sems):
        for cp in copies(ins, outs, send_sems, recv_sems):
            cp.start()

    def finish(ins, outs, send_sems, recv_sems):
        for cp in copies(ins, outs, send_sems, recv_sems):
            cp.wait()

    outs = [jax.ShapeDtypeStruct((3,) + q.shape[1:], q.dtype) for q in partials]
    if into is None:
        return _Step(partials, outs, {}, 3 * n, start, finish)
    return _Step(list(partials) + list(into), outs, {n + k: k for k in range(n)}, 3 * n, start, finish)


def _step_pair_gather(shards, splits=None):
    splits = _splits(shards, splits)

    def copies(outs, send_sems, recv_sems, received):
        x, y, c = _place()
        for k, o in enumerate(outs):
            part = _half(o, splits[k], 1 - c if received else c)
            yield _remote(part, part, send_sems, recv_sems, k, (x, y, 1 - c))

    def start(ins, outs, send_sems, recv_sems):
        for cp in copies(outs, send_sems, recv_sems, False):
            cp.start()

    def finish(ins, outs, send_sems, recv_sems):
        for cp in copies(outs, send_sems, recv_sems, True):
            cp.wait_recv()
        for cp in copies(outs, send_sems, recv_sems, False):
            cp.wait_send()

    return _Step(shards, [_like(s) for s in shards], {k: k for k in range(len(shards))}, len(shards), start, finish)


def _call(body, *, name, out_shape, grid, in_specs, out_specs, operands, scratch_shapes=(), semantics=None, steps=()):
    single = not isinstance(out_shape, (tuple, list))
    out_shapes = [out_shape] if single else list(out_shape)
    out_spec_list = [out_specs] if single else list(out_specs)
    steps = list(steps)
    if not steps:
        res = pl.pallas_call(body, name=name, out_shape=out_shapes, grid=grid, in_specs=list(in_specs),
                             out_specs=out_spec_list, scratch_shapes=list(scratch_shapes),
                             compiler_params=_cparams(semantics))(*operands)
        return res[0] if single else res
    n_in, n_out, n_scr = len(operands), len(out_shapes), len(scratch_shapes)
    x_in = [a for s in steps for a in s.ins]
    x_out = [o for s in steps for o in s.outs]
    aliases, in_off, out_off = {}, 0, 0
    for s in steps:
        for i, o in s.aliases.items():
            aliases[n_in + in_off + i] = n_out + out_off + o
        in_off += len(s.ins)
        out_off += len(s.outs)
    sems = []
    for s in steps:
        sems += [pltpu.SemaphoreType.DMA((s.n_sems,)), pltpu.SemaphoreType.DMA((s.n_sems,))]
    any_spec = pl.BlockSpec(memory_space=pl.ANY)

    def carried(*refs):
        pos = 0
        ins = refs[pos:pos + n_in]; pos += n_in
        xi = refs[pos:pos + len(x_in)]; pos += len(x_in)
        outs = refs[pos:pos + n_out]; pos += n_out
        xo = refs[pos:pos + len(x_out)]; pos += len(x_out)
        scr = refs[pos:pos + n_scr]; pos += n_scr
        sem_refs = refs[pos:]

        def each(action):
            i0 = o0 = 0
            for k, s in enumerate(steps):
                getattr(s, action)(xi[i0:i0 + len(s.ins)], xo[o0:o0 + len(s.outs)], sem_refs[2 * k], sem_refs[2 * k + 1])
                i0 += len(s.ins)
                o0 += len(s.outs)

        if grid:
            first = functools.reduce(jnp.logical_and, [pl.program_id(d) == 0 for d in range(len(grid))])
            last = functools.reduce(jnp.logical_and, [pl.program_id(d) == grid[d] - 1 for d in range(len(grid))])
            pl.when(first)(lambda: each("start"))
            body(*ins, *outs, *scr)
            pl.when(last)(lambda: each("finish"))
        else:
            each("start")
            body(*ins, *outs, *scr)
            each("finish")

    res = pl.pallas_call(
        carried, name=name, out_shape=out_shapes + x_out, grid=grid,
        in_specs=list(in_specs) + [any_spec] * len(x_in), out_specs=out_spec_list + [any_spec] * len(x_out),
        scratch_shapes=list(scratch_shapes) + sems, input_output_aliases=aliases,
        compiler_params=_cparams(None if semantics is None else ("arbitrary",) * len(grid)),
    )(*operands, *x_in)
    o0 = n_out
    for s in steps:
        s.results = list(res[o0:o0 + len(s.outs)])
        o0 += len(s.outs)
    return res[0] if single else tuple(res[:n_out])


def _run_steps(name, steps):
    _call(lambda: None, name=name, out_shape=[], grid=(), in_specs=[], out_specs=[], operands=[], steps=steps)
    return [s.results for s in steps]


def _mm(name, a, b, mode, out_dtype=F32, acc=None, b_colblock=0, k_rows=None, out_rows=None, steps=()):
    resident_bytes = 8 << 20
    if mode == "nn":
        m, k = a.shape
        n = b.shape[1]
        tm = m
        while tm * k * 2 > resident_bytes and tm % 32 == 0:
            tm //= 2
        tn = _tile(n)
        grid = (m // tm, n // tn)
        in_specs = [pl.BlockSpec((tm, k), lambda i, j: (i, 0)), pl.BlockSpec((k, tn), lambda i, j: (0, j))]
        out_shape, out_block = (m, n), (tm, tn)
    elif mode == "nt":
        m, n = a.shape
        k = k_rows or b.shape[0]
        tm = m
        while tm * n * 2 > resident_bytes and tm % 32 == 0:
            tm //= 2
        tk = _tile(k)
        grid = (m // tm, k // tk)
        in_specs = [pl.BlockSpec((tm, n), lambda i, j: (i, 0)), pl.BlockSpec((tk, n), lambda i, j: (j, b_colblock))]
        out_shape, out_block = (m, k), (tm, tk)
    else:
        m, k = a.shape
        n = b.shape[1]
        tk, tn = _tile(k), (n if m * n * 2 <= resident_bytes else _tile(n))
        grid = (k // tk, n // tn)
        in_specs = [pl.BlockSpec((m, tk), lambda i, j: (0, i)), pl.BlockSpec((m, tn), lambda i, j: (0, j))]
        out_shape, out_block = (out_rows or k, n), (tk, tn)
    out_spec = pl.BlockSpec(out_block, lambda i, j: (i, j))
    has_acc = acc is not None

    def body(*refs):
        a_ref, b_ref = refs[0], refs[1]
        o_ref = refs[-1]
        av, bv = a_ref[...], b_ref[...]
        if mode == "nn":
            r = _dot(av, bv)
        elif mode == "nt":
            r = _dot_nt(av, bv)
        else:
            r = _dot_tn(av, bv)
        if has_acc:
            r = r + refs[2][...]
        o_ref[...] = r.astype(o_ref.dtype)

    operands = [a, b]
    if has_acc:
        in_specs = in_specs + [out_spec]
        operands.append(acc)
    return _call(body, name=name, out_shape=jax.ShapeDtypeStruct(out_shape, out_dtype), grid=grid, in_specs=in_specs,
                 out_specs=out_spec, operands=operands, semantics=("parallel", "parallel"), steps=steps)


def _tn_rows_into(name, a, b, into, row0, nrows):
    m, k = a.shape
    n = b.shape[1]

    def body(a_ref, b_ref, into_ref, o_ref):
        o_ref[...] = _dot_tn(a_ref[...], b_ref[...])[0:nrows].astype(o_ref.dtype)

    return pl.pallas_call(
        body, name=name, out_shape=jax.ShapeDtypeStruct(into.shape, into.dtype), grid=(1,),
        in_specs=[pl.BlockSpec((m, k), lambda i: (0, 0)), pl.BlockSpec((m, n), lambda i: (0, 0)),
                  pl.BlockSpec(memory_space=pl.ANY)],
        out_specs=pl.BlockSpec((nrows, n), lambda i: (row0 // nrows, 0)),
        input_output_aliases={2: 0}, compiler_params=_cparams(("arbitrary",)),
    )(a, b, into)


def _rms_fwd(name, h, w):
    rows, width = h.shape
    tm = _row_tile(rows, width)

    def body(h_ref, w_ref, o_ref):
        x = h_ref[...]
        r = lax.rsqrt(jnp.mean(x * x, axis=-1, keepdims=True) + RMS_EPS)
        o_ref[...] = (x * r * w_ref[...]).astype(BF16)

    return pl.pallas_call(
        body, name=name, out_shape=jax.ShapeDtypeStruct((rows, width), BF16), grid=(rows // tm,),
        in_specs=[pl.BlockSpec((tm, width), lambda i: (i, 0)), pl.BlockSpec((1, width), lambda i: (0, 0))],
        out_specs=pl.BlockSpec((tm, width), lambda i: (i, 0)), compiler_params=_cparams(("parallel",)),
    )(h, w)


def _resid_norm_fwd(name, h, pre, w, next_norms=()):
    rows, width = h.shape
    tm = _row_tile(rows, width)
    n_next = len(next_norms)

    def body(*refs):
        h_ref, p_ref, w_ref = refs[:3]
        v_refs = refs[3:3 + n_next]
        o_ref = refs[3 + n_next]
        n_refs = refs[4 + n_next:]
        p = p_ref[...]
        r = lax.rsqrt(jnp.mean(p * p, axis=-1, keepdims=True) + RMS_EPS)
        x = h_ref[...] + jnp.where(_rows_mask(pl.program_id(0), tm), p * r * w_ref[...], 0.0)
        o_ref[...] = x
        if n_next:
            rx = lax.rsqrt(jnp.mean(x * x, axis=-1, keepdims=True) + RMS_EPS)
            for v_ref, n_ref in zip(v_refs, n_refs):
                n_ref[...] = (x * rx * v_ref[...]).astype(BF16)

    row_spec = pl.BlockSpec((tm, width), lambda i: (i, 0))
    vec_spec = pl.BlockSpec((1, width), lambda i: (0, 0))
    outs = pl.pallas_call(
        body, name=name,
        out_shape=[jax.ShapeDtypeStruct((rows, width), F32)] + [jax.ShapeDtypeStruct((rows, width), BF16)] * n_next,
        grid=(rows // tm,), in_specs=[row_spec, row_spec, vec_spec] + [vec_spec] * n_next,
        out_specs=[row_spec] * (1 + n_next), compiler_params=_cparams(("parallel",)),
    )(h, pre, w, *next_norms)
    return outs[0], list(outs[1:])


def _resid_norm_loss(name, h, pre, w, target):
    rows, width = h.shape

    def body(h_ref, p_ref, w_ref, t_ref, dh_ref, loss_ref, dp_ref, dw_ref):
        i = pl.program_id(0)
        p = p_ref[...]
        r = lax.rsqrt(jnp.mean(p * p, axis=-1, keepdims=True) + RMS_EPS)
        x = h_ref[...] + p * r * w_ref[...]
        real = (i + jnp.zeros((CHUNK, 1), jnp.int32)) >= 1
        diff = jnp.where(real, x - t_ref[...], 0.0)
        dh = diff * (1.0 / D_MODEL)
        dh_ref[...] = dh
        dp, dw_rows = _rms_bwd(dh, p, w_ref[...])
        dp_ref[...] = dp.astype(BF16)

        @pl.when(i == 0)
        def _():
            loss_ref[...] = jnp.zeros_like(loss_ref)
            dw_ref[...] = jnp.zeros_like(dw_ref)

        loss_ref[...] += jnp.sum(diff * diff) * (0.5 / D_MODEL)
        dw_ref[...] += jnp.sum(dw_rows, axis=0, keepdims=True)

    blk = pl.BlockSpec((CHUNK, width), lambda i: (i, 0))
    vec_spec = pl.BlockSpec((1, width), lambda i: (0, 0))
    return pl.pallas_call(
        body, name=name,
        out_shape=(jax.ShapeDtypeStruct((rows, width), F32), jax.ShapeDtypeStruct((1, LANES), F32),
                   jax.ShapeDtypeStruct((rows, width), BF16), jax.ShapeDtypeStruct((1, width), F32)),
        grid=(rows // CHUNK,),
        in_specs=[blk, blk, vec_spec, pl.BlockSpec((CHUNK, width), lambda i: (jnp.maximum(i - 1, 0), 0))],
        out_specs=(blk, pl.BlockSpec((1, LANES), lambda i: (0, 0)), blk, vec_spec),
        compiler_params=_cparams(("arbitrary",)),
    )(h, pre, w, target)


def _rms_bwd(dy, x, w):
    r = lax.rsqrt(jnp.mean(x * x, axis=-1, keepdims=True) + RMS_EPS)
    xhat = x * r
    dxhat = dy * w
    return r * (dxhat - xhat * jnp.mean(dxhat * xhat, axis=-1, keepdims=True)), dy * xhat


def _norm_bwd_add(name, dh, dhn, h, w, then=None, steps=()):
    rows, width = dh.shape
    tm = _row_tile(rows, width)
    fused = then is not None

    def body(*refs):
        dh_ref, dhn_ref, h_ref, w_ref = refs[:4]
        o_ref, dw_ref = refs[6:8] if fused else refs[4:6]
        i = pl.program_id(0)
        valid = _rows_mask(i, tm)
        dx, dw_rows = _rms_bwd(dhn_ref[...], h_ref[...], w_ref[...])
        dh_new = dh_ref[...] + jnp.where(valid, dx, 0.0)
        o_ref[...] = dh_new

        @pl.when(i == 0)
        def _():
            dw_ref[...] = jnp.zeros_like(dw_ref)

        dw_ref[...] += jnp.sum(dw_rows, axis=0, keepdims=True)
        if fused:
            p_ref, wp_ref, dp_ref, dwp_ref = refs[4], refs[5], refs[8], refs[9]
            dp, dwp_rows = _rms_bwd(jnp.where(valid, dh_new, 0.0), p_ref[...], wp_ref[...])
            dp_ref[...] = dp.astype(BF16)

            @pl.when(i == 0)
            def _():
                dwp_ref[...] = jnp.zeros_like(dwp_ref)

            dwp_ref[...] += jnp.sum(dwp_rows, axis=0, keepdims=True)

    row_spec = pl.BlockSpec((tm, width), lambda i: (i, 0))
    vec_spec = pl.BlockSpec((1, width), lambda i: (0, 0))
    row_f32, vec_f32 = jax.ShapeDtypeStruct((rows, width), F32), jax.ShapeDtypeStruct((1, width), F32)
    in_specs, operands = [row_spec, row_spec, row_spec, vec_spec], [dh, dhn, h, w]
    out_shape, out_specs = [row_f32, vec_f32], [row_spec, vec_spec]
    if fused:
        in_specs += [row_spec, vec_spec]
        operands += list(then)
        out_shape += [jax.ShapeDtypeStruct((rows, width), BF16), vec_f32]
        out_specs += [row_spec, vec_spec]
    return _call(body, name=name, out_shape=out_shape, grid=(rows // tm,), in_specs=in_specs, out_specs=out_specs,
                 operands=operands, semantics=("arbitrary",), steps=steps)


def _shift_down(x, s, rows):
    return pltpu.roll(x, s, 0) if s else x


def _shift_up(x, s, rows):
    return pltpu.roll(x, rows - s, 0) if s else x


def _conv4_fwd(name, zx, cw, cb, steps=()):
    rows = zx.shape[0]
    off = D_INNER // LANES

    def body(x_ref, w_ref, b_ref, o_ref):
        x = x_ref[...]
        acc = b_ref[...] + w_ref[pl.ds(SSM_CONV - 1, 1), :] * x
        for s in range(1, SSM_CONV):
            acc = acc + w_ref[pl.ds(SSM_CONV - 1 - s, 1), :] * _shift_down(x, s, rows)
        valid = lax.broadcasted_iota(jnp.int32, (rows, 1), 0) >= PAD_ROWS
        o_ref[...] = jnp.where(valid, acc * _sigmoid(acc), 0.0)

    return _call(
        body, name=name, out_shape=jax.ShapeDtypeStruct((rows, D_XBC), F32), grid=(D_XBC // LANES,),
        in_specs=[pl.BlockSpec((rows, LANES), lambda j: (0, j + off)),
                  pl.BlockSpec((SSM_CONV, LANES), lambda j: (0, j)),
                  pl.BlockSpec((1, LANES), lambda j: (0, j))],
        out_specs=pl.BlockSpec((rows, LANES), lambda j: (0, j)), operands=[zx, cw, cb],
        semantics=("parallel",), steps=steps)


def _conv4_bwd(name, zx, dout, cw, cb, col0):
    rows, width = dout.shape
    zoff = (D_INNER + col0) // LANES
    woff = col0 // LANES

    def body(x_ref, d_ref, w_ref, b_ref, dx_ref, dw_ref, db_ref):
        x = x_ref[...]
        shifted = [_shift_down(x, s, rows) for s in range(SSM_CONV)]
        acc = b_ref[...]
        for s in range(SSM_CONV):
            acc = acc + w_ref[pl.ds(SSM_CONV - 1 - s, 1), :] * shifted[s]
        sig = _sigmoid(acc)
        valid = lax.broadcasted_iota(jnp.int32, (rows, 1), 0) >= PAD_ROWS
        dpre = jnp.where(valid, d_ref[...] * sig * (1.0 + acc * (1.0 - sig)), 0.0)
        dx = w_ref[pl.ds(SSM_CONV - 1, 1), :] * dpre
        for s in range(1, SSM_CONV):
            dx = dx + w_ref[pl.ds(SSM_CONV - 1 - s, 1), :] * _shift_up(dpre, s, rows)
        dx_ref[...] = dx.astype(BF16)
        for s in range(SSM_CONV):
            dw_ref[pl.ds(SSM_CONV - 1 - s, 1), :] = jnp.sum(dpre * shifted[s], axis=0, keepdims=True)
        db_ref[...] = jnp.sum(dpre, axis=0, keepdims=True)

    return pl.pallas_call(
        body, name=name,
        out_shape=(jax.ShapeDtypeStruct((rows, width), BF16), jax.ShapeDtypeStruct((SSM_CONV, width), F32),
                   jax.ShapeDtypeStruct((1, width), F32)),
        grid=(width // LANES,),
        in_specs=[pl.BlockSpec((rows, LANES), lambda j: (0, j + zoff)),
                  pl.BlockSpec((rows, LANES), lambda j: (0, j)),
                  pl.BlockSpec((SSM_CONV, LANES), lambda j: (0, j + woff)),
                  pl.BlockSpec((1, LANES), lambda j: (0, j + woff))],
        out_specs=(pl.BlockSpec((rows, LANES), lambda j: (0, j)),
                   pl.BlockSpec((SSM_CONV, LANES), lambda j: (0, j)),
                   pl.BlockSpec((1, LANES), lambda j: (0, j))),
        compiler_params=_cparams(("parallel",)),
    )(zx, dout, cw, cb)


FFN_TILE = 2 * LANES


def _ffn_up_conv(name, hn, w_up, cw, cb, steps=()):
    rows, k = hn.shape
    chip_blocks = w_up.shape[2] // LANES
    half_blocks = D_FF // LANES
    nt = D_FF // FFN_TILE

    def weight_block(offset):
        return pl.BlockSpec((None, k, LANES), lambda j: ((2 * j + offset) // chip_blocks, 0, (2 * j + offset) % chip_blocks))

    def body(a_ref, g0, g1, v0, v1, wg_ref, wv_ref, bg_ref, bv_ref, upg_ref, upv_ref, act_ref):
        a = a_ref[...]
        g = _dot(a, jnp.concatenate([g0[...], g1[...]], axis=1))
        v = _dot(a, jnp.concatenate([v0[...], v1[...]], axis=1))
        upg_ref[...] = g
        upv_ref[...] = v
        ug, uv = bg_ref[...], bv_ref[...]
        for s in range(FFN_CONV):
            ug = ug + wg_ref[pl.ds(FFN_CONV - 1 - s, 1), :] * _shift_down(g, s, rows)
            uv = uv + wv_ref[pl.ds(FFN_CONV - 1 - s, 1), :] * _shift_down(v, s, rows)
        act_ref[...] = (ug * _sigmoid(ug) * uv).astype(BF16)

    col = pl.BlockSpec((rows, FFN_TILE), lambda j: (0, j))
    wsp = lambda shift: pl.BlockSpec((FFN_CONV, FFN_TILE), lambda j: (0, j + shift))
    bsp = lambda shift: pl.BlockSpec((1, FFN_TILE), lambda j: (0, j + shift))
    half = jax.ShapeDtypeStruct((rows, D_FF), F32)
    return _call(
        body, name=name, out_shape=(half, half, jax.ShapeDtypeStruct((rows, D_FF), BF16)), grid=(nt,),
        in_specs=[pl.BlockSpec((rows, k), lambda j: (0, 0)), weight_block(0), weight_block(1),
                  weight_block(half_blocks), weight_block(half_blocks + 1), wsp(0), wsp(nt), bsp(0), bsp(nt)],
        out_specs=(col, col, col), operands=[hn, w_up, w_up, w_up, w_up, cw, cw, cb, cb],
        semantics=("parallel",), steps=steps)


def _ffn_conv_bwd(name, up_g, up_v, dact, cw, cb, hn, w_up, steps=()):
    rows, k = hn.shape
    chip_blocks = w_up.shape[2] // LANES
    nt = D_FF // LANES

    def weight_block(shift):
        return pl.BlockSpec((None, k, LANES), lambda j: ((j + shift) // chip_blocks, 0, (j + shift) % chip_blocks))

    def body(g_ref, v_ref, d_ref, wg_ref, wv_ref, bg_ref, bv_ref, upg_ref, upv_ref, hn_ref,
             dwg_ref, dwv_ref, dbg_ref, dbv_ref, dhn_ref, dup_ref, acc, hn_scr, hnt_scr, dup_scr, sems):
        j = pl.program_id(0)
        hn_copy = pltpu.make_async_copy(hn_ref, hn_scr, sems.at[0])
        dhn_copy = pltpu.make_async_copy(acc, dhn_ref, sems.at[0])

        def dup_copy(step, half):
            block, slot = step + half * nt, 2 * (step % 2) + half
            cols = pl.ds(pl.multiple_of((block % chip_blocks) * LANES, LANES), LANES)
            return pltpu.make_async_copy(dup_scr.at[slot], dup_ref.at[block // chip_blocks, :, cols], sems.at[1 + slot])

        @pl.when(j == 0)
        def _():
            hn_copy.start()
            acc[...] = jnp.zeros_like(acc)
            hn_copy.wait()
            for r in range(0, rows, LANES):
                hnt_scr[:, r:r + LANES] = hn_scr[r:r + LANES, :].T

        @pl.when(j >= 2)
        def _():
            dup_copy(j - 2, 0).wait()
            dup_copy(j - 2, 1).wait()

        g, v = g_ref[...], v_ref[...]
        gs = [_shift_down(g, s, rows) for s in range(FFN_CONV)]
        vs = [_shift_down(v, s, rows) for s in range(FFN_CONV)]
        ug, uv = bg_ref[...], bv_ref[...]
        for s in range(FFN_CONV):
            ug = ug + wg_ref[pl.ds(FFN_CONV - 1 - s, 1), :] * gs[s]
            uv = uv + wv_ref[pl.ds(FFN_CONV - 1 - s, 1), :] * vs[s]
        sig = _sigmoid(ug)
        dsig = d_ref[...] * sig
        dup = []
        for dpre, src, w_ref, dw_ref, db_ref in (
                (dsig * uv * (1.0 + ug * (1.0 - sig)), gs, wg_ref, dwg_ref, dbg_ref),
                (dsig * ug, vs, wv_ref, dwv_ref, dbv_ref)):
            dx = w_ref[pl.ds(FFN_CONV - 1, 1), :] * dpre
            for s in range(1, FFN_CONV):
                dx = dx + w_ref[pl.ds(FFN_CONV - 1 - s, 1), :] * _shift_up(dpre, s, rows)
            dup.append(dx.astype(BF16))
            for s in range(FFN_CONV):
                dw_ref[pl.ds(FFN_CONV - 1 - s, 1), :] = jnp.sum(dpre * src[s], axis=0, keepdims=True)
            db_ref[...] = jnp.sum(dpre, axis=0, keepdims=True)
        dup = jnp.concatenate(dup, axis=1)
        acc[...] += _dot_nt(dup, jnp.concatenate([upg_ref[...], upv_ref[...]], axis=1))
        dw = _dot(hnt_scr[...], dup)
        slot = 2 * (j % 2)
        dup_scr[slot] = dw[:, :LANES].astype(BF16)
        dup_scr[slot + 1] = dw[:, LANES:].astype(BF16)
        dup_copy(j, 0).start()
        dup_copy(j, 1).start()

        @pl.when(j == nt - 1)
        def _():
            dhn_copy.start()
            for step in (j - 1, j):
                dup_copy(step, 0).wait()
                dup_copy(step, 1).wait()
            dhn_copy.wait()

    col = pl.BlockSpec((rows, LANES), lambda j: (0, j))
    wsp = lambda shift: pl.BlockSpec((FFN_CONV, LANES), lambda j: (0, j + shift))
    bsp = lambda shift: pl.BlockSpec((1, LANES), lambda j: (0, j + shift))
    any_spec = pl.BlockSpec(memory_space=pl.ANY)
    dw_shape = jax.ShapeDtypeStruct((FFN_CONV, D_FF), F32)
    db_shape = jax.ShapeDtypeStruct((1, D_FF), F32)
    return _call(
        body, name=name, grid=(nt,),
        out_shape=(dw_shape, dw_shape, db_shape, db_shape, jax.ShapeDtypeStruct((rows, k), F32),
                   jax.ShapeDtypeStruct(w_up.shape, BF16)),
        in_specs=[col, col, col, wsp(0), wsp(nt), bsp(0), bsp(nt), weight_block(0), weight_block(nt), any_spec],
        out_specs=(wsp(0), wsp(0), bsp(0), bsp(0), any_spec, any_spec),
        operands=[up_g, up_v, dact, cw, cw, cb, cb, w_up, w_up, hn],
        scratch_shapes=[pltpu.VMEM((rows, k), F32), pltpu.VMEM((rows, k), BF16), pltpu.VMEM((k, rows), BF16),
                        pltpu.VMEM((4, k, LANES), BF16), pltpu.SemaphoreType.DMA((5,))],
        semantics=("arbitrary",), steps=steps)


def _dt_fwd(name, dtr, bias):
    rows = dtr.shape[0]
    tm = _row_tile(rows, LANES)

    def body(d_ref, b_ref, o_ref):
        v = d_ref[...] + b_ref[...]
        sp = jnp.maximum(v, 0.0) + jnp.log1p(jnp.exp(-jnp.abs(v)))
        lane = lax.broadcasted_iota(jnp.int32, (tm, LANES), 1)
        ok = _rows_mask(pl.program_id(0), tm) & (lane < SSM_HEADS)
        o_ref[...] = jnp.where(ok, sp, 0.0)

    return pl.pallas_call(
        body, name=name, out_shape=jax.ShapeDtypeStruct((rows, LANES), F32), grid=(rows // tm,),
        in_specs=[pl.BlockSpec((tm, LANES), lambda i: (i, 0)), pl.BlockSpec((1, LANES), lambda i: (0, 0))],
        out_specs=pl.BlockSpec((tm, LANES), lambda i: (i, 0)), compiler_params=_cparams(("parallel",)),
    )(dtr, bias)


def _dt_bwd(name, ddt, dtr, bias):
    rows = dtr.shape[0]
    tm = _row_tile(rows, LANES)

    def body(g_ref, d_ref, b_ref, o_ref, db_ref):
        i = pl.program_id(0)
        lane = lax.broadcasted_iota(jnp.int32, (tm, LANES), 1)
        ok = _rows_mask(i, tm) & (lane < SSM_HEADS)
        dv = jnp.where(ok, g_ref[...] * _sigmoid(d_ref[...] + b_ref[...]), 0.0)
        o_ref[...] = dv.astype(BF16)

        @pl.when(i == 0)
        def _():
            db_ref[...] = jnp.zeros_like(db_ref)

        db_ref[...] += jnp.sum(dv, axis=0, keepdims=True)

    row_spec = pl.BlockSpec((tm, LANES), lambda i: (i, 0))
    vec_spec = pl.BlockSpec((1, LANES), lambda i: (0, 0))
    return pl.pallas_call(
        body, name=name,
        out_shape=(jax.ShapeDtypeStruct((rows, LANES), BF16), jax.ShapeDtypeStruct((1, LANES), F32)),
        grid=(rows // tm,), in_specs=[row_spec, row_spec, vec_spec], out_specs=(row_spec, vec_spec),
        compiler_params=_cparams(("arbitrary",)),
    )(ddt, dtr, bias)


def _gate_fwd(name, y, zx, w, steps=()):
    rows = y.shape[0]
    tm = _row_tile(rows, D_INNER)

    def body(y_ref, z_ref, w_ref, o_ref):
        z = z_ref[...]
        g = y_ref[...] * (z * _sigmoid(z))
        r = lax.rsqrt(jnp.mean(g * g, axis=-1, keepdims=True) + RMS_EPS)
        o_ref[...] = (g * r * w_ref[...]).astype(BF16)

    row_spec = pl.BlockSpec((tm, D_INNER), lambda i: (i, 0))
    return _call(
        body, name=name, out_shape=jax.ShapeDtypeStruct((rows, D_INNER), BF16), grid=(rows // tm,),
        in_specs=[row_spec, row_spec, pl.BlockSpec((1, D_INNER), lambda i: (0, 0))],
        out_specs=row_spec, operands=[y, zx, w], semantics=("parallel",), steps=steps)


def _gate_bwd(name, dyn, y, zx, w):
    rows = y.shape[0]
    tm = _row_tile(rows, D_INNER)

    def body(d_ref, y_ref, z_ref, w_ref, dy_ref, dz_ref, dw_ref):
        i = pl.program_id(0)
        z, yv = z_ref[...], y_ref[...]
        sig = _sigmoid(z)
        sz = z * sig
        g = yv * sz
        r = lax.rsqrt(jnp.mean(g * g, axis=-1, keepdims=True) + RMS_EPS)
        ghat = g * r
        dn = d_ref[...]
        dghat = dn * w_ref[...]
        dg = r * (dghat - ghat * jnp.mean(dghat * ghat, axis=-1, keepdims=True))
        dy_ref[...] = dg * sz
        dz_ref[...] = (dg * yv * sig * (1.0 + z * (1.0 - sig))).astype(BF16)

        @pl.when(i == 0)
        def _():
            dw_ref[...] = jnp.zeros_like(dw_ref)

        dw_ref[...] += jnp.sum(dn * ghat, axis=0, keepdims=True)

    row_spec = pl.BlockSpec((tm, D_INNER), lambda i: (i, 0))
    vec_spec = pl.BlockSpec((1, D_INNER), lambda i: (0, 0))
    return pl.pallas_call(
        body, name=name,
        out_shape=(jax.ShapeDtypeStruct((rows, D_INNER), F32), jax.ShapeDtypeStruct((rows, D_INNER), BF16),
                   jax.ShapeDtypeStruct((1, D_INNER), F32)),
        grid=(rows // tm,), in_specs=[row_spec, row_spec, row_spec, vec_spec],
        out_specs=(row_spec, row_spec, vec_spec), compiler_params=_cparams(("arbitrary",)),
    )(dyn, y, zx, w)


def _split3(x):
    hi = x.astype(BF16)
    r1 = x - hi.astype(F32)
    mid = r1.astype(BF16)
    lo = (r1 - mid.astype(F32)).astype(BF16)
    return hi, mid, lo


def _dot3_data_lhs(x, sel):
    sel16 = sel.astype(F32).astype(BF16)
    hi, mid, lo = _split3(x)
    return _dot(hi, sel16) + _dot(mid, sel16) + _dot(lo, sel16)


def _dot2_data_lhs(x, sel):
    sel16 = sel.astype(F32).astype(BF16)
    hi = x.astype(BF16)
    mid = (x - hi.astype(F32)).astype(BF16)
    return _dot(hi, sel16) + _dot(mid, sel16)


def _dot3_data_rhs(sel, x):
    sel16 = sel.astype(F32).astype(BF16)
    hi, mid, lo = _split3(x)
    return _dot(sel16, hi) + _dot(sel16, mid) + _dot(sel16, lo)


def _causal_masks():
    r = lax.broadcasted_iota(jnp.int32, (CHUNK, CHUNK), 0)
    c = lax.broadcasted_iota(jnp.int32, (CHUNK, CHUNK), 1)
    return r >= c, r <= c


def _expand_heads_matrix(g):
    k = lax.broadcasted_iota(jnp.int32, (LANES, GROUP_W), 0)
    j = lax.broadcasted_iota(jnp.int32, (LANES, GROUP_W), 1)
    return HEADS_PER_GROUP * g + jnp.right_shift(j, 6) == k


def _reduce_heads_matrix(g):
    j = lax.broadcasted_iota(jnp.int32, (GROUP_W, LANES), 0)
    k = lax.broadcasted_iota(jnp.int32, (GROUP_W, LANES), 1)
    return HEADS_PER_GROUP * g + jnp.right_shift(j, 6) == k


def _reduce_pair_matrix(g, p):
    j = lax.broadcasted_iota(jnp.int32, (LANES, LANES), 0)
    k = lax.broadcasted_iota(jnp.int32, (LANES, LANES), 1)
    return HEADS_PER_GROUP * g + 2 * p + jnp.right_shift(j, 6) == k


def _group_cols(ref, g, width):
    return ref.at[:, pl.ds(g * width, width)]


def _ssd_prep(name, dt, a128, steps=()):
    rows = dt.shape[0]
    nc = rows // CHUNK

    def body(dt_ref, a_ref, dte_ref, acs_ref, acst_ref):
        causal, _ = _causal_masks()
        dtv = dt_ref[...]
        acs = _dot3_data_rhs(causal, dtv) * a_ref[...]
        acst_ref[...] = acs.T[0:SSM_HEADS]
        for g in range(N_GROUPS):
            expand = _expand_heads_matrix(g)
            _group_cols(dte_ref, g, GROUP_W)[...] = _dot3_data_lhs(dtv, expand)
            _group_cols(acs_ref, g, GROUP_W)[...] = _dot3_data_lhs(acs, expand)

    blk = pl.BlockSpec((CHUNK, D_INNER), lambda c: (c, 0))
    shp = jax.ShapeDtypeStruct((rows, D_INNER), F32)
    return _call(
        body, name=name, out_shape=(shp, shp, jax.ShapeDtypeStruct((nc, SSM_HEADS, CHUNK), F32)), grid=(nc,),
        in_specs=[pl.BlockSpec((CHUNK, LANES), lambda c: (c, 0)), pl.BlockSpec((1, LANES), lambda c: (0, 0))],
        out_specs=(blk, blk, pl.BlockSpec((None, SSM_HEADS, CHUNK), lambda c: (c, 0, 0))),
        operands=[dt, a128], semantics=("parallel",), steps=steps)


def _ssd_common(x_ref, b_ref, c_ref, dte_ref, acs_ref):
    x = x_ref[...]
    dt_exp = dte_ref[...]
    acs_exp = acs_ref[...]
    tot_exp = acs_ref[pl.ds(CHUNK - 1, 1), :]
    xdt = x * dt_exp
    e_exp = jnp.exp(acs_exp)
    f_exp = jnp.exp(tot_exp - acs_exp)
    return _causal_masks(), x, dt_exp, acs_exp, tot_exp, xdt, e_exp, f_exp, b_ref[...], c_ref[...]


def _pair_decay(acs_pair, acs_row, e, causal):
    lane = lax.broadcasted_iota(jnp.int32, (CHUNK, LANES), 1)
    mine = (lane < HEAD_DIM) if e == 0 else (lane >= HEAD_DIM)
    a_l = jnp.where(mine, acs_pair, pltpu.roll(acs_pair, HEAD_DIM, 1))
    seg = a_l - acs_row
    dm = jnp.where(causal[0], jnp.exp(jnp.minimum(seg, 0.0)), 0.0)
    dmt = jnp.where(causal[1], jnp.exp(jnp.minimum(-seg, 0.0)), 0.0)
    return dm, dmt


def _ssd_specs(index_of_chunk):
    wide = pl.BlockSpec((CHUNK, D_INNER), lambda c: (index_of_chunk(c), 0))
    b_spec = pl.BlockSpec((CHUNK, D_BC), lambda c: (index_of_chunk(c), D_INNER // D_BC))
    c_spec = pl.BlockSpec((CHUNK, D_BC), lambda c: (index_of_chunk(c), D_INNER // D_BC + 1))
    rows_spec = pl.BlockSpec((None, SSM_HEADS, CHUNK), lambda c: (index_of_chunk(c), 0, 0))
    state_spec = pl.BlockSpec((N_GROUPS, None, D_STATE, GROUP_W), lambda c: (0, index_of_chunk(c), 0, 0))
    return wide, b_spec, c_spec, rows_spec, state_spec


def _ssd_fwd(name, xbc, dt_exp, acs_exp, acs_rows, dskexp, steps=()):
    rows = xbc.shape[0]
    nc = rows // CHUNK

    def body(x_ref, b_ref, c_ref, dte_ref, acs_ref, acst_ref, dsk_ref, y_ref, st_ref, s_scr):
        @pl.when(pl.program_id(0) == 0)
        def _():
            s_scr[...] = jnp.zeros_like(s_scr)

        lane = lax.broadcasted_iota(jnp.int32, (CHUNK, LANES), 1)
        for g in range(N_GROUPS):
            y_g = _group_cols(y_ref, g, GROUP_W)
            causal, x, _, acs_exp_v, tot_exp, xdt, e_exp, f_exp, bm, cm = _ssd_common(
                _group_cols(x_ref, g, GROUP_W), _group_cols(b_ref, g, D_STATE), _group_cols(c_ref, g, D_STATE),
                _group_cols(dte_ref, g, GROUP_W), _group_cols(acs_ref, g, GROUP_W))
            state = s_scr[g]
            st_ref[g] = state
            cb16, bb16 = cm.astype(BF16), bm.astype(BF16)
            cb = _dot_nt(cb16, bb16)
            base = e_exp * _dot(cb16, state.astype(BF16)) + _group_cols(dsk_ref, g, GROUP_W)[...] * x
            for p in range(HEADS_PER_GROUP // 2):
                sl = slice(p * LANES, (p + 1) * LANES)
                xp = xdt[:, sl].astype(BF16)
                yd = []
                for e in range(2):
                    acs_row = acst_ref[pl.ds(g * HEADS_PER_GROUP + 2 * p + e, 1), :]
                    dm, _ = _pair_decay(acs_exp_v[:, sl], acs_row, e, causal)
                    yd.append(_dot((cb * dm).astype(BF16), xp))
                y_g[:, sl] = jnp.where(lane < HEAD_DIM, yd[0], yd[1]) + base[:, sl]
            s_scr[g] = jnp.exp(tot_exp) * state + _dot_tn(bb16, (f_exp * xdt).astype(BF16))

    wide, b_spec, c_spec, rows_spec, state_spec = _ssd_specs(lambda c: c)
    return _call(
        body, name=name,
        out_shape=(jax.ShapeDtypeStruct((rows, D_INNER), F32),
                   jax.ShapeDtypeStruct((N_GROUPS, nc, D_STATE, GROUP_W), F32)),
        grid=(nc,),
        in_specs=[wide, b_spec, c_spec, wide, wide, rows_spec, pl.BlockSpec((1, D_INNER), lambda c: (0, 0))],
        out_specs=(wide, state_spec),
        scratch_shapes=[pltpu.VMEM((N_GROUPS, D_STATE, GROUP_W), F32)],
        operands=[xbc, xbc, xbc, dt_exp, acs_exp, acs_rows, dskexp], semantics=("arbitrary",), steps=steps)


def _ssd_bwd(name, xbc, dt_exp, acs_exp, acs_rows, dt, a128, dskexp, dy, states, steps=()):
    rows = xbc.shape[0]
    nc = rows // CHUNK
    last = nc - 1

    def body(x_ref, b_ref, c_ref, dte_ref, acs_ref, acst_ref, dt_ref, a128_ref, dsk_all, dy_all, st_all,
             dx_all, db_all, dc_all, ddt_ref, dalog_ref, ddsk_ref, ds_all):
        @pl.when(pl.program_id(0) == 0)
        def _():
            ds_all[...] = jnp.zeros_like(ds_all)
            dalog_ref[...] = jnp.zeros_like(dalog_ref)
            ddsk_ref[...] = jnp.zeros_like(ddsk_ref)

        dacs = jnp.zeros((CHUNK, LANES), F32)
        ddt_x = jnp.zeros((CHUNK, LANES), F32)
        for g in range(N_GROUPS):
            dacs_g, ddt_x_g = group(
                g, _group_cols(x_ref, g, GROUP_W), _group_cols(b_ref, g, D_STATE), _group_cols(c_ref, g, D_STATE),
                _group_cols(dte_ref, g, GROUP_W), _group_cols(acs_ref, g, GROUP_W), acst_ref,
                _group_cols(dsk_all, g, GROUP_W), _group_cols(dy_all, g, GROUP_W), st_all.at[g],
                _group_cols(dx_all, g, GROUP_W), _group_cols(db_all, g, D_STATE), _group_cols(dc_all, g, D_STATE),
                ddsk_ref, ds_all.at[g])
            dacs, ddt_x = dacs + dacs_g, ddt_x + ddt_x_g
        _, causal_t = _causal_masks()
        da = _dot3_data_rhs(causal_t, dacs)
        ddt_ref[...] = da * a128_ref[...] + ddt_x
        dalog_ref[...] += jnp.sum(da * dt_ref[...], axis=0, keepdims=True) * a128_ref[...]

    def group(g, x_ref, b_ref, c_ref, dte_ref, acs_ref, acst_ref, dsk_ref, dy_ref, st_ref,
              dx_ref, db_ref, dc_ref, ddsk_ref, ds_scr):
        causal, x, dt_exp, acs_exp_v, tot_exp, xdt, e_exp, f_exp, bm, cm = _ssd_common(
            x_ref, b_ref, c_ref, dte_ref, acs_ref)
        reduce_heads = _reduce_heads_matrix(g)
        state, dstate = st_ref[...], ds_scr[...]
        dyv = dy_ref[...]
        cb16, bb16 = cm.astype(BF16), bm.astype(BF16)
        s16, ds16 = state.astype(BF16), dstate.astype(BF16)
        cb = _dot_nt(cb16, bb16)
        cbt = _dot_nt(bb16, cb16)
        cs = _dot(cb16, s16)
        bds = _dot(bb16, ds16)
        edy = e_exp * dyv
        fx = f_exp * xdt
        dxdt_base = f_exp * bds
        dc_acc = _dot_nt(edy.astype(BF16), s16)
        db_acc = _dot_nt(fx.astype(BF16), ds16)
        ds_scr[...] = jnp.exp(tot_exp) * dstate + _dot_tn(cb16, edy.astype(BF16))
        q = fx * bds
        dacs = _dot2_data_lhs(edy * cs - q, reduce_heads)
        dtot = jnp.sum(_dot2_data_lhs(q + jnp.exp(tot_exp) * dstate * state, reduce_heads), axis=0, keepdims=True)
        ddsk_ref[...] += jnp.sum(_dot2_data_lhs(dyv * x, reduce_heads), axis=0, keepdims=True)
        lane = lax.broadcasted_iota(jnp.int32, (CHUNK, LANES), 1)
        dcb = jnp.zeros((CHUNK, CHUNK), F32)
        dcbt = jnp.zeros((CHUNK, CHUNK), F32)
        ddt_x = jnp.zeros((CHUNK, LANES), F32)
        for p in range(HEADS_PER_GROUP // 2):
            sl = slice(p * LANES, (p + 1) * LANES)
            xp, dyp = xdt[:, sl], dyv[:, sl]
            xp16, dyp16 = xp.astype(BF16), dyp.astype(BF16)
            dxh = []
            for e in range(2):
                h = 2 * p + e
                mine = (lane < HEAD_DIM) if e == 0 else (lane >= HEAD_DIM)
                acs_row = acst_ref[pl.ds(g * HEADS_PER_GROUP + h, 1), :]
                dm, dmt = _pair_decay(acs_exp_v[:, sl], acs_row, e, causal)
                m, mt = cb * dm, cbt * dmt
                xh16 = jnp.where(mine, xp, 0.0).astype(BF16)
                dyh16 = jnp.where(mine, dyp, 0.0).astype(BF16)
                d_m = _dot_nt(dyh16, xp16)
                d_mt = _dot_nt(xh16, dyp16)
                dacs_h = (jnp.sum(d_m * m, axis=-1, keepdims=True)
                          - jnp.sum(d_mt * mt, axis=-1, keepdims=True))
                dacs = dacs + jnp.where(lane == HEADS_PER_GROUP * g + h, dacs_h, 0.0)
                dcb = dcb + d_m * dm
                dcbt = dcbt + d_mt * dmt
                dxh.append(_dot(mt.astype(BF16), dyp16))
            dxdt = jnp.where(lane < HEAD_DIM, dxh[0], dxh[1]) + dxdt_base[:, sl]
            dx_ref[:, sl] = dxdt * dt_exp[:, sl] + dsk_ref[:, sl] * dyp
            ddt_x = ddt_x + _dot2_data_lhs(dxdt * x[:, sl], _reduce_pair_matrix(g, p))
        dc_ref[...] = dc_acc + _dot(dcb.astype(BF16), bb16)
        db_ref[...] = db_acc + _dot(dcbt.astype(BF16), cb16)
        row = lax.broadcasted_iota(jnp.int32, (CHUNK, LANES), 0)
        return dacs + jnp.where(row == CHUNK - 1, dtot, 0.0), ddt_x

    wide, b_spec, c_spec, rows_spec, state_spec = _ssd_specs(lambda c: last - c)
    heads_spec = pl.BlockSpec((CHUNK, LANES), lambda c: (last - c, 0))
    vec_spec = pl.BlockSpec((1, LANES), lambda c: (0, 0))
    bc_out = pl.BlockSpec((CHUNK, D_BC), lambda c: (last - c, 0))
    vec_shape = jax.ShapeDtypeStruct((1, LANES), F32)
    return _call(
        body, name=name,
        out_shape=(jax.ShapeDtypeStruct((rows, D_INNER), F32), jax.ShapeDtypeStruct((rows, D_BC), F32),
                   jax.ShapeDtypeStruct((rows, D_BC), F32), jax.ShapeDtypeStruct((rows, LANES), F32),
                   vec_shape, vec_shape),
        grid=(nc,),
        in_specs=[wide, b_spec, c_spec, wide, wide, rows_spec, heads_spec, vec_spec,
                  pl.BlockSpec((1, D_INNER), lambda c: (0, 0)), wide, state_spec],
        out_specs=(wide, bc_out, bc_out, heads_spec, vec_spec, vec_spec),
        scratch_shapes=[pltpu.VMEM((N_GROUPS, D_STATE, GROUP_W), F32)],
        operands=[xbc, xbc, xbc, dt_exp, acs_exp, acs_rows, dt, a128, dskexp, dy, states],
        semantics=("arbitrary",), steps=steps)


def _attn_visible(b, heads=1):
    row = jnp.bitwise_and(lax.broadcasted_iota(jnp.int32, (heads * CHUNK, 3 * CHUNK), 0), CHUNK - 1)
    col = lax.broadcasted_iota(jnp.int32, (heads * CHUNK, 3 * CHUNK), 1)
    bb = b + jnp.zeros_like(col)
    meta = (col < CHUNK) & (bb >= 1) & (col >= PAD_ROWS)
    prev = (col >= CHUNK) & (col < 2 * CHUNK) & (bb >= 2) & ((col - CHUNK) > row)
    cur = (col >= 2 * CHUNK) & ((col - 2 * CHUNK) <= row) & ((bb >= 1) | ((col - 2 * CHUNK) >= PAD_ROWS))
    return meta | prev | cur


def _attn_visible4(b):
    return _attn_visible(b, 4)


def _stack_heads(q_ref, sink_ref, kvh, scale):
    lane = lax.broadcasted_iota(jnp.int32, (CHUNK, LANES), 1)
    parts, sinks = [], []
    for pp in range(2):
        pair = kvh * 2 + pp
        qp = q_ref[:, pair * LANES:(pair + 1) * LANES] * scale
        for e in range(2):
            mine = (lane < HEAD_DIM) if e == 0 else (lane >= HEAD_DIM)
            parts.append(jnp.where(mine, qp, 0.0).astype(BF16))
            sinks.append(jnp.full((CHUNK, 1), sink_ref[2 * pair + e], F32))
    return jnp.concatenate(parts, axis=0), jnp.concatenate(sinks, axis=0)


def _attn_operands(q_ref, k0, kp, kc, v0, vp, vc, sink_ref):
    kcat, vcat, q4, sink4 = [], [], [], []
    for kvh in range(N_KV_HEADS):
        ksl = slice(kvh * LANES, (kvh + 1) * LANES)
        kcat.append(jnp.concatenate([k0[:, ksl], kp[:, ksl], kc[:, ksl]], axis=0).astype(BF16))
        vcat.append(jnp.concatenate([v0[:, ksl], vp[:, ksl], vc[:, ksl]], axis=0).astype(BF16))
        stacked, sinks = _stack_heads(q_ref, sink_ref, kvh, ATTN_SCALE)
        q4.append(stacked)
        sink4.append(sinks)
    return kcat, vcat, q4, sink4


def _attn_probs(q4, kcat, visible, sink4):
    heads = range(N_KV_HEADS)
    s = [jnp.where(visible, _dot_nt(q4[h], kcat[h]), NEG_INF) for h in heads]
    m = [jnp.maximum(jnp.max(s[h], axis=-1, keepdims=True), sink4[h]) for h in heads]
    pe = [jnp.exp(s[h] - m[h]) for h in heads]
    pe_sink = [jnp.exp(sink4[h] - m[h]) for h in heads]
    inv = [1.0 / (jnp.sum(pe[h], axis=-1, keepdims=True) + pe_sink[h]) for h in heads]
    return [pe[h] * inv[h] for h in heads], [pe_sink[h] * inv[h] for h in heads]


def _unstack_pairs(stacked, pp):
    lane = lax.broadcasted_iota(jnp.int32, (CHUNK, LANES), 1)
    return jnp.where(lane < HEAD_DIM, stacked[(2 * pp) * CHUNK:(2 * pp + 1) * CHUNK],
                     stacked[(2 * pp + 1) * CHUNK:(2 * pp + 2) * CHUNK])


def _attn_specs(colblock):
    blk = lambda f: pl.BlockSpec((CHUNK, 2 * D_KV), f)
    return [blk(lambda b: (0, colblock)), blk(lambda b: (jnp.maximum(b - 1, 0), colblock)), blk(lambda b: (b, colblock))]


def _attn_fwd(name, q, kv2, sinks, steps=()):
    rows = q.shape[0]

    def body(q_ref, k0, kp, kc, v0, vp, vc, sink_ref, o_ref):
        visible = _attn_visible4(pl.program_id(0))
        kcat, vcat, q4, sink4 = _attn_operands(q_ref, k0, kp, kc, v0, vp, vc, sink_ref)
        pn, _ = _attn_probs(q4, kcat, visible, sink4)
        o4 = [_dot(pn[h].astype(BF16), vcat[h]) for h in range(N_KV_HEADS)]
        for kvh in range(N_KV_HEADS):
            for pp in range(2):
                qsl = slice((kvh * 2 + pp) * LANES, (kvh * 2 + pp + 1) * LANES)
                o_ref[:, qsl] = _unstack_pairs(o4[kvh], pp).astype(BF16)

    return _call(
        body, name=name, out_shape=jax.ShapeDtypeStruct((rows, D_MODEL), BF16), grid=(rows // CHUNK,),
        in_specs=[pl.BlockSpec((CHUNK, D_MODEL), lambda b: (b, 0))] + _attn_specs(0) + _attn_specs(1)
        + [pl.BlockSpec(memory_space=pltpu.SMEM)],
        out_specs=pl.BlockSpec((CHUNK, D_MODEL), lambda b: (b, 0)),
        operands=[q, kv2, kv2, kv2, kv2, kv2, kv2, sinks], semantics=("parallel",), steps=steps)


def _attn_bwd(name, q, kv2, sinks, do, steps=()):
    rows = q.shape[0]

    def body(q_ref, k0, kp, kc, v0, vp, vc, sink_ref, do_ref,
             dq_ref, dkc_ref, dkp_ref, dvc_ref, dvp_ref, dkm_ref, dvm_ref, dsink_ref):
        @pl.when(pl.program_id(0) == 0)
        def _():
            dkm_ref[...] = jnp.zeros_like(dkm_ref)
            dvm_ref[...] = jnp.zeros_like(dvm_ref)
            dsink_ref[...] = jnp.zeros_like(dsink_ref)

        visible = _attn_visible4(pl.program_id(0))
        heads = range(N_KV_HEADS)
        lane1 = lax.broadcasted_iota(jnp.int32, (1, LANES), 1)
        kcat, vcat, q4, sink4 = _attn_operands(q_ref, k0, kp, kc, v0, vp, vc, sink_ref)
        do4 = [_stack_heads(do_ref, sink_ref, h, 1.0)[0] for h in heads]
        pn, psink = _attn_probs(q4, kcat, visible, sink4)
        dp = [_dot_nt(do4[h], vcat[h]) for h in heads]
        delta = [jnp.sum(pn[h] * dp[h], axis=-1, keepdims=True) for h in heads]
        ds16 = [(pn[h] * (dp[h] - delta[h])).astype(BF16) for h in heads]
        dq4 = [_dot(ds16[h], kcat[h]) for h in heads]
        dk_acc = [_dot_tn(ds16[h], q4[h]) for h in heads]
        dv_acc = [_dot_tn(pn[h].astype(BF16), do4[h]) for h in heads]
        dsink = jnp.zeros((1, LANES), F32)
        for kvh in heads:
            ksl = slice(kvh * LANES, (kvh + 1) * LANES)
            sink_terms = psink[kvh] * delta[kvh]
            for j in range(4):
                part = jnp.sum(sink_terms[j * CHUNK:(j + 1) * CHUNK], axis=0, keepdims=True)
                dsink = dsink - jnp.where(lane1 == kvh * 4 + j, part, 0.0)
            for pp in range(2):
                qsl = slice((kvh * 2 + pp) * LANES, (kvh * 2 + pp + 1) * LANES)
                dq_ref[:, qsl] = (_unstack_pairs(dq4[kvh], pp) * ATTN_SCALE).astype(BF16)
            dkm_ref[:, ksl] += dk_acc[kvh][0:CHUNK]
            dvm_ref[:, ksl] += dv_acc[kvh][0:CHUNK]
            dkp_ref[:, ksl] = dk_acc[kvh][CHUNK:2 * CHUNK]
            dvp_ref[:, ksl] = dv_acc[kvh][CHUNK:2 * CHUNK]
            dkc_ref[:, ksl] = dk_acc[kvh][2 * CHUNK:3 * CHUNK]
            dvc_ref[:, ksl] = dv_acc[kvh][2 * CHUNK:3 * CHUNK]
        dsink_ref[...] += dsink

    qspec = pl.BlockSpec((CHUNK, D_MODEL), lambda b: (b, 0))
    kvspec = pl.BlockSpec((CHUNK, 2 * D_KV), lambda b: (b, 0))
    fixed = pl.BlockSpec((CHUNK, 2 * D_KV), lambda b: (0, 0))
    kv_shape = jax.ShapeDtypeStruct((rows, 2 * D_KV), F32)
    meta_shape = jax.ShapeDtypeStruct((CHUNK, 2 * D_KV), F32)
    return _call(
        body, name=name,
        out_shape=(jax.ShapeDtypeStruct((rows, D_MODEL), BF16), kv_shape, kv_shape, kv_shape, kv_shape,
                   meta_shape, meta_shape, jax.ShapeDtypeStruct((1, LANES), F32)),
        grid=(rows // CHUNK,),
        in_specs=[qspec] + _attn_specs(0) + _attn_specs(1) + [pl.BlockSpec(memory_space=pltpu.SMEM), qspec],
        out_specs=(qspec, kvspec, kvspec, kvspec, kvspec, fixed, fixed, pl.BlockSpec((1, LANES), lambda b: (0, 0))),
        operands=[q, kv2, kv2, kv2, kv2, kv2, kv2, sinks, do], semantics=("arbitrary",), steps=steps)


def _kv_grad_combine(name, dk_cur, dk_prev, dk_meta, dv_cur, dv_prev, dv_meta):
    rows = dk_cur.shape[0]
    nb = rows // CHUNK
    width = 2 * D_KV

    def body(kc_ref, kp_ref, km_ref, vc_ref, vp_ref, vm_ref, o_ref):
        jj = pl.program_id(0) + jnp.zeros((CHUNK, 1), jnp.int32)
        for half, (c_ref, p_ref, m_ref) in enumerate(((kc_ref, kp_ref, km_ref), (vc_ref, vp_ref, vm_ref))):
            total = c_ref[...] + jnp.where(jj < nb - 1, p_ref[...], 0.0) + jnp.where(jj == 0, m_ref[...], 0.0)
            o_ref[:, half * width:(half + 1) * width] = total.astype(BF16)

    blk = lambda f: pl.BlockSpec((CHUNK, width), f)
    three = lambda: [blk(lambda j: (j, 0)), blk(lambda j: (jnp.minimum(j + 1, nb - 1), 0)), blk(lambda j: (0, 0))]
    return pl.pallas_call(
        body, name=name, out_shape=jax.ShapeDtypeStruct((rows, 2 * width), BF16), grid=(nb,),
        in_specs=three() + three(), out_specs=pl.BlockSpec((CHUNK, 2 * width), lambda j: (j, 0)),
        compiler_params=_cparams(("parallel",)),
    )(dk_cur, dk_prev, dk_meta, dv_cur, dv_prev, dv_meta)


def _adamw(name, w, g, m, v, steps=()):
    rows, width = w.shape
    tr = rows
    for cand in range(8, rows + 1, 8):
        if rows % cand == 0 and cand * width * 4 <= (1 << 20):
            tr = cand

    def body(*refs):
        _adamw_update(*refs)

    blk = pl.BlockSpec((tr, width), lambda i: (i, 0))
    shp = jax.ShapeDtypeStruct((rows, width), F32)
    return _call(body, name=name, out_shape=(shp, shp, shp), grid=(rows // tr,), in_specs=[blk] * 4,
                 out_specs=(blk,) * 3, operands=[w, g, m, v], semantics=("parallel",), steps=steps)


def _adamw_update(w_ref, g_ref, m_ref, v_ref, d_ref, mo_ref, vo_ref):
    gv = g_ref[...]
    mn = ADAM_B1 * m_ref[...] + (1.0 - ADAM_B1) * gv
    vn = ADAM_B2 * v_ref[...] + (1.0 - ADAM_B2) * (gv * gv)
    m_hat = mn / (1.0 - ADAM_B1 ** ADAM_STEP)
    v_hat = vn / (1.0 - ADAM_B2 ** ADAM_STEP)
    d_ref[...] = -ADAM_LR * (m_hat / (jnp.sqrt(v_hat) + ADAM_EPS) + ADAM_WD * w_ref[...])
    mo_ref[...] = mn
    vo_ref[...] = vn


def _adamw_small(name, ws, gs, ms, vs):
    n = len(ws)

    def body(*refs):
        for i in range(n):
            _adamw_update(*refs[i::n])

    shapes = [jax.ShapeDtypeStruct(a.shape, F32) for a in ws]
    outs = pl.pallas_call(body, name=name, out_shape=shapes * 3, in_specs=[VMEM_SPEC] * (4 * n),
                          out_specs=[VMEM_SPEC] * (3 * n), compiler_params=_cparams())(*ws, *gs, *ms, *vs)
    return outs[:n], outs[n:2 * n], outs[2 * n:]


def _ffn_fwd(tag, h, hn, p, i, plan):
    up_g, up_v, act = _ffn_up_conv(f"ffn{tag}_up", hn, plan.weight("f_w_up", i), p["f_conv_w"][i],
                                   p["f_conv_b"][i:i + 1], steps=plan.steps(f"ffn{tag}_up"))
    pre = _mm(f"ffn{tag}_down", act, plan.weight("f_w_down", i), "nn", steps=plan.steps(f"ffn{tag}_down"))
    return pre, (h, hn, up_g, up_v, act, pre)


def _ffn_bwd(tag, dpre, saved, p, i, plan):
    h, hn, up_g, up_v, act, pre = saved
    plan.grad("f_w_down", i, _mm(f"ffn{tag}_down_dw", act, dpre, "tn", out_dtype=BF16))
    dact = _mm(f"ffn{tag}_down_dx", dpre, plan.weight("f_w_down", i), "nt", steps=plan.steps(f"ffn{tag}_down_dx"))
    gwg, gwv, gbg, gbv, dhn, g_up = _ffn_conv_bwd(
        f"ffn{tag}_conv_bwd", up_g, up_v, dact, p["f_conv_w"][i], p["f_conv_b"][i:i + 1], hn,
        plan.weight("f_w_up", i), steps=plan.steps(f"ffn{tag}_conv_bwd"))
    g_cw, g_cb = jnp.concatenate([gwg, gwv], axis=1), jnp.concatenate([gbg, gbv], axis=1)
    plan.grad("f_w_up", i, g_up)
    return dhn, dict(f_conv_w=g_cw, f_conv_b=g_cb)


def _lanes_pad(a, width=LANES):
    return jnp.pad(a, [(0, 0)] * (a.ndim - 1) + [(0, width - a.shape[-1])])


def _dup_heads(w):
    rows = w.shape[0]
    w = w.reshape(rows, 2 * N_KV_HEADS, 1, HEAD_DIM)
    return jnp.broadcast_to(w, (rows, 2 * N_KV_HEADS, 2, HEAD_DIM)).reshape(rows, 4 * D_KV)


def _undup_heads(g):
    rows = g.shape[0]
    return g.reshape(rows, 2 * N_KV_HEADS, 2, HEAD_DIM).sum(axis=2).reshape(rows, 2 * D_KV)


def _local_step(x2, target, p, plan):
    seq = x2.shape[0]
    rows = seq + CHUNK
    g = {}

    h0 = jnp.concatenate([jnp.zeros((PAD_ROWS, D_MODEL), F32), p["meta_tokens"], x2], axis=0)

    w_in = plan.weight("a_w_in")
    w_dt = jnp.pad(w_in[D_MAIN:], ((0, LANES - SSM_HEADS), (0, 0)))
    dt_bias = _lanes_pad(p["a_dt_bias"])
    a128 = _lanes_pad(-jnp.exp(p["a_a_log"]))
    dskexp = jnp.repeat(p["a_d_skip"].reshape(SSM_HEADS), HEAD_DIM).reshape(1, D_INNER)

    hn0 = _rms_fwd("a_norm", h0, p["a_norm_pre"])
    zx = _mm("a_in_main", hn0, w_in, "nt", k_rows=D_MAIN, steps=plan.steps("a_in_main"))
    dtr = _mm("a_in_dt", hn0, w_dt, "nt")
    xbc = _conv4_fwd("a_conv", zx, p["a_conv_w"], p["a_conv_b"], steps=plan.steps("a_conv"))
    dt = _dt_fwd("a_dt", dtr, dt_bias)
    dt_exp, acs_exp, acs_rows = _ssd_prep("a_ssd_prep", dt, a128, steps=plan.steps("a_ssd_prep"))
    y, states = _ssd_fwd("a_ssd", xbc, dt_exp, acs_exp, acs_rows, dskexp, steps=plan.steps("a_ssd"))
    yn = _gate_fwd("a_gate", y, zx, p["a_gate_norm"], steps=plan.steps("a_gate"))
    mix = _mm("a_out", yn, plan.weight("a_w_out"), "nn", steps=plan.steps("a_out"))
    h1, (hn_f0,) = _resid_norm_fwd("a_resid", h0, mix, p["a_norm_post"], [p["f_norm_pre"][0:1]])

    pre_f0, ffn0 = _ffn_fwd("0", h1, hn_f0, p, 0, plan)
    h2, (hkv, hn2) = _resid_norm_fwd("ffn0_resid", h1, pre_f0, p["f_norm_post"][0:1], [p["kv_norm"], p["b_norm_pre"]])

    w_kv2 = _dup_heads(plan.weight("w_kv"))
    kv2 = _mm("kv_proj", hkv, w_kv2, "nn")
    q = _mm("b_q", hn2, plan.weight("b_w_q"), "nn")
    sinks = p["b_sinks"].reshape(N_Q_HEADS)
    o = _attn_fwd("b_attn", q, kv2, sinks, steps=plan.steps("b_attn"))
    attn = _mm("b_o", o, plan.weight("b_w_o"), "nn", steps=plan.steps("b_o"))
    h3, (hn_f1,) = _resid_norm_fwd("b_resid", h2, attn, p["b_norm_post"], [p["f_norm_pre"][1:2]])

    pre_f1, ffn1 = _ffn_fwd("1", h3, hn_f1, p, 1, plan)
    dh, loss_vec, dpre_f1, g_post1 = _resid_norm_loss("ffn1_resid_loss", h3, pre_f1, p["f_norm_post"][1:2], target)
    loss = loss_vec[0, 0]

    dhn_f1, g1 = _ffn_bwd("1", dpre_f1, ffn1, p, 1, plan)
    dh, g_pre1, dpre, g["b_norm_post"] = _norm_bwd_add("ffn1_norm_bwd", dh, dhn_f1, h3, p["f_norm_pre"][1:2],
                                                        then=(attn, p["b_norm_post"]))
    plan.grad("b_w_o", None, _mm("b_o_dw", o, dpre, "tn", out_dtype=BF16))
    do = _mm("b_o_dx", dpre, plan.weight("b_w_o"), "nt", steps=plan.steps("b_o_dx"))
    dq, dkc, dkp, dvc, dvp, dkm, dvm, dsink = _attn_bwd("b_attn_bwd", q, kv2, sinks, do, steps=plan.steps("b_attn_bwd"))
    g["b_sinks"] = dsink[:, :N_Q_HEADS]
    dhn2 = _mm("b_q_dx", dq, plan.weight("b_w_q"), "nt")
    plan.grad("b_w_q", None, _mm("b_q_dw", hn2, dq, "tn", out_dtype=BF16))
    dh, g["b_norm_pre"] = _norm_bwd_add("b_norm_bwd", dh, dhn2, h2, p["b_norm_pre"])
    dkv2 = _kv_grad_combine("kv_grad", dkc, dkp, dkm, dvc, dvp, dvm)
    dhkv = _mm("kv_proj_dx", dkv2, w_kv2, "nt")
    plan.grad("w_kv", None, _undup_heads(_mm("kv_proj_dw", hkv, dkv2, "tn")))
    dh, g["kv_norm"], dpre_f0, g_post0 = _norm_bwd_add("kv_norm_bwd", dh, dhkv, h2, p["kv_norm"],
                                                       then=(pre_f0, p["f_norm_post"][0:1]))

    dhn_f0, g0 = _ffn_bwd("0", dpre_f0, ffn0, p, 0, plan)
    dh, g_pre0, dpre, g["a_norm_post"] = _norm_bwd_add("ffn0_norm_bwd", dh, dhn_f0, h1, p["f_norm_pre"][0:1],
                                                        then=(mix, p["a_norm_post"]))
    g["f_norm_post"] = jnp.concatenate([g_post0, g_post1], axis=0)
    g["f_norm_pre"] = jnp.concatenate([g_pre0, g_pre1], axis=0)
    g["f_conv_w"] = jnp.stack([g0["f_conv_w"], g1["f_conv_w"]])
    g["f_conv_b"] = jnp.concatenate([g0["f_conv_b"], g1["f_conv_b"]], axis=0)
    plan.grad("a_w_out", None, _mm("a_out_dw", yn, dpre, "tn", out_dtype=BF16))
    dyn = _mm("a_out_dx", dpre, plan.weight("a_w_out"), "nt", steps=plan.steps("a_out_dx"))
    dy, dz, g["a_gate_norm"] = _gate_bwd("a_gate_bwd", dyn, y, zx, p["a_gate_norm"])
    dxs, dbm, dcm, ddt, dalog, ddsk = _ssd_bwd("a_ssd_bwd", xbc, dt_exp, acs_exp, acs_rows, dt, a128, dskexp, dy, states,
                                              steps=plan.steps("a_ssd_bwd"))
    g["a_a_log"] = dalog[:, :SSM_HEADS]
    g["a_d_skip"] = ddsk[:, :SSM_HEADS]
    ddtr, dbias = _dt_bwd("a_dt_bwd", ddt, dtr, dt_bias)
    g["a_dt_bias"] = dbias[:, :SSM_HEADS]
    dxp, gw_x, gb_x = _conv4_bwd("a_conv_bwd_x", zx, dxs, p["a_conv_w"], p["a_conv_b"], 0)
    dbp, gw_b, gb_b = _conv4_bwd("a_conv_bwd_b", zx, dbm, p["a_conv_w"], p["a_conv_b"], D_INNER)
    dcp, gw_c, gb_c = _conv4_bwd("a_conv_bwd_c", zx, dcm, p["a_conv_w"], p["a_conv_b"], D_INNER + D_BC)
    g["a_conv_w"] = jnp.concatenate([gw_x, gw_b, gw_c], axis=1)
    g["a_conv_b"] = jnp.concatenate([gb_x, gb_b, gb_c], axis=1)
    dzx = jnp.concatenate([dz, dxp, dbp, dcp], axis=1)
    g_in = _mm("a_in_main_dw", dzx, hn0, "tn", out_dtype=BF16, out_rows=D_IN_PROJ, steps=plan.steps("a_in_main_dw"))
    plan.grad("a_w_in", None, _tn_rows_into("a_in_dt_dw", ddtr, hn0, g_in, D_MAIN, SSM_HEADS))
    dhn0 = _mm("a_in_dt_dx", ddtr, w_dt, "nn", steps=plan.steps("a_in_dt_dx"))
    dhn0 = _mm("a_in_main_dx", dzx, w_in, "nn", acc=dhn0, steps=plan.steps("a_in_main_dx"))
    dh, g["a_norm_pre"] = _norm_bwd_add("a_norm_bwd", dh, dhn0, h0, p["a_norm_pre"], steps=plan.steps("a_norm_bwd"))

    g["meta_tokens"] = dh[PAD_ROWS:CHUNK]
    return loss, dh[CHUNK:], g


ANY = pl.BlockSpec(memory_space=pl.ANY)
VMEM_SPEC = pl.BlockSpec(memory_space=pltpu.VMEM)


def _allgather_small(name, shard):
    rows = shard.shape[0]

    def body(s_ref, o_ref, send_sems, recv_sems):
        x, y, c = _place()
        me = 2 * x + y
        o_ref[me] = s_ref[...]
        chips = _other_chips(x, y)
        sends = [pltpu.make_async_remote_copy(s_ref, o_ref.at[me], send_sems.at[j], recv_sems.at[j],
                                              device_id=(cx, cy, c), device_id_type=MESH)
                 for j, (cx, cy) in enumerate(chips)]
        for cp in sends:
            cp.start()
        for j, (cx, cy) in enumerate(chips):
            pltpu.make_async_remote_copy(s_ref, o_ref.at[2 * cx + cy], send_sems.at[j], recv_sems.at[j],
                                         device_id=(cx, cy, c), device_id_type=MESH).wait_recv()
        for cp in sends:
            cp.wait_send()

    return pl.pallas_call(
        body, name=name, out_shape=jax.ShapeDtypeStruct((N_CHIPS, rows, LANES), F32),
        in_specs=[VMEM_SPEC], out_specs=VMEM_SPEC,
        scratch_shapes=[pltpu.SemaphoreType.DMA((3,)), pltpu.SemaphoreType.DMA((3,))],
        compiler_params=pltpu.CompilerParams(vmem_limit_bytes=VMEM_LIMIT),
    )(shard)


def _row_block(rows, width, itemsize, align, budget=2 << 20):
    best = rows
    for cand in range(align, rows + 1, align):
        if rows % cand == 0 and cand * width * itemsize <= budget:
            best = cand
    return best


def _cast_into_slot(name, chip, w, layer=None):
    rows, width = w.shape[-2:]
    tr = _row_block(rows, width, 4, 16)
    if layer is None:
        in_spec = pl.BlockSpec((tr, width), lambda i, chip_ref: (i, 0))
    else:
        in_spec = pl.BlockSpec((None, tr, width), lambda i, chip_ref: (layer, i, 0))

    def body(chip_ref, w_ref, o_ref):
        o_ref[...] = w_ref[...].astype(BF16)

    return pl.pallas_call(
        body, name=name, out_shape=jax.ShapeDtypeStruct((N_CHIPS, rows, width), BF16),
        grid_spec=pltpu.PrefetchScalarGridSpec(
            num_scalar_prefetch=1, grid=(rows // tr,), in_specs=[in_spec],
            out_specs=pl.BlockSpec((None, tr, width), lambda i, chip_ref: (chip_ref[0], i, 0))),
        compiler_params=_cparams(("parallel",)),
    )(chip, w)


def _allreduce_small(name, vec):
    rows = -(-vec.shape[0] // (2 * SUBLANES)) * (2 * SUBLANES)
    hr = rows // 2
    padded = jnp.pad(vec, ((0, rows - vec.shape[0]), (0, 0)))

    def body(v_ref, o_ref, theirs, pair, by_chip, send_sems, recv_sems):
        x, y, c = _place()
        me = 2 * x + y
        sibling = (x, y, 1 - c)
        mine = pl.ds(pl.multiple_of(c * hr, SUBLANES), hr)
        other = pl.ds(pl.multiple_of((1 - c) * hr, SUBLANES), hr)

        swap = _remote(v_ref, theirs, send_sems, recv_sems, 0, sibling)
        swap.start()
        swap.wait()
        south = (c + jnp.zeros((1, 1), jnp.int32)) == 0
        pair[...] = jnp.where(south, v_ref[...], theirs[...]) + jnp.where(south, theirs[...], v_ref[...])

        by_chip[me] = pair[mine, :]
        sends = [_remote(by_chip.at[me], by_chip.at[me], send_sems, recv_sems, 1 + j, (cx, cy, c))
                 for j, (cx, cy) in enumerate(_other_chips(x, y))]
        for cp in sends:
            cp.start()
        for j, (cx, cy) in enumerate(_other_chips(x, y)):
            _remote(by_chip.at[me], by_chip.at[2 * cx + cy], send_sems, recv_sems, 1 + j, (cx, cy, c)).wait_recv()
        for cp in sends:
            cp.wait_send()
        total = by_chip[0]
        for s in range(1, N_CHIPS):
            total = total + by_chip[s]

        o_ref[mine, :] = total
        back = _remote(o_ref.at[mine], o_ref.at[mine], send_sems, recv_sems, 4, sibling)
        back.start()
        _remote(o_ref.at[other], o_ref.at[other], send_sems, recv_sems, 4, sibling).wait_recv()
        back.wait_send()

    out = pl.pallas_call(
        body, name=name, out_shape=jax.ShapeDtypeStruct((rows, LANES), F32),
        in_specs=[VMEM_SPEC], out_specs=VMEM_SPEC,
        scratch_shapes=[pltpu.VMEM((rows, LANES), F32), pltpu.VMEM((rows, LANES), F32),
                        pltpu.VMEM((N_CHIPS, hr, LANES), F32), pltpu.SemaphoreType.DMA((5,)),
                        pltpu.SemaphoreType.DMA((5,))],
        compiler_params=pltpu.CompilerParams(vmem_limit_bytes=VMEM_LIMIT),
    )(padded)
    return out[:vec.shape[0]]


def _rs_pair_add(name, place, grads, partner, split="rows"):
    _, half_rows, width = partner.shape
    tr = _row_block(half_rows, width, 2, 16)
    nb = half_rows // tr
    if split == "rows":
        mine = pl.BlockSpec((None, tr, width), lambda s, i, pr: (s, pr[1] * nb + i, 0))
    else:
        mine = pl.BlockSpec((None, tr, width), lambda s, i, pr: (s, i, pr[1]))

    def body(place_ref, g_ref, p_ref, o_ref):
        o_ref[...] = (g_ref[...].astype(F32) + p_ref[...].astype(F32)).astype(BF16)

    return pl.pallas_call(
        body, name=name, out_shape=jax.ShapeDtypeStruct(partner.shape, BF16),
        grid_spec=pltpu.PrefetchScalarGridSpec(
            num_scalar_prefetch=1, grid=(N_CHIPS, nb),
            in_specs=[mine, pl.BlockSpec((None, tr, width), lambda s, i, pr: (s, i, 0))],
            out_specs=pl.BlockSpec((None, tr, width), lambda s, i, pr: (s, i, 0))),
        compiler_params=_cparams(("parallel", "parallel")),
    )(place, grads, partner)


def _rs_chip_add(name, place, mine, others, split="rows"):
    _, half_rows, width = mine.shape
    tr = _row_block(half_rows, width, 4, 16, budget=1 << 20)
    nb = half_rows // tr
    if split == "rows":
        out_shape, out_spec = (2 * half_rows, width), pl.BlockSpec((tr, width), lambda i, pr: (pr[1] * nb + i, 0))
    else:
        out_shape, out_spec = (half_rows, 2 * width), pl.BlockSpec((tr, width), lambda i, pr: (i, pr[1]))

    def body(place_ref, q_ref, r_ref, o_ref):
        acc = q_ref[...].astype(F32)
        for j in range(3):
            acc = acc + r_ref[j].astype(F32)
        o_ref[...] = acc

    return pl.pallas_call(
        body, name=name, out_shape=jax.ShapeDtypeStruct(out_shape, F32),
        grid_spec=pltpu.PrefetchScalarGridSpec(
            num_scalar_prefetch=1, grid=(nb,),
            in_specs=[pl.BlockSpec((None, tr, width), lambda i, pr: (pr[0], i, 0)),
                      pl.BlockSpec((3, tr, width), lambda i, pr: (0, i, 0))],
            out_specs=out_spec),
        compiler_params=_cparams(("parallel",)),
    )(place, mine, others)


WEIGHTS = ["meta_tokens", "a_norm_pre", "a_w_in", "a_conv_w", "a_conv_b", "a_dt_bias", "a_a_log", "a_d_skip",
           "a_gate_norm", "a_w_out", "a_norm_post", "kv_norm", "w_kv", "b_norm_pre", "b_w_q", "b_sinks", "b_w_o",
           "b_norm_post", "f_norm_pre", "f_w_up", "f_conv_w", "f_conv_b", "f_w_down", "f_norm_post"]
FULL_SHAPE = {
    "meta_tokens": (16, 1024), "a_norm_pre": (1, 1024), "a_w_in": (1, 1024, 5152), "a_conv_w": (1, 4, 3072),
    "a_conv_b": (1, 3072), "a_dt_bias": (1, 32), "a_a_log": (1, 32), "a_d_skip": (1, 32), "a_gate_norm": (1, 2048),
    "a_w_out": (1, 2048, 1024), "a_norm_post": (1, 1024), "kv_norm": (1024,), "w_kv": (1024, 512),
    "b_norm_pre": (1, 1024), "b_w_q": (1, 1024, 1024), "b_sinks": (1, 16), "b_w_o": (1, 1024, 1024),
    "b_norm_post": (1, 1024), "f_norm_pre": (2, 1024), "f_w_up": (2, 1024, 5632), "f_conv_w": (2, 3, 5632),
    "f_conv_b": (2, 5632), "f_w_down": (2, 2816, 1024), "f_norm_post": (2, 1024),
}
SHARD_AXIS = {
    "meta_tokens": 1, "a_norm_pre": 1, "a_w_in": 2, "a_conv_w": 2, "a_conv_b": 1, "a_dt_bias": None, "a_a_log": None,
    "a_d_skip": None, "a_gate_norm": 1, "a_w_out": 1, "a_norm_post": 1, "kv_norm": None, "w_kv": 0, "b_norm_pre": None,
    "b_w_q": 1, "b_sinks": None, "b_w_o": 1, "b_norm_post": None, "f_norm_pre": None, "f_w_up": 2, "f_conv_w": 2,
    "f_conv_b": None, "f_w_down": 1, "f_norm_post": None,
}
BIG = ["a_w_in", "a_w_out", "w_kv", "b_w_q", "b_w_o", "f_w_up", "f_w_down"]
SMALL = [n for n in WEIGHTS if n not in BIG]
SMALL_SHARDED = [n for n in SMALL if SHARD_AXIS[n] is not None]


def _shard_shape(name):
    shape = list(FULL_SHAPE[name])
    if SHARD_AXIS[name] is not None:
        shape[SHARD_AXIS[name]] //= N_CHIPS
    return tuple(shape)


def _numel(shape):
    return int(math.prod(shape))


SUBLANES = 8


def _packed_rows(shape):
    rows = -(-_numel(shape) // LANES)
    return -(-rows // SUBLANES) * SUBLANES


def _pack(arrays):
    parts = []
    for a in arrays:
        size, rows = _numel(a.shape), _packed_rows(a.shape)
        if size % LANES == 0:
            part = jnp.pad(a.reshape(size // LANES, LANES), ((0, rows - size // LANES), (0, 0)))
        else:
            part = jnp.pad(a.reshape(-1), (0, rows * LANES - size)).reshape(rows, LANES)
        parts.append(part)
    return jnp.concatenate(parts, axis=0)


def _unpack(packed, names, shape_of):
    out, off = {}, 0
    lead = packed.shape[:-2]
    for n in names:
        shape = tuple(shape_of(n))
        size, rows = _numel(shape), _packed_rows(shape)
        part = packed[..., off:off + rows, :]
        if size % LANES == 0:
            out[n] = part[..., :size // LANES, :].reshape(lead + shape)
        else:
            out[n] = part.reshape(lead + (rows * LANES,))[..., :size].reshape(lead + shape)
        off += rows
    return out


def _split_chips(name, full):
    ax = SHARD_AXIS[name]
    shape = full.shape
    cut = shape[:ax] + (N_CHIPS, shape[ax] // N_CHIPS) + shape[ax + 1:]
    return jnp.moveaxis(full.reshape(cut), ax, 0)


def _join_chips(name, stacked):
    ax = SHARD_AXIS[name]
    moved = jnp.moveaxis(stacked, 0, ax)
    shape = moved.shape
    return moved.reshape(shape[:ax] + (shape[ax] * shape[ax + 1],) + shape[ax + 2:])


def _as2d(a):
    return a.reshape(-1, a.shape[-1])


BUFFERS = [("a_w_in", "a_w_in", None), ("a_w_out", "a_w_out", None), ("w_kv", "w_kv", None),
           ("b_w_q", "b_w_q", None), ("b_w_o", "b_w_o", None), ("f_w_up0", "f_w_up", 0), ("f_w_up1", "f_w_up", 1),
           ("f_w_down0", "f_w_down", 0), ("f_w_down1", "f_w_down", 1)]


TRANSPOSED = ("a_w_in",)
SPLIT = {"a_w_in": "cols"}


def _local_shard(arrays, weight, layer):
    if weight in TRANSPOSED:
        return arrays[weight][0].T
    return _as2d(arrays[weight]) if layer is None else arrays[weight]


def _weight_from_gathered(weight, buf):
    if weight == "f_w_up":
        return buf
    return buf.reshape(N_CHIPS * buf.shape[1], buf.shape[2])


def _gathered_from_grad(weight, g):
    if weight == "f_w_up":
        return g
    return g.reshape(N_CHIPS, g.shape[0] // N_CHIPS, g.shape[1]).astype(BF16)


GATHER_SCHEDULE = {
    "a_in_main": [("ici", ["a_w_out"])],
    "a_conv": [("d2d", ["a_w_out"]), ("ici", ["f_w_down0"])],
    "a_ssd_prep": [("d2d", ["f_w_down0"]), ("ici_near", ["f_w_up0"])],
    "a_ssd": [("ici_far", ["f_w_up0"])],
    "a_gate": [("d2d", ["f_w_up0"]), ("ici", ["w_kv", "b_w_q", "b_w_o"])],
    "ffn0_up": [("d2d", ["w_kv", "b_w_q", "b_w_o"]), ("ici", ["f_w_down1"])],
    "ffn0_down": [("d2d", ["f_w_down1"])],
    "b_attn": [("ici", ["f_w_up1"])],
    "b_o": [("d2d", ["f_w_up1"])],
}
REDUCE_SCHEDULE = {
    "b_attn_bwd": [("all", ["f_w_down1", "f_w_up1", "b_w_o"])],
    "ffn0_conv_bwd": [("all", ["b_w_q", "w_kv", "f_w_down0"])],
    "a_ssd_bwd": [("all", ["f_w_up0", "a_w_out"])],
    "a_in_main_dx": [("near", ["a_w_in"])],
    "a_norm_bwd": [("far", ["a_w_in"])],
}
REDUCE_LAST = ("a_w_in",)
ICI_PEERS = {"ici": ALL_PEERS, "ici_near": NEAR_PEERS, "ici_far": FAR_PEERS,
             "all": ALL_PEERS, "near": NEAR_PEERS, "far": FAR_PEERS}
PAIR_SCHEDULE = {
    "b_o_dx": ["f_w_down1", "f_w_up1", "b_w_o"],
    "ffn0_down_dx": ["b_w_q", "w_kv", "f_w_down0"],
    "a_out_dx": ["f_w_up0", "a_w_out"],
    "a_in_dt_dx": ["a_w_in"],
}
SWAP_SCHEDULE = {"a_in_main_dw": ["f_w_down1", "f_w_up1", "b_w_o", "b_w_q", "w_kv", "f_w_down0", "f_w_up0", "a_w_out"]}


def _buffer_of(weight, layer):
    return weight if layer is None else f"{weight}{layer}"


class _Pipeline:
    def __init__(self, place, slots):
        self.place = place
        self.slots = dict(slots)
        self.running = []
        self.grads = {}
        self.theirs = {}
        self.partials = {}
        self.peers = {}
        self.reduced = {}

    def _collect(self):
        for step, buffers, table in self.running:
            table.update(zip(buffers, step.results))
        self.running = []

    @staticmethod
    def _splits(buffers):
        return [SPLIT.get(b, "rows") for b in buffers]

    def gather_now(self, name, buffers):
        step = _step_gather_full([self.slots[b] for b in buffers], self._splits(buffers))
        _run_steps(name, [step])
        self.slots.update(zip(buffers, step.results))

    def weight(self, name, layer=None):
        self._collect()
        return _weight_from_gathered(name, self.slots[_buffer_of(name, layer)])

    def grad(self, name, layer, g):
        self.grads[_buffer_of(name, layer)] = _gathered_from_grad(name, g)

    def steps(self, kernel):
        self._collect()
        steps = []
        for phase, buffers in GATHER_SCHEDULE.get(kernel, []):
            bufs, splits = [self.slots[b] for b in buffers], self._splits(buffers)
            step = (_step_gather_d2d(bufs, splits) if phase == "d2d"
                    else _step_gather_ici(bufs, splits, ICI_PEERS[phase]))
            self.running.append((step, buffers, self.slots))
            steps.append(step)
        buffers = PAIR_SCHEDULE.get(kernel)
        if buffers:
            step = _step_pair_exchange([self.grads[b] for b in buffers], self._splits(buffers))
            self.running.append((step, buffers, self.theirs))
            steps.append(step)
        for part, buffers in REDUCE_SCHEDULE.get(kernel, []):
            for b in buffers:
                if b not in self.partials:
                    self.partials[b] = _rs_pair_add("reduce_pair_add_" + b, self.place, self.grads[b], self.theirs[b],
                                                    SPLIT.get(b, "rows"))
            started = [self.peers[b] for b in buffers] if all(b in self.peers for b in buffers) else None
            step = _step_chip_exchange([self.partials[b] for b in buffers], ICI_PEERS[part], into=started)
            self.running.append((step, buffers, self.peers))
            steps.append(step)
        buffers = SWAP_SCHEDULE.get(kernel)
        if buffers:
            step = self._swap_step(buffers)
            self.running.append((step, buffers, self.reduced))
            steps.append(step)
        return steps

    def _swap_step(self, buffers):
        halves = [_rs_chip_add("reduce_chip_add_" + b, self.place, self.partials[b], self.peers[b], SPLIT.get(b, "rows"))
                  for b in buffers]
        return _step_pair_gather(halves, self._splits(buffers))

    def shard(self, buffer):
        self._collect()
        return self.reduced[buffer]

    def finish(self):
        self._collect()
        rest = [b for b, _, _ in BUFFERS if b not in self.reduced]
        step = self._swap_step(rest)
        _run_steps("reduce_pair_gather", [step])
        self.reduced.update(zip(rest, step.results))


def kernel(x, meta_tokens, a_norm_pre, a_w_in, a_conv_w, a_conv_b, a_dt_bias, a_a_log, a_d_skip, a_gate_norm, a_w_out, a_norm_post, kv_norm, w_kv, b_norm_pre, b_w_q, b_sinks, b_w_o, b_norm_post, f_norm_pre, f_w_up, f_conv_w, f_conv_b, f_w_down, f_norm_post, loss_target, m_meta_tokens, m_a_norm_pre, m_a_w_in, m_a_conv_w, m_a_conv_b, m_a_dt_bias, m_a_a_log, m_a_d_skip, m_a_gate_norm, m_a_w_out, m_a_norm_post, m_kv_norm, m_w_kv, m_b_norm_pre, m_b_w_q, m_b_sinks, m_b_w_o, m_b_norm_post, m_f_norm_pre, m_f_w_up, m_f_conv_w, m_f_conv_b, m_f_w_down, m_f_norm_post, v_meta_tokens, v_a_norm_pre, v_a_w_in, v_a_conv_w, v_a_conv_b, v_a_dt_bias, v_a_a_log, v_a_d_skip, v_a_gate_norm, v_a_w_out, v_a_norm_post, v_kv_norm, v_w_kv, v_b_norm_pre, v_b_w_q, v_b_sinks, v_b_w_o, v_b_norm_post, v_f_norm_pre, v_f_w_up, v_f_conv_w, v_f_conv_b, v_f_w_down, v_f_norm_post):
    given = dict(locals())
    w = {n: given[n] for n in WEIGHTS}
    mom = {n: given["m_" + n] for n in WEIGHTS}
    var = {n: given["v_" + n] for n in WEIGHTS}
    chip = 2 * lax.axis_index("x") + lax.axis_index("y")
    core = lax.axis_index("c")
    place = jnp.stack([chip, core]).astype(jnp.int32)

    small_all = _allgather_small("gather_small", _pack([w[n] for n in SMALL_SHARDED]))
    small_parts = _unpack(small_all, SMALL_SHARDED, _shard_shape)
    slots = {b: _cast_into_slot("cast_" + b, place, _local_shard(w, wn, layer), layer) for b, wn, layer in BUFFERS}
    pipeline = _Pipeline(place, slots)
    pipeline.gather_now("gather_first", ["a_w_in"])
    p = {}
    for n in SMALL:
        p[n] = _join_chips(n, small_parts[n]) if n in SMALL_SHARDED else w[n]
    p["a_conv_w"] = p["a_conv_w"][0]
    p["kv_norm"] = p["kv_norm"].reshape(1, D_MODEL)

    loss_local, grad_x, g = _local_step(x[0], loss_target[0], p, pipeline)

    small_sum = _allreduce_small("reduce_small", _pack([g[n].reshape(FULL_SHAPE[n]) for n in SMALL]
                                                       + [loss_local.reshape(1, 1)]))
    small_red = _unpack(small_sum, SMALL + ["loss"], lambda n: (1, 1) if n == "loss" else FULL_SHAPE[n])
    loss = small_red["loss"][0, 0]
    grads = {}
    for n in SMALL:
        if SHARD_AXIS[n] is None:
            grads[n] = small_red[n]
        else:
            grads[n] = lax.dynamic_index_in_dim(_split_chips(n, small_red[n]), chip, 0, keepdims=False)

    delta, new_m, new_v = {}, {}, {}
    for n in sorted(BIG, key=lambda name: name in REDUCE_LAST):
        shape = _shard_shape(n)
        if n in REDUCE_LAST:
            pipeline.finish()
        if n in TRANSPOSED:
            g2d = pipeline.shard(n)
            w2d, m2d, v2d = (arrays[n][0].T for arrays in (w, mom, var))
            back = lambda a: a.T.reshape(shape)
        else:
            g2d = (jnp.concatenate([pipeline.shard(n + "0"), pipeline.shard(n + "1")], axis=0)
                   if n in ("f_w_up", "f_w_down") else pipeline.shard(n))
            w2d, m2d, v2d = (_as2d(arrays[n]) for arrays in (w, mom, var))
            back = lambda a: a.reshape(shape)
        d, m2, v2 = _adamw("adamw_" + n, w2d, g2d, m2d, v2d, steps=pipeline.steps("adamw_" + n))
        grads[n], delta[n], new_m[n], new_v[n] = back(g2d), back(d), back(m2), back(v2)
    at_least_2d = lambda n: (1,) * (2 - len(_shard_shape(n))) + _shard_shape(n)
    outs = _adamw_small("adamw_small", *[[src[n].reshape(at_least_2d(n)) for n in SMALL] for src in (w, grads, mom, var)])
    for dst, arrays in zip((delta, new_m, new_v), outs):
        dst.update({n: a.reshape(_shard_shape(n)) for n, a in zip(SMALL, arrays)})

    return (loss, grad_x[None], *[grads[n].reshape(_shard_shape(n)) for n in WEIGHTS],
            *[delta[n] for n in WEIGHTS], *[new_m[n] for n in WEIGHTS], *[new_v[n] for n in WEIGHTS])
```

```python
import functools
import math

import jax
import jax.numpy as jnp
from jax import lax
from jax.experimental import pallas as pl
from jax.experimental.pallas import tpu as pltpu

F32, BF16 = jnp.float32, jnp.bfloat16
MESH = pl.DeviceIdType.MESH

D_MODEL = 1024
N_META = 16
CHUNK = 128
PAD_ROWS = CHUNK - N_META
D_INNER = 2048
D_STATE = 128
N_GROUPS = 4
HEADS_PER_GROUP = 8
SSM_HEADS = 32
HEAD_DIM = 64
D_BC = N_GROUPS * D_STATE
D_XBC = D_INNER + 2 * D_BC
D_MAIN = D_INNER + D_XBC
D_IN_PROJ = D_MAIN + SSM_HEADS
GROUP_W = HEADS_PER_GROUP * HEAD_DIM
SSM_CONV = 4
D_FF = 2816
FFN_CONV = 3
N_Q_HEADS = 16
N_KV_HEADS = 4
D_KV = 256
ATTN_SCALE = 1.0 / math.sqrt(HEAD_DIM)
RMS_EPS = 1e-6
NEG_INF = -1e30
LANES = 128
VMEM_PHYSICAL = 64 * 1024 * 1024
VMEM_LIMIT = VMEM_PHYSICAL - 2 * 1024 * 1024

ADAM_LR, ADAM_B1, ADAM_B2, ADAM_EPS, ADAM_WD, ADAM_STEP = 0.001, 0.9, 0.999, 1e-08, 0.01, 10

N_CHIPS = 4


def _cparams(sem=None):
    return pltpu.CompilerParams(dimension_semantics=sem, vmem_limit_bytes=VMEM_LIMIT)


def _tile(n, cands=(512, 256, 128)):
    for t in cands:
        if n % t == 0:
            return t
    return n


def _row_tile(rows, width):
    for t in (544, 272):
        if rows % t == 0 and t * width * 4 <= (3 << 20):
            return t
    return 128


def _rows_mask(i, tm):
    rows = i * tm + lax.broadcasted_iota(jnp.int32, (tm, 1), 0)
    return rows >= PAD_ROWS


def _dot(a, b):
    return jnp.dot(a, b, preferred_element_type=F32)


def _dot_nt(a, b):
    return lax.dot_general(a, b, (((1,), (1,)), ((), ())), preferred_element_type=F32)


def _dot_tn(a, b):
    return lax.dot_general(a, b, (((0,), (0,)), ((), ())), preferred_element_type=F32)


def _sigmoid(x):
    return 1.0 / (1.0 + jnp.exp(-x))


def _place():
    return lax.axis_index("x"), lax.axis_index("y"), lax.axis_index("c")


def _other_chips(x, y):
    return [(1 - x, y), (x, 1 - y), (1 - x, 1 - y)]


class _Step:
    def __init__(self, ins, outs, aliases, n_sems, start, finish):
        self.ins, self.outs, self.aliases, self.n_sems = list(ins), list(outs), dict(aliases), n_sems
        self.start, self.finish = start, finish
        self.results = None


def _like(a):
    return jax.ShapeDtypeStruct(a.shape, a.dtype)


def _remote(src, dst, send_sems, recv_sems, k, device):
    return pltpu.make_async_remote_copy(src, dst, send_sems.at[k], recv_sems.at[k], device_id=device, device_id_type=MESH)


def _half(ref, split, which, lead=()):
    if split == "rows":
        hr = ref.shape[-2] // 2
        return ref.at[lead + (pl.ds(which * hr, hr),)]
    hc = ref.shape[-1] // 2
    return ref.at[lead + (slice(None), pl.ds(which * hc, hc))]


def _splits(bufs, splits):
    return list(splits) if splits is not None else ["rows"] * len(bufs)


ALL_PEERS = (0, 1, 2)
NEAR_PEERS = (0, 1)
FAR_PEERS = (2,)


def _step_gather_ici(bufs, splits=None, peers=ALL_PEERS):
    splits = _splits(bufs, splits)

    def copies(outs, send_sems, recv_sems, received):
        x, y, c = _place()
        me = 2 * x + y
        for k, o in enumerate(outs):
            for j, (cx, cy) in enumerate(_other_chips(x, y)):
                if j in peers:
                    part = _half(o, splits[k], c, (2 * cx + cy if received else me,))
                    yield _remote(part, part, send_sems, recv_sems, 3 * k + j, (cx, cy, c))

    def start(ins, outs, send_sems, recv_sems):
        for cp in copies(outs, send_sems, recv_sems, False):
            cp.start()

    def finish(ins, outs, send_sems, recv_sems):
        for cp in copies(outs, send_sems, recv_sems, True):
            cp.wait_recv()
        for cp in copies(outs, send_sems, recv_sems, False):
            cp.wait_send()

    return _Step(bufs, [_like(b) for b in bufs], {k: k for k in range(len(bufs))}, 3 * len(bufs), start, finish)


def _step_gather_d2d(bufs, splits=None):
    splits = _splits(bufs, splits)

    def copies(outs, send_sems, recv_sems, received):
        x, y, c = _place()
        for k, o in enumerate(outs):
            for j, (cx, cy) in enumerate(_other_chips(x, y)):
                part = _half(o, splits[k], 1 - c if received else c, (2 * cx + cy,))
                yield _remote(part, part, send_sems, recv_sems, 3 * k + j, (x, y, 1 - c))

    def start(ins, outs, send_sems, recv_sems):
        for cp in copies(outs, send_sems, recv_sems, False):
            cp.start()

    def finish(ins, outs, send_sems, recv_sems):
        for cp in copies(outs, send_sems, recv_sems, True):
            cp.wait_recv()
        for cp in copies(outs, send_sems, recv_sems, False):
            cp.wait_send()

    return _Step(bufs, [_like(b) for b in bufs], {k: k for k in range(len(bufs))}, 3 * len(bufs), start, finish)


def _step_gather_full(bufs, splits=None):
    n = len(bufs)
    splits = _splits(bufs, splits)

    def ici(outs, send_sems, recv_sems, received):
        x, y, c = _place()
        me = 2 * x + y
        for k, o in enumerate(outs):
            for j, (cx, cy) in enumerate(_other_chips(x, y)):
                part = _half(o, splits[k], c, (2 * cx + cy if received else me,))
                yield _remote(part, part, send_sems, recv_sems, 3 * k + j, (cx, cy, c))

    def d2d(outs, send_sems, recv_sems, received):
        x, y, c = _place()
        for k, o in enumerate(outs):
            for j, (cx, cy) in enumerate(_other_chips(x, y)):
                part = _half(o, splits[k], 1 - c if received else c, (2 * cx + cy,))
                yield _remote(part, part, send_sems, recv_sems, 3 * n + 3 * k + j, (x, y, 1 - c))

    def start(ins, outs, send_sems, recv_sems):
        for cp in ici(outs, send_sems, recv_sems, False):
            cp.start()

    def finish(ins, outs, send_sems, recv_sems):
        for arrived, onward in zip(ici(outs, send_sems, recv_sems, True), d2d(outs, send_sems, recv_sems, False)):
            arrived.wait_recv()
            onward.start()
        for cp in d2d(outs, send_sems, recv_sems, True):
            cp.wait_recv()
        for cp in ici(outs, send_sems, recv_sems, False):
            cp.wait_send()
        for cp in d2d(outs, send_sems, recv_sems, False):
            cp.wait_send()

    return _Step(bufs, [_like(b) for b in bufs], {k: k for k in range(n)}, 6 * n, start, finish)


def _half_shape(shape, split):
    return shape[:-2] + ((shape[-2] // 2, shape[-1]) if split == "rows" else (shape[-2], shape[-1] // 2))


def _step_pair_exchange(grads, splits=None):
    splits = _splits(grads, splits)

    def copies(ins, outs, send_sems, recv_sems):
        x, y, c = _place()
        for k, (g, o) in enumerate(zip(ins, outs)):
            yield _remote(_half(g, splits[k], 1 - c, (slice(None),)), o, send_sems, recv_sems, k, (x, y, 1 - c))

    def start(ins, outs, send_sems, recv_sems):
        for cp in copies(ins, outs, send_sems, recv_sems):
            cp.start()

    def finish(ins, outs, send_sems, recv_sems):
        for cp in copies(ins, outs, send_sems, recv_sems):
            cp.wait()

    outs = [jax.ShapeDtypeStruct(_half_shape(g.shape, s), g.dtype) for g, s in zip(grads, splits)]
    return _Step(grads, outs, {}, len(grads), start, finish)


def _step_chip_exchange(partials, peers=ALL_PEERS, into=None):
    n = len(partials)

    def copies(ins, outs, send_sems, recv_sems):
        x, y, c = _place()
        for k, (q, o) in enumerate(zip(ins[:n], outs)):
            for j, (cx, cy) in enumerate(_other_chips(x, y)):
                if j in peers:
                    yield _remote(q.at[2 * cx + cy], o.at[j], send_sems, recv_sems, 3 * k + j, (cx, cy, c))

    def start(ins, outs, send_sems, recv_sems):
        for cp in copies(ins, outs, send_sems, recv_sems):
            cp.start()

    def finish(ins, outs, send_sems, recv_sems):
        for cp in copies(ins, outs, send_sems, recv_sems):
            cp.wait()

    outs = [jax.ShapeDtypeStruct((3,) + q.shape[1:], q.dtype) for q in partials]
    if into is None:
        return _Step(partials, outs, {}, 3 * n, start, finish)
    return _Step(list(partials) + list(into), outs, {n + k: k for k in range(n)}, 3 * n, start, finish)


def _step_pair_gather(shards, splits=None):
    splits = _splits(shards, splits)

    def copies(outs, send_sems, recv_sems, received):
        x, y, c = _place()
        for k, o in enumerate(outs):
            part = _half(o, splits[k], 1 - c if received else c)
            yield _remote(part, part, send_sems, recv_sems, k, (x, y, 1 - c))

    def start(ins, outs, send_sems, recv_sems):
        for cp in copies(outs, send_sems, recv_sems, False):
            cp.start()

    def finish(ins, outs, send_sems, recv_sems):
        for cp in copies(outs, send_sems, recv_sems, True):
            cp.wait_recv()
        for cp in copies(outs, send_sems, recv_sems, False):
            cp.wait_send()

    return _Step(shards, [_like(s) for s in shards], {k: k for k in range(len(shards))}, len(shards), start, finish)


def _call(body, *, name, out_shape, grid, in_specs, out_specs, operands, scratch_shapes=(), semantics=None, steps=()):
    single = not isinstance(out_shape, (tuple, list))
    out_shapes = [out_shape] if single else list(out_shape)
    out_spec_list = [out_specs] if single else list(out_specs)
    steps = list(steps)
    if not steps:
        res = pl.pallas_call(body, name=name, out_shape=out_shapes, grid=grid, in_specs=list(in_specs),
                             out_specs=out_spec_list, scratch_shapes=list(scratch_shapes),
                             compiler_params=_cparams(semantics))(*operands)
        return res[0] if single else res
    n_in, n_out, n_scr = len(operands), len(out_shapes), len(scratch_shapes)
    x_in = [a for s in steps for a in s.ins]
    x_out = [o for s in steps for o in s.outs]
    aliases, in_off, out_off = {}, 0, 0
    for s in steps:
        for i, o in s.aliases.items():
            aliases[n_in + in_off + i] = n_out + out_off + o
        in_off += len(s.ins)
        out_off += len(s.outs)
    sems = []
    for s in steps:
        sems += [pltpu.SemaphoreType.DMA((s.n_sems,)), pltpu.SemaphoreType.DMA((s.n_sems,))]
    any_spec = pl.BlockSpec(memory_space=pl.ANY)

    def carried(*refs):
        pos = 0
        ins = refs[pos:pos + n_in]; pos += n_in
        xi = refs[pos:pos + len(x_in)]; pos += len(x_in)
        outs = refs[pos:pos + n_out]; pos += n_out
        xo = refs[pos:pos + len(x_out)]; pos += len(x_out)
        scr = refs[pos:pos + n_scr]; pos += n_scr
        sem_refs = refs[pos:]

        def each(action):
            i0 = o0 = 0
            for k, s in enumerate(steps):
                getattr(s, action)(xi[i0:i0 + len(s.ins)], xo[o0:o0 + len(s.outs)], sem_refs[2 * k], sem_refs[2 * k + 1])
                i0 += len(s.ins)
                o0 += len(s.outs)

        if grid:
            first = functools.reduce(jnp.logical_and, [pl.program_id(d) == 0 for d in range(len(grid))])
            last = functools.reduce(jnp.logical_and, [pl.program_id(d) == grid[d] - 1 for d in range(len(grid))])
            pl.when(first)(lambda: each("start"))
            body(*ins, *outs, *scr)
            pl.when(last)(lambda: each("finish"))
        else:
            each("start")
            body(*ins, *outs, *scr)
            each("finish")

    res = pl.pallas_call(
        carried, name=name, out_shape=out_shapes + x_out, grid=grid,
        in_specs=list(in_specs) + [any_spec] * len(x_in), out_specs=out_spec_list + [any_spec] * len(x_out),
        scratch_shapes=list(scratch_shapes) + sems, input_output_aliases=aliases,
        compiler_params=_cparams(None if semantics is None else ("arbitrary",) * len(grid)),
    )(*operands, *x_in)
    o0 = n_out
    for s in steps:
        s.results = list(res[o0:o0 + len(s.outs)])
        o0 += len(s.outs)
    return res[0] if single else tuple(res[:n_out])


def _run_steps(name, steps):
    _call(lambda: None, name=name, out_shape=[], grid=(), in_specs=[], out_specs=[], operands=[], steps=steps)
    return [s.results for s in steps]


def _mm(name, a, b, mode, out_dtype=F32, acc=None, b_colblock=0, k_rows=None, out_rows=None, steps=()):
    resident_bytes = 8 << 20
    if mode == "nn":
        m, k = a.shape
        n = b.shape[1]
        tm = m
        while tm * k * 2 > resident_bytes and tm % 32 == 0:
            tm //= 2
        tn = _tile(n)
        grid = (m // tm, n // tn)
        in_specs = [pl.BlockSpec((tm, k), lambda i, j: (i, 0)), pl.BlockSpec((k, tn), lambda i, j: (0, j))]
        out_shape, out_block = (m, n), (tm, tn)
    elif mode == "nt":
        m, n = a.shape
        k = k_rows or b.shape[0]
        tm = m
        while tm * n * 2 > resident_bytes and tm % 32 == 0:
            tm //= 2
        tk = _tile(k)
        grid = (m // tm, k // tk)
        in_specs = [pl.BlockSpec((tm, n), lambda i, j: (i, 0)), pl.BlockSpec((tk, n), lambda i, j: (j, b_colblock))]
        out_shape, out_block = (m, k), (tm, tk)
    else:
        m, k = a.shape
        n = b.shape[1]
        tk, tn = _tile(k), (n if m * n * 2 <= resident_bytes else _tile(n))
        grid = (k // tk, n // tn)
        in_specs = [pl.BlockSpec((m, tk), lambda i, j: (0, i)), pl.BlockSpec((m, tn), lambda i, j: (0, j))]
        out_shape, out_block = (out_rows or k, n), (tk, tn)
    out_spec = pl.BlockSpec(out_block, lambda i, j: (i, j))
    has_acc = acc is not None

    def body(*refs):
        a_ref, b_ref = refs[0], refs[1]
        o_ref = refs[-1]
        av, bv = a_ref[...], b_ref[...]
        if mode == "nn":
            r = _dot(av, bv)
        elif mode == "nt":
            r = _dot_nt(av, bv)
        else:
            r = _dot_tn(av, bv)
        if has_acc:
            r = r + refs[2][...]
        o_ref[...] = r.astype(o_ref.dtype)

    operands = [a, b]
    if has_acc:
        in_specs = in_specs + [out_spec]
        operands.append(acc)
    return _call(body, name=name, out_shape=jax.ShapeDtypeStruct(out_shape, out_dtype), grid=grid, in_specs=in_specs,
                 out_specs=out_spec, operands=operands, semantics=("parallel", "parallel"), steps=steps)


def _tn_rows_into(name, a, b, into, row0, nrows):
    m, k = a.shape
    n = b.shape[1]

    def body(a_ref, b_ref, into_ref, o_ref):
        o_ref[...] = _dot_tn(a_ref[...], b_ref[...])[0:nrows].astype(o_ref.dtype)

    return pl.pallas_call(
        body, name=name, out_shape=jax.ShapeDtypeStruct(into.shape, into.dtype), grid=(1,),
        in_specs=[pl.BlockSpec((m, k), lambda i: (0, 0)), pl.BlockSpec((m, n), lambda i: (0, 0)),
                  pl.BlockSpec(memory_space=pl.ANY)],
        out_specs=pl.BlockSpec((nrows, n), lambda i: (row0 // nrows, 0)),
        input_output_aliases={2: 0}, compiler_params=_cparams(("arbitrary",)),
    )(a, b, into)


def _rms_fwd(name, h, w):
    rows, width = h.shape
    tm = _row_tile(rows, width)

    def body(h_ref, w_ref, o_ref):
        x = h_ref[...]
        r = lax.rsqrt(jnp.mean(x * x, axis=-1, keepdims=True) + RMS_EPS)
        o_ref[...] = (x * r * w_ref[...]).astype(BF16)

    return pl.pallas_call(
        body, name=name, out_shape=jax.ShapeDtypeStruct((rows, width), BF16), grid=(rows // tm,),
        in_specs=[pl.BlockSpec((tm, width), lambda i: (i, 0)), pl.BlockSpec((1, width), lambda i: (0, 0))],
        out_specs=pl.BlockSpec((tm, width), lambda i: (i, 0)), compiler_params=_cparams(("parallel",)),
    )(h, w)


def _resid_norm_fwd(name, h, pre, w, next_norms=()):
    rows, width = h.shape
    tm = _row_tile(rows, width)
    n_next = len(next_norms)

    def body(*refs):
        h_ref, p_ref, w_ref = refs[:3]
        v_refs = refs[3:3 + n_next]
        o_ref = refs[3 + n_next]
        n_refs = refs[4 + n_next:]
        p = p_ref[...]
        r = lax.rsqrt(jnp.mean(p * p, axis=-1, keepdims=True) + RMS_EPS)
        x = h_ref[...] + jnp.where(_rows_mask(pl.program_id(0), tm), p * r * w_ref[...], 0.0)
        o_ref[...] = x
        if n_next:
            rx = lax.rsqrt(jnp.mean(x * x, axis=-1, keepdims=True) + RMS_EPS)
            for v_ref, n_ref in zip(v_refs, n_refs):
                n_ref[...] = (x * rx * v_ref[...]).astype(BF16)

    row_spec = pl.BlockSpec((tm, width), lambda i: (i, 0))
    vec_spec = pl.BlockSpec((1, width), lambda i: (0, 0))
    outs = pl.pallas_call(
        body, name=name,
        out_shape=[jax.ShapeDtypeStruct((rows, width), F32)] + [jax.ShapeDtypeStruct((rows, width), BF16)] * n_next,
        grid=(rows // tm,), in_specs=[row_spec, row_spec, vec_spec] + [vec_spec] * n_next,
        out_specs=[row_spec] * (1 + n_next), compiler_params=_cparams(("parallel",)),
    )(h, pre, w, *next_norms)
    return outs[0], list(outs[1:])


def _resid_norm_loss(name, h, pre, w, target):
    rows, width = h.shape

    def body(h_ref, p_ref, w_ref, t_ref, dh_ref, loss_ref, dp_ref, dw_ref):
        i = pl.program_id(0)
        p = p_ref[...]
        r = lax.rsqrt(jnp.mean(p * p, axis=-1, keepdims=True) + RMS_EPS)
        x = h_ref[...] + p * r * w_ref[...]
        real = (i + jnp.zeros((CHUNK, 1), jnp.int32)) >= 1
        diff = jnp.where(real, x - t_ref[...], 0.0)
        dh = diff * (1.0 / D_MODEL)
        dh_ref[...] = dh
        dp, dw_rows = _rms_bwd(dh, p, w_ref[...])
        dp_ref[...] = dp.astype(BF16)

        @pl.when(i == 0)
        def _():
            loss_ref[...] = jnp.zeros_like(loss_ref)
            dw_ref[...] = jnp.zeros_like(dw_ref)

        loss_ref[...] += jnp.sum(diff * diff) * (0.5 / D_MODEL)
        dw_ref[...] += jnp.sum(dw_rows, axis=0, keepdims=True)

    blk = pl.BlockSpec((CHUNK, width), lambda i: (i, 0))
    vec_spec = pl.BlockSpec((1, width), lambda i: (0, 0))
    return pl.pallas_call(
        body, name=name,
        out_shape=(jax.ShapeDtypeStruct((rows, width), F32), jax.ShapeDtypeStruct((1, LANES), F32),
                   jax.ShapeDtypeStruct((rows, width), BF16), jax.ShapeDtypeStruct((1, width), F32)),
        grid=(rows // CHUNK,),
        in_specs=[blk, blk, vec_spec, pl.BlockSpec((CHUNK, width), lambda i: (jnp.maximum(i - 1, 0), 0))],
        out_specs=(blk, pl.BlockSpec((1, LANES), lambda i: (0, 0)), blk, vec_spec),
        compiler_params=_cparams(("arbitrary",)),
    )(h, pre, w, target)


def _rms_bwd(dy, x, w):
    r = lax.rsqrt(jnp.mean(x * x, axis=-1, keepdims=True) + RMS_EPS)
    xhat = x * r
    dxhat = dy * w
    return r * (dxhat - xhat * jnp.mean(dxhat * xhat, axis=-1, keepdims=True)), dy * xhat


def _norm_bwd_add(name, dh, dhn, h, w, then=None, steps=()):
    rows, width = dh.shape
    tm = _row_tile(rows, width)
    fused = then is not None

    def body(*refs):
        dh_ref, dhn_ref, h_ref, w_ref = refs[:4]
        o_ref, dw_ref = refs[6:8] if fused else refs[4:6]
        i = pl.program_id(0)
        valid = _rows_mask(i, tm)
        dx, dw_rows = _rms_bwd(dhn_ref[...], h_ref[...], w_ref[...])
        dh_new = dh_ref[...] + jnp.where(valid, dx, 0.0)
        o_ref[...] = dh_new

        @pl.when(i == 0)
        def _():
            dw_ref[...] = jnp.zeros_like(dw_ref)

        dw_ref[...] += jnp.sum(dw_rows, axis=0, keepdims=True)
        if fused:
            p_ref, wp_ref, dp_ref, dwp_ref = refs[4], refs[5], refs[8], refs[9]
            dp, dwp_rows = _rms_bwd(jnp.where(valid, dh_new, 0.0), p_ref[...], wp_ref[...])
            dp_ref[...] = dp.astype(BF16)

            @pl.when(i == 0)
            def _():
                dwp_ref[...] = jnp.zeros_like(dwp_ref)

            dwp_ref[...] += jnp.sum(dwp_rows, axis=0, keepdims=True)

    row_spec = pl.BlockSpec((tm, width), lambda i: (i, 0))
    vec_spec = pl.BlockSpec((1, width), lambda i: (0, 0))
    row_f32, vec_f32 = jax.ShapeDtypeStruct((rows, width), F32), jax.ShapeDtypeStruct((1, width), F32)
    in_specs, operands = [row_spec, row_spec, row_spec, vec_spec], [dh, dhn, h, w]
    out_shape, out_specs = [row_f32, vec_f32], [row_spec, vec_spec]
    if fused:
        in_specs += [row_spec, vec_spec]
        operands += list(then)
        out_shape += [jax.ShapeDtypeStruct((rows, width), BF16), vec_f32]
        out_specs += [row_spec, vec_spec]
    return _call(body, name=name, out_shape=out_shape, grid=(rows // tm,), in_specs=in_specs, out_specs=out_specs,
                 operands=operands, semantics=("arbitrary",), steps=steps)


def _shift_down(x, s, rows):
    return pltpu.roll(x, s, 0) if s else x


def _shift_up(x, s, rows):
    return pltpu.roll(x, rows - s, 0) if s else x


def _conv4_fwd(name, zx, cw, cb, steps=()):
    rows = zx.shape[0]
    off = D_INNER // LANES

    def body(x_ref, w_ref, b_ref, o_ref):
        x = x_ref[...]
        acc = b_ref[...] + w_ref[pl.ds(SSM_CONV - 1, 1), :] * x
        for s in range(1, SSM_CONV):
            acc = acc + w_ref[pl.ds(SSM_CONV - 1 - s, 1), :] * _shift_down(x, s, rows)
        valid = lax.broadcasted_iota(jnp.int32, (rows, 1), 0) >= PAD_ROWS
        o_ref[...] = jnp.where(valid, acc * _sigmoid(acc), 0.0)

    return _call(
        body, name=name, out_shape=jax.ShapeDtypeStruct((rows, D_XBC), F32), grid=(D_XBC // LANES,),
        in_specs=[pl.BlockSpec((rows, LANES), lambda j: (0, j + off)),
                  pl.BlockSpec((SSM_CONV, LANES), lambda j: (0, j)),
                  pl.BlockSpec((1, LANES), lambda j: (0, j))],
        out_specs=pl.BlockSpec((rows, LANES), lambda j: (0, j)), operands=[zx, cw, cb],
        semantics=("parallel",), steps=steps)


def _conv4_bwd(name, zx, dout, cw, cb, col0):
    rows, width = dout.shape
    zoff = (D_INNER + col0) // LANES
    woff = col0 // LANES

    def body(x_ref, d_ref, w_ref, b_ref, dx_ref, dw_ref, db_ref):
        x = x_ref[...]
        shifted = [_shift_down(x, s, rows) for s in range(SSM_CONV)]
        acc = b_ref[...]
        for s in range(SSM_CONV):
            acc = acc + w_ref[pl.ds(SSM_CONV - 1 - s, 1), :] * shifted[s]
        sig = _sigmoid(acc)
        valid = lax.broadcasted_iota(jnp.int32, (rows, 1), 0) >= PAD_ROWS
        dpre = jnp.where(valid, d_ref[...] * sig * (1.0 + acc * (1.0 - sig)), 0.0)
        dx = w_ref[pl.ds(SSM_CONV - 1, 1), :] * dpre
        for s in range(1, SSM_CONV):
            dx = dx + w_ref[pl.ds(SSM_CONV - 1 - s, 1), :] * _shift_up(dpre, s, rows)
        dx_ref[...] = dx.astype(BF16)
        for s in range(SSM_CONV):
            dw_ref[pl.ds(SSM_CONV - 1 - s, 1), :] = jnp.sum(dpre * shifted[s], axis=0, keepdims=True)
        db_ref[...] = jnp.sum(dpre, axis=0, keepdims=True)

    return pl.pallas_call(
        body, name=name,
        out_shape=(jax.ShapeDtypeStruct((rows, width), BF16), jax.ShapeDtypeStruct((SSM_CONV, width), F32),
                   jax.ShapeDtypeStruct((1, width), F32)),
        grid=(width // LANES,),
        in_specs=[pl.BlockSpec((rows, LANES), lambda j: (0, j + zoff)),
                  pl.BlockSpec((rows, LANES), lambda j: (0, j)),
                  pl.BlockSpec((SSM_CONV, LANES), lambda j: (0, j + woff)),
                  pl.BlockSpec((1, LANES), lambda j: (0, j + woff))],
        out_specs=(pl.BlockSpec((rows, LANES), lambda j: (0, j)),
                   pl.BlockSpec((SSM_CONV, LANES), lambda j: (0, j)),
                   pl.BlockSpec((1, LANES), lambda j: (0, j))),
        compiler_params=_cparams(("parallel",)),
    )(zx, dout, cw, cb)


FFN_TILE = 2 * LANES


def _ffn_up_conv(name, hn, w_up, cw, cb, steps=()):
    rows, k = hn.shape
    chip_blocks = w_up.shape[2] // LANES
    half_blocks = D_FF // LANES
    nt = D_FF // FFN_TILE

    def weight_block(offset):
        return pl.BlockSpec((None, k, LANES), lambda j: ((2 * j + offset) // chip_blocks, 0, (2 * j + offset) % chip_blocks))

    def body(a_ref, g0, g1, v0, v1, wg_ref, wv_ref, bg_ref, bv_ref, upg_ref, upv_ref, act_ref):
        a = a_ref[...]
        g = _dot(a, jnp.concatenate([g0[...], g1[...]], axis=1))
        v = _dot(a, jnp.concatenate([v0[...], v1[...]], axis=1))
        upg_ref[...] = g
        upv_ref[...] = v
        ug, uv = bg_ref[...], bv_ref[...]
        for s in range(FFN_CONV):
            ug = ug + wg_ref[pl.ds(FFN_CONV - 1 - s, 1), :] * _shift_down(g, s, rows)
            uv = uv + wv_ref[pl.ds(FFN_CONV - 1 - s, 1), :] * _shift_down(v, s, rows)
        act_ref[...] = (ug * _sigmoid(ug) * uv).astype(BF16)

    col = pl.BlockSpec((rows, FFN_TILE), lambda j: (0, j))
    wsp = lambda shift: pl.BlockSpec((FFN_CONV, FFN_TILE), lambda j: (0, j + shift))
    bsp = lambda shift: pl.BlockSpec((1, FFN_TILE), lambda j: (0, j + shift))
    half = jax.ShapeDtypeStruct((rows, D_FF), F32)
    return _call(
        body, name=name, out_shape=(half, half, jax.ShapeDtypeStruct((rows, D_FF), BF16)), grid=(nt,),
        in_specs=[pl.BlockSpec((rows, k), lambda j: (0, 0)), weight_block(0), weight_block(1),
                  weight_block(half_blocks), weight_block(half_blocks + 1), wsp(0), wsp(nt), bsp(0), bsp(nt)],
        out_specs=(col, col, col), operands=[hn, w_up, w_up, w_up, w_up, cw, cw, cb, cb],
        semantics=("parallel",), steps=steps)


def _ffn_conv_bwd(name, up_g, up_v, dact, cw, cb, hn, w_up, steps=()):
    rows, k = hn.shape
    chip_blocks = w_up.shape[2] // LANES
    nt = D_FF // LANES

    def weight_block(shift):
        return pl.BlockSpec((None, k, LANES), lambda j: ((j + shift) // chip_blocks, 0, (j + shift) % chip_blocks))

    def body(g_ref, v_ref, d_ref, wg_ref, wv_ref, bg_ref, bv_ref, upg_ref, upv_ref, hn_ref,
             dwg_ref, dwv_ref, dbg_ref, dbv_ref, dhn_ref, dup_ref, acc, hn_scr, hnt_scr, dup_scr, sems):
        j = pl.program_id(0)
        hn_copy = pltpu.make_async_copy(hn_ref, hn_scr, sems.at[0])
        dhn_copy = pltpu.make_async_copy(acc, dhn_ref, sems.at[0])

        def dup_copy(step, half):
            block, slot = step + half * nt, 2 * (step % 2) + half
            cols = pl.ds(pl.multiple_of((block % chip_blocks) * LANES, LANES), LANES)
            return pltpu.make_async_copy(dup_scr.at[slot], dup_ref.at[block // chip_blocks, :, cols], sems.at[1 + slot])

        @pl.when(j == 0)
        def _():
            hn_copy.start()
            acc[...] = jnp.zeros_like(acc)
            hn_copy.wait()
            for r in range(0, rows, LANES):
                hnt_scr[:, r:r + LANES] = hn_scr[r:r + LANES, :].T

        @pl.when(j >= 2)
        def _():
            dup_copy(j - 2, 0).wait()
            dup_copy(j - 2, 1).wait()

        g, v = g_ref[...], v_ref[...]
        gs = [_shift_down(g, s, rows) for s in range(FFN_CONV)]
        vs = [_shift_down(v, s, rows) for s in range(FFN_CONV)]
        ug, uv = bg_ref[...], bv_ref[...]
        for s in range(FFN_CONV):
            ug = ug + wg_ref[pl.ds(FFN_CONV - 1 - s, 1), :] * gs[s]
            uv = uv + wv_ref[pl.ds(FFN_CONV - 1 - s, 1), :] * vs[s]
        sig = _sigmoid(ug)
        dsig = d_ref[...] * sig
        dup = []
        for dpre, src, w_ref, dw_ref, db_ref in (
                (dsig * uv * (1.0 + ug * (1.0 - sig)), gs, wg_ref, dwg_ref, dbg_ref),
                (dsig * ug, vs, wv_ref, dwv_ref, dbv_ref)):
            dx = w_ref[pl.ds(FFN_CONV - 1, 1), :] * dpre
            for s in range(1, FFN_CONV):
                dx = dx + w_ref[pl.ds(FFN_CONV - 1 - s, 1), :] * _shift_up(dpre, s, rows)
            dup.append(dx.astype(BF16))
            for s in range(FFN_CONV):
                dw_ref[pl.ds(FFN_CONV - 1 - s, 1), :] = jnp.sum(dpre * src[s], axis=0, keepdims=True)
            db_ref[...] = jnp.sum(dpre, axis=0, keepdims=True)
        dup = jnp.concatenate(dup, axis=1)
        acc[...] += _dot_nt(dup, jnp.concatenate([upg_ref[...], upv_ref[...]], axis=1))
        dw = _dot(hnt_scr[...], dup)
        slot = 2 * (j % 2)
        dup_scr[slot] = dw[:, :LANES].astype(BF16)
        dup_scr[slot + 1] = dw[:, LANES:].astype(BF16)
        dup_copy(j, 0).start()
        dup_copy(j, 1).start()

        @pl.when(j == nt - 1)
        def _():
            dhn_copy.start()
            for step in (j - 1, j):
                dup_copy(step, 0).wait()
                dup_copy(step, 1).wait()
            dhn_copy.wait()

    col = pl.BlockSpec((rows, LANES), lambda j: (0, j))
    wsp = lambda shift: pl.BlockSpec((FFN_CONV, LANES), lambda j: (0, j + shift))
    bsp = lambda shift: pl.BlockSpec((1, LANES), lambda j: (0, j + shift))
    any_spec = pl.BlockSpec(memory_space=pl.ANY)
    dw_shape = jax.ShapeDtypeStruct((FFN_CONV, D_FF), F32)
    db_shape = jax.ShapeDtypeStruct((1, D_FF), F32)
    return _call(
        body, name=name, grid=(nt,),
        out_shape=(dw_shape, dw_shape, db_shape, db_shape, jax.ShapeDtypeStruct((rows, k), F32),
                   jax.ShapeDtypeStruct(w_up.shape, BF16)),
        in_specs=[col, col, col, wsp(0), wsp(nt), bsp(0), bsp(nt), weight_block(0), weight_block(nt), any_spec],
        out_specs=(wsp(0), wsp(0), bsp(0), bsp(0), any_spec, any_spec),
        operands=[up_g, up_v, dact, cw, cw, cb, cb, w_up, w_up, hn],
        scratch_shapes=[pltpu.VMEM((rows, k), F32), pltpu.VMEM((rows, k), BF16), pltpu.VMEM((k, rows), BF16),
                        pltpu.VMEM((4, k, LANES), BF16), pltpu.SemaphoreType.DMA((5,))],
        semantics=("arbitrary",), steps=steps)


def _dt_fwd(name, dtr, bias):
    rows = dtr.shape[0]
    tm = _row_tile(rows, LANES)

    def body(d_ref, b_ref, o_ref):
        v = d_ref[...] + b_ref[...]
        sp = jnp.maximum(v, 0.0) + jnp.log1p(jnp.exp(-jnp.abs(v)))
        lane = lax.broadcasted_iota(jnp.int32, (tm, LANES), 1)
        ok = _rows_mask(pl.program_id(0), tm) & (lane < SSM_HEADS)
        o_ref[...] = jnp.where(ok, sp, 0.0)

    return pl.pallas_call(
        body, name=name, out_shape=jax.ShapeDtypeStruct((rows, LANES), F32), grid=(rows // tm,),
        in_specs=[pl.BlockSpec((tm, LANES), lambda i: (i, 0)), pl.BlockSpec((1, LANES), lambda i: (0, 0))],
        out_specs=pl.BlockSpec((tm, LANES), lambda i: (i, 0)), compiler_params=_cparams(("parallel",)),
    )(dtr, bias)


def _dt_bwd(name, ddt, dtr, bias):
    rows = dtr.shape[0]
    tm = _row_tile(rows, LANES)

    def body(g_ref, d_ref, b_ref, o_ref, db_ref):
        i = pl.program_id(0)
        lane = lax.broadcasted_iota(jnp.int32, (tm, LANES), 1)
        ok = _rows_mask(i, tm) & (lane < SSM_HEADS)
        dv = jnp.where(ok, g_ref[...] * _sigmoid(d_ref[...] + b_ref[...]), 0.0)
        o_ref[...] = dv.astype(BF16)

        @pl.when(i == 0)
        def _():
            db_ref[...] = jnp.zeros_like(db_ref)

        db_ref[...] += jnp.sum(dv, axis=0, keepdims=True)

    row_spec = pl.BlockSpec((tm, LANES), lambda i: (i, 0))
    vec_spec = pl.BlockSpec((1, LANES), lambda i: (0, 0))
    return pl.pallas_call(
        body, name=name,
        out_shape=(jax.ShapeDtypeStruct((rows, LANES), BF16), jax.ShapeDtypeStruct((1, LANES), F32)),
        grid=(rows // tm,), in_specs=[row_spec, row_spec, vec_spec], out_specs=(row_spec, vec_spec),
        compiler_params=_cparams(("arbitrary",)),
    )(ddt, dtr, bias)


def _gate_fwd(name, y, zx, w, steps=()):
    rows = y.shape[0]
    tm = _row_tile(rows, D_INNER)

    def body(y_ref, z_ref, w_ref, o_ref):
        z = z_ref[...]
        g = y_ref[...] * (z * _sigmoid(z))
        r = lax.rsqrt(jnp.mean(g * g, axis=-1, keepdims=True) + RMS_EPS)
        o_ref[...] = (g * r * w_ref[...]).astype(BF16)

    row_spec = pl.BlockSpec((tm, D_INNER), lambda i: (i, 0))
    return _call(
        body, name=name, out_shape=jax.ShapeDtypeStruct((rows, D_INNER), BF16), grid=(rows // tm,),
        in_specs=[row_spec, row_spec, pl.BlockSpec((1, D_INNER), lambda i: (0, 0))],
        out_specs=row_spec, operands=[y, zx, w], semantics=("parallel",), steps=steps)


def _gate_bwd(name, dyn, y, zx, w):
    rows = y.shape[0]
    tm = _row_tile(rows, D_INNER)

    def body(d_ref, y_ref, z_ref, w_ref, dy_ref, dz_ref, dw_ref):
        i = pl.program_id(0)
        z, yv = z_ref[...], y_ref[...]
        sig = _sigmoid(z)
        sz = z * sig
        g = yv * sz
        r = lax.rsqrt(jnp.mean(g * g, axis=-1, keepdims=True) + RMS_EPS)
        ghat = g * r
        dn = d_ref[...]
        dghat = dn * w_ref[...]
        dg = r * (dghat - ghat * jnp.mean(dghat * ghat, axis=-1, keepdims=True))
        dy_ref[...] = dg * sz
        dz_ref[...] = (dg * yv * sig * (1.0 + z * (1.0 - sig))).astype(BF16)

        @pl.when(i == 0)
        def _():
            dw_ref[...] = jnp.zeros_like(dw_ref)

        dw_ref[...] += jnp.sum(dn * ghat, axis=0, keepdims=True)

    row_spec = pl.BlockSpec((tm, D_INNER), lambda i: (i, 0))
    vec_spec = pl.BlockSpec((1, D_INNER), lambda i: (0, 0))
    return pl.pallas_call(
        body, name=name,
        out_shape=(jax.ShapeDtypeStruct((rows, D_INNER), F32), jax.ShapeDtypeStruct((rows, D_INNER), BF16),
                   jax.ShapeDtypeStruct((1, D_INNER), F32)),
        grid=(rows // tm,), in_specs=[row_spec, row_spec, row_spec, vec_spec],
        out_specs=(row_spec, row_spec, vec_spec), compiler_params=_cparams(("arbitrary",)),
    )(dyn, y, zx, w)


def _split3(x):
    hi = x.astype(BF16)
    r1 = x - hi.astype(F32)
    mid = r1.astype(BF16)
    lo = (r1 - mid.astype(F32)).astype(BF16)
    return hi, mid, lo


def _dot3_data_lhs(x, sel):
    sel16 = sel.astype(F32).astype(BF16)
    hi, mid, lo = _split3(x)
    return _dot(hi, sel16) + _dot(mid, sel16) + _dot(lo, sel16)


def _dot2_data_lhs(x, sel):
    sel16 = sel.astype(F32).astype(BF16)
    hi = x.astype(BF16)
    mid = (x - hi.astype(F32)).astype(BF16)
    return _dot(hi, sel16) + _dot(mid, sel16)


def _dot3_data_rhs(sel, x):
    sel16 = sel.astype(F32).astype(BF16)
    hi, mid, lo = _split3(x)
    return _dot(sel16, hi) + _dot(sel16, mid) + _dot(sel16, lo)


def _causal_masks():
    r = lax.broadcasted_iota(jnp.int32, (CHUNK, CHUNK), 0)
    c = lax.broadcasted_iota(jnp.int32, (CHUNK, CHUNK), 1)
    return r >= c, r <= c


def _expand_heads_matrix(g):
    k = lax.broadcasted_iota(jnp.int32, (LANES, GROUP_W), 0)
    j = lax.broadcasted_iota(jnp.int32, (LANES, GROUP_W), 1)
    return HEADS_PER_GROUP * g + jnp.right_shift(j, 6) == k


def _reduce_heads_matrix(g):
    j = lax.broadcasted_iota(jnp.int32, (GROUP_W, LANES), 0)
    k = lax.broadcasted_iota(jnp.int32, (GROUP_W, LANES), 1)
    return HEADS_PER_GROUP * g + jnp.right_shift(j, 6) == k


def _reduce_pair_matrix(g, p):
    j = lax.broadcasted_iota(jnp.int32, (LANES, LANES), 0)
    k = lax.broadcasted_iota(jnp.int32, (LANES, LANES), 1)
    return HEADS_PER_GROUP * g + 2 * p + jnp.right_shift(j, 6) == k


def _group_cols(ref, g, width):
    return ref.at[:, pl.ds(g * width, width)]


def _ssd_prep(name, dt, a128, steps=()):
    rows = dt.shape[0]
    nc = rows // CHUNK

    def body(dt_ref, a_ref, dte_ref, acs_ref, acst_ref):
        causal, _ = _causal_masks()
        dtv = dt_ref[...]
        acs = _dot3_data_rhs(causal, dtv) * a_ref[...]
        acst_ref[...] = acs.T[0:SSM_HEADS]
        for g in range(N_GROUPS):
            expand = _expand_heads_matrix(g)
            _group_cols(dte_ref, g, GROUP_W)[...] = _dot3_data_lhs(dtv, expand)
            _group_cols(acs_ref, g, GROUP_W)[...] = _dot3_data_lhs(acs, expand)

    blk = pl.BlockSpec((CHUNK, D_INNER), lambda c: (c, 0))
    shp = jax.ShapeDtypeStruct((rows, D_INNER), F32)
    return _call(
        body, name=name, out_shape=(shp, shp, jax.ShapeDtypeStruct((nc, SSM_HEADS, CHUNK), F32)), grid=(nc,),
        in_specs=[pl.BlockSpec((CHUNK, LANES), lambda c: (c, 0)), pl.BlockSpec((1, LANES), lambda c: (0, 0))],
        out_specs=(blk, blk, pl.BlockSpec((None, SSM_HEADS, CHUNK), lambda c: (c, 0, 0))),
        operands=[dt, a128], semantics=("parallel",), steps=steps)


def _ssd_common(x_ref, b_ref, c_ref, dte_ref, acs_ref):
    x = x_ref[...]
    dt_exp = dte_ref[...]
    acs_exp = acs_ref[...]
    tot_exp = acs_ref[pl.ds(CHUNK - 1, 1), :]
    xdt = x * dt_exp
    e_exp = jnp.exp(acs_exp)
    f_exp = jnp.exp(tot_exp - acs_exp)
    return _causal_masks(), x, dt_exp, acs_exp, tot_exp, xdt, e_exp, f_exp, b_ref[...], c_ref[...]


def _pair_decay(acs_pair, acs_row, e, causal):
    lane = lax.broadcasted_iota(jnp.int32, (CHUNK, LANES), 1)
    mine = (lane < HEAD_DIM) if e == 0 else (lane >= HEAD_DIM)
    a_l = jnp.where(mine, acs_pair, pltpu.roll(acs_pair, HEAD_DIM, 1))
    seg = a_l - acs_row
    dm = jnp.where(causal[0], jnp.exp(jnp.minimum(seg, 0.0)), 0.0)
    dmt = jnp.where(causal[1], jnp.exp(jnp.minimum(-seg, 0.0)), 0.0)
    return dm, dmt


def _ssd_specs(index_of_chunk):
    wide = pl.BlockSpec((CHUNK, D_INNER), lambda c: (index_of_chunk(c), 0))
    b_spec = pl.BlockSpec((CHUNK, D_BC), lambda c: (index_of_chunk(c), D_INNER // D_BC))
    c_spec = pl.BlockSpec((CHUNK, D_BC), lambda c: (index_of_chunk(c), D_INNER // D_BC + 1))
    rows_spec = pl.BlockSpec((None, SSM_HEADS, CHUNK), lambda c: (index_of_chunk(c), 0, 0))
    state_spec = pl.BlockSpec((N_GROUPS, None, D_STATE, GROUP_W), lambda c: (0, index_of_chunk(c), 0, 0))
    return wide, b_spec, c_spec, rows_spec, state_spec


def _ssd_fwd(name, xbc, dt_exp, acs_exp, acs_rows, dskexp, steps=()):
    rows = xbc.shape[0]
    nc = rows // CHUNK

    def body(x_ref, b_ref, c_ref, dte_ref, acs_ref, acst_ref, dsk_ref, y_ref, st_ref, s_scr):
        @pl.when(pl.program_id(0) == 0)
        def _():
            s_scr[...] = jnp.zeros_like(s_scr)

        lane = lax.broadcasted_iota(jnp.int32, (CHUNK, LANES), 1)
        for g in range(N_GROUPS):
            y_g = _group_cols(y_ref, g, GROUP_W)
            causal, x, _, acs_exp_v, tot_exp, xdt, e_exp, f_exp, bm, cm = _ssd_common(
                _group_cols(x_ref, g, GROUP_W), _group_cols(b_ref, g, D_STATE), _group_cols(c_ref, g, D_STATE),
                _group_cols(dte_ref, g, GROUP_W), _group_cols(acs_ref, g, GROUP_W))
            state = s_scr[g]
            st_ref[g] = state
            cb16, bb16 = cm.astype(BF16), bm.astype(BF16)
            cb = _dot_nt(cb16, bb16)
            base = e_exp * _dot(cb16, state.astype(BF16)) + _group_cols(dsk_ref, g, GROUP_W)[...] * x
            for p in range(HEADS_PER_GROUP // 2):
                sl = slice(p * LANES, (p + 1) * LANES)
                xp = xdt[:, sl].astype(BF16)
                yd = []
                for e in range(2):
                    acs_row = acst_ref[pl.ds(g * HEADS_PER_GROUP + 2 * p + e, 1), :]
                    dm, _ = _pair_decay(acs_exp_v[:, sl], acs_row, e, causal)
                    yd.append(_dot((cb * dm).astype(BF16), xp))
                y_g[:, sl] = jnp.where(lane < HEAD_DIM, yd[0], yd[1]) + base[:, sl]
            s_scr[g] = jnp.exp(tot_exp) * state + _dot_tn(bb16, (f_exp * xdt).astype(BF16))

    wide, b_spec, c_spec, rows_spec, state_spec = _ssd_specs(lambda c: c)
    return _call(
        body, name=name,
        out_shape=(jax.ShapeDtypeStruct((rows, D_INNER), F32),
                   jax.ShapeDtypeStruct((N_GROUPS, nc, D_STATE, GROUP_W), F32)),
        grid=(nc,),
        in_specs=[wide, b_spec, c_spec, wide, wide, rows_spec, pl.BlockSpec((1, D_INNER), lambda c: (0, 0))],
        out_specs=(wide, state_spec),
        scratch_shapes=[pltpu.VMEM((N_GROUPS, D_STATE, GROUP_W), F32)],
        operands=[xbc, xbc, xbc, dt_exp, acs_exp, acs_rows, dskexp], semantics=("arbitrary",), steps=steps)


def _ssd_bwd(name, xbc, dt_exp, acs_exp, acs_rows, dt, a128, dskexp, dy, states, steps=()):
    rows = xbc.shape[0]
    nc = rows // CHUNK
    last = nc - 1

    def body(x_ref, b_ref, c_ref, dte_ref, acs_ref, acst_ref, dt_ref, a128_ref, dsk_all, dy_all, st_all,
             dx_all, db_all, dc_all, ddt_ref, dalog_ref, ddsk_ref, ds_all):
        @pl.when(pl.program_id(0) == 0)
        def _():
            ds_all[...] = jnp.zeros_like(ds_all)
            dalog_ref[...] = jnp.zeros_like(dalog_ref)
            ddsk_ref[...] = jnp.zeros_like(ddsk_ref)

        dacs = jnp.zeros((CHUNK, LANES), F32)
        ddt_x = jnp.zeros((CHUNK, LANES), F32)
        for g in range(N_GROUPS):
            dacs_g, ddt_x_g = group(
                g, _group_cols(x_ref, g, GROUP_W), _group_cols(b_ref, g, D_STATE), _group_cols(c_ref, g, D_STATE),
                _group_cols(dte_ref, g, GROUP_W), _group_cols(acs_ref, g, GROUP_W), acst_ref,
                _group_cols(dsk_all, g, GROUP_W), _group_cols(dy_all, g, GROUP_W), st_all.at[g],
                _group_cols(dx_all, g, GROUP_W), _group_cols(db_all, g, D_STATE), _group_cols(dc_all, g, D_STATE),
                ddsk_ref, ds_all.at[g])
            dacs, ddt_x = dacs + dacs_g, ddt_x + ddt_x_g
        _, causal_t = _causal_masks()
        da = _dot3_data_rhs(causal_t, dacs)
        ddt_ref[...] = da * a128_ref[...] + ddt_x
        dalog_ref[...] += jnp.sum(da * dt_ref[...], axis=0, keepdims=True) * a128_ref[...]

    def group(g, x_ref, b_ref, c_ref, dte_ref, acs_ref, acst_ref, dsk_ref, dy_ref, st_ref,
              dx_ref, db_ref, dc_ref, ddsk_ref, ds_scr):
        causal, x, dt_exp, acs_exp_v, tot_exp, xdt, e_exp, f_exp, bm, cm = _ssd_common(
            x_ref, b_ref, c_ref, dte_ref, acs_ref)
        reduce_heads = _reduce_heads_matrix(g)
        state, dstate = st_ref[...], ds_scr[...]
        dyv = dy_ref[...]
        cb16, bb16 = cm.astype(BF16), bm.astype(BF16)
        s16, ds16 = state.astype(BF16), dstate.astype(BF16)
        cb = _dot_nt(cb16, bb16)
        cbt = _dot_nt(bb16, cb16)
        cs = _dot(cb16, s16)
        bds = _dot(bb16, ds16)
        edy = e_exp * dyv
        fx = f_exp * xdt
        dxdt_base = f_exp * bds
        dc_acc = _dot_nt(edy.astype(BF16), s16)
        db_acc = _dot_nt(fx.astype(BF16), ds16)
        ds_scr[...] = jnp.exp(tot_exp) * dstate + _dot_tn(cb16, edy.astype(BF16))
        q = fx * bds
        dacs = _dot2_data_lhs(edy * cs - q, reduce_heads)
        dtot = jnp.sum(_dot2_data_lhs(q + jnp.exp(tot_exp) * dstate * state, reduce_heads), axis=0, keepdims=True)
        ddsk_ref[...] += jnp.sum(_dot2_data_lhs(dyv * x, reduce_heads), axis=0, keepdims=True)
        lane = lax.broadcasted_iota(jnp.int32, (CHUNK, LANES), 1)
        dcb = jnp.zeros((CHUNK, CHUNK), F32)
        dcbt = jnp.zeros((CHUNK, CHUNK), F32)
        ddt_x = jnp.zeros((CHUNK, LANES), F32)
        for p in range(HEADS_PER_GROUP // 2):
            sl = slice(p * LANES, (p + 1) * LANES)
            xp, dyp = xdt[:, sl], dyv[:, sl]
            xp16, dyp16 = xp.astype(BF16), dyp.astype(BF16)
            dxh = []
            for e in range(2):
                h = 2 * p + e
                mine = (lane < HEAD_DIM) if e == 0 else (lane >= HEAD_DIM)
                acs_row = acst_ref[pl.ds(g * HEADS_PER_GROUP + h, 1), :]
                dm, dmt = _pair_decay(acs_exp_v[:, sl], acs_row, e, causal)
                m, mt = cb * dm, cbt * dmt
                xh16 = jnp.where(mine, xp, 0.0).astype(BF16)
                dyh16 = jnp.where(mine, dyp, 0.0).astype(BF16)
                d_m = _dot_nt(dyh16, xp16)
                d_mt = _dot_nt(xh16, dyp16)
                dacs_h = (jnp.sum(d_m * m, axis=-1, keepdims=True)
                          - jnp.sum(d_mt * mt, axis=-1, keepdims=True))
                dacs = dacs + jnp.where(lane == HEADS_PER_GROUP * g + h, dacs_h, 0.0)
                dcb = dcb + d_m * dm
                dcbt = dcbt + d_mt * dmt
                dxh.append(_dot(mt.astype(BF16), dyp16))
            dxdt = jnp.where(lane < HEAD_DIM, dxh[0], dxh[1]) + dxdt_base[:, sl]
            dx_ref[:, sl] = dxdt * dt_exp[:, sl] + dsk_ref[:, sl] * dyp
            ddt_x = ddt_x + _dot2_data_lhs(dxdt * x[:, sl], _reduce_pair_matrix(g, p))
        dc_ref[...] = dc_acc + _dot(dcb.astype(BF16), bb16)
        db_ref[...] = db_acc + _dot(dcbt.astype(BF16), cb16)
        row = lax.broadcasted_iota(jnp.int32, (CHUNK, LANES), 0)
        return dacs + jnp.where(row == CHUNK - 1, dtot, 0.0), ddt_x

    wide, b_spec, c_spec, rows_spec, state_spec = _ssd_specs(lambda c: last - c)
    heads_spec = pl.BlockSpec((CHUNK, LANES), lambda c: (last - c, 0))
    vec_spec = pl.BlockSpec((1, LANES), lambda c: (0, 0))
    bc_out = pl.BlockSpec((CHUNK, D_BC), lambda c: (last - c, 0))
    vec_shape = jax.ShapeDtypeStruct((1, LANES), F32)
    return _call(
        body, name=name,
        out_shape=(jax.ShapeDtypeStruct((rows, D_INNER), F32), jax.ShapeDtypeStruct((rows, D_BC), F32),
                   jax.ShapeDtypeStruct((rows, D_BC), F32), jax.ShapeDtypeStruct((rows, LANES), F32),
                   vec_shape, vec_shape),
        grid=(nc,),
        in_specs=[wide, b_spec, c_spec, wide, wide, rows_spec, heads_spec, vec_spec,
                  pl.BlockSpec((1, D_INNER), lambda c: (0, 0)), wide, state_spec],
        out_specs=(wide, bc_out, bc_out, heads_spec, vec_spec, vec_spec),
        scratch_shapes=[pltpu.VMEM((N_GROUPS, D_STATE, GROUP_W), F32)],
        operands=[xbc, xbc, xbc, dt_exp, acs_exp, acs_rows, dt, a128, dskexp, dy, states],
        semantics=("arbitrary",), steps=steps)


def _attn_visible(b, heads=1):
    row = jnp.bitwise_and(lax.broadcasted_iota(jnp.int32, (heads * CHUNK, 3 * CHUNK), 0), CHUNK - 1)
    col = lax.broadcasted_iota(jnp.int32, (heads * CHUNK, 3 * CHUNK), 1)
    bb = b + jnp.zeros_like(col)
    meta = (col < CHUNK) & (bb >= 1) & (col >= PAD_ROWS)
    prev = (col >= CHUNK) & (col < 2 * CHUNK) & (bb >= 2) & ((col - CHUNK) > row)
    cur = (col >= 2 * CHUNK) & ((col - 2 * CHUNK) <= row) & ((bb >= 1) | ((col - 2 * CHUNK) >= PAD_ROWS))
    return meta | prev | cur


def _attn_visible4(b):
    return _attn_visible(b, 4)


def _stack_heads(q_ref, sink_ref, kvh, scale):
    lane = lax.broadcasted_iota(jnp.int32, (CHUNK, LANES), 1)
    parts, sinks = [], []
    for pp in range(2):
        pair = kvh * 2 + pp
        qp = q_ref[:, pair * LANES:(pair + 1) * LANES] * scale
        for e in range(2):
            mine = (lane < HEAD_DIM) if e == 0 else (lane >= HEAD_DIM)
            parts.append(jnp.where(mine, qp, 0.0).astype(BF16))
            sinks.append(jnp.full((CHUNK, 1), sink_ref[2 * pair + e], F32))
    return jnp.concatenate(parts, axis=0), jnp.concatenate(sinks, axis=0)


def _attn_operands(q_ref, k0, kp, kc, v0, vp, vc, sink_ref):
    kcat, vcat, q4, sink4 = [], [], [], []
    for kvh in range(N_KV_HEADS):
        ksl = slice(kvh * LANES, (kvh + 1) * LANES)
        kcat.append(jnp.concatenate([k0[:, ksl], kp[:, ksl], kc[:, ksl]], axis=0).astype(BF16))
        vcat.append(jnp.concatenate([v0[:, ksl], vp[:, ksl], vc[:, ksl]], axis=0).astype(BF16))
        stacked, sinks = _stack_heads(q_ref, sink_ref, kvh, ATTN_SCALE)
        q4.append(stacked)
        sink4.append(sinks)
    return kcat, vcat, q4, sink4


def _attn_probs(q4, kcat, visible, sink4):
    heads = range(N_KV_HEADS)
    s = [jnp.where(visible, _dot_nt(q4[h], kcat[h]), NEG_INF) for h in heads]
    m = [jnp.maximum(jnp.max(s[h], axis=-1, keepdims=True), sink4[h]) for h in heads]
    pe = [jnp.exp(s[h] - m[h]) for h in heads]
    pe_sink = [jnp.exp(sink4[h] - m[h]) for h in heads]
    inv = [1.0 / (jnp.sum(pe[h], axis=-1, keepdims=True) + pe_sink[h]) for h in heads]
    return [pe[h] * inv[h] for h in heads], [pe_sink[h] * inv[h] for h in heads]


def _unstack_pairs(stacked, pp):
    lane = lax.broadcasted_iota(jnp.int32, (CHUNK, LANES), 1)
    return jnp.where(lane < HEAD_DIM, stacked[(2 * pp) * CHUNK:(2 * pp + 1) * CHUNK],
                     stacked[(2 * pp + 1) * CHUNK:(2 * pp + 2) * CHUNK])


def _attn_specs(colblock):
    blk = lambda f: pl.BlockSpec((CHUNK, 2 * D_KV), f)
    return [blk(lambda b: (0, colblock)), blk(lambda b: (jnp.maximum(b - 1, 0), colblock)), blk(lambda b: (b, colblock))]


def _attn_fwd(name, q, kv2, sinks, steps=()):
    rows = q.shape[0]

    def body(q_ref, k0, kp, kc, v0, vp, vc, sink_ref, o_ref):
        visible = _attn_visible4(pl.program_id(0))
        kcat, vcat, q4, sink4 = _attn_operands(q_ref, k0, kp, kc, v0, vp, vc, sink_ref)
        pn, _ = _attn_probs(q4, kcat, visible, sink4)
        o4 = [_dot(pn[h].astype(BF16), vcat[h]) for h in range(N_KV_HEADS)]
        for kvh in range(N_KV_HEADS):
            for pp in range(2):
                qsl = slice((kvh * 2 + pp) * LANES, (kvh * 2 + pp + 1) * LANES)
                o_ref[:, qsl] = _unstack_pairs(o4[kvh], pp).astype(BF16)

    return _call(
        body, name=name, out_shape=jax.ShapeDtypeStruct((rows, D_MODEL), BF16), grid=(rows // CHUNK,),
        in_specs=[pl.BlockSpec((CHUNK, D_MODEL), lambda b: (b, 0))] + _attn_specs(0) + _attn_specs(1)
        + [pl.BlockSpec(memory_space=pltpu.SMEM)],
        out_specs=pl.BlockSpec((CHUNK, D_MODEL), lambda b: (b, 0)),
        operands=[q, kv2, kv2, kv2, kv2, kv2, kv2, sinks], semantics=("parallel",), steps=steps)


def _attn_bwd(name, q, kv2, sinks, do, steps=()):
    rows = q.shape[0]

    def body(q_ref, k0, kp, kc, v0, vp, vc, sink_ref, do_ref,
             dq_ref, dkc_ref, dkp_ref, dvc_ref, dvp_ref, dkm_ref, dvm_ref, dsink_ref):
        @pl.when(pl.program_id(0) == 0)
        def _():
            dkm_ref[...] = jnp.zeros_like(dkm_ref)
            dvm_ref[...] = jnp.zeros_like(dvm_ref)
            dsink_ref[...] = jnp.zeros_like(dsink_ref)

        visible = _attn_visible4(pl.program_id(0))
        heads = range(N_KV_HEADS)
        lane1 = lax.broadcasted_iota(jnp.int32, (1, LANES), 1)
        kcat, vcat, q4, sink4 = _attn_operands(q_ref, k0, kp, kc, v0, vp, vc, sink_ref)
        do4 = [_stack_heads(do_ref, sink_ref, h, 1.0)[0] for h in heads]
        pn, psink = _attn_probs(q4, kcat, visible, sink4)
        dp = [_dot_nt(do4[h], vcat[h]) for h in heads]
        delta = [jnp.sum(pn[h] * dp[h], axis=-1, keepdims=True) for h in heads]
        ds16 = [(pn[h] * (dp[h] - delta[h])).astype(BF16) for h in heads]
        dq4 = [_dot(ds16[h], kcat[h]) for h in heads]
        dk_acc = [_dot_tn(ds16[h], q4[h]) for h in heads]
        dv_acc = [_dot_tn(pn[h].astype(BF16), do4[h]) for h in heads]
        dsink = jnp.zeros((1, LANES), F32)
        for kvh in heads:
            ksl = slice(kvh * LANES, (kvh + 1) * LANES)
            sink_terms = psink[kvh] * delta[kvh]
            for j in range(4):
                part = jnp.sum(sink_terms[j * CHUNK:(j + 1) * CHUNK], axis=0, keepdims=True)
                dsink = dsink - jnp.where(lane1 == kvh * 4 + j, part, 0.0)
            for pp in range(2):
                qsl = slice((kvh * 2 + pp) * LANES, (kvh * 2 + pp + 1) * LANES)
                dq_ref[:, qsl] = (_unstack_pairs(dq4[kvh], pp) * ATTN_SCALE).astype(BF16)
            dkm_ref[:, ksl] += dk_acc[kvh][0:CHUNK]
            dvm_ref[:, ksl] += dv_acc[kvh][0:CHUNK]
            dkp_ref[:, ksl] = dk_acc[kvh][CHUNK:2 * CHUNK]
            dvp_ref[:, ksl] = dv_acc[kvh][CHUNK:2 * CHUNK]
            dkc_ref[:, ksl] = dk_acc[kvh][2 * CHUNK:3 * CHUNK]
            dvc_ref[:, ksl] = dv_acc[kvh][2 * CHUNK:3 * CHUNK]
        dsink_ref[...] += dsink

    qspec = pl.BlockSpec((CHUNK, D_MODEL), lambda b: (b, 0))
    kvspec = pl.BlockSpec((CHUNK, 2 * D_KV), lambda b: (b, 0))
    fixed = pl.BlockSpec((CHUNK, 2 * D_KV), lambda b: (0, 0))
    kv_shape = jax.ShapeDtypeStruct((rows, 2 * D_KV), F32)
    meta_shape = jax.ShapeDtypeStruct((CHUNK, 2 * D_KV), F32)
    return _call(
        body, name=name,
        out_shape=(jax.ShapeDtypeStruct((rows, D_MODEL), BF16), kv_shape, kv_shape, kv_shape, kv_shape,
                   meta_shape, meta_shape, jax.ShapeDtypeStruct((1, LANES), F32)),
        grid=(rows // CHUNK,),
        in_specs=[qspec] + _attn_specs(0) + _attn_specs(1) + [pl.BlockSpec(memory_space=pltpu.SMEM), qspec],
        out_specs=(qspec, kvspec, kvspec, kvspec, kvspec, fixed, fixed, pl.BlockSpec((1, LANES), lambda b: (0, 0))),
        operands=[q, kv2, kv2, kv2, kv2, kv2, kv2, sinks, do], semantics=("arbitrary",), steps=steps)


def _kv_grad_combine(name, dk_cur, dk_prev, dk_meta, dv_cur, dv_prev, dv_meta):
    rows = dk_cur.shape[0]
    nb = rows // CHUNK
    width = 2 * D_KV

    def body(kc_ref, kp_ref, km_ref, vc_ref, vp_ref, vm_ref, o_ref):
        jj = pl.program_id(0) + jnp.zeros((CHUNK, 1), jnp.int32)
        for half, (c_ref, p_ref, m_ref) in enumerate(((kc_ref, kp_ref, km_ref), (vc_ref, vp_ref, vm_ref))):
            total = c_ref[...] + jnp.where(jj < nb - 1, p_ref[...], 0.0) + jnp.where(jj == 0, m_ref[...], 0.0)
            o_ref[:, half * width:(half + 1) * width] = total.astype(BF16)

    blk = lambda f: pl.BlockSpec((CHUNK, width), f)
    three = lambda: [blk(lambda j: (j, 0)), blk(lambda j: (jnp.minimum(j + 1, nb - 1), 0)), blk(lambda j: (0, 0))]
    return pl.pallas_call(
        body, name=name, out_shape=jax.ShapeDtypeStruct((rows, 2 * width), BF16), grid=(nb,),
        in_specs=three() + three(), out_specs=pl.BlockSpec((CHUNK, 2 * width), lambda j: (j, 0)),
        compiler_params=_cparams(("parallel",)),
    )(dk_cur, dk_prev, dk_meta, dv_cur, dv_prev, dv_meta)


def _adamw(name, w, g, m, v, steps=()):
    rows, width = w.shape
    tr = rows
    for cand in range(8, rows + 1, 8):
        if rows % cand == 0 and cand * width * 4 <= (1 << 20):
            tr = cand

    def body(*refs):
        _adamw_update(*refs)

    blk = pl.BlockSpec((tr, width), lambda i: (i, 0))
    shp = jax.ShapeDtypeStruct((rows, width), F32)
    return _call(body, name=name, out_shape=(shp, shp, shp), grid=(rows // tr,), in_specs=[blk] * 4,
                 out_specs=(blk,) * 3, operands=[w, g, m, v], semantics=("parallel",), steps=steps)


def _adamw_update(w_ref, g_ref, m_ref, v_ref, d_ref, mo_ref, vo_ref):
    gv = g_ref[...]
    mn = ADAM_B1 * m_ref[...] + (1.0 - ADAM_B1) * gv
    vn = ADAM_B2 * v_ref[...] + (1.0 - ADAM_B2) * (gv * gv)
    m_hat = mn / (1.0 - ADAM_B1 ** ADAM_STEP)
    v_hat = vn / (1.0 - ADAM_B2 ** ADAM_STEP)
    d_ref[...] = -ADAM_LR * (m_hat / (jnp.sqrt(v_hat) + ADAM_EPS) + ADAM_WD * w_ref[...])
    mo_ref[...] = mn
    vo_ref[...] = vn


def _adamw_small(name, ws, gs, ms, vs):
    n = len(ws)

    def body(*refs):
        for i in range(n):
            _adamw_update(*refs[i::n])

    shapes = [jax.ShapeDtypeStruct(a.shape, F32) for a in ws]
    outs = pl.pallas_call(body, name=name, out_shape=shapes * 3, in_specs=[VMEM_SPEC] * (4 * n),
                          out_specs=[VMEM_SPEC] * (3 * n), compiler_params=_cparams())(*ws, *gs, *ms, *vs)
    return outs[:n], outs[n:2 * n], outs[2 * n:]


def _ffn_fwd(tag, h, hn, p, i, plan):
    up_g, up_v, act = _ffn_up_conv(f"ffn{tag}_up", hn, plan.weight("f_w_up", i), p["f_conv_w"][i],
                                   p["f_conv_b"][i:i + 1], steps=plan.steps(f"ffn{tag}_up"))
    pre = _mm(f"ffn{tag}_down", act, plan.weight("f_w_down", i), "nn", steps=plan.steps(f"ffn{tag}_down"))
    return pre, (h, hn, up_g, up_v, act, pre)


def _ffn_bwd(tag, dpre, saved, p, i, plan):
    h, hn, up_g, up_v, act, pre = saved
    plan.grad("f_w_down", i, _mm(f"ffn{tag}_down_dw", act, dpre, "tn", out_dtype=BF16))
    dact = _mm(f"ffn{tag}_down_dx", dpre, plan.weight("f_w_down", i), "nt", steps=plan.steps(f"ffn{tag}_down_dx"))
    gwg, gwv, gbg, gbv, dhn, g_up = _ffn_conv_bwd(
        f"ffn{tag}_conv_bwd", up_g, up_v, dact, p["f_conv_w"][i], p["f_conv_b"][i:i + 1], hn,
        plan.weight("f_w_up", i), steps=plan.steps(f"ffn{tag}_conv_bwd"))
    g_cw, g_cb = jnp.concatenate([gwg, gwv], axis=1), jnp.concatenate([gbg, gbv], axis=1)
    plan.grad("f_w_up", i, g_up)
    return dhn, dict(f_conv_w=g_cw, f_conv_b=g_cb)


def _lanes_pad(a, width=LANES):
    return jnp.pad(a, [(0, 0)] * (a.ndim - 1) + [(0, width - a.shape[-1])])


def _dup_heads(w):
    rows = w.shape[0]
    w = w.reshape(rows, 2 * N_KV_HEADS, 1, HEAD_DIM)
    return jnp.broadcast_to(w, (rows, 2 * N_KV_HEADS, 2, HEAD_DIM)).reshape(rows, 4 * D_KV)


def _undup_heads(g):
    rows = g.shape[0]
    return g.reshape(rows, 2 * N_KV_HEADS, 2, HEAD_DIM).sum(axis=2).reshape(rows, 2 * D_KV)


def _local_step(x2, target, p, plan):
    seq = x2.shape[0]
    rows = seq + CHUNK
    g = {}

    h0 = jnp.concatenate([jnp.zeros((PAD_ROWS, D_MODEL), F32), p["meta_tokens"], x2], axis=0)

    w_in = plan.weight("a_w_in")
    w_dt = jnp.pad(w_in[D_MAIN:], ((0, LANES - SSM_HEADS), (0, 0)))
    dt_bias = _lanes_pad(p["a_dt_bias"])
    a128 = _lanes_pad(-jnp.exp(p["a_a_log"]))
    dskexp = jnp.repeat(p["a_d_skip"].reshape(SSM_HEADS), HEAD_DIM).reshape(1, D_INNER)

    hn0 = _rms_fwd("a_norm", h0, p["a_norm_pre"])
    zx = _mm("a_in_main", hn0, w_in, "nt", k_rows=D_MAIN, steps=plan.steps("a_in_main"))
    dtr = _mm("a_in_dt", hn0, w_dt, "nt")
    xbc = _conv4_fwd("a_conv", zx, p["a_conv_w"], p["a_conv_b"], steps=plan.steps("a_conv"))
    dt = _dt_fwd("a_dt", dtr, dt_bias)
    dt_exp, acs_exp, acs_rows = _ssd_prep("a_ssd_prep", dt, a128, steps=plan.steps("a_ssd_prep"))
    y, states = _ssd_fwd("a_ssd", xbc, dt_exp, acs_exp, acs_rows, dskexp, steps=plan.steps("a_ssd"))
    yn = _gate_fwd("a_gate", y, zx, p["a_gate_norm"], steps=plan.steps("a_gate"))
    mix = _mm("a_out", yn, plan.weight("a_w_out"), "nn", steps=plan.steps("a_out"))
    h1, (hn_f0,) = _resid_norm_fwd("a_resid", h0, mix, p["a_norm_post"], [p["f_norm_pre"][0:1]])

    pre_f0, ffn0 = _ffn_fwd("0", h1, hn_f0, p, 0, plan)
    h2, (hkv, hn2) = _resid_norm_fwd("ffn0_resid", h1, pre_f0, p["f_norm_post"][0:1], [p["kv_norm"], p["b_norm_pre"]])

    w_kv2 = _dup_heads(plan.weight("w_kv"))
    kv2 = _mm("kv_proj", hkv, w_kv2, "nn")
    q = _mm("b_q", hn2, plan.weight("b_w_q"), "nn")
    sinks = p["b_sinks"].reshape(N_Q_HEADS)
    o = _attn_fwd("b_attn", q, kv2, sinks, steps=plan.steps("b_attn"))
    attn = _mm("b_o", o, plan.weight("b_w_o"), "nn", steps=plan.steps("b_o"))
    h3, (hn_f1,) = _resid_norm_fwd("b_resid", h2, attn, p["b_norm_post"], [p["f_norm_pre"][1:2]])

    pre_f1, ffn1 = _ffn_fwd("1", h3, hn_f1, p, 1, plan)
    dh, loss_vec, dpre_f1, g_post1 = _resid_norm_loss("ffn1_resid_loss", h3, pre_f1, p["f_norm_post"][1:2], target)
    loss = loss_vec[0, 0]

    dhn_f1, g1 = _ffn_bwd("1", dpre_f1, ffn1, p, 1, plan)
    dh, g_pre1, dpre, g["b_norm_post"] = _norm_bwd_add("ffn1_norm_bwd", dh, dhn_f1, h3, p["f_norm_pre"][1:2],
                                                        then=(attn, p["b_norm_post"]))
    plan.grad("b_w_o", None, _mm("b_o_dw", o, dpre, "tn", out_dtype=BF16))
    do = _mm("b_o_dx", dpre, plan.weight("b_w_o"), "nt", steps=plan.steps("b_o_dx"))
    dq, dkc, dkp, dvc, dvp, dkm, dvm, dsink = _attn_bwd("b_attn_bwd", q, kv2, sinks, do, steps=plan.steps("b_attn_bwd"))
    g["b_sinks"] = dsink[:, :N_Q_HEADS]
    dhn2 = _mm("b_q_dx", dq, plan.weight("b_w_q"), "nt")
    plan.grad("b_w_q", None, _mm("b_q_dw", hn2, dq, "tn", out_dtype=BF16))
    dh, g["b_norm_pre"] = _norm_bwd_add("b_norm_bwd", dh, dhn2, h2, p["b_norm_pre"])
    dkv2 = _kv_grad_combine("kv_grad", dkc, dkp, dkm, dvc, dvp, dvm)
    dhkv = _mm("kv_proj_dx", dkv2, w_kv2, "nt")
    plan.grad("w_kv", None, _undup_heads(_mm("kv_proj_dw", hkv, dkv2, "tn")))
    dh, g["kv_norm"], dpre_f0, g_post0 = _norm_bwd_add("kv_norm_bwd", dh, dhkv, h2, p["kv_norm"],
                                                       then=(pre_f0, p["f_norm_post"][0:1]))

    dhn_f0, g0 = _ffn_bwd("0", dpre_f0, ffn0, p, 0, plan)
    dh, g_pre0, dpre, g["a_norm_post"] = _norm_bwd_add("ffn0_norm_bwd", dh, dhn_f0, h1, p["f_norm_pre"][0:1],
                                                        then=(mix, p["a_norm_post"]))
    g["f_norm_post"] = jnp.concatenate([g_post0, g_post1], axis=0)
    g["f_norm_pre"] = jnp.concatenate([g_pre0, g_pre1], axis=0)
    g["f_conv_w"] = jnp.stack([g0["f_conv_w"], g1["f_conv_w"]])
    g["f_conv_b"] = jnp.concatenate([g0["f_conv_b"], g1["f_conv_b"]], axis=0)
    plan.grad("a_w_out", None, _mm("a_out_dw", yn, dpre, "tn", out_dtype=BF16))
    dyn = _mm("a_out_dx", dpre, plan.weight("a_w_out"), "nt", steps=plan.steps("a_out_dx"))
    dy, dz, g["a_gate_norm"] = _gate_bwd("a_gate_bwd", dyn, y, zx, p["a_gate_norm"])
    dxs, dbm, dcm, ddt, dalog, ddsk = _ssd_bwd("a_ssd_bwd", xbc, dt_exp, acs_exp, acs_rows, dt, a128, dskexp, dy, states,
                                              steps=plan.steps("a_ssd_bwd"))
    g["a_a_log"] = dalog[:, :SSM_HEADS]
    g["a_d_skip"] = ddsk[:, :SSM_HEADS]
    ddtr, dbias = _dt_bwd("a_dt_bwd", ddt, dtr, dt_bias)
    g["a_dt_bias"] = dbias[:, :SSM_HEADS]
    dxp, gw_x, gb_x = _conv4_bwd("a_conv_bwd_x", zx, dxs, p["a_conv_w"], p["a_conv_b"], 0)
    dbp, gw_b, gb_b = _conv4_bwd("a_conv_bwd_b", zx, dbm, p["a_conv_w"], p["a_conv_b"], D_INNER)
    dcp, gw_c, gb_c = _conv4_bwd("a_conv_bwd_c", zx, dcm, p["a_conv_w"], p["a_conv_b"], D_INNER + D_BC)
    g["a_conv_w"] = jnp.concatenate([gw_x, gw_b, gw_c], axis=1)
    g["a_conv_b"] = jnp.concatenate([gb_x, gb_b, gb_c], axis=1)
    dzx = jnp.concatenate([dz, dxp, dbp, dcp], axis=1)
    g_in = _mm("a_in_main_dw", dzx, hn0, "tn", out_dtype=BF16, out_rows=D_IN_PROJ, steps=plan.steps("a_in_main_dw"))
    plan.grad("a_w_in", None, _tn_rows_into("a_in_dt_dw", ddtr, hn0, g_in, D_MAIN, SSM_HEADS))
    dhn0 = _mm("a_in_dt_dx", ddtr, w_dt, "nn", steps=plan.steps("a_in_dt_dx"))
    dhn0 = _mm("a_in_main_dx", dzx, w_in, "nn", acc=dhn0, steps=plan.steps("a_in_main_dx"))
    dh, g["a_norm_pre"] = _norm_bwd_add("a_norm_bwd", dh, dhn0, h0, p["a_norm_pre"], steps=plan.steps("a_norm_bwd"))

    g["meta_tokens"] = dh[PAD_ROWS:CHUNK]
    return loss, dh[CHUNK:], g


ANY = pl.BlockSpec(memory_space=pl.ANY)
VMEM_SPEC = pl.BlockSpec(memory_space=pltpu.VMEM)


def _allgather_small(name, shard):
    rows = shard.shape[0]

    def body(s_ref, o_ref, send_sems, recv_sems):
        x, y, c = _place()
        me = 2 * x + y
        o_ref[me] = s_ref[...]
        chips = _other_chips(x, y)
        sends = [pltpu.make_async_remote_copy(s_ref, o_ref.at[me], send_sems.at[j], recv_sems.at[j],
                                              device_id=(cx, cy, c), device_id_type=MESH)
                 for j, (cx, cy) in enumerate(chips)]
        for cp in sends:
            cp.start()
        for j, (cx, cy) in enumerate(chips):
            pltpu.make_async_remote_copy(s_ref, o_ref.at[2 * cx + cy], send_sems.at[j], recv_sems.at[j],
                                         device_id=(cx, cy, c), device_id_type=MESH).wait_recv()
        for cp in sends:
            cp.wait_send()

    return pl.pallas_call(
        body, name=name, out_shape=jax.ShapeDtypeStruct((N_CHIPS, rows, LANES), F32),
        in_specs=[VMEM_SPEC], out_specs=VMEM_SPEC,
        scratch_shapes=[pltpu.SemaphoreType.DMA((3,)), pltpu.SemaphoreType.DMA((3,))],
        compiler_params=pltpu.CompilerParams(vmem_limit_bytes=VMEM_LIMIT),
    )(shard)


def _row_block(rows, width, itemsize, align, budget=2 << 20):
    best = rows
    for cand in range(align, rows + 1, align):
        if rows % cand == 0 and cand * width * itemsize <= budget:
            best = cand
    return best


def _cast_into_slot(name, chip, w, layer=None):
    rows, width = w.shape[-2:]
    tr = _row_block(rows, width, 4, 16)
    if layer is None:
        in_spec = pl.BlockSpec((tr, width), lambda i, chip_ref: (i, 0))
    else:
        in_spec = pl.BlockSpec((None, tr, width), lambda i, chip_ref: (layer, i, 0))

    def body(chip_ref, w_ref, o_ref):
        o_ref[...] = w_ref[...].astype(BF16)

    return pl.pallas_call(
        body, name=name, out_shape=jax.ShapeDtypeStruct((N_CHIPS, rows, width), BF16),
        grid_spec=pltpu.PrefetchScalarGridSpec(
            num_scalar_prefetch=1, grid=(rows // tr,), in_specs=[in_spec],
            out_specs=pl.BlockSpec((None, tr, width), lambda i, chip_ref: (chip_ref[0], i, 0))),
        compiler_params=_cparams(("parallel",)),
    )(chip, w)


def _allreduce_small(name, vec):
    rows = -(-vec.shape[0] // (2 * SUBLANES)) * (2 * SUBLANES)
    hr = rows // 2
    padded = jnp.pad(vec, ((0, rows - vec.shape[0]), (0, 0)))

    def body(v_ref, o_ref, theirs, pair, by_chip, send_sems, recv_sems):
        x, y, c = _place()
        me = 2 * x + y
        sibling = (x, y, 1 - c)
        mine = pl.ds(pl.multiple_of(c * hr, SUBLANES), hr)
        other = pl.ds(pl.multiple_of((1 - c) * hr, SUBLANES), hr)

        swap = _remote(v_ref, theirs, send_sems, recv_sems, 0, sibling)
        swap.start()
        swap.wait()
        south = (c + jnp.zeros((1, 1), jnp.int32)) == 0
        pair[...] = jnp.where(south, v_ref[...], theirs[...]) + jnp.where(south, theirs[...], v_ref[...])

        by_chip[me] = pair[mine, :]
        sends = [_remote(by_chip.at[me], by_chip.at[me], send_sems, recv_sems, 1 + j, (cx, cy, c))
                 for j, (cx, cy) in enumerate(_other_chips(x, y))]
        for cp in sends:
            cp.start()
        for j, (cx, cy) in enumerate(_other_chips(x, y)):
            _remote(by_chip.at[me], by_chip.at[2 * cx + cy], send_sems, recv_sems, 1 + j, (cx, cy, c)).wait_recv()
        for cp in sends:
            cp.wait_send()
        total = by_chip[0]
        for s in range(1, N_CHIPS):
            total = total + by_chip[s]

        o_ref[mine, :] = total
        back = _remote(o_ref.at[mine], o_ref.at[mine], send_sems, recv_sems, 4, sibling)
        back.start()
        _remote(o_ref.at[other], o_ref.at[other], send_sems, recv_sems, 4, sibling).wait_recv()
        back.wait_send()

    out = pl.pallas_call(
        body, name=name, out_shape=jax.ShapeDtypeStruct((rows, LANES), F32),
        in_specs=[VMEM_SPEC], out_specs=VMEM_SPEC,
        scratch_shapes=[pltpu.VMEM((rows, LANES), F32), pltpu.VMEM((rows, LANES), F32),
                        pltpu.VMEM((N_CHIPS, hr, LANES), F32), pltpu.SemaphoreType.DMA((5,)),
                        pltpu.SemaphoreType.DMA((5,))],
        compiler_params=pltpu.CompilerParams(vmem_limit_bytes=VMEM_LIMIT),
    )(padded)
    return out[:vec.shape[0]]


def _rs_pair_add(name, place, grads, partner, split="rows"):
    _, half_rows, width = partner.shape
    tr = _row_block(half_rows, width, 2, 16)
    nb = half_rows // tr
    if split == "rows":
        mine = pl.BlockSpec((None, tr, width), lambda s, i, pr: (s, pr[1] * nb + i, 0))
    else:
        mine = pl.BlockSpec((None, tr, width), lambda s, i, pr: (s, i, pr[1]))

    def body(place_ref, g_ref, p_ref, o_ref):
        o_ref[...] = (g_ref[...].astype(F32) + p_ref[...].astype(F32)).astype(BF16)

    return pl.pallas_call(
        body, name=name, out_shape=jax.ShapeDtypeStruct(partner.shape, BF16),
        grid_spec=pltpu.PrefetchScalarGridSpec(
            num_scalar_prefetch=1, grid=(N_CHIPS, nb),
            in_specs=[mine, pl.BlockSpec((None, tr, width), lambda s, i, pr: (s, i, 0))],
            out_specs=pl.BlockSpec((None, tr, width), lambda s, i, pr: (s, i, 0))),
        compiler_params=_cparams(("parallel", "parallel")),
    )(place, grads, partner)


def _rs_chip_add(name, place, mine, others, split="rows"):
    _, half_rows, width = mine.shape
    tr = _row_block(half_rows, width, 4, 16, budget=1 << 20)
    nb = half_rows // tr
    if split == "rows":
        out_shape, out_spec = (2 * half_rows, width), pl.BlockSpec((tr, width), lambda i, pr: (pr[1] * nb + i, 0))
    else:
        out_shape, out_spec = (half_rows, 2 * width), pl.BlockSpec((tr, width), lambda i, pr: (i, pr[1]))

    def body(place_ref, q_ref, r_ref, o_ref):
        acc = q_ref[...].astype(F32)
        for j in range(3):
            acc = acc + r_ref[j].astype(F32)
        o_ref[...] = acc

    return pl.pallas_call(
        body, name=name, out_shape=jax.ShapeDtypeStruct(out_shape, F32),
        grid_spec=pltpu.PrefetchScalarGridSpec(
            num_scalar_prefetch=1, grid=(nb,),
            in_specs=[pl.BlockSpec((None, tr, width), lambda i, pr: (pr[0], i, 0)),
                      pl.BlockSpec((3, tr, width), lambda i, pr: (0, i, 0))],
            out_specs=out_spec),
        compiler_params=_cparams(("parallel",)),
    )(place, mine, others)


WEIGHTS = ["meta_tokens", "a_norm_pre", "a_w_in", "a_conv_w", "a_conv_b", "a_dt_bias", "a_a_log", "a_d_skip",
           "a_gate_norm", "a_w_out", "a_norm_post", "kv_norm", "w_kv", "b_norm_pre", "b_w_q", "b_sinks", "b_w_o",
           "b_norm_post", "f_norm_pre", "f_w_up", "f_conv_w", "f_conv_b", "f_w_down", "f_norm_post"]
FULL_SHAPE = {
    "meta_tokens": (16, 1024), "a_norm_pre": (1, 1024), "a_w_in": (1, 1024, 5152), "a_conv_w": (1, 4, 3072),
    "a_conv_b": (1, 3072), "a_dt_bias": (1, 32), "a_a_log": (1, 32), "a_d_skip": (1, 32), "a_gate_norm": (1, 2048),
    "a_w_out": (1, 2048, 1024), "a_norm_post": (1, 1024), "kv_norm": (1024,), "w_kv": (1024, 512),
    "b_norm_pre": (1, 1024), "b_w_q": (1, 1024, 1024), "b_sinks": (1, 16), "b_w_o": (1, 1024, 1024),
    "b_norm_post": (1, 1024), "f_norm_pre": (2, 1024), "f_w_up": (2, 1024, 5632), "f_conv_w": (2, 3, 5632),
    "f_conv_b": (2, 5632), "f_w_down": (2, 2816, 1024), "f_norm_post": (2, 1024),
}
SHARD_AXIS = {
    "meta_tokens": 1, "a_norm_pre": 1, "a_w_in": 2, "a_conv_w": 2, "a_conv_b": 1, "a_dt_bias": None, "a_a_log": None,
    "a_d_skip": None, "a_gate_norm": 1, "a_w_out": 1, "a_norm_post": 1, "kv_norm": None, "w_kv": 0, "b_norm_pre": None,
    "b_w_q": 1, "b_sinks": None, "b_w_o": 1, "b_norm_post": None, "f_norm_pre": None, "f_w_up": 2, "f_conv_w": 2,
    "f_conv_b": None, "f_w_down": 1, "f_norm_post": None,
}
BIG = ["a_w_in", "a_w_out", "w_kv", "b_w_q", "b_w_o", "f_w_up", "f_w_down"]
SMALL = [n for n in WEIGHTS if n not in BIG]
SMALL_SHARDED = [n for n in SMALL if SHARD_AXIS[n] is not None]


def _shard_shape(name):
    shape = list(FULL_SHAPE[name])
    if SHARD_AXIS[name] is not None:
        shape[SHARD_AXIS[name]] //= N_CHIPS
    return tuple(shape)


def _numel(shape):
    return int(math.prod(shape))


SUBLANES = 8


def _packed_rows(shape):
    rows = -(-_numel(shape) // LANES)
    return -(-rows // SUBLANES) * SUBLANES


def _pack(arrays):
    parts = []
    for a in arrays:
        size, rows = _numel(a.shape), _packed_rows(a.shape)
        if size % LANES == 0:
            part = jnp.pad(a.reshape(size // LANES, LANES), ((0, rows - size // LANES), (0, 0)))
        else:
            part = jnp.pad(a.reshape(-1), (0, rows * LANES - size)).reshape(rows, LANES)
        parts.append(part)
    return jnp.concatenate(parts, axis=0)


def _unpack(packed, names, shape_of):
    out, off = {}, 0
    lead = packed.shape[:-2]
    for n in names:
        shape = tuple(shape_of(n))
        size, rows = _numel(shape), _packed_rows(shape)
        part = packed[..., off:off + rows, :]
        if size % LANES == 0:
            out[n] = part[..., :size // LANES, :].reshape(lead + shape)
        else:
            out[n] = part.reshape(lead + (rows * LANES,))[..., :size].reshape(lead + shape)
        off += rows
    return out


def _split_chips(name, full):
    ax = SHARD_AXIS[name]
    shape = full.shape
    cut = shape[:ax] + (N_CHIPS, shape[ax] // N_CHIPS) + shape[ax + 1:]
    return jnp.moveaxis(full.reshape(cut), ax, 0)


def _join_chips(name, stacked):
    ax = SHARD_AXIS[name]
    moved = jnp.moveaxis(stacked, 0, ax)
    shape = moved.shape
    return moved.reshape(shape[:ax] + (shape[ax] * shape[ax + 1],) + shape[ax + 2:])


def _as2d(a):
    return a.reshape(-1, a.shape[-1])


BUFFERS = [("a_w_in", "a_w_in", None), ("a_w_out", "a_w_out", None), ("w_kv", "w_kv", None),
           ("b_w_q", "b_w_q", None), ("b_w_o", "b_w_o", None), ("f_w_up0", "f_w_up", 0), ("f_w_up1", "f_w_up", 1),
           ("f_w_down0", "f_w_down", 0), ("f_w_down1", "f_w_down", 1)]


TRANSPOSED = ("a_w_in",)
SPLIT = {"a_w_in": "cols"}


def _local_shard(arrays, weight, layer):
    if weight in TRANSPOSED:
        return arrays[weight][0].T
    return _as2d(arrays[weight]) if layer is None else arrays[weight]


def _weight_from_gathered(weight, buf):
    if weight == "f_w_up":
        return buf
    return buf.reshape(N_CHIPS * buf.shape[1], buf.shape[2])


def _gathered_from_grad(weight, g):
    if weight == "f_w_up":
        return g
    return g.reshape(N_CHIPS, g.shape[0] // N_CHIPS, g.shape[1]).astype(BF16)


GATHER_SCHEDULE = {
    "a_in_main": [("ici", ["a_w_out"])],
    "a_conv": [("d2d", ["a_w_out"]), ("ici", ["f_w_down0"])],
    "a_ssd_prep": [("d2d", ["f_w_down0"]), ("ici_near", ["f_w_up0"])],
    "a_ssd": [("ici_far", ["f_w_up0"])],
    "a_gate": [("d2d", ["f_w_up0"]), ("ici", ["w_kv", "b_w_q", "b_w_o"])],
    "ffn0_up": [("d2d", ["w_kv", "b_w_q", "b_w_o"]), ("ici", ["f_w_down1"])],
    "ffn0_down": [("d2d", ["f_w_down1"])],
    "b_attn": [("ici", ["f_w_up1"])],
    "b_o": [("d2d", ["f_w_up1"])],
}
REDUCE_SCHEDULE = {
    "b_attn_bwd": [("all", ["f_w_down1", "f_w_up1", "b_w_o"])],
    "ffn0_conv_bwd": [("all", ["b_w_q", "w_kv", "f_w_down0"])],
    "a_ssd_bwd": [("all", ["f_w_up0", "a_w_out"])],
    "a_in_main_dx": [("near", ["a_w_in"])],
    "a_norm_bwd": [("far", ["a_w_in"])],
}
REDUCE_LAST = ("a_w_in",)
ICI_PEERS = {"ici": ALL_PEERS, "ici_near": NEAR_PEERS, "ici_far": FAR_PEERS,
             "all": ALL_PEERS, "near": NEAR_PEERS, "far": FAR_PEERS}
PAIR_SCHEDULE = {
    "b_o_dx": ["f_w_down1", "f_w_up1", "b_w_o"],
    "ffn0_down_dx": ["b_w_q", "w_kv", "f_w_down0"],
    "a_out_dx": ["f_w_up0", "a_w_out"],
    "a_in_dt_dx": ["a_w_in"],
}
SWAP_SCHEDULE = {"a_in_main_dw": ["f_w_down1", "f_w_up1", "b_w_o", "b_w_q", "w_kv", "f_w_down0", "f_w_up0", "a_w_out"]}


def _buffer_of(weight, layer):
    return weight if layer is None else f"{weight}{layer}"


class _Pipeline:
    def __init__(self, place, slots):
        self.place = place
        self.slots = dict(slots)
        self.running = []
        self.grads = {}
        self.theirs = {}
        self.partials = {}
        self.peers = {}
        self.reduced = {}

    def _collect(self):
        for step, buffers, table in self.running:
            table.update(zip(buffers, step.results))
        self.running = []

    @staticmethod
    def _splits(buffers):
        return [SPLIT.get(b, "rows") for b in buffers]

    def gather_now(self, name, buffers):
        step = _step_gather_full([self.slots[b] for b in buffers], self._splits(buffers))
        _run_steps(name, [step])
        self.slots.update(zip(buffers, step.results))

    def weight(self, name, layer=None):
        self._collect()
        return _weight_from_gathered(name, self.slots[_buffer_of(name, layer)])

    def grad(self, name, layer, g):
        self.grads[_buffer_of(name, layer)] = _gathered_from_grad(name, g)

    def steps(self, kernel):
        self._collect()
        steps = []
        for phase, buffers in GATHER_SCHEDULE.get(kernel, []):
            bufs, splits = [self.slots[b] for b in buffers], self._splits(buffers)
            step = (_step_gather_d2d(bufs, splits) if phase == "d2d"
                    else _step_gather_ici(bufs, splits, ICI_PEERS[phase]))
            self.running.append((step, buffers, self.slots))
            steps.append(step)
        buffers = PAIR_SCHEDULE.get(kernel)
        if buffers:
            step = _step_pair_exchange([self.grads[b] for b in buffers], self._splits(buffers))
            self.running.append((step, buffers, self.theirs))
            steps.append(step)
        for part, buffers in REDUCE_SCHEDULE.get(kernel, []):
            for b in buffers:
                if b not in self.partials:
                    self.partials[b] = _rs_pair_add("reduce_pair_add_" + b, self.place, self.grads[b], self.theirs[b],
                                                    SPLIT.get(b, "rows"))
            started = [self.peers[b] for b in buffers] if all(b in self.peers for b in buffers) else None
            step = _step_chip_exchange([self.partials[b] for b in buffers], ICI_PEERS[part], into=started)
            self.running.append((step, buffers, self.peers))
            steps.append(step)
        buffers = SWAP_SCHEDULE.get(kernel)
        if buffers:
            step = self._swap_step(buffers)
            self.running.append((step, buffers, self.reduced))
            steps.append(step)
        return steps

    def _swap_step(self, buffers):
        halves = [_rs_chip_add("reduce_chip_add_" + b, self.place, self.partials[b], self.peers[b], SPLIT.get(b, "rows"))
                  for b in buffers]
        return _step_pair_gather(halves, self._splits(buffers))

    def shard(self, buffer):
        self._collect()
        return self.reduced[buffer]

    def finish(self):
        self._collect()
        rest = [b for b, _, _ in BUFFERS if b not in self.reduced]
        step = self._swap_step(rest)
        _run_steps("reduce_pair_gather", [step])
        self.reduced.update(zip(rest, step.results))


def kernel(x, meta_tokens, a_norm_pre, a_w_in, a_conv_w, a_conv_b, a_dt_bias, a_a_log, a_d_skip, a_gate_norm, a_w_out, a_norm_post, kv_norm, w_kv, b_norm_pre, b_w_q, b_sinks, b_w_o, b_norm_post, f_norm_pre, f_w_up, f_conv_w, f_conv_b, f_w_down, f_norm_post, loss_target, m_meta_tokens, m_a_norm_pre, m_a_w_in, m_a_conv_w, m_a_conv_b, m_a_dt_bias, m_a_a_log, m_a_d_skip, m_a_gate_norm, m_a_w_out, m_a_norm_post, m_kv_norm, m_w_kv, m_b_norm_pre, m_b_w_q, m_b_sinks, m_b_w_o, m_b_norm_post, m_f_norm_pre, m_f_w_up, m_f_conv_w, m_f_conv_b, m_f_w_down, m_f_norm_post, v_meta_tokens, v_a_norm_pre, v_a_w_in, v_a_conv_w, v_a_conv_b, v_a_dt_bias, v_a_a_log, v_a_d_skip, v_a_gate_norm, v_a_w_out, v_a_norm_post, v_kv_norm, v_w_kv, v_b_norm_pre, v_b_w_q, v_b_sinks, v_b_w_o, v_b_norm_post, v_f_norm_pre, v_f_w_up, v_f_conv_w, v_f_conv_b, v_f_w_down, v_f_norm_post):
    given = dict(locals())
    w = {n: given[n] for n in WEIGHTS}
    mom = {n: given["m_" + n] for n in WEIGHTS}
    var = {n: given["v_" + n] for n in WEIGHTS}
    chip = 2 * lax.axis_index("x") + lax.axis_index("y")
    core = lax.axis_index("c")
    place = jnp.stack([chip, core]).astype(jnp.int32)

    small_all = _allgather_small("gather_small", _pack([w[n] for n in SMALL_SHARDED]))
    small_parts = _unpack(small_all, SMALL_SHARDED, _shard_shape)
    slots = {b: _cast_into_slot("cast_" + b, place, _local_shard(w, wn, layer), layer) for b, wn, layer in BUFFERS}
    pipeline = _Pipeline(place, slots)
    pipeline.gather_now("gather_first", ["a_w_in"])
    p = {}
    for n in SMALL:
        p[n] = _join_chips(n, small_parts[n]) if n in SMALL_SHARDED else w[n]
    p["a_conv_w"] = p["a_conv_w"][0]
    p["kv_norm"] = p["kv_norm"].reshape(1, D_MODEL)

    loss_local, grad_x, g = _local_step(x[0], loss_target[0], p, pipeline)

    small_sum = _allreduce_small("reduce_small", _pack([g[n].reshape(FULL_SHAPE[n]) for n in SMALL]
                                                       + [loss_local.reshape(1, 1)]))
    small_red = _unpack(small_sum, SMALL + ["loss"], lambda n: (1, 1) if n == "loss" else FULL_SHAPE[n])
    loss = small_red["loss"][0, 0]
    grads = {}
    for n in SMALL:
        if SHARD_AXIS[n] is None:
            grads[n] = small_red[n]
        else:
            grads[n] = lax.dynamic_index_in_dim(_split_chips(n, small_red[n]), chip, 0, keepdims=False)

    delta, new_m, new_v = {}, {}, {}
    for n in sorted(BIG, key=lambda name: name in REDUCE_LAST):
        shape = _shard_shape(n)
        if n in REDUCE_LAST:
            pipeline.finish()
        if n in TRANSPOSED:
            g2d = pipeline.shard(n)
            w2d, m2d, v2d = (arrays[n][0].T for arrays in (w, mom, var))
            back = lambda a: a.T.reshape(shape)
        else:
            g2d = (jnp.concatenate([pipeline.shard(n + "0"), pipeline.shard(n + "1")], axis=0)
                   if n in ("f_w_up", "f_w_down") else pipeline.shard(n))
            w2d, m2d, v2d = (_as2d(arrays[n]) for arrays in (w, mom, var))
            back = lambda a: a.reshape(shape)
        d, m2, v2 = _adamw("adamw_" + n, w2d, g2d, m2d, v2d, steps=pipeline.steps("adamw_" + n))
        grads[n], delta[n], new_m[n], new_v[n] = back(g2d), back(d), back(m2), back(v2)
    at_least_2d = lambda n: (1,) * (2 - len(_shard_shape(n))) + _shard_shape(n)
    outs = _adamw_small("adamw_small", *[[src[n].reshape(at_least_2d(n)) for n in SMALL] for src in (w, grads, mom, var)])
    for dst, arrays in zip((delta, new_m, new_v), outs):
        dst.update({n: a.reshape(_shard_shape(n)) for n, a in zip(SMALL, arrays)})

    return (loss, grad_x[None], *[grads[n].reshape(_shard_shape(n)) for n in WEIGHTS],
            *[delta[n] for n in WEIGHTS], *[new_m[n] for n in WEIGHTS], *[new_v[n] for n in WEIGHTS])
```

```python
import functools
import math

import jax
import jax.numpy as jnp
from jax import lax
from jax.experimental import pallas as pl
from jax.experimental.pallas import tpu as pltpu

F32, BF16 = jnp.float32, jnp.bfloat16
MESH = pl.DeviceIdType.MESH

D_MODEL = 1024
N_META = 16
CHUNK = 128
PAD_ROWS = CHUNK - N_META
D_INNER = 2048
D_STATE = 128
N_GROUPS = 4
HEADS_PER_GROUP = 8
SSM_HEADS = 32
HEAD_DIM = 64
D_BC = N_GROUPS * D_STATE
D_XBC = D_INNER + 2 * D_BC
D_MAIN = D_INNER + D_XBC
D_IN_PROJ = D_MAIN + SSM_HEADS
GROUP_W = HEADS_PER_GROUP * HEAD_DIM
SSM_CONV = 4
D_FF = 2816
FFN_CONV = 3
N_Q_HEADS = 16
N_KV_HEADS = 4
D_KV = 256
ATTN_SCALE = 1.0 / math.sqrt(HEAD_DIM)
RMS_EPS = 1e-6
NEG_INF = -1e30
LANES = 128
VMEM_LIMIT = 40 * 1024 * 1024

ADAM_LR, ADAM_B1, ADAM_B2, ADAM_EPS, ADAM_WD, ADAM_STEP = 0.001, 0.9, 0.999, 1e-08, 0.01, 10

N_CHIPS = 4


def _cparams(sem=None):
    return pltpu.CompilerParams(dimension_semantics=sem, vmem_limit_bytes=VMEM_LIMIT)


def _tile(n, cands=(512, 256, 128)):
    for t in cands:
        if n % t == 0:
            return t
    return n


def _row_tile(rows, width):
    for t in (544, 272):
        if rows % t == 0 and t * width * 4 <= (3 << 20):
            return t
    return 128


def _rows_mask(i, tm):
    rows = i * tm + lax.broadcasted_iota(jnp.int32, (tm, 1), 0)
    return rows >= PAD_ROWS


def _dot(a, b):
    return jnp.dot(a, b, preferred_element_type=F32)


def _dot_nt(a, b):
    return lax.dot_general(a, b, (((1,), (1,)), ((), ())), preferred_element_type=F32)


def _dot_tn(a, b):
    return lax.dot_general(a, b, (((0,), (0,)), ((), ())), preferred_element_type=F32)


def _sigmoid(x):
    return 1.0 / (1.0 + jnp.exp(-x))


def _place():
    return lax.axis_index("x"), lax.axis_index("y"), lax.axis_index("c")


def _other_chips(x, y):
    return [(1 - x, y), (x, 1 - y), (1 - x, 1 - y)]


class _Step:
    def __init__(self, ins, outs, aliases, n_sems, start, finish):
        self.ins, self.outs, self.aliases, self.n_sems = list(ins), list(outs), dict(aliases), n_sems
        self.start, self.finish = start, finish
        self.results = None


def _like(a):
    return jax.ShapeDtypeStruct(a.shape, a.dtype)


def _remote(src, dst, send_sems, recv_sems, k, device):
    return pltpu.make_async_remote_copy(src, dst, send_sems.at[k], recv_sems.at[k], device_id=device, device_id_type=MESH)


def _half(ref, split, which, lead=()):
    if split == "rows":
        hr = ref.shape[-2] // 2
        return ref.at[lead + (pl.ds(which * hr, hr),)]
    hc = ref.shape[-1] // 2
    return ref.at[lead + (slice(None), pl.ds(which * hc, hc))]


def _splits(bufs, splits):
    return list(splits) if splits is not None else ["rows"] * len(bufs)


ALL_PEERS = (0, 1, 2)
NEAR_PEERS = (0, 1)
FAR_PEERS = (2,)


def _step_gather_ici(bufs, splits=None, peers=ALL_PEERS):
    splits = _splits(bufs, splits)

    def copies(outs, send_sems, recv_sems, received):
        x, y, c = _place()
        me = 2 * x + y
        for k, o in enumerate(outs):
            for j, (cx, cy) in enumerate(_other_chips(x, y)):
                if j in peers:
                    part = _half(o, splits[k], c, (2 * cx + cy if received else me,))
                    yield _remote(part, part, send_sems, recv_sems, 3 * k + j, (cx, cy, c))

    def start(ins, outs, send_sems, recv_sems):
        for cp in copies(outs, send_sems, recv_sems, False):
            cp.start()

    def finish(ins, outs, send_sems, recv_sems):
        for cp in copies(outs, send_sems, recv_sems, True):
            cp.wait_recv()
        for cp in copies(outs, send_sems, recv_sems, False):
            cp.wait_send()

    return _Step(bufs, [_like(b) for b in bufs], {k: k for k in range(len(bufs))}, 3 * len(bufs), start, finish)


def _step_gather_d2d(bufs, splits=None):
    splits = _splits(bufs, splits)

    def copies(outs, send_sems, recv_sems, received):
        x, y, c = _place()
        for k, o in enumerate(outs):
            for j, (cx, cy) in enumerate(_other_chips(x, y)):
                part = _half(o, splits[k], 1 - c if received else c, (2 * cx + cy,))
                yield _remote(part, part, send_sems, recv_sems, 3 * k + j, (x, y, 1 - c))

    def start(ins, outs, send_sems, recv_sems):
        for cp in copies(outs, send_sems, recv_sems, False):
            cp.start()

    def finish(ins, outs, send_sems, recv_sems):
        for cp in copies(outs, send_sems, recv_sems, True):
            cp.wait_recv()
        for cp in copies(outs, send_sems, recv_sems, False):
            cp.wait_send()

    return _Step(bufs, [_like(b) for b in bufs], {k: k for k in range(len(bufs))}, 3 * len(bufs), start, finish)


def _step_gather_full(bufs, splits=None):
    n = len(bufs)
    splits = _splits(bufs, splits)

    def ici(outs, send_sems, recv_sems, received):
        x, y, c = _place()
        me = 2 * x + y
        for k, o in enumerate(outs):
            for j, (cx, cy) in enumerate(_other_chips(x, y)):
                part = _half(o, splits[k], c, (2 * cx + cy if received else me,))
                yield _remote(part, part, send_sems, recv_sems, 3 * k + j, (cx, cy, c))

    def d2d(outs, send_sems, recv_sems, received):
        x, y, c = _place()
        for k, o in enumerate(outs):
            for j, (cx, cy) in enumerate(_other_chips(x, y)):
                part = _half(o, splits[k], 1 - c if received else c, (2 * cx + cy,))
                yield _remote(part, part, send_sems, recv_sems, 3 * n + 3 * k + j, (x, y, 1 - c))

    def start(ins, outs, send_sems, recv_sems):
        for cp in ici(outs, send_sems, recv_sems, False):
            cp.start()

    def finish(ins, outs, send_sems, recv_sems):
        for arrived, onward in zip(ici(outs, send_sems, recv_sems, True), d2d(outs, send_sems, recv_sems, False)):
            arrived.wait_recv()
            onward.start()
        for cp in d2d(outs, send_sems, recv_sems, True):
            cp.wait_recv()
        for cp in ici(outs, send_sems, recv_sems, False):
            cp.wait_send()
        for cp in d2d(outs, send_sems, recv_sems, False):
            cp.wait_send()

    return _Step(bufs, [_like(b) for b in bufs], {k: k for k in range(n)}, 6 * n, start, finish)


def _half_shape(shape, split):
    return shape[:-2] + ((shape[-2] // 2, shape[-1]) if split == "rows" else (shape[-2], shape[-1] // 2))


def _step_pair_exchange(grads, splits=None):
    splits = _splits(grads, splits)

    def copies(ins, outs, send_sems, recv_sems):
        x, y, c = _place()
        for k, (g, o) in enumerate(zip(ins, outs)):
            yield _remote(_half(g, splits[k], 1 - c, (slice(None),)), o, send_sems, recv_sems, k, (x, y, 1 - c))

    def start(ins, outs, send_sems, recv_sems):
        for cp in copies(ins, outs, send_sems, recv_sems):
            cp.start()

    def finish(ins, outs, send_sems, recv_sems):
        for cp in copies(ins, outs, send_sems, recv_sems):
            cp.wait()

    outs = [jax.ShapeDtypeStruct(_half_shape(g.shape, s), g.dtype) for g, s in zip(grads, splits)]
    return _Step(grads, outs, {}, len(grads), start, finish)


def _step_chip_exchange(partials, peers=ALL_PEERS, into=None):
    n = len(partials)

    def copies(ins, outs, send_sems, recv_sems):
        x, y, c = _place()
        for k, (q, o) in enumerate(zip(ins[:n], outs)):
            for j, (cx, cy) in enumerate(_other_chips(x, y)):
                if j in peers:
                    yield _remote(q.at[2 * cx + cy], o.at[j], send_sems, recv_sems, 3 * k + j, (cx, cy, c))

    def start(ins, outs, send_sems, recv_sems):
        for cp in copies(ins, outs, send_sems, recv_sems):
            cp.start()

    def finish(ins, outs, send_sems, recv_sems):
        for cp in copies(ins, outs, send_sems, recv_sems):
            cp.wait()

    outs = [jax.ShapeDtypeStruct((3,) + q.shape[1:], q.dtype) for q in partials]
    if into is None:
        return _Step(partials, outs, {}, 3 * n, start, finish)
    return _Step(list(partials) + list(into), outs, {n + k: k for k in range(n)}, 3 * n, start, finish)


def _step_pair_gather(shards, splits=None):
    splits = _splits(shards, splits)

    def copies(outs, send_sems, recv_sems, received):
        x, y, c = _place()
        for k, o in enumerate(outs):
            part = _half(o, splits[k], 1 - c if received else c)
            yield _remote(part, part, send_sems, recv_sems, k, (x, y, 1 - c))

    def start(ins, outs, send_sems, recv_sems):
        for cp in copies(outs, send_sems, recv_sems, False):
            cp.start()

    def finish(ins, outs, send_sems, recv_sems):
        for cp in copies(outs, send_sems, recv_sems, True):
            cp.wait_recv()
        for cp in copies(outs, send_sems, recv_sems, False):
            cp.wait_send()

    return _Step(shards, [_like(s) for s in shards], {k: k for k in range(len(shards))}, len(shards), start, finish)


def _call(body, *, name, out_shape, grid, in_specs, out_specs, operands, scratch_shapes=(), semantics=None, steps=()):
    single = not isinstance(out_shape, (tuple, list))
    out_shapes = [out_shape] if single else list(out_shape)
    out_spec_list = [out_specs] if single else list(out_specs)
    steps = list(steps)
    if not steps:
        res = pl.pallas_call(body, name=name, out_shape=out_shapes, grid=grid, in_specs=list(in_specs),
                             out_specs=out_spec_list, scratch_shapes=list(scratch_shapes),
                             compiler_params=_cparams(semantics))(*operands)
        return res[0] if single else res
    n_in, n_out, n_scr = len(operands), len(out_shapes), len(scratch_shapes)
    x_in = [a for s in steps for a in s.ins]
    x_out = [o for s in steps for o in s.outs]
    aliases, in_off, out_off = {}, 0, 0
    for s in steps:
        for i, o in s.aliases.items():
            aliases[n_in + in_off + i] = n_out + out_off + o
        in_off += len(s.ins)
        out_off += len(s.outs)
    sems = []
    for s in steps:
        sems += [pltpu.SemaphoreType.DMA((s.n_sems,)), pltpu.SemaphoreType.DMA((s.n_sems,))]
    any_spec = pl.BlockSpec(memory_space=pl.ANY)

    def carried(*refs):
        pos = 0
        ins = refs[pos:pos + n_in]; pos += n_in
        xi = refs[pos:pos + len(x_in)]; pos += len(x_in)
        outs = refs[pos:pos + n_out]; pos += n_out
        xo = refs[pos:pos + len(x_out)]; pos += len(x_out)
        scr = refs[pos:pos + n_scr]; pos += n_scr
        sem_refs = refs[pos:]

        def each(action):
            i0 = o0 = 0
            for k, s in enumerate(steps):
                getattr(s, action)(xi[i0:i0 + len(s.ins)], xo[o0:o0 + len(s.outs)], sem_refs[2 * k], sem_refs[2 * k + 1])
                i0 += len(s.ins)
                o0 += len(s.outs)

        if grid:
            first = functools.reduce(jnp.logical_and, [pl.program_id(d) == 0 for d in range(len(grid))])
            last = functools.reduce(jnp.logical_and, [pl.program_id(d) == grid[d] - 1 for d in range(len(grid))])
            pl.when(first)(lambda: each("start"))
            body(*ins, *outs, *scr)
            pl.when(last)(lambda: each("finish"))
        else:
            each("start")
            body(*ins, *outs, *scr)
            each("finish")

    res = pl.pallas_call(
        carried, name=name, out_shape=out_shapes + x_out, grid=grid,
        in_specs=list(in_specs) + [any_spec] * len(x_in), out_specs=out_spec_list + [any_spec] * len(x_out),
        scratch_shapes=list(scratch_shapes) + sems, input_output_aliases=aliases,
        compiler_params=_cparams(None if semantics is None else ("arbitrary",) * len(grid)),
    )(*operands, *x_in)
    o0 = n_out
    for s in steps:
        s.results = list(res[o0:o0 + len(s.outs)])
        o0 += len(s.outs)
    return res[0] if single else tuple(res[:n_out])


def _run_steps(name, steps):
    _call(lambda: None, name=name, out_shape=[], grid=(), in_specs=[], out_specs=[], operands=[], steps=steps)
    return [s.results for s in steps]


def _mm(name, a, b, mode, out_dtype=F32, acc=None, b_colblock=0, k_rows=None, out_rows=None, steps=()):
    resident_bytes = 8 << 20
    if mode == "nn":
        m, k = a.shape
        n = b.shape[1]
        tm = m
        while tm * k * 2 > resident_bytes and tm % 32 == 0:
            tm //= 2
        tn = _tile(n)
        grid = (m // tm, n // tn)
        in_specs = [pl.BlockSpec((tm, k), lambda i, j: (i, 0)), pl.BlockSpec((k, tn), lambda i, j: (0, j))]
        out_shape, out_block = (m, n), (tm, tn)
    elif mode == "nt":
        m, n = a.shape
        k = k_rows or b.shape[0]
        tm = m
        while tm * n * 2 > resident_bytes and tm % 32 == 0:
            tm //= 2
        tk = _tile(k)
        grid = (m // tm, k // tk)
        in_specs = [pl.BlockSpec((tm, n), lambda i, j: (i, 0)), pl.BlockSpec((tk, n), lambda i, j: (j, b_colblock))]
        out_shape, out_block = (m, k), (tm, tk)
    else:
        m, k = a.shape
        n = b.shape[1]
        tk, tn = _tile(k), (n if m * n * 2 <= resident_bytes else _tile(n))
        grid = (k // tk, n // tn)
        in_specs = [pl.BlockSpec((m, tk), lambda i, j: (0, i)), pl.BlockSpec((m, tn), lambda i, j: (0, j))]
        out_shape, out_block = (out_rows or k, n), (tk, tn)
    out_spec = pl.BlockSpec(out_block, lambda i, j: (i, j))
    has_acc = acc is not None

    def body(*refs):
        a_ref, b_ref = refs[0], refs[1]
        o_ref = refs[-1]
        av, bv = a_ref[...], b_ref[...]
        if mode == "nn":
            r = _dot(av, bv)
        elif mode == "nt":
            r = _dot_nt(av, bv)
        else:
            r = _dot_tn(av, bv)
        if has_acc:
            r = r + refs[2][...]
        o_ref[...] = r.astype(o_ref.dtype)

    operands = [a, b]
    if has_acc:
        in_specs = in_specs + [out_spec]
        operands.append(acc)
    return _call(body, name=name, out_shape=jax.ShapeDtypeStruct(out_shape, out_dtype), grid=grid, in_specs=in_specs,
                 out_specs=out_spec, operands=operands, semantics=("parallel", "parallel"), steps=steps)


def _tn_rows_into(name, a, b, into, row0, nrows):
    m, k = a.shape
    n = b.shape[1]

    def body(a_ref, b_ref, into_ref, o_ref):
        o_ref[...] = _dot_tn(a_ref[...], b_ref[...])[0:nrows].astype(o_ref.dtype)

    return pl.pallas_call(
        body, name=name, out_shape=jax.ShapeDtypeStruct(into.shape, into.dtype), grid=(1,),
        in_specs=[pl.BlockSpec((m, k), lambda i: (0, 0)), pl.BlockSpec((m, n), lambda i: (0, 0)),
                  pl.BlockSpec(memory_space=pl.ANY)],
        out_specs=pl.BlockSpec((nrows, n), lambda i: (row0 // nrows, 0)),
        input_output_aliases={2: 0}, compiler_params=_cparams(("arbitrary",)),
    )(a, b, into)


def _rms_fwd(name, h, w):
    rows, width = h.shape
    tm = _row_tile(rows, width)

    def body(h_ref, w_ref, o_ref):
        x = h_ref[...]
        r = lax.rsqrt(jnp.mean(x * x, axis=-1, keepdims=True) + RMS_EPS)
        o_ref[...] = (x * r * w_ref[...]).astype(BF16)

    return pl.pallas_call(
        body, name=name, out_shape=jax.ShapeDtypeStruct((rows, width), BF16), grid=(rows // tm,),
        in_specs=[pl.BlockSpec((tm, width), lambda i: (i, 0)), pl.BlockSpec((1, width), lambda i: (0, 0))],
        out_specs=pl.BlockSpec((tm, width), lambda i: (i, 0)), compiler_params=_cparams(("parallel",)),
    )(h, w)


def _resid_norm_fwd(name, h, pre, w, next_norms=()):
    rows, width = h.shape
    tm = _row_tile(rows, width)
    n_next = len(next_norms)

    def body(*refs):
        h_ref, p_ref, w_ref = refs[:3]
        v_refs = refs[3:3 + n_next]
        o_ref = refs[3 + n_next]
        n_refs = refs[4 + n_next:]
        p = p_ref[...]
        r = lax.rsqrt(jnp.mean(p * p, axis=-1, keepdims=True) + RMS_EPS)
        x = h_ref[...] + jnp.where(_rows_mask(pl.program_id(0), tm), p * r * w_ref[...], 0.0)
        o_ref[...] = x
        if n_next:
            rx = lax.rsqrt(jnp.mean(x * x, axis=-1, keepdims=True) + RMS_EPS)
            for v_ref, n_ref in zip(v_refs, n_refs):
                n_ref[...] = (x * rx * v_ref[...]).astype(BF16)

    row_spec = pl.BlockSpec((tm, width), lambda i: (i, 0))
    vec_spec = pl.BlockSpec((1, width), lambda i: (0, 0))
    outs = pl.pallas_call(
        body, name=name,
        out_shape=[jax.ShapeDtypeStruct((rows, width), F32)] + [jax.ShapeDtypeStruct((rows, width), BF16)] * n_next,
        grid=(rows // tm,), in_specs=[row_spec, row_spec, vec_spec] + [vec_spec] * n_next,
        out_specs=[row_spec] * (1 + n_next), compiler_params=_cparams(("parallel",)),
    )(h, pre, w, *next_norms)
    return outs[0], list(outs[1:])


def _resid_norm_loss(name, h, pre, w, target):
    rows, width = h.shape

    def body(h_ref, p_ref, w_ref, t_ref, dh_ref, loss_ref, dp_ref, dw_ref):
        i = pl.program_id(0)
        p = p_ref[...]
        r = lax.rsqrt(jnp.mean(p * p, axis=-1, keepdims=True) + RMS_EPS)
        x = h_ref[...] + p * r * w_ref[...]
        real = (i + jnp.zeros((CHUNK, 1), jnp.int32)) >= 1
        diff = jnp.where(real, x - t_ref[...], 0.0)
        dh = diff * (1.0 / D_MODEL)
        dh_ref[...] = dh
        dp, dw_rows = _rms_bwd(dh, p, w_ref[...])
        dp_ref[...] = dp.astype(BF16)

        @pl.when(i == 0)
        def _():
            loss_ref[...] = jnp.zeros_like(loss_ref)
            dw_ref[...] = jnp.zeros_like(dw_ref)

        loss_ref[...] += jnp.sum(diff * diff) * (0.5 / D_MODEL)
        dw_ref[...] += jnp.sum(dw_rows, axis=0, keepdims=True)

    blk = pl.BlockSpec((CHUNK, width), lambda i: (i, 0))
    vec_spec = pl.BlockSpec((1, width), lambda i: (0, 0))
    return pl.pallas_call(
        body, name=name,
        out_shape=(jax.ShapeDtypeStruct((rows, width), F32), jax.ShapeDtypeStruct((1, LANES), F32),
                   jax.ShapeDtypeStruct((rows, width), BF16), jax.ShapeDtypeStruct((1, width), F32)),
        grid=(rows // CHUNK,),
        in_specs=[blk, blk, vec_spec, pl.BlockSpec((CHUNK, width), lambda i: (jnp.maximum(i - 1, 0), 0))],
        out_specs=(blk, pl.BlockSpec((1, LANES), lambda i: (0, 0)), blk, vec_spec),
        compiler_params=_cparams(("arbitrary",)),
    )(h, pre, w, target)


def _rms_bwd(dy, x, w):
    r = lax.rsqrt(jnp.mean(x * x, axis=-1, keepdims=True) + RMS_EPS)
    xhat = x * r
    dxhat = dy * w
    return r * (dxhat - xhat * jnp.mean(dxhat * xhat, axis=-1, keepdims=True)), dy * xhat


def _norm_bwd_add(name, dh, dhn, h, w, then=None, steps=()):
    rows, width = dh.shape
    tm = _row_tile(rows, width)
    fused = then is not None

    def body(*refs):
        dh_ref, dhn_ref, h_ref, w_ref = refs[:4]
        o_ref, dw_ref = refs[6:8] if fused else refs[4:6]
        i = pl.program_id(0)
        valid = _rows_mask(i, tm)
        dx, dw_rows = _rms_bwd(dhn_ref[...], h_ref[...], w_ref[...])
        dh_new = dh_ref[...] + jnp.where(valid, dx, 0.0)
        o_ref[...] = dh_new

        @pl.when(i == 0)
        def _():
            dw_ref[...] = jnp.zeros_like(dw_ref)

        dw_ref[...] += jnp.sum(dw_rows, axis=0, keepdims=True)
        if fused:
            p_ref, wp_ref, dp_ref, dwp_ref = refs[4], refs[5], refs[8], refs[9]
            dp, dwp_rows = _rms_bwd(jnp.where(valid, dh_new, 0.0), p_ref[...], wp_ref[...])
            dp_ref[...] = dp.astype(BF16)

            @pl.when(i == 0)
            def _():
                dwp_ref[...] = jnp.zeros_like(dwp_ref)

            dwp_ref[...] += jnp.sum(dwp_rows, axis=0, keepdims=True)

    row_spec = pl.BlockSpec((tm, width), lambda i: (i, 0))
    vec_spec = pl.BlockSpec((1, width), lambda i: (0, 0))
    row_f32, vec_f32 = jax.ShapeDtypeStruct((rows, width), F32), jax.ShapeDtypeStruct((1, width), F32)
    in_specs, operands = [row_spec, row_spec, row_spec, vec_spec], [dh, dhn, h, w]
    out_shape, out_specs = [row_f32, vec_f32], [row_spec, vec_spec]
    if fused:
        in_specs += [row_spec, vec_spec]
        operands += list(then)
        out_shape += [jax.ShapeDtypeStruct((rows, width), BF16), vec_f32]
        out_specs += [row_spec, vec_spec]
    return _call(body, name=name, out_shape=out_shape, grid=(rows // tm,), in_specs=in_specs, out_specs=out_specs,
                 operands=operands, semantics=("arbitrary",), steps=steps)


def _shift_down(x, s, rows):
    return pltpu.roll(x, s, 0) if s else x


def _shift_up(x, s, rows):
    return pltpu.roll(x, rows - s, 0) if s else x


def _conv4_fwd(name, zx, cw, cb, steps=()):
    rows = zx.shape[0]
    off = D_INNER // LANES

    def body(x_ref, w_ref, b_ref, o_ref):
        x = x_ref[...]
        acc = b_ref[...] + w_ref[pl.ds(SSM_CONV - 1, 1), :] * x
        for s in range(1, SSM_CONV):
            acc = acc + w_ref[pl.ds(SSM_CONV - 1 - s, 1), :] * _shift_down(x, s, rows)
        valid = lax.broadcasted_iota(jnp.int32, (rows, 1), 0) >= PAD_ROWS
        o_ref[...] = jnp.where(valid, acc * _sigmoid(acc), 0.0)

    return _call(
        body, name=name, out_shape=jax.ShapeDtypeStruct((rows, D_XBC), F32), grid=(D_XBC // LANES,),
        in_specs=[pl.BlockSpec((rows, LANES), lambda j: (0, j + off)),
                  pl.BlockSpec((SSM_CONV, LANES), lambda j: (0, j)),
                  pl.BlockSpec((1, LANES), lambda j: (0, j))],
        out_specs=pl.BlockSpec((rows, LANES), lambda j: (0, j)), operands=[zx, cw, cb],
        semantics=("parallel",), steps=steps)


def _conv4_bwd(name, zx, dout, cw, cb, col0):
    rows, width = dout.shape
    zoff = (D_INNER + col0) // LANES
    woff = col0 // LANES

    def body(x_ref, d_ref, w_ref, b_ref, dx_ref, dw_ref, db_ref):
        x = x_ref[...]
        shifted = [_shift_down(x, s, rows) for s in range(SSM_CONV)]
        acc = b_ref[...]
        for s in range(SSM_CONV):
            acc = acc + w_ref[pl.ds(SSM_CONV - 1 - s, 1), :] * shifted[s]
        sig = _sigmoid(acc)
        valid = lax.broadcasted_iota(jnp.int32, (rows, 1), 0) >= PAD_ROWS
        dpre = jnp.where(valid, d_ref[...] * sig * (1.0 + acc * (1.0 - sig)), 0.0)
        dx = w_ref[pl.ds(SSM_CONV - 1, 1), :] * dpre
        for s in range(1, SSM_CONV):
            dx = dx + w_ref[pl.ds(SSM_CONV - 1 - s, 1), :] * _shift_up(dpre, s, rows)
        dx_ref[...] = dx.astype(BF16)
        for s in range(SSM_CONV):
            dw_ref[pl.ds(SSM_CONV - 1 - s, 1), :] = jnp.sum(dpre * shifted[s], axis=0, keepdims=True)
        db_ref[...] = jnp.sum(dpre, axis=0, keepdims=True)

    return pl.pallas_call(
        body, name=name,
        out_shape=(jax.ShapeDtypeStruct((rows, width), BF16), jax.ShapeDtypeStruct((SSM_CONV, width), F32),
                   jax.ShapeDtypeStruct((1, width), F32)),
        grid=(width // LANES,),
        in_specs=[pl.BlockSpec((rows, LANES), lambda j: (0, j + zoff)),
                  pl.BlockSpec((rows, LANES), lambda j: (0, j)),
                  pl.BlockSpec((SSM_CONV, LANES), lambda j: (0, j + woff)),
                  pl.BlockSpec((1, LANES), lambda j: (0, j + woff))],
        out_specs=(pl.BlockSpec((rows, LANES), lambda j: (0, j)),
                   pl.BlockSpec((SSM_CONV, LANES), lambda j: (0, j)),
                   pl.BlockSpec((1, LANES), lambda j: (0, j))),
        compiler_params=_cparams(("parallel",)),
    )(zx, dout, cw, cb)


FFN_TILE = 2 * LANES


def _ffn_up_conv(name, hn, w_up, cw, cb, steps=()):
    rows, k = hn.shape
    chip_blocks = w_up.shape[2] // LANES
    half_blocks = D_FF // LANES
    nt = D_FF // FFN_TILE

    def weight_block(offset):
        return pl.BlockSpec((None, k, LANES), lambda j: ((2 * j + offset) // chip_blocks, 0, (2 * j + offset) % chip_blocks))

    def body(a_ref, g0, g1, v0, v1, wg_ref, wv_ref, bg_ref, bv_ref, upg_ref, upv_ref, act_ref):
        a = a_ref[...]
        g = _dot(a, jnp.concatenate([g0[...], g1[...]], axis=1))
        v = _dot(a, jnp.concatenate([v0[...], v1[...]], axis=1))
        upg_ref[...] = g
        upv_ref[...] = v
        ug, uv = bg_ref[...], bv_ref[...]
        for s in range(FFN_CONV):
            ug = ug + wg_ref[pl.ds(FFN_CONV - 1 - s, 1), :] * _shift_down(g, s, rows)
            uv = uv + wv_ref[pl.ds(FFN_CONV - 1 - s, 1), :] * _shift_down(v, s, rows)
        act_ref[...] = (ug * _sigmoid(ug) * uv).astype(BF16)

    col = pl.BlockSpec((rows, FFN_TILE), lambda j: (0, j))
    wsp = lambda shift: pl.BlockSpec((FFN_CONV, FFN_TILE), lambda j: (0, j + shift))
    bsp = lambda shift: pl.BlockSpec((1, FFN_TILE), lambda j: (0, j + shift))
    half = jax.ShapeDtypeStruct((rows, D_FF), F32)
    return _call(
        body, name=name, out_shape=(half, half, jax.ShapeDtypeStruct((rows, D_FF), BF16)), grid=(nt,),
        in_specs=[pl.BlockSpec((rows, k), lambda j: (0, 0)), weight_block(0), weight_block(1),
                  weight_block(half_blocks), weight_block(half_blocks + 1), wsp(0), wsp(nt), bsp(0), bsp(nt)],
        out_specs=(col, col, col), operands=[hn, w_up, w_up, w_up, w_up, cw, cw, cb, cb],
        semantics=("parallel",), steps=steps)


def _ffn_conv_bwd(name, up_g, up_v, dact, cw, cb, hn, w_up, steps=()):
    rows, k = hn.shape
    chip_blocks = w_up.shape[2] // LANES
    nt = D_FF // LANES

    def weight_block(shift):
        return pl.BlockSpec((None, k, LANES), lambda j: ((j + shift) // chip_blocks, 0, (j + shift) % chip_blocks))

    def body(g_ref, v_ref, d_ref, wg_ref, wv_ref, bg_ref, bv_ref, upg_ref, upv_ref, hn_ref,
             dwg_ref, dwv_ref, dbg_ref, dbv_ref, dhn_ref, dup_ref, acc, hn_scr, hnt_scr, dup_scr, sems):
        j = pl.program_id(0)
        hn_copy = pltpu.make_async_copy(hn_ref, hn_scr, sems.at[0])
        dhn_copy = pltpu.make_async_copy(acc, dhn_ref, sems.at[0])

        def dup_copy(step, half):
            block, slot = step + half * nt, 2 * (step % 2) + half
            cols = pl.ds(pl.multiple_of((block % chip_blocks) * LANES, LANES), LANES)
            return pltpu.make_async_copy(dup_scr.at[slot], dup_ref.at[block // chip_blocks, :, cols], sems.at[1 + slot])

        @pl.when(j == 0)
        def _():
            hn_copy.start()
            acc[...] = jnp.zeros_like(acc)
            hn_copy.wait()
            for r in range(0, rows, LANES):
                hnt_scr[:, r:r + LANES] = hn_scr[r:r + LANES, :].T

        @pl.when(j >= 2)
        def _():
            dup_copy(j - 2, 0).wait()
            dup_copy(j - 2, 1).wait()

        g, v = g_ref[...], v_ref[...]
        gs = [_shift_down(g, s, rows) for s in range(FFN_CONV)]
        vs = [_shift_down(v, s, rows) for s in range(FFN_CONV)]
        ug, uv = bg_ref[...], bv_ref[...]
        for s in range(FFN_CONV):
            ug = ug + wg_ref[pl.ds(FFN_CONV - 1 - s, 1), :] * gs[s]
            uv = uv + wv_ref[pl.ds(FFN_CONV - 1 - s, 1), :] * vs[s]
        sig = _sigmoid(ug)
        dsig = d_ref[...] * sig
        dup = []
        for dpre, src, w_ref, dw_ref, db_ref in (
                (dsig * uv * (1.0 + ug * (1.0 - sig)), gs, wg_ref, dwg_ref, dbg_ref),
                (dsig * ug, vs, wv_ref, dwv_ref, dbv_ref)):
            dx = w_ref[pl.ds(FFN_CONV - 1, 1), :] * dpre
            for s in range(1, FFN_CONV):
                dx = dx + w_ref[pl.ds(FFN_CONV - 1 - s, 1), :] * _shift_up(dpre, s, rows)
            dup.append(dx.astype(BF16))
            for s in range(FFN_CONV):
                dw_ref[pl.ds(FFN_CONV - 1 - s, 1), :] = jnp.sum(dpre * src[s], axis=0, keepdims=True)
            db_ref[...] = jnp.sum(dpre, axis=0, keepdims=True)
        dup = jnp.concatenate(dup, axis=1)
        acc[...] += _dot_nt(dup, jnp.concatenate([upg_ref[...], upv_ref[...]], axis=1))
        dw = _dot(hnt_scr[...], dup)
        slot = 2 * (j % 2)
        dup_scr[slot] = dw[:, :LANES].astype(BF16)
        dup_scr[slot + 1] = dw[:, LANES:].astype(BF16)
        dup_copy(j, 0).start()
        dup_copy(j, 1).start()

        @pl.when(j == nt - 1)
        def _():
            dhn_copy.start()
            for step in (j - 1, j):
                dup_copy(step, 0).wait()
                dup_copy(step, 1).wait()
            dhn_copy.wait()

    col = pl.BlockSpec((rows, LANES), lambda j: (0, j))
    wsp = lambda shift: pl.BlockSpec((FFN_CONV, LANES), lambda j: (0, j + shift))
    bsp = lambda shift: pl.BlockSpec((1, LANES), lambda j: (0, j + shift))
    any_spec = pl.BlockSpec(memory_space=pl.ANY)
    dw_shape = jax.ShapeDtypeStruct((FFN_CONV, D_FF), F32)
    db_shape = jax.ShapeDtypeStruct((1, D_FF), F32)
    return _call(
        body, name=name, grid=(nt,),
        out_shape=(dw_shape, dw_shape, db_shape, db_shape, jax.ShapeDtypeStruct((rows, k), F32),
                   jax.ShapeDtypeStruct(w_up.shape, BF16)),
        in_specs=[col, col, col, wsp(0), wsp(nt), bsp(0), bsp(nt), weight_block(0), weight_block(nt), any_spec],
        out_specs=(wsp(0), wsp(0), bsp(0), bsp(0), any_spec, any_spec),
        operands=[up_g, up_v, dact, cw, cw, cb, cb, w_up, w_up, hn],
        scratch_shapes=[pltpu.VMEM((rows, k), F32), pltpu.VMEM((rows, k), BF16), pltpu.VMEM((k, rows), BF16),
                        pltpu.VMEM((4, k, LANES), BF16), pltpu.SemaphoreType.DMA((5,))],
        semantics=("arbitrary",), steps=steps)


def _dt_fwd(name, dtr, bias):
    rows = dtr.shape[0]
    tm = _row_tile(rows, LANES)

    def body(d_ref, b_ref, o_ref):
        v = d_ref[...] + b_ref[...]
        sp = jnp.maximum(v, 0.0) + jnp.log1p(jnp.exp(-jnp.abs(v)))
        lane = lax.broadcasted_iota(jnp.int32, (tm, LANES), 1)
        ok = _rows_mask(pl.program_id(0), tm) & (lane < SSM_HEADS)
        o_ref[...] = jnp.where(ok, sp, 0.0)

    return pl.pallas_call(
        body, name=name, out_shape=jax.ShapeDtypeStruct((rows, LANES), F32), grid=(rows // tm,),
        in_specs=[pl.BlockSpec((tm, LANES), lambda i: (i, 0)), pl.BlockSpec((1, LANES), lambda i: (0, 0))],
        out_specs=pl.BlockSpec((tm, LANES), lambda i: (i, 0)), compiler_params=_cparams(("parallel",)),
    )(dtr, bias)


def _dt_bwd(name, ddt, dtr, bias):
    rows = dtr.shape[0]
    tm = _row_tile(rows, LANES)

    def body(g_ref, d_ref, b_ref, o_ref, db_ref):
        i = pl.program_id(0)
        lane = lax.broadcasted_iota(jnp.int32, (tm, LANES), 1)
        ok = _rows_mask(i, tm) & (lane < SSM_HEADS)
        dv = jnp.where(ok, g_ref[...] * _sigmoid(d_ref[...] + b_ref[...]), 0.0)
        o_ref[...] = dv.astype(BF16)

        @pl.when(i == 0)
        def _():
            db_ref[...] = jnp.zeros_like(db_ref)

        db_ref[...] += jnp.sum(dv, axis=0, keepdims=True)

    row_spec = pl.BlockSpec((tm, LANES), lambda i: (i, 0))
    vec_spec = pl.BlockSpec((1, LANES), lambda i: (0, 0))
    return pl.pallas_call(
        body, name=name,
        out_shape=(jax.ShapeDtypeStruct((rows, LANES), BF16), jax.ShapeDtypeStruct((1, LANES), F32)),
        grid=(rows // tm,), in_specs=[row_spec, row_spec, vec_spec], out_specs=(row_spec, vec_spec),
        compiler_params=_cparams(("arbitrary",)),
    )(ddt, dtr, bias)


def _gate_fwd(name, y, zx, w, steps=()):
    rows = y.shape[0]
    tm = _row_tile(rows, D_INNER)

    def body(y_ref, z_ref, w_ref, o_ref):
        z = z_ref[...]
        g = y_ref[...] * (z * _sigmoid(z))
        r = lax.rsqrt(jnp.mean(g * g, axis=-1, keepdims=True) + RMS_EPS)
        o_ref[...] = (g * r * w_ref[...]).astype(BF16)

    row_spec = pl.BlockSpec((tm, D_INNER), lambda i: (i, 0))
    return _call(
        body, name=name, out_shape=jax.ShapeDtypeStruct((rows, D_INNER), BF16), grid=(rows // tm,),
        in_specs=[row_spec, row_spec, pl.BlockSpec((1, D_INNER), lambda i: (0, 0))],
        out_specs=row_spec, operands=[y, zx, w], semantics=("parallel",), steps=steps)


def _gate_bwd(name, dyn, y, zx, w):
    rows = y.shape[0]
    tm = _row_tile(rows, D_INNER)

    def body(d_ref, y_ref, z_ref, w_ref, dy_ref, dz_ref, dw_ref):
        i = pl.program_id(0)
        z, yv = z_ref[...], y_ref[...]
        sig = _sigmoid(z)
        sz = z * sig
        g = yv * sz
        r = lax.rsqrt(jnp.mean(g * g, axis=-1, keepdims=True) + RMS_EPS)
        ghat = g * r
        dn = d_ref[...]
        dghat = dn * w_ref[...]
        dg = r * (dghat - ghat * jnp.mean(dghat * ghat, axis=-1, keepdims=True))
        dy_ref[...] = dg * sz
        dz_ref[...] = (dg * yv * sig * (1.0 + z * (1.0 - sig))).astype(BF16)

        @pl.when(i == 0)
        def _():
            dw_ref[...] = jnp.zeros_like(dw_ref)

        dw_ref[...] += jnp.sum(dn * ghat, axis=0, keepdims=True)

    row_spec = pl.BlockSpec((tm, D_INNER), lambda i: (i, 0))
    vec_spec = pl.BlockSpec((1, D_INNER), lambda i: (0, 0))
    return pl.pallas_call(
        body, name=name,
        out_shape=(jax.ShapeDtypeStruct((rows, D_INNER), F32), jax.ShapeDtypeStruct((rows, D_INNER), BF16),
                   jax.ShapeDtypeStruct((1, D_INNER), F32)),
        grid=(rows // tm,), in_specs=[row_spec, row_spec, row_spec, vec_spec],
        out_specs=(row_spec, row_spec, vec_spec), compiler_params=_cparams(("arbitrary",)),
    )(dyn, y, zx, w)


def _split3(x):
    hi = x.astype(BF16)
    r1 = x - hi.astype(F32)
    mid = r1.astype(BF16)
    lo = (r1 - mid.astype(F32)).astype(BF16)
    return hi, mid, lo


def _dot3_data_lhs(x, sel):
    sel16 = sel.astype(F32).astype(BF16)
    hi, mid, lo = _split3(x)
    return _dot(hi, sel16) + _dot(mid, sel16) + _dot(lo, sel16)


def _dot2_data_lhs(x, sel):
    sel16 = sel.astype(F32).astype(BF16)
    hi = x.astype(BF16)
    mid = (x - hi.astype(F32)).astype(BF16)
    return _dot(hi, sel16) + _dot(mid, sel16)


def _dot3_data_rhs(sel, x):
    sel16 = sel.astype(F32).astype(BF16)
    hi, mid, lo = _split3(x)
    return _dot(sel16, hi) + _dot(sel16, mid) + _dot(sel16, lo)


def _causal_masks():
    r = lax.broadcasted_iota(jnp.int32, (CHUNK, CHUNK), 0)
    c = lax.broadcasted_iota(jnp.int32, (CHUNK, CHUNK), 1)
    return r >= c, r <= c


def _expand_heads_matrix(g):
    k = lax.broadcasted_iota(jnp.int32, (LANES, GROUP_W), 0)
    j = lax.broadcasted_iota(jnp.int32, (LANES, GROUP_W), 1)
    return HEADS_PER_GROUP * g + jnp.right_shift(j, 6) == k


def _reduce_heads_matrix(g):
    j = lax.broadcasted_iota(jnp.int32, (GROUP_W, LANES), 0)
    k = lax.broadcasted_iota(jnp.int32, (GROUP_W, LANES), 1)
    return HEADS_PER_GROUP * g + jnp.right_shift(j, 6) == k


def _reduce_pair_matrix(g, p):
    j = lax.broadcasted_iota(jnp.int32, (LANES, LANES), 0)
    k = lax.broadcasted_iota(jnp.int32, (LANES, LANES), 1)
    return HEADS_PER_GROUP * g + 2 * p + jnp.right_shift(j, 6) == k


def _group_cols(ref, g, width):
    return ref.at[:, pl.ds(g * width, width)]


def _ssd_prep(name, dt, a128, steps=()):
    rows = dt.shape[0]
    nc = rows // CHUNK

    def body(dt_ref, a_ref, dte_ref, acs_ref, acst_ref):
        causal, _ = _causal_masks()
        dtv = dt_ref[...]
        acs = _dot3_data_rhs(causal, dtv) * a_ref[...]
        acst_ref[...] = acs.T[0:SSM_HEADS]
        for g in range(N_GROUPS):
            expand = _expand_heads_matrix(g)
            _group_cols(dte_ref, g, GROUP_W)[...] = _dot3_data_lhs(dtv, expand)
            _group_cols(acs_ref, g, GROUP_W)[...] = _dot3_data_lhs(acs, expand)

    blk = pl.BlockSpec((CHUNK, D_INNER), lambda c: (c, 0))
    shp = jax.ShapeDtypeStruct((rows, D_INNER), F32)
    return _call(
        body, name=name, out_shape=(shp, shp, jax.ShapeDtypeStruct((nc, SSM_HEADS, CHUNK), F32)), grid=(nc,),
        in_specs=[pl.BlockSpec((CHUNK, LANES), lambda c: (c, 0)), pl.BlockSpec((1, LANES), lambda c: (0, 0))],
        out_specs=(blk, blk, pl.BlockSpec((None, SSM_HEADS, CHUNK), lambda c: (c, 0, 0))),
        operands=[dt, a128], semantics=("parallel",), steps=steps)


def _ssd_common(x_ref, b_ref, c_ref, dte_ref, acs_ref):
    x = x_ref[...]
    dt_exp = dte_ref[...]
    acs_exp = acs_ref[...]
    tot_exp = acs_ref[pl.ds(CHUNK - 1, 1), :]
    xdt = x * dt_exp
    e_exp = jnp.exp(acs_exp)
    f_exp = jnp.exp(tot_exp - acs_exp)
    return _causal_masks(), x, dt_exp, acs_exp, tot_exp, xdt, e_exp, f_exp, b_ref[...], c_ref[...]


def _pair_decay(acs_pair, acs_row, e, causal):
    lane = lax.broadcasted_iota(jnp.int32, (CHUNK, LANES), 1)
    mine = (lane < HEAD_DIM) if e == 0 else (lane >= HEAD_DIM)
    a_l = jnp.where(mine, acs_pair, pltpu.roll(acs_pair, HEAD_DIM, 1))
    seg = a_l - acs_row
    dm = jnp.where(causal[0], jnp.exp(jnp.minimum(seg, 0.0)), 0.0)
    dmt = jnp.where(causal[1], jnp.exp(jnp.minimum(-seg, 0.0)), 0.0)
    return dm, dmt


def _ssd_specs(index_of_chunk):
    wide = pl.BlockSpec((CHUNK, D_INNER), lambda c: (index_of_chunk(c), 0))
    b_spec = pl.BlockSpec((CHUNK, D_BC), lambda c: (index_of_chunk(c), D_INNER // D_BC))
    c_spec = pl.BlockSpec((CHUNK, D_BC), lambda c: (index_of_chunk(c), D_INNER // D_BC + 1))
    rows_spec = pl.BlockSpec((None, SSM_HEADS, CHUNK), lambda c: (index_of_chunk(c), 0, 0))
    state_spec = pl.BlockSpec((N_GROUPS, None, D_STATE, GROUP_W), lambda c: (0, index_of_chunk(c), 0, 0))
    return wide, b_spec, c_spec, rows_spec, state_spec


def _ssd_fwd(name, xbc, dt_exp, acs_exp, acs_rows, dskexp, steps=()):
    rows = xbc.shape[0]
    nc = rows // CHUNK

    def body(x_ref, b_ref, c_ref, dte_ref, acs_ref, acst_ref, dsk_ref, y_ref, st_ref, s_scr):
        @pl.when(pl.program_id(0) == 0)
        def _():
            s_scr[...] = jnp.zeros_like(s_scr)

        lane = lax.broadcasted_iota(jnp.int32, (CHUNK, LANES), 1)
        for g in range(N_GROUPS):
            y_g = _group_cols(y_ref, g, GROUP_W)
            causal, x, _, acs_exp_v, tot_exp, xdt, e_exp, f_exp, bm, cm = _ssd_common(
                _group_cols(x_ref, g, GROUP_W), _group_cols(b_ref, g, D_STATE), _group_cols(c_ref, g, D_STATE),
                _group_cols(dte_ref, g, GROUP_W), _group_cols(acs_ref, g, GROUP_W))
            state = s_scr[g]
            st_ref[g] = state
            cb16, bb16 = cm.astype(BF16), bm.astype(BF16)
            cb = _dot_nt(cb16, bb16)
            base = e_exp * _dot(cb16, state.astype(BF16)) + _group_cols(dsk_ref, g, GROUP_W)[...] * x
            for p in range(HEADS_PER_GROUP // 2):
                sl = slice(p * LANES, (p + 1) * LANES)
                xp = xdt[:, sl].astype(BF16)
                yd = []
                for e in range(2):
                    acs_row = acst_ref[pl.ds(g * HEADS_PER_GROUP + 2 * p + e, 1), :]
                    dm, _ = _pair_decay(acs_exp_v[:, sl], acs_row, e, causal)
                    yd.append(_dot((cb * dm).astype(BF16), xp))
                y_g[:, sl] = jnp.where(lane < HEAD_DIM, yd[0], yd[1]) + base[:, sl]
            s_scr[g] = jnp.exp(tot_exp) * state + _dot_tn(bb16, (f_exp * xdt).astype(BF16))

    wide, b_spec, c_spec, rows_spec, state_spec = _ssd_specs(lambda c: c)
    return _call(
        body, name=name,
        out_shape=(jax.ShapeDtypeStruct((rows, D_INNER), F32),
                   jax.ShapeDtypeStruct((N_GROUPS, nc, D_STATE, GROUP_W), F32)),
        grid=(nc,),
        in_specs=[wide, b_spec, c_spec, wide, wide, rows_spec, pl.BlockSpec((1, D_INNER), lambda c: (0, 0))],
        out_specs=(wide, state_spec),
        scratch_shapes=[pltpu.VMEM((N_GROUPS, D_STATE, GROUP_W), F32)],
        operands=[xbc, xbc, xbc, dt_exp, acs_exp, acs_rows, dskexp], semantics=("arbitrary",), steps=steps)


def _ssd_bwd(name, xbc, dt_exp, acs_exp, acs_rows, dt, a128, dskexp, dy, states, steps=()):
    rows = xbc.shape[0]
    nc = rows // CHUNK
    last = nc - 1

    def body(x_ref, b_ref, c_ref, dte_ref, acs_ref, acst_ref, dt_ref, a128_ref, dsk_all, dy_all, st_all,
             dx_all, db_all, dc_all, ddt_ref, dalog_ref, ddsk_ref, ds_all):
        @pl.when(pl.program_id(0) == 0)
        def _():
            ds_all[...] = jnp.zeros_like(ds_all)
            dalog_ref[...] = jnp.zeros_like(dalog_ref)
            ddsk_ref[...] = jnp.zeros_like(ddsk_ref)

        dacs = jnp.zeros((CHUNK, LANES), F32)
        ddt_x = jnp.zeros((CHUNK, LANES), F32)
        for g in range(N_GROUPS):
            dacs_g, ddt_x_g = group(
                g, _group_cols(x_ref, g, GROUP_W), _group_cols(b_ref, g, D_STATE), _group_cols(c_ref, g, D_STATE),
                _group_cols(dte_ref, g, GROUP_W), _group_cols(acs_ref, g, GROUP_W), acst_ref,
                _group_cols(dsk_all, g, GROUP_W), _group_cols(dy_all, g, GROUP_W), st_all.at[g],
                _group_cols(dx_all, g, GROUP_W), _group_cols(db_all, g, D_STATE), _group_cols(dc_all, g, D_STATE),
                ddsk_ref, ds_all.at[g])
            dacs, ddt_x = dacs + dacs_g, ddt_x + ddt_x_g
        _, causal_t = _causal_masks()
        da = _dot3_data_rhs(causal_t, dacs)
        ddt_ref[...] = da * a128_ref[...] + ddt_x
        dalog_ref[...] += jnp.sum(da * dt_ref[...], axis=0, keepdims=True) * a128_ref[...]

    def group(g, x_ref, b_ref, c_ref, dte_ref, acs_ref, acst_ref, dsk_ref, dy_ref, st_ref,
              dx_ref, db_ref, dc_ref, ddsk_ref, ds_scr):
        causal, x, dt_exp, acs_exp_v, tot_exp, xdt, e_exp, f_exp, bm, cm = _ssd_common(
            x_ref, b_ref, c_ref, dte_ref, acs_ref)
        reduce_heads = _reduce_heads_matrix(g)
        state, dstate = st_ref[...], ds_scr[...]
        dyv = dy_ref[...]
        cb16, bb16 = cm.astype(BF16), bm.astype(BF16)
        s16, ds16 = state.astype(BF16), dstate.astype(BF16)
        cb = _dot_nt(cb16, bb16)
        cbt = _dot_nt(bb16, cb16)
        cs = _dot(cb16, s16)
        bds = _dot(bb16, ds16)
        edy = e_exp * dyv
        fx = f_exp * xdt
        dxdt_base = f_exp * bds
        dc_acc = _dot_nt(edy.astype(BF16), s16)
        db_acc = _dot_nt(fx.astype(BF16), ds16)
        ds_scr[...] = jnp.exp(tot_exp) * dstate + _dot_tn(cb16, edy.astype(BF16))
        q = fx * bds
        dacs = _dot2_data_lhs(edy * cs - q, reduce_heads)
        dtot = jnp.sum(_dot2_data_lhs(q + jnp.exp(tot_exp) * dstate * state, reduce_heads), axis=0, keepdims=True)
        ddsk_ref[...] += jnp.sum(_dot2_data_lhs(dyv * x, reduce_heads), axis=0, keepdims=True)
        lane = lax.broadcasted_iota(jnp.int32, (CHUNK, LANES), 1)
        dcb = jnp.zeros((CHUNK, CHUNK), F32)
        dcbt = jnp.zeros((CHUNK, CHUNK), F32)
        ddt_x = jnp.zeros((CHUNK, LANES), F32)
        for p in range(HEADS_PER_GROUP // 2):
            sl = slice(p * LANES, (p + 1) * LANES)
            xp, dyp = xdt[:, sl], dyv[:, sl]
            xp16, dyp16 = xp.astype(BF16), dyp.astype(BF16)
            dxh = []
            for e in range(2):
                h = 2 * p + e
                mine = (lane < HEAD_DIM) if e == 0 else (lane >= HEAD_DIM)
                acs_row = acst_ref[pl.ds(g * HEADS_PER_GROUP + h, 1), :]
                dm, dmt = _pair_decay(acs_exp_v[:, sl], acs_row, e, causal)
                m, mt = cb * dm, cbt * dmt
                xh16 = jnp.where(mine, xp, 0.0).astype(BF16)
                dyh16 = jnp.where(mine, dyp, 0.0).astype(BF16)
                d_m = _dot_nt(dyh16, xp16)
                d_mt = _dot_nt(xh16, dyp16)
                dacs_h = (jnp.sum(d_m * m, axis=-1, keepdims=True)
                          - jnp.sum(d_mt * mt, axis=-1, keepdims=True))
                dacs = dacs + jnp.where(lane == HEADS_PER_GROUP * g + h, dacs_h, 0.0)
                dcb = dcb + d_m * dm
                dcbt = dcbt + d_mt * dmt
                dxh.append(_dot(mt.astype(BF16), dyp16))
            dxdt = jnp.where(lane < HEAD_DIM, dxh[0], dxh[1]) + dxdt_base[:, sl]
            dx_ref[:, sl] = dxdt * dt_exp[:, sl] + dsk_ref[:, sl] * dyp
            ddt_x = ddt_x + _dot2_data_lhs(dxdt * x[:, sl], _reduce_pair_matrix(g, p))
        dc_ref[...] = dc_acc + _dot(dcb.astype(BF16), bb16)
        db_ref[...] = db_acc + _dot(dcbt.astype(BF16), cb16)
        row = lax.broadcasted_iota(jnp.int32, (CHUNK, LANES), 0)
        return dacs + jnp.where(row == CHUNK - 1, dtot, 0.0), ddt_x

    wide, b_spec, c_spec, rows_spec, state_spec = _ssd_specs(lambda c: last - c)
    heads_spec = pl.BlockSpec((CHUNK, LANES), lambda c: (last - c, 0))
    vec_spec = pl.BlockSpec((1, LANES), lambda c: (0, 0))
    bc_out = pl.BlockSpec((CHUNK, D_BC), lambda c: (last - c, 0))
    vec_shape = jax.ShapeDtypeStruct((1, LANES), F32)
    return _call(
        body, name=name,
        out_shape=(jax.ShapeDtypeStruct((rows, D_INNER), F32), jax.ShapeDtypeStruct((rows, D_BC), F32),
                   jax.ShapeDtypeStruct((rows, D_BC), F32), jax.ShapeDtypeStruct((rows, LANES), F32),
                   vec_shape, vec_shape),
        grid=(nc,),
        in_specs=[wide, b_spec, c_spec, wide, wide, rows_spec, heads_spec, vec_spec,
                  pl.BlockSpec((1, D_INNER), lambda c: (0, 0)), wide, state_spec],
        out_specs=(wide, bc_out, bc_out, heads_spec, vec_spec, vec_spec),
        scratch_shapes=[pltpu.VMEM((N_GROUPS, D_STATE, GROUP_W), F32)],
        operands=[xbc, xbc, xbc, dt_exp, acs_exp, acs_rows, dt, a128, dskexp, dy, states],
        semantics=("arbitrary",), steps=steps)


def _attn_visible(b, heads=1):
    row = jnp.bitwise_and(lax.broadcasted_iota(jnp.int32, (heads * CHUNK, 3 * CHUNK), 0), CHUNK - 1)
    col = lax.broadcasted_iota(jnp.int32, (heads * CHUNK, 3 * CHUNK), 1)
    bb = b + jnp.zeros_like(col)
    meta = (col < CHUNK) & (bb >= 1) & (col >= PAD_ROWS)
    prev = (col >= CHUNK) & (col < 2 * CHUNK) & (bb >= 2) & ((col - CHUNK) > row)
    cur = (col >= 2 * CHUNK) & ((col - 2 * CHUNK) <= row) & ((bb >= 1) | ((col - 2 * CHUNK) >= PAD_ROWS))
    return meta | prev | cur


def _attn_visible4(b):
    return _attn_visible(b, 4)


def _stack_heads(q_ref, sink_ref, kvh, scale):
    lane = lax.broadcasted_iota(jnp.int32, (CHUNK, LANES), 1)
    parts, sinks = [], []
    for pp in range(2):
        pair = kvh * 2 + pp
        qp = q_ref[:, pair * LANES:(pair + 1) * LANES] * scale
        for e in range(2):
            mine = (lane < HEAD_DIM) if e == 0 else (lane >= HEAD_DIM)
            parts.append(jnp.where(mine, qp, 0.0).astype(BF16))
            sinks.append(jnp.full((CHUNK, 1), sink_ref[2 * pair + e], F32))
    return jnp.concatenate(parts, axis=0), jnp.concatenate(sinks, axis=0)


def _attn_operands(q_ref, k0, kp, kc, v0, vp, vc, sink_ref):
    kcat, vcat, q4, sink4 = [], [], [], []
    for kvh in range(N_KV_HEADS):
        ksl = slice(kvh * LANES, (kvh + 1) * LANES)
        kcat.append(jnp.concatenate([k0[:, ksl], kp[:, ksl], kc[:, ksl]], axis=0).astype(BF16))
        vcat.append(jnp.concatenate([v0[:, ksl], vp[:, ksl], vc[:, ksl]], axis=0).astype(BF16))
        stacked, sinks = _stack_heads(q_ref, sink_ref, kvh, ATTN_SCALE)
        q4.append(stacked)
        sink4.append(sinks)
    return kcat, vcat, q4, sink4


def _attn_probs(q4, kcat, visible, sink4):
    heads = range(N_KV_HEADS)
    s = [jnp.where(visible, _dot_nt(q4[h], kcat[h]), NEG_INF) for h in heads]
    m = [jnp.maximum(jnp.max(s[h], axis=-1, keepdims=True), sink4[h]) for h in heads]
    pe = [jnp.exp(s[h] - m[h]) for h in heads]
    pe_sink = [jnp.exp(sink4[h] - m[h]) for h in heads]
    inv = [1.0 / (jnp.sum(pe[h], axis=-1, keepdims=True) + pe_sink[h]) for h in heads]
    return [pe[h] * inv[h] for h in heads], [pe_sink[h] * inv[h] for h in heads]


def _unstack_pairs(stacked, pp):
    lane = lax.broadcasted_iota(jnp.int32, (CHUNK, LANES), 1)
    return jnp.where(lane < HEAD_DIM, stacked[(2 * pp) * CHUNK:(2 * pp + 1) * CHUNK],
                     stacked[(2 * pp + 1) * CHUNK:(2 * pp + 2) * CHUNK])


def _attn_specs(colblock):
    blk = lambda f: pl.BlockSpec((CHUNK, 2 * D_KV), f)
    return [blk(lambda b: (0, colblock)), blk(lambda b: (jnp.maximum(b - 1, 0), colblock)), blk(lambda b: (b, colblock))]


def _attn_fwd(name, q, kv2, sinks, steps=()):
    rows = q.shape[0]

    def body(q_ref, k0, kp, kc, v0, vp, vc, sink_ref, o_ref):
        visible = _attn_visible4(pl.program_id(0))
        kcat, vcat, q4, sink4 = _attn_operands(q_ref, k0, kp, kc, v0, vp, vc, sink_ref)
        pn, _ = _attn_probs(q4, kcat, visible, sink4)
        o4 = [_dot(pn[h].astype(BF16), vcat[h]) for h in range(N_KV_HEADS)]
        for kvh in range(N_KV_HEADS):
            for pp in range(2):
                qsl = slice((kvh * 2 + pp) * LANES, (kvh * 2 + pp + 1) * LANES)
                o_ref[:, qsl] = _unstack_pairs(o4[kvh], pp).astype(BF16)

    return _call(
        body, name=name, out_shape=jax.ShapeDtypeStruct((rows, D_MODEL), BF16), grid=(rows // CHUNK,),
        in_specs=[pl.BlockSpec((CHUNK, D_MODEL), lambda b: (b, 0))] + _attn_specs(0) + _attn_specs(1)
        + [pl.BlockSpec(memory_space=pltpu.SMEM)],
        out_specs=pl.BlockSpec((CHUNK, D_MODEL), lambda b: (b, 0)),
        operands=[q, kv2, kv2, kv2, kv2, kv2, kv2, sinks], semantics=("parallel",), steps=steps)


def _attn_bwd(name, q, kv2, sinks, do, steps=()):
    rows = q.shape[0]

    def body(q_ref, k0, kp, kc, v0, vp, vc, sink_ref, do_ref,
             dq_ref, dkc_ref, dkp_ref, dvc_ref, dvp_ref, dkm_ref, dvm_ref, dsink_ref):
        @pl.when(pl.program_id(0) == 0)
        def _():
            dkm_ref[...] = jnp.zeros_like(dkm_ref)
            dvm_ref[...] = jnp.zeros_like(dvm_ref)
            dsink_ref[...] = jnp.zeros_like(dsink_ref)

        visible = _attn_visible4(pl.program_id(0))
        heads = range(N_KV_HEADS)
        lane1 = lax.broadcasted_iota(jnp.int32, (1, LANES), 1)
        kcat, vcat, q4, sink4 = _attn_operands(q_ref, k0, kp, kc, v0, vp, vc, sink_ref)
        do4 = [_stack_heads(do_ref, sink_ref, h, 1.0)[0] for h in heads]
        pn, psink = _attn_probs(q4, kcat, visible, sink4)
        dp = [_dot_nt(do4[h], vcat[h]) for h in heads]
        delta = [jnp.sum(pn[h] * dp[h], axis=-1, keepdims=True) for h in heads]
        ds16 = [(pn[h] * (dp[h] - delta[h])).astype(BF16) for h in heads]
        dq4 = [_dot(ds16[h], kcat[h]) for h in heads]
        dk_acc = [_dot_tn(ds16[h], q4[h]) for h in heads]
        dv_acc = [_dot_tn(pn[h].astype(BF16), do4[h]) for h in heads]
        dsink = jnp.zeros((1, LANES), F32)
        for kvh in heads:
            ksl = slice(kvh * LANES, (kvh + 1) * LANES)
            sink_terms = psink[kvh] * delta[kvh]
            for j in range(4):
                part = jnp.sum(sink_terms[j * CHUNK:(j + 1) * CHUNK], axis=0, keepdims=True)
                dsink = dsink - jnp.where(lane1 == kvh * 4 + j, part, 0.0)
            for pp in range(2):
                qsl = slice((kvh * 2 + pp) * LANES, (kvh * 2 + pp + 1) * LANES)
                dq_ref[:, qsl] = (_unstack_pairs(dq4[kvh], pp) * ATTN_SCALE).astype(BF16)
            dkm_ref[:, ksl] += dk_acc[kvh][0:CHUNK]
            dvm_ref[:, ksl] += dv_acc[kvh][0:CHUNK]
            dkp_ref[:, ksl] = dk_acc[kvh][CHUNK:2 * CHUNK]
            dvp_ref[:, ksl] = dv_acc[kvh][CHUNK:2 * CHUNK]
            dkc_ref[:, ksl] = dk_acc[kvh][2 * CHUNK:3 * CHUNK]
            dvc_ref[:, ksl] = dv_acc[kvh][2 * CHUNK:3 * CHUNK]
        dsink_ref[...] += dsink

    qspec = pl.BlockSpec((CHUNK, D_MODEL), lambda b: (b, 0))
    kvspec = pl.BlockSpec((CHUNK, 2 * D_KV), lambda b: (b, 0))
    fixed = pl.BlockSpec((CHUNK, 2 * D_KV), lambda b: (0, 0))
    kv_shape = jax.ShapeDtypeStruct((rows, 2 * D_KV), F32)
    meta_shape = jax.ShapeDtypeStruct((CHUNK, 2 * D_KV), F32)
    return _call(
        body, name=name,
        out_shape=(jax.ShapeDtypeStruct((rows, D_MODEL), BF16), kv_shape, kv_shape, kv_shape, kv_shape,
                   meta_shape, meta_shape, jax.ShapeDtypeStruct((1, LANES), F32)),
        grid=(rows // CHUNK,),
        in_specs=[qspec] + _attn_specs(0) + _attn_specs(1) + [pl.BlockSpec(memory_space=pltpu.SMEM), qspec],
        out_specs=(qspec, kvspec, kvspec, kvspec, kvspec, fixed, fixed, pl.BlockSpec((1, LANES), lambda b: (0, 0))),
        operands=[q, kv2, kv2, kv2, kv2, kv2, kv2, sinks, do], semantics=("arbitrary",), steps=steps)


def _kv_grad_combine(name, dk_cur, dk_prev, dk_meta, dv_cur, dv_prev, dv_meta):
    rows = dk_cur.shape[0]
    nb = rows // CHUNK
    width = 2 * D_KV

    def body(kc_ref, kp_ref, km_ref, vc_ref, vp_ref, vm_ref, o_ref):
        jj = pl.program_id(0) + jnp.zeros((CHUNK, 1), jnp.int32)
        for half, (c_ref, p_ref, m_ref) in enumerate(((kc_ref, kp_ref, km_ref), (vc_ref, vp_ref, vm_ref))):
            total = c_ref[...] + jnp.where(jj < nb - 1, p_ref[...], 0.0) + jnp.where(jj == 0, m_ref[...], 0.0)
            o_ref[:, half * width:(half + 1) * width] = total.astype(BF16)

    blk = lambda f: pl.BlockSpec((CHUNK, width), f)
    three = lambda: [blk(lambda j: (j, 0)), blk(lambda j: (jnp.minimum(j + 1, nb - 1), 0)), blk(lambda j: (0, 0))]
    return pl.pallas_call(
        body, name=name, out_shape=jax.ShapeDtypeStruct((rows, 2 * width), BF16), grid=(nb,),
        in_specs=three() + three(), out_specs=pl.BlockSpec((CHUNK, 2 * width), lambda j: (j, 0)),
        compiler_params=_cparams(("parallel",)),
    )(dk_cur, dk_prev, dk_meta, dv_cur, dv_prev, dv_meta)


def _adamw(name, w, g, m, v, steps=()):
    rows, width = w.shape
    tr = rows
    for cand in range(8, rows + 1, 8):
        if rows % cand == 0 and cand * width * 4 <= (1 << 20):
            tr = cand

    def body(*refs):
        _adamw_update(*refs)

    blk = pl.BlockSpec((tr, width), lambda i: (i, 0))
    shp = jax.ShapeDtypeStruct((rows, width), F32)
    return _call(body, name=name, out_shape=(shp, shp, shp), grid=(rows // tr,), in_specs=[blk] * 4,
                 out_specs=(blk,) * 3, operands=[w, g, m, v], semantics=("parallel",), steps=steps)


def _adamw_update(w_ref, g_ref, m_ref, v_ref, d_ref, mo_ref, vo_ref):
    gv = g_ref[...]
    mn = ADAM_B1 * m_ref[...] + (1.0 - ADAM_B1) * gv
    vn = ADAM_B2 * v_ref[...] + (1.0 - ADAM_B2) * (gv * gv)
    m_hat = mn / (1.0 - ADAM_B1 ** ADAM_STEP)
    v_hat = vn / (1.0 - ADAM_B2 ** ADAM_STEP)
    d_ref[...] = -ADAM_LR * (m_hat / (jnp.sqrt(v_hat) + ADAM_EPS) + ADAM_WD * w_ref[...])
    mo_ref[...] = mn
    vo_ref[...] = vn


def _adamw_small(name, ws, gs, ms, vs):
    n = len(ws)

    def body(*refs):
        for i in range(n):
            _adamw_update(*refs[i::n])

    shapes = [jax.ShapeDtypeStruct(a.shape, F32) for a in ws]
    outs = pl.pallas_call(body, name=name, out_shape=shapes * 3, in_specs=[VMEM_SPEC] * (4 * n),
                          out_specs=[VMEM_SPEC] * (3 * n), compiler_params=_cparams())(*ws, *gs, *ms, *vs)
    return outs[:n], outs[n:2 * n], outs[2 * n:]


def _ffn_fwd(tag, h, hn, p, i, plan):
    up_g, up_v, act = _ffn_up_conv(f"ffn{tag}_up", hn, plan.weight("f_w_up", i), p["f_conv_w"][i],
                                   p["f_conv_b"][i:i + 1], steps=plan.steps(f"ffn{tag}_up"))
    pre = _mm(f"ffn{tag}_down", act, plan.weight("f_w_down", i), "nn", steps=plan.steps(f"ffn{tag}_down"))
    return pre, (h, hn, up_g, up_v, act, pre)


def _ffn_bwd(tag, dpre, saved, p, i, plan):
    h, hn, up_g, up_v, act, pre = saved
    plan.grad("f_w_down", i, _mm(f"ffn{tag}_down_dw", act, dpre, "tn", out_dtype=BF16))
    dact = _mm(f"ffn{tag}_down_dx", dpre, plan.weight("f_w_down", i), "nt", steps=plan.steps(f"ffn{tag}_down_dx"))
    gwg, gwv, gbg, gbv, dhn, g_up = _ffn_conv_bwd(
        f"ffn{tag}_conv_bwd", up_g, up_v, dact, p["f_conv_w"][i], p["f_conv_b"][i:i + 1], hn,
        plan.weight("f_w_up", i), steps=plan.steps(f"ffn{tag}_conv_bwd"))
    g_cw, g_cb = jnp.concatenate([gwg, gwv], axis=1), jnp.concatenate([gbg, gbv], axis=1)
    plan.grad("f_w_up", i, g_up)
    return dhn, dict(f_conv_w=g_cw, f_conv_b=g_cb)


def _lanes_pad(a, width=LANES):
    return jnp.pad(a, [(0, 0)] * (a.ndim - 1) + [(0, width - a.shape[-1])])


def _dup_heads(w):
    rows = w.shape[0]
    w = w.reshape(rows, 2 * N_KV_HEADS, 1, HEAD_DIM)
    return jnp.broadcast_to(w, (rows, 2 * N_KV_HEADS, 2, HEAD_DIM)).reshape(rows, 4 * D_KV)


def _undup_heads(g):
    rows = g.shape[0]
    return g.reshape(rows, 2 * N_KV_HEADS, 2, HEAD_DIM).sum(axis=2).reshape(rows, 2 * D_KV)


def _local_step(x2, target, p, plan):
    seq = x2.shape[0]
    rows = seq + CHUNK
    g = {}

    h0 = jnp.concatenate([jnp.zeros((PAD_ROWS, D_MODEL), F32), p["meta_tokens"], x2], axis=0)

    w_in = plan.weight("a_w_in")
    w_dt = jnp.pad(w_in[D_MAIN:], ((0, LANES - SSM_HEADS), (0, 0)))
    dt_bias = _lanes_pad(p["a_dt_bias"])
    a128 = _lanes_pad(-jnp.exp(p["a_a_log"]))
    dskexp = jnp.repeat(p["a_d_skip"].reshape(SSM_HEADS), HEAD_DIM).reshape(1, D_INNER)

    hn0 = _rms_fwd("a_norm", h0, p["a_norm_pre"])
    zx = _mm("a_in_main", hn0, w_in, "nt", k_rows=D_MAIN, steps=plan.steps("a_in_main"))
    dtr = _mm("a_in_dt", hn0, w_dt, "nt")
    xbc = _conv4_fwd("a_conv", zx, p["a_conv_w"], p["a_conv_b"], steps=plan.steps("a_conv"))
    dt = _dt_fwd("a_dt", dtr, dt_bias)
    dt_exp, acs_exp, acs_rows = _ssd_prep("a_ssd_prep", dt, a128, steps=plan.steps("a_ssd_prep"))
    y, states = _ssd_fwd("a_ssd", xbc, dt_exp, acs_exp, acs_rows, dskexp, steps=plan.steps("a_ssd"))
    yn = _gate_fwd("a_gate", y, zx, p["a_gate_norm"], steps=plan.steps("a_gate"))
    mix = _mm("a_out", yn, plan.weight("a_w_out"), "nn", steps=plan.steps("a_out"))
    h1, (hn_f0,) = _resid_norm_fwd("a_resid", h0, mix, p["a_norm_post"], [p["f_norm_pre"][0:1]])

    pre_f0, ffn0 = _ffn_fwd("0", h1, hn_f0, p, 0, plan)
    h2, (hkv, hn2) = _resid_norm_fwd("ffn0_resid", h1, pre_f0, p["f_norm_post"][0:1], [p["kv_norm"], p["b_norm_pre"]])

    w_kv2 = _dup_heads(plan.weight("w_kv"))
    kv2 = _mm("kv_proj", hkv, w_kv2, "nn")
    q = _mm("b_q", hn2, plan.weight("b_w_q"), "nn")
    sinks = p["b_sinks"].reshape(N_Q_HEADS)
    o = _attn_fwd("b_attn", q, kv2, sinks, steps=plan.steps("b_attn"))
    attn = _mm("b_o", o, plan.weight("b_w_o"), "nn", steps=plan.steps("b_o"))
    h3, (hn_f1,) = _resid_norm_fwd("b_resid", h2, attn, p["b_norm_post"], [p["f_norm_pre"][1:2]])

    pre_f1, ffn1 = _ffn_fwd("1", h3, hn_f1, p, 1, plan)
    dh, loss_vec, dpre_f1, g_post1 = _resid_norm_loss("ffn1_resid_loss", h3, pre_f1, p["f_norm_post"][1:2], target)
    loss = loss_vec[0, 0]

    dhn_f1, g1 = _ffn_bwd("1", dpre_f1, ffn1, p, 1, plan)
    dh, g_pre1, dpre, g["b_norm_post"] = _norm_bwd_add("ffn1_norm_bwd", dh, dhn_f1, h3, p["f_norm_pre"][1:2],
                                                        then=(attn, p["b_norm_post"]))
    plan.grad("b_w_o", None, _mm("b_o_dw", o, dpre, "tn", out_dtype=BF16))
    do = _mm("b_o_dx", dpre, plan.weight("b_w_o"), "nt", steps=plan.steps("b_o_dx"))
    dq, dkc, dkp, dvc, dvp, dkm, dvm, dsink = _attn_bwd("b_attn_bwd", q, kv2, sinks, do, steps=plan.steps("b_attn_bwd"))
    g["b_sinks"] = dsink[:, :N_Q_HEADS]
    dhn2 = _mm("b_q_dx", dq, plan.weight("b_w_q"), "nt")
    plan.grad("b_w_q", None, _mm("b_q_dw", hn2, dq, "tn", out_dtype=BF16))
    dh, g["b_norm_pre"] = _norm_bwd_add("b_norm_bwd", dh, dhn2, h2, p["b_norm_pre"])
    dkv2 = _kv_grad_combine("kv_grad", dkc, dkp, dkm, dvc, dvp, dvm)
    dhkv = _mm("kv_proj_dx", dkv2, w_kv2, "nt")
    plan.grad("w_kv", None, _undup_heads(_mm("kv_proj_dw", hkv, dkv2, "tn")))
    dh, g["kv_norm"], dpre_f0, g_post0 = _norm_bwd_add("kv_norm_bwd", dh, dhkv, h2, p["kv_norm"],
                                                       then=(pre_f0, p["f_norm_post"][0:1]))

    dhn_f0, g0 = _ffn_bwd("0", dpre_f0, ffn0, p, 0, plan)
    dh, g_pre0, dpre, g["a_norm_post"] = _norm_bwd_add("ffn0_norm_bwd", dh, dhn_f0, h1, p["f_norm_pre"][0:1],
                                                        then=(mix, p["a_norm_post"]))
    g["f_norm_post"] = jnp.concatenate([g_post0, g_post1], axis=0)
    g["f_norm_pre"] = jnp.concatenate([g_pre0, g_pre1], axis=0)
    g["f_conv_w"] = jnp.stack([g0["f_conv_w"], g1["f_conv_w"]])
    g["f_conv_b"] = jnp.concatenate([g0["f_conv_b"], g1["f_conv_b"]], axis=0)
    plan.grad("a_w_out", None, _mm("a_out_dw", yn, dpre, "tn", out_dtype=BF16))
    dyn = _mm("a_out_dx", dpre, plan.weight("a_w_out"), "nt", steps=plan.steps("a_out_dx"))
    dy, dz, g["a_gate_norm"] = _gate_bwd("a_gate_bwd", dyn, y, zx, p["a_gate_norm"])
    dxs, dbm, dcm, ddt, dalog, ddsk = _ssd_bwd("a_ssd_bwd", xbc, dt_exp, acs_exp, acs_rows, dt, a128, dskexp, dy, states,
                                              steps=plan.steps("a_ssd_bwd"))
    g["a_a_log"] = dalog[:, :SSM_HEADS]
    g["a_d_skip"] = ddsk[:, :SSM_HEADS]
    ddtr, dbias = _dt_bwd("a_dt_bwd", ddt, dtr, dt_bias)
    g["a_dt_bias"] = dbias[:, :SSM_HEADS]
    dxp, gw_x, gb_x = _conv4_bwd("a_conv_bwd_x", zx, dxs, p["a_conv_w"], p["a_conv_b"], 0)
    dbp, gw_b, gb_b = _conv4_bwd("a_conv_bwd_b", zx, dbm, p["a_conv_w"], p["a_conv_b"], D_INNER)
    dcp, gw_c, gb_c = _conv4_bwd("a_conv_bwd_c", zx, dcm, p["a_conv_w"], p["a_conv_b"], D_INNER + D_BC)
    g["a_conv_w"] = jnp.concatenate([gw_x, gw_b, gw_c], axis=1)
    g["a_conv_b"] = jnp.concatenate([gb_x, gb_b, gb_c], axis=1)
    dzx = jnp.concatenate([dz, dxp, dbp, dcp], axis=1)
    g_in = _mm("a_in_main_dw", dzx, hn0, "tn", out_dtype=BF16, out_rows=D_IN_PROJ, steps=plan.steps("a_in_main_dw"))
    plan.grad("a_w_in", None, _tn_rows_into("a_in_dt_dw", ddtr, hn0, g_in, D_MAIN, SSM_HEADS))
    dhn0 = _mm("a_in_dt_dx", ddtr, w_dt, "nn", steps=plan.steps("a_in_dt_dx"))
    dhn0 = _mm("a_in_main_dx", dzx, w_in, "nn", acc=dhn0, steps=plan.steps("a_in_main_dx"))
    dh, g["a_norm_pre"] = _norm_bwd_add("a_norm_bwd", dh, dhn0, h0, p["a_norm_pre"], steps=plan.steps("a_norm_bwd"))

    g["meta_tokens"] = dh[PAD_ROWS:CHUNK]
    return loss, dh[CHUNK:], g


ANY = pl.BlockSpec(memory_space=pl.ANY)
VMEM_SPEC = pl.BlockSpec(memory_space=pltpu.VMEM)


def _allgather_small(name, shard):
    rows = shard.shape[0]

    def body(s_ref, o_ref, send_sems, recv_sems):
        x, y, c = _place()
        me = 2 * x + y
        o_ref[me] = s_ref[...]
        chips = _other_chips(x, y)
        sends = [pltpu.make_async_remote_copy(s_ref, o_ref.at[me], send_sems.at[j], recv_sems.at[j],
                                              device_id=(cx, cy, c), device_id_type=MESH)
                 for j, (cx, cy) in enumerate(chips)]
        for cp in sends:
            cp.start()
        for j, (cx, cy) in enumerate(chips):
            pltpu.make_async_remote_copy(s_ref, o_ref.at[2 * cx + cy], send_sems.at[j], recv_sems.at[j],
                                         device_id=(cx, cy, c), device_id_type=MESH).wait_recv()
        for cp in sends:
            cp.wait_send()

    return pl.pallas_call(
        body, name=name, out_shape=jax.ShapeDtypeStruct((N_CHIPS, rows, LANES), F32),
        in_specs=[VMEM_SPEC], out_specs=VMEM_SPEC,
        scratch_shapes=[pltpu.SemaphoreType.DMA((3,)), pltpu.SemaphoreType.DMA((3,))],
        compiler_params=pltpu.CompilerParams(vmem_limit_bytes=VMEM_LIMIT),
    )(shard)


def _row_block(rows, width, itemsize, align, budget=2 << 20):
    best = rows
    for cand in range(align, rows + 1, align):
        if rows % cand == 0 and cand * width * itemsize <= budget:
            best = cand
    return best


def _cast_into_slot(name, chip, w, layer=None):
    rows, width = w.shape[-2:]
    tr = _row_block(rows, width, 4, 16)
    if layer is None:
        in_spec = pl.BlockSpec((tr, width), lambda i, chip_ref: (i, 0))
    else:
        in_spec = pl.BlockSpec((None, tr, width), lambda i, chip_ref: (layer, i, 0))

    def body(chip_ref, w_ref, o_ref):
        o_ref[...] = w_ref[...].astype(BF16)

    return pl.pallas_call(
        body, name=name, out_shape=jax.ShapeDtypeStruct((N_CHIPS, rows, width), BF16),
        grid_spec=pltpu.PrefetchScalarGridSpec(
            num_scalar_prefetch=1, grid=(rows // tr,), in_specs=[in_spec],
            out_specs=pl.BlockSpec((None, tr, width), lambda i, chip_ref: (chip_ref[0], i, 0))),
        compiler_params=_cparams(("parallel",)),
    )(chip, w)


def _allreduce_small(name, vec):
    rows = -(-vec.shape[0] // (2 * SUBLANES)) * (2 * SUBLANES)
    hr = rows // 2
    padded = jnp.pad(vec, ((0, rows - vec.shape[0]), (0, 0)))

    def body(v_ref, o_ref, theirs, pair, by_chip, send_sems, recv_sems):
        x, y, c = _place()
        me = 2 * x + y
        sibling = (x, y, 1 - c)
        mine = pl.ds(pl.multiple_of(c * hr, SUBLANES), hr)
        other = pl.ds(pl.multiple_of((1 - c) * hr, SUBLANES), hr)

        swap = _remote(v_ref, theirs, send_sems, recv_sems, 0, sibling)
        swap.start()
        swap.wait()
        south = (c + jnp.zeros((1, 1), jnp.int32)) == 0
        pair[...] = jnp.where(south, v_ref[...], theirs[...]) + jnp.where(south, theirs[...], v_ref[...])

        by_chip[me] = pair[mine, :]
        sends = [_remote(by_chip.at[me], by_chip.at[me], send_sems, recv_sems, 1 + j, (cx, cy, c))
                 for j, (cx, cy) in enumerate(_other_chips(x, y))]
        for cp in sends:
            cp.start()
        for j, (cx, cy) in enumerate(_other_chips(x, y)):
            _remote(by_chip.at[me], by_chip.at[2 * cx + cy], send_sems, recv_sems, 1 + j, (cx, cy, c)).wait_recv()
        for cp in sends:
            cp.wait_send()
        total = by_chip[0]
        for s in range(1, N_CHIPS):
            total = total + by_chip[s]

        o_ref[mine, :] = total
        back = _remote(o_ref.at[mine], o_ref.at[mine], send_sems, recv_sems, 4, sibling)
        back.start()
        _remote(o_ref.at[other], o_ref.at[other], send_sems, recv_sems, 4, sibling).wait_recv()
        back.wait_send()

    out = pl.pallas_call(
        body, name=name, out_shape=jax.ShapeDtypeStruct((rows, LANES), F32),
        in_specs=[VMEM_SPEC], out_specs=VMEM_SPEC,
        scratch_shapes=[pltpu.VMEM((rows, LANES), F32), pltpu.VMEM((rows, LANES), F32),
                        pltpu.VMEM((N_CHIPS, hr, LANES), F32), pltpu.SemaphoreType.DMA((5,)),
                        pltpu.SemaphoreType.DMA((5,))],
        compiler_params=pltpu.CompilerParams(vmem_limit_bytes=VMEM_LIMIT),
    )(padded)
    return out[:vec.shape[0]]


def _rs_pair_add(name, place, grads, partner, split="rows"):
    _, half_rows, width = partner.shape
    tr = _row_block(half_rows, width, 2, 16)
    nb = half_rows // tr
    if split == "rows":
        mine = pl.BlockSpec((None, tr, width), lambda s, i, pr: (s, pr[1] * nb + i, 0))
    else:
        mine = pl.BlockSpec((None, tr, width), lambda s, i, pr: (s, i, pr[1]))

    def body(place_ref, g_ref, p_ref, o_ref):
        o_ref[...] = (g_ref[...].astype(F32) + p_ref[...].astype(F32)).astype(BF16)

    return pl.pallas_call(
        body, name=name, out_shape=jax.ShapeDtypeStruct(partner.shape, BF16),
        grid_spec=pltpu.PrefetchScalarGridSpec(
            num_scalar_prefetch=1, grid=(N_CHIPS, nb),
            in_specs=[mine, pl.BlockSpec((None, tr, width), lambda s, i, pr: (s, i, 0))],
            out_specs=pl.BlockSpec((None, tr, width), lambda s, i, pr: (s, i, 0))),
        compiler_params=_cparams(("parallel", "parallel")),
    )(place, grads, partner)


def _rs_chip_add(name, place, mine, others, split="rows"):
    _, half_rows, width = mine.shape
    tr = _row_block(half_rows, width, 4, 16, budget=1 << 20)
    nb = half_rows // tr
    if split == "rows":
        out_shape, out_spec = (2 * half_rows, width), pl.BlockSpec((tr, width), lambda i, pr: (pr[1] * nb + i, 0))
    else:
        out_shape, out_spec = (half_rows, 2 * width), pl.BlockSpec((tr, width), lambda i, pr: (i, pr[1]))

    def body(place_ref, q_ref, r_ref, o_ref):
        acc = q_ref[...].astype(F32)
        for j in range(3):
            acc = acc + r_ref[j].astype(F32)
        o_ref[...] = acc

    return pl.pallas_call(
        body, name=name, out_shape=jax.ShapeDtypeStruct(out_shape, F32),
        grid_spec=pltpu.PrefetchScalarGridSpec(
            num_scalar_prefetch=1, grid=(nb,),
            in_specs=[pl.BlockSpec((None, tr, width), lambda i, pr: (pr[0], i, 0)),
                      pl.BlockSpec((3, tr, width), lambda i, pr: (0, i, 0))],
            out_specs=out_spec),
        compiler_params=_cparams(("parallel",)),
    )(place, mine, others)


WEIGHTS = ["meta_tokens", "a_norm_pre", "a_w_in", "a_conv_w", "a_conv_b", "a_dt_bias", "a_a_log", "a_d_skip",
           "a_gate_norm", "a_w_out", "a_norm_post", "kv_norm", "w_kv", "b_norm_pre", "b_w_q", "b_sinks", "b_w_o",
           "b_norm_post", "f_norm_pre", "f_w_up", "f_conv_w", "f_conv_b", "f_w_down", "f_norm_post"]
FULL_SHAPE = {
    "meta_tokens": (16, 1024), "a_norm_pre": (1, 1024), "a_w_in": (1, 1024, 5152), "a_conv_w": (1, 4, 3072),
    "a_conv_b": (1, 3072), "a_dt_bias": (1, 32), "a_a_log": (1, 32), "a_d_skip": (1, 32), "a_gate_norm": (1, 2048),
    "a_w_out": (1, 2048, 1024), "a_norm_post": (1, 1024), "kv_norm": (1024,), "w_kv": (1024, 512),
    "b_norm_pre": (1, 1024), "b_w_q": (1, 1024, 1024), "b_sinks": (1, 16), "b_w_o": (1, 1024, 1024),
    "b_norm_post": (1, 1024), "f_norm_pre": (2, 1024), "f_w_up": (2, 1024, 5632), "f_conv_w": (2, 3, 5632),
    "f_conv_b": (2, 5632), "f_w_down": (2, 2816, 1024), "f_norm_post": (2, 1024),
}
SHARD_AXIS = {
    "meta_tokens": 1, "a_norm_pre": 1, "a_w_in": 2, "a_conv_w": 2, "a_conv_b": 1, "a_dt_bias": None, "a_a_log": None,
    "a_d_skip": None, "a_gate_norm": 1, "a_w_out": 1, "a_norm_post": 1, "kv_norm": None, "w_kv": 0, "b_norm_pre": None,
    "b_w_q": 1, "b_sinks": None, "b_w_o": 1, "b_norm_post": None, "f_norm_pre": None, "f_w_up": 2, "f_conv_w": 2,
    "f_conv_b": None, "f_w_down": 1, "f_norm_post": None,
}
BIG = ["a_w_in", "a_w_out", "w_kv", "b_w_q", "b_w_o", "f_w_up", "f_w_down"]
SMALL = [n for n in WEIGHTS if n not in BIG]
SMALL_SHARDED = [n for n in SMALL if SHARD_AXIS[n] is not None]


def _shard_shape(name):
    shape = list(FULL_SHAPE[name])
    if SHARD_AXIS[name] is not None:
        shape[SHARD_AXIS[name]] //= N_CHIPS
    return tuple(shape)


def _numel(shape):
    return int(math.prod(shape))


SUBLANES = 8


def _packed_rows(shape):
    rows = -(-_numel(shape) // LANES)
    return -(-rows // SUBLANES) * SUBLANES


def _pack(arrays):
    parts = []
    for a in arrays:
        size, rows = _numel(a.shape), _packed_rows(a.shape)
        if size % LANES == 0:
            part = jnp.pad(a.reshape(size // LANES, LANES), ((0, rows - size // LANES), (0, 0)))
        else:
            part = jnp.pad(a.reshape(-1), (0, rows * LANES - size)).reshape(rows, LANES)
        parts.append(part)
    return jnp.concatenate(parts, axis=0)


def _unpack(packed, names, shape_of):
    out, off = {}, 0
    lead = packed.shape[:-2]
    for n in names:
        shape = tuple(shape_of(n))
        size, rows = _numel(shape), _packed_rows(shape)
        part = packed[..., off:off + rows, :]
        if size % LANES == 0:
            out[n] = part[..., :size // LANES, :].reshape(lead + shape)
        else:
            out[n] = part.reshape(lead + (rows * LANES,))[..., :size].reshape(lead + shape)
        off += rows
    return out


def _split_chips(name, full):
    ax = SHARD_AXIS[name]
    shape = full.shape
    cut = shape[:ax] + (N_CHIPS, shape[ax] // N_CHIPS) + shape[ax + 1:]
    return jnp.moveaxis(full.reshape(cut), ax, 0)


def _join_chips(name, stacked):
    ax = SHARD_AXIS[name]
    moved = jnp.moveaxis(stacked, 0, ax)
    shape = moved.shape
    return moved.reshape(shape[:ax] + (shape[ax] * shape[ax + 1],) + shape[ax + 2:])


def _as2d(a):
    return a.reshape(-1, a.shape[-1])


BUFFERS = [("a_w_in", "a_w_in", None), ("a_w_out", "a_w_out", None), ("w_kv", "w_kv", None),
           ("b_w_q", "b_w_q", None), ("b_w_o", "b_w_o", None), ("f_w_up0", "f_w_up", 0), ("f_w_up1", "f_w_up", 1),
           ("f_w_down0", "f_w_down", 0), ("f_w_down1", "f_w_down", 1)]


TRANSPOSED = ("a_w_in",)
SPLIT = {"a_w_in": "cols"}


def _local_shard(arrays, weight, layer):
    if weight in TRANSPOSED:
        return arrays[weight][0].T
    return _as2d(arrays[weight]) if layer is None else arrays[weight]


def _weight_from_gathered(weight, buf):
    if weight == "f_w_up":
        return buf
    return buf.reshape(N_CHIPS * buf.shape[1], buf.shape[2])


def _gathered_from_grad(weight, g):
    if weight == "f_w_up":
        return g
    return g.reshape(N_CHIPS, g.shape[0] // N_CHIPS, g.shape[1]).astype(BF16)


GATHER_SCHEDULE = {
    "a_in_main": [("ici", ["a_w_out"])],
    "a_conv": [("d2d", ["a_w_out"]), ("ici", ["f_w_down0"])],
    "a_ssd_prep": [("d2d", ["f_w_down0"]), ("ici_near", ["f_w_up0"])],
    "a_ssd": [("ici_far", ["f_w_up0"])],
    "a_gate": [("d2d", ["f_w_up0"]), ("ici", ["w_kv", "b_w_q", "b_w_o"])],
    "ffn0_up": [("d2d", ["w_kv", "b_w_q", "b_w_o"]), ("ici", ["f_w_down1"])],
    "ffn0_down": [("d2d", ["f_w_down1"])],
    "b_attn": [("ici", ["f_w_up1"])],
    "b_o": [("d2d", ["f_w_up1"])],
}
REDUCE_SCHEDULE = {
    "b_attn_bwd": [("all", ["f_w_down1", "f_w_up1", "b_w_o"])],
    "ffn0_conv_bwd": [("all", ["b_w_q", "w_kv", "f_w_down0"])],
    "a_ssd_bwd": [("all", ["f_w_up0", "a_w_out"])],
    "a_in_main_dx": [("near", ["a_w_in"])],
    "a_norm_bwd": [("far", ["a_w_in"])],
}
REDUCE_LAST = ("a_w_in",)
ICI_PEERS = {"ici": ALL_PEERS, "ici_near": NEAR_PEERS, "ici_far": FAR_PEERS,
             "all": ALL_PEERS, "near": NEAR_PEERS, "far": FAR_PEERS}
PAIR_SCHEDULE = {
    "b_o_dx": ["f_w_down1", "f_w_up1", "b_w_o"],
    "ffn0_down_dx": ["b_w_q", "w_kv", "f_w_down0"],
    "a_out_dx": ["f_w_up0", "a_w_out"],
    "a_in_dt_dx": ["a_w_in"],
}
SWAP_SCHEDULE = {"a_in_main_dw": ["f_w_down1", "f_w_up1", "b_w_o", "b_w_q", "w_kv", "f_w_down0", "f_w_up0", "a_w_out"]}


def _buffer_of(weight, layer):
    return weight if layer is None else f"{weight}{layer}"


class _Pipeline:
    def __init__(self, place, slots):
        self.place = place
        self.slots = dict(slots)
        self.running = []
        self.grads = {}
        self.theirs = {}
        self.partials = {}
        self.peers = {}
        self.reduced = {}

    def _collect(self):
        for step, buffers, table in self.running:
            table.update(zip(buffers, step.results))
        self.running = []

    @staticmethod
    def _splits(buffers):
        return [SPLIT.get(b, "rows") for b in buffers]

    def gather_now(self, name, buffers):
        step = _step_gather_full([self.slots[b] for b in buffers], self._splits(buffers))
        _run_steps(name, [step])
        self.slots.update(zip(buffers, step.results))

    def weight(self, name, layer=None):
        self._collect()
        return _weight_from_gathered(name, self.slots[_buffer_of(name, layer)])

    def grad(self, name, layer, g):
        self.grads[_buffer_of(name, layer)] = _gathered_from_grad(name, g)

    def steps(self, kernel):
        self._collect()
        steps = []
        for phase, buffers in GATHER_SCHEDULE.get(kernel, []):
            bufs, splits = [self.slots[b] for b in buffers], self._splits(buffers)
            step = (_step_gather_d2d(bufs, splits) if phase == "d2d"
                    else _step_gather_ici(bufs, splits, ICI_PEERS[phase]))
            self.running.append((step, buffers, self.slots))
            steps.append(step)
        buffers = PAIR_SCHEDULE.get(kernel)
        if buffers:
            step = _step_pair_exchange([self.grads[b] for b in buffers], self._splits(buffers))
            self.running.append((step, buffers, self.theirs))
            steps.append(step)
        for part, buffers in REDUCE_SCHEDULE.get(kernel, []):
            for b in buffers:
                if b not in self.partials:
                    self.partials[b] = _rs_pair_add("reduce_pair_add_" + b, self.place, self.grads[b], self.theirs[b],
                                                    SPLIT.get(b, "rows"))
            started = [self.peers[b] for b in buffers] if all(b in self.peers for b in buffers) else None
            step = _step_chip_exchange([self.partials[b] for b in buffers], ICI_PEERS[part], into=started)
            self.running.append((step, buffers, self.peers))
            steps.append(step)
        buffers = SWAP_SCHEDULE.get(kernel)
        if buffers:
            step = self._swap_step(buffers)
            self.running.append((step, buffers, self.reduced))
            steps.append(step)
        return steps

    def _swap_step(self, buffers):
        halves = [_rs_chip_add("reduce_chip_add_" + b, self.place, self.partials[b], self.peers[b], SPLIT.get(b, "rows"))
                  for b in buffers]
        return _step_pair_gather(halves, self._splits(buffers))

    def shard(self, buffer):
        self._collect()
        return self.reduced[buffer]

    def finish(self):
        self._collect()
        rest = [b for b, _, _ in BUFFERS if b not in self.reduced]
        step = self._swap_step(rest)
        _run_steps("reduce_pair_gather", [step])
        self.reduced.update(zip(rest, step.results))


def kernel(x, meta_tokens, a_norm_pre, a_w_in, a_conv_w, a_conv_b, a_dt_bias, a_a_log, a_d_skip, a_gate_norm, a_w_out, a_norm_post, kv_norm, w_kv, b_norm_pre, b_w_q, b_sinks, b_w_o, b_norm_post, f_norm_pre, f_w_up, f_conv_w, f_conv_b, f_w_down, f_norm_post, loss_target, m_meta_tokens, m_a_norm_pre, m_a_w_in, m_a_conv_w, m_a_conv_b, m_a_dt_bias, m_a_a_log, m_a_d_skip, m_a_gate_norm, m_a_w_out, m_a_norm_post, m_kv_norm, m_w_kv, m_b_norm_pre, m_b_w_q, m_b_sinks, m_b_w_o, m_b_norm_post, m_f_norm_pre, m_f_w_up, m_f_conv_w, m_f_conv_b, m_f_w_down, m_f_norm_post, v_meta_tokens, v_a_norm_pre, v_a_w_in, v_a_conv_w, v_a_conv_b, v_a_dt_bias, v_a_a_log, v_a_d_skip, v_a_gate_norm, v_a_w_out, v_a_norm_post, v_kv_norm, v_w_kv, v_b_norm_pre, v_b_w_q, v_b_sinks, v_b_w_o, v_b_norm_post, v_f_norm_pre, v_f_w_up, v_f_conv_w, v_f_conv_b, v_f_w_down, v_f_norm_post):
    given = dict(locals())
    w = {n: given[n] for n in WEIGHTS}
    mom = {n: given["m_" + n] for n in WEIGHTS}
    var = {n: given["v_" + n] for n in WEIGHTS}
    chip = 2 * lax.axis_index("x") + lax.axis_index("y")
    core = lax.axis_index("c")
    place = jnp.stack([chip, core]).astype(jnp.int32)

    small_all = _allgather_small("gather_small", _pack([w[n] for n in SMALL_SHARDED]))
    small_parts = _unpack(small_all, SMALL_SHARDED, _shard_shape)
    slots = {b: _cast_into_slot("cast_" + b, place, _local_shard(w, wn, layer), layer) for b, wn, layer in BUFFERS}
    pipeline = _Pipeline(place, slots)
    pipeline.gather_now("gather_first", ["a_w_in"])
    p = {}
    for n in SMALL:
        p[n] = _join_chips(n, small_parts[n]) if n in SMALL_SHARDED else w[n]
    p["a_conv_w"] = p["a_conv_w"][0]
    p["kv_norm"] = p["kv_norm"].reshape(1, D_MODEL)

    loss_local, grad_x, g = _local_step(x[0], loss_target[0], p, pipeline)

    small_sum = _allreduce_small("reduce_small", _pack([g[n].reshape(FULL_SHAPE[n]) for n in SMALL]
                                                       + [loss_local.reshape(1, 1)]))
    small_red = _unpack(small_sum, SMALL + ["loss"], lambda n: (1, 1) if n == "loss" else FULL_SHAPE[n])
    loss = small_red["loss"][0, 0]
    grads = {}
    for n in SMALL:
        if SHARD_AXIS[n] is None:
            grads[n] = small_red[n]
        else:
            grads[n] = lax.dynamic_index_in_dim(_split_chips(n, small_red[n]), chip, 0, keepdims=False)

    delta, new_m, new_v = {}, {}, {}
    for n in sorted(BIG, key=lambda name: name in REDUCE_LAST):
        shape = _shard_shape(n)
        if n in REDUCE_LAST:
            pipeline.finish()
        if n in TRANSPOSED:
            g2d = pipeline.shard(n)
            w2d, m2d, v2d = (arrays[n][0].T for arrays in (w, mom, var))
            back = lambda a: a.T.reshape(shape)
        else:
            g2d = (jnp.concatenate([pipeline.shard(n + "0"), pipeline.shard(n + "1")], axis=0)
                   if n in ("f_w_up", "f_w_down") else pipeline.shard(n))
            w2d, m2d, v2d = (_as2d(arrays[n]) for arrays in (w, mom, var))
            back = lambda a: a.reshape(shape)
        d, m2, v2 = _adamw("adamw_" + n, w2d, g2d, m2d, v2d, steps=pipeline.steps("adamw_" + n))
        grads[n], delta[n], new_m[n], new_v[n] = back(g2d), back(d), back(m2), back(v2)
    at_least_2d = lambda n: (1,) * (2 - len(_shard_shape(n))) + _shard_shape(n)
    outs = _adamw_small("adamw_small", *[[src[n].reshape(at_least_2d(n)) for n in SMALL] for src in (w, grads, mom, var)])
    for dst, arrays in zip((delta, new_m, new_v), outs):
        dst.update({n: a.reshape(_shard_shape(n)) for n, a in zip(SMALL, arrays)})

    return (loss, grad_x[None], *[grads[n].reshape(_shard_shape(n)) for n in WEIGHTS],
            *[delta[n] for n in WEIGHTS], *[new_m[n] for n in WEIGHTS], *[new_v[n] for n in WEIGHTS])
```

```python
import functools
import math

import jax
import jax.numpy as jnp
from jax import lax
from jax.experimental import pallas as pl
from jax.experimental.pallas import tpu as pltpu

F32, BF16 = jnp.float32, jnp.bfloat16
MESH = pl.DeviceIdType.MESH

D_MODEL = 1024
N_META = 16
CHUNK = 128
PAD_ROWS = CHUNK - N_META
D_INNER = 2048
D_STATE = 128
N_GROUPS = 4
HEADS_PER_GROUP = 8
SSM_HEADS = 32
HEAD_DIM = 64
D_BC = N_GROUPS * D_STATE
D_XBC = D_INNER + 2 * D_BC
D_MAIN = D_INNER + D_XBC
D_IN_PROJ = D_MAIN + SSM_HEADS
GROUP_W = HEADS_PER_GROUP * HEAD_DIM
SSM_CONV = 4
D_FF = 2816
FFN_CONV = 3
N_Q_HEADS = 16
N_KV_HEADS = 4
D_KV = 256
ATTN_SCALE = 1.0 / math.sqrt(HEAD_DIM)
RMS_EPS = 1e-6
NEG_INF = -1e30
LANES = 128
VMEM_LIMIT = 48 * 1024 * 1024

ADAM_LR, ADAM_B1, ADAM_B2, ADAM_EPS, ADAM_WD, ADAM_STEP = 0.001, 0.9, 0.999, 1e-08, 0.01, 10

N_CHIPS = 4


def _cparams(sem=None):
    return pltpu.CompilerParams(dimension_semantics=sem, vmem_limit_bytes=VMEM_LIMIT)


def _tile(n, cands=(512, 256, 128)):
    for t in cands:
        if n % t == 0:
            return t
    return n


def _row_tile(rows, width):
    for t in (544, 272):
        if rows % t == 0 and t * width * 4 <= (3 << 20):
            return t
    return 128


def _rows_mask(i, tm):
    rows = i * tm + lax.broadcasted_iota(jnp.int32, (tm, 1), 0)
    return rows >= PAD_ROWS


def _dot(a, b):
    return jnp.dot(a, b, preferred_element_type=F32)


def _dot_nt(a, b):
    return lax.dot_general(a, b, (((1,), (1,)), ((), ())), preferred_element_type=F32)


def _dot_tn(a, b):
    return lax.dot_general(a, b, (((0,), (0,)), ((), ())), preferred_element_type=F32)


def _sigmoid(x):
    return 1.0 / (1.0 + jnp.exp(-x))


def _place():
    return lax.axis_index("x"), lax.axis_index("y"), lax.axis_index("c")


def _other_chips(x, y):
    return [(1 - x, y), (x, 1 - y), (1 - x, 1 - y)]


class _Step:
    def __init__(self, ins, outs, aliases, n_sems, start, finish):
        self.ins, self.outs, self.aliases, self.n_sems = list(ins), list(outs), dict(aliases), n_sems
        self.start, self.finish = start, finish
        self.results = None


def _like(a):
    return jax.ShapeDtypeStruct(a.shape, a.dtype)


def _remote(src, dst, send_sems, recv_sems, k, device):
    return pltpu.make_async_remote_copy(src, dst, send_sems.at[k], recv_sems.at[k], device_id=device, device_id_type=MESH)


def _half(ref, split, which, lead=()):
    if split == "rows":
        hr = ref.shape[-2] // 2
        return ref.at[lead + (pl.ds(which * hr, hr),)]
    hc = ref.shape[-1] // 2
    return ref.at[lead + (slice(None), pl.ds(which * hc, hc))]


def _splits(bufs, splits):
    return list(splits) if splits is not None else ["rows"] * len(bufs)


ALL_PEERS = (0, 1, 2)
NEAR_PEERS = (0, 1)
FAR_PEERS = (2,)


def _step_gather_ici(bufs, splits=None, peers=ALL_PEERS):
    splits = _splits(bufs, splits)

    def copies(outs, send_sems, recv_sems, received):
        x, y, c = _place()
        me = 2 * x + y
        for k, o in enumerate(outs):
            for j, (cx, cy) in enumerate(_other_chips(x, y)):
                if j in peers:
                    part = _half(o, splits[k], c, (2 * cx + cy if received else me,))
                    yield _remote(part, part, send_sems, recv_sems, 3 * k + j, (cx, cy, c))

    def start(ins, outs, send_sems, recv_sems):
        for cp in copies(outs, send_sems, recv_sems, False):
            cp.start()

    def finish(ins, outs, send_sems, recv_sems):
        for cp in copies(outs, send_sems, recv_sems, True):
            cp.wait_recv()
        for cp in copies(outs, send_sems, recv_sems, False):
            cp.wait_send()

    return _Step(bufs, [_like(b) for b in bufs], {k: k for k in range(len(bufs))}, 3 * len(bufs), start, finish)


def _step_gather_d2d(bufs, splits=None):
    splits = _splits(bufs, splits)

    def copies(outs, send_sems, recv_sems, received):
        x, y, c = _place()
        for k, o in enumerate(outs):
            for j, (cx, cy) in enumerate(_other_chips(x, y)):
                part = _half(o, splits[k], 1 - c if received else c, (2 * cx + cy,))
                yield _remote(part, part, send_sems, recv_sems, 3 * k + j, (x, y, 1 - c))

    def start(ins, outs, send_sems, recv_sems):
        for cp in copies(outs, send_sems, recv_sems, False):
            cp.start()

    def finish(ins, outs, send_sems, recv_sems):
        for cp in copies(outs, send_sems, recv_sems, True):
            cp.wait_recv()
        for cp in copies(outs, send_sems, recv_sems, False):
            cp.wait_send()

    return _Step(bufs, [_like(b) for b in bufs], {k: k for k in range(len(bufs))}, 3 * len(bufs), start, finish)


def _step_gather_full(bufs, splits=None):
    n = len(bufs)
    splits = _splits(bufs, splits)

    def ici(outs, send_sems, recv_sems, received):
        x, y, c = _place()
        me = 2 * x + y
        for k, o in enumerate(outs):
            for j, (cx, cy) in enumerate(_other_chips(x, y)):
                part = _half(o, splits[k], c, (2 * cx + cy if received else me,))
                yield _remote(part, part, send_sems, recv_sems, 3 * k + j, (cx, cy, c))

    def d2d(outs, send_sems, recv_sems, received):
        x, y, c = _place()
        for k, o in enumerate(outs):
            for j, (cx, cy) in enumerate(_other_chips(x, y)):
                part = _half(o, splits[k], 1 - c if received else c, (2 * cx + cy,))
                yield _remote(part, part, send_sems, recv_sems, 3 * n + 3 * k + j, (x, y, 1 - c))

    def start(ins, outs, send_sems, recv_sems):
        for cp in ici(outs, send_sems, recv_sems, False):
            cp.start()

    def finish(ins, outs, send_sems, recv_sems):
        for arrived, onward in zip(ici(outs, send_sems, recv_sems, True), d2d(outs, send_sems, recv_sems, False)):
            arrived.wait_recv()
            onward.start()
        for cp in d2d(outs, send_sems, recv_sems, True):
            cp.wait_recv()
        for cp in ici(outs, send_sems, recv_sems, False):
            cp.wait_send()
        for cp in d2d(outs, send_sems, recv_sems, False):
            cp.wait_send()

    return _Step(bufs, [_like(b) for b in bufs], {k: k for k in range(n)}, 6 * n, start, finish)


def _half_shape(shape, split):
    return shape[:-2] + ((shape[-2] // 2, shape[-1]) if split == "rows" else (shape[-2], shape[-1] // 2))


def _step_pair_exchange(grads, splits=None):
    splits = _splits(grads, splits)

    def copies(ins, outs, send_sems, recv_sems):
        x, y, c = _place()
        for k, (g, o) in enumerate(zip(ins, outs)):
            yield _remote(_half(g, splits[k], 1 - c, (slice(None),)), o, send_sems, recv_sems, k, (x, y, 1 - c))

    def start(ins, outs, send_sems, recv_sems):
        for cp in copies(ins, outs, send_sems, recv_sems):
            cp.start()

    def finish(ins, outs, send_sems, recv_sems):
        for cp in copies(ins, outs, send_sems, recv_sems):
            cp.wait()

    outs = [jax.ShapeDtypeStruct(_half_shape(g.shape, s), g.dtype) for g, s in zip(grads, splits)]
    return _Step(grads, outs, {}, len(grads), start, finish)


def _step_chip_exchange(partials, peers=ALL_PEERS, into=None):
    n = len(partials)

    def copies(ins, outs, send_sems, recv_sems):
        x, y, c = _place()
        for k, (q, o) in enumerate(zip(ins[:n], outs)):
            for j, (cx, cy) in enumerate(_other_chips(x, y)):
                if j in peers:
                    yield _remote(q.at[2 * cx + cy], o.at[j], send_sems, recv_sems, 3 * k + j, (cx, cy, c))

    def start(ins, outs, send_sems, recv_sems):
        for cp in copies(ins, outs, send_sems, recv_sems):
            cp.start()

    def finish(ins, outs, send_sems, recv_sems):
        for cp in copies(ins, outs, send_sems, recv_sems):
            cp.wait()

    outs = [jax.ShapeDtypeStruct((3,) + q.shape[1:], q.dtype) for q in partials]
    if into is None:
        return _Step(partials, outs, {}, 3 * n, start, finish)
    return _Step(list(partials) + list(into), outs, {n + k: k for k in range(n)}, 3 * n, start, finish)


def _step_pair_gather(shards, splits=None):
    splits = _splits(shards, splits)

    def copies(outs, send_sems, recv_sems, received):
        x, y, c = _place()
        for k, o in enumerate(outs):
            part = _half(o, splits[k], 1 - c if received else c)
            yield _remote(part, part, send_sems, recv_sems, k, (x, y, 1 - c))

    def start(ins, outs, send_sems, recv_sems):
        for cp in copies(outs, send_sems, recv_sems, False):
            cp.start()

    def finish(ins, outs, send_sems, recv_sems):
        for cp in copies(outs, send_sems, recv_sems, True):
            cp.wait_recv()
        for cp in copies(outs, send_sems, recv_sems, False):
            cp.wait_send()

    return _Step(shards, [_like(s) for s in shards], {k: k for k in range(len(shards))}, len(shards), start, finish)


def _call(body, *, name, out_shape, grid, in_specs, out_specs, operands, scratch_shapes=(), semantics=None, steps=()):
    single = not isinstance(out_shape, (tuple, list))
    out_shapes = [out_shape] if single else list(out_shape)
    out_spec_list = [out_specs] if single else list(out_specs)
    steps = list(steps)
    if not steps:
        res = pl.pallas_call(body, name=name, out_shape=out_shapes, grid=grid, in_specs=list(in_specs),
                             out_specs=out_spec_list, scratch_shapes=list(scratch_shapes),
                             compiler_params=_cparams(semantics))(*operands)
        return res[0] if single else res
    n_in, n_out, n_scr = len(operands), len(out_shapes), len(scratch_shapes)
    x_in = [a for s in steps for a in s.ins]
    x_out = [o for s in steps for o in s.outs]
    aliases, in_off, out_off = {}, 0, 0
    for s in steps:
        for i, o in s.aliases.items():
            aliases[n_in + in_off + i] = n_out + out_off + o
        in_off += len(s.ins)
        out_off += len(s.outs)
    sems = []
    for s in steps:
        sems += [pltpu.SemaphoreType.DMA((s.n_sems,)), pltpu.SemaphoreType.DMA((s.n_sems,))]
    any_spec = pl.BlockSpec(memory_space=pl.ANY)

    def carried(*refs):
        pos = 0
        ins = refs[pos:pos + n_in]; pos += n_in
        xi = refs[pos:pos + len(x_in)]; pos += len(x_in)
        outs = refs[pos:pos + n_out]; pos += n_out
        xo = refs[pos:pos + len(x_out)]; pos += len(x_out)
        scr = refs[pos:pos + n_scr]; pos += n_scr
        sem_refs = refs[pos:]

        def each(action):
            i0 = o0 = 0
            for k, s in enumerate(steps):
                getattr(s, action)(xi[i0:i0 + len(s.ins)], xo[o0:o0 + len(s.outs)], sem_refs[2 * k], sem_refs[2 * k + 1])
                i0 += len(s.ins)
                o0 += len(s.outs)

        if grid:
            first = functools.reduce(jnp.logical_and, [pl.program_id(d) == 0 for d in range(len(grid))])
            last = functools.reduce(jnp.logical_and, [pl.program_id(d) == grid[d] - 1 for d in range(len(grid))])
            pl.when(first)(lambda: each("start"))
            body(*ins, *outs, *scr)
            pl.when(last)(lambda: each("finish"))
        else:
            each("start")
            body(*ins, *outs, *scr)
            each("finish")

    res = pl.pallas_call(
        carried, name=name, out_shape=out_shapes + x_out, grid=grid,
        in_specs=list(in_specs) + [any_spec] * len(x_in), out_specs=out_spec_list + [any_spec] * len(x_out),
        scratch_shapes=list(scratch_shapes) + sems, input_output_aliases=aliases,
        compiler_params=_cparams(None if semantics is None else ("arbitrary",) * len(grid)),
    )(*operands, *x_in)
    o0 = n_out
    for s in steps:
        s.results = list(res[o0:o0 + len(s.outs)])
        o0 += len(s.outs)
    return res[0] if single else tuple(res[:n_out])


def _run_steps(name, steps):
    _call(lambda: None, name=name, out_shape=[], grid=(), in_specs=[], out_specs=[], operands=[], steps=steps)
    return [s.results for s in steps]


def _mm(name, a, b, mode, out_dtype=F32, acc=None, b_colblock=0, k_rows=None, out_rows=None, steps=()):
    resident_bytes = 8 << 20
    if mode == "nn":
        m, k = a.shape
        n = b.shape[1]
        tm = m
        while tm * k * 2 > resident_bytes and tm % 32 == 0:
            tm //= 2
        tn = _tile(n)
        grid = (m // tm, n // tn)
        in_specs = [pl.BlockSpec((tm, k), lambda i, j: (i, 0)), pl.BlockSpec((k, tn), lambda i, j: (0, j))]
        out_shape, out_block = (m, n), (tm, tn)
    elif mode == "nt":
        m, n = a.shape
        k = k_rows or b.shape[0]
        tm = m
        while tm * n * 2 > resident_bytes and tm % 32 == 0:
            tm //= 2
        tk = _tile(k)
        grid = (m // tm, k // tk)
        in_specs = [pl.BlockSpec((tm, n), lambda i, j: (i, 0)), pl.BlockSpec((tk, n), lambda i, j: (j, b_colblock))]
        out_shape, out_block = (m, k), (tm, tk)
    else:
        m, k = a.shape
        n = b.shape[1]
        tk, tn = _tile(k), (n if m * n * 2 <= resident_bytes else _tile(n))
        grid = (k // tk, n // tn)
        in_specs = [pl.BlockSpec((m, tk), lambda i, j: (0, i)), pl.BlockSpec((m, tn), lambda i, j: (0, j))]
        out_shape, out_block = (out_rows or k, n), (tk, tn)
    out_spec = pl.BlockSpec(out_block, lambda i, j: (i, j))
    has_acc = acc is not None

    def body(*refs):
        a_ref, b_ref = refs[0], refs[1]
        o_ref = refs[-1]
        av, bv = a_ref[...], b_ref[...]
        if mode == "nn":
            r = _dot(av, bv)
        elif mode == "nt":
            r = _dot_nt(av, bv)
        else:
            r = _dot_tn(av, bv)
        if has_acc:
            r = r + refs[2][...]
        o_ref[...] = r.astype(o_ref.dtype)

    operands = [a, b]
    if has_acc:
        in_specs = in_specs + [out_spec]
        operands.append(acc)
    return _call(body, name=name, out_shape=jax.ShapeDtypeStruct(out_shape, out_dtype), grid=grid, in_specs=in_specs,
                 out_specs=out_spec, operands=operands, semantics=("parallel", "parallel"), steps=steps)


def _tn_rows_into(name, a, b, into, row0, nrows):
    m, k = a.shape
    n = b.shape[1]

    def body(a_ref, b_ref, into_ref, o_ref):
        o_ref[...] = _dot_tn(a_ref[...], b_ref[...])[0:nrows].astype(o_ref.dtype)

    return pl.pallas_call(
        body, name=name, out_shape=jax.ShapeDtypeStruct(into.shape, into.dtype), grid=(1,),
        in_specs=[pl.BlockSpec((m, k), lambda i: (0, 0)), pl.BlockSpec((m, n), lambda i: (0, 0)),
                  pl.BlockSpec(memory_space=pl.ANY)],
        out_specs=pl.BlockSpec((nrows, n), lambda i: (row0 // nrows, 0)),
        input_output_aliases={2: 0}, compiler_params=_cparams(("arbitrary",)),
    )(a, b, into)


def _rms_fwd(name, h, w):
    rows, width = h.shape
    tm = _row_tile(rows, width)

    def body(h_ref, w_ref, o_ref):
        x = h_ref[...]
        r = lax.rsqrt(jnp.mean(x * x, axis=-1, keepdims=True) + RMS_EPS)
        o_ref[...] = (x * r * w_ref[...]).astype(BF16)

    return pl.pallas_call(
        body, name=name, out_shape=jax.ShapeDtypeStruct((rows, width), BF16), grid=(rows // tm,),
        in_specs=[pl.BlockSpec((tm, width), lambda i: (i, 0)), pl.BlockSpec((1, width), lambda i: (0, 0))],
        out_specs=pl.BlockSpec((tm, width), lambda i: (i, 0)), compiler_params=_cparams(("parallel",)),
    )(h, w)


def _resid_norm_fwd(name, h, pre, w, next_norms=()):
    rows, width = h.shape
    tm = _row_tile(rows, width)
    n_next = len(next_norms)

    def body(*refs):
        h_ref, p_ref, w_ref = refs[:3]
        v_refs = refs[3:3 + n_next]
        o_ref = refs[3 + n_next]
        n_refs = refs[4 + n_next:]
        p = p_ref[...]
        r = lax.rsqrt(jnp.mean(p * p, axis=-1, keepdims=True) + RMS_EPS)
        x = h_ref[...] + jnp.where(_rows_mask(pl.program_id(0), tm), p * r * w_ref[...], 0.0)
        o_ref[...] = x
        if n_next:
            rx = lax.rsqrt(jnp.mean(x * x, axis=-1, keepdims=True) + RMS_EPS)
            for v_ref, n_ref in zip(v_refs, n_refs):
                n_ref[...] = (x * rx * v_ref[...]).astype(BF16)

    row_spec = pl.BlockSpec((tm, width), lambda i: (i, 0))
    vec_spec = pl.BlockSpec((1, width), lambda i: (0, 0))
    outs = pl.pallas_call(
        body, name=name,
        out_shape=[jax.ShapeDtypeStruct((rows, width), F32)] + [jax.ShapeDtypeStruct((rows, width), BF16)] * n_next,
        grid=(rows // tm,), in_specs=[row_spec, row_spec, vec_spec] + [vec_spec] * n_next,
        out_specs=[row_spec] * (1 + n_next), compiler_params=_cparams(("parallel",)),
    )(h, pre, w, *next_norms)
    return outs[0], list(outs[1:])


def _resid_norm_loss(name, h, pre, w, target):
    rows, width = h.shape

    def body(h_ref, p_ref, w_ref, t_ref, dh_ref, loss_ref, dp_ref, dw_ref):
        i = pl.program_id(0)
        p = p_ref[...]
        r = lax.rsqrt(jnp.mean(p * p, axis=-1, keepdims=True) + RMS_EPS)
        x = h_ref[...] + p * r * w_ref[...]
        real = (i + jnp.zeros((CHUNK, 1), jnp.int32)) >= 1
        diff = jnp.where(real, x - t_ref[...], 0.0)
        dh = diff * (1.0 / D_MODEL)
        dh_ref[...] = dh
        dp, dw_rows = _rms_bwd(dh, p, w_ref[...])
        dp_ref[...] = dp.astype(BF16)

        @pl.when(i == 0)
        def _():
            loss_ref[...] = jnp.zeros_like(loss_ref)
            dw_ref[...] = jnp.zeros_like(dw_ref)

        loss_ref[...] += jnp.sum(diff * diff) * (0.5 / D_MODEL)
        dw_ref[...] += jnp.sum(dw_rows, axis=0, keepdims=True)

    blk = pl.BlockSpec((CHUNK, width), lambda i: (i, 0))
    vec_spec = pl.BlockSpec((1, width), lambda i: (0, 0))
    return pl.pallas_call(
        body, name=name,
        out_shape=(jax.ShapeDtypeStruct((rows, width), F32), jax.ShapeDtypeStruct((1, LANES), F32),
                   jax.ShapeDtypeStruct((rows, width), BF16), jax.ShapeDtypeStruct((1, width), F32)),
        grid=(rows // CHUNK,),
        in_specs=[blk, blk, vec_spec, pl.BlockSpec((CHUNK, width), lambda i: (jnp.maximum(i - 1, 0), 0))],
        out_specs=(blk, pl.BlockSpec((1, LANES), lambda i: (0, 0)), blk, vec_spec),
        compiler_params=_cparams(("arbitrary",)),
    )(h, pre, w, target)


def _rms_bwd(dy, x, w):
    r = lax.rsqrt(jnp.mean(x * x, axis=-1, keepdims=True) + RMS_EPS)
    xhat = x * r
    dxhat = dy * w
    return r * (dxhat - xhat * jnp.mean(dxhat * xhat, axis=-1, keepdims=True)), dy * xhat


def _norm_bwd_add(name, dh, dhn, h, w, then=None, steps=()):
    rows, width = dh.shape
    tm = _row_tile(rows, width)
    fused = then is not None

    def body(*refs):
        dh_ref, dhn_ref, h_ref, w_ref = refs[:4]
        o_ref, dw_ref = refs[6:8] if fused else refs[4:6]
        i = pl.program_id(0)
        valid = _rows_mask(i, tm)
        dx, dw_rows = _rms_bwd(dhn_ref[...], h_ref[...], w_ref[...])
        dh_new = dh_ref[...] + jnp.where(valid, dx, 0.0)
        o_ref[...] = dh_new

        @pl.when(i == 0)
        def _():
            dw_ref[...] = jnp.zeros_like(dw_ref)

        dw_ref[...] += jnp.sum(dw_rows, axis=0, keepdims=True)
        if fused:
            p_ref, wp_ref, dp_ref, dwp_ref = refs[4], refs[5], refs[8], refs[9]
            dp, dwp_rows = _rms_bwd(jnp.where(valid, dh_new, 0.0), p_ref[...], wp_ref[...])
            dp_ref[...] = dp.astype(BF16)

            @pl.when(i == 0)
            def _():
                dwp_ref[...] = jnp.zeros_like(dwp_ref)

            dwp_ref[...] += jnp.sum(dwp_rows, axis=0, keepdims=True)

    row_spec = pl.BlockSpec((tm, width), lambda i: (i, 0))
    vec_spec = pl.BlockSpec((1, width), lambda i: (0, 0))
    row_f32, vec_f32 = jax.ShapeDtypeStruct((rows, width), F32), jax.ShapeDtypeStruct((1, width), F32)
    in_specs, operands = [row_spec, row_spec, row_spec, vec_spec], [dh, dhn, h, w]
    out_shape, out_specs = [row_f32, vec_f32], [row_spec, vec_spec]
    if fused:
        in_specs += [row_spec, vec_spec]
        operands += list(then)
        out_shape += [jax.ShapeDtypeStruct((rows, width), BF16), vec_f32]
        out_specs += [row_spec, vec_spec]
    return _call(body, name=name, out_shape=out_shape, grid=(rows // tm,), in_specs=in_specs, out_specs=out_specs,
                 operands=operands, semantics=("arbitrary",), steps=steps)


def _shift_down(x, s, rows):
    return pltpu.roll(x, s, 0) if s else x


def _shift_up(x, s, rows):
    return pltpu.roll(x, rows - s, 0) if s else x


def _conv4_fwd(name, zx, cw, cb, steps=()):
    rows = zx.shape[0]
    off = D_INNER // LANES

    def body(x_ref, w_ref, b_ref, o_ref):
        x = x_ref[...]
        acc = b_ref[...] + w_ref[pl.ds(SSM_CONV - 1, 1), :] * x
        for s in range(1, SSM_CONV):
            acc = acc + w_ref[pl.ds(SSM_CONV - 1 - s, 1), :] * _shift_down(x, s, rows)
        valid = lax.broadcasted_iota(jnp.int32, (rows, 1), 0) >= PAD_ROWS
        o_ref[...] = jnp.where(valid, acc * _sigmoid(acc), 0.0)

    return _call(
        body, name=name, out_shape=jax.ShapeDtypeStruct((rows, D_XBC), F32), grid=(D_XBC // LANES,),
        in_specs=[pl.BlockSpec((rows, LANES), lambda j: (0, j + off)),
                  pl.BlockSpec((SSM_CONV, LANES), lambda j: (0, j)),
                  pl.BlockSpec((1, LANES), lambda j: (0, j))],
        out_specs=pl.BlockSpec((rows, LANES), lambda j: (0, j)), operands=[zx, cw, cb],
        semantics=("parallel",), steps=steps)


def _conv4_bwd(name, zx, dout, cw, cb, col0, into):
    rows, width = dout.shape
    zoff = (D_INNER + col0) // LANES
    woff = col0 // LANES

    def body(x_ref, d_ref, w_ref, b_ref, into_ref, dx_ref, dw_ref, db_ref):
        x = x_ref[...]
        shifted = [_shift_down(x, s, rows) for s in range(SSM_CONV)]
        acc = b_ref[...]
        for s in range(SSM_CONV):
            acc = acc + w_ref[pl.ds(SSM_CONV - 1 - s, 1), :] * shifted[s]
        sig = _sigmoid(acc)
        valid = lax.broadcasted_iota(jnp.int32, (rows, 1), 0) >= PAD_ROWS
        dpre = jnp.where(valid, d_ref[...] * sig * (1.0 + acc * (1.0 - sig)), 0.0)
        dx = w_ref[pl.ds(SSM_CONV - 1, 1), :] * dpre
        for s in range(1, SSM_CONV):
            dx = dx + w_ref[pl.ds(SSM_CONV - 1 - s, 1), :] * _shift_up(dpre, s, rows)
        dx_ref[...] = dx.astype(BF16)
        for s in range(SSM_CONV):
            dw_ref[pl.ds(SSM_CONV - 1 - s, 1), :] = jnp.sum(dpre * shifted[s], axis=0, keepdims=True)
        db_ref[...] = jnp.sum(dpre, axis=0, keepdims=True)

    return pl.pallas_call(
        body, name=name,
        out_shape=(jax.ShapeDtypeStruct(into.shape, BF16), jax.ShapeDtypeStruct((SSM_CONV, width), F32),
                   jax.ShapeDtypeStruct((1, width), F32)),
        grid=(width // LANES,),
        in_specs=[pl.BlockSpec((rows, LANES), lambda j: (0, j + zoff)),
                  pl.BlockSpec((rows, LANES), lambda j: (0, j)),
                  pl.BlockSpec((SSM_CONV, LANES), lambda j: (0, j + woff)),
                  pl.BlockSpec((1, LANES), lambda j: (0, j + woff)),
                  pl.BlockSpec(memory_space=pl.ANY)],
        out_specs=(pl.BlockSpec((rows, LANES), lambda j: (0, j + zoff)),
                   pl.BlockSpec((SSM_CONV, LANES), lambda j: (0, j)),
                   pl.BlockSpec((1, LANES), lambda j: (0, j))),
        input_output_aliases={4: 0}, compiler_params=_cparams(("parallel",)),
    )(zx, dout, cw, cb, into)


FFN_TILE = 2 * LANES


def _ffn_up_conv(name, hn, w_up, cw, cb, steps=()):
    rows, k = hn.shape
    chip_blocks = w_up.shape[2] // LANES
    half_blocks = D_FF // LANES
    nt = D_FF // FFN_TILE

    def weight_block(offset):
        return pl.BlockSpec((None, k, LANES), lambda j: ((2 * j + offset) // chip_blocks, 0, (2 * j + offset) % chip_blocks))

    def body(a_ref, g0, g1, v0, v1, wg_ref, wv_ref, bg_ref, bv_ref, upg_ref, upv_ref, act_ref):
        a = a_ref[...]
        g = _dot(a, jnp.concatenate([g0[...], g1[...]], axis=1))
        v = _dot(a, jnp.concatenate([v0[...], v1[...]], axis=1))
        upg_ref[...] = g
        upv_ref[...] = v
        ug, uv = bg_ref[...], bv_ref[...]
        for s in range(FFN_CONV):
            ug = ug + wg_ref[pl.ds(FFN_CONV - 1 - s, 1), :] * _shift_down(g, s, rows)
            uv = uv + wv_ref[pl.ds(FFN_CONV - 1 - s, 1), :] * _shift_down(v, s, rows)
        act_ref[...] = (ug * _sigmoid(ug) * uv).astype(BF16)

    col = pl.BlockSpec((rows, FFN_TILE), lambda j: (0, j))
    wsp = lambda shift: pl.BlockSpec((FFN_CONV, FFN_TILE), lambda j: (0, j + shift))
    bsp = lambda shift: pl.BlockSpec((1, FFN_TILE), lambda j: (0, j + shift))
    half = jax.ShapeDtypeStruct((rows, D_FF), F32)
    return _call(
        body, name=name, out_shape=(half, half, jax.ShapeDtypeStruct((rows, D_FF), BF16)), grid=(nt,),
        in_specs=[pl.BlockSpec((rows, k), lambda j: (0, 0)), weight_block(0), weight_block(1),
                  weight_block(half_blocks), weight_block(half_blocks + 1), wsp(0), wsp(nt), bsp(0), bsp(nt)],
        out_specs=(col, col, col), operands=[hn, w_up, w_up, w_up, w_up, cw, cw, cb, cb],
        semantics=("parallel",), steps=steps)


def _ffn_conv_bwd(name, up_g, up_v, dact, cw, cb, hn, w_up, steps=()):
    rows, k = hn.shape
    chip_blocks = w_up.shape[2] // LANES
    nt = D_FF // LANES

    def weight_block(shift):
        return pl.BlockSpec((None, k, LANES), lambda j: ((j + shift) // chip_blocks, 0, (j + shift) % chip_blocks))

    def body(g_ref, v_ref, d_ref, wg_ref, wv_ref, bg_ref, bv_ref, upg_ref, upv_ref, hn_ref,
             dwg_ref, dwv_ref, dbg_ref, dbv_ref, dhn_ref, dup_ref, acc, hn_scr, hnt_scr, dup_scr, sems):
        j = pl.program_id(0)
        hn_copy = pltpu.make_async_copy(hn_ref, hn_scr, sems.at[0])
        dhn_copy = pltpu.make_async_copy(acc, dhn_ref, sems.at[0])

        def dup_copy(step, half):
            block, slot = step + half * nt, 2 * (step % 2) + half
            cols = pl.ds(pl.multiple_of((block % chip_blocks) * LANES, LANES), LANES)
            return pltpu.make_async_copy(dup_scr.at[slot], dup_ref.at[block // chip_blocks, :, cols], sems.at[1 + slot])

        @pl.when(j == 0)
        def _():
            hn_copy.start()
            acc[...] = jnp.zeros_like(acc)
            hn_copy.wait()
            for r in range(0, rows, LANES):
                hnt_scr[:, r:r + LANES] = hn_scr[r:r + LANES, :].T

        @pl.when(j >= 2)
        def _():
            dup_copy(j - 2, 0).wait()
            dup_copy(j - 2, 1).wait()

        g, v = g_ref[...], v_ref[...]
        gs = [_shift_down(g, s, rows) for s in range(FFN_CONV)]
        vs = [_shift_down(v, s, rows) for s in range(FFN_CONV)]
        ug, uv = bg_ref[...], bv_ref[...]
        for s in range(FFN_CONV):
            ug = ug + wg_ref[pl.ds(FFN_CONV - 1 - s, 1), :] * gs[s]
            uv = uv + wv_ref[pl.ds(FFN_CONV - 1 - s, 1), :] * vs[s]
        sig = _sigmoid(ug)
        dsig = d_ref[...] * sig
        dup = []
        for dpre, src, w_ref, dw_ref, db_ref in (
                (dsig * uv * (1.0 + ug * (1.0 - sig)), gs, wg_ref, dwg_ref, dbg_ref),
                (dsig * ug, vs, wv_ref, dwv_ref, dbv_ref)):
            dx = w_ref[pl.ds(FFN_CONV - 1, 1), :] * dpre
            for s in range(1, FFN_CONV):
                dx = dx + w_ref[pl.ds(FFN_CONV - 1 - s, 1), :] * _shift_up(dpre, s, rows)
            dup.append(dx.astype(BF16))
            for s in range(FFN_CONV):
                dw_ref[pl.ds(FFN_CONV - 1 - s, 1), :] = jnp.sum(dpre * src[s], axis=0, keepdims=True)
            db_ref[...] = jnp.sum(dpre, axis=0, keepdims=True)
        dup = jnp.concatenate(dup, axis=1)
        acc[...] += _dot_nt(dup, jnp.concatenate([upg_ref[...], upv_ref[...]], axis=1))
        dw = _dot(hnt_scr[...], dup)
        slot = 2 * (j % 2)
        dup_scr[slot] = dw[:, :LANES].astype(BF16)
        dup_scr[slot + 1] = dw[:, LANES:].astype(BF16)
        dup_copy(j, 0).start()
        dup_copy(j, 1).start()

        @pl.when(j == nt - 1)
        def _():
            dhn_copy.start()
            for step in (j - 1, j):
                dup_copy(step, 0).wait()
                dup_copy(step, 1).wait()
            dhn_copy.wait()

    col = pl.BlockSpec((rows, LANES), lambda j: (0, j))
    wsp = lambda shift: pl.BlockSpec((FFN_CONV, LANES), lambda j: (0, j + shift))
    bsp = lambda shift: pl.BlockSpec((1, LANES), lambda j: (0, j + shift))
    any_spec = pl.BlockSpec(memory_space=pl.ANY)
    dw_shape = jax.ShapeDtypeStruct((FFN_CONV, D_FF), F32)
    db_shape = jax.ShapeDtypeStruct((1, D_FF), F32)
    return _call(
        body, name=name, grid=(nt,),
        out_shape=(dw_shape, dw_shape, db_shape, db_shape, jax.ShapeDtypeStruct((rows, k), F32),
                   jax.ShapeDtypeStruct(w_up.shape, BF16)),
        in_specs=[col, col, col, wsp(0), wsp(nt), bsp(0), bsp(nt), weight_block(0), weight_block(nt), any_spec],
        out_specs=(wsp(0), wsp(0), bsp(0), bsp(0), any_spec, any_spec),
        operands=[up_g, up_v, dact, cw, cw, cb, cb, w_up, w_up, hn],
        scratch_shapes=[pltpu.VMEM((rows, k), F32), pltpu.VMEM((rows, k), BF16), pltpu.VMEM((k, rows), BF16),
                        pltpu.VMEM((4, k, LANES), BF16), pltpu.SemaphoreType.DMA((5,))],
        semantics=("arbitrary",), steps=steps)


def _dt_fwd(name, dtr, bias):
    rows = dtr.shape[0]
    tm = _row_tile(rows, LANES)

    def body(d_ref, b_ref, o_ref):
        v = d_ref[...] + b_ref[...]
        sp = jnp.maximum(v, 0.0) + jnp.log1p(jnp.exp(-jnp.abs(v)))
        lane = lax.broadcasted_iota(jnp.int32, (tm, LANES), 1)
        ok = _rows_mask(pl.program_id(0), tm) & (lane < SSM_HEADS)
        o_ref[...] = jnp.where(ok, sp, 0.0)

    return pl.pallas_call(
        body, name=name, out_shape=jax.ShapeDtypeStruct((rows, LANES), F32), grid=(rows // tm,),
        in_specs=[pl.BlockSpec((tm, LANES), lambda i: (i, 0)), pl.BlockSpec((1, LANES), lambda i: (0, 0))],
        out_specs=pl.BlockSpec((tm, LANES), lambda i: (i, 0)), compiler_params=_cparams(("parallel",)),
    )(dtr, bias)


def _dt_bwd(name, ddt, dtr, bias):
    rows = dtr.shape[0]
    tm = _row_tile(rows, LANES)

    def body(g_ref, d_ref, b_ref, o_ref, db_ref):
        i = pl.program_id(0)
        lane = lax.broadcasted_iota(jnp.int32, (tm, LANES), 1)
        ok = _rows_mask(i, tm) & (lane < SSM_HEADS)
        dv = jnp.where(ok, g_ref[...] * _sigmoid(d_ref[...] + b_ref[...]), 0.0)
        o_ref[...] = dv.astype(BF16)

        @pl.when(i == 0)
        def _():
            db_ref[...] = jnp.zeros_like(db_ref)

        db_ref[...] += jnp.sum(dv, axis=0, keepdims=True)

    row_spec = pl.BlockSpec((tm, LANES), lambda i: (i, 0))
    vec_spec = pl.BlockSpec((1, LANES), lambda i: (0, 0))
    return pl.pallas_call(
        body, name=name,
        out_shape=(jax.ShapeDtypeStruct((rows, LANES), BF16), jax.ShapeDtypeStruct((1, LANES), F32)),
        grid=(rows // tm,), in_specs=[row_spec, row_spec, vec_spec], out_specs=(row_spec, vec_spec),
        compiler_params=_cparams(("arbitrary",)),
    )(ddt, dtr, bias)


def _gate_fwd(name, y, zx, w, steps=()):
    rows = y.shape[0]
    tm = _row_tile(rows, D_INNER)

    def body(y_ref, z_ref, w_ref, o_ref):
        z = z_ref[...]
        g = y_ref[...] * (z * _sigmoid(z))
        r = lax.rsqrt(jnp.mean(g * g, axis=-1, keepdims=True) + RMS_EPS)
        o_ref[...] = (g * r * w_ref[...]).astype(BF16)

    row_spec = pl.BlockSpec((tm, D_INNER), lambda i: (i, 0))
    return _call(
        body, name=name, out_shape=jax.ShapeDtypeStruct((rows, D_INNER), BF16), grid=(rows // tm,),
        in_specs=[row_spec, row_spec, pl.BlockSpec((1, D_INNER), lambda i: (0, 0))],
        out_specs=row_spec, operands=[y, zx, w], semantics=("parallel",), steps=steps)


def _gate_bwd(name, dyn, y, zx, w):
    rows = y.shape[0]
    tm = _row_tile(rows, D_INNER)

    def body(d_ref, y_ref, z_ref, w_ref, dy_ref, dz_ref, dw_ref):
        i = pl.program_id(0)
        z, yv = z_ref[...], y_ref[...]
        sig = _sigmoid(z)
        sz = z * sig
        g = yv * sz
        r = lax.rsqrt(jnp.mean(g * g, axis=-1, keepdims=True) + RMS_EPS)
        ghat = g * r
        dn = d_ref[...]
        dghat = dn * w_ref[...]
        dg = r * (dghat - ghat * jnp.mean(dghat * ghat, axis=-1, keepdims=True))
        dy_ref[...] = dg * sz
        dz_ref[...] = (dg * yv * sig * (1.0 + z * (1.0 - sig))).astype(BF16)

        @pl.when(i == 0)
        def _():
            dw_ref[...] = jnp.zeros_like(dw_ref)

        dw_ref[...] += jnp.sum(dn * ghat, axis=0, keepdims=True)

    row_spec = pl.BlockSpec((tm, D_INNER), lambda i: (i, 0))
    vec_spec = pl.BlockSpec((1, D_INNER), lambda i: (0, 0))
    return pl.pallas_call(
        body, name=name,
        out_shape=(jax.ShapeDtypeStruct((rows, D_INNER), F32), jax.ShapeDtypeStruct((rows, D_MAIN), BF16),
                   jax.ShapeDtypeStruct((1, D_INNER), F32)),
        grid=(rows // tm,), in_specs=[row_spec, row_spec, row_spec, vec_spec],
        out_specs=(row_spec, row_spec, vec_spec), compiler_params=_cparams(("arbitrary",)),
    )(dyn, y, zx, w)


def _split3(x):
    hi = x.astype(BF16)
    r1 = x - hi.astype(F32)
    mid = r1.astype(BF16)
    lo = (r1 - mid.astype(F32)).astype(BF16)
    return hi, mid, lo


def _dot3_data_lhs(x, sel):
    sel16 = sel.astype(F32).astype(BF16)
    hi, mid, lo = _split3(x)
    return _dot(hi, sel16) + _dot(mid, sel16) + _dot(lo, sel16)


def _dot2_data_lhs(x, sel):
    sel16 = sel.astype(F32).astype(BF16)
    hi = x.astype(BF16)
    mid = (x - hi.astype(F32)).astype(BF16)
    return _dot(hi, sel16) + _dot(mid, sel16)


def _dot3_data_rhs(sel, x):
    sel16 = sel.astype(F32).astype(BF16)
    hi, mid, lo = _split3(x)
    return _dot(sel16, hi) + _dot(sel16, mid) + _dot(sel16, lo)


def _causal_masks():
    r = lax.broadcasted_iota(jnp.int32, (CHUNK, CHUNK), 0)
    c = lax.broadcasted_iota(jnp.int32, (CHUNK, CHUNK), 1)
    return r >= c, r <= c


def _expand_heads_matrix(g):
    k = lax.broadcasted_iota(jnp.int32, (LANES, GROUP_W), 0)
    j = lax.broadcasted_iota(jnp.int32, (LANES, GROUP_W), 1)
    return HEADS_PER_GROUP * g + jnp.right_shift(j, 6) == k


def _reduce_heads_matrix(g):
    j = lax.broadcasted_iota(jnp.int32, (GROUP_W, LANES), 0)
    k = lax.broadcasted_iota(jnp.int32, (GROUP_W, LANES), 1)
    return HEADS_PER_GROUP * g + jnp.right_shift(j, 6) == k


def _reduce_pair_matrix(g, p):
    j = lax.broadcasted_iota(jnp.int32, (LANES, LANES), 0)
    k = lax.broadcasted_iota(jnp.int32, (LANES, LANES), 1)
    return HEADS_PER_GROUP * g + 2 * p + jnp.right_shift(j, 6) == k


def _group_cols(ref, g, width):
    return ref.at[:, pl.ds(g * width, width)]


def _ssd_prep(name, dt, a128, steps=()):
    rows = dt.shape[0]
    nc = rows // CHUNK

    def body(dt_ref, a_ref, dte_ref, acs_ref, acst_ref):
        causal, _ = _causal_masks()
        dtv = dt_ref[...]
        acs = _dot3_data_rhs(causal, dtv) * a_ref[...]
        acst_ref[...] = acs.T[0:SSM_HEADS]
        for g in range(N_GROUPS):
            expand = _expand_heads_matrix(g)
            _group_cols(dte_ref, g, GROUP_W)[...] = _dot3_data_lhs(dtv, expand)
            _group_cols(acs_ref, g, GROUP_W)[...] = _dot3_data_lhs(acs, expand)

    blk = pl.BlockSpec((CHUNK, D_INNER), lambda c: (c, 0))
    shp = jax.ShapeDtypeStruct((rows, D_INNER), F32)
    return _call(
        body, name=name, out_shape=(shp, shp, jax.ShapeDtypeStruct((nc, SSM_HEADS, CHUNK), F32)), grid=(nc,),
        in_specs=[pl.BlockSpec((CHUNK, LANES), lambda c: (c, 0)), pl.BlockSpec((1, LANES), lambda c: (0, 0))],
        out_specs=(blk, blk, pl.BlockSpec((None, SSM_HEADS, CHUNK), lambda c: (c, 0, 0))),
        operands=[dt, a128], semantics=("parallel",), steps=steps)


def _ssd_common(x_ref, b_ref, c_ref, dte_ref, acs_ref):
    x = x_ref[...]
    dt_exp = dte_ref[...]
    acs_exp = acs_ref[...]
    tot_exp = acs_ref[pl.ds(CHUNK - 1, 1), :]
    xdt = x * dt_exp
    e_exp = jnp.exp(acs_exp)
    f_exp = jnp.exp(tot_exp - acs_exp)
    return _causal_masks(), x, dt_exp, acs_exp, tot_exp, xdt, e_exp, f_exp, b_ref[...], c_ref[...]


def _pair_decay(acs_pair, acs_row, e, causal):
    lane = lax.broadcasted_iota(jnp.int32, (CHUNK, LANES), 1)
    mine = (lane < HEAD_DIM) if e == 0 else (lane >= HEAD_DIM)
    a_l = jnp.where(mine, acs_pair, pltpu.roll(acs_pair, HEAD_DIM, 1))
    seg = a_l - acs_row
    dm = jnp.where(causal[0], jnp.exp(jnp.minimum(seg, 0.0)), 0.0)
    dmt = jnp.where(causal[1], jnp.exp(jnp.minimum(-seg, 0.0)), 0.0)
    return dm, dmt


def _ssd_specs(index_of_chunk):
    wide = pl.BlockSpec((CHUNK, D_INNER), lambda c: (index_of_chunk(c), 0))
    b_spec = pl.BlockSpec((CHUNK, D_BC), lambda c: (index_of_chunk(c), D_INNER // D_BC))
    c_spec = pl.BlockSpec((CHUNK, D_BC), lambda c: (index_of_chunk(c), D_INNER // D_BC + 1))
    rows_spec = pl.BlockSpec((None, SSM_HEADS, CHUNK), lambda c: (index_of_chunk(c), 0, 0))
    state_spec = pl.BlockSpec((N_GROUPS, None, D_STATE, GROUP_W), lambda c: (0, index_of_chunk(c), 0, 0))
    return wide, b_spec, c_spec, rows_spec, state_spec


def _ssd_fwd(name, xbc, dt_exp, acs_exp, acs_rows, dskexp, steps=()):
    rows = xbc.shape[0]
    nc = rows // CHUNK

    def body(x_ref, b_ref, c_ref, dte_ref, acs_ref, acst_ref, dsk_ref, y_ref, st_ref, s_scr):
        @pl.when(pl.program_id(0) == 0)
        def _():
            s_scr[...] = jnp.zeros_like(s_scr)

        lane = lax.broadcasted_iota(jnp.int32, (CHUNK, LANES), 1)
        for g in range(N_GROUPS):
            y_g = _group_cols(y_ref, g, GROUP_W)
            causal, x, _, acs_exp_v, tot_exp, xdt, e_exp, f_exp, bm, cm = _ssd_common(
                _group_cols(x_ref, g, GROUP_W), _group_cols(b_ref, g, D_STATE), _group_cols(c_ref, g, D_STATE),
                _group_cols(dte_ref, g, GROUP_W), _group_cols(acs_ref, g, GROUP_W))
            state = s_scr[g]
            st_ref[g] = state
            cb16, bb16 = cm.astype(BF16), bm.astype(BF16)
            cb = _dot_nt(cb16, bb16)
            base = e_exp * _dot(cb16, state.astype(BF16)) + _group_cols(dsk_ref, g, GROUP_W)[...] * x
            for p in range(HEADS_PER_GROUP // 2):
                sl = slice(p * LANES, (p + 1) * LANES)
                xp = xdt[:, sl].astype(BF16)
                yd = []
                for e in range(2):
                    acs_row = acst_ref[pl.ds(g * HEADS_PER_GROUP + 2 * p + e, 1), :]
                    dm, _ = _pair_decay(acs_exp_v[:, sl], acs_row, e, causal)
                    yd.append(_dot((cb * dm).astype(BF16), xp))
                y_g[:, sl] = jnp.where(lane < HEAD_DIM, yd[0], yd[1]) + base[:, sl]
            s_scr[g] = jnp.exp(tot_exp) * state + _dot_tn(bb16, (f_exp * xdt).astype(BF16))

    wide, b_spec, c_spec, rows_spec, state_spec = _ssd_specs(lambda c: c)
    return _call(
        body, name=name,
        out_shape=(jax.ShapeDtypeStruct((rows, D_INNER), F32),
                   jax.ShapeDtypeStruct((N_GROUPS, nc, D_STATE, GROUP_W), F32)),
        grid=(nc,),
        in_specs=[wide, b_spec, c_spec, wide, wide, rows_spec, pl.BlockSpec((1, D_INNER), lambda c: (0, 0))],
        out_specs=(wide, state_spec),
        scratch_shapes=[pltpu.VMEM((N_GROUPS, D_STATE, GROUP_W), F32)],
        operands=[xbc, xbc, xbc, dt_exp, acs_exp, acs_rows, dskexp], semantics=("arbitrary",), steps=steps)


def _ssd_bwd(name, xbc, dt_exp, acs_exp, acs_rows, dt, a128, dskexp, dy, states, steps=()):
    rows = xbc.shape[0]
    nc = rows // CHUNK
    last = nc - 1

    def body(x_ref, b_ref, c_ref, dte_ref, acs_ref, acst_ref, dt_ref, a128_ref, dsk_all, dy_all, st_all,
             dx_all, db_all, dc_all, ddt_ref, dalog_ref, ddsk_ref, ds_all):
        @pl.when(pl.program_id(0) == 0)
        def _():
            ds_all[...] = jnp.zeros_like(ds_all)
            dalog_ref[...] = jnp.zeros_like(dalog_ref)
            ddsk_ref[...] = jnp.zeros_like(ddsk_ref)

        dacs = jnp.zeros((CHUNK, LANES), F32)
        ddt_x = jnp.zeros((CHUNK, LANES), F32)
        for g in range(N_GROUPS):
            dacs_g, ddt_x_g = group(
                g, _group_cols(x_ref, g, GROUP_W), _group_cols(b_ref, g, D_STATE), _group_cols(c_ref, g, D_STATE),
                _group_cols(dte_ref, g, GROUP_W), _group_cols(acs_ref, g, GROUP_W), acst_ref,
                _group_cols(dsk_all, g, GROUP_W), _group_cols(dy_all, g, GROUP_W), st_all.at[g],
                _group_cols(dx_all, g, GROUP_W), _group_cols(db_all, g, D_STATE), _group_cols(dc_all, g, D_STATE),
                ddsk_ref, ds_all.at[g])
            dacs, ddt_x = dacs + dacs_g, ddt_x + ddt_x_g
        _, causal_t = _causal_masks()
        da = _dot3_data_rhs(causal_t, dacs)
        ddt_ref[...] = da * a128_ref[...] + ddt_x
        dalog_ref[...] += jnp.sum(da * dt_ref[...], axis=0, keepdims=True) * a128_ref[...]

    def group(g, x_ref, b_ref, c_ref, dte_ref, acs_ref, acst_ref, dsk_ref, dy_ref, st_ref,
              dx_ref, db_ref, dc_ref, ddsk_ref, ds_scr):
        causal, x, dt_exp, acs_exp_v, tot_exp, xdt, e_exp, f_exp, bm, cm = _ssd_common(
            x_ref, b_ref, c_ref, dte_ref, acs_ref)
        reduce_heads = _reduce_heads_matrix(g)
        state, dstate = st_ref[...], ds_scr[...]
        dyv = dy_ref[...]
        cb16, bb16 = cm.astype(BF16), bm.astype(BF16)
        s16, ds16 = state.astype(BF16), dstate.astype(BF16)
        cb = _dot_nt(cb16, bb16)
        cbt = _dot_nt(bb16, cb16)
        cs = _dot(cb16, s16)
        bds = _dot(bb16, ds16)
        edy = e_exp * dyv
        fx = f_exp * xdt
        dxdt_base = f_exp * bds
        dc_acc = _dot_nt(edy.astype(BF16), s16)
        db_acc = _dot_nt(fx.astype(BF16), ds16)
        ds_scr[...] = jnp.exp(tot_exp) * dstate + _dot_tn(cb16, edy.astype(BF16))
        q = fx * bds
        dacs = _dot2_data_lhs(edy * cs - q, reduce_heads)
        dtot = jnp.sum(_dot2_data_lhs(q + jnp.exp(tot_exp) * dstate * state, reduce_heads), axis=0, keepdims=True)
        ddsk_ref[...] += jnp.sum(_dot2_data_lhs(dyv * x, reduce_heads), axis=0, keepdims=True)
        lane = lax.broadcasted_iota(jnp.int32, (CHUNK, LANES), 1)
        dcb = jnp.zeros((CHUNK, CHUNK), F32)
        dcbt = jnp.zeros((CHUNK, CHUNK), F32)
        ddt_x = jnp.zeros((CHUNK, LANES), F32)
        for p in range(HEADS_PER_GROUP // 2):
            sl = slice(p * LANES, (p + 1) * LANES)
            xp, dyp = xdt[:, sl], dyv[:, sl]
            xp16, dyp16 = xp.astype(BF16), dyp.astype(BF16)
            dxh = []
            for e in range(2):
                h = 2 * p + e
                mine = (lane < HEAD_DIM) if e == 0 else (lane >= HEAD_DIM)
                acs_row = acst_ref[pl.ds(g * HEADS_PER_GROUP + h, 1), :]
                dm, dmt = _pair_decay(acs_exp_v[:, sl], acs_row, e, causal)
                m, mt = cb * dm, cbt * dmt
                xh16 = jnp.where(mine, xp, 0.0).astype(BF16)
                dyh16 = jnp.where(mine, dyp, 0.0).astype(BF16)
                d_m = _dot_nt(dyh16, xp16)
                d_mt = _dot_nt(xh16, dyp16)
                dacs_h = (jnp.sum(d_m * m, axis=-1, keepdims=True)
                          - jnp.sum(d_mt * mt, axis=-1, keepdims=True))
                dacs = dacs + jnp.where(lane == HEADS_PER_GROUP * g + h, dacs_h, 0.0)
                dcb = dcb + d_m * dm
                dcbt = dcbt + d_mt * dmt
                dxh.append(_dot(mt.astype(BF16), dyp16))
            dxdt = jnp.where(lane < HEAD_DIM, dxh[0], dxh[1]) + dxdt_base[:, sl]
            dx_ref[:, sl] = dxdt * dt_exp[:, sl] + dsk_ref[:, sl] * dyp
            ddt_x = ddt_x + _dot2_data_lhs(dxdt * x[:, sl], _reduce_pair_matrix(g, p))
        dc_ref[...] = dc_acc + _dot(dcb.astype(BF16), bb16)
        db_ref[...] = db_acc + _dot(dcbt.astype(BF16), cb16)
        row = lax.broadcasted_iota(jnp.int32, (CHUNK, LANES), 0)
        return dacs + jnp.where(row == CHUNK - 1, dtot, 0.0), ddt_x

    wide, b_spec, c_spec, rows_spec, state_spec = _ssd_specs(lambda c: last - c)
    heads_spec = pl.BlockSpec((CHUNK, LANES), lambda c: (last - c, 0))
    vec_spec = pl.BlockSpec((1, LANES), lambda c: (0, 0))
    bc_out = pl.BlockSpec((CHUNK, D_BC), lambda c: (last - c, 0))
    vec_shape = jax.ShapeDtypeStruct((1, LANES), F32)
    return _call(
        body, name=name,
        out_shape=(jax.ShapeDtypeStruct((rows, D_INNER), F32), jax.ShapeDtypeStruct((rows, D_BC), F32),
                   jax.ShapeDtypeStruct((rows, D_BC), F32), jax.ShapeDtypeStruct((rows, LANES), F32),
                   vec_shape, vec_shape),
        grid=(nc,),
        in_specs=[wide, b_spec, c_spec, wide, wide, rows_spec, heads_spec, vec_spec,
                  pl.BlockSpec((1, D_INNER), lambda c: (0, 0)), wide, state_spec],
        out_specs=(wide, bc_out, bc_out, heads_spec, vec_spec, vec_spec),
        scratch_shapes=[pltpu.VMEM((N_GROUPS, D_STATE, GROUP_W), F32)],
        operands=[xbc, xbc, xbc, dt_exp, acs_exp, acs_rows, dt, a128, dskexp, dy, states],
        semantics=("arbitrary",), steps=steps)


def _attn_visible(b, heads=1):
    row = jnp.bitwise_and(lax.broadcasted_iota(jnp.int32, (heads * CHUNK, 3 * CHUNK), 0), CHUNK - 1)
    col = lax.broadcasted_iota(jnp.int32, (heads * CHUNK, 3 * CHUNK), 1)
    bb = b + jnp.zeros_like(col)
    meta = (col < CHUNK) & (bb >= 1) & (col >= PAD_ROWS)
    prev = (col >= CHUNK) & (col < 2 * CHUNK) & (bb >= 2) & ((col - CHUNK) > row)
    cur = (col >= 2 * CHUNK) & ((col - 2 * CHUNK) <= row) & ((bb >= 1) | ((col - 2 * CHUNK) >= PAD_ROWS))
    return meta | prev | cur


def _attn_visible4(b):
    return _attn_visible(b, 4)


def _stack_heads(q_ref, sink_ref, kvh, scale):
    lane = lax.broadcasted_iota(jnp.int32, (CHUNK, LANES), 1)
    parts, sinks = [], []
    for pp in range(2):
        pair = kvh * 2 + pp
        qp = q_ref[:, pair * LANES:(pair + 1) * LANES] * scale
        for e in range(2):
            mine = (lane < HEAD_DIM) if e == 0 else (lane >= HEAD_DIM)
            parts.append(jnp.where(mine, qp, 0.0).astype(BF16))
            sinks.append(jnp.full((CHUNK, 1), sink_ref[2 * pair + e], F32))
    return jnp.concatenate(parts, axis=0), jnp.concatenate(sinks, axis=0)


def _attn_operands(q_ref, k0, kp, kc, v0, vp, vc, sink_ref):
    kcat, vcat, q4, sink4 = [], [], [], []
    for kvh in range(N_KV_HEADS):
        ksl = slice(kvh * LANES, (kvh + 1) * LANES)
        kcat.append(jnp.concatenate([k0[:, ksl], kp[:, ksl], kc[:, ksl]], axis=0).astype(BF16))
        vcat.append(jnp.concatenate([v0[:, ksl], vp[:, ksl], vc[:, ksl]], axis=0).astype(BF16))
        stacked, sinks = _stack_heads(q_ref, sink_ref, kvh, ATTN_SCALE)
        q4.append(stacked)
        sink4.append(sinks)
    return kcat, vcat, q4, sink4


def _attn_probs(q4, kcat, visible, sink4):
    heads = range(N_KV_HEADS)
    s = [jnp.where(visible, _dot_nt(q4[h], kcat[h]), NEG_INF) for h in heads]
    m = [jnp.maximum(jnp.max(s[h], axis=-1, keepdims=True), sink4[h]) for h in heads]
    pe = [jnp.exp(s[h] - m[h]) for h in heads]
    pe_sink = [jnp.exp(sink4[h] - m[h]) for h in heads]
    inv = [1.0 / (jnp.sum(pe[h], axis=-1, keepdims=True) + pe_sink[h]) for h in heads]
    return [pe[h] * inv[h] for h in heads], [pe_sink[h] * inv[h] for h in heads]


def _unstack_pairs(stacked, pp):
    lane = lax.broadcasted_iota(jnp.int32, (CHUNK, LANES), 1)
    return jnp.where(lane < HEAD_DIM, stacked[(2 * pp) * CHUNK:(2 * pp + 1) * CHUNK],
                     stacked[(2 * pp + 1) * CHUNK:(2 * pp + 2) * CHUNK])


def _attn_specs(colblock):
    blk = lambda f: pl.BlockSpec((CHUNK, 2 * D_KV), f)
    return [blk(lambda b: (0, colblock)), blk(lambda b: (jnp.maximum(b - 1, 0), colblock)), blk(lambda b: (b, colblock))]


def _attn_fwd(name, q, kv2, sinks, steps=()):
    rows = q.shape[0]

    def body(q_ref, k0, kp, kc, v0, vp, vc, sink_ref, o_ref):
        visible = _attn_visible4(pl.program_id(0))
        kcat, vcat, q4, sink4 = _attn_operands(q_ref, k0, kp, kc, v0, vp, vc, sink_ref)
        pn, _ = _attn_probs(q4, kcat, visible, sink4)
        o4 = [_dot(pn[h].astype(BF16), vcat[h]) for h in range(N_KV_HEADS)]
        for kvh in range(N_KV_HEADS):
            for pp in range(2):
                qsl = slice((kvh * 2 + pp) * LANES, (kvh * 2 + pp + 1) * LANES)
                o_ref[:, qsl] = _unstack_pairs(o4[kvh], pp).astype(BF16)

    return _call(
        body, name=name, out_shape=jax.ShapeDtypeStruct((rows, D_MODEL), BF16), grid=(rows // CHUNK,),
        in_specs=[pl.BlockSpec((CHUNK, D_MODEL), lambda b: (b, 0))] + _attn_specs(0) + _attn_specs(1)
        + [pl.BlockSpec(memory_space=pltpu.SMEM)],
        out_specs=pl.BlockSpec((CHUNK, D_MODEL), lambda b: (b, 0)),
        operands=[q, kv2, kv2, kv2, kv2, kv2, kv2, sinks], semantics=("parallel",), steps=steps)


def _attn_bwd(name, q, kv2, sinks, do, steps=()):
    rows = q.shape[0]

    def body(q_ref, k0, kp, kc, v0, vp, vc, sink_ref, do_ref,
             dq_ref, dkc_ref, dkp_ref, dvc_ref, dvp_ref, dkm_ref, dvm_ref, dsink_ref):
        @pl.when(pl.program_id(0) == 0)
        def _():
            dkm_ref[...] = jnp.zeros_like(dkm_ref)
            dvm_ref[...] = jnp.zeros_like(dvm_ref)
            dsink_ref[...] = jnp.zeros_like(dsink_ref)

        visible = _attn_visible4(pl.program_id(0))
        heads = range(N_KV_HEADS)
        lane1 = lax.broadcasted_iota(jnp.int32, (1, LANES), 1)
        kcat, vcat, q4, sink4 = _attn_operands(q_ref, k0, kp, kc, v0, vp, vc, sink_ref)
        do4 = [_stack_heads(do_ref, sink_ref, h, 1.0)[0] for h in heads]
        pn, psink = _attn_probs(q4, kcat, visible, sink4)
        dp = [_dot_nt(do4[h], vcat[h]) for h in heads]
        delta = [jnp.sum(pn[h] * dp[h], axis=-1, keepdims=True) for h in heads]
        ds16 = [(pn[h] * (dp[h] - delta[h])).astype(BF16) for h in heads]
        dq4 = [_dot(ds16[h], kcat[h]) for h in heads]
        dk_acc = [_dot_tn(ds16[h], q4[h]) for h in heads]
        dv_acc = [_dot_tn(pn[h].astype(BF16), do4[h]) for h in heads]
        dsink = jnp.zeros((1, LANES), F32)
        for kvh in heads:
            ksl = slice(kvh * LANES, (kvh + 1) * LANES)
            sink_terms = psink[kvh] * delta[kvh]
            for j in range(4):
                part = jnp.sum(sink_terms[j * CHUNK:(j + 1) * CHUNK], axis=0, keepdims=True)
                dsink = dsink - jnp.where(lane1 == kvh * 4 + j, part, 0.0)
            for pp in range(2):
                qsl = slice((kvh * 2 + pp) * LANES, (kvh * 2 + pp + 1) * LANES)
                dq_ref[:, qsl] = (_unstack_pairs(dq4[kvh], pp) * ATTN_SCALE).astype(BF16)
            dkm_ref[:, ksl] += dk_acc[kvh][0:CHUNK]
            dvm_ref[:, ksl] += dv_acc[kvh][0:CHUNK]
            dkp_ref[:, ksl] = dk_acc[kvh][CHUNK:2 * CHUNK]
            dvp_ref[:, ksl] = dv_acc[kvh][CHUNK:2 * CHUNK]
            dkc_ref[:, ksl] = dk_acc[kvh][2 * CHUNK:3 * CHUNK]
            dvc_ref[:, ksl] = dv_acc[kvh][2 * CHUNK:3 * CHUNK]
        dsink_ref[...] += dsink

    qspec = pl.BlockSpec((CHUNK, D_MODEL), lambda b: (b, 0))
    kvspec = pl.BlockSpec((CHUNK, 2 * D_KV), lambda b: (b, 0))
    fixed = pl.BlockSpec((CHUNK, 2 * D_KV), lambda b: (0, 0))
    kv_shape = jax.ShapeDtypeStruct((rows, 2 * D_KV), F32)
    meta_shape = jax.ShapeDtypeStruct((CHUNK, 2 * D_KV), F32)
    return _call(
        body, name=name,
        out_shape=(jax.ShapeDtypeStruct((rows, D_MODEL), BF16), kv_shape, kv_shape, kv_shape, kv_shape,
                   meta_shape, meta_shape, jax.ShapeDtypeStruct((1, LANES), F32)),
        grid=(rows // CHUNK,),
        in_specs=[qspec] + _attn_specs(0) + _attn_specs(1) + [pl.BlockSpec(memory_space=pltpu.SMEM), qspec],
        out_specs=(qspec, kvspec, kvspec, kvspec, kvspec, fixed, fixed, pl.BlockSpec((1, LANES), lambda b: (0, 0))),
        operands=[q, kv2, kv2, kv2, kv2, kv2, kv2, sinks, do], semantics=("arbitrary",), steps=steps)


def _kv_grad_combine(name, dk_cur, dk_prev, dk_meta, dv_cur, dv_prev, dv_meta):
    rows = dk_cur.shape[0]
    nb = rows // CHUNK
    width = 2 * D_KV

    def body(kc_ref, kp_ref, km_ref, vc_ref, vp_ref, vm_ref, o_ref):
        jj = pl.program_id(0) + jnp.zeros((CHUNK, 1), jnp.int32)
        for half, (c_ref, p_ref, m_ref) in enumerate(((kc_ref, kp_ref, km_ref), (vc_ref, vp_ref, vm_ref))):
            total = c_ref[...] + jnp.where(jj < nb - 1, p_ref[...], 0.0) + jnp.where(jj == 0, m_ref[...], 0.0)
            o_ref[:, half * width:(half + 1) * width] = total.astype(BF16)

    blk = lambda f: pl.BlockSpec((CHUNK, width), f)
    three = lambda: [blk(lambda j: (j, 0)), blk(lambda j: (jnp.minimum(j + 1, nb - 1), 0)), blk(lambda j: (0, 0))]
    return pl.pallas_call(
        body, name=name, out_shape=jax.ShapeDtypeStruct((rows, 2 * width), BF16), grid=(nb,),
        in_specs=three() + three(), out_specs=pl.BlockSpec((CHUNK, 2 * width), lambda j: (j, 0)),
        compiler_params=_cparams(("parallel",)),
    )(dk_cur, dk_prev, dk_meta, dv_cur, dv_prev, dv_meta)


def _adamw(name, w, g, m, v, steps=()):
    rows, width = w.shape
    tr = rows
    for cand in range(8, rows + 1, 8):
        if rows % cand == 0 and cand * width * 4 <= (1 << 20):
            tr = cand

    def body(*refs):
        _adamw_update(*refs)

    blk = pl.BlockSpec((tr, width), lambda i: (i, 0))
    shp = jax.ShapeDtypeStruct((rows, width), F32)
    return _call(body, name=name, out_shape=(shp, shp, shp), grid=(rows // tr,), in_specs=[blk] * 4,
                 out_specs=(blk,) * 3, operands=[w, g, m, v], semantics=("parallel",), steps=steps)


def _adamw_update(w_ref, g_ref, m_ref, v_ref, d_ref, mo_ref, vo_ref):
    gv = g_ref[...]
    mn = ADAM_B1 * m_ref[...] + (1.0 - ADAM_B1) * gv
    vn = ADAM_B2 * v_ref[...] + (1.0 - ADAM_B2) * (gv * gv)
    m_hat = mn / (1.0 - ADAM_B1 ** ADAM_STEP)
    v_hat = vn / (1.0 - ADAM_B2 ** ADAM_STEP)
    d_ref[...] = -ADAM_LR * (m_hat / (jnp.sqrt(v_hat) + ADAM_EPS) + ADAM_WD * w_ref[...])
    mo_ref[...] = mn
    vo_ref[...] = vn


def _adamw_small(name, ws, gs, ms, vs):
    n = len(ws)

    def body(*refs):
        for i in range(n):
            _adamw_update(*refs[i::n])

    shapes = [jax.ShapeDtypeStruct(a.shape, F32) for a in ws]
    outs = pl.pallas_call(body, name=name, out_shape=shapes * 3, in_specs=[VMEM_SPEC] * (4 * n),
                          out_specs=[VMEM_SPEC] * (3 * n), compiler_params=_cparams())(*ws, *gs, *ms, *vs)
    return outs[:n], outs[n:2 * n], outs[2 * n:]


def _ffn_fwd(tag, h, hn, p, i, plan):
    up_g, up_v, act = _ffn_up_conv(f"ffn{tag}_up", hn, plan.weight("f_w_up", i), p["f_conv_w"][i],
                                   p["f_conv_b"][i:i + 1], steps=plan.steps(f"ffn{tag}_up"))
    pre = _mm(f"ffn{tag}_down", act, plan.weight("f_w_down", i), "nn", steps=plan.steps(f"ffn{tag}_down"))
    return pre, (h, hn, up_g, up_v, act, pre)


def _ffn_bwd(tag, dpre, saved, p, i, plan):
    h, hn, up_g, up_v, act, pre = saved
    plan.grad("f_w_down", i, _mm(f"ffn{tag}_down_dw", act, dpre, "tn", out_dtype=BF16))
    dact = _mm(f"ffn{tag}_down_dx", dpre, plan.weight("f_w_down", i), "nt", steps=plan.steps(f"ffn{tag}_down_dx"))
    gwg, gwv, gbg, gbv, dhn, g_up = _ffn_conv_bwd(
        f"ffn{tag}_conv_bwd", up_g, up_v, dact, p["f_conv_w"][i], p["f_conv_b"][i:i + 1], hn,
        plan.weight("f_w_up", i), steps=plan.steps(f"ffn{tag}_conv_bwd"))
    g_cw, g_cb = jnp.concatenate([gwg, gwv], axis=1), jnp.concatenate([gbg, gbv], axis=1)
    plan.grad("f_w_up", i, g_up)
    return dhn, dict(f_conv_w=g_cw, f_conv_b=g_cb)


def _lanes_pad(a, width=LANES):
    return jnp.pad(a, [(0, 0)] * (a.ndim - 1) + [(0, width - a.shape[-1])])


def _dup_heads(w):
    rows = w.shape[0]
    w = w.reshape(rows, 2 * N_KV_HEADS, 1, HEAD_DIM)
    return jnp.broadcast_to(w, (rows, 2 * N_KV_HEADS, 2, HEAD_DIM)).reshape(rows, 4 * D_KV)


def _undup_heads(g):
    rows = g.shape[0]
    return g.reshape(rows, 2 * N_KV_HEADS, 2, HEAD_DIM).sum(axis=2).reshape(rows, 2 * D_KV)


def _local_step(x2, target, p, plan):
    seq = x2.shape[0]
    rows = seq + CHUNK
    g = {}

    h0 = jnp.concatenate([jnp.zeros((PAD_ROWS, D_MODEL), F32), p["meta_tokens"], x2], axis=0)

    w_in = plan.weight("a_w_in")
    w_dt = jnp.pad(w_in[D_MAIN:], ((0, LANES - SSM_HEADS), (0, 0)))
    dt_bias = _lanes_pad(p["a_dt_bias"])
    a128 = _lanes_pad(-jnp.exp(p["a_a_log"]))
    dskexp = jnp.repeat(p["a_d_skip"].reshape(SSM_HEADS), HEAD_DIM).reshape(1, D_INNER)

    hn0 = _rms_fwd("a_norm", h0, p["a_norm_pre"])
    zx = _mm("a_in_main", hn0, w_in, "nt", k_rows=D_MAIN, steps=plan.steps("a_in_main"))
    dtr = _mm("a_in_dt", hn0, w_dt, "nt")
    xbc = _conv4_fwd("a_conv", zx, p["a_conv_w"], p["a_conv_b"], steps=plan.steps("a_conv"))
    dt = _dt_fwd("a_dt", dtr, dt_bias)
    dt_exp, acs_exp, acs_rows = _ssd_prep("a_ssd_prep", dt, a128, steps=plan.steps("a_ssd_prep"))
    y, states = _ssd_fwd("a_ssd", xbc, dt_exp, acs_exp, acs_rows, dskexp, steps=plan.steps("a_ssd"))
    yn = _gate_fwd("a_gate", y, zx, p["a_gate_norm"], steps=plan.steps("a_gate"))
    mix = _mm("a_out", yn, plan.weight("a_w_out"), "nn", steps=plan.steps("a_out"))
    h1, (hn_f0,) = _resid_norm_fwd("a_resid", h0, mix, p["a_norm_post"], [p["f_norm_pre"][0:1]])

    pre_f0, ffn0 = _ffn_fwd("0", h1, hn_f0, p, 0, plan)
    h2, (hkv, hn2) = _resid_norm_fwd("ffn0_resid", h1, pre_f0, p["f_norm_post"][0:1], [p["kv_norm"], p["b_norm_pre"]])

    w_kv2 = _dup_heads(plan.weight("w_kv"))
    kv2 = _mm("kv_proj", hkv, w_kv2, "nn")
    q = _mm("b_q", hn2, plan.weight("b_w_q"), "nn")
    sinks = p["b_sinks"].reshape(N_Q_HEADS)
    o = _attn_fwd("b_attn", q, kv2, sinks, steps=plan.steps("b_attn"))
    attn = _mm("b_o", o, plan.weight("b_w_o"), "nn", steps=plan.steps("b_o"))
    h3, (hn_f1,) = _resid_norm_fwd("b_resid", h2, attn, p["b_norm_post"], [p["f_norm_pre"][1:2]])

    pre_f1, ffn1 = _ffn_fwd("1", h3, hn_f1, p, 1, plan)
    dh, loss_vec, dpre_f1, g_post1 = _resid_norm_loss("ffn1_resid_loss", h3, pre_f1, p["f_norm_post"][1:2], target)
    loss = loss_vec[0, 0]

    dhn_f1, g1 = _ffn_bwd("1", dpre_f1, ffn1, p, 1, plan)
    dh, g_pre1, dpre, g["b_norm_post"] = _norm_bwd_add("ffn1_norm_bwd", dh, dhn_f1, h3, p["f_norm_pre"][1:2],
                                                        then=(attn, p["b_norm_post"]))
    plan.grad("b_w_o", None, _mm("b_o_dw", o, dpre, "tn", out_dtype=BF16))
    do = _mm("b_o_dx", dpre, plan.weight("b_w_o"), "nt", steps=plan.steps("b_o_dx"))
    dq, dkc, dkp, dvc, dvp, dkm, dvm, dsink = _attn_bwd("b_attn_bwd", q, kv2, sinks, do, steps=plan.steps("b_attn_bwd"))
    g["b_sinks"] = dsink[:, :N_Q_HEADS]
    dhn2 = _mm("b_q_dx", dq, plan.weight("b_w_q"), "nt")
    plan.grad("b_w_q", None, _mm("b_q_dw", hn2, dq, "tn", out_dtype=BF16))
    dh, g["b_norm_pre"] = _norm_bwd_add("b_norm_bwd", dh, dhn2, h2, p["b_norm_pre"])
    dkv2 = _kv_grad_combine("kv_grad", dkc, dkp, dkm, dvc, dvp, dvm)
    dhkv = _mm("kv_proj_dx", dkv2, w_kv2, "nt")
    plan.grad("w_kv", None, _undup_heads(_mm("kv_proj_dw", hkv, dkv2, "tn")))
    dh, g["kv_norm"], dpre_f0, g_post0 = _norm_bwd_add("kv_norm_bwd", dh, dhkv, h2, p["kv_norm"],
                                                       then=(pre_f0, p["f_norm_post"][0:1]))

    dhn_f0, g0 = _ffn_bwd("0", dpre_f0, ffn0, p, 0, plan)
    dh, g_pre0, dpre, g["a_norm_post"] = _norm_bwd_add("ffn0_norm_bwd", dh, dhn_f0, h1, p["f_norm_pre"][0:1],
                                                        then=(mix, p["a_norm_post"]))
    g["f_norm_post"] = jnp.concatenate([g_post0, g_post1], axis=0)
    g["f_norm_pre"] = jnp.concatenate([g_pre0, g_pre1], axis=0)
    g["f_conv_w"] = jnp.stack([g0["f_conv_w"], g1["f_conv_w"]])
    g["f_conv_b"] = jnp.concatenate([g0["f_conv_b"], g1["f_conv_b"]], axis=0)
    plan.grad("a_w_out", None, _mm("a_out_dw", yn, dpre, "tn", out_dtype=BF16))
    dyn = _mm("a_out_dx", dpre, plan.weight("a_w_out"), "nt", steps=plan.steps("a_out_dx"))
    dy, dzx, g["a_gate_norm"] = _gate_bwd("a_gate_bwd", dyn, y, zx, p["a_gate_norm"])
    dxs, dbm, dcm, ddt, dalog, ddsk = _ssd_bwd("a_ssd_bwd", xbc, dt_exp, acs_exp, acs_rows, dt, a128, dskexp, dy, states,
                                              steps=plan.steps("a_ssd_bwd"))
    g["a_a_log"] = dalog[:, :SSM_HEADS]
    g["a_d_skip"] = ddsk[:, :SSM_HEADS]
    ddtr, dbias = _dt_bwd("a_dt_bwd", ddt, dtr, dt_bias)
    g["a_dt_bias"] = dbias[:, :SSM_HEADS]
    dzx, gw_x, gb_x = _conv4_bwd("a_conv_bwd_x", zx, dxs, p["a_conv_w"], p["a_conv_b"], 0, dzx)
    dzx, gw_b, gb_b = _conv4_bwd("a_conv_bwd_b", zx, dbm, p["a_conv_w"], p["a_conv_b"], D_INNER, dzx)
    dzx, gw_c, gb_c = _conv4_bwd("a_conv_bwd_c", zx, dcm, p["a_conv_w"], p["a_conv_b"], D_INNER + D_BC, dzx)
    g["a_conv_w"] = jnp.concatenate([gw_x, gw_b, gw_c], axis=1)
    g["a_conv_b"] = jnp.concatenate([gb_x, gb_b, gb_c], axis=1)
    g_in = _mm("a_in_main_dw", dzx, hn0, "tn", out_dtype=BF16, out_rows=D_IN_PROJ, steps=plan.steps("a_in_main_dw"))
    plan.grad("a_w_in", None, _tn_rows_into("a_in_dt_dw", ddtr, hn0, g_in, D_MAIN, SSM_HEADS))
    dhn0 = _mm("a_in_dt_dx", ddtr, w_dt, "nn", steps=plan.steps("a_in_dt_dx"))
    dhn0 = _mm("a_in_main_dx", dzx, w_in, "nn", acc=dhn0, steps=plan.steps("a_in_main_dx"))
    dh, g["a_norm_pre"] = _norm_bwd_add("a_norm_bwd", dh, dhn0, h0, p["a_norm_pre"], steps=plan.steps("a_norm_bwd"))

    g["meta_tokens"] = dh[PAD_ROWS:CHUNK]
    return loss, dh[CHUNK:], g


ANY = pl.BlockSpec(memory_space=pl.ANY)
VMEM_SPEC = pl.BlockSpec(memory_space=pltpu.VMEM)


def _allgather_small(name, shard):
    rows = shard.shape[0]

    def body(s_ref, o_ref, send_sems, recv_sems):
        x, y, c = _place()
        me = 2 * x + y
        o_ref[me] = s_ref[...]
        chips = _other_chips(x, y)
        sends = [pltpu.make_async_remote_copy(s_ref, o_ref.at[me], send_sems.at[j], recv_sems.at[j],
                                              device_id=(cx, cy, c), device_id_type=MESH)
                 for j, (cx, cy) in enumerate(chips)]
        for cp in sends:
            cp.start()
        for j, (cx, cy) in enumerate(chips):
            pltpu.make_async_remote_copy(s_ref, o_ref.at[2 * cx + cy], send_sems.at[j], recv_sems.at[j],
                                         device_id=(cx, cy, c), device_id_type=MESH).wait_recv()
        for cp in sends:
            cp.wait_send()

    return pl.pallas_call(
        body, name=name, out_shape=jax.ShapeDtypeStruct((N_CHIPS, rows, LANES), F32),
        in_specs=[VMEM_SPEC], out_specs=VMEM_SPEC,
        scratch_shapes=[pltpu.SemaphoreType.DMA((3,)), pltpu.SemaphoreType.DMA((3,))],
        compiler_params=pltpu.CompilerParams(vmem_limit_bytes=VMEM_LIMIT),
    )(shard)


def _row_block(rows, width, itemsize, align, budget=2 << 20):
    best = rows
    for cand in range(align, rows + 1, align):
        if rows % cand == 0 and cand * width * itemsize <= budget:
            best = cand
    return best


def _cast_into_slot(name, chip, w, layer=None):
    rows, width = w.shape[-2:]
    tr = _row_block(rows, width, 4, 16)
    if layer is None:
        in_spec = pl.BlockSpec((tr, width), lambda i, chip_ref: (i, 0))
    else:
        in_spec = pl.BlockSpec((None, tr, width), lambda i, chip_ref: (layer, i, 0))

    def body(chip_ref, w_ref, o_ref):
        o_ref[...] = w_ref[...].astype(BF16)

    return pl.pallas_call(
        body, name=name, out_shape=jax.ShapeDtypeStruct((N_CHIPS, rows, width), BF16),
        grid_spec=pltpu.PrefetchScalarGridSpec(
            num_scalar_prefetch=1, grid=(rows // tr,), in_specs=[in_spec],
            out_specs=pl.BlockSpec((None, tr, width), lambda i, chip_ref: (chip_ref[0], i, 0))),
        compiler_params=_cparams(("parallel",)),
    )(chip, w)


def _allreduce_small(name, vec):
    rows = -(-vec.shape[0] // (2 * SUBLANES)) * (2 * SUBLANES)
    hr = rows // 2
    padded = jnp.pad(vec, ((0, rows - vec.shape[0]), (0, 0)))

    def body(v_ref, o_ref, theirs, pair, by_chip, send_sems, recv_sems):
        x, y, c = _place()
        me = 2 * x + y
        sibling = (x, y, 1 - c)
        mine = pl.ds(pl.multiple_of(c * hr, SUBLANES), hr)
        other = pl.ds(pl.multiple_of((1 - c) * hr, SUBLANES), hr)

        swap = _remote(v_ref, theirs, send_sems, recv_sems, 0, sibling)
        swap.start()
        swap.wait()
        south = (c + jnp.zeros((1, 1), jnp.int32)) == 0
        pair[...] = jnp.where(south, v_ref[...], theirs[...]) + jnp.where(south, theirs[...], v_ref[...])

        by_chip[me] = pair[mine, :]
        sends = [_remote(by_chip.at[me], by_chip.at[me], send_sems, recv_sems, 1 + j, (cx, cy, c))
                 for j, (cx, cy) in enumerate(_other_chips(x, y))]
        for cp in sends:
            cp.start()
        for j, (cx, cy) in enumerate(_other_chips(x, y)):
            _remote(by_chip.at[me], by_chip.at[2 * cx + cy], send_sems, recv_sems, 1 + j, (cx, cy, c)).wait_recv()
        for cp in sends:
            cp.wait_send()
        total = by_chip[0]
        for s in range(1, N_CHIPS):
            total = total + by_chip[s]

        o_ref[mine, :] = total
        back = _remote(o_ref.at[mine], o_ref.at[mine], send_sems, recv_sems, 4, sibling)
        back.start()
        _remote(o_ref.at[other], o_ref.at[other], send_sems, recv_sems, 4, sibling).wait_recv()
        back.wait_send()

    out = pl.pallas_call(
        body, name=name, out_shape=jax.ShapeDtypeStruct((rows, LANES), F32),
        in_specs=[VMEM_SPEC], out_specs=VMEM_SPEC,
        scratch_shapes=[pltpu.VMEM((rows, LANES), F32), pltpu.VMEM((rows, LANES), F32),
                        pltpu.VMEM((N_CHIPS, hr, LANES), F32), pltpu.SemaphoreType.DMA((5,)),
                        pltpu.SemaphoreType.DMA((5,))],
        compiler_params=pltpu.CompilerParams(vmem_limit_bytes=VMEM_LIMIT),
    )(padded)
    return out[:vec.shape[0]]


def _rs_pair_add(name, place, grads, partner, split="rows"):
    _, half_rows, width = partner.shape
    tr = _row_block(half_rows, width, 2, 16)
    nb = half_rows // tr
    if split == "rows":
        mine = pl.BlockSpec((None, tr, width), lambda s, i, pr: (s, pr[1] * nb + i, 0))
    else:
        mine = pl.BlockSpec((None, tr, width), lambda s, i, pr: (s, i, pr[1]))

    def body(place_ref, g_ref, p_ref, o_ref):
        o_ref[...] = (g_ref[...].astype(F32) + p_ref[...].astype(F32)).astype(BF16)

    return pl.pallas_call(
        body, name=name, out_shape=jax.ShapeDtypeStruct(partner.shape, BF16),
        grid_spec=pltpu.PrefetchScalarGridSpec(
            num_scalar_prefetch=1, grid=(N_CHIPS, nb),
            in_specs=[mine, pl.BlockSpec((None, tr, width), lambda s, i, pr: (s, i, 0))],
            out_specs=pl.BlockSpec((None, tr, width), lambda s, i, pr: (s, i, 0))),
        compiler_params=_cparams(("parallel", "parallel")),
    )(place, grads, partner)


def _rs_chip_add(name, place, mine, others, split="rows"):
    _, half_rows, width = mine.shape
    tr = _row_block(half_rows, width, 4, 16, budget=1 << 20)
    nb = half_rows // tr
    if split == "rows":
        out_shape, out_spec = (2 * half_rows, width), pl.BlockSpec((tr, width), lambda i, pr: (pr[1] * nb + i, 0))
    else:
        out_shape, out_spec = (half_rows, 2 * width), pl.BlockSpec((tr, width), lambda i, pr: (i, pr[1]))

    def body(place_ref, q_ref, r_ref, o_ref):
        acc = q_ref[...].astype(F32)
        for j in range(3):
            acc = acc + r_ref[j].astype(F32)
        o_ref[...] = acc

    return pl.pallas_call(
        body, name=name, out_shape=jax.ShapeDtypeStruct(out_shape, F32),
        grid_spec=pltpu.PrefetchScalarGridSpec(
            num_scalar_prefetch=1, grid=(nb,),
            in_specs=[pl.BlockSpec((None, tr, width), lambda i, pr: (pr[0], i, 0)),
                      pl.BlockSpec((3, tr, width), lambda i, pr: (0, i, 0))],
            out_specs=out_spec),
        compiler_params=_cparams(("parallel",)),
    )(place, mine, others)


WEIGHTS = ["meta_tokens", "a_norm_pre", "a_w_in", "a_conv_w", "a_conv_b", "a_dt_bias", "a_a_log", "a_d_skip",
           "a_gate_norm", "a_w_out", "a_norm_post", "kv_norm", "w_kv", "b_norm_pre", "b_w_q", "b_sinks", "b_w_o",
           "b_norm_post", "f_norm_pre", "f_w_up", "f_conv_w", "f_conv_b", "f_w_down", "f_norm_post"]
FULL_SHAPE = {
    "meta_tokens": (16, 1024), "a_norm_pre": (1, 1024), "a_w_in": (1, 1024, 5152), "a_conv_w": (1, 4, 3072),
    "a_conv_b": (1, 3072), "a_dt_bias": (1, 32), "a_a_log": (1, 32), "a_d_skip": (1, 32), "a_gate_norm": (1, 2048),
    "a_w_out": (1, 2048, 1024), "a_norm_post": (1, 1024), "kv_norm": (1024,), "w_kv": (1024, 512),
    "b_norm_pre": (1, 1024), "b_w_q": (1, 1024, 1024), "b_sinks": (1, 16), "b_w_o": (1, 1024, 1024),
    "b_norm_post": (1, 1024), "f_norm_pre": (2, 1024), "f_w_up": (2, 1024, 5632), "f_conv_w": (2, 3, 5632),
    "f_conv_b": (2, 5632), "f_w_down": (2, 2816, 1024), "f_norm_post": (2, 1024),
}
SHARD_AXIS = {
    "meta_tokens": 1, "a_norm_pre": 1, "a_w_in": 2, "a_conv_w": 2, "a_conv_b": 1, "a_dt_bias": None, "a_a_log": None,
    "a_d_skip": None, "a_gate_norm": 1, "a_w_out": 1, "a_norm_post": 1, "kv_norm": None, "w_kv": 0, "b_norm_pre": None,
    "b_w_q": 1, "b_sinks": None, "b_w_o": 1, "b_norm_post": None, "f_norm_pre": None, "f_w_up": 2, "f_conv_w": 2,
    "f_conv_b": None, "f_w_down": 1, "f_norm_post": None,
}
BIG = ["a_w_in", "a_w_out", "w_kv", "b_w_q", "b_w_o", "f_w_up", "f_w_down"]
SMALL = [n for n in WEIGHTS if n not in BIG]
SMALL_SHARDED = [n for n in SMALL if SHARD_AXIS[n] is not None]


def _shard_shape(name):
    shape = list(FULL_SHAPE[name])
    if SHARD_AXIS[name] is not None:
        shape[SHARD_AXIS[name]] //= N_CHIPS
    return tuple(shape)


def _numel(shape):
    return int(math.prod(shape))


SUBLANES = 8


def _packed_rows(shape):
    rows = -(-_numel(shape) // LANES)
    return -(-rows // SUBLANES) * SUBLANES


def _pack(arrays):
    parts = []
    for a in arrays:
        size, rows = _numel(a.shape), _packed_rows(a.shape)
        if size % LANES == 0:
            part = jnp.pad(a.reshape(size // LANES, LANES), ((0, rows - size // LANES), (0, 0)))
        else:
            part = jnp.pad(a.reshape(-1), (0, rows * LANES - size)).reshape(rows, LANES)
        parts.append(part)
    return jnp.concatenate(parts, axis=0)


def _unpack(packed, names, shape_of):
    out, off = {}, 0
    lead = packed.shape[:-2]
    for n in names:
        shape = tuple(shape_of(n))
        size, rows = _numel(shape), _packed_rows(shape)
        part = packed[..., off:off + rows, :]
        if size % LANES == 0:
            out[n] = part[..., :size // LANES, :].reshape(lead + shape)
        else:
            out[n] = part.reshape(lead + (rows * LANES,))[..., :size].reshape(lead + shape)
        off += rows
    return out


def _split_chips(name, full):
    ax = SHARD_AXIS[name]
    shape = full.shape
    cut = shape[:ax] + (N_CHIPS, shape[ax] // N_CHIPS) + shape[ax + 1:]
    return jnp.moveaxis(full.reshape(cut), ax, 0)


def _join_chips(name, stacked):
    ax = SHARD_AXIS[name]
    moved = jnp.moveaxis(stacked, 0, ax)
    shape = moved.shape
    return moved.reshape(shape[:ax] + (shape[ax] * shape[ax + 1],) + shape[ax + 2:])


def _as2d(a):
    return a.reshape(-1, a.shape[-1])


BUFFERS = [("a_w_in", "a_w_in", None), ("a_w_out", "a_w_out", None), ("w_kv", "w_kv", None),
           ("b_w_q", "b_w_q", None), ("b_w_o", "b_w_o", None), ("f_w_up0", "f_w_up", 0), ("f_w_up1", "f_w_up", 1),
           ("f_w_down0", "f_w_down", 0), ("f_w_down1", "f_w_down", 1)]


TRANSPOSED = ("a_w_in",)
SPLIT = {"a_w_in": "cols"}


def _local_shard(arrays, weight, layer):
    if weight in TRANSPOSED:
        return arrays[weight][0].T
    return _as2d(arrays[weight]) if layer is None else arrays[weight]


def _weight_from_gathered(weight, buf):
    if weight == "f_w_up":
        return buf
    return buf.reshape(N_CHIPS * buf.shape[1], buf.shape[2])


def _gathered_from_grad(weight, g):
    if weight == "f_w_up":
        return g
    return g.reshape(N_CHIPS, g.shape[0] // N_CHIPS, g.shape[1]).astype(BF16)


GATHER_SCHEDULE = {
    "a_in_main": [("ici", ["a_w_out"])],
    "a_conv": [("d2d", ["a_w_out"]), ("ici", ["f_w_down0"])],
    "a_ssd_prep": [("d2d", ["f_w_down0"]), ("ici_near", ["f_w_up0"])],
    "a_ssd": [("ici_far", ["f_w_up0"])],
    "a_gate": [("d2d", ["f_w_up0"]), ("ici", ["w_kv", "b_w_q", "b_w_o"])],
    "ffn0_up": [("d2d", ["w_kv", "b_w_q", "b_w_o"]), ("ici", ["f_w_down1"])],
    "ffn0_down": [("d2d", ["f_w_down1"])],
    "b_attn": [("ici", ["f_w_up1"])],
    "b_o": [("d2d", ["f_w_up1"])],
}
REDUCE_SCHEDULE = {
    "b_attn_bwd": [("all", ["f_w_down1", "f_w_up1", "b_w_o"])],
    "ffn0_conv_bwd": [("all", ["b_w_q", "w_kv", "f_w_down0"])],
    "a_ssd_bwd": [("all", ["f_w_up0", "a_w_out"])],
    "a_in_main_dx": [("near", ["a_w_in"])],
    "a_norm_bwd": [("far", ["a_w_in"])],
}
REDUCE_LAST = ("a_w_in",)
ICI_PEERS = {"ici": ALL_PEERS, "ici_near": NEAR_PEERS, "ici_far": FAR_PEERS,
             "all": ALL_PEERS, "near": NEAR_PEERS, "far": FAR_PEERS}
PAIR_SCHEDULE = {
    "b_o_dx": ["f_w_down1", "f_w_up1", "b_w_o"],
    "ffn0_down_dx": ["b_w_q", "w_kv", "f_w_down0"],
    "a_out_dx": ["f_w_up0", "a_w_out"],
    "a_in_dt_dx": ["a_w_in"],
}
SWAP_SCHEDULE = {"a_in_main_dw": ["f_w_down1", "f_w_up1", "b_w_o", "b_w_q", "w_kv", "f_w_down0", "f_w_up0", "a_w_out"]}


def _buffer_of(weight, layer):
    return weight if layer is None else f"{weight}{layer}"


class _Pipeline:
    def __init__(self, place, slots):
        self.place = place
        self.slots = dict(slots)
        self.running = []
        self.grads = {}
        self.theirs = {}
        self.partials = {}
        self.peers = {}
        self.reduced = {}

    def _collect(self):
        for step, buffers, table in self.running:
            table.update(zip(buffers, step.results))
        self.running = []

    @staticmethod
    def _splits(buffers):
        return [SPLIT.get(b, "rows") for b in buffers]

    def gather_now(self, name, buffers):
        step = _step_gather_full([self.slots[b] for b in buffers], self._splits(buffers))
        _run_steps(name, [step])
        self.slots.update(zip(buffers, step.results))

    def weight(self, name, layer=None):
        self._collect()
        return _weight_from_gathered(name, self.slots[_buffer_of(name, layer)])

    def grad(self, name, layer, g):
        self.grads[_buffer_of(name, layer)] = _gathered_from_grad(name, g)

    def steps(self, kernel):
        self._collect()
        steps = []
        for phase, buffers in GATHER_SCHEDULE.get(kernel, []):
            bufs, splits = [self.slots[b] for b in buffers], self._splits(buffers)
            step = (_step_gather_d2d(bufs, splits) if phase == "d2d"
                    else _step_gather_ici(bufs, splits, ICI_PEERS[phase]))
            self.running.append((step, buffers, self.slots))
            steps.append(step)
        buffers = PAIR_SCHEDULE.get(kernel)
        if buffers:
            step = _step_pair_exchange([self.grads[b] for b in buffers], self._splits(buffers))
            self.running.append((step, buffers, self.theirs))
            steps.append(step)
        for part, buffers in REDUCE_SCHEDULE.get(kernel, []):
            for b in buffers:
                if b not in self.partials:
                    self.partials[b] = _rs_pair_add("reduce_pair_add_" + b, self.place, self.grads[b], self.theirs[b],
                                                    SPLIT.get(b, "rows"))
            started = [self.peers[b] for b in buffers] if all(b in self.peers for b in buffers) else None
            step = _step_chip_exchange([self.partials[b] for b in buffers], ICI_PEERS[part], into=started)
            self.running.append((step, buffers, self.peers))
            steps.append(step)
        buffers = SWAP_SCHEDULE.get(kernel)
        if buffers:
            step = self._swap_step(buffers)
            self.running.append((step, buffers, self.reduced))
            steps.append(step)
        return steps

    def _swap_step(self, buffers):
        halves = [_rs_chip_add("reduce_chip_add_" + b, self.place, self.partials[b], self.peers[b], SPLIT.get(b, "rows"))
                  for b in buffers]
        return _step_pair_gather(halves, self._splits(buffers))

    def shard(self, buffer):
        self._collect()
        return self.reduced[buffer]

    def finish(self):
        self._collect()
        rest = [b for b, _, _ in BUFFERS if b not in self.reduced]
        step = self._swap_step(rest)
        _run_steps("reduce_pair_gather", [step])
        self.reduced.update(zip(rest, step.results))


def kernel(x, meta_tokens, a_norm_pre, a_w_in, a_conv_w, a_conv_b, a_dt_bias, a_a_log, a_d_skip, a_gate_norm, a_w_out, a_norm_post, kv_norm, w_kv, b_norm_pre, b_w_q, b_sinks, b_w_o, b_norm_post, f_norm_pre, f_w_up, f_conv_w, f_conv_b, f_w_down, f_norm_post, loss_target, m_meta_tokens, m_a_norm_pre, m_a_w_in, m_a_conv_w, m_a_conv_b, m_a_dt_bias, m_a_a_log, m_a_d_skip, m_a_gate_norm, m_a_w_out, m_a_norm_post, m_kv_norm, m_w_kv, m_b_norm_pre, m_b_w_q, m_b_sinks, m_b_w_o, m_b_norm_post, m_f_norm_pre, m_f_w_up, m_f_conv_w, m_f_conv_b, m_f_w_down, m_f_norm_post, v_meta_tokens, v_a_norm_pre, v_a_w_in, v_a_conv_w, v_a_conv_b, v_a_dt_bias, v_a_a_log, v_a_d_skip, v_a_gate_norm, v_a_w_out, v_a_norm_post, v_kv_norm, v_w_kv, v_b_norm_pre, v_b_w_q, v_b_sinks, v_b_w_o, v_b_norm_post, v_f_norm_pre, v_f_w_up, v_f_conv_w, v_f_conv_b, v_f_w_down, v_f_norm_post):
    given = dict(locals())
    w = {n: given[n] for n in WEIGHTS}
    mom = {n: given["m_" + n] for n in WEIGHTS}
    var = {n: given["v_" + n] for n in WEIGHTS}
    chip = 2 * lax.axis_index("x") + lax.axis_index("y")
    core = lax.axis_index("c")
    place = jnp.stack([chip, core]).astype(jnp.int32)

    small_all = _allgather_small("gather_small", _pack([w[n] for n in SMALL_SHARDED]))
    small_parts = _unpack(small_all, SMALL_SHARDED, _shard_shape)
    slots = {b: _cast_into_slot("cast_" + b, place, _local_shard(w, wn, layer), layer) for b, wn, layer in BUFFERS}
    pipeline = _Pipeline(place, slots)
    pipeline.gather_now("gather_first", ["a_w_in"])
    p = {}
    for n in SMALL:
        p[n] = _join_chips(n, small_parts[n]) if n in SMALL_SHARDED else w[n]
    p["a_conv_w"] = p["a_conv_w"][0]
    p["kv_norm"] = p["kv_norm"].reshape(1, D_MODEL)

    loss_local, grad_x, g = _local_step(x[0], loss_target[0], p, pipeline)

    small_sum = _allreduce_small("reduce_small", _pack([g[n].reshape(FULL_SHAPE[n]) for n in SMALL]
                                                       + [loss_local.reshape(1, 1)]))
    small_red = _unpack(small_sum, SMALL + ["loss"], lambda n: (1, 1) if n == "loss" else FULL_SHAPE[n])
    loss = small_red["loss"][0, 0]
    grads = {}
    for n in SMALL:
        if SHARD_AXIS[n] is None:
            grads[n] = small_red[n]
        else:
            grads[n] = lax.dynamic_index_in_dim(_split_chips(n, small_red[n]), chip, 0, keepdims=False)

    delta, new_m, new_v = {}, {}, {}
    for n in sorted(BIG, key=lambda name: name in REDUCE_LAST):
        shape = _shard_shape(n)
        if n in REDUCE_LAST:
            pipeline.finish()
        if n in TRANSPOSED:
            g2d = pipeline.shard(n)
            w2d, m2d, v2d = (arrays[n][0].T for arrays in (w, mom, var))
            back = lambda a: a.T.reshape(shape)
        else:
            g2d = (jnp.concatenate([pipeline.shard(n + "0"), pipeline.shard(n + "1")], axis=0)
                   if n in ("f_w_up", "f_w_down") else pipeline.shard(n))
            w2d, m2d, v2d = (_as2d(arrays[n]) for arrays in (w, mom, var))
            back = lambda a: a.reshape(shape)
        d, m2, v2 = _adamw("adamw_" + n, w2d, g2d, m2d, v2d, steps=pipeline.steps("adamw_" + n))
        grads[n], delta[n], new_m[n], new_v[n] = back(g2d), back(d), back(m2), back(v2)
    at_least_2d = lambda n: (1,) * (2 - len(_shard_shape(n))) + _shard_shape(n)
    outs = _adamw_small("adamw_small", *[[src[n].reshape(at_least_2d(n)) for n in SMALL] for src in (w, grads, mom, var)])
    for dst, arrays in zip((delta, new_m, new_v), outs):
        dst.update({n: a.reshape(_shard_shape(n)) for n, a in zip(SMALL, arrays)})

    return (loss, grad_x[None], *[grads[n].reshape(_shard_shape(n)) for n in WEIGHTS],
            *[delta[n] for n in WEIGHTS], *[new_m[n] for n in WEIGHTS], *[new_v[n] for n in WEIGHTS])
```

```python
import functools
import math

import jax
import jax.numpy as jnp
from jax import lax
from jax.experimental import pallas as pl
from jax.experimental.pallas import tpu as pltpu

F32, BF16 = jnp.float32, jnp.bfloat16
MESH = pl.DeviceIdType.MESH

D_MODEL = 1024
N_META = 16
CHUNK = 128
PAD_ROWS = CHUNK - N_META
D_INNER = 2048
D_STATE = 128
N_GROUPS = 4
HEADS_PER_GROUP = 8
SSM_HEADS = 32
HEAD_DIM = 64
D_BC = N_GROUPS * D_STATE
D_XBC = D_INNER + 2 * D_BC
D_MAIN = D_INNER + D_XBC
D_IN_PROJ = D_MAIN + SSM_HEADS
GROUP_W = HEADS_PER_GROUP * HEAD_DIM
SSM_CONV = 4
D_FF = 2816
FFN_CONV = 3
N_Q_HEADS = 16
N_KV_HEADS = 4
D_KV = 256
ATTN_SCALE = 1.0 / math.sqrt(HEAD_DIM)
RMS_EPS = 1e-6
NEG_INF = -1e30
LANES = 128
VMEM_LIMIT = 48 * 1024 * 1024

ADAM_LR, ADAM_B1, ADAM_B2, ADAM_EPS, ADAM_WD, ADAM_STEP = 0.001, 0.9, 0.999, 1e-08, 0.01, 10

N_CHIPS = 4


def _cparams(sem=None):
    return pltpu.CompilerParams(dimension_semantics=sem, vmem_limit_bytes=VMEM_LIMIT)


def _tile(n, cands=(512, 256, 128)):
    for t in cands:
        if n % t == 0:
            return t
    return n


def _row_tile(rows, width):
    for t in (544, 272):
        if rows % t == 0 and t * width * 4 <= (3 << 20):
            return t
    return 128


def _rows_mask(i, tm):
    rows = i * tm + lax.broadcasted_iota(jnp.int32, (tm, 1), 0)
    return rows >= PAD_ROWS


def _dot(a, b):
    return jnp.dot(a, b, preferred_element_type=F32)


def _dot_nt(a, b):
    return lax.dot_general(a, b, (((1,), (1,)), ((), ())), preferred_element_type=F32)


def _dot_tn(a, b):
    return lax.dot_general(a, b, (((0,), (0,)), ((), ())), preferred_element_type=F32)


def _sigmoid(x):
    return 1.0 / (1.0 + jnp.exp(-x))


def _place():
    return lax.axis_index("x"), lax.axis_index("y"), lax.axis_index("c")


def _other_chips(x, y):
    return [(1 - x, y), (x, 1 - y), (1 - x, 1 - y)]


class _Step:
    def __init__(self, ins, outs, aliases, n_sems, start, finish):
        self.ins, self.outs, self.aliases, self.n_sems = list(ins), list(outs), dict(aliases), n_sems
        self.start, self.finish = start, finish
        self.results = None


def _like(a):
    return jax.ShapeDtypeStruct(a.shape, a.dtype)


def _remote(src, dst, send_sems, recv_sems, k, device):
    return pltpu.make_async_remote_copy(src, dst, send_sems.at[k], recv_sems.at[k], device_id=device, device_id_type=MESH)


def _half(ref, split, which, lead=()):
    if split == "rows":
        hr = ref.shape[-2] // 2
        return ref.at[lead + (pl.ds(which * hr, hr),)]
    hc = ref.shape[-1] // 2
    return ref.at[lead + (slice(None), pl.ds(which * hc, hc))]


def _splits(bufs, splits):
    return list(splits) if splits is not None else ["rows"] * len(bufs)


ALL_PEERS = (0, 1, 2)
NEAR_PEERS = (0, 1)
FAR_PEERS = (2,)


def _step_gather_ici(bufs, splits=None, peers=ALL_PEERS):
    splits = _splits(bufs, splits)

    def copies(outs, send_sems, recv_sems, received):
        x, y, c = _place()
        me = 2 * x + y
        for k, o in enumerate(outs):
            for j, (cx, cy) in enumerate(_other_chips(x, y)):
                if j in peers:
                    part = _half(o, splits[k], c, (2 * cx + cy if received else me,))
                    yield _remote(part, part, send_sems, recv_sems, 3 * k + j, (cx, cy, c))

    def start(ins, outs, send_sems, recv_sems):
        for cp in copies(outs, send_sems, recv_sems, False):
            cp.start()

    def finish(ins, outs, send_sems, recv_sems):
        for cp in copies(outs, send_sems, recv_sems, True):
            cp.wait_recv()
        for cp in copies(outs, send_sems, recv_sems, False):
            cp.wait_send()

    return _Step(bufs, [_like(b) for b in bufs], {k: k for k in range(len(bufs))}, 3 * len(bufs), start, finish)


def _step_gather_d2d(bufs, splits=None):
    splits = _splits(bufs, splits)

    def copies(outs, send_sems, recv_sems, received):
        x, y, c = _place()
        for k, o in enumerate(outs):
            for j, (cx, cy) in enumerate(_other_chips(x, y)):
                part = _half(o, splits[k], 1 - c if received else c, (2 * cx + cy,))
                yield _remote(part, part, send_sems, recv_sems, 3 * k + j, (x, y, 1 - c))

    def start(ins, outs, send_sems, recv_sems):
        for cp in copies(outs, send_sems, recv_sems, False):
            cp.start()

    def finish(ins, outs, send_sems, recv_sems):
        for cp in copies(outs, send_sems, recv_sems, True):
            cp.wait_recv()
        for cp in copies(outs, send_sems, recv_sems, False):
            cp.wait_send()

    return _Step(bufs, [_like(b) for b in bufs], {k: k for k in range(len(bufs))}, 3 * len(bufs), start, finish)


def _step_gather_full(bufs, splits=None):
    n = len(bufs)
    splits = _splits(bufs, splits)

    def ici(outs, send_sems, recv_sems, received):
        x, y, c = _place()
        me = 2 * x + y
        for k, o in enumerate(outs):
            for j, (cx, cy) in enumerate(_other_chips(x, y)):
                part = _half(o, splits[k], c, (2 * cx + cy if received else me,))
                yield _remote(part, part, send_sems, recv_sems, 3 * k + j, (cx, cy, c))

    def d2d(outs, send_sems, recv_sems, received):
        x, y, c = _place()
        for k, o in enumerate(outs):
            for j, (cx, cy) in enumerate(_other_chips(x, y)):
                part = _half(o, splits[k], 1 - c if received else c, (2 * cx + cy,))
                yield _remote(part, part, send_sems, recv_sems, 3 * n + 3 * k + j, (x, y, 1 - c))

    def start(ins, outs, send_sems, recv_sems):
        for cp in ici(outs, send_sems, recv_sems, False):
            cp.start()

    def finish(ins, outs, send_sems, recv_sems):
        for arrived, onward in zip(ici(outs, send_sems, recv_sems, True), d2d(outs, send_sems, recv_sems, False)):
            arrived.wait_recv()
            onward.start()
        for cp in d2d(outs, send_sems, recv_sems, True):
            cp.wait_recv()
        for cp in ici(outs, send_sems, recv_sems, False):
            cp.wait_send()
        for cp in d2d(outs, send_sems, recv_sems, False):
            cp.wait_send()

    return _Step(bufs, [_like(b) for b in bufs], {k: k for k in range(n)}, 6 * n, start, finish)


def _half_shape(shape, split):
    return shape[:-2] + ((shape[-2] // 2, shape[-1]) if split == "rows" else (shape[-2], shape[-1] // 2))


def _step_pair_exchange(grads, splits=None):
    splits = _splits(grads, splits)

    def copies(ins, outs, send_sems, recv_sems):
        x, y, c = _place()
        for k, (g, o) in enumerate(zip(ins, outs)):
            yield _remote(_half(g, splits[k], 1 - c, (slice(None),)), o, send_sems, recv_sems, k, (x, y, 1 - c))

    def start(ins, outs, send_sems, recv_sems):
        for cp in copies(ins, outs, send_sems, recv_sems):
            cp.start()

    def finish(ins, outs, send_sems, recv_sems):
        for cp in copies(ins, outs, send_sems, recv_sems):
            cp.wait()

    outs = [jax.ShapeDtypeStruct(_half_shape(g.shape, s), g.dtype) for g, s in zip(grads, splits)]
    return _Step(grads, outs, {}, len(grads), start, finish)


def _step_chip_exchange(partials, peers=ALL_PEERS, into=None):
    n = len(partials)

    def copies(ins, outs, send_sems, recv_sems):
        x, y, c = _place()
        for k, (q, o) in enumerate(zip(ins[:n], outs)):
            for j, (cx, cy) in enumerate(_other_chips(x, y)):
                if j in peers:
                    yield _remote(q.at[2 * cx + cy], o.at[j], send_sems, recv_sems, 3 * k + j, (cx, cy, c))

    def start(ins, outs, send_sems, recv_sems):
        for cp in copies(ins, outs, send_sems, recv_sems):
            cp.start()

    def finish(ins, outs, send_sems, recv_sems):
        for cp in copies(ins, outs, send_sems, recv_sems):
            cp.wait()

    outs = [jax.ShapeDtypeStruct((3,) + q.shape[1:], q.dtype) for q in partials]
    if into is None:
        return _Step(partials, outs, {}, 3 * n, start, finish)
    return _Step(list(partials) + list(into), outs, {n + k: k for k in range(n)}, 3 * n, start, finish)


def _step_pair_gather(shards, splits=None):
    splits = _splits(shards, splits)

    def copies(outs, send_sems, recv_sems, received):
        x, y, c = _place()
        for k, o in enumerate(outs):
            part = _half(o, splits[k], 1 - c if received else c)
            yield _remote(part, part, send_sems, recv_sems, k, (x, y, 1 - c))

    def start(ins, outs, send_sems, recv_sems):
        for cp in copies(outs, send_sems, recv_sems, False):
            cp.start()

    def finish(ins, outs, send_sems, recv_sems):
        for cp in copies(outs, send_sems, recv_sems, True):
            cp.wait_recv()
        for cp in copies(outs, send_sems, recv_sems, False):
            cp.wait_send()

    return _Step(shards, [_like(s) for s in shards], {k: k for k in range(len(shards))}, len(shards), start, finish)


def _call(body, *, name, out_shape, grid, in_specs, out_specs, operands, scratch_shapes=(), semantics=None, steps=()):
    single = not isinstance(out_shape, (tuple, list))
    out_shapes = [out_shape] if single else list(out_shape)
    out_spec_list = [out_specs] if single else list(out_specs)
    steps = list(steps)
    if not steps:
        res = pl.pallas_call(body, name=name, out_shape=out_shapes, grid=grid, in_specs=list(in_specs),
                             out_specs=out_spec_list, scratch_shapes=list(scratch_shapes),
                             compiler_params=_cparams(semantics))(*operands)
        return res[0] if single else res
    n_in, n_out, n_scr = len(operands), len(out_shapes), len(scratch_shapes)
    x_in = [a for s in steps for a in s.ins]
    x_out = [o for s in steps for o in s.outs]
    aliases, in_off, out_off = {}, 0, 0
    for s in steps:
        for i, o in s.aliases.items():
            aliases[n_in + in_off + i] = n_out + out_off + o
        in_off += len(s.ins)
        out_off += len(s.outs)
    sems = []
    for s in steps:
        sems += [pltpu.SemaphoreType.DMA((s.n_sems,)), pltpu.SemaphoreType.DMA((s.n_sems,))]
    any_spec = pl.BlockSpec(memory_space=pl.ANY)

    def carried(*refs):
        pos = 0
        ins = refs[pos:pos + n_in]; pos += n_in
        xi = refs[pos:pos + len(x_in)]; pos += len(x_in)
        outs = refs[pos:pos + n_out]; pos += n_out
        xo = refs[pos:pos + len(x_out)]; pos += len(x_out)
        scr = refs[pos:pos + n_scr]; pos += n_scr
        sem_refs = refs[pos:]

        def each(action):
            i0 = o0 = 0
            for k, s in enumerate(steps):
                getattr(s, action)(xi[i0:i0 + len(s.ins)], xo[o0:o0 + len(s.outs)], sem_refs[2 * k], sem_refs[2 * k + 1])
                i0 += len(s.ins)
                o0 += len(s.outs)

        if grid:
            first = functools.reduce(jnp.logical_and, [pl.program_id(d) == 0 for d in range(len(grid))])
            last = functools.reduce(jnp.logical_and, [pl.program_id(d) == grid[d] - 1 for d in range(len(grid))])
            pl.when(first)(lambda: each("start"))
            body(*ins, *outs, *scr)
            pl.when(last)(lambda: each("finish"))
        else:
            each("start")
            body(*ins, *outs, *scr)
            each("finish")

    res = pl.pallas_call(
        carried, name=name, out_shape=out_shapes + x_out, grid=grid,
        in_specs=list(in_specs) + [any_spec] * len(x_in), out_specs=out_spec_list + [any_spec] * len(x_out),
        scratch_shapes=list(scratch_shapes) + sems, input_output_aliases=aliases,
        compiler_params=_cparams(None if semantics is None else ("arbitrary",) * len(grid)),
    )(*operands, *x_in)
    o0 = n_out
    for s in steps:
        s.results = list(res[o0:o0 + len(s.outs)])
        o0 += len(s.outs)
    return res[0] if single else tuple(res[:n_out])


def _run_steps(name, steps):
    _call(lambda: None, name=name, out_shape=[], grid=(), in_specs=[], out_specs=[], operands=[], steps=steps)
    return [s.results for s in steps]


def _mm(name, a, b, mode, out_dtype=F32, acc=None, b_colblock=0, k_rows=None, out_rows=None, steps=()):
    resident_bytes = 8 << 20
    if mode == "nn":
        m, k = a.shape
        n = b.shape[1]
        tm = m
        while tm * k * 2 > resident_bytes and tm % 32 == 0:
            tm //= 2
        tn = _tile(n)
        grid = (m // tm, n // tn)
        in_specs = [pl.BlockSpec((tm, k), lambda i, j: (i, 0)), pl.BlockSpec((k, tn), lambda i, j: (0, j))]
        out_shape, out_block = (m, n), (tm, tn)
    elif mode == "nt":
        m, n = a.shape
        k = k_rows or b.shape[0]
        tm = m
        while tm * n * 2 > resident_bytes and tm % 32 == 0:
            tm //= 2
        tk = _tile(k)
        grid = (m // tm, k // tk)
        in_specs = [pl.BlockSpec((tm, n), lambda i, j: (i, 0)), pl.BlockSpec((tk, n), lambda i, j: (j, b_colblock))]
        out_shape, out_block = (m, k), (tm, tk)
    else:
        m, k = a.shape
        n = b.shape[1]
        tk, tn = _tile(k), (n if m * n * 2 <= resident_bytes else _tile(n))
        grid = (k // tk, n // tn)
        in_specs = [pl.BlockSpec((m, tk), lambda i, j: (0, i)), pl.BlockSpec((m, tn), lambda i, j: (0, j))]
        out_shape, out_block = (out_rows or k, n), (tk, tn)
    out_spec = pl.BlockSpec(out_block, lambda i, j: (i, j))
    has_acc = acc is not None

    def body(*refs):
        a_ref, b_ref = refs[0], refs[1]
        o_ref = refs[-1]
        av, bv = a_ref[...], b_ref[...]
        if mode == "nn":
            r = _dot(av, bv)
        elif mode == "nt":
            r = _dot_nt(av, bv)
        else:
            r = _dot_tn(av, bv)
        if has_acc:
            r = r + refs[2][...]
        o_ref[...] = r.astype(o_ref.dtype)

    operands = [a, b]
    if has_acc:
        in_specs = in_specs + [out_spec]
        operands.append(acc)
    return _call(body, name=name, out_shape=jax.ShapeDtypeStruct(out_shape, out_dtype), grid=grid, in_specs=in_specs,
                 out_specs=out_spec, operands=operands, semantics=("parallel", "parallel"), steps=steps)


def _tn_rows_into(name, a, b, into, row0, nrows):
    m, k = a.shape
    n = b.shape[1]

    def body(a_ref, b_ref, into_ref, o_ref):
        o_ref[...] = _dot_tn(a_ref[...], b_ref[...])[0:nrows].astype(o_ref.dtype)

    return pl.pallas_call(
        body, name=name, out_shape=jax.ShapeDtypeStruct(into.shape, into.dtype), grid=(1,),
        in_specs=[pl.BlockSpec((m, k), lambda i: (0, 0)), pl.BlockSpec((m, n), lambda i: (0, 0)),
                  pl.BlockSpec(memory_space=pl.ANY)],
        out_specs=pl.BlockSpec((nrows, n), lambda i: (row0 // nrows, 0)),
        input_output_aliases={2: 0}, compiler_params=_cparams(("arbitrary",)),
    )(a, b, into)


def _rms_fwd(name, h, w):
    rows, width = h.shape
    tm = _row_tile(rows, width)

    def body(h_ref, w_ref, o_ref):
        x = h_ref[...]
        r = lax.rsqrt(jnp.mean(x * x, axis=-1, keepdims=True) + RMS_EPS)
        o_ref[...] = (x * r * w_ref[...]).astype(BF16)

    return pl.pallas_call(
        body, name=name, out_shape=jax.ShapeDtypeStruct((rows, width), BF16), grid=(rows // tm,),
        in_specs=[pl.BlockSpec((tm, width), lambda i: (i, 0)), pl.BlockSpec((1, width), lambda i: (0, 0))],
        out_specs=pl.BlockSpec((tm, width), lambda i: (i, 0)), compiler_params=_cparams(("parallel",)),
    )(h, w)


def _resid_norm_fwd(name, h, pre, w, next_norms=()):
    rows, width = h.shape
    tm = _row_tile(rows, width)
    n_next = len(next_norms)

    def body(*refs):
        h_ref, p_ref, w_ref = refs[:3]
        v_refs = refs[3:3 + n_next]
        o_ref = refs[3 + n_next]
        n_refs = refs[4 + n_next:]
        p = p_ref[...]
        r = lax.rsqrt(jnp.mean(p * p, axis=-1, keepdims=True) + RMS_EPS)
        x = h_ref[...] + jnp.where(_rows_mask(pl.program_id(0), tm), p * r * w_ref[...], 0.0)
        o_ref[...] = x
        if n_next:
            rx = lax.rsqrt(jnp.mean(x * x, axis=-1, keepdims=True) + RMS_EPS)
            for v_ref, n_ref in zip(v_refs, n_refs):
                n_ref[...] = (x * rx * v_ref[...]).astype(BF16)

    row_spec = pl.BlockSpec((tm, width), lambda i: (i, 0))
    vec_spec = pl.BlockSpec((1, width), lambda i: (0, 0))
    outs = pl.pallas_call(
        body, name=name,
        out_shape=[jax.ShapeDtypeStruct((rows, width), F32)] + [jax.ShapeDtypeStruct((rows, width), BF16)] * n_next,
        grid=(rows // tm,), in_specs=[row_spec, row_spec, vec_spec] + [vec_spec] * n_next,
        out_specs=[row_spec] * (1 + n_next), compiler_params=_cparams(("parallel",)),
    )(h, pre, w, *next_norms)
    return outs[0], list(outs[1:])


def _resid_norm_loss(name, h, pre, w, target):
    rows, width = h.shape

    def body(h_ref, p_ref, w_ref, t_ref, dh_ref, loss_ref, dp_ref, dw_ref):
        i = pl.program_id(0)
        p = p_ref[...]
        r = lax.rsqrt(jnp.mean(p * p, axis=-1, keepdims=True) + RMS_EPS)
        x = h_ref[...] + p * r * w_ref[...]
        real = (i + jnp.zeros((CHUNK, 1), jnp.int32)) >= 1
        diff = jnp.where(real, x - t_ref[...], 0.0)
        dh = diff * (1.0 / D_MODEL)
        dh_ref[...] = dh
        dp, dw_rows = _rms_bwd(dh, p, w_ref[...])
        dp_ref[...] = dp.astype(BF16)

        @pl.when(i == 0)
        def _():
            loss_ref[...] = jnp.zeros_like(loss_ref)
            dw_ref[...] = jnp.zeros_like(dw_ref)

        loss_ref[...] += jnp.sum(diff * diff) * (0.5 / D_MODEL)
        dw_ref[...] += jnp.sum(dw_rows, axis=0, keepdims=True)

    blk = pl.BlockSpec((CHUNK, width), lambda i: (i, 0))
    vec_spec = pl.BlockSpec((1, width), lambda i: (0, 0))
    return pl.pallas_call(
        body, name=name,
        out_shape=(jax.ShapeDtypeStruct((rows, width), F32), jax.ShapeDtypeStruct((1, LANES), F32),
                   jax.ShapeDtypeStruct((rows, width), BF16), jax.ShapeDtypeStruct((1, width), F32)),
        grid=(rows // CHUNK,),
        in_specs=[blk, blk, vec_spec, pl.BlockSpec((CHUNK, width), lambda i: (jnp.maximum(i - 1, 0), 0))],
        out_specs=(blk, pl.BlockSpec((1, LANES), lambda i: (0, 0)), blk, vec_spec),
        compiler_params=_cparams(("arbitrary",)),
    )(h, pre, w, target)


def _rms_bwd(dy, x, w):
    r = lax.rsqrt(jnp.mean(x * x, axis=-1, keepdims=True) + RMS_EPS)
    xhat = x * r
    dxhat = dy * w
    return r * (dxhat - xhat * jnp.mean(dxhat * xhat, axis=-1, keepdims=True)), dy * xhat


def _norm_bwd_add(name, dh, dhn, h, w, then=None, split_first_block=False, steps=()):
    rows, width = dh.shape
    tm = CHUNK if split_first_block else _row_tile(rows, width)
    fused = then is not None
    assert not (fused and split_first_block)

    def body(*refs):
        dh_ref, dhn_ref, h_ref, w_ref = refs[:4]
        o_ref, dw_ref = refs[6:8] if fused else refs[-2:]
        i = pl.program_id(0)
        valid = _rows_mask(i, tm)
        dx, dw_rows = _rms_bwd(dhn_ref[...], h_ref[...], w_ref[...])
        dh_new = dh_ref[...] + jnp.where(valid, dx, 0.0)
        if split_first_block:
            first_ref = refs[4]

            @pl.when(i == 0)
            def _():
                first_ref[...] = dh_new

            @pl.when(i > 0)
            def _():
                o_ref[...] = dh_new
        else:
            o_ref[...] = dh_new

        @pl.when(i == 0)
        def _():
            dw_ref[...] = jnp.zeros_like(dw_ref)

        dw_ref[...] += jnp.sum(dw_rows, axis=0, keepdims=True)
        if fused:
            p_ref, wp_ref, dp_ref, dwp_ref = refs[4], refs[5], refs[8], refs[9]
            dp, dwp_rows = _rms_bwd(jnp.where(valid, dh_new, 0.0), p_ref[...], wp_ref[...])
            dp_ref[...] = dp.astype(BF16)

            @pl.when(i == 0)
            def _():
                dwp_ref[...] = jnp.zeros_like(dwp_ref)

            dwp_ref[...] += jnp.sum(dwp_rows, axis=0, keepdims=True)

    row_spec = pl.BlockSpec((tm, width), lambda i: (i, 0))
    vec_spec = pl.BlockSpec((1, width), lambda i: (0, 0))
    row_f32, vec_f32 = jax.ShapeDtypeStruct((rows, width), F32), jax.ShapeDtypeStruct((1, width), F32)
    in_specs, operands = [row_spec, row_spec, row_spec, vec_spec], [dh, dhn, h, w]
    out_shape, out_specs = [row_f32, vec_f32], [row_spec, vec_spec]
    if split_first_block:
        out_shape = [jax.ShapeDtypeStruct((tm, width), F32), jax.ShapeDtypeStruct((rows - tm, width), F32), vec_f32]
        out_specs = [pl.BlockSpec((tm, width), lambda i: (0, 0)),
                     pl.BlockSpec((tm, width), lambda i: (jnp.maximum(i - 1, 0), 0)), vec_spec]
    if fused:
        in_specs += [row_spec, vec_spec]
        operands += list(then)
        out_shape += [jax.ShapeDtypeStruct((rows, width), BF16), vec_f32]
        out_specs += [row_spec, vec_spec]
    return _call(body, name=name, out_shape=out_shape, grid=(rows // tm,), in_specs=in_specs, out_specs=out_specs,
                 operands=operands, semantics=("arbitrary",), steps=steps)


def _shift_down(x, s, rows):
    return pltpu.roll(x, s, 0) if s else x


def _shift_up(x, s, rows):
    return pltpu.roll(x, rows - s, 0) if s else x


def _conv4_fwd(name, zx, cw, cb, steps=()):
    rows = zx.shape[0]
    off = D_INNER // LANES

    def body(x_ref, w_ref, b_ref, o_ref):
        x = x_ref[...]
        acc = b_ref[...] + w_ref[pl.ds(SSM_CONV - 1, 1), :] * x
        for s in range(1, SSM_CONV):
            acc = acc + w_ref[pl.ds(SSM_CONV - 1 - s, 1), :] * _shift_down(x, s, rows)
        valid = lax.broadcasted_iota(jnp.int32, (rows, 1), 0) >= PAD_ROWS
        o_ref[...] = jnp.where(valid, acc * _sigmoid(acc), 0.0)

    return _call(
        body, name=name, out_shape=jax.ShapeDtypeStruct((rows, D_XBC), F32), grid=(D_XBC // LANES,),
        in_specs=[pl.BlockSpec((rows, LANES), lambda j: (0, j + off)),
                  pl.BlockSpec((SSM_CONV, LANES), lambda j: (0, j)),
                  pl.BlockSpec((1, LANES), lambda j: (0, j))],
        out_specs=pl.BlockSpec((rows, LANES), lambda j: (0, j)), operands=[zx, cw, cb],
        semantics=("parallel",), steps=steps)


def _conv4_bwd(name, zx, dout, cw, cb, into):
    rows, width = dout.shape
    zoff = D_INNER // LANES

    def body(x_ref, d_ref, w_ref, b_ref, into_ref, dx_ref, dw_ref, db_ref):
        x = x_ref[...]
        shifted = [_shift_down(x, s, rows) for s in range(SSM_CONV)]
        acc = b_ref[...]
        for s in range(SSM_CONV):
            acc = acc + w_ref[pl.ds(SSM_CONV - 1 - s, 1), :] * shifted[s]
        sig = _sigmoid(acc)
        valid = lax.broadcasted_iota(jnp.int32, (rows, 1), 0) >= PAD_ROWS
        dpre = jnp.where(valid, d_ref[...] * sig * (1.0 + acc * (1.0 - sig)), 0.0)
        dx = w_ref[pl.ds(SSM_CONV - 1, 1), :] * dpre
        for s in range(1, SSM_CONV):
            dx = dx + w_ref[pl.ds(SSM_CONV - 1 - s, 1), :] * _shift_up(dpre, s, rows)
        dx_ref[...] = dx.astype(BF16)
        for s in range(SSM_CONV):
            dw_ref[pl.ds(SSM_CONV - 1 - s, 1), :] = jnp.sum(dpre * shifted[s], axis=0, keepdims=True)
        db_ref[...] = jnp.sum(dpre, axis=0, keepdims=True)

    return pl.pallas_call(
        body, name=name,
        out_shape=(jax.ShapeDtypeStruct(into.shape, BF16), jax.ShapeDtypeStruct((SSM_CONV, width), F32),
                   jax.ShapeDtypeStruct((1, width), F32)),
        grid=(width // LANES,),
        in_specs=[pl.BlockSpec((rows, LANES), lambda j: (0, j + zoff)),
                  pl.BlockSpec((rows, LANES), lambda j: (0, j)),
                  pl.BlockSpec((SSM_CONV, LANES), lambda j: (0, j)),
                  pl.BlockSpec((1, LANES), lambda j: (0, j)),
                  pl.BlockSpec(memory_space=pl.ANY)],
        out_specs=(pl.BlockSpec((rows, LANES), lambda j: (0, j + zoff)),
                   pl.BlockSpec((SSM_CONV, LANES), lambda j: (0, j)),
                   pl.BlockSpec((1, LANES), lambda j: (0, j))),
        input_output_aliases={4: 0}, compiler_params=_cparams(("parallel",)),
    )(zx, dout, cw, cb, into)


FFN_TILE = 2 * LANES


def _ffn_up_conv(name, hn, w_up, cw, cb, steps=()):
    rows, k = hn.shape
    chip_blocks = w_up.shape[2] // LANES
    half_blocks = D_FF // LANES
    nt = D_FF // FFN_TILE

    def weight_block(offset):
        return pl.BlockSpec((None, k, LANES), lambda j: ((2 * j + offset) // chip_blocks, 0, (2 * j + offset) % chip_blocks))

    def body(a_ref, g0, g1, v0, v1, wg_ref, wv_ref, bg_ref, bv_ref, upg_ref, upv_ref, act_ref):
        a = a_ref[...]
        g = _dot(a, jnp.concatenate([g0[...], g1[...]], axis=1))
        v = _dot(a, jnp.concatenate([v0[...], v1[...]], axis=1))
        upg_ref[...] = g
        upv_ref[...] = v
        ug, uv = bg_ref[...], bv_ref[...]
        for s in range(FFN_CONV):
            ug = ug + wg_ref[pl.ds(FFN_CONV - 1 - s, 1), :] * _shift_down(g, s, rows)
            uv = uv + wv_ref[pl.ds(FFN_CONV - 1 - s, 1), :] * _shift_down(v, s, rows)
        act_ref[...] = (ug * _sigmoid(ug) * uv).astype(BF16)

    col = pl.BlockSpec((rows, FFN_TILE), lambda j: (0, j))
    wsp = lambda shift: pl.BlockSpec((FFN_CONV, FFN_TILE), lambda j: (0, j + shift))
    bsp = lambda shift: pl.BlockSpec((1, FFN_TILE), lambda j: (0, j + shift))
    half = jax.ShapeDtypeStruct((rows, D_FF), F32)
    return _call(
        body, name=name, out_shape=(half, half, jax.ShapeDtypeStruct((rows, D_FF), BF16)), grid=(nt,),
        in_specs=[pl.BlockSpec((rows, k), lambda j: (0, 0)), weight_block(0), weight_block(1),
                  weight_block(half_blocks), weight_block(half_blocks + 1), wsp(0), wsp(nt), bsp(0), bsp(nt)],
        out_specs=(col, col, col), operands=[hn, w_up, w_up, w_up, w_up, cw, cw, cb, cb],
        semantics=("parallel",), steps=steps)


def _ffn_conv_bwd(name, up_g, up_v, dact, cw, cb, hn, w_up, steps=()):
    rows, k = hn.shape
    chip_blocks = w_up.shape[2] // LANES
    nt = D_FF // LANES

    def weight_block(shift):
        return pl.BlockSpec((None, k, LANES), lambda j: ((j + shift) // chip_blocks, 0, (j + shift) % chip_blocks))

    def body(g_ref, v_ref, d_ref, wg_ref, wv_ref, bg_ref, bv_ref, upg_ref, upv_ref, hn_ref,
             dwg_ref, dwv_ref, dbg_ref, dbv_ref, dhn_ref, dup_ref, acc, hn_scr, hnt_scr, dup_scr, sems):
        j = pl.program_id(0)
        hn_copy = pltpu.make_async_copy(hn_ref, hn_scr, sems.at[0])
        dhn_copy = pltpu.make_async_copy(acc, dhn_ref, sems.at[0])

        def dup_copy(step, half):
            block, slot = step + half * nt, 2 * (step % 2) + half
            cols = pl.ds(pl.multiple_of((block % chip_blocks) * LANES, LANES), LANES)
            return pltpu.make_async_copy(dup_scr.at[slot], dup_ref.at[block // chip_blocks, :, cols], sems.at[1 + slot])

        @pl.when(j == 0)
        def _():
            hn_copy.start()
            acc[...] = jnp.zeros_like(acc)
            hn_copy.wait()
            for r in range(0, rows, LANES):
                hnt_scr[:, r:r + LANES] = hn_scr[r:r + LANES, :].T

        @pl.when(j >= 2)
        def _():
            dup_copy(j - 2, 0).wait()
            dup_copy(j - 2, 1).wait()

        g, v = g_ref[...], v_ref[...]
        gs = [_shift_down(g, s, rows) for s in range(FFN_CONV)]
        vs = [_shift_down(v, s, rows) for s in range(FFN_CONV)]
        ug, uv = bg_ref[...], bv_ref[...]
        for s in range(FFN_CONV):
            ug = ug + wg_ref[pl.ds(FFN_CONV - 1 - s, 1), :] * gs[s]
            uv = uv + wv_ref[pl.ds(FFN_CONV - 1 - s, 1), :] * vs[s]
        sig = _sigmoid(ug)
        dsig = d_ref[...] * sig
        dup = []
        for dpre, src, w_ref, dw_ref, db_ref in (
                (dsig * uv * (1.0 + ug * (1.0 - sig)), gs, wg_ref, dwg_ref, dbg_ref),
                (dsig * ug, vs, wv_ref, dwv_ref, dbv_ref)):
            dx = w_ref[pl.ds(FFN_CONV - 1, 1), :] * dpre
            for s in range(1, FFN_CONV):
                dx = dx + w_ref[pl.ds(FFN_CONV - 1 - s, 1), :] * _shift_up(dpre, s, rows)
            dup.append(dx.astype(BF16))
            for s in range(FFN_CONV):
                dw_ref[pl.ds(FFN_CONV - 1 - s, 1), :] = jnp.sum(dpre * src[s], axis=0, keepdims=True)
            db_ref[...] = jnp.sum(dpre, axis=0, keepdims=True)
        dup = jnp.concatenate(dup, axis=1)
        acc[...] += _dot_nt(dup, jnp.concatenate([upg_ref[...], upv_ref[...]], axis=1))
        dw = _dot(hnt_scr[...], dup)
        slot = 2 * (j % 2)
        dup_scr[slot] = dw[:, :LANES].astype(BF16)
        dup_scr[slot + 1] = dw[:, LANES:].astype(BF16)
        dup_copy(j, 0).start()
        dup_copy(j, 1).start()

        @pl.when(j == nt - 1)
        def _():
            dhn_copy.start()
            for step in (j - 1, j):
                dup_copy(step, 0).wait()
                dup_copy(step, 1).wait()
            dhn_copy.wait()

    col = pl.BlockSpec((rows, LANES), lambda j: (0, j))
    wsp = lambda shift: pl.BlockSpec((FFN_CONV, LANES), lambda j: (0, j + shift))
    bsp = lambda shift: pl.BlockSpec((1, LANES), lambda j: (0, j + shift))
    any_spec = pl.BlockSpec(memory_space=pl.ANY)
    dw_shape = jax.ShapeDtypeStruct((FFN_CONV, D_FF), F32)
    db_shape = jax.ShapeDtypeStruct((1, D_FF), F32)
    return _call(
        body, name=name, grid=(nt,),
        out_shape=(dw_shape, dw_shape, db_shape, db_shape, jax.ShapeDtypeStruct((rows, k), F32),
                   jax.ShapeDtypeStruct(w_up.shape, BF16)),
        in_specs=[col, col, col, wsp(0), wsp(nt), bsp(0), bsp(nt), weight_block(0), weight_block(nt), any_spec],
        out_specs=(wsp(0), wsp(0), bsp(0), bsp(0), any_spec, any_spec),
        operands=[up_g, up_v, dact, cw, cw, cb, cb, w_up, w_up, hn],
        scratch_shapes=[pltpu.VMEM((rows, k), F32), pltpu.VMEM((rows, k), BF16), pltpu.VMEM((k, rows), BF16),
                        pltpu.VMEM((4, k, LANES), BF16), pltpu.SemaphoreType.DMA((5,))],
        semantics=("arbitrary",), steps=steps)


def _dt_fwd(name, dtr, bias):
    rows = dtr.shape[0]
    tm = _row_tile(rows, LANES)

    def body(d_ref, b_ref, o_ref):
        v = d_ref[...] + b_ref[...]
        sp = jnp.maximum(v, 0.0) + jnp.log1p(jnp.exp(-jnp.abs(v)))
        lane = lax.broadcasted_iota(jnp.int32, (tm, LANES), 1)
        ok = _rows_mask(pl.program_id(0), tm) & (lane < SSM_HEADS)
        o_ref[...] = jnp.where(ok, sp, 0.0)

    return pl.pallas_call(
        body, name=name, out_shape=jax.ShapeDtypeStruct((rows, LANES), F32), grid=(rows // tm,),
        in_specs=[pl.BlockSpec((tm, LANES), lambda i: (i, 0)), pl.BlockSpec((1, LANES), lambda i: (0, 0))],
        out_specs=pl.BlockSpec((tm, LANES), lambda i: (i, 0)), compiler_params=_cparams(("parallel",)),
    )(dtr, bias)


def _dt_bwd(name, ddt, dtr, bias):
    rows = dtr.shape[0]
    tm = _row_tile(rows, LANES)

    def body(g_ref, d_ref, b_ref, o_ref, db_ref):
        i = pl.program_id(0)
        lane = lax.broadcasted_iota(jnp.int32, (tm, LANES), 1)
        ok = _rows_mask(i, tm) & (lane < SSM_HEADS)
        dv = jnp.where(ok, g_ref[...] * _sigmoid(d_ref[...] + b_ref[...]), 0.0)
        o_ref[...] = dv.astype(BF16)

        @pl.when(i == 0)
        def _():
            db_ref[...] = jnp.zeros_like(db_ref)

        db_ref[...] += jnp.sum(dv, axis=0, keepdims=True)

    row_spec = pl.BlockSpec((tm, LANES), lambda i: (i, 0))
    vec_spec = pl.BlockSpec((1, LANES), lambda i: (0, 0))
    return pl.pallas_call(
        body, name=name,
        out_shape=(jax.ShapeDtypeStruct((rows, LANES), BF16), jax.ShapeDtypeStruct((1, LANES), F32)),
        grid=(rows // tm,), in_specs=[row_spec, row_spec, vec_spec], out_specs=(row_spec, vec_spec),
        compiler_params=_cparams(("arbitrary",)),
    )(ddt, dtr, bias)


def _gate_fwd(name, y, zx, w, steps=()):
    rows = y.shape[0]
    tm = _row_tile(rows, D_INNER)

    def body(y_ref, z_ref, w_ref, o_ref):
        z = z_ref[...]
        g = y_ref[...] * (z * _sigmoid(z))
        r = lax.rsqrt(jnp.mean(g * g, axis=-1, keepdims=True) + RMS_EPS)
        o_ref[...] = (g * r * w_ref[...]).astype(BF16)

    row_spec = pl.BlockSpec((tm, D_INNER), lambda i: (i, 0))
    return _call(
        body, name=name, out_shape=jax.ShapeDtypeStruct((rows, D_INNER), BF16), grid=(rows // tm,),
        in_specs=[row_spec, row_spec, pl.BlockSpec((1, D_INNER), lambda i: (0, 0))],
        out_specs=row_spec, operands=[y, zx, w], semantics=("parallel",), steps=steps)


def _gate_bwd(name, dyn, y, zx, w):
    rows = y.shape[0]
    tm = _row_tile(rows, D_INNER)

    def body(d_ref, y_ref, z_ref, w_ref, dy_ref, dz_ref, dw_ref):
        i = pl.program_id(0)
        z, yv = z_ref[...], y_ref[...]
        sig = _sigmoid(z)
        sz = z * sig
        g = yv * sz
        r = lax.rsqrt(jnp.mean(g * g, axis=-1, keepdims=True) + RMS_EPS)
        ghat = g * r
        dn = d_ref[...]
        dghat = dn * w_ref[...]
        dg = r * (dghat - ghat * jnp.mean(dghat * ghat, axis=-1, keepdims=True))
        dy_ref[...] = dg * sz
        dz_ref[...] = (dg * yv * sig * (1.0 + z * (1.0 - sig))).astype(BF16)

        @pl.when(i == 0)
        def _():
            dw_ref[...] = jnp.zeros_like(dw_ref)

        dw_ref[...] += jnp.sum(dn * ghat, axis=0, keepdims=True)

    row_spec = pl.BlockSpec((tm, D_INNER), lambda i: (i, 0))
    vec_spec = pl.BlockSpec((1, D_INNER), lambda i: (0, 0))
    return pl.pallas_call(
        body, name=name,
        out_shape=(jax.ShapeDtypeStruct((rows, D_INNER), F32), jax.ShapeDtypeStruct((rows, D_MAIN), BF16),
                   jax.ShapeDtypeStruct((1, D_INNER), F32)),
        grid=(rows // tm,), in_specs=[row_spec, row_spec, row_spec, vec_spec],
        out_specs=(row_spec, row_spec, vec_spec), compiler_params=_cparams(("arbitrary",)),
    )(dyn, y, zx, w)


def _split3(x):
    hi = x.astype(BF16)
    r1 = x - hi.astype(F32)
    mid = r1.astype(BF16)
    lo = (r1 - mid.astype(F32)).astype(BF16)
    return hi, mid, lo


def _dot3_data_lhs(x, sel):
    sel16 = sel.astype(F32).astype(BF16)
    hi, mid, lo = _split3(x)
    return _dot(hi, sel16) + _dot(mid, sel16) + _dot(lo, sel16)


def _dot2_data_lhs(x, sel):
    sel16 = sel.astype(F32).astype(BF16)
    hi = x.astype(BF16)
    mid = (x - hi.astype(F32)).astype(BF16)
    return _dot(hi, sel16) + _dot(mid, sel16)


def _dot3_data_rhs(sel, x):
    sel16 = sel.astype(F32).astype(BF16)
    hi, mid, lo = _split3(x)
    return _dot(sel16, hi) + _dot(sel16, mid) + _dot(sel16, lo)


def _causal_masks():
    r = lax.broadcasted_iota(jnp.int32, (CHUNK, CHUNK), 0)
    c = lax.broadcasted_iota(jnp.int32, (CHUNK, CHUNK), 1)
    return r >= c, r <= c


def _expand_heads_matrix(g):
    k = lax.broadcasted_iota(jnp.int32, (LANES, GROUP_W), 0)
    j = lax.broadcasted_iota(jnp.int32, (LANES, GROUP_W), 1)
    return HEADS_PER_GROUP * g + jnp.right_shift(j, 6) == k


def _reduce_heads_matrix(g):
    j = lax.broadcasted_iota(jnp.int32, (GROUP_W, LANES), 0)
    k = lax.broadcasted_iota(jnp.int32, (GROUP_W, LANES), 1)
    return HEADS_PER_GROUP * g + jnp.right_shift(j, 6) == k


def _reduce_pair_matrix(g, p):
    j = lax.broadcasted_iota(jnp.int32, (LANES, LANES), 0)
    k = lax.broadcasted_iota(jnp.int32, (LANES, LANES), 1)
    return HEADS_PER_GROUP * g + 2 * p + jnp.right_shift(j, 6) == k


def _group_cols(ref, g, width):
    return ref.at[:, pl.ds(g * width, width)]


def _ssd_prep(name, dt, a128, steps=()):
    rows = dt.shape[0]
    nc = rows // CHUNK

    def body(dt_ref, a_ref, dte_ref, acs_ref, acst_ref):
        causal, _ = _causal_masks()
        dtv = dt_ref[...]
        acs = _dot3_data_rhs(causal, dtv) * a_ref[...]
        acst_ref[...] = acs.T[0:SSM_HEADS]
        for g in range(N_GROUPS):
            expand = _expand_heads_matrix(g)
            _group_cols(dte_ref, g, GROUP_W)[...] = _dot3_data_lhs(dtv, expand)
            _group_cols(acs_ref, g, GROUP_W)[...] = _dot3_data_lhs(acs, expand)

    blk = pl.BlockSpec((CHUNK, D_INNER), lambda c: (c, 0))
    shp = jax.ShapeDtypeStruct((rows, D_INNER), F32)
    return _call(
        body, name=name, out_shape=(shp, shp, jax.ShapeDtypeStruct((nc, SSM_HEADS, CHUNK), F32)), grid=(nc,),
        in_specs=[pl.BlockSpec((CHUNK, LANES), lambda c: (c, 0)), pl.BlockSpec((1, LANES), lambda c: (0, 0))],
        out_specs=(blk, blk, pl.BlockSpec((None, SSM_HEADS, CHUNK), lambda c: (c, 0, 0))),
        operands=[dt, a128], semantics=("parallel",), steps=steps)


def _ssd_common(x_ref, b_ref, c_ref, dte_ref, acs_ref):
    x = x_ref[...]
    dt_exp = dte_ref[...]
    acs_exp = acs_ref[...]
    tot_exp = acs_ref[pl.ds(CHUNK - 1, 1), :]
    xdt = x * dt_exp
    e_exp = jnp.exp(acs_exp)
    f_exp = jnp.exp(tot_exp - acs_exp)
    return _causal_masks(), x, dt_exp, acs_exp, tot_exp, xdt, e_exp, f_exp, b_ref[...], c_ref[...]


def _pair_decay(acs_pair, acs_row, e, causal):
    lane = lax.broadcasted_iota(jnp.int32, (CHUNK, LANES), 1)
    mine = (lane < HEAD_DIM) if e == 0 else (lane >= HEAD_DIM)
    a_l = jnp.where(mine, acs_pair, pltpu.roll(acs_pair, HEAD_DIM, 1))
    seg = a_l - acs_row
    dm = jnp.where(causal[0], jnp.exp(jnp.minimum(seg, 0.0)), 0.0)
    dmt = jnp.where(causal[1], jnp.exp(jnp.minimum(-seg, 0.0)), 0.0)
    return dm, dmt


def _ssd_specs(index_of_chunk):
    wide = pl.BlockSpec((CHUNK, D_INNER), lambda c: (index_of_chunk(c), 0))
    b_spec = pl.BlockSpec((CHUNK, D_BC), lambda c: (index_of_chunk(c), D_INNER // D_BC))
    c_spec = pl.BlockSpec((CHUNK, D_BC), lambda c: (index_of_chunk(c), D_INNER // D_BC + 1))
    rows_spec = pl.BlockSpec((None, SSM_HEADS, CHUNK), lambda c: (index_of_chunk(c), 0, 0))
    state_spec = pl.BlockSpec((N_GROUPS, None, D_STATE, GROUP_W), lambda c: (0, index_of_chunk(c), 0, 0))
    return wide, b_spec, c_spec, rows_spec, state_spec


def _ssd_fwd(name, xbc, dt_exp, acs_exp, acs_rows, dskexp, steps=()):
    rows = xbc.shape[0]
    nc = rows // CHUNK

    def body(x_ref, b_ref, c_ref, dte_ref, acs_ref, acst_ref, dsk_ref, y_ref, st_ref, s_scr):
        @pl.when(pl.program_id(0) == 0)
        def _():
            s_scr[...] = jnp.zeros_like(s_scr)

        lane = lax.broadcasted_iota(jnp.int32, (CHUNK, LANES), 1)
        for g in range(N_GROUPS):
            y_g = _group_cols(y_ref, g, GROUP_W)
            causal, x, _, acs_exp_v, tot_exp, xdt, e_exp, f_exp, bm, cm = _ssd_common(
                _group_cols(x_ref, g, GROUP_W), _group_cols(b_ref, g, D_STATE), _group_cols(c_ref, g, D_STATE),
                _group_cols(dte_ref, g, GROUP_W), _group_cols(acs_ref, g, GROUP_W))
            state = s_scr[g]
            st_ref[g] = state
            cb16, bb16 = cm.astype(BF16), bm.astype(BF16)
            cb = _dot_nt(cb16, bb16)
            base = e_exp * _dot(cb16, state.astype(BF16)) + _group_cols(dsk_ref, g, GROUP_W)[...] * x
            for p in range(HEADS_PER_GROUP // 2):
                sl = slice(p * LANES, (p + 1) * LANES)
                xp = xdt[:, sl].astype(BF16)
                yd = []
                for e in range(2):
                    acs_row = acst_ref[pl.ds(g * HEADS_PER_GROUP + 2 * p + e, 1), :]
                    dm, _ = _pair_decay(acs_exp_v[:, sl], acs_row, e, causal)
                    yd.append(_dot((cb * dm).astype(BF16), xp))
                y_g[:, sl] = jnp.where(lane < HEAD_DIM, yd[0], yd[1]) + base[:, sl]
            s_scr[g] = jnp.exp(tot_exp) * state + _dot_tn(bb16, (f_exp * xdt).astype(BF16))

    wide, b_spec, c_spec, rows_spec, state_spec = _ssd_specs(lambda c: c)
    return _call(
        body, name=name,
        out_shape=(jax.ShapeDtypeStruct((rows, D_INNER), F32),
                   jax.ShapeDtypeStruct((N_GROUPS, nc, D_STATE, GROUP_W), F32)),
        grid=(nc,),
        in_specs=[wide, b_spec, c_spec, wide, wide, rows_spec, pl.BlockSpec((1, D_INNER), lambda c: (0, 0))],
        out_specs=(wide, state_spec),
        scratch_shapes=[pltpu.VMEM((N_GROUPS, D_STATE, GROUP_W), F32)],
        operands=[xbc, xbc, xbc, dt_exp, acs_exp, acs_rows, dskexp], semantics=("arbitrary",), steps=steps)


def _ssd_bwd(name, xbc, dt_exp, acs_exp, acs_rows, dt, a128, dskexp, dy, states, steps=()):
    rows = xbc.shape[0]
    nc = rows // CHUNK
    last = nc - 1

    def body(x_ref, b_ref, c_ref, dte_ref, acs_ref, acst_ref, dt_ref, a128_ref, dsk_all, dy_all, st_all,
             dxbc_all, ddt_ref, dalog_ref, ddsk_ref, ds_all):
        dx_all, db_all, dc_all = (dxbc_all.at[:, :D_INNER], dxbc_all.at[:, D_INNER:D_INNER + D_BC],
                                  dxbc_all.at[:, D_INNER + D_BC:])

        @pl.when(pl.program_id(0) == 0)
        def _():
            ds_all[...] = jnp.zeros_like(ds_all)
            dalog_ref[...] = jnp.zeros_like(dalog_ref)
            ddsk_ref[...] = jnp.zeros_like(ddsk_ref)

        dacs = jnp.zeros((CHUNK, LANES), F32)
        ddt_x = jnp.zeros((CHUNK, LANES), F32)
        for g in range(N_GROUPS):
            dacs_g, ddt_x_g = group(
                g, _group_cols(x_ref, g, GROUP_W), _group_cols(b_ref, g, D_STATE), _group_cols(c_ref, g, D_STATE),
                _group_cols(dte_ref, g, GROUP_W), _group_cols(acs_ref, g, GROUP_W), acst_ref,
                _group_cols(dsk_all, g, GROUP_W), _group_cols(dy_all, g, GROUP_W), st_all.at[g],
                _group_cols(dx_all, g, GROUP_W), _group_cols(db_all, g, D_STATE), _group_cols(dc_all, g, D_STATE),
                ddsk_ref, ds_all.at[g])
            dacs, ddt_x = dacs + dacs_g, ddt_x + ddt_x_g
        _, causal_t = _causal_masks()
        da = _dot3_data_rhs(causal_t, dacs)
        ddt_ref[...] = da * a128_ref[...] + ddt_x
        dalog_ref[...] += jnp.sum(da * dt_ref[...], axis=0, keepdims=True) * a128_ref[...]

    def group(g, x_ref, b_ref, c_ref, dte_ref, acs_ref, acst_ref, dsk_ref, dy_ref, st_ref,
              dx_ref, db_ref, dc_ref, ddsk_ref, ds_scr):
        causal, x, dt_exp, acs_exp_v, tot_exp, xdt, e_exp, f_exp, bm, cm = _ssd_common(
            x_ref, b_ref, c_ref, dte_ref, acs_ref)
        reduce_heads = _reduce_heads_matrix(g)
        state, dstate = st_ref[...], ds_scr[...]
        dyv = dy_ref[...]
        cb16, bb16 = cm.astype(BF16), bm.astype(BF16)
        s16, ds16 = state.astype(BF16), dstate.astype(BF16)
        cb = _dot_nt(cb16, bb16)
        cbt = _dot_nt(bb16, cb16)
        cs = _dot(cb16, s16)
        bds = _dot(bb16, ds16)
        edy = e_exp * dyv
        fx = f_exp * xdt
        dxdt_base = f_exp * bds
        dc_acc = _dot_nt(edy.astype(BF16), s16)
        db_acc = _dot_nt(fx.astype(BF16), ds16)
        ds_scr[...] = jnp.exp(tot_exp) * dstate + _dot_tn(cb16, edy.astype(BF16))
        q = fx * bds
        dacs = _dot2_data_lhs(edy * cs - q, reduce_heads)
        dtot = jnp.sum(_dot2_data_lhs(q + jnp.exp(tot_exp) * dstate * state, reduce_heads), axis=0, keepdims=True)
        ddsk_ref[...] += jnp.sum(_dot2_data_lhs(dyv * x, reduce_heads), axis=0, keepdims=True)
        lane = lax.broadcasted_iota(jnp.int32, (CHUNK, LANES), 1)
        dcb = jnp.zeros((CHUNK, CHUNK), F32)
        dcbt = jnp.zeros((CHUNK, CHUNK), F32)
        ddt_x = jnp.zeros((CHUNK, LANES), F32)
        for p in range(HEADS_PER_GROUP // 2):
            sl = slice(p * LANES, (p + 1) * LANES)
            xp, dyp = xdt[:, sl], dyv[:, sl]
            xp16, dyp16 = xp.astype(BF16), dyp.astype(BF16)
            dxh = []
            for e in range(2):
                h = 2 * p + e
                mine = (lane < HEAD_DIM) if e == 0 else (lane >= HEAD_DIM)
                acs_row = acst_ref[pl.ds(g * HEADS_PER_GROUP + h, 1), :]
                dm, dmt = _pair_decay(acs_exp_v[:, sl], acs_row, e, causal)
                m, mt = cb * dm, cbt * dmt
                xh16 = jnp.where(mine, xp, 0.0).astype(BF16)
                dyh16 = jnp.where(mine, dyp, 0.0).astype(BF16)
                d_m = _dot_nt(dyh16, xp16)
                d_mt = _dot_nt(xh16, dyp16)
                dacs_h = (jnp.sum(d_m * m, axis=-1, keepdims=True)
                          - jnp.sum(d_mt * mt, axis=-1, keepdims=True))
                dacs = dacs + jnp.where(lane == HEADS_PER_GROUP * g + h, dacs_h, 0.0)
                dcb = dcb + d_m * dm
                dcbt = dcbt + d_mt * dmt
                dxh.append(_dot(mt.astype(BF16), dyp16))
            dxdt = jnp.where(lane < HEAD_DIM, dxh[0], dxh[1]) + dxdt_base[:, sl]
            dx_ref[:, sl] = dxdt * dt_exp[:, sl] + dsk_ref[:, sl] * dyp
            ddt_x = ddt_x + _dot2_data_lhs(dxdt * x[:, sl], _reduce_pair_matrix(g, p))
        dc_ref[...] = dc_acc + _dot(dcb.astype(BF16), bb16)
        db_ref[...] = db_acc + _dot(dcbt.astype(BF16), cb16)
        row = lax.broadcasted_iota(jnp.int32, (CHUNK, LANES), 0)
        return dacs + jnp.where(row == CHUNK - 1, dtot, 0.0), ddt_x

    wide, b_spec, c_spec, rows_spec, state_spec = _ssd_specs(lambda c: last - c)
    heads_spec = pl.BlockSpec((CHUNK, LANES), lambda c: (last - c, 0))
    vec_spec = pl.BlockSpec((1, LANES), lambda c: (0, 0))
    vec_shape = jax.ShapeDtypeStruct((1, LANES), F32)
    return _call(
        body, name=name,
        out_shape=(jax.ShapeDtypeStruct((rows, D_XBC), F32), jax.ShapeDtypeStruct((rows, LANES), F32),
                   vec_shape, vec_shape),
        grid=(nc,),
        in_specs=[wide, b_spec, c_spec, wide, wide, rows_spec, heads_spec, vec_spec,
                  pl.BlockSpec((1, D_INNER), lambda c: (0, 0)), wide, state_spec],
        out_specs=(pl.BlockSpec((CHUNK, D_XBC), lambda c: (last - c, 0)), heads_spec, vec_spec, vec_spec),
        scratch_shapes=[pltpu.VMEM((N_GROUPS, D_STATE, GROUP_W), F32)],
        operands=[xbc, xbc, xbc, dt_exp, acs_exp, acs_rows, dt, a128, dskexp, dy, states],
        semantics=("arbitrary",), steps=steps)


def _attn_visible(b, heads=1):
    row = jnp.bitwise_and(lax.broadcasted_iota(jnp.int32, (heads * CHUNK, 3 * CHUNK), 0), CHUNK - 1)
    col = lax.broadcasted_iota(jnp.int32, (heads * CHUNK, 3 * CHUNK), 1)
    bb = b + jnp.zeros_like(col)
    meta = (col < CHUNK) & (bb >= 1) & (col >= PAD_ROWS)
    prev = (col >= CHUNK) & (col < 2 * CHUNK) & (bb >= 2) & ((col - CHUNK) > row)
    cur = (col >= 2 * CHUNK) & ((col - 2 * CHUNK) <= row) & ((bb >= 1) | ((col - 2 * CHUNK) >= PAD_ROWS))
    return meta | prev | cur


def _attn_visible4(b):
    return _attn_visible(b, 4)


def _stack_heads(q_ref, sink_ref, kvh, scale):
    lane = lax.broadcasted_iota(jnp.int32, (CHUNK, LANES), 1)
    parts, sinks = [], []
    for pp in range(2):
        pair = kvh * 2 + pp
        qp = q_ref[:, pair * LANES:(pair + 1) * LANES] * scale
        for e in range(2):
            mine = (lane < HEAD_DIM) if e == 0 else (lane >= HEAD_DIM)
            parts.append(jnp.where(mine, qp, 0.0).astype(BF16))
            sinks.append(jnp.full((CHUNK, 1), sink_ref[2 * pair + e], F32))
    return jnp.concatenate(parts, axis=0), jnp.concatenate(sinks, axis=0)


def _attn_operands(q_ref, k0, kp, kc, v0, vp, vc, sink_ref):
    kcat, vcat, q4, sink4 = [], [], [], []
    for kvh in range(N_KV_HEADS):
        ksl = slice(kvh * LANES, (kvh + 1) * LANES)
        kcat.append(jnp.concatenate([k0[:, ksl], kp[:, ksl], kc[:, ksl]], axis=0).astype(BF16))
        vcat.append(jnp.concatenate([v0[:, ksl], vp[:, ksl], vc[:, ksl]], axis=0).astype(BF16))
        stacked, sinks = _stack_heads(q_ref, sink_ref, kvh, ATTN_SCALE)
        q4.append(stacked)
        sink4.append(sinks)
    return kcat, vcat, q4, sink4


def _attn_probs(q4, kcat, visible, sink4):
    heads = range(N_KV_HEADS)
    s = [jnp.where(visible, _dot_nt(q4[h], kcat[h]), NEG_INF) for h in heads]
    m = [jnp.maximum(jnp.max(s[h], axis=-1, keepdims=True), sink4[h]) for h in heads]
    pe = [jnp.exp(s[h] - m[h]) for h in heads]
    pe_sink = [jnp.exp(sink4[h] - m[h]) for h in heads]
    inv = [1.0 / (jnp.sum(pe[h], axis=-1, keepdims=True) + pe_sink[h]) for h in heads]
    return [pe[h] * inv[h] for h in heads], [pe_sink[h] * inv[h] for h in heads]


def _unstack_pairs(stacked, pp):
    lane = lax.broadcasted_iota(jnp.int32, (CHUNK, LANES), 1)
    return jnp.where(lane < HEAD_DIM, stacked[(2 * pp) * CHUNK:(2 * pp + 1) * CHUNK],
                     stacked[(2 * pp + 1) * CHUNK:(2 * pp + 2) * CHUNK])


def _attn_specs(colblock):
    blk = lambda f: pl.BlockSpec((CHUNK, 2 * D_KV), f)
    return [blk(lambda b: (0, colblock)), blk(lambda b: (jnp.maximum(b - 1, 0), colblock)), blk(lambda b: (b, colblock))]


def _attn_fwd(name, q, kv2, sinks, steps=()):
    rows = q.shape[0]

    def body(q_ref, k0, kp, kc, v0, vp, vc, sink_ref, o_ref):
        visible = _attn_visible4(pl.program_id(0))
        kcat, vcat, q4, sink4 = _attn_operands(q_ref, k0, kp, kc, v0, vp, vc, sink_ref)
        pn, _ = _attn_probs(q4, kcat, visible, sink4)
        o4 = [_dot(pn[h].astype(BF16), vcat[h]) for h in range(N_KV_HEADS)]
        for kvh in range(N_KV_HEADS):
            for pp in range(2):
                qsl = slice((kvh * 2 + pp) * LANES, (kvh * 2 + pp + 1) * LANES)
                o_ref[:, qsl] = _unstack_pairs(o4[kvh], pp).astype(BF16)

    return _call(
        body, name=name, out_shape=jax.ShapeDtypeStruct((rows, D_MODEL), BF16), grid=(rows // CHUNK,),
        in_specs=[pl.BlockSpec((CHUNK, D_MODEL), lambda b: (b, 0))] + _attn_specs(0) + _attn_specs(1)
        + [pl.BlockSpec(memory_space=pltpu.SMEM)],
        out_specs=pl.BlockSpec((CHUNK, D_MODEL), lambda b: (b, 0)),
        operands=[q, kv2, kv2, kv2, kv2, kv2, kv2, sinks], semantics=("parallel",), steps=steps)


def _attn_bwd(name, q, kv2, sinks, do, steps=()):
    rows = q.shape[0]

    def body(q_ref, k0, kp, kc, v0, vp, vc, sink_ref, do_ref,
             dq_ref, dkc_ref, dkp_ref, dvc_ref, dvp_ref, dkm_ref, dvm_ref, dsink_ref):
        @pl.when(pl.program_id(0) == 0)
        def _():
            dkm_ref[...] = jnp.zeros_like(dkm_ref)
            dvm_ref[...] = jnp.zeros_like(dvm_ref)
            dsink_ref[...] = jnp.zeros_like(dsink_ref)

        visible = _attn_visible4(pl.program_id(0))
        heads = range(N_KV_HEADS)
        lane1 = lax.broadcasted_iota(jnp.int32, (1, LANES), 1)
        kcat, vcat, q4, sink4 = _attn_operands(q_ref, k0, kp, kc, v0, vp, vc, sink_ref)
        do4 = [_stack_heads(do_ref, sink_ref, h, 1.0)[0] for h in heads]
        pn, psink = _attn_probs(q4, kcat, visible, sink4)
        dp = [_dot_nt(do4[h], vcat[h]) for h in heads]
        delta = [jnp.sum(pn[h] * dp[h], axis=-1, keepdims=True) for h in heads]
        ds16 = [(pn[h] * (dp[h] - delta[h])).astype(BF16) for h in heads]
        dq4 = [_dot(ds16[h], kcat[h]) for h in heads]
        dk_acc = [_dot_tn(ds16[h], q4[h]) for h in heads]
        dv_acc = [_dot_tn(pn[h].astype(BF16), do4[h]) for h in heads]
        dsink = jnp.zeros((1, LANES), F32)
        for kvh in heads:
            ksl = slice(kvh * LANES, (kvh + 1) * LANES)
            sink_terms = psink[kvh] * delta[kvh]
            for j in range(4):
                part = jnp.sum(sink_terms[j * CHUNK:(j + 1) * CHUNK], axis=0, keepdims=True)
                dsink = dsink - jnp.where(lane1 == kvh * 4 + j, part, 0.0)
            for pp in range(2):
                qsl = slice((kvh * 2 + pp) * LANES, (kvh * 2 + pp + 1) * LANES)
                dq_ref[:, qsl] = (_unstack_pairs(dq4[kvh], pp) * ATTN_SCALE).astype(BF16)
            dkm_ref[:, ksl] += dk_acc[kvh][0:CHUNK]
            dvm_ref[:, ksl] += dv_acc[kvh][0:CHUNK]
            dkp_ref[:, ksl] = dk_acc[kvh][CHUNK:2 * CHUNK]
            dvp_ref[:, ksl] = dv_acc[kvh][CHUNK:2 * CHUNK]
            dkc_ref[:, ksl] = dk_acc[kvh][2 * CHUNK:3 * CHUNK]
            dvc_ref[:, ksl] = dv_acc[kvh][2 * CHUNK:3 * CHUNK]
        dsink_ref[...] += dsink

    qspec = pl.BlockSpec((CHUNK, D_MODEL), lambda b: (b, 0))
    kvspec = pl.BlockSpec((CHUNK, 2 * D_KV), lambda b: (b, 0))
    fixed = pl.BlockSpec((CHUNK, 2 * D_KV), lambda b: (0, 0))
    kv_shape = jax.ShapeDtypeStruct((rows, 2 * D_KV), F32)
    meta_shape = jax.ShapeDtypeStruct((CHUNK, 2 * D_KV), F32)
    return _call(
        body, name=name,
        out_shape=(jax.ShapeDtypeStruct((rows, D_MODEL), BF16), kv_shape, kv_shape, kv_shape, kv_shape,
                   meta_shape, meta_shape, jax.ShapeDtypeStruct((1, LANES), F32)),
        grid=(rows // CHUNK,),
        in_specs=[qspec] + _attn_specs(0) + _attn_specs(1) + [pl.BlockSpec(memory_space=pltpu.SMEM), qspec],
        out_specs=(qspec, kvspec, kvspec, kvspec, kvspec, fixed, fixed, pl.BlockSpec((1, LANES), lambda b: (0, 0))),
        operands=[q, kv2, kv2, kv2, kv2, kv2, kv2, sinks, do], semantics=("arbitrary",), steps=steps)


def _kv_grad_combine(name, dk_cur, dk_prev, dk_meta, dv_cur, dv_prev, dv_meta):
    rows = dk_cur.shape[0]
    nb = rows // CHUNK
    width = 2 * D_KV

    def body(kc_ref, kp_ref, km_ref, vc_ref, vp_ref, vm_ref, o_ref):
        jj = pl.program_id(0) + jnp.zeros((CHUNK, 1), jnp.int32)
        for half, (c_ref, p_ref, m_ref) in enumerate(((kc_ref, kp_ref, km_ref), (vc_ref, vp_ref, vm_ref))):
            total = c_ref[...] + jnp.where(jj < nb - 1, p_ref[...], 0.0) + jnp.where(jj == 0, m_ref[...], 0.0)
            o_ref[:, half * width:(half + 1) * width] = total.astype(BF16)

    blk = lambda f: pl.BlockSpec((CHUNK, width), f)
    three = lambda: [blk(lambda j: (j, 0)), blk(lambda j: (jnp.minimum(j + 1, nb - 1), 0)), blk(lambda j: (0, 0))]
    return pl.pallas_call(
        body, name=name, out_shape=jax.ShapeDtypeStruct((rows, 2 * width), BF16), grid=(nb,),
        in_specs=three() + three(), out_specs=pl.BlockSpec((CHUNK, 2 * width), lambda j: (j, 0)),
        compiler_params=_cparams(("parallel",)),
    )(dk_cur, dk_prev, dk_meta, dv_cur, dv_prev, dv_meta)


def _adamw(name, w, g, m, v, steps=()):
    rows, width = w.shape
    tr = rows
    for cand in range(8, rows + 1, 8):
        if rows % cand == 0 and cand * width * 4 <= (1 << 20):
            tr = cand

    def body(*refs):
        _adamw_update(*refs)

    blk = pl.BlockSpec((tr, width), lambda i: (i, 0))
    shp = jax.ShapeDtypeStruct((rows, width), F32)
    return _call(body, name=name, out_shape=(shp, shp, shp), grid=(rows // tr,), in_specs=[blk] * 4,
                 out_specs=(blk,) * 3, operands=[w, g, m, v], semantics=("parallel",), steps=steps)


def _adamw_update(w_ref, g_ref, m_ref, v_ref, d_ref, mo_ref, vo_ref):
    gv = g_ref[...]
    mn = ADAM_B1 * m_ref[...] + (1.0 - ADAM_B1) * gv
    vn = ADAM_B2 * v_ref[...] + (1.0 - ADAM_B2) * (gv * gv)
    m_hat = mn / (1.0 - ADAM_B1 ** ADAM_STEP)
    v_hat = vn / (1.0 - ADAM_B2 ** ADAM_STEP)
    d_ref[...] = -ADAM_LR * (m_hat / (jnp.sqrt(v_hat) + ADAM_EPS) + ADAM_WD * w_ref[...])
    mo_ref[...] = mn
    vo_ref[...] = vn


def _adamw_small(name, ws, gs, ms, vs):
    n = len(ws)

    def body(*refs):
        for i in range(n):
            _adamw_update(*refs[i::n])

    shapes = [jax.ShapeDtypeStruct(a.shape, F32) for a in ws]
    outs = pl.pallas_call(body, name=name, out_shape=shapes * 3, in_specs=[VMEM_SPEC] * (4 * n),
                          out_specs=[VMEM_SPEC] * (3 * n), compiler_params=_cparams())(*ws, *gs, *ms, *vs)
    return outs[:n], outs[n:2 * n], outs[2 * n:]


def _ffn_fwd(tag, h, hn, p, i, plan):
    up_g, up_v, act = _ffn_up_conv(f"ffn{tag}_up", hn, plan.weight("f_w_up", i), p["f_conv_w"][i],
                                   p["f_conv_b"][i:i + 1], steps=plan.steps(f"ffn{tag}_up"))
    pre = _mm(f"ffn{tag}_down", act, plan.weight("f_w_down", i), "nn", steps=plan.steps(f"ffn{tag}_down"))
    return pre, (h, hn, up_g, up_v, act, pre)


def _ffn_bwd(tag, dpre, saved, p, i, plan):
    h, hn, up_g, up_v, act, pre = saved
    plan.grad("f_w_down", i, _mm(f"ffn{tag}_down_dw", act, dpre, "tn", out_dtype=BF16))
    dact = _mm(f"ffn{tag}_down_dx", dpre, plan.weight("f_w_down", i), "nt", steps=plan.steps(f"ffn{tag}_down_dx"))
    gwg, gwv, gbg, gbv, dhn, g_up = _ffn_conv_bwd(
        f"ffn{tag}_conv_bwd", up_g, up_v, dact, p["f_conv_w"][i], p["f_conv_b"][i:i + 1], hn,
        plan.weight("f_w_up", i), steps=plan.steps(f"ffn{tag}_conv_bwd"))
    g_cw, g_cb = jnp.concatenate([gwg, gwv], axis=1), jnp.concatenate([gbg, gbv], axis=1)
    plan.grad("f_w_up", i, g_up)
    return dhn, dict(f_conv_w=g_cw, f_conv_b=g_cb)


def _lanes_pad(a, width=LANES):
    return jnp.pad(a, [(0, 0)] * (a.ndim - 1) + [(0, width - a.shape[-1])])


def _dup_heads(w):
    rows = w.shape[0]
    w = w.reshape(rows, 2 * N_KV_HEADS, 1, HEAD_DIM)
    return jnp.broadcast_to(w, (rows, 2 * N_KV_HEADS, 2, HEAD_DIM)).reshape(rows, 4 * D_KV)


def _undup_heads(g):
    rows = g.shape[0]
    return g.reshape(rows, 2 * N_KV_HEADS, 2, HEAD_DIM).sum(axis=2).reshape(rows, 2 * D_KV)


def _local_step(x2, target, p, plan):
    seq = x2.shape[0]
    rows = seq + CHUNK
    g = {}

    h0 = jnp.concatenate([jnp.zeros((PAD_ROWS, D_MODEL), F32), p["meta_tokens"], x2], axis=0)

    w_in = plan.weight("a_w_in")
    w_dt = jnp.pad(w_in[D_MAIN:], ((0, LANES - SSM_HEADS), (0, 0)))
    dt_bias = _lanes_pad(p["a_dt_bias"])
    a128 = _lanes_pad(-jnp.exp(p["a_a_log"]))
    dskexp = jnp.repeat(p["a_d_skip"].reshape(SSM_HEADS), HEAD_DIM).reshape(1, D_INNER)

    hn0 = _rms_fwd("a_norm", h0, p["a_norm_pre"])
    zx = _mm("a_in_main", hn0, w_in, "nt", k_rows=D_MAIN, steps=plan.steps("a_in_main"))
    dtr = _mm("a_in_dt", hn0, w_dt, "nt")
    xbc = _conv4_fwd("a_conv", zx, p["a_conv_w"], p["a_conv_b"], steps=plan.steps("a_conv"))
    dt = _dt_fwd("a_dt", dtr, dt_bias)
    dt_exp, acs_exp, acs_rows = _ssd_prep("a_ssd_prep", dt, a128, steps=plan.steps("a_ssd_prep"))
    y, states = _ssd_fwd("a_ssd", xbc, dt_exp, acs_exp, acs_rows, dskexp, steps=plan.steps("a_ssd"))
    yn = _gate_fwd("a_gate", y, zx, p["a_gate_norm"], steps=plan.steps("a_gate"))
    mix = _mm("a_out", yn, plan.weight("a_w_out"), "nn", steps=plan.steps("a_out"))
    h1, (hn_f0,) = _resid_norm_fwd("a_resid", h0, mix, p["a_norm_post"], [p["f_norm_pre"][0:1]])

    pre_f0, ffn0 = _ffn_fwd("0", h1, hn_f0, p, 0, plan)
    h2, (hkv, hn2) = _resid_norm_fwd("ffn0_resid", h1, pre_f0, p["f_norm_post"][0:1], [p["kv_norm"], p["b_norm_pre"]])

    w_kv2 = _dup_heads(plan.weight("w_kv"))
    kv2 = _mm("kv_proj", hkv, w_kv2, "nn")
    q = _mm("b_q", hn2, plan.weight("b_w_q"), "nn")
    sinks = p["b_sinks"].reshape(N_Q_HEADS)
    o = _attn_fwd("b_attn", q, kv2, sinks, steps=plan.steps("b_attn"))
    attn = _mm("b_o", o, plan.weight("b_w_o"), "nn", steps=plan.steps("b_o"))
    h3, (hn_f1,) = _resid_norm_fwd("b_resid", h2, attn, p["b_norm_post"], [p["f_norm_pre"][1:2]])

    pre_f1, ffn1 = _ffn_fwd("1", h3, hn_f1, p, 1, plan)
    dh, loss_vec, dpre_f1, g_post1 = _resid_norm_loss("ffn1_resid_loss", h3, pre_f1, p["f_norm_post"][1:2], target)
    loss = loss_vec[0, 0]

    dhn_f1, g1 = _ffn_bwd("1", dpre_f1, ffn1, p, 1, plan)
    dh, g_pre1, dpre, g["b_norm_post"] = _norm_bwd_add("ffn1_norm_bwd", dh, dhn_f1, h3, p["f_norm_pre"][1:2],
                                                        then=(attn, p["b_norm_post"]))
    plan.grad("b_w_o", None, _mm("b_o_dw", o, dpre, "tn", out_dtype=BF16))
    do = _mm("b_o_dx", dpre, plan.weight("b_w_o"), "nt", steps=plan.steps("b_o_dx"))
    dq, dkc, dkp, dvc, dvp, dkm, dvm, dsink = _attn_bwd("b_attn_bwd", q, kv2, sinks, do, steps=plan.steps("b_attn_bwd"))
    g["b_sinks"] = dsink[:, :N_Q_HEADS]
    dhn2 = _mm("b_q_dx", dq, plan.weight("b_w_q"), "nt")
    plan.grad("b_w_q", None, _mm("b_q_dw", hn2, dq, "tn", out_dtype=BF16))
    dh, g["b_norm_pre"] = _norm_bwd_add("b_norm_bwd", dh, dhn2, h2, p["b_norm_pre"])
    dkv2 = _kv_grad_combine("kv_grad", dkc, dkp, dkm, dvc, dvp, dvm)
    dhkv = _mm("kv_proj_dx", dkv2, w_kv2, "nt")
    plan.grad("w_kv", None, _undup_heads(_mm("kv_proj_dw", hkv, dkv2, "tn")))
    dh, g["kv_norm"], dpre_f0, g_post0 = _norm_bwd_add("kv_norm_bwd", dh, dhkv, h2, p["kv_norm"],
                                                       then=(pre_f0, p["f_norm_post"][0:1]))

    dhn_f0, g0 = _ffn_bwd("0", dpre_f0, ffn0, p, 0, plan)
    dh, g_pre0, dpre, g["a_norm_post"] = _norm_bwd_add("ffn0_norm_bwd", dh, dhn_f0, h1, p["f_norm_pre"][0:1],
                                                        then=(mix, p["a_norm_post"]))
    g["f_norm_post"] = jnp.concatenate([g_post0, g_post1], axis=0)
    g["f_norm_pre"] = jnp.concatenate([g_pre0, g_pre1], axis=0)
    g["f_conv_w"] = jnp.stack([g0["f_conv_w"], g1["f_conv_w"]])
    g["f_conv_b"] = jnp.concatenate([g0["f_conv_b"], g1["f_conv_b"]], axis=0)
    plan.grad("a_w_out", None, _mm("a_out_dw", yn, dpre, "tn", out_dtype=BF16))
    dyn = _mm("a_out_dx", dpre, plan.weight("a_w_out"), "nt", steps=plan.steps("a_out_dx"))
    dy, dzx, g["a_gate_norm"] = _gate_bwd("a_gate_bwd", dyn, y, zx, p["a_gate_norm"])
    dxbc, ddt, dalog, ddsk = _ssd_bwd("a_ssd_bwd", xbc, dt_exp, acs_exp, acs_rows, dt, a128, dskexp, dy, states,
                                      steps=plan.steps("a_ssd_bwd"))
    g["a_a_log"] = dalog[:, :SSM_HEADS]
    g["a_d_skip"] = ddsk[:, :SSM_HEADS]
    ddtr, dbias = _dt_bwd("a_dt_bwd", ddt, dtr, dt_bias)
    g["a_dt_bias"] = dbias[:, :SSM_HEADS]
    dzx, g["a_conv_w"], g["a_conv_b"] = _conv4_bwd("a_conv_bwd", zx, dxbc, p["a_conv_w"], p["a_conv_b"], dzx)
    g_in = _mm("a_in_main_dw", dzx, hn0, "tn", out_dtype=BF16, out_rows=D_IN_PROJ, steps=plan.steps("a_in_main_dw"))
    plan.grad("a_w_in", None, _tn_rows_into("a_in_dt_dw", ddtr, hn0, g_in, D_MAIN, SSM_HEADS))
    dhn0 = _mm("a_in_dt_dx", ddtr, w_dt, "nn", steps=plan.steps("a_in_dt_dx"))
    dhn0 = _mm("a_in_main_dx", dzx, w_in, "nn", acc=dhn0, steps=plan.steps("a_in_main_dx"))
    dh_first, grad_x, g["a_norm_pre"] = _norm_bwd_add("a_norm_bwd", dh, dhn0, h0, p["a_norm_pre"],
                                                      split_first_block=True, steps=plan.steps("a_norm_bwd"))
    g["meta_tokens"] = dh_first[PAD_ROWS:]
    return loss, grad_x, g


ANY = pl.BlockSpec(memory_space=pl.ANY)
VMEM_SPEC = pl.BlockSpec(memory_space=pltpu.VMEM)


def _allgather_small(name, shard):
    rows = shard.shape[0]

    def body(s_ref, o_ref, send_sems, recv_sems):
        x, y, c = _place()
        me = 2 * x + y
        o_ref[me] = s_ref[...]
        chips = _other_chips(x, y)
        sends = [pltpu.make_async_remote_copy(s_ref, o_ref.at[me], send_sems.at[j], recv_sems.at[j],
                                              device_id=(cx, cy, c), device_id_type=MESH)
                 for j, (cx, cy) in enumerate(chips)]
        for cp in sends:
            cp.start()
        for j, (cx, cy) in enumerate(chips):
            pltpu.make_async_remote_copy(s_ref, o_ref.at[2 * cx + cy], send_sems.at[j], recv_sems.at[j],
                                         device_id=(cx, cy, c), device_id_type=MESH).wait_recv()
        for cp in sends:
            cp.wait_send()

    return pl.pallas_call(
        body, name=name, out_shape=jax.ShapeDtypeStruct((N_CHIPS, rows, LANES), F32),
        in_specs=[VMEM_SPEC], out_specs=VMEM_SPEC,
        scratch_shapes=[pltpu.SemaphoreType.DMA((3,)), pltpu.SemaphoreType.DMA((3,))],
        compiler_params=pltpu.CompilerParams(vmem_limit_bytes=VMEM_LIMIT),
    )(shard)


def _row_block(rows, width, itemsize, align, budget=2 << 20):
    best = rows
    for cand in range(align, rows + 1, align):
        if rows % cand == 0 and cand * width * itemsize <= budget:
            best = cand
    return best


def _cast_into_slot(name, chip, w, layer=None):
    rows, width = w.shape[-2:]
    tr = _row_block(rows, width, 4, 16)
    if layer is None:
        in_spec = pl.BlockSpec((tr, width), lambda i, chip_ref: (i, 0))
    else:
        in_spec = pl.BlockSpec((None, tr, width), lambda i, chip_ref: (layer, i, 0))

    def body(chip_ref, w_ref, o_ref):
        o_ref[...] = w_ref[...].astype(BF16)

    return pl.pallas_call(
        body, name=name, out_shape=jax.ShapeDtypeStruct((N_CHIPS, rows, width), BF16),
        grid_spec=pltpu.PrefetchScalarGridSpec(
            num_scalar_prefetch=1, grid=(rows // tr,), in_specs=[in_spec],
            out_specs=pl.BlockSpec((None, tr, width), lambda i, chip_ref: (chip_ref[0], i, 0))),
        compiler_params=_cparams(("parallel",)),
    )(chip, w)


def _allreduce_small(name, vec):
    rows = -(-vec.shape[0] // (2 * SUBLANES)) * (2 * SUBLANES)
    hr = rows // 2
    padded = jnp.pad(vec, ((0, rows - vec.shape[0]), (0, 0)))

    def body(v_ref, o_ref, theirs, pair, by_chip, send_sems, recv_sems):
        x, y, c = _place()
        me = 2 * x + y
        sibling = (x, y, 1 - c)
        mine = pl.ds(pl.multiple_of(c * hr, SUBLANES), hr)
        other = pl.ds(pl.multiple_of((1 - c) * hr, SUBLANES), hr)

        swap = _remote(v_ref, theirs, send_sems, recv_sems, 0, sibling)
        swap.start()
        swap.wait()
        south = (c + jnp.zeros((1, 1), jnp.int32)) == 0
        pair[...] = jnp.where(south, v_ref[...], theirs[...]) + jnp.where(south, theirs[...], v_ref[...])

        by_chip[me] = pair[mine, :]
        sends = [_remote(by_chip.at[me], by_chip.at[me], send_sems, recv_sems, 1 + j, (cx, cy, c))
                 for j, (cx, cy) in enumerate(_other_chips(x, y))]
        for cp in sends:
            cp.start()
        for j, (cx, cy) in enumerate(_other_chips(x, y)):
            _remote(by_chip.at[me], by_chip.at[2 * cx + cy], send_sems, recv_sems, 1 + j, (cx, cy, c)).wait_recv()
        for cp in sends:
            cp.wait_send()
        total = by_chip[0]
        for s in range(1, N_CHIPS):
            total = total + by_chip[s]

        o_ref[mine, :] = total
        back = _remote(o_ref.at[mine], o_ref.at[mine], send_sems, recv_sems, 4, sibling)
        back.start()
        _remote(o_ref.at[other], o_ref.at[other], send_sems, recv_sems, 4, sibling).wait_recv()
        back.wait_send()

    out = pl.pallas_call(
        body, name=name, out_shape=jax.ShapeDtypeStruct((rows, LANES), F32),
        in_specs=[VMEM_SPEC], out_specs=VMEM_SPEC,
        scratch_shapes=[pltpu.VMEM((rows, LANES), F32), pltpu.VMEM((rows, LANES), F32),
                        pltpu.VMEM((N_CHIPS, hr, LANES), F32), pltpu.SemaphoreType.DMA((5,)),
                        pltpu.SemaphoreType.DMA((5,))],
        compiler_params=pltpu.CompilerParams(vmem_limit_bytes=VMEM_LIMIT),
    )(padded)
    return out[:vec.shape[0]]


def _rs_pair_add(name, place, grads, partner, split="rows"):
    _, half_rows, width = partner.shape
    tr = _row_block(half_rows, width, 2, 16)
    nb = half_rows // tr
    if split == "rows":
        mine = pl.BlockSpec((None, tr, width), lambda s, i, pr: (s, pr[1] * nb + i, 0))
    else:
        mine = pl.BlockSpec((None, tr, width), lambda s, i, pr: (s, i, pr[1]))

    def body(place_ref, g_ref, p_ref, o_ref):
        o_ref[...] = (g_ref[...].astype(F32) + p_ref[...].astype(F32)).astype(BF16)

    return pl.pallas_call(
        body, name=name, out_shape=jax.ShapeDtypeStruct(partner.shape, BF16),
        grid_spec=pltpu.PrefetchScalarGridSpec(
            num_scalar_prefetch=1, grid=(N_CHIPS, nb),
            in_specs=[mine, pl.BlockSpec((None, tr, width), lambda s, i, pr: (s, i, 0))],
            out_specs=pl.BlockSpec((None, tr, width), lambda s, i, pr: (s, i, 0))),
        compiler_params=_cparams(("parallel", "parallel")),
    )(place, grads, partner)


def _rs_chip_add(name, place, mine, others, split="rows"):
    _, half_rows, width = mine.shape
    tr = _row_block(half_rows, width, 4, 16, budget=1 << 20)
    nb = half_rows // tr
    if split == "rows":
        out_shape, out_spec = (2 * half_rows, width), pl.BlockSpec((tr, width), lambda i, pr: (pr[1] * nb + i, 0))
    else:
        out_shape, out_spec = (half_rows, 2 * width), pl.BlockSpec((tr, width), lambda i, pr: (i, pr[1]))

    def body(place_ref, q_ref, r_ref, o_ref):
        acc = q_ref[...].astype(F32)
        for j in range(3):
            acc = acc + r_ref[j].astype(F32)
        o_ref[...] = acc

    return pl.pallas_call(
        body, name=name, out_shape=jax.ShapeDtypeStruct(out_shape, F32),
        grid_spec=pltpu.PrefetchScalarGridSpec(
            num_scalar_prefetch=1, grid=(nb,),
            in_specs=[pl.BlockSpec((None, tr, width), lambda i, pr: (pr[0], i, 0)),
                      pl.BlockSpec((3, tr, width), lambda i, pr: (0, i, 0))],
            out_specs=out_spec),
        compiler_params=_cparams(("parallel",)),
    )(place, mine, others)


WEIGHTS = ["meta_tokens", "a_norm_pre", "a_w_in", "a_conv_w", "a_conv_b", "a_dt_bias", "a_a_log", "a_d_skip",
           "a_gate_norm", "a_w_out", "a_norm_post", "kv_norm", "w_kv", "b_norm_pre", "b_w_q", "b_sinks", "b_w_o",
           "b_norm_post", "f_norm_pre", "f_w_up", "f_conv_w", "f_conv_b", "f_w_down", "f_norm_post"]
FULL_SHAPE = {
    "meta_tokens": (16, 1024), "a_norm_pre": (1, 1024), "a_w_in": (1, 1024, 5152), "a_conv_w": (1, 4, 3072),
    "a_conv_b": (1, 3072), "a_dt_bias": (1, 32), "a_a_log": (1, 32), "a_d_skip": (1, 32), "a_gate_norm": (1, 2048),
    "a_w_out": (1, 2048, 1024), "a_norm_post": (1, 1024), "kv_norm": (1024,), "w_kv": (1024, 512),
    "b_norm_pre": (1, 1024), "b_w_q": (1, 1024, 1024), "b_sinks": (1, 16), "b_w_o": (1, 1024, 1024),
    "b_norm_post": (1, 1024), "f_norm_pre": (2, 1024), "f_w_up": (2, 1024, 5632), "f_conv_w": (2, 3, 5632),
    "f_conv_b": (2, 5632), "f_w_down": (2, 2816, 1024), "f_norm_post": (2, 1024),
}
SHARD_AXIS = {
    "meta_tokens": 1, "a_norm_pre": 1, "a_w_in": 2, "a_conv_w": 2, "a_conv_b": 1, "a_dt_bias": None, "a_a_log": None,
    "a_d_skip": None, "a_gate_norm": 1, "a_w_out": 1, "a_norm_post": 1, "kv_norm": None, "w_kv": 0, "b_norm_pre": None,
    "b_w_q": 1, "b_sinks": None, "b_w_o": 1, "b_norm_post": None, "f_norm_pre": None, "f_w_up": 2, "f_conv_w": 2,
    "f_conv_b": None, "f_w_down": 1, "f_norm_post": None,
}
BIG = ["a_w_in", "a_w_out", "w_kv", "b_w_q", "b_w_o", "f_w_up", "f_w_down"]
SMALL = [n for n in WEIGHTS if n not in BIG]
SMALL_SHARDED = [n for n in SMALL if SHARD_AXIS[n] is not None]


def _shard_shape(name):
    shape = list(FULL_SHAPE[name])
    if SHARD_AXIS[name] is not None:
        shape[SHARD_AXIS[name]] //= N_CHIPS
    return tuple(shape)


def _numel(shape):
    return int(math.prod(shape))


SUBLANES = 8


def _packed_rows(shape):
    rows = -(-_numel(shape) // LANES)
    return -(-rows // SUBLANES) * SUBLANES


def _pack(arrays):
    parts = []
    for a in arrays:
        size, rows = _numel(a.shape), _packed_rows(a.shape)
        if size % LANES == 0:
            part = jnp.pad(a.reshape(size // LANES, LANES), ((0, rows - size // LANES), (0, 0)))
        else:
            part = jnp.pad(a.reshape(-1), (0, rows * LANES - size)).reshape(rows, LANES)
        parts.append(part)
    return jnp.concatenate(parts, axis=0)


def _unpack(packed, names, shape_of):
    out, off = {}, 0
    lead = packed.shape[:-2]
    for n in names:
        shape = tuple(shape_of(n))
        size, rows = _numel(shape), _packed_rows(shape)
        part = packed[..., off:off + rows, :]
        if size % LANES == 0:
            out[n] = part[..., :size // LANES, :].reshape(lead + shape)
        else:
            out[n] = part.reshape(lead + (rows * LANES,))[..., :size].reshape(lead + shape)
        off += rows
    return out


def _split_chips(name, full):
    ax = SHARD_AXIS[name]
    shape = full.shape
    cut = shape[:ax] + (N_CHIPS, shape[ax] // N_CHIPS) + shape[ax + 1:]
    return jnp.moveaxis(full.reshape(cut), ax, 0)


def _join_chips(name, stacked):
    ax = SHARD_AXIS[name]
    moved = jnp.moveaxis(stacked, 0, ax)
    shape = moved.shape
    return moved.reshape(shape[:ax] + (shape[ax] * shape[ax + 1],) + shape[ax + 2:])


def _as2d(a):
    return a.reshape(-1, a.shape[-1])


BUFFERS = [("a_w_in", "a_w_in", None), ("a_w_out", "a_w_out", None), ("w_kv", "w_kv", None),
           ("b_w_q", "b_w_q", None), ("b_w_o", "b_w_o", None), ("f_w_up0", "f_w_up", 0), ("f_w_up1", "f_w_up", 1),
           ("f_w_down0", "f_w_down", 0), ("f_w_down1", "f_w_down", 1)]


TRANSPOSED = ("a_w_in",)
SPLIT = {"a_w_in": "cols"}


def _local_shard(arrays, weight, layer):
    if weight in TRANSPOSED:
        return arrays[weight][0].T
    return _as2d(arrays[weight]) if layer is None else arrays[weight]


def _weight_from_gathered(weight, buf):
    if weight == "f_w_up":
        return buf
    return buf.reshape(N_CHIPS * buf.shape[1], buf.shape[2])


def _gathered_from_grad(weight, g):
    if weight == "f_w_up":
        return g
    return g.reshape(N_CHIPS, g.shape[0] // N_CHIPS, g.shape[1]).astype(BF16)


GATHER_SCHEDULE = {
    "a_in_main": [("ici", ["a_w_out"])],
    "a_conv": [("d2d", ["a_w_out"]), ("ici", ["f_w_down0"])],
    "a_ssd_prep": [("d2d", ["f_w_down0"]), ("ici_near", ["f_w_up0"])],
    "a_ssd": [("ici_far", ["f_w_up0"])],
    "a_gate": [("d2d", ["f_w_up0"]), ("ici", ["w_kv", "b_w_q", "b_w_o"])],
    "ffn0_up": [("d2d", ["w_kv", "b_w_q", "b_w_o"]), ("ici", ["f_w_down1"])],
    "ffn0_down": [("d2d", ["f_w_down1"])],
    "b_attn": [("ici", ["f_w_up1"])],
    "b_o": [("d2d", ["f_w_up1"])],
}
REDUCE_SCHEDULE = {
    "b_attn_bwd": [("all", ["f_w_down1", "f_w_up1", "b_w_o"])],
    "ffn0_conv_bwd": [("all", ["b_w_q", "w_kv", "f_w_down0"])],
    "a_ssd_bwd": [("all", ["f_w_up0", "a_w_out"])],
    "a_in_main_dx": [("near", ["a_w_in"])],
    "a_norm_bwd": [("far", ["a_w_in"])],
}
REDUCE_LAST = ("a_w_in",)
ICI_PEERS = {"ici": ALL_PEERS, "ici_near": NEAR_PEERS, "ici_far": FAR_PEERS,
             "all": ALL_PEERS, "near": NEAR_PEERS, "far": FAR_PEERS}
PAIR_SCHEDULE = {
    "b_o_dx": ["f_w_down1", "f_w_up1", "b_w_o"],
    "ffn0_down_dx": ["b_w_q", "w_kv", "f_w_down0"],
    "a_out_dx": ["f_w_up0", "a_w_out"],
    "a_in_dt_dx": ["a_w_in"],
}
SWAP_SCHEDULE = {"a_in_main_dw": ["f_w_down1", "f_w_up1", "b_w_o", "b_w_q", "w_kv", "f_w_down0", "f_w_up0", "a_w_out"]}


def _buffer_of(weight, layer):
    return weight if layer is None else f"{weight}{layer}"


class _Pipeline:
    def __init__(self, place, slots):
        self.place = place
        self.slots = dict(slots)
        self.running = []
        self.grads = {}
        self.theirs = {}
        self.partials = {}
        self.peers = {}
        self.reduced = {}

    def _collect(self):
        for step, buffers, table in self.running:
            table.update(zip(buffers, step.results))
        self.running = []

    @staticmethod
    def _splits(buffers):
        return [SPLIT.get(b, "rows") for b in buffers]

    def gather_now(self, name, buffers):
        step = _step_gather_full([self.slots[b] for b in buffers], self._splits(buffers))
        _run_steps(name, [step])
        self.slots.update(zip(buffers, step.results))

    def weight(self, name, layer=None):
        self._collect()
        return _weight_from_gathered(name, self.slots[_buffer_of(name, layer)])

    def grad(self, name, layer, g):
        self.grads[_buffer_of(name, layer)] = _gathered_from_grad(name, g)

    def steps(self, kernel):
        self._collect()
        steps = []
        for phase, buffers in GATHER_SCHEDULE.get(kernel, []):
            bufs, splits = [self.slots[b] for b in buffers], self._splits(buffers)
            step = (_step_gather_d2d(bufs, splits) if phase == "d2d"
                    else _step_gather_ici(bufs, splits, ICI_PEERS[phase]))
            self.running.append((step, buffers, self.slots))
            steps.append(step)
        buffers = PAIR_SCHEDULE.get(kernel)
        if buffers:
            step = _step_pair_exchange([self.grads[b] for b in buffers], self._splits(buffers))
            self.running.append((step, buffers, self.theirs))
            steps.append(step)
        for part, buffers in REDUCE_SCHEDULE.get(kernel, []):
            for b in buffers:
                if b not in self.partials:
                    self.partials[b] = _rs_pair_add("reduce_pair_add_" + b, self.place, self.grads[b], self.theirs[b],
                                                    SPLIT.get(b, "rows"))
            started = [self.peers[b] for b in buffers] if all(b in self.peers for b in buffers) else None
            step = _step_chip_exchange([self.partials[b] for b in buffers], ICI_PEERS[part], into=started)
            self.running.append((step, buffers, self.peers))
            steps.append(step)
        buffers = SWAP_SCHEDULE.get(kernel)
        if buffers:
            step = self._swap_step(buffers)
            self.running.append((step, buffers, self.reduced))
            steps.append(step)
        return steps

    def _swap_step(self, buffers):
        halves = [_rs_chip_add("reduce_chip_add_" + b, self.place, self.partials[b], self.peers[b], SPLIT.get(b, "rows"))
                  for b in buffers]
        return _step_pair_gather(halves, self._splits(buffers))

    def shard(self, buffer):
        self._collect()
        return self.reduced[buffer]

    def finish(self):
        self._collect()
        rest = [b for b, _, _ in BUFFERS if b not in self.reduced]
        step = self._swap_step(rest)
        _run_steps("reduce_pair_gather", [step])
        self.reduced.update(zip(rest, step.results))


def kernel(x, meta_tokens, a_norm_pre, a_w_in, a_conv_w, a_conv_b, a_dt_bias, a_a_log, a_d_skip, a_gate_norm, a_w_out, a_norm_post, kv_norm, w_kv, b_norm_pre, b_w_q, b_sinks, b_w_o, b_norm_post, f_norm_pre, f_w_up, f_conv_w, f_conv_b, f_w_down, f_norm_post, loss_target, m_meta_tokens, m_a_norm_pre, m_a_w_in, m_a_conv_w, m_a_conv_b, m_a_dt_bias, m_a_a_log, m_a_d_skip, m_a_gate_norm, m_a_w_out, m_a_norm_post, m_kv_norm, m_w_kv, m_b_norm_pre, m_b_w_q, m_b_sinks, m_b_w_o, m_b_norm_post, m_f_norm_pre, m_f_w_up, m_f_conv_w, m_f_conv_b, m_f_w_down, m_f_norm_post, v_meta_tokens, v_a_norm_pre, v_a_w_in, v_a_conv_w, v_a_conv_b, v_a_dt_bias, v_a_a_log, v_a_d_skip, v_a_gate_norm, v_a_w_out, v_a_norm_post, v_kv_norm, v_w_kv, v_b_norm_pre, v_b_w_q, v_b_sinks, v_b_w_o, v_b_norm_post, v_f_norm_pre, v_f_w_up, v_f_conv_w, v_f_conv_b, v_f_w_down, v_f_norm_post):
    given = dict(locals())
    w = {n: given[n] for n in WEIGHTS}
    mom = {n: given["m_" + n] for n in WEIGHTS}
    var = {n: given["v_" + n] for n in WEIGHTS}
    chip = 2 * lax.axis_index("x") + lax.axis_index("y")
    core = lax.axis_index("c")
    place = jnp.stack([chip, core]).astype(jnp.int32)

    small_all = _allgather_small("gather_small", _pack([w[n] for n in SMALL_SHARDED]))
    small_parts = _unpack(small_all, SMALL_SHARDED, _shard_shape)
    slots = {b: _cast_into_slot("cast_" + b, place, _local_shard(w, wn, layer), layer) for b, wn, layer in BUFFERS}
    pipeline = _Pipeline(place, slots)
    pipeline.gather_now("gather_first", ["a_w_in"])
    p = {}
    for n in SMALL:
        p[n] = _join_chips(n, small_parts[n]) if n in SMALL_SHARDED else w[n]
    p["a_conv_w"] = p["a_conv_w"][0]
    p["kv_norm"] = p["kv_norm"].reshape(1, D_MODEL)

    loss_local, grad_x, g = _local_step(x[0], loss_target[0], p, pipeline)

    small_sum = _allreduce_small("reduce_small", _pack([g[n].reshape(FULL_SHAPE[n]) for n in SMALL]
                                                       + [loss_local.reshape(1, 1)]))
    small_red = _unpack(small_sum, SMALL + ["loss"], lambda n: (1, 1) if n == "loss" else FULL_SHAPE[n])
    loss = small_red["loss"][0, 0]
    grads = {}
    for n in SMALL:
        if SHARD_AXIS[n] is None:
            grads[n] = small_red[n]
        else:
            grads[n] = lax.dynamic_index_in_dim(_split_chips(n, small_red[n]), chip, 0, keepdims=False)

    delta, new_m, new_v = {}, {}, {}
    for n in sorted(BIG, key=lambda name: name in REDUCE_LAST):
        shape = _shard_shape(n)
        if n in REDUCE_LAST:
            pipeline.finish()
        if n in TRANSPOSED:
            g2d = pipeline.shard(n)
            w2d, m2d, v2d = (arrays[n][0].T for arrays in (w, mom, var))
            back = lambda a: a.T.reshape(shape)
        else:
            g2d = (jnp.concatenate([pipeline.shard(n + "0"), pipeline.shard(n + "1")], axis=0)
                   if n in ("f_w_up", "f_w_down") else pipeline.shard(n))
            w2d, m2d, v2d = (_as2d(arrays[n]) for arrays in (w, mom, var))
            back = lambda a: a.reshape(shape)
        d, m2, v2 = _adamw("adamw_" + n, w2d, g2d, m2d, v2d, steps=pipeline.steps("adamw_" + n))
        grads[n], delta[n], new_m[n], new_v[n] = back(g2d), back(d), back(m2), back(v2)
    at_least_2d = lambda n: (1,) * (2 - len(_shard_shape(n))) + _shard_shape(n)
    outs = _adamw_small("adamw_small", *[[src[n].reshape(at_least_2d(n)) for n in SMALL] for src in (w, grads, mom, var)])
    for dst, arrays in zip((delta, new_m, new_v), outs):
        dst.update({n: a.reshape(_shard_shape(n)) for n, a in zip(SMALL, arrays)})

    return (loss, grad_x[None], *[grads[n].reshape(_shard_shape(n)) for n in WEIGHTS],
            *[delta[n] for n in WEIGHTS], *[new_m[n] for n in WEIGHTS], *[new_v[n] for n in WEIGHTS])
```

```python
import functools
import math

import jax
import jax.numpy as jnp
from jax import lax
from jax.experimental import pallas as pl
from jax.experimental.pallas import tpu as pltpu

F32, BF16 = jnp.float32, jnp.bfloat16
MESH = pl.DeviceIdType.MESH

D_MODEL = 1024
N_META = 16
CHUNK = 128
PAD_ROWS = CHUNK - N_META
D_INNER = 2048
D_STATE = 128
N_GROUPS = 4
HEADS_PER_GROUP = 8
SSM_HEADS = 32
HEAD_DIM = 64
D_BC = N_GROUPS * D_STATE
D_XBC = D_INNER + 2 * D_BC
D_MAIN = D_INNER + D_XBC
D_IN_PROJ = D_MAIN + SSM_HEADS
GROUP_W = HEADS_PER_GROUP * HEAD_DIM
SSM_CONV = 4
D_FF = 2816
FFN_CONV = 3
N_Q_HEADS = 16
N_KV_HEADS = 4
D_KV = 256
ATTN_SCALE = 1.0 / math.sqrt(HEAD_DIM)
RMS_EPS = 1e-6
NEG_INF = -1e30
LANES = 128
VMEM_LIMIT = 52 * 1024 * 1024

ADAM_LR, ADAM_B1, ADAM_B2, ADAM_EPS, ADAM_WD, ADAM_STEP = 0.001, 0.9, 0.999, 1e-08, 0.01, 10

N_CHIPS = 4


def _cparams(sem=None):
    return pltpu.CompilerParams(dimension_semantics=sem, vmem_limit_bytes=VMEM_LIMIT)


def _tile(n, cands=(512, 256, 128)):
    for t in cands:
        if n % t == 0:
            return t
    return n


def _row_tile(rows, width):
    for t in (544, 272):
        if rows % t == 0 and t * width * 4 <= (3 << 20):
            return t
    return 128


def _rows_mask(i, tm):
    rows = i * tm + lax.broadcasted_iota(jnp.int32, (tm, 1), 0)
    return rows >= PAD_ROWS


def _dot(a, b):
    return jnp.dot(a, b, preferred_element_type=F32)


def _dot_nt(a, b):
    return lax.dot_general(a, b, (((1,), (1,)), ((), ())), preferred_element_type=F32)


def _dot_tn(a, b):
    return lax.dot_general(a, b, (((0,), (0,)), ((), ())), preferred_element_type=F32)


def _sigmoid(x):
    return 1.0 / (1.0 + jnp.exp(-x))


def _place():
    return lax.axis_index("x"), lax.axis_index("y"), lax.axis_index("c")


def _other_chips(x, y):
    return [(1 - x, y), (x, 1 - y), (1 - x, 1 - y)]


class _Step:
    def __init__(self, ins, outs, aliases, n_sems, start, finish):
        self.ins, self.outs, self.aliases, self.n_sems = list(ins), list(outs), dict(aliases), n_sems
        self.start, self.finish = start, finish
        self.results = None


def _like(a):
    return jax.ShapeDtypeStruct(a.shape, a.dtype)


def _remote(src, dst, send_sems, recv_sems, k, device):
    return pltpu.make_async_remote_copy(src, dst, send_sems.at[k], recv_sems.at[k], device_id=device, device_id_type=MESH)


def _half(ref, split, which, lead=()):
    if split == "rows":
        hr = ref.shape[-2] // 2
        return ref.at[lead + (pl.ds(which * hr, hr),)]
    hc = ref.shape[-1] // 2
    return ref.at[lead + (slice(None), pl.ds(which * hc, hc))]


def _splits(bufs, splits):
    return list(splits) if splits is not None else ["rows"] * len(bufs)


ALL_PEERS = (0, 1, 2)
NEAR_PEERS = (0, 1)
FAR_PEERS = (2,)


def _step_gather_ici(bufs, splits=None, peers=ALL_PEERS):
    splits = _splits(bufs, splits)

    def copies(outs, send_sems, recv_sems, received):
        x, y, c = _place()
        me = 2 * x + y
        for k, o in enumerate(outs):
            for j, (cx, cy) in enumerate(_other_chips(x, y)):
                if j in peers:
                    part = _half(o, splits[k], c, (2 * cx + cy if received else me,))
                    yield _remote(part, part, send_sems, recv_sems, 3 * k + j, (cx, cy, c))

    def start(ins, outs, send_sems, recv_sems):
        for cp in copies(outs, send_sems, recv_sems, False):
            cp.start()

    def finish(ins, outs, send_sems, recv_sems):
        for cp in copies(outs, send_sems, recv_sems, True):
            cp.wait_recv()
        for cp in copies(outs, send_sems, recv_sems, False):
            cp.wait_send()

    return _Step(bufs, [_like(b) for b in bufs], {k: k for k in range(len(bufs))}, 3 * len(bufs), start, finish)


def _step_gather_d2d(bufs, splits=None):
    splits = _splits(bufs, splits)

    def copies(outs, send_sems, recv_sems, received):
        x, y, c = _place()
        for k, o in enumerate(outs):
            for j, (cx, cy) in enumerate(_other_chips(x, y)):
                part = _half(o, splits[k], 1 - c if received else c, (2 * cx + cy,))
                yield _remote(part, part, send_sems, recv_sems, 3 * k + j, (x, y, 1 - c))

    def start(ins, outs, send_sems, recv_sems):
        for cp in copies(outs, send_sems, recv_sems, False):
            cp.start()

    def finish(ins, outs, send_sems, recv_sems):
        for cp in copies(outs, send_sems, recv_sems, True):
            cp.wait_recv()
        for cp in copies(outs, send_sems, recv_sems, False):
            cp.wait_send()

    return _Step(bufs, [_like(b) for b in bufs], {k: k for k in range(len(bufs))}, 3 * len(bufs), start, finish)


def _step_gather_full(bufs, splits=None):
    n = len(bufs)
    splits = _splits(bufs, splits)

    def ici(outs, send_sems, recv_sems, received):
        x, y, c = _place()
        me = 2 * x + y
        for k, o in enumerate(outs):
            for j, (cx, cy) in enumerate(_other_chips(x, y)):
                part = _half(o, splits[k], c, (2 * cx + cy if received else me,))
                yield _remote(part, part, send_sems, recv_sems, 3 * k + j, (cx, cy, c))

    def d2d(outs, send_sems, recv_sems, received):
        x, y, c = _place()
        for k, o in enumerate(outs):
            for j, (cx, cy) in enumerate(_other_chips(x, y)):
                part = _half(o, splits[k], 1 - c if received else c, (2 * cx + cy,))
                yield _remote(part, part, send_sems, recv_sems, 3 * n + 3 * k + j, (x, y, 1 - c))

    def start(ins, outs, send_sems, recv_sems):
        for cp in ici(outs, send_sems, recv_sems, False):
            cp.start()

    def finish(ins, outs, send_sems, recv_sems):
        for arrived, onward in zip(ici(outs, send_sems, recv_sems, True), d2d(outs, send_sems, recv_sems, False)):
            arrived.wait_recv()
            onward.start()
        for cp in d2d(outs, send_sems, recv_sems, True):
            cp.wait_recv()
        for cp in ici(outs, send_sems, recv_sems, False):
            cp.wait_send()
        for cp in d2d(outs, send_sems, recv_sems, False):
            cp.wait_send()

    return _Step(bufs, [_like(b) for b in bufs], {k: k for k in range(n)}, 6 * n, start, finish)


def _half_shape(shape, split):
    return shape[:-2] + ((shape[-2] // 2, shape[-1]) if split == "rows" else (shape[-2], shape[-1] // 2))


def _step_pair_exchange(grads, splits=None):
    splits = _splits(grads, splits)

    def copies(ins, outs, send_sems, recv_sems):
        x, y, c = _place()
        for k, (g, o) in enumerate(zip(ins, outs)):
            yield _remote(_half(g, splits[k], 1 - c, (slice(None),)), o, send_sems, recv_sems, k, (x, y, 1 - c))

    def start(ins, outs, send_sems, recv_sems):
        for cp in copies(ins, outs, send_sems, recv_sems):
            cp.start()

    def finish(ins, outs, send_sems, recv_sems):
        for cp in copies(ins, outs, send_sems, recv_sems):
            cp.wait()

    outs = [jax.ShapeDtypeStruct(_half_shape(g.shape, s), g.dtype) for g, s in zip(grads, splits)]
    return _Step(grads, outs, {}, len(grads), start, finish)


def _step_chip_exchange(partials, peers=ALL_PEERS, into=None):
    n = len(partials)

    def copies(ins, outs, send_sems, recv_sems):
        x, y, c = _place()
        for k, (q, o) in enumerate(zip(ins[:n], outs)):
            for j, (cx, cy) in enumerate(_other_chips(x, y)):
                if j in peers:
                    yield _remote(q.at[2 * cx + cy], o.at[j], send_sems, recv_sems, 3 * k + j, (cx, cy, c))

    def start(ins, outs, send_sems, recv_sems):
        for cp in copies(ins, outs, send_sems, recv_sems):
            cp.start()

    def finish(ins, outs, send_sems, recv_sems):
        for cp in copies(ins, outs, send_sems, recv_sems):
            cp.wait()

    outs = [jax.ShapeDtypeStruct((3,) + q.shape[1:], q.dtype) for q in partials]
    if into is None:
        return _Step(partials, outs, {}, 3 * n, start, finish)
    return _Step(list(partials) + list(into), outs, {n + k: k for k in range(n)}, 3 * n, start, finish)


def _step_pair_gather(shards, splits=None):
    splits = _splits(shards, splits)

    def copies(outs, send_sems, recv_sems, received):
        x, y, c = _place()
        for k, o in enumerate(outs):
            part = _half(o, splits[k], 1 - c if received else c)
            yield _remote(part, part, send_sems, recv_sems, k, (x, y, 1 - c))

    def start(ins, outs, send_sems, recv_sems):
        for cp in copies(outs, send_sems, recv_sems, False):
            cp.start()

    def finish(ins, outs, send_sems, recv_sems):
        for cp in copies(outs, send_sems, recv_sems, True):
            cp.wait_recv()
        for cp in copies(outs, send_sems, recv_sems, False):
            cp.wait_send()

    return _Step(shards, [_like(s) for s in shards], {k: k for k in range(len(shards))}, len(shards), start, finish)


def _call(body, *, name, out_shape, grid, in_specs, out_specs, operands, scratch_shapes=(), semantics=None, steps=()):
    single = not isinstance(out_shape, (tuple, list))
    out_shapes = [out_shape] if single else list(out_shape)
    out_spec_list = [out_specs] if single else list(out_specs)
    steps = list(steps)
    if not steps:
        res = pl.pallas_call(body, name=name, out_shape=out_shapes, grid=grid, in_specs=list(in_specs),
                             out_specs=out_spec_list, scratch_shapes=list(scratch_shapes),
                             compiler_params=_cparams(semantics))(*operands)
        return res[0] if single else res
    n_in, n_out, n_scr = len(operands), len(out_shapes), len(scratch_shapes)
    x_in = [a for s in steps for a in s.ins]
    x_out = [o for s in steps for o in s.outs]
    aliases, in_off, out_off = {}, 0, 0
    for s in steps:
        for i, o in s.aliases.items():
            aliases[n_in + in_off + i] = n_out + out_off + o
        in_off += len(s.ins)
        out_off += len(s.outs)
    sems = []
    for s in steps:
        sems += [pltpu.SemaphoreType.DMA((s.n_sems,)), pltpu.SemaphoreType.DMA((s.n_sems,))]
    any_spec = pl.BlockSpec(memory_space=pl.ANY)

    def carried(*refs):
        pos = 0
        ins = refs[pos:pos + n_in]; pos += n_in
        xi = refs[pos:pos + len(x_in)]; pos += len(x_in)
        outs = refs[pos:pos + n_out]; pos += n_out
        xo = refs[pos:pos + len(x_out)]; pos += len(x_out)
        scr = refs[pos:pos + n_scr]; pos += n_scr
        sem_refs = refs[pos:]

        def each(action):
            i0 = o0 = 0
            for k, s in enumerate(steps):
                getattr(s, action)(xi[i0:i0 + len(s.ins)], xo[o0:o0 + len(s.outs)], sem_refs[2 * k], sem_refs[2 * k + 1])
                i0 += len(s.ins)
                o0 += len(s.outs)

        if grid:
            first = functools.reduce(jnp.logical_and, [pl.program_id(d) == 0 for d in range(len(grid))])
            last = functools.reduce(jnp.logical_and, [pl.program_id(d) == grid[d] - 1 for d in range(len(grid))])
            pl.when(first)(lambda: each("start"))
            body(*ins, *outs, *scr)
            pl.when(last)(lambda: each("finish"))
        else:
            each("start")
            body(*ins, *outs, *scr)
            each("finish")

    res = pl.pallas_call(
        carried, name=name, out_shape=out_shapes + x_out, grid=grid,
        in_specs=list(in_specs) + [any_spec] * len(x_in), out_specs=out_spec_list + [any_spec] * len(x_out),
        scratch_shapes=list(scratch_shapes) + sems, input_output_aliases=aliases,
        compiler_params=_cparams(None if semantics is None else ("arbitrary",) * len(grid)),
    )(*operands, *x_in)
    o0 = n_out
    for s in steps:
        s.results = list(res[o0:o0 + len(s.outs)])
        o0 += len(s.outs)
    return res[0] if single else tuple(res[:n_out])


def _run_steps(name, steps):
    _call(lambda: None, name=name, out_shape=[], grid=(), in_specs=[], out_specs=[], operands=[], steps=steps)
    return [s.results for s in steps]


def _mm(name, a, b, mode, out_dtype=F32, acc=None, b_colblock=0, k_rows=None, out_rows=None, steps=()):
    resident_bytes = 8 << 20
    if mode == "nn":
        m, k = a.shape
        n = b.shape[1]
        tm = m
        while tm * k * 2 > resident_bytes and tm % 32 == 0:
            tm //= 2
        tn = _tile(n)
        grid = (m // tm, n // tn)
        in_specs = [pl.BlockSpec((tm, k), lambda i, j: (i, 0)), pl.BlockSpec((k, tn), lambda i, j: (0, j))]
        out_shape, out_block = (m, n), (tm, tn)
    elif mode == "nt":
        m, n = a.shape
        k = k_rows or b.shape[0]
        tm = m
        while tm * n * 2 > resident_bytes and tm % 32 == 0:
            tm //= 2
        tk = _tile(k)
        grid = (m // tm, k // tk)
        in_specs = [pl.BlockSpec((tm, n), lambda i, j: (i, 0)), pl.BlockSpec((tk, n), lambda i, j: (j, b_colblock))]
        out_shape, out_block = (m, k), (tm, tk)
    else:
        m, k = a.shape
        n = b.shape[1]
        tk, tn = _tile(k), (n if m * n * 2 <= resident_bytes else _tile(n))
        grid = (k // tk, n // tn)
        in_specs = [pl.BlockSpec((m, tk), lambda i, j: (0, i)), pl.BlockSpec((m, tn), lambda i, j: (0, j))]
        out_shape, out_block = (out_rows or k, n), (tk, tn)
    out_spec = pl.BlockSpec(out_block, lambda i, j: (i, j))
    has_acc = acc is not None

    def body(*refs):
        a_ref, b_ref = refs[0], refs[1]
        o_ref = refs[-1]
        av, bv = a_ref[...], b_ref[...]
        if mode == "nn":
            r = _dot(av, bv)
        elif mode == "nt":
            r = _dot_nt(av, bv)
        else:
            r = _dot_tn(av, bv)
        if has_acc:
            r = r + refs[2][...]
        o_ref[...] = r.astype(o_ref.dtype)

    operands = [a, b]
    if has_acc:
        in_specs = in_specs + [out_spec]
        operands.append(acc)
    return _call(body, name=name, out_shape=jax.ShapeDtypeStruct(out_shape, out_dtype), grid=grid, in_specs=in_specs,
                 out_specs=out_spec, operands=operands, semantics=("parallel", "parallel"), steps=steps)


def _tn_rows_into(name, a, b, into, row0, nrows):
    m, k = a.shape
    n = b.shape[1]

    def body(a_ref, b_ref, into_ref, o_ref):
        o_ref[...] = _dot_tn(a_ref[...], b_ref[...])[0:nrows].astype(o_ref.dtype)

    return pl.pallas_call(
        body, name=name, out_shape=jax.ShapeDtypeStruct(into.shape, into.dtype), grid=(1,),
        in_specs=[pl.BlockSpec((m, k), lambda i: (0, 0)), pl.BlockSpec((m, n), lambda i: (0, 0)),
                  pl.BlockSpec(memory_space=pl.ANY)],
        out_specs=pl.BlockSpec((nrows, n), lambda i: (row0 // nrows, 0)),
        input_output_aliases={2: 0}, compiler_params=_cparams(("arbitrary",)),
    )(a, b, into)


def _rms_fwd(name, h, w):
    rows, width = h.shape
    tm = _row_tile(rows, width)

    def body(h_ref, w_ref, o_ref):
        x = h_ref[...]
        r = lax.rsqrt(jnp.mean(x * x, axis=-1, keepdims=True) + RMS_EPS)
        o_ref[...] = (x * r * w_ref[...]).astype(BF16)

    return pl.pallas_call(
        body, name=name, out_shape=jax.ShapeDtypeStruct((rows, width), BF16), grid=(rows // tm,),
        in_specs=[pl.BlockSpec((tm, width), lambda i: (i, 0)), pl.BlockSpec((1, width), lambda i: (0, 0))],
        out_specs=pl.BlockSpec((tm, width), lambda i: (i, 0)), compiler_params=_cparams(("parallel",)),
    )(h, w)


def _resid_norm_fwd(name, h, pre, w, next_norms=()):
    rows, width = h.shape
    tm = _row_tile(rows, width)
    n_next = len(next_norms)

    def body(*refs):
        h_ref, p_ref, w_ref = refs[:3]
        v_refs = refs[3:3 + n_next]
        o_ref = refs[3 + n_next]
        n_refs = refs[4 + n_next:]
        p = p_ref[...]
        r = lax.rsqrt(jnp.mean(p * p, axis=-1, keepdims=True) + RMS_EPS)
        x = h_ref[...] + jnp.where(_rows_mask(pl.program_id(0), tm), p * r * w_ref[...], 0.0)
        o_ref[...] = x
        if n_next:
            rx = lax.rsqrt(jnp.mean(x * x, axis=-1, keepdims=True) + RMS_EPS)
            for v_ref, n_ref in zip(v_refs, n_refs):
                n_ref[...] = (x * rx * v_ref[...]).astype(BF16)

    row_spec = pl.BlockSpec((tm, width), lambda i: (i, 0))
    vec_spec = pl.BlockSpec((1, width), lambda i: (0, 0))
    outs = pl.pallas_call(
        body, name=name,
        out_shape=[jax.ShapeDtypeStruct((rows, width), F32)] + [jax.ShapeDtypeStruct((rows, width), BF16)] * n_next,
        grid=(rows // tm,), in_specs=[row_spec, row_spec, vec_spec] + [vec_spec] * n_next,
        out_specs=[row_spec] * (1 + n_next), compiler_params=_cparams(("parallel",)),
    )(h, pre, w, *next_norms)
    return outs[0], list(outs[1:])


def _resid_norm_loss(name, h, pre, w, target):
    rows, width = h.shape

    def body(h_ref, p_ref, w_ref, t_ref, dh_ref, loss_ref, dp_ref, dw_ref):
        i = pl.program_id(0)
        p = p_ref[...]
        r = lax.rsqrt(jnp.mean(p * p, axis=-1, keepdims=True) + RMS_EPS)
        x = h_ref[...] + p * r * w_ref[...]
        real = (i + jnp.zeros((CHUNK, 1), jnp.int32)) >= 1
        diff = jnp.where(real, x - t_ref[...], 0.0)
        dh = diff * (1.0 / D_MODEL)
        dh_ref[...] = dh
        dp, dw_rows = _rms_bwd(dh, p, w_ref[...])
        dp_ref[...] = dp.astype(BF16)

        @pl.when(i == 0)
        def _():
            loss_ref[...] = jnp.zeros_like(loss_ref)
            dw_ref[...] = jnp.zeros_like(dw_ref)

        loss_ref[...] += jnp.sum(diff * diff) * (0.5 / D_MODEL)
        dw_ref[...] += jnp.sum(dw_rows, axis=0, keepdims=True)

    blk = pl.BlockSpec((CHUNK, width), lambda i: (i, 0))
    vec_spec = pl.BlockSpec((1, width), lambda i: (0, 0))
    return pl.pallas_call(
        body, name=name,
        out_shape=(jax.ShapeDtypeStruct((rows, width), F32), jax.ShapeDtypeStruct((1, LANES), F32),
                   jax.ShapeDtypeStruct((rows, width), BF16), jax.ShapeDtypeStruct((1, width), F32)),
        grid=(rows // CHUNK,),
        in_specs=[blk, blk, vec_spec, pl.BlockSpec((CHUNK, width), lambda i: (jnp.maximum(i - 1, 0), 0))],
        out_specs=(blk, pl.BlockSpec((1, LANES), lambda i: (0, 0)), blk, vec_spec),
        compiler_params=_cparams(("arbitrary",)),
    )(h, pre, w, target)


def _rms_bwd(dy, x, w):
    r = lax.rsqrt(jnp.mean(x * x, axis=-1, keepdims=True) + RMS_EPS)
    xhat = x * r
    dxhat = dy * w
    return r * (dxhat - xhat * jnp.mean(dxhat * xhat, axis=-1, keepdims=True)), dy * xhat


def _norm_bwd_add(name, dh, dhn, h, w, then=None, split_first_block=False, steps=()):
    rows, width = dh.shape
    tm = CHUNK if split_first_block else _row_tile(rows, width)
    fused = then is not None
    assert not (fused and split_first_block)

    def body(*refs):
        dh_ref, dhn_ref, h_ref, w_ref = refs[:4]
        o_ref, dw_ref = refs[6:8] if fused else refs[-2:]
        i = pl.program_id(0)
        valid = _rows_mask(i, tm)
        dx, dw_rows = _rms_bwd(dhn_ref[...], h_ref[...], w_ref[...])
        dh_new = dh_ref[...] + jnp.where(valid, dx, 0.0)
        if split_first_block:
            first_ref = refs[4]

            @pl.when(i == 0)
            def _():
                first_ref[...] = dh_new

            @pl.when(i > 0)
            def _():
                o_ref[...] = dh_new
        else:
            o_ref[...] = dh_new

        @pl.when(i == 0)
        def _():
            dw_ref[...] = jnp.zeros_like(dw_ref)

        dw_ref[...] += jnp.sum(dw_rows, axis=0, keepdims=True)
        if fused:
            p_ref, wp_ref, dp_ref, dwp_ref = refs[4], refs[5], refs[8], refs[9]
            dp, dwp_rows = _rms_bwd(jnp.where(valid, dh_new, 0.0), p_ref[...], wp_ref[...])
            dp_ref[...] = dp.astype(BF16)

            @pl.when(i == 0)
            def _():
                dwp_ref[...] = jnp.zeros_like(dwp_ref)

            dwp_ref[...] += jnp.sum(dwp_rows, axis=0, keepdims=True)

    row_spec = pl.BlockSpec((tm, width), lambda i: (i, 0))
    vec_spec = pl.BlockSpec((1, width), lambda i: (0, 0))
    row_f32, vec_f32 = jax.ShapeDtypeStruct((rows, width), F32), jax.ShapeDtypeStruct((1, width), F32)
    in_specs, operands = [row_spec, row_spec, row_spec, vec_spec], [dh, dhn, h, w]
    out_shape, out_specs = [row_f32, vec_f32], [row_spec, vec_spec]
    if split_first_block:
        out_shape = [jax.ShapeDtypeStruct((tm, width), F32), jax.ShapeDtypeStruct((rows - tm, width), F32), vec_f32]
        out_specs = [pl.BlockSpec((tm, width), lambda i: (0, 0)),
                     pl.BlockSpec((tm, width), lambda i: (jnp.maximum(i - 1, 0), 0)), vec_spec]
    if fused:
        in_specs += [row_spec, vec_spec]
        operands += list(then)
        out_shape += [jax.ShapeDtypeStruct((rows, width), BF16), vec_f32]
        out_specs += [row_spec, vec_spec]
    return _call(body, name=name, out_shape=out_shape, grid=(rows // tm,), in_specs=in_specs, out_specs=out_specs,
                 operands=operands, semantics=("arbitrary",), steps=steps)


def _shift_down(x, s, rows):
    return pltpu.roll(x, s, 0) if s else x


def _shift_up(x, s, rows):
    return pltpu.roll(x, rows - s, 0) if s else x


def _conv4_fwd(name, zx, cw, cb, steps=()):
    rows = zx.shape[0]
    off = D_INNER // LANES

    def body(x_ref, w_ref, b_ref, o_ref):
        x = x_ref[...]
        acc = b_ref[...] + w_ref[pl.ds(SSM_CONV - 1, 1), :] * x
        for s in range(1, SSM_CONV):
            acc = acc + w_ref[pl.ds(SSM_CONV - 1 - s, 1), :] * _shift_down(x, s, rows)
        valid = lax.broadcasted_iota(jnp.int32, (rows, 1), 0) >= PAD_ROWS
        o_ref[...] = jnp.where(valid, acc * _sigmoid(acc), 0.0)

    return _call(
        body, name=name, out_shape=jax.ShapeDtypeStruct((rows, D_XBC), F32), grid=(D_XBC // LANES,),
        in_specs=[pl.BlockSpec((rows, LANES), lambda j: (0, j + off)),
                  pl.BlockSpec((SSM_CONV, LANES), lambda j: (0, j)),
                  pl.BlockSpec((1, LANES), lambda j: (0, j))],
        out_specs=pl.BlockSpec((rows, LANES), lambda j: (0, j)), operands=[zx, cw, cb],
        semantics=("parallel",), steps=steps)


def _conv4_bwd(name, zx, dout, cw, cb, into):
    rows, width = dout.shape
    zoff = D_INNER // LANES

    def body(x_ref, d_ref, w_ref, b_ref, into_ref, dx_ref, dw_ref, db_ref):
        x = x_ref[...]
        shifted = [_shift_down(x, s, rows) for s in range(SSM_CONV)]
        acc = b_ref[...]
        for s in range(SSM_CONV):
            acc = acc + w_ref[pl.ds(SSM_CONV - 1 - s, 1), :] * shifted[s]
        sig = _sigmoid(acc)
        valid = lax.broadcasted_iota(jnp.int32, (rows, 1), 0) >= PAD_ROWS
        dpre = jnp.where(valid, d_ref[...] * sig * (1.0 + acc * (1.0 - sig)), 0.0)
        dx = w_ref[pl.ds(SSM_CONV - 1, 1), :] * dpre
        for s in range(1, SSM_CONV):
            dx = dx + w_ref[pl.ds(SSM_CONV - 1 - s, 1), :] * _shift_up(dpre, s, rows)
        dx_ref[...] = dx.astype(BF16)
        for s in range(SSM_CONV):
            dw_ref[pl.ds(SSM_CONV - 1 - s, 1), :] = jnp.sum(dpre * shifted[s], axis=0, keepdims=True)
        db_ref[...] = jnp.sum(dpre, axis=0, keepdims=True)

    return pl.pallas_call(
        body, name=name,
        out_shape=(jax.ShapeDtypeStruct(into.shape, BF16), jax.ShapeDtypeStruct((SSM_CONV, width), F32),
                   jax.ShapeDtypeStruct((1, width), F32)),
        grid=(width // LANES,),
        in_specs=[pl.BlockSpec((rows, LANES), lambda j: (0, j + zoff)),
                  pl.BlockSpec((rows, LANES), lambda j: (0, j)),
                  pl.BlockSpec((SSM_CONV, LANES), lambda j: (0, j)),
                  pl.BlockSpec((1, LANES), lambda j: (0, j)),
                  pl.BlockSpec(memory_space=pl.ANY)],
        out_specs=(pl.BlockSpec((rows, LANES), lambda j: (0, j + zoff)),
                   pl.BlockSpec((SSM_CONV, LANES), lambda j: (0, j)),
                   pl.BlockSpec((1, LANES), lambda j: (0, j))),
        input_output_aliases={4: 0}, compiler_params=_cparams(("parallel",)),
    )(zx, dout, cw, cb, into)


FFN_TILE = 2 * LANES


def _ffn_up_conv(name, hn, w_up, cw, cb, steps=()):
    rows, k = hn.shape
    chip_blocks = w_up.shape[2] // LANES
    half_blocks = D_FF // LANES
    nt = D_FF // FFN_TILE

    def weight_block(offset):
        return pl.BlockSpec((None, k, LANES), lambda j: ((2 * j + offset) // chip_blocks, 0, (2 * j + offset) % chip_blocks))

    def body(a_ref, g0, g1, v0, v1, wg_ref, wv_ref, bg_ref, bv_ref, upg_ref, upv_ref, act_ref):
        a = a_ref[...]
        g = _dot(a, jnp.concatenate([g0[...], g1[...]], axis=1))
        v = _dot(a, jnp.concatenate([v0[...], v1[...]], axis=1))
        upg_ref[...] = g
        upv_ref[...] = v
        ug, uv = bg_ref[...], bv_ref[...]
        for s in range(FFN_CONV):
            ug = ug + wg_ref[pl.ds(FFN_CONV - 1 - s, 1), :] * _shift_down(g, s, rows)
            uv = uv + wv_ref[pl.ds(FFN_CONV - 1 - s, 1), :] * _shift_down(v, s, rows)
        act_ref[...] = (ug * _sigmoid(ug) * uv).astype(BF16)

    col = pl.BlockSpec((rows, FFN_TILE), lambda j: (0, j))
    wsp = lambda shift: pl.BlockSpec((FFN_CONV, FFN_TILE), lambda j: (0, j + shift))
    bsp = lambda shift: pl.BlockSpec((1, FFN_TILE), lambda j: (0, j + shift))
    half = jax.ShapeDtypeStruct((rows, D_FF), F32)
    return _call(
        body, name=name, out_shape=(half, half, jax.ShapeDtypeStruct((rows, D_FF), BF16)), grid=(nt,),
        in_specs=[pl.BlockSpec((rows, k), lambda j: (0, 0)), weight_block(0), weight_block(1),
                  weight_block(half_blocks), weight_block(half_blocks + 1), wsp(0), wsp(nt), bsp(0), bsp(nt)],
        out_specs=(col, col, col), operands=[hn, w_up, w_up, w_up, w_up, cw, cw, cb, cb],
        semantics=("parallel",), steps=steps)


def _ffn_conv_bwd(name, up_g, up_v, dact, cw, cb, hn, w_up, steps=()):
    rows, k = hn.shape
    chip_blocks = w_up.shape[2] // LANES
    nt = D_FF // LANES

    def weight_block(shift):
        return pl.BlockSpec((None, k, LANES), lambda j: ((j + shift) // chip_blocks, 0, (j + shift) % chip_blocks))

    def body(g_ref, v_ref, d_ref, wg_ref, wv_ref, bg_ref, bv_ref, upg_ref, upv_ref, hn_ref,
             dwg_ref, dwv_ref, dbg_ref, dbv_ref, dhn_ref, dup_ref, acc, hn_scr, hnt_scr, dup_scr, sems):
        j = pl.program_id(0)
        hn_copy = pltpu.make_async_copy(hn_ref, hn_scr, sems.at[0])
        dhn_copy = pltpu.make_async_copy(acc, dhn_ref, sems.at[0])

        def dup_copy(step, half):
            block, slot = step + half * nt, 2 * (step % 2) + half
            cols = pl.ds(pl.multiple_of((block % chip_blocks) * LANES, LANES), LANES)
            return pltpu.make_async_copy(dup_scr.at[slot], dup_ref.at[block // chip_blocks, :, cols], sems.at[1 + slot])

        @pl.when(j == 0)
        def _():
            hn_copy.start()
            acc[...] = jnp.zeros_like(acc)
            hn_copy.wait()
            for r in range(0, rows, LANES):
                hnt_scr[:, r:r + LANES] = hn_scr[r:r + LANES, :].T

        @pl.when(j >= 2)
        def _():
            dup_copy(j - 2, 0).wait()
            dup_copy(j - 2, 1).wait()

        g, v = g_ref[...], v_ref[...]
        gs = [_shift_down(g, s, rows) for s in range(FFN_CONV)]
        vs = [_shift_down(v, s, rows) for s in range(FFN_CONV)]
        ug, uv = bg_ref[...], bv_ref[...]
        for s in range(FFN_CONV):
            ug = ug + wg_ref[pl.ds(FFN_CONV - 1 - s, 1), :] * gs[s]
            uv = uv + wv_ref[pl.ds(FFN_CONV - 1 - s, 1), :] * vs[s]
        sig = _sigmoid(ug)
        dsig = d_ref[...] * sig
        dup = []
        for dpre, src, w_ref, dw_ref, db_ref in (
                (dsig * uv * (1.0 + ug * (1.0 - sig)), gs, wg_ref, dwg_ref, dbg_ref),
                (dsig * ug, vs, wv_ref, dwv_ref, dbv_ref)):
            dx = w_ref[pl.ds(FFN_CONV - 1, 1), :] * dpre
            for s in range(1, FFN_CONV):
                dx = dx + w_ref[pl.ds(FFN_CONV - 1 - s, 1), :] * _shift_up(dpre, s, rows)
            dup.append(dx.astype(BF16))
            for s in range(FFN_CONV):
                dw_ref[pl.ds(FFN_CONV - 1 - s, 1), :] = jnp.sum(dpre * src[s], axis=0, keepdims=True)
            db_ref[...] = jnp.sum(dpre, axis=0, keepdims=True)
        dup = jnp.concatenate(dup, axis=1)
        acc[...] += _dot_nt(dup, jnp.concatenate([upg_ref[...], upv_ref[...]], axis=1))
        dw = _dot(hnt_scr[...], dup)
        slot = 2 * (j % 2)
        dup_scr[slot] = dw[:, :LANES].astype(BF16)
        dup_scr[slot + 1] = dw[:, LANES:].astype(BF16)
        dup_copy(j, 0).start()
        dup_copy(j, 1).start()

        @pl.when(j == nt - 1)
        def _():
            dhn_copy.start()
            for step in (j - 1, j):
                dup_copy(step, 0).wait()
                dup_copy(step, 1).wait()
            dhn_copy.wait()

    col = pl.BlockSpec((rows, LANES), lambda j: (0, j))
    wsp = lambda shift: pl.BlockSpec((FFN_CONV, LANES), lambda j: (0, j + shift))
    bsp = lambda shift: pl.BlockSpec((1, LANES), lambda j: (0, j + shift))
    any_spec = pl.BlockSpec(memory_space=pl.ANY)
    dw_shape = jax.ShapeDtypeStruct((FFN_CONV, D_FF), F32)
    db_shape = jax.ShapeDtypeStruct((1, D_FF), F32)
    return _call(
        body, name=name, grid=(nt,),
        out_shape=(dw_shape, dw_shape, db_shape, db_shape, jax.ShapeDtypeStruct((rows, k), F32),
                   jax.ShapeDtypeStruct(w_up.shape, BF16)),
        in_specs=[col, col, col, wsp(0), wsp(nt), bsp(0), bsp(nt), weight_block(0), weight_block(nt), any_spec],
        out_specs=(wsp(0), wsp(0), bsp(0), bsp(0), any_spec, any_spec),
        operands=[up_g, up_v, dact, cw, cw, cb, cb, w_up, w_up, hn],
        scratch_shapes=[pltpu.VMEM((rows, k), F32), pltpu.VMEM((rows, k), BF16), pltpu.VMEM((k, rows), BF16),
                        pltpu.VMEM((4, k, LANES), BF16), pltpu.SemaphoreType.DMA((5,))],
        semantics=("arbitrary",), steps=steps)


def _dt_fwd(name, dtr, bias):
    rows = dtr.shape[0]
    tm = _row_tile(rows, LANES)

    def body(d_ref, b_ref, o_ref):
        v = d_ref[...] + b_ref[...]
        sp = jnp.maximum(v, 0.0) + jnp.log1p(jnp.exp(-jnp.abs(v)))
        lane = lax.broadcasted_iota(jnp.int32, (tm, LANES), 1)
        ok = _rows_mask(pl.program_id(0), tm) & (lane < SSM_HEADS)
        o_ref[...] = jnp.where(ok, sp, 0.0)

    return pl.pallas_call(
        body, name=name, out_shape=jax.ShapeDtypeStruct((rows, LANES), F32), grid=(rows // tm,),
        in_specs=[pl.BlockSpec((tm, LANES), lambda i: (i, 0)), pl.BlockSpec((1, LANES), lambda i: (0, 0))],
        out_specs=pl.BlockSpec((tm, LANES), lambda i: (i, 0)), compiler_params=_cparams(("parallel",)),
    )(dtr, bias)


def _dt_bwd(name, ddt, dtr, bias):
    rows = dtr.shape[0]
    tm = _row_tile(rows, LANES)

    def body(g_ref, d_ref, b_ref, o_ref, db_ref):
        i = pl.program_id(0)
        lane = lax.broadcasted_iota(jnp.int32, (tm, LANES), 1)
        ok = _rows_mask(i, tm) & (lane < SSM_HEADS)
        dv = jnp.where(ok, g_ref[...] * _sigmoid(d_ref[...] + b_ref[...]), 0.0)
        o_ref[...] = dv.astype(BF16)

        @pl.when(i == 0)
        def _():
            db_ref[...] = jnp.zeros_like(db_ref)

        db_ref[...] += jnp.sum(dv, axis=0, keepdims=True)

    row_spec = pl.BlockSpec((tm, LANES), lambda i: (i, 0))
    vec_spec = pl.BlockSpec((1, LANES), lambda i: (0, 0))
    return pl.pallas_call(
        body, name=name,
        out_shape=(jax.ShapeDtypeStruct((rows, LANES), BF16), jax.ShapeDtypeStruct((1, LANES), F32)),
        grid=(rows // tm,), in_specs=[row_spec, row_spec, vec_spec], out_specs=(row_spec, vec_spec),
        compiler_params=_cparams(("arbitrary",)),
    )(ddt, dtr, bias)


def _gate_fwd(name, y, zx, w, steps=()):
    rows = y.shape[0]
    tm = _row_tile(rows, D_INNER)

    def body(y_ref, z_ref, w_ref, o_ref):
        z = z_ref[...]
        g = y_ref[...] * (z * _sigmoid(z))
        r = lax.rsqrt(jnp.mean(g * g, axis=-1, keepdims=True) + RMS_EPS)
        o_ref[...] = (g * r * w_ref[...]).astype(BF16)

    row_spec = pl.BlockSpec((tm, D_INNER), lambda i: (i, 0))
    return _call(
        body, name=name, out_shape=jax.ShapeDtypeStruct((rows, D_INNER), BF16), grid=(rows // tm,),
        in_specs=[row_spec, row_spec, pl.BlockSpec((1, D_INNER), lambda i: (0, 0))],
        out_specs=row_spec, operands=[y, zx, w], semantics=("parallel",), steps=steps)


def _gate_bwd(name, dyn, y, zx, w):
    rows = y.shape[0]
    tm = _row_tile(rows, D_INNER)

    def body(d_ref, y_ref, z_ref, w_ref, dy_ref, dz_ref, dw_ref):
        i = pl.program_id(0)
        z, yv = z_ref[...], y_ref[...]
        sig = _sigmoid(z)
        sz = z * sig
        g = yv * sz
        r = lax.rsqrt(jnp.mean(g * g, axis=-1, keepdims=True) + RMS_EPS)
        ghat = g * r
        dn = d_ref[...]
        dghat = dn * w_ref[...]
        dg = r * (dghat - ghat * jnp.mean(dghat * ghat, axis=-1, keepdims=True))
        dy_ref[...] = dg * sz
        dz_ref[...] = (dg * yv * sig * (1.0 + z * (1.0 - sig))).astype(BF16)

        @pl.when(i == 0)
        def _():
            dw_ref[...] = jnp.zeros_like(dw_ref)

        dw_ref[...] += jnp.sum(dn * ghat, axis=0, keepdims=True)

    row_spec = pl.BlockSpec((tm, D_INNER), lambda i: (i, 0))
    vec_spec = pl.BlockSpec((1, D_INNER), lambda i: (0, 0))
    return pl.pallas_call(
        body, name=name,
        out_shape=(jax.ShapeDtypeStruct((rows, D_INNER), F32), jax.ShapeDtypeStruct((rows, D_MAIN), BF16),
                   jax.ShapeDtypeStruct((1, D_INNER), F32)),
        grid=(rows // tm,), in_specs=[row_spec, row_spec, row_spec, vec_spec],
        out_specs=(row_spec, row_spec, vec_spec), compiler_params=_cparams(("arbitrary",)),
    )(dyn, y, zx, w)


def _split3(x):
    hi = x.astype(BF16)
    r1 = x - hi.astype(F32)
    mid = r1.astype(BF16)
    lo = (r1 - mid.astype(F32)).astype(BF16)
    return hi, mid, lo


def _dot3_data_lhs(x, sel):
    sel16 = sel.astype(F32).astype(BF16)
    hi, mid, lo = _split3(x)
    return _dot(hi, sel16) + _dot(mid, sel16) + _dot(lo, sel16)


def _dot2_data_lhs(x, sel):
    sel16 = sel.astype(F32).astype(BF16)
    hi = x.astype(BF16)
    mid = (x - hi.astype(F32)).astype(BF16)
    return _dot(hi, sel16) + _dot(mid, sel16)


def _dot3_data_rhs(sel, x):
    sel16 = sel.astype(F32).astype(BF16)
    hi, mid, lo = _split3(x)
    return _dot(sel16, hi) + _dot(sel16, mid) + _dot(sel16, lo)


def _causal_masks():
    r = lax.broadcasted_iota(jnp.int32, (CHUNK, CHUNK), 0)
    c = lax.broadcasted_iota(jnp.int32, (CHUNK, CHUNK), 1)
    return r >= c, r <= c


def _expand_heads_matrix(g):
    k = lax.broadcasted_iota(jnp.int32, (LANES, GROUP_W), 0)
    j = lax.broadcasted_iota(jnp.int32, (LANES, GROUP_W), 1)
    return HEADS_PER_GROUP * g + jnp.right_shift(j, 6) == k


def _reduce_heads_matrix(g):
    j = lax.broadcasted_iota(jnp.int32, (GROUP_W, LANES), 0)
    k = lax.broadcasted_iota(jnp.int32, (GROUP_W, LANES), 1)
    return HEADS_PER_GROUP * g + jnp.right_shift(j, 6) == k


def _reduce_pair_matrix(g, p):
    j = lax.broadcasted_iota(jnp.int32, (LANES, LANES), 0)
    k = lax.broadcasted_iota(jnp.int32, (LANES, LANES), 1)
    return HEADS_PER_GROUP * g + 2 * p + jnp.right_shift(j, 6) == k


def _group_cols(ref, g, width):
    return ref.at[:, pl.ds(g * width, width)]


def _ssd_prep(name, dt, a128, steps=()):
    rows = dt.shape[0]
    nc = rows // CHUNK

    def body(dt_ref, a_ref, dte_ref, acs_ref, acst_ref):
        causal, _ = _causal_masks()
        dtv = dt_ref[...]
        acs = _dot3_data_rhs(causal, dtv) * a_ref[...]
        acst_ref[...] = acs.T[0:SSM_HEADS]
        for g in range(N_GROUPS):
            expand = _expand_heads_matrix(g)
            _group_cols(dte_ref, g, GROUP_W)[...] = _dot3_data_lhs(dtv, expand)
            _group_cols(acs_ref, g, GROUP_W)[...] = _dot3_data_lhs(acs, expand)

    blk = pl.BlockSpec((CHUNK, D_INNER), lambda c: (c, 0))
    shp = jax.ShapeDtypeStruct((rows, D_INNER), F32)
    return _call(
        body, name=name, out_shape=(shp, shp, jax.ShapeDtypeStruct((nc, SSM_HEADS, CHUNK), F32)), grid=(nc,),
        in_specs=[pl.BlockSpec((CHUNK, LANES), lambda c: (c, 0)), pl.BlockSpec((1, LANES), lambda c: (0, 0))],
        out_specs=(blk, blk, pl.BlockSpec((None, SSM_HEADS, CHUNK), lambda c: (c, 0, 0))),
        operands=[dt, a128], semantics=("parallel",), steps=steps)


def _ssd_common(x_ref, b_ref, c_ref, dte_ref, acs_ref):
    x = x_ref[...]
    dt_exp = dte_ref[...]
    acs_exp = acs_ref[...]
    tot_exp = acs_ref[pl.ds(CHUNK - 1, 1), :]
    xdt = x * dt_exp
    e_exp = jnp.exp(acs_exp)
    f_exp = jnp.exp(tot_exp - acs_exp)
    return _causal_masks(), x, dt_exp, acs_exp, tot_exp, xdt, e_exp, f_exp, b_ref[...], c_ref[...]


def _pair_decay(acs_pair, acs_row, e, causal):
    lane = lax.broadcasted_iota(jnp.int32, (CHUNK, LANES), 1)
    mine = (lane < HEAD_DIM) if e == 0 else (lane >= HEAD_DIM)
    a_l = jnp.where(mine, acs_pair, pltpu.roll(acs_pair, HEAD_DIM, 1))
    seg = a_l - acs_row
    dm = jnp.where(causal[0], jnp.exp(jnp.minimum(seg, 0.0)), 0.0)
    dmt = jnp.where(causal[1], jnp.exp(jnp.minimum(-seg, 0.0)), 0.0)
    return dm, dmt


def _ssd_specs(index_of_chunk):
    wide = pl.BlockSpec((CHUNK, D_INNER), lambda c: (index_of_chunk(c), 0))
    b_spec = pl.BlockSpec((CHUNK, D_BC), lambda c: (index_of_chunk(c), D_INNER // D_BC))
    c_spec = pl.BlockSpec((CHUNK, D_BC), lambda c: (index_of_chunk(c), D_INNER // D_BC + 1))
    rows_spec = pl.BlockSpec((None, SSM_HEADS, CHUNK), lambda c: (index_of_chunk(c), 0, 0))
    state_spec = pl.BlockSpec((N_GROUPS, None, D_STATE, GROUP_W), lambda c: (0, index_of_chunk(c), 0, 0))
    return wide, b_spec, c_spec, rows_spec, state_spec


def _ssd_fwd(name, xbc, dt_exp, acs_exp, acs_rows, dskexp, steps=()):
    rows = xbc.shape[0]
    nc = rows // CHUNK

    def body(x_ref, b_ref, c_ref, dte_ref, acs_ref, acst_ref, dsk_ref, y_ref, st_ref, s_scr):
        @pl.when(pl.program_id(0) == 0)
        def _():
            s_scr[...] = jnp.zeros_like(s_scr)

        lane = lax.broadcasted_iota(jnp.int32, (CHUNK, LANES), 1)
        for g in range(N_GROUPS):
            y_g = _group_cols(y_ref, g, GROUP_W)
            causal, x, _, acs_exp_v, tot_exp, xdt, e_exp, f_exp, bm, cm = _ssd_common(
                _group_cols(x_ref, g, GROUP_W), _group_cols(b_ref, g, D_STATE), _group_cols(c_ref, g, D_STATE),
                _group_cols(dte_ref, g, GROUP_W), _group_cols(acs_ref, g, GROUP_W))
            state = s_scr[g]
            st_ref[g] = state
            cb16, bb16 = cm.astype(BF16), bm.astype(BF16)
            cb = _dot_nt(cb16, bb16)
            base = e_exp * _dot(cb16, state.astype(BF16)) + _group_cols(dsk_ref, g, GROUP_W)[...] * x
            for p in range(HEADS_PER_GROUP // 2):
                sl = slice(p * LANES, (p + 1) * LANES)
                xp = xdt[:, sl].astype(BF16)
                yd = []
                for e in range(2):
                    acs_row = acst_ref[pl.ds(g * HEADS_PER_GROUP + 2 * p + e, 1), :]
                    dm, _ = _pair_decay(acs_exp_v[:, sl], acs_row, e, causal)
                    yd.append(_dot((cb * dm).astype(BF16), xp))
                y_g[:, sl] = jnp.where(lane < HEAD_DIM, yd[0], yd[1]) + base[:, sl]
            s_scr[g] = jnp.exp(tot_exp) * state + _dot_tn(bb16, (f_exp * xdt).astype(BF16))

    wide, b_spec, c_spec, rows_spec, state_spec = _ssd_specs(lambda c: c)
    return _call(
        body, name=name,
        out_shape=(jax.ShapeDtypeStruct((rows, D_INNER), F32),
                   jax.ShapeDtypeStruct((N_GROUPS, nc, D_STATE, GROUP_W), F32)),
        grid=(nc,),
        in_specs=[wide, b_spec, c_spec, wide, wide, rows_spec, pl.BlockSpec((1, D_INNER), lambda c: (0, 0))],
        out_specs=(wide, state_spec),
        scratch_shapes=[pltpu.VMEM((N_GROUPS, D_STATE, GROUP_W), F32)],
        operands=[xbc, xbc, xbc, dt_exp, acs_exp, acs_rows, dskexp], semantics=("arbitrary",), steps=steps)


def _ssd_bwd(name, xbc, dt_exp, acs_exp, acs_rows, dt, a128, dskexp, dy, states, steps=()):
    rows = xbc.shape[0]
    nc = rows // CHUNK
    last = nc - 1

    def body(x_ref, b_ref, c_ref, dte_ref, acs_ref, acst_ref, dt_ref, a128_ref, dsk_all, dy_all, st_all,
             dxbc_all, ddt_ref, dalog_ref, ddsk_ref, ds_all):
        dx_all, db_all, dc_all = (dxbc_all.at[:, :D_INNER], dxbc_all.at[:, D_INNER:D_INNER + D_BC],
                                  dxbc_all.at[:, D_INNER + D_BC:])

        @pl.when(pl.program_id(0) == 0)
        def _():
            ds_all[...] = jnp.zeros_like(ds_all)
            dalog_ref[...] = jnp.zeros_like(dalog_ref)
            ddsk_ref[...] = jnp.zeros_like(ddsk_ref)

        dacs = jnp.zeros((CHUNK, LANES), F32)
        ddt_x = jnp.zeros((CHUNK, LANES), F32)
        for g in range(N_GROUPS):
            dacs_g, ddt_x_g = group(
                g, _group_cols(x_ref, g, GROUP_W), _group_cols(b_ref, g, D_STATE), _group_cols(c_ref, g, D_STATE),
                _group_cols(dte_ref, g, GROUP_W), _group_cols(acs_ref, g, GROUP_W), acst_ref,
                _group_cols(dsk_all, g, GROUP_W), _group_cols(dy_all, g, GROUP_W), st_all.at[g],
                _group_cols(dx_all, g, GROUP_W), _group_cols(db_all, g, D_STATE), _group_cols(dc_all, g, D_STATE),
                ddsk_ref, ds_all.at[g])
            dacs, ddt_x = dacs + dacs_g, ddt_x + ddt_x_g
        _, causal_t = _causal_masks()
        da = _dot3_data_rhs(causal_t, dacs)
        ddt_ref[...] = da * a128_ref[...] + ddt_x
        dalog_ref[...] += jnp.sum(da * dt_ref[...], axis=0, keepdims=True) * a128_ref[...]

    def group(g, x_ref, b_ref, c_ref, dte_ref, acs_ref, acst_ref, dsk_ref, dy_ref, st_ref,
              dx_ref, db_ref, dc_ref, ddsk_ref, ds_scr):
        causal, x, dt_exp, acs_exp_v, tot_exp, xdt, e_exp, f_exp, bm, cm = _ssd_common(
            x_ref, b_ref, c_ref, dte_ref, acs_ref)
        reduce_heads = _reduce_heads_matrix(g)
        state, dstate = st_ref[...], ds_scr[...]
        dyv = dy_ref[...]
        cb16, bb16 = cm.astype(BF16), bm.astype(BF16)
        s16, ds16 = state.astype(BF16), dstate.astype(BF16)
        cb = _dot_nt(cb16, bb16)
        cbt = _dot_nt(bb16, cb16)
        cs = _dot(cb16, s16)
        bds = _dot(bb16, ds16)
        edy = e_exp * dyv
        fx = f_exp * xdt
        dxdt_base = f_exp * bds
        dc_acc = _dot_nt(edy.astype(BF16), s16)
        db_acc = _dot_nt(fx.astype(BF16), ds16)
        ds_scr[...] = jnp.exp(tot_exp) * dstate + _dot_tn(cb16, edy.astype(BF16))
        q = fx * bds
        dacs = _dot2_data_lhs(edy * cs - q, reduce_heads)
        dtot = jnp.sum(_dot2_data_lhs(q + jnp.exp(tot_exp) * dstate * state, reduce_heads), axis=0, keepdims=True)
        ddsk_ref[...] += jnp.sum(_dot2_data_lhs(dyv * x, reduce_heads), axis=0, keepdims=True)
        lane = lax.broadcasted_iota(jnp.int32, (CHUNK, LANES), 1)
        dcb = jnp.zeros((CHUNK, CHUNK), F32)
        dcbt = jnp.zeros((CHUNK, CHUNK), F32)
        ddt_x = jnp.zeros((CHUNK, LANES), F32)
        for p in range(HEADS_PER_GROUP // 2):
            sl = slice(p * LANES, (p + 1) * LANES)
            xp, dyp = xdt[:, sl], dyv[:, sl]
            xp16, dyp16 = xp.astype(BF16), dyp.astype(BF16)
            dxh = []
            for e in range(2):
                h = 2 * p + e
                mine = (lane < HEAD_DIM) if e == 0 else (lane >= HEAD_DIM)
                acs_row = acst_ref[pl.ds(g * HEADS_PER_GROUP + h, 1), :]
                dm, dmt = _pair_decay(acs_exp_v[:, sl], acs_row, e, causal)
                m, mt = cb * dm, cbt * dmt
                xh16 = jnp.where(mine, xp, 0.0).astype(BF16)
                dyh16 = jnp.where(mine, dyp, 0.0).astype(BF16)
                d_m = _dot_nt(dyh16, xp16)
                d_mt = _dot_nt(xh16, dyp16)
                dacs_h = (jnp.sum(d_m * m, axis=-1, keepdims=True)
                          - jnp.sum(d_mt * mt, axis=-1, keepdims=True))
                dacs = dacs + jnp.where(lane == HEADS_PER_GROUP * g + h, dacs_h, 0.0)
                dcb = dcb + d_m * dm
                dcbt = dcbt + d_mt * dmt
                dxh.append(_dot(mt.astype(BF16), dyp16))
            dxdt = jnp.where(lane < HEAD_DIM, dxh[0], dxh[1]) + dxdt_base[:, sl]
            dx_ref[:, sl] = dxdt * dt_exp[:, sl] + dsk_ref[:, sl] * dyp
            ddt_x = ddt_x + _dot2_data_lhs(dxdt * x[:, sl], _reduce_pair_matrix(g, p))
        dc_ref[...] = dc_acc + _dot(dcb.astype(BF16), bb16)
        db_ref[...] = db_acc + _dot(dcbt.astype(BF16), cb16)
        row = lax.broadcasted_iota(jnp.int32, (CHUNK, LANES), 0)
        return dacs + jnp.where(row == CHUNK - 1, dtot, 0.0), ddt_x

    wide, b_spec, c_spec, rows_spec, state_spec = _ssd_specs(lambda c: last - c)
    heads_spec = pl.BlockSpec((CHUNK, LANES), lambda c: (last - c, 0))
    vec_spec = pl.BlockSpec((1, LANES), lambda c: (0, 0))
    vec_shape = jax.ShapeDtypeStruct((1, LANES), F32)
    return _call(
        body, name=name,
        out_shape=(jax.ShapeDtypeStruct((rows, D_XBC), F32), jax.ShapeDtypeStruct((rows, LANES), F32),
                   vec_shape, vec_shape),
        grid=(nc,),
        in_specs=[wide, b_spec, c_spec, wide, wide, rows_spec, heads_spec, vec_spec,
                  pl.BlockSpec((1, D_INNER), lambda c: (0, 0)), wide, state_spec],
        out_specs=(pl.BlockSpec((CHUNK, D_XBC), lambda c: (last - c, 0)), heads_spec, vec_spec, vec_spec),
        scratch_shapes=[pltpu.VMEM((N_GROUPS, D_STATE, GROUP_W), F32)],
        operands=[xbc, xbc, xbc, dt_exp, acs_exp, acs_rows, dt, a128, dskexp, dy, states],
        semantics=("arbitrary",), steps=steps)


def _attn_visible(b, heads=1):
    row = jnp.bitwise_and(lax.broadcasted_iota(jnp.int32, (heads * CHUNK, 3 * CHUNK), 0), CHUNK - 1)
    col = lax.broadcasted_iota(jnp.int32, (heads * CHUNK, 3 * CHUNK), 1)
    bb = b + jnp.zeros_like(col)
    meta = (col < CHUNK) & (bb >= 1) & (col >= PAD_ROWS)
    prev = (col >= CHUNK) & (col < 2 * CHUNK) & (bb >= 2) & ((col - CHUNK) > row)
    cur = (col >= 2 * CHUNK) & ((col - 2 * CHUNK) <= row) & ((bb >= 1) | ((col - 2 * CHUNK) >= PAD_ROWS))
    return meta | prev | cur


def _attn_visible4(b):
    return _attn_visible(b, 4)


def _stack_heads(q_ref, sink_ref, kvh, scale):
    lane = lax.broadcasted_iota(jnp.int32, (CHUNK, LANES), 1)
    parts, sinks = [], []
    for pp in range(2):
        pair = kvh * 2 + pp
        qp = q_ref[:, pair * LANES:(pair + 1) * LANES] * scale
        for e in range(2):
            mine = (lane < HEAD_DIM) if e == 0 else (lane >= HEAD_DIM)
            parts.append(jnp.where(mine, qp, 0.0).astype(BF16))
            sinks.append(jnp.full((CHUNK, 1), sink_ref[2 * pair + e], F32))
    return jnp.concatenate(parts, axis=0), jnp.concatenate(sinks, axis=0)


def _attn_operands(q_ref, k0, kp, kc, v0, vp, vc, sink_ref):
    kcat, vcat, q4, sink4 = [], [], [], []
    for kvh in range(N_KV_HEADS):
        ksl = slice(kvh * LANES, (kvh + 1) * LANES)
        kcat.append(jnp.concatenate([k0[:, ksl], kp[:, ksl], kc[:, ksl]], axis=0).astype(BF16))
        vcat.append(jnp.concatenate([v0[:, ksl], vp[:, ksl], vc[:, ksl]], axis=0).astype(BF16))
        stacked, sinks = _stack_heads(q_ref, sink_ref, kvh, ATTN_SCALE)
        q4.append(stacked)
        sink4.append(sinks)
    return kcat, vcat, q4, sink4


def _attn_probs(q4, kcat, visible, sink4):
    heads = range(N_KV_HEADS)
    s = [jnp.where(visible, _dot_nt(q4[h], kcat[h]), NEG_INF) for h in heads]
    m = [jnp.maximum(jnp.max(s[h], axis=-1, keepdims=True), sink4[h]) for h in heads]
    pe = [jnp.exp(s[h] - m[h]) for h in heads]
    pe_sink = [jnp.exp(sink4[h] - m[h]) for h in heads]
    inv = [1.0 / (jnp.sum(pe[h], axis=-1, keepdims=True) + pe_sink[h]) for h in heads]
    return [pe[h] * inv[h] for h in heads], [pe_sink[h] * inv[h] for h in heads]


def _unstack_pairs(stacked, pp):
    lane = lax.broadcasted_iota(jnp.int32, (CHUNK, LANES), 1)
    return jnp.where(lane < HEAD_DIM, stacked[(2 * pp) * CHUNK:(2 * pp + 1) * CHUNK],
                     stacked[(2 * pp + 1) * CHUNK:(2 * pp + 2) * CHUNK])


def _attn_specs(colblock):
    blk = lambda f: pl.BlockSpec((CHUNK, 2 * D_KV), f)
    return [blk(lambda b: (0, colblock)), blk(lambda b: (jnp.maximum(b - 1, 0), colblock)), blk(lambda b: (b, colblock))]


def _attn_fwd(name, q, kv2, sinks, steps=()):
    rows = q.shape[0]

    def body(q_ref, k0, kp, kc, v0, vp, vc, sink_ref, o_ref):
        visible = _attn_visible4(pl.program_id(0))
        kcat, vcat, q4, sink4 = _attn_operands(q_ref, k0, kp, kc, v0, vp, vc, sink_ref)
        pn, _ = _attn_probs(q4, kcat, visible, sink4)
        o4 = [_dot(pn[h].astype(BF16), vcat[h]) for h in range(N_KV_HEADS)]
        for kvh in range(N_KV_HEADS):
            for pp in range(2):
                qsl = slice((kvh * 2 + pp) * LANES, (kvh * 2 + pp + 1) * LANES)
                o_ref[:, qsl] = _unstack_pairs(o4[kvh], pp).astype(BF16)

    return _call(
        body, name=name, out_shape=jax.ShapeDtypeStruct((rows, D_MODEL), BF16), grid=(rows // CHUNK,),
        in_specs=[pl.BlockSpec((CHUNK, D_MODEL), lambda b: (b, 0))] + _attn_specs(0) + _attn_specs(1)
        + [pl.BlockSpec(memory_space=pltpu.SMEM)],
        out_specs=pl.BlockSpec((CHUNK, D_MODEL), lambda b: (b, 0)),
        operands=[q, kv2, kv2, kv2, kv2, kv2, kv2, sinks], semantics=("parallel",), steps=steps)


def _attn_bwd(name, q, kv2, sinks, do, steps=()):
    rows = q.shape[0]

    def body(q_ref, k0, kp, kc, v0, vp, vc, sink_ref, do_ref,
             dq_ref, dkc_ref, dkp_ref, dvc_ref, dvp_ref, dkm_ref, dvm_ref, dsink_ref):
        @pl.when(pl.program_id(0) == 0)
        def _():
            dkm_ref[...] = jnp.zeros_like(dkm_ref)
            dvm_ref[...] = jnp.zeros_like(dvm_ref)
            dsink_ref[...] = jnp.zeros_like(dsink_ref)

        visible = _attn_visible4(pl.program_id(0))
        heads = range(N_KV_HEADS)
        lane1 = lax.broadcasted_iota(jnp.int32, (1, LANES), 1)
        kcat, vcat, q4, sink4 = _attn_operands(q_ref, k0, kp, kc, v0, vp, vc, sink_ref)
        do4 = [_stack_heads(do_ref, sink_ref, h, 1.0)[0] for h in heads]
        pn, psink = _attn_probs(q4, kcat, visible, sink4)
        dp = [_dot_nt(do4[h], vcat[h]) for h in heads]
        delta = [jnp.sum(pn[h] * dp[h], axis=-1, keepdims=True) for h in heads]
        ds16 = [(pn[h] * (dp[h] - delta[h])).astype(BF16) for h in heads]
        dq4 = [_dot(ds16[h], kcat[h]) for h in heads]
        dk_acc = [_dot_tn(ds16[h], q4[h]) for h in heads]
        dv_acc = [_dot_tn(pn[h].astype(BF16), do4[h]) for h in heads]
        dsink = jnp.zeros((1, LANES), F32)
        for kvh in heads:
            ksl = slice(kvh * LANES, (kvh + 1) * LANES)
            sink_terms = psink[kvh] * delta[kvh]
            for j in range(4):
                part = jnp.sum(sink_terms[j * CHUNK:(j + 1) * CHUNK], axis=0, keepdims=True)
                dsink = dsink - jnp.where(lane1 == kvh * 4 + j, part, 0.0)
            for pp in range(2):
                qsl = slice((kvh * 2 + pp) * LANES, (kvh * 2 + pp + 1) * LANES)
                dq_ref[:, qsl] = (_unstack_pairs(dq4[kvh], pp) * ATTN_SCALE).astype(BF16)
            dkm_ref[:, ksl] += dk_acc[kvh][0:CHUNK]
            dvm_ref[:, ksl] += dv_acc[kvh][0:CHUNK]
            dkp_ref[:, ksl] = dk_acc[kvh][CHUNK:2 * CHUNK]
            dvp_ref[:, ksl] = dv_acc[kvh][CHUNK:2 * CHUNK]
            dkc_ref[:, ksl] = dk_acc[kvh][2 * CHUNK:3 * CHUNK]
            dvc_ref[:, ksl] = dv_acc[kvh][2 * CHUNK:3 * CHUNK]
        dsink_ref[...] += dsink

    qspec = pl.BlockSpec((CHUNK, D_MODEL), lambda b: (b, 0))
    kvspec = pl.BlockSpec((CHUNK, 2 * D_KV), lambda b: (b, 0))
    fixed = pl.BlockSpec((CHUNK, 2 * D_KV), lambda b: (0, 0))
    kv_shape = jax.ShapeDtypeStruct((rows, 2 * D_KV), F32)
    meta_shape = jax.ShapeDtypeStruct((CHUNK, 2 * D_KV), F32)
    return _call(
        body, name=name,
        out_shape=(jax.ShapeDtypeStruct((rows, D_MODEL), BF16), kv_shape, kv_shape, kv_shape, kv_shape,
                   meta_shape, meta_shape, jax.ShapeDtypeStruct((1, LANES), F32)),
        grid=(rows // CHUNK,),
        in_specs=[qspec] + _attn_specs(0) + _attn_specs(1) + [pl.BlockSpec(memory_space=pltpu.SMEM), qspec],
        out_specs=(qspec, kvspec, kvspec, kvspec, kvspec, fixed, fixed, pl.BlockSpec((1, LANES), lambda b: (0, 0))),
        operands=[q, kv2, kv2, kv2, kv2, kv2, kv2, sinks, do], semantics=("arbitrary",), steps=steps)


def _kv_grad_combine(name, dk_cur, dk_prev, dk_meta, dv_cur, dv_prev, dv_meta):
    rows = dk_cur.shape[0]
    nb = rows // CHUNK
    width = 2 * D_KV

    def body(kc_ref, kp_ref, km_ref, vc_ref, vp_ref, vm_ref, o_ref):
        jj = pl.program_id(0) + jnp.zeros((CHUNK, 1), jnp.int32)
        for half, (c_ref, p_ref, m_ref) in enumerate(((kc_ref, kp_ref, km_ref), (vc_ref, vp_ref, vm_ref))):
            total = c_ref[...] + jnp.where(jj < nb - 1, p_ref[...], 0.0) + jnp.where(jj == 0, m_ref[...], 0.0)
            o_ref[:, half * width:(half + 1) * width] = total.astype(BF16)

    blk = lambda f: pl.BlockSpec((CHUNK, width), f)
    three = lambda: [blk(lambda j: (j, 0)), blk(lambda j: (jnp.minimum(j + 1, nb - 1), 0)), blk(lambda j: (0, 0))]
    return pl.pallas_call(
        body, name=name, out_shape=jax.ShapeDtypeStruct((rows, 2 * width), BF16), grid=(nb,),
        in_specs=three() + three(), out_specs=pl.BlockSpec((CHUNK, 2 * width), lambda j: (j, 0)),
        compiler_params=_cparams(("parallel",)),
    )(dk_cur, dk_prev, dk_meta, dv_cur, dv_prev, dv_meta)


def _adamw(name, w, g, m, v, steps=()):
    rows, width = w.shape
    tr = rows
    for cand in range(8, rows + 1, 8):
        if rows % cand == 0 and cand * width * 4 <= (1 << 20):
            tr = cand

    def body(*refs):
        _adamw_update(*refs)

    blk = pl.BlockSpec((tr, width), lambda i: (i, 0))
    shp = jax.ShapeDtypeStruct((rows, width), F32)
    return _call(body, name=name, out_shape=(shp, shp, shp), grid=(rows // tr,), in_specs=[blk] * 4,
                 out_specs=(blk,) * 3, operands=[w, g, m, v], semantics=("parallel",), steps=steps)


def _adamw_update(w_ref, g_ref, m_ref, v_ref, d_ref, mo_ref, vo_ref):
    gv = g_ref[...]
    mn = ADAM_B1 * m_ref[...] + (1.0 - ADAM_B1) * gv
    vn = ADAM_B2 * v_ref[...] + (1.0 - ADAM_B2) * (gv * gv)
    m_hat = mn / (1.0 - ADAM_B1 ** ADAM_STEP)
    v_hat = vn / (1.0 - ADAM_B2 ** ADAM_STEP)
    d_ref[...] = -ADAM_LR * (m_hat / (jnp.sqrt(v_hat) + ADAM_EPS) + ADAM_WD * w_ref[...])
    mo_ref[...] = mn
    vo_ref[...] = vn


def _adamw_small(name, ws, gs, ms, vs):
    n = len(ws)

    def body(*refs):
        for i in range(n):
            _adamw_update(*refs[i::n])

    shapes = [jax.ShapeDtypeStruct(a.shape, F32) for a in ws]
    outs = pl.pallas_call(body, name=name, out_shape=shapes * 3, in_specs=[VMEM_SPEC] * (4 * n),
                          out_specs=[VMEM_SPEC] * (3 * n), compiler_params=_cparams())(*ws, *gs, *ms, *vs)
    return outs[:n], outs[n:2 * n], outs[2 * n:]


def _ffn_fwd(tag, h, hn, p, i, plan):
    up_g, up_v, act = _ffn_up_conv(f"ffn{tag}_up", hn, plan.weight("f_w_up", i), p["f_conv_w"][i],
                                   p["f_conv_b"][i:i + 1], steps=plan.steps(f"ffn{tag}_up"))
    pre = _mm(f"ffn{tag}_down", act, plan.weight("f_w_down", i), "nn", steps=plan.steps(f"ffn{tag}_down"))
    return pre, (h, hn, up_g, up_v, act, pre)


def _ffn_bwd(tag, dpre, saved, p, i, plan):
    h, hn, up_g, up_v, act, pre = saved
    plan.grad("f_w_down", i, _mm(f"ffn{tag}_down_dw", act, dpre, "tn", out_dtype=BF16))
    dact = _mm(f"ffn{tag}_down_dx", dpre, plan.weight("f_w_down", i), "nt", steps=plan.steps(f"ffn{tag}_down_dx"))
    gwg, gwv, gbg, gbv, dhn, g_up = _ffn_conv_bwd(
        f"ffn{tag}_conv_bwd", up_g, up_v, dact, p["f_conv_w"][i], p["f_conv_b"][i:i + 1], hn,
        plan.weight("f_w_up", i), steps=plan.steps(f"ffn{tag}_conv_bwd"))
    g_cw, g_cb = jnp.concatenate([gwg, gwv], axis=1), jnp.concatenate([gbg, gbv], axis=1)
    plan.grad("f_w_up", i, g_up)
    return dhn, dict(f_conv_w=g_cw, f_conv_b=g_cb)


def _lanes_pad(a, width=LANES):
    return jnp.pad(a, [(0, 0)] * (a.ndim - 1) + [(0, width - a.shape[-1])])


def _dup_heads(w):
    rows = w.shape[0]
    w = w.reshape(rows, 2 * N_KV_HEADS, 1, HEAD_DIM)
    return jnp.broadcast_to(w, (rows, 2 * N_KV_HEADS, 2, HEAD_DIM)).reshape(rows, 4 * D_KV)


def _undup_heads(g):
    rows = g.shape[0]
    return g.reshape(rows, 2 * N_KV_HEADS, 2, HEAD_DIM).sum(axis=2).reshape(rows, 2 * D_KV)


def _local_step(x2, target, p, plan):
    seq = x2.shape[0]
    rows = seq + CHUNK
    g = {}

    h0 = jnp.concatenate([jnp.zeros((PAD_ROWS, D_MODEL), F32), p["meta_tokens"], x2], axis=0)

    w_in = plan.weight("a_w_in")
    w_dt = jnp.pad(w_in[D_MAIN:], ((0, LANES - SSM_HEADS), (0, 0)))
    dt_bias = _lanes_pad(p["a_dt_bias"])
    a128 = _lanes_pad(-jnp.exp(p["a_a_log"]))
    dskexp = jnp.repeat(p["a_d_skip"].reshape(SSM_HEADS), HEAD_DIM).reshape(1, D_INNER)

    hn0 = _rms_fwd("a_norm", h0, p["a_norm_pre"])
    zx = _mm("a_in_main", hn0, w_in, "nt", k_rows=D_MAIN, steps=plan.steps("a_in_main"))
    dtr = _mm("a_in_dt", hn0, w_dt, "nt")
    xbc = _conv4_fwd("a_conv", zx, p["a_conv_w"], p["a_conv_b"], steps=plan.steps("a_conv"))
    dt = _dt_fwd("a_dt", dtr, dt_bias)
    dt_exp, acs_exp, acs_rows = _ssd_prep("a_ssd_prep", dt, a128, steps=plan.steps("a_ssd_prep"))
    y, states = _ssd_fwd("a_ssd", xbc, dt_exp, acs_exp, acs_rows, dskexp, steps=plan.steps("a_ssd"))
    yn = _gate_fwd("a_gate", y, zx, p["a_gate_norm"], steps=plan.steps("a_gate"))
    mix = _mm("a_out", yn, plan.weight("a_w_out"), "nn", steps=plan.steps("a_out"))
    h1, (hn_f0,) = _resid_norm_fwd("a_resid", h0, mix, p["a_norm_post"], [p["f_norm_pre"][0:1]])

    pre_f0, ffn0 = _ffn_fwd("0", h1, hn_f0, p, 0, plan)
    h2, (hkv, hn2) = _resid_norm_fwd("ffn0_resid", h1, pre_f0, p["f_norm_post"][0:1], [p["kv_norm"], p["b_norm_pre"]])

    w_kv2 = _dup_heads(plan.weight("w_kv"))
    kv2 = _mm("kv_proj", hkv, w_kv2, "nn")
    q = _mm("b_q", hn2, plan.weight("b_w_q"), "nn")
    sinks = p["b_sinks"].reshape(N_Q_HEADS)
    o = _attn_fwd("b_attn", q, kv2, sinks, steps=plan.steps("b_attn"))
    attn = _mm("b_o", o, plan.weight("b_w_o"), "nn", steps=plan.steps("b_o"))
    h3, (hn_f1,) = _resid_norm_fwd("b_resid", h2, attn, p["b_norm_post"], [p["f_norm_pre"][1:2]])

    pre_f1, ffn1 = _ffn_fwd("1", h3, hn_f1, p, 1, plan)
    dh, loss_vec, dpre_f1, g_post1 = _resid_norm_loss("ffn1_resid_loss", h3, pre_f1, p["f_norm_post"][1:2], target)
    loss = loss_vec[0, 0]

    dhn_f1, g1 = _ffn_bwd("1", dpre_f1, ffn1, p, 1, plan)
    dh, g_pre1, dpre, g["b_norm_post"] = _norm_bwd_add("ffn1_norm_bwd", dh, dhn_f1, h3, p["f_norm_pre"][1:2],
                                                        then=(attn, p["b_norm_post"]))
    plan.grad("b_w_o", None, _mm("b_o_dw", o, dpre, "tn", out_dtype=BF16))
    do = _mm("b_o_dx", dpre, plan.weight("b_w_o"), "nt", steps=plan.steps("b_o_dx"))
    dq, dkc, dkp, dvc, dvp, dkm, dvm, dsink = _attn_bwd("b_attn_bwd", q, kv2, sinks, do, steps=plan.steps("b_attn_bwd"))
    g["b_sinks"] = dsink[:, :N_Q_HEADS]
    dhn2 = _mm("b_q_dx", dq, plan.weight("b_w_q"), "nt")
    plan.grad("b_w_q", None, _mm("b_q_dw", hn2, dq, "tn", out_dtype=BF16))
    dh, g["b_norm_pre"] = _norm_bwd_add("b_norm_bwd", dh, dhn2, h2, p["b_norm_pre"])
    dkv2 = _kv_grad_combine("kv_grad", dkc, dkp, dkm, dvc, dvp, dvm)
    dhkv = _mm("kv_proj_dx", dkv2, w_kv2, "nt")
    plan.grad("w_kv", None, _undup_heads(_mm("kv_proj_dw", hkv, dkv2, "tn")))
    dh, g["kv_norm"], dpre_f0, g_post0 = _norm_bwd_add("kv_norm_bwd", dh, dhkv, h2, p["kv_norm"],
                                                       then=(pre_f0, p["f_norm_post"][0:1]))

    dhn_f0, g0 = _ffn_bwd("0", dpre_f0, ffn0, p, 0, plan)
    dh, g_pre0, dpre, g["a_norm_post"] = _norm_bwd_add("ffn0_norm_bwd", dh, dhn_f0, h1, p["f_norm_pre"][0:1],
                                                        then=(mix, p["a_norm_post"]))
    g["f_norm_post"] = jnp.concatenate([g_post0, g_post1], axis=0)
    g["f_norm_pre"] = jnp.concatenate([g_pre0, g_pre1], axis=0)
    g["f_conv_w"] = jnp.stack([g0["f_conv_w"], g1["f_conv_w"]])
    g["f_conv_b"] = jnp.concatenate([g0["f_conv_b"], g1["f_conv_b"]], axis=0)
    plan.grad("a_w_out", None, _mm("a_out_dw", yn, dpre, "tn", out_dtype=BF16))
    dyn = _mm("a_out_dx", dpre, plan.weight("a_w_out"), "nt", steps=plan.steps("a_out_dx"))
    dy, dzx, g["a_gate_norm"] = _gate_bwd("a_gate_bwd", dyn, y, zx, p["a_gate_norm"])
    dxbc, ddt, dalog, ddsk = _ssd_bwd("a_ssd_bwd", xbc, dt_exp, acs_exp, acs_rows, dt, a128, dskexp, dy, states,
                                      steps=plan.steps("a_ssd_bwd"))
    g["a_a_log"] = dalog[:, :SSM_HEADS]
    g["a_d_skip"] = ddsk[:, :SSM_HEADS]
    ddtr, dbias = _dt_bwd("a_dt_bwd", ddt, dtr, dt_bias)
    g["a_dt_bias"] = dbias[:, :SSM_HEADS]
    dzx, g["a_conv_w"], g["a_conv_b"] = _conv4_bwd("a_conv_bwd", zx, dxbc, p["a_conv_w"], p["a_conv_b"], dzx)
    g_in = _mm("a_in_main_dw", dzx, hn0, "tn", out_dtype=BF16, out_rows=D_IN_PROJ, steps=plan.steps("a_in_main_dw"))
    plan.grad("a_w_in", None, _tn_rows_into("a_in_dt_dw", ddtr, hn0, g_in, D_MAIN, SSM_HEADS))
    dhn0 = _mm("a_in_dt_dx", ddtr, w_dt, "nn", steps=plan.steps("a_in_dt_dx"))
    dhn0 = _mm("a_in_main_dx", dzx, w_in, "nn", acc=dhn0, steps=plan.steps("a_in_main_dx"))
    dh_first, grad_x, g["a_norm_pre"] = _norm_bwd_add("a_norm_bwd", dh, dhn0, h0, p["a_norm_pre"],
                                                      split_first_block=True, steps=plan.steps("a_norm_bwd"))
    g["meta_tokens"] = dh_first[PAD_ROWS:]
    return loss, grad_x, g


ANY = pl.BlockSpec(memory_space=pl.ANY)
VMEM_SPEC = pl.BlockSpec(memory_space=pltpu.VMEM)


def _allgather_small(name, shard):
    rows = shard.shape[0]

    def body(s_ref, o_ref, send_sems, recv_sems):
        x, y, c = _place()
        me = 2 * x + y
        o_ref[me] = s_ref[...]
        chips = _other_chips(x, y)
        sends = [pltpu.make_async_remote_copy(s_ref, o_ref.at[me], send_sems.at[j], recv_sems.at[j],
                                              device_id=(cx, cy, c), device_id_type=MESH)
                 for j, (cx, cy) in enumerate(chips)]
        for cp in sends:
            cp.start()
        for j, (cx, cy) in enumerate(chips):
            pltpu.make_async_remote_copy(s_ref, o_ref.at[2 * cx + cy], send_sems.at[j], recv_sems.at[j],
                                         device_id=(cx, cy, c), device_id_type=MESH).wait_recv()
        for cp in sends:
            cp.wait_send()

    return pl.pallas_call(
        body, name=name, out_shape=jax.ShapeDtypeStruct((N_CHIPS, rows, LANES), F32),
        in_specs=[VMEM_SPEC], out_specs=VMEM_SPEC,
        scratch_shapes=[pltpu.SemaphoreType.DMA((3,)), pltpu.SemaphoreType.DMA((3,))],
        compiler_params=pltpu.CompilerParams(vmem_limit_bytes=VMEM_LIMIT),
    )(shard)


def _row_block(rows, width, itemsize, align, budget=2 << 20):
    best = rows
    for cand in range(align, rows + 1, align):
        if rows % cand == 0 and cand * width * itemsize <= budget:
            best = cand
    return best


def _cast_into_slot(name, chip, w, layer=None):
    rows, width = w.shape[-2:]
    tr = _row_block(rows, width, 4, 16)
    if layer is None:
        in_spec = pl.BlockSpec((tr, width), lambda i, chip_ref: (i, 0))
    else:
        in_spec = pl.BlockSpec((None, tr, width), lambda i, chip_ref: (layer, i, 0))

    def body(chip_ref, w_ref, o_ref):
        o_ref[...] = w_ref[...].astype(BF16)

    return pl.pallas_call(
        body, name=name, out_shape=jax.ShapeDtypeStruct((N_CHIPS, rows, width), BF16),
        grid_spec=pltpu.PrefetchScalarGridSpec(
            num_scalar_prefetch=1, grid=(rows // tr,), in_specs=[in_spec],
            out_specs=pl.BlockSpec((None, tr, width), lambda i, chip_ref: (chip_ref[0], i, 0))),
        compiler_params=_cparams(("parallel",)),
    )(chip, w)


def _allreduce_small(name, vec):
    rows = -(-vec.shape[0] // (2 * SUBLANES)) * (2 * SUBLANES)
    hr = rows // 2
    padded = jnp.pad(vec, ((0, rows - vec.shape[0]), (0, 0)))

    def body(v_ref, o_ref, theirs, pair, by_chip, send_sems, recv_sems):
        x, y, c = _place()
        me = 2 * x + y
        sibling = (x, y, 1 - c)
        mine = pl.ds(pl.multiple_of(c * hr, SUBLANES), hr)
        other = pl.ds(pl.multiple_of((1 - c) * hr, SUBLANES), hr)

        swap = _remote(v_ref, theirs, send_sems, recv_sems, 0, sibling)
        swap.start()
        swap.wait()
        south = (c + jnp.zeros((1, 1), jnp.int32)) == 0
        pair[...] = jnp.where(south, v_ref[...], theirs[...]) + jnp.where(south, theirs[...], v_ref[...])

        by_chip[me] = pair[mine, :]
        sends = [_remote(by_chip.at[me], by_chip.at[me], send_sems, recv_sems, 1 + j, (cx, cy, c))
                 for j, (cx, cy) in enumerate(_other_chips(x, y))]
        for cp in sends:
            cp.start()
        for j, (cx, cy) in enumerate(_other_chips(x, y)):
            _remote(by_chip.at[me], by_chip.at[2 * cx + cy], send_sems, recv_sems, 1 + j, (cx, cy, c)).wait_recv()
        for cp in sends:
            cp.wait_send()
        total = by_chip[0]
        for s in range(1, N_CHIPS):
            total = total + by_chip[s]

        o_ref[mine, :] = total
        back = _remote(o_ref.at[mine], o_ref.at[mine], send_sems, recv_sems, 4, sibling)
        back.start()
        _remote(o_ref.at[other], o_ref.at[other], send_sems, recv_sems, 4, sibling).wait_recv()
        back.wait_send()

    out = pl.pallas_call(
        body, name=name, out_shape=jax.ShapeDtypeStruct((rows, LANES), F32),
        in_specs=[VMEM_SPEC], out_specs=VMEM_SPEC,
        scratch_shapes=[pltpu.VMEM((rows, LANES), F32), pltpu.VMEM((rows, LANES), F32),
                        pltpu.VMEM((N_CHIPS, hr, LANES), F32), pltpu.SemaphoreType.DMA((5,)),
                        pltpu.SemaphoreType.DMA((5,))],
        compiler_params=pltpu.CompilerParams(vmem_limit_bytes=VMEM_LIMIT),
    )(padded)
    return out[:vec.shape[0]]


def _rs_pair_add(name, place, grads, partner, split="rows"):
    _, half_rows, width = partner.shape
    tr = _row_block(half_rows, width, 2, 16)
    nb = half_rows // tr
    if split == "rows":
        mine = pl.BlockSpec((None, tr, width), lambda s, i, pr: (s, pr[1] * nb + i, 0))
    else:
        mine = pl.BlockSpec((None, tr, width), lambda s, i, pr: (s, i, pr[1]))

    def body(place_ref, g_ref, p_ref, o_ref):
        o_ref[...] = (g_ref[...].astype(F32) + p_ref[...].astype(F32)).astype(BF16)

    return pl.pallas_call(
        body, name=name, out_shape=jax.ShapeDtypeStruct(partner.shape, BF16),
        grid_spec=pltpu.PrefetchScalarGridSpec(
            num_scalar_prefetch=1, grid=(N_CHIPS, nb),
            in_specs=[mine, pl.BlockSpec((None, tr, width), lambda s, i, pr: (s, i, 0))],
            out_specs=pl.BlockSpec((None, tr, width), lambda s, i, pr: (s, i, 0))),
        compiler_params=_cparams(("parallel", "parallel")),
    )(place, grads, partner)


def _rs_chip_add(name, place, mine, others, split="rows"):
    _, half_rows, width = mine.shape
    tr = _row_block(half_rows, width, 4, 16, budget=1 << 20)
    nb = half_rows // tr
    if split == "rows":
        out_shape, out_spec = (2 * half_rows, width), pl.BlockSpec((tr, width), lambda i, pr: (pr[1] * nb + i, 0))
    else:
        out_shape, out_spec = (half_rows, 2 * width), pl.BlockSpec((tr, width), lambda i, pr: (i, pr[1]))

    def body(place_ref, q_ref, r_ref, o_ref):
        acc = q_ref[...].astype(F32)
        for j in range(3):
            acc = acc + r_ref[j].astype(F32)
        o_ref[...] = acc

    return pl.pallas_call(
        body, name=name, out_shape=jax.ShapeDtypeStruct(out_shape, F32),
        grid_spec=pltpu.PrefetchScalarGridSpec(
            num_scalar_prefetch=1, grid=(nb,),
            in_specs=[pl.BlockSpec((None, tr, width), lambda i, pr: (pr[0], i, 0)),
                      pl.BlockSpec((3, tr, width), lambda i, pr: (0, i, 0))],
            out_specs=out_spec),
        compiler_params=_cparams(("parallel",)),
    )(place, mine, others)


WEIGHTS = ["meta_tokens", "a_norm_pre", "a_w_in", "a_conv_w", "a_conv_b", "a_dt_bias", "a_a_log", "a_d_skip",
           "a_gate_norm", "a_w_out", "a_norm_post", "kv_norm", "w_kv", "b_norm_pre", "b_w_q", "b_sinks", "b_w_o",
           "b_norm_post", "f_norm_pre", "f_w_up", "f_conv_w", "f_conv_b", "f_w_down", "f_norm_post"]
FULL_SHAPE = {
    "meta_tokens": (16, 1024), "a_norm_pre": (1, 1024), "a_w_in": (1, 1024, 5152), "a_conv_w": (1, 4, 3072),
    "a_conv_b": (1, 3072), "a_dt_bias": (1, 32), "a_a_log": (1, 32), "a_d_skip": (1, 32), "a_gate_norm": (1, 2048),
    "a_w_out": (1, 2048, 1024), "a_norm_post": (1, 1024), "kv_norm": (1024,), "w_kv": (1024, 512),
    "b_norm_pre": (1, 1024), "b_w_q": (1, 1024, 1024), "b_sinks": (1, 16), "b_w_o": (1, 1024, 1024),
    "b_norm_post": (1, 1024), "f_norm_pre": (2, 1024), "f_w_up": (2, 1024, 5632), "f_conv_w": (2, 3, 5632),
    "f_conv_b": (2, 5632), "f_w_down": (2, 2816, 1024), "f_norm_post": (2, 1024),
}
SHARD_AXIS = {
    "meta_tokens": 1, "a_norm_pre": 1, "a_w_in": 2, "a_conv_w": 2, "a_conv_b": 1, "a_dt_bias": None, "a_a_log": None,
    "a_d_skip": None, "a_gate_norm": 1, "a_w_out": 1, "a_norm_post": 1, "kv_norm": None, "w_kv": 0, "b_norm_pre": None,
    "b_w_q": 1, "b_sinks": None, "b_w_o": 1, "b_norm_post": None, "f_norm_pre": None, "f_w_up": 2, "f_conv_w": 2,
    "f_conv_b": None, "f_w_down": 1, "f_norm_post": None,
}
BIG = ["a_w_in", "a_w_out", "w_kv", "b_w_q", "b_w_o", "f_w_up", "f_w_down"]
SMALL = [n for n in WEIGHTS if n not in BIG]
SMALL_SHARDED = [n for n in SMALL if SHARD_AXIS[n] is not None]


def _shard_shape(name):
    shape = list(FULL_SHAPE[name])
    if SHARD_AXIS[name] is not None:
        shape[SHARD_AXIS[name]] //= N_CHIPS
    return tuple(shape)


def _numel(shape):
    return int(math.prod(shape))


SUBLANES = 8


def _packed_rows(shape):
    rows = -(-_numel(shape) // LANES)
    return -(-rows // SUBLANES) * SUBLANES


def _pack(arrays):
    parts = []
    for a in arrays:
        size, rows = _numel(a.shape), _packed_rows(a.shape)
        if size % LANES == 0:
            part = jnp.pad(a.reshape(size // LANES, LANES), ((0, rows - size // LANES), (0, 0)))
        else:
            part = jnp.pad(a.reshape(-1), (0, rows * LANES - size)).reshape(rows, LANES)
        parts.append(part)
    return jnp.concatenate(parts, axis=0)


def _unpack(packed, names, shape_of):
    out, off = {}, 0
    lead = packed.shape[:-2]
    for n in names:
        shape = tuple(shape_of(n))
        size, rows = _numel(shape), _packed_rows(shape)
        part = packed[..., off:off + rows, :]
        if size % LANES == 0:
            out[n] = part[..., :size // LANES, :].reshape(lead + shape)
        else:
            out[n] = part.reshape(lead + (rows * LANES,))[..., :size].reshape(lead + shape)
        off += rows
    return out


def _split_chips(name, full):
    ax = SHARD_AXIS[name]
    shape = full.shape
    cut = shape[:ax] + (N_CHIPS, shape[ax] // N_CHIPS) + shape[ax + 1:]
    return jnp.moveaxis(full.reshape(cut), ax, 0)


def _join_chips(name, stacked):
    ax = SHARD_AXIS[name]
    moved = jnp.moveaxis(stacked, 0, ax)
    shape = moved.shape
    return moved.reshape(shape[:ax] + (shape[ax] * shape[ax + 1],) + shape[ax + 2:])


def _as2d(a):
    return a.reshape(-1, a.shape[-1])


BUFFERS = [("a_w_in", "a_w_in", None), ("a_w_out", "a_w_out", None), ("w_kv", "w_kv", None),
           ("b_w_q", "b_w_q", None), ("b_w_o", "b_w_o", None), ("f_w_up0", "f_w_up", 0), ("f_w_up1", "f_w_up", 1),
           ("f_w_down0", "f_w_down", 0), ("f_w_down1", "f_w_down", 1)]


TRANSPOSED = ("a_w_in",)
SPLIT = {"a_w_in": "cols"}


def _local_shard(arrays, weight, layer):
    if weight in TRANSPOSED:
        return arrays[weight][0].T
    return _as2d(arrays[weight]) if layer is None else arrays[weight]


def _weight_from_gathered(weight, buf):
    if weight == "f_w_up":
        return buf
    return buf.reshape(N_CHIPS * buf.shape[1], buf.shape[2])


def _gathered_from_grad(weight, g):
    if weight == "f_w_up":
        return g
    return g.reshape(N_CHIPS, g.shape[0] // N_CHIPS, g.shape[1]).astype(BF16)


GATHER_SCHEDULE = {
    "a_in_main": [("ici", ["a_w_out"])],
    "a_conv": [("d2d", ["a_w_out"]), ("ici", ["f_w_down0"])],
    "a_ssd_prep": [("d2d", ["f_w_down0"]), ("ici_near", ["f_w_up0"])],
    "a_ssd": [("ici_far", ["f_w_up0"])],
    "a_gate": [("d2d", ["f_w_up0"]), ("ici", ["w_kv", "b_w_q", "b_w_o"])],
    "ffn0_up": [("d2d", ["w_kv", "b_w_q", "b_w_o"]), ("ici", ["f_w_down1"])],
    "ffn0_down": [("d2d", ["f_w_down1"])],
    "b_attn": [("ici", ["f_w_up1"])],
    "b_o": [("d2d", ["f_w_up1"])],
}
REDUCE_SCHEDULE = {
    "b_attn_bwd": [("all", ["f_w_down1", "f_w_up1", "b_w_o"])],
    "ffn0_conv_bwd": [("all", ["b_w_q", "w_kv", "f_w_down0"])],
    "a_ssd_bwd": [("all", ["f_w_up0", "a_w_out"])],
    "a_in_main_dx": [("near", ["a_w_in"])],
    "a_norm_bwd": [("far", ["a_w_in"])],
}
REDUCE_LAST = ("a_w_in",)
ICI_PEERS = {"ici": ALL_PEERS, "ici_near": NEAR_PEERS, "ici_far": FAR_PEERS,
             "all": ALL_PEERS, "near": NEAR_PEERS, "far": FAR_PEERS}
PAIR_SCHEDULE = {
    "b_o_dx": ["f_w_down1", "f_w_up1", "b_w_o"],
    "ffn0_down_dx": ["b_w_q", "w_kv", "f_w_down0"],
    "a_out_dx": ["f_w_up0", "a_w_out"],
    "a_in_dt_dx": ["a_w_in"],
}
SWAP_SCHEDULE = {"a_in_main_dw": ["f_w_down1", "f_w_up1", "b_w_o", "b_w_q", "w_kv", "f_w_down0", "f_w_up0", "a_w_out"]}


def _buffer_of(weight, layer):
    return weight if layer is None else f"{weight}{layer}"


class _Pipeline:
    def __init__(self, place, slots):
        self.place = place
        self.slots = dict(slots)
        self.running = []
        self.grads = {}
        self.theirs = {}
        self.partials = {}
        self.peers = {}
        self.reduced = {}

    def _collect(self):
        for step, buffers, table in self.running:
            table.update(zip(buffers, step.results))
        self.running = []

    @staticmethod
    def _splits(buffers):
        return [SPLIT.get(b, "rows") for b in buffers]

    def gather_now(self, name, buffers):
        step = _step_gather_full([self.slots[b] for b in buffers], self._splits(buffers))
        _run_steps(name, [step])
        self.slots.update(zip(buffers, step.results))

    def weight(self, name, layer=None):
        self._collect()
        return _weight_from_gathered(name, self.slots[_buffer_of(name, layer)])

    def grad(self, name, layer, g):
        self.grads[_buffer_of(name, layer)] = _gathered_from_grad(name, g)

    def steps(self, kernel):
        self._collect()
        steps = []
        for phase, buffers in GATHER_SCHEDULE.get(kernel, []):
            bufs, splits = [self.slots[b] for b in buffers], self._splits(buffers)
            step = (_step_gather_d2d(bufs, splits) if phase == "d2d"
                    else _step_gather_ici(bufs, splits, ICI_PEERS[phase]))
            self.running.append((step, buffers, self.slots))
            steps.append(step)
        buffers = PAIR_SCHEDULE.get(kernel)
        if buffers:
            step = _step_pair_exchange([self.grads[b] for b in buffers], self._splits(buffers))
            self.running.append((step, buffers, self.theirs))
            steps.append(step)
        for part, buffers in REDUCE_SCHEDULE.get(kernel, []):
            for b in buffers:
                if b not in self.partials:
                    self.partials[b] = _rs_pair_add("reduce_pair_add_" + b, self.place, self.grads[b], self.theirs[b],
                                                    SPLIT.get(b, "rows"))
            started = [self.peers[b] for b in buffers] if all(b in self.peers for b in buffers) else None
            step = _step_chip_exchange([self.partials[b] for b in buffers], ICI_PEERS[part], into=started)
            self.running.append((step, buffers, self.peers))
            steps.append(step)
        buffers = SWAP_SCHEDULE.get(kernel)
        if buffers:
            step = self._swap_step(buffers)
            self.running.append((step, buffers, self.reduced))
            steps.append(step)
        return steps

    def _swap_step(self, buffers):
        halves = [_rs_chip_add("reduce_chip_add_" + b, self.place, self.partials[b], self.peers[b], SPLIT.get(b, "rows"))
                  for b in buffers]
        return _step_pair_gather(halves, self._splits(buffers))

    def shard(self, buffer):
        self._collect()
        return self.reduced[buffer]

    def finish(self):
        self._collect()
        rest = [b for b, _, _ in BUFFERS if b not in self.reduced]
        step = self._swap_step(rest)
        _run_steps("reduce_pair_gather", [step])
        self.reduced.update(zip(rest, step.results))


def kernel(x, meta_tokens, a_norm_pre, a_w_in, a_conv_w, a_conv_b, a_dt_bias, a_a_log, a_d_skip, a_gate_norm, a_w_out, a_norm_post, kv_norm, w_kv, b_norm_pre, b_w_q, b_sinks, b_w_o, b_norm_post, f_norm_pre, f_w_up, f_conv_w, f_conv_b, f_w_down, f_norm_post, loss_target, m_meta_tokens, m_a_norm_pre, m_a_w_in, m_a_conv_w, m_a_conv_b, m_a_dt_bias, m_a_a_log, m_a_d_skip, m_a_gate_norm, m_a_w_out, m_a_norm_post, m_kv_norm, m_w_kv, m_b_norm_pre, m_b_w_q, m_b_sinks, m_b_w_o, m_b_norm_post, m_f_norm_pre, m_f_w_up, m_f_conv_w, m_f_conv_b, m_f_w_down, m_f_norm_post, v_meta_tokens, v_a_norm_pre, v_a_w_in, v_a_conv_w, v_a_conv_b, v_a_dt_bias, v_a_a_log, v_a_d_skip, v_a_gate_norm, v_a_w_out, v_a_norm_post, v_kv_norm, v_w_kv, v_b_norm_pre, v_b_w_q, v_b_sinks, v_b_w_o, v_b_norm_post, v_f_norm_pre, v_f_w_up, v_f_conv_w, v_f_conv_b, v_f_w_down, v_f_norm_post):
    given = dict(locals())
    w = {n: given[n] for n in WEIGHTS}
    mom = {n: given["m_" + n] for n in WEIGHTS}
    var = {n: given["v_" + n] for n in WEIGHTS}
    chip = 2 * lax.axis_index("x") + lax.axis_index("y")
    core = lax.axis_index("c")
    place = jnp.stack([chip, core]).astype(jnp.int32)

    small_all = _allgather_small("gather_small", _pack([w[n] for n in SMALL_SHARDED]))
    small_parts = _unpack(small_all, SMALL_SHARDED, _shard_shape)
    slots = {b: _cast_into_slot("cast_" + b, place, _local_shard(w, wn, layer), layer) for b, wn, layer in BUFFERS}
    pipeline = _Pipeline(place, slots)
    pipeline.gather_now("gather_first", ["a_w_in"])
    p = {}
    for n in SMALL:
        p[n] = _join_chips(n, small_parts[n]) if n in SMALL_SHARDED else w[n]
    p["a_conv_w"] = p["a_conv_w"][0]
    p["kv_norm"] = p["kv_norm"].reshape(1, D_MODEL)

    loss_local, grad_x, g = _local_step(x[0], loss_target[0], p, pipeline)

    small_sum = _allreduce_small("reduce_small", _pack([g[n].reshape(FULL_SHAPE[n]) for n in SMALL]
                                                       + [loss_local.reshape(1, 1)]))
    small_red = _unpack(small_sum, SMALL + ["loss"], lambda n: (1, 1) if n == "loss" else FULL_SHAPE[n])
    loss = small_red["loss"][0, 0]
    grads = {}
    for n in SMALL:
        if SHARD_AXIS[n] is None:
            grads[n] = small_red[n]
        else:
            grads[n] = lax.dynamic_index_in_dim(_split_chips(n, small_red[n]), chip, 0, keepdims=False)

    delta, new_m, new_v = {}, {}, {}
    for n in sorted(BIG, key=lambda name: name in REDUCE_LAST):
        shape = _shard_shape(n)
        if n in REDUCE_LAST:
            pipeline.finish()
        if n in TRANSPOSED:
            g2d = pipeline.shard(n)
            w2d, m2d, v2d = (arrays[n][0].T for arrays in (w, mom, var))
            back = lambda a: a.T.reshape(shape)
        else:
            g2d = (jnp.concatenate([pipeline.shard(n + "0"), pipeline.shard(n + "1")], axis=0)
                   if n in ("f_w_up", "f_w_down") else pipeline.shard(n))
            w2d, m2d, v2d = (_as2d(arrays[n]) for arrays in (w, mom, var))
            back = lambda a: a.reshape(shape)
        d, m2, v2 = _adamw("adamw_" + n, w2d, g2d, m2d, v2d, steps=pipeline.steps("adamw_" + n))
        grads[n], delta[n], new_m[n], new_v[n] = back(g2d), back(d), back(m2), back(v2)
    at_least_2d = lambda n: (1,) * (2 - len(_shard_shape(n))) + _shard_shape(n)
    outs = _adamw_small("adamw_small", *[[src[n].reshape(at_least_2d(n)) for n in SMALL] for src in (w, grads, mom, var)])
    for dst, arrays in zip((delta, new_m, new_v), outs):
        dst.update({n: a.reshape(_shard_shape(n)) for n, a in zip(SMALL, arrays)})

    return (loss, grad_x[None], *[grads[n].reshape(_shard_shape(n)) for n in WEIGHTS],
            *[delta[n] for n in WEIGHTS], *[new_m[n] for n in WEIGHTS], *[new_v[n] for n in WEIGHTS])
```

```python
import functools
import math

import jax
import jax.numpy as jnp
from jax import lax
from jax.experimental import pallas as pl
from jax.experimental.pallas import tpu as pltpu

F32, BF16 = jnp.float32, jnp.bfloat16
MESH = pl.DeviceIdType.MESH

D_MODEL = 1024
N_META = 16
CHUNK = 128
PAD_ROWS = CHUNK - N_META
D_INNER = 2048
D_STATE = 128
N_GROUPS = 4
HEADS_PER_GROUP = 8
SSM_HEADS = 32
HEAD_DIM = 64
D_BC = N_GROUPS * D_STATE
D_XBC = D_INNER + 2 * D_BC
D_MAIN = D_INNER + D_XBC
D_IN_PROJ = D_MAIN + SSM_HEADS
GROUP_W = HEADS_PER_GROUP * HEAD_DIM
SSM_CONV = 4
D_FF = 2816
FFN_CONV = 3
N_Q_HEADS = 16
N_KV_HEADS = 4
D_KV = 256
ATTN_SCALE = 1.0 / math.sqrt(HEAD_DIM)
RMS_EPS = 1e-6
NEG_INF = -1e30
LANES = 128
VMEM_LIMIT = 56 * 1024 * 1024

ADAM_LR, ADAM_B1, ADAM_B2, ADAM_EPS, ADAM_WD, ADAM_STEP = 0.001, 0.9, 0.999, 1e-08, 0.01, 10

N_CHIPS = 4


def _cparams(sem=None):
    return pltpu.CompilerParams(dimension_semantics=sem, vmem_limit_bytes=VMEM_LIMIT)


def _tile(n, cands=(512, 256, 128)):
    for t in cands:
        if n % t == 0:
            return t
    return n


def _row_tile(rows, width):
    for t in (544, 272):
        if rows % t == 0 and t * width * 4 <= (3 << 20):
            return t
    return 128


def _rows_mask(i, tm):
    rows = i * tm + lax.broadcasted_iota(jnp.int32, (tm, 1), 0)
    return rows >= PAD_ROWS


def _dot(a, b):
    return jnp.dot(a, b, preferred_element_type=F32)


def _dot_nt(a, b):
    return lax.dot_general(a, b, (((1,), (1,)), ((), ())), preferred_element_type=F32)


def _dot_tn(a, b):
    return lax.dot_general(a, b, (((0,), (0,)), ((), ())), preferred_element_type=F32)


def _sigmoid(x):
    return 1.0 / (1.0 + jnp.exp(-x))


def _place():
    return lax.axis_index("x"), lax.axis_index("y"), lax.axis_index("c")


def _other_chips(x, y):
    return [(1 - x, y), (x, 1 - y), (1 - x, 1 - y)]


class _Step:
    def __init__(self, ins, outs, aliases, n_sems, start, finish):
        self.ins, self.outs, self.aliases, self.n_sems = list(ins), list(outs), dict(aliases), n_sems
        self.start, self.finish = start, finish
        self.results = None


def _like(a):
    return jax.ShapeDtypeStruct(a.shape, a.dtype)


def _remote(src, dst, send_sems, recv_sems, k, device):
    return pltpu.make_async_remote_copy(src, dst, send_sems.at[k], recv_sems.at[k], device_id=device, device_id_type=MESH)


def _half(ref, split, which, lead=()):
    if split == "rows":
        hr = ref.shape[-2] // 2
        return ref.at[lead + (pl.ds(which * hr, hr),)]
    hc = ref.shape[-1] // 2
    return ref.at[lead + (slice(None), pl.ds(which * hc, hc))]


def _splits(bufs, splits):
    return list(splits) if splits is not None else ["rows"] * len(bufs)


ALL_PEERS = (0, 1, 2)
NEAR_PEERS = (0, 1)
FAR_PEERS = (2,)


def _step_gather_ici(bufs, splits=None, peers=ALL_PEERS):
    splits = _splits(bufs, splits)

    def copies(outs, send_sems, recv_sems, received):
        x, y, c = _place()
        me = 2 * x + y
        for k, o in enumerate(outs):
            for j, (cx, cy) in enumerate(_other_chips(x, y)):
                if j in peers:
                    part = _half(o, splits[k], c, (2 * cx + cy if received else me,))
                    yield _remote(part, part, send_sems, recv_sems, 3 * k + j, (cx, cy, c))

    def start(ins, outs, send_sems, recv_sems):
        for cp in copies(outs, send_sems, recv_sems, False):
            cp.start()

    def finish(ins, outs, send_sems, recv_sems):
        for cp in copies(outs, send_sems, recv_sems, True):
            cp.wait_recv()
        for cp in copies(outs, send_sems, recv_sems, False):
            cp.wait_send()

    return _Step(bufs, [_like(b) for b in bufs], {k: k for k in range(len(bufs))}, 3 * len(bufs), start, finish)


def _step_gather_d2d(bufs, splits=None):
    splits = _splits(bufs, splits)

    def copies(outs, send_sems, recv_sems, received):
        x, y, c = _place()
        for k, o in enumerate(outs):
            for j, (cx, cy) in enumerate(_other_chips(x, y)):
                part = _half(o, splits[k], 1 - c if received else c, (2 * cx + cy,))
                yield _remote(part, part, send_sems, recv_sems, 3 * k + j, (x, y, 1 - c))

    def start(ins, outs, send_sems, recv_sems):
        for cp in copies(outs, send_sems, recv_sems, False):
            cp.start()

    def finish(ins, outs, send_sems, recv_sems):
        for cp in copies(outs, send_sems, recv_sems, True):
            cp.wait_recv()
        for cp in copies(outs, send_sems, recv_sems, False):
            cp.wait_send()

    return _Step(bufs, [_like(b) for b in bufs], {k: k for k in range(len(bufs))}, 3 * len(bufs), start, finish)


def _step_gather_full(bufs, splits=None):
    n = len(bufs)
    splits = _splits(bufs, splits)

    def ici(outs, send_sems, recv_sems, received):
        x, y, c = _place()
        me = 2 * x + y
        for k, o in enumerate(outs):
            for j, (cx, cy) in enumerate(_other_chips(x, y)):
                part = _half(o, splits[k], c, (2 * cx + cy if received else me,))
                yield _remote(part, part, send_sems, recv_sems, 3 * k + j, (cx, cy, c))

    def d2d(outs, send_sems, recv_sems, received):
        x, y, c = _place()
        for k, o in enumerate(outs):
            for j, (cx, cy) in enumerate(_other_chips(x, y)):
                part = _half(o, splits[k], 1 - c if received else c, (2 * cx + cy,))
                yield _remote(part, part, send_sems, recv_sems, 3 * n + 3 * k + j, (x, y, 1 - c))

    def start(ins, outs, send_sems, recv_sems):
        for cp in ici(outs, send_sems, recv_sems, False):
            cp.start()

    def finish(ins, outs, send_sems, recv_sems):
        for arrived, onward in zip(ici(outs, send_sems, recv_sems, True), d2d(outs, send_sems, recv_sems, False)):
            arrived.wait_recv()
            onward.start()
        for cp in d2d(outs, send_sems, recv_sems, True):
            cp.wait_recv()
        for cp in ici(outs, send_sems, recv_sems, False):
            cp.wait_send()
        for cp in d2d(outs, send_sems, recv_sems, False):
            cp.wait_send()

    return _Step(bufs, [_like(b) for b in bufs], {k: k for k in range(n)}, 6 * n, start, finish)


def _half_shape(shape, split):
    return shape[:-2] + ((shape[-2] // 2, shape[-1]) if split == "rows" else (shape[-2], shape[-1] // 2))


def _step_pair_exchange(grads, splits=None):
    splits = _splits(grads, splits)

    def copies(ins, outs, send_sems, recv_sems):
        x, y, c = _place()
        for k, (g, o) in enumerate(zip(ins, outs)):
            yield _remote(_half(g, splits[k], 1 - c, (slice(None),)), o, send_sems, recv_sems, k, (x, y, 1 - c))

    def start(ins, outs, send_sems, recv_sems):
        for cp in copies(ins, outs, send_sems, recv_sems):
            cp.start()

    def finish(ins, outs, send_sems, recv_sems):
        for cp in copies(ins, outs, send_sems, recv_sems):
            cp.wait()

    outs = [jax.ShapeDtypeStruct(_half_shape(g.shape, s), g.dtype) for g, s in zip(grads, splits)]
    return _Step(grads, outs, {}, len(grads), start, finish)


def _step_chip_exchange(partials, peers=ALL_PEERS, into=None):
    n = len(partials)

    def copies(ins, outs, send_sems, recv_sems):
        x, y, c = _place()
        for k, (q, o) in enumerate(zip(ins[:n], outs)):
            for j, (cx, cy) in enumerate(_other_chips(x, y)):
                if j in peers:
                    yield _remote(q.at[2 * cx + cy], o.at[j], send_sems, recv_sems, 3 * k + j, (cx, cy, c))

    def start(ins, outs, send_sems, recv_sems):
        for cp in copies(ins, outs, send_sems, recv_sems):
            cp.start()

    def finish(ins, outs, send_sems, recv_sems):
        for cp in copies(ins, outs, send_sems, recv_sems):
            cp.wait()

    outs = [jax.ShapeDtypeStruct((3,) + q.shape[1:], q.dtype) for q in partials]
    if into is None:
        return _Step(partials, outs, {}, 3 * n, start, finish)
    return _Step(list(partials) + list(into), outs, {n + k: k for k in range(n)}, 3 * n, start, finish)


def _step_pair_gather(shards, splits=None):
    splits = _splits(shards, splits)

    def copies(outs, send_sems, recv_sems, received):
        x, y, c = _place()
        for k, o in enumerate(outs):
            part = _half(o, splits[k], 1 - c if received else c)
            yield _remote(part, part, send_sems, recv_sems, k, (x, y, 1 - c))

    def start(ins, outs, send_sems, recv_sems):
        for cp in copies(outs, send_sems, recv_sems, False):
            cp.start()

    def finish(ins, outs, send_sems, recv_sems):
        for cp in copies(outs, send_sems, recv_sems, True):
            cp.wait_recv()
        for cp in copies(outs, send_sems, recv_sems, False):
            cp.wait_send()

    return _Step(shards, [_like(s) for s in shards], {k: k for k in range(len(shards))}, len(shards), start, finish)


def _call(body, *, name, out_shape, grid, in_specs, out_specs, operands, scratch_shapes=(), semantics=None, steps=()):
    single = not isinstance(out_shape, (tuple, list))
    out_shapes = [out_shape] if single else list(out_shape)
    out_spec_list = [out_specs] if single else list(out_specs)
    steps = list(steps)
    if not steps:
        res = pl.pallas_call(body, name=name, out_shape=out_shapes, grid=grid, in_specs=list(in_specs),
                             out_specs=out_spec_list, scratch_shapes=list(scratch_shapes),
                             compiler_params=_cparams(semantics))(*operands)
        return res[0] if single else res
    n_in, n_out, n_scr = len(operands), len(out_shapes), len(scratch_shapes)
    x_in = [a for s in steps for a in s.ins]
    x_out = [o for s in steps for o in s.outs]
    aliases, in_off, out_off = {}, 0, 0
    for s in steps:
        for i, o in s.aliases.items():
            aliases[n_in + in_off + i] = n_out + out_off + o
        in_off += len(s.ins)
        out_off += len(s.outs)
    sems = []
    for s in steps:
        sems += [pltpu.SemaphoreType.DMA((s.n_sems,)), pltpu.SemaphoreType.DMA((s.n_sems,))]
    any_spec = pl.BlockSpec(memory_space=pl.ANY)

    def carried(*refs):
        pos = 0
        ins = refs[pos:pos + n_in]; pos += n_in
        xi = refs[pos:pos + len(x_in)]; pos += len(x_in)
        outs = refs[pos:pos + n_out]; pos += n_out
        xo = refs[pos:pos + len(x_out)]; pos += len(x_out)
        scr = refs[pos:pos + n_scr]; pos += n_scr
        sem_refs = refs[pos:]

        def each(action):
            i0 = o0 = 0
            for k, s in enumerate(steps):
                getattr(s, action)(xi[i0:i0 + len(s.ins)], xo[o0:o0 + len(s.outs)], sem_refs[2 * k], sem_refs[2 * k + 1])
                i0 += len(s.ins)
                o0 += len(s.outs)

        if grid:
            first = functools.reduce(jnp.logical_and, [pl.program_id(d) == 0 for d in range(len(grid))])
            last = functools.reduce(jnp.logical_and, [pl.program_id(d) == grid[d] - 1 for d in range(len(grid))])
            pl.when(first)(lambda: each("start"))
            body(*ins, *outs, *scr)
            pl.when(last)(lambda: each("finish"))
        else:
            each("start")
            body(*ins, *outs, *scr)
            each("finish")

    res = pl.pallas_call(
        carried, name=name, out_shape=out_shapes + x_out, grid=grid,
        in_specs=list(in_specs) + [any_spec] * len(x_in), out_specs=out_spec_list + [any_spec] * len(x_out),
        scratch_shapes=list(scratch_shapes) + sems, input_output_aliases=aliases,
        compiler_params=_cparams(None if semantics is None else ("arbitrary",) * len(grid)),
    )(*operands, *x_in)
    o0 = n_out
    for s in steps:
        s.results = list(res[o0:o0 + len(s.outs)])
        o0 += len(s.outs)
    return res[0] if single else tuple(res[:n_out])


def _run_steps(name, steps):
    _call(lambda: None, name=name, out_shape=[], grid=(), in_specs=[], out_specs=[], operands=[], steps=steps)
    return [s.results for s in steps]


def _mm(name, a, b, mode, out_dtype=F32, acc=None, b_colblock=0, k_rows=None, out_rows=None, steps=()):
    resident_bytes = 8 << 20
    if mode == "nn":
        m, k = a.shape
        n = b.shape[1]
        tm = m
        while tm * k * 2 > resident_bytes and tm % 32 == 0:
            tm //= 2
        tn = _tile(n)
        grid = (m // tm, n // tn)
        in_specs = [pl.BlockSpec((tm, k), lambda i, j: (i, 0)), pl.BlockSpec((k, tn), lambda i, j: (0, j))]
        out_shape, out_block = (m, n), (tm, tn)
    elif mode == "nt":
        m, n = a.shape
        k = k_rows or b.shape[0]
        tm = m
        while tm * n * 2 > resident_bytes and tm % 32 == 0:
            tm //= 2
        tk = _tile(k)
        grid = (m // tm, k // tk)
        in_specs = [pl.BlockSpec((tm, n), lambda i, j: (i, 0)), pl.BlockSpec((tk, n), lambda i, j: (j, b_colblock))]
        out_shape, out_block = (m, k), (tm, tk)
    else:
        m, k = a.shape
        n = b.shape[1]
        tk, tn = _tile(k), (n if m * n * 2 <= resident_bytes else _tile(n))
        grid = (k // tk, n // tn)
        in_specs = [pl.BlockSpec((m, tk), lambda i, j: (0, i)), pl.BlockSpec((m, tn), lambda i, j: (0, j))]
        out_shape, out_block = (out_rows or k, n), (tk, tn)
    out_spec = pl.BlockSpec(out_block, lambda i, j: (i, j))
    has_acc = acc is not None

    def body(*refs):
        a_ref, b_ref = refs[0], refs[1]
        o_ref = refs[-1]
        av, bv = a_ref[...], b_ref[...]
        if mode == "nn":
            r = _dot(av, bv)
        elif mode == "nt":
            r = _dot_nt(av, bv)
        else:
            r = _dot_tn(av, bv)
        if has_acc:
            r = r + refs[2][...]
        o_ref[...] = r.astype(o_ref.dtype)

    operands = [a, b]
    if has_acc:
        in_specs = in_specs + [out_spec]
        operands.append(acc)
    return _call(body, name=name, out_shape=jax.ShapeDtypeStruct(out_shape, out_dtype), grid=grid, in_specs=in_specs,
                 out_specs=out_spec, operands=operands, semantics=("parallel", "parallel"), steps=steps)


def _tn_rows_into(name, a, b, into, row0, nrows):
    m, k = a.shape
    n = b.shape[1]

    def body(a_ref, b_ref, into_ref, o_ref):
        o_ref[...] = _dot_tn(a_ref[...], b_ref[...])[0:nrows].astype(o_ref.dtype)

    return pl.pallas_call(
        body, name=name, out_shape=jax.ShapeDtypeStruct(into.shape, into.dtype), grid=(1,),
        in_specs=[pl.BlockSpec((m, k), lambda i: (0, 0)), pl.BlockSpec((m, n), lambda i: (0, 0)),
                  pl.BlockSpec(memory_space=pl.ANY)],
        out_specs=pl.BlockSpec((nrows, n), lambda i: (row0 // nrows, 0)),
        input_output_aliases={2: 0}, compiler_params=_cparams(("arbitrary",)),
    )(a, b, into)


def _rms_fwd(name, h, w):
    rows, width = h.shape
    tm = _row_tile(rows, width)

    def body(h_ref, w_ref, o_ref):
        x = h_ref[...]
        r = lax.rsqrt(jnp.mean(x * x, axis=-1, keepdims=True) + RMS_EPS)
        o_ref[...] = (x * r * w_ref[...]).astype(BF16)

    return pl.pallas_call(
        body, name=name, out_shape=jax.ShapeDtypeStruct((rows, width), BF16), grid=(rows // tm,),
        in_specs=[pl.BlockSpec((tm, width), lambda i: (i, 0)), pl.BlockSpec((1, width), lambda i: (0, 0))],
        out_specs=pl.BlockSpec((tm, width), lambda i: (i, 0)), compiler_params=_cparams(("parallel",)),
    )(h, w)


def _resid_norm_fwd(name, h, pre, w, next_norms=()):
    rows, width = h.shape
    tm = _row_tile(rows, width)
    n_next = len(next_norms)

    def body(*refs):
        h_ref, p_ref, w_ref = refs[:3]
        v_refs = refs[3:3 + n_next]
        o_ref = refs[3 + n_next]
        n_refs = refs[4 + n_next:]
        p = p_ref[...]
        r = lax.rsqrt(jnp.mean(p * p, axis=-1, keepdims=True) + RMS_EPS)
        x = h_ref[...] + jnp.where(_rows_mask(pl.program_id(0), tm), p * r * w_ref[...], 0.0)
        o_ref[...] = x
        if n_next:
            rx = lax.rsqrt(jnp.mean(x * x, axis=-1, keepdims=True) + RMS_EPS)
            for v_ref, n_ref in zip(v_refs, n_refs):
                n_ref[...] = (x * rx * v_ref[...]).astype(BF16)

    row_spec = pl.BlockSpec((tm, width), lambda i: (i, 0))
    vec_spec = pl.BlockSpec((1, width), lambda i: (0, 0))
    outs = pl.pallas_call(
        body, name=name,
        out_shape=[jax.ShapeDtypeStruct((rows, width), F32)] + [jax.ShapeDtypeStruct((rows, width), BF16)] * n_next,
        grid=(rows // tm,), in_specs=[row_spec, row_spec, vec_spec] + [vec_spec] * n_next,
        out_specs=[row_spec] * (1 + n_next), compiler_params=_cparams(("parallel",)),
    )(h, pre, w, *next_norms)
    return outs[0], list(outs[1:])


def _resid_norm_loss(name, h, pre, w, target):
    rows, width = h.shape

    def body(h_ref, p_ref, w_ref, t_ref, dh_ref, loss_ref, dp_ref, dw_ref):
        i = pl.program_id(0)
        p = p_ref[...]
        r = lax.rsqrt(jnp.mean(p * p, axis=-1, keepdims=True) + RMS_EPS)
        x = h_ref[...] + p * r * w_ref[...]
        real = (i + jnp.zeros((CHUNK, 1), jnp.int32)) >= 1
        diff = jnp.where(real, x - t_ref[...], 0.0)
        dh = diff * (1.0 / D_MODEL)
        dh_ref[...] = dh
        dp, dw_rows = _rms_bwd(dh, p, w_ref[...])
        dp_ref[...] = dp.astype(BF16)

        @pl.when(i == 0)
        def _():
            loss_ref[...] = jnp.zeros_like(loss_ref)
            dw_ref[...] = jnp.zeros_like(dw_ref)

        loss_ref[...] += jnp.sum(diff * diff) * (0.5 / D_MODEL)
        dw_ref[...] += jnp.sum(dw_rows, axis=0, keepdims=True)

    blk = pl.BlockSpec((CHUNK, width), lambda i: (i, 0))
    vec_spec = pl.BlockSpec((1, width), lambda i: (0, 0))
    return pl.pallas_call(
        body, name=name,
        out_shape=(jax.ShapeDtypeStruct((rows, width), F32), jax.ShapeDtypeStruct((1, LANES), F32),
                   jax.ShapeDtypeStruct((rows, width), BF16), jax.ShapeDtypeStruct((1, width), F32)),
        grid=(rows // CHUNK,),
        in_specs=[blk, blk, vec_spec, pl.BlockSpec((CHUNK, width), lambda i: (jnp.maximum(i - 1, 0), 0))],
        out_specs=(blk, pl.BlockSpec((1, LANES), lambda i: (0, 0)), blk, vec_spec),
        compiler_params=_cparams(("arbitrary",)),
    )(h, pre, w, target)


def _rms_bwd(dy, x, w):
    r = lax.rsqrt(jnp.mean(x * x, axis=-1, keepdims=True) + RMS_EPS)
    xhat = x * r
    dxhat = dy * w
    return r * (dxhat - xhat * jnp.mean(dxhat * xhat, axis=-1, keepdims=True)), dy * xhat


def _norm_bwd_add(name, dh, dhn, h, w, then=None, split_first_block=False, steps=()):
    rows, width = dh.shape
    tm = CHUNK if split_first_block else _row_tile(rows, width)
    fused = then is not None
    assert not (fused and split_first_block)

    def body(*refs):
        dh_ref, dhn_ref, h_ref, w_ref = refs[:4]
        o_ref, dw_ref = refs[6:8] if fused else refs[-2:]
        i = pl.program_id(0)
        valid = _rows_mask(i, tm)
        dx, dw_rows = _rms_bwd(dhn_ref[...], h_ref[...], w_ref[...])
        dh_new = dh_ref[...] + jnp.where(valid, dx, 0.0)
        if split_first_block:
            first_ref = refs[4]

            @pl.when(i == 0)
            def _():
                first_ref[...] = dh_new

            @pl.when(i > 0)
            def _():
                o_ref[...] = dh_new
        else:
            o_ref[...] = dh_new

        @pl.when(i == 0)
        def _():
            dw_ref[...] = jnp.zeros_like(dw_ref)

        dw_ref[...] += jnp.sum(dw_rows, axis=0, keepdims=True)
        if fused:
            p_ref, wp_ref, dp_ref, dwp_ref = refs[4], refs[5], refs[8], refs[9]
            dp, dwp_rows = _rms_bwd(jnp.where(valid, dh_new, 0.0), p_ref[...], wp_ref[...])
            dp_ref[...] = dp.astype(BF16)

            @pl.when(i == 0)
            def _():
                dwp_ref[...] = jnp.zeros_like(dwp_ref)

            dwp_ref[...] += jnp.sum(dwp_rows, axis=0, keepdims=True)

    row_spec = pl.BlockSpec((tm, width), lambda i: (i, 0))
    vec_spec = pl.BlockSpec((1, width), lambda i: (0, 0))
    row_f32, vec_f32 = jax.ShapeDtypeStruct((rows, width), F32), jax.ShapeDtypeStruct((1, width), F32)
    in_specs, operands = [row_spec, row_spec, row_spec, vec_spec], [dh, dhn, h, w]
    out_shape, out_specs = [row_f32, vec_f32], [row_spec, vec_spec]
    if split_first_block:
        out_shape = [jax.ShapeDtypeStruct((tm, width), F32), jax.ShapeDtypeStruct((rows - tm, width), F32), vec_f32]
        out_specs = [pl.BlockSpec((tm, width), lambda i: (0, 0)),
                     pl.BlockSpec((tm, width), lambda i: (jnp.maximum(i - 1, 0), 0)), vec_spec]
    if fused:
        in_specs += [row_spec, vec_spec]
        operands += list(then)
        out_shape += [jax.ShapeDtypeStruct((rows, width), BF16), vec_f32]
        out_specs += [row_spec, vec_spec]
    return _call(body, name=name, out_shape=out_shape, grid=(rows // tm,), in_specs=in_specs, out_specs=out_specs,
                 operands=operands, semantics=("arbitrary",), steps=steps)


def _shift_down(x, s, rows):
    return pltpu.roll(x, s, 0) if s else x


def _shift_up(x, s, rows):
    return pltpu.roll(x, rows - s, 0) if s else x


def _conv4_fwd(name, zx, cw, cb, steps=()):
    rows = zx.shape[0]
    off = D_INNER // LANES

    def body(x_ref, w_ref, b_ref, o_ref):
        x = x_ref[...]
        acc = b_ref[...] + w_ref[pl.ds(SSM_CONV - 1, 1), :] * x
        for s in range(1, SSM_CONV):
            acc = acc + w_ref[pl.ds(SSM_CONV - 1 - s, 1), :] * _shift_down(x, s, rows)
        valid = lax.broadcasted_iota(jnp.int32, (rows, 1), 0) >= PAD_ROWS
        o_ref[...] = jnp.where(valid, acc * _sigmoid(acc), 0.0)

    return _call(
        body, name=name, out_shape=jax.ShapeDtypeStruct((rows, D_XBC), F32), grid=(D_XBC // LANES,),
        in_specs=[pl.BlockSpec((rows, LANES), lambda j: (0, j + off)),
                  pl.BlockSpec((SSM_CONV, LANES), lambda j: (0, j)),
                  pl.BlockSpec((1, LANES), lambda j: (0, j))],
        out_specs=pl.BlockSpec((rows, LANES), lambda j: (0, j)), operands=[zx, cw, cb],
        semantics=("parallel",), steps=steps)


def _conv4_bwd(name, zx, dout, cw, cb, into):
    rows, width = dout.shape
    zoff = D_INNER // LANES

    def body(x_ref, d_ref, w_ref, b_ref, into_ref, dx_ref, dw_ref, db_ref):
        x = x_ref[...]
        shifted = [_shift_down(x, s, rows) for s in range(SSM_CONV)]
        acc = b_ref[...]
        for s in range(SSM_CONV):
            acc = acc + w_ref[pl.ds(SSM_CONV - 1 - s, 1), :] * shifted[s]
        sig = _sigmoid(acc)
        valid = lax.broadcasted_iota(jnp.int32, (rows, 1), 0) >= PAD_ROWS
        dpre = jnp.where(valid, d_ref[...] * sig * (1.0 + acc * (1.0 - sig)), 0.0)
        dx = w_ref[pl.ds(SSM_CONV - 1, 1), :] * dpre
        for s in range(1, SSM_CONV):
            dx = dx + w_ref[pl.ds(SSM_CONV - 1 - s, 1), :] * _shift_up(dpre, s, rows)
        dx_ref[...] = dx.astype(BF16)
        for s in range(SSM_CONV):
            dw_ref[pl.ds(SSM_CONV - 1 - s, 1), :] = jnp.sum(dpre * shifted[s], axis=0, keepdims=True)
        db_ref[...] = jnp.sum(dpre, axis=0, keepdims=True)

    return pl.pallas_call(
        body, name=name,
        out_shape=(jax.ShapeDtypeStruct(into.shape, BF16), jax.ShapeDtypeStruct((SSM_CONV, width), F32),
                   jax.ShapeDtypeStruct((1, width), F32)),
        grid=(width // LANES,),
        in_specs=[pl.BlockSpec((rows, LANES), lambda j: (0, j + zoff)),
                  pl.BlockSpec((rows, LANES), lambda j: (0, j)),
                  pl.BlockSpec((SSM_CONV, LANES), lambda j: (0, j)),
                  pl.BlockSpec((1, LANES), lambda j: (0, j)),
                  pl.BlockSpec(memory_space=pl.ANY)],
        out_specs=(pl.BlockSpec((rows, LANES), lambda j: (0, j + zoff)),
                   pl.BlockSpec((SSM_CONV, LANES), lambda j: (0, j)),
                   pl.BlockSpec((1, LANES), lambda j: (0, j))),
        input_output_aliases={4: 0}, compiler_params=_cparams(("parallel",)),
    )(zx, dout, cw, cb, into)


FFN_TILE = 2 * LANES


def _ffn_up_conv(name, hn, w_up, cw, cb, steps=()):
    rows, k = hn.shape
    chip_blocks = w_up.shape[2] // LANES
    half_blocks = D_FF // LANES
    nt = D_FF // FFN_TILE

    def weight_block(offset):
        return pl.BlockSpec((None, k, LANES), lambda j: ((2 * j + offset) // chip_blocks, 0, (2 * j + offset) % chip_blocks))

    def body(a_ref, g0, g1, v0, v1, wg_ref, wv_ref, bg_ref, bv_ref, upg_ref, upv_ref, act_ref):
        a = a_ref[...]
        g = _dot(a, jnp.concatenate([g0[...], g1[...]], axis=1))
        v = _dot(a, jnp.concatenate([v0[...], v1[...]], axis=1))
        upg_ref[...] = g
        upv_ref[...] = v
        ug, uv = bg_ref[...], bv_ref[...]
        for s in range(FFN_CONV):
            ug = ug + wg_ref[pl.ds(FFN_CONV - 1 - s, 1), :] * _shift_down(g, s, rows)
            uv = uv + wv_ref[pl.ds(FFN_CONV - 1 - s, 1), :] * _shift_down(v, s, rows)
        act_ref[...] = (ug * _sigmoid(ug) * uv).astype(BF16)

    col = pl.BlockSpec((rows, FFN_TILE), lambda j: (0, j))
    wsp = lambda shift: pl.BlockSpec((FFN_CONV, FFN_TILE), lambda j: (0, j + shift))
    bsp = lambda shift: pl.BlockSpec((1, FFN_TILE), lambda j: (0, j + shift))
    half = jax.ShapeDtypeStruct((rows, D_FF), F32)
    return _call(
        body, name=name, out_shape=(half, half, jax.ShapeDtypeStruct((rows, D_FF), BF16)), grid=(nt,),
        in_specs=[pl.BlockSpec((rows, k), lambda j: (0, 0)), weight_block(0), weight_block(1),
                  weight_block(half_blocks), weight_block(half_blocks + 1), wsp(0), wsp(nt), bsp(0), bsp(nt)],
        out_specs=(col, col, col), operands=[hn, w_up, w_up, w_up, w_up, cw, cw, cb, cb],
        semantics=("parallel",), steps=steps)


def _ffn_conv_bwd(name, up_g, up_v, dact, cw, cb, hn, w_up, steps=()):
    rows, k = hn.shape
    chip_blocks = w_up.shape[2] // LANES
    nt = D_FF // LANES

    def weight_block(shift):
        return pl.BlockSpec((None, k, LANES), lambda j: ((j + shift) // chip_blocks, 0, (j + shift) % chip_blocks))

    def body(g_ref, v_ref, d_ref, wg_ref, wv_ref, bg_ref, bv_ref, upg_ref, upv_ref, hn_ref,
             dwg_ref, dwv_ref, dbg_ref, dbv_ref, dhn_ref, dup_ref, acc, hn_scr, hnt_scr, dup_scr, sems):
        j = pl.program_id(0)
        hn_copy = pltpu.make_async_copy(hn_ref, hn_scr, sems.at[0])
        dhn_copy = pltpu.make_async_copy(acc, dhn_ref, sems.at[0])

        def dup_copy(step, half):
            block, slot = step + half * nt, 2 * (step % 2) + half
            cols = pl.ds(pl.multiple_of((block % chip_blocks) * LANES, LANES), LANES)
            return pltpu.make_async_copy(dup_scr.at[slot], dup_ref.at[block // chip_blocks, :, cols], sems.at[1 + slot])

        @pl.when(j == 0)
        def _():
            hn_copy.start()
            acc[...] = jnp.zeros_like(acc)
            hn_copy.wait()
            for r in range(0, rows, LANES):
                hnt_scr[:, r:r + LANES] = hn_scr[r:r + LANES, :].T

        @pl.when(j >= 2)
        def _():
            dup_copy(j - 2, 0).wait()
            dup_copy(j - 2, 1).wait()

        g, v = g_ref[...], v_ref[...]
        gs = [_shift_down(g, s, rows) for s in range(FFN_CONV)]
        vs = [_shift_down(v, s, rows) for s in range(FFN_CONV)]
        ug, uv = bg_ref[...], bv_ref[...]
        for s in range(FFN_CONV):
            ug = ug + wg_ref[pl.ds(FFN_CONV - 1 - s, 1), :] * gs[s]
            uv = uv + wv_ref[pl.ds(FFN_CONV - 1 - s, 1), :] * vs[s]
        sig = _sigmoid(ug)
        dsig = d_ref[...] * sig
        dup = []
        for dpre, src, w_ref, dw_ref, db_ref in (
                (dsig * uv * (1.0 + ug * (1.0 - sig)), gs, wg_ref, dwg_ref, dbg_ref),
                (dsig * ug, vs, wv_ref, dwv_ref, dbv_ref)):
            dx = w_ref[pl.ds(FFN_CONV - 1, 1), :] * dpre
            for s in range(1, FFN_CONV):
                dx = dx + w_ref[pl.ds(FFN_CONV - 1 - s, 1), :] * _shift_up(dpre, s, rows)
            dup.append(dx.astype(BF16))
            for s in range(FFN_CONV):
                dw_ref[pl.ds(FFN_CONV - 1 - s, 1), :] = jnp.sum(dpre * src[s], axis=0, keepdims=True)
            db_ref[...] = jnp.sum(dpre, axis=0, keepdims=True)
        dup = jnp.concatenate(dup, axis=1)
        acc[...] += _dot_nt(dup, jnp.concatenate([upg_ref[...], upv_ref[...]], axis=1))
        dw = _dot(hnt_scr[...], dup)
        slot = 2 * (j % 2)
        dup_scr[slot] = dw[:, :LANES].astype(BF16)
        dup_scr[slot + 1] = dw[:, LANES:].astype(BF16)
        dup_copy(j, 0).start()
        dup_copy(j, 1).start()

        @pl.when(j == nt - 1)
        def _():
            dhn_copy.start()
            for step in (j - 1, j):
                dup_copy(step, 0).wait()
                dup_copy(step, 1).wait()
            dhn_copy.wait()

    col = pl.BlockSpec((rows, LANES), lambda j: (0, j))
    wsp = lambda shift: pl.BlockSpec((FFN_CONV, LANES), lambda j: (0, j + shift))
    bsp = lambda shift: pl.BlockSpec((1, LANES), lambda j: (0, j + shift))
    any_spec = pl.BlockSpec(memory_space=pl.ANY)
    dw_shape = jax.ShapeDtypeStruct((FFN_CONV, D_FF), F32)
    db_shape = jax.ShapeDtypeStruct((1, D_FF), F32)
    return _call(
        body, name=name, grid=(nt,),
        out_shape=(dw_shape, dw_shape, db_shape, db_shape, jax.ShapeDtypeStruct((rows, k), F32),
                   jax.ShapeDtypeStruct(w_up.shape, BF16)),
        in_specs=[col, col, col, wsp(0), wsp(nt), bsp(0), bsp(nt), weight_block(0), weight_block(nt), any_spec],
        out_specs=(wsp(0), wsp(0), bsp(0), bsp(0), any_spec, any_spec),
        operands=[up_g, up_v, dact, cw, cw, cb, cb, w_up, w_up, hn],
        scratch_shapes=[pltpu.VMEM((rows, k), F32), pltpu.VMEM((rows, k), BF16), pltpu.VMEM((k, rows), BF16),
                        pltpu.VMEM((4, k, LANES), BF16), pltpu.SemaphoreType.DMA((5,))],
        semantics=("arbitrary",), steps=steps)


def _dt_fwd(name, dtr, bias):
    rows = dtr.shape[0]
    tm = _row_tile(rows, LANES)

    def body(d_ref, b_ref, o_ref):
        v = d_ref[...] + b_ref[...]
        sp = jnp.maximum(v, 0.0) + jnp.log1p(jnp.exp(-jnp.abs(v)))
        lane = lax.broadcasted_iota(jnp.int32, (tm, LANES), 1)
        ok = _rows_mask(pl.program_id(0), tm) & (lane < SSM_HEADS)
        o_ref[...] = jnp.where(ok, sp, 0.0)

    return pl.pallas_call(
        body, name=name, out_shape=jax.ShapeDtypeStruct((rows, LANES), F32), grid=(rows // tm,),
        in_specs=[pl.BlockSpec((tm, LANES), lambda i: (i, 0)), pl.BlockSpec((1, LANES), lambda i: (0, 0))],
        out_specs=pl.BlockSpec((tm, LANES), lambda i: (i, 0)), compiler_params=_cparams(("parallel",)),
    )(dtr, bias)


def _dt_bwd(name, ddt, dtr, bias):
    rows = dtr.shape[0]
    tm = _row_tile(rows, LANES)

    def body(g_ref, d_ref, b_ref, o_ref, db_ref):
        i = pl.program_id(0)
        lane = lax.broadcasted_iota(jnp.int32, (tm, LANES), 1)
        ok = _rows_mask(i, tm) & (lane < SSM_HEADS)
        dv = jnp.where(ok, g_ref[...] * _sigmoid(d_ref[...] + b_ref[...]), 0.0)
        o_ref[...] = dv.astype(BF16)

        @pl.when(i == 0)
        def _():
            db_ref[...] = jnp.zeros_like(db_ref)

        db_ref[...] += jnp.sum(dv, axis=0, keepdims=True)

    row_spec = pl.BlockSpec((tm, LANES), lambda i: (i, 0))
    vec_spec = pl.BlockSpec((1, LANES), lambda i: (0, 0))
    return pl.pallas_call(
        body, name=name,
        out_shape=(jax.ShapeDtypeStruct((rows, LANES), BF16), jax.ShapeDtypeStruct((1, LANES), F32)),
        grid=(rows // tm,), in_specs=[row_spec, row_spec, vec_spec], out_specs=(row_spec, vec_spec),
        compiler_params=_cparams(("arbitrary",)),
    )(ddt, dtr, bias)


def _gate_fwd(name, y, zx, w, steps=()):
    rows = y.shape[0]
    tm = _row_tile(rows, D_INNER)

    def body(y_ref, z_ref, w_ref, o_ref):
        z = z_ref[...]
        g = y_ref[...] * (z * _sigmoid(z))
        r = lax.rsqrt(jnp.mean(g * g, axis=-1, keepdims=True) + RMS_EPS)
        o_ref[...] = (g * r * w_ref[...]).astype(BF16)

    row_spec = pl.BlockSpec((tm, D_INNER), lambda i: (i, 0))
    return _call(
        body, name=name, out_shape=jax.ShapeDtypeStruct((rows, D_INNER), BF16), grid=(rows // tm,),
        in_specs=[row_spec, row_spec, pl.BlockSpec((1, D_INNER), lambda i: (0, 0))],
        out_specs=row_spec, operands=[y, zx, w], semantics=("parallel",), steps=steps)


def _gate_bwd(name, dyn, y, zx, w):
    rows = y.shape[0]
    tm = _row_tile(rows, D_INNER)

    def body(d_ref, y_ref, z_ref, w_ref, dy_ref, dz_ref, dw_ref):
        i = pl.program_id(0)
        z, yv = z_ref[...], y_ref[...]
        sig = _sigmoid(z)
        sz = z * sig
        g = yv * sz
        r = lax.rsqrt(jnp.mean(g * g, axis=-1, keepdims=True) + RMS_EPS)
        ghat = g * r
        dn = d_ref[...]
        dghat = dn * w_ref[...]
        dg = r * (dghat - ghat * jnp.mean(dghat * ghat, axis=-1, keepdims=True))
        dy_ref[...] = dg * sz
        dz_ref[...] = (dg * yv * sig * (1.0 + z * (1.0 - sig))).astype(BF16)

        @pl.when(i == 0)
        def _():
            dw_ref[...] = jnp.zeros_like(dw_ref)

        dw_ref[...] += jnp.sum(dn * ghat, axis=0, keepdims=True)

    row_spec = pl.BlockSpec((tm, D_INNER), lambda i: (i, 0))
    vec_spec = pl.BlockSpec((1, D_INNER), lambda i: (0, 0))
    return pl.pallas_call(
        body, name=name,
        out_shape=(jax.ShapeDtypeStruct((rows, D_INNER), F32), jax.ShapeDtypeStruct((rows, D_MAIN), BF16),
                   jax.ShapeDtypeStruct((1, D_INNER), F32)),
        grid=(rows // tm,), in_specs=[row_spec, row_spec, row_spec, vec_spec],
        out_specs=(row_spec, row_spec, vec_spec), compiler_params=_cparams(("arbitrary",)),
    )(dyn, y, zx, w)


def _split3(x):
    hi = x.astype(BF16)
    r1 = x - hi.astype(F32)
    mid = r1.astype(BF16)
    lo = (r1 - mid.astype(F32)).astype(BF16)
    return hi, mid, lo


def _dot3_data_lhs(x, sel):
    sel16 = sel.astype(F32).astype(BF16)
    hi, mid, lo = _split3(x)
    return _dot(hi, sel16) + _dot(mid, sel16) + _dot(lo, sel16)


def _dot2_data_lhs(x, sel):
    sel16 = sel.astype(F32).astype(BF16)
    hi = x.astype(BF16)
    mid = (x - hi.astype(F32)).astype(BF16)
    return _dot(hi, sel16) + _dot(mid, sel16)


def _dot3_data_rhs(sel, x):
    sel16 = sel.astype(F32).astype(BF16)
    hi, mid, lo = _split3(x)
    return _dot(sel16, hi) + _dot(sel16, mid) + _dot(sel16, lo)


def _causal_masks():
    r = lax.broadcasted_iota(jnp.int32, (CHUNK, CHUNK), 0)
    c = lax.broadcasted_iota(jnp.int32, (CHUNK, CHUNK), 1)
    return r >= c, r <= c


def _expand_heads_matrix(g):
    k = lax.broadcasted_iota(jnp.int32, (LANES, GROUP_W), 0)
    j = lax.broadcasted_iota(jnp.int32, (LANES, GROUP_W), 1)
    return HEADS_PER_GROUP * g + jnp.right_shift(j, 6) == k


def _reduce_heads_matrix(g):
    j = lax.broadcasted_iota(jnp.int32, (GROUP_W, LANES), 0)
    k = lax.broadcasted_iota(jnp.int32, (GROUP_W, LANES), 1)
    return HEADS_PER_GROUP * g + jnp.right_shift(j, 6) == k


def _reduce_pair_matrix(g, p):
    j = lax.broadcasted_iota(jnp.int32, (LANES, LANES), 0)
    k = lax.broadcasted_iota(jnp.int32, (LANES, LANES), 1)
    return HEADS_PER_GROUP * g + 2 * p + jnp.right_shift(j, 6) == k


def _group_cols(ref, g, width):
    return ref.at[:, pl.ds(g * width, width)]


def _ssd_prep(name, dt, a128, steps=()):
    rows = dt.shape[0]
    nc = rows // CHUNK

    def body(dt_ref, a_ref, dte_ref, acs_ref, acst_ref):
        causal, _ = _causal_masks()
        dtv = dt_ref[...]
        acs = _dot3_data_rhs(causal, dtv) * a_ref[...]
        acst_ref[...] = acs.T[0:SSM_HEADS]
        for g in range(N_GROUPS):
            expand = _expand_heads_matrix(g)
            _group_cols(dte_ref, g, GROUP_W)[...] = _dot3_data_lhs(dtv, expand)
            _group_cols(acs_ref, g, GROUP_W)[...] = _dot3_data_lhs(acs, expand)

    blk = pl.BlockSpec((CHUNK, D_INNER), lambda c: (c, 0))
    shp = jax.ShapeDtypeStruct((rows, D_INNER), F32)
    return _call(
        body, name=name, out_shape=(shp, shp, jax.ShapeDtypeStruct((nc, SSM_HEADS, CHUNK), F32)), grid=(nc,),
        in_specs=[pl.BlockSpec((CHUNK, LANES), lambda c: (c, 0)), pl.BlockSpec((1, LANES), lambda c: (0, 0))],
        out_specs=(blk, blk, pl.BlockSpec((None, SSM_HEADS, CHUNK), lambda c: (c, 0, 0))),
        operands=[dt, a128], semantics=("parallel",), steps=steps)


def _ssd_common(x_ref, b_ref, c_ref, dte_ref, acs_ref):
    x = x_ref[...]
    dt_exp = dte_ref[...]
    acs_exp = acs_ref[...]
    tot_exp = acs_ref[pl.ds(CHUNK - 1, 1), :]
    xdt = x * dt_exp
    e_exp = jnp.exp(acs_exp)
    f_exp = jnp.exp(tot_exp - acs_exp)
    return _causal_masks(), x, dt_exp, acs_exp, tot_exp, xdt, e_exp, f_exp, b_ref[...], c_ref[...]


def _pair_decay(acs_pair, acs_row, e, causal):
    lane = lax.broadcasted_iota(jnp.int32, (CHUNK, LANES), 1)
    mine = (lane < HEAD_DIM) if e == 0 else (lane >= HEAD_DIM)
    a_l = jnp.where(mine, acs_pair, pltpu.roll(acs_pair, HEAD_DIM, 1))
    seg = a_l - acs_row
    dm = jnp.where(causal[0], jnp.exp(jnp.minimum(seg, 0.0)), 0.0)
    dmt = jnp.where(causal[1], jnp.exp(jnp.minimum(-seg, 0.0)), 0.0)
    return dm, dmt


def _ssd_specs(index_of_chunk):
    wide = pl.BlockSpec((CHUNK, D_INNER), lambda c: (index_of_chunk(c), 0))
    b_spec = pl.BlockSpec((CHUNK, D_BC), lambda c: (index_of_chunk(c), D_INNER // D_BC))
    c_spec = pl.BlockSpec((CHUNK, D_BC), lambda c: (index_of_chunk(c), D_INNER // D_BC + 1))
    rows_spec = pl.BlockSpec((None, SSM_HEADS, CHUNK), lambda c: (index_of_chunk(c), 0, 0))
    state_spec = pl.BlockSpec((N_GROUPS, None, D_STATE, GROUP_W), lambda c: (0, index_of_chunk(c), 0, 0))
    return wide, b_spec, c_spec, rows_spec, state_spec


def _ssd_fwd(name, xbc, dt_exp, acs_exp, acs_rows, dskexp, steps=()):
    rows = xbc.shape[0]
    nc = rows // CHUNK

    def body(x_ref, b_ref, c_ref, dte_ref, acs_ref, acst_ref, dsk_ref, y_ref, st_ref, s_scr):
        @pl.when(pl.program_id(0) == 0)
        def _():
            s_scr[...] = jnp.zeros_like(s_scr)

        lane = lax.broadcasted_iota(jnp.int32, (CHUNK, LANES), 1)
        for g in range(N_GROUPS):
            y_g = _group_cols(y_ref, g, GROUP_W)
            causal, x, _, acs_exp_v, tot_exp, xdt, e_exp, f_exp, bm, cm = _ssd_common(
                _group_cols(x_ref, g, GROUP_W), _group_cols(b_ref, g, D_STATE), _group_cols(c_ref, g, D_STATE),
                _group_cols(dte_ref, g, GROUP_W), _group_cols(acs_ref, g, GROUP_W))
            state = s_scr[g]
            st_ref[g] = state
            cb16, bb16 = cm.astype(BF16), bm.astype(BF16)
            cb = _dot_nt(cb16, bb16)
            base = e_exp * _dot(cb16, state.astype(BF16)) + _group_cols(dsk_ref, g, GROUP_W)[...] * x
            for p in range(HEADS_PER_GROUP // 2):
                sl = slice(p * LANES, (p + 1) * LANES)
                xp = xdt[:, sl].astype(BF16)
                yd = []
                for e in range(2):
                    acs_row = acst_ref[pl.ds(g * HEADS_PER_GROUP + 2 * p + e, 1), :]
                    dm, _ = _pair_decay(acs_exp_v[:, sl], acs_row, e, causal)
                    yd.append(_dot((cb * dm).astype(BF16), xp))
                y_g[:, sl] = jnp.where(lane < HEAD_DIM, yd[0], yd[1]) + base[:, sl]
            s_scr[g] = jnp.exp(tot_exp) * state + _dot_tn(bb16, (f_exp * xdt).astype(BF16))

    wide, b_spec, c_spec, rows_spec, state_spec = _ssd_specs(lambda c: c)
    return _call(
        body, name=name,
        out_shape=(jax.ShapeDtypeStruct((rows, D_INNER), F32),
                   jax.ShapeDtypeStruct((N_GROUPS, nc, D_STATE, GROUP_W), F32)),
        grid=(nc,),
        in_specs=[wide, b_spec, c_spec, wide, wide, rows_spec, pl.BlockSpec((1, D_INNER), lambda c: (0, 0))],
        out_specs=(wide, state_spec),
        scratch_shapes=[pltpu.VMEM((N_GROUPS, D_STATE, GROUP_W), F32)],
        operands=[xbc, xbc, xbc, dt_exp, acs_exp, acs_rows, dskexp], semantics=("arbitrary",), steps=steps)


def _ssd_bwd(name, xbc, dt_exp, acs_exp, acs_rows, dt, a128, dskexp, dy, states, steps=()):
    rows = xbc.shape[0]
    nc = rows // CHUNK
    last = nc - 1

    def body(x_ref, b_ref, c_ref, dte_ref, acs_ref, acst_ref, dt_ref, a128_ref, dsk_all, dy_all, st_all,
             dxbc_all, ddt_ref, dalog_ref, ddsk_ref, ds_all):
        dx_all, db_all, dc_all = (dxbc_all.at[:, :D_INNER], dxbc_all.at[:, D_INNER:D_INNER + D_BC],
                                  dxbc_all.at[:, D_INNER + D_BC:])

        @pl.when(pl.program_id(0) == 0)
        def _():
            ds_all[...] = jnp.zeros_like(ds_all)
            dalog_ref[...] = jnp.zeros_like(dalog_ref)
            ddsk_ref[...] = jnp.zeros_like(ddsk_ref)

        dacs = jnp.zeros((CHUNK, LANES), F32)
        ddt_x = jnp.zeros((CHUNK, LANES), F32)
        for g in range(N_GROUPS):
            dacs_g, ddt_x_g = group(
                g, _group_cols(x_ref, g, GROUP_W), _group_cols(b_ref, g, D_STATE), _group_cols(c_ref, g, D_STATE),
                _group_cols(dte_ref, g, GROUP_W), _group_cols(acs_ref, g, GROUP_W), acst_ref,
                _group_cols(dsk_all, g, GROUP_W), _group_cols(dy_all, g, GROUP_W), st_all.at[g],
                _group_cols(dx_all, g, GROUP_W), _group_cols(db_all, g, D_STATE), _group_cols(dc_all, g, D_STATE),
                ddsk_ref, ds_all.at[g])
            dacs, ddt_x = dacs + dacs_g, ddt_x + ddt_x_g
        _, causal_t = _causal_masks()
        da = _dot3_data_rhs(causal_t, dacs)
        ddt_ref[...] = da * a128_ref[...] + ddt_x
        dalog_ref[...] += jnp.sum(da * dt_ref[...], axis=0, keepdims=True) * a128_ref[...]

    def group(g, x_ref, b_ref, c_ref, dte_ref, acs_ref, acst_ref, dsk_ref, dy_ref, st_ref,
              dx_ref, db_ref, dc_ref, ddsk_ref, ds_scr):
        causal, x, dt_exp, acs_exp_v, tot_exp, xdt, e_exp, f_exp, bm, cm = _ssd_common(
            x_ref, b_ref, c_ref, dte_ref, acs_ref)
        reduce_heads = _reduce_heads_matrix(g)
        state, dstate = st_ref[...], ds_scr[...]
        dyv = dy_ref[...]
        cb16, bb16 = cm.astype(BF16), bm.astype(BF16)
        s16, ds16 = state.astype(BF16), dstate.astype(BF16)
        cb = _dot_nt(cb16, bb16)
        cbt = _dot_nt(bb16, cb16)
        cs = _dot(cb16, s16)
        bds = _dot(bb16, ds16)
        edy = e_exp * dyv
        fx = f_exp * xdt
        dxdt_base = f_exp * bds
        dc_acc = _dot_nt(edy.astype(BF16), s16)
        db_acc = _dot_nt(fx.astype(BF16), ds16)
        ds_scr[...] = jnp.exp(tot_exp) * dstate + _dot_tn(cb16, edy.astype(BF16))
        q = fx * bds
        dacs = _dot2_data_lhs(edy * cs - q, reduce_heads)
        dtot = jnp.sum(_dot2_data_lhs(q + jnp.exp(tot_exp) * dstate * state, reduce_heads), axis=0, keepdims=True)
        ddsk_ref[...] += jnp.sum(_dot2_data_lhs(dyv * x, reduce_heads), axis=0, keepdims=True)
        lane = lax.broadcasted_iota(jnp.int32, (CHUNK, LANES), 1)
        dcb = jnp.zeros((CHUNK, CHUNK), F32)
        dcbt = jnp.zeros((CHUNK, CHUNK), F32)
        ddt_x = jnp.zeros((CHUNK, LANES), F32)
        for p in range(HEADS_PER_GROUP // 2):
            sl = slice(p * LANES, (p + 1) * LANES)
            xp, dyp = xdt[:, sl], dyv[:, sl]
            xp16, dyp16 = xp.astype(BF16), dyp.astype(BF16)
            dxh = []
            for e in range(2):
                h = 2 * p + e
                mine = (lane < HEAD_DIM) if e == 0 else (lane >= HEAD_DIM)
                acs_row = acst_ref[pl.ds(g * HEADS_PER_GROUP + h, 1), :]
                dm, dmt = _pair_decay(acs_exp_v[:, sl], acs_row, e, causal)
                m, mt = cb * dm, cbt * dmt
                xh16 = jnp.where(mine, xp, 0.0).astype(BF16)
                dyh16 = jnp.where(mine, dyp, 0.0).astype(BF16)
                d_m = _dot_nt(dyh16, xp16)
                d_mt = _dot_nt(xh16, dyp16)
                dacs_h = (jnp.sum(d_m * m, axis=-1, keepdims=True)
                          - jnp.sum(d_mt * mt, axis=-1, keepdims=True))
                dacs = dacs + jnp.where(lane == HEADS_PER_GROUP * g + h, dacs_h, 0.0)
                dcb = dcb + d_m * dm
                dcbt = dcbt + d_mt * dmt
                dxh.append(_dot(mt.astype(BF16), dyp16))
            dxdt = jnp.where(lane < HEAD_DIM, dxh[0], dxh[1]) + dxdt_base[:, sl]
            dx_ref[:, sl] = dxdt * dt_exp[:, sl] + dsk_ref[:, sl] * dyp
            ddt_x = ddt_x + _dot2_data_lhs(dxdt * x[:, sl], _reduce_pair_matrix(g, p))
        dc_ref[...] = dc_acc + _dot(dcb.astype(BF16), bb16)
        db_ref[...] = db_acc + _dot(dcbt.astype(BF16), cb16)
        row = lax.broadcasted_iota(jnp.int32, (CHUNK, LANES), 0)
        return dacs + jnp.where(row == CHUNK - 1, dtot, 0.0), ddt_x

    wide, b_spec, c_spec, rows_spec, state_spec = _ssd_specs(lambda c: last - c)
    heads_spec = pl.BlockSpec((CHUNK, LANES), lambda c: (last - c, 0))
    vec_spec = pl.BlockSpec((1, LANES), lambda c: (0, 0))
    vec_shape = jax.ShapeDtypeStruct((1, LANES), F32)
    return _call(
        body, name=name,
        out_shape=(jax.ShapeDtypeStruct((rows, D_XBC), F32), jax.ShapeDtypeStruct((rows, LANES), F32),
                   vec_shape, vec_shape),
        grid=(nc,),
        in_specs=[wide, b_spec, c_spec, wide, wide, rows_spec, heads_spec, vec_spec,
                  pl.BlockSpec((1, D_INNER), lambda c: (0, 0)), wide, state_spec],
        out_specs=(pl.BlockSpec((CHUNK, D_XBC), lambda c: (last - c, 0)), heads_spec, vec_spec, vec_spec),
        scratch_shapes=[pltpu.VMEM((N_GROUPS, D_STATE, GROUP_W), F32)],
        operands=[xbc, xbc, xbc, dt_exp, acs_exp, acs_rows, dt, a128, dskexp, dy, states],
        semantics=("arbitrary",), steps=steps)


def _attn_visible(b, heads=1):
    row = jnp.bitwise_and(lax.broadcasted_iota(jnp.int32, (heads * CHUNK, 3 * CHUNK), 0), CHUNK - 1)
    col = lax.broadcasted_iota(jnp.int32, (heads * CHUNK, 3 * CHUNK), 1)
    bb = b + jnp.zeros_like(col)
    meta = (col < CHUNK) & (bb >= 1) & (col >= PAD_ROWS)
    prev = (col >= CHUNK) & (col < 2 * CHUNK) & (bb >= 2) & ((col - CHUNK) > row)
    cur = (col >= 2 * CHUNK) & ((col - 2 * CHUNK) <= row) & ((bb >= 1) | ((col - 2 * CHUNK) >= PAD_ROWS))
    return meta | prev | cur


def _attn_visible4(b):
    return _attn_visible(b, 4)


def _stack_heads(q_ref, sink_ref, kvh, scale):
    lane = lax.broadcasted_iota(jnp.int32, (CHUNK, LANES), 1)
    parts, sinks = [], []
    for pp in range(2):
        pair = kvh * 2 + pp
        qp = q_ref[:, pair * LANES:(pair + 1) * LANES] * scale
        for e in range(2):
            mine = (lane < HEAD_DIM) if e == 0 else (lane >= HEAD_DIM)
            parts.append(jnp.where(mine, qp, 0.0).astype(BF16))
            sinks.append(jnp.full((CHUNK, 1), sink_ref[2 * pair + e], F32))
    return jnp.concatenate(parts, axis=0), jnp.concatenate(sinks, axis=0)


def _attn_operands(q_ref, k0, kp, kc, v0, vp, vc, sink_ref):
    kcat, vcat, q4, sink4 = [], [], [], []
    for kvh in range(N_KV_HEADS):
        ksl = slice(kvh * LANES, (kvh + 1) * LANES)
        kcat.append(jnp.concatenate([k0[:, ksl], kp[:, ksl], kc[:, ksl]], axis=0).astype(BF16))
        vcat.append(jnp.concatenate([v0[:, ksl], vp[:, ksl], vc[:, ksl]], axis=0).astype(BF16))
        stacked, sinks = _stack_heads(q_ref, sink_ref, kvh, ATTN_SCALE)
        q4.append(stacked)
        sink4.append(sinks)
    return kcat, vcat, q4, sink4


def _attn_probs(q4, kcat, visible, sink4):
    heads = range(N_KV_HEADS)
    s = [jnp.where(visible, _dot_nt(q4[h], kcat[h]), NEG_INF) for h in heads]
    m = [jnp.maximum(jnp.max(s[h], axis=-1, keepdims=True), sink4[h]) for h in heads]
    pe = [jnp.exp(s[h] - m[h]) for h in heads]
    pe_sink = [jnp.exp(sink4[h] - m[h]) for h in heads]
    inv = [1.0 / (jnp.sum(pe[h], axis=-1, keepdims=True) + pe_sink[h]) for h in heads]
    return [pe[h] * inv[h] for h in heads], [pe_sink[h] * inv[h] for h in heads]


def _unstack_pairs(stacked, pp):
    lane = lax.broadcasted_iota(jnp.int32, (CHUNK, LANES), 1)
    return jnp.where(lane < HEAD_DIM, stacked[(2 * pp) * CHUNK:(2 * pp + 1) * CHUNK],
                     stacked[(2 * pp + 1) * CHUNK:(2 * pp + 2) * CHUNK])


def _attn_specs(colblock):
    blk = lambda f: pl.BlockSpec((CHUNK, 2 * D_KV), f)
    return [blk(lambda b: (0, colblock)), blk(lambda b: (jnp.maximum(b - 1, 0), colblock)), blk(lambda b: (b, colblock))]


def _attn_fwd(name, q, kv2, sinks, steps=()):
    rows = q.shape[0]

    def body(q_ref, k0, kp, kc, v0, vp, vc, sink_ref, o_ref):
        visible = _attn_visible4(pl.program_id(0))
        kcat, vcat, q4, sink4 = _attn_operands(q_ref, k0, kp, kc, v0, vp, vc, sink_ref)
        pn, _ = _attn_probs(q4, kcat, visible, sink4)
        o4 = [_dot(pn[h].astype(BF16), vcat[h]) for h in range(N_KV_HEADS)]
        for kvh in range(N_KV_HEADS):
            for pp in range(2):
                qsl = slice((kvh * 2 + pp) * LANES, (kvh * 2 + pp + 1) * LANES)
                o_ref[:, qsl] = _unstack_pairs(o4[kvh], pp).astype(BF16)

    return _call(
        body, name=name, out_shape=jax.ShapeDtypeStruct((rows, D_MODEL), BF16), grid=(rows // CHUNK,),
        in_specs=[pl.BlockSpec((CHUNK, D_MODEL), lambda b: (b, 0))] + _attn_specs(0) + _attn_specs(1)
        + [pl.BlockSpec(memory_space=pltpu.SMEM)],
        out_specs=pl.BlockSpec((CHUNK, D_MODEL), lambda b: (b, 0)),
        operands=[q, kv2, kv2, kv2, kv2, kv2, kv2, sinks], semantics=("parallel",), steps=steps)


def _attn_bwd(name, q, kv2, sinks, do, steps=()):
    rows = q.shape[0]

    def body(q_ref, k0, kp, kc, v0, vp, vc, sink_ref, do_ref,
             dq_ref, dkc_ref, dkp_ref, dvc_ref, dvp_ref, dkm_ref, dvm_ref, dsink_ref):
        @pl.when(pl.program_id(0) == 0)
        def _():
            dkm_ref[...] = jnp.zeros_like(dkm_ref)
            dvm_ref[...] = jnp.zeros_like(dvm_ref)
            dsink_ref[...] = jnp.zeros_like(dsink_ref)

        visible = _attn_visible4(pl.program_id(0))
        heads = range(N_KV_HEADS)
        lane1 = lax.broadcasted_iota(jnp.int32, (1, LANES), 1)
        kcat, vcat, q4, sink4 = _attn_operands(q_ref, k0, kp, kc, v0, vp, vc, sink_ref)
        do4 = [_stack_heads(do_ref, sink_ref, h, 1.0)[0] for h in heads]
        pn, psink = _attn_probs(q4, kcat, visible, sink4)
        dp = [_dot_nt(do4[h], vcat[h]) for h in heads]
        delta = [jnp.sum(pn[h] * dp[h], axis=-1, keepdims=True) for h in heads]
        ds16 = [(pn[h] * (dp[h] - delta[h])).astype(BF16) for h in heads]
        dq4 = [_dot(ds16[h], kcat[h]) for h in heads]
        dk_acc = [_dot_tn(ds16[h], q4[h]) for h in heads]
        dv_acc = [_dot_tn(pn[h].astype(BF16), do4[h]) for h in heads]
        dsink = jnp.zeros((1, LANES), F32)
        for kvh in heads:
            ksl = slice(kvh * LANES, (kvh + 1) * LANES)
            sink_terms = psink[kvh] * delta[kvh]
            for j in range(4):
                part = jnp.sum(sink_terms[j * CHUNK:(j + 1) * CHUNK], axis=0, keepdims=True)
                dsink = dsink - jnp.where(lane1 == kvh * 4 + j, part, 0.0)
            for pp in range(2):
                qsl = slice((kvh * 2 + pp) * LANES, (kvh * 2 + pp + 1) * LANES)
                dq_ref[:, qsl] = (_unstack_pairs(dq4[kvh], pp) * ATTN_SCALE).astype(BF16)
            dkm_ref[:, ksl] += dk_acc[kvh][0:CHUNK]
            dvm_ref[:, ksl] += dv_acc[kvh][0:CHUNK]
            dkp_ref[:, ksl] = dk_acc[kvh][CHUNK:2 * CHUNK]
            dvp_ref[:, ksl] = dv_acc[kvh][CHUNK:2 * CHUNK]
            dkc_ref[:, ksl] = dk_acc[kvh][2 * CHUNK:3 * CHUNK]
            dvc_ref[:, ksl] = dv_acc[kvh][2 * CHUNK:3 * CHUNK]
        dsink_ref[...] += dsink

    qspec = pl.BlockSpec((CHUNK, D_MODEL), lambda b: (b, 0))
    kvspec = pl.BlockSpec((CHUNK, 2 * D_KV), lambda b: (b, 0))
    fixed = pl.BlockSpec((CHUNK, 2 * D_KV), lambda b: (0, 0))
    kv_shape = jax.ShapeDtypeStruct((rows, 2 * D_KV), F32)
    meta_shape = jax.ShapeDtypeStruct((CHUNK, 2 * D_KV), F32)
    return _call(
        body, name=name,
        out_shape=(jax.ShapeDtypeStruct((rows, D_MODEL), BF16), kv_shape, kv_shape, kv_shape, kv_shape,
                   meta_shape, meta_shape, jax.ShapeDtypeStruct((1, LANES), F32)),
        grid=(rows // CHUNK,),
        in_specs=[qspec] + _attn_specs(0) + _attn_specs(1) + [pl.BlockSpec(memory_space=pltpu.SMEM), qspec],
        out_specs=(qspec, kvspec, kvspec, kvspec, kvspec, fixed, fixed, pl.BlockSpec((1, LANES), lambda b: (0, 0))),
        operands=[q, kv2, kv2, kv2, kv2, kv2, kv2, sinks, do], semantics=("arbitrary",), steps=steps)


def _kv_grad_combine(name, dk_cur, dk_prev, dk_meta, dv_cur, dv_prev, dv_meta):
    rows = dk_cur.shape[0]
    nb = rows // CHUNK
    width = 2 * D_KV

    def body(kc_ref, kp_ref, km_ref, vc_ref, vp_ref, vm_ref, o_ref):
        jj = pl.program_id(0) + jnp.zeros((CHUNK, 1), jnp.int32)
        for half, (c_ref, p_ref, m_ref) in enumerate(((kc_ref, kp_ref, km_ref), (vc_ref, vp_ref, vm_ref))):
            total = c_ref[...] + jnp.where(jj < nb - 1, p_ref[...], 0.0) + jnp.where(jj == 0, m_ref[...], 0.0)
            o_ref[:, half * width:(half + 1) * width] = total.astype(BF16)

    blk = lambda f: pl.BlockSpec((CHUNK, width), f)
    three = lambda: [blk(lambda j: (j, 0)), blk(lambda j: (jnp.minimum(j + 1, nb - 1), 0)), blk(lambda j: (0, 0))]
    return pl.pallas_call(
        body, name=name, out_shape=jax.ShapeDtypeStruct((rows, 2 * width), BF16), grid=(nb,),
        in_specs=three() + three(), out_specs=pl.BlockSpec((CHUNK, 2 * width), lambda j: (j, 0)),
        compiler_params=_cparams(("parallel",)),
    )(dk_cur, dk_prev, dk_meta, dv_cur, dv_prev, dv_meta)


def _adamw(name, w, g, m, v, steps=()):
    rows, width = w.shape
    tr = rows
    for cand in range(8, rows + 1, 8):
        if rows % cand == 0 and cand * width * 4 <= (1 << 20):
            tr = cand

    def body(*refs):
        _adamw_update(*refs)

    blk = pl.BlockSpec((tr, width), lambda i: (i, 0))
    shp = jax.ShapeDtypeStruct((rows, width), F32)
    return _call(body, name=name, out_shape=(shp, shp, shp), grid=(rows // tr,), in_specs=[blk] * 4,
                 out_specs=(blk,) * 3, operands=[w, g, m, v], semantics=("parallel",), steps=steps)


def _adamw_update(w_ref, g_ref, m_ref, v_ref, d_ref, mo_ref, vo_ref):
    gv = g_ref[...]
    mn = ADAM_B1 * m_ref[...] + (1.0 - ADAM_B1) * gv
    vn = ADAM_B2 * v_ref[...] + (1.0 - ADAM_B2) * (gv * gv)
    m_hat = mn / (1.0 - ADAM_B1 ** ADAM_STEP)
    v_hat = vn / (1.0 - ADAM_B2 ** ADAM_STEP)
    d_ref[...] = -ADAM_LR * (m_hat / (jnp.sqrt(v_hat) + ADAM_EPS) + ADAM_WD * w_ref[...])
    mo_ref[...] = mn
    vo_ref[...] = vn


def _adamw_small(name, ws, gs, ms, vs):
    n = len(ws)

    def body(*refs):
        for i in range(n):
            _adamw_update(*refs[i::n])

    shapes = [jax.ShapeDtypeStruct(a.shape, F32) for a in ws]
    outs = pl.pallas_call(body, name=name, out_shape=shapes * 3, in_specs=[VMEM_SPEC] * (4 * n),
                          out_specs=[VMEM_SPEC] * (3 * n), compiler_params=_cparams())(*ws, *gs, *ms, *vs)
    return outs[:n], outs[n:2 * n], outs[2 * n:]


def _ffn_fwd(tag, h, hn, p, i, plan):
    up_g, up_v, act = _ffn_up_conv(f"ffn{tag}_up", hn, plan.weight("f_w_up", i), p["f_conv_w"][i],
                                   p["f_conv_b"][i:i + 1], steps=plan.steps(f"ffn{tag}_up"))
    pre = _mm(f"ffn{tag}_down", act, plan.weight("f_w_down", i), "nn", steps=plan.steps(f"ffn{tag}_down"))
    return pre, (h, hn, up_g, up_v, act, pre)


def _ffn_bwd(tag, dpre, saved, p, i, plan):
    h, hn, up_g, up_v, act, pre = saved
    plan.grad("f_w_down", i, _mm(f"ffn{tag}_down_dw", act, dpre, "tn", out_dtype=BF16))
    dact = _mm(f"ffn{tag}_down_dx", dpre, plan.weight("f_w_down", i), "nt", steps=plan.steps(f"ffn{tag}_down_dx"))
    gwg, gwv, gbg, gbv, dhn, g_up = _ffn_conv_bwd(
        f"ffn{tag}_conv_bwd", up_g, up_v, dact, p["f_conv_w"][i], p["f_conv_b"][i:i + 1], hn,
        plan.weight("f_w_up", i), steps=plan.steps(f"ffn{tag}_conv_bwd"))
    g_cw, g_cb = jnp.concatenate([gwg, gwv], axis=1), jnp.concatenate([gbg, gbv], axis=1)
    plan.grad("f_w_up", i, g_up)
    return dhn, dict(f_conv_w=g_cw, f_conv_b=g_cb)


def _lanes_pad(a, width=LANES):
    return jnp.pad(a, [(0, 0)] * (a.ndim - 1) + [(0, width - a.shape[-1])])


def _dup_heads(w):
    rows = w.shape[0]
    w = w.reshape(rows, 2 * N_KV_HEADS, 1, HEAD_DIM)
    return jnp.broadcast_to(w, (rows, 2 * N_KV_HEADS, 2, HEAD_DIM)).reshape(rows, 4 * D_KV)


def _undup_heads(g):
    rows = g.shape[0]
    return g.reshape(rows, 2 * N_KV_HEADS, 2, HEAD_DIM).sum(axis=2).reshape(rows, 2 * D_KV)


def _local_step(x2, target, p, plan):
    seq = x2.shape[0]
    rows = seq + CHUNK
    g = {}

    h0 = jnp.concatenate([jnp.zeros((PAD_ROWS, D_MODEL), F32), p["meta_tokens"], x2], axis=0)

    w_in = plan.weight("a_w_in")
    w_dt = jnp.pad(w_in[D_MAIN:], ((0, LANES - SSM_HEADS), (0, 0)))
    dt_bias = _lanes_pad(p["a_dt_bias"])
    a128 = _lanes_pad(-jnp.exp(p["a_a_log"]))
    dskexp = jnp.repeat(p["a_d_skip"].reshape(SSM_HEADS), HEAD_DIM).reshape(1, D_INNER)

    hn0 = _rms_fwd("a_norm", h0, p["a_norm_pre"])
    zx = _mm("a_in_main", hn0, w_in, "nt", k_rows=D_MAIN, steps=plan.steps("a_in_main"))
    dtr = _mm("a_in_dt", hn0, w_dt, "nt")
    xbc = _conv4_fwd("a_conv", zx, p["a_conv_w"], p["a_conv_b"], steps=plan.steps("a_conv"))
    dt = _dt_fwd("a_dt", dtr, dt_bias)
    dt_exp, acs_exp, acs_rows = _ssd_prep("a_ssd_prep", dt, a128, steps=plan.steps("a_ssd_prep"))
    y, states = _ssd_fwd("a_ssd", xbc, dt_exp, acs_exp, acs_rows, dskexp, steps=plan.steps("a_ssd"))
    yn = _gate_fwd("a_gate", y, zx, p["a_gate_norm"], steps=plan.steps("a_gate"))
    mix = _mm("a_out", yn, plan.weight("a_w_out"), "nn", steps=plan.steps("a_out"))
    h1, (hn_f0,) = _resid_norm_fwd("a_resid", h0, mix, p["a_norm_post"], [p["f_norm_pre"][0:1]])

    pre_f0, ffn0 = _ffn_fwd("0", h1, hn_f0, p, 0, plan)
    h2, (hkv, hn2) = _resid_norm_fwd("ffn0_resid", h1, pre_f0, p["f_norm_post"][0:1], [p["kv_norm"], p["b_norm_pre"]])

    w_kv2 = _dup_heads(plan.weight("w_kv"))
    kv2 = _mm("kv_proj", hkv, w_kv2, "nn")
    q = _mm("b_q", hn2, plan.weight("b_w_q"), "nn")
    sinks = p["b_sinks"].reshape(N_Q_HEADS)
    o = _attn_fwd("b_attn", q, kv2, sinks, steps=plan.steps("b_attn"))
    attn = _mm("b_o", o, plan.weight("b_w_o"), "nn", steps=plan.steps("b_o"))
    h3, (hn_f1,) = _resid_norm_fwd("b_resid", h2, attn, p["b_norm_post"], [p["f_norm_pre"][1:2]])

    pre_f1, ffn1 = _ffn_fwd("1", h3, hn_f1, p, 1, plan)
    dh, loss_vec, dpre_f1, g_post1 = _resid_norm_loss("ffn1_resid_loss", h3, pre_f1, p["f_norm_post"][1:2], target)
    loss = loss_vec[0, 0]

    dhn_f1, g1 = _ffn_bwd("1", dpre_f1, ffn1, p, 1, plan)
    dh, g_pre1, dpre, g["b_norm_post"] = _norm_bwd_add("ffn1_norm_bwd", dh, dhn_f1, h3, p["f_norm_pre"][1:2],
                                                        then=(attn, p["b_norm_post"]))
    plan.grad("b_w_o", None, _mm("b_o_dw", o, dpre, "tn", out_dtype=BF16))
    do = _mm("b_o_dx", dpre, plan.weight("b_w_o"), "nt", steps=plan.steps("b_o_dx"))
    dq, dkc, dkp, dvc, dvp, dkm, dvm, dsink = _attn_bwd("b_attn_bwd", q, kv2, sinks, do, steps=plan.steps("b_attn_bwd"))
    g["b_sinks"] = dsink[:, :N_Q_HEADS]
    dhn2 = _mm("b_q_dx", dq, plan.weight("b_w_q"), "nt")
    plan.grad("b_w_q", None, _mm("b_q_dw", hn2, dq, "tn", out_dtype=BF16))
    dh, g["b_norm_pre"] = _norm_bwd_add("b_norm_bwd", dh, dhn2, h2, p["b_norm_pre"])
    dkv2 = _kv_grad_combine("kv_grad", dkc, dkp, dkm, dvc, dvp, dvm)
    dhkv = _mm("kv_proj_dx", dkv2, w_kv2, "nt")
    plan.grad("w_kv", None, _undup_heads(_mm("kv_proj_dw", hkv, dkv2, "tn")))
    dh, g["kv_norm"], dpre_f0, g_post0 = _norm_bwd_add("kv_norm_bwd", dh, dhkv, h2, p["kv_norm"],
                                                       then=(pre_f0, p["f_norm_post"][0:1]))

    dhn_f0, g0 = _ffn_bwd("0", dpre_f0, ffn0, p, 0, plan)
    dh, g_pre0, dpre, g["a_norm_post"] = _norm_bwd_add("ffn0_norm_bwd", dh, dhn_f0, h1, p["f_norm_pre"][0:1],
                                                        then=(mix, p["a_norm_post"]))
    g["f_norm_post"] = jnp.concatenate([g_post0, g_post1], axis=0)
    g["f_norm_pre"] = jnp.concatenate([g_pre0, g_pre1], axis=0)
    g["f_conv_w"] = jnp.stack([g0["f_conv_w"], g1["f_conv_w"]])
    g["f_conv_b"] = jnp.concatenate([g0["f_conv_b"], g1["f_conv_b"]], axis=0)
    plan.grad("a_w_out", None, _mm("a_out_dw", yn, dpre, "tn", out_dtype=BF16))
    dyn = _mm("a_out_dx", dpre, plan.weight("a_w_out"), "nt", steps=plan.steps("a_out_dx"))
    dy, dzx, g["a_gate_norm"] = _gate_bwd("a_gate_bwd", dyn, y, zx, p["a_gate_norm"])
    dxbc, ddt, dalog, ddsk = _ssd_bwd("a_ssd_bwd", xbc, dt_exp, acs_exp, acs_rows, dt, a128, dskexp, dy, states,
                                      steps=plan.steps("a_ssd_bwd"))
    g["a_a_log"] = dalog[:, :SSM_HEADS]
    g["a_d_skip"] = ddsk[:, :SSM_HEADS]
    ddtr, dbias = _dt_bwd("a_dt_bwd", ddt, dtr, dt_bias)
    g["a_dt_bias"] = dbias[:, :SSM_HEADS]
    dzx, g["a_conv_w"], g["a_conv_b"] = _conv4_bwd("a_conv_bwd", zx, dxbc, p["a_conv_w"], p["a_conv_b"], dzx)
    g_in = _mm("a_in_main_dw", dzx, hn0, "tn", out_dtype=BF16, out_rows=D_IN_PROJ, steps=plan.steps("a_in_main_dw"))
    plan.grad("a_w_in", None, _tn_rows_into("a_in_dt_dw", ddtr, hn0, g_in, D_MAIN, SSM_HEADS))
    dhn0 = _mm("a_in_dt_dx", ddtr, w_dt, "nn", steps=plan.steps("a_in_dt_dx"))
    dhn0 = _mm("a_in_main_dx", dzx, w_in, "nn", acc=dhn0, steps=plan.steps("a_in_main_dx"))
    dh_first, grad_x, g["a_norm_pre"] = _norm_bwd_add("a_norm_bwd", dh, dhn0, h0, p["a_norm_pre"],
                                                      split_first_block=True, steps=plan.steps("a_norm_bwd"))
    g["meta_tokens"] = dh_first[PAD_ROWS:]
    return loss, grad_x, g


ANY = pl.BlockSpec(memory_space=pl.ANY)
VMEM_SPEC = pl.BlockSpec(memory_space=pltpu.VMEM)


def _allgather_small(name, shard):
    rows = shard.shape[0]

    def body(s_ref, o_ref, send_sems, recv_sems):
        x, y, c = _place()
        me = 2 * x + y
        o_ref[me] = s_ref[...]
        chips = _other_chips(x, y)
        sends = [pltpu.make_async_remote_copy(s_ref, o_ref.at[me], send_sems.at[j], recv_sems.at[j],
                                              device_id=(cx, cy, c), device_id_type=MESH)
                 for j, (cx, cy) in enumerate(chips)]
        for cp in sends:
            cp.start()
        for j, (cx, cy) in enumerate(chips):
            pltpu.make_async_remote_copy(s_ref, o_ref.at[2 * cx + cy], send_sems.at[j], recv_sems.at[j],
                                         device_id=(cx, cy, c), device_id_type=MESH).wait_recv()
        for cp in sends:
            cp.wait_send()

    return pl.pallas_call(
        body, name=name, out_shape=jax.ShapeDtypeStruct((N_CHIPS, rows, LANES), F32),
        in_specs=[VMEM_SPEC], out_specs=VMEM_SPEC,
        scratch_shapes=[pltpu.SemaphoreType.DMA((3,)), pltpu.SemaphoreType.DMA((3,))],
        compiler_params=pltpu.CompilerParams(vmem_limit_bytes=VMEM_LIMIT),
    )(shard)


def _row_block(rows, width, itemsize, align, budget=2 << 20):
    best = rows
    for cand in range(align, rows + 1, align):
        if rows % cand == 0 and cand * width * itemsize <= budget:
            best = cand
    return best


def _cast_into_slot(name, chip, w, layer=None):
    rows, width = w.shape[-2:]
    tr = _row_block(rows, width, 4, 16)
    if layer is None:
        in_spec = pl.BlockSpec((tr, width), lambda i, chip_ref: (i, 0))
    else:
        in_spec = pl.BlockSpec((None, tr, width), lambda i, chip_ref: (layer, i, 0))

    def body(chip_ref, w_ref, o_ref):
        o_ref[...] = w_ref[...].astype(BF16)

    return pl.pallas_call(
        body, name=name, out_shape=jax.ShapeDtypeStruct((N_CHIPS, rows, width), BF16),
        grid_spec=pltpu.PrefetchScalarGridSpec(
            num_scalar_prefetch=1, grid=(rows // tr,), in_specs=[in_spec],
            out_specs=pl.BlockSpec((None, tr, width), lambda i, chip_ref: (chip_ref[0], i, 0))),
        compiler_params=_cparams(("parallel",)),
    )(chip, w)


def _allreduce_small(name, vec):
    rows = -(-vec.shape[0] // (2 * SUBLANES)) * (2 * SUBLANES)
    hr = rows // 2
    padded = jnp.pad(vec, ((0, rows - vec.shape[0]), (0, 0)))

    def body(v_ref, o_ref, theirs, pair, by_chip, send_sems, recv_sems):
        x, y, c = _place()
        me = 2 * x + y
        sibling = (x, y, 1 - c)
        mine = pl.ds(pl.multiple_of(c * hr, SUBLANES), hr)
        other = pl.ds(pl.multiple_of((1 - c) * hr, SUBLANES), hr)

        swap = _remote(v_ref, theirs, send_sems, recv_sems, 0, sibling)
        swap.start()
        swap.wait()
        south = (c + jnp.zeros((1, 1), jnp.int32)) == 0
        pair[...] = jnp.where(south, v_ref[...], theirs[...]) + jnp.where(south, theirs[...], v_ref[...])

        by_chip[me] = pair[mine, :]
        sends = [_remote(by_chip.at[me], by_chip.at[me], send_sems, recv_sems, 1 + j, (cx, cy, c))
                 for j, (cx, cy) in enumerate(_other_chips(x, y))]
        for cp in sends:
            cp.start()
        for j, (cx, cy) in enumerate(_other_chips(x, y)):
            _remote(by_chip.at[me], by_chip.at[2 * cx + cy], send_sems, recv_sems, 1 + j, (cx, cy, c)).wait_recv()
        for cp in sends:
            cp.wait_send()
        total = by_chip[0]
        for s in range(1, N_CHIPS):
            total = total + by_chip[s]

        o_ref[mine, :] = total
        back = _remote(o_ref.at[mine], o_ref.at[mine], send_sems, recv_sems, 4, sibling)
        back.start()
        _remote(o_ref.at[other], o_ref.at[other], send_sems, recv_sems, 4, sibling).wait_recv()
        back.wait_send()

    out = pl.pallas_call(
        body, name=name, out_shape=jax.ShapeDtypeStruct((rows, LANES), F32),
        in_specs=[VMEM_SPEC], out_specs=VMEM_SPEC,
        scratch_shapes=[pltpu.VMEM((rows, LANES), F32), pltpu.VMEM((rows, LANES), F32),
                        pltpu.VMEM((N_CHIPS, hr, LANES), F32), pltpu.SemaphoreType.DMA((5,)),
                        pltpu.SemaphoreType.DMA((5,))],
        compiler_params=pltpu.CompilerParams(vmem_limit_bytes=VMEM_LIMIT),
    )(padded)
    return out[:vec.shape[0]]


def _rs_pair_add(name, place, grads, partner, split="rows"):
    _, half_rows, width = partner.shape
    tr = _row_block(half_rows, width, 2, 16)
    nb = half_rows // tr
    if split == "rows":
        mine = pl.BlockSpec((None, tr, width), lambda s, i, pr: (s, pr[1] * nb + i, 0))
    else:
        mine = pl.BlockSpec((None, tr, width), lambda s, i, pr: (s, i, pr[1]))

    def body(place_ref, g_ref, p_ref, o_ref):
        o_ref[...] = (g_ref[...].astype(F32) + p_ref[...].astype(F32)).astype(BF16)

    return pl.pallas_call(
        body, name=name, out_shape=jax.ShapeDtypeStruct(partner.shape, BF16),
        grid_spec=pltpu.PrefetchScalarGridSpec(
            num_scalar_prefetch=1, grid=(N_CHIPS, nb),
            in_specs=[mine, pl.BlockSpec((None, tr, width), lambda s, i, pr: (s, i, 0))],
            out_specs=pl.BlockSpec((None, tr, width), lambda s, i, pr: (s, i, 0))),
        compiler_params=_cparams(("parallel", "parallel")),
    )(place, grads, partner)


def _rs_chip_add(name, place, mine, others, split="rows"):
    _, half_rows, width = mine.shape
    tr = _row_block(half_rows, width, 4, 16, budget=1 << 20)
    nb = half_rows // tr
    if split == "rows":
        out_shape, out_spec = (2 * half_rows, width), pl.BlockSpec((tr, width), lambda i, pr: (pr[1] * nb + i, 0))
    else:
        out_shape, out_spec = (half_rows, 2 * width), pl.BlockSpec((tr, width), lambda i, pr: (i, pr[1]))

    def body(place_ref, q_ref, r_ref, o_ref):
        acc = q_ref[...].astype(F32)
        for j in range(3):
            acc = acc + r_ref[j].astype(F32)
        o_ref[...] = acc

    return pl.pallas_call(
        body, name=name, out_shape=jax.ShapeDtypeStruct(out_shape, F32),
        grid_spec=pltpu.PrefetchScalarGridSpec(
            num_scalar_prefetch=1, grid=(nb,),
            in_specs=[pl.BlockSpec((None, tr, width), lambda i, pr: (pr[0], i, 0)),
                      pl.BlockSpec((3, tr, width), lambda i, pr: (0, i, 0))],
            out_specs=out_spec),
        compiler_params=_cparams(("parallel",)),
    )(place, mine, others)


WEIGHTS = ["meta_tokens", "a_norm_pre", "a_w_in", "a_conv_w", "a_conv_b", "a_dt_bias", "a_a_log", "a_d_skip",
           "a_gate_norm", "a_w_out", "a_norm_post", "kv_norm", "w_kv", "b_norm_pre", "b_w_q", "b_sinks", "b_w_o",
           "b_norm_post", "f_norm_pre", "f_w_up", "f_conv_w", "f_conv_b", "f_w_down", "f_norm_post"]
FULL_SHAPE = {
    "meta_tokens": (16, 1024), "a_norm_pre": (1, 1024), "a_w_in": (1, 1024, 5152), "a_conv_w": (1, 4, 3072),
    "a_conv_b": (1, 3072), "a_dt_bias": (1, 32), "a_a_log": (1, 32), "a_d_skip": (1, 32), "a_gate_norm": (1, 2048),
    "a_w_out": (1, 2048, 1024), "a_norm_post": (1, 1024), "kv_norm": (1024,), "w_kv": (1024, 512),
    "b_norm_pre": (1, 1024), "b_w_q": (1, 1024, 1024), "b_sinks": (1, 16), "b_w_o": (1, 1024, 1024),
    "b_norm_post": (1, 1024), "f_norm_pre": (2, 1024), "f_w_up": (2, 1024, 5632), "f_conv_w": (2, 3, 5632),
    "f_conv_b": (2, 5632), "f_w_down": (2, 2816, 1024), "f_norm_post": (2, 1024),
}
SHARD_AXIS = {
    "meta_tokens": 1, "a_norm_pre": 1, "a_w_in": 2, "a_conv_w": 2, "a_conv_b": 1, "a_dt_bias": None, "a_a_log": None,
    "a_d_skip": None, "a_gate_norm": 1, "a_w_out": 1, "a_norm_post": 1, "kv_norm": None, "w_kv": 0, "b_norm_pre": None,
    "b_w_q": 1, "b_sinks": None, "b_w_o": 1, "b_norm_post": None, "f_norm_pre": None, "f_w_up": 2, "f_conv_w": 2,
    "f_conv_b": None, "f_w_down": 1, "f_norm_post": None,
}
BIG = ["a_w_in", "a_w_out", "w_kv", "b_w_q", "b_w_o", "f_w_up", "f_w_down"]
SMALL = [n for n in WEIGHTS if n not in BIG]
SMALL_SHARDED = [n for n in SMALL if SHARD_AXIS[n] is not None]


def _shard_shape(name):
    shape = list(FULL_SHAPE[name])
    if SHARD_AXIS[name] is not None:
        shape[SHARD_AXIS[name]] //= N_CHIPS
    return tuple(shape)


def _numel(shape):
    return int(math.prod(shape))


SUBLANES = 8


def _packed_rows(shape):
    rows = -(-_numel(shape) // LANES)
    return -(-rows // SUBLANES) * SUBLANES


def _pack(arrays):
    parts = []
    for a in arrays:
        size, rows = _numel(a.shape), _packed_rows(a.shape)
        if size % LANES == 0:
            part = jnp.pad(a.reshape(size // LANES, LANES), ((0, rows - size // LANES), (0, 0)))
        else:
            part = jnp.pad(a.reshape(-1), (0, rows * LANES - size)).reshape(rows, LANES)
        parts.append(part)
    return jnp.concatenate(parts, axis=0)


def _unpack(packed, names, shape_of):
    out, off = {}, 0
    lead = packed.shape[:-2]
    for n in names:
        shape = tuple(shape_of(n))
        size, rows = _numel(shape), _packed_rows(shape)
        part = packed[..., off:off + rows, :]
        if size % LANES == 0:
            out[n] = part[..., :size // LANES, :].reshape(lead + shape)
        else:
            out[n] = part.reshape(lead + (rows * LANES,))[..., :size].reshape(lead + shape)
        off += rows
    return out


def _split_chips(name, full):
    ax = SHARD_AXIS[name]
    shape = full.shape
    cut = shape[:ax] + (N_CHIPS, shape[ax] // N_CHIPS) + shape[ax + 1:]
    return jnp.moveaxis(full.reshape(cut), ax, 0)


def _join_chips(name, stacked):
    ax = SHARD_AXIS[name]
    moved = jnp.moveaxis(stacked, 0, ax)
    shape = moved.shape
    return moved.reshape(shape[:ax] + (shape[ax] * shape[ax + 1],) + shape[ax + 2:])


def _as2d(a):
    return a.reshape(-1, a.shape[-1])


BUFFERS = [("a_w_in", "a_w_in", None), ("a_w_out", "a_w_out", None), ("w_kv", "w_kv", None),
           ("b_w_q", "b_w_q", None), ("b_w_o", "b_w_o", None), ("f_w_up0", "f_w_up", 0), ("f_w_up1", "f_w_up", 1),
           ("f_w_down0", "f_w_down", 0), ("f_w_down1", "f_w_down", 1)]


TRANSPOSED = ("a_w_in",)
SPLIT = {"a_w_in": "cols"}


def _local_shard(arrays, weight, layer):
    if weight in TRANSPOSED:
        return arrays[weight][0].T
    return _as2d(arrays[weight]) if layer is None else arrays[weight]


def _weight_from_gathered(weight, buf):
    if weight == "f_w_up":
        return buf
    return buf.reshape(N_CHIPS * buf.shape[1], buf.shape[2])


def _gathered_from_grad(weight, g):
    if weight == "f_w_up":
        return g
    return g.reshape(N_CHIPS, g.shape[0] // N_CHIPS, g.shape[1]).astype(BF16)


GATHER_SCHEDULE = {
    "a_in_main": [("ici", ["a_w_out"])],
    "a_conv": [("d2d", ["a_w_out"]), ("ici", ["f_w_down0"])],
    "a_ssd_prep": [("d2d", ["f_w_down0"]), ("ici_near", ["f_w_up0"])],
    "a_ssd": [("ici_far", ["f_w_up0"])],
    "a_gate": [("d2d", ["f_w_up0"]), ("ici", ["w_kv", "b_w_q", "b_w_o"])],
    "ffn0_up": [("d2d", ["w_kv", "b_w_q", "b_w_o"]), ("ici", ["f_w_down1"])],
    "ffn0_down": [("d2d", ["f_w_down1"])],
    "b_attn": [("ici", ["f_w_up1"])],
    "b_o": [("d2d", ["f_w_up1"])],
}
REDUCE_SCHEDULE = {
    "b_attn_bwd": [("all", ["f_w_down1", "f_w_up1", "b_w_o"])],
    "ffn0_conv_bwd": [("all", ["b_w_q", "w_kv", "f_w_down0"])],
    "a_ssd_bwd": [("all", ["f_w_up0", "a_w_out"])],
    "a_in_main_dx": [("near", ["a_w_in"])],
    "a_norm_bwd": [("far", ["a_w_in"])],
}
REDUCE_LAST = ("a_w_in",)
ICI_PEERS = {"ici": ALL_PEERS, "ici_near": NEAR_PEERS, "ici_far": FAR_PEERS,
             "all": ALL_PEERS, "near": NEAR_PEERS, "far": FAR_PEERS}
PAIR_SCHEDULE = {
    "b_o_dx": ["f_w_down1", "f_w_up1", "b_w_o"],
    "ffn0_down_dx": ["b_w_q", "w_kv", "f_w_down0"],
    "a_out_dx": ["f_w_up0", "a_w_out"],
    "a_in_dt_dx": ["a_w_in"],
}
SWAP_SCHEDULE = {"a_in_main_dw": ["f_w_down1", "f_w_up1", "b_w_o", "b_w_q", "w_kv", "f_w_down0", "f_w_up0", "a_w_out"]}


def _buffer_of(weight, layer):
    return weight if layer is None else f"{weight}{layer}"


class _Pipeline:
    def __init__(self, place, slots):
        self.place = place
        self.slots = dict(slots)
        self.running = []
        self.grads = {}
        self.theirs = {}
        self.partials = {}
        self.peers = {}
        self.reduced = {}

    def _collect(self):
        for step, buffers, table in self.running:
            table.update(zip(buffers, step.results))
        self.running = []

    @staticmethod
    def _splits(buffers):
        return [SPLIT.get(b, "rows") for b in buffers]

    def gather_now(self, name, buffers):
        step = _step_gather_full([self.slots[b] for b in buffers], self._splits(buffers))
        _run_steps(name, [step])
        self.slots.update(zip(buffers, step.results))

    def weight(self, name, layer=None):
        self._collect()
        return _weight_from_gathered(name, self.slots[_buffer_of(name, layer)])

    def grad(self, name, layer, g):
        self.grads[_buffer_of(name, layer)] = _gathered_from_grad(name, g)

    def steps(self, kernel):
        self._collect()
        steps = []
        for phase, buffers in GATHER_SCHEDULE.get(kernel, []):
            bufs, splits = [self.slots[b] for b in buffers], self._splits(buffers)
            step = (_step_gather_d2d(bufs, splits) if phase == "d2d"
                    else _step_gather_ici(bufs, splits, ICI_PEERS[phase]))
            self.running.append((step, buffers, self.slots))
            steps.append(step)
        buffers = PAIR_SCHEDULE.get(kernel)
        if buffers:
            step = _step_pair_exchange([self.grads[b] for b in buffers], self._splits(buffers))
            self.running.append((step, buffers, self.theirs))
            steps.append(step)
        for part, buffers in REDUCE_SCHEDULE.get(kernel, []):
            for b in buffers:
                if b not in self.partials:
                    self.partials[b] = _rs_pair_add("reduce_pair_add_" + b, self.place, self.grads[b], self.theirs[b],
                                                    SPLIT.get(b, "rows"))
            started = [self.peers[b] for b in buffers] if all(b in self.peers for b in buffers) else None
            step = _step_chip_exchange([self.partials[b] for b in buffers], ICI_PEERS[part], into=started)
            self.running.append((step, buffers, self.peers))
            steps.append(step)
        buffers = SWAP_SCHEDULE.get(kernel)
        if buffers:
            step = self._swap_step(buffers)
            self.running.append((step, buffers, self.reduced))
            steps.append(step)
        return steps

    def _swap_step(self, buffers):
        halves = [_rs_chip_add("reduce_chip_add_" + b, self.place, self.partials[b], self.peers[b], SPLIT.get(b, "rows"))
                  for b in buffers]
        return _step_pair_gather(halves, self._splits(buffers))

    def shard(self, buffer):
        self._collect()
        return self.reduced[buffer]

    def finish(self):
        self._collect()
        rest = [b for b, _, _ in BUFFERS if b not in self.reduced]
        step = self._swap_step(rest)
        _run_steps("reduce_pair_gather", [step])
        self.reduced.update(zip(rest, step.results))


def kernel(x, meta_tokens, a_norm_pre, a_w_in, a_conv_w, a_conv_b, a_dt_bias, a_a_log, a_d_skip, a_gate_norm, a_w_out, a_norm_post, kv_norm, w_kv, b_norm_pre, b_w_q, b_sinks, b_w_o, b_norm_post, f_norm_pre, f_w_up, f_conv_w, f_conv_b, f_w_down, f_norm_post, loss_target, m_meta_tokens, m_a_norm_pre, m_a_w_in, m_a_conv_w, m_a_conv_b, m_a_dt_bias, m_a_a_log, m_a_d_skip, m_a_gate_norm, m_a_w_out, m_a_norm_post, m_kv_norm, m_w_kv, m_b_norm_pre, m_b_w_q, m_b_sinks, m_b_w_o, m_b_norm_post, m_f_norm_pre, m_f_w_up, m_f_conv_w, m_f_conv_b, m_f_w_down, m_f_norm_post, v_meta_tokens, v_a_norm_pre, v_a_w_in, v_a_conv_w, v_a_conv_b, v_a_dt_bias, v_a_a_log, v_a_d_skip, v_a_gate_norm, v_a_w_out, v_a_norm_post, v_kv_norm, v_w_kv, v_b_norm_pre, v_b_w_q, v_b_sinks, v_b_w_o, v_b_norm_post, v_f_norm_pre, v_f_w_up, v_f_conv_w, v_f_conv_b, v_f_w_down, v_f_norm_post):
    given = dict(locals())
    w = {n: given[n] for n in WEIGHTS}
    mom = {n: given["m_" + n] for n in WEIGHTS}
    var = {n: given["v_" + n] for n in WEIGHTS}
    chip = 2 * lax.axis_index("x") + lax.axis_index("y")
    core = lax.axis_index("c")
    place = jnp.stack([chip, core]).astype(jnp.int32)

    small_all = _allgather_small("gather_small", _pack([w[n] for n in SMALL_SHARDED]))
    small_parts = _unpack(small_all, SMALL_SHARDED, _shard_shape)
    slots = {b: _cast_into_slot("cast_" + b, place, _local_shard(w, wn, layer), layer) for b, wn, layer in BUFFERS}
    pipeline = _Pipeline(place, slots)
    pipeline.gather_now("gather_first", ["a_w_in"])
    p = {}
    for n in SMALL:
        p[n] = _join_chips(n, small_parts[n]) if n in SMALL_SHARDED else w[n]
    p["a_conv_w"] = p["a_conv_w"][0]
    p["kv_norm"] = p["kv_norm"].reshape(1, D_MODEL)

    loss_local, grad_x, g = _local_step(x[0], loss_target[0], p, pipeline)

    small_sum = _allreduce_small("reduce_small", _pack([g[n].reshape(FULL_SHAPE[n]) for n in SMALL]
                                                       + [loss_local.reshape(1, 1)]))
    small_red = _unpack(small_sum, SMALL + ["loss"], lambda n: (1, 1) if n == "loss" else FULL_SHAPE[n])
    loss = small_red["loss"][0, 0]
    grads = {}
    for n in SMALL:
        if SHARD_AXIS[n] is None:
            grads[n] = small_red[n]
        else:
            grads[n] = lax.dynamic_index_in_dim(_split_chips(n, small_red[n]), chip, 0, keepdims=False)

    delta, new_m, new_v = {}, {}, {}
    for n in sorted(BIG, key=lambda name: name in REDUCE_LAST):
        shape = _shard_shape(n)
        if n in REDUCE_LAST:
            pipeline.finish()
        if n in TRANSPOSED:
            g2d = pipeline.shard(n)
            w2d, m2d, v2d = (arrays[n][0].T for arrays in (w, mom, var))
            back = lambda a: a.T.reshape(shape)
        else:
            g2d = (jnp.concatenate([pipeline.shard(n + "0"), pipeline.shard(n + "1")], axis=0)
                   if n in ("f_w_up", "f_w_down") else pipeline.shard(n))
            w2d, m2d, v2d = (_as2d(arrays[n]) for arrays in (w, mom, var))
            back = lambda a: a.reshape(shape)
        d, m2, v2 = _adamw("adamw_" + n, w2d, g2d, m2d, v2d, steps=pipeline.steps("adamw_" + n))
        grads[n], delta[n], new_m[n], new_v[n] = back(g2d), back(d), back(m2), back(v2)
    at_least_2d = lambda n: (1,) * (2 - len(_shard_shape(n))) + _shard_shape(n)
    outs = _adamw_small("adamw_small", *[[src[n].reshape(at_least_2d(n)) for n in SMALL] for src in (w, grads, mom, var)])
    for dst, arrays in zip((delta, new_m, new_v), outs):
        dst.update({n: a.reshape(_shard_shape(n)) for n, a in zip(SMALL, arrays)})

    return (loss, grad_x[None], *[grads[n].reshape(_shard_shape(n)) for n in WEIGHTS],
            *[delta[n] for n in WEIGHTS], *[new_m[n] for n in WEIGHTS], *[new_v[n] for n in WEIGHTS])
```

```python
import functools
import math

import jax
import jax.numpy as jnp
from jax import lax
from jax.experimental import pallas as pl
from jax.experimental.pallas import tpu as pltpu

F32, BF16 = jnp.float32, jnp.bfloat16
MESH = pl.DeviceIdType.MESH

D_MODEL = 1024
N_META = 16
CHUNK = 128
PAD_ROWS = CHUNK - N_META
D_INNER = 2048
D_STATE = 128
N_GROUPS = 4
HEADS_PER_GROUP = 8
SSM_HEADS = 32
HEAD_DIM = 64
D_BC = N_GROUPS * D_STATE
D_XBC = D_INNER + 2 * D_BC
D_MAIN = D_INNER + D_XBC
D_IN_PROJ = D_MAIN + SSM_HEADS
GROUP_W = HEADS_PER_GROUP * HEAD_DIM
SSM_CONV = 4
D_FF = 2816
FFN_CONV = 3
N_Q_HEADS = 16
N_KV_HEADS = 4
D_KV = 256
ATTN_SCALE = 1.0 / math.sqrt(HEAD_DIM)
RMS_EPS = 1e-6
NEG_INF = -1e30
LANES = 128
VMEM_LIMIT = 54 * 1024 * 1024

ADAM_LR, ADAM_B1, ADAM_B2, ADAM_EPS, ADAM_WD, ADAM_STEP = 0.001, 0.9, 0.999, 1e-08, 0.01, 10

N_CHIPS = 4


def _cparams(sem=None):
    return pltpu.CompilerParams(dimension_semantics=sem, vmem_limit_bytes=VMEM_LIMIT)


def _tile(n, cands=(512, 256, 128)):
    for t in cands:
        if n % t == 0:
            return t
    return n


def _row_tile(rows, width):
    for t in (544, 272):
        if rows % t == 0 and t * width * 4 <= (3 << 20):
            return t
    return 128


def _rows_mask(i, tm):
    rows = i * tm + lax.broadcasted_iota(jnp.int32, (tm, 1), 0)
    return rows >= PAD_ROWS


def _dot(a, b):
    return jnp.dot(a, b, preferred_element_type=F32)


def _dot_nt(a, b):
    return lax.dot_general(a, b, (((1,), (1,)), ((), ())), preferred_element_type=F32)


def _dot_tn(a, b):
    return lax.dot_general(a, b, (((0,), (0,)), ((), ())), preferred_element_type=F32)


def _sigmoid(x):
    return 1.0 / (1.0 + jnp.exp(-x))


def _place():
    return lax.axis_index("x"), lax.axis_index("y"), lax.axis_index("c")


def _other_chips(x, y):
    return [(1 - x, y), (x, 1 - y), (1 - x, 1 - y)]


class _Step:
    def __init__(self, ins, outs, aliases, n_sems, start, finish):
        self.ins, self.outs, self.aliases, self.n_sems = list(ins), list(outs), dict(aliases), n_sems
        self.start, self.finish = start, finish
        self.results = None


def _like(a):
    return jax.ShapeDtypeStruct(a.shape, a.dtype)


def _remote(src, dst, send_sems, recv_sems, k, device):
    return pltpu.make_async_remote_copy(src, dst, send_sems.at[k], recv_sems.at[k], device_id=device, device_id_type=MESH)


def _half(ref, split, which, lead=()):
    if split == "rows":
        hr = ref.shape[-2] // 2
        return ref.at[lead + (pl.ds(which * hr, hr),)]
    hc = ref.shape[-1] // 2
    return ref.at[lead + (slice(None), pl.ds(which * hc, hc))]


def _splits(bufs, splits):
    return list(splits) if splits is not None else ["rows"] * len(bufs)


ALL_PEERS = (0, 1, 2)
NEAR_PEERS = (0, 1)
FAR_PEERS = (2,)


def _step_gather_ici(bufs, splits=None, peers=ALL_PEERS):
    splits = _splits(bufs, splits)

    def copies(outs, send_sems, recv_sems, received):
        x, y, c = _place()
        me = 2 * x + y
        for k, o in enumerate(outs):
            for j, (cx, cy) in enumerate(_other_chips(x, y)):
                if j in peers:
                    part = _half(o, splits[k], c, (2 * cx + cy if received else me,))
                    yield _remote(part, part, send_sems, recv_sems, 3 * k + j, (cx, cy, c))

    def start(ins, outs, send_sems, recv_sems):
        for cp in copies(outs, send_sems, recv_sems, False):
            cp.start()

    def finish(ins, outs, send_sems, recv_sems):
        for cp in copies(outs, send_sems, recv_sems, True):
            cp.wait_recv()
        for cp in copies(outs, send_sems, recv_sems, False):
            cp.wait_send()

    return _Step(bufs, [_like(b) for b in bufs], {k: k for k in range(len(bufs))}, 3 * len(bufs), start, finish)


def _step_gather_d2d(bufs, splits=None):
    splits = _splits(bufs, splits)

    def copies(outs, send_sems, recv_sems, received):
        x, y, c = _place()
        for k, o in enumerate(outs):
            for j, (cx, cy) in enumerate(_other_chips(x, y)):
                part = _half(o, splits[k], 1 - c if received else c, (2 * cx + cy,))
                yield _remote(part, part, send_sems, recv_sems, 3 * k + j, (x, y, 1 - c))

    def start(ins, outs, send_sems, recv_sems):
        for cp in copies(outs, send_sems, recv_sems, False):
            cp.start()

    def finish(ins, outs, send_sems, recv_sems):
        for cp in copies(outs, send_sems, recv_sems, True):
            cp.wait_recv()
        for cp in copies(outs, send_sems, recv_sems, False):
            cp.wait_send()

    return _Step(bufs, [_like(b) for b in bufs], {k: k for k in range(len(bufs))}, 3 * len(bufs), start, finish)


def _step_gather_full(bufs, splits=None):
    n = len(bufs)
    splits = _splits(bufs, splits)

    def ici(outs, send_sems, recv_sems, received):
        x, y, c = _place()
        me = 2 * x + y
        for k, o in enumerate(outs):
            for j, (cx, cy) in enumerate(_other_chips(x, y)):
                part = _half(o, splits[k], c, (2 * cx + cy if received else me,))
                yield _remote(part, part, send_sems, recv_sems, 3 * k + j, (cx, cy, c))

    def d2d(outs, send_sems, recv_sems, received):
        x, y, c = _place()
        for k, o in enumerate(outs):
            for j, (cx, cy) in enumerate(_other_chips(x, y)):
                part = _half(o, splits[k], 1 - c if received else c, (2 * cx + cy,))
                yield _remote(part, part, send_sems, recv_sems, 3 * n + 3 * k + j, (x, y, 1 - c))

    def start(ins, outs, send_sems, recv_sems):
        for cp in ici(outs, send_sems, recv_sems, False):
            cp.start()

    def finish(ins, outs, send_sems, recv_sems):
        for arrived, onward in zip(ici(outs, send_sems, recv_sems, True), d2d(outs, send_sems, recv_sems, False)):
            arrived.wait_recv()
            onward.start()
        for cp in d2d(outs, send_sems, recv_sems, True):
            cp.wait_recv()
        for cp in ici(outs, send_sems, recv_sems, False):
            cp.wait_send()
        for cp in d2d(outs, send_sems, recv_sems, False):
            cp.wait_send()

    return _Step(bufs, [_like(b) for b in bufs], {k: k for k in range(n)}, 6 * n, start, finish)


def _half_shape(shape, split):
    return shape[:-2] + ((shape[-2] // 2, shape[-1]) if split == "rows" else (shape[-2], shape[-1] // 2))


def _step_pair_exchange(grads, splits=None):
    splits = _splits(grads, splits)

    def copies(ins, outs, send_sems, recv_sems):
        x, y, c = _place()
        for k, (g, o) in enumerate(zip(ins, outs)):
            yield _remote(_half(g, splits[k], 1 - c, (slice(None),)), o, send_sems, recv_sems, k, (x, y, 1 - c))

    def start(ins, outs, send_sems, recv_sems):
        for cp in copies(ins, outs, send_sems, recv_sems):
            cp.start()

    def finish(ins, outs, send_sems, recv_sems):
        for cp in copies(ins, outs, send_sems, recv_sems):
            cp.wait()

    outs = [jax.ShapeDtypeStruct(_half_shape(g.shape, s), g.dtype) for g, s in zip(grads, splits)]
    return _Step(grads, outs, {}, len(grads), start, finish)


def _step_chip_exchange(partials, peers=ALL_PEERS, into=None):
    n = len(partials)

    def copies(ins, outs, send_sems, recv_sems):
        x, y, c = _place()
        for k, (q, o) in enumerate(zip(ins[:n], outs)):
            for j, (cx, cy) in enumerate(_other_chips(x, y)):
                if j in peers:
                    yield _remote(q.at[2 * cx + cy], o.at[j], send_sems, recv_sems, 3 * k + j, (cx, cy, c))

    def start(ins, outs, send_sems, recv_sems):
        for cp in copies(ins, outs, send_sems, recv_sems):
            cp.start()

    def finish(ins, outs, send_sems, recv_sems):
        for cp in copies(ins, outs, send_sems, recv_sems):
            cp.wait()

    outs = [jax.ShapeDtypeStruct((3,) + q.shape[1:], q.dtype) for q in partials]
    if into is None:
        return _Step(partials, outs, {}, 3 * n, start, finish)
    return _Step(list(partials) + list(into), outs, {n + k: k for k in range(n)}, 3 * n, start, finish)


def _step_pair_gather(shards, splits=None):
    splits = _splits(shards, splits)

    def copies(outs, send_sems, recv_sems, received):
        x, y, c = _place()
        for k, o in enumerate(outs):
            part = _half(o, splits[k], 1 - c if received else c)
            yield _remote(part, part, send_sems, recv_sems, k, (x, y, 1 - c))

    def start(ins, outs, send_sems, recv_sems):
        for cp in copies(outs, send_sems, recv_sems, False):
            cp.start()

    def finish(ins, outs, send_sems, recv_sems):
        for cp in copies(outs, send_sems, recv_sems, True):
            cp.wait_recv()
        for cp in copies(outs, send_sems, recv_sems, False):
            cp.wait_send()

    return _Step(shards, [_like(s) for s in shards], {k: k for k in range(len(shards))}, len(shards), start, finish)


def _call(body, *, name, out_shape, grid, in_specs, out_specs, operands, scratch_shapes=(), semantics=None, steps=()):
    single = not isinstance(out_shape, (tuple, list))
    out_shapes = [out_shape] if single else list(out_shape)
    out_spec_list = [out_specs] if single else list(out_specs)
    steps = list(steps)
    if not steps:
        res = pl.pallas_call(body, name=name, out_shape=out_shapes, grid=grid, in_specs=list(in_specs),
                             out_specs=out_spec_list, scratch_shapes=list(scratch_shapes),
                             compiler_params=_cparams(semantics))(*operands)
        return res[0] if single else res
    n_in, n_out, n_scr = len(operands), len(out_shapes), len(scratch_shapes)
    x_in = [a for s in steps for a in s.ins]
    x_out = [o for s in steps for o in s.outs]
    aliases, in_off, out_off = {}, 0, 0
    for s in steps:
        for i, o in s.aliases.items():
            aliases[n_in + in_off + i] = n_out + out_off + o
        in_off += len(s.ins)
        out_off += len(s.outs)
    sems = []
    for s in steps:
        sems += [pltpu.SemaphoreType.DMA((s.n_sems,)), pltpu.SemaphoreType.DMA((s.n_sems,))]
    any_spec = pl.BlockSpec(memory_space=pl.ANY)

    def carried(*refs):
        pos = 0
        ins = refs[pos:pos + n_in]; pos += n_in
        xi = refs[pos:pos + len(x_in)]; pos += len(x_in)
        outs = refs[pos:pos + n_out]; pos += n_out
        xo = refs[pos:pos + len(x_out)]; pos += len(x_out)
        scr = refs[pos:pos + n_scr]; pos += n_scr
        sem_refs = refs[pos:]

        def each(action):
            i0 = o0 = 0
            for k, s in enumerate(steps):
                getattr(s, action)(xi[i0:i0 + len(s.ins)], xo[o0:o0 + len(s.outs)], sem_refs[2 * k], sem_refs[2 * k + 1])
                i0 += len(s.ins)
                o0 += len(s.outs)

        if grid:
            first = functools.reduce(jnp.logical_and, [pl.program_id(d) == 0 for d in range(len(grid))])
            last = functools.reduce(jnp.logical_and, [pl.program_id(d) == grid[d] - 1 for d in range(len(grid))])
            pl.when(first)(lambda: each("start"))
            body(*ins, *outs, *scr)
            pl.when(last)(lambda: each("finish"))
        else:
            each("start")
            body(*ins, *outs, *scr)
            each("finish")

    res = pl.pallas_call(
        carried, name=name, out_shape=out_shapes + x_out, grid=grid,
        in_specs=list(in_specs) + [any_spec] * len(x_in), out_specs=out_spec_list + [any_spec] * len(x_out),
        scratch_shapes=list(scratch_shapes) + sems, input_output_aliases=aliases,
        compiler_params=_cparams(None if semantics is None else ("arbitrary",) * len(grid)),
    )(*operands, *x_in)
    o0 = n_out
    for s in steps:
        s.results = list(res[o0:o0 + len(s.outs)])
        o0 += len(s.outs)
    return res[0] if single else tuple(res[:n_out])


def _run_steps(name, steps):
    _call(lambda: None, name=name, out_shape=[], grid=(), in_specs=[], out_specs=[], operands=[], steps=steps)
    return [s.results for s in steps]


def _mm(name, a, b, mode, out_dtype=F32, acc=None, b_colblock=0, k_rows=None, out_rows=None, steps=()):
    resident_bytes = 8 << 20
    if mode == "nn":
        m, k = a.shape
        n = b.shape[1]
        tm = m
        while tm * k * 2 > resident_bytes and tm % 32 == 0:
            tm //= 2
        tn = _tile(n)
        grid = (m // tm, n // tn)
        in_specs = [pl.BlockSpec((tm, k), lambda i, j: (i, 0)), pl.BlockSpec((k, tn), lambda i, j: (0, j))]
        out_shape, out_block = (m, n), (tm, tn)
    elif mode == "nt":
        m, n = a.shape
        k = k_rows or b.shape[0]
        tm = m
        while tm * n * 2 > resident_bytes and tm % 32 == 0:
            tm //= 2
        tk = _tile(k)
        grid = (m // tm, k // tk)
        in_specs = [pl.BlockSpec((tm, n), lambda i, j: (i, 0)), pl.BlockSpec((tk, n), lambda i, j: (j, b_colblock))]
        out_shape, out_block = (m, k), (tm, tk)
    else:
        m, k = a.shape
        n = b.shape[1]
        tk, tn = _tile(k), (n if m * n * 2 <= resident_bytes else _tile(n))
        grid = (k // tk, n // tn)
        in_specs = [pl.BlockSpec((m, tk), lambda i, j: (0, i)), pl.BlockSpec((m, tn), lambda i, j: (0, j))]
        out_shape, out_block = (out_rows or k, n), (tk, tn)
    out_spec = pl.BlockSpec(out_block, lambda i, j: (i, j))
    has_acc = acc is not None

    def body(*refs):
        a_ref, b_ref = refs[0], refs[1]
        o_ref = refs[-1]
        av, bv = a_ref[...], b_ref[...]
        if mode == "nn":
            r = _dot(av, bv)
        elif mode == "nt":
            r = _dot_nt(av, bv)
        else:
            r = _dot_tn(av, bv)
        if has_acc:
            r = r + refs[2][...]
        o_ref[...] = r.astype(o_ref.dtype)

    operands = [a, b]
    if has_acc:
        in_specs = in_specs + [out_spec]
        operands.append(acc)
    return _call(body, name=name, out_shape=jax.ShapeDtypeStruct(out_shape, out_dtype), grid=grid, in_specs=in_specs,
                 out_specs=out_spec, operands=operands, semantics=("parallel", "parallel"), steps=steps)


def _tn_rows_into(name, a, b, into, row0, nrows):
    m, k = a.shape
    n = b.shape[1]

    def body(a_ref, b_ref, into_ref, o_ref):
        o_ref[...] = _dot_tn(a_ref[...], b_ref[...])[0:nrows].astype(o_ref.dtype)

    return pl.pallas_call(
        body, name=name, out_shape=jax.ShapeDtypeStruct(into.shape, into.dtype), grid=(1,),
        in_specs=[pl.BlockSpec((m, k), lambda i: (0, 0)), pl.BlockSpec((m, n), lambda i: (0, 0)),
                  pl.BlockSpec(memory_space=pl.ANY)],
        out_specs=pl.BlockSpec((nrows, n), lambda i: (row0 // nrows, 0)),
        input_output_aliases={2: 0}, compiler_params=_cparams(("arbitrary",)),
    )(a, b, into)


def _rms_fwd(name, h, w):
    rows, width = h.shape
    tm = _row_tile(rows, width)

    def body(h_ref, w_ref, o_ref):
        x = h_ref[...]
        r = lax.rsqrt(jnp.mean(x * x, axis=-1, keepdims=True) + RMS_EPS)
        o_ref[...] = (x * r * w_ref[...]).astype(BF16)

    return pl.pallas_call(
        body, name=name, out_shape=jax.ShapeDtypeStruct((rows, width), BF16), grid=(rows // tm,),
        in_specs=[pl.BlockSpec((tm, width), lambda i: (i, 0)), pl.BlockSpec((1, width), lambda i: (0, 0))],
        out_specs=pl.BlockSpec((tm, width), lambda i: (i, 0)), compiler_params=_cparams(("parallel",)),
    )(h, w)


def _resid_norm_fwd(name, h, pre, w, next_norms=()):
    rows, width = h.shape
    tm = _row_tile(rows, width)
    n_next = len(next_norms)

    def body(*refs):
        h_ref, p_ref, w_ref = refs[:3]
        v_refs = refs[3:3 + n_next]
        o_ref = refs[3 + n_next]
        n_refs = refs[4 + n_next:]
        p = p_ref[...]
        r = lax.rsqrt(jnp.mean(p * p, axis=-1, keepdims=True) + RMS_EPS)
        x = h_ref[...] + jnp.where(_rows_mask(pl.program_id(0), tm), p * r * w_ref[...], 0.0)
        o_ref[...] = x
        if n_next:
            rx = lax.rsqrt(jnp.mean(x * x, axis=-1, keepdims=True) + RMS_EPS)
            for v_ref, n_ref in zip(v_refs, n_refs):
                n_ref[...] = (x * rx * v_ref[...]).astype(BF16)

    row_spec = pl.BlockSpec((tm, width), lambda i: (i, 0))
    vec_spec = pl.BlockSpec((1, width), lambda i: (0, 0))
    outs = pl.pallas_call(
        body, name=name,
        out_shape=[jax.ShapeDtypeStruct((rows, width), F32)] + [jax.ShapeDtypeStruct((rows, width), BF16)] * n_next,
        grid=(rows // tm,), in_specs=[row_spec, row_spec, vec_spec] + [vec_spec] * n_next,
        out_specs=[row_spec] * (1 + n_next), compiler_params=_cparams(("parallel",)),
    )(h, pre, w, *next_norms)
    return outs[0], list(outs[1:])


def _resid_norm_loss(name, h, pre, w, target):
    rows, width = h.shape

    def body(h_ref, p_ref, w_ref, t_ref, dh_ref, loss_ref, dp_ref, dw_ref):
        i = pl.program_id(0)
        p = p_ref[...]
        r = lax.rsqrt(jnp.mean(p * p, axis=-1, keepdims=True) + RMS_EPS)
        x = h_ref[...] + p * r * w_ref[...]
        real = (i + jnp.zeros((CHUNK, 1), jnp.int32)) >= 1
        diff = jnp.where(real, x - t_ref[...], 0.0)
        dh = diff * (1.0 / D_MODEL)
        dh_ref[...] = dh
        dp, dw_rows = _rms_bwd(dh, p, w_ref[...])
        dp_ref[...] = dp.astype(BF16)

        @pl.when(i == 0)
        def _():
            loss_ref[...] = jnp.zeros_like(loss_ref)
            dw_ref[...] = jnp.zeros_like(dw_ref)

        loss_ref[...] += jnp.sum(diff * diff) * (0.5 / D_MODEL)
        dw_ref[...] += jnp.sum(dw_rows, axis=0, keepdims=True)

    blk = pl.BlockSpec((CHUNK, width), lambda i: (i, 0))
    vec_spec = pl.BlockSpec((1, width), lambda i: (0, 0))
    return pl.pallas_call(
        body, name=name,
        out_shape=(jax.ShapeDtypeStruct((rows, width), F32), jax.ShapeDtypeStruct((1, LANES), F32),
                   jax.ShapeDtypeStruct((rows, width), BF16), jax.ShapeDtypeStruct((1, width), F32)),
        grid=(rows // CHUNK,),
        in_specs=[blk, blk, vec_spec, pl.BlockSpec((CHUNK, width), lambda i: (jnp.maximum(i - 1, 0), 0))],
        out_specs=(blk, pl.BlockSpec((1, LANES), lambda i: (0, 0)), blk, vec_spec),
        compiler_params=_cparams(("arbitrary",)),
    )(h, pre, w, target)


def _rms_bwd(dy, x, w):
    r = lax.rsqrt(jnp.mean(x * x, axis=-1, keepdims=True) + RMS_EPS)
    xhat = x * r
    dxhat = dy * w
    return r * (dxhat - xhat * jnp.mean(dxhat * xhat, axis=-1, keepdims=True)), dy * xhat


def _norm_bwd_add(name, dh, dhn, h, w, then=None, split_first_block=False, steps=()):
    rows, width = dh.shape
    tm = CHUNK if split_first_block else _row_tile(rows, width)
    fused = then is not None
    assert not (fused and split_first_block)

    def body(*refs):
        dh_ref, dhn_ref, h_ref, w_ref = refs[:4]
        o_ref, dw_ref = refs[6:8] if fused else refs[-2:]
        i = pl.program_id(0)
        valid = _rows_mask(i, tm)
        dx, dw_rows = _rms_bwd(dhn_ref[...], h_ref[...], w_ref[...])
        dh_new = dh_ref[...] + jnp.where(valid, dx, 0.0)
        if split_first_block:
            first_ref = refs[4]

            @pl.when(i == 0)
            def _():
                first_ref[...] = dh_new

            @pl.when(i > 0)
            def _():
                o_ref[...] = dh_new
        else:
            o_ref[...] = dh_new

        @pl.when(i == 0)
        def _():
            dw_ref[...] = jnp.zeros_like(dw_ref)

        dw_ref[...] += jnp.sum(dw_rows, axis=0, keepdims=True)
        if fused:
            p_ref, wp_ref, dp_ref, dwp_ref = refs[4], refs[5], refs[8], refs[9]
            dp, dwp_rows = _rms_bwd(jnp.where(valid, dh_new, 0.0), p_ref[...], wp_ref[...])
            dp_ref[...] = dp.astype(BF16)

            @pl.when(i == 0)
            def _():
                dwp_ref[...] = jnp.zeros_like(dwp_ref)

            dwp_ref[...] += jnp.sum(dwp_rows, axis=0, keepdims=True)

    row_spec = pl.BlockSpec((tm, width), lambda i: (i, 0))
    vec_spec = pl.BlockSpec((1, width), lambda i: (0, 0))
    row_f32, vec_f32 = jax.ShapeDtypeStruct((rows, width), F32), jax.ShapeDtypeStruct((1, width), F32)
    in_specs, operands = [row_spec, row_spec, row_spec, vec_spec], [dh, dhn, h, w]
    out_shape, out_specs = [row_f32, vec_f32], [row_spec, vec_spec]
    if split_first_block:
        out_shape = [jax.ShapeDtypeStruct((tm, width), F32), jax.ShapeDtypeStruct((rows - tm, width), F32), vec_f32]
        out_specs = [pl.BlockSpec((tm, width), lambda i: (0, 0)),
                     pl.BlockSpec((tm, width), lambda i: (jnp.maximum(i - 1, 0), 0)), vec_spec]
    if fused:
        in_specs += [row_spec, vec_spec]
        operands += list(then)
        out_shape += [jax.ShapeDtypeStruct((rows, width), BF16), vec_f32]
        out_specs += [row_spec, vec_spec]
    return _call(body, name=name, out_shape=out_shape, grid=(rows // tm,), in_specs=in_specs, out_specs=out_specs,
                 operands=operands, semantics=("arbitrary",), steps=steps)


def _shift_down(x, s, rows):
    return pltpu.roll(x, s, 0) if s else x


def _shift_up(x, s, rows):
    return pltpu.roll(x, rows - s, 0) if s else x


def _conv4_fwd(name, zx, cw, cb, steps=()):
    rows = zx.shape[0]
    off = D_INNER // LANES

    def body(x_ref, w_ref, b_ref, o_ref):
        x = x_ref[...]
        acc = b_ref[...] + w_ref[pl.ds(SSM_CONV - 1, 1), :] * x
        for s in range(1, SSM_CONV):
            acc = acc + w_ref[pl.ds(SSM_CONV - 1 - s, 1), :] * _shift_down(x, s, rows)
        valid = lax.broadcasted_iota(jnp.int32, (rows, 1), 0) >= PAD_ROWS
        o_ref[...] = jnp.where(valid, acc * _sigmoid(acc), 0.0)

    return _call(
        body, name=name, out_shape=jax.ShapeDtypeStruct((rows, D_XBC), F32), grid=(D_XBC // LANES,),
        in_specs=[pl.BlockSpec((rows, LANES), lambda j: (0, j + off)),
                  pl.BlockSpec((SSM_CONV, LANES), lambda j: (0, j)),
                  pl.BlockSpec((1, LANES), lambda j: (0, j))],
        out_specs=pl.BlockSpec((rows, LANES), lambda j: (0, j)), operands=[zx, cw, cb],
        semantics=("parallel",), steps=steps)


def _conv4_bwd(name, zx, dout, cw, cb, into):
    rows, width = dout.shape
    zoff = D_INNER // LANES

    def body(x_ref, d_ref, w_ref, b_ref, into_ref, dx_ref, dw_ref, db_ref):
        x = x_ref[...]
        shifted = [_shift_down(x, s, rows) for s in range(SSM_CONV)]
        acc = b_ref[...]
        for s in range(SSM_CONV):
            acc = acc + w_ref[pl.ds(SSM_CONV - 1 - s, 1), :] * shifted[s]
        sig = _sigmoid(acc)
        valid = lax.broadcasted_iota(jnp.int32, (rows, 1), 0) >= PAD_ROWS
        dpre = jnp.where(valid, d_ref[...] * sig * (1.0 + acc * (1.0 - sig)), 0.0)
        dx = w_ref[pl.ds(SSM_CONV - 1, 1), :] * dpre
        for s in range(1, SSM_CONV):
            dx = dx + w_ref[pl.ds(SSM_CONV - 1 - s, 1), :] * _shift_up(dpre, s, rows)
        dx_ref[...] = dx.astype(BF16)
        for s in range(SSM_CONV):
            dw_ref[pl.ds(SSM_CONV - 1 - s, 1), :] = jnp.sum(dpre * shifted[s], axis=0, keepdims=True)
        db_ref[...] = jnp.sum(dpre, axis=0, keepdims=True)

    return pl.pallas_call(
        body, name=name,
        out_shape=(jax.ShapeDtypeStruct(into.shape, BF16), jax.ShapeDtypeStruct((SSM_CONV, width), F32),
                   jax.ShapeDtypeStruct((1, width), F32)),
        grid=(width // LANES,),
        in_specs=[pl.BlockSpec((rows, LANES), lambda j: (0, j + zoff)),
                  pl.BlockSpec((rows, LANES), lambda j: (0, j)),
                  pl.BlockSpec((SSM_CONV, LANES), lambda j: (0, j)),
                  pl.BlockSpec((1, LANES), lambda j: (0, j)),
                  pl.BlockSpec(memory_space=pl.ANY)],
        out_specs=(pl.BlockSpec((rows, LANES), lambda j: (0, j + zoff)),
                   pl.BlockSpec((SSM_CONV, LANES), lambda j: (0, j)),
                   pl.BlockSpec((1, LANES), lambda j: (0, j))),
        input_output_aliases={4: 0}, compiler_params=_cparams(("parallel",)),
    )(zx, dout, cw, cb, into)


FFN_TILE = 2 * LANES


def _ffn_up_conv(name, hn, w_up, cw, cb, steps=()):
    rows, k = hn.shape
    chip_blocks = w_up.shape[2] // LANES
    half_blocks = D_FF // LANES
    nt = D_FF // FFN_TILE

    def weight_block(offset):
        return pl.BlockSpec((None, k, LANES), lambda j: ((2 * j + offset) // chip_blocks, 0, (2 * j + offset) % chip_blocks))

    def body(a_ref, g0, g1, v0, v1, wg_ref, wv_ref, bg_ref, bv_ref, upg_ref, upv_ref, act_ref):
        a = a_ref[...]
        g = _dot(a, jnp.concatenate([g0[...], g1[...]], axis=1))
        v = _dot(a, jnp.concatenate([v0[...], v1[...]], axis=1))
        upg_ref[...] = g
        upv_ref[...] = v
        ug, uv = bg_ref[...], bv_ref[...]
        for s in range(FFN_CONV):
            ug = ug + wg_ref[pl.ds(FFN_CONV - 1 - s, 1), :] * _shift_down(g, s, rows)
            uv = uv + wv_ref[pl.ds(FFN_CONV - 1 - s, 1), :] * _shift_down(v, s, rows)
        act_ref[...] = (ug * _sigmoid(ug) * uv).astype(BF16)

    col = pl.BlockSpec((rows, FFN_TILE), lambda j: (0, j))
    wsp = lambda shift: pl.BlockSpec((FFN_CONV, FFN_TILE), lambda j: (0, j + shift))
    bsp = lambda shift: pl.BlockSpec((1, FFN_TILE), lambda j: (0, j + shift))
    half = jax.ShapeDtypeStruct((rows, D_FF), F32)
    return _call(
        body, name=name, out_shape=(half, half, jax.ShapeDtypeStruct((rows, D_FF), BF16)), grid=(nt,),
        in_specs=[pl.BlockSpec((rows, k), lambda j: (0, 0)), weight_block(0), weight_block(1),
                  weight_block(half_blocks), weight_block(half_blocks + 1), wsp(0), wsp(nt), bsp(0), bsp(nt)],
        out_specs=(col, col, col), operands=[hn, w_up, w_up, w_up, w_up, cw, cw, cb, cb],
        semantics=("parallel",), steps=steps)


def _ffn_conv_bwd(name, up_g, up_v, dact, cw, cb, hn, w_up, steps=()):
    rows, k = hn.shape
    chip_blocks = w_up.shape[2] // LANES
    nt = D_FF // LANES

    def weight_block(shift):
        return pl.BlockSpec((None, k, LANES), lambda j: ((j + shift) // chip_blocks, 0, (j + shift) % chip_blocks))

    def body(g_ref, v_ref, d_ref, wg_ref, wv_ref, bg_ref, bv_ref, upg_ref, upv_ref, hn_ref,
             dwg_ref, dwv_ref, dbg_ref, dbv_ref, dhn_ref, dup_ref, acc, hn_scr, hnt_scr, dup_scr, sems):
        j = pl.program_id(0)
        hn_copy = pltpu.make_async_copy(hn_ref, hn_scr, sems.at[0])
        dhn_copy = pltpu.make_async_copy(acc, dhn_ref, sems.at[0])

        def dup_copy(step, half):
            block, slot = step + half * nt, 2 * (step % 2) + half
            cols = pl.ds(pl.multiple_of((block % chip_blocks) * LANES, LANES), LANES)
            return pltpu.make_async_copy(dup_scr.at[slot], dup_ref.at[block // chip_blocks, :, cols], sems.at[1 + slot])

        @pl.when(j == 0)
        def _():
            hn_copy.start()
            acc[...] = jnp.zeros_like(acc)
            hn_copy.wait()
            for r in range(0, rows, LANES):
                hnt_scr[:, r:r + LANES] = hn_scr[r:r + LANES, :].T

        @pl.when(j >= 2)
        def _():
            dup_copy(j - 2, 0).wait()
            dup_copy(j - 2, 1).wait()

        g, v = g_ref[...], v_ref[...]
        gs = [_shift_down(g, s, rows) for s in range(FFN_CONV)]
        vs = [_shift_down(v, s, rows) for s in range(FFN_CONV)]
        ug, uv = bg_ref[...], bv_ref[...]
        for s in range(FFN_CONV):
            ug = ug + wg_ref[pl.ds(FFN_CONV - 1 - s, 1), :] * gs[s]
            uv = uv + wv_ref[pl.ds(FFN_CONV - 1 - s, 1), :] * vs[s]
        sig = _sigmoid(ug)
        dsig = d_ref[...] * sig
        dup = []
        for dpre, src, w_ref, dw_ref, db_ref in (
                (dsig * uv * (1.0 + ug * (1.0 - sig)), gs, wg_ref, dwg_ref, dbg_ref),
                (dsig * ug, vs, wv_ref, dwv_ref, dbv_ref)):
            dx = w_ref[pl.ds(FFN_CONV - 1, 1), :] * dpre
            for s in range(1, FFN_CONV):
                dx = dx + w_ref[pl.ds(FFN_CONV - 1 - s, 1), :] * _shift_up(dpre, s, rows)
            dup.append(dx.astype(BF16))
            for s in range(FFN_CONV):
                dw_ref[pl.ds(FFN_CONV - 1 - s, 1), :] = jnp.sum(dpre * src[s], axis=0, keepdims=True)
            db_ref[...] = jnp.sum(dpre, axis=0, keepdims=True)
        dup = jnp.concatenate(dup, axis=1)
        acc[...] += _dot_nt(dup, jnp.concatenate([upg_ref[...], upv_ref[...]], axis=1))
        dw = _dot(hnt_scr[...], dup)
        slot = 2 * (j % 2)
        dup_scr[slot] = dw[:, :LANES].astype(BF16)
        dup_scr[slot + 1] = dw[:, LANES:].astype(BF16)
        dup_copy(j, 0).start()
        dup_copy(j, 1).start()

        @pl.when(j == nt - 1)
        def _():
            dhn_copy.start()
            for step in (j - 1, j):
                dup_copy(step, 0).wait()
                dup_copy(step, 1).wait()
            dhn_copy.wait()

    col = pl.BlockSpec((rows, LANES), lambda j: (0, j))
    wsp = lambda shift: pl.BlockSpec((FFN_CONV, LANES), lambda j: (0, j + shift))
    bsp = lambda shift: pl.BlockSpec((1, LANES), lambda j: (0, j + shift))
    any_spec = pl.BlockSpec(memory_space=pl.ANY)
    dw_shape = jax.ShapeDtypeStruct((FFN_CONV, D_FF), F32)
    db_shape = jax.ShapeDtypeStruct((1, D_FF), F32)
    return _call(
        body, name=name, grid=(nt,),
        out_shape=(dw_shape, dw_shape, db_shape, db_shape, jax.ShapeDtypeStruct((rows, k), F32),
                   jax.ShapeDtypeStruct(w_up.shape, BF16)),
        in_specs=[col, col, col, wsp(0), wsp(nt), bsp(0), bsp(nt), weight_block(0), weight_block(nt), any_spec],
        out_specs=(wsp(0), wsp(0), bsp(0), bsp(0), any_spec, any_spec),
        operands=[up_g, up_v, dact, cw, cw, cb, cb, w_up, w_up, hn],
        scratch_shapes=[pltpu.VMEM((rows, k), F32), pltpu.VMEM((rows, k), BF16), pltpu.VMEM((k, rows), BF16),
                        pltpu.VMEM((4, k, LANES), BF16), pltpu.SemaphoreType.DMA((5,))],
        semantics=("arbitrary",), steps=steps)


def _dt_fwd(name, dtr, bias):
    rows = dtr.shape[0]
    tm = _row_tile(rows, LANES)

    def body(d_ref, b_ref, o_ref):
        v = d_ref[...] + b_ref[...]
        sp = jnp.maximum(v, 0.0) + jnp.log1p(jnp.exp(-jnp.abs(v)))
        lane = lax.broadcasted_iota(jnp.int32, (tm, LANES), 1)
        ok = _rows_mask(pl.program_id(0), tm) & (lane < SSM_HEADS)
        o_ref[...] = jnp.where(ok, sp, 0.0)

    return pl.pallas_call(
        body, name=name, out_shape=jax.ShapeDtypeStruct((rows, LANES), F32), grid=(rows // tm,),
        in_specs=[pl.BlockSpec((tm, LANES), lambda i: (i, 0)), pl.BlockSpec((1, LANES), lambda i: (0, 0))],
        out_specs=pl.BlockSpec((tm, LANES), lambda i: (i, 0)), compiler_params=_cparams(("parallel",)),
    )(dtr, bias)


def _dt_bwd(name, ddt, dtr, bias):
    rows = dtr.shape[0]
    tm = _row_tile(rows, LANES)

    def body(g_ref, d_ref, b_ref, o_ref, db_ref):
        i = pl.program_id(0)
        lane = lax.broadcasted_iota(jnp.int32, (tm, LANES), 1)
        ok = _rows_mask(i, tm) & (lane < SSM_HEADS)
        dv = jnp.where(ok, g_ref[...] * _sigmoid(d_ref[...] + b_ref[...]), 0.0)
        o_ref[...] = dv.astype(BF16)

        @pl.when(i == 0)
        def _():
            db_ref[...] = jnp.zeros_like(db_ref)

        db_ref[...] += jnp.sum(dv, axis=0, keepdims=True)

    row_spec = pl.BlockSpec((tm, LANES), lambda i: (i, 0))
    vec_spec = pl.BlockSpec((1, LANES), lambda i: (0, 0))
    return pl.pallas_call(
        body, name=name,
        out_shape=(jax.ShapeDtypeStruct((rows, LANES), BF16), jax.ShapeDtypeStruct((1, LANES), F32)),
        grid=(rows // tm,), in_specs=[row_spec, row_spec, vec_spec], out_specs=(row_spec, vec_spec),
        compiler_params=_cparams(("arbitrary",)),
    )(ddt, dtr, bias)


def _gate_fwd(name, y, zx, w, steps=()):
    rows = y.shape[0]
    tm = _row_tile(rows, D_INNER)

    def body(y_ref, z_ref, w_ref, o_ref):
        z = z_ref[...]
        g = y_ref[...] * (z * _sigmoid(z))
        r = lax.rsqrt(jnp.mean(g * g, axis=-1, keepdims=True) + RMS_EPS)
        o_ref[...] = (g * r * w_ref[...]).astype(BF16)

    row_spec = pl.BlockSpec((tm, D_INNER), lambda i: (i, 0))
    return _call(
        body, name=name, out_shape=jax.ShapeDtypeStruct((rows, D_INNER), BF16), grid=(rows // tm,),
        in_specs=[row_spec, row_spec, pl.BlockSpec((1, D_INNER), lambda i: (0, 0))],
        out_specs=row_spec, operands=[y, zx, w], semantics=("parallel",), steps=steps)


def _gate_bwd(name, dyn, y, zx, w):
    rows = y.shape[0]
    tm = _row_tile(rows, D_INNER)

    def body(d_ref, y_ref, z_ref, w_ref, dy_ref, dz_ref, dw_ref):
        i = pl.program_id(0)
        z, yv = z_ref[...], y_ref[...]
        sig = _sigmoid(z)
        sz = z * sig
        g = yv * sz
        r = lax.rsqrt(jnp.mean(g * g, axis=-1, keepdims=True) + RMS_EPS)
        ghat = g * r
        dn = d_ref[...]
        dghat = dn * w_ref[...]
        dg = r * (dghat - ghat * jnp.mean(dghat * ghat, axis=-1, keepdims=True))
        dy_ref[...] = dg * sz
        dz_ref[...] = (dg * yv * sig * (1.0 + z * (1.0 - sig))).astype(BF16)

        @pl.when(i == 0)
        def _():
            dw_ref[...] = jnp.zeros_like(dw_ref)

        dw_ref[...] += jnp.sum(dn * ghat, axis=0, keepdims=True)

    row_spec = pl.BlockSpec((tm, D_INNER), lambda i: (i, 0))
    vec_spec = pl.BlockSpec((1, D_INNER), lambda i: (0, 0))
    return pl.pallas_call(
        body, name=name,
        out_shape=(jax.ShapeDtypeStruct((rows, D_INNER), F32), jax.ShapeDtypeStruct((rows, D_MAIN), BF16),
                   jax.ShapeDtypeStruct((1, D_INNER), F32)),
        grid=(rows // tm,), in_specs=[row_spec, row_spec, row_spec, vec_spec],
        out_specs=(row_spec, row_spec, vec_spec), compiler_params=_cparams(("arbitrary",)),
    )(dyn, y, zx, w)


def _split3(x):
    hi = x.astype(BF16)
    r1 = x - hi.astype(F32)
    mid = r1.astype(BF16)
    lo = (r1 - mid.astype(F32)).astype(BF16)
    return hi, mid, lo


def _dot3_data_lhs(x, sel):
    sel16 = sel.astype(F32).astype(BF16)
    hi, mid, lo = _split3(x)
    return _dot(hi, sel16) + _dot(mid, sel16) + _dot(lo, sel16)


def _dot2_data_lhs(x, sel):
    sel16 = sel.astype(F32).astype(BF16)
    hi = x.astype(BF16)
    mid = (x - hi.astype(F32)).astype(BF16)
    return _dot(hi, sel16) + _dot(mid, sel16)


def _dot3_data_rhs(sel, x):
    sel16 = sel.astype(F32).astype(BF16)
    hi, mid, lo = _split3(x)
    return _dot(sel16, hi) + _dot(sel16, mid) + _dot(sel16, lo)


def _causal_masks():
    r = lax.broadcasted_iota(jnp.int32, (CHUNK, CHUNK), 0)
    c = lax.broadcasted_iota(jnp.int32, (CHUNK, CHUNK), 1)
    return r >= c, r <= c


def _expand_heads_matrix(g):
    k = lax.broadcasted_iota(jnp.int32, (LANES, GROUP_W), 0)
    j = lax.broadcasted_iota(jnp.int32, (LANES, GROUP_W), 1)
    return HEADS_PER_GROUP * g + jnp.right_shift(j, 6) == k


def _reduce_heads_matrix(g):
    j = lax.broadcasted_iota(jnp.int32, (GROUP_W, LANES), 0)
    k = lax.broadcasted_iota(jnp.int32, (GROUP_W, LANES), 1)
    return HEADS_PER_GROUP * g + jnp.right_shift(j, 6) == k


def _reduce_pair_matrix(g, p):
    j = lax.broadcasted_iota(jnp.int32, (LANES, LANES), 0)
    k = lax.broadcasted_iota(jnp.int32, (LANES, LANES), 1)
    return HEADS_PER_GROUP * g + 2 * p + jnp.right_shift(j, 6) == k


def _group_cols(ref, g, width):
    return ref.at[:, pl.ds(g * width, width)]


def _ssd_prep(name, dt, a128, steps=()):
    rows = dt.shape[0]
    nc = rows // CHUNK

    def body(dt_ref, a_ref, dte_ref, acs_ref, acst_ref):
        causal, _ = _causal_masks()
        dtv = dt_ref[...]
        acs = _dot3_data_rhs(causal, dtv) * a_ref[...]
        acst_ref[...] = acs.T[0:SSM_HEADS]
        for g in range(N_GROUPS):
            expand = _expand_heads_matrix(g)
            _group_cols(dte_ref, g, GROUP_W)[...] = _dot3_data_lhs(dtv, expand)
            _group_cols(acs_ref, g, GROUP_W)[...] = _dot3_data_lhs(acs, expand)

    blk = pl.BlockSpec((CHUNK, D_INNER), lambda c: (c, 0))
    shp = jax.ShapeDtypeStruct((rows, D_INNER), F32)
    return _call(
        body, name=name, out_shape=(shp, shp, jax.ShapeDtypeStruct((nc, SSM_HEADS, CHUNK), F32)), grid=(nc,),
        in_specs=[pl.BlockSpec((CHUNK, LANES), lambda c: (c, 0)), pl.BlockSpec((1, LANES), lambda c: (0, 0))],
        out_specs=(blk, blk, pl.BlockSpec((None, SSM_HEADS, CHUNK), lambda c: (c, 0, 0))),
        operands=[dt, a128], semantics=("parallel",), steps=steps)


def _ssd_common(x_ref, b_ref, c_ref, dte_ref, acs_ref):
    x = x_ref[...]
    dt_exp = dte_ref[...]
    acs_exp = acs_ref[...]
    tot_exp = acs_ref[pl.ds(CHUNK - 1, 1), :]
    xdt = x * dt_exp
    e_exp = jnp.exp(acs_exp)
    f_exp = jnp.exp(tot_exp - acs_exp)
    return _causal_masks(), x, dt_exp, acs_exp, tot_exp, xdt, e_exp, f_exp, b_ref[...], c_ref[...]


def _pair_decay(acs_pair, acs_row, e, causal):
    lane = lax.broadcasted_iota(jnp.int32, (CHUNK, LANES), 1)
    mine = (lane < HEAD_DIM) if e == 0 else (lane >= HEAD_DIM)
    a_l = jnp.where(mine, acs_pair, pltpu.roll(acs_pair, HEAD_DIM, 1))
    seg = a_l - acs_row
    dm = jnp.where(causal[0], jnp.exp(jnp.minimum(seg, 0.0)), 0.0)
    dmt = jnp.where(causal[1], jnp.exp(jnp.minimum(-seg, 0.0)), 0.0)
    return dm, dmt


def _ssd_specs(index_of_chunk):
    wide = pl.BlockSpec((CHUNK, D_INNER), lambda c: (index_of_chunk(c), 0))
    b_spec = pl.BlockSpec((CHUNK, D_BC), lambda c: (index_of_chunk(c), D_INNER // D_BC))
    c_spec = pl.BlockSpec((CHUNK, D_BC), lambda c: (index_of_chunk(c), D_INNER // D_BC + 1))
    rows_spec = pl.BlockSpec((None, SSM_HEADS, CHUNK), lambda c: (index_of_chunk(c), 0, 0))
    state_spec = pl.BlockSpec((N_GROUPS, None, D_STATE, GROUP_W), lambda c: (0, index_of_chunk(c), 0, 0))
    return wide, b_spec, c_spec, rows_spec, state_spec


def _ssd_fwd(name, xbc, dt_exp, acs_exp, acs_rows, dskexp, steps=()):
    rows = xbc.shape[0]
    nc = rows // CHUNK

    def body(x_ref, b_ref, c_ref, dte_ref, acs_ref, acst_ref, dsk_ref, y_ref, st_ref, s_scr):
        @pl.when(pl.program_id(0) == 0)
        def _():
            s_scr[...] = jnp.zeros_like(s_scr)

        lane = lax.broadcasted_iota(jnp.int32, (CHUNK, LANES), 1)
        for g in range(N_GROUPS):
            y_g = _group_cols(y_ref, g, GROUP_W)
            causal, x, _, acs_exp_v, tot_exp, xdt, e_exp, f_exp, bm, cm = _ssd_common(
                _group_cols(x_ref, g, GROUP_W), _group_cols(b_ref, g, D_STATE), _group_cols(c_ref, g, D_STATE),
                _group_cols(dte_ref, g, GROUP_W), _group_cols(acs_ref, g, GROUP_W))
            state = s_scr[g]
            st_ref[g] = state
            cb16, bb16 = cm.astype(BF16), bm.astype(BF16)
            cb = _dot_nt(cb16, bb16)
            base = e_exp * _dot(cb16, state.astype(BF16)) + _group_cols(dsk_ref, g, GROUP_W)[...] * x
            for p in range(HEADS_PER_GROUP // 2):
                sl = slice(p * LANES, (p + 1) * LANES)
                xp = xdt[:, sl].astype(BF16)
                yd = []
                for e in range(2):
                    acs_row = acst_ref[pl.ds(g * HEADS_PER_GROUP + 2 * p + e, 1), :]
                    dm, _ = _pair_decay(acs_exp_v[:, sl], acs_row, e, causal)
                    yd.append(_dot((cb * dm).astype(BF16), xp))
                y_g[:, sl] = jnp.where(lane < HEAD_DIM, yd[0], yd[1]) + base[:, sl]
            s_scr[g] = jnp.exp(tot_exp) * state + _dot_tn(bb16, (f_exp * xdt).astype(BF16))

    wide, b_spec, c_spec, rows_spec, state_spec = _ssd_specs(lambda c: c)
    return _call(
        body, name=name,
        out_shape=(jax.ShapeDtypeStruct((rows, D_INNER), F32),
                   jax.ShapeDtypeStruct((N_GROUPS, nc, D_STATE, GROUP_W), F32)),
        grid=(nc,),
        in_specs=[wide, b_spec, c_spec, wide, wide, rows_spec, pl.BlockSpec((1, D_INNER), lambda c: (0, 0))],
        out_specs=(wide, state_spec),
        scratch_shapes=[pltpu.VMEM((N_GROUPS, D_STATE, GROUP_W), F32)],
        operands=[xbc, xbc, xbc, dt_exp, acs_exp, acs_rows, dskexp], semantics=("arbitrary",), steps=steps)


def _ssd_bwd(name, xbc, dt_exp, acs_exp, acs_rows, dt, a128, dskexp, dy, states, steps=()):
    rows = xbc.shape[0]
    nc = rows // CHUNK
    last = nc - 1

    def body(x_ref, b_ref, c_ref, dte_ref, acs_ref, acst_ref, dt_ref, a128_ref, dsk_all, dy_all, st_all,
             dxbc_all, ddt_ref, dalog_ref, ddsk_ref, ds_all):
        dx_all, db_all, dc_all = (dxbc_all.at[:, :D_INNER], dxbc_all.at[:, D_INNER:D_INNER + D_BC],
                                  dxbc_all.at[:, D_INNER + D_BC:])

        @pl.when(pl.program_id(0) == 0)
        def _():
            ds_all[...] = jnp.zeros_like(ds_all)
            dalog_ref[...] = jnp.zeros_like(dalog_ref)
            ddsk_ref[...] = jnp.zeros_like(ddsk_ref)

        dacs = jnp.zeros((CHUNK, LANES), F32)
        ddt_x = jnp.zeros((CHUNK, LANES), F32)
        for g in range(N_GROUPS):
            dacs_g, ddt_x_g = group(
                g, _group_cols(x_ref, g, GROUP_W), _group_cols(b_ref, g, D_STATE), _group_cols(c_ref, g, D_STATE),
                _group_cols(dte_ref, g, GROUP_W), _group_cols(acs_ref, g, GROUP_W), acst_ref,
                _group_cols(dsk_all, g, GROUP_W), _group_cols(dy_all, g, GROUP_W), st_all.at[g],
                _group_cols(dx_all, g, GROUP_W), _group_cols(db_all, g, D_STATE), _group_cols(dc_all, g, D_STATE),
                ddsk_ref, ds_all.at[g])
            dacs, ddt_x = dacs + dacs_g, ddt_x + ddt_x_g
        _, causal_t = _causal_masks()
        da = _dot3_data_rhs(causal_t, dacs)
        ddt_ref[...] = da * a128_ref[...] + ddt_x
        dalog_ref[...] += jnp.sum(da * dt_ref[...], axis=0, keepdims=True) * a128_ref[...]

    def group(g, x_ref, b_ref, c_ref, dte_ref, acs_ref, acst_ref, dsk_ref, dy_ref, st_ref,
              dx_ref, db_ref, dc_ref, ddsk_ref, ds_scr):
        causal, x, dt_exp, acs_exp_v, tot_exp, xdt, e_exp, f_exp, bm, cm = _ssd_common(
            x_ref, b_ref, c_ref, dte_ref, acs_ref)
        reduce_heads = _reduce_heads_matrix(g)
        state, dstate = st_ref[...], ds_scr[...]
        dyv = dy_ref[...]
        cb16, bb16 = cm.astype(BF16), bm.astype(BF16)
        s16, ds16 = state.astype(BF16), dstate.astype(BF16)
        cb = _dot_nt(cb16, bb16)
        cbt = _dot_nt(bb16, cb16)
        cs = _dot(cb16, s16)
        bds = _dot(bb16, ds16)
        edy = e_exp * dyv
        fx = f_exp * xdt
        dxdt_base = f_exp * bds
        dc_acc = _dot_nt(edy.astype(BF16), s16)
        db_acc = _dot_nt(fx.astype(BF16), ds16)
        ds_scr[...] = jnp.exp(tot_exp) * dstate + _dot_tn(cb16, edy.astype(BF16))
        q = fx * bds
        dacs = _dot2_data_lhs(edy * cs - q, reduce_heads)
        dtot = jnp.sum(_dot2_data_lhs(q + jnp.exp(tot_exp) * dstate * state, reduce_heads), axis=0, keepdims=True)
        ddsk_ref[...] += jnp.sum(_dot2_data_lhs(dyv * x, reduce_heads), axis=0, keepdims=True)
        lane = lax.broadcasted_iota(jnp.int32, (CHUNK, LANES), 1)
        dcb = jnp.zeros((CHUNK, CHUNK), F32)
        dcbt = jnp.zeros((CHUNK, CHUNK), F32)
        ddt_x = jnp.zeros((CHUNK, LANES), F32)
        for p in range(HEADS_PER_GROUP // 2):
            sl = slice(p * LANES, (p + 1) * LANES)
            xp, dyp = xdt[:, sl], dyv[:, sl]
            xp16, dyp16 = xp.astype(BF16), dyp.astype(BF16)
            dxh = []
            for e in range(2):
                h = 2 * p + e
                mine = (lane < HEAD_DIM) if e == 0 else (lane >= HEAD_DIM)
                acs_row = acst_ref[pl.ds(g * HEADS_PER_GROUP + h, 1), :]
                dm, dmt = _pair_decay(acs_exp_v[:, sl], acs_row, e, causal)
                m, mt = cb * dm, cbt * dmt
                xh16 = jnp.where(mine, xp, 0.0).astype(BF16)
                dyh16 = jnp.where(mine, dyp, 0.0).astype(BF16)
                d_m = _dot_nt(dyh16, xp16)
                d_mt = _dot_nt(xh16, dyp16)
                dacs_h = (jnp.sum(d_m * m, axis=-1, keepdims=True)
                          - jnp.sum(d_mt * mt, axis=-1, keepdims=True))
                dacs = dacs + jnp.where(lane == HEADS_PER_GROUP * g + h, dacs_h, 0.0)
                dcb = dcb + d_m * dm
                dcbt = dcbt + d_mt * dmt
                dxh.append(_dot(mt.astype(BF16), dyp16))
            dxdt = jnp.where(lane < HEAD_DIM, dxh[0], dxh[1]) + dxdt_base[:, sl]
            dx_ref[:, sl] = dxdt * dt_exp[:, sl] + dsk_ref[:, sl] * dyp
            ddt_x = ddt_x + _dot2_data_lhs(dxdt * x[:, sl], _reduce_pair_matrix(g, p))
        dc_ref[...] = dc_acc + _dot(dcb.astype(BF16), bb16)
        db_ref[...] = db_acc + _dot(dcbt.astype(BF16), cb16)
        row = lax.broadcasted_iota(jnp.int32, (CHUNK, LANES), 0)
        return dacs + jnp.where(row == CHUNK - 1, dtot, 0.0), ddt_x

    wide, b_spec, c_spec, rows_spec, state_spec = _ssd_specs(lambda c: last - c)
    heads_spec = pl.BlockSpec((CHUNK, LANES), lambda c: (last - c, 0))
    vec_spec = pl.BlockSpec((1, LANES), lambda c: (0, 0))
    vec_shape = jax.ShapeDtypeStruct((1, LANES), F32)
    return _call(
        body, name=name,
        out_shape=(jax.ShapeDtypeStruct((rows, D_XBC), F32), jax.ShapeDtypeStruct((rows, LANES), F32),
                   vec_shape, vec_shape),
        grid=(nc,),
        in_specs=[wide, b_spec, c_spec, wide, wide, rows_spec, heads_spec, vec_spec,
                  pl.BlockSpec((1, D_INNER), lambda c: (0, 0)), wide, state_spec],
        out_specs=(pl.BlockSpec((CHUNK, D_XBC), lambda c: (last - c, 0)), heads_spec, vec_spec, vec_spec),
        scratch_shapes=[pltpu.VMEM((N_GROUPS, D_STATE, GROUP_W), F32)],
        operands=[xbc, xbc, xbc, dt_exp, acs_exp, acs_rows, dt, a128, dskexp, dy, states],
        semantics=("arbitrary",), steps=steps)


def _attn_visible(b, heads=1):
    row = jnp.bitwise_and(lax.broadcasted_iota(jnp.int32, (heads * CHUNK, 3 * CHUNK), 0), CHUNK - 1)
    col = lax.broadcasted_iota(jnp.int32, (heads * CHUNK, 3 * CHUNK), 1)
    bb = b + jnp.zeros_like(col)
    meta = (col < CHUNK) & (bb >= 1) & (col >= PAD_ROWS)
    prev = (col >= CHUNK) & (col < 2 * CHUNK) & (bb >= 2) & ((col - CHUNK) > row)
    cur = (col >= 2 * CHUNK) & ((col - 2 * CHUNK) <= row) & ((bb >= 1) | ((col - 2 * CHUNK) >= PAD_ROWS))
    return meta | prev | cur


def _attn_visible4(b):
    return _attn_visible(b, 4)


def _stack_heads(q_ref, sink_ref, kvh, scale):
    lane = lax.broadcasted_iota(jnp.int32, (CHUNK, LANES), 1)
    parts, sinks = [], []
    for pp in range(2):
        pair = kvh * 2 + pp
        qp = q_ref[:, pair * LANES:(pair + 1) * LANES] * scale
        for e in range(2):
            mine = (lane < HEAD_DIM) if e == 0 else (lane >= HEAD_DIM)
            parts.append(jnp.where(mine, qp, 0.0).astype(BF16))
            sinks.append(jnp.full((CHUNK, 1), sink_ref[2 * pair + e], F32))
    return jnp.concatenate(parts, axis=0), jnp.concatenate(sinks, axis=0)


def _attn_operands(q_ref, k0, kp, kc, v0, vp, vc, sink_ref):
    kcat, vcat, q4, sink4 = [], [], [], []
    for kvh in range(N_KV_HEADS):
        ksl = slice(kvh * LANES, (kvh + 1) * LANES)
        kcat.append(jnp.concatenate([k0[:, ksl], kp[:, ksl], kc[:, ksl]], axis=0).astype(BF16))
        vcat.append(jnp.concatenate([v0[:, ksl], vp[:, ksl], vc[:, ksl]], axis=0).astype(BF16))
        stacked, sinks = _stack_heads(q_ref, sink_ref, kvh, ATTN_SCALE)
        q4.append(stacked)
        sink4.append(sinks)
    return kcat, vcat, q4, sink4


def _attn_probs(q4, kcat, visible, sink4):
    heads = range(N_KV_HEADS)
    s = [jnp.where(visible, _dot_nt(q4[h], kcat[h]), NEG_INF) for h in heads]
    m = [jnp.maximum(jnp.max(s[h], axis=-1, keepdims=True), sink4[h]) for h in heads]
    pe = [jnp.exp(s[h] - m[h]) for h in heads]
    pe_sink = [jnp.exp(sink4[h] - m[h]) for h in heads]
    inv = [1.0 / (jnp.sum(pe[h], axis=-1, keepdims=True) + pe_sink[h]) for h in heads]
    return [pe[h] * inv[h] for h in heads], [pe_sink[h] * inv[h] for h in heads]


def _unstack_pairs(stacked, pp):
    lane = lax.broadcasted_iota(jnp.int32, (CHUNK, LANES), 1)
    return jnp.where(lane < HEAD_DIM, stacked[(2 * pp) * CHUNK:(2 * pp + 1) * CHUNK],
                     stacked[(2 * pp + 1) * CHUNK:(2 * pp + 2) * CHUNK])


def _attn_specs(colblock):
    blk = lambda f: pl.BlockSpec((CHUNK, 2 * D_KV), f)
    return [blk(lambda b: (0, colblock)), blk(lambda b: (jnp.maximum(b - 1, 0), colblock)), blk(lambda b: (b, colblock))]


def _attn_fwd(name, q, kv2, sinks, steps=()):
    rows = q.shape[0]

    def body(q_ref, k0, kp, kc, v0, vp, vc, sink_ref, o_ref):
        visible = _attn_visible4(pl.program_id(0))
        kcat, vcat, q4, sink4 = _attn_operands(q_ref, k0, kp, kc, v0, vp, vc, sink_ref)
        pn, _ = _attn_probs(q4, kcat, visible, sink4)
        o4 = [_dot(pn[h].astype(BF16), vcat[h]) for h in range(N_KV_HEADS)]
        for kvh in range(N_KV_HEADS):
            for pp in range(2):
                qsl = slice((kvh * 2 + pp) * LANES, (kvh * 2 + pp + 1) * LANES)
                o_ref[:, qsl] = _unstack_pairs(o4[kvh], pp).astype(BF16)

    return _call(
        body, name=name, out_shape=jax.ShapeDtypeStruct((rows, D_MODEL), BF16), grid=(rows // CHUNK,),
        in_specs=[pl.BlockSpec((CHUNK, D_MODEL), lambda b: (b, 0))] + _attn_specs(0) + _attn_specs(1)
        + [pl.BlockSpec(memory_space=pltpu.SMEM)],
        out_specs=pl.BlockSpec((CHUNK, D_MODEL), lambda b: (b, 0)),
        operands=[q, kv2, kv2, kv2, kv2, kv2, kv2, sinks], semantics=("parallel",), steps=steps)


def _attn_bwd(name, q, kv2, sinks, do, steps=()):
    rows = q.shape[0]

    def body(q_ref, k0, kp, kc, v0, vp, vc, sink_ref, do_ref,
             dq_ref, dkc_ref, dkp_ref, dvc_ref, dvp_ref, dkm_ref, dvm_ref, dsink_ref):
        @pl.when(pl.program_id(0) == 0)
        def _():
            dkm_ref[...] = jnp.zeros_like(dkm_ref)
            dvm_ref[...] = jnp.zeros_like(dvm_ref)
            dsink_ref[...] = jnp.zeros_like(dsink_ref)

        visible = _attn_visible4(pl.program_id(0))
        heads = range(N_KV_HEADS)
        lane1 = lax.broadcasted_iota(jnp.int32, (1, LANES), 1)
        kcat, vcat, q4, sink4 = _attn_operands(q_ref, k0, kp, kc, v0, vp, vc, sink_ref)
        do4 = [_stack_heads(do_ref, sink_ref, h, 1.0)[0] for h in heads]
        pn, psink = _attn_probs(q4, kcat, visible, sink4)
        dp = [_dot_nt(do4[h], vcat[h]) for h in heads]
        delta = [jnp.sum(pn[h] * dp[h], axis=-1, keepdims=True) for h in heads]
        ds16 = [(pn[h] * (dp[h] - delta[h])).astype(BF16) for h in heads]
        dq4 = [_dot(ds16[h], kcat[h]) for h in heads]
        dk_acc = [_dot_tn(ds16[h], q4[h]) for h in heads]
        dv_acc = [_dot_tn(pn[h].astype(BF16), do4[h]) for h in heads]
        dsink = jnp.zeros((1, LANES), F32)
        for kvh in heads:
            ksl = slice(kvh * LANES, (kvh + 1) * LANES)
            sink_terms = psink[kvh] * delta[kvh]
            for j in range(4):
                part = jnp.sum(sink_terms[j * CHUNK:(j + 1) * CHUNK], axis=0, keepdims=True)
                dsink = dsink - jnp.where(lane1 == kvh * 4 + j, part, 0.0)
            for pp in range(2):
                qsl = slice((kvh * 2 + pp) * LANES, (kvh * 2 + pp + 1) * LANES)
                dq_ref[:, qsl] = (_unstack_pairs(dq4[kvh], pp) * ATTN_SCALE).astype(BF16)
            dkm_ref[:, ksl] += dk_acc[kvh][0:CHUNK]
            dvm_ref[:, ksl] += dv_acc[kvh][0:CHUNK]
            dkp_ref[:, ksl] = dk_acc[kvh][CHUNK:2 * CHUNK]
            dvp_ref[:, ksl] = dv_acc[kvh][CHUNK:2 * CHUNK]
            dkc_ref[:, ksl] = dk_acc[kvh][2 * CHUNK:3 * CHUNK]
            dvc_ref[:, ksl] = dv_acc[kvh][2 * CHUNK:3 * CHUNK]
        dsink_ref[...] += dsink

    qspec = pl.BlockSpec((CHUNK, D_MODEL), lambda b: (b, 0))
    kvspec = pl.BlockSpec((CHUNK, 2 * D_KV), lambda b: (b, 0))
    fixed = pl.BlockSpec((CHUNK, 2 * D_KV), lambda b: (0, 0))
    kv_shape = jax.ShapeDtypeStruct((rows, 2 * D_KV), F32)
    meta_shape = jax.ShapeDtypeStruct((CHUNK, 2 * D_KV), F32)
    return _call(
        body, name=name,
        out_shape=(jax.ShapeDtypeStruct((rows, D_MODEL), BF16), kv_shape, kv_shape, kv_shape, kv_shape,
                   meta_shape, meta_shape, jax.ShapeDtypeStruct((1, LANES), F32)),
        grid=(rows // CHUNK,),
        in_specs=[qspec] + _attn_specs(0) + _attn_specs(1) + [pl.BlockSpec(memory_space=pltpu.SMEM), qspec],
        out_specs=(qspec, kvspec, kvspec, kvspec, kvspec, fixed, fixed, pl.BlockSpec((1, LANES), lambda b: (0, 0))),
        operands=[q, kv2, kv2, kv2, kv2, kv2, kv2, sinks, do], semantics=("arbitrary",), steps=steps)


def _kv_grad_combine(name, dk_cur, dk_prev, dk_meta, dv_cur, dv_prev, dv_meta):
    rows = dk_cur.shape[0]
    nb = rows // CHUNK
    width = 2 * D_KV

    def body(kc_ref, kp_ref, km_ref, vc_ref, vp_ref, vm_ref, o_ref):
        jj = pl.program_id(0) + jnp.zeros((CHUNK, 1), jnp.int32)
        for half, (c_ref, p_ref, m_ref) in enumerate(((kc_ref, kp_ref, km_ref), (vc_ref, vp_ref, vm_ref))):
            total = c_ref[...] + jnp.where(jj < nb - 1, p_ref[...], 0.0) + jnp.where(jj == 0, m_ref[...], 0.0)
            o_ref[:, half * width:(half + 1) * width] = total.astype(BF16)

    blk = lambda f: pl.BlockSpec((CHUNK, width), f)
    three = lambda: [blk(lambda j: (j, 0)), blk(lambda j: (jnp.minimum(j + 1, nb - 1), 0)), blk(lambda j: (0, 0))]
    return pl.pallas_call(
        body, name=name, out_shape=jax.ShapeDtypeStruct((rows, 2 * width), BF16), grid=(nb,),
        in_specs=three() + three(), out_specs=pl.BlockSpec((CHUNK, 2 * width), lambda j: (j, 0)),
        compiler_params=_cparams(("parallel",)),
    )(dk_cur, dk_prev, dk_meta, dv_cur, dv_prev, dv_meta)


def _adamw(name, w, g, m, v, steps=()):
    rows, width = w.shape
    tr = rows
    for cand in range(8, rows + 1, 8):
        if rows % cand == 0 and cand * width * 4 <= (1 << 20):
            tr = cand

    def body(*refs):
        _adamw_update(*refs)

    blk = pl.BlockSpec((tr, width), lambda i: (i, 0))
    shp = jax.ShapeDtypeStruct((rows, width), F32)
    return _call(body, name=name, out_shape=(shp, shp, shp), grid=(rows // tr,), in_specs=[blk] * 4,
                 out_specs=(blk,) * 3, operands=[w, g, m, v], semantics=("parallel",), steps=steps)


def _adamw_update(w_ref, g_ref, m_ref, v_ref, d_ref, mo_ref, vo_ref):
    gv = g_ref[...]
    mn = ADAM_B1 * m_ref[...] + (1.0 - ADAM_B1) * gv
    vn = ADAM_B2 * v_ref[...] + (1.0 - ADAM_B2) * (gv * gv)
    m_hat = mn / (1.0 - ADAM_B1 ** ADAM_STEP)
    v_hat = vn / (1.0 - ADAM_B2 ** ADAM_STEP)
    d_ref[...] = -ADAM_LR * (m_hat / (jnp.sqrt(v_hat) + ADAM_EPS) + ADAM_WD * w_ref[...])
    mo_ref[...] = mn
    vo_ref[...] = vn


def _adamw_small(name, ws, gs, ms, vs):
    n = len(ws)

    def body(*refs):
        for i in range(n):
            _adamw_update(*refs[i::n])

    shapes = [jax.ShapeDtypeStruct(a.shape, F32) for a in ws]
    outs = pl.pallas_call(body, name=name, out_shape=shapes * 3, in_specs=[VMEM_SPEC] * (4 * n),
                          out_specs=[VMEM_SPEC] * (3 * n), compiler_params=_cparams())(*ws, *gs, *ms, *vs)
    return outs[:n], outs[n:2 * n], outs[2 * n:]


def _ffn_fwd(tag, h, hn, p, i, plan):
    up_g, up_v, act = _ffn_up_conv(f"ffn{tag}_up", hn, plan.weight("f_w_up", i), p["f_conv_w"][i],
                                   p["f_conv_b"][i:i + 1], steps=plan.steps(f"ffn{tag}_up"))
    pre = _mm(f"ffn{tag}_down", act, plan.weight("f_w_down", i), "nn", steps=plan.steps(f"ffn{tag}_down"))
    return pre, (h, hn, up_g, up_v, act, pre)


def _ffn_bwd(tag, dpre, saved, p, i, plan):
    h, hn, up_g, up_v, act, pre = saved
    plan.grad("f_w_down", i, _mm(f"ffn{tag}_down_dw", act, dpre, "tn", out_dtype=BF16))
    dact = _mm(f"ffn{tag}_down_dx", dpre, plan.weight("f_w_down", i), "nt", steps=plan.steps(f"ffn{tag}_down_dx"))
    gwg, gwv, gbg, gbv, dhn, g_up = _ffn_conv_bwd(
        f"ffn{tag}_conv_bwd", up_g, up_v, dact, p["f_conv_w"][i], p["f_conv_b"][i:i + 1], hn,
        plan.weight("f_w_up", i), steps=plan.steps(f"ffn{tag}_conv_bwd"))
    g_cw, g_cb = jnp.concatenate([gwg, gwv], axis=1), jnp.concatenate([gbg, gbv], axis=1)
    plan.grad("f_w_up", i, g_up)
    return dhn, dict(f_conv_w=g_cw, f_conv_b=g_cb)


def _lanes_pad(a, width=LANES):
    return jnp.pad(a, [(0, 0)] * (a.ndim - 1) + [(0, width - a.shape[-1])])


def _dup_heads(w):
    rows = w.shape[0]
    w = w.reshape(rows, 2 * N_KV_HEADS, 1, HEAD_DIM)
    return jnp.broadcast_to(w, (rows, 2 * N_KV_HEADS, 2, HEAD_DIM)).reshape(rows, 4 * D_KV)


def _undup_heads(g):
    rows = g.shape[0]
    return g.reshape(rows, 2 * N_KV_HEADS, 2, HEAD_DIM).sum(axis=2).reshape(rows, 2 * D_KV)


def _local_step(x2, target, p, plan):
    seq = x2.shape[0]
    rows = seq + CHUNK
    g = {}

    h0 = jnp.concatenate([jnp.zeros((PAD_ROWS, D_MODEL), F32), p["meta_tokens"], x2], axis=0)

    w_in = plan.weight("a_w_in")
    w_dt = jnp.pad(w_in[D_MAIN:], ((0, LANES - SSM_HEADS), (0, 0)))
    dt_bias = _lanes_pad(p["a_dt_bias"])
    a128 = _lanes_pad(-jnp.exp(p["a_a_log"]))
    dskexp = jnp.repeat(p["a_d_skip"].reshape(SSM_HEADS), HEAD_DIM).reshape(1, D_INNER)

    hn0 = _rms_fwd("a_norm", h0, p["a_norm_pre"])
    zx = _mm("a_in_main", hn0, w_in, "nt", k_rows=D_MAIN, steps=plan.steps("a_in_main"))
    dtr = _mm("a_in_dt", hn0, w_dt, "nt")
    xbc = _conv4_fwd("a_conv", zx, p["a_conv_w"], p["a_conv_b"], steps=plan.steps("a_conv"))
    dt = _dt_fwd("a_dt", dtr, dt_bias)
    dt_exp, acs_exp, acs_rows = _ssd_prep("a_ssd_prep", dt, a128, steps=plan.steps("a_ssd_prep"))
    y, states = _ssd_fwd("a_ssd", xbc, dt_exp, acs_exp, acs_rows, dskexp, steps=plan.steps("a_ssd"))
    yn = _gate_fwd("a_gate", y, zx, p["a_gate_norm"], steps=plan.steps("a_gate"))
    mix = _mm("a_out", yn, plan.weight("a_w_out"), "nn", steps=plan.steps("a_out"))
    h1, (hn_f0,) = _resid_norm_fwd("a_resid", h0, mix, p["a_norm_post"], [p["f_norm_pre"][0:1]])

    pre_f0, ffn0 = _ffn_fwd("0", h1, hn_f0, p, 0, plan)
    h2, (hkv, hn2) = _resid_norm_fwd("ffn0_resid", h1, pre_f0, p["f_norm_post"][0:1], [p["kv_norm"], p["b_norm_pre"]])

    w_kv2 = _dup_heads(plan.weight("w_kv"))
    kv2 = _mm("kv_proj", hkv, w_kv2, "nn")
    q = _mm("b_q", hn2, plan.weight("b_w_q"), "nn")
    sinks = p["b_sinks"].reshape(N_Q_HEADS)
    o = _attn_fwd("b_attn", q, kv2, sinks, steps=plan.steps("b_attn"))
    attn = _mm("b_o", o, plan.weight("b_w_o"), "nn", steps=plan.steps("b_o"))
    h3, (hn_f1,) = _resid_norm_fwd("b_resid", h2, attn, p["b_norm_post"], [p["f_norm_pre"][1:2]])

    pre_f1, ffn1 = _ffn_fwd("1", h3, hn_f1, p, 1, plan)
    dh, loss_vec, dpre_f1, g_post1 = _resid_norm_loss("ffn1_resid_loss", h3, pre_f1, p["f_norm_post"][1:2], target)
    loss = loss_vec[0, 0]

    dhn_f1, g1 = _ffn_bwd("1", dpre_f1, ffn1, p, 1, plan)
    dh, g_pre1, dpre, g["b_norm_post"] = _norm_bwd_add("ffn1_norm_bwd", dh, dhn_f1, h3, p["f_norm_pre"][1:2],
                                                        then=(attn, p["b_norm_post"]))
    plan.grad("b_w_o", None, _mm("b_o_dw", o, dpre, "tn", out_dtype=BF16))
    do = _mm("b_o_dx", dpre, plan.weight("b_w_o"), "nt", steps=plan.steps("b_o_dx"))
    dq, dkc, dkp, dvc, dvp, dkm, dvm, dsink = _attn_bwd("b_attn_bwd", q, kv2, sinks, do, steps=plan.steps("b_attn_bwd"))
    g["b_sinks"] = dsink[:, :N_Q_HEADS]
    dhn2 = _mm("b_q_dx", dq, plan.weight("b_w_q"), "nt")
    plan.grad("b_w_q", None, _mm("b_q_dw", hn2, dq, "tn", out_dtype=BF16))
    dh, g["b_norm_pre"] = _norm_bwd_add("b_norm_bwd", dh, dhn2, h2, p["b_norm_pre"])
    dkv2 = _kv_grad_combine("kv_grad", dkc, dkp, dkm, dvc, dvp, dvm)
    dhkv = _mm("kv_proj_dx", dkv2, w_kv2, "nt")
    plan.grad("w_kv", None, _undup_heads(_mm("kv_proj_dw", hkv, dkv2, "tn")))
    dh, g["kv_norm"], dpre_f0, g_post0 = _norm_bwd_add("kv_norm_bwd", dh, dhkv, h2, p["kv_norm"],
                                                       then=(pre_f0, p["f_norm_post"][0:1]))

    dhn_f0, g0 = _ffn_bwd("0", dpre_f0, ffn0, p, 0, plan)
    dh, g_pre0, dpre, g["a_norm_post"] = _norm_bwd_add("ffn0_norm_bwd", dh, dhn_f0, h1, p["f_norm_pre"][0:1],
                                                        then=(mix, p["a_norm_post"]))
    g["f_norm_post"] = jnp.concatenate([g_post0, g_post1], axis=0)
    g["f_norm_pre"] = jnp.concatenate([g_pre0, g_pre1], axis=0)
    g["f_conv_w"] = jnp.stack([g0["f_conv_w"], g1["f_conv_w"]])
    g["f_conv_b"] = jnp.concatenate([g0["f_conv_b"], g1["f_conv_b"]], axis=0)
    plan.grad("a_w_out", None, _mm("a_out_dw", yn, dpre, "tn", out_dtype=BF16))
    dyn = _mm("a_out_dx", dpre, plan.weight("a_w_out"), "nt", steps=plan.steps("a_out_dx"))
    dy, dzx, g["a_gate_norm"] = _gate_bwd("a_gate_bwd", dyn, y, zx, p["a_gate_norm"])
    dxbc, ddt, dalog, ddsk = _ssd_bwd("a_ssd_bwd", xbc, dt_exp, acs_exp, acs_rows, dt, a128, dskexp, dy, states,
                                      steps=plan.steps("a_ssd_bwd"))
    g["a_a_log"] = dalog[:, :SSM_HEADS]
    g["a_d_skip"] = ddsk[:, :SSM_HEADS]
    ddtr, dbias = _dt_bwd("a_dt_bwd", ddt, dtr, dt_bias)
    g["a_dt_bias"] = dbias[:, :SSM_HEADS]
    dzx, g["a_conv_w"], g["a_conv_b"] = _conv4_bwd("a_conv_bwd", zx, dxbc, p["a_conv_w"], p["a_conv_b"], dzx)
    g_in = _mm("a_in_main_dw", dzx, hn0, "tn", out_dtype=BF16, out_rows=D_IN_PROJ, steps=plan.steps("a_in_main_dw"))
    plan.grad("a_w_in", None, _tn_rows_into("a_in_dt_dw", ddtr, hn0, g_in, D_MAIN, SSM_HEADS))
    dhn0 = _mm("a_in_dt_dx", ddtr, w_dt, "nn", steps=plan.steps("a_in_dt_dx"))
    dhn0 = _mm("a_in_main_dx", dzx, w_in, "nn", acc=dhn0, steps=plan.steps("a_in_main_dx"))
    dh_first, grad_x, g["a_norm_pre"] = _norm_bwd_add("a_norm_bwd", dh, dhn0, h0, p["a_norm_pre"],
                                                      split_first_block=True, steps=plan.steps("a_norm_bwd"))
    g["meta_tokens"] = dh_first[PAD_ROWS:]
    return loss, grad_x, g


ANY = pl.BlockSpec(memory_space=pl.ANY)
VMEM_SPEC = pl.BlockSpec(memory_space=pltpu.VMEM)


def _allgather_small(name, shard):
    rows = shard.shape[0]

    def body(s_ref, o_ref, send_sems, recv_sems):
        x, y, c = _place()
        me = 2 * x + y
        o_ref[me] = s_ref[...]
        chips = _other_chips(x, y)
        sends = [pltpu.make_async_remote_copy(s_ref, o_ref.at[me], send_sems.at[j], recv_sems.at[j],
                                              device_id=(cx, cy, c), device_id_type=MESH)
                 for j, (cx, cy) in enumerate(chips)]
        for cp in sends:
            cp.start()
        for j, (cx, cy) in enumerate(chips):
            pltpu.make_async_remote_copy(s_ref, o_ref.at[2 * cx + cy], send_sems.at[j], recv_sems.at[j],
                                         device_id=(cx, cy, c), device_id_type=MESH).wait_recv()
        for cp in sends:
            cp.wait_send()

    return pl.pallas_call(
        body, name=name, out_shape=jax.ShapeDtypeStruct((N_CHIPS, rows, LANES), F32),
        in_specs=[VMEM_SPEC], out_specs=VMEM_SPEC,
        scratch_shapes=[pltpu.SemaphoreType.DMA((3,)), pltpu.SemaphoreType.DMA((3,))],
        compiler_params=pltpu.CompilerParams(vmem_limit_bytes=VMEM_LIMIT),
    )(shard)


def _row_block(rows, width, itemsize, align, budget=2 << 20):
    best = rows
    for cand in range(align, rows + 1, align):
        if rows % cand == 0 and cand * width * itemsize <= budget:
            best = cand
    return best


def _cast_into_slot(name, chip, w, layer=None):
    rows, width = w.shape[-2:]
    tr = _row_block(rows, width, 4, 16)
    if layer is None:
        in_spec = pl.BlockSpec((tr, width), lambda i, chip_ref: (i, 0))
    else:
        in_spec = pl.BlockSpec((None, tr, width), lambda i, chip_ref: (layer, i, 0))

    def body(chip_ref, w_ref, o_ref):
        o_ref[...] = w_ref[...].astype(BF16)

    return pl.pallas_call(
        body, name=name, out_shape=jax.ShapeDtypeStruct((N_CHIPS, rows, width), BF16),
        grid_spec=pltpu.PrefetchScalarGridSpec(
            num_scalar_prefetch=1, grid=(rows // tr,), in_specs=[in_spec],
            out_specs=pl.BlockSpec((None, tr, width), lambda i, chip_ref: (chip_ref[0], i, 0))),
        compiler_params=_cparams(("parallel",)),
    )(chip, w)


def _allreduce_small(name, vec):
    rows = -(-vec.shape[0] // (2 * SUBLANES)) * (2 * SUBLANES)
    hr = rows // 2
    padded = jnp.pad(vec, ((0, rows - vec.shape[0]), (0, 0)))

    def body(v_ref, o_ref, theirs, pair, by_chip, send_sems, recv_sems):
        x, y, c = _place()
        me = 2 * x + y
        sibling = (x, y, 1 - c)
        mine = pl.ds(pl.multiple_of(c * hr, SUBLANES), hr)
        other = pl.ds(pl.multiple_of((1 - c) * hr, SUBLANES), hr)

        swap = _remote(v_ref, theirs, send_sems, recv_sems, 0, sibling)
        swap.start()
        swap.wait()
        south = (c + jnp.zeros((1, 1), jnp.int32)) == 0
        pair[...] = jnp.where(south, v_ref[...], theirs[...]) + jnp.where(south, theirs[...], v_ref[...])

        by_chip[me] = pair[mine, :]
        sends = [_remote(by_chip.at[me], by_chip.at[me], send_sems, recv_sems, 1 + j, (cx, cy, c))
                 for j, (cx, cy) in enumerate(_other_chips(x, y))]
        for cp in sends:
            cp.start()
        for j, (cx, cy) in enumerate(_other_chips(x, y)):
            _remote(by_chip.at[me], by_chip.at[2 * cx + cy], send_sems, recv_sems, 1 + j, (cx, cy, c)).wait_recv()
        for cp in sends:
            cp.wait_send()
        total = by_chip[0]
        for s in range(1, N_CHIPS):
            total = total + by_chip[s]

        o_ref[mine, :] = total
        back = _remote(o_ref.at[mine], o_ref.at[mine], send_sems, recv_sems, 4, sibling)
        back.start()
        _remote(o_ref.at[other], o_ref.at[other], send_sems, recv_sems, 4, sibling).wait_recv()
        back.wait_send()

    out = pl.pallas_call(
        body, name=name, out_shape=jax.ShapeDtypeStruct((rows, LANES), F32),
        in_specs=[VMEM_SPEC], out_specs=VMEM_SPEC,
        scratch_shapes=[pltpu.VMEM((rows, LANES), F32), pltpu.VMEM((rows, LANES), F32),
                        pltpu.VMEM((N_CHIPS, hr, LANES), F32), pltpu.SemaphoreType.DMA((5,)),
                        pltpu.SemaphoreType.DMA((5,))],
        compiler_params=pltpu.CompilerParams(vmem_limit_bytes=VMEM_LIMIT),
    )(padded)
    return out[:vec.shape[0]]


def _rs_pair_add(name, place, grads, partner, split="rows"):
    _, half_rows, width = partner.shape
    tr = _row_block(half_rows, width, 2, 16)
    nb = half_rows // tr
    if split == "rows":
        mine = pl.BlockSpec((None, tr, width), lambda s, i, pr: (s, pr[1] * nb + i, 0))
    else:
        mine = pl.BlockSpec((None, tr, width), lambda s, i, pr: (s, i, pr[1]))

    def body(place_ref, g_ref, p_ref, o_ref):
        o_ref[...] = (g_ref[...].astype(F32) + p_ref[...].astype(F32)).astype(BF16)

    return pl.pallas_call(
        body, name=name, out_shape=jax.ShapeDtypeStruct(partner.shape, BF16),
        grid_spec=pltpu.PrefetchScalarGridSpec(
            num_scalar_prefetch=1, grid=(N_CHIPS, nb),
            in_specs=[mine, pl.BlockSpec((None, tr, width), lambda s, i, pr: (s, i, 0))],
            out_specs=pl.BlockSpec((None, tr, width), lambda s, i, pr: (s, i, 0))),
        compiler_params=_cparams(("parallel", "parallel")),
    )(place, grads, partner)


def _rs_chip_add(name, place, mine, others, split="rows"):
    _, half_rows, width = mine.shape
    tr = _row_block(half_rows, width, 4, 16, budget=1 << 20)
    nb = half_rows // tr
    if split == "rows":
        out_shape, out_spec = (2 * half_rows, width), pl.BlockSpec((tr, width), lambda i, pr: (pr[1] * nb + i, 0))
    else:
        out_shape, out_spec = (half_rows, 2 * width), pl.BlockSpec((tr, width), lambda i, pr: (i, pr[1]))

    def body(place_ref, q_ref, r_ref, o_ref):
        acc = q_ref[...].astype(F32)
        for j in range(3):
            acc = acc + r_ref[j].astype(F32)
        o_ref[...] = acc

    return pl.pallas_call(
        body, name=name, out_shape=jax.ShapeDtypeStruct(out_shape, F32),
        grid_spec=pltpu.PrefetchScalarGridSpec(
            num_scalar_prefetch=1, grid=(nb,),
            in_specs=[pl.BlockSpec((None, tr, width), lambda i, pr: (pr[0], i, 0)),
                      pl.BlockSpec((3, tr, width), lambda i, pr: (0, i, 0))],
            out_specs=out_spec),
        compiler_params=_cparams(("parallel",)),
    )(place, mine, others)


WEIGHTS = ["meta_tokens", "a_norm_pre", "a_w_in", "a_conv_w", "a_conv_b", "a_dt_bias", "a_a_log", "a_d_skip",
           "a_gate_norm", "a_w_out", "a_norm_post", "kv_norm", "w_kv", "b_norm_pre", "b_w_q", "b_sinks", "b_w_o",
           "b_norm_post", "f_norm_pre", "f_w_up", "f_conv_w", "f_conv_b", "f_w_down", "f_norm_post"]
FULL_SHAPE = {
    "meta_tokens": (16, 1024), "a_norm_pre": (1, 1024), "a_w_in": (1, 1024, 5152), "a_conv_w": (1, 4, 3072),
    "a_conv_b": (1, 3072), "a_dt_bias": (1, 32), "a_a_log": (1, 32), "a_d_skip": (1, 32), "a_gate_norm": (1, 2048),
    "a_w_out": (1, 2048, 1024), "a_norm_post": (1, 1024), "kv_norm": (1024,), "w_kv": (1024, 512),
    "b_norm_pre": (1, 1024), "b_w_q": (1, 1024, 1024), "b_sinks": (1, 16), "b_w_o": (1, 1024, 1024),
    "b_norm_post": (1, 1024), "f_norm_pre": (2, 1024), "f_w_up": (2, 1024, 5632), "f_conv_w": (2, 3, 5632),
    "f_conv_b": (2, 5632), "f_w_down": (2, 2816, 1024), "f_norm_post": (2, 1024),
}
SHARD_AXIS = {
    "meta_tokens": 1, "a_norm_pre": 1, "a_w_in": 2, "a_conv_w": 2, "a_conv_b": 1, "a_dt_bias": None, "a_a_log": None,
    "a_d_skip": None, "a_gate_norm": 1, "a_w_out": 1, "a_norm_post": 1, "kv_norm": None, "w_kv": 0, "b_norm_pre": None,
    "b_w_q": 1, "b_sinks": None, "b_w_o": 1, "b_norm_post": None, "f_norm_pre": None, "f_w_up": 2, "f_conv_w": 2,
    "f_conv_b": None, "f_w_down": 1, "f_norm_post": None,
}
BIG = ["a_w_in", "a_w_out", "w_kv", "b_w_q", "b_w_o", "f_w_up", "f_w_down"]
SMALL = [n for n in WEIGHTS if n not in BIG]
SMALL_SHARDED = [n for n in SMALL if SHARD_AXIS[n] is not None]


def _shard_shape(name):
    shape = list(FULL_SHAPE[name])
    if SHARD_AXIS[name] is not None:
        shape[SHARD_AXIS[name]] //= N_CHIPS
    return tuple(shape)


def _numel(shape):
    return int(math.prod(shape))


SUBLANES = 8


def _packed_rows(shape):
    rows = -(-_numel(shape) // LANES)
    return -(-rows // SUBLANES) * SUBLANES


def _pack(arrays):
    parts = []
    for a in arrays:
        size, rows = _numel(a.shape), _packed_rows(a.shape)
        if size % LANES == 0:
            part = jnp.pad(a.reshape(size // LANES, LANES), ((0, rows - size // LANES), (0, 0)))
        else:
            part = jnp.pad(a.reshape(-1), (0, rows * LANES - size)).reshape(rows, LANES)
        parts.append(part)
    return jnp.concatenate(parts, axis=0)


def _unpack(packed, names, shape_of):
    out, off = {}, 0
    lead = packed.shape[:-2]
    for n in names:
        shape = tuple(shape_of(n))
        size, rows = _numel(shape), _packed_rows(shape)
        part = packed[..., off:off + rows, :]
        if size % LANES == 0:
            out[n] = part[..., :size // LANES, :].reshape(lead + shape)
        else:
            out[n] = part.reshape(lead + (rows * LANES,))[..., :size].reshape(lead + shape)
        off += rows
    return out


def _split_chips(name, full):
    ax = SHARD_AXIS[name]
    shape = full.shape
    cut = shape[:ax] + (N_CHIPS, shape[ax] // N_CHIPS) + shape[ax + 1:]
    return jnp.moveaxis(full.reshape(cut), ax, 0)


def _join_chips(name, stacked):
    ax = SHARD_AXIS[name]
    moved = jnp.moveaxis(stacked, 0, ax)
    shape = moved.shape
    return moved.reshape(shape[:ax] + (shape[ax] * shape[ax + 1],) + shape[ax + 2:])


def _as2d(a):
    return a.reshape(-1, a.shape[-1])


BUFFERS = [("a_w_in", "a_w_in", None), ("a_w_out", "a_w_out", None), ("w_kv", "w_kv", None),
           ("b_w_q", "b_w_q", None), ("b_w_o", "b_w_o", None), ("f_w_up0", "f_w_up", 0), ("f_w_up1", "f_w_up", 1),
           ("f_w_down0", "f_w_down", 0), ("f_w_down1", "f_w_down", 1)]


TRANSPOSED = ("a_w_in",)
SPLIT = {"a_w_in": "cols"}


def _local_shard(arrays, weight, layer):
    if weight in TRANSPOSED:
        return arrays[weight][0].T
    return _as2d(arrays[weight]) if layer is None else arrays[weight]


def _weight_from_gathered(weight, buf):
    if weight == "f_w_up":
        return buf
    return buf.reshape(N_CHIPS * buf.shape[1], buf.shape[2])


def _gathered_from_grad(weight, g):
    if weight == "f_w_up":
        return g
    return g.reshape(N_CHIPS, g.shape[0] // N_CHIPS, g.shape[1]).astype(BF16)


GATHER_SCHEDULE = {
    "a_in_main": [("ici", ["a_w_out"])],
    "a_conv": [("d2d", ["a_w_out"]), ("ici", ["f_w_down0"])],
    "a_ssd_prep": [("d2d", ["f_w_down0"]), ("ici_near", ["f_w_up0"])],
    "a_ssd": [("ici_far", ["f_w_up0"])],
    "a_gate": [("d2d", ["f_w_up0"]), ("ici", ["w_kv", "b_w_q", "b_w_o"])],
    "ffn0_up": [("d2d", ["w_kv", "b_w_q", "b_w_o"]), ("ici", ["f_w_down1"])],
    "ffn0_down": [("d2d", ["f_w_down1"])],
    "b_attn": [("ici", ["f_w_up1"])],
    "b_o": [("d2d", ["f_w_up1"])],
}
REDUCE_SCHEDULE = {
    "b_attn_bwd": [("all", ["f_w_down1", "f_w_up1", "b_w_o"])],
    "ffn0_conv_bwd": [("all", ["b_w_q", "w_kv", "f_w_down0"])],
    "a_ssd_bwd": [("all", ["f_w_up0", "a_w_out"])],
    "a_in_main_dx": [("near", ["a_w_in"])],
    "a_norm_bwd": [("far", ["a_w_in"])],
}
REDUCE_LAST = ("a_w_in",)
ICI_PEERS = {"ici": ALL_PEERS, "ici_near": NEAR_PEERS, "ici_far": FAR_PEERS,
             "all": ALL_PEERS, "near": NEAR_PEERS, "far": FAR_PEERS}
PAIR_SCHEDULE = {
    "b_o_dx": ["f_w_down1", "f_w_up1", "b_w_o"],
    "ffn0_down_dx": ["b_w_q", "w_kv", "f_w_down0"],
    "a_out_dx": ["f_w_up0", "a_w_out"],
    "a_in_dt_dx": ["a_w_in"],
}
SWAP_SCHEDULE = {"a_in_main_dw": ["f_w_down1", "f_w_up1", "b_w_o", "b_w_q", "w_kv", "f_w_down0", "f_w_up0", "a_w_out"]}


def _buffer_of(weight, layer):
    return weight if layer is None else f"{weight}{layer}"


class _Pipeline:
    def __init__(self, place, slots):
        self.place = place
        self.slots = dict(slots)
        self.running = []
        self.grads = {}
        self.theirs = {}
        self.partials = {}
        self.peers = {}
        self.reduced = {}

    def _collect(self):
        for step, buffers, table in self.running:
            table.update(zip(buffers, step.results))
        self.running = []

    @staticmethod
    def _splits(buffers):
        return [SPLIT.get(b, "rows") for b in buffers]

    def gather_now(self, name, buffers):
        step = _step_gather_full([self.slots[b] for b in buffers], self._splits(buffers))
        _run_steps(name, [step])
        self.slots.update(zip(buffers, step.results))

    def weight(self, name, layer=None):
        self._collect()
        return _weight_from_gathered(name, self.slots[_buffer_of(name, layer)])

    def grad(self, name, layer, g):
        self.grads[_buffer_of(name, layer)] = _gathered_from_grad(name, g)

    def steps(self, kernel):
        self._collect()
        steps = []
        for phase, buffers in GATHER_SCHEDULE.get(kernel, []):
            bufs, splits = [self.slots[b] for b in buffers], self._splits(buffers)
            step = (_step_gather_d2d(bufs, splits) if phase == "d2d"
                    else _step_gather_ici(bufs, splits, ICI_PEERS[phase]))
            self.running.append((step, buffers, self.slots))
            steps.append(step)
        buffers = PAIR_SCHEDULE.get(kernel)
        if buffers:
            step = _step_pair_exchange([self.grads[b] for b in buffers], self._splits(buffers))
            self.running.append((step, buffers, self.theirs))
            steps.append(step)
        for part, buffers in REDUCE_SCHEDULE.get(kernel, []):
            for b in buffers:
                if b not in self.partials:
                    self.partials[b] = _rs_pair_add("reduce_pair_add_" + b, self.place, self.grads[b], self.theirs[b],
                                                    SPLIT.get(b, "rows"))
            started = [self.peers[b] for b in buffers] if all(b in self.peers for b in buffers) else None
            step = _step_chip_exchange([self.partials[b] for b in buffers], ICI_PEERS[part], into=started)
            self.running.append((step, buffers, self.peers))
            steps.append(step)
        buffers = SWAP_SCHEDULE.get(kernel)
        if buffers:
            step = self._swap_step(buffers)
            self.running.append((step, buffers, self.reduced))
            steps.append(step)
        return steps

    def _swap_step(self, buffers):
        halves = [_rs_chip_add("reduce_chip_add_" + b, self.place, self.partials[b], self.peers[b], SPLIT.get(b, "rows"))
                  for b in buffers]
        return _step_pair_gather(halves, self._splits(buffers))

    def shard(self, buffer):
        self._collect()
        return self.reduced[buffer]

    def finish(self):
        self._collect()
        rest = [b for b, _, _ in BUFFERS if b not in self.reduced]
        step = self._swap_step(rest)
        _run_steps("reduce_pair_gather", [step])
        self.reduced.update(zip(rest, step.results))


def kernel(x, meta_tokens, a_norm_pre, a_w_in, a_conv_w, a_conv_b, a_dt_bias, a_a_log, a_d_skip, a_gate_norm, a_w_out, a_norm_post, kv_norm, w_kv, b_norm_pre, b_w_q, b_sinks, b_w_o, b_norm_post, f_norm_pre, f_w_up, f_conv_w, f_conv_b, f_w_down, f_norm_post, loss_target, m_meta_tokens, m_a_norm_pre, m_a_w_in, m_a_conv_w, m_a_conv_b, m_a_dt_bias, m_a_a_log, m_a_d_skip, m_a_gate_norm, m_a_w_out, m_a_norm_post, m_kv_norm, m_w_kv, m_b_norm_pre, m_b_w_q, m_b_sinks, m_b_w_o, m_b_norm_post, m_f_norm_pre, m_f_w_up, m_f_conv_w, m_f_conv_b, m_f_w_down, m_f_norm_post, v_meta_tokens, v_a_norm_pre, v_a_w_in, v_a_conv_w, v_a_conv_b, v_a_dt_bias, v_a_a_log, v_a_d_skip, v_a_gate_norm, v_a_w_out, v_a_norm_post, v_kv_norm, v_w_kv, v_b_norm_pre, v_b_w_q, v_b_sinks, v_b_w_o, v_b_norm_post, v_f_norm_pre, v_f_w_up, v_f_conv_w, v_f_conv_b, v_f_w_down, v_f_norm_post):
    given = dict(locals())
    w = {n: given[n] for n in WEIGHTS}
    mom = {n: given["m_" + n] for n in WEIGHTS}
    var = {n: given["v_" + n] for n in WEIGHTS}
    chip = 2 * lax.axis_index("x") + lax.axis_index("y")
    core = lax.axis_index("c")
    place = jnp.stack([chip, core]).astype(jnp.int32)

    small_all = _allgather_small("gather_small", _pack([w[n] for n in SMALL_SHARDED]))
    small_parts = _unpack(small_all, SMALL_SHARDED, _shard_shape)
    slots = {b: _cast_into_slot("cast_" + b, place, _local_shard(w, wn, layer), layer) for b, wn, layer in BUFFERS}
    pipeline = _Pipeline(place, slots)
    pipeline.gather_now("gather_first", ["a_w_in"])
    p = {}
    for n in SMALL:
        p[n] = _join_chips(n, small_parts[n]) if n in SMALL_SHARDED else w[n]
    p["a_conv_w"] = p["a_conv_w"][0]
    p["kv_norm"] = p["kv_norm"].reshape(1, D_MODEL)

    loss_local, grad_x, g = _local_step(x[0], loss_target[0], p, pipeline)

    small_sum = _allreduce_small("reduce_small", _pack([g[n].reshape(FULL_SHAPE[n]) for n in SMALL]
                                                       + [loss_local.reshape(1, 1)]))
    small_red = _unpack(small_sum, SMALL + ["loss"], lambda n: (1, 1) if n == "loss" else FULL_SHAPE[n])
    loss = small_red["loss"][0, 0]
    grads = {}
    for n in SMALL:
        if SHARD_AXIS[n] is None:
            grads[n] = small_red[n]
        else:
            grads[n] = lax.dynamic_index_in_dim(_split_chips(n, small_red[n]), chip, 0, keepdims=False)

    delta, new_m, new_v = {}, {}, {}
    for n in sorted(BIG, key=lambda name: name in REDUCE_LAST):
        shape = _shard_shape(n)
        if n in REDUCE_LAST:
            pipeline.finish()
        if n in TRANSPOSED:
            g2d = pipeline.shard(n)
            w2d, m2d, v2d = (arrays[n][0].T for arrays in (w, mom, var))
            back = lambda a: a.T.reshape(shape)
        else:
            g2d = (jnp.concatenate([pipeline.shard(n + "0"), pipeline.shard(n + "1")], axis=0)
                   if n in ("f_w_up", "f_w_down") else pipeline.shard(n))
            w2d, m2d, v2d = (_as2d(arrays[n]) for arrays in (w, mom, var))
            back = lambda a: a.reshape(shape)
        d, m2, v2 = _adamw("adamw_" + n, w2d, g2d, m2d, v2d, steps=pipeline.steps("adamw_" + n))
        grads[n], delta[n], new_m[n], new_v[n] = back(g2d), back(d), back(m2), back(v2)
    at_least_2d = lambda n: (1,) * (2 - len(_shard_shape(n))) + _shard_shape(n)
    outs = _adamw_small("adamw_small", *[[src[n].reshape(at_least_2d(n)) for n in SMALL] for src in (w, grads, mom, var)])
    for dst, arrays in zip((delta, new_m, new_v), outs):
        dst.update({n: a.reshape(_shard_shape(n)) for n, a in zip(SMALL, arrays)})

    return (loss, grad_x[None], *[grads[n].reshape(_shard_shape(n)) for n in WEIGHTS],
            *[delta[n] for n in WEIGHTS], *[new_m[n] for n in WEIGHTS], *[new_v[n] for n in WEIGHTS])
```

```python
import functools
import math

import jax
import jax.numpy as jnp
from jax import lax
from jax.experimental import pallas as pl
from jax.experimental.pallas import tpu as pltpu

F32, BF16 = jnp.float32, jnp.bfloat16
MESH = pl.DeviceIdType.MESH

D_MODEL = 1024
N_META = 16
CHUNK = 128
PAD_ROWS = CHUNK - N_META
D_INNER = 2048
D_STATE = 128
N_GROUPS = 4
HEADS_PER_GROUP = 8
SSM_HEADS = 32
HEAD_DIM = 64
D_BC = N_GROUPS * D_STATE
D_XBC = D_INNER + 2 * D_BC
D_MAIN = D_INNER + D_XBC
D_IN_PROJ = D_MAIN + SSM_HEADS
GROUP_W = HEADS_PER_GROUP * HEAD_DIM
SSM_CONV = 4
D_FF = 2816
FFN_CONV = 3
N_Q_HEADS = 16
N_KV_HEADS = 4
D_KV = 256
ATTN_SCALE = 1.0 / math.sqrt(HEAD_DIM)
RMS_EPS = 1e-6
NEG_INF = -1e30
LANES = 128
VMEM_LIMIT = 50 * 1024 * 1024

ADAM_LR, ADAM_B1, ADAM_B2, ADAM_EPS, ADAM_WD, ADAM_STEP = 0.001, 0.9, 0.999, 1e-08, 0.01, 10

N_CHIPS = 4


def _cparams(sem=None):
    return pltpu.CompilerParams(dimension_semantics=sem, vmem_limit_bytes=VMEM_LIMIT)


def _tile(n, cands=(512, 256, 128)):
    for t in cands:
        if n % t == 0:
            return t
    return n


def _row_tile(rows, width):
    for t in (544, 272):
        if rows % t == 0 and t * width * 4 <= (3 << 20):
            return t
    return 128


def _rows_mask(i, tm):
    rows = i * tm + lax.broadcasted_iota(jnp.int32, (tm, 1), 0)
    return rows >= PAD_ROWS


def _dot(a, b):
    return jnp.dot(a, b, preferred_element_type=F32)


def _dot_nt(a, b):
    return lax.dot_general(a, b, (((1,), (1,)), ((), ())), preferred_element_type=F32)


def _dot_tn(a, b):
    return lax.dot_general(a, b, (((0,), (0,)), ((), ())), preferred_element_type=F32)


def _sigmoid(x):
    return 1.0 / (1.0 + jnp.exp(-x))


def _place():
    return lax.axis_index("x"), lax.axis_index("y"), lax.axis_index("c")


def _other_chips(x, y):
    return [(1 - x, y), (x, 1 - y), (1 - x, 1 - y)]


class _Step:
    def __init__(self, ins, outs, aliases, n_sems, start, finish):
        self.ins, self.outs, self.aliases, self.n_sems = list(ins), list(outs), dict(aliases), n_sems
        self.start, self.finish = start, finish
        self.results = None


def _like(a):
    return jax.ShapeDtypeStruct(a.shape, a.dtype)


def _remote(src, dst, send_sems, recv_sems, k, device):
    return pltpu.make_async_remote_copy(src, dst, send_sems.at[k], recv_sems.at[k], device_id=device, device_id_type=MESH)


def _half(ref, split, which, lead=()):
    if split == "rows":
        hr = ref.shape[-2] // 2
        return ref.at[lead + (pl.ds(which * hr, hr),)]
    hc = ref.shape[-1] // 2
    return ref.at[lead + (slice(None), pl.ds(which * hc, hc))]


def _splits(bufs, splits):
    return list(splits) if splits is not None else ["rows"] * len(bufs)


ALL_PEERS = (0, 1, 2)
NEAR_PEERS = (0, 1)
FAR_PEERS = (2,)


def _step_gather_ici(bufs, splits=None, peers=ALL_PEERS):
    splits = _splits(bufs, splits)

    def copies(outs, send_sems, recv_sems, received):
        x, y, c = _place()
        me = 2 * x + y
        for k, o in enumerate(outs):
            for j, (cx, cy) in enumerate(_other_chips(x, y)):
                if j in peers:
                    part = _half(o, splits[k], c, (2 * cx + cy if received else me,))
                    yield _remote(part, part, send_sems, recv_sems, 3 * k + j, (cx, cy, c))

    def start(ins, outs, send_sems, recv_sems):
        for cp in copies(outs, send_sems, recv_sems, False):
            cp.start()

    def finish(ins, outs, send_sems, recv_sems):
        for cp in copies(outs, send_sems, recv_sems, True):
            cp.wait_recv()
        for cp in copies(outs, send_sems, recv_sems, False):
            cp.wait_send()

    return _Step(bufs, [_like(b) for b in bufs], {k: k for k in range(len(bufs))}, 3 * len(bufs), start, finish)


def _step_gather_d2d(bufs, splits=None):
    splits = _splits(bufs, splits)

    def copies(outs, send_sems, recv_sems, received):
        x, y, c = _place()
        for k, o in enumerate(outs):
            for j, (cx, cy) in enumerate(_other_chips(x, y)):
                part = _half(o, splits[k], 1 - c if received else c, (2 * cx + cy,))
                yield _remote(part, part, send_sems, recv_sems, 3 * k + j, (x, y, 1 - c))

    def start(ins, outs, send_sems, recv_sems):
        for cp in copies(outs, send_sems, recv_sems, False):
            cp.start()

    def finish(ins, outs, send_sems, recv_sems):
        for cp in copies(outs, send_sems, recv_sems, True):
            cp.wait_recv()
        for cp in copies(outs, send_sems, recv_sems, False):
            cp.wait_send()

    return _Step(bufs, [_like(b) for b in bufs], {k: k for k in range(len(bufs))}, 3 * len(bufs), start, finish)


def _step_gather_full(bufs, splits=None):
    n = len(bufs)
    splits = _splits(bufs, splits)

    def ici(outs, send_sems, recv_sems, received):
        x, y, c = _place()
        me = 2 * x + y
        for k, o in enumerate(outs):
            for j, (cx, cy) in enumerate(_other_chips(x, y)):
                part = _half(o, splits[k], c, (2 * cx + cy if received else me,))
                yield _remote(part, part, send_sems, recv_sems, 3 * k + j, (cx, cy, c))

    def d2d(outs, send_sems, recv_sems, received):
        x, y, c = _place()
        for k, o in enumerate(outs):
            for j, (cx, cy) in enumerate(_other_chips(x, y)):
                part = _half(o, splits[k], 1 - c if received else c, (2 * cx + cy,))
                yield _remote(part, part, send_sems, recv_sems, 3 * n + 3 * k + j, (x, y, 1 - c))

    def start(ins, outs, send_sems, recv_sems):
        for cp in ici(outs, send_sems, recv_sems, False):
            cp.start()

    def finish(ins, outs, send_sems, recv_sems):
        for arrived, onward in zip(ici(outs, send_sems, recv_sems, True), d2d(outs, send_sems, recv_sems, False)):
            arrived.wait_recv()
            onward.start()
        for cp in d2d(outs, send_sems, recv_sems, True):
            cp.wait_recv()
        for cp in ici(outs, send_sems, recv_sems, False):
            cp.wait_send()
        for cp in d2d(outs, send_sems, recv_sems, False):
            cp.wait_send()

    return _Step(bufs, [_like(b) for b in bufs], {k: k for k in range(n)}, 6 * n, start, finish)


def _half_shape(shape, split):
    return shape[:-2] + ((shape[-2] // 2, shape[-1]) if split == "rows" else (shape[-2], shape[-1] // 2))


def _step_pair_exchange(grads, splits=None):
    splits = _splits(grads, splits)

    def copies(ins, outs, send_sems, recv_sems):
        x, y, c = _place()
        for k, (g, o) in enumerate(zip(ins, outs)):
            yield _remote(_half(g, splits[k], 1 - c, (slice(None),)), o, send_sems, recv_sems, k, (x, y, 1 - c))

    def start(ins, outs, send_sems, recv_sems):
        for cp in copies(ins, outs, send_sems, recv_sems):
            cp.start()

    def finish(ins, outs, send_sems, recv_sems):
        for cp in copies(ins, outs, send_sems, recv_sems):
            cp.wait()

    outs = [jax.ShapeDtypeStruct(_half_shape(g.shape, s), g.dtype) for g, s in zip(grads, splits)]
    return _Step(grads, outs, {}, len(grads), start, finish)


def _step_chip_exchange(partials, peers=ALL_PEERS, into=None):
    n = len(partials)

    def copies(ins, outs, send_sems, recv_sems):
        x, y, c = _place()
        for k, (q, o) in enumerate(zip(ins[:n], outs)):
            for j, (cx, cy) in enumerate(_other_chips(x, y)):
                if j in peers:
                    yield _remote(q.at[2 * cx + cy], o.at[j], send_sems, recv_sems, 3 * k + j, (cx, cy, c))

    def start(ins, outs, send_sems, recv_sems):
        for cp in copies(ins, outs, send_sems, recv_sems):
            cp.start()

    def finish(ins, outs, send_sems, recv_sems):
        for cp in copies(ins, outs, send_sems, recv_sems):
            cp.wait()

    outs = [jax.ShapeDtypeStruct((3,) + q.shape[1:], q.dtype) for q in partials]
    if into is None:
        return _Step(partials, outs, {}, 3 * n, start, finish)
    return _Step(list(partials) + list(into), outs, {n + k: k for k in range(n)}, 3 * n, start, finish)


def _step_pair_gather(shards, splits=None):
    splits = _splits(shards, splits)

    def copies(outs, send_sems, recv_sems, received):
        x, y, c = _place()
        for k, o in enumerate(outs):
            part = _half(o, splits[k], 1 - c if received else c)
            yield _remote(part, part, send_sems, recv_sems, k, (x, y, 1 - c))

    def start(ins, outs, send_sems, recv_sems):
        for cp in copies(outs, send_sems, recv_sems, False):
            cp.start()

    def finish(ins, outs, send_sems, recv_sems):
        for cp in copies(outs, send_sems, recv_sems, True):
            cp.wait_recv()
        for cp in copies(outs, send_sems, recv_sems, False):
            cp.wait_send()

    return _Step(shards, [_like(s) for s in shards], {k: k for k in range(len(shards))}, len(shards), start, finish)


def _call(body, *, name, out_shape, grid, in_specs, out_specs, operands, scratch_shapes=(), semantics=None, steps=()):
    single = not isinstance(out_shape, (tuple, list))
    out_shapes = [out_shape] if single else list(out_shape)
    out_spec_list = [out_specs] if single else list(out_specs)
    steps = list(steps)
    if not steps:
        res = pl.pallas_call(body, name=name, out_shape=out_shapes, grid=grid, in_specs=list(in_specs),
                             out_specs=out_spec_list, scratch_shapes=list(scratch_shapes),
                             compiler_params=_cparams(semantics))(*operands)
        return res[0] if single else res
    n_in, n_out, n_scr = len(operands), len(out_shapes), len(scratch_shapes)
    x_in = [a for s in steps for a in s.ins]
    x_out = [o for s in steps for o in s.outs]
    aliases, in_off, out_off = {}, 0, 0
    for s in steps:
        for i, o in s.aliases.items():
            aliases[n_in + in_off + i] = n_out + out_off + o
        in_off += len(s.ins)
        out_off += len(s.outs)
    sems = []
    for s in steps:
        sems += [pltpu.SemaphoreType.DMA((s.n_sems,)), pltpu.SemaphoreType.DMA((s.n_sems,))]
    any_spec = pl.BlockSpec(memory_space=pl.ANY)

    def carried(*refs):
        pos = 0
        ins = refs[pos:pos + n_in]; pos += n_in
        xi = refs[pos:pos + len(x_in)]; pos += len(x_in)
        outs = refs[pos:pos + n_out]; pos += n_out
        xo = refs[pos:pos + len(x_out)]; pos += len(x_out)
        scr = refs[pos:pos + n_scr]; pos += n_scr
        sem_refs = refs[pos:]

        def each(action):
            i0 = o0 = 0
            for k, s in enumerate(steps):
                getattr(s, action)(xi[i0:i0 + len(s.ins)], xo[o0:o0 + len(s.outs)], sem_refs[2 * k], sem_refs[2 * k + 1])
                i0 += len(s.ins)
                o0 += len(s.outs)

        if grid:
            first = functools.reduce(jnp.logical_and, [pl.program_id(d) == 0 for d in range(len(grid))])
            last = functools.reduce(jnp.logical_and, [pl.program_id(d) == grid[d] - 1 for d in range(len(grid))])
            pl.when(first)(lambda: each("start"))
            body(*ins, *outs, *scr)
            pl.when(last)(lambda: each("finish"))
        else:
            each("start")
            body(*ins, *outs, *scr)
            each("finish")

    res = pl.pallas_call(
        carried, name=name, out_shape=out_shapes + x_out, grid=grid,
        in_specs=list(in_specs) + [any_spec] * len(x_in), out_specs=out_spec_list + [any_spec] * len(x_out),
        scratch_shapes=list(scratch_shapes) + sems, input_output_aliases=aliases,
        compiler_params=_cparams(None if semantics is None else ("arbitrary",) * len(grid)),
    )(*operands, *x_in)
    o0 = n_out
    for s in steps:
        s.results = list(res[o0:o0 + len(s.outs)])
        o0 += len(s.outs)
    return res[0] if single else tuple(res[:n_out])


def _run_steps(name, steps):
    _call(lambda: None, name=name, out_shape=[], grid=(), in_specs=[], out_specs=[], operands=[], steps=steps)
    return [s.results for s in steps]


def _mm(name, a, b, mode, out_dtype=F32, acc=None, b_colblock=0, k_rows=None, out_rows=None, steps=()):
    resident_bytes = 8 << 20
    if mode == "nn":
        m, k = a.shape
        n = b.shape[1]
        tm = m
        while tm * k * 2 > resident_bytes and tm % 32 == 0:
            tm //= 2
        tn = _tile(n)
        grid = (m // tm, n // tn)
        in_specs = [pl.BlockSpec((tm, k), lambda i, j: (i, 0)), pl.BlockSpec((k, tn), lambda i, j: (0, j))]
        out_shape, out_block = (m, n), (tm, tn)
    elif mode == "nt":
        m, n = a.shape
        k = k_rows or b.shape[0]
        tm = m
        while tm * n * 2 > resident_bytes and tm % 32 == 0:
            tm //= 2
        tk = _tile(k)
        grid = (m // tm, k // tk)
        in_specs = [pl.BlockSpec((tm, n), lambda i, j: (i, 0)), pl.BlockSpec((tk, n), lambda i, j: (j, b_colblock))]
        out_shape, out_block = (m, k), (tm, tk)
    else:
        m, k = a.shape
        n = b.shape[1]
        tk, tn = _tile(k), (n if m * n * 2 <= resident_bytes else _tile(n))
        grid = (k // tk, n // tn)
        in_specs = [pl.BlockSpec((m, tk), lambda i, j: (0, i)), pl.BlockSpec((m, tn), lambda i, j: (0, j))]
        out_shape, out_block = (out_rows or k, n), (tk, tn)
    out_spec = pl.BlockSpec(out_block, lambda i, j: (i, j))
    has_acc = acc is not None

    def body(*refs):
        a_ref, b_ref = refs[0], refs[1]
        o_ref = refs[-1]
        av, bv = a_ref[...], b_ref[...]
        if mode == "nn":
            r = _dot(av, bv)
        elif mode == "nt":
            r = _dot_nt(av, bv)
        else:
            r = _dot_tn(av, bv)
        if has_acc:
            r = r + refs[2][...]
        o_ref[...] = r.astype(o_ref.dtype)

    operands = [a, b]
    if has_acc:
        in_specs = in_specs + [out_spec]
        operands.append(acc)
    return _call(body, name=name, out_shape=jax.ShapeDtypeStruct(out_shape, out_dtype), grid=grid, in_specs=in_specs,
                 out_specs=out_spec, operands=operands, semantics=("parallel", "parallel"), steps=steps)


def _tn_rows_into(name, a, b, into, row0, nrows):
    m, k = a.shape
    n = b.shape[1]

    def body(a_ref, b_ref, into_ref, o_ref):
        o_ref[...] = _dot_tn(a_ref[...], b_ref[...])[0:nrows].astype(o_ref.dtype)

    return pl.pallas_call(
        body, name=name, out_shape=jax.ShapeDtypeStruct(into.shape, into.dtype), grid=(1,),
        in_specs=[pl.BlockSpec((m, k), lambda i: (0, 0)), pl.BlockSpec((m, n), lambda i: (0, 0)),
                  pl.BlockSpec(memory_space=pl.ANY)],
        out_specs=pl.BlockSpec((nrows, n), lambda i: (row0 // nrows, 0)),
        input_output_aliases={2: 0}, compiler_params=_cparams(("arbitrary",)),
    )(a, b, into)


def _rms_fwd(name, h, w):
    rows, width = h.shape
    tm = _row_tile(rows, width)

    def body(h_ref, w_ref, o_ref):
        x = h_ref[...]
        r = lax.rsqrt(jnp.mean(x * x, axis=-1, keepdims=True) + RMS_EPS)
        o_ref[...] = (x * r * w_ref[...]).astype(BF16)

    return pl.pallas_call(
        body, name=name, out_shape=jax.ShapeDtypeStruct((rows, width), BF16), grid=(rows // tm,),
        in_specs=[pl.BlockSpec((tm, width), lambda i: (i, 0)), pl.BlockSpec((1, width), lambda i: (0, 0))],
        out_specs=pl.BlockSpec((tm, width), lambda i: (i, 0)), compiler_params=_cparams(("parallel",)),
    )(h, w)


def _resid_norm_fwd(name, h, pre, w, next_norms=()):
    rows, width = h.shape
    tm = _row_tile(rows, width)
    n_next = len(next_norms)

    def body(*refs):
        h_ref, p_ref, w_ref = refs[:3]
        v_refs = refs[3:3 + n_next]
        o_ref = refs[3 + n_next]
        n_refs = refs[4 + n_next:]
        p = p_ref[...]
        r = lax.rsqrt(jnp.mean(p * p, axis=-1, keepdims=True) + RMS_EPS)
        x = h_ref[...] + jnp.where(_rows_mask(pl.program_id(0), tm), p * r * w_ref[...], 0.0)
        o_ref[...] = x
        if n_next:
            rx = lax.rsqrt(jnp.mean(x * x, axis=-1, keepdims=True) + RMS_EPS)
            for v_ref, n_ref in zip(v_refs, n_refs):
                n_ref[...] = (x * rx * v_ref[...]).astype(BF16)

    row_spec = pl.BlockSpec((tm, width), lambda i: (i, 0))
    vec_spec = pl.BlockSpec((1, width), lambda i: (0, 0))
    outs = pl.pallas_call(
        body, name=name,
        out_shape=[jax.ShapeDtypeStruct((rows, width), F32)] + [jax.ShapeDtypeStruct((rows, width), BF16)] * n_next,
        grid=(rows // tm,), in_specs=[row_spec, row_spec, vec_spec] + [vec_spec] * n_next,
        out_specs=[row_spec] * (1 + n_next), compiler_params=_cparams(("parallel",)),
    )(h, pre, w, *next_norms)
    return outs[0], list(outs[1:])


def _resid_norm_loss(name, h, pre, w, target):
    rows, width = h.shape

    def body(h_ref, p_ref, w_ref, t_ref, dh_ref, loss_ref, dp_ref, dw_ref):
        i = pl.program_id(0)
        p = p_ref[...]
        r = lax.rsqrt(jnp.mean(p * p, axis=-1, keepdims=True) + RMS_EPS)
        x = h_ref[...] + p * r * w_ref[...]
        real = (i + jnp.zeros((CHUNK, 1), jnp.int32)) >= 1
        diff = jnp.where(real, x - t_ref[...], 0.0)
        dh = diff * (1.0 / D_MODEL)
        dh_ref[...] = dh
        dp, dw_rows = _rms_bwd(dh, p, w_ref[...])
        dp_ref[...] = dp.astype(BF16)

        @pl.when(i == 0)
        def _():
            loss_ref[...] = jnp.zeros_like(loss_ref)
            dw_ref[...] = jnp.zeros_like(dw_ref)

        loss_ref[...] += jnp.sum(diff * diff) * (0.5 / D_MODEL)
        dw_ref[...] += jnp.sum(dw_rows, axis=0, keepdims=True)

    blk = pl.BlockSpec((CHUNK, width), lambda i: (i, 0))
    vec_spec = pl.BlockSpec((1, width), lambda i: (0, 0))
    return pl.pallas_call(
        body, name=name,
        out_shape=(jax.ShapeDtypeStruct((rows, width), F32), jax.ShapeDtypeStruct((1, LANES), F32),
                   jax.ShapeDtypeStruct((rows, width), BF16), jax.ShapeDtypeStruct((1, width), F32)),
        grid=(rows // CHUNK,),
        in_specs=[blk, blk, vec_spec, pl.BlockSpec((CHUNK, width), lambda i: (jnp.maximum(i - 1, 0), 0))],
        out_specs=(blk, pl.BlockSpec((1, LANES), lambda i: (0, 0)), blk, vec_spec),
        compiler_params=_cparams(("arbitrary",)),
    )(h, pre, w, target)


def _rms_bwd(dy, x, w):
    r = lax.rsqrt(jnp.mean(x * x, axis=-1, keepdims=True) + RMS_EPS)
    xhat = x * r
    dxhat = dy * w
    return r * (dxhat - xhat * jnp.mean(dxhat * xhat, axis=-1, keepdims=True)), dy * xhat


def _norm_bwd_add(name, dh, dhn, h, w, then=None, split_first_block=False, steps=()):
    rows, width = dh.shape
    tm = CHUNK if split_first_block else _row_tile(rows, width)
    fused = then is not None
    assert not (fused and split_first_block)

    def body(*refs):
        dh_ref, dhn_ref, h_ref, w_ref = refs[:4]
        o_ref, dw_ref = refs[6:8] if fused else refs[-2:]
        i = pl.program_id(0)
        valid = _rows_mask(i, tm)
        dx, dw_rows = _rms_bwd(dhn_ref[...], h_ref[...], w_ref[...])
        dh_new = dh_ref[...] + jnp.where(valid, dx, 0.0)
        if split_first_block:
            first_ref = refs[4]

            @pl.when(i == 0)
            def _():
                first_ref[...] = dh_new

            @pl.when(i > 0)
            def _():
                o_ref[...] = dh_new
        else:
            o_ref[...] = dh_new

        @pl.when(i == 0)
        def _():
            dw_ref[...] = jnp.zeros_like(dw_ref)

        dw_ref[...] += jnp.sum(dw_rows, axis=0, keepdims=True)
        if fused:
            p_ref, wp_ref, dp_ref, dwp_ref = refs[4], refs[5], refs[8], refs[9]
            dp, dwp_rows = _rms_bwd(jnp.where(valid, dh_new, 0.0), p_ref[...], wp_ref[...])
            dp_ref[...] = dp.astype(BF16)

            @pl.when(i == 0)
            def _():
                dwp_ref[...] = jnp.zeros_like(dwp_ref)

            dwp_ref[...] += jnp.sum(dwp_rows, axis=0, keepdims=True)

    row_spec = pl.BlockSpec((tm, width), lambda i: (i, 0))
    vec_spec = pl.BlockSpec((1, width), lambda i: (0, 0))
    row_f32, vec_f32 = jax.ShapeDtypeStruct((rows, width), F32), jax.ShapeDtypeStruct((1, width), F32)
    in_specs, operands = [row_spec, row_spec, row_spec, vec_spec], [dh, dhn, h, w]
    out_shape, out_specs = [row_f32, vec_f32], [row_spec, vec_spec]
    if split_first_block:
        out_shape = [jax.ShapeDtypeStruct((tm, width), F32), jax.ShapeDtypeStruct((rows - tm, width), F32), vec_f32]
        out_specs = [pl.BlockSpec((tm, width), lambda i: (0, 0)),
                     pl.BlockSpec((tm, width), lambda i: (jnp.maximum(i - 1, 0), 0)), vec_spec]
    if fused:
        in_specs += [row_spec, vec_spec]
        operands += list(then)
        out_shape += [jax.ShapeDtypeStruct((rows, width), BF16), vec_f32]
        out_specs += [row_spec, vec_spec]
    return _call(body, name=name, out_shape=out_shape, grid=(rows // tm,), in_specs=in_specs, out_specs=out_specs,
                 operands=operands, semantics=("arbitrary",), steps=steps)


def _shift_down(x, s, rows):
    return pltpu.roll(x, s, 0) if s else x


def _shift_up(x, s, rows):
    return pltpu.roll(x, rows - s, 0) if s else x


def _conv4_fwd(name, zx, cw, cb, steps=()):
    rows = zx.shape[0]
    off = D_INNER // LANES

    def body(x_ref, w_ref, b_ref, o_ref):
        x = x_ref[...]
        acc = b_ref[...] + w_ref[pl.ds(SSM_CONV - 1, 1), :] * x
        for s in range(1, SSM_CONV):
            acc = acc + w_ref[pl.ds(SSM_CONV - 1 - s, 1), :] * _shift_down(x, s, rows)
        valid = lax.broadcasted_iota(jnp.int32, (rows, 1), 0) >= PAD_ROWS
        o_ref[...] = jnp.where(valid, acc * _sigmoid(acc), 0.0)

    return _call(
        body, name=name, out_shape=jax.ShapeDtypeStruct((rows, D_XBC), F32), grid=(D_XBC // LANES,),
        in_specs=[pl.BlockSpec((rows, LANES), lambda j: (0, j + off)),
                  pl.BlockSpec((SSM_CONV, LANES), lambda j: (0, j)),
                  pl.BlockSpec((1, LANES), lambda j: (0, j))],
        out_specs=pl.BlockSpec((rows, LANES), lambda j: (0, j)), operands=[zx, cw, cb],
        semantics=("parallel",), steps=steps)


def _conv4_bwd(name, zx, dout, cw, cb, into):
    rows, width = dout.shape
    zoff = D_INNER // LANES

    def body(x_ref, d_ref, w_ref, b_ref, into_ref, dx_ref, dw_ref, db_ref):
        x = x_ref[...]
        shifted = [_shift_down(x, s, rows) for s in range(SSM_CONV)]
        acc = b_ref[...]
        for s in range(SSM_CONV):
            acc = acc + w_ref[pl.ds(SSM_CONV - 1 - s, 1), :] * shifted[s]
        sig = _sigmoid(acc)
        valid = lax.broadcasted_iota(jnp.int32, (rows, 1), 0) >= PAD_ROWS
        dpre = jnp.where(valid, d_ref[...] * sig * (1.0 + acc * (1.0 - sig)), 0.0)
        dx = w_ref[pl.ds(SSM_CONV - 1, 1), :] * dpre
        for s in range(1, SSM_CONV):
            dx = dx + w_ref[pl.ds(SSM_CONV - 1 - s, 1), :] * _shift_up(dpre, s, rows)
        dx_ref[...] = dx.astype(BF16)
        for s in range(SSM_CONV):
            dw_ref[pl.ds(SSM_CONV - 1 - s, 1), :] = jnp.sum(dpre * shifted[s], axis=0, keepdims=True)
        db_ref[...] = jnp.sum(dpre, axis=0, keepdims=True)

    return pl.pallas_call(
        body, name=name,
        out_shape=(jax.ShapeDtypeStruct(into.shape, BF16), jax.ShapeDtypeStruct((SSM_CONV, width), F32),
                   jax.ShapeDtypeStruct((1, width), F32)),
        grid=(width // LANES,),
        in_specs=[pl.BlockSpec((rows, LANES), lambda j: (0, j + zoff)),
                  pl.BlockSpec((rows, LANES), lambda j: (0, j)),
                  pl.BlockSpec((SSM_CONV, LANES), lambda j: (0, j)),
                  pl.BlockSpec((1, LANES), lambda j: (0, j)),
                  pl.BlockSpec(memory_space=pl.ANY)],
        out_specs=(pl.BlockSpec((rows, LANES), lambda j: (0, j + zoff)),
                   pl.BlockSpec((SSM_CONV, LANES), lambda j: (0, j)),
                   pl.BlockSpec((1, LANES), lambda j: (0, j))),
        input_output_aliases={4: 0}, compiler_params=_cparams(("parallel",)),
    )(zx, dout, cw, cb, into)


FFN_TILE = 2 * LANES


def _ffn_up_conv(name, hn, w_up, cw, cb, steps=()):
    rows, k = hn.shape
    chip_blocks = w_up.shape[2] // LANES
    half_blocks = D_FF // LANES
    nt = D_FF // FFN_TILE

    def weight_block(offset):
        return pl.BlockSpec((None, k, LANES), lambda j: ((2 * j + offset) // chip_blocks, 0, (2 * j + offset) % chip_blocks))

    def body(a_ref, g0, g1, v0, v1, wg_ref, wv_ref, bg_ref, bv_ref, upg_ref, upv_ref, act_ref):
        a = a_ref[...]
        g = _dot(a, jnp.concatenate([g0[...], g1[...]], axis=1))
        v = _dot(a, jnp.concatenate([v0[...], v1[...]], axis=1))
        upg_ref[...] = g
        upv_ref[...] = v
        ug, uv = bg_ref[...], bv_ref[...]
        for s in range(FFN_CONV):
            ug = ug + wg_ref[pl.ds(FFN_CONV - 1 - s, 1), :] * _shift_down(g, s, rows)
            uv = uv + wv_ref[pl.ds(FFN_CONV - 1 - s, 1), :] * _shift_down(v, s, rows)
        act_ref[...] = (ug * _sigmoid(ug) * uv).astype(BF16)

    col = pl.BlockSpec((rows, FFN_TILE), lambda j: (0, j))
    wsp = lambda shift: pl.BlockSpec((FFN_CONV, FFN_TILE), lambda j: (0, j + shift))
    bsp = lambda shift: pl.BlockSpec((1, FFN_TILE), lambda j: (0, j + shift))
    half = jax.ShapeDtypeStruct((rows, D_FF), F32)
    return _call(
        body, name=name, out_shape=(half, half, jax.ShapeDtypeStruct((rows, D_FF), BF16)), grid=(nt,),
        in_specs=[pl.BlockSpec((rows, k), lambda j: (0, 0)), weight_block(0), weight_block(1),
                  weight_block(half_blocks), weight_block(half_blocks + 1), wsp(0), wsp(nt), bsp(0), bsp(nt)],
        out_specs=(col, col, col), operands=[hn, w_up, w_up, w_up, w_up, cw, cw, cb, cb],
        semantics=("parallel",), steps=steps)


def _ffn_conv_bwd(name, up_g, up_v, dact, cw, cb, hn, w_up, steps=()):
    rows, k = hn.shape
    chip_blocks = w_up.shape[2] // LANES
    nt = D_FF // LANES

    def weight_block(shift):
        return pl.BlockSpec((None, k, LANES), lambda j: ((j + shift) // chip_blocks, 0, (j + shift) % chip_blocks))

    def body(g_ref, v_ref, d_ref, wg_ref, wv_ref, bg_ref, bv_ref, upg_ref, upv_ref, hn_ref,
             dwg_ref, dwv_ref, dbg_ref, dbv_ref, dhn_ref, dup_ref, acc, hn_scr, hnt_scr, dup_scr, sems):
        j = pl.program_id(0)
        hn_copy = pltpu.make_async_copy(hn_ref, hn_scr, sems.at[0])
        dhn_copy = pltpu.make_async_copy(acc, dhn_ref, sems.at[0])

        def dup_copy(step, half):
            block, slot = step + half * nt, 2 * (step % 2) + half
            cols = pl.ds(pl.multiple_of((block % chip_blocks) * LANES, LANES), LANES)
            return pltpu.make_async_copy(dup_scr.at[slot], dup_ref.at[block // chip_blocks, :, cols], sems.at[1 + slot])

        @pl.when(j == 0)
        def _():
            hn_copy.start()
            acc[...] = jnp.zeros_like(acc)
            hn_copy.wait()
            for r in range(0, rows, LANES):
                hnt_scr[:, r:r + LANES] = hn_scr[r:r + LANES, :].T

        @pl.when(j >= 2)
        def _():
            dup_copy(j - 2, 0).wait()
            dup_copy(j - 2, 1).wait()

        g, v = g_ref[...], v_ref[...]
        gs = [_shift_down(g, s, rows) for s in range(FFN_CONV)]
        vs = [_shift_down(v, s, rows) for s in range(FFN_CONV)]
        ug, uv = bg_ref[...], bv_ref[...]
        for s in range(FFN_CONV):
            ug = ug + wg_ref[pl.ds(FFN_CONV - 1 - s, 1), :] * gs[s]
            uv = uv + wv_ref[pl.ds(FFN_CONV - 1 - s, 1), :] * vs[s]
        sig = _sigmoid(ug)
        dsig = d_ref[...] * sig
        dup = []
        for dpre, src, w_ref, dw_ref, db_ref in (
                (dsig * uv * (1.0 + ug * (1.0 - sig)), gs, wg_ref, dwg_ref, dbg_ref),
                (dsig * ug, vs, wv_ref, dwv_ref, dbv_ref)):
            dx = w_ref[pl.ds(FFN_CONV - 1, 1), :] * dpre
            for s in range(1, FFN_CONV):
                dx = dx + w_ref[pl.ds(FFN_CONV - 1 - s, 1), :] * _shift_up(dpre, s, rows)
            dup.append(dx.astype(BF16))
            for s in range(FFN_CONV):
                dw_ref[pl.ds(FFN_CONV - 1 - s, 1), :] = jnp.sum(dpre * src[s], axis=0, keepdims=True)
            db_ref[...] = jnp.sum(dpre, axis=0, keepdims=True)
        dup = jnp.concatenate(dup, axis=1)
        acc[...] += _dot_nt(dup, jnp.concatenate([upg_ref[...], upv_ref[...]], axis=1))
        dw = _dot(hnt_scr[...], dup)
        slot = 2 * (j % 2)
        dup_scr[slot] = dw[:, :LANES].astype(BF16)
        dup_scr[slot + 1] = dw[:, LANES:].astype(BF16)
        dup_copy(j, 0).start()
        dup_copy(j, 1).start()

        @pl.when(j == nt - 1)
        def _():
            dhn_copy.start()
            for step in (j - 1, j):
                dup_copy(step, 0).wait()
                dup_copy(step, 1).wait()
            dhn_copy.wait()

    col = pl.BlockSpec((rows, LANES), lambda j: (0, j))
    wsp = lambda shift: pl.BlockSpec((FFN_CONV, LANES), lambda j: (0, j + shift))
    bsp = lambda shift: pl.BlockSpec((1, LANES), lambda j: (0, j + shift))
    any_spec = pl.BlockSpec(memory_space=pl.ANY)
    dw_shape = jax.ShapeDtypeStruct((FFN_CONV, D_FF), F32)
    db_shape = jax.ShapeDtypeStruct((1, D_FF), F32)
    return _call(
        body, name=name, grid=(nt,),
        out_shape=(dw_shape, dw_shape, db_shape, db_shape, jax.ShapeDtypeStruct((rows, k), F32),
                   jax.ShapeDtypeStruct(w_up.shape, BF16)),
        in_specs=[col, col, col, wsp(0), wsp(nt), bsp(0), bsp(nt), weight_block(0), weight_block(nt), any_spec],
        out_specs=(wsp(0), wsp(0), bsp(0), bsp(0), any_spec, any_spec),
        operands=[up_g, up_v, dact, cw, cw, cb, cb, w_up, w_up, hn],
        scratch_shapes=[pltpu.VMEM((rows, k), F32), pltpu.VMEM((rows, k), BF16), pltpu.VMEM((k, rows), BF16),
                        pltpu.VMEM((4, k, LANES), BF16), pltpu.SemaphoreType.DMA((5,))],
        semantics=("arbitrary",), steps=steps)


def _dt_fwd(name, dtr, bias):
    rows = dtr.shape[0]
    tm = _row_tile(rows, LANES)

    def body(d_ref, b_ref, o_ref):
        v = d_ref[...] + b_ref[...]
        sp = jnp.maximum(v, 0.0) + jnp.log1p(jnp.exp(-jnp.abs(v)))
        lane = lax.broadcasted_iota(jnp.int32, (tm, LANES), 1)
        ok = _rows_mask(pl.program_id(0), tm) & (lane < SSM_HEADS)
        o_ref[...] = jnp.where(ok, sp, 0.0)

    return pl.pallas_call(
        body, name=name, out_shape=jax.ShapeDtypeStruct((rows, LANES), F32), grid=(rows // tm,),
        in_specs=[pl.BlockSpec((tm, LANES), lambda i: (i, 0)), pl.BlockSpec((1, LANES), lambda i: (0, 0))],
        out_specs=pl.BlockSpec((tm, LANES), lambda i: (i, 0)), compiler_params=_cparams(("parallel",)),
    )(dtr, bias)


def _dt_bwd(name, ddt, dtr, bias):
    rows = dtr.shape[0]
    tm = _row_tile(rows, LANES)

    def body(g_ref, d_ref, b_ref, o_ref, db_ref):
        i = pl.program_id(0)
        lane = lax.broadcasted_iota(jnp.int32, (tm, LANES), 1)
        ok = _rows_mask(i, tm) & (lane < SSM_HEADS)
        dv = jnp.where(ok, g_ref[...] * _sigmoid(d_ref[...] + b_ref[...]), 0.0)
        o_ref[...] = dv.astype(BF16)

        @pl.when(i == 0)
        def _():
            db_ref[...] = jnp.zeros_like(db_ref)

        db_ref[...] += jnp.sum(dv, axis=0, keepdims=True)

    row_spec = pl.BlockSpec((tm, LANES), lambda i: (i, 0))
    vec_spec = pl.BlockSpec((1, LANES), lambda i: (0, 0))
    return pl.pallas_call(
        body, name=name,
        out_shape=(jax.ShapeDtypeStruct((rows, LANES), BF16), jax.ShapeDtypeStruct((1, LANES), F32)),
        grid=(rows // tm,), in_specs=[row_spec, row_spec, vec_spec], out_specs=(row_spec, vec_spec),
        compiler_params=_cparams(("arbitrary",)),
    )(ddt, dtr, bias)


def _gate_fwd(name, y, zx, w, steps=()):
    rows = y.shape[0]
    tm = _row_tile(rows, D_INNER)

    def body(y_ref, z_ref, w_ref, o_ref):
        z = z_ref[...]
        g = y_ref[...] * (z * _sigmoid(z))
        r = lax.rsqrt(jnp.mean(g * g, axis=-1, keepdims=True) + RMS_EPS)
        o_ref[...] = (g * r * w_ref[...]).astype(BF16)

    row_spec = pl.BlockSpec((tm, D_INNER), lambda i: (i, 0))
    return _call(
        body, name=name, out_shape=jax.ShapeDtypeStruct((rows, D_INNER), BF16), grid=(rows // tm,),
        in_specs=[row_spec, row_spec, pl.BlockSpec((1, D_INNER), lambda i: (0, 0))],
        out_specs=row_spec, operands=[y, zx, w], semantics=("parallel",), steps=steps)


def _gate_bwd(name, dyn, y, zx, w):
    rows = y.shape[0]
    tm = _row_tile(rows, D_INNER)

    def body(d_ref, y_ref, z_ref, w_ref, dy_ref, dz_ref, dw_ref):
        i = pl.program_id(0)
        z, yv = z_ref[...], y_ref[...]
        sig = _sigmoid(z)
        sz = z * sig
        g = yv * sz
        r = lax.rsqrt(jnp.mean(g * g, axis=-1, keepdims=True) + RMS_EPS)
        ghat = g * r
        dn = d_ref[...]
        dghat = dn * w_ref[...]
        dg = r * (dghat - ghat * jnp.mean(dghat * ghat, axis=-1, keepdims=True))
        dy_ref[...] = dg * sz
        dz_ref[...] = (dg * yv * sig * (1.0 + z * (1.0 - sig))).astype(BF16)

        @pl.when(i == 0)
        def _():
            dw_ref[...] = jnp.zeros_like(dw_ref)

        dw_ref[...] += jnp.sum(dn * ghat, axis=0, keepdims=True)

    row_spec = pl.BlockSpec((tm, D_INNER), lambda i: (i, 0))
    vec_spec = pl.BlockSpec((1, D_INNER), lambda i: (0, 0))
    return pl.pallas_call(
        body, name=name,
        out_shape=(jax.ShapeDtypeStruct((rows, D_INNER), F32), jax.ShapeDtypeStruct((rows, D_MAIN), BF16),
                   jax.ShapeDtypeStruct((1, D_INNER), F32)),
        grid=(rows // tm,), in_specs=[row_spec, row_spec, row_spec, vec_spec],
        out_specs=(row_spec, row_spec, vec_spec), compiler_params=_cparams(("arbitrary",)),
    )(dyn, y, zx, w)


def _split3(x):
    hi = x.astype(BF16)
    r1 = x - hi.astype(F32)
    mid = r1.astype(BF16)
    lo = (r1 - mid.astype(F32)).astype(BF16)
    return hi, mid, lo


def _dot3_data_lhs(x, sel):
    sel16 = sel.astype(F32).astype(BF16)
    hi, mid, lo = _split3(x)
    return _dot(hi, sel16) + _dot(mid, sel16) + _dot(lo, sel16)


def _dot2_data_lhs(x, sel):
    sel16 = sel.astype(F32).astype(BF16)
    hi = x.astype(BF16)
    mid = (x - hi.astype(F32)).astype(BF16)
    return _dot(hi, sel16) + _dot(mid, sel16)


def _dot3_data_rhs(sel, x):
    sel16 = sel.astype(F32).astype(BF16)
    hi, mid, lo = _split3(x)
    return _dot(sel16, hi) + _dot(sel16, mid) + _dot(sel16, lo)


def _causal_masks():
    r = lax.broadcasted_iota(jnp.int32, (CHUNK, CHUNK), 0)
    c = lax.broadcasted_iota(jnp.int32, (CHUNK, CHUNK), 1)
    return r >= c, r <= c


def _expand_heads_matrix(g):
    k = lax.broadcasted_iota(jnp.int32, (LANES, GROUP_W), 0)
    j = lax.broadcasted_iota(jnp.int32, (LANES, GROUP_W), 1)
    return HEADS_PER_GROUP * g + jnp.right_shift(j, 6) == k


def _reduce_heads_matrix(g):
    j = lax.broadcasted_iota(jnp.int32, (GROUP_W, LANES), 0)
    k = lax.broadcasted_iota(jnp.int32, (GROUP_W, LANES), 1)
    return HEADS_PER_GROUP * g + jnp.right_shift(j, 6) == k


def _reduce_pair_matrix(g, p):
    j = lax.broadcasted_iota(jnp.int32, (LANES, LANES), 0)
    k = lax.broadcasted_iota(jnp.int32, (LANES, LANES), 1)
    return HEADS_PER_GROUP * g + 2 * p + jnp.right_shift(j, 6) == k


def _group_cols(ref, g, width):
    return ref.at[:, pl.ds(g * width, width)]


def _ssd_prep(name, dt, a128, steps=()):
    rows = dt.shape[0]
    nc = rows // CHUNK

    def body(dt_ref, a_ref, dte_ref, acs_ref, acst_ref):
        causal, _ = _causal_masks()
        dtv = dt_ref[...]
        acs = _dot3_data_rhs(causal, dtv) * a_ref[...]
        acst_ref[...] = acs.T[0:SSM_HEADS]
        for g in range(N_GROUPS):
            expand = _expand_heads_matrix(g)
            _group_cols(dte_ref, g, GROUP_W)[...] = _dot3_data_lhs(dtv, expand)
            _group_cols(acs_ref, g, GROUP_W)[...] = _dot3_data_lhs(acs, expand)

    blk = pl.BlockSpec((CHUNK, D_INNER), lambda c: (c, 0))
    shp = jax.ShapeDtypeStruct((rows, D_INNER), F32)
    return _call(
        body, name=name, out_shape=(shp, shp, jax.ShapeDtypeStruct((nc, SSM_HEADS, CHUNK), F32)), grid=(nc,),
        in_specs=[pl.BlockSpec((CHUNK, LANES), lambda c: (c, 0)), pl.BlockSpec((1, LANES), lambda c: (0, 0))],
        out_specs=(blk, blk, pl.BlockSpec((None, SSM_HEADS, CHUNK), lambda c: (c, 0, 0))),
        operands=[dt, a128], semantics=("parallel",), steps=steps)


def _ssd_common(x_ref, b_ref, c_ref, dte_ref, acs_ref):
    x = x_ref[...]
    dt_exp = dte_ref[...]
    acs_exp = acs_ref[...]
    tot_exp = acs_ref[pl.ds(CHUNK - 1, 1), :]
    xdt = x * dt_exp
    e_exp = jnp.exp(acs_exp)
    f_exp = jnp.exp(tot_exp - acs_exp)
    return _causal_masks(), x, dt_exp, acs_exp, tot_exp, xdt, e_exp, f_exp, b_ref[...], c_ref[...]


def _pair_decay(acs_pair, acs_row, e, causal):
    lane = lax.broadcasted_iota(jnp.int32, (CHUNK, LANES), 1)
    mine = (lane < HEAD_DIM) if e == 0 else (lane >= HEAD_DIM)
    a_l = jnp.where(mine, acs_pair, pltpu.roll(acs_pair, HEAD_DIM, 1))
    seg = a_l - acs_row
    dm = jnp.where(causal[0], jnp.exp(jnp.minimum(seg, 0.0)), 0.0)
    dmt = jnp.where(causal[1], jnp.exp(jnp.minimum(-seg, 0.0)), 0.0)
    return dm, dmt


def _ssd_specs(index_of_chunk):
    wide = pl.BlockSpec((CHUNK, D_INNER), lambda c: (index_of_chunk(c), 0))
    b_spec = pl.BlockSpec((CHUNK, D_BC), lambda c: (index_of_chunk(c), D_INNER // D_BC))
    c_spec = pl.BlockSpec((CHUNK, D_BC), lambda c: (index_of_chunk(c), D_INNER // D_BC + 1))
    rows_spec = pl.BlockSpec((None, SSM_HEADS, CHUNK), lambda c: (index_of_chunk(c), 0, 0))
    state_spec = pl.BlockSpec((N_GROUPS, None, D_STATE, GROUP_W), lambda c: (0, index_of_chunk(c), 0, 0))
    return wide, b_spec, c_spec, rows_spec, state_spec


def _ssd_fwd(name, xbc, dt_exp, acs_exp, acs_rows, dskexp, steps=()):
    rows = xbc.shape[0]
    nc = rows // CHUNK

    def body(x_ref, b_ref, c_ref, dte_ref, acs_ref, acst_ref, dsk_ref, y_ref, st_ref, s_scr):
        @pl.when(pl.program_id(0) == 0)
        def _():
            s_scr[...] = jnp.zeros_like(s_scr)

        lane = lax.broadcasted_iota(jnp.int32, (CHUNK, LANES), 1)
        for g in range(N_GROUPS):
            y_g = _group_cols(y_ref, g, GROUP_W)
            causal, x, _, acs_exp_v, tot_exp, xdt, e_exp, f_exp, bm, cm = _ssd_common(
                _group_cols(x_ref, g, GROUP_W), _group_cols(b_ref, g, D_STATE), _group_cols(c_ref, g, D_STATE),
                _group_cols(dte_ref, g, GROUP_W), _group_cols(acs_ref, g, GROUP_W))
            state = s_scr[g]
            st_ref[g] = state
            cb16, bb16 = cm.astype(BF16), bm.astype(BF16)
            cb = _dot_nt(cb16, bb16)
            base = e_exp * _dot(cb16, state.astype(BF16)) + _group_cols(dsk_ref, g, GROUP_W)[...] * x
            for p in range(HEADS_PER_GROUP // 2):
                sl = slice(p * LANES, (p + 1) * LANES)
                xp = xdt[:, sl].astype(BF16)
                yd = []
                for e in range(2):
                    acs_row = acst_ref[pl.ds(g * HEADS_PER_GROUP + 2 * p + e, 1), :]
                    dm, _ = _pair_decay(acs_exp_v[:, sl], acs_row, e, causal)
                    yd.append(_dot((cb * dm).astype(BF16), xp))
                y_g[:, sl] = jnp.where(lane < HEAD_DIM, yd[0], yd[1]) + base[:, sl]
            s_scr[g] = jnp.exp(tot_exp) * state + _dot_tn(bb16, (f_exp * xdt).astype(BF16))

    wide, b_spec, c_spec, rows_spec, state_spec = _ssd_specs(lambda c: c)
    return _call(
        body, name=name,
        out_shape=(jax.ShapeDtypeStruct((rows, D_INNER), F32),
                   jax.ShapeDtypeStruct((N_GROUPS, nc, D_STATE, GROUP_W), F32)),
        grid=(nc,),
        in_specs=[wide, b_spec, c_spec, wide, wide, rows_spec, pl.BlockSpec((1, D_INNER), lambda c: (0, 0))],
        out_specs=(wide, state_spec),
        scratch_shapes=[pltpu.VMEM((N_GROUPS, D_STATE, GROUP_W), F32)],
        operands=[xbc, xbc, xbc, dt_exp, acs_exp, acs_rows, dskexp], semantics=("arbitrary",), steps=steps)


def _ssd_bwd(name, xbc, dt_exp, acs_exp, acs_rows, dt, a128, dskexp, dy, states, steps=()):
    rows = xbc.shape[0]
    nc = rows // CHUNK
    last = nc - 1

    def body(x_ref, b_ref, c_ref, dte_ref, acs_ref, acst_ref, dt_ref, a128_ref, dsk_all, dy_all, st_all,
             dxbc_all, ddt_ref, dalog_ref, ddsk_ref, ds_all):
        dx_all, db_all, dc_all = (dxbc_all.at[:, :D_INNER], dxbc_all.at[:, D_INNER:D_INNER + D_BC],
                                  dxbc_all.at[:, D_INNER + D_BC:])

        @pl.when(pl.program_id(0) == 0)
        def _():
            ds_all[...] = jnp.zeros_like(ds_all)
            dalog_ref[...] = jnp.zeros_like(dalog_ref)
            ddsk_ref[...] = jnp.zeros_like(ddsk_ref)

        dacs = jnp.zeros((CHUNK, LANES), F32)
        ddt_x = jnp.zeros((CHUNK, LANES), F32)
        for g in range(N_GROUPS):
            dacs_g, ddt_x_g = group(
                g, _group_cols(x_ref, g, GROUP_W), _group_cols(b_ref, g, D_STATE), _group_cols(c_ref, g, D_STATE),
                _group_cols(dte_ref, g, GROUP_W), _group_cols(acs_ref, g, GROUP_W), acst_ref,
                _group_cols(dsk_all, g, GROUP_W), _group_cols(dy_all, g, GROUP_W), st_all.at[g],
                _group_cols(dx_all, g, GROUP_W), _group_cols(db_all, g, D_STATE), _group_cols(dc_all, g, D_STATE),
                ddsk_ref, ds_all.at[g])
            dacs, ddt_x = dacs + dacs_g, ddt_x + ddt_x_g
        _, causal_t = _causal_masks()
        da = _dot3_data_rhs(causal_t, dacs)
        ddt_ref[...] = da * a128_ref[...] + ddt_x
        dalog_ref[...] += jnp.sum(da * dt_ref[...], axis=0, keepdims=True) * a128_ref[...]

    def group(g, x_ref, b_ref, c_ref, dte_ref, acs_ref, acst_ref, dsk_ref, dy_ref, st_ref,
              dx_ref, db_ref, dc_ref, ddsk_ref, ds_scr):
        causal, x, dt_exp, acs_exp_v, tot_exp, xdt, e_exp, f_exp, bm, cm = _ssd_common(
            x_ref, b_ref, c_ref, dte_ref, acs_ref)
        reduce_heads = _reduce_heads_matrix(g)
        state, dstate = st_ref[...], ds_scr[...]
        dyv = dy_ref[...]
        cb16, bb16 = cm.astype(BF16), bm.astype(BF16)
        s16, ds16 = state.astype(BF16), dstate.astype(BF16)
        cb = _dot_nt(cb16, bb16)
        cbt = _dot_nt(bb16, cb16)
        cs = _dot(cb16, s16)
        bds = _dot(bb16, ds16)
        edy = e_exp * dyv
        fx = f_exp * xdt
        dxdt_base = f_exp * bds
        dc_acc = _dot_nt(edy.astype(BF16), s16)
        db_acc = _dot_nt(fx.astype(BF16), ds16)
        ds_scr[...] = jnp.exp(tot_exp) * dstate + _dot_tn(cb16, edy.astype(BF16))
        q = fx * bds
        dacs = _dot2_data_lhs(edy * cs - q, reduce_heads)
        dtot = jnp.sum(_dot2_data_lhs(q + jnp.exp(tot_exp) * dstate * state, reduce_heads), axis=0, keepdims=True)
        ddsk_ref[...] += jnp.sum(_dot2_data_lhs(dyv * x, reduce_heads), axis=0, keepdims=True)
        lane = lax.broadcasted_iota(jnp.int32, (CHUNK, LANES), 1)
        dcb = jnp.zeros((CHUNK, CHUNK), F32)
        dcbt = jnp.zeros((CHUNK, CHUNK), F32)
        ddt_x = jnp.zeros((CHUNK, LANES), F32)
        for p in range(HEADS_PER_GROUP // 2):
            sl = slice(p * LANES, (p + 1) * LANES)
            xp, dyp = xdt[:, sl], dyv[:, sl]
            xp16, dyp16 = xp.astype(BF16), dyp.astype(BF16)
            dxh = []
            for e in range(2):
                h = 2 * p + e
                mine = (lane < HEAD_DIM) if e == 0 else (lane >= HEAD_DIM)
                acs_row = acst_ref[pl.ds(g * HEADS_PER_GROUP + h, 1), :]
                dm, dmt = _pair_decay(acs_exp_v[:, sl], acs_row, e, causal)
                m, mt = cb * dm, cbt * dmt
                xh16 = jnp.where(mine, xp, 0.0).astype(BF16)
                dyh16 = jnp.where(mine, dyp, 0.0).astype(BF16)
                d_m = _dot_nt(dyh16, xp16)
                d_mt = _dot_nt(xh16, dyp16)
                dacs_h = (jnp.sum(d_m * m, axis=-1, keepdims=True)
                          - jnp.sum(d_mt * mt, axis=-1, keepdims=True))
                dacs = dacs + jnp.where(lane == HEADS_PER_GROUP * g + h, dacs_h, 0.0)
                dcb = dcb + d_m * dm
                dcbt = dcbt + d_mt * dmt
                dxh.append(_dot(mt.astype(BF16), dyp16))
            dxdt = jnp.where(lane < HEAD_DIM, dxh[0], dxh[1]) + dxdt_base[:, sl]
            dx_ref[:, sl] = dxdt * dt_exp[:, sl] + dsk_ref[:, sl] * dyp
            ddt_x = ddt_x + _dot2_data_lhs(dxdt * x[:, sl], _reduce_pair_matrix(g, p))
        dc_ref[...] = dc_acc + _dot(dcb.astype(BF16), bb16)
        db_ref[...] = db_acc + _dot(dcbt.astype(BF16), cb16)
        row = lax.broadcasted_iota(jnp.int32, (CHUNK, LANES), 0)
        return dacs + jnp.where(row == CHUNK - 1, dtot, 0.0), ddt_x

    wide, b_spec, c_spec, rows_spec, state_spec = _ssd_specs(lambda c: last - c)
    heads_spec = pl.BlockSpec((CHUNK, LANES), lambda c: (last - c, 0))
    vec_spec = pl.BlockSpec((1, LANES), lambda c: (0, 0))
    vec_shape = jax.ShapeDtypeStruct((1, LANES), F32)
    return _call(
        body, name=name,
        out_shape=(jax.ShapeDtypeStruct((rows, D_XBC), F32), jax.ShapeDtypeStruct((rows, LANES), F32),
                   vec_shape, vec_shape),
        grid=(nc,),
        in_specs=[wide, b_spec, c_spec, wide, wide, rows_spec, heads_spec, vec_spec,
                  pl.BlockSpec((1, D_INNER), lambda c: (0, 0)), wide, state_spec],
        out_specs=(pl.BlockSpec((CHUNK, D_XBC), lambda c: (last - c, 0)), heads_spec, vec_spec, vec_spec),
        scratch_shapes=[pltpu.VMEM((N_GROUPS, D_STATE, GROUP_W), F32)],
        operands=[xbc, xbc, xbc, dt_exp, acs_exp, acs_rows, dt, a128, dskexp, dy, states],
        semantics=("arbitrary",), steps=steps)


def _attn_visible(b, heads=1):
    row = jnp.bitwise_and(lax.broadcasted_iota(jnp.int32, (heads * CHUNK, 3 * CHUNK), 0), CHUNK - 1)
    col = lax.broadcasted_iota(jnp.int32, (heads * CHUNK, 3 * CHUNK), 1)
    bb = b + jnp.zeros_like(col)
    meta = (col < CHUNK) & (bb >= 1) & (col >= PAD_ROWS)
    prev = (col >= CHUNK) & (col < 2 * CHUNK) & (bb >= 2) & ((col - CHUNK) > row)
    cur = (col >= 2 * CHUNK) & ((col - 2 * CHUNK) <= row) & ((bb >= 1) | ((col - 2 * CHUNK) >= PAD_ROWS))
    return meta | prev | cur


def _attn_visible4(b):
    return _attn_visible(b, 4)


def _stack_heads(q_ref, sink_ref, kvh, scale):
    lane = lax.broadcasted_iota(jnp.int32, (CHUNK, LANES), 1)
    parts, sinks = [], []
    for pp in range(2):
        pair = kvh * 2 + pp
        qp = q_ref[:, pair * LANES:(pair + 1) * LANES] * scale
        for e in range(2):
            mine = (lane < HEAD_DIM) if e == 0 else (lane >= HEAD_DIM)
            parts.append(jnp.where(mine, qp, 0.0).astype(BF16))
            sinks.append(jnp.full((CHUNK, 1), sink_ref[2 * pair + e], F32))
    return jnp.concatenate(parts, axis=0), jnp.concatenate(sinks, axis=0)


def _attn_operands(q_ref, k0, kp, kc, v0, vp, vc, sink_ref):
    kcat, vcat, q4, sink4 = [], [], [], []
    for kvh in range(N_KV_HEADS):
        ksl = slice(kvh * LANES, (kvh + 1) * LANES)
        kcat.append(jnp.concatenate([k0[:, ksl], kp[:, ksl], kc[:, ksl]], axis=0).astype(BF16))
        vcat.append(jnp.concatenate([v0[:, ksl], vp[:, ksl], vc[:, ksl]], axis=0).astype(BF16))
        stacked, sinks = _stack_heads(q_ref, sink_ref, kvh, ATTN_SCALE)
        q4.append(stacked)
        sink4.append(sinks)
    return kcat, vcat, q4, sink4


def _attn_probs(q4, kcat, visible, sink4):
    heads = range(N_KV_HEADS)
    s = [jnp.where(visible, _dot_nt(q4[h], kcat[h]), NEG_INF) for h in heads]
    m = [jnp.maximum(jnp.max(s[h], axis=-1, keepdims=True), sink4[h]) for h in heads]
    pe = [jnp.exp(s[h] - m[h]) for h in heads]
    pe_sink = [jnp.exp(sink4[h] - m[h]) for h in heads]
    inv = [1.0 / (jnp.sum(pe[h], axis=-1, keepdims=True) + pe_sink[h]) for h in heads]
    return [pe[h] * inv[h] for h in heads], [pe_sink[h] * inv[h] for h in heads]


def _unstack_pairs(stacked, pp):
    lane = lax.broadcasted_iota(jnp.int32, (CHUNK, LANES), 1)
    return jnp.where(lane < HEAD_DIM, stacked[(2 * pp) * CHUNK:(2 * pp + 1) * CHUNK],
                     stacked[(2 * pp + 1) * CHUNK:(2 * pp + 2) * CHUNK])


def _attn_specs(colblock):
    blk = lambda f: pl.BlockSpec((CHUNK, 2 * D_KV), f)
    return [blk(lambda b: (0, colblock)), blk(lambda b: (jnp.maximum(b - 1, 0), colblock)), blk(lambda b: (b, colblock))]


def _attn_fwd(name, q, kv2, sinks, steps=()):
    rows = q.shape[0]

    def body(q_ref, k0, kp, kc, v0, vp, vc, sink_ref, o_ref):
        visible = _attn_visible4(pl.program_id(0))
        kcat, vcat, q4, sink4 = _attn_operands(q_ref, k0, kp, kc, v0, vp, vc, sink_ref)
        pn, _ = _attn_probs(q4, kcat, visible, sink4)
        o4 = [_dot(pn[h].astype(BF16), vcat[h]) for h in range(N_KV_HEADS)]
        for kvh in range(N_KV_HEADS):
            for pp in range(2):
                qsl = slice((kvh * 2 + pp) * LANES, (kvh * 2 + pp + 1) * LANES)
                o_ref[:, qsl] = _unstack_pairs(o4[kvh], pp).astype(BF16)

    return _call(
        body, name=name, out_shape=jax.ShapeDtypeStruct((rows, D_MODEL), BF16), grid=(rows // CHUNK,),
        in_specs=[pl.BlockSpec((CHUNK, D_MODEL), lambda b: (b, 0))] + _attn_specs(0) + _attn_specs(1)
        + [pl.BlockSpec(memory_space=pltpu.SMEM)],
        out_specs=pl.BlockSpec((CHUNK, D_MODEL), lambda b: (b, 0)),
        operands=[q, kv2, kv2, kv2, kv2, kv2, kv2, sinks], semantics=("parallel",), steps=steps)


def _attn_bwd(name, q, kv2, sinks, do, steps=()):
    rows = q.shape[0]

    def body(q_ref, k0, kp, kc, v0, vp, vc, sink_ref, do_ref,
             dq_ref, dkc_ref, dkp_ref, dvc_ref, dvp_ref, dkm_ref, dvm_ref, dsink_ref):
        @pl.when(pl.program_id(0) == 0)
        def _():
            dkm_ref[...] = jnp.zeros_like(dkm_ref)
            dvm_ref[...] = jnp.zeros_like(dvm_ref)
            dsink_ref[...] = jnp.zeros_like(dsink_ref)

        visible = _attn_visible4(pl.program_id(0))
        heads = range(N_KV_HEADS)
        lane1 = lax.broadcasted_iota(jnp.int32, (1, LANES), 1)
        kcat, vcat, q4, sink4 = _attn_operands(q_ref, k0, kp, kc, v0, vp, vc, sink_ref)
        do4 = [_stack_heads(do_ref, sink_ref, h, 1.0)[0] for h in heads]
        pn, psink = _attn_probs(q4, kcat, visible, sink4)
        dp = [_dot_nt(do4[h], vcat[h]) for h in heads]
        delta = [jnp.sum(pn[h] * dp[h], axis=-1, keepdims=True) for h in heads]
        ds16 = [(pn[h] * (dp[h] - delta[h])).astype(BF16) for h in heads]
        dq4 = [_dot(ds16[h], kcat[h]) for h in heads]
        dk_acc = [_dot_tn(ds16[h], q4[h]) for h in heads]
        dv_acc = [_dot_tn(pn[h].astype(BF16), do4[h]) for h in heads]
        dsink = jnp.zeros((1, LANES), F32)
        for kvh in heads:
            ksl = slice(kvh * LANES, (kvh + 1) * LANES)
            sink_terms = psink[kvh] * delta[kvh]
            for j in range(4):
                part = jnp.sum(sink_terms[j * CHUNK:(j + 1) * CHUNK], axis=0, keepdims=True)
                dsink = dsink - jnp.where(lane1 == kvh * 4 + j, part, 0.0)
            for pp in range(2):
                qsl = slice((kvh * 2 + pp) * LANES, (kvh * 2 + pp + 1) * LANES)
                dq_ref[:, qsl] = (_unstack_pairs(dq4[kvh], pp) * ATTN_SCALE).astype(BF16)
            dkm_ref[:, ksl] += dk_acc[kvh][0:CHUNK]
            dvm_ref[:, ksl] += dv_acc[kvh][0:CHUNK]
            dkp_ref[:, ksl] = dk_acc[kvh][CHUNK:2 * CHUNK]
            dvp_ref[:, ksl] = dv_acc[kvh][CHUNK:2 * CHUNK]
            dkc_ref[:, ksl] = dk_acc[kvh][2 * CHUNK:3 * CHUNK]
            dvc_ref[:, ksl] = dv_acc[kvh][2 * CHUNK:3 * CHUNK]
        dsink_ref[...] += dsink

    qspec = pl.BlockSpec((CHUNK, D_MODEL), lambda b: (b, 0))
    kvspec = pl.BlockSpec((CHUNK, 2 * D_KV), lambda b: (b, 0))
    fixed = pl.BlockSpec((CHUNK, 2 * D_KV), lambda b: (0, 0))
    kv_shape = jax.ShapeDtypeStruct((rows, 2 * D_KV), F32)
    meta_shape = jax.ShapeDtypeStruct((CHUNK, 2 * D_KV), F32)
    return _call(
        body, name=name,
        out_shape=(jax.ShapeDtypeStruct((rows, D_MODEL), BF16), kv_shape, kv_shape, kv_shape, kv_shape,
                   meta_shape, meta_shape, jax.ShapeDtypeStruct((1, LANES), F32)),
        grid=(rows // CHUNK,),
        in_specs=[qspec] + _attn_specs(0) + _attn_specs(1) + [pl.BlockSpec(memory_space=pltpu.SMEM), qspec],
        out_specs=(qspec, kvspec, kvspec, kvspec, kvspec, fixed, fixed, pl.BlockSpec((1, LANES), lambda b: (0, 0))),
        operands=[q, kv2, kv2, kv2, kv2, kv2, kv2, sinks, do], semantics=("arbitrary",), steps=steps)


def _kv_grad_combine(name, dk_cur, dk_prev, dk_meta, dv_cur, dv_prev, dv_meta):
    rows = dk_cur.shape[0]
    nb = rows // CHUNK
    width = 2 * D_KV

    def body(kc_ref, kp_ref, km_ref, vc_ref, vp_ref, vm_ref, o_ref):
        jj = pl.program_id(0) + jnp.zeros((CHUNK, 1), jnp.int32)
        for half, (c_ref, p_ref, m_ref) in enumerate(((kc_ref, kp_ref, km_ref), (vc_ref, vp_ref, vm_ref))):
            total = c_ref[...] + jnp.where(jj < nb - 1, p_ref[...], 0.0) + jnp.where(jj == 0, m_ref[...], 0.0)
            o_ref[:, half * width:(half + 1) * width] = total.astype(BF16)

    blk = lambda f: pl.BlockSpec((CHUNK, width), f)
    three = lambda: [blk(lambda j: (j, 0)), blk(lambda j: (jnp.minimum(j + 1, nb - 1), 0)), blk(lambda j: (0, 0))]
    return pl.pallas_call(
        body, name=name, out_shape=jax.ShapeDtypeStruct((rows, 2 * width), BF16), grid=(nb,),
        in_specs=three() + three(), out_specs=pl.BlockSpec((CHUNK, 2 * width), lambda j: (j, 0)),
        compiler_params=_cparams(("parallel",)),
    )(dk_cur, dk_prev, dk_meta, dv_cur, dv_prev, dv_meta)


def _adamw(name, w, g, m, v, steps=()):
    rows, width = w.shape
    tr = rows
    for cand in range(8, rows + 1, 8):
        if rows % cand == 0 and cand * width * 4 <= (1 << 20):
            tr = cand

    def body(*refs):
        _adamw_update(*refs)

    blk = pl.BlockSpec((tr, width), lambda i: (i, 0))
    shp = jax.ShapeDtypeStruct((rows, width), F32)
    return _call(body, name=name, out_shape=(shp, shp, shp), grid=(rows // tr,), in_specs=[blk] * 4,
                 out_specs=(blk,) * 3, operands=[w, g, m, v], semantics=("parallel",), steps=steps)


def _adamw_update(w_ref, g_ref, m_ref, v_ref, d_ref, mo_ref, vo_ref):
    gv = g_ref[...]
    mn = ADAM_B1 * m_ref[...] + (1.0 - ADAM_B1) * gv
    vn = ADAM_B2 * v_ref[...] + (1.0 - ADAM_B2) * (gv * gv)
    m_hat = mn / (1.0 - ADAM_B1 ** ADAM_STEP)
    v_hat = vn / (1.0 - ADAM_B2 ** ADAM_STEP)
    d_ref[...] = -ADAM_LR * (m_hat / (jnp.sqrt(v_hat) + ADAM_EPS) + ADAM_WD * w_ref[...])
    mo_ref[...] = mn
    vo_ref[...] = vn


def _adamw_small(name, ws, gs, ms, vs):
    n = len(ws)

    def body(*refs):
        for i in range(n):
            _adamw_update(*refs[i::n])

    shapes = [jax.ShapeDtypeStruct(a.shape, F32) for a in ws]
    outs = pl.pallas_call(body, name=name, out_shape=shapes * 3, in_specs=[VMEM_SPEC] * (4 * n),
                          out_specs=[VMEM_SPEC] * (3 * n), compiler_params=_cparams())(*ws, *gs, *ms, *vs)
    return outs[:n], outs[n:2 * n], outs[2 * n:]


def _ffn_fwd(tag, h, hn, p, i, plan):
    up_g, up_v, act = _ffn_up_conv(f"ffn{tag}_up", hn, plan.weight("f_w_up", i), p["f_conv_w"][i],
                                   p["f_conv_b"][i:i + 1], steps=plan.steps(f"ffn{tag}_up"))
    pre = _mm(f"ffn{tag}_down", act, plan.weight("f_w_down", i), "nn", steps=plan.steps(f"ffn{tag}_down"))
    return pre, (h, hn, up_g, up_v, act, pre)


def _ffn_bwd(tag, dpre, saved, p, i, plan):
    h, hn, up_g, up_v, act, pre = saved
    plan.grad("f_w_down", i, _mm(f"ffn{tag}_down_dw", act, dpre, "tn", out_dtype=BF16))
    dact = _mm(f"ffn{tag}_down_dx", dpre, plan.weight("f_w_down", i), "nt", steps=plan.steps(f"ffn{tag}_down_dx"))
    gwg, gwv, gbg, gbv, dhn, g_up = _ffn_conv_bwd(
        f"ffn{tag}_conv_bwd", up_g, up_v, dact, p["f_conv_w"][i], p["f_conv_b"][i:i + 1], hn,
        plan.weight("f_w_up", i), steps=plan.steps(f"ffn{tag}_conv_bwd"))
    g_cw, g_cb = jnp.concatenate([gwg, gwv], axis=1), jnp.concatenate([gbg, gbv], axis=1)
    plan.grad("f_w_up", i, g_up)
    return dhn, dict(f_conv_w=g_cw, f_conv_b=g_cb)


def _lanes_pad(a, width=LANES):
    return jnp.pad(a, [(0, 0)] * (a.ndim - 1) + [(0, width - a.shape[-1])])


def _dup_heads(w):
    rows = w.shape[0]
    w = w.reshape(rows, 2 * N_KV_HEADS, 1, HEAD_DIM)
    return jnp.broadcast_to(w, (rows, 2 * N_KV_HEADS, 2, HEAD_DIM)).reshape(rows, 4 * D_KV)


def _undup_heads(g):
    rows = g.shape[0]
    return g.reshape(rows, 2 * N_KV_HEADS, 2, HEAD_DIM).sum(axis=2).reshape(rows, 2 * D_KV)


def _local_step(x2, target, p, plan):
    seq = x2.shape[0]
    rows = seq + CHUNK
    g = {}

    h0 = jnp.concatenate([jnp.zeros((PAD_ROWS, D_MODEL), F32), p["meta_tokens"], x2], axis=0)

    w_in = plan.weight("a_w_in")
    w_dt = jnp.pad(w_in[D_MAIN:], ((0, LANES - SSM_HEADS), (0, 0)))
    dt_bias = _lanes_pad(p["a_dt_bias"])
    a128 = _lanes_pad(-jnp.exp(p["a_a_log"]))
    dskexp = jnp.repeat(p["a_d_skip"].reshape(SSM_HEADS), HEAD_DIM).reshape(1, D_INNER)

    hn0 = _rms_fwd("a_norm", h0, p["a_norm_pre"])
    zx = _mm("a_in_main", hn0, w_in, "nt", k_rows=D_MAIN, steps=plan.steps("a_in_main"))
    dtr = _mm("a_in_dt", hn0, w_dt, "nt")
    xbc = _conv4_fwd("a_conv", zx, p["a_conv_w"], p["a_conv_b"], steps=plan.steps("a_conv"))
    dt = _dt_fwd("a_dt", dtr, dt_bias)
    dt_exp, acs_exp, acs_rows = _ssd_prep("a_ssd_prep", dt, a128, steps=plan.steps("a_ssd_prep"))
    y, states = _ssd_fwd("a_ssd", xbc, dt_exp, acs_exp, acs_rows, dskexp, steps=plan.steps("a_ssd"))
    yn = _gate_fwd("a_gate", y, zx, p["a_gate_norm"], steps=plan.steps("a_gate"))
    mix = _mm("a_out", yn, plan.weight("a_w_out"), "nn", steps=plan.steps("a_out"))
    h1, (hn_f0,) = _resid_norm_fwd("a_resid", h0, mix, p["a_norm_post"], [p["f_norm_pre"][0:1]])

    pre_f0, ffn0 = _ffn_fwd("0", h1, hn_f0, p, 0, plan)
    h2, (hkv, hn2) = _resid_norm_fwd("ffn0_resid", h1, pre_f0, p["f_norm_post"][0:1], [p["kv_norm"], p["b_norm_pre"]])

    w_kv2 = _dup_heads(plan.weight("w_kv"))
    kv2 = _mm("kv_proj", hkv, w_kv2, "nn")
    q = _mm("b_q", hn2, plan.weight("b_w_q"), "nn")
    sinks = p["b_sinks"].reshape(N_Q_HEADS)
    o = _attn_fwd("b_attn", q, kv2, sinks, steps=plan.steps("b_attn"))
    attn = _mm("b_o", o, plan.weight("b_w_o"), "nn", steps=plan.steps("b_o"))
    h3, (hn_f1,) = _resid_norm_fwd("b_resid", h2, attn, p["b_norm_post"], [p["f_norm_pre"][1:2]])

    pre_f1, ffn1 = _ffn_fwd("1", h3, hn_f1, p, 1, plan)
    dh, loss_vec, dpre_f1, g_post1 = _resid_norm_loss("ffn1_resid_loss", h3, pre_f1, p["f_norm_post"][1:2], target)
    loss = loss_vec[0, 0]

    dhn_f1, g1 = _ffn_bwd("1", dpre_f1, ffn1, p, 1, plan)
    dh, g_pre1, dpre, g["b_norm_post"] = _norm_bwd_add("ffn1_norm_bwd", dh, dhn_f1, h3, p["f_norm_pre"][1:2],
                                                        then=(attn, p["b_norm_post"]))
    plan.grad("b_w_o", None, _mm("b_o_dw", o, dpre, "tn", out_dtype=BF16))
    do = _mm("b_o_dx", dpre, plan.weight("b_w_o"), "nt", steps=plan.steps("b_o_dx"))
    dq, dkc, dkp, dvc, dvp, dkm, dvm, dsink = _attn_bwd("b_attn_bwd", q, kv2, sinks, do, steps=plan.steps("b_attn_bwd"))
    g["b_sinks"] = dsink[:, :N_Q_HEADS]
    dhn2 = _mm("b_q_dx", dq, plan.weight("b_w_q"), "nt")
    plan.grad("b_w_q", None, _mm("b_q_dw", hn2, dq, "tn", out_dtype=BF16))
    dh, g["b_norm_pre"] = _norm_bwd_add("b_norm_bwd", dh, dhn2, h2, p["b_norm_pre"])
    dkv2 = _kv_grad_combine("kv_grad", dkc, dkp, dkm, dvc, dvp, dvm)
    dhkv = _mm("kv_proj_dx", dkv2, w_kv2, "nt")
    plan.grad("w_kv", None, _undup_heads(_mm("kv_proj_dw", hkv, dkv2, "tn")))
    dh, g["kv_norm"], dpre_f0, g_post0 = _norm_bwd_add("kv_norm_bwd", dh, dhkv, h2, p["kv_norm"],
                                                       then=(pre_f0, p["f_norm_post"][0:1]))

    dhn_f0, g0 = _ffn_bwd("0", dpre_f0, ffn0, p, 0, plan)
    dh, g_pre0, dpre, g["a_norm_post"] = _norm_bwd_add("ffn0_norm_bwd", dh, dhn_f0, h1, p["f_norm_pre"][0:1],
                                                        then=(mix, p["a_norm_post"]))
    g["f_norm_post"] = jnp.concatenate([g_post0, g_post1], axis=0)
    g["f_norm_pre"] = jnp.concatenate([g_pre0, g_pre1], axis=0)
    g["f_conv_w"] = jnp.stack([g0["f_conv_w"], g1["f_conv_w"]])
    g["f_conv_b"] = jnp.concatenate([g0["f_conv_b"], g1["f_conv_b"]], axis=0)
    plan.grad("a_w_out", None, _mm("a_out_dw", yn, dpre, "tn", out_dtype=BF16))
    dyn = _mm("a_out_dx", dpre, plan.weight("a_w_out"), "nt", steps=plan.steps("a_out_dx"))
    dy, dzx, g["a_gate_norm"] = _gate_bwd("a_gate_bwd", dyn, y, zx, p["a_gate_norm"])
    dxbc, ddt, dalog, ddsk = _ssd_bwd("a_ssd_bwd", xbc, dt_exp, acs_exp, acs_rows, dt, a128, dskexp, dy, states,
                                      steps=plan.steps("a_ssd_bwd"))
    g["a_a_log"] = dalog[:, :SSM_HEADS]
    g["a_d_skip"] = ddsk[:, :SSM_HEADS]
    ddtr, dbias = _dt_bwd("a_dt_bwd", ddt, dtr, dt_bias)
    g["a_dt_bias"] = dbias[:, :SSM_HEADS]
    dzx, g["a_conv_w"], g["a_conv_b"] = _conv4_bwd("a_conv_bwd", zx, dxbc, p["a_conv_w"], p["a_conv_b"], dzx)
    g_in = _mm("a_in_main_dw", dzx, hn0, "tn", out_dtype=BF16, out_rows=D_IN_PROJ, steps=plan.steps("a_in_main_dw"))
    plan.grad("a_w_in", None, _tn_rows_into("a_in_dt_dw", ddtr, hn0, g_in, D_MAIN, SSM_HEADS))
    dhn0 = _mm("a_in_dt_dx", ddtr, w_dt, "nn", steps=plan.steps("a_in_dt_dx"))
    dhn0 = _mm("a_in_main_dx", dzx, w_in, "nn", acc=dhn0, steps=plan.steps("a_in_main_dx"))
    dh_first, grad_x, g["a_norm_pre"] = _norm_bwd_add("a_norm_bwd", dh, dhn0, h0, p["a_norm_pre"],
                                                      split_first_block=True, steps=plan.steps("a_norm_bwd"))
    g["meta_tokens"] = dh_first[PAD_ROWS:]
    return loss, grad_x, g


ANY = pl.BlockSpec(memory_space=pl.ANY)
VMEM_SPEC = pl.BlockSpec(memory_space=pltpu.VMEM)


def _allgather_small(name, shard):
    rows = shard.shape[0]

    def body(s_ref, o_ref, send_sems, recv_sems):
        x, y, c = _place()
        me = 2 * x + y
        o_ref[me] = s_ref[...]
        chips = _other_chips(x, y)
        sends = [pltpu.make_async_remote_copy(s_ref, o_ref.at[me], send_sems.at[j], recv_sems.at[j],
                                              device_id=(cx, cy, c), device_id_type=MESH)
                 for j, (cx, cy) in enumerate(chips)]
        for cp in sends:
            cp.start()
        for j, (cx, cy) in enumerate(chips):
            pltpu.make_async_remote_copy(s_ref, o_ref.at[2 * cx + cy], send_sems.at[j], recv_sems.at[j],
                                         device_id=(cx, cy, c), device_id_type=MESH).wait_recv()
        for cp in sends:
            cp.wait_send()

    return pl.pallas_call(
        body, name=name, out_shape=jax.ShapeDtypeStruct((N_CHIPS, rows, LANES), F32),
        in_specs=[VMEM_SPEC], out_specs=VMEM_SPEC,
        scratch_shapes=[pltpu.SemaphoreType.DMA((3,)), pltpu.SemaphoreType.DMA((3,))],
        compiler_params=pltpu.CompilerParams(vmem_limit_bytes=VMEM_LIMIT),
    )(shard)


def _row_block(rows, width, itemsize, align, budget=2 << 20):
    best = rows
    for cand in range(align, rows + 1, align):
        if rows % cand == 0 and cand * width * itemsize <= budget:
            best = cand
    return best


def _cast_into_slot(name, chip, w, layer=None):
    rows, width = w.shape[-2:]
    tr = _row_block(rows, width, 4, 16)
    if layer is None:
        in_spec = pl.BlockSpec((tr, width), lambda i, chip_ref: (i, 0))
    else:
        in_spec = pl.BlockSpec((None, tr, width), lambda i, chip_ref: (layer, i, 0))

    def body(chip_ref, w_ref, o_ref):
        o_ref[...] = w_ref[...].astype(BF16)

    return pl.pallas_call(
        body, name=name, out_shape=jax.ShapeDtypeStruct((N_CHIPS, rows, width), BF16),
        grid_spec=pltpu.PrefetchScalarGridSpec(
            num_scalar_prefetch=1, grid=(rows // tr,), in_specs=[in_spec],
            out_specs=pl.BlockSpec((None, tr, width), lambda i, chip_ref: (chip_ref[0], i, 0))),
        compiler_params=_cparams(("parallel",)),
    )(chip, w)


def _allreduce_small(name, vec):
    rows = -(-vec.shape[0] // (2 * SUBLANES)) * (2 * SUBLANES)
    hr = rows // 2
    padded = jnp.pad(vec, ((0, rows - vec.shape[0]), (0, 0)))

    def body(v_ref, o_ref, theirs, pair, by_chip, send_sems, recv_sems):
        x, y, c = _place()
        me = 2 * x + y
        sibling = (x, y, 1 - c)
        mine = pl.ds(pl.multiple_of(c * hr, SUBLANES), hr)
        other = pl.ds(pl.multiple_of((1 - c) * hr, SUBLANES), hr)

        swap = _remote(v_ref, theirs, send_sems, recv_sems, 0, sibling)
        swap.start()
        swap.wait()
        south = (c + jnp.zeros((1, 1), jnp.int32)) == 0
        pair[...] = jnp.where(south, v_ref[...], theirs[...]) + jnp.where(south, theirs[...], v_ref[...])

        by_chip[me] = pair[mine, :]
        sends = [_remote(by_chip.at[me], by_chip.at[me], send_sems, recv_sems, 1 + j, (cx, cy, c))
                 for j, (cx, cy) in enumerate(_other_chips(x, y))]
        for cp in sends:
            cp.start()
        for j, (cx, cy) in enumerate(_other_chips(x, y)):
            _remote(by_chip.at[me], by_chip.at[2 * cx + cy], send_sems, recv_sems, 1 + j, (cx, cy, c)).wait_recv()
        for cp in sends:
            cp.wait_send()
        total = by_chip[0]
        for s in range(1, N_CHIPS):
            total = total + by_chip[s]

        o_ref[mine, :] = total
        back = _remote(o_ref.at[mine], o_ref.at[mine], send_sems, recv_sems, 4, sibling)
        back.start()
        _remote(o_ref.at[other], o_ref.at[other], send_sems, recv_sems, 4, sibling).wait_recv()
        back.wait_send()

    out = pl.pallas_call(
        body, name=name, out_shape=jax.ShapeDtypeStruct((rows, LANES), F32),
        in_specs=[VMEM_SPEC], out_specs=VMEM_SPEC,
        scratch_shapes=[pltpu.VMEM((rows, LANES), F32), pltpu.VMEM((rows, LANES), F32),
                        pltpu.VMEM((N_CHIPS, hr, LANES), F32), pltpu.SemaphoreType.DMA((5,)),
                        pltpu.SemaphoreType.DMA((5,))],
        compiler_params=pltpu.CompilerParams(vmem_limit_bytes=VMEM_LIMIT),
    )(padded)
    return out[:vec.shape[0]]


def _rs_pair_add(name, place, grads, partner, split="rows"):
    _, half_rows, width = partner.shape
    tr = _row_block(half_rows, width, 2, 16)
    nb = half_rows // tr
    if split == "rows":
        mine = pl.BlockSpec((None, tr, width), lambda s, i, pr: (s, pr[1] * nb + i, 0))
    else:
        mine = pl.BlockSpec((None, tr, width), lambda s, i, pr: (s, i, pr[1]))

    def body(place_ref, g_ref, p_ref, o_ref):
        o_ref[...] = (g_ref[...].astype(F32) + p_ref[...].astype(F32)).astype(BF16)

    return pl.pallas_call(
        body, name=name, out_shape=jax.ShapeDtypeStruct(partner.shape, BF16),
        grid_spec=pltpu.PrefetchScalarGridSpec(
            num_scalar_prefetch=1, grid=(N_CHIPS, nb),
            in_specs=[mine, pl.BlockSpec((None, tr, width), lambda s, i, pr: (s, i, 0))],
            out_specs=pl.BlockSpec((None, tr, width), lambda s, i, pr: (s, i, 0))),
        compiler_params=_cparams(("parallel", "parallel")),
    )(place, grads, partner)


def _rs_chip_add(name, place, mine, others, split="rows"):
    _, half_rows, width = mine.shape
    tr = _row_block(half_rows, width, 4, 16, budget=1 << 20)
    nb = half_rows // tr
    if split == "rows":
        out_shape, out_spec = (2 * half_rows, width), pl.BlockSpec((tr, width), lambda i, pr: (pr[1] * nb + i, 0))
    else:
        out_shape, out_spec = (half_rows, 2 * width), pl.BlockSpec((tr, width), lambda i, pr: (i, pr[1]))

    def body(place_ref, q_ref, r_ref, o_ref):
        acc = q_ref[...].astype(F32)
        for j in range(3):
            acc = acc + r_ref[j].astype(F32)
        o_ref[...] = acc

    return pl.pallas_call(
        body, name=name, out_shape=jax.ShapeDtypeStruct(out_shape, F32),
        grid_spec=pltpu.PrefetchScalarGridSpec(
            num_scalar_prefetch=1, grid=(nb,),
            in_specs=[pl.BlockSpec((None, tr, width), lambda i, pr: (pr[0], i, 0)),
                      pl.BlockSpec((3, tr, width), lambda i, pr: (0, i, 0))],
            out_specs=out_spec),
        compiler_params=_cparams(("parallel",)),
    )(place, mine, others)


WEIGHTS = ["meta_tokens", "a_norm_pre", "a_w_in", "a_conv_w", "a_conv_b", "a_dt_bias", "a_a_log", "a_d_skip",
           "a_gate_norm", "a_w_out", "a_norm_post", "kv_norm", "w_kv", "b_norm_pre", "b_w_q", "b_sinks", "b_w_o",
           "b_norm_post", "f_norm_pre", "f_w_up", "f_conv_w", "f_conv_b", "f_w_down", "f_norm_post"]
FULL_SHAPE = {
    "meta_tokens": (16, 1024), "a_norm_pre": (1, 1024), "a_w_in": (1, 1024, 5152), "a_conv_w": (1, 4, 3072),
    "a_conv_b": (1, 3072), "a_dt_bias": (1, 32), "a_a_log": (1, 32), "a_d_skip": (1, 32), "a_gate_norm": (1, 2048),
    "a_w_out": (1, 2048, 1024), "a_norm_post": (1, 1024), "kv_norm": (1024,), "w_kv": (1024, 512),
    "b_norm_pre": (1, 1024), "b_w_q": (1, 1024, 1024), "b_sinks": (1, 16), "b_w_o": (1, 1024, 1024),
    "b_norm_post": (1, 1024), "f_norm_pre": (2, 1024), "f_w_up": (2, 1024, 5632), "f_conv_w": (2, 3, 5632),
    "f_conv_b": (2, 5632), "f_w_down": (2, 2816, 1024), "f_norm_post": (2, 1024),
}
SHARD_AXIS = {
    "meta_tokens": 1, "a_norm_pre": 1, "a_w_in": 2, "a_conv_w": 2, "a_conv_b": 1, "a_dt_bias": None, "a_a_log": None,
    "a_d_skip": None, "a_gate_norm": 1, "a_w_out": 1, "a_norm_post": 1, "kv_norm": None, "w_kv": 0, "b_norm_pre": None,
    "b_w_q": 1, "b_sinks": None, "b_w_o": 1, "b_norm_post": None, "f_norm_pre": None, "f_w_up": 2, "f_conv_w": 2,
    "f_conv_b": None, "f_w_down": 1, "f_norm_post": None,
}
BIG = ["a_w_in", "a_w_out", "w_kv", "b_w_q", "b_w_o", "f_w_up", "f_w_down"]
SMALL = [n for n in WEIGHTS if n not in BIG]
SMALL_SHARDED = [n for n in SMALL if SHARD_AXIS[n] is not None]


def _shard_shape(name):
    shape = list(FULL_SHAPE[name])
    if SHARD_AXIS[name] is not None:
        shape[SHARD_AXIS[name]] //= N_CHIPS
    return tuple(shape)


def _numel(shape):
    return int(math.prod(shape))


SUBLANES = 8


def _packed_rows(shape):
    rows = -(-_numel(shape) // LANES)
    return -(-rows // SUBLANES) * SUBLANES


def _pack(arrays):
    parts = []
    for a in arrays:
        size, rows = _numel(a.shape), _packed_rows(a.shape)
        if size % LANES == 0:
            part = jnp.pad(a.reshape(size // LANES, LANES), ((0, rows - size // LANES), (0, 0)))
        else:
            part = jnp.pad(a.reshape(-1), (0, rows * LANES - size)).reshape(rows, LANES)
        parts.append(part)
    return jnp.concatenate(parts, axis=0)


def _unpack(packed, names, shape_of):
    out, off = {}, 0
    lead = packed.shape[:-2]
    for n in names:
        shape = tuple(shape_of(n))
        size, rows = _numel(shape), _packed_rows(shape)
        part = packed[..., off:off + rows, :]
        if size % LANES == 0:
            out[n] = part[..., :size // LANES, :].reshape(lead + shape)
        else:
            out[n] = part.reshape(lead + (rows * LANES,))[..., :size].reshape(lead + shape)
        off += rows
    return out


def _split_chips(name, full):
    ax = SHARD_AXIS[name]
    shape = full.shape
    cut = shape[:ax] + (N_CHIPS, shape[ax] // N_CHIPS) + shape[ax + 1:]
    return jnp.moveaxis(full.reshape(cut), ax, 0)


def _join_chips(name, stacked):
    ax = SHARD_AXIS[name]
    moved = jnp.moveaxis(stacked, 0, ax)
    shape = moved.shape
    return moved.reshape(shape[:ax] + (shape[ax] * shape[ax + 1],) + shape[ax + 2:])


def _as2d(a):
    return a.reshape(-1, a.shape[-1])


BUFFERS = [("a_w_in", "a_w_in", None), ("a_w_out", "a_w_out", None), ("w_kv", "w_kv", None),
           ("b_w_q", "b_w_q", None), ("b_w_o", "b_w_o", None), ("f_w_up0", "f_w_up", 0), ("f_w_up1", "f_w_up", 1),
           ("f_w_down0", "f_w_down", 0), ("f_w_down1", "f_w_down", 1)]


TRANSPOSED = ("a_w_in",)
SPLIT = {"a_w_in": "cols"}


def _local_shard(arrays, weight, layer):
    if weight in TRANSPOSED:
        return arrays[weight][0].T
    return _as2d(arrays[weight]) if layer is None else arrays[weight]


def _weight_from_gathered(weight, buf):
    if weight == "f_w_up":
        return buf
    return buf.reshape(N_CHIPS * buf.shape[1], buf.shape[2])


def _gathered_from_grad(weight, g):
    if weight == "f_w_up":
        return g
    return g.reshape(N_CHIPS, g.shape[0] // N_CHIPS, g.shape[1]).astype(BF16)


GATHER_SCHEDULE = {
    "a_in_main": [("ici", ["a_w_out"])],
    "a_conv": [("d2d", ["a_w_out"]), ("ici", ["f_w_down0"])],
    "a_ssd_prep": [("d2d", ["f_w_down0"]), ("ici_near", ["f_w_up0"])],
    "a_ssd": [("ici_far", ["f_w_up0"])],
    "a_gate": [("d2d", ["f_w_up0"]), ("ici", ["w_kv", "b_w_q", "b_w_o"])],
    "ffn0_up": [("d2d", ["w_kv", "b_w_q", "b_w_o"]), ("ici", ["f_w_down1"])],
    "ffn0_down": [("d2d", ["f_w_down1"])],
    "b_attn": [("ici", ["f_w_up1"])],
    "b_o": [("d2d", ["f_w_up1"])],
}
REDUCE_SCHEDULE = {
    "b_attn_bwd": [("all", ["f_w_down1", "f_w_up1", "b_w_o"])],
    "ffn0_conv_bwd": [("all", ["b_w_q", "w_kv", "f_w_down0"])],
    "a_ssd_bwd": [("all", ["f_w_up0", "a_w_out"])],
    "a_in_main_dx": [("near", ["a_w_in"])],
    "a_norm_bwd": [("far", ["a_w_in"])],
}
REDUCE_LAST = ("a_w_in",)
ICI_PEERS = {"ici": ALL_PEERS, "ici_near": NEAR_PEERS, "ici_far": FAR_PEERS,
             "all": ALL_PEERS, "near": NEAR_PEERS, "far": FAR_PEERS}
PAIR_SCHEDULE = {
    "b_o_dx": ["f_w_down1", "f_w_up1", "b_w_o"],
    "ffn0_down_dx": ["b_w_q", "w_kv", "f_w_down0"],
    "a_out_dx": ["f_w_up0", "a_w_out"],
    "a_in_dt_dx": ["a_w_in"],
}
SWAP_SCHEDULE = {"a_in_main_dw": ["f_w_down1", "f_w_up1", "b_w_o", "b_w_q", "w_kv", "f_w_down0", "f_w_up0", "a_w_out"]}


def _buffer_of(weight, layer):
    return weight if layer is None else f"{weight}{layer}"


class _Pipeline:
    def __init__(self, place, slots):
        self.place = place
        self.slots = dict(slots)
        self.running = []
        self.grads = {}
        self.theirs = {}
        self.partials = {}
        self.peers = {}
        self.reduced = {}

    def _collect(self):
        for step, buffers, table in self.running:
            table.update(zip(buffers, step.results))
        self.running = []

    @staticmethod
    def _splits(buffers):
        return [SPLIT.get(b, "rows") for b in buffers]

    def gather_now(self, name, buffers):
        step = _step_gather_full([self.slots[b] for b in buffers], self._splits(buffers))
        _run_steps(name, [step])
        self.slots.update(zip(buffers, step.results))

    def weight(self, name, layer=None):
        self._collect()
        return _weight_from_gathered(name, self.slots[_buffer_of(name, layer)])

    def grad(self, name, layer, g):
        self.grads[_buffer_of(name, layer)] = _gathered_from_grad(name, g)

    def steps(self, kernel):
        self._collect()
        steps = []
        for phase, buffers in GATHER_SCHEDULE.get(kernel, []):
            bufs, splits = [self.slots[b] for b in buffers], self._splits(buffers)
            step = (_step_gather_d2d(bufs, splits) if phase == "d2d"
                    else _step_gather_ici(bufs, splits, ICI_PEERS[phase]))
            self.running.append((step, buffers, self.slots))
            steps.append(step)
        buffers = PAIR_SCHEDULE.get(kernel)
        if buffers:
            step = _step_pair_exchange([self.grads[b] for b in buffers], self._splits(buffers))
            self.running.append((step, buffers, self.theirs))
            steps.append(step)
        for part, buffers in REDUCE_SCHEDULE.get(kernel, []):
            for b in buffers:
                if b not in self.partials:
                    self.partials[b] = _rs_pair_add("reduce_pair_add_" + b, self.place, self.grads[b], self.theirs[b],
                                                    SPLIT.get(b, "rows"))
            started = [self.peers[b] for b in buffers] if all(b in self.peers for b in buffers) else None
            step = _step_chip_exchange([self.partials[b] for b in buffers], ICI_PEERS[part], into=started)
            self.running.append((step, buffers, self.peers))
            steps.append(step)
        buffers = SWAP_SCHEDULE.get(kernel)
        if buffers:
            step = self._swap_step(buffers)
            self.running.append((step, buffers, self.reduced))
            steps.append(step)
        return steps

    def _swap_step(self, buffers):
        halves = [_rs_chip_add("reduce_chip_add_" + b, self.place, self.partials[b], self.peers[b], SPLIT.get(b, "rows"))
                  for b in buffers]
        return _step_pair_gather(halves, self._splits(buffers))

    def shard(self, buffer):
        self._collect()
        return self.reduced[buffer]

    def finish(self):
        self._collect()
        rest = [b for b, _, _ in BUFFERS if b not in self.reduced]
        step = self._swap_step(rest)
        _run_steps("reduce_pair_gather", [step])
        self.reduced.update(zip(rest, step.results))


def kernel(x, meta_tokens, a_norm_pre, a_w_in, a_conv_w, a_conv_b, a_dt_bias, a_a_log, a_d_skip, a_gate_norm, a_w_out, a_norm_post, kv_norm, w_kv, b_norm_pre, b_w_q, b_sinks, b_w_o, b_norm_post, f_norm_pre, f_w_up, f_conv_w, f_conv_b, f_w_down, f_norm_post, loss_target, m_meta_tokens, m_a_norm_pre, m_a_w_in, m_a_conv_w, m_a_conv_b, m_a_dt_bias, m_a_a_log, m_a_d_skip, m_a_gate_norm, m_a_w_out, m_a_norm_post, m_kv_norm, m_w_kv, m_b_norm_pre, m_b_w_q, m_b_sinks, m_b_w_o, m_b_norm_post, m_f_norm_pre, m_f_w_up, m_f_conv_w, m_f_conv_b, m_f_w_down, m_f_norm_post, v_meta_tokens, v_a_norm_pre, v_a_w_in, v_a_conv_w, v_a_conv_b, v_a_dt_bias, v_a_a_log, v_a_d_skip, v_a_gate_norm, v_a_w_out, v_a_norm_post, v_kv_norm, v_w_kv, v_b_norm_pre, v_b_w_q, v_b_sinks, v_b_w_o, v_b_norm_post, v_f_norm_pre, v_f_w_up, v_f_conv_w, v_f_conv_b, v_f_w_down, v_f_norm_post):
    given = dict(locals())
    w = {n: given[n] for n in WEIGHTS}
    mom = {n: given["m_" + n] for n in WEIGHTS}
    var = {n: given["v_" + n] for n in WEIGHTS}
    chip = 2 * lax.axis_index("x") + lax.axis_index("y")
    core = lax.axis_index("c")
    place = jnp.stack([chip, core]).astype(jnp.int32)

    small_all = _allgather_small("gather_small", _pack([w[n] for n in SMALL_SHARDED]))
    small_parts = _unpack(small_all, SMALL_SHARDED, _shard_shape)
    slots = {b: _cast_into_slot("cast_" + b, place, _local_shard(w, wn, layer), layer) for b, wn, layer in BUFFERS}
    pipeline = _Pipeline(place, slots)
    pipeline.gather_now("gather_first", ["a_w_in"])
    p = {}
    for n in SMALL:
        p[n] = _join_chips(n, small_parts[n]) if n in SMALL_SHARDED else w[n]
    p["a_conv_w"] = p["a_conv_w"][0]
    p["kv_norm"] = p["kv_norm"].reshape(1, D_MODEL)

    loss_local, grad_x, g = _local_step(x[0], loss_target[0], p, pipeline)

    small_sum = _allreduce_small("reduce_small", _pack([g[n].reshape(FULL_SHAPE[n]) for n in SMALL]
                                                       + [loss_local.reshape(1, 1)]))
    small_red = _unpack(small_sum, SMALL + ["loss"], lambda n: (1, 1) if n == "loss" else FULL_SHAPE[n])
    loss = small_red["loss"][0, 0]
    grads = {}
    for n in SMALL:
        if SHARD_AXIS[n] is None:
            grads[n] = small_red[n]
        else:
            grads[n] = lax.dynamic_index_in_dim(_split_chips(n, small_red[n]), chip, 0, keepdims=False)

    delta, new_m, new_v = {}, {}, {}
    for n in sorted(BIG, key=lambda name: name in REDUCE_LAST):
        shape = _shard_shape(n)
        if n in REDUCE_LAST:
            pipeline.finish()
        if n in TRANSPOSED:
            g2d = pipeline.shard(n)
            w2d, m2d, v2d = (arrays[n][0].T for arrays in (w, mom, var))
            back = lambda a: a.T.reshape(shape)
        else:
            g2d = (jnp.concatenate([pipeline.shard(n + "0"), pipeline.shard(n + "1")], axis=0)
                   if n in ("f_w_up", "f_w_down") else pipeline.shard(n))
            w2d, m2d, v2d = (_as2d(arrays[n]) for arrays in (w, mom, var))
            back = lambda a: a.reshape(shape)
        d, m2, v2 = _adamw("adamw_" + n, w2d, g2d, m2d, v2d, steps=pipeline.steps("adamw_" + n))
        grads[n], delta[n], new_m[n], new_v[n] = back(g2d), back(d), back(m2), back(v2)
    at_least_2d = lambda n: (1,) * (2 - len(_shard_shape(n))) + _shard_shape(n)
    outs = _adamw_small("adamw_small", *[[src[n].reshape(at_least_2d(n)) for n in SMALL] for src in (w, grads, mom, var)])
    for dst, arrays in zip((delta, new_m, new_v), outs):
        dst.update({n: a.reshape(_shard_shape(n)) for n, a in zip(SMALL, arrays)})

    return (loss, grad_x[None], *[grads[n].reshape(_shard_shape(n)) for n in WEIGHTS],
            *[delta[n] for n in WEIGHTS], *[new_m[n] for n in WEIGHTS], *[new_v[n] for n in WEIGHTS])
```

```python
import functools
import math

import jax
import jax.numpy as jnp
from jax import lax
from jax.experimental import pallas as pl
from jax.experimental.pallas import tpu as pltpu

F32, BF16 = jnp.float32, jnp.bfloat16
MESH = pl.DeviceIdType.MESH

D_MODEL = 1024
N_META = 16
CHUNK = 128
PAD_ROWS = CHUNK - N_META
D_INNER = 2048
D_STATE = 128
N_GROUPS = 4
HEADS_PER_GROUP = 8
SSM_HEADS = 32
HEAD_DIM = 64
D_BC = N_GROUPS * D_STATE
D_XBC = D_INNER + 2 * D_BC
D_MAIN = D_INNER + D_XBC
D_IN_PROJ = D_MAIN + SSM_HEADS
GROUP_W = HEADS_PER_GROUP * HEAD_DIM
SSM_CONV = 4
D_FF = 2816
FFN_CONV = 3
N_Q_HEADS = 16
N_KV_HEADS = 4
D_KV = 256
ATTN_SCALE = 1.0 / math.sqrt(HEAD_DIM)
RMS_EPS = 1e-6
NEG_INF = -1e30
LANES = 128
VMEM_LIMIT = 51 * 1024 * 1024

ADAM_LR, ADAM_B1, ADAM_B2, ADAM_EPS, ADAM_WD, ADAM_STEP = 0.001, 0.9, 0.999, 1e-08, 0.01, 10

N_CHIPS = 4


def _cparams(sem=None):
    return pltpu.CompilerParams(dimension_semantics=sem, vmem_limit_bytes=VMEM_LIMIT)


def _tile(n, cands=(512, 256, 128)):
    for t in cands:
        if n % t == 0:
            return t
    return n


def _row_tile(rows, width):
    for t in (544, 272):
        if rows % t == 0 and t * width * 4 <= (3 << 20):
            return t
    return 128


def _rows_mask(i, tm):
    rows = i * tm + lax.broadcasted_iota(jnp.int32, (tm, 1), 0)
    return rows >= PAD_ROWS


def _dot(a, b):
    return jnp.dot(a, b, preferred_element_type=F32)


def _dot_nt(a, b):
    return lax.dot_general(a, b, (((1,), (1,)), ((), ())), preferred_element_type=F32)


def _dot_tn(a, b):
    return lax.dot_general(a, b, (((0,), (0,)), ((), ())), preferred_element_type=F32)


def _sigmoid(x):
    return 1.0 / (1.0 + jnp.exp(-x))


def _place():
    return lax.axis_index("x"), lax.axis_index("y"), lax.axis_index("c")


def _other_chips(x, y):
    return [(1 - x, y), (x, 1 - y), (1 - x, 1 - y)]


class _Step:
    def __init__(self, ins, outs, aliases, n_sems, start, finish):
        self.ins, self.outs, self.aliases, self.n_sems = list(ins), list(outs), dict(aliases), n_sems
        self.start, self.finish = start, finish
        self.results = None


def _like(a):
    return jax.ShapeDtypeStruct(a.shape, a.dtype)


def _remote(src, dst, send_sems, recv_sems, k, device):
    return pltpu.make_async_remote_copy(src, dst, send_sems.at[k], recv_sems.at[k], device_id=device, device_id_type=MESH)


def _half(ref, split, which, lead=()):
    if split == "rows":
        hr = ref.shape[-2] // 2
        return ref.at[lead + (pl.ds(which * hr, hr),)]
    hc = ref.shape[-1] // 2
    return ref.at[lead + (slice(None), pl.ds(which * hc, hc))]


def _splits(bufs, splits):
    return list(splits) if splits is not None else ["rows"] * len(bufs)


ALL_PEERS = (0, 1, 2)
NEAR_PEERS = (0, 1)
FAR_PEERS = (2,)


def _step_gather_ici(bufs, splits=None, peers=ALL_PEERS):
    splits = _splits(bufs, splits)

    def copies(outs, send_sems, recv_sems, received):
        x, y, c = _place()
        me = 2 * x + y
        for k, o in enumerate(outs):
            for j, (cx, cy) in enumerate(_other_chips(x, y)):
                if j in peers:
                    part = _half(o, splits[k], c, (2 * cx + cy if received else me,))
                    yield _remote(part, part, send_sems, recv_sems, 3 * k + j, (cx, cy, c))

    def start(ins, outs, send_sems, recv_sems):
        for cp in copies(outs, send_sems, recv_sems, False):
            cp.start()

    def finish(ins, outs, send_sems, recv_sems):
        for cp in copies(outs, send_sems, recv_sems, True):
            cp.wait_recv()
        for cp in copies(outs, send_sems, recv_sems, False):
            cp.wait_send()

    return _Step(bufs, [_like(b) for b in bufs], {k: k for k in range(len(bufs))}, 3 * len(bufs), start, finish)


def _step_gather_d2d(bufs, splits=None):
    splits = _splits(bufs, splits)

    def copies(outs, send_sems, recv_sems, received):
        x, y, c = _place()
        for k, o in enumerate(outs):
            for j, (cx, cy) in enumerate(_other_chips(x, y)):
                part = _half(o, splits[k], 1 - c if received else c, (2 * cx + cy,))
                yield _remote(part, part, send_sems, recv_sems, 3 * k + j, (x, y, 1 - c))

    def start(ins, outs, send_sems, recv_sems):
        for cp in copies(outs, send_sems, recv_sems, False):
            cp.start()

    def finish(ins, outs, send_sems, recv_sems):
        for cp in copies(outs, send_sems, recv_sems, True):
            cp.wait_recv()
        for cp in copies(outs, send_sems, recv_sems, False):
            cp.wait_send()

    return _Step(bufs, [_like(b) for b in bufs], {k: k for k in range(len(bufs))}, 3 * len(bufs), start, finish)


def _step_gather_full(bufs, splits=None):
    n = len(bufs)
    splits = _splits(bufs, splits)

    def ici(outs, send_sems, recv_sems, received):
        x, y, c = _place()
        me = 2 * x + y
        for k, o in enumerate(outs):
            for j, (cx, cy) in enumerate(_other_chips(x, y)):
                part = _half(o, splits[k], c, (2 * cx + cy if received else me,))
                yield _remote(part, part, send_sems, recv_sems, 3 * k + j, (cx, cy, c))

    def d2d(outs, send_sems, recv_sems, received):
        x, y, c = _place()
        for k, o in enumerate(outs):
            for j, (cx, cy) in enumerate(_other_chips(x, y)):
                part = _half(o, splits[k], 1 - c if received else c, (2 * cx + cy,))
                yield _remote(part, part, send_sems, recv_sems, 3 * n + 3 * k + j, (x, y, 1 - c))

    def start(ins, outs, send_sems, recv_sems):
        for cp in ici(outs, send_sems, recv_sems, False):
            cp.start()

    def finish(ins, outs, send_sems, recv_sems):
        for arrived, onward in zip(ici(outs, send_sems, recv_sems, True), d2d(outs, send_sems, recv_sems, False)):
            arrived.wait_recv()
            onward.start()
        for cp in d2d(outs, send_sems, recv_sems, True):
            cp.wait_recv()
        for cp in ici(outs, send_sems, recv_sems, False):
            cp.wait_send()
        for cp in d2d(outs, send_sems, recv_sems, False):
            cp.wait_send()

    return _Step(bufs, [_like(b) for b in bufs], {k: k for k in range(n)}, 6 * n, start, finish)


def _half_shape(shape, split):
    return shape[:-2] + ((shape[-2] // 2, shape[-1]) if split == "rows" else (shape[-2], shape[-1] // 2))


def _step_pair_exchange(grads, splits=None):
    splits = _splits(grads, splits)

    def copies(ins, outs, send_sems, recv_sems):
        x, y, c = _place()
        for k, (g, o) in enumerate(zip(ins, outs)):
            yield _remote(_half(g, splits[k], 1 - c, (slice(None),)), o, send_sems, recv_sems, k, (x, y, 1 - c))

    def start(ins, outs, send_sems, recv_sems):
        for cp in copies(ins, outs, send_sems, recv_sems):
            cp.start()

    def finish(ins, outs, send_sems, recv_sems):
        for cp in copies(ins, outs, send_sems, recv_sems):
            cp.wait()

    outs = [jax.ShapeDtypeStruct(_half_shape(g.shape, s), g.dtype) for g, s in zip(grads, splits)]
    return _Step(grads, outs, {}, len(grads), start, finish)


def _step_chip_exchange(partials, peers=ALL_PEERS, into=None):
    n = len(partials)

    def copies(ins, outs, send_sems, recv_sems):
        x, y, c = _place()
        for k, (q, o) in enumerate(zip(ins[:n], outs)):
            for j, (cx, cy) in enumerate(_other_chips(x, y)):
                if j in peers:
                    yield _remote(q.at[2 * cx + cy], o.at[j], send_sems, recv_sems, 3 * k + j, (cx, cy, c))

    def start(ins, outs, send_sems, recv_sems):
        for cp in copies(ins, outs, send_sems, recv_sems):
            cp.start()

    def finish(ins, outs, send_sems, recv_sems):
        for cp in copies(ins, outs, send_sems, recv_sems):
            cp.wait()

    outs = [jax.ShapeDtypeStruct((3,) + q.shape[1:], q.dtype) for q in partials]
    if into is None:
        return _Step(partials, outs, {}, 3 * n, start, finish)
    return _Step(list(partials) + list(into), outs, {n + k: k for k in range(n)}, 3 * n, start, finish)


def _step_pair_gather(shards, splits=None):
    splits = _splits(shards, splits)

    def copies(outs, send_sems, recv_sems, received):
        x, y, c = _place()
        for k, o in enumerate(outs):
            part = _half(o, splits[k], 1 - c if received else c)
            yield _remote(part, part, send_sems, recv_sems, k, (x, y, 1 - c))

    def start(ins, outs, send_sems, recv_sems):
        for cp in copies(outs, send_sems, recv_sems, False):
            cp.start()

    def finish(ins, outs, send_sems, recv_sems):
        for cp in copies(outs, send_sems, recv_sems, True):
            cp.wait_recv()
        for cp in copies(outs, send_sems, recv_sems, False):
            cp.wait_send()

    return _Step(shards, [_like(s) for s in shards], {k: k for k in range(len(shards))}, len(shards), start, finish)


def _call(body, *, name, out_shape, grid, in_specs, out_specs, operands, scratch_shapes=(), semantics=None, steps=()):
    single = not isinstance(out_shape, (tuple, list))
    out_shapes = [out_shape] if single else list(out_shape)
    out_spec_list = [out_specs] if single else list(out_specs)
    steps = list(steps)
    if not steps:
        res = pl.pallas_call(body, name=name, out_shape=out_shapes, grid=grid, in_specs=list(in_specs),
                             out_specs=out_spec_list, scratch_shapes=list(scratch_shapes),
                             compiler_params=_cparams(semantics))(*operands)
        return res[0] if single else res
    n_in, n_out, n_scr = len(operands), len(out_shapes), len(scratch_shapes)
    x_in = [a for s in steps for a in s.ins]
    x_out = [o for s in steps for o in s.outs]
    aliases, in_off, out_off = {}, 0, 0
    for s in steps:
        for i, o in s.aliases.items():
            aliases[n_in + in_off + i] = n_out + out_off + o
        in_off += len(s.ins)
        out_off += len(s.outs)
    sems = []
    for s in steps:
        sems += [pltpu.SemaphoreType.DMA((s.n_sems,)), pltpu.SemaphoreType.DMA((s.n_sems,))]
    any_spec = pl.BlockSpec(memory_space=pl.ANY)

    def carried(*refs):
        pos = 0
        ins = refs[pos:pos + n_in]; pos += n_in
        xi = refs[pos:pos + len(x_in)]; pos += len(x_in)
        outs = refs[pos:pos + n_out]; pos += n_out
        xo = refs[pos:pos + len(x_out)]; pos += len(x_out)
        scr = refs[pos:pos + n_scr]; pos += n_scr
        sem_refs = refs[pos:]

        def each(action):
            i0 = o0 = 0
            for k, s in enumerate(steps):
                getattr(s, action)(xi[i0:i0 + len(s.ins)], xo[o0:o0 + len(s.outs)], sem_refs[2 * k], sem_refs[2 * k + 1])
                i0 += len(s.ins)
                o0 += len(s.outs)

        if grid:
            first = functools.reduce(jnp.logical_and, [pl.program_id(d) == 0 for d in range(len(grid))])
            last = functools.reduce(jnp.logical_and, [pl.program_id(d) == grid[d] - 1 for d in range(len(grid))])
            pl.when(first)(lambda: each("start"))
            body(*ins, *outs, *scr)
            pl.when(last)(lambda: each("finish"))
        else:
            each("start")
            body(*ins, *outs, *scr)
            each("finish")

    res = pl.pallas_call(
        carried, name=name, out_shape=out_shapes + x_out, grid=grid,
        in_specs=list(in_specs) + [any_spec] * len(x_in), out_specs=out_spec_list + [any_spec] * len(x_out),
        scratch_shapes=list(scratch_shapes) + sems, input_output_aliases=aliases,
        compiler_params=_cparams(None if semantics is None else ("arbitrary",) * len(grid)),
    )(*operands, *x_in)
    o0 = n_out
    for s in steps:
        s.results = list(res[o0:o0 + len(s.outs)])
        o0 += len(s.outs)
    return res[0] if single else tuple(res[:n_out])


def _run_steps(name, steps):
    _call(lambda: None, name=name, out_shape=[], grid=(), in_specs=[], out_specs=[], operands=[], steps=steps)
    return [s.results for s in steps]


def _mm(name, a, b, mode, out_dtype=F32, acc=None, b_colblock=0, k_rows=None, out_rows=None, steps=()):
    resident_bytes = 8 << 20
    if mode == "nn":
        m, k = a.shape
        n = b.shape[1]
        tm = m
        while tm * k * 2 > resident_bytes and tm % 32 == 0:
            tm //= 2
        tn = _tile(n)
        grid = (m // tm, n // tn)
        in_specs = [pl.BlockSpec((tm, k), lambda i, j: (i, 0)), pl.BlockSpec((k, tn), lambda i, j: (0, j))]
        out_shape, out_block = (m, n), (tm, tn)
    elif mode == "nt":
        m, n = a.shape
        k = k_rows or b.shape[0]
        tm = m
        while tm * n * 2 > resident_bytes and tm % 32 == 0:
            tm //= 2
        tk = _tile(k)
        grid = (m // tm, k // tk)
        in_specs = [pl.BlockSpec((tm, n), lambda i, j: (i, 0)), pl.BlockSpec((tk, n), lambda i, j: (j, b_colblock))]
        out_shape, out_block = (m, k), (tm, tk)
    else:
        m, k = a.shape
        n = b.shape[1]
        tk, tn = _tile(k), (n if m * n * 2 <= resident_bytes else _tile(n))
        grid = (k // tk, n // tn)
        in_specs = [pl.BlockSpec((m, tk), lambda i, j: (0, i)), pl.BlockSpec((m, tn), lambda i, j: (0, j))]
        out_shape, out_block = (out_rows or k, n), (tk, tn)
    out_spec = pl.BlockSpec(out_block, lambda i, j: (i, j))
    has_acc = acc is not None

    def body(*refs):
        a_ref, b_ref = refs[0], refs[1]
        o_ref = refs[-1]
        av, bv = a_ref[...], b_ref[...]
        if mode == "nn":
            r = _dot(av, bv)
        elif mode == "nt":
            r = _dot_nt(av, bv)
        else:
            r = _dot_tn(av, bv)
        if has_acc:
            r = r + refs[2][...]
        o_ref[...] = r.astype(o_ref.dtype)

    operands = [a, b]
    if has_acc:
        in_specs = in_specs + [out_spec]
        operands.append(acc)
    return _call(body, name=name, out_shape=jax.ShapeDtypeStruct(out_shape, out_dtype), grid=grid, in_specs=in_specs,
                 out_specs=out_spec, operands=operands, semantics=("parallel", "parallel"), steps=steps)


def _tn_rows_into(name, a, b, into, row0, nrows):
    m, k = a.shape
    n = b.shape[1]

    def body(a_ref, b_ref, into_ref, o_ref):
        o_ref[...] = _dot_tn(a_ref[...], b_ref[...])[0:nrows].astype(o_ref.dtype)

    return pl.pallas_call(
        body, name=name, out_shape=jax.ShapeDtypeStruct(into.shape, into.dtype), grid=(1,),
        in_specs=[pl.BlockSpec((m, k), lambda i: (0, 0)), pl.BlockSpec((m, n), lambda i: (0, 0)),
                  pl.BlockSpec(memory_space=pl.ANY)],
        out_specs=pl.BlockSpec((nrows, n), lambda i: (row0 // nrows, 0)),
        input_output_aliases={2: 0}, compiler_params=_cparams(("arbitrary",)),
    )(a, b, into)


def _rms_fwd(name, h, w):
    rows, width = h.shape
    tm = _row_tile(rows, width)

    def body(h_ref, w_ref, o_ref):
        x = h_ref[...]
        r = lax.rsqrt(jnp.mean(x * x, axis=-1, keepdims=True) + RMS_EPS)
        o_ref[...] = (x * r * w_ref[...]).astype(BF16)

    return pl.pallas_call(
        body, name=name, out_shape=jax.ShapeDtypeStruct((rows, width), BF16), grid=(rows // tm,),
        in_specs=[pl.BlockSpec((tm, width), lambda i: (i, 0)), pl.BlockSpec((1, width), lambda i: (0, 0))],
        out_specs=pl.BlockSpec((tm, width), lambda i: (i, 0)), compiler_params=_cparams(("parallel",)),
    )(h, w)


def _resid_norm_fwd(name, h, pre, w, next_norms=()):
    rows, width = h.shape
    tm = _row_tile(rows, width)
    n_next = len(next_norms)

    def body(*refs):
        h_ref, p_ref, w_ref = refs[:3]
        v_refs = refs[3:3 + n_next]
        o_ref = refs[3 + n_next]
        n_refs = refs[4 + n_next:]
        p = p_ref[...]
        r = lax.rsqrt(jnp.mean(p * p, axis=-1, keepdims=True) + RMS_EPS)
        x = h_ref[...] + jnp.where(_rows_mask(pl.program_id(0), tm), p * r * w_ref[...], 0.0)
        o_ref[...] = x
        if n_next:
            rx = lax.rsqrt(jnp.mean(x * x, axis=-1, keepdims=True) + RMS_EPS)
            for v_ref, n_ref in zip(v_refs, n_refs):
                n_ref[...] = (x * rx * v_ref[...]).astype(BF16)

    row_spec = pl.BlockSpec((tm, width), lambda i: (i, 0))
    vec_spec = pl.BlockSpec((1, width), lambda i: (0, 0))
    outs = pl.pallas_call(
        body, name=name,
        out_shape=[jax.ShapeDtypeStruct((rows, width), F32)] + [jax.ShapeDtypeStruct((rows, width), BF16)] * n_next,
        grid=(rows // tm,), in_specs=[row_spec, row_spec, vec_spec] + [vec_spec] * n_next,
        out_specs=[row_spec] * (1 + n_next), compiler_params=_cparams(("parallel",)),
    )(h, pre, w, *next_norms)
    return outs[0], list(outs[1:])


def _resid_norm_loss(name, h, pre, w, target):
    rows, width = h.shape

    def body(h_ref, p_ref, w_ref, t_ref, dh_ref, loss_ref, dp_ref, dw_ref):
        i = pl.program_id(0)
        p = p_ref[...]
        r = lax.rsqrt(jnp.mean(p * p, axis=-1, keepdims=True) + RMS_EPS)
        x = h_ref[...] + p * r * w_ref[...]
        real = (i + jnp.zeros((CHUNK, 1), jnp.int32)) >= 1
        diff = jnp.where(real, x - t_ref[...], 0.0)
        dh = diff * (1.0 / D_MODEL)
        dh_ref[...] = dh
        dp, dw_rows = _rms_bwd(dh, p, w_ref[...])
        dp_ref[...] = dp.astype(BF16)

        @pl.when(i == 0)
        def _():
            loss_ref[...] = jnp.zeros_like(loss_ref)
            dw_ref[...] = jnp.zeros_like(dw_ref)

        loss_ref[...] += jnp.sum(diff * diff) * (0.5 / D_MODEL)
        dw_ref[...] += jnp.sum(dw_rows, axis=0, keepdims=True)

    blk = pl.BlockSpec((CHUNK, width), lambda i: (i, 0))
    vec_spec = pl.BlockSpec((1, width), lambda i: (0, 0))
    return pl.pallas_call(
        body, name=name,
        out_shape=(jax.ShapeDtypeStruct((rows, width), F32), jax.ShapeDtypeStruct((1, LANES), F32),
                   jax.ShapeDtypeStruct((rows, width), BF16), jax.ShapeDtypeStruct((1, width), F32)),
        grid=(rows // CHUNK,),
        in_specs=[blk, blk, vec_spec, pl.BlockSpec((CHUNK, width), lambda i: (jnp.maximum(i - 1, 0), 0))],
        out_specs=(blk, pl.BlockSpec((1, LANES), lambda i: (0, 0)), blk, vec_spec),
        compiler_params=_cparams(("arbitrary",)),
    )(h, pre, w, target)


def _rms_bwd(dy, x, w):
    r = lax.rsqrt(jnp.mean(x * x, axis=-1, keepdims=True) + RMS_EPS)
    xhat = x * r
    dxhat = dy * w
    return r * (dxhat - xhat * jnp.mean(dxhat * xhat, axis=-1, keepdims=True)), dy * xhat


def _norm_bwd_add(name, dh, dhn, h, w, then=None, split_first_block=False, steps=()):
    rows, width = dh.shape
    tm = CHUNK if split_first_block else _row_tile(rows, width)
    fused = then is not None
    assert not (fused and split_first_block)

    def body(*refs):
        dh_ref, dhn_ref, h_ref, w_ref = refs[:4]
        o_ref, dw_ref = refs[6:8] if fused else refs[-2:]
        i = pl.program_id(0)
        valid = _rows_mask(i, tm)
        dx, dw_rows = _rms_bwd(dhn_ref[...], h_ref[...], w_ref[...])
        dh_new = dh_ref[...] + jnp.where(valid, dx, 0.0)
        if split_first_block:
            first_ref = refs[4]

            @pl.when(i == 0)
            def _():
                first_ref[...] = dh_new

            @pl.when(i > 0)
            def _():
                o_ref[...] = dh_new
        else:
            o_ref[...] = dh_new

        @pl.when(i == 0)
        def _():
            dw_ref[...] = jnp.zeros_like(dw_ref)

        dw_ref[...] += jnp.sum(dw_rows, axis=0, keepdims=True)
        if fused:
            p_ref, wp_ref, dp_ref, dwp_ref = refs[4], refs[5], refs[8], refs[9]
            dp, dwp_rows = _rms_bwd(jnp.where(valid, dh_new, 0.0), p_ref[...], wp_ref[...])
            dp_ref[...] = dp.astype(BF16)

            @pl.when(i == 0)
            def _():
                dwp_ref[...] = jnp.zeros_like(dwp_ref)

            dwp_ref[...] += jnp.sum(dwp_rows, axis=0, keepdims=True)

    row_spec = pl.BlockSpec((tm, width), lambda i: (i, 0))
    vec_spec = pl.BlockSpec((1, width), lambda i: (0, 0))
    row_f32, vec_f32 = jax.ShapeDtypeStruct((rows, width), F32), jax.ShapeDtypeStruct((1, width), F32)
    in_specs, operands = [row_spec, row_spec, row_spec, vec_spec], [dh, dhn, h, w]
    out_shape, out_specs = [row_f32, vec_f32], [row_spec, vec_spec]
    if split_first_block:
        out_shape = [jax.ShapeDtypeStruct((tm, width), F32), jax.ShapeDtypeStruct((rows - tm, width), F32), vec_f32]
        out_specs = [pl.BlockSpec((tm, width), lambda i: (0, 0)),
                     pl.BlockSpec((tm, width), lambda i: (jnp.maximum(i - 1, 0), 0)), vec_spec]
    if fused:
        in_specs += [row_spec, vec_spec]
        operands += list(then)
        out_shape += [jax.ShapeDtypeStruct((rows, width), BF16), vec_f32]
        out_specs += [row_spec, vec_spec]
    return _call(body, name=name, out_shape=out_shape, grid=(rows // tm,), in_specs=in_specs, out_specs=out_specs,
                 operands=operands, semantics=("arbitrary",), steps=steps)


def _shift_down(x, s, rows):
    return pltpu.roll(x, s, 0) if s else x


def _shift_up(x, s, rows):
    return pltpu.roll(x, rows - s, 0) if s else x


def _conv4_fwd(name, zx, cw, cb, steps=()):
    rows = zx.shape[0]
    off = D_INNER // LANES

    def body(x_ref, w_ref, b_ref, o_ref):
        x = x_ref[...]
        acc = b_ref[...] + w_ref[pl.ds(SSM_CONV - 1, 1), :] * x
        for s in range(1, SSM_CONV):
            acc = acc + w_ref[pl.ds(SSM_CONV - 1 - s, 1), :] * _shift_down(x, s, rows)
        valid = lax.broadcasted_iota(jnp.int32, (rows, 1), 0) >= PAD_ROWS
        o_ref[...] = jnp.where(valid, acc * _sigmoid(acc), 0.0)

    return _call(
        body, name=name, out_shape=jax.ShapeDtypeStruct((rows, D_XBC), F32), grid=(D_XBC // LANES,),
        in_specs=[pl.BlockSpec((rows, LANES), lambda j: (0, j + off)),
                  pl.BlockSpec((SSM_CONV, LANES), lambda j: (0, j)),
                  pl.BlockSpec((1, LANES), lambda j: (0, j))],
        out_specs=pl.BlockSpec((rows, LANES), lambda j: (0, j)), operands=[zx, cw, cb],
        semantics=("parallel",), steps=steps)


def _conv4_bwd(name, zx, dout, cw, cb, into):
    rows, width = dout.shape
    zoff = D_INNER // LANES

    def body(x_ref, d_ref, w_ref, b_ref, into_ref, dx_ref, dw_ref, db_ref):
        x = x_ref[...]
        shifted = [_shift_down(x, s, rows) for s in range(SSM_CONV)]
        acc = b_ref[...]
        for s in range(SSM_CONV):
            acc = acc + w_ref[pl.ds(SSM_CONV - 1 - s, 1), :] * shifted[s]
        sig = _sigmoid(acc)
        valid = lax.broadcasted_iota(jnp.int32, (rows, 1), 0) >= PAD_ROWS
        dpre = jnp.where(valid, d_ref[...] * sig * (1.0 + acc * (1.0 - sig)), 0.0)
        dx = w_ref[pl.ds(SSM_CONV - 1, 1), :] * dpre
        for s in range(1, SSM_CONV):
            dx = dx + w_ref[pl.ds(SSM_CONV - 1 - s, 1), :] * _shift_up(dpre, s, rows)
        dx_ref[...] = dx.astype(BF16)
        for s in range(SSM_CONV):
            dw_ref[pl.ds(SSM_CONV - 1 - s, 1), :] = jnp.sum(dpre * shifted[s], axis=0, keepdims=True)
        db_ref[...] = jnp.sum(dpre, axis=0, keepdims=True)

    return pl.pallas_call(
        body, name=name,
        out_shape=(jax.ShapeDtypeStruct(into.shape, BF16), jax.ShapeDtypeStruct((SSM_CONV, width), F32),
                   jax.ShapeDtypeStruct((1, width), F32)),
        grid=(width // LANES,),
        in_specs=[pl.BlockSpec((rows, LANES), lambda j: (0, j + zoff)),
                  pl.BlockSpec((rows, LANES), lambda j: (0, j)),
                  pl.BlockSpec((SSM_CONV, LANES), lambda j: (0, j)),
                  pl.BlockSpec((1, LANES), lambda j: (0, j)),
                  pl.BlockSpec(memory_space=pl.ANY)],
        out_specs=(pl.BlockSpec((rows, LANES), lambda j: (0, j + zoff)),
                   pl.BlockSpec((SSM_CONV, LANES), lambda j: (0, j)),
                   pl.BlockSpec((1, LANES), lambda j: (0, j))),
        input_output_aliases={4: 0}, compiler_params=_cparams(("parallel",)),
    )(zx, dout, cw, cb, into)


FFN_TILE = 2 * LANES


def _ffn_up_conv(name, hn, w_up, cw, cb, steps=()):
    rows, k = hn.shape
    chip_blocks = w_up.shape[2] // LANES
    half_blocks = D_FF // LANES
    nt = D_FF // FFN_TILE

    def weight_block(offset):
        return pl.BlockSpec((None, k, LANES), lambda j: ((2 * j + offset) // chip_blocks, 0, (2 * j + offset) % chip_blocks))

    def body(a_ref, g0, g1, v0, v1, wg_ref, wv_ref, bg_ref, bv_ref, upg_ref, upv_ref, act_ref):
        a = a_ref[...]
        g = _dot(a, jnp.concatenate([g0[...], g1[...]], axis=1))
        v = _dot(a, jnp.concatenate([v0[...], v1[...]], axis=1))
        upg_ref[...] = g
        upv_ref[...] = v
        ug, uv = bg_ref[...], bv_ref[...]
        for s in range(FFN_CONV):
            ug = ug + wg_ref[pl.ds(FFN_CONV - 1 - s, 1), :] * _shift_down(g, s, rows)
            uv = uv + wv_ref[pl.ds(FFN_CONV - 1 - s, 1), :] * _shift_down(v, s, rows)
        act_ref[...] = (ug * _sigmoid(ug) * uv).astype(BF16)

    col = pl.BlockSpec((rows, FFN_TILE), lambda j: (0, j))
    wsp = lambda shift: pl.BlockSpec((FFN_CONV, FFN_TILE), lambda j: (0, j + shift))
    bsp = lambda shift: pl.BlockSpec((1, FFN_TILE), lambda j: (0, j + shift))
    half = jax.ShapeDtypeStruct((rows, D_FF), F32)
    return _call(
        body, name=name, out_shape=(half, half, jax.ShapeDtypeStruct((rows, D_FF), BF16)), grid=(nt,),
        in_specs=[pl.BlockSpec((rows, k), lambda j: (0, 0)), weight_block(0), weight_block(1),
                  weight_block(half_blocks), weight_block(half_blocks + 1), wsp(0), wsp(nt), bsp(0), bsp(nt)],
        out_specs=(col, col, col), operands=[hn, w_up, w_up, w_up, w_up, cw, cw, cb, cb],
        semantics=("parallel",), steps=steps)


def _ffn_conv_bwd(name, up_g, up_v, dact, cw, cb, hn, w_up, steps=()):
    rows, k = hn.shape
    chip_blocks = w_up.shape[2] // LANES
    nt = D_FF // LANES

    def weight_block(shift):
        return pl.BlockSpec((None, k, LANES), lambda j: ((j + shift) // chip_blocks, 0, (j + shift) % chip_blocks))

    def body(g_ref, v_ref, d_ref, wg_ref, wv_ref, bg_ref, bv_ref, upg_ref, upv_ref, hn_ref,
             dwg_ref, dwv_ref, dbg_ref, dbv_ref, dhn_ref, dup_ref, acc, hn_scr, hnt_scr, dup_scr, sems):
        j = pl.program_id(0)
        hn_copy = pltpu.make_async_copy(hn_ref, hn_scr, sems.at[0])
        dhn_copy = pltpu.make_async_copy(acc, dhn_ref, sems.at[0])

        def dup_copy(step, half):
            block, slot = step + half * nt, 2 * (step % 2) + half
            cols = pl.ds(pl.multiple_of((block % chip_blocks) * LANES, LANES), LANES)
            return pltpu.make_async_copy(dup_scr.at[slot], dup_ref.at[block // chip_blocks, :, cols], sems.at[1 + slot])

        @pl.when(j == 0)
        def _():
            hn_copy.start()
            acc[...] = jnp.zeros_like(acc)
            hn_copy.wait()
            for r in range(0, rows, LANES):
                hnt_scr[:, r:r + LANES] = hn_scr[r:r + LANES, :].T

        @pl.when(j >= 2)
        def _():
            dup_copy(j - 2, 0).wait()
            dup_copy(j - 2, 1).wait()

        g, v = g_ref[...], v_ref[...]
        gs = [_shift_down(g, s, rows) for s in range(FFN_CONV)]
        vs = [_shift_down(v, s, rows) for s in range(FFN_CONV)]
        ug, uv = bg_ref[...], bv_ref[...]
        for s in range(FFN_CONV):
            ug = ug + wg_ref[pl.ds(FFN_CONV - 1 - s, 1), :] * gs[s]
            uv = uv + wv_ref[pl.ds(FFN_CONV - 1 - s, 1), :] * vs[s]
        sig = _sigmoid(ug)
        dsig = d_ref[...] * sig
        dup = []
        for dpre, src, w_ref, dw_ref, db_ref in (
                (dsig * uv * (1.0 + ug * (1.0 - sig)), gs, wg_ref, dwg_ref, dbg_ref),
                (dsig * ug, vs, wv_ref, dwv_ref, dbv_ref)):
            dx = w_ref[pl.ds(FFN_CONV - 1, 1), :] * dpre
            for s in range(1, FFN_CONV):
                dx = dx + w_ref[pl.ds(FFN_CONV - 1 - s, 1), :] * _shift_up(dpre, s, rows)
            dup.append(dx.astype(BF16))
            for s in range(FFN_CONV):
                dw_ref[pl.ds(FFN_CONV - 1 - s, 1), :] = jnp.sum(dpre * src[s], axis=0, keepdims=True)
            db_ref[...] = jnp.sum(dpre, axis=0, keepdims=True)
        dup = jnp.concatenate(dup, axis=1)
        acc[...] += _dot_nt(dup, jnp.concatenate([upg_ref[...], upv_ref[...]], axis=1))
        dw = _dot(hnt_scr[...], dup)
        slot = 2 * (j % 2)
        dup_scr[slot] = dw[:, :LANES].astype(BF16)
        dup_scr[slot + 1] = dw[:, LANES:].astype(BF16)
        dup_copy(j, 0).start()
        dup_copy(j, 1).start()

        @pl.when(j == nt - 1)
        def _():
            dhn_copy.start()
            for step in (j - 1, j):
                dup_copy(step, 0).wait()
                dup_copy(step, 1).wait()
            dhn_copy.wait()

    col = pl.BlockSpec((rows, LANES), lambda j: (0, j))
    wsp = lambda shift: pl.BlockSpec((FFN_CONV, LANES), lambda j: (0, j + shift))
    bsp = lambda shift: pl.BlockSpec((1, LANES), lambda j: (0, j + shift))
    any_spec = pl.BlockSpec(memory_space=pl.ANY)
    dw_shape = jax.ShapeDtypeStruct((FFN_CONV, D_FF), F32)
    db_shape = jax.ShapeDtypeStruct((1, D_FF), F32)
    return _call(
        body, name=name, grid=(nt,),
        out_shape=(dw_shape, dw_shape, db_shape, db_shape, jax.ShapeDtypeStruct((rows, k), F32),
                   jax.ShapeDtypeStruct(w_up.shape, BF16)),
        in_specs=[col, col, col, wsp(0), wsp(nt), bsp(0), bsp(nt), weight_block(0), weight_block(nt), any_spec],
        out_specs=(wsp(0), wsp(0), bsp(0), bsp(0), any_spec, any_spec),
        operands=[up_g, up_v, dact, cw, cw, cb, cb, w_up, w_up, hn],
        scratch_shapes=[pltpu.VMEM((rows, k), F32), pltpu.VMEM((rows, k), BF16), pltpu.VMEM((k, rows), BF16),
                        pltpu.VMEM((4, k, LANES), BF16), pltpu.SemaphoreType.DMA((5,))],
        semantics=("arbitrary",), steps=steps)


def _dt_fwd(name, dtr, bias):
    rows = dtr.shape[0]
    tm = _row_tile(rows, LANES)

    def body(d_ref, b_ref, o_ref):
        v = d_ref[...] + b_ref[...]
        sp = jnp.maximum(v, 0.0) + jnp.log1p(jnp.exp(-jnp.abs(v)))
        lane = lax.broadcasted_iota(jnp.int32, (tm, LANES), 1)
        ok = _rows_mask(pl.program_id(0), tm) & (lane < SSM_HEADS)
        o_ref[...] = jnp.where(ok, sp, 0.0)

    return pl.pallas_call(
        body, name=name, out_shape=jax.ShapeDtypeStruct((rows, LANES), F32), grid=(rows // tm,),
        in_specs=[pl.BlockSpec((tm, LANES), lambda i: (i, 0)), pl.BlockSpec((1, LANES), lambda i: (0, 0))],
        out_specs=pl.BlockSpec((tm, LANES), lambda i: (i, 0)), compiler_params=_cparams(("parallel",)),
    )(dtr, bias)


def _dt_bwd(name, ddt, dtr, bias):
    rows = dtr.shape[0]
    tm = _row_tile(rows, LANES)

    def body(g_ref, d_ref, b_ref, o_ref, db_ref):
        i = pl.program_id(0)
        lane = lax.broadcasted_iota(jnp.int32, (tm, LANES), 1)
        ok = _rows_mask(i, tm) & (lane < SSM_HEADS)
        dv = jnp.where(ok, g_ref[...] * _sigmoid(d_ref[...] + b_ref[...]), 0.0)
        o_ref[...] = dv.astype(BF16)

        @pl.when(i == 0)
        def _():
            db_ref[...] = jnp.zeros_like(db_ref)

        db_ref[...] += jnp.sum(dv, axis=0, keepdims=True)

    row_spec = pl.BlockSpec((tm, LANES), lambda i: (i, 0))
    vec_spec = pl.BlockSpec((1, LANES), lambda i: (0, 0))
    return pl.pallas_call(
        body, name=name,
        out_shape=(jax.ShapeDtypeStruct((rows, LANES), BF16), jax.ShapeDtypeStruct((1, LANES), F32)),
        grid=(rows // tm,), in_specs=[row_spec, row_spec, vec_spec], out_specs=(row_spec, vec_spec),
        compiler_params=_cparams(("arbitrary",)),
    )(ddt, dtr, bias)


def _gate_fwd(name, y, zx, w, steps=()):
    rows = y.shape[0]
    tm = _row_tile(rows, D_INNER)

    def body(y_ref, z_ref, w_ref, o_ref):
        z = z_ref[...]
        g = y_ref[...] * (z * _sigmoid(z))
        r = lax.rsqrt(jnp.mean(g * g, axis=-1, keepdims=True) + RMS_EPS)
        o_ref[...] = (g * r * w_ref[...]).astype(BF16)

    row_spec = pl.BlockSpec((tm, D_INNER), lambda i: (i, 0))
    return _call(
        body, name=name, out_shape=jax.ShapeDtypeStruct((rows, D_INNER), BF16), grid=(rows // tm,),
        in_specs=[row_spec, row_spec, pl.BlockSpec((1, D_INNER), lambda i: (0, 0))],
        out_specs=row_spec, operands=[y, zx, w], semantics=("parallel",), steps=steps)


def _gate_bwd(name, dyn, y, zx, w):
    rows = y.shape[0]
    tm = _row_tile(rows, D_INNER)

    def body(d_ref, y_ref, z_ref, w_ref, dy_ref, dz_ref, dw_ref):
        i = pl.program_id(0)
        z, yv = z_ref[...], y_ref[...]
        sig = _sigmoid(z)
        sz = z * sig
        g = yv * sz
        r = lax.rsqrt(jnp.mean(g * g, axis=-1, keepdims=True) + RMS_EPS)
        ghat = g * r
        dn = d_ref[...]
        dghat = dn * w_ref[...]
        dg = r * (dghat - ghat * jnp.mean(dghat * ghat, axis=-1, keepdims=True))
        dy_ref[...] = dg * sz
        dz_ref[...] = (dg * yv * sig * (1.0 + z * (1.0 - sig))).astype(BF16)

        @pl.when(i == 0)
        def _():
            dw_ref[...] = jnp.zeros_like(dw_ref)

        dw_ref[...] += jnp.sum(dn * ghat, axis=0, keepdims=True)

    row_spec = pl.BlockSpec((tm, D_INNER), lambda i: (i, 0))
    vec_spec = pl.BlockSpec((1, D_INNER), lambda i: (0, 0))
    return pl.pallas_call(
        body, name=name,
        out_shape=(jax.ShapeDtypeStruct((rows, D_INNER), F32), jax.ShapeDtypeStruct((rows, D_MAIN), BF16),
                   jax.ShapeDtypeStruct((1, D_INNER), F32)),
        grid=(rows // tm,), in_specs=[row_spec, row_spec, row_spec, vec_spec],
        out_specs=(row_spec, row_spec, vec_spec), compiler_params=_cparams(("arbitrary",)),
    )(dyn, y, zx, w)


def _split3(x):
    hi = x.astype(BF16)
    r1 = x - hi.astype(F32)
    mid = r1.astype(BF16)
    lo = (r1 - mid.astype(F32)).astype(BF16)
    return hi, mid, lo


def _dot3_data_lhs(x, sel):
    sel16 = sel.astype(F32).astype(BF16)
    hi, mid, lo = _split3(x)
    return _dot(hi, sel16) + _dot(mid, sel16) + _dot(lo, sel16)


def _dot2_data_lhs(x, sel):
    sel16 = sel.astype(F32).astype(BF16)
    hi = x.astype(BF16)
    mid = (x - hi.astype(F32)).astype(BF16)
    return _dot(hi, sel16) + _dot(mid, sel16)


def _dot3_data_rhs(sel, x):
    sel16 = sel.astype(F32).astype(BF16)
    hi, mid, lo = _split3(x)
    return _dot(sel16, hi) + _dot(sel16, mid) + _dot(sel16, lo)


def _causal_masks():
    r = lax.broadcasted_iota(jnp.int32, (CHUNK, CHUNK), 0)
    c = lax.broadcasted_iota(jnp.int32, (CHUNK, CHUNK), 1)
    return r >= c, r <= c


def _expand_heads_matrix(g):
    k = lax.broadcasted_iota(jnp.int32, (LANES, GROUP_W), 0)
    j = lax.broadcasted_iota(jnp.int32, (LANES, GROUP_W), 1)
    return HEADS_PER_GROUP * g + jnp.right_shift(j, 6) == k


def _reduce_heads_matrix(g):
    j = lax.broadcasted_iota(jnp.int32, (GROUP_W, LANES), 0)
    k = lax.broadcasted_iota(jnp.int32, (GROUP_W, LANES), 1)
    return HEADS_PER_GROUP * g + jnp.right_shift(j, 6) == k


def _reduce_pair_matrix(g, p):
    j = lax.broadcasted_iota(jnp.int32, (LANES, LANES), 0)
    k = lax.broadcasted_iota(jnp.int32, (LANES, LANES), 1)
    return HEADS_PER_GROUP * g + 2 * p + jnp.right_shift(j, 6) == k


def _group_cols(ref, g, width):
    return ref.at[:, pl.ds(g * width, width)]


def _ssd_prep(name, dt, a128, steps=()):
    rows = dt.shape[0]
    nc = rows // CHUNK

    def body(dt_ref, a_ref, dte_ref, acs_ref, acst_ref):
        causal, _ = _causal_masks()
        dtv = dt_ref[...]
        acs = _dot3_data_rhs(causal, dtv) * a_ref[...]
        acst_ref[...] = acs.T[0:SSM_HEADS]
        for g in range(N_GROUPS):
            expand = _expand_heads_matrix(g)
            _group_cols(dte_ref, g, GROUP_W)[...] = _dot3_data_lhs(dtv, expand)
            _group_cols(acs_ref, g, GROUP_W)[...] = _dot3_data_lhs(acs, expand)

    blk = pl.BlockSpec((CHUNK, D_INNER), lambda c: (c, 0))
    shp = jax.ShapeDtypeStruct((rows, D_INNER), F32)
    return _call(
        body, name=name, out_shape=(shp, shp, jax.ShapeDtypeStruct((nc, SSM_HEADS, CHUNK), F32)), grid=(nc,),
        in_specs=[pl.BlockSpec((CHUNK, LANES), lambda c: (c, 0)), pl.BlockSpec((1, LANES), lambda c: (0, 0))],
        out_specs=(blk, blk, pl.BlockSpec((None, SSM_HEADS, CHUNK), lambda c: (c, 0, 0))),
        operands=[dt, a128], semantics=("parallel",), steps=steps)


def _ssd_common(x_ref, b_ref, c_ref, dte_ref, acs_ref):
    x = x_ref[...]
    dt_exp = dte_ref[...]
    acs_exp = acs_ref[...]
    tot_exp = acs_ref[pl.ds(CHUNK - 1, 1), :]
    xdt = x * dt_exp
    e_exp = jnp.exp(acs_exp)
    f_exp = jnp.exp(tot_exp - acs_exp)
    return _causal_masks(), x, dt_exp, acs_exp, tot_exp, xdt, e_exp, f_exp, b_ref[...], c_ref[...]


def _pair_decay(acs_pair, acs_row, e, causal):
    lane = lax.broadcasted_iota(jnp.int32, (CHUNK, LANES), 1)
    mine = (lane < HEAD_DIM) if e == 0 else (lane >= HEAD_DIM)
    a_l = jnp.where(mine, acs_pair, pltpu.roll(acs_pair, HEAD_DIM, 1))
    seg = a_l - acs_row
    dm = jnp.where(causal[0], jnp.exp(jnp.minimum(seg, 0.0)), 0.0)
    dmt = jnp.where(causal[1], jnp.exp(jnp.minimum(-seg, 0.0)), 0.0)
    return dm, dmt


def _ssd_specs(index_of_chunk):
    wide = pl.BlockSpec((CHUNK, D_INNER), lambda c: (index_of_chunk(c), 0))
    b_spec = pl.BlockSpec((CHUNK, D_BC), lambda c: (index_of_chunk(c), D_INNER // D_BC))
    c_spec = pl.BlockSpec((CHUNK, D_BC), lambda c: (index_of_chunk(c), D_INNER // D_BC + 1))
    rows_spec = pl.BlockSpec((None, SSM_HEADS, CHUNK), lambda c: (index_of_chunk(c), 0, 0))
    state_spec = pl.BlockSpec((N_GROUPS, None, D_STATE, GROUP_W), lambda c: (0, index_of_chunk(c), 0, 0))
    return wide, b_spec, c_spec, rows_spec, state_spec


def _ssd_fwd(name, xbc, dt_exp, acs_exp, acs_rows, dskexp, steps=()):
    rows = xbc.shape[0]
    nc = rows // CHUNK

    def body(x_ref, b_ref, c_ref, dte_ref, acs_ref, acst_ref, dsk_ref, y_ref, st_ref, s_scr):
        @pl.when(pl.program_id(0) == 0)
        def _():
            s_scr[...] = jnp.zeros_like(s_scr)

        lane = lax.broadcasted_iota(jnp.int32, (CHUNK, LANES), 1)
        for g in range(N_GROUPS):
            y_g = _group_cols(y_ref, g, GROUP_W)
            causal, x, _, acs_exp_v, tot_exp, xdt, e_exp, f_exp, bm, cm = _ssd_common(
                _group_cols(x_ref, g, GROUP_W), _group_cols(b_ref, g, D_STATE), _group_cols(c_ref, g, D_STATE),
                _group_cols(dte_ref, g, GROUP_W), _group_cols(acs_ref, g, GROUP_W))
            state = s_scr[g]
            st_ref[g] = state
            cb16, bb16 = cm.astype(BF16), bm.astype(BF16)
            cb = _dot_nt(cb16, bb16)
            base = e_exp * _dot(cb16, state.astype(BF16)) + _group_cols(dsk_ref, g, GROUP_W)[...] * x
            for p in range(HEADS_PER_GROUP // 2):
                sl = slice(p * LANES, (p + 1) * LANES)
                xp = xdt[:, sl].astype(BF16)
                yd = []
                for e in range(2):
                    acs_row = acst_ref[pl.ds(g * HEADS_PER_GROUP + 2 * p + e, 1), :]
                    dm, _ = _pair_decay(acs_exp_v[:, sl], acs_row, e, causal)
                    yd.append(_dot((cb * dm).astype(BF16), xp))
                y_g[:, sl] = jnp.where(lane < HEAD_DIM, yd[0], yd[1]) + base[:, sl]
            s_scr[g] = jnp.exp(tot_exp) * state + _dot_tn(bb16, (f_exp * xdt).astype(BF16))

    wide, b_spec, c_spec, rows_spec, state_spec = _ssd_specs(lambda c: c)
    return _call(
        body, name=name,
        out_shape=(jax.ShapeDtypeStruct((rows, D_INNER), F32),
                   jax.ShapeDtypeStruct((N_GROUPS, nc, D_STATE, GROUP_W), F32)),
        grid=(nc,),
        in_specs=[wide, b_spec, c_spec, wide, wide, rows_spec, pl.BlockSpec((1, D_INNER), lambda c: (0, 0))],
        out_specs=(wide, state_spec),
        scratch_shapes=[pltpu.VMEM((N_GROUPS, D_STATE, GROUP_W), F32)],
        operands=[xbc, xbc, xbc, dt_exp, acs_exp, acs_rows, dskexp], semantics=("arbitrary",), steps=steps)


def _ssd_bwd(name, xbc, dt_exp, acs_exp, acs_rows, dt, a128, dskexp, dy, states, steps=()):
    rows = xbc.shape[0]
    nc = rows // CHUNK
    last = nc - 1

    def body(x_ref, b_ref, c_ref, dte_ref, acs_ref, acst_ref, dt_ref, a128_ref, dsk_all, dy_all, st_all,
             dxbc_all, ddt_ref, dalog_ref, ddsk_ref, ds_all):
        dx_all, db_all, dc_all = (dxbc_all.at[:, :D_INNER], dxbc_all.at[:, D_INNER:D_INNER + D_BC],
                                  dxbc_all.at[:, D_INNER + D_BC:])

        @pl.when(pl.program_id(0) == 0)
        def _():
            ds_all[...] = jnp.zeros_like(ds_all)
            dalog_ref[...] = jnp.zeros_like(dalog_ref)
            ddsk_ref[...] = jnp.zeros_like(ddsk_ref)

        dacs = jnp.zeros((CHUNK, LANES), F32)
        ddt_x = jnp.zeros((CHUNK, LANES), F32)
        for g in range(N_GROUPS):
            dacs_g, ddt_x_g = group(
                g, _group_cols(x_ref, g, GROUP_W), _group_cols(b_ref, g, D_STATE), _group_cols(c_ref, g, D_STATE),
                _group_cols(dte_ref, g, GROUP_W), _group_cols(acs_ref, g, GROUP_W), acst_ref,
                _group_cols(dsk_all, g, GROUP_W), _group_cols(dy_all, g, GROUP_W), st_all.at[g],
                _group_cols(dx_all, g, GROUP_W), _group_cols(db_all, g, D_STATE), _group_cols(dc_all, g, D_STATE),
                ddsk_ref, ds_all.at[g])
            dacs, ddt_x = dacs + dacs_g, ddt_x + ddt_x_g
        _, causal_t = _causal_masks()
        da = _dot3_data_rhs(causal_t, dacs)
        ddt_ref[...] = da * a128_ref[...] + ddt_x
        dalog_ref[...] += jnp.sum(da * dt_ref[...], axis=0, keepdims=True) * a128_ref[...]

    def group(g, x_ref, b_ref, c_ref, dte_ref, acs_ref, acst_ref, dsk_ref, dy_ref, st_ref,
              dx_ref, db_ref, dc_ref, ddsk_ref, ds_scr):
        causal, x, dt_exp, acs_exp_v, tot_exp, xdt, e_exp, f_exp, bm, cm = _ssd_common(
            x_ref, b_ref, c_ref, dte_ref, acs_ref)
        reduce_heads = _reduce_heads_matrix(g)
        state, dstate = st_ref[...], ds_scr[...]
        dyv = dy_ref[...]
        cb16, bb16 = cm.astype(BF16), bm.astype(BF16)
        s16, ds16 = state.astype(BF16), dstate.astype(BF16)
        cb = _dot_nt(cb16, bb16)
        cbt = _dot_nt(bb16, cb16)
        cs = _dot(cb16, s16)
        bds = _dot(bb16, ds16)
        edy = e_exp * dyv
        fx = f_exp * xdt
        dxdt_base = f_exp * bds
        dc_acc = _dot_nt(edy.astype(BF16), s16)
        db_acc = _dot_nt(fx.astype(BF16), ds16)
        ds_scr[...] = jnp.exp(tot_exp) * dstate + _dot_tn(cb16, edy.astype(BF16))
        q = fx * bds
        dacs = _dot2_data_lhs(edy * cs - q, reduce_heads)
        dtot = jnp.sum(_dot2_data_lhs(q + jnp.exp(tot_exp) * dstate * state, reduce_heads), axis=0, keepdims=True)
        ddsk_ref[...] += jnp.sum(_dot2_data_lhs(dyv * x, reduce_heads), axis=0, keepdims=True)
        lane = lax.broadcasted_iota(jnp.int32, (CHUNK, LANES), 1)
        dcb = jnp.zeros((CHUNK, CHUNK), F32)
        dcbt = jnp.zeros((CHUNK, CHUNK), F32)
        ddt_x = jnp.zeros((CHUNK, LANES), F32)
        for p in range(HEADS_PER_GROUP // 2):
            sl = slice(p * LANES, (p + 1) * LANES)
            xp, dyp = xdt[:, sl], dyv[:, sl]
            xp16, dyp16 = xp.astype(BF16), dyp.astype(BF16)
            dxh = []
            for e in range(2):
                h = 2 * p + e
                mine = (lane < HEAD_DIM) if e == 0 else (lane >= HEAD_DIM)
                acs_row = acst_ref[pl.ds(g * HEADS_PER_GROUP + h, 1), :]
                dm, dmt = _pair_decay(acs_exp_v[:, sl], acs_row, e, causal)
                m, mt = cb * dm, cbt * dmt
                xh16 = jnp.where(mine, xp, 0.0).astype(BF16)
                dyh16 = jnp.where(mine, dyp, 0.0).astype(BF16)
                d_m = _dot_nt(dyh16, xp16)
                d_mt = _dot_nt(xh16, dyp16)
                dacs_h = (jnp.sum(d_m * m, axis=-1, keepdims=True)
                          - jnp.sum(d_mt * mt, axis=-1, keepdims=True))
                dacs = dacs + jnp.where(lane == HEADS_PER_GROUP * g + h, dacs_h, 0.0)
                dcb = dcb + d_m * dm
                dcbt = dcbt + d_mt * dmt
                dxh.append(_dot(mt.astype(BF16), dyp16))
            dxdt = jnp.where(lane < HEAD_DIM, dxh[0], dxh[1]) + dxdt_base[:, sl]
            dx_ref[:, sl] = dxdt * dt_exp[:, sl] + dsk_ref[:, sl] * dyp
            ddt_x = ddt_x + _dot2_data_lhs(dxdt * x[:, sl], _reduce_pair_matrix(g, p))
        dc_ref[...] = dc_acc + _dot(dcb.astype(BF16), bb16)
        db_ref[...] = db_acc + _dot(dcbt.astype(BF16), cb16)
        row = lax.broadcasted_iota(jnp.int32, (CHUNK, LANES), 0)
        return dacs + jnp.where(row == CHUNK - 1, dtot, 0.0), ddt_x

    wide, b_spec, c_spec, rows_spec, state_spec = _ssd_specs(lambda c: last - c)
    heads_spec = pl.BlockSpec((CHUNK, LANES), lambda c: (last - c, 0))
    vec_spec = pl.BlockSpec((1, LANES), lambda c: (0, 0))
    vec_shape = jax.ShapeDtypeStruct((1, LANES), F32)
    return _call(
        body, name=name,
        out_shape=(jax.ShapeDtypeStruct((rows, D_XBC), F32), jax.ShapeDtypeStruct((rows, LANES), F32),
                   vec_shape, vec_shape),
        grid=(nc,),
        in_specs=[wide, b_spec, c_spec, wide, wide, rows_spec, heads_spec, vec_spec,
                  pl.BlockSpec((1, D_INNER), lambda c: (0, 0)), wide, state_spec],
        out_specs=(pl.BlockSpec((CHUNK, D_XBC), lambda c: (last - c, 0)), heads_spec, vec_spec, vec_spec),
        scratch_shapes=[pltpu.VMEM((N_GROUPS, D_STATE, GROUP_W), F32)],
        operands=[xbc, xbc, xbc, dt_exp, acs_exp, acs_rows, dt, a128, dskexp, dy, states],
        semantics=("arbitrary",), steps=steps)


def _attn_visible(b, heads=1):
    row = jnp.bitwise_and(lax.broadcasted_iota(jnp.int32, (heads * CHUNK, 3 * CHUNK), 0), CHUNK - 1)
    col = lax.broadcasted_iota(jnp.int32, (heads * CHUNK, 3 * CHUNK), 1)
    bb = b + jnp.zeros_like(col)
    meta = (col < CHUNK) & (bb >= 1) & (col >= PAD_ROWS)
    prev = (col >= CHUNK) & (col < 2 * CHUNK) & (bb >= 2) & ((col - CHUNK) > row)
    cur = (col >= 2 * CHUNK) & ((col - 2 * CHUNK) <= row) & ((bb >= 1) | ((col - 2 * CHUNK) >= PAD_ROWS))
    return meta | prev | cur


def _attn_visible4(b):
    return _attn_visible(b, 4)


def _stack_heads(q_ref, sink_ref, kvh, scale):
    lane = lax.broadcasted_iota(jnp.int32, (CHUNK, LANES), 1)
    parts, sinks = [], []
    for pp in range(2):
        pair = kvh * 2 + pp
        qp = q_ref[:, pair * LANES:(pair + 1) * LANES] * scale
        for e in range(2):
            mine = (lane < HEAD_DIM) if e == 0 else (lane >= HEAD_DIM)
            parts.append(jnp.where(mine, qp, 0.0).astype(BF16))
            sinks.append(jnp.full((CHUNK, 1), sink_ref[2 * pair + e], F32))
    return jnp.concatenate(parts, axis=0), jnp.concatenate(sinks, axis=0)


def _attn_operands(q_ref, k0, kp, kc, v0, vp, vc, sink_ref):
    kcat, vcat, q4, sink4 = [], [], [], []
    for kvh in range(N_KV_HEADS):
        ksl = slice(kvh * LANES, (kvh + 1) * LANES)
        kcat.append(jnp.concatenate([k0[:, ksl], kp[:, ksl], kc[:, ksl]], axis=0).astype(BF16))
        vcat.append(jnp.concatenate([v0[:, ksl], vp[:, ksl], vc[:, ksl]], axis=0).astype(BF16))
        stacked, sinks = _stack_heads(q_ref, sink_ref, kvh, ATTN_SCALE)
        q4.append(stacked)
        sink4.append(sinks)
    return kcat, vcat, q4, sink4


def _attn_probs(q4, kcat, visible, sink4):
    heads = range(N_KV_HEADS)
    s = [jnp.where(visible, _dot_nt(q4[h], kcat[h]), NEG_INF) for h in heads]
    m = [jnp.maximum(jnp.max(s[h], axis=-1, keepdims=True), sink4[h]) for h in heads]
    pe = [jnp.exp(s[h] - m[h]) for h in heads]
    pe_sink = [jnp.exp(sink4[h] - m[h]) for h in heads]
    inv = [1.0 / (jnp.sum(pe[h], axis=-1, keepdims=True) + pe_sink[h]) for h in heads]
    return [pe[h] * inv[h] for h in heads], [pe_sink[h] * inv[h] for h in heads]


def _unstack_pairs(stacked, pp):
    lane = lax.broadcasted_iota(jnp.int32, (CHUNK, LANES), 1)
    return jnp.where(lane < HEAD_DIM, stacked[(2 * pp) * CHUNK:(2 * pp + 1) * CHUNK],
                     stacked[(2 * pp + 1) * CHUNK:(2 * pp + 2) * CHUNK])


def _attn_specs(colblock):
    blk = lambda f: pl.BlockSpec((CHUNK, 2 * D_KV), f)
    return [blk(lambda b: (0, colblock)), blk(lambda b: (jnp.maximum(b - 1, 0), colblock)), blk(lambda b: (b, colblock))]


def _attn_fwd(name, q, kv2, sinks, steps=()):
    rows = q.shape[0]

    def body(q_ref, k0, kp, kc, v0, vp, vc, sink_ref, o_ref):
        visible = _attn_visible4(pl.program_id(0))
        kcat, vcat, q4, sink4 = _attn_operands(q_ref, k0, kp, kc, v0, vp, vc, sink_ref)
        pn, _ = _attn_probs(q4, kcat, visible, sink4)
        o4 = [_dot(pn[h].astype(BF16), vcat[h]) for h in range(N_KV_HEADS)]
        for kvh in range(N_KV_HEADS):
            for pp in range(2):
                qsl = slice((kvh * 2 + pp) * LANES, (kvh * 2 + pp + 1) * LANES)
                o_ref[:, qsl] = _unstack_pairs(o4[kvh], pp).astype(BF16)

    return _call(
        body, name=name, out_shape=jax.ShapeDtypeStruct((rows, D_MODEL), BF16), grid=(rows // CHUNK,),
        in_specs=[pl.BlockSpec((CHUNK, D_MODEL), lambda b: (b, 0))] + _attn_specs(0) + _attn_specs(1)
        + [pl.BlockSpec(memory_space=pltpu.SMEM)],
        out_specs=pl.BlockSpec((CHUNK, D_MODEL), lambda b: (b, 0)),
        operands=[q, kv2, kv2, kv2, kv2, kv2, kv2, sinks], semantics=("parallel",), steps=steps)


def _attn_bwd(name, q, kv2, sinks, do, steps=()):
    rows = q.shape[0]

    def body(q_ref, k0, kp, kc, v0, vp, vc, sink_ref, do_ref,
             dq_ref, dkc_ref, dkp_ref, dvc_ref, dvp_ref, dkm_ref, dvm_ref, dsink_ref):
        @pl.when(pl.program_id(0) == 0)
        def _():
            dkm_ref[...] = jnp.zeros_like(dkm_ref)
            dvm_ref[...] = jnp.zeros_like(dvm_ref)
            dsink_ref[...] = jnp.zeros_like(dsink_ref)

        visible = _attn_visible4(pl.program_id(0))
        heads = range(N_KV_HEADS)
        lane1 = lax.broadcasted_iota(jnp.int32, (1, LANES), 1)
        kcat, vcat, q4, sink4 = _attn_operands(q_ref, k0, kp, kc, v0, vp, vc, sink_ref)
        do4 = [_stack_heads(do_ref, sink_ref, h, 1.0)[0] for h in heads]
        pn, psink = _attn_probs(q4, kcat, visible, sink4)
        dp = [_dot_nt(do4[h], vcat[h]) for h in heads]
        delta = [jnp.sum(pn[h] * dp[h], axis=-1, keepdims=True) for h in heads]
        ds16 = [(pn[h] * (dp[h] - delta[h])).astype(BF16) for h in heads]
        dq4 = [_dot(ds16[h], kcat[h]) for h in heads]
        dk_acc = [_dot_tn(ds16[h], q4[h]) for h in heads]
        dv_acc = [_dot_tn(pn[h].astype(BF16), do4[h]) for h in heads]
        dsink = jnp.zeros((1, LANES), F32)
        for kvh in heads:
            ksl = slice(kvh * LANES, (kvh + 1) * LANES)
            sink_terms = psink[kvh] * delta[kvh]
            for j in range(4):
                part = jnp.sum(sink_terms[j * CHUNK:(j + 1) * CHUNK], axis=0, keepdims=True)
                dsink = dsink - jnp.where(lane1 == kvh * 4 + j, part, 0.0)
            for pp in range(2):
                qsl = slice((kvh * 2 + pp) * LANES, (kvh * 2 + pp + 1) * LANES)
                dq_ref[:, qsl] = (_unstack_pairs(dq4[kvh], pp) * ATTN_SCALE).astype(BF16)
            dkm_ref[:, ksl] += dk_acc[kvh][0:CHUNK]
            dvm_ref[:, ksl] += dv_acc[kvh][0:CHUNK]
            dkp_ref[:, ksl] = dk_acc[kvh][CHUNK:2 * CHUNK]
            dvp_ref[:, ksl] = dv_acc[kvh][CHUNK:2 * CHUNK]
            dkc_ref[:, ksl] = dk_acc[kvh][2 * CHUNK:3 * CHUNK]
            dvc_ref[:, ksl] = dv_acc[kvh][2 * CHUNK:3 * CHUNK]
        dsink_ref[...] += dsink

    qspec = pl.BlockSpec((CHUNK, D_MODEL), lambda b: (b, 0))
    kvspec = pl.BlockSpec((CHUNK, 2 * D_KV), lambda b: (b, 0))
    fixed = pl.BlockSpec((CHUNK, 2 * D_KV), lambda b: (0, 0))
    kv_shape = jax.ShapeDtypeStruct((rows, 2 * D_KV), F32)
    meta_shape = jax.ShapeDtypeStruct((CHUNK, 2 * D_KV), F32)
    return _call(
        body, name=name,
        out_shape=(jax.ShapeDtypeStruct((rows, D_MODEL), BF16), kv_shape, kv_shape, kv_shape, kv_shape,
                   meta_shape, meta_shape, jax.ShapeDtypeStruct((1, LANES), F32)),
        grid=(rows // CHUNK,),
        in_specs=[qspec] + _attn_specs(0) + _attn_specs(1) + [pl.BlockSpec(memory_space=pltpu.SMEM), qspec],
        out_specs=(qspec, kvspec, kvspec, kvspec, kvspec, fixed, fixed, pl.BlockSpec((1, LANES), lambda b: (0, 0))),
        operands=[q, kv2, kv2, kv2, kv2, kv2, kv2, sinks, do], semantics=("arbitrary",), steps=steps)


def _kv_grad_combine(name, dk_cur, dk_prev, dk_meta, dv_cur, dv_prev, dv_meta):
    rows = dk_cur.shape[0]
    nb = rows // CHUNK
    width = 2 * D_KV

    def body(kc_ref, kp_ref, km_ref, vc_ref, vp_ref, vm_ref, o_ref):
        jj = pl.program_id(0) + jnp.zeros((CHUNK, 1), jnp.int32)
        for half, (c_ref, p_ref, m_ref) in enumerate(((kc_ref, kp_ref, km_ref), (vc_ref, vp_ref, vm_ref))):
            total = c_ref[...] + jnp.where(jj < nb - 1, p_ref[...], 0.0) + jnp.where(jj == 0, m_ref[...], 0.0)
            o_ref[:, half * width:(half + 1) * width] = total.astype(BF16)

    blk = lambda f: pl.BlockSpec((CHUNK, width), f)
    three = lambda: [blk(lambda j: (j, 0)), blk(lambda j: (jnp.minimum(j + 1, nb - 1), 0)), blk(lambda j: (0, 0))]
    return pl.pallas_call(
        body, name=name, out_shape=jax.ShapeDtypeStruct((rows, 2 * width), BF16), grid=(nb,),
        in_specs=three() + three(), out_specs=pl.BlockSpec((CHUNK, 2 * width), lambda j: (j, 0)),
        compiler_params=_cparams(("parallel",)),
    )(dk_cur, dk_prev, dk_meta, dv_cur, dv_prev, dv_meta)


def _adamw(name, w, g, m, v, steps=()):
    rows, width = w.shape
    tr = rows
    for cand in range(8, rows + 1, 8):
        if rows % cand == 0 and cand * width * 4 <= (1 << 20):
            tr = cand

    def body(*refs):
        _adamw_update(*refs)

    blk = pl.BlockSpec((tr, width), lambda i: (i, 0))
    shp = jax.ShapeDtypeStruct((rows, width), F32)
    return _call(body, name=name, out_shape=(shp, shp, shp), grid=(rows // tr,), in_specs=[blk] * 4,
                 out_specs=(blk,) * 3, operands=[w, g, m, v], semantics=("parallel",), steps=steps)


def _adamw_update(w_ref, g_ref, m_ref, v_ref, d_ref, mo_ref, vo_ref):
    gv = g_ref[...]
    mn = ADAM_B1 * m_ref[...] + (1.0 - ADAM_B1) * gv
    vn = ADAM_B2 * v_ref[...] + (1.0 - ADAM_B2) * (gv * gv)
    m_hat = mn / (1.0 - ADAM_B1 ** ADAM_STEP)
    v_hat = vn / (1.0 - ADAM_B2 ** ADAM_STEP)
    d_ref[...] = -ADAM_LR * (m_hat / (jnp.sqrt(v_hat) + ADAM_EPS) + ADAM_WD * w_ref[...])
    mo_ref[...] = mn
    vo_ref[...] = vn


def _adamw_small(name, ws, gs, ms, vs):
    n = len(ws)

    def body(*refs):
        for i in range(n):
            _adamw_update(*refs[i::n])

    shapes = [jax.ShapeDtypeStruct(a.shape, F32) for a in ws]
    outs = pl.pallas_call(body, name=name, out_shape=shapes * 3, in_specs=[VMEM_SPEC] * (4 * n),
                          out_specs=[VMEM_SPEC] * (3 * n), compiler_params=_cparams())(*ws, *gs, *ms, *vs)
    return outs[:n], outs[n:2 * n], outs[2 * n:]


def _ffn_fwd(tag, h, hn, p, i, plan):
    up_g, up_v, act = _ffn_up_conv(f"ffn{tag}_up", hn, plan.weight("f_w_up", i), p["f_conv_w"][i],
                                   p["f_conv_b"][i:i + 1], steps=plan.steps(f"ffn{tag}_up"))
    pre = _mm(f"ffn{tag}_down", act, plan.weight("f_w_down", i), "nn", steps=plan.steps(f"ffn{tag}_down"))
    return pre, (h, hn, up_g, up_v, act, pre)


def _ffn_bwd(tag, dpre, saved, p, i, plan):
    h, hn, up_g, up_v, act, pre = saved
    plan.grad("f_w_down", i, _mm(f"ffn{tag}_down_dw", act, dpre, "tn", out_dtype=BF16))
    dact = _mm(f"ffn{tag}_down_dx", dpre, plan.weight("f_w_down", i), "nt", steps=plan.steps(f"ffn{tag}_down_dx"))
    gwg, gwv, gbg, gbv, dhn, g_up = _ffn_conv_bwd(
        f"ffn{tag}_conv_bwd", up_g, up_v, dact, p["f_conv_w"][i], p["f_conv_b"][i:i + 1], hn,
        plan.weight("f_w_up", i), steps=plan.steps(f"ffn{tag}_conv_bwd"))
    g_cw, g_cb = jnp.concatenate([gwg, gwv], axis=1), jnp.concatenate([gbg, gbv], axis=1)
    plan.grad("f_w_up", i, g_up)
    return dhn, dict(f_conv_w=g_cw, f_conv_b=g_cb)


def _lanes_pad(a, width=LANES):
    return jnp.pad(a, [(0, 0)] * (a.ndim - 1) + [(0, width - a.shape[-1])])


def _dup_heads(w):
    rows = w.shape[0]
    w = w.reshape(rows, 2 * N_KV_HEADS, 1, HEAD_DIM)
    return jnp.broadcast_to(w, (rows, 2 * N_KV_HEADS, 2, HEAD_DIM)).reshape(rows, 4 * D_KV)


def _undup_heads(g):
    rows = g.shape[0]
    return g.reshape(rows, 2 * N_KV_HEADS, 2, HEAD_DIM).sum(axis=2).reshape(rows, 2 * D_KV)


def _local_step(x2, target, p, plan):
    seq = x2.shape[0]
    rows = seq + CHUNK
    g = {}

    h0 = jnp.concatenate([jnp.zeros((PAD_ROWS, D_MODEL), F32), p["meta_tokens"], x2], axis=0)

    w_in = plan.weight("a_w_in")
    w_dt = jnp.pad(w_in[D_MAIN:], ((0, LANES - SSM_HEADS), (0, 0)))
    dt_bias = _lanes_pad(p["a_dt_bias"])
    a128 = _lanes_pad(-jnp.exp(p["a_a_log"]))
    dskexp = jnp.repeat(p["a_d_skip"].reshape(SSM_HEADS), HEAD_DIM).reshape(1, D_INNER)

    hn0 = _rms_fwd("a_norm", h0, p["a_norm_pre"])
    zx = _mm("a_in_main", hn0, w_in, "nt", k_rows=D_MAIN, steps=plan.steps("a_in_main"))
    dtr = _mm("a_in_dt", hn0, w_dt, "nt")
    xbc = _conv4_fwd("a_conv", zx, p["a_conv_w"], p["a_conv_b"], steps=plan.steps("a_conv"))
    dt = _dt_fwd("a_dt", dtr, dt_bias)
    dt_exp, acs_exp, acs_rows = _ssd_prep("a_ssd_prep", dt, a128, steps=plan.steps("a_ssd_prep"))
    y, states = _ssd_fwd("a_ssd", xbc, dt_exp, acs_exp, acs_rows, dskexp, steps=plan.steps("a_ssd"))
    yn = _gate_fwd("a_gate", y, zx, p["a_gate_norm"], steps=plan.steps("a_gate"))
    mix = _mm("a_out", yn, plan.weight("a_w_out"), "nn", steps=plan.steps("a_out"))
    h1, (hn_f0,) = _resid_norm_fwd("a_resid", h0, mix, p["a_norm_post"], [p["f_norm_pre"][0:1]])

    pre_f0, ffn0 = _ffn_fwd("0", h1, hn_f0, p, 0, plan)
    h2, (hkv, hn2) = _resid_norm_fwd("ffn0_resid", h1, pre_f0, p["f_norm_post"][0:1], [p["kv_norm"], p["b_norm_pre"]])

    w_kv2 = _dup_heads(plan.weight("w_kv"))
    kv2 = _mm("kv_proj", hkv, w_kv2, "nn")
    q = _mm("b_q", hn2, plan.weight("b_w_q"), "nn")
    sinks = p["b_sinks"].reshape(N_Q_HEADS)
    o = _attn_fwd("b_attn", q, kv2, sinks, steps=plan.steps("b_attn"))
    attn = _mm("b_o", o, plan.weight("b_w_o"), "nn", steps=plan.steps("b_o"))
    h3, (hn_f1,) = _resid_norm_fwd("b_resid", h2, attn, p["b_norm_post"], [p["f_norm_pre"][1:2]])

    pre_f1, ffn1 = _ffn_fwd("1", h3, hn_f1, p, 1, plan)
    dh, loss_vec, dpre_f1, g_post1 = _resid_norm_loss("ffn1_resid_loss", h3, pre_f1, p["f_norm_post"][1:2], target)
    loss = loss_vec[0, 0]

    dhn_f1, g1 = _ffn_bwd("1", dpre_f1, ffn1, p, 1, plan)
    dh, g_pre1, dpre, g["b_norm_post"] = _norm_bwd_add("ffn1_norm_bwd", dh, dhn_f1, h3, p["f_norm_pre"][1:2],
                                                        then=(attn, p["b_norm_post"]))
    plan.grad("b_w_o", None, _mm("b_o_dw", o, dpre, "tn", out_dtype=BF16))
    do = _mm("b_o_dx", dpre, plan.weight("b_w_o"), "nt", steps=plan.steps("b_o_dx"))
    dq, dkc, dkp, dvc, dvp, dkm, dvm, dsink = _attn_bwd("b_attn_bwd", q, kv2, sinks, do, steps=plan.steps("b_attn_bwd"))
    g["b_sinks"] = dsink[:, :N_Q_HEADS]
    dhn2 = _mm("b_q_dx", dq, plan.weight("b_w_q"), "nt")
    plan.grad("b_w_q", None, _mm("b_q_dw", hn2, dq, "tn", out_dtype=BF16))
    dh, g["b_norm_pre"] = _norm_bwd_add("b_norm_bwd", dh, dhn2, h2, p["b_norm_pre"])
    dkv2 = _kv_grad_combine("kv_grad", dkc, dkp, dkm, dvc, dvp, dvm)
    dhkv = _mm("kv_proj_dx", dkv2, w_kv2, "nt")
    plan.grad("w_kv", None, _undup_heads(_mm("kv_proj_dw", hkv, dkv2, "tn")))
    dh, g["kv_norm"], dpre_f0, g_post0 = _norm_bwd_add("kv_norm_bwd", dh, dhkv, h2, p["kv_norm"],
                                                       then=(pre_f0, p["f_norm_post"][0:1]))

    dhn_f0, g0 = _ffn_bwd("0", dpre_f0, ffn0, p, 0, plan)
    dh, g_pre0, dpre, g["a_norm_post"] = _norm_bwd_add("ffn0_norm_bwd", dh, dhn_f0, h1, p["f_norm_pre"][0:1],
                                                        then=(mix, p["a_norm_post"]))
    g["f_norm_post"] = jnp.concatenate([g_post0, g_post1], axis=0)
    g["f_norm_pre"] = jnp.concatenate([g_pre0, g_pre1], axis=0)
    g["f_conv_w"] = jnp.stack([g0["f_conv_w"], g1["f_conv_w"]])
    g["f_conv_b"] = jnp.concatenate([g0["f_conv_b"], g1["f_conv_b"]], axis=0)
    plan.grad("a_w_out", None, _mm("a_out_dw", yn, dpre, "tn", out_dtype=BF16))
    dyn = _mm("a_out_dx", dpre, plan.weight("a_w_out"), "nt", steps=plan.steps("a_out_dx"))
    dy, dzx, g["a_gate_norm"] = _gate_bwd("a_gate_bwd", dyn, y, zx, p["a_gate_norm"])
    dxbc, ddt, dalog, ddsk = _ssd_bwd("a_ssd_bwd", xbc, dt_exp, acs_exp, acs_rows, dt, a128, dskexp, dy, states,
                                      steps=plan.steps("a_ssd_bwd"))
    g["a_a_log"] = dalog[:, :SSM_HEADS]
    g["a_d_skip"] = ddsk[:, :SSM_HEADS]
    ddtr, dbias = _dt_bwd("a_dt_bwd", ddt, dtr, dt_bias)
    g["a_dt_bias"] = dbias[:, :SSM_HEADS]
    dzx, g["a_conv_w"], g["a_conv_b"] = _conv4_bwd("a_conv_bwd", zx, dxbc, p["a_conv_w"], p["a_conv_b"], dzx)
    g_in = _mm("a_in_main_dw", dzx, hn0, "tn", out_dtype=BF16, out_rows=D_IN_PROJ, steps=plan.steps("a_in_main_dw"))
    plan.grad("a_w_in", None, _tn_rows_into("a_in_dt_dw", ddtr, hn0, g_in, D_MAIN, SSM_HEADS))
    dhn0 = _mm("a_in_dt_dx", ddtr, w_dt, "nn", steps=plan.steps("a_in_dt_dx"))
    dhn0 = _mm("a_in_main_dx", dzx, w_in, "nn", acc=dhn0, steps=plan.steps("a_in_main_dx"))
    dh_first, grad_x, g["a_norm_pre"] = _norm_bwd_add("a_norm_bwd", dh, dhn0, h0, p["a_norm_pre"],
                                                      split_first_block=True, steps=plan.steps("a_norm_bwd"))
    g["meta_tokens"] = dh_first[PAD_ROWS:]
    return loss, grad_x, g


ANY = pl.BlockSpec(memory_space=pl.ANY)
VMEM_SPEC = pl.BlockSpec(memory_space=pltpu.VMEM)


def _allgather_small(name, shard):
    rows = shard.shape[0]

    def body(s_ref, o_ref, send_sems, recv_sems):
        x, y, c = _place()
        me = 2 * x + y
        o_ref[me] = s_ref[...]
        chips = _other_chips(x, y)
        sends = [pltpu.make_async_remote_copy(s_ref, o_ref.at[me], send_sems.at[j], recv_sems.at[j],
                                              device_id=(cx, cy, c), device_id_type=MESH)
                 for j, (cx, cy) in enumerate(chips)]
        for cp in sends:
            cp.start()
        for j, (cx, cy) in enumerate(chips):
            pltpu.make_async_remote_copy(s_ref, o_ref.at[2 * cx + cy], send_sems.at[j], recv_sems.at[j],
                                         device_id=(cx, cy, c), device_id_type=MESH).wait_recv()
        for cp in sends:
            cp.wait_send()

    return pl.pallas_call(
        body, name=name, out_shape=jax.ShapeDtypeStruct((N_CHIPS, rows, LANES), F32),
        in_specs=[VMEM_SPEC], out_specs=VMEM_SPEC,
        scratch_shapes=[pltpu.SemaphoreType.DMA((3,)), pltpu.SemaphoreType.DMA((3,))],
        compiler_params=pltpu.CompilerParams(vmem_limit_bytes=VMEM_LIMIT),
    )(shard)


def _row_block(rows, width, itemsize, align, budget=2 << 20):
    best = rows
    for cand in range(align, rows + 1, align):
        if rows % cand == 0 and cand * width * itemsize <= budget:
            best = cand
    return best


def _cast_into_slot(name, chip, w, layer=None):
    rows, width = w.shape[-2:]
    tr = _row_block(rows, width, 4, 16)
    if layer is None:
        in_spec = pl.BlockSpec((tr, width), lambda i, chip_ref: (i, 0))
    else:
        in_spec = pl.BlockSpec((None, tr, width), lambda i, chip_ref: (layer, i, 0))

    def body(chip_ref, w_ref, o_ref):
        o_ref[...] = w_ref[...].astype(BF16)

    return pl.pallas_call(
        body, name=name, out_shape=jax.ShapeDtypeStruct((N_CHIPS, rows, width), BF16),
        grid_spec=pltpu.PrefetchScalarGridSpec(
            num_scalar_prefetch=1, grid=(rows // tr,), in_specs=[in_spec],
            out_specs=pl.BlockSpec((None, tr, width), lambda i, chip_ref: (chip_ref[0], i, 0))),
        compiler_params=_cparams(("parallel",)),
    )(chip, w)


def _allreduce_small(name, vec):
    rows = -(-vec.shape[0] // (2 * SUBLANES)) * (2 * SUBLANES)
    hr = rows // 2
    padded = jnp.pad(vec, ((0, rows - vec.shape[0]), (0, 0)))

    def body(v_ref, o_ref, theirs, pair, by_chip, send_sems, recv_sems):
        x, y, c = _place()
        me = 2 * x + y
        sibling = (x, y, 1 - c)
        mine = pl.ds(pl.multiple_of(c * hr, SUBLANES), hr)
        other = pl.ds(pl.multiple_of((1 - c) * hr, SUBLANES), hr)

        swap = _remote(v_ref, theirs, send_sems, recv_sems, 0, sibling)
        swap.start()
        swap.wait()
        south = (c + jnp.zeros((1, 1), jnp.int32)) == 0
        pair[...] = jnp.where(south, v_ref[...], theirs[...]) + jnp.where(south, theirs[...], v_ref[...])

        by_chip[me] = pair[mine, :]
        sends = [_remote(by_chip.at[me], by_chip.at[me], send_sems, recv_sems, 1 + j, (cx, cy, c))
                 for j, (cx, cy) in enumerate(_other_chips(x, y))]
        for cp in sends:
            cp.start()
        for j, (cx, cy) in enumerate(_other_chips(x, y)):
            _remote(by_chip.at[me], by_chip.at[2 * cx + cy], send_sems, recv_sems, 1 + j, (cx, cy, c)).wait_recv()
        for cp in sends:
            cp.wait_send()
        total = by_chip[0]
        for s in range(1, N_CHIPS):
            total = total + by_chip[s]

        o_ref[mine, :] = total
        back = _remote(o_ref.at[mine], o_ref.at[mine], send_sems, recv_sems, 4, sibling)
        back.start()
        _remote(o_ref.at[other], o_ref.at[other], send_sems, recv_sems, 4, sibling).wait_recv()
        back.wait_send()

    out = pl.pallas_call(
        body, name=name, out_shape=jax.ShapeDtypeStruct((rows, LANES), F32),
        in_specs=[VMEM_SPEC], out_specs=VMEM_SPEC,
        scratch_shapes=[pltpu.VMEM((rows, LANES), F32), pltpu.VMEM((rows, LANES), F32),
                        pltpu.VMEM((N_CHIPS, hr, LANES), F32), pltpu.SemaphoreType.DMA((5,)),
                        pltpu.SemaphoreType.DMA((5,))],
        compiler_params=pltpu.CompilerParams(vmem_limit_bytes=VMEM_LIMIT),
    )(padded)
    return out[:vec.shape[0]]


def _rs_pair_add(name, place, grads, partner, split="rows"):
    _, half_rows, width = partner.shape
    tr = _row_block(half_rows, width, 2, 16)
    nb = half_rows // tr
    if split == "rows":
        mine = pl.BlockSpec((None, tr, width), lambda s, i, pr: (s, pr[1] * nb + i, 0))
    else:
        mine = pl.BlockSpec((None, tr, width), lambda s, i, pr: (s, i, pr[1]))

    def body(place_ref, g_ref, p_ref, o_ref):
        o_ref[...] = (g_ref[...].astype(F32) + p_ref[...].astype(F32)).astype(BF16)

    return pl.pallas_call(
        body, name=name, out_shape=jax.ShapeDtypeStruct(partner.shape, BF16),
        grid_spec=pltpu.PrefetchScalarGridSpec(
            num_scalar_prefetch=1, grid=(N_CHIPS, nb),
            in_specs=[mine, pl.BlockSpec((None, tr, width), lambda s, i, pr: (s, i, 0))],
            out_specs=pl.BlockSpec((None, tr, width), lambda s, i, pr: (s, i, 0))),
        compiler_params=_cparams(("parallel", "parallel")),
    )(place, grads, partner)


def _rs_chip_add(name, place, mine, others, split="rows"):
    _, half_rows, width = mine.shape
    tr = _row_block(half_rows, width, 4, 16, budget=1 << 20)
    nb = half_rows // tr
    if split == "rows":
        out_shape, out_spec = (2 * half_rows, width), pl.BlockSpec((tr, width), lambda i, pr: (pr[1] * nb + i, 0))
    else:
        out_shape, out_spec = (half_rows, 2 * width), pl.BlockSpec((tr, width), lambda i, pr: (i, pr[1]))

    def body(place_ref, q_ref, r_ref, o_ref):
        acc = q_ref[...].astype(F32)
        for j in range(3):
            acc = acc + r_ref[j].astype(F32)
        o_ref[...] = acc

    return pl.pallas_call(
        body, name=name, out_shape=jax.ShapeDtypeStruct(out_shape, F32),
        grid_spec=pltpu.PrefetchScalarGridSpec(
            num_scalar_prefetch=1, grid=(nb,),
            in_specs=[pl.BlockSpec((None, tr, width), lambda i, pr: (pr[0], i, 0)),
                      pl.BlockSpec((3, tr, width), lambda i, pr: (0, i, 0))],
            out_specs=out_spec),
        compiler_params=_cparams(("parallel",)),
    )(place, mine, others)


WEIGHTS = ["meta_tokens", "a_norm_pre", "a_w_in", "a_conv_w", "a_conv_b", "a_dt_bias", "a_a_log", "a_d_skip",
           "a_gate_norm", "a_w_out", "a_norm_post", "kv_norm", "w_kv", "b_norm_pre", "b_w_q", "b_sinks", "b_w_o",
           "b_norm_post", "f_norm_pre", "f_w_up", "f_conv_w", "f_conv_b", "f_w_down", "f_norm_post"]
FULL_SHAPE = {
    "meta_tokens": (16, 1024), "a_norm_pre": (1, 1024), "a_w_in": (1, 1024, 5152), "a_conv_w": (1, 4, 3072),
    "a_conv_b": (1, 3072), "a_dt_bias": (1, 32), "a_a_log": (1, 32), "a_d_skip": (1, 32), "a_gate_norm": (1, 2048),
    "a_w_out": (1, 2048, 1024), "a_norm_post": (1, 1024), "kv_norm": (1024,), "w_kv": (1024, 512),
    "b_norm_pre": (1, 1024), "b_w_q": (1, 1024, 1024), "b_sinks": (1, 16), "b_w_o": (1, 1024, 1024),
    "b_norm_post": (1, 1024), "f_norm_pre": (2, 1024), "f_w_up": (2, 1024, 5632), "f_conv_w": (2, 3, 5632),
    "f_conv_b": (2, 5632), "f_w_down": (2, 2816, 1024), "f_norm_post": (2, 1024),
}
SHARD_AXIS = {
    "meta_tokens": 1, "a_norm_pre": 1, "a_w_in": 2, "a_conv_w": 2, "a_conv_b": 1, "a_dt_bias": None, "a_a_log": None,
    "a_d_skip": None, "a_gate_norm": 1, "a_w_out": 1, "a_norm_post": 1, "kv_norm": None, "w_kv": 0, "b_norm_pre": None,
    "b_w_q": 1, "b_sinks": None, "b_w_o": 1, "b_norm_post": None, "f_norm_pre": None, "f_w_up": 2, "f_conv_w": 2,
    "f_conv_b": None, "f_w_down": 1, "f_norm_post": None,
}
BIG = ["a_w_in", "a_w_out", "w_kv", "b_w_q", "b_w_o", "f_w_up", "f_w_down"]
SMALL = [n for n in WEIGHTS if n not in BIG]
SMALL_SHARDED = [n for n in SMALL if SHARD_AXIS[n] is not None]


def _shard_shape(name):
    shape = list(FULL_SHAPE[name])
    if SHARD_AXIS[name] is not None:
        shape[SHARD_AXIS[name]] //= N_CHIPS
    return tuple(shape)


def _numel(shape):
    return int(math.prod(shape))


SUBLANES = 8


def _packed_rows(shape):
    rows = -(-_numel(shape) // LANES)
    return -(-rows // SUBLANES) * SUBLANES


def _pack(arrays):
    parts = []
    for a in arrays:
        size, rows = _numel(a.shape), _packed_rows(a.shape)
        if size % LANES == 0:
            part = jnp.pad(a.reshape(size // LANES, LANES), ((0, rows - size // LANES), (0, 0)))
        else:
            part = jnp.pad(a.reshape(-1), (0, rows * LANES - size)).reshape(rows, LANES)
        parts.append(part)
    return jnp.concatenate(parts, axis=0)


def _unpack(packed, names, shape_of):
    out, off = {}, 0
    lead = packed.shape[:-2]
    for n in names:
        shape = tuple(shape_of(n))
        size, rows = _numel(shape), _packed_rows(shape)
        part = packed[..., off:off + rows, :]
        if size % LANES == 0:
            out[n] = part[..., :size // LANES, :].reshape(lead + shape)
        else:
            out[n] = part.reshape(lead + (rows * LANES,))[..., :size].reshape(lead + shape)
        off += rows
    return out


def _split_chips(name, full):
    ax = SHARD_AXIS[name]
    shape = full.shape
    cut = shape[:ax] + (N_CHIPS, shape[ax] // N_CHIPS) + shape[ax + 1:]
    return jnp.moveaxis(full.reshape(cut), ax, 0)


def _join_chips(name, stacked):
    ax = SHARD_AXIS[name]
    moved = jnp.moveaxis(stacked, 0, ax)
    shape = moved.shape
    return moved.reshape(shape[:ax] + (shape[ax] * shape[ax + 1],) + shape[ax + 2:])


def _as2d(a):
    return a.reshape(-1, a.shape[-1])


BUFFERS = [("a_w_in", "a_w_in", None), ("a_w_out", "a_w_out", None), ("w_kv", "w_kv", None),
           ("b_w_q", "b_w_q", None), ("b_w_o", "b_w_o", None), ("f_w_up0", "f_w_up", 0), ("f_w_up1", "f_w_up", 1),
           ("f_w_down0", "f_w_down", 0), ("f_w_down1", "f_w_down", 1)]


TRANSPOSED = ("a_w_in",)
SPLIT = {"a_w_in": "cols"}


def _local_shard(arrays, weight, layer):
    if weight in TRANSPOSED:
        return arrays[weight][0].T
    return _as2d(arrays[weight]) if layer is None else arrays[weight]


def _weight_from_gathered(weight, buf):
    if weight == "f_w_up":
        return buf
    return buf.reshape(N_CHIPS * buf.shape[1], buf.shape[2])


def _gathered_from_grad(weight, g):
    if weight == "f_w_up":
        return g
    return g.reshape(N_CHIPS, g.shape[0] // N_CHIPS, g.shape[1]).astype(BF16)


GATHER_SCHEDULE = {
    "a_in_main": [("ici", ["a_w_out"])],
    "a_conv": [("d2d", ["a_w_out"]), ("ici", ["f_w_down0"])],
    "a_ssd_prep": [("d2d", ["f_w_down0"]), ("ici_near", ["f_w_up0"])],
    "a_ssd": [("ici_far", ["f_w_up0"])],
    "a_gate": [("d2d", ["f_w_up0"]), ("ici", ["w_kv", "b_w_q", "b_w_o"])],
    "ffn0_up": [("d2d", ["w_kv", "b_w_q", "b_w_o"]), ("ici", ["f_w_down1"])],
    "ffn0_down": [("d2d", ["f_w_down1"])],
    "b_attn": [("ici", ["f_w_up1"])],
    "b_o": [("d2d", ["f_w_up1"])],
}
REDUCE_SCHEDULE = {
    "b_attn_bwd": [("all", ["f_w_down1", "f_w_up1", "b_w_o"])],
    "ffn0_conv_bwd": [("all", ["b_w_q", "w_kv", "f_w_down0"])],
    "a_ssd_bwd": [("all", ["f_w_up0", "a_w_out"])],
    "a_in_main_dx": [("near", ["a_w_in"])],
    "a_norm_bwd": [("far", ["a_w_in"])],
}
REDUCE_LAST = ("a_w_in",)
ICI_PEERS = {"ici": ALL_PEERS, "ici_near": NEAR_PEERS, "ici_far": FAR_PEERS,
             "all": ALL_PEERS, "near": NEAR_PEERS, "far": FAR_PEERS}
PAIR_SCHEDULE = {
    "b_o_dx": ["f_w_down1", "f_w_up1", "b_w_o"],
    "ffn0_down_dx": ["b_w_q", "w_kv", "f_w_down0"],
    "a_out_dx": ["f_w_up0", "a_w_out"],
    "a_in_dt_dx": ["a_w_in"],
}
SWAP_SCHEDULE = {"a_in_main_dw": ["f_w_down1", "f_w_up1", "b_w_o", "b_w_q", "w_kv", "f_w_down0", "f_w_up0", "a_w_out"]}


def _buffer_of(weight, layer):
    return weight if layer is None else f"{weight}{layer}"


class _Pipeline:
    def __init__(self, place, slots):
        self.place = place
        self.slots = dict(slots)
        self.running = []
        self.grads = {}
        self.theirs = {}
        self.partials = {}
        self.peers = {}
        self.reduced = {}

    def _collect(self):
        for step, buffers, table in self.running:
            table.update(zip(buffers, step.results))
        self.running = []

    @staticmethod
    def _splits(buffers):
        return [SPLIT.get(b, "rows") for b in buffers]

    def gather_now(self, name, buffers):
        step = _step_gather_full([self.slots[b] for b in buffers], self._splits(buffers))
        _run_steps(name, [step])
        self.slots.update(zip(buffers, step.results))

    def weight(self, name, layer=None):
        self._collect()
        return _weight_from_gathered(name, self.slots[_buffer_of(name, layer)])

    def grad(self, name, layer, g):
        self.grads[_buffer_of(name, layer)] = _gathered_from_grad(name, g)

    def steps(self, kernel):
        self._collect()
        steps = []
        for phase, buffers in GATHER_SCHEDULE.get(kernel, []):
            bufs, splits = [self.slots[b] for b in buffers], self._splits(buffers)
            step = (_step_gather_d2d(bufs, splits) if phase == "d2d"
                    else _step_gather_ici(bufs, splits, ICI_PEERS[phase]))
            self.running.append((step, buffers, self.slots))
            steps.append(step)
        buffers = PAIR_SCHEDULE.get(kernel)
        if buffers:
            step = _step_pair_exchange([self.grads[b] for b in buffers], self._splits(buffers))
            self.running.append((step, buffers, self.theirs))
            steps.append(step)
        for part, buffers in REDUCE_SCHEDULE.get(kernel, []):
            for b in buffers:
                if b not in self.partials:
                    self.partials[b] = _rs_pair_add("reduce_pair_add_" + b, self.place, self.grads[b], self.theirs[b],
                                                    SPLIT.get(b, "rows"))
            started = [self.peers[b] for b in buffers] if all(b in self.peers for b in buffers) else None
            step = _step_chip_exchange([self.partials[b] for b in buffers], ICI_PEERS[part], into=started)
            self.running.append((step, buffers, self.peers))
            steps.append(step)
        buffers = SWAP_SCHEDULE.get(kernel)
        if buffers:
            step = self._swap_step(buffers)
            self.running.append((step, buffers, self.reduced))
            steps.append(step)
        return steps

    def _swap_step(self, buffers):
        halves = [_rs_chip_add("reduce_chip_add_" + b, self.place, self.partials[b], self.peers[b], SPLIT.get(b, "rows"))
                  for b in buffers]
        return _step_pair_gather(halves, self._splits(buffers))

    def shard(self, buffer):
        self._collect()
        return self.reduced[buffer]

    def finish(self):
        self._collect()
        rest = [b for b, _, _ in BUFFERS if b not in self.reduced]
        step = self._swap_step(rest)
        _run_steps("reduce_pair_gather", [step])
        self.reduced.update(zip(rest, step.results))


def kernel(x, meta_tokens, a_norm_pre, a_w_in, a_conv_w, a_conv_b, a_dt_bias, a_a_log, a_d_skip, a_gate_norm, a_w_out, a_norm_post, kv_norm, w_kv, b_norm_pre, b_w_q, b_sinks, b_w_o, b_norm_post, f_norm_pre, f_w_up, f_conv_w, f_conv_b, f_w_down, f_norm_post, loss_target, m_meta_tokens, m_a_norm_pre, m_a_w_in, m_a_conv_w, m_a_conv_b, m_a_dt_bias, m_a_a_log, m_a_d_skip, m_a_gate_norm, m_a_w_out, m_a_norm_post, m_kv_norm, m_w_kv, m_b_norm_pre, m_b_w_q, m_b_sinks, m_b_w_o, m_b_norm_post, m_f_norm_pre, m_f_w_up, m_f_conv_w, m_f_conv_b, m_f_w_down, m_f_norm_post, v_meta_tokens, v_a_norm_pre, v_a_w_in, v_a_conv_w, v_a_conv_b, v_a_dt_bias, v_a_a_log, v_a_d_skip, v_a_gate_norm, v_a_w_out, v_a_norm_post, v_kv_norm, v_w_kv, v_b_norm_pre, v_b_w_q, v_b_sinks, v_b_w_o, v_b_norm_post, v_f_norm_pre, v_f_w_up, v_f_conv_w, v_f_conv_b, v_f_w_down, v_f_norm_post):
    given = dict(locals())
    w = {n: given[n] for n in WEIGHTS}
    mom = {n: given["m_" + n] for n in WEIGHTS}
    var = {n: given["v_" + n] for n in WEIGHTS}
    chip = 2 * lax.axis_index("x") + lax.axis_index("y")
    core = lax.axis_index("c")
    place = jnp.stack([chip, core]).astype(jnp.int32)

    small_all = _allgather_small("gather_small", _pack([w[n] for n in SMALL_SHARDED]))
    small_parts = _unpack(small_all, SMALL_SHARDED, _shard_shape)
    slots = {b: _cast_into_slot("cast_" + b, place, _local_shard(w, wn, layer), layer) for b, wn, layer in BUFFERS}
    pipeline = _Pipeline(place, slots)
    pipeline.gather_now("gather_first", ["a_w_in"])
    p = {}
    for n in SMALL:
        p[n] = _join_chips(n, small_parts[n]) if n in SMALL_SHARDED else w[n]
    p["a_conv_w"] = p["a_conv_w"][0]
    p["kv_norm"] = p["kv_norm"].reshape(1, D_MODEL)

    loss_local, grad_x, g = _local_step(x[0], loss_target[0], p, pipeline)

    small_sum = _allreduce_small("reduce_small", _pack([g[n].reshape(FULL_SHAPE[n]) for n in SMALL]
                                                       + [loss_local.reshape(1, 1)]))
    small_red = _unpack(small_sum, SMALL + ["loss"], lambda n: (1, 1) if n == "loss" else FULL_SHAPE[n])
    loss = small_red["loss"][0, 0]
    grads = {}
    for n in SMALL:
        if SHARD_AXIS[n] is None:
            grads[n] = small_red[n]
        else:
            grads[n] = lax.dynamic_index_in_dim(_split_chips(n, small_red[n]), chip, 0, keepdims=False)

    delta, new_m, new_v = {}, {}, {}
    for n in sorted(BIG, key=lambda name: name in REDUCE_LAST):
        shape = _shard_shape(n)
        if n in REDUCE_LAST:
            pipeline.finish()
        if n in TRANSPOSED:
            g2d = pipeline.shard(n)
            w2d, m2d, v2d = (arrays[n][0].T for arrays in (w, mom, var))
            back = lambda a: a.T.reshape(shape)
        else:
            g2d = (jnp.concatenate([pipeline.shard(n + "0"), pipeline.shard(n + "1")], axis=0)
                   if n in ("f_w_up", "f_w_down") else pipeline.shard(n))
            w2d, m2d, v2d = (_as2d(arrays[n]) for arrays in (w, mom, var))
            back = lambda a: a.reshape(shape)
        d, m2, v2 = _adamw("adamw_" + n, w2d, g2d, m2d, v2d, steps=pipeline.steps("adamw_" + n))
        grads[n], delta[n], new_m[n], new_v[n] = back(g2d), back(d), back(m2), back(v2)
    at_least_2d = lambda n: (1,) * (2 - len(_shard_shape(n))) + _shard_shape(n)
    outs = _adamw_small("adamw_small", *[[src[n].reshape(at_least_2d(n)) for n in SMALL] for src in (w, grads, mom, var)])
    for dst, arrays in zip((delta, new_m, new_v), outs):
        dst.update({n: a.reshape(_shard_shape(n)) for n, a in zip(SMALL, arrays)})

    return (loss, grad_x[None], *[grads[n].reshape(_shard_shape(n)) for n in WEIGHTS],
            *[delta[n] for n in WEIGHTS], *[new_m[n] for n in WEIGHTS], *[new_v[n] for n in WEIGHTS])
```

```python
import functools
import math

import jax
import jax.numpy as jnp
from jax import lax
from jax.experimental import pallas as pl
from jax.experimental.pallas import tpu as pltpu

F32, BF16 = jnp.float32, jnp.bfloat16
MESH = pl.DeviceIdType.MESH

D_MODEL = 1024
N_META = 16
CHUNK = 128
PAD_ROWS = CHUNK - N_META
D_INNER = 2048
D_STATE = 128
N_GROUPS = 4
HEADS_PER_GROUP = 8
SSM_HEADS = 32
HEAD_DIM = 64
D_BC = N_GROUPS * D_STATE
D_XBC = D_INNER + 2 * D_BC
D_MAIN = D_INNER + D_XBC
D_IN_PROJ = D_MAIN + SSM_HEADS
GROUP_W = HEADS_PER_GROUP * HEAD_DIM
SSM_CONV = 4
D_FF = 2816
FFN_CONV = 3
N_Q_HEADS = 16
N_KV_HEADS = 4
D_KV = 256
ATTN_SCALE = 1.0 / math.sqrt(HEAD_DIM)
RMS_EPS = 1e-6
NEG_INF = -1e30
LANES = 128
VMEM_LIMIT = 51 * 1024 * 1024

ADAM_LR, ADAM_B1, ADAM_B2, ADAM_EPS, ADAM_WD, ADAM_STEP = 0.001, 0.9, 0.999, 1e-08, 0.01, 10

N_CHIPS = 4


def _cparams(sem=None):
    return pltpu.CompilerParams(dimension_semantics=sem, vmem_limit_bytes=VMEM_LIMIT)


def _tile(n, cands=(512, 256, 128)):
    for t in cands:
        if n % t == 0:
            return t
    return n


def _row_tile(rows, width):
    for t in (544, 272):
        if rows % t == 0 and t * width * 4 <= (3 << 20):
            return t
    return 128


def _rows_mask(i, tm):
    rows = i * tm + lax.broadcasted_iota(jnp.int32, (tm, 1), 0)
    return rows >= PAD_ROWS


def _dot(a, b):
    return jnp.dot(a, b, preferred_element_type=F32)


def _dot_nt(a, b):
    return lax.dot_general(a, b, (((1,), (1,)), ((), ())), preferred_element_type=F32)


def _dot_tn(a, b):
    return lax.dot_general(a, b, (((0,), (0,)), ((), ())), preferred_element_type=F32)


def _sigmoid(x):
    return 1.0 / (1.0 + jnp.exp(-x))


def _place():
    return lax.axis_index("x"), lax.axis_index("y"), lax.axis_index("c")


def _other_chips(x, y):
    return [(1 - x, y), (x, 1 - y), (1 - x, 1 - y)]


class _Step:
    def __init__(self, ins, outs, aliases, n_sems, start, finish):
        self.ins, self.outs, self.aliases, self.n_sems = list(ins), list(outs), dict(aliases), n_sems
        self.start, self.finish = start, finish
        self.results = None


def _like(a):
    return jax.ShapeDtypeStruct(a.shape, a.dtype)


def _remote(src, dst, send_sems, recv_sems, k, device):
    return pltpu.make_async_remote_copy(src, dst, send_sems.at[k], recv_sems.at[k], device_id=device, device_id_type=MESH)


def _half(ref, split, which, lead=()):
    if split == "rows":
        hr = ref.shape[-2] // 2
        return ref.at[lead + (pl.ds(which * hr, hr),)]
    hc = ref.shape[-1] // 2
    return ref.at[lead + (slice(None), pl.ds(which * hc, hc))]


def _splits(bufs, splits):
    return list(splits) if splits is not None else ["rows"] * len(bufs)


ALL_PEERS = (0, 1, 2)
NEAR_PEERS = (0, 1)
FAR_PEERS = (2,)


def _step_gather_ici(bufs, splits=None, peers=ALL_PEERS):
    splits = _splits(bufs, splits)

    def copies(outs, send_sems, recv_sems, received):
        x, y, c = _place()
        me = 2 * x + y
        for k, o in enumerate(outs):
            for j, (cx, cy) in enumerate(_other_chips(x, y)):
                if j in peers:
                    part = _half(o, splits[k], c, (2 * cx + cy if received else me,))
                    yield _remote(part, part, send_sems, recv_sems, 3 * k + j, (cx, cy, c))

    def start(ins, outs, send_sems, recv_sems):
        for cp in copies(outs, send_sems, recv_sems, False):
            cp.start()

    def finish(ins, outs, send_sems, recv_sems):
        for cp in copies(outs, send_sems, recv_sems, True):
            cp.wait_recv()
        for cp in copies(outs, send_sems, recv_sems, False):
            cp.wait_send()

    return _Step(bufs, [_like(b) for b in bufs], {k: k for k in range(len(bufs))}, 3 * len(bufs), start, finish)


def _step_gather_d2d(bufs, splits=None):
    splits = _splits(bufs, splits)

    def copies(outs, send_sems, recv_sems, received):
        x, y, c = _place()
        for k, o in enumerate(outs):
            for j, (cx, cy) in enumerate(_other_chips(x, y)):
                part = _half(o, splits[k], 1 - c if received else c, (2 * cx + cy,))
                yield _remote(part, part, send_sems, recv_sems, 3 * k + j, (x, y, 1 - c))

    def start(ins, outs, send_sems, recv_sems):
        for cp in copies(outs, send_sems, recv_sems, False):
            cp.start()

    def finish(ins, outs, send_sems, recv_sems):
        for cp in copies(outs, send_sems, recv_sems, True):
            cp.wait_recv()
        for cp in copies(outs, send_sems, recv_sems, False):
            cp.wait_send()

    return _Step(bufs, [_like(b) for b in bufs], {k: k for k in range(len(bufs))}, 3 * len(bufs), start, finish)


def _step_gather_full(bufs, splits=None):
    n = len(bufs)
    splits = _splits(bufs, splits)

    def ici(outs, send_sems, recv_sems, received):
        x, y, c = _place()
        me = 2 * x + y
        for k, o in enumerate(outs):
            for j, (cx, cy) in enumerate(_other_chips(x, y)):
                part = _half(o, splits[k], c, (2 * cx + cy if received else me,))
                yield _remote(part, part, send_sems, recv_sems, 3 * k + j, (cx, cy, c))

    def d2d(outs, send_sems, recv_sems, received):
        x, y, c = _place()
        for k, o in enumerate(outs):
            for j, (cx, cy) in enumerate(_other_chips(x, y)):
                part = _half(o, splits[k], 1 - c if received else c, (2 * cx + cy,))
                yield _remote(part, part, send_sems, recv_sems, 3 * n + 3 * k + j, (x, y, 1 - c))

    def start(ins, outs, send_sems, recv_sems):
        for cp in ici(outs, send_sems, recv_sems, False):
            cp.start()

    def finish(ins, outs, send_sems, recv_sems):
        for arrived, onward in zip(ici(outs, send_sems, recv_sems, True), d2d(outs, send_sems, recv_sems, False)):
            arrived.wait_recv()
            onward.start()
        for cp in d2d(outs, send_sems, recv_sems, True):
            cp.wait_recv()
        for cp in ici(outs, send_sems, recv_sems, False):
            cp.wait_send()
        for cp in d2d(outs, send_sems, recv_sems, False):
            cp.wait_send()

    return _Step(bufs, [_like(b) for b in bufs], {k: k for k in range(n)}, 6 * n, start, finish)


def _half_shape(shape, split):
    return shape[:-2] + ((shape[-2] // 2, shape[-1]) if split == "rows" else (shape[-2], shape[-1] // 2))


def _step_pair_exchange(grads, splits=None):
    splits = _splits(grads, splits)

    def copies(ins, outs, send_sems, recv_sems):
        x, y, c = _place()
        for k, (g, o) in enumerate(zip(ins, outs)):
            yield _remote(_half(g, splits[k], 1 - c, (slice(None),)), o, send_sems, recv_sems, k, (x, y, 1 - c))

    def start(ins, outs, send_sems, recv_sems):
        for cp in copies(ins, outs, send_sems, recv_sems):
            cp.start()

    def finish(ins, outs, send_sems, recv_sems):
        for cp in copies(ins, outs, send_sems, recv_sems):
            cp.wait()

    outs = [jax.ShapeDtypeStruct(_half_shape(g.shape, s), g.dtype) for g, s in zip(grads, splits)]
    return _Step(grads, outs, {}, len(grads), start, finish)


def _step_chip_exchange(partials, peers=ALL_PEERS, into=None):
    n = len(partials)

    def copies(ins, outs, send_sems, recv_sems):
        x, y, c = _place()
        for k, (q, o) in enumerate(zip(ins[:n], outs)):
            for j, (cx, cy) in enumerate(_other_chips(x, y)):
                if j in peers:
                    yield _remote(q.at[2 * cx + cy], o.at[j], send_sems, recv_sems, 3 * k + j, (cx, cy, c))

    def start(ins, outs, send_sems, recv_sems):
        for cp in copies(ins, outs, send_sems, recv_sems):
            cp.start()

    def finish(ins, outs, send_sems, recv_sems):
        for cp in copies(ins, outs, send_sems, recv_sems):
            cp.wait()

    outs = [jax.ShapeDtypeStruct((3,) + q.shape[1:], q.dtype) for q in partials]
    if into is None:
        return _Step(partials, outs, {}, 3 * n, start, finish)
    return _Step(list(partials) + list(into), outs, {n + k: k for k in range(n)}, 3 * n, start, finish)


def _step_pair_gather(shards, splits=None):
    splits = _splits(shards, splits)

    def copies(outs, send_sems, recv_sems, received):
        x, y, c = _place()
        for k, o in enumerate(outs):
            part = _half(o, splits[k], 1 - c if received else c)
            yield _remote(part, part, send_sems, recv_sems, k, (x, y, 1 - c))

    def start(ins, outs, send_sems, recv_sems):
        for cp in copies(outs, send_sems, recv_sems, False):
            cp.start()

    def finish(ins, outs, send_sems, recv_sems):
        for cp in copies(outs, send_sems, recv_sems, True):
            cp.wait_recv()
        for cp in copies(outs, send_sems, recv_sems, False):
            cp.wait_send()

    return _Step(shards, [_like(s) for s in shards], {k: k for k in range(len(shards))}, len(shards), start, finish)


def _call(body, *, name, out_shape, grid, in_specs, out_specs, operands, scratch_shapes=(), semantics=None, steps=()):
    single = not isinstance(out_shape, (tuple, list))
    out_shapes = [out_shape] if single else list(out_shape)
    out_spec_list = [out_specs] if single else list(out_specs)
    steps = list(steps)
    if not steps:
        res = pl.pallas_call(body, name=name, out_shape=out_shapes, grid=grid, in_specs=list(in_specs),
                             out_specs=out_spec_list, scratch_shapes=list(scratch_shapes),
                             compiler_params=_cparams(semantics))(*operands)
        return res[0] if single else res
    n_in, n_out, n_scr = len(operands), len(out_shapes), len(scratch_shapes)
    x_in = [a for s in steps for a in s.ins]
    x_out = [o for s in steps for o in s.outs]
    aliases, in_off, out_off = {}, 0, 0
    for s in steps:
        for i, o in s.aliases.items():
            aliases[n_in + in_off + i] = n_out + out_off + o
        in_off += len(s.ins)
        out_off += len(s.outs)
    sems = []
    for s in steps:
        sems += [pltpu.SemaphoreType.DMA((s.n_sems,)), pltpu.SemaphoreType.DMA((s.n_sems,))]
    any_spec = pl.BlockSpec(memory_space=pl.ANY)

    def carried(*refs):
        pos = 0
        ins = refs[pos:pos + n_in]; pos += n_in
        xi = refs[pos:pos + len(x_in)]; pos += len(x_in)
        outs = refs[pos:pos + n_out]; pos += n_out
        xo = refs[pos:pos + len(x_out)]; pos += len(x_out)
        scr = refs[pos:pos + n_scr]; pos += n_scr
        sem_refs = refs[pos:]

        def each(action):
            i0 = o0 = 0
            for k, s in enumerate(steps):
                getattr(s, action)(xi[i0:i0 + len(s.ins)], xo[o0:o0 + len(s.outs)], sem_refs[2 * k], sem_refs[2 * k + 1])
                i0 += len(s.ins)
                o0 += len(s.outs)

        if grid:
            first = functools.reduce(jnp.logical_and, [pl.program_id(d) == 0 for d in range(len(grid))])
            last = functools.reduce(jnp.logical_and, [pl.program_id(d) == grid[d] - 1 for d in range(len(grid))])
            pl.when(first)(lambda: each("start"))
            body(*ins, *outs, *scr)
            pl.when(last)(lambda: each("finish"))
        else:
            each("start")
            body(*ins, *outs, *scr)
            each("finish")

    res = pl.pallas_call(
        carried, name=name, out_shape=out_shapes + x_out, grid=grid,
        in_specs=list(in_specs) + [any_spec] * len(x_in), out_specs=out_spec_list + [any_spec] * len(x_out),
        scratch_shapes=list(scratch_shapes) + sems, input_output_aliases=aliases,
        compiler_params=_cparams(None if semantics is None else ("arbitrary",) * len(grid)),
    )(*operands, *x_in)
    o0 = n_out
    for s in steps:
        s.results = list(res[o0:o0 + len(s.outs)])
        o0 += len(s.outs)
    return res[0] if single else tuple(res[:n_out])


def _run_steps(name, steps):
    _call(lambda: None, name=name, out_shape=[], grid=(), in_specs=[], out_specs=[], operands=[], steps=steps)
    return [s.results for s in steps]


def _mm(name, a, b, mode, out_dtype=F32, acc=None, b_colblock=0, k_rows=None, out_rows=None, steps=()):
    resident_bytes = 8 << 20
    if mode == "nn":
        m, k = a.shape
        n = b.shape[1]
        tm = m
        while tm * k * 2 > resident_bytes and tm % 32 == 0:
            tm //= 2
        tn = _tile(n)
        grid = (m // tm, n // tn)
        in_specs = [pl.BlockSpec((tm, k), lambda i, j: (i, 0)), pl.BlockSpec((k, tn), lambda i, j: (0, j))]
        out_shape, out_block = (m, n), (tm, tn)
    elif mode == "nt":
        m, n = a.shape
        k = k_rows or b.shape[0]
        tm = m
        while tm * n * 2 > resident_bytes and tm % 32 == 0:
            tm //= 2
        tk = _tile(k)
        grid = (m // tm, k // tk)
        in_specs = [pl.BlockSpec((tm, n), lambda i, j: (i, 0)), pl.BlockSpec((tk, n), lambda i, j: (j, b_colblock))]
        out_shape, out_block = (m, k), (tm, tk)
    else:
        m, k = a.shape
        n = b.shape[1]
        tk, tn = _tile(k), (n if m * n * 2 <= resident_bytes else _tile(n))
        grid = (k // tk, n // tn)
        in_specs = [pl.BlockSpec((m, tk), lambda i, j: (0, i)), pl.BlockSpec((m, tn), lambda i, j: (0, j))]
        out_shape, out_block = (out_rows or k, n), (tk, tn)
    out_spec = pl.BlockSpec(out_block, lambda i, j: (i, j))
    has_acc = acc is not None

    def body(*refs):
        a_ref, b_ref = refs[0], refs[1]
        o_ref = refs[-1]
        av, bv = a_ref[...], b_ref[...]
        if mode == "nn":
            r = _dot(av, bv)
        elif mode == "nt":
            r = _dot_nt(av, bv)
        else:
            r = _dot_tn(av, bv)
        if has_acc:
            r = r + refs[2][...]
        o_ref[...] = r.astype(o_ref.dtype)

    operands = [a, b]
    if has_acc:
        in_specs = in_specs + [out_spec]
        operands.append(acc)
    return _call(body, name=name, out_shape=jax.ShapeDtypeStruct(out_shape, out_dtype), grid=grid, in_specs=in_specs,
                 out_specs=out_spec, operands=operands, semantics=("parallel", "parallel"), steps=steps)


def _tn_rows_into(name, a, b, into, row0, nrows):
    m, k = a.shape
    n = b.shape[1]

    def body(a_ref, b_ref, into_ref, o_ref):
        o_ref[...] = _dot_tn(a_ref[...], b_ref[...])[0:nrows].astype(o_ref.dtype)

    return pl.pallas_call(
        body, name=name, out_shape=jax.ShapeDtypeStruct(into.shape, into.dtype), grid=(1,),
        in_specs=[pl.BlockSpec((m, k), lambda i: (0, 0)), pl.BlockSpec((m, n), lambda i: (0, 0)),
                  pl.BlockSpec(memory_space=pl.ANY)],
        out_specs=pl.BlockSpec((nrows, n), lambda i: (row0 // nrows, 0)),
        input_output_aliases={2: 0}, compiler_params=_cparams(("arbitrary",)),
    )(a, b, into)


def _rms_fwd(name, h, w):
    rows, width = h.shape
    tm = _row_tile(rows, width)

    def body(h_ref, w_ref, o_ref):
        x = h_ref[...]
        r = lax.rsqrt(jnp.mean(x * x, axis=-1, keepdims=True) + RMS_EPS)
        o_ref[...] = (x * r * w_ref[...]).astype(BF16)

    return pl.pallas_call(
        body, name=name, out_shape=jax.ShapeDtypeStruct((rows, width), BF16), grid=(rows // tm,),
        in_specs=[pl.BlockSpec((tm, width), lambda i: (i, 0)), pl.BlockSpec((1, width), lambda i: (0, 0))],
        out_specs=pl.BlockSpec((tm, width), lambda i: (i, 0)), compiler_params=_cparams(("parallel",)),
    )(h, w)


def _resid_norm_fwd(name, h, pre, w, next_norms=()):
    rows, width = h.shape
    tm = _row_tile(rows, width)
    n_next = len(next_norms)

    def body(*refs):
        h_ref, p_ref, w_ref = refs[:3]
        v_refs = refs[3:3 + n_next]
        o_ref = refs[3 + n_next]
        n_refs = refs[4 + n_next:]
        p = p_ref[...]
        r = lax.rsqrt(jnp.mean(p * p, axis=-1, keepdims=True) + RMS_EPS)
        x = h_ref[...] + jnp.where(_rows_mask(pl.program_id(0), tm), p * r * w_ref[...], 0.0)
        o_ref[...] = x
        if n_next:
            rx = lax.rsqrt(jnp.mean(x * x, axis=-1, keepdims=True) + RMS_EPS)
            for v_ref, n_ref in zip(v_refs, n_refs):
                n_ref[...] = (x * rx * v_ref[...]).astype(BF16)

    row_spec = pl.BlockSpec((tm, width), lambda i: (i, 0))
    vec_spec = pl.BlockSpec((1, width), lambda i: (0, 0))
    outs = pl.pallas_call(
        body, name=name,
        out_shape=[jax.ShapeDtypeStruct((rows, width), F32)] + [jax.ShapeDtypeStruct((rows, width), BF16)] * n_next,
        grid=(rows // tm,), in_specs=[row_spec, row_spec, vec_spec] + [vec_spec] * n_next,
        out_specs=[row_spec] * (1 + n_next), compiler_params=_cparams(("parallel",)),
    )(h, pre, w, *next_norms)
    return outs[0], list(outs[1:])


def _resid_norm_loss(name, h, pre, w, target):
    rows, width = h.shape

    def body(h_ref, p_ref, w_ref, t_ref, dh_ref, loss_ref, dp_ref, dw_ref):
        i = pl.program_id(0)
        p = p_ref[...]
        r = lax.rsqrt(jnp.mean(p * p, axis=-1, keepdims=True) + RMS_EPS)
        x = h_ref[...] + p * r * w_ref[...]
        real = (i + jnp.zeros((CHUNK, 1), jnp.int32)) >= 1
        diff = jnp.where(real, x - t_ref[...], 0.0)
        dh = diff * (1.0 / D_MODEL)
        dh_ref[...] = dh
        dp, dw_rows = _rms_bwd(dh, p, w_ref[...])
        dp_ref[...] = dp.astype(BF16)

        @pl.when(i == 0)
        def _():
            loss_ref[...] = jnp.zeros_like(loss_ref)
            dw_ref[...] = jnp.zeros_like(dw_ref)

        loss_ref[...] += jnp.sum(diff * diff) * (0.5 / D_MODEL)
        dw_ref[...] += jnp.sum(dw_rows, axis=0, keepdims=True)

    blk = pl.BlockSpec((CHUNK, width), lambda i: (i, 0))
    vec_spec = pl.BlockSpec((1, width), lambda i: (0, 0))
    return pl.pallas_call(
        body, name=name,
        out_shape=(jax.ShapeDtypeStruct((rows, width), F32), jax.ShapeDtypeStruct((1, LANES), F32),
                   jax.ShapeDtypeStruct((rows, width), BF16), jax.ShapeDtypeStruct((1, width), F32)),
        grid=(rows // CHUNK,),
        in_specs=[blk, blk, vec_spec, pl.BlockSpec((CHUNK, width), lambda i: (jnp.maximum(i - 1, 0), 0))],
        out_specs=(blk, pl.BlockSpec((1, LANES), lambda i: (0, 0)), blk, vec_spec),
        compiler_params=_cparams(("arbitrary",)),
    )(h, pre, w, target)


def _rms_bwd(dy, x, w):
    r = lax.rsqrt(jnp.mean(x * x, axis=-1, keepdims=True) + RMS_EPS)
    xhat = x * r
    dxhat = dy * w
    return r * (dxhat - xhat * jnp.mean(dxhat * xhat, axis=-1, keepdims=True)), dy * xhat


def _norm_bwd_add(name, dh, dhn, h, w, then=None, split_first_block=False, steps=()):
    rows, width = dh.shape
    tm = CHUNK if split_first_block else _row_tile(rows, width)
    fused = then is not None
    assert not (fused and split_first_block)

    def body(*refs):
        dh_ref, dhn_ref, h_ref, w_ref = refs[:4]
        o_ref, dw_ref = refs[6:8] if fused else refs[-2:]
        i = pl.program_id(0)
        valid = _rows_mask(i, tm)
        dx, dw_rows = _rms_bwd(dhn_ref[...], h_ref[...], w_ref[...])
        dh_new = dh_ref[...] + jnp.where(valid, dx, 0.0)
        if split_first_block:
            first_ref = refs[4]

            @pl.when(i == 0)
            def _():
                first_ref[...] = dh_new

            @pl.when(i > 0)
            def _():
                o_ref[...] = dh_new
        else:
            o_ref[...] = dh_new

        @pl.when(i == 0)
        def _():
            dw_ref[...] = jnp.zeros_like(dw_ref)

        dw_ref[...] += jnp.sum(dw_rows, axis=0, keepdims=True)
        if fused:
            p_ref, wp_ref, dp_ref, dwp_ref = refs[4], refs[5], refs[8], refs[9]
            dp, dwp_rows = _rms_bwd(jnp.where(valid, dh_new, 0.0), p_ref[...], wp_ref[...])
            dp_ref[...] = dp.astype(BF16)

            @pl.when(i == 0)
            def _():
                dwp_ref[...] = jnp.zeros_like(dwp_ref)

            dwp_ref[...] += jnp.sum(dwp_rows, axis=0, keepdims=True)

    row_spec = pl.BlockSpec((tm, width), lambda i: (i, 0))
    vec_spec = pl.BlockSpec((1, width), lambda i: (0, 0))
    row_f32, vec_f32 = jax.ShapeDtypeStruct((rows, width), F32), jax.ShapeDtypeStruct((1, width), F32)
    in_specs, operands = [row_spec, row_spec, row_spec, vec_spec], [dh, dhn, h, w]
    out_shape, out_specs = [row_f32, vec_f32], [row_spec, vec_spec]
    if split_first_block:
        out_shape = [jax.ShapeDtypeStruct((tm, width), F32), jax.ShapeDtypeStruct((rows - tm, width), F32), vec_f32]
        out_specs = [pl.BlockSpec((tm, width), lambda i: (0, 0)),
                     pl.BlockSpec((tm, width), lambda i: (jnp.maximum(i - 1, 0), 0)), vec_spec]
    if fused:
        in_specs += [row_spec, vec_spec]
        operands += list(then)
        out_shape += [jax.ShapeDtypeStruct((rows, width), BF16), vec_f32]
        out_specs += [row_spec, vec_spec]
    return _call(body, name=name, out_shape=out_shape, grid=(rows // tm,), in_specs=in_specs, out_specs=out_specs,
                 operands=operands, semantics=("arbitrary",), steps=steps)


def _shift_down(x, s, rows):
    return pltpu.roll(x, s, 0) if s else x


def _shift_up(x, s, rows):
    return pltpu.roll(x, rows - s, 0) if s else x


def _conv4_fwd(name, zx, cw, cb, steps=()):
    rows = zx.shape[0]
    off = D_INNER // LANES

    def body(x_ref, w_ref, b_ref, o_ref):
        x = x_ref[...]
        acc = b_ref[...] + w_ref[pl.ds(SSM_CONV - 1, 1), :] * x
        for s in range(1, SSM_CONV):
            acc = acc + w_ref[pl.ds(SSM_CONV - 1 - s, 1), :] * _shift_down(x, s, rows)
        valid = lax.broadcasted_iota(jnp.int32, (rows, 1), 0) >= PAD_ROWS
        o_ref[...] = jnp.where(valid, acc * _sigmoid(acc), 0.0)

    return _call(
        body, name=name, out_shape=jax.ShapeDtypeStruct((rows, D_XBC), F32), grid=(D_XBC // LANES,),
        in_specs=[pl.BlockSpec((rows, LANES), lambda j: (0, j + off)),
                  pl.BlockSpec((SSM_CONV, LANES), lambda j: (0, j)),
                  pl.BlockSpec((1, LANES), lambda j: (0, j))],
        out_specs=pl.BlockSpec((rows, LANES), lambda j: (0, j)), operands=[zx, cw, cb],
        semantics=("parallel",), steps=steps)


def _conv4_bwd(name, zx, dout, cw, cb, into):
    rows, width = dout.shape
    zoff = D_INNER // LANES

    def body(x_ref, d_ref, w_ref, b_ref, into_ref, dx_ref, dw_ref, db_ref):
        x = x_ref[...]
        shifted = [_shift_down(x, s, rows) for s in range(SSM_CONV)]
        acc = b_ref[...]
        for s in range(SSM_CONV):
            acc = acc + w_ref[pl.ds(SSM_CONV - 1 - s, 1), :] * shifted[s]
        sig = _sigmoid(acc)
        valid = lax.broadcasted_iota(jnp.int32, (rows, 1), 0) >= PAD_ROWS
        dpre = jnp.where(valid, d_ref[...] * sig * (1.0 + acc * (1.0 - sig)), 0.0)
        dx = w_ref[pl.ds(SSM_CONV - 1, 1), :] * dpre
        for s in range(1, SSM_CONV):
            dx = dx + w_ref[pl.ds(SSM_CONV - 1 - s, 1), :] * _shift_up(dpre, s, rows)
        dx_ref[...] = dx.astype(BF16)
        for s in range(SSM_CONV):
            dw_ref[pl.ds(SSM_CONV - 1 - s, 1), :] = jnp.sum(dpre * shifted[s], axis=0, keepdims=True)
        db_ref[...] = jnp.sum(dpre, axis=0, keepdims=True)

    return pl.pallas_call(
        body, name=name,
        out_shape=(jax.ShapeDtypeStruct(into.shape, BF16), jax.ShapeDtypeStruct((SSM_CONV, width), F32),
                   jax.ShapeDtypeStruct((1, width), F32)),
        grid=(width // LANES,),
        in_specs=[pl.BlockSpec((rows, LANES), lambda j: (0, j + zoff)),
                  pl.BlockSpec((rows, LANES), lambda j: (0, j)),
                  pl.BlockSpec((SSM_CONV, LANES), lambda j: (0, j)),
                  pl.BlockSpec((1, LANES), lambda j: (0, j)),
                  pl.BlockSpec(memory_space=pl.ANY)],
        out_specs=(pl.BlockSpec((rows, LANES), lambda j: (0, j + zoff)),
                   pl.BlockSpec((SSM_CONV, LANES), lambda j: (0, j)),
                   pl.BlockSpec((1, LANES), lambda j: (0, j))),
        input_output_aliases={4: 0}, compiler_params=_cparams(("parallel",)),
    )(zx, dout, cw, cb, into)


FFN_TILE = 2 * LANES


def _ffn_up_conv(name, hn, w_up, cw, cb, steps=()):
    rows, k = hn.shape
    chip_blocks = w_up.shape[2] // LANES
    half_blocks = D_FF // LANES
    nt = D_FF // FFN_TILE

    def weight_block(offset):
        return pl.BlockSpec((None, k, LANES), lambda j: ((2 * j + offset) // chip_blocks, 0, (2 * j + offset) % chip_blocks))

    def body(a_ref, g0, g1, v0, v1, wg_ref, wv_ref, bg_ref, bv_ref, upg_ref, upv_ref, act_ref):
        a = a_ref[...]
        g = _dot(a, jnp.concatenate([g0[...], g1[...]], axis=1))
        v = _dot(a, jnp.concatenate([v0[...], v1[...]], axis=1))
        upg_ref[...] = g
        upv_ref[...] = v
        ug, uv = bg_ref[...], bv_ref[...]
        for s in range(FFN_CONV):
            ug = ug + wg_ref[pl.ds(FFN_CONV - 1 - s, 1), :] * _shift_down(g, s, rows)
            uv = uv + wv_ref[pl.ds(FFN_CONV - 1 - s, 1), :] * _shift_down(v, s, rows)
        act_ref[...] = (ug * _sigmoid(ug) * uv).astype(BF16)

    col = pl.BlockSpec((rows, FFN_TILE), lambda j: (0, j))
    wsp = lambda shift: pl.BlockSpec((FFN_CONV, FFN_TILE), lambda j: (0, j + shift))
    bsp = lambda shift: pl.BlockSpec((1, FFN_TILE), lambda j: (0, j + shift))
    half = jax.ShapeDtypeStruct((rows, D_FF), F32)
    return _call(
        body, name=name, out_shape=(half, half, jax.ShapeDtypeStruct((rows, D_FF), BF16)), grid=(nt,),
        in_specs=[pl.BlockSpec((rows, k), lambda j: (0, 0)), weight_block(0), weight_block(1),
                  weight_block(half_blocks), weight_block(half_blocks + 1), wsp(0), wsp(nt), bsp(0), bsp(nt)],
        out_specs=(col, col, col), operands=[hn, w_up, w_up, w_up, w_up, cw, cw, cb, cb],
        semantics=("parallel",), steps=steps)


def _ffn_conv_bwd(name, up_g, up_v, dact, cw, cb, hn, w_up, steps=()):
    rows, k = hn.shape
    chip_blocks = w_up.shape[2] // LANES
    nt = D_FF // LANES

    def weight_block(shift):
        return pl.BlockSpec((None, k, LANES), lambda j: ((j + shift) // chip_blocks, 0, (j + shift) % chip_blocks))

    def body(g_ref, v_ref, d_ref, wg_ref, wv_ref, bg_ref, bv_ref, upg_ref, upv_ref, hn_ref,
             dwg_ref, dwv_ref, dbg_ref, dbv_ref, dhn_ref, dup_ref, acc, hn_scr, hnt_scr, dup_scr, sems):
        j = pl.program_id(0)
        hn_copy = pltpu.make_async_copy(hn_ref, hn_scr, sems.at[0])
        dhn_copy = pltpu.make_async_copy(acc, dhn_ref, sems.at[0])

        def dup_copy(step, half):
            block, slot = step + half * nt, 2 * (step % 2) + half
            cols = pl.ds(pl.multiple_of((block % chip_blocks) * LANES, LANES), LANES)
            return pltpu.make_async_copy(dup_scr.at[slot], dup_ref.at[block // chip_blocks, :, cols], sems.at[1 + slot])

        @pl.when(j == 0)
        def _():
            hn_copy.start()
            acc[...] = jnp.zeros_like(acc)
            hn_copy.wait()
            for r in range(0, rows, LANES):
                hnt_scr[:, r:r + LANES] = hn_scr[r:r + LANES, :].T

        @pl.when(j >= 2)
        def _():
            dup_copy(j - 2, 0).wait()
            dup_copy(j - 2, 1).wait()

        g, v = g_ref[...], v_ref[...]
        gs = [_shift_down(g, s, rows) for s in range(FFN_CONV)]
        vs = [_shift_down(v, s, rows) for s in range(FFN_CONV)]
        ug, uv = bg_ref[...], bv_ref[...]
        for s in range(FFN_CONV):
            ug = ug + wg_ref[pl.ds(FFN_CONV - 1 - s, 1), :] * gs[s]
            uv = uv + wv_ref[pl.ds(FFN_CONV - 1 - s, 1), :] * vs[s]
        sig = _sigmoid(ug)
        dsig = d_ref[...] * sig
        dup = []
        for dpre, src, w_ref, dw_ref, db_ref in (
                (dsig * uv * (1.0 + ug * (1.0 - sig)), gs, wg_ref, dwg_ref, dbg_ref),
                (dsig * ug, vs, wv_ref, dwv_ref, dbv_ref)):
            dx = w_ref[pl.ds(FFN_CONV - 1, 1), :] * dpre
            for s in range(1, FFN_CONV):
                dx = dx + w_ref[pl.ds(FFN_CONV - 1 - s, 1), :] * _shift_up(dpre, s, rows)
            dup.append(dx.astype(BF16))
            for s in range(FFN_CONV):
                dw_ref[pl.ds(FFN_CONV - 1 - s, 1), :] = jnp.sum(dpre * src[s], axis=0, keepdims=True)
            db_ref[...] = jnp.sum(dpre, axis=0, keepdims=True)
        dup = jnp.concatenate(dup, axis=1)
        acc[...] += _dot_nt(dup, jnp.concatenate([upg_ref[...], upv_ref[...]], axis=1))
        dw = _dot(hnt_scr[...], dup)
        slot = 2 * (j % 2)
        dup_scr[slot] = dw[:, :LANES].astype(BF16)
        dup_scr[slot + 1] = dw[:, LANES:].astype(BF16)
        dup_copy(j, 0).start()
        dup_copy(j, 1).start()

        @pl.when(j == nt - 1)
        def _():
            dhn_copy.start()
            for step in (j - 1, j):
                dup_copy(step, 0).wait()
                dup_copy(step, 1).wait()
            dhn_copy.wait()

    col = pl.BlockSpec((rows, LANES), lambda j: (0, j))
    wsp = lambda shift: pl.BlockSpec((FFN_CONV, LANES), lambda j: (0, j + shift))
    bsp = lambda shift: pl.BlockSpec((1, LANES), lambda j: (0, j + shift))
    any_spec = pl.BlockSpec(memory_space=pl.ANY)
    dw_shape = jax.ShapeDtypeStruct((FFN_CONV, D_FF), F32)
    db_shape = jax.ShapeDtypeStruct((1, D_FF), F32)
    return _call(
        body, name=name, grid=(nt,),
        out_shape=(dw_shape, dw_shape, db_shape, db_shape, jax.ShapeDtypeStruct((rows, k), F32),
                   jax.ShapeDtypeStruct(w_up.shape, BF16)),
        in_specs=[col, col, col, wsp(0), wsp(nt), bsp(0), bsp(nt), weight_block(0), weight_block(nt), any_spec],
        out_specs=(wsp(0), wsp(0), bsp(0), bsp(0), any_spec, any_spec),
        operands=[up_g, up_v, dact, cw, cw, cb, cb, w_up, w_up, hn],
        scratch_shapes=[pltpu.VMEM((rows, k), F32), pltpu.VMEM((rows, k), BF16), pltpu.VMEM((k, rows), BF16),
                        pltpu.VMEM((4, k, LANES), BF16), pltpu.SemaphoreType.DMA((5,))],
        semantics=("arbitrary",), steps=steps)


def _dt_fwd(name, dtr, bias):
    rows = dtr.shape[0]
    tm = _row_tile(rows, LANES)

    def body(d_ref, b_ref, o_ref):
        v = d_ref[...] + b_ref[...]
        sp = jnp.maximum(v, 0.0) + jnp.log1p(jnp.exp(-jnp.abs(v)))
        lane = lax.broadcasted_iota(jnp.int32, (tm, LANES), 1)
        ok = _rows_mask(pl.program_id(0), tm) & (lane < SSM_HEADS)
        o_ref[...] = jnp.where(ok, sp, 0.0)

    return pl.pallas_call(
        body, name=name, out_shape=jax.ShapeDtypeStruct((rows, LANES), F32), grid=(rows // tm,),
        in_specs=[pl.BlockSpec((tm, LANES), lambda i: (i, 0)), pl.BlockSpec((1, LANES), lambda i: (0, 0))],
        out_specs=pl.BlockSpec((tm, LANES), lambda i: (i, 0)), compiler_params=_cparams(("parallel",)),
    )(dtr, bias)


def _dt_bwd(name, ddt, dtr, bias):
    rows = dtr.shape[0]
    tm = _row_tile(rows, LANES)

    def body(g_ref, d_ref, b_ref, o_ref, db_ref):
        i = pl.program_id(0)
        lane = lax.broadcasted_iota(jnp.int32, (tm, LANES), 1)
        ok = _rows_mask(i, tm) & (lane < SSM_HEADS)
        dv = jnp.where(ok, g_ref[...] * _sigmoid(d_ref[...] + b_ref[...]), 0.0)
        o_ref[...] = dv.astype(BF16)

        @pl.when(i == 0)
        def _():
            db_ref[...] = jnp.zeros_like(db_ref)

        db_ref[...] += jnp.sum(dv, axis=0, keepdims=True)

    row_spec = pl.BlockSpec((tm, LANES), lambda i: (i, 0))
    vec_spec = pl.BlockSpec((1, LANES), lambda i: (0, 0))
    return pl.pallas_call(
        body, name=name,
        out_shape=(jax.ShapeDtypeStruct((rows, LANES), BF16), jax.ShapeDtypeStruct((1, LANES), F32)),
        grid=(rows // tm,), in_specs=[row_spec, row_spec, vec_spec], out_specs=(row_spec, vec_spec),
        compiler_params=_cparams(("arbitrary",)),
    )(ddt, dtr, bias)


def _gate_fwd(name, y, zx, w, steps=()):
    rows = y.shape[0]
    tm = _row_tile(rows, D_INNER)

    def body(y_ref, z_ref, w_ref, o_ref):
        z = z_ref[...]
        g = y_ref[...] * (z * _sigmoid(z))
        r = lax.rsqrt(jnp.mean(g * g, axis=-1, keepdims=True) + RMS_EPS)
        o_ref[...] = (g * r * w_ref[...]).astype(BF16)

    row_spec = pl.BlockSpec((tm, D_INNER), lambda i: (i, 0))
    return _call(
        body, name=name, out_shape=jax.ShapeDtypeStruct((rows, D_INNER), BF16), grid=(rows // tm,),
        in_specs=[row_spec, row_spec, pl.BlockSpec((1, D_INNER), lambda i: (0, 0))],
        out_specs=row_spec, operands=[y, zx, w], semantics=("parallel",), steps=steps)


def _gate_bwd(name, dyn, y, zx, w):
    rows = y.shape[0]
    tm = _row_tile(rows, D_INNER)

    def body(d_ref, y_ref, z_ref, w_ref, dy_ref, dz_ref, dw_ref):
        i = pl.program_id(0)
        z, yv = z_ref[...], y_ref[...]
        sig = _sigmoid(z)
        sz = z * sig
        g = yv * sz
        r = lax.rsqrt(jnp.mean(g * g, axis=-1, keepdims=True) + RMS_EPS)
        ghat = g * r
        dn = d_ref[...]
        dghat = dn * w_ref[...]
        dg = r * (dghat - ghat * jnp.mean(dghat * ghat, axis=-1, keepdims=True))
        dy_ref[...] = dg * sz
        dz_ref[...] = (dg * yv * sig * (1.0 + z * (1.0 - sig))).astype(BF16)

        @pl.when(i == 0)
        def _():
            dw_ref[...] = jnp.zeros_like(dw_ref)

        dw_ref[...] += jnp.sum(dn * ghat, axis=0, keepdims=True)

    row_spec = pl.BlockSpec((tm, D_INNER), lambda i: (i, 0))
    vec_spec = pl.BlockSpec((1, D_INNER), lambda i: (0, 0))
    return pl.pallas_call(
        body, name=name,
        out_shape=(jax.ShapeDtypeStruct((rows, D_INNER), F32), jax.ShapeDtypeStruct((rows, D_MAIN), BF16),
                   jax.ShapeDtypeStruct((1, D_INNER), F32)),
        grid=(rows // tm,), in_specs=[row_spec, row_spec, row_spec, vec_spec],
        out_specs=(row_spec, row_spec, vec_spec), compiler_params=_cparams(("arbitrary",)),
    )(dyn, y, zx, w)


def _split3(x):
    hi = x.astype(BF16)
    r1 = x - hi.astype(F32)
    mid = r1.astype(BF16)
    lo = (r1 - mid.astype(F32)).astype(BF16)
    return hi, mid, lo


def _dot3_data_lhs(x, sel):
    sel16 = sel.astype(F32).astype(BF16)
    hi, mid, lo = _split3(x)
    return _dot(hi, sel16) + _dot(mid, sel16) + _dot(lo, sel16)


def _dot2_data_lhs(x, sel):
    sel16 = sel.astype(F32).astype(BF16)
    hi = x.astype(BF16)
    mid = (x - hi.astype(F32)).astype(BF16)
    return _dot(hi, sel16) + _dot(mid, sel16)


def _dot3_data_rhs(sel, x):
    sel16 = sel.astype(F32).astype(BF16)
    hi, mid, lo = _split3(x)
    return _dot(sel16, hi) + _dot(sel16, mid) + _dot(sel16, lo)


def _causal_masks():
    r = lax.broadcasted_iota(jnp.int32, (CHUNK, CHUNK), 0)
    c = lax.broadcasted_iota(jnp.int32, (CHUNK, CHUNK), 1)
    return r >= c, r <= c


def _expand_heads_matrix(g):
    k = lax.broadcasted_iota(jnp.int32, (LANES, GROUP_W), 0)
    j = lax.broadcasted_iota(jnp.int32, (LANES, GROUP_W), 1)
    return HEADS_PER_GROUP * g + jnp.right_shift(j, 6) == k


def _reduce_heads_matrix(g):
    j = lax.broadcasted_iota(jnp.int32, (GROUP_W, LANES), 0)
    k = lax.broadcasted_iota(jnp.int32, (GROUP_W, LANES), 1)
    return HEADS_PER_GROUP * g + jnp.right_shift(j, 6) == k


def _reduce_pair_matrix(g, p):
    j = lax.broadcasted_iota(jnp.int32, (LANES, LANES), 0)
    k = lax.broadcasted_iota(jnp.int32, (LANES, LANES), 1)
    return HEADS_PER_GROUP * g + 2 * p + jnp.right_shift(j, 6) == k


def _group_cols(ref, g, width):
    return ref.at[:, pl.ds(g * width, width)]


def _ssd_prep(name, dt, a128, steps=()):
    rows = dt.shape[0]
    nc = rows // CHUNK

    def body(dt_ref, a_ref, dte_ref, acs_ref, acst_ref):
        causal, _ = _causal_masks()
        dtv = dt_ref[...]
        acs = _dot3_data_rhs(causal, dtv) * a_ref[...]
        acst_ref[...] = acs.T[0:SSM_HEADS]
        for g in range(N_GROUPS):
            expand = _expand_heads_matrix(g)
            _group_cols(dte_ref, g, GROUP_W)[...] = _dot3_data_lhs(dtv, expand)
            _group_cols(acs_ref, g, GROUP_W)[...] = _dot3_data_lhs(acs, expand)

    blk = pl.BlockSpec((CHUNK, D_INNER), lambda c: (c, 0))
    shp = jax.ShapeDtypeStruct((rows, D_INNER), F32)
    return _call(
        body, name=name, out_shape=(shp, shp, jax.ShapeDtypeStruct((nc, SSM_HEADS, CHUNK), F32)), grid=(nc,),
        in_specs=[pl.BlockSpec((CHUNK, LANES), lambda c: (c, 0)), pl.BlockSpec((1, LANES), lambda c: (0, 0))],
        out_specs=(blk, blk, pl.BlockSpec((None, SSM_HEADS, CHUNK), lambda c: (c, 0, 0))),
        operands=[dt, a128], semantics=("parallel",), steps=steps)


def _ssd_common(x_ref, b_ref, c_ref, dte_ref, acs_ref):
    x = x_ref[...]
    dt_exp = dte_ref[...]
    acs_exp = acs_ref[...]
    tot_exp = acs_ref[pl.ds(CHUNK - 1, 1), :]
    xdt = x * dt_exp
    e_exp = jnp.exp(acs_exp)
    f_exp = jnp.exp(tot_exp - acs_exp)
    return _causal_masks(), x, dt_exp, acs_exp, tot_exp, xdt, e_exp, f_exp, b_ref[...], c_ref[...]


def _pair_decay(acs_pair, acs_row, e, causal):
    lane = lax.broadcasted_iota(jnp.int32, (CHUNK, LANES), 1)
    mine = (lane < HEAD_DIM) if e == 0 else (lane >= HEAD_DIM)
    a_l = jnp.where(mine, acs_pair, pltpu.roll(acs_pair, HEAD_DIM, 1))
    seg = a_l - acs_row
    dm = jnp.where(causal[0], jnp.exp(jnp.minimum(seg, 0.0)), 0.0)
    dmt = jnp.where(causal[1], jnp.exp(jnp.minimum(-seg, 0.0)), 0.0)
    return dm, dmt


def _ssd_specs(index_of_chunk):
    wide = pl.BlockSpec((CHUNK, D_INNER), lambda c: (index_of_chunk(c), 0))
    b_spec = pl.BlockSpec((CHUNK, D_BC), lambda c: (index_of_chunk(c), D_INNER // D_BC))
    c_spec = pl.BlockSpec((CHUNK, D_BC), lambda c: (index_of_chunk(c), D_INNER // D_BC + 1))
    rows_spec = pl.BlockSpec((None, SSM_HEADS, CHUNK), lambda c: (index_of_chunk(c), 0, 0))
    state_spec = pl.BlockSpec((N_GROUPS, None, D_STATE, GROUP_W), lambda c: (0, index_of_chunk(c), 0, 0))
    return wide, b_spec, c_spec, rows_spec, state_spec


def _ssd_fwd(name, xbc, dt_exp, acs_exp, acs_rows, dskexp, steps=()):
    rows = xbc.shape[0]
    nc = rows // CHUNK

    def body(x_ref, b_ref, c_ref, dte_ref, acs_ref, acst_ref, dsk_ref, y_ref, st_ref, s_scr):
        @pl.when(pl.program_id(0) == 0)
        def _():
            s_scr[...] = jnp.zeros_like(s_scr)

        lane = lax.broadcasted_iota(jnp.int32, (CHUNK, LANES), 1)
        for g in range(N_GROUPS):
            y_g = _group_cols(y_ref, g, GROUP_W)
            causal, x, _, acs_exp_v, tot_exp, xdt, e_exp, f_exp, bm, cm = _ssd_common(
                _group_cols(x_ref, g, GROUP_W), _group_cols(b_ref, g, D_STATE), _group_cols(c_ref, g, D_STATE),
                _group_cols(dte_ref, g, GROUP_W), _group_cols(acs_ref, g, GROUP_W))
            state = s_scr[g]
            st_ref[g] = state
            cb16, bb16 = cm.astype(BF16), bm.astype(BF16)
            cb = _dot_nt(cb16, bb16)
            base = e_exp * _dot(cb16, state.astype(BF16)) + _group_cols(dsk_ref, g, GROUP_W)[...] * x
            for p in range(HEADS_PER_GROUP // 2):
                sl = slice(p * LANES, (p + 1) * LANES)
                xp = xdt[:, sl].astype(BF16)
                yd = []
                for e in range(2):
                    acs_row = acst_ref[pl.ds(g * HEADS_PER_GROUP + 2 * p + e, 1), :]
                    dm, _ = _pair_decay(acs_exp_v[:, sl], acs_row, e, causal)
                    yd.append(_dot((cb * dm).astype(BF16), xp))
                y_g[:, sl] = jnp.where(lane < HEAD_DIM, yd[0], yd[1]) + base[:, sl]
            s_scr[g] = jnp.exp(tot_exp) * state + _dot_tn(bb16, (f_exp * xdt).astype(BF16))

    wide, b_spec, c_spec, rows_spec, state_spec = _ssd_specs(lambda c: c)
    return _call(
        body, name=name,
        out_shape=(jax.ShapeDtypeStruct((rows, D_INNER), F32),
                   jax.ShapeDtypeStruct((N_GROUPS, nc, D_STATE, GROUP_W), F32)),
        grid=(nc,),
        in_specs=[wide, b_spec, c_spec, wide, wide, rows_spec, pl.BlockSpec((1, D_INNER), lambda c: (0, 0))],
        out_specs=(wide, state_spec),
        scratch_shapes=[pltpu.VMEM((N_GROUPS, D_STATE, GROUP_W), F32)],
        operands=[xbc, xbc, xbc, dt_exp, acs_exp, acs_rows, dskexp], semantics=("arbitrary",), steps=steps)


def _ssd_bwd(name, xbc, dt_exp, acs_exp, acs_rows, dt, a128, dskexp, dy, states, steps=()):
    rows = xbc.shape[0]
    nc = rows // CHUNK
    last = nc - 1

    def body(x_ref, b_ref, c_ref, dte_ref, acs_ref, acst_ref, dt_ref, a128_ref, dsk_all, dy_all, st_all,
             dxbc_all, ddt_ref, dalog_ref, ddsk_ref, ds_all):
        dx_all, db_all, dc_all = (dxbc_all.at[:, :D_INNER], dxbc_all.at[:, D_INNER:D_INNER + D_BC],
                                  dxbc_all.at[:, D_INNER + D_BC:])

        @pl.when(pl.program_id(0) == 0)
        def _():
            ds_all[...] = jnp.zeros_like(ds_all)
            dalog_ref[...] = jnp.zeros_like(dalog_ref)
            ddsk_ref[...] = jnp.zeros_like(ddsk_ref)

        dacs = jnp.zeros((CHUNK, LANES), F32)
        ddt_x = jnp.zeros((CHUNK, LANES), F32)
        for g in range(N_GROUPS):
            dacs_g, ddt_x_g = group(
                g, _group_cols(x_ref, g, GROUP_W), _group_cols(b_ref, g, D_STATE), _group_cols(c_ref, g, D_STATE),
                _group_cols(dte_ref, g, GROUP_W), _group_cols(acs_ref, g, GROUP_W), acst_ref,
                _group_cols(dsk_all, g, GROUP_W), _group_cols(dy_all, g, GROUP_W), st_all.at[g],
                _group_cols(dx_all, g, GROUP_W), _group_cols(db_all, g, D_STATE), _group_cols(dc_all, g, D_STATE),
                ddsk_ref, ds_all.at[g])
            dacs, ddt_x = dacs + dacs_g, ddt_x + ddt_x_g
        _, causal_t = _causal_masks()
        da = _dot3_data_rhs(causal_t, dacs)
        ddt_ref[...] = da * a128_ref[...] + ddt_x
        dalog_ref[...] += jnp.sum(da * dt_ref[...], axis=0, keepdims=True) * a128_ref[...]

    def group(g, x_ref, b_ref, c_ref, dte_ref, acs_ref, acst_ref, dsk_ref, dy_ref, st_ref,
              dx_ref, db_ref, dc_ref, ddsk_ref, ds_scr):
        causal, x, dt_exp, acs_exp_v, tot_exp, xdt, e_exp, f_exp, bm, cm = _ssd_common(
            x_ref, b_ref, c_ref, dte_ref, acs_ref)
        reduce_heads = _reduce_heads_matrix(g)
        state, dstate = st_ref[...], ds_scr[...]
        dyv = dy_ref[...]
        cb16, bb16 = cm.astype(BF16), bm.astype(BF16)
        s16, ds16 = state.astype(BF16), dstate.astype(BF16)
        cb = _dot_nt(cb16, bb16)
        cbt = _dot_nt(bb16, cb16)
        cs = _dot(cb16, s16)
        bds = _dot(bb16, ds16)
        edy = e_exp * dyv
        fx = f_exp * xdt
        dxdt_base = f_exp * bds
        dc_acc = _dot_nt(edy.astype(BF16), s16)
        db_acc = _dot_nt(fx.astype(BF16), ds16)
        ds_scr[...] = jnp.exp(tot_exp) * dstate + _dot_tn(cb16, edy.astype(BF16))
        q = fx * bds
        dacs = _dot2_data_lhs(edy * cs - q, reduce_heads)
        dtot = jnp.sum(_dot2_data_lhs(q + jnp.exp(tot_exp) * dstate * state, reduce_heads), axis=0, keepdims=True)
        ddsk_ref[...] += jnp.sum(_dot2_data_lhs(dyv * x, reduce_heads), axis=0, keepdims=True)
        lane = lax.broadcasted_iota(jnp.int32, (CHUNK, LANES), 1)
        dcb = jnp.zeros((CHUNK, CHUNK), F32)
        dcbt = jnp.zeros((CHUNK, CHUNK), F32)
        ddt_x = jnp.zeros((CHUNK, LANES), F32)
        for p in range(HEADS_PER_GROUP // 2):
            sl = slice(p * LANES, (p + 1) * LANES)
            xp, dyp = xdt[:, sl], dyv[:, sl]
            xp16, dyp16 = xp.astype(BF16), dyp.astype(BF16)
            dxh = []
            for e in range(2):
                h = 2 * p + e
                mine = (lane < HEAD_DIM) if e == 0 else (lane >= HEAD_DIM)
                acs_row = acst_ref[pl.ds(g * HEADS_PER_GROUP + h, 1), :]
                dm, dmt = _pair_decay(acs_exp_v[:, sl], acs_row, e, causal)
                m, mt = cb * dm, cbt * dmt
                xh16 = jnp.where(mine, xp, 0.0).astype(BF16)
                dyh16 = jnp.where(mine, dyp, 0.0).astype(BF16)
                d_m = _dot_nt(dyh16, xp16)
                d_mt = _dot_nt(xh16, dyp16)
                dacs_h = (jnp.sum(d_m * m, axis=-1, keepdims=True)
                          - jnp.sum(d_mt * mt, axis=-1, keepdims=True))
                dacs = dacs + jnp.where(lane == HEADS_PER_GROUP * g + h, dacs_h, 0.0)
                dcb = dcb + d_m * dm
                dcbt = dcbt + d_mt * dmt
                dxh.append(_dot(mt.astype(BF16), dyp16))
            dxdt = jnp.where(lane < HEAD_DIM, dxh[0], dxh[1]) + dxdt_base[:, sl]
            dx_ref[:, sl] = dxdt * dt_exp[:, sl] + dsk_ref[:, sl] * dyp
            ddt_x = ddt_x + _dot2_data_lhs(dxdt * x[:, sl], _reduce_pair_matrix(g, p))
        dc_ref[...] = dc_acc + _dot(dcb.astype(BF16), bb16)
        db_ref[...] = db_acc + _dot(dcbt.astype(BF16), cb16)
        row = lax.broadcasted_iota(jnp.int32, (CHUNK, LANES), 0)
        return dacs + jnp.where(row == CHUNK - 1, dtot, 0.0), ddt_x

    wide, b_spec, c_spec, rows_spec, state_spec = _ssd_specs(lambda c: last - c)
    heads_spec = pl.BlockSpec((CHUNK, LANES), lambda c: (last - c, 0))
    vec_spec = pl.BlockSpec((1, LANES), lambda c: (0, 0))
    vec_shape = jax.ShapeDtypeStruct((1, LANES), F32)
    return _call(
        body, name=name,
        out_shape=(jax.ShapeDtypeStruct((rows, D_XBC), F32), jax.ShapeDtypeStruct((rows, LANES), F32),
                   vec_shape, vec_shape),
        grid=(nc,),
        in_specs=[wide, b_spec, c_spec, wide, wide, rows_spec, heads_spec, vec_spec,
                  pl.BlockSpec((1, D_INNER), lambda c: (0, 0)), wide, state_spec],
        out_specs=(pl.BlockSpec((CHUNK, D_XBC), lambda c: (last - c, 0)), heads_spec, vec_spec, vec_spec),
        scratch_shapes=[pltpu.VMEM((N_GROUPS, D_STATE, GROUP_W), F32)],
        operands=[xbc, xbc, xbc, dt_exp, acs_exp, acs_rows, dt, a128, dskexp, dy, states],
        semantics=("arbitrary",), steps=steps)


def _attn_visible(b, heads=1):
    row = jnp.bitwise_and(lax.broadcasted_iota(jnp.int32, (heads * CHUNK, 3 * CHUNK), 0), CHUNK - 1)
    col = lax.broadcasted_iota(jnp.int32, (heads * CHUNK, 3 * CHUNK), 1)
    bb = b + jnp.zeros_like(col)
    meta = (col < CHUNK) & (bb >= 1) & (col >= PAD_ROWS)
    prev = (col >= CHUNK) & (col < 2 * CHUNK) & (bb >= 2) & ((col - CHUNK) > row)
    cur = (col >= 2 * CHUNK) & ((col - 2 * CHUNK) <= row) & ((bb >= 1) | ((col - 2 * CHUNK) >= PAD_ROWS))
    return meta | prev | cur


def _attn_visible4(b):
    return _attn_visible(b, 4)


def _stack_heads(q_ref, sink_ref, kvh, scale):
    lane = lax.broadcasted_iota(jnp.int32, (CHUNK, LANES), 1)
    parts, sinks = [], []
    for pp in range(2):
        pair = kvh * 2 + pp
        qp = q_ref[:, pair * LANES:(pair + 1) * LANES] * scale
        for e in range(2):
            mine = (lane < HEAD_DIM) if e == 0 else (lane >= HEAD_DIM)
            parts.append(jnp.where(mine, qp, 0.0).astype(BF16))
            sinks.append(jnp.full((CHUNK, 1), sink_ref[2 * pair + e], F32))
    return jnp.concatenate(parts, axis=0), jnp.concatenate(sinks, axis=0)


def _attn_operands(q_ref, k0, kp, kc, v0, vp, vc, sink_ref):
    kcat, vcat, q4, sink4 = [], [], [], []
    for kvh in range(N_KV_HEADS):
        ksl = slice(kvh * LANES, (kvh + 1) * LANES)
        kcat.append(jnp.concatenate([k0[:, ksl], kp[:, ksl], kc[:, ksl]], axis=0).astype(BF16))
        vcat.append(jnp.concatenate([v0[:, ksl], vp[:, ksl], vc[:, ksl]], axis=0).astype(BF16))
        stacked, sinks = _stack_heads(q_ref, sink_ref, kvh, ATTN_SCALE)
        q4.append(stacked)
        sink4.append(sinks)
    return kcat, vcat, q4, sink4


def _attn_probs(q4, kcat, visible, sink4):
    heads = range(N_KV_HEADS)
    s = [jnp.where(visible, _dot_nt(q4[h], kcat[h]), NEG_INF) for h in heads]
    m = [jnp.maximum(jnp.max(s[h], axis=-1, keepdims=True), sink4[h]) for h in heads]
    pe = [jnp.exp(s[h] - m[h]) for h in heads]
    pe_sink = [jnp.exp(sink4[h] - m[h]) for h in heads]
    inv = [1.0 / (jnp.sum(pe[h], axis=-1, keepdims=True) + pe_sink[h]) for h in heads]
    return [pe[h] * inv[h] for h in heads], [pe_sink[h] * inv[h] for h in heads]


def _unstack_pairs(stacked, pp):
    lane = lax.broadcasted_iota(jnp.int32, (CHUNK, LANES), 1)
    return jnp.where(lane < HEAD_DIM, stacked[(2 * pp) * CHUNK:(2 * pp + 1) * CHUNK],
                     stacked[(2 * pp + 1) * CHUNK:(2 * pp + 2) * CHUNK])


def _attn_specs(colblock):
    blk = lambda f: pl.BlockSpec((CHUNK, 2 * D_KV), f)
    return [blk(lambda b: (0, colblock)), blk(lambda b: (jnp.maximum(b - 1, 0), colblock)), blk(lambda b: (b, colblock))]


def _attn_fwd(name, q, kv2, sinks, steps=()):
    rows = q.shape[0]

    def body(q_ref, k0, kp, kc, v0, vp, vc, sink_ref, o_ref):
        visible = _attn_visible4(pl.program_id(0))
        kcat, vcat, q4, sink4 = _attn_operands(q_ref, k0, kp, kc, v0, vp, vc, sink_ref)
        pn, _ = _attn_probs(q4, kcat, visible, sink4)
        o4 = [_dot(pn[h].astype(BF16), vcat[h]) for h in range(N_KV_HEADS)]
        for kvh in range(N_KV_HEADS):
            for pp in range(2):
                qsl = slice((kvh * 2 + pp) * LANES, (kvh * 2 + pp + 1) * LANES)
                o_ref[:, qsl] = _unstack_pairs(o4[kvh], pp).astype(BF16)

    return _call(
        body, name=name, out_shape=jax.ShapeDtypeStruct((rows, D_MODEL), BF16), grid=(rows // CHUNK,),
        in_specs=[pl.BlockSpec((CHUNK, D_MODEL), lambda b: (b, 0))] + _attn_specs(0) + _attn_specs(1)
        + [pl.BlockSpec(memory_space=pltpu.SMEM)],
        out_specs=pl.BlockSpec((CHUNK, D_MODEL), lambda b: (b, 0)),
        operands=[q, kv2, kv2, kv2, kv2, kv2, kv2, sinks], semantics=("parallel",), steps=steps)


def _attn_bwd(name, q, kv2, sinks, do, steps=()):
    rows = q.shape[0]

    def body(q_ref, k0, kp, kc, v0, vp, vc, sink_ref, do_ref,
             dq_ref, dkc_ref, dkp_ref, dvc_ref, dvp_ref, dkm_ref, dvm_ref, dsink_ref):
        @pl.when(pl.program_id(0) == 0)
        def _():
            dkm_ref[...] = jnp.zeros_like(dkm_ref)
            dvm_ref[...] = jnp.zeros_like(dvm_ref)
            dsink_ref[...] = jnp.zeros_like(dsink_ref)

        visible = _attn_visible4(pl.program_id(0))
        heads = range(N_KV_HEADS)
        lane1 = lax.broadcasted_iota(jnp.int32, (1, LANES), 1)
        kcat, vcat, q4, sink4 = _attn_operands(q_ref, k0, kp, kc, v0, vp, vc, sink_ref)
        do4 = [_stack_heads(do_ref, sink_ref, h, 1.0)[0] for h in heads]
        pn, psink = _attn_probs(q4, kcat, visible, sink4)
        dp = [_dot_nt(do4[h], vcat[h]) for h in heads]
        delta = [jnp.sum(pn[h] * dp[h], axis=-1, keepdims=True) for h in heads]
        ds16 = [(pn[h] * (dp[h] - delta[h])).astype(BF16) for h in heads]
        dq4 = [_dot(ds16[h], kcat[h]) for h in heads]
        dk_acc = [_dot_tn(ds16[h], q4[h]) for h in heads]
        dv_acc = [_dot_tn(pn[h].astype(BF16), do4[h]) for h in heads]
        dsink = jnp.zeros((1, LANES), F32)
        for kvh in heads:
            ksl = slice(kvh * LANES, (kvh + 1) * LANES)
            sink_terms = psink[kvh] * delta[kvh]
            for j in range(4):
                part = jnp.sum(sink_terms[j * CHUNK:(j + 1) * CHUNK], axis=0, keepdims=True)
                dsink = dsink - jnp.where(lane1 == kvh * 4 + j, part, 0.0)
            for pp in range(2):
                qsl = slice((kvh * 2 + pp) * LANES, (kvh * 2 + pp + 1) * LANES)
                dq_ref[:, qsl] = (_unstack_pairs(dq4[kvh], pp) * ATTN_SCALE).astype(BF16)
            dkm_ref[:, ksl] += dk_acc[kvh][0:CHUNK]
            dvm_ref[:, ksl] += dv_acc[kvh][0:CHUNK]
            dkp_ref[:, ksl] = dk_acc[kvh][CHUNK:2 * CHUNK]
            dvp_ref[:, ksl] = dv_acc[kvh][CHUNK:2 * CHUNK]
            dkc_ref[:, ksl] = dk_acc[kvh][2 * CHUNK:3 * CHUNK]
            dvc_ref[:, ksl] = dv_acc[kvh][2 * CHUNK:3 * CHUNK]
        dsink_ref[...] += dsink

    qspec = pl.BlockSpec((CHUNK, D_MODEL), lambda b: (b, 0))
    kvspec = pl.BlockSpec((CHUNK, 2 * D_KV), lambda b: (b, 0))
    fixed = pl.BlockSpec((CHUNK, 2 * D_KV), lambda b: (0, 0))
    kv_shape = jax.ShapeDtypeStruct((rows, 2 * D_KV), F32)
    meta_shape = jax.ShapeDtypeStruct((CHUNK, 2 * D_KV), F32)
    return _call(
        body, name=name,
        out_shape=(jax.ShapeDtypeStruct((rows, D_MODEL), BF16), kv_shape, kv_shape, kv_shape, kv_shape,
                   meta_shape, meta_shape, jax.ShapeDtypeStruct((1, LANES), F32)),
        grid=(rows // CHUNK,),
        in_specs=[qspec] + _attn_specs(0) + _attn_specs(1) + [pl.BlockSpec(memory_space=pltpu.SMEM), qspec],
        out_specs=(qspec, kvspec, kvspec, kvspec, kvspec, fixed, fixed, pl.BlockSpec((1, LANES), lambda b: (0, 0))),
        operands=[q, kv2, kv2, kv2, kv2, kv2, kv2, sinks, do], semantics=("arbitrary",), steps=steps)


def _kv_grad_combine(name, dk_cur, dk_prev, dk_meta, dv_cur, dv_prev, dv_meta):
    rows = dk_cur.shape[0]
    nb = rows // CHUNK
    width = 2 * D_KV

    def body(kc_ref, kp_ref, km_ref, vc_ref, vp_ref, vm_ref, o_ref):
        jj = pl.program_id(0) + jnp.zeros((CHUNK, 1), jnp.int32)
        for half, (c_ref, p_ref, m_ref) in enumerate(((kc_ref, kp_ref, km_ref), (vc_ref, vp_ref, vm_ref))):
            total = c_ref[...] + jnp.where(jj < nb - 1, p_ref[...], 0.0) + jnp.where(jj == 0, m_ref[...], 0.0)
            o_ref[:, half * width:(half + 1) * width] = total.astype(BF16)

    blk = lambda f: pl.BlockSpec((CHUNK, width), f)
    three = lambda: [blk(lambda j: (j, 0)), blk(lambda j: (jnp.minimum(j + 1, nb - 1), 0)), blk(lambda j: (0, 0))]
    return pl.pallas_call(
        body, name=name, out_shape=jax.ShapeDtypeStruct((rows, 2 * width), BF16), grid=(nb,),
        in_specs=three() + three(), out_specs=pl.BlockSpec((CHUNK, 2 * width), lambda j: (j, 0)),
        compiler_params=_cparams(("parallel",)),
    )(dk_cur, dk_prev, dk_meta, dv_cur, dv_prev, dv_meta)


def _adamw(name, w, g, m, v, steps=()):
    rows, width = w.shape
    tr = rows
    for cand in range(8, rows + 1, 8):
        if rows % cand == 0 and cand * width * 4 <= (1 << 20):
            tr = cand

    def body(*refs):
        _adamw_update(*refs)

    blk = pl.BlockSpec((tr, width), lambda i: (i, 0))
    shp = jax.ShapeDtypeStruct((rows, width), F32)
    return _call(body, name=name, out_shape=(shp, shp, shp), grid=(rows // tr,), in_specs=[blk] * 4,
                 out_specs=(blk,) * 3, operands=[w, g, m, v], semantics=("parallel",), steps=steps)


def _adamw_update(w_ref, g_ref, m_ref, v_ref, d_ref, mo_ref, vo_ref):
    gv = g_ref[...]
    mn = ADAM_B1 * m_ref[...] + (1.0 - ADAM_B1) * gv
    vn = ADAM_B2 * v_ref[...] + (1.0 - ADAM_B2) * (gv * gv)
    m_hat = mn / (1.0 - ADAM_B1 ** ADAM_STEP)
    v_hat = vn / (1.0 - ADAM_B2 ** ADAM_STEP)
    d_ref[...] = -ADAM_LR * (m_hat / (jnp.sqrt(v_hat) + ADAM_EPS) + ADAM_WD * w_ref[...])
    mo_ref[...] = mn
    vo_ref[...] = vn


def _adamw_small(name, ws, gs, ms, vs):
    n = len(ws)

    def body(*refs):
        for i in range(n):
            _adamw_update(*refs[i::n])

    shapes = [jax.ShapeDtypeStruct(a.shape, F32) for a in ws]
    outs = pl.pallas_call(body, name=name, out_shape=shapes * 3, in_specs=[VMEM_SPEC] * (4 * n),
                          out_specs=[VMEM_SPEC] * (3 * n), compiler_params=_cparams())(*ws, *gs, *ms, *vs)
    return outs[:n], outs[n:2 * n], outs[2 * n:]


def _ffn_fwd(tag, h, hn, p, i, plan):
    up_g, up_v, act = _ffn_up_conv(f"ffn{tag}_up", hn, plan.weight("f_w_up", i), p["f_conv_w"][i],
                                   p["f_conv_b"][i:i + 1], steps=plan.steps(f"ffn{tag}_up"))
    pre = _mm(f"ffn{tag}_down", act, plan.weight("f_w_down", i), "nn", steps=plan.steps(f"ffn{tag}_down"))
    return pre, (h, hn, up_g, up_v, act, pre)


def _ffn_bwd(tag, dpre, saved, p, i, plan):
    h, hn, up_g, up_v, act, pre = saved
    plan.grad("f_w_down", i, _mm(f"ffn{tag}_down_dw", act, dpre, "tn", out_dtype=BF16))
    dact = _mm(f"ffn{tag}_down_dx", dpre, plan.weight("f_w_down", i), "nt", steps=plan.steps(f"ffn{tag}_down_dx"))
    gwg, gwv, gbg, gbv, dhn, g_up = _ffn_conv_bwd(
        f"ffn{tag}_conv_bwd", up_g, up_v, dact, p["f_conv_w"][i], p["f_conv_b"][i:i + 1], hn,
        plan.weight("f_w_up", i), steps=plan.steps(f"ffn{tag}_conv_bwd"))
    g_cw, g_cb = jnp.concatenate([gwg, gwv], axis=1), jnp.concatenate([gbg, gbv], axis=1)
    plan.grad("f_w_up", i, g_up)
    return dhn, dict(f_conv_w=g_cw, f_conv_b=g_cb)


def _lanes_pad(a, width=LANES):
    return jnp.pad(a, [(0, 0)] * (a.ndim - 1) + [(0, width - a.shape[-1])])


def _dup_heads(w):
    rows = w.shape[0]
    w = w.reshape(rows, 2 * N_KV_HEADS, 1, HEAD_DIM)
    return jnp.broadcast_to(w, (rows, 2 * N_KV_HEADS, 2, HEAD_DIM)).reshape(rows, 4 * D_KV)


def _undup_heads(g):
    rows = g.shape[0]
    return g.reshape(rows, 2 * N_KV_HEADS, 2, HEAD_DIM).sum(axis=2).reshape(rows, 2 * D_KV)


def _local_step(x2, target, p, plan):
    seq = x2.shape[0]
    rows = seq + CHUNK
    g = {}

    h0 = jnp.concatenate([jnp.zeros((PAD_ROWS, D_MODEL), F32), p["meta_tokens"], x2], axis=0)

    w_in = plan.weight("a_w_in")
    w_dt = jnp.pad(w_in[D_MAIN:], ((0, LANES - SSM_HEADS), (0, 0)))
    dt_bias = _lanes_pad(p["a_dt_bias"])
    a128 = _lanes_pad(-jnp.exp(p["a_a_log"]))
    dskexp = jnp.repeat(p["a_d_skip"].reshape(SSM_HEADS), HEAD_DIM).reshape(1, D_INNER)

    hn0 = _rms_fwd("a_norm", h0, p["a_norm_pre"])
    zx = _mm("a_in_main", hn0, w_in, "nt", k_rows=D_MAIN, steps=plan.steps("a_in_main"))
    dtr = _mm("a_in_dt", hn0, w_dt, "nt")
    xbc = _conv4_fwd("a_conv", zx, p["a_conv_w"], p["a_conv_b"], steps=plan.steps("a_conv"))
    dt = _dt_fwd("a_dt", dtr, dt_bias)
    dt_exp, acs_exp, acs_rows = _ssd_prep("a_ssd_prep", dt, a128, steps=plan.steps("a_ssd_prep"))
    y, states = _ssd_fwd("a_ssd", xbc, dt_exp, acs_exp, acs_rows, dskexp, steps=plan.steps("a_ssd"))
    yn = _gate_fwd("a_gate", y, zx, p["a_gate_norm"], steps=plan.steps("a_gate"))
    mix = _mm("a_out", yn, plan.weight("a_w_out"), "nn", steps=plan.steps("a_out"))
    h1, (hn_f0,) = _resid_norm_fwd("a_resid", h0, mix, p["a_norm_post"], [p["f_norm_pre"][0:1]])

    pre_f0, ffn0 = _ffn_fwd("0", h1, hn_f0, p, 0, plan)
    h2, (hkv, hn2) = _resid_norm_fwd("ffn0_resid", h1, pre_f0, p["f_norm_post"][0:1], [p["kv_norm"], p["b_norm_pre"]])

    w_kv2 = _dup_heads(plan.weight("w_kv"))
    kv2 = _mm("kv_proj", hkv, w_kv2, "nn")
    q = _mm("b_q", hn2, plan.weight("b_w_q"), "nn")
    sinks = p["b_sinks"].reshape(N_Q_HEADS)
    o = _attn_fwd("b_attn", q, kv2, sinks, steps=plan.steps("b_attn"))
    attn = _mm("b_o", o, plan.weight("b_w_o"), "nn", steps=plan.steps("b_o"))
    h3, (hn_f1,) = _resid_norm_fwd("b_resid", h2, attn, p["b_norm_post"], [p["f_norm_pre"][1:2]])

    pre_f1, ffn1 = _ffn_fwd("1", h3, hn_f1, p, 1, plan)
    dh, loss_vec, dpre_f1, g_post1 = _resid_norm_loss("ffn1_resid_loss", h3, pre_f1, p["f_norm_post"][1:2], target)
    loss = loss_vec[0, 0]

    dhn_f1, g1 = _ffn_bwd("1", dpre_f1, ffn1, p, 1, plan)
    dh, g_pre1, dpre, g["b_norm_post"] = _norm_bwd_add("ffn1_norm_bwd", dh, dhn_f1, h3, p["f_norm_pre"][1:2],
                                                        then=(attn, p["b_norm_post"]))
    plan.grad("b_w_o", None, _mm("b_o_dw", o, dpre, "tn", out_dtype=BF16))
    do = _mm("b_o_dx", dpre, plan.weight("b_w_o"), "nt", steps=plan.steps("b_o_dx"))
    dq, dkc, dkp, dvc, dvp, dkm, dvm, dsink = _attn_bwd("b_attn_bwd", q, kv2, sinks, do, steps=plan.steps("b_attn_bwd"))
    g["b_sinks"] = dsink[:, :N_Q_HEADS]
    dhn2 = _mm("b_q_dx", dq, plan.weight("b_w_q"), "nt")
    plan.grad("b_w_q", None, _mm("b_q_dw", hn2, dq, "tn", out_dtype=BF16))
    dh, g["b_norm_pre"] = _norm_bwd_add("b_norm_bwd", dh, dhn2, h2, p["b_norm_pre"])
    dkv2 = _kv_grad_combine("kv_grad", dkc, dkp, dkm, dvc, dvp, dvm)
    dhkv = _mm("kv_proj_dx", dkv2, w_kv2, "nt")
    plan.grad("w_kv", None, _undup_heads(_mm("kv_proj_dw", hkv, dkv2, "tn")))
    dh, g["kv_norm"], dpre_f0, g_post0 = _norm_bwd_add("kv_norm_bwd", dh, dhkv, h2, p["kv_norm"],
                                                       then=(pre_f0, p["f_norm_post"][0:1]))

    dhn_f0, g0 = _ffn_bwd("0", dpre_f0, ffn0, p, 0, plan)
    dh, g_pre0, dpre, g["a_norm_post"] = _norm_bwd_add("ffn0_norm_bwd", dh, dhn_f0, h1, p["f_norm_pre"][0:1],
                                                        then=(mix, p["a_norm_post"]))
    g["f_norm_post"] = jnp.concatenate([g_post0, g_post1], axis=0)
    g["f_norm_pre"] = jnp.concatenate([g_pre0, g_pre1], axis=0)
    g["f_conv_w"] = jnp.stack([g0["f_conv_w"], g1["f_conv_w"]])
    g["f_conv_b"] = jnp.concatenate([g0["f_conv_b"], g1["f_conv_b"]], axis=0)
    plan.grad("a_w_out", None, _mm("a_out_dw", yn, dpre, "tn", out_dtype=BF16))
    dyn = _mm("a_out_dx", dpre, plan.weight("a_w_out"), "nt", steps=plan.steps("a_out_dx"))
    dy, dzx, g["a_gate_norm"] = _gate_bwd("a_gate_bwd", dyn, y, zx, p["a_gate_norm"])
    dxbc, ddt, dalog, ddsk = _ssd_bwd("a_ssd_bwd", xbc, dt_exp, acs_exp, acs_rows, dt, a128, dskexp, dy, states,
                                      steps=plan.steps("a_ssd_bwd"))
    g["a_a_log"] = dalog[:, :SSM_HEADS]
    g["a_d_skip"] = ddsk[:, :SSM_HEADS]
    ddtr, dbias = _dt_bwd("a_dt_bwd", ddt, dtr, dt_bias)
    g["a_dt_bias"] = dbias[:, :SSM_HEADS]
    dzx, g["a_conv_w"], g["a_conv_b"] = _conv4_bwd("a_conv_bwd", zx, dxbc, p["a_conv_w"], p["a_conv_b"], dzx)
    g_in = _mm("a_in_main_dw", dzx, hn0, "tn", out_dtype=BF16, out_rows=D_IN_PROJ, steps=plan.steps("a_in_main_dw"))
    plan.grad("a_w_in", None, _tn_rows_into("a_in_dt_dw", ddtr, hn0, g_in, D_MAIN, SSM_HEADS))
    dhn0 = _mm("a_in_dt_dx", ddtr, w_dt, "nn", steps=plan.steps("a_in_dt_dx"))
    dhn0 = _mm("a_in_main_dx", dzx, w_in, "nn", acc=dhn0, steps=plan.steps("a_in_main_dx"))
    dh_first, grad_x, g["a_norm_pre"] = _norm_bwd_add("a_norm_bwd", dh, dhn0, h0, p["a_norm_pre"],
                                                      split_first_block=True, steps=plan.steps("a_norm_bwd"))
    g["meta_tokens"] = dh_first[PAD_ROWS:]
    return loss, grad_x, g


ANY = pl.BlockSpec(memory_space=pl.ANY)
VMEM_SPEC = pl.BlockSpec(memory_space=pltpu.VMEM)


def _step_gather_small(slots):
    def copies(outs, send_sems, recv_sems, received):
        x, y, c = _place()
        me = 2 * x + y
        for j, (cx, cy) in enumerate(_other_chips(x, y)):
            slot = outs[0].at[2 * cx + cy if received else me]
            yield _remote(slot, slot, send_sems, recv_sems, j, (cx, cy, c))

    def start(ins, outs, send_sems, recv_sems):
        for cp in copies(outs, send_sems, recv_sems, False):
            cp.start()

    def finish(ins, outs, send_sems, recv_sems):
        for cp in copies(outs, send_sems, recv_sems, True):
            cp.wait_recv()
        for cp in copies(outs, send_sems, recv_sems, False):
            cp.wait_send()

    return _Step([slots], [_like(slots)], {0: 0}, 3, start, finish)


def _row_block(rows, width, itemsize, align, budget=2 << 20):
    best = rows
    for cand in range(align, rows + 1, align):
        if rows % cand == 0 and cand * width * itemsize <= budget:
            best = cand
    return best


def _cast_into_slot(name, chip, w, layer=None):
    rows, width = w.shape[-2:]
    tr = _row_block(rows, width, 4, 16)
    if layer is None:
        in_spec = pl.BlockSpec((tr, width), lambda i, chip_ref: (i, 0))
    else:
        in_spec = pl.BlockSpec((None, tr, width), lambda i, chip_ref: (layer, i, 0))

    def body(chip_ref, w_ref, o_ref):
        o_ref[...] = w_ref[...].astype(BF16)

    return pl.pallas_call(
        body, name=name, out_shape=jax.ShapeDtypeStruct((N_CHIPS, rows, width), BF16),
        grid_spec=pltpu.PrefetchScalarGridSpec(
            num_scalar_prefetch=1, grid=(rows // tr,), in_specs=[in_spec],
            out_specs=pl.BlockSpec((None, tr, width), lambda i, chip_ref: (chip_ref[0], i, 0))),
        compiler_params=_cparams(("parallel",)),
    )(chip, w)


def _allreduce_small(name, vec):
    rows = -(-vec.shape[0] // (2 * SUBLANES)) * (2 * SUBLANES)
    hr = rows // 2
    padded = jnp.pad(vec, ((0, rows - vec.shape[0]), (0, 0)))

    def body(v_ref, o_ref, theirs, pair, by_chip, send_sems, recv_sems):
        x, y, c = _place()
        me = 2 * x + y
        sibling = (x, y, 1 - c)
        mine = pl.ds(pl.multiple_of(c * hr, SUBLANES), hr)
        other = pl.ds(pl.multiple_of((1 - c) * hr, SUBLANES), hr)

        swap = _remote(v_ref, theirs, send_sems, recv_sems, 0, sibling)
        swap.start()
        swap.wait()
        south = (c + jnp.zeros((1, 1), jnp.int32)) == 0
        pair[...] = jnp.where(south, v_ref[...], theirs[...]) + jnp.where(south, theirs[...], v_ref[...])

        by_chip[me] = pair[mine, :]
        sends = [_remote(by_chip.at[me], by_chip.at[me], send_sems, recv_sems, 1 + j, (cx, cy, c))
                 for j, (cx, cy) in enumerate(_other_chips(x, y))]
        for cp in sends:
            cp.start()
        for j, (cx, cy) in enumerate(_other_chips(x, y)):
            _remote(by_chip.at[me], by_chip.at[2 * cx + cy], send_sems, recv_sems, 1 + j, (cx, cy, c)).wait_recv()
        for cp in sends:
            cp.wait_send()
        total = by_chip[0]
        for s in range(1, N_CHIPS):
            total = total + by_chip[s]

        o_ref[mine, :] = total
        back = _remote(o_ref.at[mine], o_ref.at[mine], send_sems, recv_sems, 4, sibling)
        back.start()
        _remote(o_ref.at[other], o_ref.at[other], send_sems, recv_sems, 4, sibling).wait_recv()
        back.wait_send()

    out = pl.pallas_call(
        body, name=name, out_shape=jax.ShapeDtypeStruct((rows, LANES), F32),
        in_specs=[VMEM_SPEC], out_specs=VMEM_SPEC,
        scratch_shapes=[pltpu.VMEM((rows, LANES), F32), pltpu.VMEM((rows, LANES), F32),
                        pltpu.VMEM((N_CHIPS, hr, LANES), F32), pltpu.SemaphoreType.DMA((5,)),
                        pltpu.SemaphoreType.DMA((5,))],
        compiler_params=pltpu.CompilerParams(vmem_limit_bytes=VMEM_LIMIT),
    )(padded)
    return out[:vec.shape[0]]


def _rs_pair_add(name, place, grads, partner, split="rows"):
    _, half_rows, width = partner.shape
    tr = _row_block(half_rows, width, 2, 16)
    nb = half_rows // tr
    if split == "rows":
        mine = pl.BlockSpec((None, tr, width), lambda s, i, pr: (s, pr[1] * nb + i, 0))
    else:
        mine = pl.BlockSpec((None, tr, width), lambda s, i, pr: (s, i, pr[1]))

    def body(place_ref, g_ref, p_ref, o_ref):
        o_ref[...] = (g_ref[...].astype(F32) + p_ref[...].astype(F32)).astype(BF16)

    return pl.pallas_call(
        body, name=name, out_shape=jax.ShapeDtypeStruct(partner.shape, BF16),
        grid_spec=pltpu.PrefetchScalarGridSpec(
            num_scalar_prefetch=1, grid=(N_CHIPS, nb),
            in_specs=[mine, pl.BlockSpec((None, tr, width), lambda s, i, pr: (s, i, 0))],
            out_specs=pl.BlockSpec((None, tr, width), lambda s, i, pr: (s, i, 0))),
        compiler_params=_cparams(("parallel", "parallel")),
    )(place, grads, partner)


def _rs_chip_add(name, place, mine, others, split="rows"):
    _, half_rows, width = mine.shape
    tr = _row_block(half_rows, width, 4, 16, budget=1 << 20)
    nb = half_rows // tr
    if split == "rows":
        out_shape, out_spec = (2 * half_rows, width), pl.BlockSpec((tr, width), lambda i, pr: (pr[1] * nb + i, 0))
    else:
        out_shape, out_spec = (half_rows, 2 * width), pl.BlockSpec((tr, width), lambda i, pr: (i, pr[1]))

    def body(place_ref, q_ref, r_ref, o_ref):
        acc = q_ref[...].astype(F32)
        for j in range(3):
            acc = acc + r_ref[j].astype(F32)
        o_ref[...] = acc

    return pl.pallas_call(
        body, name=name, out_shape=jax.ShapeDtypeStruct(out_shape, F32),
        grid_spec=pltpu.PrefetchScalarGridSpec(
            num_scalar_prefetch=1, grid=(nb,),
            in_specs=[pl.BlockSpec((None, tr, width), lambda i, pr: (pr[0], i, 0)),
                      pl.BlockSpec((3, tr, width), lambda i, pr: (0, i, 0))],
            out_specs=out_spec),
        compiler_params=_cparams(("parallel",)),
    )(place, mine, others)


WEIGHTS = ["meta_tokens", "a_norm_pre", "a_w_in", "a_conv_w", "a_conv_b", "a_dt_bias", "a_a_log", "a_d_skip",
           "a_gate_norm", "a_w_out", "a_norm_post", "kv_norm", "w_kv", "b_norm_pre", "b_w_q", "b_sinks", "b_w_o",
           "b_norm_post", "f_norm_pre", "f_w_up", "f_conv_w", "f_conv_b", "f_w_down", "f_norm_post"]
FULL_SHAPE = {
    "meta_tokens": (16, 1024), "a_norm_pre": (1, 1024), "a_w_in": (1, 1024, 5152), "a_conv_w": (1, 4, 3072),
    "a_conv_b": (1, 3072), "a_dt_bias": (1, 32), "a_a_log": (1, 32), "a_d_skip": (1, 32), "a_gate_norm": (1, 2048),
    "a_w_out": (1, 2048, 1024), "a_norm_post": (1, 1024), "kv_norm": (1024,), "w_kv": (1024, 512),
    "b_norm_pre": (1, 1024), "b_w_q": (1, 1024, 1024), "b_sinks": (1, 16), "b_w_o": (1, 1024, 1024),
    "b_norm_post": (1, 1024), "f_norm_pre": (2, 1024), "f_w_up": (2, 1024, 5632), "f_conv_w": (2, 3, 5632),
    "f_conv_b": (2, 5632), "f_w_down": (2, 2816, 1024), "f_norm_post": (2, 1024),
}
SHARD_AXIS = {
    "meta_tokens": 1, "a_norm_pre": 1, "a_w_in": 2, "a_conv_w": 2, "a_conv_b": 1, "a_dt_bias": None, "a_a_log": None,
    "a_d_skip": None, "a_gate_norm": 1, "a_w_out": 1, "a_norm_post": 1, "kv_norm": None, "w_kv": 0, "b_norm_pre": None,
    "b_w_q": 1, "b_sinks": None, "b_w_o": 1, "b_norm_post": None, "f_norm_pre": None, "f_w_up": 2, "f_conv_w": 2,
    "f_conv_b": None, "f_w_down": 1, "f_norm_post": None,
}
BIG = ["a_w_in", "a_w_out", "w_kv", "b_w_q", "b_w_o", "f_w_up", "f_w_down"]
SMALL = [n for n in WEIGHTS if n not in BIG]
SMALL_SHARDED = [n for n in SMALL if SHARD_AXIS[n] is not None]


def _shard_shape(name):
    shape = list(FULL_SHAPE[name])
    if SHARD_AXIS[name] is not None:
        shape[SHARD_AXIS[name]] //= N_CHIPS
    return tuple(shape)


def _numel(shape):
    return int(math.prod(shape))


SUBLANES = 8


def _packed_rows(shape):
    rows = -(-_numel(shape) // LANES)
    return -(-rows // SUBLANES) * SUBLANES


def _pack(arrays):
    parts = []
    for a in arrays:
        size, rows = _numel(a.shape), _packed_rows(a.shape)
        if size % LANES == 0:
            part = jnp.pad(a.reshape(size // LANES, LANES), ((0, rows - size // LANES), (0, 0)))
        else:
            part = jnp.pad(a.reshape(-1), (0, rows * LANES - size)).reshape(rows, LANES)
        parts.append(part)
    return jnp.concatenate(parts, axis=0)


def _unpack(packed, names, shape_of):
    out, off = {}, 0
    lead = packed.shape[:-2]
    for n in names:
        shape = tuple(shape_of(n))
        size, rows = _numel(shape), _packed_rows(shape)
        part = packed[..., off:off + rows, :]
        if size % LANES == 0:
            out[n] = part[..., :size // LANES, :].reshape(lead + shape)
        else:
            out[n] = part.reshape(lead + (rows * LANES,))[..., :size].reshape(lead + shape)
        off += rows
    return out


def _split_chips(name, full):
    ax = SHARD_AXIS[name]
    shape = full.shape
    cut = shape[:ax] + (N_CHIPS, shape[ax] // N_CHIPS) + shape[ax + 1:]
    return jnp.moveaxis(full.reshape(cut), ax, 0)


def _join_chips(name, stacked):
    ax = SHARD_AXIS[name]
    moved = jnp.moveaxis(stacked, 0, ax)
    shape = moved.shape
    return moved.reshape(shape[:ax] + (shape[ax] * shape[ax + 1],) + shape[ax + 2:])


def _as2d(a):
    return a.reshape(-1, a.shape[-1])


BUFFERS = [("a_w_in", "a_w_in", None), ("a_w_out", "a_w_out", None), ("w_kv", "w_kv", None),
           ("b_w_q", "b_w_q", None), ("b_w_o", "b_w_o", None), ("f_w_up0", "f_w_up", 0), ("f_w_up1", "f_w_up", 1),
           ("f_w_down0", "f_w_down", 0), ("f_w_down1", "f_w_down", 1)]


TRANSPOSED = ("a_w_in",)
SPLIT = {"a_w_in": "cols"}


def _local_shard(arrays, weight, layer):
    if weight in TRANSPOSED:
        return arrays[weight][0].T
    return _as2d(arrays[weight]) if layer is None else arrays[weight]


def _weight_from_gathered(weight, buf):
    if weight == "f_w_up":
        return buf
    return buf.reshape(N_CHIPS * buf.shape[1], buf.shape[2])


def _gathered_from_grad(weight, g):
    if weight == "f_w_up":
        return g
    return g.reshape(N_CHIPS, g.shape[0] // N_CHIPS, g.shape[1]).astype(BF16)


GATHER_SCHEDULE = {
    "a_in_main": [("ici", ["a_w_out"])],
    "a_conv": [("d2d", ["a_w_out"]), ("ici", ["f_w_down0"])],
    "a_ssd_prep": [("d2d", ["f_w_down0"]), ("ici_near", ["f_w_up0"])],
    "a_ssd": [("ici_far", ["f_w_up0"])],
    "a_gate": [("d2d", ["f_w_up0"]), ("ici", ["w_kv", "b_w_q", "b_w_o"])],
    "ffn0_up": [("d2d", ["w_kv", "b_w_q", "b_w_o"]), ("ici", ["f_w_down1"])],
    "ffn0_down": [("d2d", ["f_w_down1"])],
    "b_attn": [("ici", ["f_w_up1"])],
    "b_o": [("d2d", ["f_w_up1"])],
}
REDUCE_SCHEDULE = {
    "b_attn_bwd": [("all", ["f_w_down1", "f_w_up1", "b_w_o"])],
    "ffn0_conv_bwd": [("all", ["b_w_q", "w_kv", "f_w_down0"])],
    "a_ssd_bwd": [("all", ["f_w_up0", "a_w_out"])],
    "a_in_main_dx": [("near", ["a_w_in"])],
    "a_norm_bwd": [("far", ["a_w_in"])],
}
REDUCE_LAST = ("a_w_in",)
ICI_PEERS = {"ici": ALL_PEERS, "ici_near": NEAR_PEERS, "ici_far": FAR_PEERS,
             "all": ALL_PEERS, "near": NEAR_PEERS, "far": FAR_PEERS}
PAIR_SCHEDULE = {
    "b_o_dx": ["f_w_down1", "f_w_up1", "b_w_o"],
    "ffn0_down_dx": ["b_w_q", "w_kv", "f_w_down0"],
    "a_out_dx": ["f_w_up0", "a_w_out"],
    "a_in_dt_dx": ["a_w_in"],
}
SWAP_SCHEDULE = {"a_in_main_dw": ["f_w_down1", "f_w_up1", "b_w_o", "b_w_q", "w_kv", "f_w_down0", "f_w_up0", "a_w_out"]}


def _buffer_of(weight, layer):
    return weight if layer is None else f"{weight}{layer}"


class _Pipeline:
    def __init__(self, place, slots):
        self.place = place
        self.slots = dict(slots)
        self.running = []
        self.grads = {}
        self.theirs = {}
        self.partials = {}
        self.peers = {}
        self.reduced = {}

    def _collect(self):
        for step, buffers, table in self.running:
            table.update(zip(buffers, step.results))
        self.running = []

    @staticmethod
    def _splits(buffers):
        return [SPLIT.get(b, "rows") for b in buffers]

    def gather_now(self, name, buffers, also=()):
        step = _step_gather_full([self.slots[b] for b in buffers], self._splits(buffers))
        _run_steps(name, [step, *also])
        self.slots.update(zip(buffers, step.results))

    def weight(self, name, layer=None):
        self._collect()
        return _weight_from_gathered(name, self.slots[_buffer_of(name, layer)])

    def grad(self, name, layer, g):
        self.grads[_buffer_of(name, layer)] = _gathered_from_grad(name, g)

    def steps(self, kernel):
        self._collect()
        steps = []
        for phase, buffers in GATHER_SCHEDULE.get(kernel, []):
            bufs, splits = [self.slots[b] for b in buffers], self._splits(buffers)
            step = (_step_gather_d2d(bufs, splits) if phase == "d2d"
                    else _step_gather_ici(bufs, splits, ICI_PEERS[phase]))
            self.running.append((step, buffers, self.slots))
            steps.append(step)
        buffers = PAIR_SCHEDULE.get(kernel)
        if buffers:
            step = _step_pair_exchange([self.grads[b] for b in buffers], self._splits(buffers))
            self.running.append((step, buffers, self.theirs))
            steps.append(step)
        for part, buffers in REDUCE_SCHEDULE.get(kernel, []):
            for b in buffers:
                if b not in self.partials:
                    self.partials[b] = _rs_pair_add("reduce_pair_add_" + b, self.place, self.grads[b], self.theirs[b],
                                                    SPLIT.get(b, "rows"))
            started = [self.peers[b] for b in buffers] if all(b in self.peers for b in buffers) else None
            step = _step_chip_exchange([self.partials[b] for b in buffers], ICI_PEERS[part], into=started)
            self.running.append((step, buffers, self.peers))
            steps.append(step)
        buffers = SWAP_SCHEDULE.get(kernel)
        if buffers:
            step = self._swap_step(buffers)
            self.running.append((step, buffers, self.reduced))
            steps.append(step)
        return steps

    def _swap_step(self, buffers):
        halves = [_rs_chip_add("reduce_chip_add_" + b, self.place, self.partials[b], self.peers[b], SPLIT.get(b, "rows"))
                  for b in buffers]
        return _step_pair_gather(halves, self._splits(buffers))

    def shard(self, buffer):
        self._collect()
        return self.reduced[buffer]

    def finish(self):
        self._collect()
        rest = [b for b, _, _ in BUFFERS if b not in self.reduced]
        step = self._swap_step(rest)
        _run_steps("reduce_pair_gather", [step])
        self.reduced.update(zip(rest, step.results))


def kernel(x, meta_tokens, a_norm_pre, a_w_in, a_conv_w, a_conv_b, a_dt_bias, a_a_log, a_d_skip, a_gate_norm, a_w_out, a_norm_post, kv_norm, w_kv, b_norm_pre, b_w_q, b_sinks, b_w_o, b_norm_post, f_norm_pre, f_w_up, f_conv_w, f_conv_b, f_w_down, f_norm_post, loss_target, m_meta_tokens, m_a_norm_pre, m_a_w_in, m_a_conv_w, m_a_conv_b, m_a_dt_bias, m_a_a_log, m_a_d_skip, m_a_gate_norm, m_a_w_out, m_a_norm_post, m_kv_norm, m_w_kv, m_b_norm_pre, m_b_w_q, m_b_sinks, m_b_w_o, m_b_norm_post, m_f_norm_pre, m_f_w_up, m_f_conv_w, m_f_conv_b, m_f_w_down, m_f_norm_post, v_meta_tokens, v_a_norm_pre, v_a_w_in, v_a_conv_w, v_a_conv_b, v_a_dt_bias, v_a_a_log, v_a_d_skip, v_a_gate_norm, v_a_w_out, v_a_norm_post, v_kv_norm, v_w_kv, v_b_norm_pre, v_b_w_q, v_b_sinks, v_b_w_o, v_b_norm_post, v_f_norm_pre, v_f_w_up, v_f_conv_w, v_f_conv_b, v_f_w_down, v_f_norm_post):
    given = dict(locals())
    w = {n: given[n] for n in WEIGHTS}
    mom = {n: given["m_" + n] for n in WEIGHTS}
    var = {n: given["v_" + n] for n in WEIGHTS}
    chip = 2 * lax.axis_index("x") + lax.axis_index("y")
    core = lax.axis_index("c")
    place = jnp.stack([chip, core]).astype(jnp.int32)

    small_mine = _pack([w[n] for n in SMALL_SHARDED])
    small_step = _step_gather_small(lax.dynamic_update_slice(
        jnp.zeros((N_CHIPS,) + small_mine.shape, F32), small_mine[None], (chip, 0, 0)))
    slots = {b: _cast_into_slot("cast_" + b, place, _local_shard(w, wn, layer), layer) for b, wn, layer in BUFFERS}
    pipeline = _Pipeline(place, slots)
    pipeline.gather_now("gather_first", ["a_w_in"], also=[small_step])
    small_parts = _unpack(small_step.results[0], SMALL_SHARDED, _shard_shape)
    p = {}
    for n in SMALL:
        p[n] = _join_chips(n, small_parts[n]) if n in SMALL_SHARDED else w[n]
    p["a_conv_w"] = p["a_conv_w"][0]
    p["kv_norm"] = p["kv_norm"].reshape(1, D_MODEL)

    loss_local, grad_x, g = _local_step(x[0], loss_target[0], p, pipeline)

    small_sum = _allreduce_small("reduce_small", _pack([g[n].reshape(FULL_SHAPE[n]) for n in SMALL]
                                                       + [loss_local.reshape(1, 1)]))
    small_red = _unpack(small_sum, SMALL + ["loss"], lambda n: (1, 1) if n == "loss" else FULL_SHAPE[n])
    loss = small_red["loss"][0, 0]
    grads = {}
    for n in SMALL:
        if SHARD_AXIS[n] is None:
            grads[n] = small_red[n]
        else:
            grads[n] = lax.dynamic_index_in_dim(_split_chips(n, small_red[n]), chip, 0, keepdims=False)

    delta, new_m, new_v = {}, {}, {}
    for n in sorted(BIG, key=lambda name: name in REDUCE_LAST):
        shape = _shard_shape(n)
        if n in REDUCE_LAST:
            pipeline.finish()
        if n in TRANSPOSED:
            g2d = pipeline.shard(n)
            w2d, m2d, v2d = (arrays[n][0].T for arrays in (w, mom, var))
            back = lambda a: a.T.reshape(shape)
        else:
            g2d = (jnp.concatenate([pipeline.shard(n + "0"), pipeline.shard(n + "1")], axis=0)
                   if n in ("f_w_up", "f_w_down") else pipeline.shard(n))
            w2d, m2d, v2d = (_as2d(arrays[n]) for arrays in (w, mom, var))
            back = lambda a: a.reshape(shape)
        d, m2, v2 = _adamw("adamw_" + n, w2d, g2d, m2d, v2d, steps=pipeline.steps("adamw_" + n))
        grads[n], delta[n], new_m[n], new_v[n] = back(g2d), back(d), back(m2), back(v2)
    at_least_2d = lambda n: (1,) * (2 - len(_shard_shape(n))) + _shard_shape(n)
    outs = _adamw_small("adamw_small", *[[src[n].reshape(at_least_2d(n)) for n in SMALL] for src in (w, grads, mom, var)])
    for dst, arrays in zip((delta, new_m, new_v), outs):
        dst.update({n: a.reshape(_shard_shape(n)) for n, a in zip(SMALL, arrays)})

    return (loss, grad_x[None], *[grads[n].reshape(_shard_shape(n)) for n in WEIGHTS],
            *[delta[n] for n in WEIGHTS], *[new_m[n] for n in WEIGHTS], *[new_v[n] for n in WEIGHTS])
```

```python
import functools
import math

import jax
import jax.numpy as jnp
from jax import lax
from jax.experimental import pallas as pl
from jax.experimental.pallas import tpu as pltpu

F32, BF16 = jnp.float32, jnp.bfloat16
MESH = pl.DeviceIdType.MESH

D_MODEL = 1024
N_META = 16
CHUNK = 128
PAD_ROWS = CHUNK - N_META
D_INNER = 2048
D_STATE = 128
N_GROUPS = 4
HEADS_PER_GROUP = 8
SSM_HEADS = 32
HEAD_DIM = 64
D_BC = N_GROUPS * D_STATE
D_XBC = D_INNER + 2 * D_BC
D_MAIN = D_INNER + D_XBC
D_IN_PROJ = D_MAIN + SSM_HEADS
GROUP_W = HEADS_PER_GROUP * HEAD_DIM
SSM_CONV = 4
D_FF = 2816
FFN_CONV = 3
N_Q_HEADS = 16
N_KV_HEADS = 4
D_KV = 256
ATTN_SCALE = 1.0 / math.sqrt(HEAD_DIM)
RMS_EPS = 1e-6
NEG_INF = -1e30
LANES = 128
VMEM_LIMIT = 51 * 1024 * 1024

ADAM_LR, ADAM_B1, ADAM_B2, ADAM_EPS, ADAM_WD, ADAM_STEP = 0.001, 0.9, 0.999, 1e-08, 0.01, 10

N_CHIPS = 4


def _cparams(sem=None):
    return pltpu.CompilerParams(dimension_semantics=sem, vmem_limit_bytes=VMEM_LIMIT)


def _tile(n, cands=(512, 256, 128)):
    for t in cands:
        if n % t == 0:
            return t
    return n


def _row_tile(rows, width):
    for t in (544, 272):
        if rows % t == 0 and t * width * 4 <= (3 << 20):
            return t
    return 128


def _rows_mask(i, tm):
    rows = i * tm + lax.broadcasted_iota(jnp.int32, (tm, 1), 0)
    return rows >= PAD_ROWS


def _dot(a, b):
    return jnp.dot(a, b, preferred_element_type=F32)


def _dot_nt(a, b):
    return lax.dot_general(a, b, (((1,), (1,)), ((), ())), preferred_element_type=F32)


def _dot_tn(a, b):
    return lax.dot_general(a, b, (((0,), (0,)), ((), ())), preferred_element_type=F32)


def _sigmoid(x):
    return 1.0 / (1.0 + jnp.exp(-x))


def _place():
    return lax.axis_index("x"), lax.axis_index("y"), lax.axis_index("c")


def _other_chips(x, y):
    return [(1 - x, y), (x, 1 - y), (1 - x, 1 - y)]


class _Step:
    def __init__(self, ins, outs, aliases, n_sems, start, finish):
        self.ins, self.outs, self.aliases, self.n_sems = list(ins), list(outs), dict(aliases), n_sems
        self.start, self.finish = start, finish
        self.results = None


def _like(a):
    return jax.ShapeDtypeStruct(a.shape, a.dtype)


def _remote(src, dst, send_sems, recv_sems, k, device):
    return pltpu.make_async_remote_copy(src, dst, send_sems.at[k], recv_sems.at[k], device_id=device, device_id_type=MESH)


def _half(ref, split, which, lead=()):
    if split == "rows":
        hr = ref.shape[-2] // 2
        return ref.at[lead + (pl.ds(which * hr, hr),)]
    hc = ref.shape[-1] // 2
    return ref.at[lead + (slice(None), pl.ds(which * hc, hc))]


def _splits(bufs, splits):
    return list(splits) if splits is not None else ["rows"] * len(bufs)


ALL_PEERS = (0, 1, 2)
NEAR_PEERS = (0, 1)
FAR_PEERS = (2,)


def _step_gather_ici(bufs, splits=None, peers=ALL_PEERS):
    splits = _splits(bufs, splits)

    def copies(outs, send_sems, recv_sems, received):
        x, y, c = _place()
        me = 2 * x + y
        for k, o in enumerate(outs):
            for j, (cx, cy) in enumerate(_other_chips(x, y)):
                if j in peers:
                    part = _half(o, splits[k], c, (2 * cx + cy if received else me,))
                    yield _remote(part, part, send_sems, recv_sems, 3 * k + j, (cx, cy, c))

    def start(ins, outs, send_sems, recv_sems):
        for cp in copies(outs, send_sems, recv_sems, False):
            cp.start()

    def finish(ins, outs, send_sems, recv_sems):
        for cp in copies(outs, send_sems, recv_sems, True):
            cp.wait_recv()
        for cp in copies(outs, send_sems, recv_sems, False):
            cp.wait_send()

    return _Step(bufs, [_like(b) for b in bufs], {k: k for k in range(len(bufs))}, 3 * len(bufs), start, finish)


def _step_gather_d2d(bufs, splits=None):
    splits = _splits(bufs, splits)

    def copies(outs, send_sems, recv_sems, received):
        x, y, c = _place()
        for k, o in enumerate(outs):
            for j, (cx, cy) in enumerate(_other_chips(x, y)):
                part = _half(o, splits[k], 1 - c if received else c, (2 * cx + cy,))
                yield _remote(part, part, send_sems, recv_sems, 3 * k + j, (x, y, 1 - c))

    def start(ins, outs, send_sems, recv_sems):
        for cp in copies(outs, send_sems, recv_sems, False):
            cp.start()

    def finish(ins, outs, send_sems, recv_sems):
        for cp in copies(outs, send_sems, recv_sems, True):
            cp.wait_recv()
        for cp in copies(outs, send_sems, recv_sems, False):
            cp.wait_send()

    return _Step(bufs, [_like(b) for b in bufs], {k: k for k in range(len(bufs))}, 3 * len(bufs), start, finish)


def _step_gather_full(bufs, splits=None):
    n = len(bufs)
    splits = _splits(bufs, splits)

    def ici(outs, send_sems, recv_sems, received):
        x, y, c = _place()
        me = 2 * x + y
        for k, o in enumerate(outs):
            for j, (cx, cy) in enumerate(_other_chips(x, y)):
                part = _half(o, splits[k], c, (2 * cx + cy if received else me,))
                yield _remote(part, part, send_sems, recv_sems, 3 * k + j, (cx, cy, c))

    def d2d(outs, send_sems, recv_sems, received):
        x, y, c = _place()
        for k, o in enumerate(outs):
            for j, (cx, cy) in enumerate(_other_chips(x, y)):
                part = _half(o, splits[k], 1 - c if received else c, (2 * cx + cy,))
                yield _remote(part, part, send_sems, recv_sems, 3 * n + 3 * k + j, (x, y, 1 - c))

    def start(ins, outs, send_sems, recv_sems):
        for cp in ici(outs, send_sems, recv_sems, False):
            cp.start()

    def finish(ins, outs, send_sems, recv_sems):
        for arrived, onward in zip(ici(outs, send_sems, recv_sems, True), d2d(outs, send_sems, recv_sems, False)):
            arrived.wait_recv()
            onward.start()
        for cp in d2d(outs, send_sems, recv_sems, True):
            cp.wait_recv()
        for cp in ici(outs, send_sems, recv_sems, False):
            cp.wait_send()
        for cp in d2d(outs, send_sems, recv_sems, False):
            cp.wait_send()

    return _Step(bufs, [_like(b) for b in bufs], {k: k for k in range(n)}, 6 * n, start, finish)


def _half_shape(shape, split):
    return shape[:-2] + ((shape[-2] // 2, shape[-1]) if split == "rows" else (shape[-2], shape[-1] // 2))


def _step_pair_exchange(grads, splits=None):
    splits = _splits(grads, splits)

    def copies(ins, outs, send_sems, recv_sems):
        x, y, c = _place()
        for k, (g, o) in enumerate(zip(ins, outs)):
            yield _remote(_half(g, splits[k], 1 - c, (slice(None),)), o, send_sems, recv_sems, k, (x, y, 1 - c))

    def start(ins, outs, send_sems, recv_sems):
        for cp in copies(ins, outs, send_sems, recv_sems):
            cp.start()

    def finish(ins, outs, send_sems, recv_sems):
        for cp in copies(ins, outs, send_sems, recv_sems):
            cp.wait()

    outs = [jax.ShapeDtypeStruct(_half_shape(g.shape, s), g.dtype) for g, s in zip(grads, splits)]
    return _Step(grads, outs, {}, len(grads), start, finish)


def _step_chip_exchange(partials, peers=ALL_PEERS, into=None):
    n = len(partials)

    def copies(ins, outs, send_sems, recv_sems):
        x, y, c = _place()
        for k, (q, o) in enumerate(zip(ins[:n], outs)):
            for j, (cx, cy) in enumerate(_other_chips(x, y)):
                if j in peers:
                    yield _remote(q.at[2 * cx + cy], o.at[j], send_sems, recv_sems, 3 * k + j, (cx, cy, c))

    def start(ins, outs, send_sems, recv_sems):
        for cp in copies(ins, outs, send_sems, recv_sems):
            cp.start()

    def finish(ins, outs, send_sems, recv_sems):
        for cp in copies(ins, outs, send_sems, recv_sems):
            cp.wait()

    outs = [jax.ShapeDtypeStruct((3,) + q.shape[1:], q.dtype) for q in partials]
    if into is None:
        return _Step(partials, outs, {}, 3 * n, start, finish)
    return _Step(list(partials) + list(into), outs, {n + k: k for k in range(n)}, 3 * n, start, finish)


def _step_pair_gather(shards, splits=None):
    splits = _splits(shards, splits)

    def copies(outs, send_sems, recv_sems, received):
        x, y, c = _place()
        for k, o in enumerate(outs):
            part = _half(o, splits[k], 1 - c if received else c)
            yield _remote(part, part, send_sems, recv_sems, k, (x, y, 1 - c))

    def start(ins, outs, send_sems, recv_sems):
        for cp in copies(outs, send_sems, recv_sems, False):
            cp.start()

    def finish(ins, outs, send_sems, recv_sems):
        for cp in copies(outs, send_sems, recv_sems, True):
            cp.wait_recv()
        for cp in copies(outs, send_sems, recv_sems, False):
            cp.wait_send()

    return _Step(shards, [_like(s) for s in shards], {k: k for k in range(len(shards))}, len(shards), start, finish)


def _call(body, *, name, out_shape, grid, in_specs, out_specs, operands, scratch_shapes=(), semantics=None, steps=()):
    single = not isinstance(out_shape, (tuple, list))
    out_shapes = [out_shape] if single else list(out_shape)
    out_spec_list = [out_specs] if single else list(out_specs)
    steps = list(steps)
    if not steps:
        res = pl.pallas_call(body, name=name, out_shape=out_shapes, grid=grid, in_specs=list(in_specs),
                             out_specs=out_spec_list, scratch_shapes=list(scratch_shapes),
                             compiler_params=_cparams(semantics))(*operands)
        return res[0] if single else res
    n_in, n_out, n_scr = len(operands), len(out_shapes), len(scratch_shapes)
    x_in = [a for s in steps for a in s.ins]
    x_out = [o for s in steps for o in s.outs]
    aliases, in_off, out_off = {}, 0, 0
    for s in steps:
        for i, o in s.aliases.items():
            aliases[n_in + in_off + i] = n_out + out_off + o
        in_off += len(s.ins)
        out_off += len(s.outs)
    sems = []
    for s in steps:
        sems += [pltpu.SemaphoreType.DMA((s.n_sems,)), pltpu.SemaphoreType.DMA((s.n_sems,))]
    any_spec = pl.BlockSpec(memory_space=pl.ANY)

    def carried(*refs):
        pos = 0
        ins = refs[pos:pos + n_in]; pos += n_in
        xi = refs[pos:pos + len(x_in)]; pos += len(x_in)
        outs = refs[pos:pos + n_out]; pos += n_out
        xo = refs[pos:pos + len(x_out)]; pos += len(x_out)
        scr = refs[pos:pos + n_scr]; pos += n_scr
        sem_refs = refs[pos:]

        def each(action):
            i0 = o0 = 0
            for k, s in enumerate(steps):
                getattr(s, action)(xi[i0:i0 + len(s.ins)], xo[o0:o0 + len(s.outs)], sem_refs[2 * k], sem_refs[2 * k + 1])
                i0 += len(s.ins)
                o0 += len(s.outs)

        if grid:
            first = functools.reduce(jnp.logical_and, [pl.program_id(d) == 0 for d in range(len(grid))])
            last = functools.reduce(jnp.logical_and, [pl.program_id(d) == grid[d] - 1 for d in range(len(grid))])
            pl.when(first)(lambda: each("start"))
            body(*ins, *outs, *scr)
            pl.when(last)(lambda: each("finish"))
        else:
            each("start")
            body(*ins, *outs, *scr)
            each("finish")

    res = pl.pallas_call(
        carried, name=name, out_shape=out_shapes + x_out, grid=grid,
        in_specs=list(in_specs) + [any_spec] * len(x_in), out_specs=out_spec_list + [any_spec] * len(x_out),
        scratch_shapes=list(scratch_shapes) + sems, input_output_aliases=aliases,
        compiler_params=_cparams(None if semantics is None else ("arbitrary",) * len(grid)),
    )(*operands, *x_in)
    o0 = n_out
    for s in steps:
        s.results = list(res[o0:o0 + len(s.outs)])
        o0 += len(s.outs)
    return res[0] if single else tuple(res[:n_out])


def _run_steps(name, steps):
    _call(lambda: None, name=name, out_shape=[], grid=(), in_specs=[], out_specs=[], operands=[], steps=steps)
    return [s.results for s in steps]


def _mm(name, a, b, mode, out_dtype=F32, acc=None, b_colblock=0, k_rows=None, out_rows=None, steps=()):
    resident_bytes = 8 << 20
    if mode == "nn":
        m, k = a.shape
        n = b.shape[1]
        tm = m
        while tm * k * 2 > resident_bytes and tm % 32 == 0:
            tm //= 2
        tn = _tile(n)
        grid = (m // tm, n // tn)
        in_specs = [pl.BlockSpec((tm, k), lambda i, j: (i, 0)), pl.BlockSpec((k, tn), lambda i, j: (0, j))]
        out_shape, out_block = (m, n), (tm, tn)
    elif mode == "nt":
        m, n = a.shape
        k = k_rows or b.shape[0]
        tm = m
        while tm * n * 2 > resident_bytes and tm % 32 == 0:
            tm //= 2
        tk = _tile(k)
        grid = (m // tm, k // tk)
        in_specs = [pl.BlockSpec((tm, n), lambda i, j: (i, 0)), pl.BlockSpec((tk, n), lambda i, j: (j, b_colblock))]
        out_shape, out_block = (m, k), (tm, tk)
    else:
        m, k = a.shape
        n = b.shape[1]
        tk, tn = _tile(k), (n if m * n * 2 <= resident_bytes else _tile(n))
        grid = (k // tk, n // tn)
        in_specs = [pl.BlockSpec((m, tk), lambda i, j: (0, i)), pl.BlockSpec((m, tn), lambda i, j: (0, j))]
        out_shape, out_block = (out_rows or k, n), (tk, tn)
    out_spec = pl.BlockSpec(out_block, lambda i, j: (i, j))
    has_acc = acc is not None

    def body(*refs):
        a_ref, b_ref = refs[0], refs[1]
        o_ref = refs[-1]
        av, bv = a_ref[...], b_ref[...]
        if mode == "nn":
            r = _dot(av, bv)
        elif mode == "nt":
            r = _dot_nt(av, bv)
        else:
            r = _dot_tn(av, bv)
        if has_acc:
            r = r + refs[2][...]
        o_ref[...] = r.astype(o_ref.dtype)

    operands = [a, b]
    if has_acc:
        in_specs = in_specs + [out_spec]
        operands.append(acc)
    return _call(body, name=name, out_shape=jax.ShapeDtypeStruct(out_shape, out_dtype), grid=grid, in_specs=in_specs,
                 out_specs=out_spec, operands=operands, semantics=("parallel", "parallel"), steps=steps)


def _tn_rows_into(name, a, b, into, row0, nrows):
    m, k = a.shape
    n = b.shape[1]

    def body(a_ref, b_ref, into_ref, o_ref):
        o_ref[...] = _dot_tn(a_ref[...], b_ref[...])[0:nrows].astype(o_ref.dtype)

    return pl.pallas_call(
        body, name=name, out_shape=jax.ShapeDtypeStruct(into.shape, into.dtype), grid=(1,),
        in_specs=[pl.BlockSpec((m, k), lambda i: (0, 0)), pl.BlockSpec((m, n), lambda i: (0, 0)),
                  pl.BlockSpec(memory_space=pl.ANY)],
        out_specs=pl.BlockSpec((nrows, n), lambda i: (row0 // nrows, 0)),
        input_output_aliases={2: 0}, compiler_params=_cparams(("arbitrary",)),
    )(a, b, into)


def _rms_fwd(name, h, w):
    rows, width = h.shape
    tm = _row_tile(rows, width)

    def body(h_ref, w_ref, o_ref):
        x = h_ref[...]
        r = lax.rsqrt(jnp.mean(x * x, axis=-1, keepdims=True) + RMS_EPS)
        o_ref[...] = (x * r * w_ref[...]).astype(BF16)

    return pl.pallas_call(
        body, name=name, out_shape=jax.ShapeDtypeStruct((rows, width), BF16), grid=(rows // tm,),
        in_specs=[pl.BlockSpec((tm, width), lambda i: (i, 0)), pl.BlockSpec((1, width), lambda i: (0, 0))],
        out_specs=pl.BlockSpec((tm, width), lambda i: (i, 0)), compiler_params=_cparams(("parallel",)),
    )(h, w)


def _resid_norm_fwd(name, h, pre, w, next_norms=()):
    rows, width = h.shape
    tm = _row_tile(rows, width)
    n_next = len(next_norms)

    def body(*refs):
        h_ref, p_ref, w_ref = refs[:3]
        v_refs = refs[3:3 + n_next]
        o_ref = refs[3 + n_next]
        n_refs = refs[4 + n_next:]
        p = p_ref[...]
        r = lax.rsqrt(jnp.mean(p * p, axis=-1, keepdims=True) + RMS_EPS)
        x = h_ref[...] + jnp.where(_rows_mask(pl.program_id(0), tm), p * r * w_ref[...], 0.0)
        o_ref[...] = x
        if n_next:
            rx = lax.rsqrt(jnp.mean(x * x, axis=-1, keepdims=True) + RMS_EPS)
            for v_ref, n_ref in zip(v_refs, n_refs):
                n_ref[...] = (x * rx * v_ref[...]).astype(BF16)

    row_spec = pl.BlockSpec((tm, width), lambda i: (i, 0))
    vec_spec = pl.BlockSpec((1, width), lambda i: (0, 0))
    outs = pl.pallas_call(
        body, name=name,
        out_shape=[jax.ShapeDtypeStruct((rows, width), F32)] + [jax.ShapeDtypeStruct((rows, width), BF16)] * n_next,
        grid=(rows // tm,), in_specs=[row_spec, row_spec, vec_spec] + [vec_spec] * n_next,
        out_specs=[row_spec] * (1 + n_next), compiler_params=_cparams(("parallel",)),
    )(h, pre, w, *next_norms)
    return outs[0], list(outs[1:])


def _resid_norm_loss(name, h, pre, w, target):
    rows, width = h.shape

    def body(h_ref, p_ref, w_ref, t_ref, dh_ref, loss_ref, dp_ref, dw_ref):
        i = pl.program_id(0)
        p = p_ref[...]
        r = lax.rsqrt(jnp.mean(p * p, axis=-1, keepdims=True) + RMS_EPS)
        x = h_ref[...] + p * r * w_ref[...]
        real = (i + jnp.zeros((CHUNK, 1), jnp.int32)) >= 1
        diff = jnp.where(real, x - t_ref[...], 0.0)
        dh = diff * (1.0 / D_MODEL)
        dh_ref[...] = dh
        dp, dw_rows = _rms_bwd(dh, p, w_ref[...])
        dp_ref[...] = dp.astype(BF16)

        @pl.when(i == 0)
        def _():
            loss_ref[...] = jnp.zeros_like(loss_ref)
            dw_ref[...] = jnp.zeros_like(dw_ref)

        loss_ref[...] += jnp.sum(diff * diff) * (0.5 / D_MODEL)
        dw_ref[...] += jnp.sum(dw_rows, axis=0, keepdims=True)

    blk = pl.BlockSpec((CHUNK, width), lambda i: (i, 0))
    vec_spec = pl.BlockSpec((1, width), lambda i: (0, 0))
    return pl.pallas_call(
        body, name=name,
        out_shape=(jax.ShapeDtypeStruct((rows, width), F32), jax.ShapeDtypeStruct((1, LANES), F32),
                   jax.ShapeDtypeStruct((rows, width), BF16), jax.ShapeDtypeStruct((1, width), F32)),
        grid=(rows // CHUNK,),
        in_specs=[blk, blk, vec_spec, pl.BlockSpec((CHUNK, width), lambda i: (jnp.maximum(i - 1, 0), 0))],
        out_specs=(blk, pl.BlockSpec((1, LANES), lambda i: (0, 0)), blk, vec_spec),
        compiler_params=_cparams(("arbitrary",)),
    )(h, pre, w, target)


def _rms_bwd(dy, x, w):
    r = lax.rsqrt(jnp.mean(x * x, axis=-1, keepdims=True) + RMS_EPS)
    xhat = x * r
    dxhat = dy * w
    return r * (dxhat - xhat * jnp.mean(dxhat * xhat, axis=-1, keepdims=True)), dy * xhat


def _norm_bwd_add(name, dh, dhn, h, w, then=None, split_first_block=False, steps=()):
    rows, width = dh.shape
    tm = CHUNK if split_first_block else _row_tile(rows, width)
    fused = then is not None
    assert not (fused and split_first_block)

    def body(*refs):
        dh_ref, dhn_ref, h_ref, w_ref = refs[:4]
        o_ref, dw_ref = refs[6:8] if fused else refs[-2:]
        i = pl.program_id(0)
        valid = _rows_mask(i, tm)
        dx, dw_rows = _rms_bwd(dhn_ref[...], h_ref[...], w_ref[...])
        dh_new = dh_ref[...] + jnp.where(valid, dx, 0.0)
        if split_first_block:
            first_ref = refs[4]

            @pl.when(i == 0)
            def _():
                first_ref[...] = dh_new

            @pl.when(i > 0)
            def _():
                o_ref[...] = dh_new
        else:
            o_ref[...] = dh_new

        @pl.when(i == 0)
        def _():
            dw_ref[...] = jnp.zeros_like(dw_ref)

        dw_ref[...] += jnp.sum(dw_rows, axis=0, keepdims=True)
        if fused:
            p_ref, wp_ref, dp_ref, dwp_ref = refs[4], refs[5], refs[8], refs[9]
            dp, dwp_rows = _rms_bwd(jnp.where(valid, dh_new, 0.0), p_ref[...], wp_ref[...])
            dp_ref[...] = dp.astype(BF16)

            @pl.when(i == 0)
            def _():
                dwp_ref[...] = jnp.zeros_like(dwp_ref)

            dwp_ref[...] += jnp.sum(dwp_rows, axis=0, keepdims=True)

    row_spec = pl.BlockSpec((tm, width), lambda i: (i, 0))
    vec_spec = pl.BlockSpec((1, width), lambda i: (0, 0))
    row_f32, vec_f32 = jax.ShapeDtypeStruct((rows, width), F32), jax.ShapeDtypeStruct((1, width), F32)
    in_specs, operands = [row_spec, row_spec, row_spec, vec_spec], [dh, dhn, h, w]
    out_shape, out_specs = [row_f32, vec_f32], [row_spec, vec_spec]
    if split_first_block:
        out_shape = [jax.ShapeDtypeStruct((tm, width), F32), jax.ShapeDtypeStruct((rows - tm, width), F32), vec_f32]
        out_specs = [pl.BlockSpec((tm, width), lambda i: (0, 0)),
                     pl.BlockSpec((tm, width), lambda i: (jnp.maximum(i - 1, 0), 0)), vec_spec]
    if fused:
        in_specs += [row_spec, vec_spec]
        operands += list(then)
        out_shape += [jax.ShapeDtypeStruct((rows, width), BF16), vec_f32]
        out_specs += [row_spec, vec_spec]
    return _call(body, name=name, out_shape=out_shape, grid=(rows // tm,), in_specs=in_specs, out_specs=out_specs,
                 operands=operands, semantics=("arbitrary",), steps=steps)


def _shift_down(x, s, rows):
    return pltpu.roll(x, s, 0) if s else x


def _shift_up(x, s, rows):
    return pltpu.roll(x, rows - s, 0) if s else x


def _conv4_fwd(name, zx, cw, cb, steps=()):
    rows = zx.shape[0]
    off = D_INNER // LANES

    def body(x_ref, w_ref, b_ref, o_ref):
        x = x_ref[...]
        acc = b_ref[...] + w_ref[pl.ds(SSM_CONV - 1, 1), :] * x
        for s in range(1, SSM_CONV):
            acc = acc + w_ref[pl.ds(SSM_CONV - 1 - s, 1), :] * _shift_down(x, s, rows)
        valid = lax.broadcasted_iota(jnp.int32, (rows, 1), 0) >= PAD_ROWS
        o_ref[...] = jnp.where(valid, acc * _sigmoid(acc), 0.0)

    return _call(
        body, name=name, out_shape=jax.ShapeDtypeStruct((rows, D_XBC), F32), grid=(D_XBC // LANES,),
        in_specs=[pl.BlockSpec((rows, LANES), lambda j: (0, j + off)),
                  pl.BlockSpec((SSM_CONV, LANES), lambda j: (0, j)),
                  pl.BlockSpec((1, LANES), lambda j: (0, j))],
        out_specs=pl.BlockSpec((rows, LANES), lambda j: (0, j)), operands=[zx, cw, cb],
        semantics=("parallel",), steps=steps)


def _conv4_bwd(name, zx, dout, cw, cb, into):
    rows, width = dout.shape
    zoff = D_INNER // LANES

    def body(x_ref, d_ref, w_ref, b_ref, into_ref, dx_ref, dw_ref, db_ref):
        x = x_ref[...]
        shifted = [_shift_down(x, s, rows) for s in range(SSM_CONV)]
        acc = b_ref[...]
        for s in range(SSM_CONV):
            acc = acc + w_ref[pl.ds(SSM_CONV - 1 - s, 1), :] * shifted[s]
        sig = _sigmoid(acc)
        valid = lax.broadcasted_iota(jnp.int32, (rows, 1), 0) >= PAD_ROWS
        dpre = jnp.where(valid, d_ref[...] * sig * (1.0 + acc * (1.0 - sig)), 0.0)
        dx = w_ref[pl.ds(SSM_CONV - 1, 1), :] * dpre
        for s in range(1, SSM_CONV):
            dx = dx + w_ref[pl.ds(SSM_CONV - 1 - s, 1), :] * _shift_up(dpre, s, rows)
        dx_ref[...] = dx.astype(BF16)
        for s in range(SSM_CONV):
            dw_ref[pl.ds(SSM_CONV - 1 - s, 1), :] = jnp.sum(dpre * shifted[s], axis=0, keepdims=True)
        db_ref[...] = jnp.sum(dpre, axis=0, keepdims=True)

    return pl.pallas_call(
        body, name=name,
        out_shape=(jax.ShapeDtypeStruct(into.shape, BF16), jax.ShapeDtypeStruct((SSM_CONV, width), F32),
                   jax.ShapeDtypeStruct((1, width), F32)),
        grid=(width // LANES,),
        in_specs=[pl.BlockSpec((rows, LANES), lambda j: (0, j + zoff)),
                  pl.BlockSpec((rows, LANES), lambda j: (0, j)),
                  pl.BlockSpec((SSM_CONV, LANES), lambda j: (0, j)),
                  pl.BlockSpec((1, LANES), lambda j: (0, j)),
                  pl.BlockSpec(memory_space=pl.ANY)],
        out_specs=(pl.BlockSpec((rows, LANES), lambda j: (0, j + zoff)),
                   pl.BlockSpec((SSM_CONV, LANES), lambda j: (0, j)),
                   pl.BlockSpec((1, LANES), lambda j: (0, j))),
        input_output_aliases={4: 0}, compiler_params=_cparams(("parallel",)),
    )(zx, dout, cw, cb, into)


FFN_TILE = 2 * LANES


def _ffn_up_conv(name, hn, w_up, cw, cb, steps=()):
    rows, k = hn.shape
    chip_blocks = w_up.shape[2] // LANES
    half_blocks = D_FF // LANES
    nt = D_FF // FFN_TILE

    def weight_block(offset):
        return pl.BlockSpec((None, k, LANES), lambda j: ((2 * j + offset) // chip_blocks, 0, (2 * j + offset) % chip_blocks))

    def body(a_ref, g0, g1, v0, v1, wg_ref, wv_ref, bg_ref, bv_ref, upg_ref, upv_ref, act_ref):
        a = a_ref[...]
        g = _dot(a, jnp.concatenate([g0[...], g1[...]], axis=1))
        v = _dot(a, jnp.concatenate([v0[...], v1[...]], axis=1))
        upg_ref[...] = g
        upv_ref[...] = v
        ug, uv = bg_ref[...], bv_ref[...]
        for s in range(FFN_CONV):
            ug = ug + wg_ref[pl.ds(FFN_CONV - 1 - s, 1), :] * _shift_down(g, s, rows)
            uv = uv + wv_ref[pl.ds(FFN_CONV - 1 - s, 1), :] * _shift_down(v, s, rows)
        act_ref[...] = (ug * _sigmoid(ug) * uv).astype(BF16)

    col = pl.BlockSpec((rows, FFN_TILE), lambda j: (0, j))
    wsp = lambda shift: pl.BlockSpec((FFN_CONV, FFN_TILE), lambda j: (0, j + shift))
    bsp = lambda shift: pl.BlockSpec((1, FFN_TILE), lambda j: (0, j + shift))
    half = jax.ShapeDtypeStruct((rows, D_FF), F32)
    return _call(
        body, name=name, out_shape=(half, half, jax.ShapeDtypeStruct((rows, D_FF), BF16)), grid=(nt,),
        in_specs=[pl.BlockSpec((rows, k), lambda j: (0, 0)), weight_block(0), weight_block(1),
                  weight_block(half_blocks), weight_block(half_blocks + 1), wsp(0), wsp(nt), bsp(0), bsp(nt)],
        out_specs=(col, col, col), operands=[hn, w_up, w_up, w_up, w_up, cw, cw, cb, cb],
        semantics=("parallel",), steps=steps)


def _ffn_conv_bwd(name, up_g, up_v, dact, cw, cb, hn, w_up, steps=()):
    rows, k = hn.shape
    chip_blocks = w_up.shape[2] // LANES
    nt = D_FF // LANES

    def weight_block(shift):
        return pl.BlockSpec((None, k, LANES), lambda j: ((j + shift) // chip_blocks, 0, (j + shift) % chip_blocks))

    def body(g_ref, v_ref, d_ref, wg_ref, wv_ref, bg_ref, bv_ref, upg_ref, upv_ref, hn_ref,
             dwg_ref, dwv_ref, dbg_ref, dbv_ref, dhn_ref, dup_ref, acc, hn_scr, hnt_scr, dup_scr, sems):
        j = pl.program_id(0)
        hn_copy = pltpu.make_async_copy(hn_ref, hn_scr, sems.at[0])
        dhn_copy = pltpu.make_async_copy(acc, dhn_ref, sems.at[0])

        def dup_copy(step, half):
            block, slot = step + half * nt, 2 * (step % 2) + half
            cols = pl.ds(pl.multiple_of((block % chip_blocks) * LANES, LANES), LANES)
            return pltpu.make_async_copy(dup_scr.at[slot], dup_ref.at[block // chip_blocks, :, cols], sems.at[1 + slot])

        @pl.when(j == 0)
        def _():
            hn_copy.start()
            acc[...] = jnp.zeros_like(acc)
            hn_copy.wait()
            for r in range(0, rows, LANES):
                hnt_scr[:, r:r + LANES] = hn_scr[r:r + LANES, :].T

        @pl.when(j >= 2)
        def _():
            dup_copy(j - 2, 0).wait()
            dup_copy(j - 2, 1).wait()

        g, v = g_ref[...], v_ref[...]
        gs = [_shift_down(g, s, rows) for s in range(FFN_CONV)]
        vs = [_shift_down(v, s, rows) for s in range(FFN_CONV)]
        ug, uv = bg_ref[...], bv_ref[...]
        for s in range(FFN_CONV):
            ug = ug + wg_ref[pl.ds(FFN_CONV - 1 - s, 1), :] * gs[s]
            uv = uv + wv_ref[pl.ds(FFN_CONV - 1 - s, 1), :] * vs[s]
        sig = _sigmoid(ug)
        dsig = d_ref[...] * sig
        dup = []
        for dpre, src, w_ref, dw_ref, db_ref in (
                (dsig * uv * (1.0 + ug * (1.0 - sig)), gs, wg_ref, dwg_ref, dbg_ref),
                (dsig * ug, vs, wv_ref, dwv_ref, dbv_ref)):
            dx = w_ref[pl.ds(FFN_CONV - 1, 1), :] * dpre
            for s in range(1, FFN_CONV):
                dx = dx + w_ref[pl.ds(FFN_CONV - 1 - s, 1), :] * _shift_up(dpre, s, rows)
            dup.append(dx.astype(BF16))
            for s in range(FFN_CONV):
                dw_ref[pl.ds(FFN_CONV - 1 - s, 1), :] = jnp.sum(dpre * src[s], axis=0, keepdims=True)
            db_ref[...] = jnp.sum(dpre, axis=0, keepdims=True)
        dup = jnp.concatenate(dup, axis=1)
        acc[...] += _dot_nt(dup, jnp.concatenate([upg_ref[...], upv_ref[...]], axis=1))
        dw = _dot(hnt_scr[...], dup)
        slot = 2 * (j % 2)
        dup_scr[slot] = dw[:, :LANES].astype(BF16)
        dup_scr[slot + 1] = dw[:, LANES:].astype(BF16)
        dup_copy(j, 0).start()
        dup_copy(j, 1).start()

        @pl.when(j == nt - 1)
        def _():
            dhn_copy.start()
            for step in (j - 1, j):
                dup_copy(step, 0).wait()
                dup_copy(step, 1).wait()
            dhn_copy.wait()

    col = pl.BlockSpec((rows, LANES), lambda j: (0, j))
    wsp = lambda shift: pl.BlockSpec((FFN_CONV, LANES), lambda j: (0, j + shift))
    bsp = lambda shift: pl.BlockSpec((1, LANES), lambda j: (0, j + shift))
    any_spec = pl.BlockSpec(memory_space=pl.ANY)
    dw_shape = jax.ShapeDtypeStruct((FFN_CONV, D_FF), F32)
    db_shape = jax.ShapeDtypeStruct((1, D_FF), F32)
    return _call(
        body, name=name, grid=(nt,),
        out_shape=(dw_shape, dw_shape, db_shape, db_shape, jax.ShapeDtypeStruct((rows, k), F32),
                   jax.ShapeDtypeStruct(w_up.shape, BF16)),
        in_specs=[col, col, col, wsp(0), wsp(nt), bsp(0), bsp(nt), weight_block(0), weight_block(nt), any_spec],
        out_specs=(wsp(0), wsp(0), bsp(0), bsp(0), any_spec, any_spec),
        operands=[up_g, up_v, dact, cw, cw, cb, cb, w_up, w_up, hn],
        scratch_shapes=[pltpu.VMEM((rows, k), F32), pltpu.VMEM((rows, k), BF16), pltpu.VMEM((k, rows), BF16),
                        pltpu.VMEM((4, k, LANES), BF16), pltpu.SemaphoreType.DMA((5,))],
        semantics=("arbitrary",), steps=steps)


def _dt_fwd(name, dtr, bias):
    rows = dtr.shape[0]
    tm = _row_tile(rows, LANES)

    def body(d_ref, b_ref, o_ref):
        v = d_ref[...] + b_ref[...]
        sp = jnp.maximum(v, 0.0) + jnp.log1p(jnp.exp(-jnp.abs(v)))
        lane = lax.broadcasted_iota(jnp.int32, (tm, LANES), 1)
        ok = _rows_mask(pl.program_id(0), tm) & (lane < SSM_HEADS)
        o_ref[...] = jnp.where(ok, sp, 0.0)

    return pl.pallas_call(
        body, name=name, out_shape=jax.ShapeDtypeStruct((rows, LANES), F32), grid=(rows // tm,),
        in_specs=[pl.BlockSpec((tm, LANES), lambda i: (i, 0)), pl.BlockSpec((1, LANES), lambda i: (0, 0))],
        out_specs=pl.BlockSpec((tm, LANES), lambda i: (i, 0)), compiler_params=_cparams(("parallel",)),
    )(dtr, bias)


def _dt_bwd(name, ddt, dtr, bias):
    rows = dtr.shape[0]
    tm = _row_tile(rows, LANES)

    def body(g_ref, d_ref, b_ref, o_ref, db_ref):
        i = pl.program_id(0)
        lane = lax.broadcasted_iota(jnp.int32, (tm, LANES), 1)
        ok = _rows_mask(i, tm) & (lane < SSM_HEADS)
        dv = jnp.where(ok, g_ref[...] * _sigmoid(d_ref[...] + b_ref[...]), 0.0)
        o_ref[...] = dv.astype(BF16)

        @pl.when(i == 0)
        def _():
            db_ref[...] = jnp.zeros_like(db_ref)

        db_ref[...] += jnp.sum(dv, axis=0, keepdims=True)

    row_spec = pl.BlockSpec((tm, LANES), lambda i: (i, 0))
    vec_spec = pl.BlockSpec((1, LANES), lambda i: (0, 0))
    return pl.pallas_call(
        body, name=name,
        out_shape=(jax.ShapeDtypeStruct((rows, LANES), BF16), jax.ShapeDtypeStruct((1, LANES), F32)),
        grid=(rows // tm,), in_specs=[row_spec, row_spec, vec_spec], out_specs=(row_spec, vec_spec),
        compiler_params=_cparams(("arbitrary",)),
    )(ddt, dtr, bias)


def _gate_fwd(name, y, zx, w, steps=()):
    rows = y.shape[0]
    tm = _row_tile(rows, D_INNER)

    def body(y_ref, z_ref, w_ref, o_ref):
        z = z_ref[...]
        g = y_ref[...] * (z * _sigmoid(z))
        r = lax.rsqrt(jnp.mean(g * g, axis=-1, keepdims=True) + RMS_EPS)
        o_ref[...] = (g * r * w_ref[...]).astype(BF16)

    row_spec = pl.BlockSpec((tm, D_INNER), lambda i: (i, 0))
    return _call(
        body, name=name, out_shape=jax.ShapeDtypeStruct((rows, D_INNER), BF16), grid=(rows // tm,),
        in_specs=[row_spec, row_spec, pl.BlockSpec((1, D_INNER), lambda i: (0, 0))],
        out_specs=row_spec, operands=[y, zx, w], semantics=("parallel",), steps=steps)


def _gate_bwd(name, dyn, y, zx, w):
    rows = y.shape[0]
    tm = _row_tile(rows, D_INNER)

    def body(d_ref, y_ref, z_ref, w_ref, dy_ref, dz_ref, dw_ref):
        i = pl.program_id(0)
        z, yv = z_ref[...], y_ref[...]
        sig = _sigmoid(z)
        sz = z * sig
        g = yv * sz
        r = lax.rsqrt(jnp.mean(g * g, axis=-1, keepdims=True) + RMS_EPS)
        ghat = g * r
        dn = d_ref[...]
        dghat = dn * w_ref[...]
        dg = r * (dghat - ghat * jnp.mean(dghat * ghat, axis=-1, keepdims=True))
        dy_ref[...] = dg * sz
        dz_ref[...] = (dg * yv * sig * (1.0 + z * (1.0 - sig))).astype(BF16)

        @pl.when(i == 0)
        def _():
            dw_ref[...] = jnp.zeros_like(dw_ref)

        dw_ref[...] += jnp.sum(dn * ghat, axis=0, keepdims=True)

    row_spec = pl.BlockSpec((tm, D_INNER), lambda i: (i, 0))
    vec_spec = pl.BlockSpec((1, D_INNER), lambda i: (0, 0))
    return pl.pallas_call(
        body, name=name,
        out_shape=(jax.ShapeDtypeStruct((rows, D_INNER), F32), jax.ShapeDtypeStruct((rows, D_MAIN), BF16),
                   jax.ShapeDtypeStruct((1, D_INNER), F32)),
        grid=(rows // tm,), in_specs=[row_spec, row_spec, row_spec, vec_spec],
        out_specs=(row_spec, row_spec, vec_spec), compiler_params=_cparams(("arbitrary",)),
    )(dyn, y, zx, w)


def _split3(x):
    hi = x.astype(BF16)
    r1 = x - hi.astype(F32)
    mid = r1.astype(BF16)
    lo = (r1 - mid.astype(F32)).astype(BF16)
    return hi, mid, lo


def _dot3_data_lhs(x, sel):
    sel16 = sel.astype(F32).astype(BF16)
    hi, mid, lo = _split3(x)
    return _dot(hi, sel16) + _dot(mid, sel16) + _dot(lo, sel16)


def _dot2_data_lhs(x, sel):
    sel16 = sel.astype(F32).astype(BF16)
    hi = x.astype(BF16)
    mid = (x - hi.astype(F32)).astype(BF16)
    return _dot(hi, sel16) + _dot(mid, sel16)


def _dot3_data_rhs(sel, x):
    sel16 = sel.astype(F32).astype(BF16)
    hi, mid, lo = _split3(x)
    return _dot(sel16, hi) + _dot(sel16, mid) + _dot(sel16, lo)


def _causal_masks():
    r = lax.broadcasted_iota(jnp.int32, (CHUNK, CHUNK), 0)
    c = lax.broadcasted_iota(jnp.int32, (CHUNK, CHUNK), 1)
    return r >= c, r <= c


def _expand_heads_matrix(g):
    k = lax.broadcasted_iota(jnp.int32, (LANES, GROUP_W), 0)
    j = lax.broadcasted_iota(jnp.int32, (LANES, GROUP_W), 1)
    return HEADS_PER_GROUP * g + jnp.right_shift(j, 6) == k


def _reduce_heads_matrix(g):
    j = lax.broadcasted_iota(jnp.int32, (GROUP_W, LANES), 0)
    k = lax.broadcasted_iota(jnp.int32, (GROUP_W, LANES), 1)
    return HEADS_PER_GROUP * g + jnp.right_shift(j, 6) == k


def _reduce_pair_matrix(g, p):
    j = lax.broadcasted_iota(jnp.int32, (LANES, LANES), 0)
    k = lax.broadcasted_iota(jnp.int32, (LANES, LANES), 1)
    return HEADS_PER_GROUP * g + 2 * p + jnp.right_shift(j, 6) == k


def _group_cols(ref, g, width):
    return ref.at[:, pl.ds(g * width, width)]


def _ssd_prep(name, dt, a128, steps=()):
    rows = dt.shape[0]
    nc = rows // CHUNK

    def body(dt_ref, a_ref, dte_ref, acs_ref, acst_ref):
        causal, _ = _causal_masks()
        dtv = dt_ref[...]
        acs = _dot3_data_rhs(causal, dtv) * a_ref[...]
        acst_ref[...] = acs.T[0:SSM_HEADS]
        for g in range(N_GROUPS):
            expand = _expand_heads_matrix(g)
            _group_cols(dte_ref, g, GROUP_W)[...] = _dot3_data_lhs(dtv, expand)
            _group_cols(acs_ref, g, GROUP_W)[...] = _dot3_data_lhs(acs, expand)

    blk = pl.BlockSpec((CHUNK, D_INNER), lambda c: (c, 0))
    shp = jax.ShapeDtypeStruct((rows, D_INNER), F32)
    return _call(
        body, name=name, out_shape=(shp, shp, jax.ShapeDtypeStruct((nc, SSM_HEADS, CHUNK), F32)), grid=(nc,),
        in_specs=[pl.BlockSpec((CHUNK, LANES), lambda c: (c, 0)), pl.BlockSpec((1, LANES), lambda c: (0, 0))],
        out_specs=(blk, blk, pl.BlockSpec((None, SSM_HEADS, CHUNK), lambda c: (c, 0, 0))),
        operands=[dt, a128], semantics=("parallel",), steps=steps)


def _ssd_common(x_ref, b_ref, c_ref, dte_ref, acs_ref):
    x = x_ref[...]
    dt_exp = dte_ref[...]
    acs_exp = acs_ref[...]
    tot_exp = acs_ref[pl.ds(CHUNK - 1, 1), :]
    xdt = x * dt_exp
    e_exp = jnp.exp(acs_exp)
    f_exp = jnp.exp(tot_exp - acs_exp)
    return _causal_masks(), x, dt_exp, acs_exp, tot_exp, xdt, e_exp, f_exp, b_ref[...], c_ref[...]


def _pair_decay(acs_pair, acs_row, e, causal):
    lane = lax.broadcasted_iota(jnp.int32, (CHUNK, LANES), 1)
    mine = (lane < HEAD_DIM) if e == 0 else (lane >= HEAD_DIM)
    a_l = jnp.where(mine, acs_pair, pltpu.roll(acs_pair, HEAD_DIM, 1))
    seg = a_l - acs_row
    dm = jnp.where(causal[0], jnp.exp(jnp.minimum(seg, 0.0)), 0.0)
    dmt = jnp.where(causal[1], jnp.exp(jnp.minimum(-seg, 0.0)), 0.0)
    return dm, dmt


def _ssd_specs(index_of_chunk):
    wide = pl.BlockSpec((CHUNK, D_INNER), lambda c: (index_of_chunk(c), 0))
    b_spec = pl.BlockSpec((CHUNK, D_BC), lambda c: (index_of_chunk(c), D_INNER // D_BC))
    c_spec = pl.BlockSpec((CHUNK, D_BC), lambda c: (index_of_chunk(c), D_INNER // D_BC + 1))
    rows_spec = pl.BlockSpec((None, SSM_HEADS, CHUNK), lambda c: (index_of_chunk(c), 0, 0))
    state_spec = pl.BlockSpec((N_GROUPS, None, D_STATE, GROUP_W), lambda c: (0, index_of_chunk(c), 0, 0))
    return wide, b_spec, c_spec, rows_spec, state_spec


def _ssd_fwd(name, xbc, dt_exp, acs_exp, acs_rows, dskexp, steps=()):
    rows = xbc.shape[0]
    nc = rows // CHUNK

    def body(x_ref, b_ref, c_ref, dte_ref, acs_ref, acst_ref, dsk_ref, y_ref, st_ref, s_scr):
        @pl.when(pl.program_id(0) == 0)
        def _():
            s_scr[...] = jnp.zeros_like(s_scr)

        lane = lax.broadcasted_iota(jnp.int32, (CHUNK, LANES), 1)
        for g in range(N_GROUPS):
            y_g = _group_cols(y_ref, g, GROUP_W)
            causal, x, _, acs_exp_v, tot_exp, xdt, e_exp, f_exp, bm, cm = _ssd_common(
                _group_cols(x_ref, g, GROUP_W), _group_cols(b_ref, g, D_STATE), _group_cols(c_ref, g, D_STATE),
                _group_cols(dte_ref, g, GROUP_W), _group_cols(acs_ref, g, GROUP_W))
            state = s_scr[g]
            st_ref[g] = state
            cb16, bb16 = cm.astype(BF16), bm.astype(BF16)
            cb = _dot_nt(cb16, bb16)
            base = e_exp * _dot(cb16, state.astype(BF16)) + _group_cols(dsk_ref, g, GROUP_W)[...] * x
            for p in range(HEADS_PER_GROUP // 2):
                sl = slice(p * LANES, (p + 1) * LANES)
                xp = xdt[:, sl].astype(BF16)
                yd = []
                for e in range(2):
                    acs_row = acst_ref[pl.ds(g * HEADS_PER_GROUP + 2 * p + e, 1), :]
                    dm, _ = _pair_decay(acs_exp_v[:, sl], acs_row, e, causal)
                    yd.append(_dot((cb * dm).astype(BF16), xp))
                y_g[:, sl] = jnp.where(lane < HEAD_DIM, yd[0], yd[1]) + base[:, sl]
            s_scr[g] = jnp.exp(tot_exp) * state + _dot_tn(bb16, (f_exp * xdt).astype(BF16))

    wide, b_spec, c_spec, rows_spec, state_spec = _ssd_specs(lambda c: c)
    return _call(
        body, name=name,
        out_shape=(jax.ShapeDtypeStruct((rows, D_INNER), F32),
                   jax.ShapeDtypeStruct((N_GROUPS, nc, D_STATE, GROUP_W), F32)),
        grid=(nc,),
        in_specs=[wide, b_spec, c_spec, wide, wide, rows_spec, pl.BlockSpec((1, D_INNER), lambda c: (0, 0))],
        out_specs=(wide, state_spec),
        scratch_shapes=[pltpu.VMEM((N_GROUPS, D_STATE, GROUP_W), F32)],
        operands=[xbc, xbc, xbc, dt_exp, acs_exp, acs_rows, dskexp], semantics=("arbitrary",), steps=steps)


def _ssd_bwd(name, xbc, dt_exp, acs_exp, acs_rows, dt, a128, dskexp, dy, states, steps=()):
    rows = xbc.shape[0]
    nc = rows // CHUNK
    last = nc - 1

    def body(x_ref, b_ref, c_ref, dte_ref, acs_ref, acst_ref, dt_ref, a128_ref, dsk_all, dy_all, st_all,
             dxbc_all, ddt_ref, dalog_ref, ddsk_ref, ds_all):
        dx_all, db_all, dc_all = (dxbc_all.at[:, :D_INNER], dxbc_all.at[:, D_INNER:D_INNER + D_BC],
                                  dxbc_all.at[:, D_INNER + D_BC:])

        @pl.when(pl.program_id(0) == 0)
        def _():
            ds_all[...] = jnp.zeros_like(ds_all)
            dalog_ref[...] = jnp.zeros_like(dalog_ref)
            ddsk_ref[...] = jnp.zeros_like(ddsk_ref)

        dacs = jnp.zeros((CHUNK, LANES), F32)
        ddt_x = jnp.zeros((CHUNK, LANES), F32)
        for g in range(N_GROUPS):
            dacs_g, ddt_x_g = group(
                g, _group_cols(x_ref, g, GROUP_W), _group_cols(b_ref, g, D_STATE), _group_cols(c_ref, g, D_STATE),
                _group_cols(dte_ref, g, GROUP_W), _group_cols(acs_ref, g, GROUP_W), acst_ref,
                _group_cols(dsk_all, g, GROUP_W), _group_cols(dy_all, g, GROUP_W), st_all.at[g],
                _group_cols(dx_all, g, GROUP_W), _group_cols(db_all, g, D_STATE), _group_cols(dc_all, g, D_STATE),
                ddsk_ref, ds_all.at[g])
            dacs, ddt_x = dacs + dacs_g, ddt_x + ddt_x_g
        _, causal_t = _causal_masks()
        da = _dot3_data_rhs(causal_t, dacs)
        ddt_ref[...] = da * a128_ref[...] + ddt_x
        dalog_ref[...] += jnp.sum(da * dt_ref[...], axis=0, keepdims=True) * a128_ref[...]

    def group(g, x_ref, b_ref, c_ref, dte_ref, acs_ref, acst_ref, dsk_ref, dy_ref, st_ref,
              dx_ref, db_ref, dc_ref, ddsk_ref, ds_scr):
        causal, x, dt_exp, acs_exp_v, tot_exp, xdt, e_exp, f_exp, bm, cm = _ssd_common(
            x_ref, b_ref, c_ref, dte_ref, acs_ref)
        reduce_heads = _reduce_heads_matrix(g)
        state, dstate = st_ref[...], ds_scr[...]
        dyv = dy_ref[...]
        cb16, bb16 = cm.astype(BF16), bm.astype(BF16)
        s16, ds16 = state.astype(BF16), dstate.astype(BF16)
        cb = _dot_nt(cb16, bb16)
        cbt = _dot_nt(bb16, cb16)
        cs = _dot(cb16, s16)
        bds = _dot(bb16, ds16)
        edy = e_exp * dyv
        fx = f_exp * xdt
        dxdt_base = f_exp * bds
        dc_acc = _dot_nt(edy.astype(BF16), s16)
        db_acc = _dot_nt(fx.astype(BF16), ds16)
        ds_scr[...] = jnp.exp(tot_exp) * dstate + _dot_tn(cb16, edy.astype(BF16))
        q = fx * bds
        dacs = _dot2_data_lhs(edy * cs - q, reduce_heads)
        dtot = jnp.sum(_dot2_data_lhs(q + jnp.exp(tot_exp) * dstate * state, reduce_heads), axis=0, keepdims=True)
        ddsk_ref[...] += jnp.sum(_dot2_data_lhs(dyv * x, reduce_heads), axis=0, keepdims=True)
        lane = lax.broadcasted_iota(jnp.int32, (CHUNK, LANES), 1)
        dcb = jnp.zeros((CHUNK, CHUNK), F32)
        dcbt = jnp.zeros((CHUNK, CHUNK), F32)
        ddt_x = jnp.zeros((CHUNK, LANES), F32)
        for p in range(HEADS_PER_GROUP // 2):
            sl = slice(p * LANES, (p + 1) * LANES)
            xp, dyp = xdt[:, sl], dyv[:, sl]
            xp16, dyp16 = xp.astype(BF16), dyp.astype(BF16)
            dxh = []
            for e in range(2):
                h = 2 * p + e
                mine = (lane < HEAD_DIM) if e == 0 else (lane >= HEAD_DIM)
                acs_row = acst_ref[pl.ds(g * HEADS_PER_GROUP + h, 1), :]
                dm, dmt = _pair_decay(acs_exp_v[:, sl], acs_row, e, causal)
                m, mt = cb * dm, cbt * dmt
                xh16 = jnp.where(mine, xp, 0.0).astype(BF16)
                dyh16 = jnp.where(mine, dyp, 0.0).astype(BF16)
                d_m = _dot_nt(dyh16, xp16)
                d_mt = _dot_nt(xh16, dyp16)
                dacs_h = (jnp.sum(d_m * m, axis=-1, keepdims=True)
                          - jnp.sum(d_mt * mt, axis=-1, keepdims=True))
                dacs = dacs + jnp.where(lane == HEADS_PER_GROUP * g + h, dacs_h, 0.0)
                dcb = dcb + d_m * dm
                dcbt = dcbt + d_mt * dmt
                dxh.append(_dot(mt.astype(BF16), dyp16))
            dxdt = jnp.where(lane < HEAD_DIM, dxh[0], dxh[1]) + dxdt_base[:, sl]
            dx_ref[:, sl] = dxdt * dt_exp[:, sl] + dsk_ref[:, sl] * dyp
            ddt_x = ddt_x + _dot2_data_lhs(dxdt * x[:, sl], _reduce_pair_matrix(g, p))
        dc_ref[...] = dc_acc + _dot(dcb.astype(BF16), bb16)
        db_ref[...] = db_acc + _dot(dcbt.astype(BF16), cb16)
        row = lax.broadcasted_iota(jnp.int32, (CHUNK, LANES), 0)
        return dacs + jnp.where(row == CHUNK - 1, dtot, 0.0), ddt_x

    wide, b_spec, c_spec, rows_spec, state_spec = _ssd_specs(lambda c: last - c)
    heads_spec = pl.BlockSpec((CHUNK, LANES), lambda c: (last - c, 0))
    vec_spec = pl.BlockSpec((1, LANES), lambda c: (0, 0))
    vec_shape = jax.ShapeDtypeStruct((1, LANES), F32)
    return _call(
        body, name=name,
        out_shape=(jax.ShapeDtypeStruct((rows, D_XBC), F32), jax.ShapeDtypeStruct((rows, LANES), F32),
                   vec_shape, vec_shape),
        grid=(nc,),
        in_specs=[wide, b_spec, c_spec, wide, wide, rows_spec, heads_spec, vec_spec,
                  pl.BlockSpec((1, D_INNER), lambda c: (0, 0)), wide, state_spec],
        out_specs=(pl.BlockSpec((CHUNK, D_XBC), lambda c: (last - c, 0)), heads_spec, vec_spec, vec_spec),
        scratch_shapes=[pltpu.VMEM((N_GROUPS, D_STATE, GROUP_W), F32)],
        operands=[xbc, xbc, xbc, dt_exp, acs_exp, acs_rows, dt, a128, dskexp, dy, states],
        semantics=("arbitrary",), steps=steps)


def _attn_visible(b, heads=1):
    row = jnp.bitwise_and(lax.broadcasted_iota(jnp.int32, (heads * CHUNK, 3 * CHUNK), 0), CHUNK - 1)
    col = lax.broadcasted_iota(jnp.int32, (heads * CHUNK, 3 * CHUNK), 1)
    bb = b + jnp.zeros_like(col)
    meta = (col < CHUNK) & (bb >= 1) & (col >= PAD_ROWS)
    prev = (col >= CHUNK) & (col < 2 * CHUNK) & (bb >= 2) & ((col - CHUNK) > row)
    cur = (col >= 2 * CHUNK) & ((col - 2 * CHUNK) <= row) & ((bb >= 1) | ((col - 2 * CHUNK) >= PAD_ROWS))
    return meta | prev | cur


def _attn_visible4(b):
    return _attn_visible(b, 4)


def _stack_heads(q_ref, sink_ref, kvh, scale):
    lane = lax.broadcasted_iota(jnp.int32, (CHUNK, LANES), 1)
    parts, sinks = [], []
    for pp in range(2):
        pair = kvh * 2 + pp
        qp = q_ref[:, pair * LANES:(pair + 1) * LANES] * scale
        for e in range(2):
            mine = (lane < HEAD_DIM) if e == 0 else (lane >= HEAD_DIM)
            parts.append(jnp.where(mine, qp, 0.0).astype(BF16))
            sinks.append(jnp.full((CHUNK, 1), sink_ref[2 * pair + e], F32))
    return jnp.concatenate(parts, axis=0), jnp.concatenate(sinks, axis=0)


def _attn_operands(q_ref, k0, kp, kc, v0, vp, vc, sink_ref):
    kcat, vcat, q4, sink4 = [], [], [], []
    for kvh in range(N_KV_HEADS):
        ksl = slice(kvh * LANES, (kvh + 1) * LANES)
        kcat.append(jnp.concatenate([k0[:, ksl], kp[:, ksl], kc[:, ksl]], axis=0).astype(BF16))
        vcat.append(jnp.concatenate([v0[:, ksl], vp[:, ksl], vc[:, ksl]], axis=0).astype(BF16))
        stacked, sinks = _stack_heads(q_ref, sink_ref, kvh, ATTN_SCALE)
        q4.append(stacked)
        sink4.append(sinks)
    return kcat, vcat, q4, sink4


def _attn_probs(q4, kcat, visible, sink4):
    heads = range(N_KV_HEADS)
    s = [jnp.where(visible, _dot_nt(q4[h], kcat[h]), NEG_INF) for h in heads]
    m = [jnp.maximum(jnp.max(s[h], axis=-1, keepdims=True), sink4[h]) for h in heads]
    pe = [jnp.exp(s[h] - m[h]) for h in heads]
    pe_sink = [jnp.exp(sink4[h] - m[h]) for h in heads]
    inv = [1.0 / (jnp.sum(pe[h], axis=-1, keepdims=True) + pe_sink[h]) for h in heads]
    return [pe[h] * inv[h] for h in heads], [pe_sink[h] * inv[h] for h in heads]


def _unstack_pairs(stacked, pp):
    lane = lax.broadcasted_iota(jnp.int32, (CHUNK, LANES), 1)
    return jnp.where(lane < HEAD_DIM, stacked[(2 * pp) * CHUNK:(2 * pp + 1) * CHUNK],
                     stacked[(2 * pp + 1) * CHUNK:(2 * pp + 2) * CHUNK])


def _attn_specs(colblock):
    blk = lambda f: pl.BlockSpec((CHUNK, 2 * D_KV), f)
    return [blk(lambda b: (0, colblock)), blk(lambda b: (jnp.maximum(b - 1, 0), colblock)), blk(lambda b: (b, colblock))]


def _attn_fwd(name, q, kv2, sinks, steps=()):
    rows = q.shape[0]

    def body(q_ref, k0, kp, kc, v0, vp, vc, sink_ref, o_ref):
        visible = _attn_visible4(pl.program_id(0))
        kcat, vcat, q4, sink4 = _attn_operands(q_ref, k0, kp, kc, v0, vp, vc, sink_ref)
        pn, _ = _attn_probs(q4, kcat, visible, sink4)
        o4 = [_dot(pn[h].astype(BF16), vcat[h]) for h in range(N_KV_HEADS)]
        for kvh in range(N_KV_HEADS):
            for pp in range(2):
                qsl = slice((kvh * 2 + pp) * LANES, (kvh * 2 + pp + 1) * LANES)
                o_ref[:, qsl] = _unstack_pairs(o4[kvh], pp).astype(BF16)

    return _call(
        body, name=name, out_shape=jax.ShapeDtypeStruct((rows, D_MODEL), BF16), grid=(rows // CHUNK,),
        in_specs=[pl.BlockSpec((CHUNK, D_MODEL), lambda b: (b, 0))] + _attn_specs(0) + _attn_specs(1)
        + [pl.BlockSpec(memory_space=pltpu.SMEM)],
        out_specs=pl.BlockSpec((CHUNK, D_MODEL), lambda b: (b, 0)),
        operands=[q, kv2, kv2, kv2, kv2, kv2, kv2, sinks], semantics=("parallel",), steps=steps)


def _attn_bwd(name, q, kv2, sinks, do, steps=()):
    rows = q.shape[0]

    def body(q_ref, k0, kp, kc, v0, vp, vc, sink_ref, do_ref,
             dq_ref, dkc_ref, dkp_ref, dvc_ref, dvp_ref, dkm_ref, dvm_ref, dsink_ref):
        @pl.when(pl.program_id(0) == 0)
        def _():
            dkm_ref[...] = jnp.zeros_like(dkm_ref)
            dvm_ref[...] = jnp.zeros_like(dvm_ref)
            dsink_ref[...] = jnp.zeros_like(dsink_ref)

        visible = _attn_visible4(pl.program_id(0))
        heads = range(N_KV_HEADS)
        lane1 = lax.broadcasted_iota(jnp.int32, (1, LANES), 1)
        kcat, vcat, q4, sink4 = _attn_operands(q_ref, k0, kp, kc, v0, vp, vc, sink_ref)
        do4 = [_stack_heads(do_ref, sink_ref, h, 1.0)[0] for h in heads]
        pn, psink = _attn_probs(q4, kcat, visible, sink4)
        dp = [_dot_nt(do4[h], vcat[h]) for h in heads]
        delta = [jnp.sum(pn[h] * dp[h], axis=-1, keepdims=True) for h in heads]
        ds16 = [(pn[h] * (dp[h] - delta[h])).astype(BF16) for h in heads]
        dq4 = [_dot(ds16[h], kcat[h]) for h in heads]
        dk_acc = [_dot_tn(ds16[h], q4[h]) for h in heads]
        dv_acc = [_dot_tn(pn[h].astype(BF16), do4[h]) for h in heads]
        dsink = jnp.zeros((1, LANES), F32)
        for kvh in heads:
            ksl = slice(kvh * LANES, (kvh + 1) * LANES)
            sink_terms = psink[kvh] * delta[kvh]
            for j in range(4):
                part = jnp.sum(sink_terms[j * CHUNK:(j + 1) * CHUNK], axis=0, keepdims=True)
                dsink = dsink - jnp.where(lane1 == kvh * 4 + j, part, 0.0)
            for pp in range(2):
                qsl = slice((kvh * 2 + pp) * LANES, (kvh * 2 + pp + 1) * LANES)
                dq_ref[:, qsl] = (_unstack_pairs(dq4[kvh], pp) * ATTN_SCALE).astype(BF16)
            dkm_ref[:, ksl] += dk_acc[kvh][0:CHUNK]
            dvm_ref[:, ksl] += dv_acc[kvh][0:CHUNK]
            dkp_ref[:, ksl] = dk_acc[kvh][CHUNK:2 * CHUNK]
            dvp_ref[:, ksl] = dv_acc[kvh][CHUNK:2 * CHUNK]
            dkc_ref[:, ksl] = dk_acc[kvh][2 * CHUNK:3 * CHUNK]
            dvc_ref[:, ksl] = dv_acc[kvh][2 * CHUNK:3 * CHUNK]
        dsink_ref[...] += dsink

    qspec = pl.BlockSpec((CHUNK, D_MODEL), lambda b: (b, 0))
    kvspec = pl.BlockSpec((CHUNK, 2 * D_KV), lambda b: (b, 0))
    fixed = pl.BlockSpec((CHUNK, 2 * D_KV), lambda b: (0, 0))
    kv_shape = jax.ShapeDtypeStruct((rows, 2 * D_KV), F32)
    meta_shape = jax.ShapeDtypeStruct((CHUNK, 2 * D_KV), F32)
    return _call(
        body, name=name,
        out_shape=(jax.ShapeDtypeStruct((rows, D_MODEL), BF16), kv_shape, kv_shape, kv_shape, kv_shape,
                   meta_shape, meta_shape, jax.ShapeDtypeStruct((1, LANES), F32)),
        grid=(rows // CHUNK,),
        in_specs=[qspec] + _attn_specs(0) + _attn_specs(1) + [pl.BlockSpec(memory_space=pltpu.SMEM), qspec],
        out_specs=(qspec, kvspec, kvspec, kvspec, kvspec, fixed, fixed, pl.BlockSpec((1, LANES), lambda b: (0, 0))),
        operands=[q, kv2, kv2, kv2, kv2, kv2, kv2, sinks, do], semantics=("arbitrary",), steps=steps)


def _kv_grad_combine(name, dk_cur, dk_prev, dk_meta, dv_cur, dv_prev, dv_meta):
    rows = dk_cur.shape[0]
    nb = rows // CHUNK
    width = 2 * D_KV

    def body(kc_ref, kp_ref, km_ref, vc_ref, vp_ref, vm_ref, o_ref):
        jj = pl.program_id(0) + jnp.zeros((CHUNK, 1), jnp.int32)
        for half, (c_ref, p_ref, m_ref) in enumerate(((kc_ref, kp_ref, km_ref), (vc_ref, vp_ref, vm_ref))):
            total = c_ref[...] + jnp.where(jj < nb - 1, p_ref[...], 0.0) + jnp.where(jj == 0, m_ref[...], 0.0)
            o_ref[:, half * width:(half + 1) * width] = total.astype(BF16)

    blk = lambda f: pl.BlockSpec((CHUNK, width), f)
    three = lambda: [blk(lambda j: (j, 0)), blk(lambda j: (jnp.minimum(j + 1, nb - 1), 0)), blk(lambda j: (0, 0))]
    return pl.pallas_call(
        body, name=name, out_shape=jax.ShapeDtypeStruct((rows, 2 * width), BF16), grid=(nb,),
        in_specs=three() + three(), out_specs=pl.BlockSpec((CHUNK, 2 * width), lambda j: (j, 0)),
        compiler_params=_cparams(("parallel",)),
    )(dk_cur, dk_prev, dk_meta, dv_cur, dv_prev, dv_meta)


def _adamw(name, w, g, m, v, steps=()):
    rows, width = w.shape
    tr = rows
    for cand in range(8, rows + 1, 8):
        if rows % cand == 0 and cand * width * 4 <= (1 << 20):
            tr = cand

    def body(*refs):
        _adamw_update(*refs)

    blk = pl.BlockSpec((tr, width), lambda i: (i, 0))
    shp = jax.ShapeDtypeStruct((rows, width), F32)
    return _call(body, name=name, out_shape=(shp, shp, shp), grid=(rows // tr,), in_specs=[blk] * 4,
                 out_specs=(blk,) * 3, operands=[w, g, m, v], semantics=("parallel",), steps=steps)


def _adamw_update(w_ref, g_ref, m_ref, v_ref, d_ref, mo_ref, vo_ref):
    gv = g_ref[...]
    mn = ADAM_B1 * m_ref[...] + (1.0 - ADAM_B1) * gv
    vn = ADAM_B2 * v_ref[...] + (1.0 - ADAM_B2) * (gv * gv)
    m_hat = mn / (1.0 - ADAM_B1 ** ADAM_STEP)
    v_hat = vn / (1.0 - ADAM_B2 ** ADAM_STEP)
    d_ref[...] = -ADAM_LR * (m_hat / (jnp.sqrt(v_hat) + ADAM_EPS) + ADAM_WD * w_ref[...])
    mo_ref[...] = mn
    vo_ref[...] = vn


def _adamw_small(name, ws, gs, ms, vs):
    n = len(ws)

    def body(*refs):
        for i in range(n):
            _adamw_update(*refs[i::n])

    shapes = [jax.ShapeDtypeStruct(a.shape, F32) for a in ws]
    outs = pl.pallas_call(body, name=name, out_shape=shapes * 3, in_specs=[VMEM_SPEC] * (4 * n),
                          out_specs=[VMEM_SPEC] * (3 * n), compiler_params=_cparams())(*ws, *gs, *ms, *vs)
    return outs[:n], outs[n:2 * n], outs[2 * n:]


def _ffn_fwd(tag, h, hn, p, i, plan):
    up_g, up_v, act = _ffn_up_conv(f"ffn{tag}_up", hn, plan.weight("f_w_up", i), p["f_conv_w"][i],
                                   p["f_conv_b"][i:i + 1], steps=plan.steps(f"ffn{tag}_up"))
    pre = _mm(f"ffn{tag}_down", act, plan.weight("f_w_down", i), "nn", steps=plan.steps(f"ffn{tag}_down"))
    return pre, (h, hn, up_g, up_v, act, pre)


def _ffn_bwd(tag, dpre, saved, p, i, plan):
    h, hn, up_g, up_v, act, pre = saved
    plan.grad("f_w_down", i, _mm(f"ffn{tag}_down_dw", act, dpre, "tn", out_dtype=BF16))
    dact = _mm(f"ffn{tag}_down_dx", dpre, plan.weight("f_w_down", i), "nt", steps=plan.steps(f"ffn{tag}_down_dx"))
    gwg, gwv, gbg, gbv, dhn, g_up = _ffn_conv_bwd(
        f"ffn{tag}_conv_bwd", up_g, up_v, dact, p["f_conv_w"][i], p["f_conv_b"][i:i + 1], hn,
        plan.weight("f_w_up", i), steps=plan.steps(f"ffn{tag}_conv_bwd"))
    g_cw, g_cb = jnp.concatenate([gwg, gwv], axis=1), jnp.concatenate([gbg, gbv], axis=1)
    plan.grad("f_w_up", i, g_up)
    return dhn, dict(f_conv_w=g_cw, f_conv_b=g_cb)


def _lanes_pad(a, width=LANES):
    return jnp.pad(a, [(0, 0)] * (a.ndim - 1) + [(0, width - a.shape[-1])])


def _dup_heads(w):
    rows = w.shape[0]
    w = w.reshape(rows, 2 * N_KV_HEADS, 1, HEAD_DIM)
    return jnp.broadcast_to(w, (rows, 2 * N_KV_HEADS, 2, HEAD_DIM)).reshape(rows, 4 * D_KV)


def _undup_heads(g):
    rows = g.shape[0]
    return g.reshape(rows, 2 * N_KV_HEADS, 2, HEAD_DIM).sum(axis=2).reshape(rows, 2 * D_KV)


def _local_step(x2, target, p, plan):
    seq = x2.shape[0]
    rows = seq + CHUNK
    g = {}

    h0 = jnp.concatenate([jnp.zeros((PAD_ROWS, D_MODEL), F32), p["meta_tokens"], x2], axis=0)

    w_in = plan.weight("a_w_in")
    w_dt = jnp.pad(w_in[D_MAIN:], ((0, LANES - SSM_HEADS), (0, 0)))
    dt_bias = _lanes_pad(p["a_dt_bias"])
    a128 = _lanes_pad(-jnp.exp(p["a_a_log"]))
    dskexp = jnp.repeat(p["a_d_skip"].reshape(SSM_HEADS), HEAD_DIM).reshape(1, D_INNER)

    hn0 = _rms_fwd("a_norm", h0, p["a_norm_pre"])
    zx = _mm("a_in_main", hn0, w_in, "nt", k_rows=D_MAIN, steps=plan.steps("a_in_main"))
    dtr = _mm("a_in_dt", hn0, w_dt, "nt")
    xbc = _conv4_fwd("a_conv", zx, p["a_conv_w"], p["a_conv_b"], steps=plan.steps("a_conv"))
    dt = _dt_fwd("a_dt", dtr, dt_bias)
    dt_exp, acs_exp, acs_rows = _ssd_prep("a_ssd_prep", dt, a128, steps=plan.steps("a_ssd_prep"))
    y, states = _ssd_fwd("a_ssd", xbc, dt_exp, acs_exp, acs_rows, dskexp, steps=plan.steps("a_ssd"))
    yn = _gate_fwd("a_gate", y, zx, p["a_gate_norm"], steps=plan.steps("a_gate"))
    mix = _mm("a_out", yn, plan.weight("a_w_out"), "nn", steps=plan.steps("a_out"))
    h1, (hn_f0,) = _resid_norm_fwd("a_resid", h0, mix, p["a_norm_post"], [p["f_norm_pre"][0:1]])

    pre_f0, ffn0 = _ffn_fwd("0", h1, hn_f0, p, 0, plan)
    h2, (hkv, hn2) = _resid_norm_fwd("ffn0_resid", h1, pre_f0, p["f_norm_post"][0:1], [p["kv_norm"], p["b_norm_pre"]])

    w_kv2 = _dup_heads(plan.weight("w_kv"))
    kv2 = _mm("kv_proj", hkv, w_kv2, "nn")
    q = _mm("b_q", hn2, plan.weight("b_w_q"), "nn")
    sinks = p["b_sinks"].reshape(N_Q_HEADS)
    o = _attn_fwd("b_attn", q, kv2, sinks, steps=plan.steps("b_attn"))
    attn = _mm("b_o", o, plan.weight("b_w_o"), "nn", steps=plan.steps("b_o"))
    h3, (hn_f1,) = _resid_norm_fwd("b_resid", h2, attn, p["b_norm_post"], [p["f_norm_pre"][1:2]])

    pre_f1, ffn1 = _ffn_fwd("1", h3, hn_f1, p, 1, plan)
    dh, loss_vec, dpre_f1, g_post1 = _resid_norm_loss("ffn1_resid_loss", h3, pre_f1, p["f_norm_post"][1:2], target)
    loss = loss_vec[0, 0]

    dhn_f1, g1 = _ffn_bwd("1", dpre_f1, ffn1, p, 1, plan)
    dh, g_pre1, dpre, g["b_norm_post"] = _norm_bwd_add("ffn1_norm_bwd", dh, dhn_f1, h3, p["f_norm_pre"][1:2],
                                                        then=(attn, p["b_norm_post"]))
    plan.grad("b_w_o", None, _mm("b_o_dw", o, dpre, "tn", out_dtype=BF16))
    do = _mm("b_o_dx", dpre, plan.weight("b_w_o"), "nt", steps=plan.steps("b_o_dx"))
    dq, dkc, dkp, dvc, dvp, dkm, dvm, dsink = _attn_bwd("b_attn_bwd", q, kv2, sinks, do, steps=plan.steps("b_attn_bwd"))
    g["b_sinks"] = dsink[:, :N_Q_HEADS]
    dhn2 = _mm("b_q_dx", dq, plan.weight("b_w_q"), "nt")
    plan.grad("b_w_q", None, _mm("b_q_dw", hn2, dq, "tn", out_dtype=BF16))
    dh, g["b_norm_pre"] = _norm_bwd_add("b_norm_bwd", dh, dhn2, h2, p["b_norm_pre"])
    dkv2 = _kv_grad_combine("kv_grad", dkc, dkp, dkm, dvc, dvp, dvm)
    dhkv = _mm("kv_proj_dx", dkv2, w_kv2, "nt")
    plan.grad("w_kv", None, _undup_heads(_mm("kv_proj_dw", hkv, dkv2, "tn")))
    dh, g["kv_norm"], dpre_f0, g_post0 = _norm_bwd_add("kv_norm_bwd", dh, dhkv, h2, p["kv_norm"],
                                                       then=(pre_f0, p["f_norm_post"][0:1]))

    dhn_f0, g0 = _ffn_bwd("0", dpre_f0, ffn0, p, 0, plan)
    dh, g_pre0, dpre, g["a_norm_post"] = _norm_bwd_add("ffn0_norm_bwd", dh, dhn_f0, h1, p["f_norm_pre"][0:1],
                                                        then=(mix, p["a_norm_post"]))
    g["f_norm_post"] = jnp.concatenate([g_post0, g_post1], axis=0)
    g["f_norm_pre"] = jnp.concatenate([g_pre0, g_pre1], axis=0)
    g["f_conv_w"] = jnp.stack([g0["f_conv_w"], g1["f_conv_w"]])
    g["f_conv_b"] = jnp.concatenate([g0["f_conv_b"], g1["f_conv_b"]], axis=0)
    plan.grad("a_w_out", None, _mm("a_out_dw", yn, dpre, "tn", out_dtype=BF16))
    dyn = _mm("a_out_dx", dpre, plan.weight("a_w_out"), "nt", steps=plan.steps("a_out_dx"))
    dy, dzx, g["a_gate_norm"] = _gate_bwd("a_gate_bwd", dyn, y, zx, p["a_gate_norm"])
    dxbc, ddt, dalog, ddsk = _ssd_bwd("a_ssd_bwd", xbc, dt_exp, acs_exp, acs_rows, dt, a128, dskexp, dy, states,
                                      steps=plan.steps("a_ssd_bwd"))
    g["a_a_log"] = dalog[:, :SSM_HEADS]
    g["a_d_skip"] = ddsk[:, :SSM_HEADS]
    ddtr, dbias = _dt_bwd("a_dt_bwd", ddt, dtr, dt_bias)
    g["a_dt_bias"] = dbias[:, :SSM_HEADS]
    dzx, g["a_conv_w"], g["a_conv_b"] = _conv4_bwd("a_conv_bwd", zx, dxbc, p["a_conv_w"], p["a_conv_b"], dzx)
    g_in = _mm("a_in_main_dw", dzx, hn0, "tn", out_dtype=BF16, out_rows=D_IN_PROJ, steps=plan.steps("a_in_main_dw"))
    plan.grad("a_w_in", None, _tn_rows_into("a_in_dt_dw", ddtr, hn0, g_in, D_MAIN, SSM_HEADS))
    dhn0 = _mm("a_in_dt_dx", ddtr, w_dt, "nn", steps=plan.steps("a_in_dt_dx"))
    dhn0 = _mm("a_in_main_dx", dzx, w_in, "nn", acc=dhn0, steps=plan.steps("a_in_main_dx"))
    dh_first, grad_x, g["a_norm_pre"] = _norm_bwd_add("a_norm_bwd", dh, dhn0, h0, p["a_norm_pre"],
                                                      split_first_block=True, steps=plan.steps("a_norm_bwd"))
    g["meta_tokens"] = dh_first[PAD_ROWS:]
    return loss, grad_x, g


ANY = pl.BlockSpec(memory_space=pl.ANY)
VMEM_SPEC = pl.BlockSpec(memory_space=pltpu.VMEM)


def _step_gather_small(slots):
    def copies(outs, send_sems, recv_sems, received):
        x, y, c = _place()
        me = 2 * x + y
        for j, (cx, cy) in enumerate(_other_chips(x, y)):
            slot = outs[0].at[2 * cx + cy if received else me]
            yield _remote(slot, slot, send_sems, recv_sems, j, (cx, cy, c))

    def start(ins, outs, send_sems, recv_sems):
        for cp in copies(outs, send_sems, recv_sems, False):
            cp.start()

    def finish(ins, outs, send_sems, recv_sems):
        for cp in copies(outs, send_sems, recv_sems, True):
            cp.wait_recv()
        for cp in copies(outs, send_sems, recv_sems, False):
            cp.wait_send()

    return _Step([slots], [_like(slots)], {0: 0}, 3, start, finish)


def _row_block(rows, width, itemsize, align, budget=2 << 20):
    best = rows
    for cand in range(align, rows + 1, align):
        if rows % cand == 0 and cand * width * itemsize <= budget:
            best = cand
    return best


def _cast_into_slot(name, chip, w, layer=None):
    rows, width = w.shape[-2:]
    tr = _row_block(rows, width, 4, 16)
    if layer is None:
        in_spec = pl.BlockSpec((tr, width), lambda i, chip_ref: (i, 0))
    else:
        in_spec = pl.BlockSpec((None, tr, width), lambda i, chip_ref: (layer, i, 0))

    def body(chip_ref, w_ref, o_ref):
        o_ref[...] = w_ref[...].astype(BF16)

    return pl.pallas_call(
        body, name=name, out_shape=jax.ShapeDtypeStruct((N_CHIPS, rows, width), BF16),
        grid_spec=pltpu.PrefetchScalarGridSpec(
            num_scalar_prefetch=1, grid=(rows // tr,), in_specs=[in_spec],
            out_specs=pl.BlockSpec((None, tr, width), lambda i, chip_ref: (chip_ref[0], i, 0))),
        compiler_params=_cparams(("parallel",)),
    )(chip, w)


def _allreduce_small(name, vec, steps=()):
    rows = -(-vec.shape[0] // (2 * SUBLANES)) * (2 * SUBLANES)
    hr = rows // 2
    padded = jnp.pad(vec, ((0, rows - vec.shape[0]), (0, 0)))

    def body(v_ref, o_ref, theirs, pair, by_chip, send_sems, recv_sems):
        x, y, c = _place()
        me = 2 * x + y
        sibling = (x, y, 1 - c)
        mine = pl.ds(pl.multiple_of(c * hr, SUBLANES), hr)
        other = pl.ds(pl.multiple_of((1 - c) * hr, SUBLANES), hr)

        swap = _remote(v_ref, theirs, send_sems, recv_sems, 0, sibling)
        swap.start()
        swap.wait()
        south = (c + jnp.zeros((1, 1), jnp.int32)) == 0
        pair[...] = jnp.where(south, v_ref[...], theirs[...]) + jnp.where(south, theirs[...], v_ref[...])

        by_chip[me] = pair[mine, :]
        sends = [_remote(by_chip.at[me], by_chip.at[me], send_sems, recv_sems, 1 + j, (cx, cy, c))
                 for j, (cx, cy) in enumerate(_other_chips(x, y))]
        for cp in sends:
            cp.start()
        for j, (cx, cy) in enumerate(_other_chips(x, y)):
            _remote(by_chip.at[me], by_chip.at[2 * cx + cy], send_sems, recv_sems, 1 + j, (cx, cy, c)).wait_recv()
        for cp in sends:
            cp.wait_send()
        total = by_chip[0]
        for s in range(1, N_CHIPS):
            total = total + by_chip[s]

        o_ref[mine, :] = total
        back = _remote(o_ref.at[mine], o_ref.at[mine], send_sems, recv_sems, 4, sibling)
        back.start()
        _remote(o_ref.at[other], o_ref.at[other], send_sems, recv_sems, 4, sibling).wait_recv()
        back.wait_send()

    out = _call(
        body, name=name, out_shape=jax.ShapeDtypeStruct((rows, LANES), F32), grid=(),
        in_specs=[VMEM_SPEC], out_specs=VMEM_SPEC, operands=[padded],
        scratch_shapes=[pltpu.VMEM((rows, LANES), F32), pltpu.VMEM((rows, LANES), F32),
                        pltpu.VMEM((N_CHIPS, hr, LANES), F32), pltpu.SemaphoreType.DMA((5,)),
                        pltpu.SemaphoreType.DMA((5,))],
        steps=steps)
    return out[:vec.shape[0]]


def _rs_pair_add(name, place, grads, partner, split="rows"):
    _, half_rows, width = partner.shape
    tr = _row_block(half_rows, width, 2, 16)
    nb = half_rows // tr
    if split == "rows":
        mine = pl.BlockSpec((None, tr, width), lambda s, i, pr: (s, pr[1] * nb + i, 0))
    else:
        mine = pl.BlockSpec((None, tr, width), lambda s, i, pr: (s, i, pr[1]))

    def body(place_ref, g_ref, p_ref, o_ref):
        o_ref[...] = (g_ref[...].astype(F32) + p_ref[...].astype(F32)).astype(BF16)

    return pl.pallas_call(
        body, name=name, out_shape=jax.ShapeDtypeStruct(partner.shape, BF16),
        grid_spec=pltpu.PrefetchScalarGridSpec(
            num_scalar_prefetch=1, grid=(N_CHIPS, nb),
            in_specs=[mine, pl.BlockSpec((None, tr, width), lambda s, i, pr: (s, i, 0))],
            out_specs=pl.BlockSpec((None, tr, width), lambda s, i, pr: (s, i, 0))),
        compiler_params=_cparams(("parallel", "parallel")),
    )(place, grads, partner)


def _rs_chip_add(name, place, mine, others, split="rows"):
    _, half_rows, width = mine.shape
    tr = _row_block(half_rows, width, 4, 16, budget=1 << 20)
    nb = half_rows // tr
    if split == "rows":
        out_shape, out_spec = (2 * half_rows, width), pl.BlockSpec((tr, width), lambda i, pr: (pr[1] * nb + i, 0))
    else:
        out_shape, out_spec = (half_rows, 2 * width), pl.BlockSpec((tr, width), lambda i, pr: (i, pr[1]))

    def body(place_ref, q_ref, r_ref, o_ref):
        acc = q_ref[...].astype(F32)
        for j in range(3):
            acc = acc + r_ref[j].astype(F32)
        o_ref[...] = acc

    return pl.pallas_call(
        body, name=name, out_shape=jax.ShapeDtypeStruct(out_shape, F32),
        grid_spec=pltpu.PrefetchScalarGridSpec(
            num_scalar_prefetch=1, grid=(nb,),
            in_specs=[pl.BlockSpec((None, tr, width), lambda i, pr: (pr[0], i, 0)),
                      pl.BlockSpec((3, tr, width), lambda i, pr: (0, i, 0))],
            out_specs=out_spec),
        compiler_params=_cparams(("parallel",)),
    )(place, mine, others)


WEIGHTS = ["meta_tokens", "a_norm_pre", "a_w_in", "a_conv_w", "a_conv_b", "a_dt_bias", "a_a_log", "a_d_skip",
           "a_gate_norm", "a_w_out", "a_norm_post", "kv_norm", "w_kv", "b_norm_pre", "b_w_q", "b_sinks", "b_w_o",
           "b_norm_post", "f_norm_pre", "f_w_up", "f_conv_w", "f_conv_b", "f_w_down", "f_norm_post"]
FULL_SHAPE = {
    "meta_tokens": (16, 1024), "a_norm_pre": (1, 1024), "a_w_in": (1, 1024, 5152), "a_conv_w": (1, 4, 3072),
    "a_conv_b": (1, 3072), "a_dt_bias": (1, 32), "a_a_log": (1, 32), "a_d_skip": (1, 32), "a_gate_norm": (1, 2048),
    "a_w_out": (1, 2048, 1024), "a_norm_post": (1, 1024), "kv_norm": (1024,), "w_kv": (1024, 512),
    "b_norm_pre": (1, 1024), "b_w_q": (1, 1024, 1024), "b_sinks": (1, 16), "b_w_o": (1, 1024, 1024),
    "b_norm_post": (1, 1024), "f_norm_pre": (2, 1024), "f_w_up": (2, 1024, 5632), "f_conv_w": (2, 3, 5632),
    "f_conv_b": (2, 5632), "f_w_down": (2, 2816, 1024), "f_norm_post": (2, 1024),
}
SHARD_AXIS = {
    "meta_tokens": 1, "a_norm_pre": 1, "a_w_in": 2, "a_conv_w": 2, "a_conv_b": 1, "a_dt_bias": None, "a_a_log": None,
    "a_d_skip": None, "a_gate_norm": 1, "a_w_out": 1, "a_norm_post": 1, "kv_norm": None, "w_kv": 0, "b_norm_pre": None,
    "b_w_q": 1, "b_sinks": None, "b_w_o": 1, "b_norm_post": None, "f_norm_pre": None, "f_w_up": 2, "f_conv_w": 2,
    "f_conv_b": None, "f_w_down": 1, "f_norm_post": None,
}
BIG = ["a_w_in", "a_w_out", "w_kv", "b_w_q", "b_w_o", "f_w_up", "f_w_down"]
SMALL = [n for n in WEIGHTS if n not in BIG]
SMALL_SHARDED = [n for n in SMALL if SHARD_AXIS[n] is not None]


def _shard_shape(name):
    shape = list(FULL_SHAPE[name])
    if SHARD_AXIS[name] is not None:
        shape[SHARD_AXIS[name]] //= N_CHIPS
    return tuple(shape)


def _numel(shape):
    return int(math.prod(shape))


SUBLANES = 8


def _packed_rows(shape):
    rows = -(-_numel(shape) // LANES)
    return -(-rows // SUBLANES) * SUBLANES


def _pack(arrays):
    parts = []
    for a in arrays:
        size, rows = _numel(a.shape), _packed_rows(a.shape)
        if size % LANES == 0:
            part = jnp.pad(a.reshape(size // LANES, LANES), ((0, rows - size // LANES), (0, 0)))
        else:
            part = jnp.pad(a.reshape(-1), (0, rows * LANES - size)).reshape(rows, LANES)
        parts.append(part)
    return jnp.concatenate(parts, axis=0)


def _unpack(packed, names, shape_of):
    out, off = {}, 0
    lead = packed.shape[:-2]
    for n in names:
        shape = tuple(shape_of(n))
        size, rows = _numel(shape), _packed_rows(shape)
        part = packed[..., off:off + rows, :]
        if size % LANES == 0:
            out[n] = part[..., :size // LANES, :].reshape(lead + shape)
        else:
            out[n] = part.reshape(lead + (rows * LANES,))[..., :size].reshape(lead + shape)
        off += rows
    return out


def _split_chips(name, full):
    ax = SHARD_AXIS[name]
    shape = full.shape
    cut = shape[:ax] + (N_CHIPS, shape[ax] // N_CHIPS) + shape[ax + 1:]
    return jnp.moveaxis(full.reshape(cut), ax, 0)


def _join_chips(name, stacked):
    ax = SHARD_AXIS[name]
    moved = jnp.moveaxis(stacked, 0, ax)
    shape = moved.shape
    return moved.reshape(shape[:ax] + (shape[ax] * shape[ax + 1],) + shape[ax + 2:])


def _as2d(a):
    return a.reshape(-1, a.shape[-1])


BUFFERS = [("a_w_in", "a_w_in", None), ("a_w_out", "a_w_out", None), ("w_kv", "w_kv", None),
           ("b_w_q", "b_w_q", None), ("b_w_o", "b_w_o", None), ("f_w_up0", "f_w_up", 0), ("f_w_up1", "f_w_up", 1),
           ("f_w_down0", "f_w_down", 0), ("f_w_down1", "f_w_down", 1)]


TRANSPOSED = ("a_w_in",)
SPLIT = {"a_w_in": "cols"}


def _local_shard(arrays, weight, layer):
    if weight in TRANSPOSED:
        return arrays[weight][0].T
    return _as2d(arrays[weight]) if layer is None else arrays[weight]


def _weight_from_gathered(weight, buf):
    if weight == "f_w_up":
        return buf
    return buf.reshape(N_CHIPS * buf.shape[1], buf.shape[2])


def _gathered_from_grad(weight, g):
    if weight == "f_w_up":
        return g
    return g.reshape(N_CHIPS, g.shape[0] // N_CHIPS, g.shape[1]).astype(BF16)


GATHER_SCHEDULE = {
    "a_in_main": [("ici", ["a_w_out"])],
    "a_conv": [("d2d", ["a_w_out"]), ("ici", ["f_w_down0"])],
    "a_ssd_prep": [("d2d", ["f_w_down0"]), ("ici_near", ["f_w_up0"])],
    "a_ssd": [("ici_far", ["f_w_up0"])],
    "a_gate": [("d2d", ["f_w_up0"]), ("ici", ["w_kv", "b_w_q", "b_w_o"])],
    "ffn0_up": [("d2d", ["w_kv", "b_w_q", "b_w_o"]), ("ici", ["f_w_down1"])],
    "ffn0_down": [("d2d", ["f_w_down1"])],
    "b_attn": [("ici", ["f_w_up1"])],
    "b_o": [("d2d", ["f_w_up1"])],
}
REDUCE_SCHEDULE = {
    "b_attn_bwd": [("all", ["f_w_down1", "f_w_up1", "b_w_o"])],
    "ffn0_conv_bwd": [("all", ["b_w_q", "w_kv", "f_w_down0"])],
    "a_ssd_bwd": [("all", ["f_w_up0", "a_w_out"])],
    "a_in_main_dx": [("near", ["a_w_in"])],
    "a_norm_bwd": [("far", ["a_w_in"])],
}
ICI_PEERS = {"ici": ALL_PEERS, "ici_near": NEAR_PEERS, "ici_far": FAR_PEERS,
             "all": ALL_PEERS, "near": NEAR_PEERS, "far": FAR_PEERS}
PAIR_SCHEDULE = {
    "b_o_dx": ["f_w_down1", "f_w_up1", "b_w_o"],
    "ffn0_down_dx": ["b_w_q", "w_kv", "f_w_down0"],
    "a_out_dx": ["f_w_up0", "a_w_out"],
    "a_in_dt_dx": ["a_w_in"],
}
SWAP_SCHEDULE = {"a_in_main_dw": ["f_w_down1", "f_w_up1", "b_w_o", "b_w_q", "w_kv", "f_w_down0", "f_w_up0", "a_w_out"]}


def _buffer_of(weight, layer):
    return weight if layer is None else f"{weight}{layer}"


class _Pipeline:
    def __init__(self, place, slots):
        self.place = place
        self.slots = dict(slots)
        self.running = []
        self.grads = {}
        self.theirs = {}
        self.partials = {}
        self.peers = {}
        self.reduced = {}

    def _collect(self):
        for step, buffers, table in self.running:
            table.update(zip(buffers, step.results))
        self.running = []

    @staticmethod
    def _splits(buffers):
        return [SPLIT.get(b, "rows") for b in buffers]

    def gather_now(self, name, buffers, also=()):
        step = _step_gather_full([self.slots[b] for b in buffers], self._splits(buffers))
        _run_steps(name, [step, *also])
        self.slots.update(zip(buffers, step.results))

    def weight(self, name, layer=None):
        self._collect()
        return _weight_from_gathered(name, self.slots[_buffer_of(name, layer)])

    def grad(self, name, layer, g):
        self.grads[_buffer_of(name, layer)] = _gathered_from_grad(name, g)

    def steps(self, kernel):
        self._collect()
        steps = []
        for phase, buffers in GATHER_SCHEDULE.get(kernel, []):
            bufs, splits = [self.slots[b] for b in buffers], self._splits(buffers)
            step = (_step_gather_d2d(bufs, splits) if phase == "d2d"
                    else _step_gather_ici(bufs, splits, ICI_PEERS[phase]))
            self.running.append((step, buffers, self.slots))
            steps.append(step)
        buffers = PAIR_SCHEDULE.get(kernel)
        if buffers:
            step = _step_pair_exchange([self.grads[b] for b in buffers], self._splits(buffers))
            self.running.append((step, buffers, self.theirs))
            steps.append(step)
        for part, buffers in REDUCE_SCHEDULE.get(kernel, []):
            for b in buffers:
                if b not in self.partials:
                    self.partials[b] = _rs_pair_add("reduce_pair_add_" + b, self.place, self.grads[b], self.theirs[b],
                                                    SPLIT.get(b, "rows"))
            started = [self.peers[b] for b in buffers] if all(b in self.peers for b in buffers) else None
            step = _step_chip_exchange([self.partials[b] for b in buffers], ICI_PEERS[part], into=started)
            self.running.append((step, buffers, self.peers))
            steps.append(step)
        buffers = SWAP_SCHEDULE.get(kernel)
        if buffers:
            step = self._swap_step(buffers)
            self.running.append((step, buffers, self.reduced))
            steps.append(step)
        return steps

    def _swap_step(self, buffers):
        halves = [_rs_chip_add("reduce_chip_add_" + b, self.place, self.partials[b], self.peers[b], SPLIT.get(b, "rows"))
                  for b in buffers]
        return _step_pair_gather(halves, self._splits(buffers))

    def shard(self, buffer):
        self._collect()
        return self.reduced[buffer]

    def last_step(self):
        self._collect()
        rest = [b for b, _, _ in BUFFERS if b not in self.reduced]
        step = self._swap_step(rest)
        self.running.append((step, rest, self.reduced))
        return step


def kernel(x, meta_tokens, a_norm_pre, a_w_in, a_conv_w, a_conv_b, a_dt_bias, a_a_log, a_d_skip, a_gate_norm, a_w_out, a_norm_post, kv_norm, w_kv, b_norm_pre, b_w_q, b_sinks, b_w_o, b_norm_post, f_norm_pre, f_w_up, f_conv_w, f_conv_b, f_w_down, f_norm_post, loss_target, m_meta_tokens, m_a_norm_pre, m_a_w_in, m_a_conv_w, m_a_conv_b, m_a_dt_bias, m_a_a_log, m_a_d_skip, m_a_gate_norm, m_a_w_out, m_a_norm_post, m_kv_norm, m_w_kv, m_b_norm_pre, m_b_w_q, m_b_sinks, m_b_w_o, m_b_norm_post, m_f_norm_pre, m_f_w_up, m_f_conv_w, m_f_conv_b, m_f_w_down, m_f_norm_post, v_meta_tokens, v_a_norm_pre, v_a_w_in, v_a_conv_w, v_a_conv_b, v_a_dt_bias, v_a_a_log, v_a_d_skip, v_a_gate_norm, v_a_w_out, v_a_norm_post, v_kv_norm, v_w_kv, v_b_norm_pre, v_b_w_q, v_b_sinks, v_b_w_o, v_b_norm_post, v_f_norm_pre, v_f_w_up, v_f_conv_w, v_f_conv_b, v_f_w_down, v_f_norm_post):
    given = dict(locals())
    w = {n: given[n] for n in WEIGHTS}
    mom = {n: given["m_" + n] for n in WEIGHTS}
    var = {n: given["v_" + n] for n in WEIGHTS}
    chip = 2 * lax.axis_index("x") + lax.axis_index("y")
    core = lax.axis_index("c")
    place = jnp.stack([chip, core]).astype(jnp.int32)

    small_mine = _pack([w[n] for n in SMALL_SHARDED])
    small_step = _step_gather_small(lax.dynamic_update_slice(
        jnp.zeros((N_CHIPS,) + small_mine.shape, F32), small_mine[None], (chip, 0, 0)))
    slots = {b: _cast_into_slot("cast_" + b, place, _local_shard(w, wn, layer), layer) for b, wn, layer in BUFFERS}
    pipeline = _Pipeline(place, slots)
    pipeline.gather_now("gather_first", ["a_w_in"], also=[small_step])
    small_parts = _unpack(small_step.results[0], SMALL_SHARDED, _shard_shape)
    p = {}
    for n in SMALL:
        p[n] = _join_chips(n, small_parts[n]) if n in SMALL_SHARDED else w[n]
    p["a_conv_w"] = p["a_conv_w"][0]
    p["kv_norm"] = p["kv_norm"].reshape(1, D_MODEL)

    loss_local, grad_x, g = _local_step(x[0], loss_target[0], p, pipeline)

    small_sum = _allreduce_small("reduce_small", _pack([g[n].reshape(FULL_SHAPE[n]) for n in SMALL]
                                                       + [loss_local.reshape(1, 1)]), steps=[pipeline.last_step()])
    small_red = _unpack(small_sum, SMALL + ["loss"], lambda n: (1, 1) if n == "loss" else FULL_SHAPE[n])
    loss = small_red["loss"][0, 0]
    grads = {}
    for n in SMALL:
        if SHARD_AXIS[n] is None:
            grads[n] = small_red[n]
        else:
            grads[n] = lax.dynamic_index_in_dim(_split_chips(n, small_red[n]), chip, 0, keepdims=False)

    delta, new_m, new_v = {}, {}, {}
    for n in BIG:
        shape = _shard_shape(n)
        if n in TRANSPOSED:
            g2d = pipeline.shard(n)
            w2d, m2d, v2d = (arrays[n][0].T for arrays in (w, mom, var))
            back = lambda a: a.T.reshape(shape)
        else:
            g2d = (jnp.concatenate([pipeline.shard(n + "0"), pipeline.shard(n + "1")], axis=0)
                   if n in ("f_w_up", "f_w_down") else pipeline.shard(n))
            w2d, m2d, v2d = (_as2d(arrays[n]) for arrays in (w, mom, var))
            back = lambda a: a.reshape(shape)
        d, m2, v2 = _adamw("adamw_" + n, w2d, g2d, m2d, v2d, steps=pipeline.steps("adamw_" + n))
        grads[n], delta[n], new_m[n], new_v[n] = back(g2d), back(d), back(m2), back(v2)
    at_least_2d = lambda n: (1,) * (2 - len(_shard_shape(n))) + _shard_shape(n)
    outs = _adamw_small("adamw_small", *[[src[n].reshape(at_least_2d(n)) for n in SMALL] for src in (w, grads, mom, var)])
    for dst, arrays in zip((delta, new_m, new_v), outs):
        dst.update({n: a.reshape(_shard_shape(n)) for n, a in zip(SMALL, arrays)})

    return (loss, grad_x[None], *[grads[n].reshape(_shard_shape(n)) for n in WEIGHTS],
            *[delta[n] for n in WEIGHTS], *[new_m[n] for n in WEIGHTS], *[new_v[n] for n in WEIGHTS])
```

```python
import functools
import math

import jax
import jax.numpy as jnp
from jax import lax
from jax.experimental import pallas as pl
from jax.experimental.pallas import tpu as pltpu

F32, BF16 = jnp.float32, jnp.bfloat16
MESH = pl.DeviceIdType.MESH

D_MODEL = 1024
N_META = 16
CHUNK = 128
PAD_ROWS = CHUNK - N_META
D_INNER = 2048
D_STATE = 128
N_GROUPS = 4
HEADS_PER_GROUP = 8
SSM_HEADS = 32
HEAD_DIM = 64
D_BC = N_GROUPS * D_STATE
D_XBC = D_INNER + 2 * D_BC
D_MAIN = D_INNER + D_XBC
D_IN_PROJ = D_MAIN + SSM_HEADS
GROUP_W = HEADS_PER_GROUP * HEAD_DIM
SSM_CONV = 4
D_FF = 2816
FFN_CONV = 3
N_Q_HEADS = 16
N_KV_HEADS = 4
D_KV = 256
ATTN_SCALE = 1.0 / math.sqrt(HEAD_DIM)
RMS_EPS = 1e-6
NEG_INF = -1e30
LANES = 128
VMEM_LIMIT = 51 * 1024 * 1024

ADAM_LR, ADAM_B1, ADAM_B2, ADAM_EPS, ADAM_WD, ADAM_STEP = 0.001, 0.9, 0.999, 1e-08, 0.01, 10

N_CHIPS = 4


def _cparams(sem=None):
    return pltpu.CompilerParams(dimension_semantics=sem, vmem_limit_bytes=VMEM_LIMIT)


def _tile(n, cands=(512, 256, 128)):
    for t in cands:
        if n % t == 0:
            return t
    return n


def _row_tile(rows, width):
    for t in (544, 272):
        if rows % t == 0 and t * width * 4 <= (3 << 20):
            return t
    return 128


def _rows_mask(i, tm):
    rows = i * tm + lax.broadcasted_iota(jnp.int32, (tm, 1), 0)
    return rows >= PAD_ROWS


def _dot(a, b):
    return jnp.dot(a, b, preferred_element_type=F32)


def _dot_nt(a, b):
    return lax.dot_general(a, b, (((1,), (1,)), ((), ())), preferred_element_type=F32)


def _dot_tn(a, b):
    return lax.dot_general(a, b, (((0,), (0,)), ((), ())), preferred_element_type=F32)


def _sigmoid(x):
    return 1.0 / (1.0 + jnp.exp(-x))


def _place():
    return lax.axis_index("x"), lax.axis_index("y"), lax.axis_index("c")


def _other_chips(x, y):
    return [(1 - x, y), (x, 1 - y), (1 - x, 1 - y)]


class _Step:
    def __init__(self, ins, outs, aliases, n_sems, start, finish):
        self.ins, self.outs, self.aliases, self.n_sems = list(ins), list(outs), dict(aliases), n_sems
        self.start, self.finish = start, finish
        self.results = None


def _like(a):
    return jax.ShapeDtypeStruct(a.shape, a.dtype)


def _remote(src, dst, send_sems, recv_sems, k, device):
    return pltpu.make_async_remote_copy(src, dst, send_sems.at[k], recv_sems.at[k], device_id=device, device_id_type=MESH)


def _half(ref, split, which, lead=()):
    if split == "rows":
        hr = ref.shape[-2] // 2
        return ref.at[lead + (pl.ds(which * hr, hr),)]
    hc = ref.shape[-1] // 2
    return ref.at[lead + (slice(None), pl.ds(which * hc, hc))]


def _splits(bufs, splits):
    return list(splits) if splits is not None else ["rows"] * len(bufs)


ALL_PEERS = (0, 1, 2)
NEAR_PEERS = (0, 1)
FAR_PEERS = (2,)


def _step_gather_ici(bufs, splits=None, peers=ALL_PEERS):
    splits = _splits(bufs, splits)

    def copies(outs, send_sems, recv_sems, received):
        x, y, c = _place()
        me = 2 * x + y
        for k, o in enumerate(outs):
            for j, (cx, cy) in enumerate(_other_chips(x, y)):
                if j in peers:
                    part = _half(o, splits[k], c, (2 * cx + cy if received else me,))
                    yield _remote(part, part, send_sems, recv_sems, 3 * k + j, (cx, cy, c))

    def start(ins, outs, send_sems, recv_sems):
        for cp in copies(outs, send_sems, recv_sems, False):
            cp.start()

    def finish(ins, outs, send_sems, recv_sems):
        for cp in copies(outs, send_sems, recv_sems, True):
            cp.wait_recv()
        for cp in copies(outs, send_sems, recv_sems, False):
            cp.wait_send()

    return _Step(bufs, [_like(b) for b in bufs], {k: k for k in range(len(bufs))}, 3 * len(bufs), start, finish)


def _step_gather_d2d(bufs, splits=None):
    splits = _splits(bufs, splits)

    def copies(outs, send_sems, recv_sems, received):
        x, y, c = _place()
        for k, o in enumerate(outs):
            for j, (cx, cy) in enumerate(_other_chips(x, y)):
                part = _half(o, splits[k], 1 - c if received else c, (2 * cx + cy,))
                yield _remote(part, part, send_sems, recv_sems, 3 * k + j, (x, y, 1 - c))

    def start(ins, outs, send_sems, recv_sems):
        for cp in copies(outs, send_sems, recv_sems, False):
            cp.start()

    def finish(ins, outs, send_sems, recv_sems):
        for cp in copies(outs, send_sems, recv_sems, True):
            cp.wait_recv()
        for cp in copies(outs, send_sems, recv_sems, False):
            cp.wait_send()

    return _Step(bufs, [_like(b) for b in bufs], {k: k for k in range(len(bufs))}, 3 * len(bufs), start, finish)


def _step_gather_full(bufs, splits=None):
    n = len(bufs)
    splits = _splits(bufs, splits)

    def ici(outs, send_sems, recv_sems, received):
        x, y, c = _place()
        me = 2 * x + y
        for k, o in enumerate(outs):
            for j, (cx, cy) in enumerate(_other_chips(x, y)):
                part = _half(o, splits[k], c, (2 * cx + cy if received else me,))
                yield _remote(part, part, send_sems, recv_sems, 3 * k + j, (cx, cy, c))

    def d2d(outs, send_sems, recv_sems, received):
        x, y, c = _place()
        for k, o in enumerate(outs):
            for j, (cx, cy) in enumerate(_other_chips(x, y)):
                part = _half(o, splits[k], 1 - c if received else c, (2 * cx + cy,))
                yield _remote(part, part, send_sems, recv_sems, 3 * n + 3 * k + j, (x, y, 1 - c))

    def start(ins, outs, send_sems, recv_sems):
        for cp in ici(outs, send_sems, recv_sems, False):
            cp.start()

    def finish(ins, outs, send_sems, recv_sems):
        for arrived, onward in zip(ici(outs, send_sems, recv_sems, True), d2d(outs, send_sems, recv_sems, False)):
            arrived.wait_recv()
            onward.start()
        for cp in d2d(outs, send_sems, recv_sems, True):
            cp.wait_recv()
        for cp in ici(outs, send_sems, recv_sems, False):
            cp.wait_send()
        for cp in d2d(outs, send_sems, recv_sems, False):
            cp.wait_send()

    return _Step(bufs, [_like(b) for b in bufs], {k: k for k in range(n)}, 6 * n, start, finish)


def _half_shape(shape, split):
    return shape[:-2] + ((shape[-2] // 2, shape[-1]) if split == "rows" else (shape[-2], shape[-1] // 2))


def _step_pair_exchange(grads, splits=None):
    splits = _splits(grads, splits)

    def copies(ins, outs, send_sems, recv_sems):
        x, y, c = _place()
        for k, (g, o) in enumerate(zip(ins, outs)):
            yield _remote(_half(g, splits[k], 1 - c, (slice(None),)), o, send_sems, recv_sems, k, (x, y, 1 - c))

    def start(ins, outs, send_sems, recv_sems):
        for cp in copies(ins, outs, send_sems, recv_sems):
            cp.start()

    def finish(ins, outs, send_sems, recv_sems):
        for cp in copies(ins, outs, send_sems, recv_sems):
            cp.wait()

    outs = [jax.ShapeDtypeStruct(_half_shape(g.shape, s), g.dtype) for g, s in zip(grads, splits)]
    return _Step(grads, outs, {}, len(grads), start, finish)


def _step_chip_exchange(partials, peers=ALL_PEERS, into=None):
    n = len(partials)

    def copies(ins, outs, send_sems, recv_sems):
        x, y, c = _place()
        for k, (q, o) in enumerate(zip(ins[:n], outs)):
            for j, (cx, cy) in enumerate(_other_chips(x, y)):
                if j in peers:
                    yield _remote(q.at[2 * cx + cy], o.at[j], send_sems, recv_sems, 3 * k + j, (cx, cy, c))

    def start(ins, outs, send_sems, recv_sems):
        for cp in copies(ins, outs, send_sems, recv_sems):
            cp.start()

    def finish(ins, outs, send_sems, recv_sems):
        for cp in copies(ins, outs, send_sems, recv_sems):
            cp.wait()

    outs = [jax.ShapeDtypeStruct((3,) + q.shape[1:], q.dtype) for q in partials]
    if into is None:
        return _Step(partials, outs, {}, 3 * n, start, finish)
    return _Step(list(partials) + list(into), outs, {n + k: k for k in range(n)}, 3 * n, start, finish)


def _step_pair_gather(shards, splits=None):
    splits = _splits(shards, splits)

    def copies(outs, send_sems, recv_sems, received):
        x, y, c = _place()
        for k, o in enumerate(outs):
            part = _half(o, splits[k], 1 - c if received else c)
            yield _remote(part, part, send_sems, recv_sems, k, (x, y, 1 - c))

    def start(ins, outs, send_sems, recv_sems):
        for cp in copies(outs, send_sems, recv_sems, False):
            cp.start()

    def finish(ins, outs, send_sems, recv_sems):
        for cp in copies(outs, send_sems, recv_sems, True):
            cp.wait_recv()
        for cp in copies(outs, send_sems, recv_sems, False):
            cp.wait_send()

    return _Step(shards, [_like(s) for s in shards], {k: k for k in range(len(shards))}, len(shards), start, finish)


def _call(body, *, name, out_shape, grid, in_specs, out_specs, operands, scratch_shapes=(), semantics=None, steps=()):
    single = not isinstance(out_shape, (tuple, list))
    out_shapes = [out_shape] if single else list(out_shape)
    out_spec_list = [out_specs] if single else list(out_specs)
    steps = list(steps)
    if not steps:
        res = pl.pallas_call(body, name=name, out_shape=out_shapes, grid=grid, in_specs=list(in_specs),
                             out_specs=out_spec_list, scratch_shapes=list(scratch_shapes),
                             compiler_params=_cparams(semantics))(*operands)
        return res[0] if single else res
    n_in, n_out, n_scr = len(operands), len(out_shapes), len(scratch_shapes)
    x_in = [a for s in steps for a in s.ins]
    x_out = [o for s in steps for o in s.outs]
    aliases, in_off, out_off = {}, 0, 0
    for s in steps:
        for i, o in s.aliases.items():
            aliases[n_in + in_off + i] = n_out + out_off + o
        in_off += len(s.ins)
        out_off += len(s.outs)
    sems = []
    for s in steps:
        sems += [pltpu.SemaphoreType.DMA((s.n_sems,)), pltpu.SemaphoreType.DMA((s.n_sems,))]
    any_spec = pl.BlockSpec(memory_space=pl.ANY)

    def carried(*refs):
        pos = 0
        ins = refs[pos:pos + n_in]; pos += n_in
        xi = refs[pos:pos + len(x_in)]; pos += len(x_in)
        outs = refs[pos:pos + n_out]; pos += n_out
        xo = refs[pos:pos + len(x_out)]; pos += len(x_out)
        scr = refs[pos:pos + n_scr]; pos += n_scr
        sem_refs = refs[pos:]

        def each(action):
            i0 = o0 = 0
            for k, s in enumerate(steps):
                getattr(s, action)(xi[i0:i0 + len(s.ins)], xo[o0:o0 + len(s.outs)], sem_refs[2 * k], sem_refs[2 * k + 1])
                i0 += len(s.ins)
                o0 += len(s.outs)

        if grid:
            first = functools.reduce(jnp.logical_and, [pl.program_id(d) == 0 for d in range(len(grid))])
            last = functools.reduce(jnp.logical_and, [pl.program_id(d) == grid[d] - 1 for d in range(len(grid))])
            pl.when(first)(lambda: each("start"))
            body(*ins, *outs, *scr)
            pl.when(last)(lambda: each("finish"))
        else:
            each("start")
            body(*ins, *outs, *scr)
            each("finish")

    res = pl.pallas_call(
        carried, name=name, out_shape=out_shapes + x_out, grid=grid,
        in_specs=list(in_specs) + [any_spec] * len(x_in), out_specs=out_spec_list + [any_spec] * len(x_out),
        scratch_shapes=list(scratch_shapes) + sems, input_output_aliases=aliases,
        compiler_params=_cparams(None if semantics is None else ("arbitrary",) * len(grid)),
    )(*operands, *x_in)
    o0 = n_out
    for s in steps:
        s.results = list(res[o0:o0 + len(s.outs)])
        o0 += len(s.outs)
    return res[0] if single else tuple(res[:n_out])


def _run_steps(name, steps):
    _call(lambda: None, name=name, out_shape=[], grid=(), in_specs=[], out_specs=[], operands=[], steps=steps)
    return [s.results for s in steps]


def _mm(name, a, b, mode, out_dtype=F32, acc=None, b_colblock=0, k_rows=None, out_rows=None, steps=()):
    resident_bytes = 8 << 20
    if mode == "nn":
        m, k = a.shape
        n = b.shape[1]
        tm = m
        while tm * k * 2 > resident_bytes and tm % 32 == 0:
            tm //= 2
        tn = _tile(n)
        grid = (m // tm, n // tn)
        in_specs = [pl.BlockSpec((tm, k), lambda i, j: (i, 0)), pl.BlockSpec((k, tn), lambda i, j: (0, j))]
        out_shape, out_block = (m, n), (tm, tn)
    elif mode == "nt":
        m, n = a.shape
        k = k_rows or b.shape[0]
        tm = m
        while tm * n * 2 > resident_bytes and tm % 32 == 0:
            tm //= 2
        tk = _tile(k)
        grid = (m // tm, k // tk)
        in_specs = [pl.BlockSpec((tm, n), lambda i, j: (i, 0)), pl.BlockSpec((tk, n), lambda i, j: (j, b_colblock))]
        out_shape, out_block = (m, k), (tm, tk)
    else:
        m, k = a.shape
        n = b.shape[1]
        tk, tn = _tile(k), (n if m * n * 2 <= resident_bytes else _tile(n))
        grid = (k // tk, n // tn)
        in_specs = [pl.BlockSpec((m, tk), lambda i, j: (0, i)), pl.BlockSpec((m, tn), lambda i, j: (0, j))]
        out_shape, out_block = (out_rows or k, n), (tk, tn)
    out_spec = pl.BlockSpec(out_block, lambda i, j: (i, j))
    has_acc = acc is not None

    def body(*refs):
        a_ref, b_ref = refs[0], refs[1]
        o_ref = refs[-1]
        av, bv = a_ref[...], b_ref[...]
        if mode == "nn":
            r = _dot(av, bv)
        elif mode == "nt":
            r = _dot_nt(av, bv)
        else:
            r = _dot_tn(av, bv)
        if has_acc:
            r = r + refs[2][...]
        o_ref[...] = r.astype(o_ref.dtype)

    operands = [a, b]
    if has_acc:
        in_specs = in_specs + [out_spec]
        operands.append(acc)
    return _call(body, name=name, out_shape=jax.ShapeDtypeStruct(out_shape, out_dtype), grid=grid, in_specs=in_specs,
                 out_specs=out_spec, operands=operands, semantics=("parallel", "parallel"), steps=steps)


def _tn_rows_into(name, a, b, into, row0, nrows):
    m, k = a.shape
    n = b.shape[1]

    def body(a_ref, b_ref, into_ref, o_ref):
        o_ref[...] = _dot_tn(a_ref[...], b_ref[...])[0:nrows].astype(o_ref.dtype)

    return pl.pallas_call(
        body, name=name, out_shape=jax.ShapeDtypeStruct(into.shape, into.dtype), grid=(1,),
        in_specs=[pl.BlockSpec((m, k), lambda i: (0, 0)), pl.BlockSpec((m, n), lambda i: (0, 0)),
                  pl.BlockSpec(memory_space=pl.ANY)],
        out_specs=pl.BlockSpec((nrows, n), lambda i: (row0 // nrows, 0)),
        input_output_aliases={2: 0}, compiler_params=_cparams(("arbitrary",)),
    )(a, b, into)


def _rms_fwd(name, h, w):
    rows, width = h.shape
    tm = _row_tile(rows, width)

    def body(h_ref, w_ref, o_ref):
        x = h_ref[...]
        r = lax.rsqrt(jnp.mean(x * x, axis=-1, keepdims=True) + RMS_EPS)
        o_ref[...] = (x * r * w_ref[...]).astype(BF16)

    return pl.pallas_call(
        body, name=name, out_shape=jax.ShapeDtypeStruct((rows, width), BF16), grid=(rows // tm,),
        in_specs=[pl.BlockSpec((tm, width), lambda i: (i, 0)), pl.BlockSpec((1, width), lambda i: (0, 0))],
        out_specs=pl.BlockSpec((tm, width), lambda i: (i, 0)), compiler_params=_cparams(("parallel",)),
    )(h, w)


def _resid_norm_fwd(name, h, pre, w, next_norms=()):
    rows, width = h.shape
    tm = _row_tile(rows, width)
    n_next = len(next_norms)

    def body(*refs):
        h_ref, p_ref, w_ref = refs[:3]
        v_refs = refs[3:3 + n_next]
        o_ref = refs[3 + n_next]
        n_refs = refs[4 + n_next:]
        p = p_ref[...]
        r = lax.rsqrt(jnp.mean(p * p, axis=-1, keepdims=True) + RMS_EPS)
        x = h_ref[...] + jnp.where(_rows_mask(pl.program_id(0), tm), p * r * w_ref[...], 0.0)
        o_ref[...] = x
        if n_next:
            rx = lax.rsqrt(jnp.mean(x * x, axis=-1, keepdims=True) + RMS_EPS)
            for v_ref, n_ref in zip(v_refs, n_refs):
                n_ref[...] = (x * rx * v_ref[...]).astype(BF16)

    row_spec = pl.BlockSpec((tm, width), lambda i: (i, 0))
    vec_spec = pl.BlockSpec((1, width), lambda i: (0, 0))
    outs = pl.pallas_call(
        body, name=name,
        out_shape=[jax.ShapeDtypeStruct((rows, width), F32)] + [jax.ShapeDtypeStruct((rows, width), BF16)] * n_next,
        grid=(rows // tm,), in_specs=[row_spec, row_spec, vec_spec] + [vec_spec] * n_next,
        out_specs=[row_spec] * (1 + n_next), compiler_params=_cparams(("parallel",)),
    )(h, pre, w, *next_norms)
    return outs[0], list(outs[1:])


def _resid_norm_loss(name, h, pre, w, target):
    rows, width = h.shape

    def body(h_ref, p_ref, w_ref, t_ref, dh_ref, loss_ref, dp_ref, dw_ref):
        i = pl.program_id(0)
        p = p_ref[...]
        r = lax.rsqrt(jnp.mean(p * p, axis=-1, keepdims=True) + RMS_EPS)
        x = h_ref[...] + p * r * w_ref[...]
        real = (i + jnp.zeros((CHUNK, 1), jnp.int32)) >= 1
        diff = jnp.where(real, x - t_ref[...], 0.0)
        dh = diff * (1.0 / D_MODEL)
        dh_ref[...] = dh
        dp, dw_rows = _rms_bwd(dh, p, w_ref[...])
        dp_ref[...] = dp.astype(BF16)

        @pl.when(i == 0)
        def _():
            loss_ref[...] = jnp.zeros_like(loss_ref)
            dw_ref[...] = jnp.zeros_like(dw_ref)

        loss_ref[...] += jnp.sum(diff * diff) * (0.5 / D_MODEL)
        dw_ref[...] += jnp.sum(dw_rows, axis=0, keepdims=True)

    blk = pl.BlockSpec((CHUNK, width), lambda i: (i, 0))
    vec_spec = pl.BlockSpec((1, width), lambda i: (0, 0))
    return pl.pallas_call(
        body, name=name,
        out_shape=(jax.ShapeDtypeStruct((rows, width), F32), jax.ShapeDtypeStruct((1, LANES), F32),
                   jax.ShapeDtypeStruct((rows, width), BF16), jax.ShapeDtypeStruct((1, width), F32)),
        grid=(rows // CHUNK,),
        in_specs=[blk, blk, vec_spec, pl.BlockSpec((CHUNK, width), lambda i: (jnp.maximum(i - 1, 0), 0))],
        out_specs=(blk, pl.BlockSpec((1, LANES), lambda i: (0, 0)), blk, vec_spec),
        compiler_params=_cparams(("arbitrary",)),
    )(h, pre, w, target)


def _rms_bwd(dy, x, w):
    r = lax.rsqrt(jnp.mean(x * x, axis=-1, keepdims=True) + RMS_EPS)
    xhat = x * r
    dxhat = dy * w
    return r * (dxhat - xhat * jnp.mean(dxhat * xhat, axis=-1, keepdims=True)), dy * xhat


def _norm_bwd_add(name, dh, dhn, h, w, then=None, split_first_block=False, steps=()):
    rows, width = dh.shape
    tm = CHUNK if split_first_block else _row_tile(rows, width)
    fused = then is not None
    assert not (fused and split_first_block)

    def body(*refs):
        dh_ref, dhn_ref, h_ref, w_ref = refs[:4]
        o_ref, dw_ref = refs[6:8] if fused else refs[-2:]
        i = pl.program_id(0)
        valid = _rows_mask(i, tm)
        dx, dw_rows = _rms_bwd(dhn_ref[...], h_ref[...], w_ref[...])
        dh_new = dh_ref[...] + jnp.where(valid, dx, 0.0)
        if split_first_block:
            first_ref = refs[4]

            @pl.when(i == 0)
            def _():
                first_ref[...] = dh_new

            @pl.when(i > 0)
            def _():
                o_ref[...] = dh_new
        else:
            o_ref[...] = dh_new

        @pl.when(i == 0)
        def _():
            dw_ref[...] = jnp.zeros_like(dw_ref)

        dw_ref[...] += jnp.sum(dw_rows, axis=0, keepdims=True)
        if fused:
            p_ref, wp_ref, dp_ref, dwp_ref = refs[4], refs[5], refs[8], refs[9]
            dp, dwp_rows = _rms_bwd(jnp.where(valid, dh_new, 0.0), p_ref[...], wp_ref[...])
            dp_ref[...] = dp.astype(BF16)

            @pl.when(i == 0)
            def _():
                dwp_ref[...] = jnp.zeros_like(dwp_ref)

            dwp_ref[...] += jnp.sum(dwp_rows, axis=0, keepdims=True)

    row_spec = pl.BlockSpec((tm, width), lambda i: (i, 0))
    vec_spec = pl.BlockSpec((1, width), lambda i: (0, 0))
    row_f32, vec_f32 = jax.ShapeDtypeStruct((rows, width), F32), jax.ShapeDtypeStruct((1, width), F32)
    in_specs, operands = [row_spec, row_spec, row_spec, vec_spec], [dh, dhn, h, w]
    out_shape, out_specs = [row_f32, vec_f32], [row_spec, vec_spec]
    if split_first_block:
        out_shape = [jax.ShapeDtypeStruct((tm, width), F32), jax.ShapeDtypeStruct((rows - tm, width), F32), vec_f32]
        out_specs = [pl.BlockSpec((tm, width), lambda i: (0, 0)),
                     pl.BlockSpec((tm, width), lambda i: (jnp.maximum(i - 1, 0), 0)), vec_spec]
    if fused:
        in_specs += [row_spec, vec_spec]
        operands += list(then)
        out_shape += [jax.ShapeDtypeStruct((rows, width), BF16), vec_f32]
        out_specs += [row_spec, vec_spec]
    return _call(body, name=name, out_shape=out_shape, grid=(rows // tm,), in_specs=in_specs, out_specs=out_specs,
                 operands=operands, semantics=("arbitrary",), steps=steps)


def _shift_down(x, s, rows):
    return pltpu.roll(x, s, 0) if s else x


def _shift_up(x, s, rows):
    return pltpu.roll(x, rows - s, 0) if s else x


def _conv4_fwd(name, zx, cw, cb, steps=()):
    rows = zx.shape[0]
    off = D_INNER // LANES

    def body(x_ref, w_ref, b_ref, o_ref):
        x = x_ref[...]
        acc = b_ref[...] + w_ref[pl.ds(SSM_CONV - 1, 1), :] * x
        for s in range(1, SSM_CONV):
            acc = acc + w_ref[pl.ds(SSM_CONV - 1 - s, 1), :] * _shift_down(x, s, rows)
        valid = lax.broadcasted_iota(jnp.int32, (rows, 1), 0) >= PAD_ROWS
        o_ref[...] = jnp.where(valid, acc * _sigmoid(acc), 0.0)

    return _call(
        body, name=name, out_shape=jax.ShapeDtypeStruct((rows, D_XBC), F32), grid=(D_XBC // LANES,),
        in_specs=[pl.BlockSpec((rows, LANES), lambda j: (0, j + off)),
                  pl.BlockSpec((SSM_CONV, LANES), lambda j: (0, j)),
                  pl.BlockSpec((1, LANES), lambda j: (0, j))],
        out_specs=pl.BlockSpec((rows, LANES), lambda j: (0, j)), operands=[zx, cw, cb],
        semantics=("parallel",), steps=steps)


def _conv4_bwd(name, zx, dout, cw, cb, into):
    rows, width = dout.shape
    zoff = D_INNER // LANES

    def body(x_ref, d_ref, w_ref, b_ref, into_ref, dx_ref, dw_ref, db_ref):
        x = x_ref[...]
        shifted = [_shift_down(x, s, rows) for s in range(SSM_CONV)]
        acc = b_ref[...]
        for s in range(SSM_CONV):
            acc = acc + w_ref[pl.ds(SSM_CONV - 1 - s, 1), :] * shifted[s]
        sig = _sigmoid(acc)
        valid = lax.broadcasted_iota(jnp.int32, (rows, 1), 0) >= PAD_ROWS
        dpre = jnp.where(valid, d_ref[...] * sig * (1.0 + acc * (1.0 - sig)), 0.0)
        dx = w_ref[pl.ds(SSM_CONV - 1, 1), :] * dpre
        for s in range(1, SSM_CONV):
            dx = dx + w_ref[pl.ds(SSM_CONV - 1 - s, 1), :] * _shift_up(dpre, s, rows)
        dx_ref[...] = dx.astype(BF16)
        for s in range(SSM_CONV):
            dw_ref[pl.ds(SSM_CONV - 1 - s, 1), :] = jnp.sum(dpre * shifted[s], axis=0, keepdims=True)
        db_ref[...] = jnp.sum(dpre, axis=0, keepdims=True)

    return pl.pallas_call(
        body, name=name,
        out_shape=(jax.ShapeDtypeStruct(into.shape, BF16), jax.ShapeDtypeStruct((SSM_CONV, width), F32),
                   jax.ShapeDtypeStruct((1, width), F32)),
        grid=(width // LANES,),
        in_specs=[pl.BlockSpec((rows, LANES), lambda j: (0, j + zoff)),
                  pl.BlockSpec((rows, LANES), lambda j: (0, j)),
                  pl.BlockSpec((SSM_CONV, LANES), lambda j: (0, j)),
                  pl.BlockSpec((1, LANES), lambda j: (0, j)),
                  pl.BlockSpec(memory_space=pl.ANY)],
        out_specs=(pl.BlockSpec((rows, LANES), lambda j: (0, j + zoff)),
                   pl.BlockSpec((SSM_CONV, LANES), lambda j: (0, j)),
                   pl.BlockSpec((1, LANES), lambda j: (0, j))),
        input_output_aliases={4: 0}, compiler_params=_cparams(("parallel",)),
    )(zx, dout, cw, cb, into)


FFN_TILE = 2 * LANES


def _ffn_up_conv(name, hn, w_up, cw, cb, steps=()):
    rows, k = hn.shape
    chip_blocks = w_up.shape[2] // LANES
    half_blocks = D_FF // LANES
    nt = D_FF // FFN_TILE

    def weight_block(offset):
        return pl.BlockSpec((None, k, LANES), lambda j: ((2 * j + offset) // chip_blocks, 0, (2 * j + offset) % chip_blocks))

    def body(a_ref, g0, g1, v0, v1, wg_ref, wv_ref, bg_ref, bv_ref, upg_ref, upv_ref, act_ref):
        a = a_ref[...]
        g = _dot(a, jnp.concatenate([g0[...], g1[...]], axis=1))
        v = _dot(a, jnp.concatenate([v0[...], v1[...]], axis=1))
        upg_ref[...] = g
        upv_ref[...] = v
        ug, uv = bg_ref[...], bv_ref[...]
        for s in range(FFN_CONV):
            ug = ug + wg_ref[pl.ds(FFN_CONV - 1 - s, 1), :] * _shift_down(g, s, rows)
            uv = uv + wv_ref[pl.ds(FFN_CONV - 1 - s, 1), :] * _shift_down(v, s, rows)
        act_ref[...] = (ug * _sigmoid(ug) * uv).astype(BF16)

    col = pl.BlockSpec((rows, FFN_TILE), lambda j: (0, j))
    wsp = lambda shift: pl.BlockSpec((FFN_CONV, FFN_TILE), lambda j: (0, j + shift))
    bsp = lambda shift: pl.BlockSpec((1, FFN_TILE), lambda j: (0, j + shift))
    half = jax.ShapeDtypeStruct((rows, D_FF), F32)
    return _call(
        body, name=name, out_shape=(half, half, jax.ShapeDtypeStruct((rows, D_FF), BF16)), grid=(nt,),
        in_specs=[pl.BlockSpec((rows, k), lambda j: (0, 0)), weight_block(0), weight_block(1),
                  weight_block(half_blocks), weight_block(half_blocks + 1), wsp(0), wsp(nt), bsp(0), bsp(nt)],
        out_specs=(col, col, col), operands=[hn, w_up, w_up, w_up, w_up, cw, cw, cb, cb],
        semantics=("parallel",), steps=steps)


def _ffn_conv_bwd(name, up_g, up_v, dact, cw, cb, hn, w_up, steps=()):
    rows, k = hn.shape
    chip_blocks = w_up.shape[2] // LANES
    nt = D_FF // LANES

    def weight_block(shift):
        return pl.BlockSpec((None, k, LANES), lambda j: ((j + shift) // chip_blocks, 0, (j + shift) % chip_blocks))

    def body(g_ref, v_ref, d_ref, wg_ref, wv_ref, bg_ref, bv_ref, upg_ref, upv_ref, hn_ref,
             dwg_ref, dwv_ref, dbg_ref, dbv_ref, dhn_ref, dup_ref, acc, hn_scr, hnt_scr, dup_scr, sems):
        j = pl.program_id(0)
        hn_copy = pltpu.make_async_copy(hn_ref, hn_scr, sems.at[0])
        dhn_copy = pltpu.make_async_copy(acc, dhn_ref, sems.at[0])

        def dup_copy(step, half):
            block, slot = step + half * nt, 2 * (step % 2) + half
            cols = pl.ds(pl.multiple_of((block % chip_blocks) * LANES, LANES), LANES)
            return pltpu.make_async_copy(dup_scr.at[slot], dup_ref.at[block // chip_blocks, :, cols], sems.at[1 + slot])

        @pl.when(j == 0)
        def _():
            hn_copy.start()
            acc[...] = jnp.zeros_like(acc)
            hn_copy.wait()
            for r in range(0, rows, LANES):
                hnt_scr[:, r:r + LANES] = hn_scr[r:r + LANES, :].T

        @pl.when(j >= 2)
        def _():
            dup_copy(j - 2, 0).wait()
            dup_copy(j - 2, 1).wait()

        g, v = g_ref[...], v_ref[...]
        gs = [_shift_down(g, s, rows) for s in range(FFN_CONV)]
        vs = [_shift_down(v, s, rows) for s in range(FFN_CONV)]
        ug, uv = bg_ref[...], bv_ref[...]
        for s in range(FFN_CONV):
            ug = ug + wg_ref[pl.ds(FFN_CONV - 1 - s, 1), :] * gs[s]
            uv = uv + wv_ref[pl.ds(FFN_CONV - 1 - s, 1), :] * vs[s]
        sig = _sigmoid(ug)
        dsig = d_ref[...] * sig
        dup = []
        for dpre, src, w_ref, dw_ref, db_ref in (
                (dsig * uv * (1.0 + ug * (1.0 - sig)), gs, wg_ref, dwg_ref, dbg_ref),
                (dsig * ug, vs, wv_ref, dwv_ref, dbv_ref)):
            dx = w_ref[pl.ds(FFN_CONV - 1, 1), :] * dpre
            for s in range(1, FFN_CONV):
                dx = dx + w_ref[pl.ds(FFN_CONV - 1 - s, 1), :] * _shift_up(dpre, s, rows)
            dup.append(dx.astype(BF16))
            for s in range(FFN_CONV):
                dw_ref[pl.ds(FFN_CONV - 1 - s, 1), :] = jnp.sum(dpre * src[s], axis=0, keepdims=True)
            db_ref[...] = jnp.sum(dpre, axis=0, keepdims=True)
        dup = jnp.concatenate(dup, axis=1)
        acc[...] += _dot_nt(dup, jnp.concatenate([upg_ref[...], upv_ref[...]], axis=1))
        dw = _dot(hnt_scr[...], dup)
        slot = 2 * (j % 2)
        dup_scr[slot] = dw[:, :LANES].astype(BF16)
        dup_scr[slot + 1] = dw[:, LANES:].astype(BF16)
        dup_copy(j, 0).start()
        dup_copy(j, 1).start()

        @pl.when(j == nt - 1)
        def _():
            dhn_copy.start()
            for step in (j - 1, j):
                dup_copy(step, 0).wait()
                dup_copy(step, 1).wait()
            dhn_copy.wait()

    col = pl.BlockSpec((rows, LANES), lambda j: (0, j))
    wsp = lambda shift: pl.BlockSpec((FFN_CONV, LANES), lambda j: (0, j + shift))
    bsp = lambda shift: pl.BlockSpec((1, LANES), lambda j: (0, j + shift))
    any_spec = pl.BlockSpec(memory_space=pl.ANY)
    dw_shape = jax.ShapeDtypeStruct((FFN_CONV, D_FF), F32)
    db_shape = jax.ShapeDtypeStruct((1, D_FF), F32)
    return _call(
        body, name=name, grid=(nt,),
        out_shape=(dw_shape, dw_shape, db_shape, db_shape, jax.ShapeDtypeStruct((rows, k), F32),
                   jax.ShapeDtypeStruct(w_up.shape, BF16)),
        in_specs=[col, col, col, wsp(0), wsp(nt), bsp(0), bsp(nt), weight_block(0), weight_block(nt), any_spec],
        out_specs=(wsp(0), wsp(0), bsp(0), bsp(0), any_spec, any_spec),
        operands=[up_g, up_v, dact, cw, cw, cb, cb, w_up, w_up, hn],
        scratch_shapes=[pltpu.VMEM((rows, k), F32), pltpu.VMEM((rows, k), BF16), pltpu.VMEM((k, rows), BF16),
                        pltpu.VMEM((4, k, LANES), BF16), pltpu.SemaphoreType.DMA((5,))],
        semantics=("arbitrary",), steps=steps)


def _dt_fwd(name, dtr, bias):
    rows = dtr.shape[0]
    tm = _row_tile(rows, LANES)

    def body(d_ref, b_ref, o_ref):
        v = d_ref[...] + b_ref[...]
        sp = jnp.maximum(v, 0.0) + jnp.log1p(jnp.exp(-jnp.abs(v)))
        lane = lax.broadcasted_iota(jnp.int32, (tm, LANES), 1)
        ok = _rows_mask(pl.program_id(0), tm) & (lane < SSM_HEADS)
        o_ref[...] = jnp.where(ok, sp, 0.0)

    return pl.pallas_call(
        body, name=name, out_shape=jax.ShapeDtypeStruct((rows, LANES), F32), grid=(rows // tm,),
        in_specs=[pl.BlockSpec((tm, LANES), lambda i: (i, 0)), pl.BlockSpec((1, LANES), lambda i: (0, 0))],
        out_specs=pl.BlockSpec((tm, LANES), lambda i: (i, 0)), compiler_params=_cparams(("parallel",)),
    )(dtr, bias)


def _dt_bwd(name, ddt, dtr, bias):
    rows = dtr.shape[0]
    tm = _row_tile(rows, LANES)

    def body(g_ref, d_ref, b_ref, o_ref, db_ref):
        i = pl.program_id(0)
        lane = lax.broadcasted_iota(jnp.int32, (tm, LANES), 1)
        ok = _rows_mask(i, tm) & (lane < SSM_HEADS)
        dv = jnp.where(ok, g_ref[...] * _sigmoid(d_ref[...] + b_ref[...]), 0.0)
        o_ref[...] = dv.astype(BF16)

        @pl.when(i == 0)
        def _():
            db_ref[...] = jnp.zeros_like(db_ref)

        db_ref[...] += jnp.sum(dv, axis=0, keepdims=True)

    row_spec = pl.BlockSpec((tm, LANES), lambda i: (i, 0))
    vec_spec = pl.BlockSpec((1, LANES), lambda i: (0, 0))
    return pl.pallas_call(
        body, name=name,
        out_shape=(jax.ShapeDtypeStruct((rows, LANES), BF16), jax.ShapeDtypeStruct((1, LANES), F32)),
        grid=(rows // tm,), in_specs=[row_spec, row_spec, vec_spec], out_specs=(row_spec, vec_spec),
        compiler_params=_cparams(("arbitrary",)),
    )(ddt, dtr, bias)


def _gate_fwd(name, y, zx, w, steps=()):
    rows = y.shape[0]
    tm = _row_tile(rows, D_INNER)

    def body(y_ref, z_ref, w_ref, o_ref):
        z = z_ref[...]
        g = y_ref[...] * (z * _sigmoid(z))
        r = lax.rsqrt(jnp.mean(g * g, axis=-1, keepdims=True) + RMS_EPS)
        o_ref[...] = (g * r * w_ref[...]).astype(BF16)

    row_spec = pl.BlockSpec((tm, D_INNER), lambda i: (i, 0))
    return _call(
        body, name=name, out_shape=jax.ShapeDtypeStruct((rows, D_INNER), BF16), grid=(rows // tm,),
        in_specs=[row_spec, row_spec, pl.BlockSpec((1, D_INNER), lambda i: (0, 0))],
        out_specs=row_spec, operands=[y, zx, w], semantics=("parallel",), steps=steps)


def _gate_bwd(name, dyn, y, zx, w):
    rows = y.shape[0]
    tm = _row_tile(rows, D_INNER)

    def body(d_ref, y_ref, z_ref, w_ref, dy_ref, dz_ref, dw_ref):
        i = pl.program_id(0)
        z, yv = z_ref[...], y_ref[...]
        sig = _sigmoid(z)
        sz = z * sig
        g = yv * sz
        r = lax.rsqrt(jnp.mean(g * g, axis=-1, keepdims=True) + RMS_EPS)
        ghat = g * r
        dn = d_ref[...]
        dghat = dn * w_ref[...]
        dg = r * (dghat - ghat * jnp.mean(dghat * ghat, axis=-1, keepdims=True))
        dy_ref[...] = dg * sz
        dz_ref[...] = (dg * yv * sig * (1.0 + z * (1.0 - sig))).astype(BF16)

        @pl.when(i == 0)
        def _():
            dw_ref[...] = jnp.zeros_like(dw_ref)

        dw_ref[...] += jnp.sum(dn * ghat, axis=0, keepdims=True)

    row_spec = pl.BlockSpec((tm, D_INNER), lambda i: (i, 0))
    vec_spec = pl.BlockSpec((1, D_INNER), lambda i: (0, 0))
    return pl.pallas_call(
        body, name=name,
        out_shape=(jax.ShapeDtypeStruct((rows, D_INNER), F32), jax.ShapeDtypeStruct((rows, D_MAIN), BF16),
                   jax.ShapeDtypeStruct((1, D_INNER), F32)),
        grid=(rows // tm,), in_specs=[row_spec, row_spec, row_spec, vec_spec],
        out_specs=(row_spec, row_spec, vec_spec), compiler_params=_cparams(("arbitrary",)),
    )(dyn, y, zx, w)


def _split3(x):
    hi = x.astype(BF16)
    r1 = x - hi.astype(F32)
    mid = r1.astype(BF16)
    lo = (r1 - mid.astype(F32)).astype(BF16)
    return hi, mid, lo


def _dot3_data_lhs(x, sel):
    sel16 = sel.astype(F32).astype(BF16)
    hi, mid, lo = _split3(x)
    return _dot(hi, sel16) + _dot(mid, sel16) + _dot(lo, sel16)


def _dot2_data_lhs(x, sel):
    sel16 = sel.astype(F32).astype(BF16)
    hi = x.astype(BF16)
    mid = (x - hi.astype(F32)).astype(BF16)
    return _dot(hi, sel16) + _dot(mid, sel16)


def _dot3_data_rhs(sel, x):
    sel16 = sel.astype(F32).astype(BF16)
    hi, mid, lo = _split3(x)
    return _dot(sel16, hi) + _dot(sel16, mid) + _dot(sel16, lo)


def _causal_masks():
    r = lax.broadcasted_iota(jnp.int32, (CHUNK, CHUNK), 0)
    c = lax.broadcasted_iota(jnp.int32, (CHUNK, CHUNK), 1)
    return r >= c, r <= c


def _expand_heads_matrix(g):
    k = lax.broadcasted_iota(jnp.int32, (LANES, GROUP_W), 0)
    j = lax.broadcasted_iota(jnp.int32, (LANES, GROUP_W), 1)
    return HEADS_PER_GROUP * g + jnp.right_shift(j, 6) == k


def _reduce_heads_matrix(g):
    j = lax.broadcasted_iota(jnp.int32, (GROUP_W, LANES), 0)
    k = lax.broadcasted_iota(jnp.int32, (GROUP_W, LANES), 1)
    return HEADS_PER_GROUP * g + jnp.right_shift(j, 6) == k


def _reduce_pair_matrix(g, p):
    j = lax.broadcasted_iota(jnp.int32, (LANES, LANES), 0)
    k = lax.broadcasted_iota(jnp.int32, (LANES, LANES), 1)
    return HEADS_PER_GROUP * g + 2 * p + jnp.right_shift(j, 6) == k


def _group_cols(ref, g, width):
    return ref.at[:, pl.ds(g * width, width)]


def _ssd_prep(name, dt, a128, steps=()):
    rows = dt.shape[0]
    nc = rows // CHUNK

    def body(dt_ref, a_ref, dte_ref, acs_ref, acst_ref):
        causal, _ = _causal_masks()
        dtv = dt_ref[...]
        acs = _dot3_data_rhs(causal, dtv) * a_ref[...]
        acst_ref[...] = acs.T[0:SSM_HEADS]
        for g in range(N_GROUPS):
            expand = _expand_heads_matrix(g)
            _group_cols(dte_ref, g, GROUP_W)[...] = _dot3_data_lhs(dtv, expand)
            _group_cols(acs_ref, g, GROUP_W)[...] = _dot3_data_lhs(acs, expand)

    blk = pl.BlockSpec((CHUNK, D_INNER), lambda c: (c, 0))
    shp = jax.ShapeDtypeStruct((rows, D_INNER), F32)
    return _call(
        body, name=name, out_shape=(shp, shp, jax.ShapeDtypeStruct((nc, SSM_HEADS, CHUNK), F32)), grid=(nc,),
        in_specs=[pl.BlockSpec((CHUNK, LANES), lambda c: (c, 0)), pl.BlockSpec((1, LANES), lambda c: (0, 0))],
        out_specs=(blk, blk, pl.BlockSpec((None, SSM_HEADS, CHUNK), lambda c: (c, 0, 0))),
        operands=[dt, a128], semantics=("parallel",), steps=steps)


def _ssd_common(x_ref, b_ref, c_ref, dte_ref, acs_ref):
    x = x_ref[...]
    dt_exp = dte_ref[...]
    acs_exp = acs_ref[...]
    tot_exp = acs_ref[pl.ds(CHUNK - 1, 1), :]
    xdt = x * dt_exp
    e_exp = jnp.exp(acs_exp)
    f_exp = jnp.exp(tot_exp - acs_exp)
    return _causal_masks(), x, dt_exp, acs_exp, tot_exp, xdt, e_exp, f_exp, b_ref[...], c_ref[...]


def _pair_decay(acs_pair, acs_row, e, causal):
    lane = lax.broadcasted_iota(jnp.int32, (CHUNK, LANES), 1)
    mine = (lane < HEAD_DIM) if e == 0 else (lane >= HEAD_DIM)
    a_l = jnp.where(mine, acs_pair, pltpu.roll(acs_pair, HEAD_DIM, 1))
    seg = a_l - acs_row
    dm = jnp.where(causal[0], jnp.exp(jnp.minimum(seg, 0.0)), 0.0)
    dmt = jnp.where(causal[1], jnp.exp(jnp.minimum(-seg, 0.0)), 0.0)
    return dm, dmt


def _ssd_specs(index_of_chunk):
    wide = pl.BlockSpec((CHUNK, D_INNER), lambda c: (index_of_chunk(c), 0))
    b_spec = pl.BlockSpec((CHUNK, D_BC), lambda c: (index_of_chunk(c), D_INNER // D_BC))
    c_spec = pl.BlockSpec((CHUNK, D_BC), lambda c: (index_of_chunk(c), D_INNER // D_BC + 1))
    rows_spec = pl.BlockSpec((None, SSM_HEADS, CHUNK), lambda c: (index_of_chunk(c), 0, 0))
    state_spec = pl.BlockSpec((N_GROUPS, None, D_STATE, GROUP_W), lambda c: (0, index_of_chunk(c), 0, 0))
    return wide, b_spec, c_spec, rows_spec, state_spec


def _ssd_fwd(name, xbc, dt_exp, acs_exp, acs_rows, dskexp, steps=()):
    rows = xbc.shape[0]
    nc = rows // CHUNK

    def body(x_ref, b_ref, c_ref, dte_ref, acs_ref, acst_ref, dsk_ref, y_ref, st_ref, s_scr):
        @pl.when(pl.program_id(0) == 0)
        def _():
            s_scr[...] = jnp.zeros_like(s_scr)

        lane = lax.broadcasted_iota(jnp.int32, (CHUNK, LANES), 1)
        for g in range(N_GROUPS):
            y_g = _group_cols(y_ref, g, GROUP_W)
            causal, x, _, acs_exp_v, tot_exp, xdt, e_exp, f_exp, bm, cm = _ssd_common(
                _group_cols(x_ref, g, GROUP_W), _group_cols(b_ref, g, D_STATE), _group_cols(c_ref, g, D_STATE),
                _group_cols(dte_ref, g, GROUP_W), _group_cols(acs_ref, g, GROUP_W))
            state = s_scr[g]
            st_ref[g] = state
            cb16, bb16 = cm.astype(BF16), bm.astype(BF16)
            cb = _dot_nt(cb16, bb16)
            base = e_exp * _dot(cb16, state.astype(BF16)) + _group_cols(dsk_ref, g, GROUP_W)[...] * x
            for p in range(HEADS_PER_GROUP // 2):
                sl = slice(p * LANES, (p + 1) * LANES)
                xp = xdt[:, sl].astype(BF16)
                yd = []
                for e in range(2):
                    acs_row = acst_ref[pl.ds(g * HEADS_PER_GROUP + 2 * p + e, 1), :]
                    dm, _ = _pair_decay(acs_exp_v[:, sl], acs_row, e, causal)
                    yd.append(_dot((cb * dm).astype(BF16), xp))
                y_g[:, sl] = jnp.where(lane < HEAD_DIM, yd[0], yd[1]) + base[:, sl]
            s_scr[g] = jnp.exp(tot_exp) * state + _dot_tn(bb16, (f_exp * xdt).astype(BF16))

    wide, b_spec, c_spec, rows_spec, state_spec = _ssd_specs(lambda c: c)
    return _call(
        body, name=name,
        out_shape=(jax.ShapeDtypeStruct((rows, D_INNER), F32),
                   jax.ShapeDtypeStruct((N_GROUPS, nc, D_STATE, GROUP_W), F32)),
        grid=(nc,),
        in_specs=[wide, b_spec, c_spec, wide, wide, rows_spec, pl.BlockSpec((1, D_INNER), lambda c: (0, 0))],
        out_specs=(wide, state_spec),
        scratch_shapes=[pltpu.VMEM((N_GROUPS, D_STATE, GROUP_W), F32)],
        operands=[xbc, xbc, xbc, dt_exp, acs_exp, acs_rows, dskexp], semantics=("arbitrary",), steps=steps)


def _ssd_bwd(name, xbc, dt_exp, acs_exp, acs_rows, dt, a128, dskexp, dy, states, steps=()):
    rows = xbc.shape[0]
    nc = rows // CHUNK
    last = nc - 1

    def body(x_ref, b_ref, c_ref, dte_ref, acs_ref, acst_ref, dt_ref, a128_ref, dsk_all, dy_all, st_all,
             dxbc_all, ddt_ref, dalog_ref, ddsk_ref, ds_all):
        dx_all, db_all, dc_all = (dxbc_all.at[:, :D_INNER], dxbc_all.at[:, D_INNER:D_INNER + D_BC],
                                  dxbc_all.at[:, D_INNER + D_BC:])

        @pl.when(pl.program_id(0) == 0)
        def _():
            ds_all[...] = jnp.zeros_like(ds_all)
            dalog_ref[...] = jnp.zeros_like(dalog_ref)
            ddsk_ref[...] = jnp.zeros_like(ddsk_ref)

        dacs = jnp.zeros((CHUNK, LANES), F32)
        ddt_x = jnp.zeros((CHUNK, LANES), F32)
        for g in range(N_GROUPS):
            dacs_g, ddt_x_g = group(
                g, _group_cols(x_ref, g, GROUP_W), _group_cols(b_ref, g, D_STATE), _group_cols(c_ref, g, D_STATE),
                _group_cols(dte_ref, g, GROUP_W), _group_cols(acs_ref, g, GROUP_W), acst_ref,
                _group_cols(dsk_all, g, GROUP_W), _group_cols(dy_all, g, GROUP_W), st_all.at[g],
                _group_cols(dx_all, g, GROUP_W), _group_cols(db_all, g, D_STATE), _group_cols(dc_all, g, D_STATE),
                ddsk_ref, ds_all.at[g])
            dacs, ddt_x = dacs + dacs_g, ddt_x + ddt_x_g
        _, causal_t = _causal_masks()
        da = _dot3_data_rhs(causal_t, dacs)
        ddt_ref[...] = da * a128_ref[...] + ddt_x
        dalog_ref[...] += jnp.sum(da * dt_ref[...], axis=0, keepdims=True) * a128_ref[...]

    def group(g, x_ref, b_ref, c_ref, dte_ref, acs_ref, acst_ref, dsk_ref, dy_ref, st_ref,
              dx_ref, db_ref, dc_ref, ddsk_ref, ds_scr):
        causal, x, dt_exp, acs_exp_v, tot_exp, xdt, e_exp, f_exp, bm, cm = _ssd_common(
            x_ref, b_ref, c_ref, dte_ref, acs_ref)
        reduce_heads = _reduce_heads_matrix(g)
        state, dstate = st_ref[...], ds_scr[...]
        dyv = dy_ref[...]
        cb16, bb16 = cm.astype(BF16), bm.astype(BF16)
        s16, ds16 = state.astype(BF16), dstate.astype(BF16)
        cb = _dot_nt(cb16, bb16)
        cbt = _dot_nt(bb16, cb16)
        cs = _dot(cb16, s16)
        bds = _dot(bb16, ds16)
        edy = e_exp * dyv
        fx = f_exp * xdt
        dxdt_base = f_exp * bds
        dc_acc = _dot_nt(edy.astype(BF16), s16)
        db_acc = _dot_nt(fx.astype(BF16), ds16)
        ds_scr[...] = jnp.exp(tot_exp) * dstate + _dot_tn(cb16, edy.astype(BF16))
        q = fx * bds
        dacs = _dot2_data_lhs(edy * cs - q, reduce_heads)
        dtot = jnp.sum(_dot2_data_lhs(q + jnp.exp(tot_exp) * dstate * state, reduce_heads), axis=0, keepdims=True)
        ddsk_ref[...] += jnp.sum(_dot2_data_lhs(dyv * x, reduce_heads), axis=0, keepdims=True)
        lane = lax.broadcasted_iota(jnp.int32, (CHUNK, LANES), 1)
        dcb = jnp.zeros((CHUNK, CHUNK), F32)
        dcbt = jnp.zeros((CHUNK, CHUNK), F32)
        ddt_x = jnp.zeros((CHUNK, LANES), F32)
        for p in range(HEADS_PER_GROUP // 2):
            sl = slice(p * LANES, (p + 1) * LANES)
            xp, dyp = xdt[:, sl], dyv[:, sl]
            xp16, dyp16 = xp.astype(BF16), dyp.astype(BF16)
            dxh = []
            for e in range(2):
                h = 2 * p + e
                mine = (lane < HEAD_DIM) if e == 0 else (lane >= HEAD_DIM)
                acs_row = acst_ref[pl.ds(g * HEADS_PER_GROUP + h, 1), :]
                dm, dmt = _pair_decay(acs_exp_v[:, sl], acs_row, e, causal)
                m, mt = cb * dm, cbt * dmt
                xh16 = jnp.where(mine, xp, 0.0).astype(BF16)
                dyh16 = jnp.where(mine, dyp, 0.0).astype(BF16)
                d_m = _dot_nt(dyh16, xp16)
                d_mt = _dot_nt(xh16, dyp16)
                dacs_h = (jnp.sum(d_m * m, axis=-1, keepdims=True)
                          - jnp.sum(d_mt * mt, axis=-1, keepdims=True))
                dacs = dacs + jnp.where(lane == HEADS_PER_GROUP * g + h, dacs_h, 0.0)
                dcb = dcb + d_m * dm
                dcbt = dcbt + d_mt * dmt
                dxh.append(_dot(mt.astype(BF16), dyp16))
            dxdt = jnp.where(lane < HEAD_DIM, dxh[0], dxh[1]) + dxdt_base[:, sl]
            dx_ref[:, sl] = dxdt * dt_exp[:, sl] + dsk_ref[:, sl] * dyp
            ddt_x = ddt_x + _dot2_data_lhs(dxdt * x[:, sl], _reduce_pair_matrix(g, p))
        dc_ref[...] = dc_acc + _dot(dcb.astype(BF16), bb16)
        db_ref[...] = db_acc + _dot(dcbt.astype(BF16), cb16)
        row = lax.broadcasted_iota(jnp.int32, (CHUNK, LANES), 0)
        return dacs + jnp.where(row == CHUNK - 1, dtot, 0.0), ddt_x

    wide, b_spec, c_spec, rows_spec, state_spec = _ssd_specs(lambda c: last - c)
    heads_spec = pl.BlockSpec((CHUNK, LANES), lambda c: (last - c, 0))
    vec_spec = pl.BlockSpec((1, LANES), lambda c: (0, 0))
    vec_shape = jax.ShapeDtypeStruct((1, LANES), F32)
    return _call(
        body, name=name,
        out_shape=(jax.ShapeDtypeStruct((rows, D_XBC), F32), jax.ShapeDtypeStruct((rows, LANES), F32),
                   vec_shape, vec_shape),
        grid=(nc,),
        in_specs=[wide, b_spec, c_spec, wide, wide, rows_spec, heads_spec, vec_spec,
                  pl.BlockSpec((1, D_INNER), lambda c: (0, 0)), wide, state_spec],
        out_specs=(pl.BlockSpec((CHUNK, D_XBC), lambda c: (last - c, 0)), heads_spec, vec_spec, vec_spec),
        scratch_shapes=[pltpu.VMEM((N_GROUPS, D_STATE, GROUP_W), F32)],
        operands=[xbc, xbc, xbc, dt_exp, acs_exp, acs_rows, dt, a128, dskexp, dy, states],
        semantics=("arbitrary",), steps=steps)


def _attn_visible(b, heads=1):
    row = jnp.bitwise_and(lax.broadcasted_iota(jnp.int32, (heads * CHUNK, 3 * CHUNK), 0), CHUNK - 1)
    col = lax.broadcasted_iota(jnp.int32, (heads * CHUNK, 3 * CHUNK), 1)
    bb = b + jnp.zeros_like(col)
    meta = (col < CHUNK) & (bb >= 1) & (col >= PAD_ROWS)
    prev = (col >= CHUNK) & (col < 2 * CHUNK) & (bb >= 2) & ((col - CHUNK) > row)
    cur = (col >= 2 * CHUNK) & ((col - 2 * CHUNK) <= row) & ((bb >= 1) | ((col - 2 * CHUNK) >= PAD_ROWS))
    return meta | prev | cur


def _attn_visible4(b):
    return _attn_visible(b, 4)


def _stack_heads(q_ref, sink_ref, kvh, scale):
    lane = lax.broadcasted_iota(jnp.int32, (CHUNK, LANES), 1)
    parts, sinks = [], []
    for pp in range(2):
        pair = kvh * 2 + pp
        qp = q_ref[:, _lane_block(pair)] * scale
        for e in range(2):
            mine = (lane < HEAD_DIM) if e == 0 else (lane >= HEAD_DIM)
            parts.append(jnp.where(mine, qp, 0.0).astype(BF16))
            sinks.append(jnp.full((CHUNK, 1), sink_ref[2 * pair + e], F32))
    return jnp.concatenate(parts, axis=0), jnp.concatenate(sinks, axis=0)


def _lane_block(i):
    return pl.ds(i * LANES, LANES) if isinstance(i, int) else pl.ds(pl.multiple_of(i * LANES, LANES), LANES)


def _attn_head_operands(q_ref, k0, kp, kc, v0, vp, vc, sink_ref, kvh):
    ksl = _lane_block(kvh)
    kcat = jnp.concatenate([k0[:, ksl], kp[:, ksl], kc[:, ksl]], axis=0).astype(BF16)
    vcat = jnp.concatenate([v0[:, ksl], vp[:, ksl], vc[:, ksl]], axis=0).astype(BF16)
    stacked, sinks = _stack_heads(q_ref, sink_ref, kvh, ATTN_SCALE)
    return kcat, vcat, stacked, sinks


def _attn_operands(q_ref, k0, kp, kc, v0, vp, vc, sink_ref):
    per_head = [_attn_head_operands(q_ref, k0, kp, kc, v0, vp, vc, sink_ref, kvh) for kvh in range(N_KV_HEADS)]
    return tuple(list(column) for column in zip(*per_head))


def _attn_probs(q4, kcat, visible, sink4):
    heads = range(len(q4))
    s = [jnp.where(visible, _dot_nt(q4[h], kcat[h]), NEG_INF) for h in heads]
    m = [jnp.maximum(jnp.max(s[h], axis=-1, keepdims=True), sink4[h]) for h in heads]
    pe = [jnp.exp(s[h] - m[h]) for h in heads]
    pe_sink = [jnp.exp(sink4[h] - m[h]) for h in heads]
    inv = [1.0 / (jnp.sum(pe[h], axis=-1, keepdims=True) + pe_sink[h]) for h in heads]
    return [pe[h] * inv[h] for h in heads], [pe_sink[h] * inv[h] for h in heads]


def _unstack_pairs(stacked, pp):
    lane = lax.broadcasted_iota(jnp.int32, (CHUNK, LANES), 1)
    return jnp.where(lane < HEAD_DIM, stacked[(2 * pp) * CHUNK:(2 * pp + 1) * CHUNK],
                     stacked[(2 * pp + 1) * CHUNK:(2 * pp + 2) * CHUNK])


def _attn_specs(colblock):
    blk = lambda f: pl.BlockSpec((CHUNK, 2 * D_KV), f)
    return [blk(lambda b: (0, colblock)), blk(lambda b: (jnp.maximum(b - 1, 0), colblock)), blk(lambda b: (b, colblock))]


def _attn_fwd(name, q, kv2, sinks, steps=()):
    rows = q.shape[0]

    def body(q_ref, k0, kp, kc, v0, vp, vc, sink_ref, o_ref):
        visible = _attn_visible4(pl.program_id(0))

        kcat, vcat, q4, sink4 = _attn_operands(q_ref, k0, kp, kc, v0, vp, vc, sink_ref)
        pn, _ = _attn_probs(q4, kcat, visible, sink4)
        o4 = [_dot(pn[h].astype(BF16), vcat[h]) for h in range(N_KV_HEADS)]
        for kvh in range(N_KV_HEADS):
            for pp in range(2):
                o_ref[:, _lane_block(kvh * 2 + pp)] = _unstack_pairs(o4[kvh], pp).astype(BF16)

    return _call(
        body, name=name, out_shape=jax.ShapeDtypeStruct((rows, D_MODEL), BF16), grid=(rows // CHUNK,),
        in_specs=[pl.BlockSpec((CHUNK, D_MODEL), lambda b: (b, 0))] + _attn_specs(0) + _attn_specs(1)
        + [pl.BlockSpec(memory_space=pltpu.SMEM)],
        out_specs=pl.BlockSpec((CHUNK, D_MODEL), lambda b: (b, 0)),
        operands=[q, kv2, kv2, kv2, kv2, kv2, kv2, sinks], semantics=("parallel",), steps=steps)


def _attn_bwd(name, q, kv2, sinks, do, steps=()):
    rows = q.shape[0]

    def body(q_ref, k0, kp, kc, v0, vp, vc, sink_ref, do_ref,
             dq_ref, dkc_ref, dkp_ref, dvc_ref, dvp_ref, dkm_ref, dvm_ref, dsink_ref):
        @pl.when(pl.program_id(0) == 0)
        def _():
            dkm_ref[...] = jnp.zeros_like(dkm_ref)
            dvm_ref[...] = jnp.zeros_like(dvm_ref)
            dsink_ref[...] = jnp.zeros_like(dsink_ref)

        visible = _attn_visible4(pl.program_id(0))
        lane1 = lax.broadcasted_iota(jnp.int32, (1, LANES), 1)

        def two_heads(trip, carry):
            kv_heads = [2 * trip, 2 * trip + 1]
            heads = range(len(kv_heads))
            kcat, vcat, q4, sink4 = zip(*[_attn_head_operands(q_ref, k0, kp, kc, v0, vp, vc, sink_ref, kvh)
                                          for kvh in kv_heads])
            do4 = [_stack_heads(do_ref, sink_ref, kvh, 1.0)[0] for kvh in kv_heads]
            pn, psink = _attn_probs(q4, kcat, visible, sink4)
            dp = [_dot_nt(do4[h], vcat[h]) for h in heads]
            delta = [jnp.sum(pn[h] * dp[h], axis=-1, keepdims=True) for h in heads]
            ds16 = [(pn[h] * (dp[h] - delta[h])).astype(BF16) for h in heads]
            dq4 = [_dot(ds16[h], kcat[h]) for h in heads]
            dk_acc = [_dot_tn(ds16[h], q4[h]) for h in heads]
            dv_acc = [_dot_tn(pn[h].astype(BF16), do4[h]) for h in heads]
            dsink = jnp.zeros((1, LANES), F32)
            for h, kvh in enumerate(kv_heads):
                ksl = _lane_block(kvh)
                sink_terms = psink[h] * delta[h]
                for j in range(4):
                    part = jnp.sum(sink_terms[j * CHUNK:(j + 1) * CHUNK], axis=0, keepdims=True)
                    dsink = dsink - jnp.where(lane1 == kvh * 4 + j, part, 0.0)
                for pp in range(2):
                    dq_ref[:, _lane_block(kvh * 2 + pp)] = (_unstack_pairs(dq4[h], pp) * ATTN_SCALE).astype(BF16)
                dkm_ref[:, ksl] += dk_acc[h][0:CHUNK]
                dvm_ref[:, ksl] += dv_acc[h][0:CHUNK]
                dkp_ref[:, ksl] = dk_acc[h][CHUNK:2 * CHUNK]
                dvp_ref[:, ksl] = dv_acc[h][CHUNK:2 * CHUNK]
                dkc_ref[:, ksl] = dk_acc[h][2 * CHUNK:3 * CHUNK]
                dvc_ref[:, ksl] = dv_acc[h][2 * CHUNK:3 * CHUNK]
            dsink_ref[...] += dsink
            return carry

        lax.fori_loop(0, N_KV_HEADS // 2, two_heads, 0)

    qspec = pl.BlockSpec((CHUNK, D_MODEL), lambda b: (b, 0))
    kvspec = pl.BlockSpec((CHUNK, 2 * D_KV), lambda b: (b, 0))
    fixed = pl.BlockSpec((CHUNK, 2 * D_KV), lambda b: (0, 0))
    kv_shape = jax.ShapeDtypeStruct((rows, 2 * D_KV), F32)
    meta_shape = jax.ShapeDtypeStruct((CHUNK, 2 * D_KV), F32)
    return _call(
        body, name=name,
        out_shape=(jax.ShapeDtypeStruct((rows, D_MODEL), BF16), kv_shape, kv_shape, kv_shape, kv_shape,
                   meta_shape, meta_shape, jax.ShapeDtypeStruct((1, LANES), F32)),
        grid=(rows // CHUNK,),
        in_specs=[qspec] + _attn_specs(0) + _attn_specs(1) + [pl.BlockSpec(memory_space=pltpu.SMEM), qspec],
        out_specs=(qspec, kvspec, kvspec, kvspec, kvspec, fixed, fixed, pl.BlockSpec((1, LANES), lambda b: (0, 0))),
        operands=[q, kv2, kv2, kv2, kv2, kv2, kv2, sinks, do], semantics=("arbitrary",), steps=steps)


def _kv_grad_combine(name, dk_cur, dk_prev, dk_meta, dv_cur, dv_prev, dv_meta):
    rows = dk_cur.shape[0]
    nb = rows // CHUNK
    width = 2 * D_KV

    def body(kc_ref, kp_ref, km_ref, vc_ref, vp_ref, vm_ref, o_ref):
        jj = pl.program_id(0) + jnp.zeros((CHUNK, 1), jnp.int32)
        for half, (c_ref, p_ref, m_ref) in enumerate(((kc_ref, kp_ref, km_ref), (vc_ref, vp_ref, vm_ref))):
            total = c_ref[...] + jnp.where(jj < nb - 1, p_ref[...], 0.0) + jnp.where(jj == 0, m_ref[...], 0.0)
            o_ref[:, half * width:(half + 1) * width] = total.astype(BF16)

    blk = lambda f: pl.BlockSpec((CHUNK, width), f)
    three = lambda: [blk(lambda j: (j, 0)), blk(lambda j: (jnp.minimum(j + 1, nb - 1), 0)), blk(lambda j: (0, 0))]
    return pl.pallas_call(
        body, name=name, out_shape=jax.ShapeDtypeStruct((rows, 2 * width), BF16), grid=(nb,),
        in_specs=three() + three(), out_specs=pl.BlockSpec((CHUNK, 2 * width), lambda j: (j, 0)),
        compiler_params=_cparams(("parallel",)),
    )(dk_cur, dk_prev, dk_meta, dv_cur, dv_prev, dv_meta)


def _adamw(name, w, g, m, v, steps=()):
    rows, width = w.shape
    tr = rows
    for cand in range(8, rows + 1, 8):
        if rows % cand == 0 and cand * width * 4 <= (1 << 20):
            tr = cand

    def body(*refs):
        _adamw_update(*refs)

    blk = pl.BlockSpec((tr, width), lambda i: (i, 0))
    shp = jax.ShapeDtypeStruct((rows, width), F32)
    return _call(body, name=name, out_shape=(shp, shp, shp), grid=(rows // tr,), in_specs=[blk] * 4,
                 out_specs=(blk,) * 3, operands=[w, g, m, v], semantics=("parallel",), steps=steps)


def _adamw_update(w_ref, g_ref, m_ref, v_ref, d_ref, mo_ref, vo_ref):
    gv = g_ref[...]
    mn = ADAM_B1 * m_ref[...] + (1.0 - ADAM_B1) * gv
    vn = ADAM_B2 * v_ref[...] + (1.0 - ADAM_B2) * (gv * gv)
    m_hat = mn / (1.0 - ADAM_B1 ** ADAM_STEP)
    v_hat = vn / (1.0 - ADAM_B2 ** ADAM_STEP)
    d_ref[...] = -ADAM_LR * (m_hat / (jnp.sqrt(v_hat) + ADAM_EPS) + ADAM_WD * w_ref[...])
    mo_ref[...] = mn
    vo_ref[...] = vn


def _adamw_small(name, ws, gs, ms, vs):
    n = len(ws)

    def body(*refs):
        for i in range(n):
            _adamw_update(*refs[i::n])

    shapes = [jax.ShapeDtypeStruct(a.shape, F32) for a in ws]
    outs = pl.pallas_call(body, name=name, out_shape=shapes * 3, in_specs=[VMEM_SPEC] * (4 * n),
                          out_specs=[VMEM_SPEC] * (3 * n), compiler_params=_cparams())(*ws, *gs, *ms, *vs)
    return outs[:n], outs[n:2 * n], outs[2 * n:]


def _ffn_fwd(tag, h, hn, p, i, plan):
    up_g, up_v, act = _ffn_up_conv(f"ffn{tag}_up", hn, plan.weight("f_w_up", i), p["f_conv_w"][i],
                                   p["f_conv_b"][i:i + 1], steps=plan.steps(f"ffn{tag}_up"))
    pre = _mm(f"ffn{tag}_down", act, plan.weight("f_w_down", i), "nn", steps=plan.steps(f"ffn{tag}_down"))
    return pre, (h, hn, up_g, up_v, act, pre)


def _ffn_bwd(tag, dpre, saved, p, i, plan):
    h, hn, up_g, up_v, act, pre = saved
    plan.grad("f_w_down", i, _mm(f"ffn{tag}_down_dw", act, dpre, "tn", out_dtype=BF16))
    dact = _mm(f"ffn{tag}_down_dx", dpre, plan.weight("f_w_down", i), "nt", steps=plan.steps(f"ffn{tag}_down_dx"))
    gwg, gwv, gbg, gbv, dhn, g_up = _ffn_conv_bwd(
        f"ffn{tag}_conv_bwd", up_g, up_v, dact, p["f_conv_w"][i], p["f_conv_b"][i:i + 1], hn,
        plan.weight("f_w_up", i), steps=plan.steps(f"ffn{tag}_conv_bwd"))
    g_cw, g_cb = jnp.concatenate([gwg, gwv], axis=1), jnp.concatenate([gbg, gbv], axis=1)
    plan.grad("f_w_up", i, g_up)
    return dhn, dict(f_conv_w=g_cw, f_conv_b=g_cb)


def _lanes_pad(a, width=LANES):
    return jnp.pad(a, [(0, 0)] * (a.ndim - 1) + [(0, width - a.shape[-1])])


def _dup_heads(w):
    rows = w.shape[0]
    w = w.reshape(rows, 2 * N_KV_HEADS, 1, HEAD_DIM)
    return jnp.broadcast_to(w, (rows, 2 * N_KV_HEADS, 2, HEAD_DIM)).reshape(rows, 4 * D_KV)


def _undup_heads(g):
    rows = g.shape[0]
    return g.reshape(rows, 2 * N_KV_HEADS, 2, HEAD_DIM).sum(axis=2).reshape(rows, 2 * D_KV)


def _local_step(x2, target, p, plan):
    seq = x2.shape[0]
    rows = seq + CHUNK
    g = {}

    h0 = jnp.concatenate([jnp.zeros((PAD_ROWS, D_MODEL), F32), p["meta_tokens"], x2], axis=0)

    w_in = plan.weight("a_w_in")
    w_dt = jnp.pad(w_in[D_MAIN:], ((0, LANES - SSM_HEADS), (0, 0)))
    dt_bias = _lanes_pad(p["a_dt_bias"])
    a128 = _lanes_pad(-jnp.exp(p["a_a_log"]))
    dskexp = jnp.repeat(p["a_d_skip"].reshape(SSM_HEADS), HEAD_DIM).reshape(1, D_INNER)

    hn0 = _rms_fwd("a_norm", h0, p["a_norm_pre"])
    zx = _mm("a_in_main", hn0, w_in, "nt", k_rows=D_MAIN, steps=plan.steps("a_in_main"))
    dtr = _mm("a_in_dt", hn0, w_dt, "nt")
    xbc = _conv4_fwd("a_conv", zx, p["a_conv_w"], p["a_conv_b"], steps=plan.steps("a_conv"))
    dt = _dt_fwd("a_dt", dtr, dt_bias)
    dt_exp, acs_exp, acs_rows = _ssd_prep("a_ssd_prep", dt, a128, steps=plan.steps("a_ssd_prep"))
    y, states = _ssd_fwd("a_ssd", xbc, dt_exp, acs_exp, acs_rows, dskexp, steps=plan.steps("a_ssd"))
    yn = _gate_fwd("a_gate", y, zx, p["a_gate_norm"], steps=plan.steps("a_gate"))
    mix = _mm("a_out", yn, plan.weight("a_w_out"), "nn", steps=plan.steps("a_out"))
    h1, (hn_f0,) = _resid_norm_fwd("a_resid", h0, mix, p["a_norm_post"], [p["f_norm_pre"][0:1]])

    pre_f0, ffn0 = _ffn_fwd("0", h1, hn_f0, p, 0, plan)
    h2, (hkv, hn2) = _resid_norm_fwd("ffn0_resid", h1, pre_f0, p["f_norm_post"][0:1], [p["kv_norm"], p["b_norm_pre"]])

    w_kv2 = _dup_heads(plan.weight("w_kv"))
    kv2 = _mm("kv_proj", hkv, w_kv2, "nn")
    q = _mm("b_q", hn2, plan.weight("b_w_q"), "nn")
    sinks = p["b_sinks"].reshape(N_Q_HEADS)
    o = _attn_fwd("b_attn", q, kv2, sinks, steps=plan.steps("b_attn"))
    attn = _mm("b_o", o, plan.weight("b_w_o"), "nn", steps=plan.steps("b_o"))
    h3, (hn_f1,) = _resid_norm_fwd("b_resid", h2, attn, p["b_norm_post"], [p["f_norm_pre"][1:2]])

    pre_f1, ffn1 = _ffn_fwd("1", h3, hn_f1, p, 1, plan)
    dh, loss_vec, dpre_f1, g_post1 = _resid_norm_loss("ffn1_resid_loss", h3, pre_f1, p["f_norm_post"][1:2], target)
    loss = loss_vec[0, 0]

    dhn_f1, g1 = _ffn_bwd("1", dpre_f1, ffn1, p, 1, plan)
    dh, g_pre1, dpre, g["b_norm_post"] = _norm_bwd_add("ffn1_norm_bwd", dh, dhn_f1, h3, p["f_norm_pre"][1:2],
                                                        then=(attn, p["b_norm_post"]))
    plan.grad("b_w_o", None, _mm("b_o_dw", o, dpre, "tn", out_dtype=BF16))
    do = _mm("b_o_dx", dpre, plan.weight("b_w_o"), "nt", steps=plan.steps("b_o_dx"))
    dq, dkc, dkp, dvc, dvp, dkm, dvm, dsink = _attn_bwd("b_attn_bwd", q, kv2, sinks, do, steps=plan.steps("b_attn_bwd"))
    g["b_sinks"] = dsink[:, :N_Q_HEADS]
    dhn2 = _mm("b_q_dx", dq, plan.weight("b_w_q"), "nt")
    plan.grad("b_w_q", None, _mm("b_q_dw", hn2, dq, "tn", out_dtype=BF16))
    dh, g["b_norm_pre"] = _norm_bwd_add("b_norm_bwd", dh, dhn2, h2, p["b_norm_pre"])
    dkv2 = _kv_grad_combine("kv_grad", dkc, dkp, dkm, dvc, dvp, dvm)
    dhkv = _mm("kv_proj_dx", dkv2, w_kv2, "nt")
    plan.grad("w_kv", None, _undup_heads(_mm("kv_proj_dw", hkv, dkv2, "tn")))
    dh, g["kv_norm"], dpre_f0, g_post0 = _norm_bwd_add("kv_norm_bwd", dh, dhkv, h2, p["kv_norm"],
                                                       then=(pre_f0, p["f_norm_post"][0:1]))

    dhn_f0, g0 = _ffn_bwd("0", dpre_f0, ffn0, p, 0, plan)
    dh, g_pre0, dpre, g["a_norm_post"] = _norm_bwd_add("ffn0_norm_bwd", dh, dhn_f0, h1, p["f_norm_pre"][0:1],
                                                        then=(mix, p["a_norm_post"]))
    g["f_norm_post"] = jnp.concatenate([g_post0, g_post1], axis=0)
    g["f_norm_pre"] = jnp.concatenate([g_pre0, g_pre1], axis=0)
    g["f_conv_w"] = jnp.stack([g0["f_conv_w"], g1["f_conv_w"]])
    g["f_conv_b"] = jnp.concatenate([g0["f_conv_b"], g1["f_conv_b"]], axis=0)
    plan.grad("a_w_out", None, _mm("a_out_dw", yn, dpre, "tn", out_dtype=BF16))
    dyn = _mm("a_out_dx", dpre, plan.weight("a_w_out"), "nt", steps=plan.steps("a_out_dx"))
    dy, dzx, g["a_gate_norm"] = _gate_bwd("a_gate_bwd", dyn, y, zx, p["a_gate_norm"])
    dxbc, ddt, dalog, ddsk = _ssd_bwd("a_ssd_bwd", xbc, dt_exp, acs_exp, acs_rows, dt, a128, dskexp, dy, states,
                                      steps=plan.steps("a_ssd_bwd"))
    g["a_a_log"] = dalog[:, :SSM_HEADS]
    g["a_d_skip"] = ddsk[:, :SSM_HEADS]
    ddtr, dbias = _dt_bwd("a_dt_bwd", ddt, dtr, dt_bias)
    g["a_dt_bias"] = dbias[:, :SSM_HEADS]
    dzx, g["a_conv_w"], g["a_conv_b"] = _conv4_bwd("a_conv_bwd", zx, dxbc, p["a_conv_w"], p["a_conv_b"], dzx)
    g_in = _mm("a_in_main_dw", dzx, hn0, "tn", out_dtype=BF16, out_rows=D_IN_PROJ, steps=plan.steps("a_in_main_dw"))
    plan.grad("a_w_in", None, _tn_rows_into("a_in_dt_dw", ddtr, hn0, g_in, D_MAIN, SSM_HEADS))
    dhn0 = _mm("a_in_dt_dx", ddtr, w_dt, "nn", steps=plan.steps("a_in_dt_dx"))
    dhn0 = _mm("a_in_main_dx", dzx, w_in, "nn", acc=dhn0, steps=plan.steps("a_in_main_dx"))
    dh_first, grad_x, g["a_norm_pre"] = _norm_bwd_add("a_norm_bwd", dh, dhn0, h0, p["a_norm_pre"],
                                                      split_first_block=True, steps=plan.steps("a_norm_bwd"))
    g["meta_tokens"] = dh_first[PAD_ROWS:]
    return loss, grad_x, g


ANY = pl.BlockSpec(memory_space=pl.ANY)
VMEM_SPEC = pl.BlockSpec(memory_space=pltpu.VMEM)


def _step_gather_small(slots):
    def copies(outs, send_sems, recv_sems, received):
        x, y, c = _place()
        me = 2 * x + y
        for j, (cx, cy) in enumerate(_other_chips(x, y)):
            slot = outs[0].at[2 * cx + cy if received else me]
            yield _remote(slot, slot, send_sems, recv_sems, j, (cx, cy, c))

    def start(ins, outs, send_sems, recv_sems):
        for cp in copies(outs, send_sems, recv_sems, False):
            cp.start()

    def finish(ins, outs, send_sems, recv_sems):
        for cp in copies(outs, send_sems, recv_sems, True):
            cp.wait_recv()
        for cp in copies(outs, send_sems, recv_sems, False):
            cp.wait_send()

    return _Step([slots], [_like(slots)], {0: 0}, 3, start, finish)


def _row_block(rows, width, itemsize, align, budget=2 << 20):
    best = rows
    for cand in range(align, rows + 1, align):
        if rows % cand == 0 and cand * width * itemsize <= budget:
            best = cand
    return best


def _cast_into_slot(name, chip, w, layer=None):
    rows, width = w.shape[-2:]
    tr = _row_block(rows, width, 4, 16)
    if layer is None:
        in_spec = pl.BlockSpec((tr, width), lambda i, chip_ref: (i, 0))
    else:
        in_spec = pl.BlockSpec((None, tr, width), lambda i, chip_ref: (layer, i, 0))

    def body(chip_ref, w_ref, o_ref):
        o_ref[...] = w_ref[...].astype(BF16)

    return pl.pallas_call(
        body, name=name, out_shape=jax.ShapeDtypeStruct((N_CHIPS, rows, width), BF16),
        grid_spec=pltpu.PrefetchScalarGridSpec(
            num_scalar_prefetch=1, grid=(rows // tr,), in_specs=[in_spec],
            out_specs=pl.BlockSpec((None, tr, width), lambda i, chip_ref: (chip_ref[0], i, 0))),
        compiler_params=_cparams(("parallel",)),
    )(chip, w)


def _allreduce_small(name, vec, steps=()):
    rows = -(-vec.shape[0] // (2 * SUBLANES)) * (2 * SUBLANES)
    hr = rows // 2
    padded = jnp.pad(vec, ((0, rows - vec.shape[0]), (0, 0)))

    def body(v_ref, o_ref, theirs, pair, by_chip, send_sems, recv_sems):
        x, y, c = _place()
        me = 2 * x + y
        sibling = (x, y, 1 - c)
        mine = pl.ds(pl.multiple_of(c * hr, SUBLANES), hr)
        other = pl.ds(pl.multiple_of((1 - c) * hr, SUBLANES), hr)

        swap = _remote(v_ref, theirs, send_sems, recv_sems, 0, sibling)
        swap.start()
        swap.wait()
        south = (c + jnp.zeros((1, 1), jnp.int32)) == 0
        pair[...] = jnp.where(south, v_ref[...], theirs[...]) + jnp.where(south, theirs[...], v_ref[...])

        by_chip[me] = pair[mine, :]
        sends = [_remote(by_chip.at[me], by_chip.at[me], send_sems, recv_sems, 1 + j, (cx, cy, c))
                 for j, (cx, cy) in enumerate(_other_chips(x, y))]
        for cp in sends:
            cp.start()
        for j, (cx, cy) in enumerate(_other_chips(x, y)):
            _remote(by_chip.at[me], by_chip.at[2 * cx + cy], send_sems, recv_sems, 1 + j, (cx, cy, c)).wait_recv()
        for cp in sends:
            cp.wait_send()
        total = by_chip[0]
        for s in range(1, N_CHIPS):
            total = total + by_chip[s]

        o_ref[mine, :] = total
        back = _remote(o_ref.at[mine], o_ref.at[mine], send_sems, recv_sems, 4, sibling)
        back.start()
        _remote(o_ref.at[other], o_ref.at[other], send_sems, recv_sems, 4, sibling).wait_recv()
        back.wait_send()

    out = _call(
        body, name=name, out_shape=jax.ShapeDtypeStruct((rows, LANES), F32), grid=(),
        in_specs=[VMEM_SPEC], out_specs=VMEM_SPEC, operands=[padded],
        scratch_shapes=[pltpu.VMEM((rows, LANES), F32), pltpu.VMEM((rows, LANES), F32),
                        pltpu.VMEM((N_CHIPS, hr, LANES), F32), pltpu.SemaphoreType.DMA((5,)),
                        pltpu.SemaphoreType.DMA((5,))],
        steps=steps)
    return out[:vec.shape[0]]


def _rs_pair_add(name, place, grads, partner, split="rows"):
    _, half_rows, width = partner.shape
    tr = _row_block(half_rows, width, 2, 16)
    nb = half_rows // tr
    if split == "rows":
        mine = pl.BlockSpec((None, tr, width), lambda s, i, pr: (s, pr[1] * nb + i, 0))
    else:
        mine = pl.BlockSpec((None, tr, width), lambda s, i, pr: (s, i, pr[1]))

    def body(place_ref, g_ref, p_ref, o_ref):
        o_ref[...] = (g_ref[...].astype(F32) + p_ref[...].astype(F32)).astype(BF16)

    return pl.pallas_call(
        body, name=name, out_shape=jax.ShapeDtypeStruct(partner.shape, BF16),
        grid_spec=pltpu.PrefetchScalarGridSpec(
            num_scalar_prefetch=1, grid=(N_CHIPS, nb),
            in_specs=[mine, pl.BlockSpec((None, tr, width), lambda s, i, pr: (s, i, 0))],
            out_specs=pl.BlockSpec((None, tr, width), lambda s, i, pr: (s, i, 0))),
        compiler_params=_cparams(("parallel", "parallel")),
    )(place, grads, partner)


def _rs_chip_add(name, place, mine, others, split="rows"):
    _, half_rows, width = mine.shape
    tr = _row_block(half_rows, width, 4, 16, budget=1 << 20)
    nb = half_rows // tr
    if split == "rows":
        out_shape, out_spec = (2 * half_rows, width), pl.BlockSpec((tr, width), lambda i, pr: (pr[1] * nb + i, 0))
    else:
        out_shape, out_spec = (half_rows, 2 * width), pl.BlockSpec((tr, width), lambda i, pr: (i, pr[1]))

    def body(place_ref, q_ref, r_ref, o_ref):
        acc = q_ref[...].astype(F32)
        for j in range(3):
            acc = acc + r_ref[j].astype(F32)
        o_ref[...] = acc

    return pl.pallas_call(
        body, name=name, out_shape=jax.ShapeDtypeStruct(out_shape, F32),
        grid_spec=pltpu.PrefetchScalarGridSpec(
            num_scalar_prefetch=1, grid=(nb,),
            in_specs=[pl.BlockSpec((None, tr, width), lambda i, pr: (pr[0], i, 0)),
                      pl.BlockSpec((3, tr, width), lambda i, pr: (0, i, 0))],
            out_specs=out_spec),
        compiler_params=_cparams(("parallel",)),
    )(place, mine, others)


WEIGHTS = ["meta_tokens", "a_norm_pre", "a_w_in", "a_conv_w", "a_conv_b", "a_dt_bias", "a_a_log", "a_d_skip",
           "a_gate_norm", "a_w_out", "a_norm_post", "kv_norm", "w_kv", "b_norm_pre", "b_w_q", "b_sinks", "b_w_o",
           "b_norm_post", "f_norm_pre", "f_w_up", "f_conv_w", "f_conv_b", "f_w_down", "f_norm_post"]
FULL_SHAPE = {
    "meta_tokens": (16, 1024), "a_norm_pre": (1, 1024), "a_w_in": (1, 1024, 5152), "a_conv_w": (1, 4, 3072),
    "a_conv_b": (1, 3072), "a_dt_bias": (1, 32), "a_a_log": (1, 32), "a_d_skip": (1, 32), "a_gate_norm": (1, 2048),
    "a_w_out": (1, 2048, 1024), "a_norm_post": (1, 1024), "kv_norm": (1024,), "w_kv": (1024, 512),
    "b_norm_pre": (1, 1024), "b_w_q": (1, 1024, 1024), "b_sinks": (1, 16), "b_w_o": (1, 1024, 1024),
    "b_norm_post": (1, 1024), "f_norm_pre": (2, 1024), "f_w_up": (2, 1024, 5632), "f_conv_w": (2, 3, 5632),
    "f_conv_b": (2, 5632), "f_w_down": (2, 2816, 1024), "f_norm_post": (2, 1024),
}
SHARD_AXIS = {
    "meta_tokens": 1, "a_norm_pre": 1, "a_w_in": 2, "a_conv_w": 2, "a_conv_b": 1, "a_dt_bias": None, "a_a_log": None,
    "a_d_skip": None, "a_gate_norm": 1, "a_w_out": 1, "a_norm_post": 1, "kv_norm": None, "w_kv": 0, "b_norm_pre": None,
    "b_w_q": 1, "b_sinks": None, "b_w_o": 1, "b_norm_post": None, "f_norm_pre": None, "f_w_up": 2, "f_conv_w": 2,
    "f_conv_b": None, "f_w_down": 1, "f_norm_post": None,
}
BIG = ["a_w_in", "a_w_out", "w_kv", "b_w_q", "b_w_o", "f_w_up", "f_w_down"]
SMALL = [n for n in WEIGHTS if n not in BIG]
SMALL_SHARDED = [n for n in SMALL if SHARD_AXIS[n] is not None]


def _shard_shape(name):
    shape = list(FULL_SHAPE[name])
    if SHARD_AXIS[name] is not None:
        shape[SHARD_AXIS[name]] //= N_CHIPS
    return tuple(shape)


def _numel(shape):
    return int(math.prod(shape))


SUBLANES = 8


def _packed_rows(shape):
    rows = -(-_numel(shape) // LANES)
    return -(-rows // SUBLANES) * SUBLANES


def _pack(arrays):
    parts = []
    for a in arrays:
        size, rows = _numel(a.shape), _packed_rows(a.shape)
        if size % LANES == 0:
            part = jnp.pad(a.reshape(size // LANES, LANES), ((0, rows - size // LANES), (0, 0)))
        else:
            part = jnp.pad(a.reshape(-1), (0, rows * LANES - size)).reshape(rows, LANES)
        parts.append(part)
    return jnp.concatenate(parts, axis=0)


def _unpack(packed, names, shape_of):
    out, off = {}, 0
    lead = packed.shape[:-2]
    for n in names:
        shape = tuple(shape_of(n))
        size, rows = _numel(shape), _packed_rows(shape)
        part = packed[..., off:off + rows, :]
        if size % LANES == 0:
            out[n] = part[..., :size // LANES, :].reshape(lead + shape)
        else:
            out[n] = part.reshape(lead + (rows * LANES,))[..., :size].reshape(lead + shape)
        off += rows
    return out


def _split_chips(name, full):
    ax = SHARD_AXIS[name]
    shape = full.shape
    cut = shape[:ax] + (N_CHIPS, shape[ax] // N_CHIPS) + shape[ax + 1:]
    return jnp.moveaxis(full.reshape(cut), ax, 0)


def _join_chips(name, stacked):
    ax = SHARD_AXIS[name]
    moved = jnp.moveaxis(stacked, 0, ax)
    shape = moved.shape
    return moved.reshape(shape[:ax] + (shape[ax] * shape[ax + 1],) + shape[ax + 2:])


def _as2d(a):
    return a.reshape(-1, a.shape[-1])


BUFFERS = [("a_w_in", "a_w_in", None), ("a_w_out", "a_w_out", None), ("w_kv", "w_kv", None),
           ("b_w_q", "b_w_q", None), ("b_w_o", "b_w_o", None), ("f_w_up0", "f_w_up", 0), ("f_w_up1", "f_w_up", 1),
           ("f_w_down0", "f_w_down", 0), ("f_w_down1", "f_w_down", 1)]


TRANSPOSED = ("a_w_in",)
SPLIT = {"a_w_in": "cols"}


def _local_shard(arrays, weight, layer):
    if weight in TRANSPOSED:
        return arrays[weight][0].T
    return _as2d(arrays[weight]) if layer is None else arrays[weight]


def _weight_from_gathered(weight, buf):
    if weight == "f_w_up":
        return buf
    return buf.reshape(N_CHIPS * buf.shape[1], buf.shape[2])


def _gathered_from_grad(weight, g):
    if weight == "f_w_up":
        return g
    return g.reshape(N_CHIPS, g.shape[0] // N_CHIPS, g.shape[1]).astype(BF16)


GATHER_SCHEDULE = {
    "a_in_main": [("ici", ["a_w_out"])],
    "a_conv": [("d2d", ["a_w_out"]), ("ici", ["f_w_down0"])],
    "a_ssd_prep": [("d2d", ["f_w_down0"]), ("ici_near", ["f_w_up0"])],
    "a_ssd": [("ici_far", ["f_w_up0"])],
    "a_gate": [("d2d", ["f_w_up0"]), ("ici", ["w_kv", "b_w_q", "b_w_o"])],
    "ffn0_up": [("d2d", ["w_kv", "b_w_q", "b_w_o"]), ("ici", ["f_w_down1"])],
    "ffn0_down": [("d2d", ["f_w_down1"])],
    "b_attn": [("ici", ["f_w_up1"])],
    "b_o": [("d2d", ["f_w_up1"])],
}
REDUCE_SCHEDULE = {
    "b_attn_bwd": [("all", ["f_w_down1", "f_w_up1", "b_w_o"])],
    "ffn0_conv_bwd": [("all", ["b_w_q", "w_kv", "f_w_down0"])],
    "a_ssd_bwd": [("all", ["f_w_up0", "a_w_out"])],
    "a_in_main_dx": [("near", ["a_w_in"])],
    "a_norm_bwd": [("far", ["a_w_in"])],
}
ICI_PEERS = {"ici": ALL_PEERS, "ici_near": NEAR_PEERS, "ici_far": FAR_PEERS,
             "all": ALL_PEERS, "near": NEAR_PEERS, "far": FAR_PEERS}
PAIR_SCHEDULE = {
    "b_o_dx": ["f_w_down1", "f_w_up1", "b_w_o"],
    "ffn0_down_dx": ["b_w_q", "w_kv", "f_w_down0"],
    "a_out_dx": ["f_w_up0", "a_w_out"],
    "a_in_dt_dx": ["a_w_in"],
}
SWAP_SCHEDULE = {"a_in_main_dw": ["f_w_down1", "f_w_up1", "b_w_o", "b_w_q", "w_kv", "f_w_down0", "f_w_up0", "a_w_out"]}


def _buffer_of(weight, layer):
    return weight if layer is None else f"{weight}{layer}"


class _Pipeline:
    def __init__(self, place, slots):
        self.place = place
        self.slots = dict(slots)
        self.running = []
        self.grads = {}
        self.theirs = {}
        self.partials = {}
        self.peers = {}
        self.reduced = {}

    def _collect(self):
        for step, buffers, table in self.running:
            table.update(zip(buffers, step.results))
        self.running = []

    @staticmethod
    def _splits(buffers):
        return [SPLIT.get(b, "rows") for b in buffers]

    def gather_now(self, name, buffers, also=()):
        step = _step_gather_full([self.slots[b] for b in buffers], self._splits(buffers))
        _run_steps(name, [step, *also])
        self.slots.update(zip(buffers, step.results))

    def weight(self, name, layer=None):
        self._collect()
        return _weight_from_gathered(name, self.slots[_buffer_of(name, layer)])

    def grad(self, name, layer, g):
        self.grads[_buffer_of(name, layer)] = _gathered_from_grad(name, g)

    def steps(self, kernel):
        self._collect()
        steps = []
        for phase, buffers in GATHER_SCHEDULE.get(kernel, []):
            bufs, splits = [self.slots[b] for b in buffers], self._splits(buffers)
            step = (_step_gather_d2d(bufs, splits) if phase == "d2d"
                    else _step_gather_ici(bufs, splits, ICI_PEERS[phase]))
            self.running.append((step, buffers, self.slots))
            steps.append(step)
        buffers = PAIR_SCHEDULE.get(kernel)
        if buffers:
            step = _step_pair_exchange([self.grads[b] for b in buffers], self._splits(buffers))
            self.running.append((step, buffers, self.theirs))
            steps.append(step)
        for part, buffers in REDUCE_SCHEDULE.get(kernel, []):
            for b in buffers:
                if b not in self.partials:
                    self.partials[b] = _rs_pair_add("reduce_pair_add_" + b, self.place, self.grads[b], self.theirs[b],
                                                    SPLIT.get(b, "rows"))
            started = [self.peers[b] for b in buffers] if all(b in self.peers for b in buffers) else None
            step = _step_chip_exchange([self.partials[b] for b in buffers], ICI_PEERS[part], into=started)
            self.running.append((step, buffers, self.peers))
            steps.append(step)
        buffers = SWAP_SCHEDULE.get(kernel)
        if buffers:
            step = self._swap_step(buffers)
            self.running.append((step, buffers, self.reduced))
            steps.append(step)
        return steps

    def _swap_step(self, buffers):
        halves = [_rs_chip_add("reduce_chip_add_" + b, self.place, self.partials[b], self.peers[b], SPLIT.get(b, "rows"))
                  for b in buffers]
        return _step_pair_gather(halves, self._splits(buffers))

    def shard(self, buffer):
        self._collect()
        return self.reduced[buffer]

    def last_step(self):
        self._collect()
        rest = [b for b, _, _ in BUFFERS if b not in self.reduced]
        step = self._swap_step(rest)
        self.running.append((step, rest, self.reduced))
        return step


def kernel(x, meta_tokens, a_norm_pre, a_w_in, a_conv_w, a_conv_b, a_dt_bias, a_a_log, a_d_skip, a_gate_norm, a_w_out, a_norm_post, kv_norm, w_kv, b_norm_pre, b_w_q, b_sinks, b_w_o, b_norm_post, f_norm_pre, f_w_up, f_conv_w, f_conv_b, f_w_down, f_norm_post, loss_target, m_meta_tokens, m_a_norm_pre, m_a_w_in, m_a_conv_w, m_a_conv_b, m_a_dt_bias, m_a_a_log, m_a_d_skip, m_a_gate_norm, m_a_w_out, m_a_norm_post, m_kv_norm, m_w_kv, m_b_norm_pre, m_b_w_q, m_b_sinks, m_b_w_o, m_b_norm_post, m_f_norm_pre, m_f_w_up, m_f_conv_w, m_f_conv_b, m_f_w_down, m_f_norm_post, v_meta_tokens, v_a_norm_pre, v_a_w_in, v_a_conv_w, v_a_conv_b, v_a_dt_bias, v_a_a_log, v_a_d_skip, v_a_gate_norm, v_a_w_out, v_a_norm_post, v_kv_norm, v_w_kv, v_b_norm_pre, v_b_w_q, v_b_sinks, v_b_w_o, v_b_norm_post, v_f_norm_pre, v_f_w_up, v_f_conv_w, v_f_conv_b, v_f_w_down, v_f_norm_post):
    given = dict(locals())
    w = {n: given[n] for n in WEIGHTS}
    mom = {n: given["m_" + n] for n in WEIGHTS}
    var = {n: given["v_" + n] for n in WEIGHTS}
    chip = 2 * lax.axis_index("x") + lax.axis_index("y")
    core = lax.axis_index("c")
    place = jnp.stack([chip, core]).astype(jnp.int32)

    small_mine = _pack([w[n] for n in SMALL_SHARDED])
    small_step = _step_gather_small(lax.dynamic_update_slice(
        jnp.zeros((N_CHIPS,) + small_mine.shape, F32), small_mine[None], (chip, 0, 0)))
    slots = {b: _cast_into_slot("cast_" + b, place, _local_shard(w, wn, layer), layer) for b, wn, layer in BUFFERS}
    pipeline = _Pipeline(place, slots)
    pipeline.gather_now("gather_first", ["a_w_in"], also=[small_step])
    small_parts = _unpack(small_step.results[0], SMALL_SHARDED, _shard_shape)
    p = {}
    for n in SMALL:
        p[n] = _join_chips(n, small_parts[n]) if n in SMALL_SHARDED else w[n]
    p["a_conv_w"] = p["a_conv_w"][0]
    p["kv_norm"] = p["kv_norm"].reshape(1, D_MODEL)

    loss_local, grad_x, g = _local_step(x[0], loss_target[0], p, pipeline)

    small_sum = _allreduce_small("reduce_small", _pack([g[n].reshape(FULL_SHAPE[n]) for n in SMALL]
                                                       + [loss_local.reshape(1, 1)]), steps=[pipeline.last_step()])
    small_red = _unpack(small_sum, SMALL + ["loss"], lambda n: (1, 1) if n == "loss" else FULL_SHAPE[n])
    loss = small_red["loss"][0, 0]
    grads = {}
    for n in SMALL:
        if SHARD_AXIS[n] is None:
            grads[n] = small_red[n]
        else:
            grads[n] = lax.dynamic_index_in_dim(_split_chips(n, small_red[n]), chip, 0, keepdims=False)

    delta, new_m, new_v = {}, {}, {}
    for n in BIG:
        shape = _shard_shape(n)
        if n in TRANSPOSED:
            g2d = pipeline.shard(n)
            w2d, m2d, v2d = (arrays[n][0].T for arrays in (w, mom, var))
            back = lambda a: a.T.reshape(shape)
        else:
            g2d = (jnp.concatenate([pipeline.shard(n + "0"), pipeline.shard(n + "1")], axis=0)
                   if n in ("f_w_up", "f_w_down") else pipeline.shard(n))
            w2d, m2d, v2d = (_as2d(arrays[n]) for arrays in (w, mom, var))
            back = lambda a: a.reshape(shape)
        d, m2, v2 = _adamw("adamw_" + n, w2d, g2d, m2d, v2d, steps=pipeline.steps("adamw_" + n))
        grads[n], delta[n], new_m[n], new_v[n] = back(g2d), back(d), back(m2), back(v2)
    at_least_2d = lambda n: (1,) * (2 - len(_shard_shape(n))) + _shard_shape(n)
    outs = _adamw_small("adamw_small", *[[src[n].reshape(at_least_2d(n)) for n in SMALL] for src in (w, grads, mom, var)])
    for dst, arrays in zip((delta, new_m, new_v), outs):
        dst.update({n: a.reshape(_shard_shape(n)) for n, a in zip(SMALL, arrays)})

    return (loss, grad_x[None], *[grads[n].reshape(_shard_shape(n)) for n in WEIGHTS],
            *[delta[n] for n in WEIGHTS], *[new_m[n] for n in WEIGHTS], *[new_v[n] for n in WEIGHTS])
```

```python
import functools
import math

import jax
import jax.numpy as jnp
from jax import lax
from jax.experimental import pallas as pl
from jax.experimental.pallas import tpu as pltpu

F32, BF16 = jnp.float32, jnp.bfloat16
MESH = pl.DeviceIdType.MESH

D_MODEL = 1024
N_META = 16
CHUNK = 128
PAD_ROWS = CHUNK - N_META
D_INNER = 2048
D_STATE = 128
N_GROUPS = 4
HEADS_PER_GROUP = 8
SSM_HEADS = 32
HEAD_DIM = 64
D_BC = N_GROUPS * D_STATE
D_XBC = D_INNER + 2 * D_BC
D_MAIN = D_INNER + D_XBC
D_IN_PROJ = D_MAIN + SSM_HEADS
GROUP_W = HEADS_PER_GROUP * HEAD_DIM
SSM_CONV = 4
D_FF = 2816
FFN_CONV = 3
N_Q_HEADS = 16
N_KV_HEADS = 4
D_KV = 256
ATTN_SCALE = 1.0 / math.sqrt(HEAD_DIM)
RMS_EPS = 1e-6
NEG_INF = -1e30
LANES = 128
VMEM_LIMIT = 51 * 1024 * 1024

ADAM_LR, ADAM_B1, ADAM_B2, ADAM_EPS, ADAM_WD, ADAM_STEP = 0.001, 0.9, 0.999, 1e-08, 0.01, 10

N_CHIPS = 4


def _cparams(sem=None):
    return pltpu.CompilerParams(dimension_semantics=sem, vmem_limit_bytes=VMEM_LIMIT)


def _tile(n, cands=(512, 256, 128)):
    for t in cands:
        if n % t == 0:
            return t
    return n


def _row_tile(rows, width):
    for t in (272,):
        if rows % t == 0 and t * width * 4 <= (3 << 20):
            return t
    return 128


def _rows_mask(i, tm):
    rows = i * tm + lax.broadcasted_iota(jnp.int32, (tm, 1), 0)
    return rows >= PAD_ROWS


def _dot(a, b):
    return jnp.dot(a, b, preferred_element_type=F32)


def _dot_nt(a, b):
    return lax.dot_general(a, b, (((1,), (1,)), ((), ())), preferred_element_type=F32)


def _dot_tn(a, b):
    return lax.dot_general(a, b, (((0,), (0,)), ((), ())), preferred_element_type=F32)


def _sigmoid(x):
    return 1.0 / (1.0 + jnp.exp(-x))


def _place():
    return lax.axis_index("x"), lax.axis_index("y"), lax.axis_index("c")


def _other_chips(x, y):
    return [(1 - x, y), (x, 1 - y), (1 - x, 1 - y)]


class _Step:
    def __init__(self, ins, outs, aliases, n_sems, start, finish):
        self.ins, self.outs, self.aliases, self.n_sems = list(ins), list(outs), dict(aliases), n_sems
        self.start, self.finish = start, finish
        self.results = None


def _like(a):
    return jax.ShapeDtypeStruct(a.shape, a.dtype)


def _remote(src, dst, send_sems, recv_sems, k, device):
    return pltpu.make_async_remote_copy(src, dst, send_sems.at[k], recv_sems.at[k], device_id=device, device_id_type=MESH)


def _half(ref, split, which, lead=()):
    if split == "rows":
        hr = ref.shape[-2] // 2
        return ref.at[lead + (pl.ds(which * hr, hr),)]
    hc = ref.shape[-1] // 2
    return ref.at[lead + (slice(None), pl.ds(which * hc, hc))]


def _splits(bufs, splits):
    return list(splits) if splits is not None else ["rows"] * len(bufs)


ALL_PEERS = (0, 1, 2)
NEAR_PEERS = (0, 1)
FAR_PEERS = (2,)


def _step_gather_ici(bufs, splits=None, peers=ALL_PEERS):
    splits = _splits(bufs, splits)

    def copies(outs, send_sems, recv_sems, received):
        x, y, c = _place()
        me = 2 * x + y
        for k, o in enumerate(outs):
            for j, (cx, cy) in enumerate(_other_chips(x, y)):
                if j in peers:
                    part = _half(o, splits[k], c, (2 * cx + cy if received else me,))
                    yield _remote(part, part, send_sems, recv_sems, 3 * k + j, (cx, cy, c))

    def start(ins, outs, send_sems, recv_sems):
        for cp in copies(outs, send_sems, recv_sems, False):
            cp.start()

    def finish(ins, outs, send_sems, recv_sems):
        for cp in copies(outs, send_sems, recv_sems, True):
            cp.wait_recv()
        for cp in copies(outs, send_sems, recv_sems, False):
            cp.wait_send()

    return _Step(bufs, [_like(b) for b in bufs], {k: k for k in range(len(bufs))}, 3 * len(bufs), start, finish)


def _step_gather_d2d(bufs, splits=None):
    splits = _splits(bufs, splits)

    def copies(outs, send_sems, recv_sems, received):
        x, y, c = _place()
        for k, o in enumerate(outs):
            for j, (cx, cy) in enumerate(_other_chips(x, y)):
                part = _half(o, splits[k], 1 - c if received else c, (2 * cx + cy,))
                yield _remote(part, part, send_sems, recv_sems, 3 * k + j, (x, y, 1 - c))

    def start(ins, outs, send_sems, recv_sems):
        for cp in copies(outs, send_sems, recv_sems, False):
            cp.start()

    def finish(ins, outs, send_sems, recv_sems):
        for cp in copies(outs, send_sems, recv_sems, True):
            cp.wait_recv()
        for cp in copies(outs, send_sems, recv_sems, False):
            cp.wait_send()

    return _Step(bufs, [_like(b) for b in bufs], {k: k for k in range(len(bufs))}, 3 * len(bufs), start, finish)


def _step_gather_full(bufs, splits=None):
    n = len(bufs)
    splits = _splits(bufs, splits)

    def ici(outs, send_sems, recv_sems, received):
        x, y, c = _place()
        me = 2 * x + y
        for k, o in enumerate(outs):
            for j, (cx, cy) in enumerate(_other_chips(x, y)):
                part = _half(o, splits[k], c, (2 * cx + cy if received else me,))
                yield _remote(part, part, send_sems, recv_sems, 3 * k + j, (cx, cy, c))

    def d2d(outs, send_sems, recv_sems, received):
        x, y, c = _place()
        for k, o in enumerate(outs):
            for j, (cx, cy) in enumerate(_other_chips(x, y)):
                part = _half(o, splits[k], 1 - c if received else c, (2 * cx + cy,))
                yield _remote(part, part, send_sems, recv_sems, 3 * n + 3 * k + j, (x, y, 1 - c))

    def start(ins, outs, send_sems, recv_sems):
        for cp in ici(outs, send_sems, recv_sems, False):
            cp.start()

    def finish(ins, outs, send_sems, recv_sems):
        for arrived, onward in zip(ici(outs, send_sems, recv_sems, True), d2d(outs, send_sems, recv_sems, False)):
            arrived.wait_recv()
            onward.start()
        for cp in d2d(outs, send_sems, recv_sems, True):
            cp.wait_recv()
        for cp in ici(outs, send_sems, recv_sems, False):
            cp.wait_send()
        for cp in d2d(outs, send_sems, recv_sems, False):
            cp.wait_send()

    return _Step(bufs, [_like(b) for b in bufs], {k: k for k in range(n)}, 6 * n, start, finish)


def _half_shape(shape, split):
    return shape[:-2] + ((shape[-2] // 2, shape[-1]) if split == "rows" else (shape[-2], shape[-1] // 2))


def _step_pair_exchange(grads, splits=None):
    splits = _splits(grads, splits)

    def copies(ins, outs, send_sems, recv_sems):
        x, y, c = _place()
        for k, (g, o) in enumerate(zip(ins, outs)):
            yield _remote(_half(g, splits[k], 1 - c, (slice(None),)), o, send_sems, recv_sems, k, (x, y, 1 - c))

    def start(ins, outs, send_sems, recv_sems):
        for cp in copies(ins, outs, send_sems, recv_sems):
            cp.start()

    def finish(ins, outs, send_sems, recv_sems):
        for cp in copies(ins, outs, send_sems, recv_sems):
            cp.wait()

    outs = [jax.ShapeDtypeStruct(_half_shape(g.shape, s), g.dtype) for g, s in zip(grads, splits)]
    return _Step(grads, outs, {}, len(grads), start, finish)


def _step_chip_exchange(partials, peers=ALL_PEERS, into=None):
    n = len(partials)

    def copies(ins, outs, send_sems, recv_sems):
        x, y, c = _place()
        for k, (q, o) in enumerate(zip(ins[:n], outs)):
            for j, (cx, cy) in enumerate(_other_chips(x, y)):
                if j in peers:
                    yield _remote(q.at[2 * cx + cy], o.at[j], send_sems, recv_sems, 3 * k + j, (cx, cy, c))

    def start(ins, outs, send_sems, recv_sems):
        for cp in copies(ins, outs, send_sems, recv_sems):
            cp.start()

    def finish(ins, outs, send_sems, recv_sems):
        for cp in copies(ins, outs, send_sems, recv_sems):
            cp.wait()

    outs = [jax.ShapeDtypeStruct((3,) + q.shape[1:], q.dtype) for q in partials]
    if into is None:
        return _Step(partials, outs, {}, 3 * n, start, finish)
    return _Step(list(partials) + list(into), outs, {n + k: k for k in range(n)}, 3 * n, start, finish)


def _step_pair_gather(shards, splits=None):
    splits = _splits(shards, splits)

    def copies(outs, send_sems, recv_sems, received):
        x, y, c = _place()
        for k, o in enumerate(outs):
            part = _half(o, splits[k], 1 - c if received else c)
            yield _remote(part, part, send_sems, recv_sems, k, (x, y, 1 - c))

    def start(ins, outs, send_sems, recv_sems):
        for cp in copies(outs, send_sems, recv_sems, False):
            cp.start()

    def finish(ins, outs, send_sems, recv_sems):
        for cp in copies(outs, send_sems, recv_sems, True):
            cp.wait_recv()
        for cp in copies(outs, send_sems, recv_sems, False):
            cp.wait_send()

    return _Step(shards, [_like(s) for s in shards], {k: k for k in range(len(shards))}, len(shards), start, finish)


def _call(body, *, name, out_shape, grid, in_specs, out_specs, operands, scratch_shapes=(), semantics=None, steps=()):
    single = not isinstance(out_shape, (tuple, list))
    out_shapes = [out_shape] if single else list(out_shape)
    out_spec_list = [out_specs] if single else list(out_specs)
    steps = list(steps)
    if not steps:
        res = pl.pallas_call(body, name=name, out_shape=out_shapes, grid=grid, in_specs=list(in_specs),
                             out_specs=out_spec_list, scratch_shapes=list(scratch_shapes),
                             compiler_params=_cparams(semantics))(*operands)
        return res[0] if single else res
    n_in, n_out, n_scr = len(operands), len(out_shapes), len(scratch_shapes)
    x_in = [a for s in steps for a in s.ins]
    x_out = [o for s in steps for o in s.outs]
    aliases, in_off, out_off = {}, 0, 0
    for s in steps:
        for i, o in s.aliases.items():
            aliases[n_in + in_off + i] = n_out + out_off + o
        in_off += len(s.ins)
        out_off += len(s.outs)
    sems = []
    for s in steps:
        sems += [pltpu.SemaphoreType.DMA((s.n_sems,)), pltpu.SemaphoreType.DMA((s.n_sems,))]
    any_spec = pl.BlockSpec(memory_space=pl.ANY)

    def carried(*refs):
        pos = 0
        ins = refs[pos:pos + n_in]; pos += n_in
        xi = refs[pos:pos + len(x_in)]; pos += len(x_in)
        outs = refs[pos:pos + n_out]; pos += n_out
        xo = refs[pos:pos + len(x_out)]; pos += len(x_out)
        scr = refs[pos:pos + n_scr]; pos += n_scr
        sem_refs = refs[pos:]

        def each(action):
            i0 = o0 = 0
            for k, s in enumerate(steps):
                getattr(s, action)(xi[i0:i0 + len(s.ins)], xo[o0:o0 + len(s.outs)], sem_refs[2 * k], sem_refs[2 * k + 1])
                i0 += len(s.ins)
                o0 += len(s.outs)

        if grid:
            first = functools.reduce(jnp.logical_and, [pl.program_id(d) == 0 for d in range(len(grid))])
            last = functools.reduce(jnp.logical_and, [pl.program_id(d) == grid[d] - 1 for d in range(len(grid))])
            pl.when(first)(lambda: each("start"))
            body(*ins, *outs, *scr)
            pl.when(last)(lambda: each("finish"))
        else:
            each("start")
            body(*ins, *outs, *scr)
            each("finish")

    res = pl.pallas_call(
        carried, name=name, out_shape=out_shapes + x_out, grid=grid,
        in_specs=list(in_specs) + [any_spec] * len(x_in), out_specs=out_spec_list + [any_spec] * len(x_out),
        scratch_shapes=list(scratch_shapes) + sems, input_output_aliases=aliases,
        compiler_params=_cparams(None if semantics is None else ("arbitrary",) * len(grid)),
    )(*operands, *x_in)
    o0 = n_out
    for s in steps:
        s.results = list(res[o0:o0 + len(s.outs)])
        o0 += len(s.outs)
    return res[0] if single else tuple(res[:n_out])


def _run_steps(name, steps):
    _call(lambda: None, name=name, out_shape=[], grid=(), in_specs=[], out_specs=[], operands=[], steps=steps)
    return [s.results for s in steps]


def _mm(name, a, b, mode, out_dtype=F32, acc=None, b_colblock=0, k_rows=None, out_rows=None, steps=()):
    resident_bytes = 8 << 20
    if mode == "nn":
        m, k = a.shape
        n = b.shape[1]
        tm = m
        while tm * k * 2 > resident_bytes and tm % 32 == 0:
            tm //= 2
        tn = _tile(n)
        grid = (m // tm, n // tn)
        in_specs = [pl.BlockSpec((tm, k), lambda i, j: (i, 0)), pl.BlockSpec((k, tn), lambda i, j: (0, j))]
        out_shape, out_block = (m, n), (tm, tn)
    elif mode == "nt":
        m, n = a.shape
        k = k_rows or b.shape[0]
        tm = m
        while tm * n * 2 > resident_bytes and tm % 32 == 0:
            tm //= 2
        tk = _tile(k)
        grid = (m // tm, k // tk)
        in_specs = [pl.BlockSpec((tm, n), lambda i, j: (i, 0)), pl.BlockSpec((tk, n), lambda i, j: (j, b_colblock))]
        out_shape, out_block = (m, k), (tm, tk)
    else:
        m, k = a.shape
        n = b.shape[1]
        tk, tn = _tile(k), (n if m * n * 2 <= resident_bytes else _tile(n))
        grid = (k // tk, n // tn)
        in_specs = [pl.BlockSpec((m, tk), lambda i, j: (0, i)), pl.BlockSpec((m, tn), lambda i, j: (0, j))]
        out_shape, out_block = (out_rows or k, n), (tk, tn)
    out_spec = pl.BlockSpec(out_block, lambda i, j: (i, j))
    has_acc = acc is not None

    def body(*refs):
        a_ref, b_ref = refs[0], refs[1]
        o_ref = refs[-1]
        av, bv = a_ref[...], b_ref[...]
        if mode == "nn":
            r = _dot(av, bv)
        elif mode == "nt":
            r = _dot_nt(av, bv)
        else:
            r = _dot_tn(av, bv)
        if has_acc:
            r = r + refs[2][...]
        o_ref[...] = r.astype(o_ref.dtype)

    operands = [a, b]
    if has_acc:
        in_specs = in_specs + [out_spec]
        operands.append(acc)
    return _call(body, name=name, out_shape=jax.ShapeDtypeStruct(out_shape, out_dtype), grid=grid, in_specs=in_specs,
                 out_specs=out_spec, operands=operands, semantics=("parallel", "parallel"), steps=steps)


def _tn_rows_into(name, a, b, into, row0, nrows):
    m, k = a.shape
    n = b.shape[1]

    def body(a_ref, b_ref, into_ref, o_ref):
        o_ref[...] = _dot_tn(a_ref[...], b_ref[...])[0:nrows].astype(o_ref.dtype)

    return pl.pallas_call(
        body, name=name, out_shape=jax.ShapeDtypeStruct(into.shape, into.dtype), grid=(1,),
        in_specs=[pl.BlockSpec((m, k), lambda i: (0, 0)), pl.BlockSpec((m, n), lambda i: (0, 0)),
                  pl.BlockSpec(memory_space=pl.ANY)],
        out_specs=pl.BlockSpec((nrows, n), lambda i: (row0 // nrows, 0)),
        input_output_aliases={2: 0}, compiler_params=_cparams(("arbitrary",)),
    )(a, b, into)


def _rms_fwd(name, h, w):
    rows, width = h.shape
    tm = _row_tile(rows, width)

    def body(h_ref, w_ref, o_ref):
        x = h_ref[...]
        r = lax.rsqrt(jnp.mean(x * x, axis=-1, keepdims=True) + RMS_EPS)
        o_ref[...] = (x * r * w_ref[...]).astype(BF16)

    return pl.pallas_call(
        body, name=name, out_shape=jax.ShapeDtypeStruct((rows, width), BF16), grid=(rows // tm,),
        in_specs=[pl.BlockSpec((tm, width), lambda i: (i, 0)), pl.BlockSpec((1, width), lambda i: (0, 0))],
        out_specs=pl.BlockSpec((tm, width), lambda i: (i, 0)), compiler_params=_cparams(("parallel",)),
    )(h, w)


def _resid_norm_fwd(name, h, pre, w, next_norms=()):
    rows, width = h.shape
    tm = _row_tile(rows, width)
    n_next = len(next_norms)

    def body(*refs):
        h_ref, p_ref, w_ref = refs[:3]
        v_refs = refs[3:3 + n_next]
        o_ref = refs[3 + n_next]
        n_refs = refs[4 + n_next:]
        p = p_ref[...]
        r = lax.rsqrt(jnp.mean(p * p, axis=-1, keepdims=True) + RMS_EPS)
        x = h_ref[...] + jnp.where(_rows_mask(pl.program_id(0), tm), p * r * w_ref[...], 0.0)
        o_ref[...] = x
        if n_next:
            rx = lax.rsqrt(jnp.mean(x * x, axis=-1, keepdims=True) + RMS_EPS)
            for v_ref, n_ref in zip(v_refs, n_refs):
                n_ref[...] = (x * rx * v_ref[...]).astype(BF16)

    row_spec = pl.BlockSpec((tm, width), lambda i: (i, 0))
    vec_spec = pl.BlockSpec((1, width), lambda i: (0, 0))
    outs = pl.pallas_call(
        body, name=name,
        out_shape=[jax.ShapeDtypeStruct((rows, width), F32)] + [jax.ShapeDtypeStruct((rows, width), BF16)] * n_next,
        grid=(rows // tm,), in_specs=[row_spec, row_spec, vec_spec] + [vec_spec] * n_next,
        out_specs=[row_spec] * (1 + n_next), compiler_params=_cparams(("parallel",)),
    )(h, pre, w, *next_norms)
    return outs[0], list(outs[1:])


def _resid_norm_loss(name, h, pre, w, target):
    rows, width = h.shape

    def body(h_ref, p_ref, w_ref, t_ref, dh_ref, loss_ref, dp_ref, dw_ref):
        i = pl.program_id(0)
        p = p_ref[...]
        r = lax.rsqrt(jnp.mean(p * p, axis=-1, keepdims=True) + RMS_EPS)
        x = h_ref[...] + p * r * w_ref[...]
        real = (i + jnp.zeros((CHUNK, 1), jnp.int32)) >= 1
        diff = jnp.where(real, x - t_ref[...], 0.0)
        dh = diff * (1.0 / D_MODEL)
        dh_ref[...] = dh
        dp, dw_rows = _rms_bwd(dh, p, w_ref[...])
        dp_ref[...] = dp.astype(BF16)

        @pl.when(i == 0)
        def _():
            loss_ref[...] = jnp.zeros_like(loss_ref)
            dw_ref[...] = jnp.zeros_like(dw_ref)

        loss_ref[...] += jnp.sum(diff * diff) * (0.5 / D_MODEL)
        dw_ref[...] += jnp.sum(dw_rows, axis=0, keepdims=True)

    blk = pl.BlockSpec((CHUNK, width), lambda i: (i, 0))
    vec_spec = pl.BlockSpec((1, width), lambda i: (0, 0))
    return pl.pallas_call(
        body, name=name,
        out_shape=(jax.ShapeDtypeStruct((rows, width), F32), jax.ShapeDtypeStruct((1, LANES), F32),
                   jax.ShapeDtypeStruct((rows, width), BF16), jax.ShapeDtypeStruct((1, width), F32)),
        grid=(rows // CHUNK,),
        in_specs=[blk, blk, vec_spec, pl.BlockSpec((CHUNK, width), lambda i: (jnp.maximum(i - 1, 0), 0))],
        out_specs=(blk, pl.BlockSpec((1, LANES), lambda i: (0, 0)), blk, vec_spec),
        compiler_params=_cparams(("arbitrary",)),
    )(h, pre, w, target)


def _rms_bwd(dy, x, w):
    r = lax.rsqrt(jnp.mean(x * x, axis=-1, keepdims=True) + RMS_EPS)
    xhat = x * r
    dxhat = dy * w
    return r * (dxhat - xhat * jnp.mean(dxhat * xhat, axis=-1, keepdims=True)), dy * xhat


def _norm_bwd_add(name, dh, dhn, h, w, then=None, split_first_block=False, steps=()):
    rows, width = dh.shape
    tm = CHUNK if split_first_block else _row_tile(rows, width)
    fused = then is not None
    assert not (fused and split_first_block)

    def body(*refs):
        dh_ref, dhn_ref, h_ref, w_ref = refs[:4]
        o_ref, dw_ref = refs[6:8] if fused else refs[-2:]
        i = pl.program_id(0)
        valid = _rows_mask(i, tm)
        dx, dw_rows = _rms_bwd(dhn_ref[...], h_ref[...], w_ref[...])
        dh_new = dh_ref[...] + jnp.where(valid, dx, 0.0)
        if split_first_block:
            first_ref = refs[4]

            @pl.when(i == 0)
            def _():
                first_ref[...] = dh_new

            @pl.when(i > 0)
            def _():
                o_ref[...] = dh_new
        else:
            o_ref[...] = dh_new

        @pl.when(i == 0)
        def _():
            dw_ref[...] = jnp.zeros_like(dw_ref)

        dw_ref[...] += jnp.sum(dw_rows, axis=0, keepdims=True)
        if fused:
            p_ref, wp_ref, dp_ref, dwp_ref = refs[4], refs[5], refs[8], refs[9]
            dp, dwp_rows = _rms_bwd(jnp.where(valid, dh_new, 0.0), p_ref[...], wp_ref[...])
            dp_ref[...] = dp.astype(BF16)

            @pl.when(i == 0)
            def _():
                dwp_ref[...] = jnp.zeros_like(dwp_ref)

            dwp_ref[...] += jnp.sum(dwp_rows, axis=0, keepdims=True)

    row_spec = pl.BlockSpec((tm, width), lambda i: (i, 0))
    vec_spec = pl.BlockSpec((1, width), lambda i: (0, 0))
    row_f32, vec_f32 = jax.ShapeDtypeStruct((rows, width), F32), jax.ShapeDtypeStruct((1, width), F32)
    in_specs, operands = [row_spec, row_spec, row_spec, vec_spec], [dh, dhn, h, w]
    out_shape, out_specs = [row_f32, vec_f32], [row_spec, vec_spec]
    if split_first_block:
        out_shape = [jax.ShapeDtypeStruct((tm, width), F32), jax.ShapeDtypeStruct((rows - tm, width), F32), vec_f32]
        out_specs = [pl.BlockSpec((tm, width), lambda i: (0, 0)),
                     pl.BlockSpec((tm, width), lambda i: (jnp.maximum(i - 1, 0), 0)), vec_spec]
    if fused:
        in_specs += [row_spec, vec_spec]
        operands += list(then)
        out_shape += [jax.ShapeDtypeStruct((rows, width), BF16), vec_f32]
        out_specs += [row_spec, vec_spec]
    return _call(body, name=name, out_shape=out_shape, grid=(rows // tm,), in_specs=in_specs, out_specs=out_specs,
                 operands=operands, semantics=("arbitrary",), steps=steps)


def _shift_down(x, s, rows):
    return pltpu.roll(x, s, 0) if s else x


def _shift_up(x, s, rows):
    return pltpu.roll(x, rows - s, 0) if s else x


def _conv4_fwd(name, zx, cw, cb, steps=()):
    rows = zx.shape[0]
    off = D_INNER // LANES

    def body(x_ref, w_ref, b_ref, o_ref):
        x = x_ref[...]
        acc = b_ref[...] + w_ref[pl.ds(SSM_CONV - 1, 1), :] * x
        for s in range(1, SSM_CONV):
            acc = acc + w_ref[pl.ds(SSM_CONV - 1 - s, 1), :] * _shift_down(x, s, rows)
        valid = lax.broadcasted_iota(jnp.int32, (rows, 1), 0) >= PAD_ROWS
        o_ref[...] = jnp.where(valid, acc * _sigmoid(acc), 0.0)

    return _call(
        body, name=name, out_shape=jax.ShapeDtypeStruct((rows, D_XBC), F32), grid=(D_XBC // LANES,),
        in_specs=[pl.BlockSpec((rows, LANES), lambda j: (0, j + off)),
                  pl.BlockSpec((SSM_CONV, LANES), lambda j: (0, j)),
                  pl.BlockSpec((1, LANES), lambda j: (0, j))],
        out_specs=pl.BlockSpec((rows, LANES), lambda j: (0, j)), operands=[zx, cw, cb],
        semantics=("parallel",), steps=steps)


def _conv4_bwd(name, zx, dout, cw, cb, into):
    rows, width = dout.shape
    zoff = D_INNER // LANES

    def body(x_ref, d_ref, w_ref, b_ref, into_ref, dx_ref, dw_ref, db_ref):
        x = x_ref[...]
        shifted = [_shift_down(x, s, rows) for s in range(SSM_CONV)]
        acc = b_ref[...]
        for s in range(SSM_CONV):
            acc = acc + w_ref[pl.ds(SSM_CONV - 1 - s, 1), :] * shifted[s]
        sig = _sigmoid(acc)
        valid = lax.broadcasted_iota(jnp.int32, (rows, 1), 0) >= PAD_ROWS
        dpre = jnp.where(valid, d_ref[...] * sig * (1.0 + acc * (1.0 - sig)), 0.0)
        dx = w_ref[pl.ds(SSM_CONV - 1, 1), :] * dpre
        for s in range(1, SSM_CONV):
            dx = dx + w_ref[pl.ds(SSM_CONV - 1 - s, 1), :] * _shift_up(dpre, s, rows)
        dx_ref[...] = dx.astype(BF16)
        for s in range(SSM_CONV):
            dw_ref[pl.ds(SSM_CONV - 1 - s, 1), :] = jnp.sum(dpre * shifted[s], axis=0, keepdims=True)
        db_ref[...] = jnp.sum(dpre, axis=0, keepdims=True)

    return pl.pallas_call(
        body, name=name,
        out_shape=(jax.ShapeDtypeStruct(into.shape, BF16), jax.ShapeDtypeStruct((SSM_CONV, width), F32),
                   jax.ShapeDtypeStruct((1, width), F32)),
        grid=(width // LANES,),
        in_specs=[pl.BlockSpec((rows, LANES), lambda j: (0, j + zoff)),
                  pl.BlockSpec((rows, LANES), lambda j: (0, j)),
                  pl.BlockSpec((SSM_CONV, LANES), lambda j: (0, j)),
                  pl.BlockSpec((1, LANES), lambda j: (0, j)),
                  pl.BlockSpec(memory_space=pl.ANY)],
        out_specs=(pl.BlockSpec((rows, LANES), lambda j: (0, j + zoff)),
                   pl.BlockSpec((SSM_CONV, LANES), lambda j: (0, j)),
                   pl.BlockSpec((1, LANES), lambda j: (0, j))),
        input_output_aliases={4: 0}, compiler_params=_cparams(("parallel",)),
    )(zx, dout, cw, cb, into)


FFN_TILE = 2 * LANES


def _ffn_up_conv(name, hn, w_up, cw, cb, steps=()):
    rows, k = hn.shape
    chip_blocks = w_up.shape[2] // LANES
    half_blocks = D_FF // LANES
    nt = D_FF // FFN_TILE

    def weight_block(offset):
        return pl.BlockSpec((None, k, LANES), lambda j: ((2 * j + offset) // chip_blocks, 0, (2 * j + offset) % chip_blocks))

    def body(a_ref, g0, g1, v0, v1, wg_ref, wv_ref, bg_ref, bv_ref, upg_ref, upv_ref, act_ref):
        a = a_ref[...]
        g = _dot(a, jnp.concatenate([g0[...], g1[...]], axis=1))
        v = _dot(a, jnp.concatenate([v0[...], v1[...]], axis=1))
        upg_ref[...] = g
        upv_ref[...] = v
        ug, uv = bg_ref[...], bv_ref[...]
        for s in range(FFN_CONV):
            ug = ug + wg_ref[pl.ds(FFN_CONV - 1 - s, 1), :] * _shift_down(g, s, rows)
            uv = uv + wv_ref[pl.ds(FFN_CONV - 1 - s, 1), :] * _shift_down(v, s, rows)
        act_ref[...] = (ug * _sigmoid(ug) * uv).astype(BF16)

    col = pl.BlockSpec((rows, FFN_TILE), lambda j: (0, j))
    wsp = lambda shift: pl.BlockSpec((FFN_CONV, FFN_TILE), lambda j: (0, j + shift))
    bsp = lambda shift: pl.BlockSpec((1, FFN_TILE), lambda j: (0, j + shift))
    half = jax.ShapeDtypeStruct((rows, D_FF), F32)
    return _call(
        body, name=name, out_shape=(half, half, jax.ShapeDtypeStruct((rows, D_FF), BF16)), grid=(nt,),
        in_specs=[pl.BlockSpec((rows, k), lambda j: (0, 0)), weight_block(0), weight_block(1),
                  weight_block(half_blocks), weight_block(half_blocks + 1), wsp(0), wsp(nt), bsp(0), bsp(nt)],
        out_specs=(col, col, col), operands=[hn, w_up, w_up, w_up, w_up, cw, cw, cb, cb],
        semantics=("parallel",), steps=steps)


def _ffn_conv_bwd(name, up_g, up_v, dact, cw, cb, hn, w_up, steps=()):
    rows, k = hn.shape
    chip_blocks = w_up.shape[2] // LANES
    nt = D_FF // LANES

    def weight_block(shift):
        return pl.BlockSpec((None, k, LANES), lambda j: ((j + shift) // chip_blocks, 0, (j + shift) % chip_blocks))

    def body(g_ref, v_ref, d_ref, wg_ref, wv_ref, bg_ref, bv_ref, upg_ref, upv_ref, hn_ref,
             dwg_ref, dwv_ref, dbg_ref, dbv_ref, dhn_ref, dup_ref, acc, hn_scr, hnt_scr, dup_scr, sems):
        j = pl.program_id(0)
        hn_copy = pltpu.make_async_copy(hn_ref, hn_scr, sems.at[0])
        dhn_copy = pltpu.make_async_copy(acc, dhn_ref, sems.at[0])

        def dup_copy(step, half):
            block, slot = step + half * nt, 2 * (step % 2) + half
            cols = pl.ds(pl.multiple_of((block % chip_blocks) * LANES, LANES), LANES)
            return pltpu.make_async_copy(dup_scr.at[slot], dup_ref.at[block // chip_blocks, :, cols], sems.at[1 + slot])

        @pl.when(j == 0)
        def _():
            hn_copy.start()
            acc[...] = jnp.zeros_like(acc)
            hn_copy.wait()
            for r in range(0, rows, LANES):
                hnt_scr[:, r:r + LANES] = hn_scr[r:r + LANES, :].T

        @pl.when(j >= 2)
        def _():
            dup_copy(j - 2, 0).wait()
            dup_copy(j - 2, 1).wait()

        g, v = g_ref[...], v_ref[...]
        gs = [_shift_down(g, s, rows) for s in range(FFN_CONV)]
        vs = [_shift_down(v, s, rows) for s in range(FFN_CONV)]
        ug, uv = bg_ref[...], bv_ref[...]
        for s in range(FFN_CONV):
            ug = ug + wg_ref[pl.ds(FFN_CONV - 1 - s, 1), :] * gs[s]
            uv = uv + wv_ref[pl.ds(FFN_CONV - 1 - s, 1), :] * vs[s]
        sig = _sigmoid(ug)
        dsig = d_ref[...] * sig
        dup = []
        for dpre, src, w_ref, dw_ref, db_ref in (
                (dsig * uv * (1.0 + ug * (1.0 - sig)), gs, wg_ref, dwg_ref, dbg_ref),
                (dsig * ug, vs, wv_ref, dwv_ref, dbv_ref)):
            dx = w_ref[pl.ds(FFN_CONV - 1, 1), :] * dpre
            for s in range(1, FFN_CONV):
                dx = dx + w_ref[pl.ds(FFN_CONV - 1 - s, 1), :] * _shift_up(dpre, s, rows)
            dup.append(dx.astype(BF16))
            for s in range(FFN_CONV):
                dw_ref[pl.ds(FFN_CONV - 1 - s, 1), :] = jnp.sum(dpre * src[s], axis=0, keepdims=True)
            db_ref[...] = jnp.sum(dpre, axis=0, keepdims=True)
        dup = jnp.concatenate(dup, axis=1)
        acc[...] += _dot_nt(dup, jnp.concatenate([upg_ref[...], upv_ref[...]], axis=1))
        dw = _dot(hnt_scr[...], dup)
        slot = 2 * (j % 2)
        dup_scr[slot] = dw[:, :LANES].astype(BF16)
        dup_scr[slot + 1] = dw[:, LANES:].astype(BF16)
        dup_copy(j, 0).start()
        dup_copy(j, 1).start()

        @pl.when(j == nt - 1)
        def _():
            dhn_copy.start()
            for step in (j - 1, j):
                dup_copy(step, 0).wait()
                dup_copy(step, 1).wait()
            dhn_copy.wait()

    col = pl.BlockSpec((rows, LANES), lambda j: (0, j))
    wsp = lambda shift: pl.BlockSpec((FFN_CONV, LANES), lambda j: (0, j + shift))
    bsp = lambda shift: pl.BlockSpec((1, LANES), lambda j: (0, j + shift))
    any_spec = pl.BlockSpec(memory_space=pl.ANY)
    dw_shape = jax.ShapeDtypeStruct((FFN_CONV, D_FF), F32)
    db_shape = jax.ShapeDtypeStruct((1, D_FF), F32)
    return _call(
        body, name=name, grid=(nt,),
        out_shape=(dw_shape, dw_shape, db_shape, db_shape, jax.ShapeDtypeStruct((rows, k), F32),
                   jax.ShapeDtypeStruct(w_up.shape, BF16)),
        in_specs=[col, col, col, wsp(0), wsp(nt), bsp(0), bsp(nt), weight_block(0), weight_block(nt), any_spec],
        out_specs=(wsp(0), wsp(0), bsp(0), bsp(0), any_spec, any_spec),
        operands=[up_g, up_v, dact, cw, cw, cb, cb, w_up, w_up, hn],
        scratch_shapes=[pltpu.VMEM((rows, k), F32), pltpu.VMEM((rows, k), BF16), pltpu.VMEM((k, rows), BF16),
                        pltpu.VMEM((4, k, LANES), BF16), pltpu.SemaphoreType.DMA((5,))],
        semantics=("arbitrary",), steps=steps)


def _dt_fwd(name, dtr, bias):
    rows = dtr.shape[0]
    tm = _row_tile(rows, LANES)

    def body(d_ref, b_ref, o_ref):
        v = d_ref[...] + b_ref[...]
        sp = jnp.maximum(v, 0.0) + jnp.log1p(jnp.exp(-jnp.abs(v)))
        lane = lax.broadcasted_iota(jnp.int32, (tm, LANES), 1)
        ok = _rows_mask(pl.program_id(0), tm) & (lane < SSM_HEADS)
        o_ref[...] = jnp.where(ok, sp, 0.0)

    return pl.pallas_call(
        body, name=name, out_shape=jax.ShapeDtypeStruct((rows, LANES), F32), grid=(rows // tm,),
        in_specs=[pl.BlockSpec((tm, LANES), lambda i: (i, 0)), pl.BlockSpec((1, LANES), lambda i: (0, 0))],
        out_specs=pl.BlockSpec((tm, LANES), lambda i: (i, 0)), compiler_params=_cparams(("parallel",)),
    )(dtr, bias)


def _dt_bwd(name, ddt, dtr, bias):
    rows = dtr.shape[0]
    tm = _row_tile(rows, LANES)

    def body(g_ref, d_ref, b_ref, o_ref, db_ref):
        i = pl.program_id(0)
        lane = lax.broadcasted_iota(jnp.int32, (tm, LANES), 1)
        ok = _rows_mask(i, tm) & (lane < SSM_HEADS)
        dv = jnp.where(ok, g_ref[...] * _sigmoid(d_ref[...] + b_ref[...]), 0.0)
        o_ref[...] = dv.astype(BF16)

        @pl.when(i == 0)
        def _():
            db_ref[...] = jnp.zeros_like(db_ref)

        db_ref[...] += jnp.sum(dv, axis=0, keepdims=True)

    row_spec = pl.BlockSpec((tm, LANES), lambda i: (i, 0))
    vec_spec = pl.BlockSpec((1, LANES), lambda i: (0, 0))
    return pl.pallas_call(
        body, name=name,
        out_shape=(jax.ShapeDtypeStruct((rows, LANES), BF16), jax.ShapeDtypeStruct((1, LANES), F32)),
        grid=(rows // tm,), in_specs=[row_spec, row_spec, vec_spec], out_specs=(row_spec, vec_spec),
        compiler_params=_cparams(("arbitrary",)),
    )(ddt, dtr, bias)


def _gate_fwd(name, y, zx, w, steps=()):
    rows = y.shape[0]
    tm = _row_tile(rows, D_INNER)

    def body(y_ref, z_ref, w_ref, o_ref):
        z = z_ref[...]
        g = y_ref[...] * (z * _sigmoid(z))
        r = lax.rsqrt(jnp.mean(g * g, axis=-1, keepdims=True) + RMS_EPS)
        o_ref[...] = (g * r * w_ref[...]).astype(BF16)

    row_spec = pl.BlockSpec((tm, D_INNER), lambda i: (i, 0))
    return _call(
        body, name=name, out_shape=jax.ShapeDtypeStruct((rows, D_INNER), BF16), grid=(rows // tm,),
        in_specs=[row_spec, row_spec, pl.BlockSpec((1, D_INNER), lambda i: (0, 0))],
        out_specs=row_spec, operands=[y, zx, w], semantics=("parallel",), steps=steps)


def _gate_bwd(name, dyn, y, zx, w):
    rows = y.shape[0]
    tm = _row_tile(rows, D_INNER)

    def body(d_ref, y_ref, z_ref, w_ref, dy_ref, dz_ref, dw_ref):
        i = pl.program_id(0)
        z, yv = z_ref[...], y_ref[...]
        sig = _sigmoid(z)
        sz = z * sig
        g = yv * sz
        r = lax.rsqrt(jnp.mean(g * g, axis=-1, keepdims=True) + RMS_EPS)
        ghat = g * r
        dn = d_ref[...]
        dghat = dn * w_ref[...]
        dg = r * (dghat - ghat * jnp.mean(dghat * ghat, axis=-1, keepdims=True))
        dy_ref[...] = dg * sz
        dz_ref[...] = (dg * yv * sig * (1.0 + z * (1.0 - sig))).astype(BF16)

        @pl.when(i == 0)
        def _():
            dw_ref[...] = jnp.zeros_like(dw_ref)

        dw_ref[...] += jnp.sum(dn * ghat, axis=0, keepdims=True)

    row_spec = pl.BlockSpec((tm, D_INNER), lambda i: (i, 0))
    vec_spec = pl.BlockSpec((1, D_INNER), lambda i: (0, 0))
    return pl.pallas_call(
        body, name=name,
        out_shape=(jax.ShapeDtypeStruct((rows, D_INNER), F32), jax.ShapeDtypeStruct((rows, D_MAIN), BF16),
                   jax.ShapeDtypeStruct((1, D_INNER), F32)),
        grid=(rows // tm,), in_specs=[row_spec, row_spec, row_spec, vec_spec],
        out_specs=(row_spec, row_spec, vec_spec), compiler_params=_cparams(("arbitrary",)),
    )(dyn, y, zx, w)


def _split3(x):
    hi = x.astype(BF16)
    r1 = x - hi.astype(F32)
    mid = r1.astype(BF16)
    lo = (r1 - mid.astype(F32)).astype(BF16)
    return hi, mid, lo


def _dot3_data_lhs(x, sel):
    sel16 = sel.astype(F32).astype(BF16)
    hi, mid, lo = _split3(x)
    return _dot(hi, sel16) + _dot(mid, sel16) + _dot(lo, sel16)


def _dot2_data_lhs(x, sel):
    sel16 = sel.astype(F32).astype(BF16)
    hi = x.astype(BF16)
    mid = (x - hi.astype(F32)).astype(BF16)
    return _dot(hi, sel16) + _dot(mid, sel16)


def _dot3_data_rhs(sel, x):
    sel16 = sel.astype(F32).astype(BF16)
    hi, mid, lo = _split3(x)
    return _dot(sel16, hi) + _dot(sel16, mid) + _dot(sel16, lo)


def _causal_masks():
    r = lax.broadcasted_iota(jnp.int32, (CHUNK, CHUNK), 0)
    c = lax.broadcasted_iota(jnp.int32, (CHUNK, CHUNK), 1)
    return r >= c, r <= c


def _expand_heads_matrix(g):
    k = lax.broadcasted_iota(jnp.int32, (LANES, GROUP_W), 0)
    j = lax.broadcasted_iota(jnp.int32, (LANES, GROUP_W), 1)
    return HEADS_PER_GROUP * g + jnp.right_shift(j, 6) == k


def _reduce_heads_matrix(g):
    j = lax.broadcasted_iota(jnp.int32, (GROUP_W, LANES), 0)
    k = lax.broadcasted_iota(jnp.int32, (GROUP_W, LANES), 1)
    return HEADS_PER_GROUP * g + jnp.right_shift(j, 6) == k


def _reduce_pair_matrix(g, p):
    j = lax.broadcasted_iota(jnp.int32, (LANES, LANES), 0)
    k = lax.broadcasted_iota(jnp.int32, (LANES, LANES), 1)
    return HEADS_PER_GROUP * g + 2 * p + jnp.right_shift(j, 6) == k


def _group_cols(ref, g, width):
    return ref.at[:, pl.ds(g * width, width)]


def _ssd_prep(name, dt, a128, steps=()):
    rows = dt.shape[0]
    nc = rows // CHUNK

    def body(dt_ref, a_ref, dte_ref, acs_ref, acst_ref):
        causal, _ = _causal_masks()
        dtv = dt_ref[...]
        acs = _dot3_data_rhs(causal, dtv) * a_ref[...]
        acst_ref[...] = acs.T[0:SSM_HEADS]
        for g in range(N_GROUPS):
            expand = _expand_heads_matrix(g)
            _group_cols(dte_ref, g, GROUP_W)[...] = _dot3_data_lhs(dtv, expand)
            _group_cols(acs_ref, g, GROUP_W)[...] = _dot3_data_lhs(acs, expand)

    blk = pl.BlockSpec((CHUNK, D_INNER), lambda c: (c, 0))
    shp = jax.ShapeDtypeStruct((rows, D_INNER), F32)
    return _call(
        body, name=name, out_shape=(shp, shp, jax.ShapeDtypeStruct((nc, SSM_HEADS, CHUNK), F32)), grid=(nc,),
        in_specs=[pl.BlockSpec((CHUNK, LANES), lambda c: (c, 0)), pl.BlockSpec((1, LANES), lambda c: (0, 0))],
        out_specs=(blk, blk, pl.BlockSpec((None, SSM_HEADS, CHUNK), lambda c: (c, 0, 0))),
        operands=[dt, a128], semantics=("parallel",), steps=steps)


def _ssd_common(x_ref, b_ref, c_ref, dte_ref, acs_ref):
    x = x_ref[...]
    dt_exp = dte_ref[...]
    acs_exp = acs_ref[...]
    tot_exp = acs_ref[pl.ds(CHUNK - 1, 1), :]
    xdt = x * dt_exp
    e_exp = jnp.exp(acs_exp)
    f_exp = jnp.exp(tot_exp - acs_exp)
    return _causal_masks(), x, dt_exp, acs_exp, tot_exp, xdt, e_exp, f_exp, b_ref[...], c_ref[...]


def _pair_decay(acs_pair, acs_row, e, causal):
    lane = lax.broadcasted_iota(jnp.int32, (CHUNK, LANES), 1)
    mine = (lane < HEAD_DIM) if e == 0 else (lane >= HEAD_DIM)
    a_l = jnp.where(mine, acs_pair, pltpu.roll(acs_pair, HEAD_DIM, 1))
    seg = a_l - acs_row
    dm = jnp.where(causal[0], jnp.exp(jnp.minimum(seg, 0.0)), 0.0)
    dmt = jnp.where(causal[1], jnp.exp(jnp.minimum(-seg, 0.0)), 0.0)
    return dm, dmt


def _ssd_specs(index_of_chunk):
    wide = pl.BlockSpec((CHUNK, D_INNER), lambda c: (index_of_chunk(c), 0))
    b_spec = pl.BlockSpec((CHUNK, D_BC), lambda c: (index_of_chunk(c), D_INNER // D_BC))
    c_spec = pl.BlockSpec((CHUNK, D_BC), lambda c: (index_of_chunk(c), D_INNER // D_BC + 1))
    rows_spec = pl.BlockSpec((None, SSM_HEADS, CHUNK), lambda c: (index_of_chunk(c), 0, 0))
    state_spec = pl.BlockSpec((N_GROUPS, None, D_STATE, GROUP_W), lambda c: (0, index_of_chunk(c), 0, 0))
    return wide, b_spec, c_spec, rows_spec, state_spec


def _ssd_fwd(name, xbc, dt_exp, acs_exp, acs_rows, dskexp, steps=()):
    rows = xbc.shape[0]
    nc = rows // CHUNK

    def body(x_ref, b_ref, c_ref, dte_ref, acs_ref, acst_ref, dsk_ref, y_ref, st_ref, s_scr):
        @pl.when(pl.program_id(0) == 0)
        def _():
            s_scr[...] = jnp.zeros_like(s_scr)

        lane = lax.broadcasted_iota(jnp.int32, (CHUNK, LANES), 1)
        for g in range(N_GROUPS):
            y_g = _group_cols(y_ref, g, GROUP_W)
            causal, x, _, acs_exp_v, tot_exp, xdt, e_exp, f_exp, bm, cm = _ssd_common(
                _group_cols(x_ref, g, GROUP_W), _group_cols(b_ref, g, D_STATE), _group_cols(c_ref, g, D_STATE),
                _group_cols(dte_ref, g, GROUP_W), _group_cols(acs_ref, g, GROUP_W))
            state = s_scr[g]
            st_ref[g] = state
            cb16, bb16 = cm.astype(BF16), bm.astype(BF16)
            cb = _dot_nt(cb16, bb16)
            base = e_exp * _dot(cb16, state.astype(BF16)) + _group_cols(dsk_ref, g, GROUP_W)[...] * x
            for p in range(HEADS_PER_GROUP // 2):
                sl = slice(p * LANES, (p + 1) * LANES)
                xp = xdt[:, sl].astype(BF16)
                yd = []
                for e in range(2):
                    acs_row = acst_ref[pl.ds(g * HEADS_PER_GROUP + 2 * p + e, 1), :]
                    dm, _ = _pair_decay(acs_exp_v[:, sl], acs_row, e, causal)
                    yd.append(_dot((cb * dm).astype(BF16), xp))
                y_g[:, sl] = jnp.where(lane < HEAD_DIM, yd[0], yd[1]) + base[:, sl]
            s_scr[g] = jnp.exp(tot_exp) * state + _dot_tn(bb16, (f_exp * xdt).astype(BF16))

    wide, b_spec, c_spec, rows_spec, state_spec = _ssd_specs(lambda c: c)
    return _call(
        body, name=name,
        out_shape=(jax.ShapeDtypeStruct((rows, D_INNER), F32),
                   jax.ShapeDtypeStruct((N_GROUPS, nc, D_STATE, GROUP_W), F32)),
        grid=(nc,),
        in_specs=[wide, b_spec, c_spec, wide, wide, rows_spec, pl.BlockSpec((1, D_INNER), lambda c: (0, 0))],
        out_specs=(wide, state_spec),
        scratch_shapes=[pltpu.VMEM((N_GROUPS, D_STATE, GROUP_W), F32)],
        operands=[xbc, xbc, xbc, dt_exp, acs_exp, acs_rows, dskexp], semantics=("arbitrary",), steps=steps)


def _ssd_bwd(name, xbc, dt_exp, acs_exp, acs_rows, dt, a128, dskexp, dy, states, steps=()):
    rows = xbc.shape[0]
    nc = rows // CHUNK
    last = nc - 1

    def body(x_ref, b_ref, c_ref, dte_ref, acs_ref, acst_ref, dt_ref, a128_ref, dsk_all, dy_all, st_all,
             dxbc_all, ddt_ref, dalog_ref, ddsk_ref, ds_all):
        dx_all, db_all, dc_all = (dxbc_all.at[:, :D_INNER], dxbc_all.at[:, D_INNER:D_INNER + D_BC],
                                  dxbc_all.at[:, D_INNER + D_BC:])

        @pl.when(pl.program_id(0) == 0)
        def _():
            ds_all[...] = jnp.zeros_like(ds_all)
            dalog_ref[...] = jnp.zeros_like(dalog_ref)
            ddsk_ref[...] = jnp.zeros_like(ddsk_ref)

        dacs = jnp.zeros((CHUNK, LANES), F32)
        ddt_x = jnp.zeros((CHUNK, LANES), F32)
        for g in range(N_GROUPS):
            dacs_g, ddt_x_g = group(
                g, _group_cols(x_ref, g, GROUP_W), _group_cols(b_ref, g, D_STATE), _group_cols(c_ref, g, D_STATE),
                _group_cols(dte_ref, g, GROUP_W), _group_cols(acs_ref, g, GROUP_W), acst_ref,
                _group_cols(dsk_all, g, GROUP_W), _group_cols(dy_all, g, GROUP_W), st_all.at[g],
                _group_cols(dx_all, g, GROUP_W), _group_cols(db_all, g, D_STATE), _group_cols(dc_all, g, D_STATE),
                ddsk_ref, ds_all.at[g])
            dacs, ddt_x = dacs + dacs_g, ddt_x + ddt_x_g
        _, causal_t = _causal_masks()
        da = _dot3_data_rhs(causal_t, dacs)
        ddt_ref[...] = da * a128_ref[...] + ddt_x
        dalog_ref[...] += jnp.sum(da * dt_ref[...], axis=0, keepdims=True) * a128_ref[...]

    def group(g, x_ref, b_ref, c_ref, dte_ref, acs_ref, acst_ref, dsk_ref, dy_ref, st_ref,
              dx_ref, db_ref, dc_ref, ddsk_ref, ds_scr):
        causal, x, dt_exp, acs_exp_v, tot_exp, xdt, e_exp, f_exp, bm, cm = _ssd_common(
            x_ref, b_ref, c_ref, dte_ref, acs_ref)
        reduce_heads = _reduce_heads_matrix(g)
        state, dstate = st_ref[...], ds_scr[...]
        dyv = dy_ref[...]
        cb16, bb16 = cm.astype(BF16), bm.astype(BF16)
        s16, ds16 = state.astype(BF16), dstate.astype(BF16)
        cb = _dot_nt(cb16, bb16)
        cbt = _dot_nt(bb16, cb16)
        cs = _dot(cb16, s16)
        bds = _dot(bb16, ds16)
        edy = e_exp * dyv
        fx = f_exp * xdt
        dxdt_base = f_exp * bds
        dc_acc = _dot_nt(edy.astype(BF16), s16)
        db_acc = _dot_nt(fx.astype(BF16), ds16)
        ds_scr[...] = jnp.exp(tot_exp) * dstate + _dot_tn(cb16, edy.astype(BF16))
        q = fx * bds
        dacs = _dot2_data_lhs(edy * cs - q, reduce_heads)
        dtot = jnp.sum(_dot2_data_lhs(q + jnp.exp(tot_exp) * dstate * state, reduce_heads), axis=0, keepdims=True)
        ddsk_ref[...] += jnp.sum(_dot2_data_lhs(dyv * x, reduce_heads), axis=0, keepdims=True)
        lane = lax.broadcasted_iota(jnp.int32, (CHUNK, LANES), 1)
        dcb = jnp.zeros((CHUNK, CHUNK), F32)
        dcbt = jnp.zeros((CHUNK, CHUNK), F32)
        ddt_x = jnp.zeros((CHUNK, LANES), F32)
        for p in range(HEADS_PER_GROUP // 2):
            sl = slice(p * LANES, (p + 1) * LANES)
            xp, dyp = xdt[:, sl], dyv[:, sl]
            xp16, dyp16 = xp.astype(BF16), dyp.astype(BF16)
            dxh = []
            for e in range(2):
                h = 2 * p + e
                mine = (lane < HEAD_DIM) if e == 0 else (lane >= HEAD_DIM)
                acs_row = acst_ref[pl.ds(g * HEADS_PER_GROUP + h, 1), :]
                dm, dmt = _pair_decay(acs_exp_v[:, sl], acs_row, e, causal)
                m, mt = cb * dm, cbt * dmt
                xh16 = jnp.where(mine, xp, 0.0).astype(BF16)
                dyh16 = jnp.where(mine, dyp, 0.0).astype(BF16)
                d_m = _dot_nt(dyh16, xp16)
                d_mt = _dot_nt(xh16, dyp16)
                dacs_h = (jnp.sum(d_m * m, axis=-1, keepdims=True)
                          - jnp.sum(d_mt * mt, axis=-1, keepdims=True))
                dacs = dacs + jnp.where(lane == HEADS_PER_GROUP * g + h, dacs_h, 0.0)
                dcb = dcb + d_m * dm
                dcbt = dcbt + d_mt * dmt
                dxh.append(_dot(mt.astype(BF16), dyp16))
            dxdt = jnp.where(lane < HEAD_DIM, dxh[0], dxh[1]) + dxdt_base[:, sl]
            dx_ref[:, sl] = dxdt * dt_exp[:, sl] + dsk_ref[:, sl] * dyp
            ddt_x = ddt_x + _dot2_data_lhs(dxdt * x[:, sl], _reduce_pair_matrix(g, p))
        dc_ref[...] = dc_acc + _dot(dcb.astype(BF16), bb16)
        db_ref[...] = db_acc + _dot(dcbt.astype(BF16), cb16)
        row = lax.broadcasted_iota(jnp.int32, (CHUNK, LANES), 0)
        return dacs + jnp.where(row == CHUNK - 1, dtot, 0.0), ddt_x

    wide, b_spec, c_spec, rows_spec, state_spec = _ssd_specs(lambda c: last - c)
    heads_spec = pl.BlockSpec((CHUNK, LANES), lambda c: (last - c, 0))
    vec_spec = pl.BlockSpec((1, LANES), lambda c: (0, 0))
    vec_shape = jax.ShapeDtypeStruct((1, LANES), F32)
    return _call(
        body, name=name,
        out_shape=(jax.ShapeDtypeStruct((rows, D_XBC), F32), jax.ShapeDtypeStruct((rows, LANES), F32),
                   vec_shape, vec_shape),
        grid=(nc,),
        in_specs=[wide, b_spec, c_spec, wide, wide, rows_spec, heads_spec, vec_spec,
                  pl.BlockSpec((1, D_INNER), lambda c: (0, 0)), wide, state_spec],
        out_specs=(pl.BlockSpec((CHUNK, D_XBC), lambda c: (last - c, 0)), heads_spec, vec_spec, vec_spec),
        scratch_shapes=[pltpu.VMEM((N_GROUPS, D_STATE, GROUP_W), F32)],
        operands=[xbc, xbc, xbc, dt_exp, acs_exp, acs_rows, dt, a128, dskexp, dy, states],
        semantics=("arbitrary",), steps=steps)


def _attn_visible(b, heads=1):
    row = jnp.bitwise_and(lax.broadcasted_iota(jnp.int32, (heads * CHUNK, 3 * CHUNK), 0), CHUNK - 1)
    col = lax.broadcasted_iota(jnp.int32, (heads * CHUNK, 3 * CHUNK), 1)
    bb = b + jnp.zeros_like(col)
    meta = (col < CHUNK) & (bb >= 1) & (col >= PAD_ROWS)
    prev = (col >= CHUNK) & (col < 2 * CHUNK) & (bb >= 2) & ((col - CHUNK) > row)
    cur = (col >= 2 * CHUNK) & ((col - 2 * CHUNK) <= row) & ((bb >= 1) | ((col - 2 * CHUNK) >= PAD_ROWS))
    return meta | prev | cur


def _attn_visible4(b):
    return _attn_visible(b, 4)


def _stack_heads(q_ref, sink_ref, kvh, scale):
    lane = lax.broadcasted_iota(jnp.int32, (CHUNK, LANES), 1)
    parts, sinks = [], []
    for pp in range(2):
        pair = kvh * 2 + pp
        qp = q_ref[:, pair * LANES:(pair + 1) * LANES] * scale
        for e in range(2):
            mine = (lane < HEAD_DIM) if e == 0 else (lane >= HEAD_DIM)
            parts.append(jnp.where(mine, qp, 0.0).astype(BF16))
            sinks.append(jnp.full((CHUNK, 1), sink_ref[2 * pair + e], F32))
    return jnp.concatenate(parts, axis=0), jnp.concatenate(sinks, axis=0)


def _attn_operands(q_ref, k0, kp, kc, v0, vp, vc, sink_ref):
    kcat, vcat, q4, sink4 = [], [], [], []
    for kvh in range(N_KV_HEADS):
        ksl = slice(kvh * LANES, (kvh + 1) * LANES)
        kcat.append(jnp.concatenate([k0[:, ksl], kp[:, ksl], kc[:, ksl]], axis=0).astype(BF16))
        vcat.append(jnp.concatenate([v0[:, ksl], vp[:, ksl], vc[:, ksl]], axis=0).astype(BF16))
        stacked, sinks = _stack_heads(q_ref, sink_ref, kvh, ATTN_SCALE)
        q4.append(stacked)
        sink4.append(sinks)
    return kcat, vcat, q4, sink4


def _attn_probs(q4, kcat, visible, sink4):
    heads = range(N_KV_HEADS)
    s = [jnp.where(visible, _dot_nt(q4[h], kcat[h]), NEG_INF) for h in heads]
    m = [jnp.maximum(jnp.max(s[h], axis=-1, keepdims=True), sink4[h]) for h in heads]
    pe = [jnp.exp(s[h] - m[h]) for h in heads]
    pe_sink = [jnp.exp(sink4[h] - m[h]) for h in heads]
    inv = [1.0 / (jnp.sum(pe[h], axis=-1, keepdims=True) + pe_sink[h]) for h in heads]
    return [pe[h] * inv[h] for h in heads], [pe_sink[h] * inv[h] for h in heads]


def _unstack_pairs(stacked, pp):
    lane = lax.broadcasted_iota(jnp.int32, (CHUNK, LANES), 1)
    return jnp.where(lane < HEAD_DIM, stacked[(2 * pp) * CHUNK:(2 * pp + 1) * CHUNK],
                     stacked[(2 * pp + 1) * CHUNK:(2 * pp + 2) * CHUNK])


def _attn_specs(colblock):
    blk = lambda f: pl.BlockSpec((CHUNK, 2 * D_KV), f)
    return [blk(lambda b: (0, colblock)), blk(lambda b: (jnp.maximum(b - 1, 0), colblock)), blk(lambda b: (b, colblock))]


def _attn_fwd(name, q, kv2, sinks, steps=()):
    rows = q.shape[0]

    def body(q_ref, k0, kp, kc, v0, vp, vc, sink_ref, o_ref):
        visible = _attn_visible4(pl.program_id(0))
        kcat, vcat, q4, sink4 = _attn_operands(q_ref, k0, kp, kc, v0, vp, vc, sink_ref)
        pn, _ = _attn_probs(q4, kcat, visible, sink4)
        o4 = [_dot(pn[h].astype(BF16), vcat[h]) for h in range(N_KV_HEADS)]
        for kvh in range(N_KV_HEADS):
            for pp in range(2):
                qsl = slice((kvh * 2 + pp) * LANES, (kvh * 2 + pp + 1) * LANES)
                o_ref[:, qsl] = _unstack_pairs(o4[kvh], pp).astype(BF16)

    return _call(
        body, name=name, out_shape=jax.ShapeDtypeStruct((rows, D_MODEL), BF16), grid=(rows // CHUNK,),
        in_specs=[pl.BlockSpec((CHUNK, D_MODEL), lambda b: (b, 0))] + _attn_specs(0) + _attn_specs(1)
        + [pl.BlockSpec(memory_space=pltpu.SMEM)],
        out_specs=pl.BlockSpec((CHUNK, D_MODEL), lambda b: (b, 0)),
        operands=[q, kv2, kv2, kv2, kv2, kv2, kv2, sinks], semantics=("parallel",), steps=steps)


def _attn_bwd(name, q, kv2, sinks, do, steps=()):
    rows = q.shape[0]

    def body(q_ref, k0, kp, kc, v0, vp, vc, sink_ref, do_ref,
             dq_ref, dkc_ref, dkp_ref, dvc_ref, dvp_ref, dkm_ref, dvm_ref, dsink_ref):
        @pl.when(pl.program_id(0) == 0)
        def _():
            dkm_ref[...] = jnp.zeros_like(dkm_ref)
            dvm_ref[...] = jnp.zeros_like(dvm_ref)
            dsink_ref[...] = jnp.zeros_like(dsink_ref)

        visible = _attn_visible4(pl.program_id(0))
        heads = range(N_KV_HEADS)
        lane1 = lax.broadcasted_iota(jnp.int32, (1, LANES), 1)
        kcat, vcat, q4, sink4 = _attn_operands(q_ref, k0, kp, kc, v0, vp, vc, sink_ref)
        do4 = [_stack_heads(do_ref, sink_ref, h, 1.0)[0] for h in heads]
        pn, psink = _attn_probs(q4, kcat, visible, sink4)
        dp = [_dot_nt(do4[h], vcat[h]) for h in heads]
        delta = [jnp.sum(pn[h] * dp[h], axis=-1, keepdims=True) for h in heads]
        ds16 = [(pn[h] * (dp[h] - delta[h])).astype(BF16) for h in heads]
        dq4 = [_dot(ds16[h], kcat[h]) for h in heads]
        dk_acc = [_dot_tn(ds16[h], q4[h]) for h in heads]
        dv_acc = [_dot_tn(pn[h].astype(BF16), do4[h]) for h in heads]
        dsink = jnp.zeros((1, LANES), F32)
        for kvh in heads:
            ksl = slice(kvh * LANES, (kvh + 1) * LANES)
            sink_terms = psink[kvh] * delta[kvh]
            for j in range(4):
                part = jnp.sum(sink_terms[j * CHUNK:(j + 1) * CHUNK], axis=0, keepdims=True)
                dsink = dsink - jnp.where(lane1 == kvh * 4 + j, part, 0.0)
            for pp in range(2):
                qsl = slice((kvh * 2 + pp) * LANES, (kvh * 2 + pp + 1) * LANES)
                dq_ref[:, qsl] = (_unstack_pairs(dq4[kvh], pp) * ATTN_SCALE).astype(BF16)
            dkm_ref[:, ksl] += dk_acc[kvh][0:CHUNK]
            dvm_ref[:, ksl] += dv_acc[kvh][0:CHUNK]
            dkp_ref[:, ksl] = dk_acc[kvh][CHUNK:2 * CHUNK]
            dvp_ref[:, ksl] = dv_acc[kvh][CHUNK:2 * CHUNK]
            dkc_ref[:, ksl] = dk_acc[kvh][2 * CHUNK:3 * CHUNK]
            dvc_ref[:, ksl] = dv_acc[kvh][2 * CHUNK:3 * CHUNK]
        dsink_ref[...] += dsink

    qspec = pl.BlockSpec((CHUNK, D_MODEL), lambda b: (b, 0))
    kvspec = pl.BlockSpec((CHUNK, 2 * D_KV), lambda b: (b, 0))
    fixed = pl.BlockSpec((CHUNK, 2 * D_KV), lambda b: (0, 0))
    kv_shape = jax.ShapeDtypeStruct((rows, 2 * D_KV), F32)
    meta_shape = jax.ShapeDtypeStruct((CHUNK, 2 * D_KV), F32)
    return _call(
        body, name=name,
        out_shape=(jax.ShapeDtypeStruct((rows, D_MODEL), BF16), kv_shape, kv_shape, kv_shape, kv_shape,
                   meta_shape, meta_shape, jax.ShapeDtypeStruct((1, LANES), F32)),
        grid=(rows // CHUNK,),
        in_specs=[qspec] + _attn_specs(0) + _attn_specs(1) + [pl.BlockSpec(memory_space=pltpu.SMEM), qspec],
        out_specs=(qspec, kvspec, kvspec, kvspec, kvspec, fixed, fixed, pl.BlockSpec((1, LANES), lambda b: (0, 0))),
        operands=[q, kv2, kv2, kv2, kv2, kv2, kv2, sinks, do], semantics=("arbitrary",), steps=steps)


def _kv_grad_combine(name, dk_cur, dk_prev, dk_meta, dv_cur, dv_prev, dv_meta):
    rows = dk_cur.shape[0]
    nb = rows // CHUNK
    width = 2 * D_KV

    def body(kc_ref, kp_ref, km_ref, vc_ref, vp_ref, vm_ref, o_ref):
        jj = pl.program_id(0) + jnp.zeros((CHUNK, 1), jnp.int32)
        for half, (c_ref, p_ref, m_ref) in enumerate(((kc_ref, kp_ref, km_ref), (vc_ref, vp_ref, vm_ref))):
            total = c_ref[...] + jnp.where(jj < nb - 1, p_ref[...], 0.0) + jnp.where(jj == 0, m_ref[...], 0.0)
            o_ref[:, half * width:(half + 1) * width] = total.astype(BF16)

    blk = lambda f: pl.BlockSpec((CHUNK, width), f)
    three = lambda: [blk(lambda j: (j, 0)), blk(lambda j: (jnp.minimum(j + 1, nb - 1), 0)), blk(lambda j: (0, 0))]
    return pl.pallas_call(
        body, name=name, out_shape=jax.ShapeDtypeStruct((rows, 2 * width), BF16), grid=(nb,),
        in_specs=three() + three(), out_specs=pl.BlockSpec((CHUNK, 2 * width), lambda j: (j, 0)),
        compiler_params=_cparams(("parallel",)),
    )(dk_cur, dk_prev, dk_meta, dv_cur, dv_prev, dv_meta)


def _adamw(name, w, g, m, v, steps=()):
    rows, width = w.shape
    tr = rows
    for cand in range(8, rows + 1, 8):
        if rows % cand == 0 and cand * width * 4 <= (1 << 20):
            tr = cand

    def body(*refs):
        _adamw_update(*refs)

    blk = pl.BlockSpec((tr, width), lambda i: (i, 0))
    shp = jax.ShapeDtypeStruct((rows, width), F32)
    return _call(body, name=name, out_shape=(shp, shp, shp), grid=(rows // tr,), in_specs=[blk] * 4,
                 out_specs=(blk,) * 3, operands=[w, g, m, v], semantics=("parallel",), steps=steps)


def _adamw_update(w_ref, g_ref, m_ref, v_ref, d_ref, mo_ref, vo_ref):
    gv = g_ref[...]
    mn = ADAM_B1 * m_ref[...] + (1.0 - ADAM_B1) * gv
    vn = ADAM_B2 * v_ref[...] + (1.0 - ADAM_B2) * (gv * gv)
    m_hat = mn / (1.0 - ADAM_B1 ** ADAM_STEP)
    v_hat = vn / (1.0 - ADAM_B2 ** ADAM_STEP)
    d_ref[...] = -ADAM_LR * (m_hat / (jnp.sqrt(v_hat) + ADAM_EPS) + ADAM_WD * w_ref[...])
    mo_ref[...] = mn
    vo_ref[...] = vn


def _adamw_small(name, ws, gs, ms, vs):
    n = len(ws)

    def body(*refs):
        for i in range(n):
            _adamw_update(*refs[i::n])

    shapes = [jax.ShapeDtypeStruct(a.shape, F32) for a in ws]
    outs = pl.pallas_call(body, name=name, out_shape=shapes * 3, in_specs=[VMEM_SPEC] * (4 * n),
                          out_specs=[VMEM_SPEC] * (3 * n), compiler_params=_cparams())(*ws, *gs, *ms, *vs)
    return outs[:n], outs[n:2 * n], outs[2 * n:]


def _ffn_fwd(tag, h, hn, p, i, plan):
    up_g, up_v, act = _ffn_up_conv(f"ffn{tag}_up", hn, plan.weight("f_w_up", i), p["f_conv_w"][i],
                                   p["f_conv_b"][i:i + 1], steps=plan.steps(f"ffn{tag}_up"))
    pre = _mm(f"ffn{tag}_down", act, plan.weight("f_w_down", i), "nn", steps=plan.steps(f"ffn{tag}_down"))
    return pre, (h, hn, up_g, up_v, act, pre)


def _ffn_bwd(tag, dpre, saved, p, i, plan):
    h, hn, up_g, up_v, act, pre = saved
    plan.grad("f_w_down", i, _mm(f"ffn{tag}_down_dw", act, dpre, "tn", out_dtype=BF16))
    dact = _mm(f"ffn{tag}_down_dx", dpre, plan.weight("f_w_down", i), "nt", steps=plan.steps(f"ffn{tag}_down_dx"))
    gwg, gwv, gbg, gbv, dhn, g_up = _ffn_conv_bwd(
        f"ffn{tag}_conv_bwd", up_g, up_v, dact, p["f_conv_w"][i], p["f_conv_b"][i:i + 1], hn,
        plan.weight("f_w_up", i), steps=plan.steps(f"ffn{tag}_conv_bwd"))
    g_cw, g_cb = jnp.concatenate([gwg, gwv], axis=1), jnp.concatenate([gbg, gbv], axis=1)
    plan.grad("f_w_up", i, g_up)
    return dhn, dict(f_conv_w=g_cw, f_conv_b=g_cb)


def _lanes_pad(a, width=LANES):
    return jnp.pad(a, [(0, 0)] * (a.ndim - 1) + [(0, width - a.shape[-1])])


def _dup_heads(w):
    rows = w.shape[0]
    w = w.reshape(rows, 2 * N_KV_HEADS, 1, HEAD_DIM)
    return jnp.broadcast_to(w, (rows, 2 * N_KV_HEADS, 2, HEAD_DIM)).reshape(rows, 4 * D_KV)


def _undup_heads(g):
    rows = g.shape[0]
    return g.reshape(rows, 2 * N_KV_HEADS, 2, HEAD_DIM).sum(axis=2).reshape(rows, 2 * D_KV)


def _local_step(x2, target, p, plan):
    seq = x2.shape[0]
    rows = seq + CHUNK
    g = {}

    h0 = jnp.concatenate([jnp.zeros((PAD_ROWS, D_MODEL), F32), p["meta_tokens"], x2], axis=0)

    w_in = plan.weight("a_w_in")
    w_dt = jnp.pad(w_in[D_MAIN:], ((0, LANES - SSM_HEADS), (0, 0)))
    dt_bias = _lanes_pad(p["a_dt_bias"])
    a128 = _lanes_pad(-jnp.exp(p["a_a_log"]))
    dskexp = jnp.repeat(p["a_d_skip"].reshape(SSM_HEADS), HEAD_DIM).reshape(1, D_INNER)

    hn0 = _rms_fwd("a_norm", h0, p["a_norm_pre"])
    zx = _mm("a_in_main", hn0, w_in, "nt", k_rows=D_MAIN, steps=plan.steps("a_in_main"))
    dtr = _mm("a_in_dt", hn0, w_dt, "nt")
    xbc = _conv4_fwd("a_conv", zx, p["a_conv_w"], p["a_conv_b"], steps=plan.steps("a_conv"))
    dt = _dt_fwd("a_dt", dtr, dt_bias)
    dt_exp, acs_exp, acs_rows = _ssd_prep("a_ssd_prep", dt, a128, steps=plan.steps("a_ssd_prep"))
    y, states = _ssd_fwd("a_ssd", xbc, dt_exp, acs_exp, acs_rows, dskexp, steps=plan.steps("a_ssd"))
    yn = _gate_fwd("a_gate", y, zx, p["a_gate_norm"], steps=plan.steps("a_gate"))
    mix = _mm("a_out", yn, plan.weight("a_w_out"), "nn", steps=plan.steps("a_out"))
    h1, (hn_f0,) = _resid_norm_fwd("a_resid", h0, mix, p["a_norm_post"], [p["f_norm_pre"][0:1]])

    pre_f0, ffn0 = _ffn_fwd("0", h1, hn_f0, p, 0, plan)
    h2, (hkv, hn2) = _resid_norm_fwd("ffn0_resid", h1, pre_f0, p["f_norm_post"][0:1], [p["kv_norm"], p["b_norm_pre"]])

    w_kv2 = _dup_heads(plan.weight("w_kv"))
    kv2 = _mm("kv_proj", hkv, w_kv2, "nn")
    q = _mm("b_q", hn2, plan.weight("b_w_q"), "nn")
    sinks = p["b_sinks"].reshape(N_Q_HEADS)
    o = _attn_fwd("b_attn", q, kv2, sinks, steps=plan.steps("b_attn"))
    attn = _mm("b_o", o, plan.weight("b_w_o"), "nn", steps=plan.steps("b_o"))
    h3, (hn_f1,) = _resid_norm_fwd("b_resid", h2, attn, p["b_norm_post"], [p["f_norm_pre"][1:2]])

    pre_f1, ffn1 = _ffn_fwd("1", h3, hn_f1, p, 1, plan)
    dh, loss_vec, dpre_f1, g_post1 = _resid_norm_loss("ffn1_resid_loss", h3, pre_f1, p["f_norm_post"][1:2], target)
    loss = loss_vec[0, 0]

    dhn_f1, g1 = _ffn_bwd("1", dpre_f1, ffn1, p, 1, plan)
    dh, g_pre1, dpre, g["b_norm_post"] = _norm_bwd_add("ffn1_norm_bwd", dh, dhn_f1, h3, p["f_norm_pre"][1:2],
                                                        then=(attn, p["b_norm_post"]))
    plan.grad("b_w_o", None, _mm("b_o_dw", o, dpre, "tn", out_dtype=BF16))
    do = _mm("b_o_dx", dpre, plan.weight("b_w_o"), "nt", steps=plan.steps("b_o_dx"))
    dq, dkc, dkp, dvc, dvp, dkm, dvm, dsink = _attn_bwd("b_attn_bwd", q, kv2, sinks, do, steps=plan.steps("b_attn_bwd"))
    g["b_sinks"] = dsink[:, :N_Q_HEADS]
    dhn2 = _mm("b_q_dx", dq, plan.weight("b_w_q"), "nt")
    plan.grad("b_w_q", None, _mm("b_q_dw", hn2, dq, "tn", out_dtype=BF16))
    dh, g["b_norm_pre"] = _norm_bwd_add("b_norm_bwd", dh, dhn2, h2, p["b_norm_pre"])
    dkv2 = _kv_grad_combine("kv_grad", dkc, dkp, dkm, dvc, dvp, dvm)
    dhkv = _mm("kv_proj_dx", dkv2, w_kv2, "nt")
    plan.grad("w_kv", None, _undup_heads(_mm("kv_proj_dw", hkv, dkv2, "tn")))
    dh, g["kv_norm"], dpre_f0, g_post0 = _norm_bwd_add("kv_norm_bwd", dh, dhkv, h2, p["kv_norm"],
                                                       then=(pre_f0, p["f_norm_post"][0:1]))

    dhn_f0, g0 = _ffn_bwd("0", dpre_f0, ffn0, p, 0, plan)
    dh, g_pre0, dpre, g["a_norm_post"] = _norm_bwd_add("ffn0_norm_bwd", dh, dhn_f0, h1, p["f_norm_pre"][0:1],
                                                        then=(mix, p["a_norm_post"]))
    g["f_norm_post"] = jnp.concatenate([g_post0, g_post1], axis=0)
    g["f_norm_pre"] = jnp.concatenate([g_pre0, g_pre1], axis=0)
    g["f_conv_w"] = jnp.stack([g0["f_conv_w"], g1["f_conv_w"]])
    g["f_conv_b"] = jnp.concatenate([g0["f_conv_b"], g1["f_conv_b"]], axis=0)
    plan.grad("a_w_out", None, _mm("a_out_dw", yn, dpre, "tn", out_dtype=BF16))
    dyn = _mm("a_out_dx", dpre, plan.weight("a_w_out"), "nt", steps=plan.steps("a_out_dx"))
    dy, dzx, g["a_gate_norm"] = _gate_bwd("a_gate_bwd", dyn, y, zx, p["a_gate_norm"])
    dxbc, ddt, dalog, ddsk = _ssd_bwd("a_ssd_bwd", xbc, dt_exp, acs_exp, acs_rows, dt, a128, dskexp, dy, states,
                                      steps=plan.steps("a_ssd_bwd"))
    g["a_a_log"] = dalog[:, :SSM_HEADS]
    g["a_d_skip"] = ddsk[:, :SSM_HEADS]
    ddtr, dbias = _dt_bwd("a_dt_bwd", ddt, dtr, dt_bias)
    g["a_dt_bias"] = dbias[:, :SSM_HEADS]
    dzx, g["a_conv_w"], g["a_conv_b"] = _conv4_bwd("a_conv_bwd", zx, dxbc, p["a_conv_w"], p["a_conv_b"], dzx)
    g_in = _mm("a_in_main_dw", dzx, hn0, "tn", out_dtype=BF16, out_rows=D_IN_PROJ, steps=plan.steps("a_in_main_dw"))
    plan.grad("a_w_in", None, _tn_rows_into("a_in_dt_dw", ddtr, hn0, g_in, D_MAIN, SSM_HEADS))
    dhn0 = _mm("a_in_dt_dx", ddtr, w_dt, "nn", steps=plan.steps("a_in_dt_dx"))
    dhn0 = _mm("a_in_main_dx", dzx, w_in, "nn", acc=dhn0, steps=plan.steps("a_in_main_dx"))
    dh_first, grad_x, g["a_norm_pre"] = _norm_bwd_add("a_norm_bwd", dh, dhn0, h0, p["a_norm_pre"],
                                                      split_first_block=True, steps=plan.steps("a_norm_bwd"))
    g["meta_tokens"] = dh_first[PAD_ROWS:]
    return loss, grad_x, g


ANY = pl.BlockSpec(memory_space=pl.ANY)
VMEM_SPEC = pl.BlockSpec(memory_space=pltpu.VMEM)


def _step_gather_small(slots):
    def copies(outs, send_sems, recv_sems, received):
        x, y, c = _place()
        me = 2 * x + y
        for j, (cx, cy) in enumerate(_other_chips(x, y)):
            slot = outs[0].at[2 * cx + cy if received else me]
            yield _remote(slot, slot, send_sems, recv_sems, j, (cx, cy, c))

    def start(ins, outs, send_sems, recv_sems):
        for cp in copies(outs, send_sems, recv_sems, False):
            cp.start()

    def finish(ins, outs, send_sems, recv_sems):
        for cp in copies(outs, send_sems, recv_sems, True):
            cp.wait_recv()
        for cp in copies(outs, send_sems, recv_sems, False):
            cp.wait_send()

    return _Step([slots], [_like(slots)], {0: 0}, 3, start, finish)


def _row_block(rows, width, itemsize, align, budget=2 << 20):
    best = rows
    for cand in range(align, rows + 1, align):
        if rows % cand == 0 and cand * width * itemsize <= budget:
            best = cand
    return best


def _cast_into_slot(name, chip, w, layer=None):
    rows, width = w.shape[-2:]
    tr = _row_block(rows, width, 4, 16)
    if layer is None:
        in_spec = pl.BlockSpec((tr, width), lambda i, chip_ref: (i, 0))
    else:
        in_spec = pl.BlockSpec((None, tr, width), lambda i, chip_ref: (layer, i, 0))

    def body(chip_ref, w_ref, o_ref):
        o_ref[...] = w_ref[...].astype(BF16)

    return pl.pallas_call(
        body, name=name, out_shape=jax.ShapeDtypeStruct((N_CHIPS, rows, width), BF16),
        grid_spec=pltpu.PrefetchScalarGridSpec(
            num_scalar_prefetch=1, grid=(rows // tr,), in_specs=[in_spec],
            out_specs=pl.BlockSpec((None, tr, width), lambda i, chip_ref: (chip_ref[0], i, 0))),
        compiler_params=_cparams(("parallel",)),
    )(chip, w)


def _allreduce_small(name, vec, steps=()):
    rows = -(-vec.shape[0] // (2 * SUBLANES)) * (2 * SUBLANES)
    hr = rows // 2
    padded = jnp.pad(vec, ((0, rows - vec.shape[0]), (0, 0)))

    def body(v_ref, o_ref, theirs, pair, by_chip, send_sems, recv_sems):
        x, y, c = _place()
        me = 2 * x + y
        sibling = (x, y, 1 - c)
        mine = pl.ds(pl.multiple_of(c * hr, SUBLANES), hr)
        other = pl.ds(pl.multiple_of((1 - c) * hr, SUBLANES), hr)

        swap = _remote(v_ref, theirs, send_sems, recv_sems, 0, sibling)
        swap.start()
        swap.wait()
        south = (c + jnp.zeros((1, 1), jnp.int32)) == 0
        pair[...] = jnp.where(south, v_ref[...], theirs[...]) + jnp.where(south, theirs[...], v_ref[...])

        by_chip[me] = pair[mine, :]
        sends = [_remote(by_chip.at[me], by_chip.at[me], send_sems, recv_sems, 1 + j, (cx, cy, c))
                 for j, (cx, cy) in enumerate(_other_chips(x, y))]
        for cp in sends:
            cp.start()
        for j, (cx, cy) in enumerate(_other_chips(x, y)):
            _remote(by_chip.at[me], by_chip.at[2 * cx + cy], send_sems, recv_sems, 1 + j, (cx, cy, c)).wait_recv()
        for cp in sends:
            cp.wait_send()
        total = by_chip[0]
        for s in range(1, N_CHIPS):
            total = total + by_chip[s]

        o_ref[mine, :] = total
        back = _remote(o_ref.at[mine], o_ref.at[mine], send_sems, recv_sems, 4, sibling)
        back.start()
        _remote(o_ref.at[other], o_ref.at[other], send_sems, recv_sems, 4, sibling).wait_recv()
        back.wait_send()

    out = _call(
        body, name=name, out_shape=jax.ShapeDtypeStruct((rows, LANES), F32), grid=(),
        in_specs=[VMEM_SPEC], out_specs=VMEM_SPEC, operands=[padded],
        scratch_shapes=[pltpu.VMEM((rows, LANES), F32), pltpu.VMEM((rows, LANES), F32),
                        pltpu.VMEM((N_CHIPS, hr, LANES), F32), pltpu.SemaphoreType.DMA((5,)),
                        pltpu.SemaphoreType.DMA((5,))],
        steps=steps)
    return out[:vec.shape[0]]


def _rs_pair_add(name, place, grads, partner, split="rows"):
    _, half_rows, width = partner.shape
    tr = _row_block(half_rows, width, 2, 16)
    nb = half_rows // tr
    if split == "rows":
        mine = pl.BlockSpec((None, tr, width), lambda s, i, pr: (s, pr[1] * nb + i, 0))
    else:
        mine = pl.BlockSpec((None, tr, width), lambda s, i, pr: (s, i, pr[1]))

    def body(place_ref, g_ref, p_ref, o_ref):
        o_ref[...] = (g_ref[...].astype(F32) + p_ref[...].astype(F32)).astype(BF16)

    return pl.pallas_call(
        body, name=name, out_shape=jax.ShapeDtypeStruct(partner.shape, BF16),
        grid_spec=pltpu.PrefetchScalarGridSpec(
            num_scalar_prefetch=1, grid=(N_CHIPS, nb),
            in_specs=[mine, pl.BlockSpec((None, tr, width), lambda s, i, pr: (s, i, 0))],
            out_specs=pl.BlockSpec((None, tr, width), lambda s, i, pr: (s, i, 0))),
        compiler_params=_cparams(("parallel", "parallel")),
    )(place, grads, partner)


def _rs_chip_add(name, place, mine, others, split="rows"):
    _, half_rows, width = mine.shape
    tr = _row_block(half_rows, width, 4, 16, budget=1 << 20)
    nb = half_rows // tr
    if split == "rows":
        out_shape, out_spec = (2 * half_rows, width), pl.BlockSpec((tr, width), lambda i, pr: (pr[1] * nb + i, 0))
    else:
        out_shape, out_spec = (half_rows, 2 * width), pl.BlockSpec((tr, width), lambda i, pr: (i, pr[1]))

    def body(place_ref, q_ref, r_ref, o_ref):
        acc = q_ref[...].astype(F32)
        for j in range(3):
            acc = acc + r_ref[j].astype(F32)
        o_ref[...] = acc

    return pl.pallas_call(
        body, name=name, out_shape=jax.ShapeDtypeStruct(out_shape, F32),
        grid_spec=pltpu.PrefetchScalarGridSpec(
            num_scalar_prefetch=1, grid=(nb,),
            in_specs=[pl.BlockSpec((None, tr, width), lambda i, pr: (pr[0], i, 0)),
                      pl.BlockSpec((3, tr, width), lambda i, pr: (0, i, 0))],
            out_specs=out_spec),
        compiler_params=_cparams(("parallel",)),
    )(place, mine, others)


WEIGHTS = ["meta_tokens", "a_norm_pre", "a_w_in", "a_conv_w", "a_conv_b", "a_dt_bias", "a_a_log", "a_d_skip",
           "a_gate_norm", "a_w_out", "a_norm_post", "kv_norm", "w_kv", "b_norm_pre", "b_w_q", "b_sinks", "b_w_o",
           "b_norm_post", "f_norm_pre", "f_w_up", "f_conv_w", "f_conv_b", "f_w_down", "f_norm_post"]
FULL_SHAPE = {
    "meta_tokens": (16, 1024), "a_norm_pre": (1, 1024), "a_w_in": (1, 1024, 5152), "a_conv_w": (1, 4, 3072),
    "a_conv_b": (1, 3072), "a_dt_bias": (1, 32), "a_a_log": (1, 32), "a_d_skip": (1, 32), "a_gate_norm": (1, 2048),
    "a_w_out": (1, 2048, 1024), "a_norm_post": (1, 1024), "kv_norm": (1024,), "w_kv": (1024, 512),
    "b_norm_pre": (1, 1024), "b_w_q": (1, 1024, 1024), "b_sinks": (1, 16), "b_w_o": (1, 1024, 1024),
    "b_norm_post": (1, 1024), "f_norm_pre": (2, 1024), "f_w_up": (2, 1024, 5632), "f_conv_w": (2, 3, 5632),
    "f_conv_b": (2, 5632), "f_w_down": (2, 2816, 1024), "f_norm_post": (2, 1024),
}
SHARD_AXIS = {
    "meta_tokens": 1, "a_norm_pre": 1, "a_w_in": 2, "a_conv_w": 2, "a_conv_b": 1, "a_dt_bias": None, "a_a_log": None,
    "a_d_skip": None, "a_gate_norm": 1, "a_w_out": 1, "a_norm_post": 1, "kv_norm": None, "w_kv": 0, "b_norm_pre": None,
    "b_w_q": 1, "b_sinks": None, "b_w_o": 1, "b_norm_post": None, "f_norm_pre": None, "f_w_up": 2, "f_conv_w": 2,
    "f_conv_b": None, "f_w_down": 1, "f_norm_post": None,
}
BIG = ["a_w_in", "a_w_out", "w_kv", "b_w_q", "b_w_o", "f_w_up", "f_w_down"]
SMALL = [n for n in WEIGHTS if n not in BIG]
SMALL_SHARDED = [n for n in SMALL if SHARD_AXIS[n] is not None]


def _shard_shape(name):
    shape = list(FULL_SHAPE[name])
    if SHARD_AXIS[name] is not None:
        shape[SHARD_AXIS[name]] //= N_CHIPS
    return tuple(shape)


def _numel(shape):
    return int(math.prod(shape))


SUBLANES = 8


def _packed_rows(shape):
    rows = -(-_numel(shape) // LANES)
    return -(-rows // SUBLANES) * SUBLANES


def _pack(arrays):
    parts = []
    for a in arrays:
        size, rows = _numel(a.shape), _packed_rows(a.shape)
        if size % LANES == 0:
            part = jnp.pad(a.reshape(size // LANES, LANES), ((0, rows - size // LANES), (0, 0)))
        else:
            part = jnp.pad(a.reshape(-1), (0, rows * LANES - size)).reshape(rows, LANES)
        parts.append(part)
    return jnp.concatenate(parts, axis=0)


def _unpack(packed, names, shape_of):
    out, off = {}, 0
    lead = packed.shape[:-2]
    for n in names:
        shape = tuple(shape_of(n))
        size, rows = _numel(shape), _packed_rows(shape)
        part = packed[..., off:off + rows, :]
        if size % LANES == 0:
            out[n] = part[..., :size // LANES, :].reshape(lead + shape)
        else:
            out[n] = part.reshape(lead + (rows * LANES,))[..., :size].reshape(lead + shape)
        off += rows
    return out


def _split_chips(name, full):
    ax = SHARD_AXIS[name]
    shape = full.shape
    cut = shape[:ax] + (N_CHIPS, shape[ax] // N_CHIPS) + shape[ax + 1:]
    return jnp.moveaxis(full.reshape(cut), ax, 0)


def _join_chips(name, stacked):
    ax = SHARD_AXIS[name]
    moved = jnp.moveaxis(stacked, 0, ax)
    shape = moved.shape
    return moved.reshape(shape[:ax] + (shape[ax] * shape[ax + 1],) + shape[ax + 2:])


def _as2d(a):
    return a.reshape(-1, a.shape[-1])


BUFFERS = [("a_w_in", "a_w_in", None), ("a_w_out", "a_w_out", None), ("w_kv", "w_kv", None),
           ("b_w_q", "b_w_q", None), ("b_w_o", "b_w_o", None), ("f_w_up0", "f_w_up", 0), ("f_w_up1", "f_w_up", 1),
           ("f_w_down0", "f_w_down", 0), ("f_w_down1", "f_w_down", 1)]


TRANSPOSED = ("a_w_in",)
SPLIT = {"a_w_in": "cols"}


def _local_shard(arrays, weight, layer):
    if weight in TRANSPOSED:
        return arrays[weight][0].T
    return _as2d(arrays[weight]) if layer is None else arrays[weight]


def _weight_from_gathered(weight, buf):
    if weight == "f_w_up":
        return buf
    return buf.reshape(N_CHIPS * buf.shape[1], buf.shape[2])


def _gathered_from_grad(weight, g):
    if weight == "f_w_up":
        return g
    return g.reshape(N_CHIPS, g.shape[0] // N_CHIPS, g.shape[1]).astype(BF16)


GATHER_SCHEDULE = {
    "a_in_main": [("ici", ["a_w_out"])],
    "a_conv": [("d2d", ["a_w_out"]), ("ici", ["f_w_down0"])],
    "a_ssd_prep": [("d2d", ["f_w_down0"]), ("ici_near", ["f_w_up0"])],
    "a_ssd": [("ici_far", ["f_w_up0"])],
    "a_gate": [("d2d", ["f_w_up0"]), ("ici", ["w_kv", "b_w_q", "b_w_o"])],
    "ffn0_up": [("d2d", ["w_kv", "b_w_q", "b_w_o"]), ("ici", ["f_w_down1"])],
    "ffn0_down": [("d2d", ["f_w_down1"])],
    "b_attn": [("ici", ["f_w_up1"])],
    "b_o": [("d2d", ["f_w_up1"])],
}
REDUCE_SCHEDULE = {
    "b_attn_bwd": [("all", ["f_w_down1", "f_w_up1", "b_w_o"])],
    "ffn0_conv_bwd": [("all", ["b_w_q", "w_kv", "f_w_down0"])],
    "a_ssd_bwd": [("all", ["f_w_up0", "a_w_out"])],
    "a_in_main_dx": [("near", ["a_w_in"])],
    "a_norm_bwd": [("far", ["a_w_in"])],
}
ICI_PEERS = {"ici": ALL_PEERS, "ici_near": NEAR_PEERS, "ici_far": FAR_PEERS,
             "all": ALL_PEERS, "near": NEAR_PEERS, "far": FAR_PEERS}
PAIR_SCHEDULE = {
    "b_o_dx": ["f_w_down1", "f_w_up1", "b_w_o"],
    "ffn0_down_dx": ["b_w_q", "w_kv", "f_w_down0"],
    "a_out_dx": ["f_w_up0", "a_w_out"],
    "a_in_dt_dx": ["a_w_in"],
}
SWAP_SCHEDULE = {"a_in_main_dw": ["f_w_down1", "f_w_up1", "b_w_o", "b_w_q", "w_kv", "f_w_down0", "f_w_up0", "a_w_out"]}


def _buffer_of(weight, layer):
    return weight if layer is None else f"{weight}{layer}"


class _Pipeline:
    def __init__(self, place, slots):
        self.place = place
        self.slots = dict(slots)
        self.running = []
        self.grads = {}
        self.theirs = {}
        self.partials = {}
        self.peers = {}
        self.reduced = {}

    def _collect(self):
        for step, buffers, table in self.running:
            table.update(zip(buffers, step.results))
        self.running = []

    @staticmethod
    def _splits(buffers):
        return [SPLIT.get(b, "rows") for b in buffers]

    def gather_now(self, name, buffers, also=()):
        step = _step_gather_full([self.slots[b] for b in buffers], self._splits(buffers))
        _run_steps(name, [step, *also])
        self.slots.update(zip(buffers, step.results))

    def weight(self, name, layer=None):
        self._collect()
        return _weight_from_gathered(name, self.slots[_buffer_of(name, layer)])

    def grad(self, name, layer, g):
        self.grads[_buffer_of(name, layer)] = _gathered_from_grad(name, g)

    def steps(self, kernel):
        self._collect()
        steps = []
        for phase, buffers in GATHER_SCHEDULE.get(kernel, []):
            bufs, splits = [self.slots[b] for b in buffers], self._splits(buffers)
            step = (_step_gather_d2d(bufs, splits) if phase == "d2d"
                    else _step_gather_ici(bufs, splits, ICI_PEERS[phase]))
            self.running.append((step, buffers, self.slots))
            steps.append(step)
        buffers = PAIR_SCHEDULE.get(kernel)
        if buffers:
            step = _step_pair_exchange([self.grads[b] for b in buffers], self._splits(buffers))
            self.running.append((step, buffers, self.theirs))
            steps.append(step)
        for part, buffers in REDUCE_SCHEDULE.get(kernel, []):
            for b in buffers:
                if b not in self.partials:
                    self.partials[b] = _rs_pair_add("reduce_pair_add_" + b, self.place, self.grads[b], self.theirs[b],
                                                    SPLIT.get(b, "rows"))
            started = [self.peers[b] for b in buffers] if all(b in self.peers for b in buffers) else None
            step = _step_chip_exchange([self.partials[b] for b in buffers], ICI_PEERS[part], into=started)
            self.running.append((step, buffers, self.peers))
            steps.append(step)
        buffers = SWAP_SCHEDULE.get(kernel)
        if buffers:
            step = self._swap_step(buffers)
            self.running.append((step, buffers, self.reduced))
            steps.append(step)
        return steps

    def _swap_step(self, buffers):
        halves = [_rs_chip_add("reduce_chip_add_" + b, self.place, self.partials[b], self.peers[b], SPLIT.get(b, "rows"))
                  for b in buffers]
        return _step_pair_gather(halves, self._splits(buffers))

    def shard(self, buffer):
        self._collect()
        return self.reduced[buffer]

    def last_step(self):
        self._collect()
        rest = [b for b, _, _ in BUFFERS if b not in self.reduced]
        step = self._swap_step(rest)
        self.running.append((step, rest, self.reduced))
        return step


def kernel(x, meta_tokens, a_norm_pre, a_w_in, a_conv_w, a_conv_b, a_dt_bias, a_a_log, a_d_skip, a_gate_norm, a_w_out, a_norm_post, kv_norm, w_kv, b_norm_pre, b_w_q, b_sinks, b_w_o, b_norm_post, f_norm_pre, f_w_up, f_conv_w, f_conv_b, f_w_down, f_norm_post, loss_target, m_meta_tokens, m_a_norm_pre, m_a_w_in, m_a_conv_w, m_a_conv_b, m_a_dt_bias, m_a_a_log, m_a_d_skip, m_a_gate_norm, m_a_w_out, m_a_norm_post, m_kv_norm, m_w_kv, m_b_norm_pre, m_b_w_q, m_b_sinks, m_b_w_o, m_b_norm_post, m_f_norm_pre, m_f_w_up, m_f_conv_w, m_f_conv_b, m_f_w_down, m_f_norm_post, v_meta_tokens, v_a_norm_pre, v_a_w_in, v_a_conv_w, v_a_conv_b, v_a_dt_bias, v_a_a_log, v_a_d_skip, v_a_gate_norm, v_a_w_out, v_a_norm_post, v_kv_norm, v_w_kv, v_b_norm_pre, v_b_w_q, v_b_sinks, v_b_w_o, v_b_norm_post, v_f_norm_pre, v_f_w_up, v_f_conv_w, v_f_conv_b, v_f_w_down, v_f_norm_post):
    given = dict(locals())
    w = {n: given[n] for n in WEIGHTS}
    mom = {n: given["m_" + n] for n in WEIGHTS}
    var = {n: given["v_" + n] for n in WEIGHTS}
    chip = 2 * lax.axis_index("x") + lax.axis_index("y")
    core = lax.axis_index("c")
    place = jnp.stack([chip, core]).astype(jnp.int32)

    small_mine = _pack([w[n] for n in SMALL_SHARDED])
    small_step = _step_gather_small(lax.dynamic_update_slice(
        jnp.zeros((N_CHIPS,) + small_mine.shape, F32), small_mine[None], (chip, 0, 0)))
    slots = {b: _cast_into_slot("cast_" + b, place, _local_shard(w, wn, layer), layer) for b, wn, layer in BUFFERS}
    pipeline = _Pipeline(place, slots)
    pipeline.gather_now("gather_first", ["a_w_in"], also=[small_step])
    small_parts = _unpack(small_step.results[0], SMALL_SHARDED, _shard_shape)
    p = {}
    for n in SMALL:
        p[n] = _join_chips(n, small_parts[n]) if n in SMALL_SHARDED else w[n]
    p["a_conv_w"] = p["a_conv_w"][0]
    p["kv_norm"] = p["kv_norm"].reshape(1, D_MODEL)

    loss_local, grad_x, g = _local_step(x[0], loss_target[0], p, pipeline)

    small_sum = _allreduce_small("reduce_small", _pack([g[n].reshape(FULL_SHAPE[n]) for n in SMALL]
                                                       + [loss_local.reshape(1, 1)]), steps=[pipeline.last_step()])
    small_red = _unpack(small_sum, SMALL + ["loss"], lambda n: (1, 1) if n == "loss" else FULL_SHAPE[n])
    loss = small_red["loss"][0, 0]
    grads = {}
    for n in SMALL:
        if SHARD_AXIS[n] is None:
            grads[n] = small_red[n]
        else:
            grads[n] = lax.dynamic_index_in_dim(_split_chips(n, small_red[n]), chip, 0, keepdims=False)

    delta, new_m, new_v = {}, {}, {}
    for n in BIG:
        shape = _shard_shape(n)
        if n in TRANSPOSED:
            g2d = pipeline.shard(n)
            w2d, m2d, v2d = (arrays[n][0].T for arrays in (w, mom, var))
            back = lambda a: a.T.reshape(shape)
        else:
            g2d = (jnp.concatenate([pipeline.shard(n + "0"), pipeline.shard(n + "1")], axis=0)
                   if n in ("f_w_up", "f_w_down") else pipeline.shard(n))
            w2d, m2d, v2d = (_as2d(arrays[n]) for arrays in (w, mom, var))
            back = lambda a: a.reshape(shape)
        d, m2, v2 = _adamw("adamw_" + n, w2d, g2d, m2d, v2d, steps=pipeline.steps("adamw_" + n))
        grads[n], delta[n], new_m[n], new_v[n] = back(g2d), back(d), back(m2), back(v2)
    at_least_2d = lambda n: (1,) * (2 - len(_shard_shape(n))) + _shard_shape(n)
    outs = _adamw_small("adamw_small", *[[src[n].reshape(at_least_2d(n)) for n in SMALL] for src in (w, grads, mom, var)])
    for dst, arrays in zip((delta, new_m, new_v), outs):
        dst.update({n: a.reshape(_shard_shape(n)) for n, a in zip(SMALL, arrays)})

    return (loss, grad_x[None], *[grads[n].reshape(_shard_shape(n)) for n in WEIGHTS],
            *[delta[n] for n in WEIGHTS], *[new_m[n] for n in WEIGHTS], *[new_v[n] for n in WEIGHTS])
```
